```python
import math
import jax, jax.numpy as jnp
from jax import lax
import numpy as np

D_MODEL = 1024
BATCH = 32
SEQ = 2048
DEPTH = 1

N_HEADS = 8
HEAD_DIM = 64
ATT_WIDTH = N_HEADS * HEAD_DIM
PATTERNS = ((128, 1), (512, 4), (2048, 16))
SSM_GROUPS = 16
SSM_GROUP_CH = 16
SSM_WIDTH = SSM_GROUPS * SSM_GROUP_CH
SSM_STATE = 64
D_FF = 2048
CONV_W = 3
IN_WIDTH = 3 * ATT_WIDTH + SSM_WIDTH + 2 * D_MODEL
EPS = 1e-6
NEG_INF = -1e30

kernel_name = "gated_dilated_attn_s5_hybrid_block"


def _rmsnorm(x, g):
    x32 = x.astype(jnp.float32)
    y = x32 * lax.rsqrt(jnp.mean(x32 * x32, axis=-1, keepdims=True) + EPS)
    return y.astype(x.dtype) * g


def _modulate(x, g, shift, scale):
    return _rmsnorm(x, g) * (1 + scale[:, None, :]) + shift[:, None, :]


def _alibi_slopes():
    return np.array([2.0 ** (-8.0 * (h + 1) / N_HEADS) for h in range(N_HEADS)], dtype=np.float32)


def _dilated_window_attention(q, k, v, slopes, window, dilation):
    b, s, h, hd = q.shape
    w = window // dilation
    L = s // dilation
    nb = -(-L // w)
    Lp = nb * w
    X = b * dilation

    def to_sub(t):
        t = t.reshape(b, L, dilation, h, hd).transpose(0, 2, 3, 1, 4)
        return t.reshape(X, h, L, hd)

    qs, ks, vs = to_sub(q), to_sub(k), to_sub(v)
    qb = jnp.pad(qs, ((0, 0), (0, 0), (0, Lp - L), (0, 0))).reshape(X, h, nb, w, hd)
    kp = jnp.pad(ks, ((0, 0), (0, 0), (w, Lp - L), (0, 0)))
    vp = jnp.pad(vs, ((0, 0), (0, 0), (w, Lp - L), (0, 0)))
    kb = jnp.concatenate([kp[:, :, :Lp].reshape(X, h, nb, w, hd),
                          kp[:, :, w:].reshape(X, h, nb, w, hd)], axis=3)
    vb = jnp.concatenate([vp[:, :, :Lp].reshape(X, h, nb, w, hd),
                          vp[:, :, w:].reshape(X, h, nb, w, hd)], axis=3)

    a_idx = np.arange(w)[:, None]
    j_idx = np.arange(2 * w)[None, :]
    dist = (w + a_idx - j_idx).astype(np.float32)
    kpos = np.arange(nb)[:, None, None] * w - w + j_idx[None]
    valid = (dist[None] >= 0) & (dist[None] <= w) & (kpos >= 0)
    bias = -(slopes[:, None, None] * dilation) * dist[None]

    scale = HEAD_DIM ** -0.5
    sc = jnp.einsum('xhnqd,xhnkd->xhnqk', qb, kb).astype(jnp.float32) * scale
    sc = jnp.where(valid, sc + bias[:, None], NEG_INF)
    m = jnp.max(sc, axis=-1, keepdims=True)
    p = jnp.exp(sc - m)
    den = jnp.sum(p, axis=-1, keepdims=True)
    o = jnp.einsum('xhnqk,xhnkd->xhnqd', p, vb.astype(jnp.float32)) / den
    lse = (m + jnp.log(den))[..., 0]

    o = o.reshape(X, h, Lp, hd)[:, :, :L].reshape(b, dilation, h, L, hd)
    o = o.transpose(0, 3, 1, 2, 4).reshape(b, s, h, hd)
    lse = lse.reshape(X, h, Lp)[:, :, :L].reshape(b, dilation, h, L)
    lse = lse.transpose(0, 3, 1, 2).reshape(b, s, h)
    return o, lse


def _s5_branch(u, a_re, a_im, log_dt, b_re, b_im, c_re, c_im, d_skip, w_glu, b_glu):
    f32 = jnp.float32
    bsz, s, _ = u.shape
    lr, li = a_re.astype(f32), a_im.astype(f32)
    dt = jnp.exp(log_dt.astype(f32))[:, None]
    mag = jnp.exp(lr * dt)
    ang = li * dt
    ab_re, ab_im = mag * jnp.cos(ang), mag * jnp.sin(ang)
    nr, ni = ab_re - 1.0, ab_im
    den = lr * lr + li * li
    f_re = (nr * lr + ni * li) / den
    f_im = (ni * lr - nr * li) / den
    br, bi = b_re.astype(f32), b_im.astype(f32)
    bb_re = f_re[..., None] * br - f_im[..., None] * bi
    bb_im = f_re[..., None] * bi + f_im[..., None] * br

    ug = u.astype(f32).reshape(bsz, s, SSM_GROUPS, SSM_GROUP_CH)
    bu_re = jnp.einsum('bsgc,gnc->bsgn', ug, bb_re)
    bu_im = jnp.einsum('bsgc,gnc->bsgn', ug, bb_im)
    a_re_t = jnp.broadcast_to(ab_re, bu_re.shape)
    a_im_t = jnp.broadcast_to(ab_im, bu_re.shape)

    def combine(left, right):
        ar1, ai1, xr1, xi1 = left
        ar2, ai2, xr2, xi2 = right
        return (ar2 * ar1 - ai2 * ai1,
                ar2 * ai1 + ai2 * ar1,
                ar2 * xr1 - ai2 * xi1 + xr2,
                ar2 * xi1 + ai2 * xr1 + xi2)

    _, _, xr, xi = lax.associative_scan(combine, (a_re_t, a_im_t, bu_re, bu_im), axis=1)
    y = (jnp.einsum('bsgn,gcn->bsgc', xr, c_re.astype(f32))
         - jnp.einsum('bsgn,gcn->bsgc', xi, c_im.astype(f32))
         + d_skip.astype(f32).reshape(SSM_GROUPS, SSM_GROUP_CH) * ug)
    y = y.reshape(bsz, s, SSM_WIDTH).astype(u.dtype)
    y = jax.nn.gelu(y)
    return y * jax.nn.sigmoid(y @ w_glu + b_glu)


def _hybrid_mixer(u, w_in, b_gate, a_re, a_im, log_dt, b_re, b_im, c_re, c_im,
                  d_skip, w_glu, b_glu, w_proj_att, w_proj_ssm, w_out):
    bsz, s, _ = u.shape
    proj = u @ w_in
    q, k, v, us, g_att, g_ssm = jnp.split(
        proj, [ATT_WIDTH, 2 * ATT_WIDTH, 3 * ATT_WIDTH, 3 * ATT_WIDTH + SSM_WIDTH,
               3 * ATT_WIDTH + SSM_WIDTH + D_MODEL], axis=-1)
    q = q.reshape(bsz, s, N_HEADS, HEAD_DIM)
    k = k.reshape(bsz, s, N_HEADS, HEAD_DIM)
    v = v.reshape(bsz, s, N_HEADS, HEAD_DIM)

    slopes = _alibi_slopes()
    outs, lses = [], []
    for window, dilation in PATTERNS:
        o, lse = _dilated_window_attention(q, k, v, slopes, window, dilation)
        outs.append(o)
        lses.append(lse)
    wts = jax.nn.softmax(jnp.stack(lses, axis=0), axis=0)
    o_att = jnp.sum(wts[..., None] * jnp.stack(outs, axis=0), axis=0)
    o_att = o_att.reshape(bsz, s, ATT_WIDTH).astype(u.dtype)
    y_att = o_att @ w_proj_att

    y_ssm = _s5_branch(us, a_re, a_im, log_dt, b_re, b_im, c_re, c_im,
                       d_skip, w_glu, b_glu) @ w_proj_ssm

    gb_att, gb_ssm = jnp.split(b_gate, 2, axis=-1)
    merged = jax.nn.sigmoid(g_att + gb_att) * y_att + jax.nn.sigmoid(g_ssm + gb_ssm) * y_ssm
    return merged @ w_out


def _conv_ffn(u, w_up, w_conv, b_conv, w_down):
    s = u.shape[1]
    a, val = jnp.split(u @ w_up, 2, axis=-1)
    ap = jnp.pad(a, ((0, 0), (CONV_W - 1, 0), (0, 0)))
    conv = b_conv
    for j in range(CONV_W):
        conv = conv + w_conv[j] * ap[:, CONV_W - 1 - j:CONV_W - 1 - j + s]
    return (jax.nn.silu(conv) * val) @ w_down


def _fwd_setup_inputs(seed: int = 0) -> dict:
    key = jax.random.key(seed)
    ks = jax.random.split(key, 32)
    f32 = jnp.float32
    L, D, G, N, C = DEPTH, D_MODEL, SSM_GROUPS, SSM_STATE, SSM_GROUP_CH
    nrm = lambda k, shape, sc: jax.random.normal(k, shape, f32) * sc
    inp = {}
    inp["x"] = nrm(ks[0], (BATCH, SEQ, D), 1.0)
    inp["c"] = nrm(ks[1], (BATCH, D), 1.0)
    inp["w_ada"] = nrm(ks[2], (L, D, 6 * D), 0.5 * D ** -0.5)
    inp["b_ada"] = nrm(ks[3], (L, 6 * D), 0.02)
    inp["g_mix"] = 1.0 + nrm(ks[4], (L, D), 0.02)
    inp["w_in"] = nrm(ks[5], (L, D, IN_WIDTH), D ** -0.5)
    inp["b_gate"] = nrm(ks[6], (L, 2 * D), 0.02)
    inp["a_re"] = -0.5 + nrm(ks[7], (L, G, N), 0.01)
    inp["a_im"] = jnp.pi * jnp.arange(N, dtype=f32)[None, None, :] + nrm(ks[8], (L, G, N), 0.01)
    inp["log_dt"] = jax.random.uniform(ks[9], (L, G), f32, math.log(1e-3), math.log(1e-1))
    inp["b_re"] = nrm(ks[10], (L, G, N, C), (2 * C) ** -0.5)
    inp["b_im"] = nrm(ks[11], (L, G, N, C), (2 * C) ** -0.5)
    inp["c_re"] = nrm(ks[12], (L, G, C, N), (2 * N) ** -0.5)
    inp["c_im"] = nrm(ks[13], (L, G, C, N), (2 * N) ** -0.5)
    inp["d_skip"] = nrm(ks[14], (L, SSM_WIDTH), 1.0)
    inp["w_glu"] = nrm(ks[15], (L, SSM_WIDTH, SSM_WIDTH), SSM_WIDTH ** -0.5)
    inp["b_glu"] = nrm(ks[16], (L, SSM_WIDTH), 0.02)
    inp["w_proj_att"] = nrm(ks[17], (L, ATT_WIDTH, D), ATT_WIDTH ** -0.5)
    inp["w_proj_ssm"] = nrm(ks[18], (L, SSM_WIDTH, D), SSM_WIDTH ** -0.5)
    inp["w_out"] = nrm(ks[19], (L, D, D), D ** -0.5)
    inp["g_ffn"] = 1.0 + nrm(ks[20], (L, D), 0.02)
    inp["w_up"] = nrm(ks[21], (L, D, 2 * D_FF), D ** -0.5)
    inp["w_conv"] = nrm(ks[22], (L, CONV_W, D_FF), CONV_W ** -0.5)
    inp["b_conv"] = nrm(ks[23], (L, D_FF), 0.02)
    inp["w_down"] = nrm(ks[24], (L, D_FF, D), D_FF ** -0.5)
    inp["g_final"] = 1.0 + nrm(ks[25], (D,), 0.02)
    return inp


def _fwd_reference(x, c, w_ada, b_ada, g_mix, w_in, b_gate, a_re, a_im, log_dt, b_re, b_im,
              c_re, c_im, d_skip, w_glu, b_glu, w_proj_att, w_proj_ssm, w_out,
              g_ffn, w_up, w_conv, b_conv, w_down, g_final):
    h = x
    c_act = jax.nn.silu(c)
    for l in range(DEPTH):
        mod = c_act @ w_ada[l] + b_ada[l]
        sh1, sc1, gt1, sh2, sc2, gt2 = jnp.split(mod, 6, axis=-1)
        u = _modulate(h, g_mix[l], sh1, sc1)
        h = h + gt1[:, None, :] * _hybrid_mixer(
            u, w_in[l], b_gate[l], a_re[l], a_im[l], log_dt[l], b_re[l], b_im[l],
            c_re[l], c_im[l], d_skip[l], w_glu[l], b_glu[l],
            w_proj_att[l], w_proj_ssm[l], w_out[l])
        u = _modulate(h, g_ffn[l], sh2, sc2)
        h = h + gt2[:, None, :] * _conv_ffn(u, w_up[l], w_conv[l], b_conv[l], w_down[l])
    return _rmsnorm(h, g_final)


import jax as _jax
import jax.numpy as _jnp

TWIN_FORMAT = 'train_step'
FWD_PARAMS = ['x', 'c', 'w_ada', 'b_ada', 'g_mix', 'w_in', 'b_gate', 'a_re', 'a_im', 'log_dt', 'b_re', 'b_im', 'c_re', 'c_im', 'd_skip', 'w_glu', 'b_glu', 'w_proj_att', 'w_proj_ssm', 'w_out', 'g_ffn', 'w_up', 'w_conv', 'b_conv', 'w_down', 'g_final']
TWIN_WEIGHTS = ['w_ada', 'b_ada', 'g_mix', 'w_in', 'b_gate', 'a_re', 'a_im', 'log_dt', 'b_re', 'b_im', 'c_re', 'c_im', 'd_skip', 'w_glu', 'b_glu', 'w_proj_att', 'w_proj_ssm', 'w_out', 'g_ffn', 'w_up', 'w_conv', 'b_conv', 'w_down', 'g_final']
TWIN_DIFF_INPUT = 'x'
TWIN_INPUTS = ['x', 'c', 'w_ada', 'b_ada', 'g_mix', 'w_in', 'b_gate', 'a_re', 'a_im', 'log_dt', 'b_re', 'b_im', 'c_re', 'c_im', 'd_skip', 'w_glu', 'b_glu', 'w_proj_att', 'w_proj_ssm', 'w_out', 'g_ffn', 'w_up', 'w_conv', 'b_conv', 'w_down', 'g_final', 'loss_target', 'm_w_ada', 'm_b_ada', 'm_g_mix', 'm_w_in', 'm_b_gate', 'm_a_re', 'm_a_im', 'm_log_dt', 'm_b_re', 'm_b_im', 'm_c_re', 'm_c_im', 'm_d_skip', 'm_w_glu', 'm_b_glu', 'm_w_proj_att', 'm_w_proj_ssm', 'm_w_out', 'm_g_ffn', 'm_w_up', 'm_w_conv', 'm_b_conv', 'm_w_down', 'm_g_final', 'v_w_ada', 'v_b_ada', 'v_g_mix', 'v_w_in', 'v_b_gate', 'v_a_re', 'v_a_im', 'v_log_dt', 'v_b_re', 'v_b_im', 'v_c_re', 'v_c_im', 'v_d_skip', 'v_w_glu', 'v_b_glu', 'v_w_proj_att', 'v_w_proj_ssm', 'v_w_out', 'v_g_ffn', 'v_w_up', 'v_w_conv', 'v_b_conv', 'v_w_down', 'v_g_final']
TWIN_OUTPUTS = ['loss', 'grad_x', 'grad_w_ada', 'grad_b_ada', 'grad_g_mix', 'grad_w_in', 'grad_b_gate', 'grad_a_re', 'grad_a_im', 'grad_log_dt', 'grad_b_re', 'grad_b_im', 'grad_c_re', 'grad_c_im', 'grad_d_skip', 'grad_w_glu', 'grad_b_glu', 'grad_w_proj_att', 'grad_w_proj_ssm', 'grad_w_out', 'grad_g_ffn', 'grad_w_up', 'grad_w_conv', 'grad_b_conv', 'grad_w_down', 'grad_g_final', 'delta_w_ada', 'delta_b_ada', 'delta_g_mix', 'delta_w_in', 'delta_b_gate', 'delta_a_re', 'delta_a_im', 'delta_log_dt', 'delta_b_re', 'delta_b_im', 'delta_c_re', 'delta_c_im', 'delta_d_skip', 'delta_w_glu', 'delta_b_glu', 'delta_w_proj_att', 'delta_w_proj_ssm', 'delta_w_out', 'delta_g_ffn', 'delta_w_up', 'delta_w_conv', 'delta_b_conv', 'delta_w_down', 'delta_g_final', 'new_m_w_ada', 'new_m_b_ada', 'new_m_g_mix', 'new_m_w_in', 'new_m_b_gate', 'new_m_a_re', 'new_m_a_im', 'new_m_log_dt', 'new_m_b_re', 'new_m_b_im', 'new_m_c_re', 'new_m_c_im', 'new_m_d_skip', 'new_m_w_glu', 'new_m_b_glu', 'new_m_w_proj_att', 'new_m_w_proj_ssm', 'new_m_w_out', 'new_m_g_ffn', 'new_m_w_up', 'new_m_w_conv', 'new_m_b_conv', 'new_m_w_down', 'new_m_g_final', 'new_v_w_ada', 'new_v_b_ada', 'new_v_g_mix', 'new_v_w_in', 'new_v_b_gate', 'new_v_a_re', 'new_v_a_im', 'new_v_log_dt', 'new_v_b_re', 'new_v_b_im', 'new_v_c_re', 'new_v_c_im', 'new_v_d_skip', 'new_v_w_glu', 'new_v_b_glu', 'new_v_w_proj_att', 'new_v_w_proj_ssm', 'new_v_w_out', 'new_v_g_ffn', 'new_v_w_up', 'new_v_w_conv', 'new_v_b_conv', 'new_v_w_down', 'new_v_g_final']
TWIN_LEAF_KINDS = {'loss': 'loss', 'grad_x': 'grad_x', 'grad_w_ada': 'grad_w', 'grad_b_ada': 'grad_w', 'grad_g_mix': 'grad_w', 'grad_w_in': 'grad_w', 'grad_b_gate': 'grad_w', 'grad_a_re': 'grad_w', 'grad_a_im': 'grad_w', 'grad_log_dt': 'grad_w', 'grad_b_re': 'grad_w', 'grad_b_im': 'grad_w', 'grad_c_re': 'grad_w', 'grad_c_im': 'grad_w', 'grad_d_skip': 'grad_w', 'grad_w_glu': 'grad_w', 'grad_b_glu': 'grad_w', 'grad_w_proj_att': 'grad_w', 'grad_w_proj_ssm': 'grad_w', 'grad_w_out': 'grad_w', 'grad_g_ffn': 'grad_w', 'grad_w_up': 'grad_w', 'grad_w_conv': 'grad_w', 'grad_b_conv': 'grad_w', 'grad_w_down': 'grad_w', 'grad_g_final': 'grad_w', 'delta_w_ada': 'delta_w', 'delta_b_ada': 'delta_w', 'delta_g_mix': 'delta_w', 'delta_w_in': 'delta_w', 'delta_b_gate': 'delta_w', 'delta_a_re': 'delta_w', 'delta_a_im': 'delta_w', 'delta_log_dt': 'delta_w', 'delta_b_re': 'delta_w', 'delta_b_im': 'delta_w', 'delta_c_re': 'delta_w', 'delta_c_im': 'delta_w', 'delta_d_skip': 'delta_w', 'delta_w_glu': 'delta_w', 'delta_b_glu': 'delta_w', 'delta_w_proj_att': 'delta_w', 'delta_w_proj_ssm': 'delta_w', 'delta_w_out': 'delta_w', 'delta_g_ffn': 'delta_w', 'delta_w_up': 'delta_w', 'delta_w_conv': 'delta_w', 'delta_b_conv': 'delta_w', 'delta_w_down': 'delta_w', 'delta_g_final': 'delta_w', 'new_m_w_ada': 'new_m', 'new_m_b_ada': 'new_m', 'new_m_g_mix': 'new_m', 'new_m_w_in': 'new_m', 'new_m_b_gate': 'new_m', 'new_m_a_re': 'new_m', 'new_m_a_im': 'new_m', 'new_m_log_dt': 'new_m', 'new_m_b_re': 'new_m', 'new_m_b_im': 'new_m', 'new_m_c_re': 'new_m', 'new_m_c_im': 'new_m', 'new_m_d_skip': 'new_m', 'new_m_w_glu': 'new_m', 'new_m_b_glu': 'new_m', 'new_m_w_proj_att': 'new_m', 'new_m_w_proj_ssm': 'new_m', 'new_m_w_out': 'new_m', 'new_m_g_ffn': 'new_m', 'new_m_w_up': 'new_m', 'new_m_w_conv': 'new_m', 'new_m_b_conv': 'new_m', 'new_m_w_down': 'new_m', 'new_m_g_final': 'new_m', 'new_v_w_ada': 'new_v', 'new_v_b_ada': 'new_v', 'new_v_g_mix': 'new_v', 'new_v_w_in': 'new_v', 'new_v_b_gate': 'new_v', 'new_v_a_re': 'new_v', 'new_v_a_im': 'new_v', 'new_v_log_dt': 'new_v', 'new_v_b_re': 'new_v', 'new_v_b_im': 'new_v', 'new_v_c_re': 'new_v', 'new_v_c_im': 'new_v', 'new_v_d_skip': 'new_v', 'new_v_w_glu': 'new_v', 'new_v_b_glu': 'new_v', 'new_v_w_proj_att': 'new_v', 'new_v_w_proj_ssm': 'new_v', 'new_v_w_out': 'new_v', 'new_v_g_ffn': 'new_v', 'new_v_w_up': 'new_v', 'new_v_w_conv': 'new_v', 'new_v_b_conv': 'new_v', 'new_v_w_down': 'new_v', 'new_v_g_final': 'new_v'}


def _forward(args):
    return _fwd_reference(*[args[k] for k in FWD_PARAMS])


def _output_shape():
    out = _jax.eval_shape(lambda: _forward(_fwd_setup_inputs(0)))
    return out.shape, out.dtype

N_MICROBATCH = 1
ADAM_LR = 0.001
ADAM_B1 = 0.9
ADAM_B2 = 0.999
ADAM_EPS = 1e-08
ADAM_WD = 0.01
ADAM_STEP = 10
PER_EXAMPLE_BATCH_AXIS = {'x': 0, 'c': 0, 'loss_target': 0}
SHARED_INPUTS = []
_WEIGHT_DTYPES = {'w_ada': _jnp.float32, 'b_ada': _jnp.float32, 'g_mix': _jnp.float32, 'w_in': _jnp.float32, 'b_gate': _jnp.float32, 'a_re': _jnp.float32, 'a_im': _jnp.float32, 'log_dt': _jnp.float32, 'b_re': _jnp.float32, 'b_im': _jnp.float32, 'c_re': _jnp.float32, 'c_im': _jnp.float32, 'd_skip': _jnp.float32, 'w_glu': _jnp.float32, 'b_glu': _jnp.float32, 'w_proj_att': _jnp.float32, 'w_proj_ssm': _jnp.float32, 'w_out': _jnp.float32, 'g_ffn': _jnp.float32, 'w_up': _jnp.float32, 'w_conv': _jnp.float32, 'b_conv': _jnp.float32, 'w_down': _jnp.float32, 'g_final': _jnp.float32}
MOMENT_SCALE = {'w_ada': 6.173370e-02, 'b_ada': 9.684659e-02, 'g_mix': 3.363063e-02, 'w_in': 1.875633e-02, 'b_gate': 7.549804e-03, 'a_re': 4.501251e-03, 'a_im': 4.107068e-03, 'log_dt': 2.709125e+00, 'b_re': 1.612596e-03, 'b_im': 1.466372e-03, 'c_re': 3.047710e-03, 'c_im': 3.067778e-03, 'd_skip': 3.598856e-02, 'w_glu': 1.045098e-02, 'b_glu': 1.480533e-02, 'w_proj_att': 2.272677e-02, 'w_proj_ssm': 1.628189e-02, 'w_out': 2.774116e-02, 'g_ffn': 7.802314e-02, 'w_up': 3.948419e-02, 'w_conv': 4.001203e-02, 'b_conv': 3.547408e-02, 'w_down': 5.495872e-02, 'g_final': 6.392431e+01}


def _to_microbatches(a, axis):
    t = _jnp.moveaxis(a, axis, 0)
    t = t.reshape((N_MICROBATCH, t.shape[0] // N_MICROBATCH) + t.shape[1:])
    return _jnp.moveaxis(t, 1, axis + 1)


def setup_inputs(seed: int = 0) -> dict:
    inp = _fwd_setup_inputs(seed)
    key = _jax.random.fold_in(_jax.random.key(seed), 7919)
    shape, _ = _output_shape()
    out = dict(inp)
    out["loss_target"] = _jax.random.normal(_jax.random.fold_in(key, 0), shape, _jnp.float32)
    for i, name in enumerate(TWIN_WEIGHTS):
        w = inp[name].astype(_jnp.float32)
        if MOMENT_SCALE is None:
            s = _jnp.sqrt(_jnp.mean(_jnp.square(w)) + 1e-30)
        else:
            s = MOMENT_SCALE[name]
        km, kv = _jax.random.split(_jax.random.fold_in(key, i + 1))
        out[name] = w
        out["m_" + name] = s * _jax.random.normal(km, w.shape, _jnp.float32)
        out["v_" + name] = (s * s) * _jax.random.uniform(kv, w.shape, _jnp.float32, 0.5, 1.5)
    if N_MICROBATCH > 1:
        for name, axis in PER_EXAMPLE_BATCH_AXIS.items():
            out[name] = _to_microbatches(out[name], axis)
    return {'x': out['x'], 'c': out['c'], 'w_ada': out['w_ada'], 'b_ada': out['b_ada'], 'g_mix': out['g_mix'], 'w_in': out['w_in'], 'b_gate': out['b_gate'], 'a_re': out['a_re'], 'a_im': out['a_im'], 'log_dt': out['log_dt'], 'b_re': out['b_re'], 'b_im': out['b_im'], 'c_re': out['c_re'], 'c_im': out['c_im'], 'd_skip': out['d_skip'], 'w_glu': out['w_glu'], 'b_glu': out['b_glu'], 'w_proj_att': out['w_proj_att'], 'w_proj_ssm': out['w_proj_ssm'], 'w_out': out['w_out'], 'g_ffn': out['g_ffn'], 'w_up': out['w_up'], 'w_conv': out['w_conv'], 'b_conv': out['b_conv'], 'w_down': out['w_down'], 'g_final': out['g_final'], 'loss_target': out['loss_target'], 'm_w_ada': out['m_w_ada'], 'm_b_ada': out['m_b_ada'], 'm_g_mix': out['m_g_mix'], 'm_w_in': out['m_w_in'], 'm_b_gate': out['m_b_gate'], 'm_a_re': out['m_a_re'], 'm_a_im': out['m_a_im'], 'm_log_dt': out['m_log_dt'], 'm_b_re': out['m_b_re'], 'm_b_im': out['m_b_im'], 'm_c_re': out['m_c_re'], 'm_c_im': out['m_c_im'], 'm_d_skip': out['m_d_skip'], 'm_w_glu': out['m_w_glu'], 'm_b_glu': out['m_b_glu'], 'm_w_proj_att': out['m_w_proj_att'], 'm_w_proj_ssm': out['m_w_proj_ssm'], 'm_w_out': out['m_w_out'], 'm_g_ffn': out['m_g_ffn'], 'm_w_up': out['m_w_up'], 'm_w_conv': out['m_w_conv'], 'm_b_conv': out['m_b_conv'], 'm_w_down': out['m_w_down'], 'm_g_final': out['m_g_final'], 'v_w_ada': out['v_w_ada'], 'v_b_ada': out['v_b_ada'], 'v_g_mix': out['v_g_mix'], 'v_w_in': out['v_w_in'], 'v_b_gate': out['v_b_gate'], 'v_a_re': out['v_a_re'], 'v_a_im': out['v_a_im'], 'v_log_dt': out['v_log_dt'], 'v_b_re': out['v_b_re'], 'v_b_im': out['v_b_im'], 'v_c_re': out['v_c_re'], 'v_c_im': out['v_c_im'], 'v_d_skip': out['v_d_skip'], 'v_w_glu': out['v_w_glu'], 'v_b_glu': out['v_b_glu'], 'v_w_proj_att': out['v_w_proj_att'], 'v_w_proj_ssm': out['v_w_proj_ssm'], 'v_w_out': out['v_w_out'], 'v_g_ffn': out['v_g_ffn'], 'v_w_up': out['v_w_up'], 'v_w_conv': out['v_w_conv'], 'v_b_conv': out['v_b_conv'], 'v_w_down': out['v_w_down'], 'v_g_final': out['v_g_final']}


def _loss(weights, diff, rest, loss_target):
    with _jax.named_scope("forward"):
        args = {**rest, TWIN_DIFF_INPUT: diff, **{k: w.astype(_WEIGHT_DTYPES[k]) for k, w in weights.items()}}
        y = _forward(args)
    with _jax.named_scope("loss_head"):
        err = _jnp.square(y.astype(_jnp.float32) - loss_target)
        return 0.5 * _jnp.sum(_jnp.mean(err, axis=-1)) if err.ndim else 0.5 * err


def _adamw(w, g, m, v):
    m = ADAM_B1 * m + (1.0 - ADAM_B1) * g
    v = ADAM_B2 * v + (1.0 - ADAM_B2) * _jnp.square(g)
    m_hat = m / (1.0 - ADAM_B1 ** ADAM_STEP)
    v_hat = v / (1.0 - ADAM_B2 ** ADAM_STEP)
    delta = -ADAM_LR * (m_hat / (_jnp.sqrt(v_hat) + ADAM_EPS) + ADAM_WD * w)
    return delta, m, v


def reference(x, c, w_ada, b_ada, g_mix, w_in, b_gate, a_re, a_im, log_dt, b_re, b_im, c_re, c_im, d_skip, w_glu, b_glu, w_proj_att, w_proj_ssm, w_out, g_ffn, w_up, w_conv, b_conv, w_down, g_final, loss_target, m_w_ada, m_b_ada, m_g_mix, m_w_in, m_b_gate, m_a_re, m_a_im, m_log_dt, m_b_re, m_b_im, m_c_re, m_c_im, m_d_skip, m_w_glu, m_b_glu, m_w_proj_att, m_w_proj_ssm, m_w_out, m_g_ffn, m_w_up, m_w_conv, m_b_conv, m_w_down, m_g_final, v_w_ada, v_b_ada, v_g_mix, v_w_in, v_b_gate, v_a_re, v_a_im, v_log_dt, v_b_re, v_b_im, v_c_re, v_c_im, v_d_skip, v_w_glu, v_b_glu, v_w_proj_att, v_w_proj_ssm, v_w_out, v_g_ffn, v_w_up, v_w_conv, v_b_conv, v_w_down, v_g_final):
    given = dict(x=x, c=c, w_ada=w_ada, b_ada=b_ada, g_mix=g_mix, w_in=w_in, b_gate=b_gate, a_re=a_re, a_im=a_im, log_dt=log_dt, b_re=b_re, b_im=b_im, c_re=c_re, c_im=c_im, d_skip=d_skip, w_glu=w_glu, b_glu=b_glu, w_proj_att=w_proj_att, w_proj_ssm=w_proj_ssm, w_out=w_out, g_ffn=g_ffn, w_up=w_up, w_conv=w_conv, b_conv=b_conv, w_down=w_down, g_final=g_final, loss_target=loss_target, m_w_ada=m_w_ada, m_b_ada=m_b_ada, m_g_mix=m_g_mix, m_w_in=m_w_in, m_b_gate=m_b_gate, m_a_re=m_a_re, m_a_im=m_a_im, m_log_dt=m_log_dt, m_b_re=m_b_re, m_b_im=m_b_im, m_c_re=m_c_re, m_c_im=m_c_im, m_d_skip=m_d_skip, m_w_glu=m_w_glu, m_b_glu=m_b_glu, m_w_proj_att=m_w_proj_att, m_w_proj_ssm=m_w_proj_ssm, m_w_out=m_w_out, m_g_ffn=m_g_ffn, m_w_up=m_w_up, m_w_conv=m_w_conv, m_b_conv=m_b_conv, m_w_down=m_w_down, m_g_final=m_g_final, v_w_ada=v_w_ada, v_b_ada=v_b_ada, v_g_mix=v_g_mix, v_w_in=v_w_in, v_b_gate=v_b_gate, v_a_re=v_a_re, v_a_im=v_a_im, v_log_dt=v_log_dt, v_b_re=v_b_re, v_b_im=v_b_im, v_c_re=v_c_re, v_c_im=v_c_im, v_d_skip=v_d_skip, v_w_glu=v_w_glu, v_b_glu=v_b_glu, v_w_proj_att=v_w_proj_att, v_w_proj_ssm=v_w_proj_ssm, v_w_out=v_w_out, v_g_ffn=v_g_ffn, v_w_up=v_w_up, v_w_conv=v_w_conv, v_b_conv=v_b_conv, v_w_down=v_w_down, v_g_final=v_g_final)
    weights = {n: given[n] for n in TWIN_WEIGHTS}
    shared = {n: given[n] for n in SHARED_INPUTS}
    per_example = {n: given[n] for n in ['x', 'c']}
    grad_fn = _jax.value_and_grad(_loss, argnums=(0, 1))

    def one_microbatch(ex, loss_target):
        ex = dict(ex)
        diff = ex.pop(TWIN_DIFF_INPUT)
        return grad_fn(weights, diff, {**shared, **ex}, loss_target)

    if N_MICROBATCH == 1:
        loss, (grad_w, grad_x) = one_microbatch(per_example, given["loss_target"])
    else:
        def body(carry, xs):
            loss_sum, grad_sum = carry
            l_k, (gw_k, gx_k) = one_microbatch(xs[0], xs[1])
            with _jax.named_scope("update"):
                return (loss_sum + l_k, _jax.tree.map(_jnp.add, grad_sum, gw_k)), gx_k

        init = (_jnp.zeros((), _jnp.float32), _jax.tree.map(_jnp.zeros_like, weights))
        (loss, grad_w), grad_x = _jax.lax.scan(body, init, (per_example, given["loss_target"]))
    with _jax.named_scope("update"):
        delta_w, new_m, new_v = {}, {}, {}
        for n in TWIN_WEIGHTS:
            delta_w[n], new_m[n], new_v[n] = _adamw(weights[n], grad_w[n], given["m_" + n], given["v_" + n])
    return (loss, grad_x, *[grad_w[n] for n in TWIN_WEIGHTS], *[delta_w[n] for n in TWIN_WEIGHTS],
            *[new_m[n] for n in TWIN_WEIGHTS], *[new_v[n] for n in TWIN_WEIGHTS])
```

```python
import functools
import math

import numpy as np
import jax
import jax.numpy as jnp
from jax import lax
from jax.experimental import pallas as pl
from jax.experimental.pallas import tpu as pltpu

F32, BF16 = jnp.float32, jnp.bfloat16

D_MODEL = 1024
N_HEADS = 8
HEAD_DIM = 64
ATT_WIDTH = 512
SSM_GROUPS = 16
SSM_GROUP_CH = 16
SSM_WIDTH = 256
SSM_STATE = 64
SSM_LANES = SSM_GROUPS * SSM_STATE
D_FF = 2048
IN_WIDTH = 3 * ATT_WIDTH + SSM_WIDTH + 2 * D_MODEL
ATT_BLOCK = 128
N_PATTERNS = 3
EPS = 1e-6
NEG_INF = -1e30

ADAM_LR, ADAM_B1, ADAM_B2, ADAM_EPS, ADAM_WD, ADAM_STEP = 0.001, 0.9, 0.999, 1e-08, 0.01, 10

V7X_VMEM_LIMIT_BYTES = 56 * 1024 * 1024
LANES = 1024

MESH_AXES = ("x", "y", "c")


def _pcall(body, *, name, out_shape, grid=(), in_specs=None, out_specs=None, scratch_shapes=(), dims=None):
    params = dict(vmem_limit_bytes=V7X_VMEM_LIMIT_BYTES)
    if dims is not None:
        params["dimension_semantics"] = dims
    specs = {}
    if in_specs is not None:
        specs = dict(grid=grid, in_specs=in_specs, out_specs=out_specs)
    return pl.pallas_call(body, name=name, out_shape=out_shape, scratch_shapes=scratch_shapes,
                          compiler_params=pltpu.CompilerParams(**params), **specs)


def _sds(shape, dtype):
    return jax.ShapeDtypeStruct(tuple(shape), dtype)


def _tile(n, target):
    if n <= target:
        return n
    for t in range(target - target % 128, 0, -128):
        if n % t == 0:
            return t
    raise ValueError((n, target))


def _sig(v):
    return 1.0 / (1.0 + jnp.exp(-v))


def _mm(a, b, *, name, ta=False, tb=False, out_dtype=F32, tm=1024, tn=1024, tk=1024):
    if ta:
        K, M = a.shape
    else:
        M, K = a.shape
    if tb:
        N, K2 = b.shape
    else:
        K2, N = b.shape
    assert K == K2, (a.shape, b.shape)
    tm, tn, tk = _tile(M, tm), _tile(N, tn), _tile(K, tk)
    nk = K // tk
    a_spec = pl.BlockSpec((tk, tm), lambda i, j, k: (k, i)) if ta else pl.BlockSpec((tm, tk), lambda i, j, k: (i, k))
    b_spec = pl.BlockSpec((tn, tk), lambda i, j, k: (j, k)) if tb else pl.BlockSpec((tk, tn), lambda i, j, k: (k, j))
    dn = (((0 if ta else 1,), (1 if tb else 0,)), ((), ()))

    def body(a_ref, b_ref, o_ref, acc_ref):
        k = pl.program_id(2)

        @pl.when(k == 0)
        def _():
            acc_ref[...] = jnp.zeros_like(acc_ref)

        acc_ref[...] += lax.dot_general(a_ref[...].astype(BF16), b_ref[...].astype(BF16), dn,
                                        preferred_element_type=F32)

        @pl.when(k == nk - 1)
        def _():
            o_ref[...] = acc_ref[...].astype(out_dtype)

    return _pcall(body, name=name, out_shape=_sds((M, N), out_dtype), grid=(M // tm, N // tn, nk),
                  in_specs=[a_spec, b_spec], out_specs=pl.BlockSpec((tm, tn), lambda i, j, k: (i, j)),
                  scratch_shapes=[pltpu.VMEM((tm, tn), F32)], dims=("parallel", "parallel", "arbitrary"))(a, b)


def _ada_fwd(c_all, w_ada, b_ada_cols):
    n = w_ada.shape[1]

    def body(c_ref, w_ref, b_ref, o_ref):
        c = c_ref[...]
        act = c * _sig(c)
        o_ref[...] = jnp.dot(act.astype(BF16), w_ref[...].astype(BF16), preferred_element_type=F32) + b_ref[...]

    return _pcall(body, name="ada_fwd", out_shape=_sds((c_all.shape[0], n), F32))(c_all, w_ada, b_ada_cols)


def _ada_bwd(c_all, dmod_all, dmod_cols):
    n = dmod_cols.shape[1]

    def body(c_ref, da_ref, dc_ref, gw_ref, gb_ref):
        c = c_ref[...]
        act = c * _sig(c)
        gw_ref[...] = lax.dot_general(act, dc_ref[...], (((0,), (0,)), ((), ())), preferred_element_type=F32,
                                      precision=lax.Precision.HIGHEST)
        gb_ref[...] = jnp.sum(da_ref[...], axis=0, keepdims=True)

    return _pcall(body, name="ada_bwd", out_shape=(_sds((D_MODEL, n), F32), _sds((1, dmod_all.shape[1]), F32)))(
        c_all, dmod_all, dmod_cols)


ROW_TILE = 512


def _row_specs(B, S):
    ts = min(S, ROW_TILE)
    row = pl.BlockSpec((1, ts, D_MODEL), lambda b, s: (b, s, 0))
    bvec = pl.BlockSpec((1, 1, D_MODEL), lambda b, s: (b, 0, 0))
    gvec = pl.BlockSpec((1, D_MODEL), lambda b, s: (0, 0))
    return ts, row, bvec, gvec


def _norm_mod(x3, g, sc, sh):
    B, S, _ = x3.shape
    ts, row, bvec, gvec = _row_specs(B, S)

    def body(x_ref, g_ref, sc_ref, sh_ref, u_ref):
        x = x_ref[0]
        r = lax.rsqrt(jnp.mean(x * x, axis=-1, keepdims=True) + EPS)
        u_ref[0] = ((x * r) * g_ref[...] * (1.0 + sc_ref[0]) + sh_ref[0]).astype(BF16)

    return _pcall(body, name="norm_mod1", out_shape=_sds(x3.shape, BF16), grid=(B, S // ts),
                  in_specs=[row, gvec, bvec, bvec], out_specs=row, dims=("parallel", "parallel"))(x3, g, sc, sh)


def _resid_norm_mod(x3, mix3, gt, g, sc, sh):
    B, S, _ = x3.shape
    ts, row, bvec, gvec = _row_specs(B, S)

    def body(x_ref, m_ref, gt_ref, g_ref, sc_ref, sh_ref, h_ref, u_ref):
        h = x_ref[0] + gt_ref[0] * m_ref[0]
        h_ref[0] = h
        r = lax.rsqrt(jnp.mean(h * h, axis=-1, keepdims=True) + EPS)
        u_ref[0] = ((h * r) * g_ref[...] * (1.0 + sc_ref[0]) + sh_ref[0]).astype(BF16)

    return _pcall(body, name="resid_norm_mod2", out_shape=(_sds(x3.shape, F32), _sds(x3.shape, BF16)),
                  grid=(B, S // ts), in_specs=[row, row, bvec, gvec, bvec, bvec], out_specs=(row, row),
                  dims=("parallel", "parallel"))(x3, mix3, gt, g, sc, sh)


def _norm_bwd(h3, du3, dres3, g, sc, name, mix3=None, gt=None):
    B, S, _ = h3.shape
    ts, row, bvec, gvec = _row_specs(B, S)
    with_gate = mix3 is not None

    def body(*refs):
        if with_gate:
            h_ref, du_ref, dr_ref, g_ref, sc_ref, m_ref, gt_ref, dh_ref, dsh_ref, dsc_ref, dg_ref, dgt_ref, dm_ref = refs
        else:
            h_ref, du_ref, dr_ref, g_ref, sc_ref, dh_ref, dsh_ref, dsc_ref, dg_ref = refs
        b, s = pl.program_id(0), pl.program_id(1)
        h = h_ref[0]
        r = lax.rsqrt(jnp.mean(h * h, axis=-1, keepdims=True) + EPS)
        xn = h * r
        du = du_ref[0]
        g = g_ref[...]
        sc1 = 1.0 + sc_ref[0]
        dxn = du * g * sc1
        dh = dr_ref[0] + r * (dxn - xn * jnp.mean(dxn * xn, axis=-1, keepdims=True))
        dh_ref[0] = dh

        @pl.when(s == 0)
        def _():
            dsh_ref[...] = jnp.zeros_like(dsh_ref)
            dsc_ref[...] = jnp.zeros_like(dsc_ref)
            if with_gate:
                dgt_ref[...] = jnp.zeros_like(dgt_ref)

        @pl.when((s == 0) & (b == 0))
        def _():
            dg_ref[...] = jnp.zeros_like(dg_ref)

        dux = du * xn
        dsh_ref[0] += jnp.sum(du, axis=0, keepdims=True)
        dsc_ref[0] += jnp.sum(dux * g, axis=0, keepdims=True)
        dg_ref[...] += jnp.sum(dux * sc1, axis=0, keepdims=True)
        if with_gate:
            dgt_ref[0] += jnp.sum(dh * m_ref[0], axis=0, keepdims=True)
            dm_ref[0] = (dh * gt_ref[0]).astype(BF16)

    bshape = _sds((B, 1, D_MODEL), F32)
    in_specs = [row, row, row, gvec, bvec]
    out_shape = [_sds(h3.shape, F32), bshape, bshape, _sds((1, D_MODEL), F32)]
    out_specs = [row, bvec, bvec, gvec]
    args = [h3, du3, dres3, g, sc]
    if with_gate:
        in_specs += [row, bvec]
        out_shape += [bshape, _sds(h3.shape, BF16)]
        out_specs += [bvec, row]
        args += [mix3, gt]
    return _pcall(body, name=name, out_shape=tuple(out_shape), grid=(B, S // ts), in_specs=in_specs,
                  out_specs=tuple(out_specs), dims=("arbitrary", "arbitrary"))(*args)


def _final_loss(h1, ffn3, tgt3, gt, gfin):
    B, S, _ = h1.shape
    ts, row, bvec, gvec = _row_specs(B, S)
    one = pl.BlockSpec((1, 1), lambda b, s: (0, 0))

    def body(h_ref, f_ref, t_ref, gt_ref, gf_ref, dh_ref, dff_ref, dgt_ref, dgf_ref, loss_ref):
        b, s = pl.program_id(0), pl.program_id(1)
        f = f_ref[0]
        gtv = gt_ref[0]
        gf = gf_ref[...]
        h2 = h_ref[0] + gtv * f
        r = lax.rsqrt(jnp.mean(h2 * h2, axis=-1, keepdims=True) + EPS)
        n = h2 * r
        e = n * gf - t_ref[0]
        dy = e * (1.0 / D_MODEL)
        dn = dy * gf
        dh2 = r * (dn - n * jnp.mean(dn * n, axis=-1, keepdims=True))
        dh_ref[0] = dh2
        dff_ref[0] = (dh2 * gtv).astype(BF16)

        @pl.when(s == 0)
        def _():
            dgt_ref[...] = jnp.zeros_like(dgt_ref)

        @pl.when((s == 0) & (b == 0))
        def _():
            dgf_ref[...] = jnp.zeros_like(dgf_ref)
            loss_ref[...] = jnp.zeros_like(loss_ref)

        dgt_ref[0] += jnp.sum(dh2 * f, axis=0, keepdims=True)
        dgf_ref[...] += jnp.sum(dy * n, axis=0, keepdims=True)
        rows = jnp.sum(e * e, axis=1, keepdims=True)
        loss_ref[...] += jnp.sum(rows, axis=0, keepdims=True) * (0.5 / D_MODEL)

    return _pcall(body, name="final_loss",
                  out_shape=(_sds(h1.shape, F32), _sds(h1.shape, BF16), _sds((B, 1, D_MODEL), F32),
                             _sds((1, D_MODEL), F32), _sds((1, 1), F32)),
                  grid=(B, S // ts), in_specs=[row, row, row, bvec, gvec], out_specs=(row, row, bvec, gvec, one),
                  dims=("arbitrary", "arbitrary"))(h1, ffn3, tgt3, gt, gfin)


def _att_scores(qh, kc, kp, h, dil, first, a_idx, j_idx):
    scale = HEAD_DIM ** -0.5
    nt = (((1,), (1,)), ((), ()))
    slope = (2.0 ** (-8.0 * (h + 1) / N_HEADS)) * dil
    dist_c = (a_idx - j_idx).astype(F32)
    s_c = lax.dot_general(qh, kc, nt, preferred_element_type=F32) * scale
    s_c = jnp.where(a_idx >= j_idx, s_c - slope * dist_c, NEG_INF)
    s_p = lax.dot_general(qh, kp, nt, preferred_element_type=F32) * scale
    s_p = jnp.where((j_idx >= a_idx) & jnp.logical_not(first), s_p - slope * (dist_c + float(ATT_BLOCK)), NEG_INF)
    return s_c, s_p


def _att_block_consts(seq_blocks):
    p = pl.program_id(0)
    j = pl.program_id(1)
    nb = lax.shift_right_logical(jnp.int32(seq_blocks), 2 * p)
    dil = lax.shift_left(jnp.int32(1), 2 * p).astype(F32)
    a_idx = lax.broadcasted_iota(jnp.int32, (ATT_BLOCK, ATT_BLOCK), 0)
    j_idx = lax.broadcasted_iota(jnp.int32, (ATT_BLOCK, ATT_BLOCK), 1)
    return j, nb, dil, a_idx, j_idx


def _attn_fwd(qb, kb, vb, seq_blocks):
    _, NB, _, _ = qb.shape
    cur = pl.BlockSpec((None, None, ATT_BLOCK, ATT_WIDTH), lambda p, j: (p, j, 0, 0))
    prev = pl.BlockSpec((None, None, ATT_BLOCK, ATT_WIDTH), lambda p, j: (p, jnp.maximum(j - 1, 0), 0, 0))
    lse_spec = pl.BlockSpec((None, None, ATT_BLOCK, N_HEADS), lambda p, j: (p, j, 0, 0))

    def body(q_ref, kc_ref, kp_ref, vc_ref, vp_ref, o_ref, lse_ref):
        j, nb, dil, a_idx, j_idx = _att_block_consts(seq_blocks)
        first = lax.rem(j, nb) == 0
        for h in range(N_HEADS):
            hs = slice(h * HEAD_DIM, (h + 1) * HEAD_DIM)
            s_c, s_p = _att_scores(q_ref[:, hs], kc_ref[:, hs], kp_ref[:, hs], h, dil, first, a_idx, j_idx)
            m = jnp.maximum(jnp.max(s_c, axis=1, keepdims=True), jnp.max(s_p, axis=1, keepdims=True))
            p_c = jnp.exp(s_c - m)
            p_p = jnp.exp(s_p - m)
            den = jnp.sum(p_c, axis=1, keepdims=True) + jnp.sum(p_p, axis=1, keepdims=True)
            o = (jnp.dot(p_c.astype(BF16), vc_ref[:, hs], preferred_element_type=F32)
                 + jnp.dot(p_p.astype(BF16), vp_ref[:, hs], preferred_element_type=F32))
            o_ref[:, hs] = o / den
            lse_ref[:, h:h + 1] = m + jnp.log(den)

    return _pcall(body, name="attn_fwd",
                  out_shape=(_sds(qb.shape, F32), _sds((N_PATTERNS, NB, ATT_BLOCK, N_HEADS), F32)),
                  grid=(N_PATTERNS, NB), in_specs=[cur, cur, prev, cur, prev], out_specs=(cur, lse_spec),
                  dims=("parallel", "parallel"))(qb, kb, kb, vb, vb)


def _attn_combine(o_p, lse_p):
    _, T, _ = o_p.shape
    tm = min(T, 1024)

    def body(o_ref, l_ref, out_ref, lse_ref):
        l0, l1, l2 = l_ref[0], l_ref[1], l_ref[2]
        m = jnp.maximum(jnp.maximum(l0, l1), l2)
        lse = m + jnp.log(jnp.exp(l0 - m) + jnp.exp(l1 - m) + jnp.exp(l2 - m))
        lse_ref[...] = lse
        w = [jnp.exp(l0 - lse), jnp.exp(l1 - lse), jnp.exp(l2 - lse)]
        for h in range(N_HEADS):
            hs = slice(h * HEAD_DIM, (h + 1) * HEAD_DIM)
            acc = w[0][:, h:h + 1] * o_ref[0, :, hs]
            acc = acc + w[1][:, h:h + 1] * o_ref[1, :, hs]
            acc = acc + w[2][:, h:h + 1] * o_ref[2, :, hs]
            out_ref[:, hs] = acc.astype(BF16)

    return _pcall(body, name="attn_combine", out_shape=(_sds((T, ATT_WIDTH), BF16), _sds((T, N_HEADS), F32)),
                  grid=(T // tm,),
                  in_specs=[pl.BlockSpec((N_PATTERNS, tm, ATT_WIDTH), lambda i: (0, i, 0)),
                            pl.BlockSpec((N_PATTERNS, tm, N_HEADS), lambda i: (0, i, 0))],
                  out_specs=(pl.BlockSpec((tm, ATT_WIDTH), lambda i: (i, 0)), pl.BlockSpec((tm, N_HEADS), lambda i: (i, 0))),
                  dims=("parallel",))(o_p, lse_p)


def _attn_bwd(qb, kb, vb, dob, ob, lseb, seq_blocks):
    _, NB, _, _ = qb.shape
    last = NB - 1
    cur = pl.BlockSpec((None, None, ATT_BLOCK, ATT_WIDTH), lambda p, j: (p, jnp.minimum(j, last), 0, 0))
    prev = pl.BlockSpec((None, None, ATT_BLOCK, ATT_WIDTH),
                        lambda p, j: (p, jnp.maximum(jnp.minimum(j, last) - 1, 0), 0, 0))
    lag = pl.BlockSpec((None, None, ATT_BLOCK, ATT_WIDTH), lambda p, j: (p, jnp.maximum(j - 1, 0), 0, 0))
    lse_spec = pl.BlockSpec((None, None, ATT_BLOCK, N_HEADS), lambda p, j: (p, jnp.minimum(j, last), 0, 0))
    scale = HEAD_DIM ** -0.5
    tn = (((0,), (0,)), ((), ()))
    nt = (((1,), (1,)), ((), ()))

    def body(q_ref, kc_ref, kp_ref, vc_ref, vp_ref, do_ref, o_ref, lse_ref, dq_ref, dk_ref, dv_ref, ck_ref, cv_ref):
        j, nb, dil, a_idx, j_idx = _att_block_consts(seq_blocks)

        @pl.when(j == 0)
        def _():
            ck_ref[...] = jnp.zeros_like(ck_ref)
            cv_ref[...] = jnp.zeros_like(cv_ref)

        @pl.when(j <= last)
        def _():
            first = lax.rem(j, nb) == 0
            for h in range(N_HEADS):
                hs = slice(h * HEAD_DIM, (h + 1) * HEAD_DIM)
                qh, kc, kp, vc, vp, doh = q_ref[:, hs], kc_ref[:, hs], kp_ref[:, hs], vc_ref[:, hs], vp_ref[:, hs], do_ref[:, hs]
                s_c, s_p = _att_scores(qh, kc, kp, h, dil, first, a_idx, j_idx)
                lse = lse_ref[:, h:h + 1]
                p_c = jnp.exp(s_c - lse)
                p_p = jnp.exp(s_p - lse)
                delta = jnp.sum(doh.astype(F32) * o_ref[:, hs].astype(F32), axis=1, keepdims=True)
                ds_c = (p_c * (lax.dot_general(doh, vc, nt, preferred_element_type=F32) - delta)).astype(BF16)
                ds_p = (p_p * (lax.dot_general(doh, vp, nt, preferred_element_type=F32) - delta)).astype(BF16)
                dq_ref[:, hs] = (jnp.dot(ds_c, kc, preferred_element_type=F32)
                                 + jnp.dot(ds_p, kp, preferred_element_type=F32)) * scale
                dk_ref[:, hs] = ck_ref[:, hs] + lax.dot_general(ds_p, qh, tn, preferred_element_type=F32) * scale
                dv_ref[:, hs] = cv_ref[:, hs] + lax.dot_general(p_p.astype(BF16), doh, tn, preferred_element_type=F32)
                ck_ref[:, hs] = lax.dot_general(ds_c, qh, tn, preferred_element_type=F32) * scale
                cv_ref[:, hs] = lax.dot_general(p_c.astype(BF16), doh, tn, preferred_element_type=F32)

        @pl.when(j == NB)
        def _():
            dk_ref[...] = ck_ref[...]
            dv_ref[...] = cv_ref[...]

    shp = _sds(qb.shape, F32)
    return _pcall(body, name="attn_bwd", out_shape=(shp, shp, shp), grid=(N_PATTERNS, NB + 1),
                  in_specs=[cur, cur, prev, cur, prev, cur, cur, lse_spec], out_specs=(cur, lag, lag),
                  scratch_shapes=[pltpu.VMEM((ATT_BLOCK, ATT_WIDTH), F32), pltpu.VMEM((ATT_BLOCK, ATT_WIDTH), F32)],
                  dims=("arbitrary", "arbitrary"))(qb, kb, kb, vb, vb, dob, ob, lseb)


def _sum3_cast(a, b, c):
    T, N = a.shape
    tm = min(T, 1024)
    spec = pl.BlockSpec((tm, N), lambda i: (i, 0))

    def body(a_ref, b_ref, c_ref, o_ref):
        o_ref[...] = (a_ref[...] + b_ref[...] + c_ref[...]).astype(BF16)

    return _pcall(body, name="sum3_cast", out_shape=_sds((T, N), BF16), grid=(T // tm,), in_specs=[spec] * 3,
                  out_specs=spec, dims=("parallel",))(a, b, c)


def _to_blocks(t, B, S):
    C = t.shape[-1]
    outs = []
    for p in range(N_PATTERNS):
        d = 4 ** p
        u = t.reshape(B, S // d, d, C).transpose(0, 2, 1, 3)
        outs.append(u.reshape(B * S // ATT_BLOCK, ATT_BLOCK, C))
    return jnp.stack(outs, axis=0)


def _from_blocks(tb, B, S):
    C = tb.shape[-1]
    outs = []
    for p in range(N_PATTERNS):
        d = 4 ** p
        u = tb[p].reshape(B, d, S // d, C).transpose(0, 2, 1, 3)
        outs.append(u.reshape(B * S, C))
    return jnp.stack(outs, axis=0)


def _expand_groups(m):
    rows = SSM_WIDTH
    t = jnp.concatenate([m] * SSM_GROUPS, axis=0)
    r = lax.broadcasted_iota(jnp.int32, (rows, SSM_LANES), 0)
    l = lax.broadcasted_iota(jnp.int32, (rows, SSM_LANES), 1)
    keep = lax.shift_right_logical(r, 4) == lax.shift_right_logical(l, 6)
    return jnp.where(keep, t, 0.0)


def _collapse_groups(m):
    rows = SSM_WIDTH
    r = lax.broadcasted_iota(jnp.int32, (rows, SSM_LANES), 0)
    l = lax.broadcasted_iota(jnp.int32, (rows, SSM_LANES), 1)
    keep = lax.shift_right_logical(r, 4) == lax.shift_right_logical(l, 6)
    t = jnp.where(keep, m, 0.0)
    acc = t[0:SSM_GROUP_CH]
    for g in range(1, SSM_GROUPS):
        acc = acc + t[g * SSM_GROUP_CH:(g + 1) * SSM_GROUP_CH]
    return acc


def _zoh(lr, li, ldt):
    dt = jnp.exp(ldt)
    mag = jnp.exp(lr * dt)
    ang = li * dt
    cs, sn = jnp.cos(ang), jnp.sin(ang)
    ab_re, ab_im = mag * cs, mag * sn
    nr, ni = ab_re - 1.0, ab_im
    den = lr * lr + li * li
    n_re = nr * lr + ni * li
    n_im = ni * lr - nr * li
    return dict(dt=dt, mag=mag, cs=cs, sn=sn, ab_re=ab_re, ab_im=ab_im, nr=nr, ni=ni, den=den, n_re=n_re, n_im=n_im,
                f_re=n_re / den, f_im=n_im / den)


def _ssm_params(lr, li, ldt, br, bi, cr, ci):
    def body(lr_ref, li_ref, ldt_ref, br_ref, bi_ref, cr_ref, ci_ref, ab_ref, w_ref, c_ref):
        z = _zoh(lr_ref[...], li_ref[...], ldt_ref[...])
        ab_ref[0:1, :] = z["ab_re"]
        ab_ref[1:2, :] = z["ab_im"]
        br, bi = br_ref[...], bi_ref[...]
        w_ref[:, 0:SSM_LANES] = _expand_groups(z["f_re"] * br - z["f_im"] * bi).astype(BF16)
        w_ref[:, SSM_LANES:] = _expand_groups(z["f_re"] * bi + z["f_im"] * br).astype(BF16)
        c_ref[:, 0:SSM_LANES] = _expand_groups(cr_ref[...]).astype(BF16)
        c_ref[:, SSM_LANES:] = _expand_groups(-ci_ref[...]).astype(BF16)

    return _pcall(body, name="ssm_params",
                  out_shape=(_sds((2, SSM_LANES), F32), _sds((SSM_WIDTH, 2 * SSM_LANES), BF16),
                             _sds((SSM_WIDTH, 2 * SSM_LANES), BF16)))(lr, li, ldt, br, bi, cr, ci)


def _ssm_params_bwd(lr, li, ldt, br, bi, dab, dw, dc):
    def body(lr_ref, li_ref, ldt_ref, br_ref, bi_ref, dab_ref, dw_ref, dc_ref,
             dlr_ref, dli_ref, dldt_ref, dbr_ref, dbi_ref, dcr_ref, dci_ref):
        lr, li = lr_ref[...], li_ref[...]
        z = _zoh(lr, li, ldt_ref[...])
        br, bi = br_ref[...], bi_ref[...]
        dbb_re = _collapse_groups(dw_ref[:, 0:SSM_LANES])
        dbb_im = _collapse_groups(dw_ref[:, SSM_LANES:])
        dcr_ref[...] = _collapse_groups(dc_ref[:, 0:SSM_LANES])
        dci_ref[...] = -_collapse_groups(dc_ref[:, SSM_LANES:])
        f_re, f_im = z["f_re"], z["f_im"]
        dbr_ref[...] = f_re * dbb_re + f_im * dbb_im
        dbi_ref[...] = f_re * dbb_im - f_im * dbb_re
        df_re = jnp.sum(dbb_re * br + dbb_im * bi, axis=0, keepdims=True)
        df_im = jnp.sum(dbb_im * br - dbb_re * bi, axis=0, keepdims=True)
        den = z["den"]
        dn_re, dn_im = df_re / den, df_im / den
        dden = -(df_re * z["n_re"] + df_im * z["n_im"]) / (den * den)
        dnr = dn_re * lr - dn_im * li
        dni = dn_re * li + dn_im * lr
        dlr = dn_re * z["nr"] + dn_im * z["ni"] + 2.0 * dden * lr
        dli = dn_re * z["ni"] - dn_im * z["nr"] + 2.0 * dden * li
        dab_re = dab_ref[0:1, :] + dnr
        dab_im = dab_ref[1:2, :] + dni
        mag, cs, sn, dt = z["mag"], z["cs"], z["sn"], z["dt"]
        dmag = dab_re * cs + dab_im * sn
        dang = mag * (dab_im * cs - dab_re * sn)
        dlr_ref[...] = dlr + dmag * mag * dt
        dli_ref[...] = dli + dang * dt
        ddt = dmag * mag * lr + dang * li
        per_lane = jnp.broadcast_to(ddt * dt, (8, SSM_LANES))
        lane = lax.broadcasted_iota(jnp.int32, (SSM_LANES, 128), 0)
        col = lax.broadcasted_iota(jnp.int32, (SSM_LANES, 128), 1)
        ind = jnp.where(lax.shift_right_logical(lane, 6) == col, 1.0, 0.0)
        dldt_ref[...] = jnp.dot(per_lane, ind, preferred_element_type=F32, precision=lax.Precision.HIGHEST)[0:1]

    vec = _sds((1, SSM_LANES), F32)
    mat = _sds((SSM_GROUP_CH, SSM_LANES), F32)
    return _pcall(body, name="ssm_params_bwd", out_shape=(vec, vec, _sds((1, 128), F32), mat, mat, mat, mat))(
        lr, li, ldt, br, bi, dab, dw, dc)


SCAN_CHUNK = 512


def _scan_consts(ar, ai, k_ref, reverse):
    row = lax.broadcasted_iota(jnp.int32, (8, SSM_LANES), 0)
    pw = [(ar, ai)]
    for _ in range(7):
        pr, pi = pw[-1]
        pw.append((pr * ar - pi * ai, pr * ai + pi * ar))
    for n, k in enumerate((1, 2, 4)):
        keep = (row < 8 - k) if reverse else (row >= k)
        k_ref[2 * n] = jnp.where(keep, jnp.broadcast_to(pw[k - 1][0], (8, SSM_LANES)), 0.0)
        k_ref[2 * n + 1] = jnp.where(keep, jnp.broadcast_to(pw[k - 1][1], (8, SSM_LANES)), 0.0)
    cr = jnp.zeros((8, SSM_LANES), F32)
    ci = jnp.zeros((8, SSM_LANES), F32)
    for r in range(8):
        e = (8 - r) if reverse else (r + 1)
        cr = jnp.where(row == r, jnp.broadcast_to(pw[e - 1][0], (8, SSM_LANES)), cr)
        ci = jnp.where(row == r, jnp.broadcast_to(pw[e - 1][1], (8, SSM_LANES)), ci)
    k_ref[6] = cr
    k_ref[7] = ci


def _scan_tile(xr, xi, k_ref, car, cai, reverse):
    for n, k in enumerate((1, 2, 4)):
        sh = (8 - k) if reverse else k
        sr = pltpu.roll(xr, sh, 0)
        si = pltpu.roll(xi, sh, 0)
        mr, mi = k_ref[2 * n], k_ref[2 * n + 1]
        xr, xi = xr + mr * sr - mi * si, xi + mr * si + mi * sr
    pr, pi = k_ref[6], k_ref[7]
    xr, xi = xr + pr * car - pi * cai, xi + pr * cai + pi * car
    return xr, xi


def _scan_fwd(bu3, abar):
    B, S, _ = bu3.shape
    ch = min(S, SCAN_CHUNK)
    blk = pl.BlockSpec((1, ch, 2 * SSM_LANES), lambda b, c: (b, c, 0))

    def body(ab_ref, bu_ref, x_ref, k_ref, carry_ref):
        _scan_consts(ab_ref[0:1, :], ab_ref[1:2, :], k_ref, False)

        @pl.when(pl.program_id(1) == 0)
        def _():
            carry_ref[...] = jnp.zeros_like(carry_ref)

        def step(i, carry):
            base = pl.multiple_of(i * 8, 8)
            xr = bu_ref[0, pl.ds(base, 8), 0:SSM_LANES]
            xi = bu_ref[0, pl.ds(base, 8), SSM_LANES:]
            xr, xi = _scan_tile(xr, xi, k_ref, carry[0], carry[1], False)
            x_ref[0, pl.ds(base, 8), 0:SSM_LANES] = xr
            x_ref[0, pl.ds(base, 8), SSM_LANES:] = xi
            return (jnp.broadcast_to(xr[7:8], (8, SSM_LANES)), jnp.broadcast_to(xi[7:8], (8, SSM_LANES)))

        cr, ci = lax.fori_loop(0, ch // 8, step, (carry_ref[0], carry_ref[1]))
        carry_ref[0] = cr
        carry_ref[1] = ci

    return _pcall(body, name="scan_fwd", out_shape=_sds(bu3.shape, F32), grid=(B, S // ch),
                  in_specs=[pl.BlockSpec((2, SSM_LANES), lambda b, c: (0, 0)), blk], out_specs=blk,
                  scratch_shapes=[pltpu.VMEM((8, 8, SSM_LANES), F32), pltpu.VMEM((2, 8, SSM_LANES), F32)],
                  dims=("arbitrary", "arbitrary"))(abar, bu3)


def _scan_bwd(dx3, xs3, abar):
    B, S, _ = dx3.shape
    ch = min(S, SCAN_CHUNK)
    nc = S // ch
    blk = pl.BlockSpec((1, ch, 2 * SSM_LANES), lambda b, c: (b, nc - 1 - c, 0))

    def body(ab_ref, dx_ref, xs_ref, g_ref, da_ref, k_ref, carry_ref, acc_ref):
        b, c = pl.program_id(0), pl.program_id(1)
        _scan_consts(ab_ref[0:1, :], -ab_ref[1:2, :], k_ref, True)
        row = lax.broadcasted_iota(jnp.int32, (8, SSM_LANES), 0)

        @pl.when(c == 0)
        def _():
            carry_ref[...] = jnp.zeros_like(carry_ref)

        @pl.when((c == 0) & (b == 0))
        def _():
            acc_ref[...] = jnp.zeros_like(acc_ref)

        def step(i, carry):
            car, cai, ar_acc, ai_acc = carry
            base = pl.multiple_of((ch // 8 - 1 - i) * 8, 8)
            gr = dx_ref[0, pl.ds(base, 8), 0:SSM_LANES]
            gi = dx_ref[0, pl.ds(base, 8), SSM_LANES:]
            gr, gi = _scan_tile(gr, gi, k_ref, car, cai, True)
            g_ref[0, pl.ds(base, 8), 0:SSM_LANES] = gr
            g_ref[0, pl.ds(base, 8), SSM_LANES:] = gi
            nr = jnp.where(row == 7, car, pltpu.roll(gr, 7, 0))
            ni = jnp.where(row == 7, cai, pltpu.roll(gi, 7, 0))
            xr = xs_ref[0, pl.ds(base, 8), 0:SSM_LANES]
            xi = xs_ref[0, pl.ds(base, 8), SSM_LANES:]
            ar_acc = ar_acc + nr * xr + ni * xi
            ai_acc = ai_acc + ni * xr - nr * xi
            return (jnp.broadcast_to(gr[0:1], (8, SSM_LANES)), jnp.broadcast_to(gi[0:1], (8, SSM_LANES)), ar_acc, ai_acc)

        cr, ci, ar_acc, ai_acc = lax.fori_loop(0, ch // 8, step, (carry_ref[0], carry_ref[1], acc_ref[0], acc_ref[1]))
        carry_ref[0] = cr
        carry_ref[1] = ci
        acc_ref[0] = ar_acc
        acc_ref[1] = ai_acc
        da_ref[0:1, :] = jnp.sum(ar_acc, axis=0, keepdims=True)
        da_ref[1:2, :] = jnp.sum(ai_acc, axis=0, keepdims=True)

    return _pcall(body, name="scan_bwd", out_shape=(_sds(dx3.shape, F32), _sds((2, SSM_LANES), F32)), grid=(B, nc),
                  in_specs=[pl.BlockSpec((2, SSM_LANES), lambda b, c: (0, 0)), blk, blk],
                  out_specs=(blk, pl.BlockSpec((2, SSM_LANES), lambda b, c: (0, 0))),
                  scratch_shapes=[pltpu.VMEM((8, 8, SSM_LANES), F32), pltpu.VMEM((2, 8, SSM_LANES), F32),
                                  pltpu.VMEM((2, 8, SSM_LANES), F32)],
                  dims=("arbitrary", "arbitrary"))(abar, dx3, xs3)


GELU_K = math.sqrt(2.0 / math.pi)
GELU_C = 0.044715


def _gelu_parts(y):
    t = jnp.tanh(GELU_K * (y + GELU_C * y * y * y))
    return 0.5 * y * (1.0 + t), t


def _ssm_post(yc, us, dsk, wglu, bglu):
    T, N = yc.shape
    tm = min(T, 1024)
    row = pl.BlockSpec((tm, N), lambda i: (i, 0))
    vec = pl.BlockSpec((1, N), lambda i: (0, 0))
    mat = pl.BlockSpec((N, N), lambda i: (0, 0))

    def body(yc_ref, us_ref, d_ref, w_ref, b_ref, y_ref, s_ref):
        y = yc_ref[...] + d_ref[...] * us_ref[...]
        y_ref[...] = y
        z, _ = _gelu_parts(y)
        gl = jnp.dot(z.astype(BF16), w_ref[...], preferred_element_type=F32) + b_ref[...]
        s_ref[...] = (z * _sig(gl)).astype(BF16)

    return _pcall(body, name="ssm_post", out_shape=(_sds((T, N), F32), _sds((T, N), BF16)), grid=(T // tm,),
                  in_specs=[row, row, vec, mat, vec], out_specs=(row, row), dims=("parallel",))(yc, us, dsk, wglu, bglu)


def _ssm_post_bwd(y5, us, ds, dsk, wglu, bglu):
    T, N = y5.shape
    tm = min(T, 1024)
    row = pl.BlockSpec((tm, N), lambda i: (i, 0))
    vec = pl.BlockSpec((1, N), lambda i: (0, 0))
    mat = pl.BlockSpec((N, N), lambda i: (0, 0))

    def body(y_ref, us_ref, ds_ref, d_ref, w_ref, b_ref, dy_ref, dd_ref, db_ref, dw_ref):
        @pl.when(pl.program_id(0) == 0)
        def _():
            dd_ref[...] = jnp.zeros_like(dd_ref)
            db_ref[...] = jnp.zeros_like(db_ref)
            dw_ref[...] = jnp.zeros_like(dw_ref)

        y = y_ref[...]
        z, t = _gelu_parts(y)
        zb = z.astype(BF16)
        gl = jnp.dot(zb, w_ref[...], preferred_element_type=F32) + b_ref[...]
        sg = _sig(gl)
        ds = ds_ref[...]
        dgl = ds * z * sg * (1.0 - sg)
        dglb = dgl.astype(BF16)
        dz = ds * sg + lax.dot_general(dglb, w_ref[...], (((1,), (1,)), ((), ())), preferred_element_type=F32)
        dgelu = 0.5 * (1.0 + t) + 0.5 * y * (1.0 - t * t) * GELU_K * (1.0 + 3.0 * GELU_C * y * y)
        dy = dz * dgelu
        dy_ref[...] = dy
        dd_ref[...] += jnp.sum(dy * us_ref[...], axis=0, keepdims=True)
        db_ref[...] += jnp.sum(dgl, axis=0, keepdims=True)
        dw_ref[...] += lax.dot_general(zb, dglb, (((0,), (0,)), ((), ())), preferred_element_type=F32)

    return _pcall(body, name="ssm_post_bwd",
                  out_shape=(_sds((T, N), F32), _sds((1, N), F32), _sds((1, N), F32), _sds((N, N), F32)),
                  grid=(T // tm,), in_specs=[row, row, row, vec, mat, vec], out_specs=(row, vec, vec, mat),
                  dims=("arbitrary",))(y5, us, ds, dsk, wglu, bglu)


def _add_scaled_cast(a, b, s):
    T, N = a.shape
    tm = min(T, 1024)
    row = pl.BlockSpec((tm, N), lambda i: (i, 0))

    def body(a_ref, b_ref, s_ref, o_ref):
        o_ref[...] = (a_ref[...] + s_ref[...] * b_ref[...]).astype(BF16)

    return _pcall(body, name="add_scaled_cast", out_shape=_sds((T, N), BF16), grid=(T // tm,),
                  in_specs=[row, row, pl.BlockSpec((1, N), lambda i: (0, 0))], out_specs=row, dims=("parallel",))(a, b, s)


GATE_TILE = 256
GATE_ATT_BLOCK0 = (3 * ATT_WIDTH + SSM_WIDTH) // GATE_TILE
GATE_SSM_BLOCK0 = (3 * ATT_WIDTH + SSM_WIDTH + D_MODEL) // GATE_TILE


def _merge(proj, y_att, y_ssm, b_gate):
    T = proj.shape[0]
    tm = min(T, 1024)
    nj = D_MODEL // GATE_TILE
    ga = pl.BlockSpec((tm, GATE_TILE), lambda i, j: (i, GATE_ATT_BLOCK0 + j))
    gs = pl.BlockSpec((tm, GATE_TILE), lambda i, j: (i, GATE_SSM_BLOCK0 + j))
    yy = pl.BlockSpec((tm, GATE_TILE), lambda i, j: (i, j))
    ba = pl.BlockSpec((1, GATE_TILE), lambda i, j: (0, j))
    bs = pl.BlockSpec((1, GATE_TILE), lambda i, j: (0, nj + j))

    def body(ga_ref, gs_ref, ya_ref, ys_ref, ba_ref, bs_ref, o_ref):
        o_ref[...] = (_sig(ga_ref[...] + ba_ref[...]) * ya_ref[...]
                      + _sig(gs_ref[...] + bs_ref[...]) * ys_ref[...]).astype(BF16)

    return _pcall(body, name="merge", out_shape=_sds((T, D_MODEL), BF16), grid=(T // tm, nj),
                  in_specs=[ga, gs, yy, yy, ba, bs], out_specs=yy, dims=("parallel", "parallel"))(
        proj, proj, y_att, y_ssm, b_gate, b_gate)


def _merge_bwd(proj, y_att, y_ssm, b_gate, dmerged):
    T = proj.shape[0]
    tm = min(T, 1024)
    nj = D_MODEL // GATE_TILE
    ga = pl.BlockSpec((tm, GATE_TILE), lambda j, i: (i, GATE_ATT_BLOCK0 + j))
    gs = pl.BlockSpec((tm, GATE_TILE), lambda j, i: (i, GATE_SSM_BLOCK0 + j))
    yy = pl.BlockSpec((tm, GATE_TILE), lambda j, i: (i, j))
    ba = pl.BlockSpec((1, GATE_TILE), lambda j, i: (0, j))
    bs = pl.BlockSpec((1, GATE_TILE), lambda j, i: (0, nj + j))

    def body(ga_ref, gs_ref, ya_ref, ys_ref, ba_ref, bs_ref, dm_ref, dya_ref, dys_ref, dga_ref, dgs_ref, dba_ref, dbs_ref):
        @pl.when(pl.program_id(1) == 0)
        def _():
            dba_ref[...] = jnp.zeros_like(dba_ref)
            dbs_ref[...] = jnp.zeros_like(dbs_ref)

        dm = dm_ref[...]
        sa = _sig(ga_ref[...] + ba_ref[...])
        ss = _sig(gs_ref[...] + bs_ref[...])
        dya_ref[...] = (dm * sa).astype(BF16)
        dys_ref[...] = (dm * ss).astype(BF16)
        dga = dm * ya_ref[...] * sa * (1.0 - sa)
        dgs = dm * ys_ref[...] * ss * (1.0 - ss)
        dga_ref[...] = dga.astype(BF16)
        dgs_ref[...] = dgs.astype(BF16)
        dba_ref[...] += jnp.sum(dga, axis=0, keepdims=True)
        dbs_ref[...] += jnp.sum(dgs, axis=0, keepdims=True)

    big = _sds((T, D_MODEL), BF16)
    vec = _sds((1, D_MODEL), F32)
    return _pcall(body, name="merge_bwd", out_shape=(big, big, big, big, vec, vec), grid=(nj, T // tm),
                  in_specs=[ga, gs, yy, yy, ba, bs, yy], out_specs=(yy, yy, yy, yy, ba, ba),
                  dims=("arbitrary", "arbitrary"))(proj, proj, y_att, y_ssm, b_gate, b_gate, dmerged)


CONV_TILE = 256


def _conv_pre(a, w_ref, b_ref, row):
    conv = b_ref[...] + w_ref[0:1, :] * a
    shifted = []
    for j in (1, 2):
        sh = jnp.where(row >= j, pltpu.roll(a, j, 0), 0.0)
        shifted.append(sh)
        conv = conv + w_ref[j:j + 1, :] * sh
    return conv, shifted


def _conv_act(up3, w_conv, b_conv):
    B, S, _ = up3.shape
    nj = D_FF // CONV_TILE
    a_spec = pl.BlockSpec((1, S, CONV_TILE), lambda b, j: (b, 0, j))
    v_spec = pl.BlockSpec((1, S, CONV_TILE), lambda b, j: (b, 0, nj + j))
    w_spec = pl.BlockSpec((3, CONV_TILE), lambda b, j: (0, j))
    b_spec = pl.BlockSpec((1, CONV_TILE), lambda b, j: (0, j))

    def body(a_ref, v_ref, w_ref, b_ref, o_ref):
        a = a_ref[0]
        row = lax.broadcasted_iota(jnp.int32, a.shape, 0)
        conv, _ = _conv_pre(a, w_ref, b_ref, row)
        o_ref[0] = (conv * _sig(conv) * v_ref[0]).astype(BF16)

    return _pcall(body, name="conv_act", out_shape=_sds((B, S, D_FF), BF16), grid=(B, nj),
                  in_specs=[a_spec, v_spec, w_spec, b_spec], out_specs=a_spec, dims=("parallel", "parallel"))(
        up3, up3, w_conv, b_conv)


def _conv_bwd(up3, dact3, w_conv, b_conv):
    B, S, _ = up3.shape
    nj = D_FF // CONV_TILE
    a_spec = pl.BlockSpec((1, S, CONV_TILE), lambda j, b: (b, 0, j))
    v_spec = pl.BlockSpec((1, S, CONV_TILE), lambda j, b: (b, 0, nj + j))
    w_spec = pl.BlockSpec((3, CONV_TILE), lambda j, b: (0, j))
    b_spec = pl.BlockSpec((1, CONV_TILE), lambda j, b: (0, j))

    def body(a_ref, v_ref, d_ref, w_ref, b_ref, da_ref, dv_ref, dw_ref, db_ref):
        @pl.when(pl.program_id(1) == 0)
        def _():
            dw_ref[...] = jnp.zeros_like(dw_ref)
            db_ref[...] = jnp.zeros_like(db_ref)

        a = a_ref[0]
        d = d_ref[0]
        row = lax.broadcasted_iota(jnp.int32, a.shape, 0)
        conv, shifted = _conv_pre(a, w_ref, b_ref, row)
        sg = _sig(conv)
        dv_ref[0] = (d * conv * sg).astype(BF16)
        dconv = d * v_ref[0] * (sg * (1.0 + conv * (1.0 - sg)))
        da = w_ref[0:1, :] * dconv
        for j in (1, 2):
            da = da + w_ref[j:j + 1, :] * jnp.where(row < S - j, pltpu.roll(dconv, S - j, 0), 0.0)
        da_ref[0] = da.astype(BF16)
        db_ref[...] += jnp.sum(dconv, axis=0, keepdims=True)
        dw_ref[0:1, :] += jnp.sum(dconv * a, axis=0, keepdims=True)
        dw_ref[1:2, :] += jnp.sum(dconv * shifted[0], axis=0, keepdims=True)
        dw_ref[2:3, :] += jnp.sum(dconv * shifted[1], axis=0, keepdims=True)

    big = _sds((B, S, D_FF), BF16)
    return _pcall(body, name="conv_bwd", out_shape=(big, big, _sds((3, D_FF), F32), _sds((1, D_FF), F32)),
                  grid=(nj, B), in_specs=[a_spec, v_spec, a_spec, w_spec, b_spec],
                  out_specs=(a_spec, a_spec, w_spec, b_spec), dims=("arbitrary", "arbitrary"))(
        up3, up3, dact3, w_conv, b_conv)


def _rows_tile(r):
    for t in (512, 256, 128, 64, 40, 32, 16, 8):
        if r % t == 0:
            return t
    return r


def _add2(a, b, out_dtype):
    R, N = a.shape
    tr = _rows_tile(R)
    spec = pl.BlockSpec((tr, N), lambda i: (i, 0))

    def body(a_ref, b_ref, o_ref):
        o_ref[...] = (a_ref[...] + b_ref[...]).astype(out_dtype)

    return _pcall(body, name="add2", out_shape=_sds((R, N), out_dtype), grid=(R // tr,), in_specs=[spec, spec],
                  out_specs=spec, dims=("parallel",))(a, b)


def _sum_slots(q, name):
    n, R, N = q.shape
    tr = _rows_tile(R)

    def body(q_ref, o_ref):
        acc = q_ref[0].astype(F32)
        for s in range(1, n):
            acc = acc + q_ref[s].astype(F32)
        o_ref[...] = acc

    return _pcall(body, name=name, out_shape=_sds((R, N), F32), grid=(R // tr,),
                  in_specs=[pl.BlockSpec((n, tr, N), lambda i: (0, i, 0))], out_specs=pl.BlockSpec((tr, N), lambda i: (i, 0)),
                  dims=("parallel",))(q)


def _adamw(w, g, m, v, name):
    R, N = w.shape
    tr = _rows_tile(R) if R * N * 4 > (1 << 20) else R
    tr = min(tr, 256) if R % 256 == 0 and R > 256 else tr
    spec = pl.BlockSpec((tr, N), lambda i: (i, 0))
    bc1 = 1.0 - ADAM_B1 ** ADAM_STEP
    bc2 = 1.0 - ADAM_B2 ** ADAM_STEP

    def body(w_ref, g_ref, m_ref, v_ref, d_ref, nm_ref, nv_ref):
        g = g_ref[...]
        m = ADAM_B1 * m_ref[...] + (1.0 - ADAM_B1) * g
        v = ADAM_B2 * v_ref[...] + (1.0 - ADAM_B2) * (g * g)
        nm_ref[...] = m
        nv_ref[...] = v
        d_ref[...] = -ADAM_LR * ((m / bc1) / (jnp.sqrt(v / bc2) + ADAM_EPS) + ADAM_WD * w_ref[...])

    shp = _sds((R, N), F32)
    return _pcall(body, name=name, out_shape=(shp, shp, shp), grid=(R // tr,), in_specs=[spec] * 4,
                  out_specs=(spec, spec, spec), dims=("parallel",))(w, g, m, v)


_GROUP_MASKS = {
    "all": [(dx, dy, dc) for dx in (0, 1) for dy in (0, 1) for dc in (0, 1) if (dx, dy, dc) != (0, 0, 0)],
    "xy": [(1, 0, 0), (0, 1, 0), (1, 1, 0)],
    "c": [(0, 0, 1)],
}
_GROUP_SLOTS = {"all": 8, "xy": 4, "c": 2}


def _group_slot(group, x, y, c):
    return {"all": 4 * x + 2 * y + c, "xy": 2 * x + y, "c": c}[group]


def _flip(v, d):
    return 1 - v if d else v


def _exchange(arr, group, mode, name):
    masks = _GROUP_MASKS[group]
    n = len(masks)
    if mode == "gather":
        out_shape = (_GROUP_SLOTS[group],) + arr.shape
    elif mode == "scatter":
        assert arr.shape[0] == _GROUP_SLOTS[group]
        out_shape = arr.shape
    else:
        assert group == "c"
        half = arr.shape[1] // 2
        out_shape = (arr.shape[0], half, arr.shape[2])

    def body(x_ref, o_ref, send_sems, recv_sems, local_sem):
        x, y, c = lax.axis_index("x"), lax.axis_index("y"), lax.axis_index("c")
        me = _group_slot(group, x, y, c)
        local = None
        if mode == "gather":
            local = pltpu.make_async_copy(x_ref, o_ref.at[me], local_sem)
        elif mode == "scatter":
            local = pltpu.make_async_copy(x_ref.at[me], o_ref.at[me], local_sem)
        if local is not None:
            local.start()
        copies = []
        for k, (dx, dy, dc) in enumerate(masks):
            px, py, pc = _flip(x, dx), _flip(y, dy), _flip(c, dc)
            if mode == "gather":
                src, dst = x_ref, o_ref.at[me]
            elif mode == "scatter":
                src, dst = x_ref.at[_group_slot(group, px, py, pc)], o_ref.at[me]
            else:
                src, dst = x_ref.at[:, pl.ds(pl.multiple_of(pc * half, 8), half), :], o_ref
            cp = pltpu.make_async_remote_copy(src_ref=src, dst_ref=dst, send_sem=send_sems.at[k], recv_sem=recv_sems.at[k],
                                              device_id=(px, py, pc), device_id_type=pl.DeviceIdType.MESH)
            cp.start()
            copies.append(cp)
        for cp in copies:
            cp.wait()
        if local is not None:
            local.wait()

    anyspec = pl.BlockSpec(memory_space=pl.ANY)
    return pl.pallas_call(body, name=name, out_shape=_sds(out_shape, arr.dtype), in_specs=[anyspec], out_specs=anyspec,
                          scratch_shapes=[pltpu.SemaphoreType.DMA((n,)), pltpu.SemaphoreType.DMA((n,)),
                                          pltpu.SemaphoreType.DMA(())])(arr)


BIG = (("w_in", (D_MODEL, IN_WIDTH), 1), ("w_out", (D_MODEL, D_MODEL), 0), ("w_up", (D_MODEL, 2 * D_FF), 1),
       ("w_down", (D_FF, D_MODEL), 0), ("w_proj_att", (ATT_WIDTH, D_MODEL), 1), ("w_proj_ssm", (SSM_WIDTH, D_MODEL), 1),
       ("w_glu", (SSM_WIDTH, SSM_WIDTH), 0))
N_XY = 4


def _big_rows(shape):
    return shape[0] * shape[1] // N_XY // LANES


FLAT_ROWS = sum(_big_rows(s) for _, s, _ in BIG)


def _shard_shape(shape, axis):
    return (shape[0] // N_XY, shape[1]) if axis == 0 else (shape[0], shape[1] // N_XY)


def _flatten_shards(shards):
    return jnp.concatenate([shards[n].reshape(_big_rows(s), LANES) for n, s, _ in BIG], axis=0)


def _unflatten_shard(flat):
    out, r = {}, 0
    for n, s, ax in BIG:
        k = _big_rows(s)
        out[n] = flat[r:r + k].reshape(_shard_shape(s, ax))
        r += k
    return out


def _unflatten_full(flat4):
    out, r = {}, 0
    for n, s, ax in BIG:
        k = _big_rows(s)
        sh = _shard_shape(s, ax)
        t = flat4[:, r:r + k].reshape((N_XY,) + sh)
        out[n] = t.reshape(s) if ax == 0 else t.transpose(1, 0, 2).reshape(s)
        r += k
    return out


def _flatten_full(full):
    parts = []
    for n, s, ax in BIG:
        sh = _shard_shape(s, ax)
        t = full[n]
        t = t.reshape((N_XY,) + sh) if ax == 0 else t.reshape(s[0], N_XY, sh[1]).transpose(1, 0, 2)
        parts.append(t.reshape(N_XY, _big_rows(s), LANES))
    return jnp.concatenate(parts, axis=1)


def _pack_rows(arrs):
    rows, counts = [], []
    for a in arrs:
        f = a.reshape(-1)
        k = -(-f.shape[0] // LANES)
        rows.append(jnp.pad(f, (0, k * LANES - f.shape[0])).reshape(k, LANES))
        counts.append(k)
    return jnp.concatenate(rows, axis=0), counts


def _unpack_rows(buf, shapes):
    out, r = [], 0
    for s in shapes:
        size = int(np.prod(s))
        k = -(-size // LANES)
        out.append(buf[r:r + k].reshape(-1)[:size].reshape(s))
        r += k
    return out


def _lanes_from_groups(a):
    return a.transpose(2, 0, 1).reshape(SSM_GROUP_CH, SSM_LANES)


def _groups_from_lanes(a):
    return a.reshape(SSM_GROUP_CH, SSM_GROUPS, SSM_STATE).transpose(1, 2, 0)


def _local_step(x3, mod, tgt3, W, P):
    B, S, _ = x3.shape
    T = B * S
    seq_blocks = S // ATT_BLOCK
    sh1, sc1, gt1, sh2, sc2, gt2 = [m.reshape(B, 1, D_MODEL) for m in jnp.split(mod, 6, axis=-1)]
    g_mix, g_ffn, g_final = P["g_mix"].reshape(1, D_MODEL), P["g_ffn"].reshape(1, D_MODEL), P["g_final"].reshape(1, D_MODEL)
    b_gate = P["b_gate"].reshape(1, 2 * D_MODEL)
    d_skip, b_glu = P["d_skip"].reshape(1, SSM_WIDTH), P["b_glu"].reshape(1, SSM_WIDTH)
    w_conv, b_conv = P["w_conv"], P["b_conv"].reshape(1, D_FF)

    u1 = _norm_mod(x3, g_mix, sc1, sh1).reshape(T, D_MODEL)
    proj = _mm(u1, W["w_in"], name="mm_proj")
    q, k, v = (proj[:, i * ATT_WIDTH:(i + 1) * ATT_WIDTH].astype(BF16) for i in range(3))
    us = proj[:, 3 * ATT_WIDTH:3 * ATT_WIDTH + SSM_WIDTH]
    qb, kb, vb = _to_blocks(q, B, S), _to_blocks(k, B, S), _to_blocks(v, B, S)
    o_pb, lse_pb = _attn_fwd(qb, kb, vb, seq_blocks)
    o_att, lse = _attn_combine(_from_blocks(o_pb, B, S), _from_blocks(lse_pb, B, S))
    y_att = _mm(o_att, W["w_proj_att"], name="mm_proj_att")

    lr = P["a_re"].reshape(1, SSM_LANES)
    li = P["a_im"].reshape(1, SSM_LANES)
    ldt = jnp.repeat(P["log_dt"], SSM_STATE).reshape(1, SSM_LANES)
    br, bi = _lanes_from_groups(P["b_re"]), _lanes_from_groups(P["b_im"])
    cr = P["c_re"].transpose(1, 0, 2).reshape(SSM_GROUP_CH, SSM_LANES)
    ci = P["c_im"].transpose(1, 0, 2).reshape(SSM_GROUP_CH, SSM_LANES)
    abar, w_bu, w_c = _ssm_params(lr, li, ldt, br, bi, cr, ci)
    bu = _mm(us, w_bu, name="mm_bu")
    xs = _scan_fwd(bu.reshape(B, S, 2 * SSM_LANES), abar).reshape(T, 2 * SSM_LANES)
    y_core = _mm(xs, w_c, tb=True, name="mm_ssm_out")
    y5, s_out = _ssm_post(y_core, us, d_skip, W["w_glu"], b_glu)
    y_ssm = _mm(s_out, W["w_proj_ssm"], name="mm_proj_ssm")

    merged = _merge(proj, y_att, y_ssm, b_gate)
    mix = _mm(merged, W["w_out"], name="mm_out")
    mix3 = mix.reshape(B, S, D_MODEL)

    h1, u2 = _resid_norm_mod(x3, mix3, gt1, g_ffn, sc2, sh2)
    u2 = u2.reshape(T, D_MODEL)
    up3 = _mm(u2, W["w_up"], name="mm_up").reshape(B, S, 2 * D_FF)
    act = _conv_act(up3, w_conv, b_conv).reshape(T, D_FF)
    ffn3 = _mm(act, W["w_down"], name="mm_down").reshape(B, S, D_MODEL)
    dh2, dffn, dgt2, dg_final, loss = _final_loss(h1, ffn3, tgt3, gt2, g_final)

    dffn = dffn.reshape(T, D_MODEL)
    gw = {}
    gw["w_down"] = _mm(act, dffn, ta=True, name="mm_dw_down")
    dact3 = _mm(dffn, W["w_down"], tb=True, name="mm_dact").reshape(B, S, D_FF)
    da3, dval3, dw_conv, db_conv = _conv_bwd(up3, dact3, w_conv, b_conv)
    dup = jnp.concatenate([da3.reshape(T, D_FF), dval3.reshape(T, D_FF)], axis=1)
    gw["w_up"] = _mm(u2, dup, ta=True, name="mm_dw_up")
    du2 = _mm(dup, W["w_up"], tb=True, name="mm_du2").reshape(B, S, D_MODEL)
    dh1, dsh2, dsc2, dg_ffn, dgt1, dmix = _norm_bwd(h1, du2, dh2, g_ffn, sc2, "norm_bwd2", mix3=mix3, gt=gt1)

    dmix = dmix.reshape(T, D_MODEL)
    gw["w_out"] = _mm(merged, dmix, ta=True, name="mm_dw_out")
    dmerged = _mm(dmix, W["w_out"], tb=True, name="mm_dmerged")
    dy_att, dy_ssm, dga, dgs, db_att, db_ssm = _merge_bwd(proj, y_att, y_ssm, b_gate, dmerged)

    gw["w_proj_ssm"] = _mm(s_out, dy_ssm, ta=True, name="mm_dw_proj_ssm")
    ds_out = _mm(dy_ssm, W["w_proj_ssm"], tb=True, name="mm_ds_out")
    dy5, dd_skip, db_glu, dw_glu = _ssm_post_bwd(y5, us, ds_out, d_skip, W["w_glu"], b_glu)
    gw["w_glu"] = dw_glu
    dxs = _mm(dy5, w_c, name="mm_dxs")
    dwc = _mm(dy5, xs, ta=True, name="mm_dwc")
    g3, dab = _scan_bwd(dxs.reshape(B, S, 2 * SSM_LANES), xs.reshape(B, S, 2 * SSM_LANES), abar)
    gs2 = g3.reshape(T, 2 * SSM_LANES)
    dwbu = _mm(us, gs2, ta=True, name="mm_dwbu")
    dus_core = _mm(gs2, w_bu, tb=True, name="mm_dus")
    dus = _add_scaled_cast(dus_core, dy5, d_skip)
    dlr, dli, dldt, dbr, dbi, dcr, dci = _ssm_params_bwd(lr, li, ldt, br, bi, dab, dwbu, dwc)

    gw["w_proj_att"] = _mm(o_att, dy_att, ta=True, name="mm_dw_proj_att")
    do_att = _mm(dy_att, W["w_proj_att"], tb=True, out_dtype=BF16, name="mm_do_att")
    dqb, dkb, dvb = _attn_bwd(qb, kb, vb, _to_blocks(do_att, B, S), _to_blocks(o_att, B, S), _to_blocks(lse, B, S),
                              seq_blocks)
    dq = _sum3_cast(*_from_blocks(dqb, B, S))
    dk = _sum3_cast(*_from_blocks(dkb, B, S))
    dv = _sum3_cast(*_from_blocks(dvb, B, S))
    dproj = jnp.concatenate([dq, dk, dv, dus, dga, dgs], axis=1)
    gw["w_in"] = _mm(u1, dproj, ta=True, name="mm_dw_in")
    du1 = _mm(dproj, W["w_in"], tb=True, name="mm_du1").reshape(B, S, D_MODEL)
    dx, dsh1, dsc1, dg_mix = _norm_bwd(x3, du1, dh1, g_mix, sc1, "norm_bwd1")

    dmod = jnp.concatenate([t.reshape(B, D_MODEL) for t in (dsh1, dsc1, dgt1, dsh2, dsc2, dgt2)], axis=1)
    gs = dict(
        g_mix=dg_mix.reshape(D_MODEL), b_gate=jnp.concatenate([db_att, db_ssm], axis=1).reshape(2 * D_MODEL),
        a_re=dlr.reshape(SSM_GROUPS, SSM_STATE), a_im=dli.reshape(SSM_GROUPS, SSM_STATE), log_dt=dldt[0, :SSM_GROUPS],
        b_re=_groups_from_lanes(dbr), b_im=_groups_from_lanes(dbi),
        c_re=dcr.reshape(SSM_GROUP_CH, SSM_GROUPS, SSM_STATE).transpose(1, 0, 2),
        c_im=dci.reshape(SSM_GROUP_CH, SSM_GROUPS, SSM_STATE).transpose(1, 0, 2),
        d_skip=dd_skip.reshape(SSM_WIDTH), b_glu=db_glu.reshape(SSM_WIDTH), g_ffn=dg_ffn.reshape(D_MODEL),
        w_conv=dw_conv, b_conv=db_conv.reshape(D_FF), g_final=dg_final.reshape(D_MODEL))
    return loss, dx, dmod, gw, gs


WEIGHTS = ['w_ada', 'b_ada', 'g_mix', 'w_in', 'b_gate', 'a_re', 'a_im', 'log_dt', 'b_re', 'b_im', 'c_re', 'c_im', 'd_skip',
           'w_glu', 'b_glu', 'w_proj_att', 'w_proj_ssm', 'w_out', 'g_ffn', 'w_up', 'w_conv', 'b_conv', 'w_down', 'g_final']
SMALL = ['g_mix', 'b_gate', 'a_re', 'a_im', 'log_dt', 'b_re', 'b_im', 'c_re', 'c_im', 'd_skip', 'b_glu', 'g_ffn', 'w_conv',
         'b_conv', 'g_final']


def kernel(x, c, w_ada, b_ada, g_mix, w_in, b_gate, a_re, a_im, log_dt, b_re, b_im, c_re, c_im, d_skip, w_glu, b_glu, w_proj_att, w_proj_ssm, w_out, g_ffn, w_up, w_conv, b_conv, w_down, g_final, loss_target, m_w_ada, m_b_ada, m_g_mix, m_w_in, m_b_gate, m_a_re, m_a_im, m_log_dt, m_b_re, m_b_im, m_c_re, m_c_im, m_d_skip, m_w_glu, m_b_glu, m_w_proj_att, m_w_proj_ssm, m_w_out, m_g_ffn, m_w_up, m_w_conv, m_b_conv, m_w_down, m_g_final, v_w_ada, v_b_ada, v_g_mix, v_w_in, v_b_gate, v_a_re, v_a_im, v_log_dt, v_b_re, v_b_im, v_c_re, v_c_im, v_d_skip, v_w_glu, v_b_glu, v_w_proj_att, v_w_proj_ssm, v_w_out, v_g_ffn, v_w_up, v_w_conv, v_b_conv, v_w_down, v_g_final):
    args = dict(locals())
    w = {n: args[n] for n in WEIGHTS}
    m = {n: args["m_" + n] for n in WEIGHTS}
    v = {n: args["v_" + n] for n in WEIGHTS}
    B, S, _ = x.shape
    ix, iy, ic = lax.axis_index("x"), lax.axis_index("y"), lax.axis_index("c")
    chip = 2 * ix + iy
    half = FLAT_ROWS // 2
    ada_cols = w_ada.shape[2]

    c_all = _exchange(c, "all", "gather", "gather_c").reshape(8 * B, D_MODEL)
    b_cols = lax.dynamic_slice_in_dim(b_ada, chip * ada_cols, ada_cols, axis=1)
    mod_cols = _ada_fwd(c_all, w_ada[0], b_cols)
    mod_all = _exchange(mod_cols, "xy", "gather", "gather_mod")
    mod_all = mod_all.transpose(1, 0, 2).reshape(8 * B, 6 * D_MODEL)
    mod = lax.dynamic_slice_in_dim(mod_all, (4 * ix + 2 * iy + ic) * B, B, axis=0)

    flat = _flatten_shards({n: w[n][0] for n, _, _ in BIG}).astype(BF16)
    mine = lax.dynamic_slice_in_dim(flat, ic * half, half, axis=0)
    halves = _exchange(mine, "xy", "gather", "gather_w_chips")
    both = _exchange(halves, "c", "gather", "gather_w_cores")
    W = _unflatten_full(both.transpose(1, 0, 2, 3).reshape(N_XY, FLAT_ROWS, LANES))

    wc_all = _exchange(w_conv[0], "xy", "gather", "gather_w_conv")
    P = {n: w[n][0] for n in SMALL if n not in ("w_conv", "g_final")}
    P["w_conv"] = wc_all.transpose(1, 0, 2).reshape(3, D_FF)
    P["g_final"] = g_final

    loss, dx, dmod, gw, gs = _local_step(x, mod, loss_target, W, P)

    loss = lax.psum(loss[0, 0], MESH_AXES)

    small_shapes = [gs[n].shape for n in SMALL]
    packed, counts = _pack_rows([gs[n] for n in SMALL] + [dmod])
    n_small = sum(counts[:-1])
    gathered = _exchange(packed, "all", "gather", "gather_small")
    small_sum = _sum_slots(gathered[:, :n_small], "sum_small")
    g_small = dict(zip(SMALL, _unpack_rows(small_sum, small_shapes)))
    dmod_all = gathered[:, n_small:].reshape(8, -1)[:, :B * 6 * D_MODEL].reshape(8 * B, 6 * D_MODEL)
    dmod_cols = lax.dynamic_slice_in_dim(dmod_all, chip * ada_cols, ada_cols, axis=1)
    g_w_ada, g_b_ada = _ada_bwd(c_all, dmod_all, dmod_cols)

    G = _flatten_full(gw)
    theirs = _exchange(G, "c", "half", "reduce_cores")
    ours = lax.dynamic_slice_in_dim(G, ic * half, half, axis=1)
    pair = _add2(ours.reshape(N_XY * half, LANES), theirs.reshape(N_XY * half, LANES), BF16).reshape(N_XY, half, LANES)
    parts = _exchange(pair, "xy", "scatter", "reduce_chips")
    red = _sum_slots(parts, "sum_chips")
    g_flat = _exchange(red, "c", "gather", "share_cores").reshape(FLAT_ROWS, LANES)
    g_big = _unflatten_shard(g_flat)

    grads = {"w_ada": g_w_ada[None], "b_ada": g_b_ada}
    for n, _, _ in BIG:
        grads[n] = g_big[n][None]
    wc_cols = w_conv.shape[2]
    for n in SMALL:
        g = g_small[n]
        if n == "w_conv":
            g = lax.dynamic_slice_in_dim(g, chip * wc_cols, wc_cols, axis=1)
        grads[n] = g.reshape(w[n].shape)

    delta, new_m, new_v = {}, {}, {}
    for n in ["w_ada"] + [b for b, _, _ in BIG]:
        shp = w[n].shape
        d2, m2, v2 = _adamw(w[n][0], grads[n][0], m[n][0], v[n][0], "adamw_" + n)
        delta[n], new_m[n], new_v[n] = d2.reshape(shp), m2.reshape(shp), v2.reshape(shp)
    rest = ["b_ada"] + SMALL
    shapes = [w[n].shape for n in rest]
    pw, _ = _pack_rows([w[n] for n in rest])
    pg, _ = _pack_rows([grads[n] for n in rest])
    pm, _ = _pack_rows([m[n] for n in rest])
    pv, _ = _pack_rows([v[n] for n in rest])
    d2, m2, v2 = _adamw(pw, pg, pm, pv, "adamw_small")
    for n, dd, mm, vv in zip(rest, _unpack_rows(d2, shapes), _unpack_rows(m2, shapes), _unpack_rows(v2, shapes)):
        delta[n], new_m[n], new_v[n] = dd, mm, vv

    return (loss, dx, *[grads[n] for n in WEIGHTS], *[delta[n] for n in WEIGHTS], *[new_m[n] for n in WEIGHTS],
            *[new_v[n] for n in WEIGHTS])
```

```python
import functools
import math

import numpy as np
import jax
import jax.numpy as jnp
from jax import lax
from jax.experimental import pallas as pl
from jax.experimental.pallas import tpu as pltpu

F32, BF16 = jnp.float32, jnp.bfloat16

D_MODEL = 1024
N_HEADS = 8
HEAD_DIM = 64
ATT_WIDTH = 512
SSM_GROUPS = 16
SSM_GROUP_CH = 16
SSM_WIDTH = 256
SSM_STATE = 64
SSM_LANES = SSM_GROUPS * SSM_STATE
D_FF = 2048
IN_WIDTH = 3 * ATT_WIDTH + SSM_WIDTH + 2 * D_MODEL
ATT_BLOCK = 128
N_PATTERNS = 3
EPS = 1e-6
NEG_INF = -1e30

ADAM_LR, ADAM_B1, ADAM_B2, ADAM_EPS, ADAM_WD, ADAM_STEP = 0.001, 0.9, 0.999, 1e-08, 0.01, 10

V7X_VMEM_LIMIT_BYTES = 56 * 1024 * 1024
LANES = 1024

MESH_AXES = ("x", "y", "c")


def _pcall(body, *, name, out_shape, grid=(), in_specs=None, out_specs=None, scratch_shapes=(), dims=None):
    params = dict(vmem_limit_bytes=V7X_VMEM_LIMIT_BYTES)
    if dims is not None:
        params["dimension_semantics"] = dims
    specs = {}
    if in_specs is not None:
        specs = dict(grid=grid, in_specs=in_specs, out_specs=out_specs)
    return pl.pallas_call(body, name=name, out_shape=out_shape, scratch_shapes=scratch_shapes,
                          compiler_params=pltpu.CompilerParams(**params), **specs)


def _sds(shape, dtype):
    return jax.ShapeDtypeStruct(tuple(shape), dtype)


def _tile(n, target):
    if n <= target:
        return n
    for t in range(target - target % 128, 0, -128):
        if n % t == 0:
            return t
    raise ValueError((n, target))


def _sig(v):
    return 1.0 / (1.0 + jnp.exp(-v))


def _mm(a, b, *, name, ta=False, tb=False, out_dtype=F32, tm=1024, tn=1024, tk=1024):
    if ta:
        K, M = a.shape
    else:
        M, K = a.shape
    if tb:
        N, K2 = b.shape
    else:
        K2, N = b.shape
    assert K == K2, (a.shape, b.shape)
    tm, tn, tk = _tile(M, tm), _tile(N, tn), _tile(K, tk)
    nk = K // tk
    a_spec = pl.BlockSpec((tk, tm), lambda i, j, k: (k, i)) if ta else pl.BlockSpec((tm, tk), lambda i, j, k: (i, k))
    b_spec = pl.BlockSpec((tn, tk), lambda i, j, k: (j, k)) if tb else pl.BlockSpec((tk, tn), lambda i, j, k: (k, j))
    dn = (((0 if ta else 1,), (1 if tb else 0,)), ((), ()))

    def body(a_ref, b_ref, o_ref, acc_ref):
        k = pl.program_id(2)

        @pl.when(k == 0)
        def _():
            acc_ref[...] = jnp.zeros_like(acc_ref)

        acc_ref[...] += lax.dot_general(a_ref[...].astype(BF16), b_ref[...].astype(BF16), dn,
                                        preferred_element_type=F32)

        @pl.when(k == nk - 1)
        def _():
            o_ref[...] = acc_ref[...].astype(out_dtype)

    def body_single(a_ref, b_ref, o_ref):
        o_ref[...] = lax.dot_general(a_ref[...].astype(BF16), b_ref[...].astype(BF16), dn,
                                     preferred_element_type=F32).astype(out_dtype)

    return _pcall(body_single if nk == 1 else body, name=name, out_shape=_sds((M, N), out_dtype),
                  grid=(M // tm, N // tn, nk), in_specs=[a_spec, b_spec],
                  out_specs=pl.BlockSpec((tm, tn), lambda i, j, k: (i, j)),
                  scratch_shapes=[] if nk == 1 else [pltpu.VMEM((tm, tn), F32)],
                  dims=("parallel", "parallel", "arbitrary"))(a, b)


def _ada_fwd(c_all, w_ada, b_ada_cols):
    n = w_ada.shape[1]

    def body(c_ref, w_ref, b_ref, o_ref):
        c = c_ref[...]
        act = c * _sig(c)
        o_ref[...] = jnp.dot(act.astype(BF16), w_ref[...].astype(BF16), preferred_element_type=F32) + b_ref[...]

    return _pcall(body, name="ada_fwd", out_shape=_sds((c_all.shape[0], n), F32))(c_all, w_ada, b_ada_cols)


def _ada_bwd(c_all, dmod_all, dmod_cols):
    n = dmod_cols.shape[1]

    def body(c_ref, da_ref, dc_ref, gw_ref, gb_ref):
        c = c_ref[...]
        act = c * _sig(c)
        gw_ref[...] = lax.dot_general(act, dc_ref[...], (((0,), (0,)), ((), ())), preferred_element_type=F32,
                                      precision=lax.Precision.HIGHEST)
        gb_ref[...] = jnp.sum(da_ref[...], axis=0, keepdims=True)

    return _pcall(body, name="ada_bwd", out_shape=(_sds((D_MODEL, n), F32), _sds((1, dmod_all.shape[1]), F32)))(
        c_all, dmod_all, dmod_cols)


ROW_TILE = 512


def _row_specs(B, S):
    ts = min(S, ROW_TILE)
    row = pl.BlockSpec((1, ts, D_MODEL), lambda b, s: (b, s, 0))
    bvec = pl.BlockSpec((1, 1, D_MODEL), lambda b, s: (b, 0, 0))
    gvec = pl.BlockSpec((1, D_MODEL), lambda b, s: (0, 0))
    return ts, row, bvec, gvec


def _norm_mod(x3, g, sc, sh):
    B, S, _ = x3.shape
    ts, row, bvec, gvec = _row_specs(B, S)

    def body(x_ref, g_ref, sc_ref, sh_ref, u_ref):
        x = x_ref[0]
        r = lax.rsqrt(jnp.mean(x * x, axis=-1, keepdims=True) + EPS)
        u_ref[0] = ((x * r) * g_ref[...] * (1.0 + sc_ref[0]) + sh_ref[0]).astype(BF16)

    return _pcall(body, name="norm_mod1", out_shape=_sds(x3.shape, BF16), grid=(B, S // ts),
                  in_specs=[row, gvec, bvec, bvec], out_specs=row, dims=("parallel", "parallel"))(x3, g, sc, sh)


def _resid_norm_mod(x3, mix3, gt, g, sc, sh):
    B, S, _ = x3.shape
    ts, row, bvec, gvec = _row_specs(B, S)

    def body(x_ref, m_ref, gt_ref, g_ref, sc_ref, sh_ref, h_ref, u_ref):
        h = x_ref[0] + gt_ref[0] * m_ref[0]
        h_ref[0] = h
        r = lax.rsqrt(jnp.mean(h * h, axis=-1, keepdims=True) + EPS)
        u_ref[0] = ((h * r) * g_ref[...] * (1.0 + sc_ref[0]) + sh_ref[0]).astype(BF16)

    return _pcall(body, name="resid_norm_mod2", out_shape=(_sds(x3.shape, F32), _sds(x3.shape, BF16)),
                  grid=(B, S // ts), in_specs=[row, row, bvec, gvec, bvec, bvec], out_specs=(row, row),
                  dims=("parallel", "parallel"))(x3, mix3, gt, g, sc, sh)


def _norm_bwd(h3, du3, dres3, g, sc, name, mix3=None, gt=None):
    B, S, _ = h3.shape
    ts, row, bvec, gvec = _row_specs(B, S)
    with_gate = mix3 is not None

    def body(*refs):
        if with_gate:
            h_ref, du_ref, dr_ref, g_ref, sc_ref, m_ref, gt_ref, dh_ref, dsh_ref, dsc_ref, dg_ref, dgt_ref, dm_ref = refs
        else:
            h_ref, du_ref, dr_ref, g_ref, sc_ref, dh_ref, dsh_ref, dsc_ref, dg_ref = refs
        b, s = pl.program_id(0), pl.program_id(1)
        h = h_ref[0]
        r = lax.rsqrt(jnp.mean(h * h, axis=-1, keepdims=True) + EPS)
        xn = h * r
        du = du_ref[0]
        g = g_ref[...]
        sc1 = 1.0 + sc_ref[0]
        dxn = du * g * sc1
        dh = dr_ref[0] + r * (dxn - xn * jnp.mean(dxn * xn, axis=-1, keepdims=True))
        dh_ref[0] = dh

        @pl.when(s == 0)
        def _():
            dsh_ref[...] = jnp.zeros_like(dsh_ref)
            dsc_ref[...] = jnp.zeros_like(dsc_ref)
            if with_gate:
                dgt_ref[...] = jnp.zeros_like(dgt_ref)

        @pl.when((s == 0) & (b == 0))
        def _():
            dg_ref[...] = jnp.zeros_like(dg_ref)

        dux = du * xn
        dsh_ref[0] += jnp.sum(du, axis=0, keepdims=True)
        dsc_ref[0] += jnp.sum(dux * g, axis=0, keepdims=True)
        dg_ref[...] += jnp.sum(dux * sc1, axis=0, keepdims=True)
        if with_gate:
            dgt_ref[0] += jnp.sum(dh * m_ref[0], axis=0, keepdims=True)
            dm_ref[0] = (dh * gt_ref[0]).astype(BF16)

    bshape = _sds((B, 1, D_MODEL), F32)
    in_specs = [row, row, row, gvec, bvec]
    out_shape = [_sds(h3.shape, F32), bshape, bshape, _sds((1, D_MODEL), F32)]
    out_specs = [row, bvec, bvec, gvec]
    args = [h3, du3, dres3, g, sc]
    if with_gate:
        in_specs += [row, bvec]
        out_shape += [bshape, _sds(h3.shape, BF16)]
        out_specs += [bvec, row]
        args += [mix3, gt]
    return _pcall(body, name=name, out_shape=tuple(out_shape), grid=(B, S // ts), in_specs=in_specs,
                  out_specs=tuple(out_specs), dims=("arbitrary", "arbitrary"))(*args)


def _final_loss(h1, ffn3, tgt3, gt, gfin):
    B, S, _ = h1.shape
    ts, row, bvec, gvec = _row_specs(B, S)
    one = pl.BlockSpec((1, 1), lambda b, s: (0, 0))

    def body(h_ref, f_ref, t_ref, gt_ref, gf_ref, dh_ref, dff_ref, dgt_ref, dgf_ref, loss_ref):
        b, s = pl.program_id(0), pl.program_id(1)
        f = f_ref[0]
        gtv = gt_ref[0]
        gf = gf_ref[...]
        h2 = h_ref[0] + gtv * f
        r = lax.rsqrt(jnp.mean(h2 * h2, axis=-1, keepdims=True) + EPS)
        n = h2 * r
        e = n * gf - t_ref[0]
        dy = e * (1.0 / D_MODEL)
        dn = dy * gf
        dh2 = r * (dn - n * jnp.mean(dn * n, axis=-1, keepdims=True))
        dh_ref[0] = dh2
        dff_ref[0] = (dh2 * gtv).astype(BF16)

        @pl.when(s == 0)
        def _():
            dgt_ref[...] = jnp.zeros_like(dgt_ref)

        @pl.when((s == 0) & (b == 0))
        def _():
            dgf_ref[...] = jnp.zeros_like(dgf_ref)
            loss_ref[...] = jnp.zeros_like(loss_ref)

        dgt_ref[0] += jnp.sum(dh2 * f, axis=0, keepdims=True)
        dgf_ref[...] += jnp.sum(dy * n, axis=0, keepdims=True)
        rows = jnp.sum(e * e, axis=1, keepdims=True)
        loss_ref[...] += jnp.sum(rows, axis=0, keepdims=True) * (0.5 / D_MODEL)

    return _pcall(body, name="final_loss",
                  out_shape=(_sds(h1.shape, F32), _sds(h1.shape, BF16), _sds((B, 1, D_MODEL), F32),
                             _sds((1, D_MODEL), F32), _sds((1, 1), F32)),
                  grid=(B, S // ts), in_specs=[row, row, row, bvec, gvec], out_specs=(row, row, bvec, gvec, one),
                  dims=("arbitrary", "arbitrary"))(h1, ffn3, tgt3, gt, gfin)


def _att_scores(qh, kc, kp, h, dil, first, a_idx, j_idx):
    scale = HEAD_DIM ** -0.5
    nt = (((1,), (1,)), ((), ()))
    slope = (2.0 ** (-8.0 * (h + 1) / N_HEADS)) * dil
    dist_c = (a_idx - j_idx).astype(F32)
    s_c = lax.dot_general(qh, kc, nt, preferred_element_type=F32) * scale
    s_c = jnp.where(a_idx >= j_idx, s_c - slope * dist_c, NEG_INF)
    s_p = lax.dot_general(qh, kp, nt, preferred_element_type=F32) * scale
    s_p = jnp.where((j_idx >= a_idx) & jnp.logical_not(first), s_p - slope * (dist_c + float(ATT_BLOCK)), NEG_INF)
    return s_c, s_p


def _att_block_consts(seq_blocks):
    p = pl.program_id(0)
    j = pl.program_id(1)
    nb = lax.shift_right_logical(jnp.int32(seq_blocks), 2 * p)
    dil = lax.shift_left(jnp.int32(1), 2 * p).astype(F32)
    a_idx = lax.broadcasted_iota(jnp.int32, (ATT_BLOCK, ATT_BLOCK), 0)
    j_idx = lax.broadcasted_iota(jnp.int32, (ATT_BLOCK, ATT_BLOCK), 1)
    return j, nb, dil, a_idx, j_idx


def _attn_fwd(qb, kb, vb, seq_blocks):
    _, NB, _, _ = qb.shape
    cur = pl.BlockSpec((None, None, ATT_BLOCK, ATT_WIDTH), lambda p, j: (p, j, 0, 0))
    prev = pl.BlockSpec((None, None, ATT_BLOCK, ATT_WIDTH), lambda p, j: (p, jnp.maximum(j - 1, 0), 0, 0))
    lse_spec = pl.BlockSpec((None, None, ATT_BLOCK, N_HEADS), lambda p, j: (p, j, 0, 0))

    def body(q_ref, kc_ref, kp_ref, vc_ref, vp_ref, o_ref, lse_ref):
        j, nb, dil, a_idx, j_idx = _att_block_consts(seq_blocks)
        first = lax.rem(j, nb) == 0
        for h in range(N_HEADS):
            hs = slice(h * HEAD_DIM, (h + 1) * HEAD_DIM)
            s_c, s_p = _att_scores(q_ref[:, hs], kc_ref[:, hs], kp_ref[:, hs], h, dil, first, a_idx, j_idx)
            m = jnp.maximum(jnp.max(s_c, axis=1, keepdims=True), jnp.max(s_p, axis=1, keepdims=True))
            p_c = jnp.exp(s_c - m)
            p_p = jnp.exp(s_p - m)
            den = jnp.sum(p_c, axis=1, keepdims=True) + jnp.sum(p_p, axis=1, keepdims=True)
            o = (jnp.dot(p_c.astype(BF16), vc_ref[:, hs], preferred_element_type=F32)
                 + jnp.dot(p_p.astype(BF16), vp_ref[:, hs], preferred_element_type=F32))
            o_ref[:, hs] = o / den
            lse_ref[:, h:h + 1] = m + jnp.log(den)

    return _pcall(body, name="attn_fwd",
                  out_shape=(_sds(qb.shape, F32), _sds((N_PATTERNS, NB, ATT_BLOCK, N_HEADS), F32)),
                  grid=(N_PATTERNS, NB), in_specs=[cur, cur, prev, cur, prev], out_specs=(cur, lse_spec),
                  dims=("parallel", "parallel"))(qb, kb, kb, vb, vb)


def _attn_combine(o_p, lse_p):
    _, T, _ = o_p.shape
    tm = min(T, 1024)

    def body(o_ref, l_ref, out_ref, lse_ref):
        l0, l1, l2 = l_ref[0], l_ref[1], l_ref[2]
        m = jnp.maximum(jnp.maximum(l0, l1), l2)
        lse = m + jnp.log(jnp.exp(l0 - m) + jnp.exp(l1 - m) + jnp.exp(l2 - m))
        lse_ref[...] = lse
        w = [jnp.exp(l0 - lse), jnp.exp(l1 - lse), jnp.exp(l2 - lse)]
        for h in range(N_HEADS):
            hs = slice(h * HEAD_DIM, (h + 1) * HEAD_DIM)
            acc = w[0][:, h:h + 1] * o_ref[0, :, hs]
            acc = acc + w[1][:, h:h + 1] * o_ref[1, :, hs]
            acc = acc + w[2][:, h:h + 1] * o_ref[2, :, hs]
            out_ref[:, hs] = acc.astype(BF16)

    return _pcall(body, name="attn_combine", out_shape=(_sds((T, ATT_WIDTH), BF16), _sds((T, N_HEADS), F32)),
                  grid=(T // tm,),
                  in_specs=[pl.BlockSpec((N_PATTERNS, tm, ATT_WIDTH), lambda i: (0, i, 0)),
                            pl.BlockSpec((N_PATTERNS, tm, N_HEADS), lambda i: (0, i, 0))],
                  out_specs=(pl.BlockSpec((tm, ATT_WIDTH), lambda i: (i, 0)), pl.BlockSpec((tm, N_HEADS), lambda i: (i, 0))),
                  dims=("parallel",))(o_p, lse_p)


def _attn_bwd(qb, kb, vb, dob, ob, lseb, seq_blocks):
    _, NB, _, _ = qb.shape
    last = NB - 1
    cur = pl.BlockSpec((None, None, ATT_BLOCK, ATT_WIDTH), lambda p, j: (p, jnp.minimum(j, last), 0, 0))
    prev = pl.BlockSpec((None, None, ATT_BLOCK, ATT_WIDTH),
                        lambda p, j: (p, jnp.maximum(jnp.minimum(j, last) - 1, 0), 0, 0))
    lag = pl.BlockSpec((None, None, ATT_BLOCK, ATT_WIDTH), lambda p, j: (p, jnp.maximum(j - 1, 0), 0, 0))
    lse_spec = pl.BlockSpec((None, None, ATT_BLOCK, N_HEADS), lambda p, j: (p, jnp.minimum(j, last), 0, 0))
    scale = HEAD_DIM ** -0.5
    tn = (((0,), (0,)), ((), ()))
    nt = (((1,), (1,)), ((), ()))

    def body(q_ref, kc_ref, kp_ref, vc_ref, vp_ref, do_ref, o_ref, lse_ref, dq_ref, dk_ref, dv_ref, ck_ref, cv_ref):
        j, nb, dil, a_idx, j_idx = _att_block_consts(seq_blocks)

        @pl.when(j == 0)
        def _():
            ck_ref[...] = jnp.zeros_like(ck_ref)
            cv_ref[...] = jnp.zeros_like(cv_ref)

        @pl.when(j <= last)
        def _():
            first = lax.rem(j, nb) == 0
            for h in range(N_HEADS):
                hs = slice(h * HEAD_DIM, (h + 1) * HEAD_DIM)
                qh, kc, kp, vc, vp, doh = q_ref[:, hs], kc_ref[:, hs], kp_ref[:, hs], vc_ref[:, hs], vp_ref[:, hs], do_ref[:, hs]
                s_c, s_p = _att_scores(qh, kc, kp, h, dil, first, a_idx, j_idx)
                lse = lse_ref[:, h:h + 1]
                p_c = jnp.exp(s_c - lse)
                p_p = jnp.exp(s_p - lse)
                delta = jnp.sum(doh.astype(F32) * o_ref[:, hs].astype(F32), axis=1, keepdims=True)
                ds_c = (p_c * (lax.dot_general(doh, vc, nt, preferred_element_type=F32) - delta)).astype(BF16)
                ds_p = (p_p * (lax.dot_general(doh, vp, nt, preferred_element_type=F32) - delta)).astype(BF16)
                dq_ref[:, hs] = (jnp.dot(ds_c, kc, preferred_element_type=F32)
                                 + jnp.dot(ds_p, kp, preferred_element_type=F32)) * scale
                dk_ref[:, hs] = ck_ref[:, hs] + lax.dot_general(ds_p, qh, tn, preferred_element_type=F32) * scale
                dv_ref[:, hs] = cv_ref[:, hs] + lax.dot_general(p_p.astype(BF16), doh, tn, preferred_element_type=F32)
                ck_ref[:, hs] = lax.dot_general(ds_c, qh, tn, preferred_element_type=F32) * scale
                cv_ref[:, hs] = lax.dot_general(p_c.astype(BF16), doh, tn, preferred_element_type=F32)

        @pl.when(j == NB)
        def _():
            dk_ref[...] = ck_ref[...]
            dv_ref[...] = cv_ref[...]

    shp = _sds(qb.shape, F32)
    return _pcall(body, name="attn_bwd", out_shape=(shp, shp, shp), grid=(N_PATTERNS, NB + 1),
                  in_specs=[cur, cur, prev, cur, prev, cur, cur, lse_spec], out_specs=(cur, lag, lag),
                  scratch_shapes=[pltpu.VMEM((ATT_BLOCK, ATT_WIDTH), F32), pltpu.VMEM((ATT_BLOCK, ATT_WIDTH), F32)],
                  dims=("arbitrary", "arbitrary"))(qb, kb, kb, vb, vb, dob, ob, lseb)


def _sum3_cast(a, b, c):
    T, N = a.shape
    tm = min(T, 1024)
    spec = pl.BlockSpec((tm, N), lambda i: (i, 0))

    def body(a_ref, b_ref, c_ref, o_ref):
        o_ref[...] = (a_ref[...] + b_ref[...] + c_ref[...]).astype(BF16)

    return _pcall(body, name="sum3_cast", out_shape=_sds((T, N), BF16), grid=(T // tm,), in_specs=[spec] * 3,
                  out_specs=spec, dims=("parallel",))(a, b, c)


def _to_blocks(t, B, S):
    C = t.shape[-1]
    outs = []
    for p in range(N_PATTERNS):
        d = 4 ** p
        u = t.reshape(B, S // d, d, C).transpose(0, 2, 1, 3)
        outs.append(u.reshape(B * S // ATT_BLOCK, ATT_BLOCK, C))
    return jnp.stack(outs, axis=0)


def _from_blocks(tb, B, S):
    C = tb.shape[-1]
    outs = []
    for p in range(N_PATTERNS):
        d = 4 ** p
        u = tb[p].reshape(B, d, S // d, C).transpose(0, 2, 1, 3)
        outs.append(u.reshape(B * S, C))
    return jnp.stack(outs, axis=0)


def _expand_groups(m):
    rows = SSM_WIDTH
    t = jnp.concatenate([m] * SSM_GROUPS, axis=0)
    r = lax.broadcasted_iota(jnp.int32, (rows, SSM_LANES), 0)
    l = lax.broadcasted_iota(jnp.int32, (rows, SSM_LANES), 1)
    keep = lax.shift_right_logical(r, 4) == lax.shift_right_logical(l, 6)
    return jnp.where(keep, t, 0.0)


def _collapse_groups(m):
    rows = SSM_WIDTH
    r = lax.broadcasted_iota(jnp.int32, (rows, SSM_LANES), 0)
    l = lax.broadcasted_iota(jnp.int32, (rows, SSM_LANES), 1)
    keep = lax.shift_right_logical(r, 4) == lax.shift_right_logical(l, 6)
    t = jnp.where(keep, m, 0.0)
    acc = t[0:SSM_GROUP_CH]
    for g in range(1, SSM_GROUPS):
        acc = acc + t[g * SSM_GROUP_CH:(g + 1) * SSM_GROUP_CH]
    return acc


def _zoh(lr, li, ldt):
    dt = jnp.exp(ldt)
    mag = jnp.exp(lr * dt)
    ang = li * dt
    cs, sn = jnp.cos(ang), jnp.sin(ang)
    ab_re, ab_im = mag * cs, mag * sn
    nr, ni = ab_re - 1.0, ab_im
    den = lr * lr + li * li
    n_re = nr * lr + ni * li
    n_im = ni * lr - nr * li
    return dict(dt=dt, mag=mag, cs=cs, sn=sn, ab_re=ab_re, ab_im=ab_im, nr=nr, ni=ni, den=den, n_re=n_re, n_im=n_im,
                f_re=n_re / den, f_im=n_im / den)


def _ssm_params(lr, li, ldt, br, bi, cr, ci):
    def body(lr_ref, li_ref, ldt_ref, br_ref, bi_ref, cr_ref, ci_ref, ab_ref, w_ref, c_ref):
        z = _zoh(lr_ref[...], li_ref[...], ldt_ref[...])
        ab_ref[0:1, :] = z["ab_re"]
        ab_ref[1:2, :] = z["ab_im"]
        br, bi = br_ref[...], bi_ref[...]
        w_ref[:, 0:SSM_LANES] = _expand_groups(z["f_re"] * br - z["f_im"] * bi).astype(BF16)
        w_ref[:, SSM_LANES:] = _expand_groups(z["f_re"] * bi + z["f_im"] * br).astype(BF16)
        c_ref[:, 0:SSM_LANES] = _expand_groups(cr_ref[...]).astype(BF16)
        c_ref[:, SSM_LANES:] = _expand_groups(-ci_ref[...]).astype(BF16)

    return _pcall(body, name="ssm_params",
                  out_shape=(_sds((2, SSM_LANES), F32), _sds((SSM_WIDTH, 2 * SSM_LANES), BF16),
                             _sds((SSM_WIDTH, 2 * SSM_LANES), BF16)))(lr, li, ldt, br, bi, cr, ci)


def _ssm_params_bwd(lr, li, ldt, br, bi, dab, dw, dc):
    def body(lr_ref, li_ref, ldt_ref, br_ref, bi_ref, dab_ref, dw_ref, dc_ref,
             dlr_ref, dli_ref, dldt_ref, dbr_ref, dbi_ref, dcr_ref, dci_ref):
        lr, li = lr_ref[...], li_ref[...]
        z = _zoh(lr, li, ldt_ref[...])
        br, bi = br_ref[...], bi_ref[...]
        dbb_re = _collapse_groups(dw_ref[:, 0:SSM_LANES])
        dbb_im = _collapse_groups(dw_ref[:, SSM_LANES:])
        dcr_ref[...] = _collapse_groups(dc_ref[:, 0:SSM_LANES])
        dci_ref[...] = -_collapse_groups(dc_ref[:, SSM_LANES:])
        f_re, f_im = z["f_re"], z["f_im"]
        dbr_ref[...] = f_re * dbb_re + f_im * dbb_im
        dbi_ref[...] = f_re * dbb_im - f_im * dbb_re
        df_re = jnp.sum(dbb_re * br + dbb_im * bi, axis=0, keepdims=True)
        df_im = jnp.sum(dbb_im * br - dbb_re * bi, axis=0, keepdims=True)
        den = z["den"]
        dn_re, dn_im = df_re / den, df_im / den
        dden = -(df_re * z["n_re"] + df_im * z["n_im"]) / (den * den)
        dnr = dn_re * lr - dn_im * li
        dni = dn_re * li + dn_im * lr
        dlr = dn_re * z["nr"] + dn_im * z["ni"] + 2.0 * dden * lr
        dli = dn_re * z["ni"] - dn_im * z["nr"] + 2.0 * dden * li
        dab_re = dab_ref[0:1, :] + dnr
        dab_im = dab_ref[1:2, :] + dni
        mag, cs, sn, dt = z["mag"], z["cs"], z["sn"], z["dt"]
        dmag = dab_re * cs + dab_im * sn
        dang = mag * (dab_im * cs - dab_re * sn)
        dlr_ref[...] = dlr + dmag * mag * dt
        dli_ref[...] = dli + dang * dt
        ddt = dmag * mag * lr + dang * li
        per_lane = jnp.broadcast_to(ddt * dt, (8, SSM_LANES))
        lane = lax.broadcasted_iota(jnp.int32, (SSM_LANES, 128), 0)
        col = lax.broadcasted_iota(jnp.int32, (SSM_LANES, 128), 1)
        ind = jnp.where(lax.shift_right_logical(lane, 6) == col, 1.0, 0.0)
        dldt_ref[...] = jnp.dot(per_lane, ind, preferred_element_type=F32, precision=lax.Precision.HIGHEST)[0:1]

    vec = _sds((1, SSM_LANES), F32)
    mat = _sds((SSM_GROUP_CH, SSM_LANES), F32)
    return _pcall(body, name="ssm_params_bwd", out_shape=(vec, vec, _sds((1, 128), F32), mat, mat, mat, mat))(
        lr, li, ldt, br, bi, dab, dw, dc)


SCAN_CHUNK = 512


def _scan_consts(ar, ai, k_ref, reverse):
    row = lax.broadcasted_iota(jnp.int32, (8, SSM_LANES), 0)
    pw = [(ar, ai)]
    for _ in range(7):
        pr, pi = pw[-1]
        pw.append((pr * ar - pi * ai, pr * ai + pi * ar))
    for n, k in enumerate((1, 2, 4)):
        keep = (row < 8 - k) if reverse else (row >= k)
        k_ref[2 * n] = jnp.where(keep, jnp.broadcast_to(pw[k - 1][0], (8, SSM_LANES)), 0.0)
        k_ref[2 * n + 1] = jnp.where(keep, jnp.broadcast_to(pw[k - 1][1], (8, SSM_LANES)), 0.0)
    cr = jnp.zeros((8, SSM_LANES), F32)
    ci = jnp.zeros((8, SSM_LANES), F32)
    for r in range(8):
        e = (8 - r) if reverse else (r + 1)
        cr = jnp.where(row == r, jnp.broadcast_to(pw[e - 1][0], (8, SSM_LANES)), cr)
        ci = jnp.where(row == r, jnp.broadcast_to(pw[e - 1][1], (8, SSM_LANES)), ci)
    k_ref[6] = cr
    k_ref[7] = ci


def _scan_tile(xr, xi, k_ref, car, cai, reverse):
    for n, k in enumerate((1, 2, 4)):
        sh = (8 - k) if reverse else k
        sr = pltpu.roll(xr, sh, 0)
        si = pltpu.roll(xi, sh, 0)
        mr, mi = k_ref[2 * n], k_ref[2 * n + 1]
        xr, xi = xr + mr * sr - mi * si, xi + mr * si + mi * sr
    pr, pi = k_ref[6], k_ref[7]
    xr, xi = xr + pr * car - pi * cai, xi + pr * cai + pi * car
    return xr, xi


def _scan_fwd(bu3, abar):
    B, S, _ = bu3.shape
    ch = min(S, SCAN_CHUNK)
    blk = pl.BlockSpec((1, ch, 2 * SSM_LANES), lambda b, c: (b, c, 0))

    def body(ab_ref, bu_ref, x_ref, k_ref, carry_ref):
        _scan_consts(ab_ref[0:1, :], ab_ref[1:2, :], k_ref, False)

        @pl.when(pl.program_id(1) == 0)
        def _():
            carry_ref[...] = jnp.zeros_like(carry_ref)

        def step(i, carry):
            base = pl.multiple_of(i * 8, 8)
            xr = bu_ref[0, pl.ds(base, 8), 0:SSM_LANES]
            xi = bu_ref[0, pl.ds(base, 8), SSM_LANES:]
            xr, xi = _scan_tile(xr, xi, k_ref, carry[0], carry[1], False)
            x_ref[0, pl.ds(base, 8), 0:SSM_LANES] = xr
            x_ref[0, pl.ds(base, 8), SSM_LANES:] = xi
            return (jnp.broadcast_to(xr[7:8], (8, SSM_LANES)), jnp.broadcast_to(xi[7:8], (8, SSM_LANES)))

        cr, ci = lax.fori_loop(0, ch // 8, step, (carry_ref[0], carry_ref[1]))
        carry_ref[0] = cr
        carry_ref[1] = ci

    return _pcall(body, name="scan_fwd", out_shape=_sds(bu3.shape, F32), grid=(B, S // ch),
                  in_specs=[pl.BlockSpec((2, SSM_LANES), lambda b, c: (0, 0)), blk], out_specs=blk,
                  scratch_shapes=[pltpu.VMEM((8, 8, SSM_LANES), F32), pltpu.VMEM((2, 8, SSM_LANES), F32)],
                  dims=("arbitrary", "arbitrary"))(abar, bu3)


def _scan_bwd(dx3, xs3, abar):
    B, S, _ = dx3.shape
    ch = min(S, SCAN_CHUNK)
    nc = S // ch
    blk = pl.BlockSpec((1, ch, 2 * SSM_LANES), lambda b, c: (b, nc - 1 - c, 0))

    def body(ab_ref, dx_ref, xs_ref, g_ref, da_ref, k_ref, carry_ref, acc_ref):
        b, c = pl.program_id(0), pl.program_id(1)
        _scan_consts(ab_ref[0:1, :], -ab_ref[1:2, :], k_ref, True)
        row = lax.broadcasted_iota(jnp.int32, (8, SSM_LANES), 0)

        @pl.when(c == 0)
        def _():
            carry_ref[...] = jnp.zeros_like(carry_ref)

        @pl.when((c == 0) & (b == 0))
        def _():
            acc_ref[...] = jnp.zeros_like(acc_ref)

        def step(i, carry):
            car, cai, ar_acc, ai_acc = carry
            base = pl.multiple_of((ch // 8 - 1 - i) * 8, 8)
            gr = dx_ref[0, pl.ds(base, 8), 0:SSM_LANES]
            gi = dx_ref[0, pl.ds(base, 8), SSM_LANES:]
            gr, gi = _scan_tile(gr, gi, k_ref, car, cai, True)
            g_ref[0, pl.ds(base, 8), 0:SSM_LANES] = gr
            g_ref[0, pl.ds(base, 8), SSM_LANES:] = gi
            nr = jnp.where(row == 7, car, pltpu.roll(gr, 7, 0))
            ni = jnp.where(row == 7, cai, pltpu.roll(gi, 7, 0))
            xr = xs_ref[0, pl.ds(base, 8), 0:SSM_LANES]
            xi = xs_ref[0, pl.ds(base, 8), SSM_LANES:]
            ar_acc = ar_acc + nr * xr + ni * xi
            ai_acc = ai_acc + ni * xr - nr * xi
            return (jnp.broadcast_to(gr[0:1], (8, SSM_LANES)), jnp.broadcast_to(gi[0:1], (8, SSM_LANES)), ar_acc, ai_acc)

        cr, ci, ar_acc, ai_acc = lax.fori_loop(0, ch // 8, step, (carry_ref[0], carry_ref[1], acc_ref[0], acc_ref[1]))
        carry_ref[0] = cr
        carry_ref[1] = ci
        acc_ref[0] = ar_acc
        acc_ref[1] = ai_acc
        da_ref[0:1, :] = jnp.sum(ar_acc, axis=0, keepdims=True)
        da_ref[1:2, :] = jnp.sum(ai_acc, axis=0, keepdims=True)

    return _pcall(body, name="scan_bwd", out_shape=(_sds(dx3.shape, F32), _sds((2, SSM_LANES), F32)), grid=(B, nc),
                  in_specs=[pl.BlockSpec((2, SSM_LANES), lambda b, c: (0, 0)), blk, blk],
                  out_specs=(blk, pl.BlockSpec((2, SSM_LANES), lambda b, c: (0, 0))),
                  scratch_shapes=[pltpu.VMEM((8, 8, SSM_LANES), F32), pltpu.VMEM((2, 8, SSM_LANES), F32),
                                  pltpu.VMEM((2, 8, SSM_LANES), F32)],
                  dims=("arbitrary", "arbitrary"))(abar, dx3, xs3)


GELU_K = math.sqrt(2.0 / math.pi)
GELU_C = 0.044715


def _gelu_parts(y):
    t = jnp.tanh(GELU_K * (y + GELU_C * y * y * y))
    return 0.5 * y * (1.0 + t), t


def _ssm_post(yc, us, dsk, wglu, bglu):
    T, N = yc.shape
    tm = min(T, 1024)
    row = pl.BlockSpec((tm, N), lambda i: (i, 0))
    vec = pl.BlockSpec((1, N), lambda i: (0, 0))
    mat = pl.BlockSpec((N, N), lambda i: (0, 0))

    def body(yc_ref, us_ref, d_ref, w_ref, b_ref, y_ref, s_ref):
        y = yc_ref[...] + d_ref[...] * us_ref[...]
        y_ref[...] = y
        z, _ = _gelu_parts(y)
        gl = jnp.dot(z.astype(BF16), w_ref[...], preferred_element_type=F32) + b_ref[...]
        s_ref[...] = (z * _sig(gl)).astype(BF16)

    return _pcall(body, name="ssm_post", out_shape=(_sds((T, N), F32), _sds((T, N), BF16)), grid=(T // tm,),
                  in_specs=[row, row, vec, mat, vec], out_specs=(row, row), dims=("parallel",))(yc, us, dsk, wglu, bglu)


def _ssm_post_bwd(y5, us, ds, dsk, wglu, bglu):
    T, N = y5.shape
    tm = min(T, 1024)
    row = pl.BlockSpec((tm, N), lambda i: (i, 0))
    vec = pl.BlockSpec((1, N), lambda i: (0, 0))
    mat = pl.BlockSpec((N, N), lambda i: (0, 0))

    def body(y_ref, us_ref, ds_ref, d_ref, w_ref, b_ref, dy_ref, dd_ref, db_ref, dw_ref):
        @pl.when(pl.program_id(0) == 0)
        def _():
            dd_ref[...] = jnp.zeros_like(dd_ref)
            db_ref[...] = jnp.zeros_like(db_ref)
            dw_ref[...] = jnp.zeros_like(dw_ref)

        y = y_ref[...]
        z, t = _gelu_parts(y)
        zb = z.astype(BF16)
        gl = jnp.dot(zb, w_ref[...], preferred_element_type=F32) + b_ref[...]
        sg = _sig(gl)
        ds = ds_ref[...]
        dgl = ds * z * sg * (1.0 - sg)
        dglb = dgl.astype(BF16)
        dz = ds * sg + lax.dot_general(dglb, w_ref[...], (((1,), (1,)), ((), ())), preferred_element_type=F32)
        dgelu = 0.5 * (1.0 + t) + 0.5 * y * (1.0 - t * t) * GELU_K * (1.0 + 3.0 * GELU_C * y * y)
        dy = dz * dgelu
        dy_ref[...] = dy
        dd_ref[...] += jnp.sum(dy * us_ref[...], axis=0, keepdims=True)
        db_ref[...] += jnp.sum(dgl, axis=0, keepdims=True)
        dw_ref[...] += lax.dot_general(zb, dglb, (((0,), (0,)), ((), ())), preferred_element_type=F32)

    return _pcall(body, name="ssm_post_bwd",
                  out_shape=(_sds((T, N), F32), _sds((1, N), F32), _sds((1, N), F32), _sds((N, N), F32)),
                  grid=(T // tm,), in_specs=[row, row, row, vec, mat, vec], out_specs=(row, vec, vec, mat),
                  dims=("arbitrary",))(y5, us, ds, dsk, wglu, bglu)


def _add_scaled_cast(a, b, s):
    T, N = a.shape
    tm = min(T, 1024)
    row = pl.BlockSpec((tm, N), lambda i: (i, 0))

    def body(a_ref, b_ref, s_ref, o_ref):
        o_ref[...] = (a_ref[...] + s_ref[...] * b_ref[...]).astype(BF16)

    return _pcall(body, name="add_scaled_cast", out_shape=_sds((T, N), BF16), grid=(T // tm,),
                  in_specs=[row, row, pl.BlockSpec((1, N), lambda i: (0, 0))], out_specs=row, dims=("parallel",))(a, b, s)


GATE_TILE = 256
GATE_ATT_BLOCK0 = (3 * ATT_WIDTH + SSM_WIDTH) // GATE_TILE
GATE_SSM_BLOCK0 = (3 * ATT_WIDTH + SSM_WIDTH + D_MODEL) // GATE_TILE


def _merge(proj, y_att, y_ssm, b_gate):
    T = proj.shape[0]
    tm = min(T, 1024)
    nj = D_MODEL // GATE_TILE
    ga = pl.BlockSpec((tm, GATE_TILE), lambda i, j: (i, GATE_ATT_BLOCK0 + j))
    gs = pl.BlockSpec((tm, GATE_TILE), lambda i, j: (i, GATE_SSM_BLOCK0 + j))
    yy = pl.BlockSpec((tm, GATE_TILE), lambda i, j: (i, j))
    ba = pl.BlockSpec((1, GATE_TILE), lambda i, j: (0, j))
    bs = pl.BlockSpec((1, GATE_TILE), lambda i, j: (0, nj + j))

    def body(ga_ref, gs_ref, ya_ref, ys_ref, ba_ref, bs_ref, o_ref):
        o_ref[...] = (_sig(ga_ref[...] + ba_ref[...]) * ya_ref[...]
                      + _sig(gs_ref[...] + bs_ref[...]) * ys_ref[...]).astype(BF16)

    return _pcall(body, name="merge", out_shape=_sds((T, D_MODEL), BF16), grid=(T // tm, nj),
                  in_specs=[ga, gs, yy, yy, ba, bs], out_specs=yy, dims=("parallel", "parallel"))(
        proj, proj, y_att, y_ssm, b_gate, b_gate)


def _merge_bwd(proj, y_att, y_ssm, b_gate, dmerged):
    T = proj.shape[0]
    tm = min(T, 1024)
    nj = D_MODEL // GATE_TILE
    ga = pl.BlockSpec((tm, GATE_TILE), lambda j, i: (i, GATE_ATT_BLOCK0 + j))
    gs = pl.BlockSpec((tm, GATE_TILE), lambda j, i: (i, GATE_SSM_BLOCK0 + j))
    yy = pl.BlockSpec((tm, GATE_TILE), lambda j, i: (i, j))
    ba = pl.BlockSpec((1, GATE_TILE), lambda j, i: (0, j))
    bs = pl.BlockSpec((1, GATE_TILE), lambda j, i: (0, nj + j))

    def body(ga_ref, gs_ref, ya_ref, ys_ref, ba_ref, bs_ref, dm_ref, dya_ref, dys_ref, dga_ref, dgs_ref, dba_ref, dbs_ref):
        @pl.when(pl.program_id(1) == 0)
        def _():
            dba_ref[...] = jnp.zeros_like(dba_ref)
            dbs_ref[...] = jnp.zeros_like(dbs_ref)

        dm = dm_ref[...]
        sa = _sig(ga_ref[...] + ba_ref[...])
        ss = _sig(gs_ref[...] + bs_ref[...])
        dya_ref[...] = (dm * sa).astype(BF16)
        dys_ref[...] = (dm * ss).astype(BF16)
        dga = dm * ya_ref[...] * sa * (1.0 - sa)
        dgs = dm * ys_ref[...] * ss * (1.0 - ss)
        dga_ref[...] = dga.astype(BF16)
        dgs_ref[...] = dgs.astype(BF16)
        dba_ref[...] += jnp.sum(dga, axis=0, keepdims=True)
        dbs_ref[...] += jnp.sum(dgs, axis=0, keepdims=True)

    big = _sds((T, D_MODEL), BF16)
    vec = _sds((1, D_MODEL), F32)
    return _pcall(body, name="merge_bwd", out_shape=(big, big, big, big, vec, vec), grid=(nj, T // tm),
                  in_specs=[ga, gs, yy, yy, ba, bs, yy], out_specs=(yy, yy, yy, yy, ba, ba),
                  dims=("arbitrary", "arbitrary"))(proj, proj, y_att, y_ssm, b_gate, b_gate, dmerged)


CONV_TILE = 256


def _conv_pre(a, w_ref, b_ref, row):
    conv = b_ref[...] + w_ref[0:1, :] * a
    shifted = []
    for j in (1, 2):
        sh = jnp.where(row >= j, pltpu.roll(a, j, 0), 0.0)
        shifted.append(sh)
        conv = conv + w_ref[j:j + 1, :] * sh
    return conv, shifted


def _conv_act(up3, w_conv, b_conv):
    B, S, _ = up3.shape
    nj = D_FF // CONV_TILE
    a_spec = pl.BlockSpec((1, S, CONV_TILE), lambda b, j: (b, 0, j))
    v_spec = pl.BlockSpec((1, S, CONV_TILE), lambda b, j: (b, 0, nj + j))
    w_spec = pl.BlockSpec((3, CONV_TILE), lambda b, j: (0, j))
    b_spec = pl.BlockSpec((1, CONV_TILE), lambda b, j: (0, j))

    def body(a_ref, v_ref, w_ref, b_ref, o_ref):
        a = a_ref[0].astype(F32)
        row = lax.broadcasted_iota(jnp.int32, a.shape, 0)
        conv, _ = _conv_pre(a, w_ref, b_ref, row)
        o_ref[0] = (conv * _sig(conv) * v_ref[0]).astype(BF16)

    return _pcall(body, name="conv_act", out_shape=_sds((B, S, D_FF), BF16), grid=(B, nj),
                  in_specs=[a_spec, v_spec, w_spec, b_spec], out_specs=a_spec, dims=("parallel", "parallel"))(
        up3, up3, w_conv, b_conv)


def _conv_bwd(up3, dact3, w_conv, b_conv):
    B, S, _ = up3.shape
    nj = D_FF // CONV_TILE
    a_spec = pl.BlockSpec((1, S, CONV_TILE), lambda j, b: (b, 0, j))
    v_spec = pl.BlockSpec((1, S, CONV_TILE), lambda j, b: (b, 0, nj + j))
    w_spec = pl.BlockSpec((3, CONV_TILE), lambda j, b: (0, j))
    b_spec = pl.BlockSpec((1, CONV_TILE), lambda j, b: (0, j))

    def body(a_ref, v_ref, d_ref, w_ref, b_ref, da_ref, dv_ref, dw_ref, db_ref):
        @pl.when(pl.program_id(1) == 0)
        def _():
            dw_ref[...] = jnp.zeros_like(dw_ref)
            db_ref[...] = jnp.zeros_like(db_ref)

        a = a_ref[0].astype(F32)
        d = d_ref[0]
        row = lax.broadcasted_iota(jnp.int32, a.shape, 0)
        conv, shifted = _conv_pre(a, w_ref, b_ref, row)
        sg = _sig(conv)
        dv_ref[0] = (d * conv * sg).astype(BF16)
        dconv = d * v_ref[0] * (sg * (1.0 + conv * (1.0 - sg)))
        da = w_ref[0:1, :] * dconv
        for j in (1, 2):
            da = da + w_ref[j:j + 1, :] * jnp.where(row < S - j, pltpu.roll(dconv, S - j, 0), 0.0)
        da_ref[0] = da.astype(BF16)
        db_ref[...] += jnp.sum(dconv, axis=0, keepdims=True)
        dw_ref[0:1, :] += jnp.sum(dconv * a, axis=0, keepdims=True)
        dw_ref[1:2, :] += jnp.sum(dconv * shifted[0], axis=0, keepdims=True)
        dw_ref[2:3, :] += jnp.sum(dconv * shifted[1], axis=0, keepdims=True)

    big = _sds((B, S, D_FF), BF16)
    return _pcall(body, name="conv_bwd", out_shape=(big, big, _sds((3, D_FF), F32), _sds((1, D_FF), F32)),
                  grid=(nj, B), in_specs=[a_spec, v_spec, a_spec, w_spec, b_spec],
                  out_specs=(a_spec, a_spec, w_spec, b_spec), dims=("arbitrary", "arbitrary"))(
        up3, up3, dact3, w_conv, b_conv)


def _rows_tile(r):
    for t in (512, 256, 128, 64, 40, 32, 16, 8):
        if r % t == 0:
            return t
    return r


def _add2(a, b, out_dtype):
    R, N = a.shape
    tr = _rows_tile(R)
    spec = pl.BlockSpec((tr, N), lambda i: (i, 0))

    def body(a_ref, b_ref, o_ref):
        o_ref[...] = (a_ref[...] + b_ref[...]).astype(out_dtype)

    return _pcall(body, name="add2", out_shape=_sds((R, N), out_dtype), grid=(R // tr,), in_specs=[spec, spec],
                  out_specs=spec, dims=("parallel",))(a, b)


def _sum_slots(q, name):
    n, R, N = q.shape
    tr = _rows_tile(R)

    def body(q_ref, o_ref):
        acc = q_ref[0].astype(F32)
        for s in range(1, n):
            acc = acc + q_ref[s].astype(F32)
        o_ref[...] = acc

    return _pcall(body, name=name, out_shape=_sds((R, N), F32), grid=(R // tr,),
                  in_specs=[pl.BlockSpec((n, tr, N), lambda i: (0, i, 0))], out_specs=pl.BlockSpec((tr, N), lambda i: (i, 0)),
                  dims=("parallel",))(q)


def _adamw(w, g, m, v, name):
    R, N = w.shape
    tr = _rows_tile(R) if R * N * 4 > (1 << 20) else R
    tr = min(tr, 256) if R % 256 == 0 and R > 256 else tr
    spec = pl.BlockSpec((tr, N), lambda i: (i, 0))
    bc1 = 1.0 - ADAM_B1 ** ADAM_STEP
    bc2 = 1.0 - ADAM_B2 ** ADAM_STEP

    def body(w_ref, g_ref, m_ref, v_ref, d_ref, nm_ref, nv_ref):
        g = g_ref[...]
        m = ADAM_B1 * m_ref[...] + (1.0 - ADAM_B1) * g
        v = ADAM_B2 * v_ref[...] + (1.0 - ADAM_B2) * (g * g)
        nm_ref[...] = m
        nv_ref[...] = v
        d_ref[...] = -ADAM_LR * ((m / bc1) / (jnp.sqrt(v / bc2) + ADAM_EPS) + ADAM_WD * w_ref[...])

    shp = _sds((R, N), F32)
    return _pcall(body, name=name, out_shape=(shp, shp, shp), grid=(R // tr,), in_specs=[spec] * 4,
                  out_specs=(spec, spec, spec), dims=("parallel",))(w, g, m, v)


_GROUP_MASKS = {
    "all": [(dx, dy, dc) for dx in (0, 1) for dy in (0, 1) for dc in (0, 1) if (dx, dy, dc) != (0, 0, 0)],
    "xy": [(1, 0, 0), (0, 1, 0), (1, 1, 0)],
    "c": [(0, 0, 1)],
}
_GROUP_SLOTS = {"all": 8, "xy": 4, "c": 2}


def _group_slot(group, x, y, c):
    return {"all": 4 * x + 2 * y + c, "xy": 2 * x + y, "c": c}[group]


def _flip(v, d):
    return 1 - v if d else v


def _exchange(arr, group, mode, name):
    masks = _GROUP_MASKS[group]
    n = len(masks)
    if mode == "gather":
        out_shape = (_GROUP_SLOTS[group],) + arr.shape
    elif mode == "scatter":
        assert arr.shape[0] == _GROUP_SLOTS[group]
        out_shape = arr.shape
    elif mode == "swap":
        assert group == "c"
        out_shape = arr.shape
    else:
        assert group == "c"
        half = arr.shape[1] // 2
        out_shape = (arr.shape[0], half, arr.shape[2])

    def body(x_ref, o_ref, send_sems, recv_sems, local_sem):
        x, y, c = lax.axis_index("x"), lax.axis_index("y"), lax.axis_index("c")
        me = _group_slot(group, x, y, c)
        local = None
        if mode == "gather":
            local = pltpu.make_async_copy(x_ref, o_ref.at[me], local_sem)
        elif mode == "scatter":
            local = pltpu.make_async_copy(x_ref.at[me], o_ref.at[me], local_sem)
        if local is not None:
            local.start()
        copies = []
        for k, (dx, dy, dc) in enumerate(masks):
            px, py, pc = _flip(x, dx), _flip(y, dy), _flip(c, dc)
            if mode == "gather":
                src, dst = x_ref, o_ref.at[me]
            elif mode == "scatter":
                src, dst = x_ref.at[_group_slot(group, px, py, pc)], o_ref.at[me]
            elif mode == "swap":
                src, dst = x_ref, o_ref
            else:
                src, dst = x_ref.at[:, pl.ds(pl.multiple_of(pc * half, 8), half), :], o_ref
            cp = pltpu.make_async_remote_copy(src_ref=src, dst_ref=dst, send_sem=send_sems.at[k], recv_sem=recv_sems.at[k],
                                              device_id=(px, py, pc), device_id_type=pl.DeviceIdType.MESH)
            cp.start()
            copies.append(cp)
        for cp in copies:
            cp.wait()
        if local is not None:
            local.wait()

    anyspec = pl.BlockSpec(memory_space=pl.ANY)
    return pl.pallas_call(body, name=name, out_shape=_sds(out_shape, arr.dtype), in_specs=[anyspec], out_specs=anyspec,
                          scratch_shapes=[pltpu.SemaphoreType.DMA((n,)), pltpu.SemaphoreType.DMA((n,)),
                                          pltpu.SemaphoreType.DMA(())])(arr)


BIG = (("w_in", (D_MODEL, IN_WIDTH), 1), ("w_out", (D_MODEL, D_MODEL), 0), ("w_up", (D_MODEL, 2 * D_FF), 1),
       ("w_down", (D_FF, D_MODEL), 0), ("w_proj_att", (ATT_WIDTH, D_MODEL), 1), ("w_proj_ssm", (SSM_WIDTH, D_MODEL), 1),
       ("w_glu", (SSM_WIDTH, SSM_WIDTH), 0))
N_XY = 4


def _big_rows(shape):
    return shape[0] * shape[1] // N_XY // LANES


FLAT_ROWS = sum(_big_rows(s) for _, s, _ in BIG)


def _shard_shape(shape, axis):
    return (shape[0] // N_XY, shape[1]) if axis == 0 else (shape[0], shape[1] // N_XY)


def _flatten_shards(shards):
    return jnp.concatenate([shards[n].reshape(_big_rows(s), LANES) for n, s, _ in BIG], axis=0)


def _unflatten_shard(flat):
    out, r = {}, 0
    for n, s, ax in BIG:
        k = _big_rows(s)
        out[n] = flat[r:r + k].reshape(_shard_shape(s, ax))
        r += k
    return out


def _unflatten_full(flat4):
    out, r = {}, 0
    for n, s, ax in BIG:
        k = _big_rows(s)
        sh = _shard_shape(s, ax)
        t = flat4[:, r:r + k].reshape((N_XY,) + sh)
        out[n] = t.reshape(s) if ax == 0 else t.transpose(1, 0, 2).reshape(s)
        r += k
    return out


def _flatten_full(full):
    parts = []
    for n, s, ax in BIG:
        sh = _shard_shape(s, ax)
        t = full[n]
        t = t.reshape((N_XY,) + sh) if ax == 0 else t.reshape(s[0], N_XY, sh[1]).transpose(1, 0, 2)
        parts.append(t.reshape(N_XY, _big_rows(s), LANES))
    return jnp.concatenate(parts, axis=1)


def _pack_rows(arrs):
    rows, counts = [], []
    for a in arrs:
        f = a.reshape(-1)
        k = -(-f.shape[0] // LANES)
        rows.append(jnp.pad(f, (0, k * LANES - f.shape[0])).reshape(k, LANES))
        counts.append(k)
    return jnp.concatenate(rows, axis=0), counts


def _unpack_rows(buf, shapes):
    out, r = [], 0
    for s in shapes:
        size = int(np.prod(s))
        k = -(-size // LANES)
        out.append(buf[r:r + k].reshape(-1)[:size].reshape(s))
        r += k
    return out


def _lanes_from_groups(a):
    return a.transpose(2, 0, 1).reshape(SSM_GROUP_CH, SSM_LANES)


def _groups_from_lanes(a):
    return a.reshape(SSM_GROUP_CH, SSM_GROUPS, SSM_STATE).transpose(1, 2, 0)


def _local_step(x3, mod, tgt3, W, P):
    B, S, _ = x3.shape
    T = B * S
    seq_blocks = S // ATT_BLOCK
    sh1, sc1, gt1, sh2, sc2, gt2 = [m.reshape(B, 1, D_MODEL) for m in jnp.split(mod, 6, axis=-1)]
    g_mix, g_ffn, g_final = P["g_mix"].reshape(1, D_MODEL), P["g_ffn"].reshape(1, D_MODEL), P["g_final"].reshape(1, D_MODEL)
    b_gate = P["b_gate"].reshape(1, 2 * D_MODEL)
    d_skip, b_glu = P["d_skip"].reshape(1, SSM_WIDTH), P["b_glu"].reshape(1, SSM_WIDTH)
    w_conv, b_conv = P["w_conv"], P["b_conv"].reshape(1, D_FF)

    u1 = _norm_mod(x3, g_mix, sc1, sh1).reshape(T, D_MODEL)
    proj = _mm(u1, W["w_in"], name="mm_proj", out_dtype=BF16)
    q, k, v = (proj[:, i * ATT_WIDTH:(i + 1) * ATT_WIDTH] for i in range(3))
    us = proj[:, 3 * ATT_WIDTH:3 * ATT_WIDTH + SSM_WIDTH]
    qb, kb, vb = _to_blocks(q, B, S), _to_blocks(k, B, S), _to_blocks(v, B, S)
    o_pb, lse_pb = _attn_fwd(qb, kb, vb, seq_blocks)
    o_att, lse = _attn_combine(_from_blocks(o_pb, B, S), _from_blocks(lse_pb, B, S))
    y_att = _mm(o_att, W["w_proj_att"], name="mm_proj_att")

    lr = P["a_re"].reshape(1, SSM_LANES)
    li = P["a_im"].reshape(1, SSM_LANES)
    ldt = jnp.repeat(P["log_dt"], SSM_STATE).reshape(1, SSM_LANES)
    br, bi = _lanes_from_groups(P["b_re"]), _lanes_from_groups(P["b_im"])
    cr = P["c_re"].transpose(1, 0, 2).reshape(SSM_GROUP_CH, SSM_LANES)
    ci = P["c_im"].transpose(1, 0, 2).reshape(SSM_GROUP_CH, SSM_LANES)
    abar, w_bu, w_c = _ssm_params(lr, li, ldt, br, bi, cr, ci)
    bu = _mm(us, w_bu, name="mm_bu")
    xs = _scan_fwd(bu.reshape(B, S, 2 * SSM_LANES), abar).reshape(T, 2 * SSM_LANES)
    y_core = _mm(xs, w_c, tb=True, name="mm_ssm_out")
    y5, s_out = _ssm_post(y_core, us, d_skip, W["w_glu"], b_glu)
    y_ssm = _mm(s_out, W["w_proj_ssm"], name="mm_proj_ssm")

    merged = _merge(proj, y_att, y_ssm, b_gate)
    mix = _mm(merged, W["w_out"], name="mm_out")
    mix3 = mix.reshape(B, S, D_MODEL)

    h1, u2 = _resid_norm_mod(x3, mix3, gt1, g_ffn, sc2, sh2)
    u2 = u2.reshape(T, D_MODEL)
    up3 = _mm(u2, W["w_up"], name="mm_up", out_dtype=BF16).reshape(B, S, 2 * D_FF)
    act = _conv_act(up3, w_conv, b_conv).reshape(T, D_FF)
    ffn3 = _mm(act, W["w_down"], name="mm_down").reshape(B, S, D_MODEL)
    dh2, dffn, dgt2, dg_final, loss = _final_loss(h1, ffn3, tgt3, gt2, g_final)

    dffn = dffn.reshape(T, D_MODEL)
    gw = {}
    gw["w_down"] = _mm(act, dffn, ta=True, name="mm_dw_down")
    dact3 = _mm(dffn, W["w_down"], tb=True, name="mm_dact").reshape(B, S, D_FF)
    da3, dval3, dw_conv, db_conv = _conv_bwd(up3, dact3, w_conv, b_conv)
    dup = jnp.concatenate([da3.reshape(T, D_FF), dval3.reshape(T, D_FF)], axis=1)
    gw["w_up"] = _mm(u2, dup, ta=True, name="mm_dw_up")
    du2 = _mm(dup, W["w_up"], tb=True, name="mm_du2").reshape(B, S, D_MODEL)
    dh1, dsh2, dsc2, dg_ffn, dgt1, dmix = _norm_bwd(h1, du2, dh2, g_ffn, sc2, "norm_bwd2", mix3=mix3, gt=gt1)

    dmix = dmix.reshape(T, D_MODEL)
    gw["w_out"] = _mm(merged, dmix, ta=True, name="mm_dw_out")
    dmerged = _mm(dmix, W["w_out"], tb=True, name="mm_dmerged")
    dy_att, dy_ssm, dga, dgs, db_att, db_ssm = _merge_bwd(proj, y_att, y_ssm, b_gate, dmerged)

    gw["w_proj_ssm"] = _mm(s_out, dy_ssm, ta=True, name="mm_dw_proj_ssm")
    ds_out = _mm(dy_ssm, W["w_proj_ssm"], tb=True, name="mm_ds_out")
    dy5, dd_skip, db_glu, dw_glu = _ssm_post_bwd(y5, us, ds_out, d_skip, W["w_glu"], b_glu)
    gw["w_glu"] = dw_glu
    dxs = _mm(dy5, w_c, name="mm_dxs")
    dwc = _mm(dy5, xs, ta=True, name="mm_dwc")
    g3, dab = _scan_bwd(dxs.reshape(B, S, 2 * SSM_LANES), xs.reshape(B, S, 2 * SSM_LANES), abar)
    gs2 = g3.reshape(T, 2 * SSM_LANES)
    dwbu = _mm(us, gs2, ta=True, name="mm_dwbu")
    dus_core = _mm(gs2, w_bu, tb=True, name="mm_dus")
    dus = _add_scaled_cast(dus_core, dy5, d_skip)
    dlr, dli, dldt, dbr, dbi, dcr, dci = _ssm_params_bwd(lr, li, ldt, br, bi, dab, dwbu, dwc)

    gw["w_proj_att"] = _mm(o_att, dy_att, ta=True, name="mm_dw_proj_att")
    do_att = _mm(dy_att, W["w_proj_att"], tb=True, out_dtype=BF16, name="mm_do_att")
    dqb, dkb, dvb = _attn_bwd(qb, kb, vb, _to_blocks(do_att, B, S), _to_blocks(o_att, B, S), _to_blocks(lse, B, S),
                              seq_blocks)
    dq = _sum3_cast(*_from_blocks(dqb, B, S))
    dk = _sum3_cast(*_from_blocks(dkb, B, S))
    dv = _sum3_cast(*_from_blocks(dvb, B, S))
    dproj = jnp.concatenate([dq, dk, dv, dus, dga, dgs], axis=1)
    gw["w_in"] = _mm(u1, dproj, ta=True, name="mm_dw_in")
    du1 = _mm(dproj, W["w_in"], tb=True, name="mm_du1").reshape(B, S, D_MODEL)
    dx, dsh1, dsc1, dg_mix = _norm_bwd(x3, du1, dh1, g_mix, sc1, "norm_bwd1")

    dmod = jnp.concatenate([t.reshape(B, D_MODEL) for t in (dsh1, dsc1, dgt1, dsh2, dsc2, dgt2)], axis=1)
    gs = dict(
        g_mix=dg_mix.reshape(D_MODEL), b_gate=jnp.concatenate([db_att, db_ssm], axis=1).reshape(2 * D_MODEL),
        a_re=dlr.reshape(SSM_GROUPS, SSM_STATE), a_im=dli.reshape(SSM_GROUPS, SSM_STATE), log_dt=dldt[0, :SSM_GROUPS],
        b_re=_groups_from_lanes(dbr), b_im=_groups_from_lanes(dbi),
        c_re=dcr.reshape(SSM_GROUP_CH, SSM_GROUPS, SSM_STATE).transpose(1, 0, 2),
        c_im=dci.reshape(SSM_GROUP_CH, SSM_GROUPS, SSM_STATE).transpose(1, 0, 2),
        d_skip=dd_skip.reshape(SSM_WIDTH), b_glu=db_glu.reshape(SSM_WIDTH), g_ffn=dg_ffn.reshape(D_MODEL),
        w_conv=dw_conv, b_conv=db_conv.reshape(D_FF), g_final=dg_final.reshape(D_MODEL))
    return loss, dx, dmod, gw, gs


WEIGHTS = ['w_ada', 'b_ada', 'g_mix', 'w_in', 'b_gate', 'a_re', 'a_im', 'log_dt', 'b_re', 'b_im', 'c_re', 'c_im', 'd_skip',
           'w_glu', 'b_glu', 'w_proj_att', 'w_proj_ssm', 'w_out', 'g_ffn', 'w_up', 'w_conv', 'b_conv', 'w_down', 'g_final']
SMALL = ['g_mix', 'b_gate', 'a_re', 'a_im', 'log_dt', 'b_re', 'b_im', 'c_re', 'c_im', 'd_skip', 'b_glu', 'g_ffn', 'w_conv',
         'b_conv', 'g_final']


def kernel(x, c, w_ada, b_ada, g_mix, w_in, b_gate, a_re, a_im, log_dt, b_re, b_im, c_re, c_im, d_skip, w_glu, b_glu, w_proj_att, w_proj_ssm, w_out, g_ffn, w_up, w_conv, b_conv, w_down, g_final, loss_target, m_w_ada, m_b_ada, m_g_mix, m_w_in, m_b_gate, m_a_re, m_a_im, m_log_dt, m_b_re, m_b_im, m_c_re, m_c_im, m_d_skip, m_w_glu, m_b_glu, m_w_proj_att, m_w_proj_ssm, m_w_out, m_g_ffn, m_w_up, m_w_conv, m_b_conv, m_w_down, m_g_final, v_w_ada, v_b_ada, v_g_mix, v_w_in, v_b_gate, v_a_re, v_a_im, v_log_dt, v_b_re, v_b_im, v_c_re, v_c_im, v_d_skip, v_w_glu, v_b_glu, v_w_proj_att, v_w_proj_ssm, v_w_out, v_g_ffn, v_w_up, v_w_conv, v_b_conv, v_w_down, v_g_final):
    args = dict(locals())
    w = {n: args[n] for n in WEIGHTS}
    m = {n: args["m_" + n] for n in WEIGHTS}
    v = {n: args["v_" + n] for n in WEIGHTS}
    B, S, _ = x.shape
    ix, iy, ic = lax.axis_index("x"), lax.axis_index("y"), lax.axis_index("c")
    chip = 2 * ix + iy
    half = FLAT_ROWS // 2
    ada_cols = w_ada.shape[2]

    c_all = _exchange(c, "all", "gather", "gather_c").reshape(8 * B, D_MODEL)
    b_cols = lax.dynamic_slice_in_dim(b_ada, chip * ada_cols, ada_cols, axis=1)
    mod_cols = _ada_fwd(c_all, w_ada[0], b_cols)
    mod_all = _exchange(mod_cols, "xy", "gather", "gather_mod")
    mod_all = mod_all.transpose(1, 0, 2).reshape(8 * B, 6 * D_MODEL)
    mod = lax.dynamic_slice_in_dim(mod_all, (4 * ix + 2 * iy + ic) * B, B, axis=0)

    flat = _flatten_shards({n: w[n][0] for n, _, _ in BIG}).astype(BF16)
    mine = lax.dynamic_slice_in_dim(flat, ic * half, half, axis=0)
    halves = _exchange(mine, "xy", "gather", "gather_w_chips")
    others = _exchange(halves, "c", "swap", "gather_w_cores")
    south = ic == 0
    W = _unflatten_full(jnp.concatenate([jnp.where(south, halves, others), jnp.where(south, others, halves)], axis=1))

    wc_all = _exchange(w_conv[0], "xy", "gather", "gather_w_conv")
    P = {n: w[n][0] for n in SMALL if n not in ("w_conv", "g_final")}
    P["w_conv"] = wc_all.transpose(1, 0, 2).reshape(3, D_FF)
    P["g_final"] = g_final

    loss, dx, dmod, gw, gs = _local_step(x, mod, loss_target, W, P)

    loss = lax.psum(loss[0, 0], MESH_AXES)

    small_shapes = [gs[n].shape for n in SMALL]
    packed, counts = _pack_rows([gs[n] for n in SMALL] + [dmod])
    n_small = sum(counts[:-1])
    gathered = _exchange(packed, "all", "gather", "gather_small")
    small_sum = _sum_slots(gathered[:, :n_small], "sum_small")
    g_small = dict(zip(SMALL, _unpack_rows(small_sum, small_shapes)))
    dmod_all = gathered[:, n_small:].reshape(8, -1)[:, :B * 6 * D_MODEL].reshape(8 * B, 6 * D_MODEL)
    dmod_cols = lax.dynamic_slice_in_dim(dmod_all, chip * ada_cols, ada_cols, axis=1)
    g_w_ada, g_b_ada = _ada_bwd(c_all, dmod_all, dmod_cols)

    G = _flatten_full(gw)
    theirs = _exchange(G, "c", "half", "reduce_cores")
    ours = lax.dynamic_slice_in_dim(G, ic * half, half, axis=1)
    pair = _add2(ours.reshape(N_XY * half, LANES), theirs.reshape(N_XY * half, LANES), BF16).reshape(N_XY, half, LANES)
    parts = _exchange(pair, "xy", "scatter", "reduce_chips")
    red = _sum_slots(parts, "sum_chips")
    red_sib = _exchange(red, "c", "swap", "share_cores")
    g_flat = jnp.concatenate([jnp.where(south, red, red_sib), jnp.where(south, red_sib, red)], axis=0)
    g_big = _unflatten_shard(g_flat)

    grads = {"w_ada": g_w_ada[None], "b_ada": g_b_ada}
    for n, _, _ in BIG:
        grads[n] = g_big[n][None]
    wc_cols = w_conv.shape[2]
    for n in SMALL:
        g = g_small[n]
        if n == "w_conv":
            g = lax.dynamic_slice_in_dim(g, chip * wc_cols, wc_cols, axis=1)
        grads[n] = g.reshape(w[n].shape)

    delta, new_m, new_v = {}, {}, {}
    for n in ["w_ada"] + [b for b, _, _ in BIG]:
        shp = w[n].shape
        d2, m2, v2 = _adamw(w[n][0], grads[n][0], m[n][0], v[n][0], "adamw_" + n)
        delta[n], new_m[n], new_v[n] = d2.reshape(shp), m2.reshape(shp), v2.reshape(shp)
    rest = ["b_ada"] + SMALL
    shapes = [w[n].shape for n in rest]
    pw, _ = _pack_rows([w[n] for n in rest])
    pg, _ = _pack_rows([grads[n] for n in rest])
    pm, _ = _pack_rows([m[n] for n in rest])
    pv, _ = _pack_rows([v[n] for n in rest])
    d2, m2, v2 = _adamw(pw, pg, pm, pv, "adamw_small")
    for n, dd, mm, vv in zip(rest, _unpack_rows(d2, shapes), _unpack_rows(m2, shapes), _unpack_rows(v2, shapes)):
        delta[n], new_m[n], new_v[n] = dd, mm, vv

    return (loss, dx, *[grads[n] for n in WEIGHTS], *[delta[n] for n in WEIGHTS], *[new_m[n] for n in WEIGHTS],
            *[new_v[n] for n in WEIGHTS])
```

```python
import functools
import math

import numpy as np
import jax
import jax.numpy as jnp
from jax import lax
from jax.experimental import pallas as pl
from jax.experimental.pallas import tpu as pltpu

F32, BF16 = jnp.float32, jnp.bfloat16

D_MODEL = 1024
N_HEADS = 8
HEAD_DIM = 64
ATT_WIDTH = 512
SSM_GROUPS = 16
SSM_GROUP_CH = 16
SSM_WIDTH = 256
SSM_STATE = 64
SSM_LANES = SSM_GROUPS * SSM_STATE
D_FF = 2048
IN_WIDTH = 3 * ATT_WIDTH + SSM_WIDTH + 2 * D_MODEL
ATT_BLOCK = 128
N_PATTERNS = 3
EPS = 1e-6
NEG_INF = -1e30

ADAM_LR, ADAM_B1, ADAM_B2, ADAM_EPS, ADAM_WD, ADAM_STEP = 0.001, 0.9, 0.999, 1e-08, 0.01, 10

V7X_VMEM_LIMIT_BYTES = 56 * 1024 * 1024
LANES = 1024

MESH_AXES = ("x", "y", "c")


def _pcall(body, *, name, out_shape, grid=(), in_specs=None, out_specs=None, scratch_shapes=(), dims=None):
    params = dict(vmem_limit_bytes=V7X_VMEM_LIMIT_BYTES)
    if dims is not None:
        params["dimension_semantics"] = dims
    specs = {}
    if in_specs is not None:
        specs = dict(grid=grid, in_specs=in_specs, out_specs=out_specs)
    return pl.pallas_call(body, name=name, out_shape=out_shape, scratch_shapes=scratch_shapes,
                          compiler_params=pltpu.CompilerParams(**params), **specs)


def _sds(shape, dtype):
    return jax.ShapeDtypeStruct(tuple(shape), dtype)


def _tile(n, target):
    if n <= target:
        return n
    for t in range(target - target % 128, 0, -128):
        if n % t == 0:
            return t
    raise ValueError((n, target))


def _sig(v):
    return 1.0 / (1.0 + jnp.exp(-v))


def _mm(a, b, *, name, ta=False, tb=False, out_dtype=F32, tm=1024, tn=1024, tk=1024):
    if ta:
        K, M = a.shape
    else:
        M, K = a.shape
    if tb:
        N, K2 = b.shape
    else:
        K2, N = b.shape
    assert K == K2, (a.shape, b.shape)
    tm, tn, tk = _tile(M, tm), _tile(N, tn), _tile(K, tk)
    nk = K // tk
    a_spec = pl.BlockSpec((tk, tm), lambda i, j, k: (k, i)) if ta else pl.BlockSpec((tm, tk), lambda i, j, k: (i, k))
    b_spec = pl.BlockSpec((tn, tk), lambda i, j, k: (j, k)) if tb else pl.BlockSpec((tk, tn), lambda i, j, k: (k, j))
    dn = (((0 if ta else 1,), (1 if tb else 0,)), ((), ()))

    def body(a_ref, b_ref, o_ref, acc_ref):
        k = pl.program_id(2)

        @pl.when(k == 0)
        def _():
            acc_ref[...] = jnp.zeros_like(acc_ref)

        acc_ref[...] += lax.dot_general(a_ref[...].astype(BF16), b_ref[...].astype(BF16), dn,
                                        preferred_element_type=F32)

        @pl.when(k == nk - 1)
        def _():
            o_ref[...] = acc_ref[...].astype(out_dtype)

    def body_single(a_ref, b_ref, o_ref):
        o_ref[...] = lax.dot_general(a_ref[...].astype(BF16), b_ref[...].astype(BF16), dn,
                                     preferred_element_type=F32).astype(out_dtype)

    return _pcall(body_single if nk == 1 else body, name=name, out_shape=_sds((M, N), out_dtype),
                  grid=(M // tm, N // tn, nk), in_specs=[a_spec, b_spec],
                  out_specs=pl.BlockSpec((tm, tn), lambda i, j, k: (i, j)),
                  scratch_shapes=[] if nk == 1 else [pltpu.VMEM((tm, tn), F32)],
                  dims=("parallel", "parallel", "arbitrary"))(a, b)


def _ada_fwd(c_all, w_ada, b_ada_cols):
    n = w_ada.shape[1]

    def body(c_ref, w_ref, b_ref, o_ref):
        c = c_ref[...]
        act = c * _sig(c)
        o_ref[...] = jnp.dot(act.astype(BF16), w_ref[...].astype(BF16), preferred_element_type=F32) + b_ref[...]

    return _pcall(body, name="ada_fwd", out_shape=_sds((c_all.shape[0], n), F32))(c_all, w_ada, b_ada_cols)


def _ada_bwd(c_all, dmod_all, dmod_cols):
    n = dmod_cols.shape[1]

    def body(c_ref, da_ref, dc_ref, gw_ref, gb_ref):
        c = c_ref[...]
        act = c * _sig(c)
        gw_ref[...] = lax.dot_general(act, dc_ref[...], (((0,), (0,)), ((), ())), preferred_element_type=F32,
                                      precision=lax.Precision.HIGHEST)
        gb_ref[...] = jnp.sum(da_ref[...], axis=0, keepdims=True)

    return _pcall(body, name="ada_bwd", out_shape=(_sds((D_MODEL, n), F32), _sds((1, dmod_all.shape[1]), F32)))(
        c_all, dmod_all, dmod_cols)


ROW_TILE = 512


def _row_specs(B, S):
    ts = min(S, ROW_TILE)
    row = pl.BlockSpec((1, ts, D_MODEL), lambda b, s: (b, s, 0))
    bvec = pl.BlockSpec((1, 1, D_MODEL), lambda b, s: (b, 0, 0))
    gvec = pl.BlockSpec((1, D_MODEL), lambda b, s: (0, 0))
    return ts, row, bvec, gvec


def _norm_mod(x3, g, sc, sh):
    B, S, _ = x3.shape
    ts, row, bvec, gvec = _row_specs(B, S)

    def body(x_ref, g_ref, sc_ref, sh_ref, u_ref):
        x = x_ref[0]
        r = lax.rsqrt(jnp.mean(x * x, axis=-1, keepdims=True) + EPS)
        u_ref[0] = ((x * r) * g_ref[...] * (1.0 + sc_ref[0]) + sh_ref[0]).astype(BF16)

    return _pcall(body, name="norm_mod1", out_shape=_sds(x3.shape, BF16), grid=(B, S // ts),
                  in_specs=[row, gvec, bvec, bvec], out_specs=row, dims=("parallel", "parallel"))(x3, g, sc, sh)


def _resid_norm_mod(x3, mix3, gt, g, sc, sh):
    B, S, _ = x3.shape
    ts, row, bvec, gvec = _row_specs(B, S)

    def body(x_ref, m_ref, gt_ref, g_ref, sc_ref, sh_ref, h_ref, u_ref):
        h = x_ref[0] + gt_ref[0] * m_ref[0]
        h_ref[0] = h
        r = lax.rsqrt(jnp.mean(h * h, axis=-1, keepdims=True) + EPS)
        u_ref[0] = ((h * r) * g_ref[...] * (1.0 + sc_ref[0]) + sh_ref[0]).astype(BF16)

    return _pcall(body, name="resid_norm_mod2", out_shape=(_sds(x3.shape, F32), _sds(x3.shape, BF16)),
                  grid=(B, S // ts), in_specs=[row, row, bvec, gvec, bvec, bvec], out_specs=(row, row),
                  dims=("parallel", "parallel"))(x3, mix3, gt, g, sc, sh)


def _norm_bwd(h3, du3, dres3, g, sc, name, mix3=None, gt=None):
    B, S, _ = h3.shape
    ts, row, bvec, gvec = _row_specs(B, S)
    with_gate = mix3 is not None

    def body(*refs):
        if with_gate:
            h_ref, du_ref, dr_ref, g_ref, sc_ref, m_ref, gt_ref, dh_ref, dsh_ref, dsc_ref, dg_ref, dgt_ref, dm_ref = refs
        else:
            h_ref, du_ref, dr_ref, g_ref, sc_ref, dh_ref, dsh_ref, dsc_ref, dg_ref = refs
        b, s = pl.program_id(0), pl.program_id(1)
        h = h_ref[0]
        r = lax.rsqrt(jnp.mean(h * h, axis=-1, keepdims=True) + EPS)
        xn = h * r
        du = du_ref[0]
        g = g_ref[...]
        sc1 = 1.0 + sc_ref[0]
        dxn = du * g * sc1
        dh = dr_ref[0] + r * (dxn - xn * jnp.mean(dxn * xn, axis=-1, keepdims=True))
        dh_ref[0] = dh

        @pl.when(s == 0)
        def _():
            dsh_ref[...] = jnp.zeros_like(dsh_ref)
            dsc_ref[...] = jnp.zeros_like(dsc_ref)
            if with_gate:
                dgt_ref[...] = jnp.zeros_like(dgt_ref)

        @pl.when((s == 0) & (b == 0))
        def _():
            dg_ref[...] = jnp.zeros_like(dg_ref)

        dux = du * xn
        dsh_ref[0] += jnp.sum(du, axis=0, keepdims=True)
        dsc_ref[0] += jnp.sum(dux * g, axis=0, keepdims=True)
        dg_ref[...] += jnp.sum(dux * sc1, axis=0, keepdims=True)
        if with_gate:
            dgt_ref[0] += jnp.sum(dh * m_ref[0], axis=0, keepdims=True)
            dm_ref[0] = (dh * gt_ref[0]).astype(BF16)

    bshape = _sds((B, 1, D_MODEL), F32)
    in_specs = [row, row, row, gvec, bvec]
    out_shape = [_sds(h3.shape, F32), bshape, bshape, _sds((1, D_MODEL), F32)]
    out_specs = [row, bvec, bvec, gvec]
    args = [h3, du3, dres3, g, sc]
    if with_gate:
        in_specs += [row, bvec]
        out_shape += [bshape, _sds(h3.shape, BF16)]
        out_specs += [bvec, row]
        args += [mix3, gt]
    return _pcall(body, name=name, out_shape=tuple(out_shape), grid=(B, S // ts), in_specs=in_specs,
                  out_specs=tuple(out_specs), dims=("arbitrary", "arbitrary"))(*args)


def _final_loss(h1, ffn3, tgt3, gt, gfin):
    B, S, _ = h1.shape
    ts, row, bvec, gvec = _row_specs(B, S)
    one = pl.BlockSpec((1, 1), lambda b, s: (0, 0))

    def body(h_ref, f_ref, t_ref, gt_ref, gf_ref, dh_ref, dff_ref, dgt_ref, dgf_ref, loss_ref):
        b, s = pl.program_id(0), pl.program_id(1)
        f = f_ref[0]
        gtv = gt_ref[0]
        gf = gf_ref[...]
        h2 = h_ref[0] + gtv * f
        r = lax.rsqrt(jnp.mean(h2 * h2, axis=-1, keepdims=True) + EPS)
        n = h2 * r
        e = n * gf - t_ref[0]
        dy = e * (1.0 / D_MODEL)
        dn = dy * gf
        dh2 = r * (dn - n * jnp.mean(dn * n, axis=-1, keepdims=True))
        dh_ref[0] = dh2
        dff_ref[0] = (dh2 * gtv).astype(BF16)

        @pl.when(s == 0)
        def _():
            dgt_ref[...] = jnp.zeros_like(dgt_ref)

        @pl.when((s == 0) & (b == 0))
        def _():
            dgf_ref[...] = jnp.zeros_like(dgf_ref)
            loss_ref[...] = jnp.zeros_like(loss_ref)

        dgt_ref[0] += jnp.sum(dh2 * f, axis=0, keepdims=True)
        dgf_ref[...] += jnp.sum(dy * n, axis=0, keepdims=True)
        rows = jnp.sum(e * e, axis=1, keepdims=True)
        loss_ref[...] += jnp.sum(rows, axis=0, keepdims=True) * (0.5 / D_MODEL)

    return _pcall(body, name="final_loss",
                  out_shape=(_sds(h1.shape, F32), _sds(h1.shape, BF16), _sds((B, 1, D_MODEL), F32),
                             _sds((1, D_MODEL), F32), _sds((1, 1), F32)),
                  grid=(B, S // ts), in_specs=[row, row, row, bvec, gvec], out_specs=(row, row, bvec, gvec, one),
                  dims=("arbitrary", "arbitrary"))(h1, ffn3, tgt3, gt, gfin)


def _att_scores(qh, kc, kp, h, dil, first, a_idx, j_idx):
    scale = HEAD_DIM ** -0.5
    nt = (((1,), (1,)), ((), ()))
    slope = (2.0 ** (-8.0 * (h + 1) / N_HEADS)) * dil
    dist_c = (a_idx - j_idx).astype(F32)
    s_c = lax.dot_general(qh, kc, nt, preferred_element_type=F32) * scale
    s_c = jnp.where(a_idx >= j_idx, s_c - slope * dist_c, NEG_INF)
    s_p = lax.dot_general(qh, kp, nt, preferred_element_type=F32) * scale
    s_p = jnp.where((j_idx >= a_idx) & jnp.logical_not(first), s_p - slope * (dist_c + float(ATT_BLOCK)), NEG_INF)
    return s_c, s_p


def _att_block_consts(seq_blocks):
    p = pl.program_id(0)
    j = pl.program_id(1)
    nb = lax.shift_right_logical(jnp.int32(seq_blocks), 2 * p)
    dil = lax.shift_left(jnp.int32(1), 2 * p).astype(F32)
    a_idx = lax.broadcasted_iota(jnp.int32, (ATT_BLOCK, ATT_BLOCK), 0)
    j_idx = lax.broadcasted_iota(jnp.int32, (ATT_BLOCK, ATT_BLOCK), 1)
    return j, nb, dil, a_idx, j_idx


def _attn_fwd(qb, kb, vb, seq_blocks):
    _, NB, _, _ = qb.shape
    cur = pl.BlockSpec((None, None, ATT_BLOCK, ATT_WIDTH), lambda p, j: (p, j, 0, 0))
    prev = pl.BlockSpec((None, None, ATT_BLOCK, ATT_WIDTH), lambda p, j: (p, jnp.maximum(j - 1, 0), 0, 0))
    lse_spec = pl.BlockSpec((None, None, ATT_BLOCK, N_HEADS), lambda p, j: (p, j, 0, 0))

    def body(q_ref, kc_ref, kp_ref, vc_ref, vp_ref, o_ref, lse_ref):
        j, nb, dil, a_idx, j_idx = _att_block_consts(seq_blocks)
        first = lax.rem(j, nb) == 0
        for h in range(N_HEADS):
            hs = slice(h * HEAD_DIM, (h + 1) * HEAD_DIM)
            s_c, s_p = _att_scores(q_ref[:, hs], kc_ref[:, hs], kp_ref[:, hs], h, dil, first, a_idx, j_idx)
            m = jnp.maximum(jnp.max(s_c, axis=1, keepdims=True), jnp.max(s_p, axis=1, keepdims=True))
            p_c = jnp.exp(s_c - m)
            p_p = jnp.exp(s_p - m)
            den = jnp.sum(p_c, axis=1, keepdims=True) + jnp.sum(p_p, axis=1, keepdims=True)
            o = (jnp.dot(p_c.astype(BF16), vc_ref[:, hs], preferred_element_type=F32)
                 + jnp.dot(p_p.astype(BF16), vp_ref[:, hs], preferred_element_type=F32))
            o_ref[:, hs] = o / den
            lse_ref[:, h:h + 1] = m + jnp.log(den)

    return _pcall(body, name="attn_fwd",
                  out_shape=(_sds(qb.shape, F32), _sds((N_PATTERNS, NB, ATT_BLOCK, N_HEADS), F32)),
                  grid=(N_PATTERNS, NB), in_specs=[cur, cur, prev, cur, prev], out_specs=(cur, lse_spec),
                  dims=("parallel", "parallel"))(qb, kb, kb, vb, vb)


def _attn_combine(o_p, lse_p):
    _, T, _ = o_p.shape
    tm = min(T, 1024)

    def body(o_ref, l_ref, out_ref, lse_ref):
        l0, l1, l2 = l_ref[0], l_ref[1], l_ref[2]
        m = jnp.maximum(jnp.maximum(l0, l1), l2)
        lse = m + jnp.log(jnp.exp(l0 - m) + jnp.exp(l1 - m) + jnp.exp(l2 - m))
        lse_ref[...] = lse
        w = [jnp.exp(l0 - lse), jnp.exp(l1 - lse), jnp.exp(l2 - lse)]
        for h in range(N_HEADS):
            hs = slice(h * HEAD_DIM, (h + 1) * HEAD_DIM)
            acc = w[0][:, h:h + 1] * o_ref[0, :, hs]
            acc = acc + w[1][:, h:h + 1] * o_ref[1, :, hs]
            acc = acc + w[2][:, h:h + 1] * o_ref[2, :, hs]
            out_ref[:, hs] = acc.astype(BF16)

    return _pcall(body, name="attn_combine", out_shape=(_sds((T, ATT_WIDTH), BF16), _sds((T, N_HEADS), F32)),
                  grid=(T // tm,),
                  in_specs=[pl.BlockSpec((N_PATTERNS, tm, ATT_WIDTH), lambda i: (0, i, 0)),
                            pl.BlockSpec((N_PATTERNS, tm, N_HEADS), lambda i: (0, i, 0))],
                  out_specs=(pl.BlockSpec((tm, ATT_WIDTH), lambda i: (i, 0)), pl.BlockSpec((tm, N_HEADS), lambda i: (i, 0))),
                  dims=("parallel",))(o_p, lse_p)


def _attn_bwd(qb, kb, vb, dob, ob, lseb, seq_blocks):
    _, NB, _, _ = qb.shape
    last = NB - 1
    cur = pl.BlockSpec((None, None, ATT_BLOCK, ATT_WIDTH), lambda p, j: (p, jnp.minimum(j, last), 0, 0))
    prev = pl.BlockSpec((None, None, ATT_BLOCK, ATT_WIDTH),
                        lambda p, j: (p, jnp.maximum(jnp.minimum(j, last) - 1, 0), 0, 0))
    lag = pl.BlockSpec((None, None, ATT_BLOCK, ATT_WIDTH), lambda p, j: (p, jnp.maximum(j - 1, 0), 0, 0))
    lse_spec = pl.BlockSpec((None, None, ATT_BLOCK, N_HEADS), lambda p, j: (p, jnp.minimum(j, last), 0, 0))
    scale = HEAD_DIM ** -0.5
    tn = (((0,), (0,)), ((), ()))
    nt = (((1,), (1,)), ((), ()))

    def body(q_ref, kc_ref, kp_ref, vc_ref, vp_ref, do_ref, o_ref, lse_ref, dq_ref, dk_ref, dv_ref, ck_ref, cv_ref):
        j, nb, dil, a_idx, j_idx = _att_block_consts(seq_blocks)

        @pl.when(j == 0)
        def _():
            ck_ref[...] = jnp.zeros_like(ck_ref)
            cv_ref[...] = jnp.zeros_like(cv_ref)

        @pl.when(j <= last)
        def _():
            first = lax.rem(j, nb) == 0
            for h in range(N_HEADS):
                hs = slice(h * HEAD_DIM, (h + 1) * HEAD_DIM)
                qh, kc, kp, vc, vp, doh = q_ref[:, hs], kc_ref[:, hs], kp_ref[:, hs], vc_ref[:, hs], vp_ref[:, hs], do_ref[:, hs]
                s_c, s_p = _att_scores(qh, kc, kp, h, dil, first, a_idx, j_idx)
                lse = lse_ref[:, h:h + 1]
                p_c = jnp.exp(s_c - lse)
                p_p = jnp.exp(s_p - lse)
                delta = jnp.sum(doh.astype(F32) * o_ref[:, hs].astype(F32), axis=1, keepdims=True)
                ds_c = (p_c * (lax.dot_general(doh, vc, nt, preferred_element_type=F32) - delta)).astype(BF16)
                ds_p = (p_p * (lax.dot_general(doh, vp, nt, preferred_element_type=F32) - delta)).astype(BF16)
                dq_ref[:, hs] = (jnp.dot(ds_c, kc, preferred_element_type=F32)
                                 + jnp.dot(ds_p, kp, preferred_element_type=F32)) * scale
                dk_ref[:, hs] = ck_ref[:, hs] + lax.dot_general(ds_p, qh, tn, preferred_element_type=F32) * scale
                dv_ref[:, hs] = cv_ref[:, hs] + lax.dot_general(p_p.astype(BF16), doh, tn, preferred_element_type=F32)
                ck_ref[:, hs] = lax.dot_general(ds_c, qh, tn, preferred_element_type=F32) * scale
                cv_ref[:, hs] = lax.dot_general(p_c.astype(BF16), doh, tn, preferred_element_type=F32)

        @pl.when(j == NB)
        def _():
            dk_ref[...] = ck_ref[...]
            dv_ref[...] = cv_ref[...]

    shp = _sds(qb.shape, F32)
    return _pcall(body, name="attn_bwd", out_shape=(shp, shp, shp), grid=(N_PATTERNS, NB + 1),
                  in_specs=[cur, cur, prev, cur, prev, cur, cur, lse_spec], out_specs=(cur, lag, lag),
                  scratch_shapes=[pltpu.VMEM((ATT_BLOCK, ATT_WIDTH), F32), pltpu.VMEM((ATT_BLOCK, ATT_WIDTH), F32)],
                  dims=("arbitrary", "arbitrary"))(qb, kb, kb, vb, vb, dob, ob, lseb)


def _sum3_cast(a, b, c):
    T, N = a.shape
    tm = min(T, 1024)
    spec = pl.BlockSpec((tm, N), lambda i: (i, 0))

    def body(a_ref, b_ref, c_ref, o_ref):
        o_ref[...] = (a_ref[...] + b_ref[...] + c_ref[...]).astype(BF16)

    return _pcall(body, name="sum3_cast", out_shape=_sds((T, N), BF16), grid=(T // tm,), in_specs=[spec] * 3,
                  out_specs=spec, dims=("parallel",))(a, b, c)


def _to_blocks(t, B, S):
    C = t.shape[-1]
    outs = []
    for p in range(N_PATTERNS):
        d = 4 ** p
        u = t.reshape(B, S // d, d, C).transpose(0, 2, 1, 3)
        outs.append(u.reshape(B * S // ATT_BLOCK, ATT_BLOCK, C))
    return jnp.stack(outs, axis=0)


def _from_blocks(tb, B, S):
    C = tb.shape[-1]
    outs = []
    for p in range(N_PATTERNS):
        d = 4 ** p
        u = tb[p].reshape(B, d, S // d, C).transpose(0, 2, 1, 3)
        outs.append(u.reshape(B * S, C))
    return jnp.stack(outs, axis=0)


ATT_GROUP = 4
ATT_GW = ATT_GROUP * HEAD_DIM
ATT_GROUPS = N_HEADS // ATT_GROUP
ATT_PAIRS = ATT_GW // ATT_BLOCK
NT_DIMS = (((1,), (1,)), ((), ()))
TN_DIMS = (((0,), (0,)), ((), ()))


def _att_rows(start, d):
    if d == 1:
        return pl.ds(start if isinstance(start, int) else pl.multiple_of(start, ATT_BLOCK), ATT_BLOCK)
    return pl.ds(start, ATT_BLOCK, stride=d)


def _att_fill_bias(bias_ref, g, d):
    a = lax.broadcasted_iota(jnp.int32, (ATT_BLOCK, ATT_BLOCK), 0)
    j = lax.broadcasted_iota(jnp.int32, (ATT_BLOCK, ATT_BLOCK), 1)
    dist = (a - j).astype(F32)
    for hh in range(ATT_GROUP):
        lo = 2.0 ** (-8.0 * (hh + 1) / N_HEADS) * d
        hi = 2.0 ** (-8.0 * (ATT_GROUP + hh + 1) / N_HEADS) * d
        slope = jnp.where(g == 0, lo, hi).astype(F32)
        bias_ref[hh, 0] = jnp.where(a >= j, -slope * dist, NEG_INF)
        bias_ref[hh, 1] = jnp.where(j >= a, -slope * (dist + float(ATT_BLOCK)), NEG_INF)


def _attention_fwd(proj3, seq_blocks):
    B, S, _ = proj3.shape
    scale = HEAD_DIM ** -0.5
    nq = ATT_WIDTH // ATT_GW

    def col(k):
        return pl.BlockSpec((1, S, ATT_GW), lambda b, g, k=k: (b, 0, k * nq + g))

    o_spec = pl.BlockSpec((1, S, ATT_GW), lambda b, g: (b, 0, g))
    l_spec = pl.BlockSpec((1, 1, S, ATT_BLOCK), lambda b, g: (b, g, 0, 0))

    def body(q_ref, k_ref, v_ref, o_ref, lse_ref, qf, kf, vf, os, ls, bias):
        g = pl.program_id(1)
        for t in range(ATT_PAIRS):
            ts = slice(t * ATT_BLOCK, (t + 1) * ATT_BLOCK)
            qf[t] = q_ref[0, :, ts].astype(F32) * scale
            kf[t] = k_ref[0, :, ts].astype(F32)
            vf[t] = v_ref[0, :, ts].astype(F32)
        lane = lax.broadcasted_iota(jnp.int32, (ATT_BLOCK, ATT_BLOCK), 1)
        low = lane < HEAD_DIM

        def block(p, d, r, n, has_prev):
            start = n * (ATT_BLOCK * d) + r
            rows = _att_rows(start, d)
            prows = _att_rows(start - ATT_BLOCK * d, d) if has_prev else None
            lse_t = jnp.zeros((ATT_BLOCK, ATT_BLOCK), F32)
            for t in range(ATT_PAIRS):
                q2 = qf[t, rows, :]
                kc = kf[t, rows, :].astype(BF16)
                vc = vf[t, rows, :].astype(BF16)
                if has_prev:
                    kp = kf[t, prows, :].astype(BF16)
                    vp = vf[t, prows, :].astype(BF16)
                outs = []
                for e in range(2):
                    hh = 2 * t + e
                    qh = jnp.where(low if e == 0 else jnp.logical_not(low), q2, 0.0).astype(BF16)
                    s_c = lax.dot_general(qh, kc, NT_DIMS, preferred_element_type=F32) + bias[hh, 0]
                    m = jnp.max(s_c, axis=1, keepdims=True)
                    if has_prev:
                        s_p = lax.dot_general(qh, kp, NT_DIMS, preferred_element_type=F32) + bias[hh, 1]
                        m = jnp.maximum(m, jnp.max(s_p, axis=1, keepdims=True))
                    p_c = jnp.exp(s_c - m)
                    den = jnp.sum(p_c, axis=1, keepdims=True)
                    o = jnp.dot(p_c.astype(BF16), vc, preferred_element_type=F32)
                    if has_prev:
                        p_p = jnp.exp(s_p - m)
                        den = den + jnp.sum(p_p, axis=1, keepdims=True)
                        o = o + jnp.dot(p_p.astype(BF16), vp, preferred_element_type=F32)
                    outs.append(o / den)
                    lse_t = jnp.where(lane == hh, m + jnp.log(den), lse_t)
                os[p, t, rows, :] = jnp.where(low, outs[0], outs[1])
            ls[p, rows, :] = lse_t

        for p in range(N_PATTERNS):
            d = 4 ** p
            _att_fill_bias(bias, g, d)
            _att_one_pattern(block, p, d, seq_blocks // d)

        def combine(i, carry):
            rows = pl.ds(pl.multiple_of(i * ATT_BLOCK, ATT_BLOCK), ATT_BLOCK)
            l0, l1, l2 = ls[0, rows, :], ls[1, rows, :], ls[2, rows, :]
            m = jnp.maximum(jnp.maximum(l0, l1), l2)
            lse = m + jnp.log(jnp.exp(l0 - m) + jnp.exp(l1 - m) + jnp.exp(l2 - m))
            lse_ref[0, 0, rows, :] = lse
            w = [jnp.exp(l0 - lse), jnp.exp(l1 - lse), jnp.exp(l2 - lse)]
            for t in range(ATT_PAIRS):
                acc = jnp.zeros((ATT_BLOCK, ATT_BLOCK), F32)
                for p in range(N_PATTERNS):
                    wt = jnp.where(low, w[p][:, 2 * t:2 * t + 1], w[p][:, 2 * t + 1:2 * t + 2])
                    acc = acc + wt * os[p, t, rows, :]
                o_ref[0, rows, t * ATT_BLOCK:(t + 1) * ATT_BLOCK] = acc.astype(BF16)
            return carry

        lax.fori_loop(0, S // ATT_BLOCK, combine, 0)

    return _pcall(body, name="attention_fwd",
                  out_shape=(_sds((B, S, ATT_WIDTH), BF16), _sds((B, ATT_GROUPS, S, ATT_BLOCK), F32)),
                  grid=(B, ATT_GROUPS), in_specs=[col(0), col(1), col(2)], out_specs=(o_spec, l_spec),
                  scratch_shapes=[pltpu.VMEM((ATT_PAIRS, S, ATT_BLOCK), F32)] * 3
                  + [pltpu.VMEM((N_PATTERNS, ATT_PAIRS, S, ATT_BLOCK), F32), pltpu.VMEM((N_PATTERNS, S, ATT_BLOCK), F32),
                     pltpu.VMEM((ATT_GROUP, 2, ATT_BLOCK, ATT_BLOCK), F32)],
                  dims=("parallel", "parallel"))(proj3, proj3, proj3)


def _att_one_pattern(block, p, d, nb):
    def per_residue(r, carry):
        block(p, d, r, 0, False)
        if nb > 1:
            def per_block(n, c2):
                block(p, d, r, n, True)
                return c2
            lax.fori_loop(1, nb, per_block, 0)
        return carry

    if d == 1:
        per_residue(0, 0)
    else:
        lax.fori_loop(0, d, per_residue, 0)


def _attention_bwd(proj3, do3, o3, lse4, seq_blocks):
    B, S, _ = proj3.shape
    scale = HEAD_DIM ** -0.5
    nq = ATT_WIDTH // ATT_GW

    def col(k):
        return pl.BlockSpec((1, S, ATT_GW), lambda b, g, k=k: (b, 0, k * nq + g))

    o_spec = pl.BlockSpec((1, S, ATT_GW), lambda b, g: (b, 0, g))
    l_spec = pl.BlockSpec((1, 1, S, ATT_BLOCK), lambda b, g: (b, g, 0, 0))

    def body(q_ref, k_ref, v_ref, do_ref, o_ref, lse_ref, dq_ref, dk_ref, dv_ref,
             qf, kf, vf, dof, dl, aq, ak, av, bias):
        g = pl.program_id(1)
        for t in range(ATT_PAIRS):
            ts = slice(t * ATT_BLOCK, (t + 1) * ATT_BLOCK)
            qf[t] = q_ref[0, :, ts].astype(F32) * scale
            kf[t] = k_ref[0, :, ts].astype(F32)
            vf[t] = v_ref[0, :, ts].astype(F32)
            dof[t] = do_ref[0, :, ts].astype(F32)
        aq[...] = jnp.zeros_like(aq)
        ak[...] = jnp.zeros_like(ak)
        av[...] = jnp.zeros_like(av)
        lane = lax.broadcasted_iota(jnp.int32, (ATT_BLOCK, ATT_BLOCK), 1)
        low = lane < HEAD_DIM

        def fill_delta(i, carry):
            rows = pl.ds(pl.multiple_of(i * ATT_BLOCK, ATT_BLOCK), ATT_BLOCK)
            acc = jnp.zeros((ATT_BLOCK, ATT_BLOCK), F32)
            for t in range(ATT_PAIRS):
                prod = dof[t, rows, :] * o_ref[0, rows, t * ATT_BLOCK:(t + 1) * ATT_BLOCK].astype(F32)
                lo = jnp.sum(jnp.where(low, prod, 0.0), axis=1, keepdims=True)
                hi = jnp.sum(prod, axis=1, keepdims=True) - lo
                acc = jnp.where(lane == 2 * t, lo, acc)
                acc = jnp.where(lane == 2 * t + 1, hi, acc)
            dl[rows, :] = acc
            return carry

        lax.fori_loop(0, S // ATT_BLOCK, fill_delta, 0)

        def block(p, d, r, n, has_prev):
            start = n * (ATT_BLOCK * d) + r
            rows = _att_rows(start, d)
            prows = _att_rows(start - ATT_BLOCK * d, d) if has_prev else None
            lse_t = lse_ref[0, 0, rows, :]
            dl_t = dl[rows, :]
            for t in range(ATT_PAIRS):
                q2 = qf[t, rows, :]
                do2 = dof[t, rows, :]
                kc = kf[t, rows, :].astype(BF16)
                vc = vf[t, rows, :].astype(BF16)
                if has_prev:
                    kp = kf[t, prows, :].astype(BF16)
                    vp = vf[t, prows, :].astype(BF16)
                dqs, dk_c, dv_c, dk_p, dv_p = [], None, None, None, None
                for e in range(2):
                    hh = 2 * t + e
                    keep = low if e == 0 else jnp.logical_not(low)
                    qh = jnp.where(keep, q2, 0.0).astype(BF16)
                    doh = jnp.where(keep, do2, 0.0).astype(BF16)
                    lse_h = lse_t[:, hh:hh + 1]
                    dl_h = dl_t[:, hh:hh + 1]
                    s_c = lax.dot_general(qh, kc, NT_DIMS, preferred_element_type=F32) + bias[hh, 0]
                    p_c = jnp.exp(s_c - lse_h)
                    ds_c = (p_c * (lax.dot_general(doh, vc, NT_DIMS, preferred_element_type=F32) - dl_h)).astype(BF16)
                    dq = jnp.dot(ds_c, kc, preferred_element_type=F32)
                    a = lax.dot_general(ds_c, qh, TN_DIMS, preferred_element_type=F32)
                    b = lax.dot_general(p_c.astype(BF16), doh, TN_DIMS, preferred_element_type=F32)
                    dk_c = a if dk_c is None else dk_c + a
                    dv_c = b if dv_c is None else dv_c + b
                    if has_prev:
                        s_p = lax.dot_general(qh, kp, NT_DIMS, preferred_element_type=F32) + bias[hh, 1]
                        p_p = jnp.exp(s_p - lse_h)
                        ds_p = (p_p * (lax.dot_general(doh, vp, NT_DIMS, preferred_element_type=F32) - dl_h)).astype(BF16)
                        dq = dq + jnp.dot(ds_p, kp, preferred_element_type=F32)
                        a = lax.dot_general(ds_p, qh, TN_DIMS, preferred_element_type=F32)
                        b = lax.dot_general(p_p.astype(BF16), doh, TN_DIMS, preferred_element_type=F32)
                        dk_p = a if dk_p is None else dk_p + a
                        dv_p = b if dv_p is None else dv_p + b
                    dqs.append(dq)
                aq[t, rows, :] = aq[t, rows, :] + jnp.where(low, dqs[0], dqs[1]) * scale
                ak[t, rows, :] = ak[t, rows, :] + dk_c
                av[t, rows, :] = av[t, rows, :] + dv_c
                if has_prev:
                    ak[t, prows, :] = ak[t, prows, :] + dk_p
                    av[t, prows, :] = av[t, prows, :] + dv_p

        for p in range(N_PATTERNS):
            d = 4 ** p
            _att_fill_bias(bias, g, d)
            _att_one_pattern(block, p, d, seq_blocks // d)

        for t in range(ATT_PAIRS):
            ts = slice(t * ATT_BLOCK, (t + 1) * ATT_BLOCK)
            dq_ref[0, :, ts] = aq[t].astype(BF16)
            dk_ref[0, :, ts] = ak[t].astype(BF16)
            dv_ref[0, :, ts] = av[t].astype(BF16)

    shp = _sds((B, S, ATT_WIDTH), BF16)
    pair_buf = pltpu.VMEM((ATT_PAIRS, S, ATT_BLOCK), F32)
    return _pcall(body, name="attention_bwd", out_shape=(shp, shp, shp), grid=(B, ATT_GROUPS),
                  in_specs=[col(0), col(1), col(2), o_spec, o_spec, l_spec], out_specs=(o_spec, o_spec, o_spec),
                  scratch_shapes=[pair_buf] * 4 + [pltpu.VMEM((S, ATT_BLOCK), F32)] + [pair_buf] * 3
                  + [pltpu.VMEM((ATT_GROUP, 2, ATT_BLOCK, ATT_BLOCK), F32)],
                  dims=("parallel", "parallel"))(proj3, proj3, proj3, do3, o3, lse4)


def _expand_groups(m):
    rows = SSM_WIDTH
    t = jnp.concatenate([m] * SSM_GROUPS, axis=0)
    r = lax.broadcasted_iota(jnp.int32, (rows, SSM_LANES), 0)
    l = lax.broadcasted_iota(jnp.int32, (rows, SSM_LANES), 1)
    keep = lax.shift_right_logical(r, 4) == lax.shift_right_logical(l, 6)
    return jnp.where(keep, t, 0.0)


def _collapse_groups(m):
    rows = SSM_WIDTH
    r = lax.broadcasted_iota(jnp.int32, (rows, SSM_LANES), 0)
    l = lax.broadcasted_iota(jnp.int32, (rows, SSM_LANES), 1)
    keep = lax.shift_right_logical(r, 4) == lax.shift_right_logical(l, 6)
    t = jnp.where(keep, m, 0.0)
    acc = t[0:SSM_GROUP_CH]
    for g in range(1, SSM_GROUPS):
        acc = acc + t[g * SSM_GROUP_CH:(g + 1) * SSM_GROUP_CH]
    return acc


def _zoh(lr, li, ldt):
    dt = jnp.exp(ldt)
    mag = jnp.exp(lr * dt)
    ang = li * dt
    cs, sn = jnp.cos(ang), jnp.sin(ang)
    ab_re, ab_im = mag * cs, mag * sn
    nr, ni = ab_re - 1.0, ab_im
    den = lr * lr + li * li
    n_re = nr * lr + ni * li
    n_im = ni * lr - nr * li
    return dict(dt=dt, mag=mag, cs=cs, sn=sn, ab_re=ab_re, ab_im=ab_im, nr=nr, ni=ni, den=den, n_re=n_re, n_im=n_im,
                f_re=n_re / den, f_im=n_im / den)


def _ssm_params(lr, li, ldt, br, bi, cr, ci):
    def body(lr_ref, li_ref, ldt_ref, br_ref, bi_ref, cr_ref, ci_ref, ab_ref, w_ref, c_ref):
        z = _zoh(lr_ref[...], li_ref[...], ldt_ref[...])
        ab_ref[0:1, :] = z["ab_re"]
        ab_ref[1:2, :] = z["ab_im"]
        br, bi = br_ref[...], bi_ref[...]
        w_ref[:, 0:SSM_LANES] = _expand_groups(z["f_re"] * br - z["f_im"] * bi).astype(BF16)
        w_ref[:, SSM_LANES:] = _expand_groups(z["f_re"] * bi + z["f_im"] * br).astype(BF16)
        c_ref[:, 0:SSM_LANES] = _expand_groups(cr_ref[...]).astype(BF16)
        c_ref[:, SSM_LANES:] = _expand_groups(-ci_ref[...]).astype(BF16)

    return _pcall(body, name="ssm_params",
                  out_shape=(_sds((2, SSM_LANES), F32), _sds((SSM_WIDTH, 2 * SSM_LANES), BF16),
                             _sds((SSM_WIDTH, 2 * SSM_LANES), BF16)))(lr, li, ldt, br, bi, cr, ci)


def _ssm_params_bwd(lr, li, ldt, br, bi, dab, dw, dc):
    def body(lr_ref, li_ref, ldt_ref, br_ref, bi_ref, dab_ref, dw_ref, dc_ref,
             dlr_ref, dli_ref, dldt_ref, dbr_ref, dbi_ref, dcr_ref, dci_ref):
        lr, li = lr_ref[...], li_ref[...]
        z = _zoh(lr, li, ldt_ref[...])
        br, bi = br_ref[...], bi_ref[...]
        dbb_re = _collapse_groups(dw_ref[:, 0:SSM_LANES])
        dbb_im = _collapse_groups(dw_ref[:, SSM_LANES:])
        dcr_ref[...] = _collapse_groups(dc_ref[:, 0:SSM_LANES])
        dci_ref[...] = -_collapse_groups(dc_ref[:, SSM_LANES:])
        f_re, f_im = z["f_re"], z["f_im"]
        dbr_ref[...] = f_re * dbb_re + f_im * dbb_im
        dbi_ref[...] = f_re * dbb_im - f_im * dbb_re
        df_re = jnp.sum(dbb_re * br + dbb_im * bi, axis=0, keepdims=True)
        df_im = jnp.sum(dbb_im * br - dbb_re * bi, axis=0, keepdims=True)
        den = z["den"]
        dn_re, dn_im = df_re / den, df_im / den
        dden = -(df_re * z["n_re"] + df_im * z["n_im"]) / (den * den)
        dnr = dn_re * lr - dn_im * li
        dni = dn_re * li + dn_im * lr
        dlr = dn_re * z["nr"] + dn_im * z["ni"] + 2.0 * dden * lr
        dli = dn_re * z["ni"] - dn_im * z["nr"] + 2.0 * dden * li
        dab_re = dab_ref[0:1, :] + dnr
        dab_im = dab_ref[1:2, :] + dni
        mag, cs, sn, dt = z["mag"], z["cs"], z["sn"], z["dt"]
        dmag = dab_re * cs + dab_im * sn
        dang = mag * (dab_im * cs - dab_re * sn)
        dlr_ref[...] = dlr + dmag * mag * dt
        dli_ref[...] = dli + dang * dt
        ddt = dmag * mag * lr + dang * li
        per_lane = jnp.broadcast_to(ddt * dt, (8, SSM_LANES))
        lane = lax.broadcasted_iota(jnp.int32, (SSM_LANES, 128), 0)
        col = lax.broadcasted_iota(jnp.int32, (SSM_LANES, 128), 1)
        ind = jnp.where(lax.shift_right_logical(lane, 6) == col, 1.0, 0.0)
        dldt_ref[...] = jnp.dot(per_lane, ind, preferred_element_type=F32, precision=lax.Precision.HIGHEST)[0:1]

    vec = _sds((1, SSM_LANES), F32)
    mat = _sds((SSM_GROUP_CH, SSM_LANES), F32)
    return _pcall(body, name="ssm_params_bwd", out_shape=(vec, vec, _sds((1, 128), F32), mat, mat, mat, mat))(
        lr, li, ldt, br, bi, dab, dw, dc)


SCAN_CHUNK = 512


def _scan_consts(ar, ai, k_ref, reverse):
    row = lax.broadcasted_iota(jnp.int32, (8, SSM_LANES), 0)
    pw = [(ar, ai)]
    for _ in range(7):
        pr, pi = pw[-1]
        pw.append((pr * ar - pi * ai, pr * ai + pi * ar))
    for n, k in enumerate((1, 2, 4)):
        keep = (row < 8 - k) if reverse else (row >= k)
        k_ref[2 * n] = jnp.where(keep, jnp.broadcast_to(pw[k - 1][0], (8, SSM_LANES)), 0.0)
        k_ref[2 * n + 1] = jnp.where(keep, jnp.broadcast_to(pw[k - 1][1], (8, SSM_LANES)), 0.0)
    cr = jnp.zeros((8, SSM_LANES), F32)
    ci = jnp.zeros((8, SSM_LANES), F32)
    for r in range(8):
        e = (8 - r) if reverse else (r + 1)
        cr = jnp.where(row == r, jnp.broadcast_to(pw[e - 1][0], (8, SSM_LANES)), cr)
        ci = jnp.where(row == r, jnp.broadcast_to(pw[e - 1][1], (8, SSM_LANES)), ci)
    k_ref[6] = cr
    k_ref[7] = ci


def _scan_tile(xr, xi, k_ref, car, cai, reverse):
    for n, k in enumerate((1, 2, 4)):
        sh = (8 - k) if reverse else k
        sr = pltpu.roll(xr, sh, 0)
        si = pltpu.roll(xi, sh, 0)
        mr, mi = k_ref[2 * n], k_ref[2 * n + 1]
        xr, xi = xr + mr * sr - mi * si, xi + mr * si + mi * sr
    pr, pi = k_ref[6], k_ref[7]
    xr, xi = xr + pr * car - pi * cai, xi + pr * cai + pi * car
    return xr, xi


def _scan_fwd(bu3, abar):
    B, S, _ = bu3.shape
    ch = min(S, SCAN_CHUNK)
    blk = pl.BlockSpec((1, ch, 2 * SSM_LANES), lambda b, c: (b, c, 0))

    def body(ab_ref, bu_ref, x_ref, k_ref, carry_ref):
        _scan_consts(ab_ref[0:1, :], ab_ref[1:2, :], k_ref, False)

        @pl.when(pl.program_id(1) == 0)
        def _():
            carry_ref[...] = jnp.zeros_like(carry_ref)

        def step(i, carry):
            base = pl.multiple_of(i * 8, 8)
            xr = bu_ref[0, pl.ds(base, 8), 0:SSM_LANES]
            xi = bu_ref[0, pl.ds(base, 8), SSM_LANES:]
            xr, xi = _scan_tile(xr, xi, k_ref, carry[0], carry[1], False)
            x_ref[0, pl.ds(base, 8), 0:SSM_LANES] = xr
            x_ref[0, pl.ds(base, 8), SSM_LANES:] = xi
            return (jnp.broadcast_to(xr[7:8], (8, SSM_LANES)), jnp.broadcast_to(xi[7:8], (8, SSM_LANES)))

        cr, ci = lax.fori_loop(0, ch // 8, step, (carry_ref[0], carry_ref[1]))
        carry_ref[0] = cr
        carry_ref[1] = ci

    return _pcall(body, name="scan_fwd", out_shape=_sds(bu3.shape, F32), grid=(B, S // ch),
                  in_specs=[pl.BlockSpec((2, SSM_LANES), lambda b, c: (0, 0)), blk], out_specs=blk,
                  scratch_shapes=[pltpu.VMEM((8, 8, SSM_LANES), F32), pltpu.VMEM((2, 8, SSM_LANES), F32)],
                  dims=("arbitrary", "arbitrary"))(abar, bu3)


def _scan_bwd(dx3, xs3, abar):
    B, S, _ = dx3.shape
    ch = min(S, SCAN_CHUNK)
    nc = S // ch
    blk = pl.BlockSpec((1, ch, 2 * SSM_LANES), lambda b, c: (b, nc - 1 - c, 0))

    def body(ab_ref, dx_ref, xs_ref, g_ref, da_ref, k_ref, carry_ref, acc_ref):
        b, c = pl.program_id(0), pl.program_id(1)
        _scan_consts(ab_ref[0:1, :], -ab_ref[1:2, :], k_ref, True)
        row = lax.broadcasted_iota(jnp.int32, (8, SSM_LANES), 0)

        @pl.when(c == 0)
        def _():
            carry_ref[...] = jnp.zeros_like(carry_ref)

        @pl.when((c == 0) & (b == 0))
        def _():
            acc_ref[...] = jnp.zeros_like(acc_ref)

        def step(i, carry):
            car, cai, ar_acc, ai_acc = carry
            base = pl.multiple_of((ch // 8 - 1 - i) * 8, 8)
            gr = dx_ref[0, pl.ds(base, 8), 0:SSM_LANES]
            gi = dx_ref[0, pl.ds(base, 8), SSM_LANES:]
            gr, gi = _scan_tile(gr, gi, k_ref, car, cai, True)
            g_ref[0, pl.ds(base, 8), 0:SSM_LANES] = gr
            g_ref[0, pl.ds(base, 8), SSM_LANES:] = gi
            nr = jnp.where(row == 7, car, pltpu.roll(gr, 7, 0))
            ni = jnp.where(row == 7, cai, pltpu.roll(gi, 7, 0))
            xr = xs_ref[0, pl.ds(base, 8), 0:SSM_LANES]
            xi = xs_ref[0, pl.ds(base, 8), SSM_LANES:]
            ar_acc = ar_acc + nr * xr + ni * xi
            ai_acc = ai_acc + ni * xr - nr * xi
            return (jnp.broadcast_to(gr[0:1], (8, SSM_LANES)), jnp.broadcast_to(gi[0:1], (8, SSM_LANES)), ar_acc, ai_acc)

        cr, ci, ar_acc, ai_acc = lax.fori_loop(0, ch // 8, step, (carry_ref[0], carry_ref[1], acc_ref[0], acc_ref[1]))
        carry_ref[0] = cr
        carry_ref[1] = ci
        acc_ref[0] = ar_acc
        acc_ref[1] = ai_acc
        da_ref[0:1, :] = jnp.sum(ar_acc, axis=0, keepdims=True)
        da_ref[1:2, :] = jnp.sum(ai_acc, axis=0, keepdims=True)

    return _pcall(body, name="scan_bwd", out_shape=(_sds(dx3.shape, F32), _sds((2, SSM_LANES), F32)), grid=(B, nc),
                  in_specs=[pl.BlockSpec((2, SSM_LANES), lambda b, c: (0, 0)), blk, blk],
                  out_specs=(blk, pl.BlockSpec((2, SSM_LANES), lambda b, c: (0, 0))),
                  scratch_shapes=[pltpu.VMEM((8, 8, SSM_LANES), F32), pltpu.VMEM((2, 8, SSM_LANES), F32),
                                  pltpu.VMEM((2, 8, SSM_LANES), F32)],
                  dims=("arbitrary", "arbitrary"))(abar, dx3, xs3)


GELU_K = math.sqrt(2.0 / math.pi)
GELU_C = 0.044715


def _gelu_parts(y):
    t = jnp.tanh(GELU_K * (y + GELU_C * y * y * y))
    return 0.5 * y * (1.0 + t), t


def _ssm_post(yc, us, dsk, wglu, bglu):
    T, N = yc.shape
    tm = min(T, 1024)
    row = pl.BlockSpec((tm, N), lambda i: (i, 0))
    vec = pl.BlockSpec((1, N), lambda i: (0, 0))
    mat = pl.BlockSpec((N, N), lambda i: (0, 0))

    def body(yc_ref, us_ref, d_ref, w_ref, b_ref, y_ref, s_ref):
        y = yc_ref[...] + d_ref[...] * us_ref[...]
        y_ref[...] = y
        z, _ = _gelu_parts(y)
        gl = jnp.dot(z.astype(BF16), w_ref[...], preferred_element_type=F32) + b_ref[...]
        s_ref[...] = (z * _sig(gl)).astype(BF16)

    return _pcall(body, name="ssm_post", out_shape=(_sds((T, N), F32), _sds((T, N), BF16)), grid=(T // tm,),
                  in_specs=[row, row, vec, mat, vec], out_specs=(row, row), dims=("parallel",))(yc, us, dsk, wglu, bglu)


def _ssm_post_bwd(y5, us, ds, dsk, wglu, bglu):
    T, N = y5.shape
    tm = min(T, 1024)
    row = pl.BlockSpec((tm, N), lambda i: (i, 0))
    vec = pl.BlockSpec((1, N), lambda i: (0, 0))
    mat = pl.BlockSpec((N, N), lambda i: (0, 0))

    def body(y_ref, us_ref, ds_ref, d_ref, w_ref, b_ref, dy_ref, dd_ref, db_ref, dw_ref):
        @pl.when(pl.program_id(0) == 0)
        def _():
            dd_ref[...] = jnp.zeros_like(dd_ref)
            db_ref[...] = jnp.zeros_like(db_ref)
            dw_ref[...] = jnp.zeros_like(dw_ref)

        y = y_ref[...]
        z, t = _gelu_parts(y)
        zb = z.astype(BF16)
        gl = jnp.dot(zb, w_ref[...], preferred_element_type=F32) + b_ref[...]
        sg = _sig(gl)
        ds = ds_ref[...]
        dgl = ds * z * sg * (1.0 - sg)
        dglb = dgl.astype(BF16)
        dz = ds * sg + lax.dot_general(dglb, w_ref[...], (((1,), (1,)), ((), ())), preferred_element_type=F32)
        dgelu = 0.5 * (1.0 + t) + 0.5 * y * (1.0 - t * t) * GELU_K * (1.0 + 3.0 * GELU_C * y * y)
        dy = dz * dgelu
        dy_ref[...] = dy
        dd_ref[...] += jnp.sum(dy * us_ref[...], axis=0, keepdims=True)
        db_ref[...] += jnp.sum(dgl, axis=0, keepdims=True)
        dw_ref[...] += lax.dot_general(zb, dglb, (((0,), (0,)), ((), ())), preferred_element_type=F32)

    return _pcall(body, name="ssm_post_bwd",
                  out_shape=(_sds((T, N), F32), _sds((1, N), F32), _sds((1, N), F32), _sds((N, N), F32)),
                  grid=(T // tm,), in_specs=[row, row, row, vec, mat, vec], out_specs=(row, vec, vec, mat),
                  dims=("arbitrary",))(y5, us, ds, dsk, wglu, bglu)


def _add_scaled_cast(a, b, s):
    T, N = a.shape
    tm = min(T, 1024)
    row = pl.BlockSpec((tm, N), lambda i: (i, 0))

    def body(a_ref, b_ref, s_ref, o_ref):
        o_ref[...] = (a_ref[...] + s_ref[...] * b_ref[...]).astype(BF16)

    return _pcall(body, name="add_scaled_cast", out_shape=_sds((T, N), BF16), grid=(T // tm,),
                  in_specs=[row, row, pl.BlockSpec((1, N), lambda i: (0, 0))], out_specs=row, dims=("parallel",))(a, b, s)


GATE_TILE = 256
GATE_ATT_BLOCK0 = (3 * ATT_WIDTH + SSM_WIDTH) // GATE_TILE
GATE_SSM_BLOCK0 = (3 * ATT_WIDTH + SSM_WIDTH + D_MODEL) // GATE_TILE


def _merge(proj, y_att, y_ssm, b_gate):
    T = proj.shape[0]
    tm = min(T, 1024)
    nj = D_MODEL // GATE_TILE
    ga = pl.BlockSpec((tm, GATE_TILE), lambda i, j: (i, GATE_ATT_BLOCK0 + j))
    gs = pl.BlockSpec((tm, GATE_TILE), lambda i, j: (i, GATE_SSM_BLOCK0 + j))
    yy = pl.BlockSpec((tm, GATE_TILE), lambda i, j: (i, j))
    ba = pl.BlockSpec((1, GATE_TILE), lambda i, j: (0, j))
    bs = pl.BlockSpec((1, GATE_TILE), lambda i, j: (0, nj + j))

    def body(ga_ref, gs_ref, ya_ref, ys_ref, ba_ref, bs_ref, o_ref):
        o_ref[...] = (_sig(ga_ref[...] + ba_ref[...]) * ya_ref[...]
                      + _sig(gs_ref[...] + bs_ref[...]) * ys_ref[...]).astype(BF16)

    return _pcall(body, name="merge", out_shape=_sds((T, D_MODEL), BF16), grid=(T // tm, nj),
                  in_specs=[ga, gs, yy, yy, ba, bs], out_specs=yy, dims=("parallel", "parallel"))(
        proj, proj, y_att, y_ssm, b_gate, b_gate)


def _merge_bwd(proj, y_att, y_ssm, b_gate, dmerged):
    T = proj.shape[0]
    tm = min(T, 1024)
    nj = D_MODEL // GATE_TILE
    ga = pl.BlockSpec((tm, GATE_TILE), lambda j, i: (i, GATE_ATT_BLOCK0 + j))
    gs = pl.BlockSpec((tm, GATE_TILE), lambda j, i: (i, GATE_SSM_BLOCK0 + j))
    yy = pl.BlockSpec((tm, GATE_TILE), lambda j, i: (i, j))
    ba = pl.BlockSpec((1, GATE_TILE), lambda j, i: (0, j))
    bs = pl.BlockSpec((1, GATE_TILE), lambda j, i: (0, nj + j))

    def body(ga_ref, gs_ref, ya_ref, ys_ref, ba_ref, bs_ref, dm_ref, dya_ref, dys_ref, dga_ref, dgs_ref, dba_ref, dbs_ref):
        @pl.when(pl.program_id(1) == 0)
        def _():
            dba_ref[...] = jnp.zeros_like(dba_ref)
            dbs_ref[...] = jnp.zeros_like(dbs_ref)

        dm = dm_ref[...]
        sa = _sig(ga_ref[...] + ba_ref[...])
        ss = _sig(gs_ref[...] + bs_ref[...])
        dya_ref[...] = (dm * sa).astype(BF16)
        dys_ref[...] = (dm * ss).astype(BF16)
        dga = dm * ya_ref[...] * sa * (1.0 - sa)
        dgs = dm * ys_ref[...] * ss * (1.0 - ss)
        dga_ref[...] = dga.astype(BF16)
        dgs_ref[...] = dgs.astype(BF16)
        dba_ref[...] += jnp.sum(dga, axis=0, keepdims=True)
        dbs_ref[...] += jnp.sum(dgs, axis=0, keepdims=True)

    big = _sds((T, D_MODEL), BF16)
    vec = _sds((1, D_MODEL), F32)
    return _pcall(body, name="merge_bwd", out_shape=(big, big, big, big, vec, vec), grid=(nj, T // tm),
                  in_specs=[ga, gs, yy, yy, ba, bs, yy], out_specs=(yy, yy, yy, yy, ba, ba),
                  dims=("arbitrary", "arbitrary"))(proj, proj, y_att, y_ssm, b_gate, b_gate, dmerged)


CONV_TILE = 256


def _conv_pre(a, w_ref, b_ref, row):
    conv = b_ref[...] + w_ref[0:1, :] * a
    shifted = []
    for j in (1, 2):
        sh = jnp.where(row >= j, pltpu.roll(a, j, 0), 0.0)
        shifted.append(sh)
        conv = conv + w_ref[j:j + 1, :] * sh
    return conv, shifted


def _conv_act(up3, w_conv, b_conv):
    B, S, _ = up3.shape
    nj = D_FF // CONV_TILE
    a_spec = pl.BlockSpec((1, S, CONV_TILE), lambda b, j: (b, 0, j))
    v_spec = pl.BlockSpec((1, S, CONV_TILE), lambda b, j: (b, 0, nj + j))
    w_spec = pl.BlockSpec((3, CONV_TILE), lambda b, j: (0, j))
    b_spec = pl.BlockSpec((1, CONV_TILE), lambda b, j: (0, j))

    def body(a_ref, v_ref, w_ref, b_ref, o_ref):
        a = a_ref[0].astype(F32)
        row = lax.broadcasted_iota(jnp.int32, a.shape, 0)
        conv, _ = _conv_pre(a, w_ref, b_ref, row)
        o_ref[0] = (conv * _sig(conv) * v_ref[0]).astype(BF16)

    return _pcall(body, name="conv_act", out_shape=_sds((B, S, D_FF), BF16), grid=(B, nj),
                  in_specs=[a_spec, v_spec, w_spec, b_spec], out_specs=a_spec, dims=("parallel", "parallel"))(
        up3, up3, w_conv, b_conv)


def _conv_bwd(up3, dact3, w_conv, b_conv):
    B, S, _ = up3.shape
    nj = D_FF // CONV_TILE
    a_spec = pl.BlockSpec((1, S, CONV_TILE), lambda j, b: (b, 0, j))
    v_spec = pl.BlockSpec((1, S, CONV_TILE), lambda j, b: (b, 0, nj + j))
    w_spec = pl.BlockSpec((3, CONV_TILE), lambda j, b: (0, j))
    b_spec = pl.BlockSpec((1, CONV_TILE), lambda j, b: (0, j))

    def body(a_ref, v_ref, d_ref, w_ref, b_ref, da_ref, dv_ref, dw_ref, db_ref):
        @pl.when(pl.program_id(1) == 0)
        def _():
            dw_ref[...] = jnp.zeros_like(dw_ref)
            db_ref[...] = jnp.zeros_like(db_ref)

        a = a_ref[0].astype(F32)
        d = d_ref[0]
        row = lax.broadcasted_iota(jnp.int32, a.shape, 0)
        conv, shifted = _conv_pre(a, w_ref, b_ref, row)
        sg = _sig(conv)
        dv_ref[0] = (d * conv * sg).astype(BF16)
        dconv = d * v_ref[0] * (sg * (1.0 + conv * (1.0 - sg)))
        da = w_ref[0:1, :] * dconv
        for j in (1, 2):
            da = da + w_ref[j:j + 1, :] * jnp.where(row < S - j, pltpu.roll(dconv, S - j, 0), 0.0)
        da_ref[0] = da.astype(BF16)
        db_ref[...] += jnp.sum(dconv, axis=0, keepdims=True)
        dw_ref[0:1, :] += jnp.sum(dconv * a, axis=0, keepdims=True)
        dw_ref[1:2, :] += jnp.sum(dconv * shifted[0], axis=0, keepdims=True)
        dw_ref[2:3, :] += jnp.sum(dconv * shifted[1], axis=0, keepdims=True)

    big = _sds((B, S, D_FF), BF16)
    return _pcall(body, name="conv_bwd", out_shape=(big, big, _sds((3, D_FF), F32), _sds((1, D_FF), F32)),
                  grid=(nj, B), in_specs=[a_spec, v_spec, a_spec, w_spec, b_spec],
                  out_specs=(a_spec, a_spec, w_spec, b_spec), dims=("arbitrary", "arbitrary"))(
        up3, up3, dact3, w_conv, b_conv)


def _rows_tile(r):
    for t in (512, 256, 128, 64, 40, 32, 16, 8):
        if r % t == 0:
            return t
    return r


def _add2(a, b, out_dtype):
    R, N = a.shape
    tr = _rows_tile(R)
    spec = pl.BlockSpec((tr, N), lambda i: (i, 0))

    def body(a_ref, b_ref, o_ref):
        o_ref[...] = (a_ref[...] + b_ref[...]).astype(out_dtype)

    return _pcall(body, name="add2", out_shape=_sds((R, N), out_dtype), grid=(R // tr,), in_specs=[spec, spec],
                  out_specs=spec, dims=("parallel",))(a, b)


def _sum_slots(q, name):
    n, R, N = q.shape
    tr = _rows_tile(R)

    def body(q_ref, o_ref):
        acc = q_ref[0].astype(F32)
        for s in range(1, n):
            acc = acc + q_ref[s].astype(F32)
        o_ref[...] = acc

    return _pcall(body, name=name, out_shape=_sds((R, N), F32), grid=(R // tr,),
                  in_specs=[pl.BlockSpec((n, tr, N), lambda i: (0, i, 0))], out_specs=pl.BlockSpec((tr, N), lambda i: (i, 0)),
                  dims=("parallel",))(q)


def _adamw(w, g, m, v, name):
    R, N = w.shape
    tr = _rows_tile(R) if R * N * 4 > (1 << 20) else R
    tr = min(tr, 256) if R % 256 == 0 and R > 256 else tr
    spec = pl.BlockSpec((tr, N), lambda i: (i, 0))
    bc1 = 1.0 - ADAM_B1 ** ADAM_STEP
    bc2 = 1.0 - ADAM_B2 ** ADAM_STEP

    def body(w_ref, g_ref, m_ref, v_ref, d_ref, nm_ref, nv_ref):
        g = g_ref[...]
        m = ADAM_B1 * m_ref[...] + (1.0 - ADAM_B1) * g
        v = ADAM_B2 * v_ref[...] + (1.0 - ADAM_B2) * (g * g)
        nm_ref[...] = m
        nv_ref[...] = v
        d_ref[...] = -ADAM_LR * ((m / bc1) / (jnp.sqrt(v / bc2) + ADAM_EPS) + ADAM_WD * w_ref[...])

    shp = _sds((R, N), F32)
    return _pcall(body, name=name, out_shape=(shp, shp, shp), grid=(R // tr,), in_specs=[spec] * 4,
                  out_specs=(spec, spec, spec), dims=("parallel",))(w, g, m, v)


_GROUP_MASKS = {
    "all": [(dx, dy, dc) for dx in (0, 1) for dy in (0, 1) for dc in (0, 1) if (dx, dy, dc) != (0, 0, 0)],
    "xy": [(1, 0, 0), (0, 1, 0), (1, 1, 0)],
    "c": [(0, 0, 1)],
}
_GROUP_SLOTS = {"all": 8, "xy": 4, "c": 2}


def _group_slot(group, x, y, c):
    return {"all": 4 * x + 2 * y + c, "xy": 2 * x + y, "c": c}[group]


def _flip(v, d):
    return 1 - v if d else v


def _exchange(arr, group, mode, name):
    masks = _GROUP_MASKS[group]
    n = len(masks)
    if mode == "gather":
        out_shape = (_GROUP_SLOTS[group],) + arr.shape
    elif mode == "scatter":
        assert arr.shape[0] == _GROUP_SLOTS[group]
        out_shape = arr.shape
    elif mode == "swap":
        assert group == "c"
        out_shape = arr.shape
    else:
        assert group == "c"
        half = arr.shape[1] // 2
        out_shape = (arr.shape[0], half, arr.shape[2])

    def body(x_ref, o_ref, send_sems, recv_sems, local_sem):
        x, y, c = lax.axis_index("x"), lax.axis_index("y"), lax.axis_index("c")
        me = _group_slot(group, x, y, c)
        local = None
        if mode == "gather":
            local = pltpu.make_async_copy(x_ref, o_ref.at[me], local_sem)
        elif mode == "scatter":
            local = pltpu.make_async_copy(x_ref.at[me], o_ref.at[me], local_sem)
        if local is not None:
            local.start()
        copies = []
        for k, (dx, dy, dc) in enumerate(masks):
            px, py, pc = _flip(x, dx), _flip(y, dy), _flip(c, dc)
            if mode == "gather":
                src, dst = x_ref, o_ref.at[me]
            elif mode == "scatter":
                src, dst = x_ref.at[_group_slot(group, px, py, pc)], o_ref.at[me]
            elif mode == "swap":
                src, dst = x_ref, o_ref
            else:
                src, dst = x_ref.at[:, pl.ds(pl.multiple_of(pc * half, 8), half), :], o_ref
            cp = pltpu.make_async_remote_copy(src_ref=src, dst_ref=dst, send_sem=send_sems.at[k], recv_sem=recv_sems.at[k],
                                              device_id=(px, py, pc), device_id_type=pl.DeviceIdType.MESH)
            cp.start()
            copies.append(cp)
        for cp in copies:
            cp.wait()
        if local is not None:
            local.wait()

    anyspec = pl.BlockSpec(memory_space=pl.ANY)
    return pl.pallas_call(body, name=name, out_shape=_sds(out_shape, arr.dtype), in_specs=[anyspec], out_specs=anyspec,
                          scratch_shapes=[pltpu.SemaphoreType.DMA((n,)), pltpu.SemaphoreType.DMA((n,)),
                                          pltpu.SemaphoreType.DMA(())])(arr)


BIG = (("w_in", (D_MODEL, IN_WIDTH), 1), ("w_out", (D_MODEL, D_MODEL), 0), ("w_up", (D_MODEL, 2 * D_FF), 1),
       ("w_down", (D_FF, D_MODEL), 0), ("w_proj_att", (ATT_WIDTH, D_MODEL), 1), ("w_proj_ssm", (SSM_WIDTH, D_MODEL), 1),
       ("w_glu", (SSM_WIDTH, SSM_WIDTH), 0))
N_XY = 4


def _big_rows(shape):
    return shape[0] * shape[1] // N_XY // LANES


FLAT_ROWS = sum(_big_rows(s) for _, s, _ in BIG)


def _shard_shape(shape, axis):
    return (shape[0] // N_XY, shape[1]) if axis == 0 else (shape[0], shape[1] // N_XY)


def _flatten_shards(shards):
    return jnp.concatenate([shards[n].reshape(_big_rows(s), LANES) for n, s, _ in BIG], axis=0)


def _unflatten_shard(flat):
    out, r = {}, 0
    for n, s, ax in BIG:
        k = _big_rows(s)
        out[n] = flat[r:r + k].reshape(_shard_shape(s, ax))
        r += k
    return out


def _unflatten_full(flat4):
    out, r = {}, 0
    for n, s, ax in BIG:
        k = _big_rows(s)
        sh = _shard_shape(s, ax)
        t = flat4[:, r:r + k].reshape((N_XY,) + sh)
        out[n] = t.reshape(s) if ax == 0 else t.transpose(1, 0, 2).reshape(s)
        r += k
    return out


def _flatten_full(full):
    parts = []
    for n, s, ax in BIG:
        sh = _shard_shape(s, ax)
        t = full[n]
        t = t.reshape((N_XY,) + sh) if ax == 0 else t.reshape(s[0], N_XY, sh[1]).transpose(1, 0, 2)
        parts.append(t.reshape(N_XY, _big_rows(s), LANES))
    return jnp.concatenate(parts, axis=1)


def _pack_rows(arrs):
    rows, counts = [], []
    for a in arrs:
        f = a.reshape(-1)
        k = -(-f.shape[0] // LANES)
        rows.append(jnp.pad(f, (0, k * LANES - f.shape[0])).reshape(k, LANES))
        counts.append(k)
    return jnp.concatenate(rows, axis=0), counts


def _unpack_rows(buf, shapes):
    out, r = [], 0
    for s in shapes:
        size = int(np.prod(s))
        k = -(-size // LANES)
        out.append(buf[r:r + k].reshape(-1)[:size].reshape(s))
        r += k
    return out


def _lanes_from_groups(a):
    return a.transpose(2, 0, 1).reshape(SSM_GROUP_CH, SSM_LANES)


def _groups_from_lanes(a):
    return a.reshape(SSM_GROUP_CH, SSM_GROUPS, SSM_STATE).transpose(1, 2, 0)


def _local_step(x3, mod, tgt3, W, P):
    B, S, _ = x3.shape
    T = B * S
    seq_blocks = S // ATT_BLOCK
    sh1, sc1, gt1, sh2, sc2, gt2 = [m.reshape(B, 1, D_MODEL) for m in jnp.split(mod, 6, axis=-1)]
    g_mix, g_ffn, g_final = P["g_mix"].reshape(1, D_MODEL), P["g_ffn"].reshape(1, D_MODEL), P["g_final"].reshape(1, D_MODEL)
    b_gate = P["b_gate"].reshape(1, 2 * D_MODEL)
    d_skip, b_glu = P["d_skip"].reshape(1, SSM_WIDTH), P["b_glu"].reshape(1, SSM_WIDTH)
    w_conv, b_conv = P["w_conv"], P["b_conv"].reshape(1, D_FF)

    u1 = _norm_mod(x3, g_mix, sc1, sh1).reshape(T, D_MODEL)
    proj = _mm(u1, W["w_in"], name="mm_proj", out_dtype=BF16)
    proj3 = proj.reshape(B, S, IN_WIDTH)
    us = proj[:, 3 * ATT_WIDTH:3 * ATT_WIDTH + SSM_WIDTH]
    o_att3, lse4 = _attention_fwd(proj3, seq_blocks)
    o_att = o_att3.reshape(T, ATT_WIDTH)
    y_att = _mm(o_att, W["w_proj_att"], name="mm_proj_att")

    lr = P["a_re"].reshape(1, SSM_LANES)
    li = P["a_im"].reshape(1, SSM_LANES)
    ldt = jnp.repeat(P["log_dt"], SSM_STATE).reshape(1, SSM_LANES)
    br, bi = _lanes_from_groups(P["b_re"]), _lanes_from_groups(P["b_im"])
    cr = P["c_re"].transpose(1, 0, 2).reshape(SSM_GROUP_CH, SSM_LANES)
    ci = P["c_im"].transpose(1, 0, 2).reshape(SSM_GROUP_CH, SSM_LANES)
    abar, w_bu, w_c = _ssm_params(lr, li, ldt, br, bi, cr, ci)
    bu = _mm(us, w_bu, name="mm_bu")
    xs = _scan_fwd(bu.reshape(B, S, 2 * SSM_LANES), abar).reshape(T, 2 * SSM_LANES)
    y_core = _mm(xs, w_c, tb=True, name="mm_ssm_out")
    y5, s_out = _ssm_post(y_core, us, d_skip, W["w_glu"], b_glu)
    y_ssm = _mm(s_out, W["w_proj_ssm"], name="mm_proj_ssm")

    merged = _merge(proj, y_att, y_ssm, b_gate)
    mix = _mm(merged, W["w_out"], name="mm_out")
    mix3 = mix.reshape(B, S, D_MODEL)

    h1, u2 = _resid_norm_mod(x3, mix3, gt1, g_ffn, sc2, sh2)
    u2 = u2.reshape(T, D_MODEL)
    up3 = _mm(u2, W["w_up"], name="mm_up", out_dtype=BF16).reshape(B, S, 2 * D_FF)
    act = _conv_act(up3, w_conv, b_conv).reshape(T, D_FF)
    ffn3 = _mm(act, W["w_down"], name="mm_down").reshape(B, S, D_MODEL)
    dh2, dffn, dgt2, dg_final, loss = _final_loss(h1, ffn3, tgt3, gt2, g_final)

    dffn = dffn.reshape(T, D_MODEL)
    gw = {}
    gw["w_down"] = _mm(act, dffn, ta=True, name="mm_dw_down")
    dact3 = _mm(dffn, W["w_down"], tb=True, name="mm_dact").reshape(B, S, D_FF)
    da3, dval3, dw_conv, db_conv = _conv_bwd(up3, dact3, w_conv, b_conv)
    dup = jnp.concatenate([da3.reshape(T, D_FF), dval3.reshape(T, D_FF)], axis=1)
    gw["w_up"] = _mm(u2, dup, ta=True, name="mm_dw_up")
    du2 = _mm(dup, W["w_up"], tb=True, name="mm_du2").reshape(B, S, D_MODEL)
    dh1, dsh2, dsc2, dg_ffn, dgt1, dmix = _norm_bwd(h1, du2, dh2, g_ffn, sc2, "norm_bwd2", mix3=mix3, gt=gt1)

    dmix = dmix.reshape(T, D_MODEL)
    gw["w_out"] = _mm(merged, dmix, ta=True, name="mm_dw_out")
    dmerged = _mm(dmix, W["w_out"], tb=True, name="mm_dmerged")
    dy_att, dy_ssm, dga, dgs, db_att, db_ssm = _merge_bwd(proj, y_att, y_ssm, b_gate, dmerged)

    gw["w_proj_ssm"] = _mm(s_out, dy_ssm, ta=True, name="mm_dw_proj_ssm")
    ds_out = _mm(dy_ssm, W["w_proj_ssm"], tb=True, name="mm_ds_out")
    dy5, dd_skip, db_glu, dw_glu = _ssm_post_bwd(y5, us, ds_out, d_skip, W["w_glu"], b_glu)
    gw["w_glu"] = dw_glu
    dxs = _mm(dy5, w_c, name="mm_dxs")
    dwc = _mm(dy5, xs, ta=True, name="mm_dwc")
    g3, dab = _scan_bwd(dxs.reshape(B, S, 2 * SSM_LANES), xs.reshape(B, S, 2 * SSM_LANES), abar)
    gs2 = g3.reshape(T, 2 * SSM_LANES)
    dwbu = _mm(us, gs2, ta=True, name="mm_dwbu")
    dus_core = _mm(gs2, w_bu, tb=True, name="mm_dus")
    dus = _add_scaled_cast(dus_core, dy5, d_skip)
    dlr, dli, dldt, dbr, dbi, dcr, dci = _ssm_params_bwd(lr, li, ldt, br, bi, dab, dwbu, dwc)

    gw["w_proj_att"] = _mm(o_att, dy_att, ta=True, name="mm_dw_proj_att")
    do_att = _mm(dy_att, W["w_proj_att"], tb=True, out_dtype=BF16, name="mm_do_att")
    dq3, dk3, dv3 = _attention_bwd(proj3, do_att.reshape(B, S, ATT_WIDTH), o_att3, lse4, seq_blocks)
    dproj = jnp.concatenate([t.reshape(T, ATT_WIDTH) for t in (dq3, dk3, dv3)] + [dus, dga, dgs], axis=1)
    gw["w_in"] = _mm(u1, dproj, ta=True, name="mm_dw_in")
    du1 = _mm(dproj, W["w_in"], tb=True, name="mm_du1").reshape(B, S, D_MODEL)
    dx, dsh1, dsc1, dg_mix = _norm_bwd(x3, du1, dh1, g_mix, sc1, "norm_bwd1")

    dmod = jnp.concatenate([t.reshape(B, D_MODEL) for t in (dsh1, dsc1, dgt1, dsh2, dsc2, dgt2)], axis=1)
    gs = dict(
        g_mix=dg_mix.reshape(D_MODEL), b_gate=jnp.concatenate([db_att, db_ssm], axis=1).reshape(2 * D_MODEL),
        a_re=dlr.reshape(SSM_GROUPS, SSM_STATE), a_im=dli.reshape(SSM_GROUPS, SSM_STATE), log_dt=dldt[0, :SSM_GROUPS],
        b_re=_groups_from_lanes(dbr), b_im=_groups_from_lanes(dbi),
        c_re=dcr.reshape(SSM_GROUP_CH, SSM_GROUPS, SSM_STATE).transpose(1, 0, 2),
        c_im=dci.reshape(SSM_GROUP_CH, SSM_GROUPS, SSM_STATE).transpose(1, 0, 2),
        d_skip=dd_skip.reshape(SSM_WIDTH), b_glu=db_glu.reshape(SSM_WIDTH), g_ffn=dg_ffn.reshape(D_MODEL),
        w_conv=dw_conv, b_conv=db_conv.reshape(D_FF), g_final=dg_final.reshape(D_MODEL))
    return loss, dx, dmod, gw, gs


WEIGHTS = ['w_ada', 'b_ada', 'g_mix', 'w_in', 'b_gate', 'a_re', 'a_im', 'log_dt', 'b_re', 'b_im', 'c_re', 'c_im', 'd_skip',
           'w_glu', 'b_glu', 'w_proj_att', 'w_proj_ssm', 'w_out', 'g_ffn', 'w_up', 'w_conv', 'b_conv', 'w_down', 'g_final']
SMALL = ['g_mix', 'b_gate', 'a_re', 'a_im', 'log_dt', 'b_re', 'b_im', 'c_re', 'c_im', 'd_skip', 'b_glu', 'g_ffn', 'w_conv',
         'b_conv', 'g_final']


def kernel(x, c, w_ada, b_ada, g_mix, w_in, b_gate, a_re, a_im, log_dt, b_re, b_im, c_re, c_im, d_skip, w_glu, b_glu, w_proj_att, w_proj_ssm, w_out, g_ffn, w_up, w_conv, b_conv, w_down, g_final, loss_target, m_w_ada, m_b_ada, m_g_mix, m_w_in, m_b_gate, m_a_re, m_a_im, m_log_dt, m_b_re, m_b_im, m_c_re, m_c_im, m_d_skip, m_w_glu, m_b_glu, m_w_proj_att, m_w_proj_ssm, m_w_out, m_g_ffn, m_w_up, m_w_conv, m_b_conv, m_w_down, m_g_final, v_w_ada, v_b_ada, v_g_mix, v_w_in, v_b_gate, v_a_re, v_a_im, v_log_dt, v_b_re, v_b_im, v_c_re, v_c_im, v_d_skip, v_w_glu, v_b_glu, v_w_proj_att, v_w_proj_ssm, v_w_out, v_g_ffn, v_w_up, v_w_conv, v_b_conv, v_w_down, v_g_final):
    args = dict(locals())
    w = {n: args[n] for n in WEIGHTS}
    m = {n: args["m_" + n] for n in WEIGHTS}
    v = {n: args["v_" + n] for n in WEIGHTS}
    B, S, _ = x.shape
    ix, iy, ic = lax.axis_index("x"), lax.axis_index("y"), lax.axis_index("c")
    chip = 2 * ix + iy
    half = FLAT_ROWS // 2
    ada_cols = w_ada.shape[2]

    c_all = _exchange(c, "all", "gather", "gather_c").reshape(8 * B, D_MODEL)
    b_cols = lax.dynamic_slice_in_dim(b_ada, chip * ada_cols, ada_cols, axis=1)
    mod_cols = _ada_fwd(c_all, w_ada[0], b_cols)
    mod_all = _exchange(mod_cols, "xy", "gather", "gather_mod")
    mod_all = mod_all.transpose(1, 0, 2).reshape(8 * B, 6 * D_MODEL)
    mod = lax.dynamic_slice_in_dim(mod_all, (4 * ix + 2 * iy + ic) * B, B, axis=0)

    flat = _flatten_shards({n: w[n][0] for n, _, _ in BIG}).astype(BF16)
    mine = lax.dynamic_slice_in_dim(flat, ic * half, half, axis=0)
    halves = _exchange(mine, "xy", "gather", "gather_w_chips")
    others = _exchange(halves, "c", "swap", "gather_w_cores")
    south = ic == 0
    W = _unflatten_full(jnp.concatenate([jnp.where(south, halves, others), jnp.where(south, others, halves)], axis=1))

    wc_all = _exchange(w_conv[0], "xy", "gather", "gather_w_conv")
    P = {n: w[n][0] for n in SMALL if n not in ("w_conv", "g_final")}
    P["w_conv"] = wc_all.transpose(1, 0, 2).reshape(3, D_FF)
    P["g_final"] = g_final

    loss, dx, dmod, gw, gs = _local_step(x, mod, loss_target, W, P)

    loss = lax.psum(loss[0, 0], MESH_AXES)

    small_shapes = [gs[n].shape for n in SMALL]
    packed, counts = _pack_rows([gs[n] for n in SMALL] + [dmod])
    n_small = sum(counts[:-1])
    gathered = _exchange(packed, "all", "gather", "gather_small")
    small_sum = _sum_slots(gathered[:, :n_small], "sum_small")
    g_small = dict(zip(SMALL, _unpack_rows(small_sum, small_shapes)))
    dmod_all = gathered[:, n_small:].reshape(8, -1)[:, :B * 6 * D_MODEL].reshape(8 * B, 6 * D_MODEL)
    dmod_cols = lax.dynamic_slice_in_dim(dmod_all, chip * ada_cols, ada_cols, axis=1)
    g_w_ada, g_b_ada = _ada_bwd(c_all, dmod_all, dmod_cols)

    G = _flatten_full(gw)
    theirs = _exchange(G, "c", "half", "reduce_cores")
    ours = lax.dynamic_slice_in_dim(G, ic * half, half, axis=1)
    pair = _add2(ours.reshape(N_XY * half, LANES), theirs.reshape(N_XY * half, LANES), BF16).reshape(N_XY, half, LANES)
    parts = _exchange(pair, "xy", "scatter", "reduce_chips")
    red = _sum_slots(parts, "sum_chips")
    red_sib = _exchange(red, "c", "swap", "share_cores")
    g_flat = jnp.concatenate([jnp.where(south, red, red_sib), jnp.where(south, red_sib, red)], axis=0)
    g_big = _unflatten_shard(g_flat)

    grads = {"w_ada": g_w_ada[None], "b_ada": g_b_ada}
    for n, _, _ in BIG:
        grads[n] = g_big[n][None]
    wc_cols = w_conv.shape[2]
    for n in SMALL:
        g = g_small[n]
        if n == "w_conv":
            g = lax.dynamic_slice_in_dim(g, chip * wc_cols, wc_cols, axis=1)
        grads[n] = g.reshape(w[n].shape)

    delta, new_m, new_v = {}, {}, {}
    for n in ["w_ada"] + [b for b, _, _ in BIG]:
        shp = w[n].shape
        d2, m2, v2 = _adamw(w[n][0], grads[n][0], m[n][0], v[n][0], "adamw_" + n)
        delta[n], new_m[n], new_v[n] = d2.reshape(shp), m2.reshape(shp), v2.reshape(shp)
    rest = ["b_ada"] + SMALL
    shapes = [w[n].shape for n in rest]
    pw, _ = _pack_rows([w[n] for n in rest])
    pg, _ = _pack_rows([grads[n] for n in rest])
    pm, _ = _pack_rows([m[n] for n in rest])
    pv, _ = _pack_rows([v[n] for n in rest])
    d2, m2, v2 = _adamw(pw, pg, pm, pv, "adamw_small")
    for n, dd, mm, vv in zip(rest, _unpack_rows(d2, shapes), _unpack_rows(m2, shapes), _unpack_rows(v2, shapes)):
        delta[n], new_m[n], new_v[n] = dd, mm, vv

    return (loss, dx, *[grads[n] for n in WEIGHTS], *[delta[n] for n in WEIGHTS], *[new_m[n] for n in WEIGHTS],
            *[new_v[n] for n in WEIGHTS])
```

```python
import functools
import math

import numpy as np
import jax
import jax.numpy as jnp
from jax import lax
from jax.experimental import pallas as pl
from jax.experimental.pallas import tpu as pltpu

F32, BF16 = jnp.float32, jnp.bfloat16

D_MODEL = 1024
N_HEADS = 8
HEAD_DIM = 64
ATT_WIDTH = 512
SSM_GROUPS = 16
SSM_GROUP_CH = 16
SSM_WIDTH = 256
SSM_STATE = 64
SSM_LANES = SSM_GROUPS * SSM_STATE
D_FF = 2048
IN_WIDTH = 3 * ATT_WIDTH + SSM_WIDTH + 2 * D_MODEL
ATT_BLOCK = 128
N_PATTERNS = 3
EPS = 1e-6
NEG_INF = -1e30

ADAM_LR, ADAM_B1, ADAM_B2, ADAM_EPS, ADAM_WD, ADAM_STEP = 0.001, 0.9, 0.999, 1e-08, 0.01, 10

V7X_VMEM_LIMIT_BYTES = 56 * 1024 * 1024
LANES = 1024

MESH_AXES = ("x", "y", "c")


def _pcall(body, *, name, out_shape, grid=(), in_specs=None, out_specs=None, scratch_shapes=(), dims=None):
    params = dict(vmem_limit_bytes=V7X_VMEM_LIMIT_BYTES)
    if dims is not None:
        params["dimension_semantics"] = dims
    specs = {}
    if in_specs is not None:
        specs = dict(grid=grid, in_specs=in_specs, out_specs=out_specs)
    return pl.pallas_call(body, name=name, out_shape=out_shape, scratch_shapes=scratch_shapes,
                          compiler_params=pltpu.CompilerParams(**params), **specs)


def _sds(shape, dtype):
    return jax.ShapeDtypeStruct(tuple(shape), dtype)


def _tile(n, target):
    if n <= target:
        return n
    for t in range(target - target % 128, 0, -128):
        if n % t == 0:
            return t
    raise ValueError((n, target))


def _sig(v):
    return 1.0 / (1.0 + jnp.exp(-v))


def _mm(a, b, *, name, ta=False, tb=False, out_dtype=F32, tm=1024, tn=1024, tk=1024):
    if ta:
        K, M = a.shape
    else:
        M, K = a.shape
    if tb:
        N, K2 = b.shape
    else:
        K2, N = b.shape
    assert K == K2, (a.shape, b.shape)
    tm, tn, tk = _tile(M, tm), _tile(N, tn), _tile(K, tk)
    nk = K // tk
    a_spec = pl.BlockSpec((tk, tm), lambda i, j, k: (k, i)) if ta else pl.BlockSpec((tm, tk), lambda i, j, k: (i, k))
    b_spec = pl.BlockSpec((tn, tk), lambda i, j, k: (j, k)) if tb else pl.BlockSpec((tk, tn), lambda i, j, k: (k, j))
    dn = (((0 if ta else 1,), (1 if tb else 0,)), ((), ()))

    def body(a_ref, b_ref, o_ref, acc_ref):
        k = pl.program_id(2)

        @pl.when(k == 0)
        def _():
            acc_ref[...] = jnp.zeros_like(acc_ref)

        acc_ref[...] += lax.dot_general(a_ref[...].astype(BF16), b_ref[...].astype(BF16), dn,
                                        preferred_element_type=F32)

        @pl.when(k == nk - 1)
        def _():
            o_ref[...] = acc_ref[...].astype(out_dtype)

    def body_single(a_ref, b_ref, o_ref):
        o_ref[...] = lax.dot_general(a_ref[...].astype(BF16), b_ref[...].astype(BF16), dn,
                                     preferred_element_type=F32).astype(out_dtype)

    return _pcall(body_single if nk == 1 else body, name=name, out_shape=_sds((M, N), out_dtype),
                  grid=(M // tm, N // tn, nk), in_specs=[a_spec, b_spec],
                  out_specs=pl.BlockSpec((tm, tn), lambda i, j, k: (i, j)),
                  scratch_shapes=[] if nk == 1 else [pltpu.VMEM((tm, tn), F32)],
                  dims=("parallel", "parallel", "arbitrary"))(a, b)


def _ada_fwd(c_all, w_ada, b_ada_cols):
    n = w_ada.shape[1]

    def body(c_ref, w_ref, b_ref, o_ref):
        c = c_ref[...]
        act = c * _sig(c)
        o_ref[...] = jnp.dot(act.astype(BF16), w_ref[...].astype(BF16), preferred_element_type=F32) + b_ref[...]

    return _pcall(body, name="ada_fwd", out_shape=_sds((c_all.shape[0], n), F32))(c_all, w_ada, b_ada_cols)


def _ada_bwd(c_all, dmod_all, dmod_cols):
    n = dmod_cols.shape[1]

    def body(c_ref, da_ref, dc_ref, gw_ref, gb_ref):
        c = c_ref[...]
        act = c * _sig(c)
        gw_ref[...] = lax.dot_general(act, dc_ref[...], (((0,), (0,)), ((), ())), preferred_element_type=F32,
                                      precision=lax.Precision.HIGHEST)
        gb_ref[...] = jnp.sum(da_ref[...], axis=0, keepdims=True)

    return _pcall(body, name="ada_bwd", out_shape=(_sds((D_MODEL, n), F32), _sds((1, dmod_all.shape[1]), F32)))(
        c_all, dmod_all, dmod_cols)


ROW_TILE = 512


def _row_specs(B, S):
    ts = min(S, ROW_TILE)
    row = pl.BlockSpec((1, ts, D_MODEL), lambda b, s: (b, s, 0))
    bvec = pl.BlockSpec((1, 1, D_MODEL), lambda b, s: (b, 0, 0))
    gvec = pl.BlockSpec((1, D_MODEL), lambda b, s: (0, 0))
    return ts, row, bvec, gvec


def _norm_mod(x3, g, sc, sh):
    B, S, _ = x3.shape
    ts, row, bvec, gvec = _row_specs(B, S)

    def body(x_ref, g_ref, sc_ref, sh_ref, u_ref):
        x = x_ref[0]
        r = lax.rsqrt(jnp.mean(x * x, axis=-1, keepdims=True) + EPS)
        u_ref[0] = ((x * r) * g_ref[...] * (1.0 + sc_ref[0]) + sh_ref[0]).astype(BF16)

    return _pcall(body, name="norm_mod1", out_shape=_sds(x3.shape, BF16), grid=(B, S // ts),
                  in_specs=[row, gvec, bvec, bvec], out_specs=row, dims=("parallel", "parallel"))(x3, g, sc, sh)


def _resid_norm_mod(x3, mix3, gt, g, sc, sh):
    B, S, _ = x3.shape
    ts, row, bvec, gvec = _row_specs(B, S)

    def body(x_ref, m_ref, gt_ref, g_ref, sc_ref, sh_ref, h_ref, u_ref):
        h = x_ref[0] + gt_ref[0] * m_ref[0]
        h_ref[0] = h
        r = lax.rsqrt(jnp.mean(h * h, axis=-1, keepdims=True) + EPS)
        u_ref[0] = ((h * r) * g_ref[...] * (1.0 + sc_ref[0]) + sh_ref[0]).astype(BF16)

    return _pcall(body, name="resid_norm_mod2", out_shape=(_sds(x3.shape, F32), _sds(x3.shape, BF16)),
                  grid=(B, S // ts), in_specs=[row, row, bvec, gvec, bvec, bvec], out_specs=(row, row),
                  dims=("parallel", "parallel"))(x3, mix3, gt, g, sc, sh)


def _norm_bwd(h3, du3, dres3, g, sc, name, mix3=None, gt=None):
    B, S, _ = h3.shape
    ts, row, bvec, gvec = _row_specs(B, S)
    with_gate = mix3 is not None

    def body(*refs):
        if with_gate:
            h_ref, du_ref, dr_ref, g_ref, sc_ref, m_ref, gt_ref, dh_ref, dsh_ref, dsc_ref, dg_ref, dgt_ref, dm_ref = refs
        else:
            h_ref, du_ref, dr_ref, g_ref, sc_ref, dh_ref, dsh_ref, dsc_ref, dg_ref = refs
        b, s = pl.program_id(0), pl.program_id(1)
        h = h_ref[0]
        r = lax.rsqrt(jnp.mean(h * h, axis=-1, keepdims=True) + EPS)
        xn = h * r
        du = du_ref[0]
        g = g_ref[...]
        sc1 = 1.0 + sc_ref[0]
        dxn = du * g * sc1
        dh = dr_ref[0] + r * (dxn - xn * jnp.mean(dxn * xn, axis=-1, keepdims=True))
        dh_ref[0] = dh

        @pl.when(s == 0)
        def _():
            dsh_ref[...] = jnp.zeros_like(dsh_ref)
            dsc_ref[...] = jnp.zeros_like(dsc_ref)
            if with_gate:
                dgt_ref[...] = jnp.zeros_like(dgt_ref)

        @pl.when((s == 0) & (b == 0))
        def _():
            dg_ref[...] = jnp.zeros_like(dg_ref)

        dux = du * xn
        dsh_ref[0] += jnp.sum(du, axis=0, keepdims=True)
        dsc_ref[0] += jnp.sum(dux * g, axis=0, keepdims=True)
        dg_ref[...] += jnp.sum(dux * sc1, axis=0, keepdims=True)
        if with_gate:
            dgt_ref[0] += jnp.sum(dh * m_ref[0], axis=0, keepdims=True)
            dm_ref[0] = (dh * gt_ref[0]).astype(BF16)

    bshape = _sds((B, 1, D_MODEL), F32)
    in_specs = [row, row, row, gvec, bvec]
    out_shape = [_sds(h3.shape, F32), bshape, bshape, _sds((1, D_MODEL), F32)]
    out_specs = [row, bvec, bvec, gvec]
    args = [h3, du3, dres3, g, sc]
    if with_gate:
        in_specs += [row, bvec]
        out_shape += [bshape, _sds(h3.shape, BF16)]
        out_specs += [bvec, row]
        args += [mix3, gt]
    return _pcall(body, name=name, out_shape=tuple(out_shape), grid=(B, S // ts), in_specs=in_specs,
                  out_specs=tuple(out_specs), dims=("arbitrary", "arbitrary"))(*args)


def _final_loss(h1, ffn3, tgt3, gt, gfin):
    B, S, _ = h1.shape
    ts, row, bvec, gvec = _row_specs(B, S)
    one = pl.BlockSpec((1, 1), lambda b, s: (0, 0))

    def body(h_ref, f_ref, t_ref, gt_ref, gf_ref, dh_ref, dff_ref, dgt_ref, dgf_ref, loss_ref):
        b, s = pl.program_id(0), pl.program_id(1)
        f = f_ref[0]
        gtv = gt_ref[0]
        gf = gf_ref[...]
        h2 = h_ref[0] + gtv * f
        r = lax.rsqrt(jnp.mean(h2 * h2, axis=-1, keepdims=True) + EPS)
        n = h2 * r
        e = n * gf - t_ref[0]
        dy = e * (1.0 / D_MODEL)
        dn = dy * gf
        dh2 = r * (dn - n * jnp.mean(dn * n, axis=-1, keepdims=True))
        dh_ref[0] = dh2
        dff_ref[0] = (dh2 * gtv).astype(BF16)

        @pl.when(s == 0)
        def _():
            dgt_ref[...] = jnp.zeros_like(dgt_ref)

        @pl.when((s == 0) & (b == 0))
        def _():
            dgf_ref[...] = jnp.zeros_like(dgf_ref)
            loss_ref[...] = jnp.zeros_like(loss_ref)

        dgt_ref[0] += jnp.sum(dh2 * f, axis=0, keepdims=True)
        dgf_ref[...] += jnp.sum(dy * n, axis=0, keepdims=True)
        rows = jnp.sum(e * e, axis=1, keepdims=True)
        loss_ref[...] += jnp.sum(rows, axis=0, keepdims=True) * (0.5 / D_MODEL)

    return _pcall(body, name="final_loss",
                  out_shape=(_sds(h1.shape, F32), _sds(h1.shape, BF16), _sds((B, 1, D_MODEL), F32),
                             _sds((1, D_MODEL), F32), _sds((1, 1), F32)),
                  grid=(B, S // ts), in_specs=[row, row, row, bvec, gvec], out_specs=(row, row, bvec, gvec, one),
                  dims=("arbitrary", "arbitrary"))(h1, ffn3, tgt3, gt, gfin)


def _att_scores(qh, kc, kp, h, dil, first, a_idx, j_idx):
    scale = HEAD_DIM ** -0.5
    nt = (((1,), (1,)), ((), ()))
    slope = (2.0 ** (-8.0 * (h + 1) / N_HEADS)) * dil
    dist_c = (a_idx - j_idx).astype(F32)
    s_c = lax.dot_general(qh, kc, nt, preferred_element_type=F32) * scale
    s_c = jnp.where(a_idx >= j_idx, s_c - slope * dist_c, NEG_INF)
    s_p = lax.dot_general(qh, kp, nt, preferred_element_type=F32) * scale
    s_p = jnp.where((j_idx >= a_idx) & jnp.logical_not(first), s_p - slope * (dist_c + float(ATT_BLOCK)), NEG_INF)
    return s_c, s_p


def _att_block_consts(seq_blocks):
    p = pl.program_id(0)
    j = pl.program_id(1)
    nb = lax.shift_right_logical(jnp.int32(seq_blocks), 2 * p)
    dil = lax.shift_left(jnp.int32(1), 2 * p).astype(F32)
    a_idx = lax.broadcasted_iota(jnp.int32, (ATT_BLOCK, ATT_BLOCK), 0)
    j_idx = lax.broadcasted_iota(jnp.int32, (ATT_BLOCK, ATT_BLOCK), 1)
    return j, nb, dil, a_idx, j_idx


def _attn_fwd(qb, kb, vb, seq_blocks):
    _, NB, _, _ = qb.shape
    cur = pl.BlockSpec((None, None, ATT_BLOCK, ATT_WIDTH), lambda p, j: (p, j, 0, 0))
    prev = pl.BlockSpec((None, None, ATT_BLOCK, ATT_WIDTH), lambda p, j: (p, jnp.maximum(j - 1, 0), 0, 0))
    lse_spec = pl.BlockSpec((None, None, ATT_BLOCK, N_HEADS), lambda p, j: (p, j, 0, 0))

    def body(q_ref, kc_ref, kp_ref, vc_ref, vp_ref, o_ref, lse_ref):
        j, nb, dil, a_idx, j_idx = _att_block_consts(seq_blocks)
        first = lax.rem(j, nb) == 0
        for h in range(N_HEADS):
            hs = slice(h * HEAD_DIM, (h + 1) * HEAD_DIM)
            s_c, s_p = _att_scores(q_ref[:, hs], kc_ref[:, hs], kp_ref[:, hs], h, dil, first, a_idx, j_idx)
            m = jnp.maximum(jnp.max(s_c, axis=1, keepdims=True), jnp.max(s_p, axis=1, keepdims=True))
            p_c = jnp.exp(s_c - m)
            p_p = jnp.exp(s_p - m)
            den = jnp.sum(p_c, axis=1, keepdims=True) + jnp.sum(p_p, axis=1, keepdims=True)
            o = (jnp.dot(p_c.astype(BF16), vc_ref[:, hs], preferred_element_type=F32)
                 + jnp.dot(p_p.astype(BF16), vp_ref[:, hs], preferred_element_type=F32))
            o_ref[:, hs] = o / den
            lse_ref[:, h:h + 1] = m + jnp.log(den)

    return _pcall(body, name="attn_fwd",
                  out_shape=(_sds(qb.shape, F32), _sds((N_PATTERNS, NB, ATT_BLOCK, N_HEADS), F32)),
                  grid=(N_PATTERNS, NB), in_specs=[cur, cur, prev, cur, prev], out_specs=(cur, lse_spec),
                  dims=("parallel", "parallel"))(qb, kb, kb, vb, vb)


def _attn_combine(o_p, lse_p):
    _, T, _ = o_p.shape
    tm = min(T, 1024)

    def body(o_ref, l_ref, out_ref, lse_ref):
        l0, l1, l2 = l_ref[0], l_ref[1], l_ref[2]
        m = jnp.maximum(jnp.maximum(l0, l1), l2)
        lse = m + jnp.log(jnp.exp(l0 - m) + jnp.exp(l1 - m) + jnp.exp(l2 - m))
        lse_ref[...] = lse
        w = [jnp.exp(l0 - lse), jnp.exp(l1 - lse), jnp.exp(l2 - lse)]
        for h in range(N_HEADS):
            hs = slice(h * HEAD_DIM, (h + 1) * HEAD_DIM)
            acc = w[0][:, h:h + 1] * o_ref[0, :, hs]
            acc = acc + w[1][:, h:h + 1] * o_ref[1, :, hs]
            acc = acc + w[2][:, h:h + 1] * o_ref[2, :, hs]
            out_ref[:, hs] = acc.astype(BF16)

    return _pcall(body, name="attn_combine", out_shape=(_sds((T, ATT_WIDTH), BF16), _sds((T, N_HEADS), F32)),
                  grid=(T // tm,),
                  in_specs=[pl.BlockSpec((N_PATTERNS, tm, ATT_WIDTH), lambda i: (0, i, 0)),
                            pl.BlockSpec((N_PATTERNS, tm, N_HEADS), lambda i: (0, i, 0))],
                  out_specs=(pl.BlockSpec((tm, ATT_WIDTH), lambda i: (i, 0)), pl.BlockSpec((tm, N_HEADS), lambda i: (i, 0))),
                  dims=("parallel",))(o_p, lse_p)


def _attn_bwd(qb, kb, vb, dob, ob, lseb, seq_blocks):
    _, NB, _, _ = qb.shape
    last = NB - 1
    cur = pl.BlockSpec((None, None, ATT_BLOCK, ATT_WIDTH), lambda p, j: (p, jnp.minimum(j, last), 0, 0))
    prev = pl.BlockSpec((None, None, ATT_BLOCK, ATT_WIDTH),
                        lambda p, j: (p, jnp.maximum(jnp.minimum(j, last) - 1, 0), 0, 0))
    lag = pl.BlockSpec((None, None, ATT_BLOCK, ATT_WIDTH), lambda p, j: (p, jnp.maximum(j - 1, 0), 0, 0))
    lse_spec = pl.BlockSpec((None, None, ATT_BLOCK, N_HEADS), lambda p, j: (p, jnp.minimum(j, last), 0, 0))
    scale = HEAD_DIM ** -0.5
    tn = (((0,), (0,)), ((), ()))
    nt = (((1,), (1,)), ((), ()))

    def body(q_ref, kc_ref, kp_ref, vc_ref, vp_ref, do_ref, o_ref, lse_ref, dq_ref, dk_ref, dv_ref, ck_ref, cv_ref):
        j, nb, dil, a_idx, j_idx = _att_block_consts(seq_blocks)

        @pl.when(j == 0)
        def _():
            ck_ref[...] = jnp.zeros_like(ck_ref)
            cv_ref[...] = jnp.zeros_like(cv_ref)

        @pl.when(j <= last)
        def _():
            first = lax.rem(j, nb) == 0
            for h in range(N_HEADS):
                hs = slice(h * HEAD_DIM, (h + 1) * HEAD_DIM)
                qh, kc, kp, vc, vp, doh = q_ref[:, hs], kc_ref[:, hs], kp_ref[:, hs], vc_ref[:, hs], vp_ref[:, hs], do_ref[:, hs]
                s_c, s_p = _att_scores(qh, kc, kp, h, dil, first, a_idx, j_idx)
                lse = lse_ref[:, h:h + 1]
                p_c = jnp.exp(s_c - lse)
                p_p = jnp.exp(s_p - lse)
                delta = jnp.sum(doh.astype(F32) * o_ref[:, hs].astype(F32), axis=1, keepdims=True)
                ds_c = (p_c * (lax.dot_general(doh, vc, nt, preferred_element_type=F32) - delta)).astype(BF16)
                ds_p = (p_p * (lax.dot_general(doh, vp, nt, preferred_element_type=F32) - delta)).astype(BF16)
                dq_ref[:, hs] = (jnp.dot(ds_c, kc, preferred_element_type=F32)
                                 + jnp.dot(ds_p, kp, preferred_element_type=F32)) * scale
                dk_ref[:, hs] = ck_ref[:, hs] + lax.dot_general(ds_p, qh, tn, preferred_element_type=F32) * scale
                dv_ref[:, hs] = cv_ref[:, hs] + lax.dot_general(p_p.astype(BF16), doh, tn, preferred_element_type=F32)
                ck_ref[:, hs] = lax.dot_general(ds_c, qh, tn, preferred_element_type=F32) * scale
                cv_ref[:, hs] = lax.dot_general(p_c.astype(BF16), doh, tn, preferred_element_type=F32)

        @pl.when(j == NB)
        def _():
            dk_ref[...] = ck_ref[...]
            dv_ref[...] = cv_ref[...]

    shp = _sds(qb.shape, F32)
    return _pcall(body, name="attn_bwd", out_shape=(shp, shp, shp), grid=(N_PATTERNS, NB + 1),
                  in_specs=[cur, cur, prev, cur, prev, cur, cur, lse_spec], out_specs=(cur, lag, lag),
                  scratch_shapes=[pltpu.VMEM((ATT_BLOCK, ATT_WIDTH), F32), pltpu.VMEM((ATT_BLOCK, ATT_WIDTH), F32)],
                  dims=("arbitrary", "arbitrary"))(qb, kb, kb, vb, vb, dob, ob, lseb)


def _sum3_cast(a, b, c):
    T, N = a.shape
    tm = min(T, 1024)
    spec = pl.BlockSpec((tm, N), lambda i: (i, 0))

    def body(a_ref, b_ref, c_ref, o_ref):
        o_ref[...] = (a_ref[...] + b_ref[...] + c_ref[...]).astype(BF16)

    return _pcall(body, name="sum3_cast", out_shape=_sds((T, N), BF16), grid=(T // tm,), in_specs=[spec] * 3,
                  out_specs=spec, dims=("parallel",))(a, b, c)


def _to_blocks(t, B, S):
    C = t.shape[-1]
    outs = []
    for p in range(N_PATTERNS):
        d = 4 ** p
        u = t.reshape(B, S // d, d, C).transpose(0, 2, 1, 3)
        outs.append(u.reshape(B * S // ATT_BLOCK, ATT_BLOCK, C))
    return jnp.stack(outs, axis=0)


def _from_blocks(tb, B, S):
    C = tb.shape[-1]
    outs = []
    for p in range(N_PATTERNS):
        d = 4 ** p
        u = tb[p].reshape(B, d, S // d, C).transpose(0, 2, 1, 3)
        outs.append(u.reshape(B * S, C))
    return jnp.stack(outs, axis=0)


ATT_GROUP = 4
ATT_GW = ATT_GROUP * HEAD_DIM
ATT_GROUPS = N_HEADS // ATT_GROUP
ATT_PAIRS = ATT_GW // ATT_BLOCK
NT_DIMS = (((1,), (1,)), ((), ()))
TN_DIMS = (((0,), (0,)), ((), ()))


def _att_rows(start, d):
    if d == 1:
        return pl.ds(start if isinstance(start, int) else pl.multiple_of(start, ATT_BLOCK), ATT_BLOCK)
    return pl.ds(start, ATT_BLOCK, stride=d)


def _att_fill_bias(bias_ref, g, d):
    a = lax.broadcasted_iota(jnp.int32, (ATT_BLOCK, ATT_BLOCK), 0)
    j = lax.broadcasted_iota(jnp.int32, (ATT_BLOCK, ATT_BLOCK), 1)
    dist = (a - j).astype(F32)
    for hh in range(ATT_GROUP):
        t, e = divmod(hh, 2)
        rs = slice(e * ATT_BLOCK, (e + 1) * ATT_BLOCK)
        lo = 2.0 ** (-8.0 * (hh + 1) / N_HEADS) * d
        hi = 2.0 ** (-8.0 * (ATT_GROUP + hh + 1) / N_HEADS) * d
        slope = jnp.where(g == 0, lo, hi).astype(F32)
        bias_ref[t, rs, 0:ATT_BLOCK] = jnp.where(a >= j, -slope * dist, NEG_INF)
        bias_ref[t, rs, ATT_BLOCK:] = jnp.where(j >= a, -slope * (dist + float(ATT_BLOCK)), NEG_INF)


def _stack_heads(v2, low):
    return jnp.concatenate([jnp.where(low, v2, 0.0), jnp.where(low, 0.0, v2)], axis=0).astype(BF16)


def _unstack_heads(r2, low):
    return jnp.where(low, r2[0:ATT_BLOCK], r2[ATT_BLOCK:])


def _attention_fwd(proj3, seq_blocks):
    B, S, _ = proj3.shape
    scale = HEAD_DIM ** -0.5
    nq = ATT_WIDTH // ATT_GW

    def col(k):
        return pl.BlockSpec((1, S, ATT_GW), lambda b, g, k=k: (b, 0, k * nq + g))

    o_spec = pl.BlockSpec((1, S, ATT_GW), lambda b, g: (b, 0, g))
    l_spec = pl.BlockSpec((1, 1, S, ATT_BLOCK), lambda b, g: (b, g, 0, 0))

    def body(q_ref, k_ref, v_ref, o_ref, lse_ref, qf, kf, vf, os, ls, bias):
        g = pl.program_id(1)
        for t in range(ATT_PAIRS):
            ts = slice(t * ATT_BLOCK, (t + 1) * ATT_BLOCK)
            qf[t] = q_ref[0, :, ts].astype(F32) * scale
            kf[t] = k_ref[0, :, ts].astype(F32)
            vf[t] = v_ref[0, :, ts].astype(F32)
        lane = lax.broadcasted_iota(jnp.int32, (ATT_BLOCK, ATT_BLOCK), 1)
        low = lane < HEAD_DIM

        def block(p, d, r, n, has_prev):
            start = n * (ATT_BLOCK * d) + r
            rows = _att_rows(start, d)
            prows = _att_rows(start - ATT_BLOCK * d, d) if has_prev else None
            lse_t = jnp.zeros((ATT_BLOCK, ATT_BLOCK), F32)
            for t in range(ATT_PAIRS):
                q2 = _stack_heads(qf[t, rows, :], low)
                k2 = kf[t, rows, :].astype(BF16)
                v2 = vf[t, rows, :].astype(BF16)
                if has_prev:
                    k2 = jnp.concatenate([k2, kf[t, prows, :].astype(BF16)], axis=0)
                    v2 = jnp.concatenate([v2, vf[t, prows, :].astype(BF16)], axis=0)
                    b2 = bias[t]
                else:
                    b2 = bias[t, :, 0:ATT_BLOCK]
                s = lax.dot_general(q2, k2, NT_DIMS, preferred_element_type=F32) + b2
                m = jnp.max(s, axis=1, keepdims=True)
                pr = jnp.exp(s - m)
                den = jnp.sum(pr, axis=1, keepdims=True)
                o = jnp.dot(pr.astype(BF16), v2, preferred_element_type=F32) / den
                os[p, t, rows, :] = _unstack_heads(o, low)
                lse2 = m + jnp.log(den)
                lse_t = jnp.where(lane == 2 * t, lse2[0:ATT_BLOCK], lse_t)
                lse_t = jnp.where(lane == 2 * t + 1, lse2[ATT_BLOCK:], lse_t)
            ls[p, rows, :] = lse_t

        for p in range(N_PATTERNS):
            d = 4 ** p
            _att_fill_bias(bias, g, d)
            _att_one_pattern(block, p, d, seq_blocks // d)

        def combine(i, carry):
            rows = pl.ds(pl.multiple_of(i * ATT_BLOCK, ATT_BLOCK), ATT_BLOCK)
            l0, l1, l2 = ls[0, rows, :], ls[1, rows, :], ls[2, rows, :]
            m = jnp.maximum(jnp.maximum(l0, l1), l2)
            lse = m + jnp.log(jnp.exp(l0 - m) + jnp.exp(l1 - m) + jnp.exp(l2 - m))
            lse_ref[0, 0, rows, :] = lse
            w = [jnp.exp(l0 - lse), jnp.exp(l1 - lse), jnp.exp(l2 - lse)]
            for t in range(ATT_PAIRS):
                acc = jnp.zeros((ATT_BLOCK, ATT_BLOCK), F32)
                for p in range(N_PATTERNS):
                    wt = jnp.where(low, w[p][:, 2 * t:2 * t + 1], w[p][:, 2 * t + 1:2 * t + 2])
                    acc = acc + wt * os[p, t, rows, :]
                o_ref[0, rows, t * ATT_BLOCK:(t + 1) * ATT_BLOCK] = acc.astype(BF16)
            return carry

        lax.fori_loop(0, S // ATT_BLOCK, combine, 0)

    return _pcall(body, name="attention_fwd",
                  out_shape=(_sds((B, S, ATT_WIDTH), BF16), _sds((B, ATT_GROUPS, S, ATT_BLOCK), F32)),
                  grid=(B, ATT_GROUPS), in_specs=[col(0), col(1), col(2)], out_specs=(o_spec, l_spec),
                  scratch_shapes=[pltpu.VMEM((ATT_PAIRS, S, ATT_BLOCK), F32)] * 3
                  + [pltpu.VMEM((N_PATTERNS, ATT_PAIRS, S, ATT_BLOCK), F32), pltpu.VMEM((N_PATTERNS, S, ATT_BLOCK), F32),
                     pltpu.VMEM((ATT_PAIRS, 2 * ATT_BLOCK, 2 * ATT_BLOCK), F32)],
                  dims=("parallel", "parallel"))(proj3, proj3, proj3)


def _att_one_pattern(block, p, d, nb):
    def per_residue(r, carry):
        block(p, d, r, 0, False)
        if nb > 1:
            def per_block(n, c2):
                block(p, d, r, n, True)
                return c2
            lax.fori_loop(1, nb, per_block, 0)
        return carry

    if d == 1:
        per_residue(0, 0)
    else:
        lax.fori_loop(0, d, per_residue, 0)


def _attention_bwd(proj3, do3, o3, lse4, seq_blocks):
    B, S, _ = proj3.shape
    scale = HEAD_DIM ** -0.5
    nq = ATT_WIDTH // ATT_GW

    def col(k):
        return pl.BlockSpec((1, S, ATT_GW), lambda b, g, k=k: (b, 0, k * nq + g))

    o_spec = pl.BlockSpec((1, S, ATT_GW), lambda b, g: (b, 0, g))
    l_spec = pl.BlockSpec((1, 1, S, ATT_BLOCK), lambda b, g: (b, g, 0, 0))

    def body(q_ref, k_ref, v_ref, do_ref, o_ref, lse_ref, dq_ref, dk_ref, dv_ref,
             qf, kf, vf, dof, dl, aq, ak, av, bias):
        g = pl.program_id(1)
        for t in range(ATT_PAIRS):
            ts = slice(t * ATT_BLOCK, (t + 1) * ATT_BLOCK)
            qf[t] = q_ref[0, :, ts].astype(F32) * scale
            kf[t] = k_ref[0, :, ts].astype(F32)
            vf[t] = v_ref[0, :, ts].astype(F32)
            dof[t] = do_ref[0, :, ts].astype(F32)
        aq[...] = jnp.zeros_like(aq)
        ak[...] = jnp.zeros_like(ak)
        av[...] = jnp.zeros_like(av)
        lane = lax.broadcasted_iota(jnp.int32, (ATT_BLOCK, ATT_BLOCK), 1)
        low = lane < HEAD_DIM

        def fill_delta(i, carry):
            rows = pl.ds(pl.multiple_of(i * ATT_BLOCK, ATT_BLOCK), ATT_BLOCK)
            acc = jnp.zeros((ATT_BLOCK, ATT_BLOCK), F32)
            for t in range(ATT_PAIRS):
                prod = dof[t, rows, :] * o_ref[0, rows, t * ATT_BLOCK:(t + 1) * ATT_BLOCK].astype(F32)
                lo = jnp.sum(jnp.where(low, prod, 0.0), axis=1, keepdims=True)
                hi = jnp.sum(prod, axis=1, keepdims=True) - lo
                acc = jnp.where(lane == 2 * t, lo, acc)
                acc = jnp.where(lane == 2 * t + 1, hi, acc)
            dl[rows, :] = acc
            return carry

        lax.fori_loop(0, S // ATT_BLOCK, fill_delta, 0)

        def block(p, d, r, n, has_prev):
            start = n * (ATT_BLOCK * d) + r
            rows = _att_rows(start, d)
            prows = _att_rows(start - ATT_BLOCK * d, d) if has_prev else None
            lse_t = lse_ref[0, 0, rows, :]
            dl_t = dl[rows, :]
            for t in range(ATT_PAIRS):
                q2 = _stack_heads(qf[t, rows, :], low)
                do2 = _stack_heads(dof[t, rows, :], low)
                k2 = kf[t, rows, :].astype(BF16)
                v2 = vf[t, rows, :].astype(BF16)
                if has_prev:
                    k2 = jnp.concatenate([k2, kf[t, prows, :].astype(BF16)], axis=0)
                    v2 = jnp.concatenate([v2, vf[t, prows, :].astype(BF16)], axis=0)
                    b2 = bias[t]
                else:
                    b2 = bias[t, :, 0:ATT_BLOCK]
                lse2 = jnp.concatenate([lse_t[:, 2 * t:2 * t + 1], lse_t[:, 2 * t + 1:2 * t + 2]], axis=0)
                dl2 = jnp.concatenate([dl_t[:, 2 * t:2 * t + 1], dl_t[:, 2 * t + 1:2 * t + 2]], axis=0)
                s = lax.dot_general(q2, k2, NT_DIMS, preferred_element_type=F32) + b2
                pr = jnp.exp(s - lse2)
                ds = (pr * (lax.dot_general(do2, v2, NT_DIMS, preferred_element_type=F32) - dl2)).astype(BF16)
                dq = _unstack_heads(jnp.dot(ds, k2, preferred_element_type=F32), low)
                dk = lax.dot_general(ds, q2, TN_DIMS, preferred_element_type=F32)
                dv = lax.dot_general(pr.astype(BF16), do2, TN_DIMS, preferred_element_type=F32)
                aq[t, rows, :] = aq[t, rows, :] + dq * scale
                ak[t, rows, :] = ak[t, rows, :] + dk[0:ATT_BLOCK]
                av[t, rows, :] = av[t, rows, :] + dv[0:ATT_BLOCK]
                if has_prev:
                    ak[t, prows, :] = ak[t, prows, :] + dk[ATT_BLOCK:]
                    av[t, prows, :] = av[t, prows, :] + dv[ATT_BLOCK:]

        for p in range(N_PATTERNS):
            d = 4 ** p
            _att_fill_bias(bias, g, d)
            _att_one_pattern(block, p, d, seq_blocks // d)

        for t in range(ATT_PAIRS):
            ts = slice(t * ATT_BLOCK, (t + 1) * ATT_BLOCK)
            dq_ref[0, :, ts] = aq[t].astype(BF16)
            dk_ref[0, :, ts] = ak[t].astype(BF16)
            dv_ref[0, :, ts] = av[t].astype(BF16)

    shp = _sds((B, S, ATT_WIDTH), BF16)
    pair_buf = pltpu.VMEM((ATT_PAIRS, S, ATT_BLOCK), F32)
    return _pcall(body, name="attention_bwd", out_shape=(shp, shp, shp), grid=(B, ATT_GROUPS),
                  in_specs=[col(0), col(1), col(2), o_spec, o_spec, l_spec], out_specs=(o_spec, o_spec, o_spec),
                  scratch_shapes=[pair_buf] * 4 + [pltpu.VMEM((S, ATT_BLOCK), F32)] + [pair_buf] * 3
                  + [pltpu.VMEM((ATT_PAIRS, 2 * ATT_BLOCK, 2 * ATT_BLOCK), F32)],
                  dims=("parallel", "parallel"))(proj3, proj3, proj3, do3, o3, lse4)


def _expand_groups(m):
    rows = SSM_WIDTH
    t = jnp.concatenate([m] * SSM_GROUPS, axis=0)
    r = lax.broadcasted_iota(jnp.int32, (rows, SSM_LANES), 0)
    l = lax.broadcasted_iota(jnp.int32, (rows, SSM_LANES), 1)
    keep = lax.shift_right_logical(r, 4) == lax.shift_right_logical(l, 6)
    return jnp.where(keep, t, 0.0)


def _collapse_groups(m):
    rows = SSM_WIDTH
    r = lax.broadcasted_iota(jnp.int32, (rows, SSM_LANES), 0)
    l = lax.broadcasted_iota(jnp.int32, (rows, SSM_LANES), 1)
    keep = lax.shift_right_logical(r, 4) == lax.shift_right_logical(l, 6)
    t = jnp.where(keep, m, 0.0)
    acc = t[0:SSM_GROUP_CH]
    for g in range(1, SSM_GROUPS):
        acc = acc + t[g * SSM_GROUP_CH:(g + 1) * SSM_GROUP_CH]
    return acc


def _zoh(lr, li, ldt):
    dt = jnp.exp(ldt)
    mag = jnp.exp(lr * dt)
    ang = li * dt
    cs, sn = jnp.cos(ang), jnp.sin(ang)
    ab_re, ab_im = mag * cs, mag * sn
    nr, ni = ab_re - 1.0, ab_im
    den = lr * lr + li * li
    n_re = nr * lr + ni * li
    n_im = ni * lr - nr * li
    return dict(dt=dt, mag=mag, cs=cs, sn=sn, ab_re=ab_re, ab_im=ab_im, nr=nr, ni=ni, den=den, n_re=n_re, n_im=n_im,
                f_re=n_re / den, f_im=n_im / den)


def _ssm_params(lr, li, ldt, br, bi, cr, ci):
    def body(lr_ref, li_ref, ldt_ref, br_ref, bi_ref, cr_ref, ci_ref, ab_ref, w_ref, c_ref):
        z = _zoh(lr_ref[...], li_ref[...], ldt_ref[...])
        ab_ref[0:1, :] = z["ab_re"]
        ab_ref[1:2, :] = z["ab_im"]
        br, bi = br_ref[...], bi_ref[...]
        w_ref[:, 0:SSM_LANES] = _expand_groups(z["f_re"] * br - z["f_im"] * bi).astype(BF16)
        w_ref[:, SSM_LANES:] = _expand_groups(z["f_re"] * bi + z["f_im"] * br).astype(BF16)
        c_ref[:, 0:SSM_LANES] = _expand_groups(cr_ref[...]).astype(BF16)
        c_ref[:, SSM_LANES:] = _expand_groups(-ci_ref[...]).astype(BF16)

    return _pcall(body, name="ssm_params",
                  out_shape=(_sds((2, SSM_LANES), F32), _sds((SSM_WIDTH, 2 * SSM_LANES), BF16),
                             _sds((SSM_WIDTH, 2 * SSM_LANES), BF16)))(lr, li, ldt, br, bi, cr, ci)


def _ssm_params_bwd(lr, li, ldt, br, bi, dab, dw, dc):
    def body(lr_ref, li_ref, ldt_ref, br_ref, bi_ref, dab_ref, dw_ref, dc_ref,
             dlr_ref, dli_ref, dldt_ref, dbr_ref, dbi_ref, dcr_ref, dci_ref):
        lr, li = lr_ref[...], li_ref[...]
        z = _zoh(lr, li, ldt_ref[...])
        br, bi = br_ref[...], bi_ref[...]
        dbb_re = _collapse_groups(dw_ref[:, 0:SSM_LANES])
        dbb_im = _collapse_groups(dw_ref[:, SSM_LANES:])
        dcr_ref[...] = _collapse_groups(dc_ref[:, 0:SSM_LANES])
        dci_ref[...] = -_collapse_groups(dc_ref[:, SSM_LANES:])
        f_re, f_im = z["f_re"], z["f_im"]
        dbr_ref[...] = f_re * dbb_re + f_im * dbb_im
        dbi_ref[...] = f_re * dbb_im - f_im * dbb_re
        df_re = jnp.sum(dbb_re * br + dbb_im * bi, axis=0, keepdims=True)
        df_im = jnp.sum(dbb_im * br - dbb_re * bi, axis=0, keepdims=True)
        den = z["den"]
        dn_re, dn_im = df_re / den, df_im / den
        dden = -(df_re * z["n_re"] + df_im * z["n_im"]) / (den * den)
        dnr = dn_re * lr - dn_im * li
        dni = dn_re * li + dn_im * lr
        dlr = dn_re * z["nr"] + dn_im * z["ni"] + 2.0 * dden * lr
        dli = dn_re * z["ni"] - dn_im * z["nr"] + 2.0 * dden * li
        dab_re = dab_ref[0:1, :] + dnr
        dab_im = dab_ref[1:2, :] + dni
        mag, cs, sn, dt = z["mag"], z["cs"], z["sn"], z["dt"]
        dmag = dab_re * cs + dab_im * sn
        dang = mag * (dab_im * cs - dab_re * sn)
        dlr_ref[...] = dlr + dmag * mag * dt
        dli_ref[...] = dli + dang * dt
        ddt = dmag * mag * lr + dang * li
        per_lane = jnp.broadcast_to(ddt * dt, (8, SSM_LANES))
        lane = lax.broadcasted_iota(jnp.int32, (SSM_LANES, 128), 0)
        col = lax.broadcasted_iota(jnp.int32, (SSM_LANES, 128), 1)
        ind = jnp.where(lax.shift_right_logical(lane, 6) == col, 1.0, 0.0)
        dldt_ref[...] = jnp.dot(per_lane, ind, preferred_element_type=F32, precision=lax.Precision.HIGHEST)[0:1]

    vec = _sds((1, SSM_LANES), F32)
    mat = _sds((SSM_GROUP_CH, SSM_LANES), F32)
    return _pcall(body, name="ssm_params_bwd", out_shape=(vec, vec, _sds((1, 128), F32), mat, mat, mat, mat))(
        lr, li, ldt, br, bi, dab, dw, dc)


SCAN_CHUNK = 512


def _scan_consts(ar, ai, k_ref, reverse):
    row = lax.broadcasted_iota(jnp.int32, (8, SSM_LANES), 0)
    pw = [(ar, ai)]
    for _ in range(7):
        pr, pi = pw[-1]
        pw.append((pr * ar - pi * ai, pr * ai + pi * ar))
    for n, k in enumerate((1, 2, 4)):
        keep = (row < 8 - k) if reverse else (row >= k)
        k_ref[2 * n] = jnp.where(keep, jnp.broadcast_to(pw[k - 1][0], (8, SSM_LANES)), 0.0)
        k_ref[2 * n + 1] = jnp.where(keep, jnp.broadcast_to(pw[k - 1][1], (8, SSM_LANES)), 0.0)
    cr = jnp.zeros((8, SSM_LANES), F32)
    ci = jnp.zeros((8, SSM_LANES), F32)
    for r in range(8):
        e = (8 - r) if reverse else (r + 1)
        cr = jnp.where(row == r, jnp.broadcast_to(pw[e - 1][0], (8, SSM_LANES)), cr)
        ci = jnp.where(row == r, jnp.broadcast_to(pw[e - 1][1], (8, SSM_LANES)), ci)
    k_ref[6] = cr
    k_ref[7] = ci


def _scan_tile(xr, xi, k_ref, car, cai, reverse):
    for n, k in enumerate((1, 2, 4)):
        sh = (8 - k) if reverse else k
        sr = pltpu.roll(xr, sh, 0)
        si = pltpu.roll(xi, sh, 0)
        mr, mi = k_ref[2 * n], k_ref[2 * n + 1]
        xr, xi = xr + mr * sr - mi * si, xi + mr * si + mi * sr
    pr, pi = k_ref[6], k_ref[7]
    xr, xi = xr + pr * car - pi * cai, xi + pr * cai + pi * car
    return xr, xi


def _scan_fwd(bu3, abar):
    B, S, _ = bu3.shape
    ch = min(S, SCAN_CHUNK)
    blk = pl.BlockSpec((1, ch, 2 * SSM_LANES), lambda b, c: (b, c, 0))

    def body(ab_ref, bu_ref, x_ref, k_ref, carry_ref):
        _scan_consts(ab_ref[0:1, :], ab_ref[1:2, :], k_ref, False)

        @pl.when(pl.program_id(1) == 0)
        def _():
            carry_ref[...] = jnp.zeros_like(carry_ref)

        def step(i, carry):
            base = pl.multiple_of(i * 8, 8)
            xr = bu_ref[0, pl.ds(base, 8), 0:SSM_LANES]
            xi = bu_ref[0, pl.ds(base, 8), SSM_LANES:]
            xr, xi = _scan_tile(xr, xi, k_ref, carry[0], carry[1], False)
            x_ref[0, pl.ds(base, 8), 0:SSM_LANES] = xr
            x_ref[0, pl.ds(base, 8), SSM_LANES:] = xi
            return (jnp.broadcast_to(xr[7:8], (8, SSM_LANES)), jnp.broadcast_to(xi[7:8], (8, SSM_LANES)))

        cr, ci = lax.fori_loop(0, ch // 8, step, (carry_ref[0], carry_ref[1]))
        carry_ref[0] = cr
        carry_ref[1] = ci

    return _pcall(body, name="scan_fwd", out_shape=_sds(bu3.shape, F32), grid=(B, S // ch),
                  in_specs=[pl.BlockSpec((2, SSM_LANES), lambda b, c: (0, 0)), blk], out_specs=blk,
                  scratch_shapes=[pltpu.VMEM((8, 8, SSM_LANES), F32), pltpu.VMEM((2, 8, SSM_LANES), F32)],
                  dims=("arbitrary", "arbitrary"))(abar, bu3)


def _scan_bwd(dx3, xs3, abar):
    B, S, _ = dx3.shape
    ch = min(S, SCAN_CHUNK)
    nc = S // ch
    blk = pl.BlockSpec((1, ch, 2 * SSM_LANES), lambda b, c: (b, nc - 1 - c, 0))

    def body(ab_ref, dx_ref, xs_ref, g_ref, da_ref, k_ref, carry_ref, acc_ref):
        b, c = pl.program_id(0), pl.program_id(1)
        _scan_consts(ab_ref[0:1, :], -ab_ref[1:2, :], k_ref, True)
        row = lax.broadcasted_iota(jnp.int32, (8, SSM_LANES), 0)

        @pl.when(c == 0)
        def _():
            carry_ref[...] = jnp.zeros_like(carry_ref)

        @pl.when((c == 0) & (b == 0))
        def _():
            acc_ref[...] = jnp.zeros_like(acc_ref)

        def step(i, carry):
            car, cai, ar_acc, ai_acc = carry
            base = pl.multiple_of((ch // 8 - 1 - i) * 8, 8)
            gr = dx_ref[0, pl.ds(base, 8), 0:SSM_LANES]
            gi = dx_ref[0, pl.ds(base, 8), SSM_LANES:]
            gr, gi = _scan_tile(gr, gi, k_ref, car, cai, True)
            g_ref[0, pl.ds(base, 8), 0:SSM_LANES] = gr
            g_ref[0, pl.ds(base, 8), SSM_LANES:] = gi
            nr = jnp.where(row == 7, car, pltpu.roll(gr, 7, 0))
            ni = jnp.where(row == 7, cai, pltpu.roll(gi, 7, 0))
            xr = xs_ref[0, pl.ds(base, 8), 0:SSM_LANES]
            xi = xs_ref[0, pl.ds(base, 8), SSM_LANES:]
            ar_acc = ar_acc + nr * xr + ni * xi
            ai_acc = ai_acc + ni * xr - nr * xi
            return (jnp.broadcast_to(gr[0:1], (8, SSM_LANES)), jnp.broadcast_to(gi[0:1], (8, SSM_LANES)), ar_acc, ai_acc)

        cr, ci, ar_acc, ai_acc = lax.fori_loop(0, ch // 8, step, (carry_ref[0], carry_ref[1], acc_ref[0], acc_ref[1]))
        carry_ref[0] = cr
        carry_ref[1] = ci
        acc_ref[0] = ar_acc
        acc_ref[1] = ai_acc
        da_ref[0:1, :] = jnp.sum(ar_acc, axis=0, keepdims=True)
        da_ref[1:2, :] = jnp.sum(ai_acc, axis=0, keepdims=True)

    return _pcall(body, name="scan_bwd", out_shape=(_sds(dx3.shape, F32), _sds((2, SSM_LANES), F32)), grid=(B, nc),
                  in_specs=[pl.BlockSpec((2, SSM_LANES), lambda b, c: (0, 0)), blk, blk],
                  out_specs=(blk, pl.BlockSpec((2, SSM_LANES), lambda b, c: (0, 0))),
                  scratch_shapes=[pltpu.VMEM((8, 8, SSM_LANES), F32), pltpu.VMEM((2, 8, SSM_LANES), F32),
                                  pltpu.VMEM((2, 8, SSM_LANES), F32)],
                  dims=("arbitrary", "arbitrary"))(abar, dx3, xs3)


GELU_K = math.sqrt(2.0 / math.pi)
GELU_C = 0.044715


def _gelu_parts(y):
    t = jnp.tanh(GELU_K * (y + GELU_C * y * y * y))
    return 0.5 * y * (1.0 + t), t


def _ssm_post(yc, us, dsk, wglu, bglu):
    T, N = yc.shape
    tm = min(T, 1024)
    row = pl.BlockSpec((tm, N), lambda i: (i, 0))
    vec = pl.BlockSpec((1, N), lambda i: (0, 0))
    mat = pl.BlockSpec((N, N), lambda i: (0, 0))

    def body(yc_ref, us_ref, d_ref, w_ref, b_ref, y_ref, s_ref):
        y = yc_ref[...] + d_ref[...] * us_ref[...]
        y_ref[...] = y
        z, _ = _gelu_parts(y)
        gl = jnp.dot(z.astype(BF16), w_ref[...], preferred_element_type=F32) + b_ref[...]
        s_ref[...] = (z * _sig(gl)).astype(BF16)

    return _pcall(body, name="ssm_post", out_shape=(_sds((T, N), F32), _sds((T, N), BF16)), grid=(T // tm,),
                  in_specs=[row, row, vec, mat, vec], out_specs=(row, row), dims=("parallel",))(yc, us, dsk, wglu, bglu)


def _ssm_post_bwd(y5, us, ds, dsk, wglu, bglu):
    T, N = y5.shape
    tm = min(T, 1024)
    row = pl.BlockSpec((tm, N), lambda i: (i, 0))
    vec = pl.BlockSpec((1, N), lambda i: (0, 0))
    mat = pl.BlockSpec((N, N), lambda i: (0, 0))

    def body(y_ref, us_ref, ds_ref, d_ref, w_ref, b_ref, dy_ref, dd_ref, db_ref, dw_ref):
        @pl.when(pl.program_id(0) == 0)
        def _():
            dd_ref[...] = jnp.zeros_like(dd_ref)
            db_ref[...] = jnp.zeros_like(db_ref)
            dw_ref[...] = jnp.zeros_like(dw_ref)

        y = y_ref[...]
        z, t = _gelu_parts(y)
        zb = z.astype(BF16)
        gl = jnp.dot(zb, w_ref[...], preferred_element_type=F32) + b_ref[...]
        sg = _sig(gl)
        ds = ds_ref[...]
        dgl = ds * z * sg * (1.0 - sg)
        dglb = dgl.astype(BF16)
        dz = ds * sg + lax.dot_general(dglb, w_ref[...], (((1,), (1,)), ((), ())), preferred_element_type=F32)
        dgelu = 0.5 * (1.0 + t) + 0.5 * y * (1.0 - t * t) * GELU_K * (1.0 + 3.0 * GELU_C * y * y)
        dy = dz * dgelu
        dy_ref[...] = dy
        dd_ref[...] += jnp.sum(dy * us_ref[...], axis=0, keepdims=True)
        db_ref[...] += jnp.sum(dgl, axis=0, keepdims=True)
        dw_ref[...] += lax.dot_general(zb, dglb, (((0,), (0,)), ((), ())), preferred_element_type=F32)

    return _pcall(body, name="ssm_post_bwd",
                  out_shape=(_sds((T, N), F32), _sds((1, N), F32), _sds((1, N), F32), _sds((N, N), F32)),
                  grid=(T // tm,), in_specs=[row, row, row, vec, mat, vec], out_specs=(row, vec, vec, mat),
                  dims=("arbitrary",))(y5, us, ds, dsk, wglu, bglu)


def _add_scaled_cast(a, b, s):
    T, N = a.shape
    tm = min(T, 1024)
    row = pl.BlockSpec((tm, N), lambda i: (i, 0))

    def body(a_ref, b_ref, s_ref, o_ref):
        o_ref[...] = (a_ref[...] + s_ref[...] * b_ref[...]).astype(BF16)

    return _pcall(body, name="add_scaled_cast", out_shape=_sds((T, N), BF16), grid=(T // tm,),
                  in_specs=[row, row, pl.BlockSpec((1, N), lambda i: (0, 0))], out_specs=row, dims=("parallel",))(a, b, s)


GATE_TILE = 256
GATE_ATT_BLOCK0 = (3 * ATT_WIDTH + SSM_WIDTH) // GATE_TILE
GATE_SSM_BLOCK0 = (3 * ATT_WIDTH + SSM_WIDTH + D_MODEL) // GATE_TILE


def _merge(proj, y_att, y_ssm, b_gate):
    T = proj.shape[0]
    tm = min(T, 1024)
    nj = D_MODEL // GATE_TILE
    ga = pl.BlockSpec((tm, GATE_TILE), lambda i, j: (i, GATE_ATT_BLOCK0 + j))
    gs = pl.BlockSpec((tm, GATE_TILE), lambda i, j: (i, GATE_SSM_BLOCK0 + j))
    yy = pl.BlockSpec((tm, GATE_TILE), lambda i, j: (i, j))
    ba = pl.BlockSpec((1, GATE_TILE), lambda i, j: (0, j))
    bs = pl.BlockSpec((1, GATE_TILE), lambda i, j: (0, nj + j))

    def body(ga_ref, gs_ref, ya_ref, ys_ref, ba_ref, bs_ref, o_ref):
        o_ref[...] = (_sig(ga_ref[...] + ba_ref[...]) * ya_ref[...]
                      + _sig(gs_ref[...] + bs_ref[...]) * ys_ref[...]).astype(BF16)

    return _pcall(body, name="merge", out_shape=_sds((T, D_MODEL), BF16), grid=(T // tm, nj),
                  in_specs=[ga, gs, yy, yy, ba, bs], out_specs=yy, dims=("parallel", "parallel"))(
        proj, proj, y_att, y_ssm, b_gate, b_gate)


def _merge_bwd(proj, y_att, y_ssm, b_gate, dmerged):
    T = proj.shape[0]
    tm = min(T, 1024)
    nj = D_MODEL // GATE_TILE
    ga = pl.BlockSpec((tm, GATE_TILE), lambda j, i: (i, GATE_ATT_BLOCK0 + j))
    gs = pl.BlockSpec((tm, GATE_TILE), lambda j, i: (i, GATE_SSM_BLOCK0 + j))
    yy = pl.BlockSpec((tm, GATE_TILE), lambda j, i: (i, j))
    ba = pl.BlockSpec((1, GATE_TILE), lambda j, i: (0, j))
    bs = pl.BlockSpec((1, GATE_TILE), lambda j, i: (0, nj + j))

    def body(ga_ref, gs_ref, ya_ref, ys_ref, ba_ref, bs_ref, dm_ref, dya_ref, dys_ref, dga_ref, dgs_ref, dba_ref, dbs_ref):
        @pl.when(pl.program_id(1) == 0)
        def _():
            dba_ref[...] = jnp.zeros_like(dba_ref)
            dbs_ref[...] = jnp.zeros_like(dbs_ref)

        dm = dm_ref[...]
        sa = _sig(ga_ref[...] + ba_ref[...])
        ss = _sig(gs_ref[...] + bs_ref[...])
        dya_ref[...] = (dm * sa).astype(BF16)
        dys_ref[...] = (dm * ss).astype(BF16)
        dga = dm * ya_ref[...] * sa * (1.0 - sa)
        dgs = dm * ys_ref[...] * ss * (1.0 - ss)
        dga_ref[...] = dga.astype(BF16)
        dgs_ref[...] = dgs.astype(BF16)
        dba_ref[...] += jnp.sum(dga, axis=0, keepdims=True)
        dbs_ref[...] += jnp.sum(dgs, axis=0, keepdims=True)

    big = _sds((T, D_MODEL), BF16)
    vec = _sds((1, D_MODEL), F32)
    return _pcall(body, name="merge_bwd", out_shape=(big, big, big, big, vec, vec), grid=(nj, T // tm),
                  in_specs=[ga, gs, yy, yy, ba, bs, yy], out_specs=(yy, yy, yy, yy, ba, ba),
                  dims=("arbitrary", "arbitrary"))(proj, proj, y_att, y_ssm, b_gate, b_gate, dmerged)


CONV_TILE = 256


def _conv_pre(a, w_ref, b_ref, row):
    conv = b_ref[...] + w_ref[0:1, :] * a
    shifted = []
    for j in (1, 2):
        sh = jnp.where(row >= j, pltpu.roll(a, j, 0), 0.0)
        shifted.append(sh)
        conv = conv + w_ref[j:j + 1, :] * sh
    return conv, shifted


def _conv_act(up3, w_conv, b_conv):
    B, S, _ = up3.shape
    nj = D_FF // CONV_TILE
    a_spec = pl.BlockSpec((1, S, CONV_TILE), lambda b, j: (b, 0, j))
    v_spec = pl.BlockSpec((1, S, CONV_TILE), lambda b, j: (b, 0, nj + j))
    w_spec = pl.BlockSpec((3, CONV_TILE), lambda b, j: (0, j))
    b_spec = pl.BlockSpec((1, CONV_TILE), lambda b, j: (0, j))

    def body(a_ref, v_ref, w_ref, b_ref, o_ref):
        a = a_ref[0].astype(F32)
        row = lax.broadcasted_iota(jnp.int32, a.shape, 0)
        conv, _ = _conv_pre(a, w_ref, b_ref, row)
        o_ref[0] = (conv * _sig(conv) * v_ref[0]).astype(BF16)

    return _pcall(body, name="conv_act", out_shape=_sds((B, S, D_FF), BF16), grid=(B, nj),
                  in_specs=[a_spec, v_spec, w_spec, b_spec], out_specs=a_spec, dims=("parallel", "parallel"))(
        up3, up3, w_conv, b_conv)


def _conv_bwd(up3, dact3, w_conv, b_conv):
    B, S, _ = up3.shape
    nj = D_FF // CONV_TILE
    a_spec = pl.BlockSpec((1, S, CONV_TILE), lambda j, b: (b, 0, j))
    v_spec = pl.BlockSpec((1, S, CONV_TILE), lambda j, b: (b, 0, nj + j))
    w_spec = pl.BlockSpec((3, CONV_TILE), lambda j, b: (0, j))
    b_spec = pl.BlockSpec((1, CONV_TILE), lambda j, b: (0, j))

    def body(a_ref, v_ref, d_ref, w_ref, b_ref, da_ref, dv_ref, dw_ref, db_ref):
        @pl.when(pl.program_id(1) == 0)
        def _():
            dw_ref[...] = jnp.zeros_like(dw_ref)
            db_ref[...] = jnp.zeros_like(db_ref)

        a = a_ref[0].astype(F32)
        d = d_ref[0]
        row = lax.broadcasted_iota(jnp.int32, a.shape, 0)
        conv, shifted = _conv_pre(a, w_ref, b_ref, row)
        sg = _sig(conv)
        dv_ref[0] = (d * conv * sg).astype(BF16)
        dconv = d * v_ref[0] * (sg * (1.0 + conv * (1.0 - sg)))
        da = w_ref[0:1, :] * dconv
        for j in (1, 2):
            da = da + w_ref[j:j + 1, :] * jnp.where(row < S - j, pltpu.roll(dconv, S - j, 0), 0.0)
        da_ref[0] = da.astype(BF16)
        db_ref[...] += jnp.sum(dconv, axis=0, keepdims=True)
        dw_ref[0:1, :] += jnp.sum(dconv * a, axis=0, keepdims=True)
        dw_ref[1:2, :] += jnp.sum(dconv * shifted[0], axis=0, keepdims=True)
        dw_ref[2:3, :] += jnp.sum(dconv * shifted[1], axis=0, keepdims=True)

    big = _sds((B, S, D_FF), BF16)
    return _pcall(body, name="conv_bwd", out_shape=(big, big, _sds((3, D_FF), F32), _sds((1, D_FF), F32)),
                  grid=(nj, B), in_specs=[a_spec, v_spec, a_spec, w_spec, b_spec],
                  out_specs=(a_spec, a_spec, w_spec, b_spec), dims=("arbitrary", "arbitrary"))(
        up3, up3, dact3, w_conv, b_conv)


def _rows_tile(r, cap=640):
    for t in range(min(r, cap) - min(r, cap) % 8, 7, -8):
        if r % t == 0:
            return t
    return r


def _add2(a, b, out_dtype):
    R, N = a.shape
    tr = _rows_tile(R)
    spec = pl.BlockSpec((tr, N), lambda i: (i, 0))

    def body(a_ref, b_ref, o_ref):
        o_ref[...] = (a_ref[...] + b_ref[...]).astype(out_dtype)

    return _pcall(body, name="add2", out_shape=_sds((R, N), out_dtype), grid=(R // tr,), in_specs=[spec, spec],
                  out_specs=spec, dims=("parallel",))(a, b)


def _sum_slots(q, name):
    n, R, N = q.shape
    tr = _rows_tile(R)

    def body(q_ref, o_ref):
        acc = q_ref[0].astype(F32)
        for s in range(1, n):
            acc = acc + q_ref[s].astype(F32)
        o_ref[...] = acc

    return _pcall(body, name=name, out_shape=_sds((R, N), F32), grid=(R // tr,),
                  in_specs=[pl.BlockSpec((n, tr, N), lambda i: (0, i, 0))], out_specs=pl.BlockSpec((tr, N), lambda i: (i, 0)),
                  dims=("parallel",))(q)


def _adamw(w, g, m, v, name):
    R, N = w.shape
    tr = _rows_tile(R) if R * N * 4 > (1 << 20) else R
    tr = min(tr, 256) if R % 256 == 0 and R > 256 else tr
    spec = pl.BlockSpec((tr, N), lambda i: (i, 0))
    bc1 = 1.0 - ADAM_B1 ** ADAM_STEP
    bc2 = 1.0 - ADAM_B2 ** ADAM_STEP

    def body(w_ref, g_ref, m_ref, v_ref, d_ref, nm_ref, nv_ref):
        g = g_ref[...]
        m = ADAM_B1 * m_ref[...] + (1.0 - ADAM_B1) * g
        v = ADAM_B2 * v_ref[...] + (1.0 - ADAM_B2) * (g * g)
        nm_ref[...] = m
        nv_ref[...] = v
        d_ref[...] = -ADAM_LR * ((m / bc1) / (jnp.sqrt(v / bc2) + ADAM_EPS) + ADAM_WD * w_ref[...])

    shp = _sds((R, N), F32)
    return _pcall(body, name=name, out_shape=(shp, shp, shp), grid=(R // tr,), in_specs=[spec] * 4,
                  out_specs=(spec, spec, spec), dims=("parallel",))(w, g, m, v)


_GROUP_MASKS = {
    "all": [(dx, dy, dc) for dx in (0, 1) for dy in (0, 1) for dc in (0, 1) if (dx, dy, dc) != (0, 0, 0)],
    "xy": [(1, 0, 0), (0, 1, 0), (1, 1, 0)],
    "c": [(0, 0, 1)],
}
_GROUP_SLOTS = {"all": 8, "xy": 4, "c": 2}


def _group_slot(group, x, y, c):
    return {"all": 4 * x + 2 * y + c, "xy": 2 * x + y, "c": c}[group]


def _flip(v, d):
    return 1 - v if d else v


def _exchange(arr, group, mode, name):
    masks = _GROUP_MASKS[group]
    n = len(masks)
    if mode == "gather":
        out_shape = (_GROUP_SLOTS[group],) + arr.shape
    elif mode == "scatter":
        assert arr.shape[0] == _GROUP_SLOTS[group]
        out_shape = arr.shape
    elif mode == "swap":
        assert group == "c"
        out_shape = arr.shape
    else:
        assert group == "c"
        half = arr.shape[1] // 2
        out_shape = (arr.shape[0], half, arr.shape[2])

    def body(x_ref, o_ref, send_sems, recv_sems, local_sem):
        x, y, c = lax.axis_index("x"), lax.axis_index("y"), lax.axis_index("c")
        me = _group_slot(group, x, y, c)
        local = None
        if mode == "gather":
            local = pltpu.make_async_copy(x_ref, o_ref.at[me], local_sem)
        elif mode == "scatter":
            local = pltpu.make_async_copy(x_ref.at[me], o_ref.at[me], local_sem)
        if local is not None:
            local.start()
        copies = []
        for k, (dx, dy, dc) in enumerate(masks):
            px, py, pc = _flip(x, dx), _flip(y, dy), _flip(c, dc)
            if mode == "gather":
                src, dst = x_ref, o_ref.at[me]
            elif mode == "scatter":
                src, dst = x_ref.at[_group_slot(group, px, py, pc)], o_ref.at[me]
            elif mode == "swap":
                src, dst = x_ref, o_ref
            else:
                src, dst = x_ref.at[:, pl.ds(pl.multiple_of(pc * half, 8), half), :], o_ref
            cp = pltpu.make_async_remote_copy(src_ref=src, dst_ref=dst, send_sem=send_sems.at[k], recv_sem=recv_sems.at[k],
                                              device_id=(px, py, pc), device_id_type=pl.DeviceIdType.MESH)
            cp.start()
            copies.append(cp)
        for cp in copies:
            cp.wait()
        if local is not None:
            local.wait()

    anyspec = pl.BlockSpec(memory_space=pl.ANY)
    return pl.pallas_call(body, name=name, out_shape=_sds(out_shape, arr.dtype), in_specs=[anyspec], out_specs=anyspec,
                          scratch_shapes=[pltpu.SemaphoreType.DMA((n,)), pltpu.SemaphoreType.DMA((n,)),
                                          pltpu.SemaphoreType.DMA(())])(arr)


BIG = (("w_in", (D_MODEL, IN_WIDTH), 1), ("w_out", (D_MODEL, D_MODEL), 0), ("w_up", (D_MODEL, 2 * D_FF), 1),
       ("w_down", (D_FF, D_MODEL), 0), ("w_proj_att", (ATT_WIDTH, D_MODEL), 1), ("w_proj_ssm", (SSM_WIDTH, D_MODEL), 1),
       ("w_glu", (SSM_WIDTH, SSM_WIDTH), 0))
N_XY = 4


def _big_rows(shape):
    return shape[0] * shape[1] // N_XY // LANES


FLAT_ROWS = sum(_big_rows(s) for _, s, _ in BIG)


def _shard_shape(shape, axis):
    return (shape[0] // N_XY, shape[1]) if axis == 0 else (shape[0], shape[1] // N_XY)


def _flatten_shards(shards):
    return jnp.concatenate([shards[n].reshape(_big_rows(s), LANES) for n, s, _ in BIG], axis=0)


def _unflatten_shard(flat):
    out, r = {}, 0
    for n, s, ax in BIG:
        k = _big_rows(s)
        out[n] = flat[r:r + k].reshape(_shard_shape(s, ax))
        r += k
    return out


def _unflatten_full(flat4):
    out, r = {}, 0
    for n, s, ax in BIG:
        k = _big_rows(s)
        sh = _shard_shape(s, ax)
        t = flat4[:, r:r + k].reshape((N_XY,) + sh)
        out[n] = t.reshape(s) if ax == 0 else t.transpose(1, 0, 2).reshape(s)
        r += k
    return out


def _flatten_full(full):
    parts = []
    for n, s, ax in BIG:
        sh = _shard_shape(s, ax)
        t = full[n]
        t = t.reshape((N_XY,) + sh) if ax == 0 else t.reshape(s[0], N_XY, sh[1]).transpose(1, 0, 2)
        parts.append(t.reshape(N_XY, _big_rows(s), LANES))
    return jnp.concatenate(parts, axis=1)


def _pack_rows(arrs):
    rows, counts = [], []
    for a in arrs:
        f = a.reshape(-1)
        k = -(-f.shape[0] // LANES)
        rows.append(jnp.pad(f, (0, k * LANES - f.shape[0])).reshape(k, LANES))
        counts.append(k)
    return jnp.concatenate(rows, axis=0), counts


def _unpack_rows(buf, shapes):
    out, r = [], 0
    for s in shapes:
        size = int(np.prod(s))
        k = -(-size // LANES)
        out.append(buf[r:r + k].reshape(-1)[:size].reshape(s))
        r += k
    return out


def _lanes_from_groups(a):
    return a.transpose(2, 0, 1).reshape(SSM_GROUP_CH, SSM_LANES)


def _groups_from_lanes(a):
    return a.reshape(SSM_GROUP_CH, SSM_GROUPS, SSM_STATE).transpose(1, 2, 0)


def _local_step(x3, mod, tgt3, W, P):
    B, S, _ = x3.shape
    T = B * S
    seq_blocks = S // ATT_BLOCK
    sh1, sc1, gt1, sh2, sc2, gt2 = [m.reshape(B, 1, D_MODEL) for m in jnp.split(mod, 6, axis=-1)]
    g_mix, g_ffn, g_final = P["g_mix"].reshape(1, D_MODEL), P["g_ffn"].reshape(1, D_MODEL), P["g_final"].reshape(1, D_MODEL)
    b_gate = P["b_gate"].reshape(1, 2 * D_MODEL)
    d_skip, b_glu = P["d_skip"].reshape(1, SSM_WIDTH), P["b_glu"].reshape(1, SSM_WIDTH)
    w_conv, b_conv = P["w_conv"], P["b_conv"].reshape(1, D_FF)

    u1 = _norm_mod(x3, g_mix, sc1, sh1).reshape(T, D_MODEL)
    proj = _mm(u1, W["w_in"], name="mm_proj", out_dtype=BF16)
    proj3 = proj.reshape(B, S, IN_WIDTH)
    us = proj[:, 3 * ATT_WIDTH:3 * ATT_WIDTH + SSM_WIDTH]
    o_att3, lse4 = _attention_fwd(proj3, seq_blocks)
    o_att = o_att3.reshape(T, ATT_WIDTH)
    y_att = _mm(o_att, W["w_proj_att"], name="mm_proj_att")

    lr = P["a_re"].reshape(1, SSM_LANES)
    li = P["a_im"].reshape(1, SSM_LANES)
    ldt = jnp.repeat(P["log_dt"], SSM_STATE).reshape(1, SSM_LANES)
    br, bi = _lanes_from_groups(P["b_re"]), _lanes_from_groups(P["b_im"])
    cr = P["c_re"].transpose(1, 0, 2).reshape(SSM_GROUP_CH, SSM_LANES)
    ci = P["c_im"].transpose(1, 0, 2).reshape(SSM_GROUP_CH, SSM_LANES)
    abar, w_bu, w_c = _ssm_params(lr, li, ldt, br, bi, cr, ci)
    bu = _mm(us, w_bu, name="mm_bu")
    xs = _scan_fwd(bu.reshape(B, S, 2 * SSM_LANES), abar).reshape(T, 2 * SSM_LANES)
    y_core = _mm(xs, w_c, tb=True, name="mm_ssm_out")
    y5, s_out = _ssm_post(y_core, us, d_skip, W["w_glu"], b_glu)
    y_ssm = _mm(s_out, W["w_proj_ssm"], name="mm_proj_ssm")

    merged = _merge(proj, y_att, y_ssm, b_gate)
    mix = _mm(merged, W["w_out"], name="mm_out")
    mix3 = mix.reshape(B, S, D_MODEL)

    h1, u2 = _resid_norm_mod(x3, mix3, gt1, g_ffn, sc2, sh2)
    u2 = u2.reshape(T, D_MODEL)
    up3 = _mm(u2, W["w_up"], name="mm_up", out_dtype=BF16).reshape(B, S, 2 * D_FF)
    act = _conv_act(up3, w_conv, b_conv).reshape(T, D_FF)
    ffn3 = _mm(act, W["w_down"], name="mm_down").reshape(B, S, D_MODEL)
    dh2, dffn, dgt2, dg_final, loss = _final_loss(h1, ffn3, tgt3, gt2, g_final)

    dffn = dffn.reshape(T, D_MODEL)
    gw = {}
    gw["w_down"] = _mm(act, dffn, ta=True, name="mm_dw_down")
    dact3 = _mm(dffn, W["w_down"], tb=True, name="mm_dact").reshape(B, S, D_FF)
    da3, dval3, dw_conv, db_conv = _conv_bwd(up3, dact3, w_conv, b_conv)
    dup = jnp.concatenate([da3.reshape(T, D_FF), dval3.reshape(T, D_FF)], axis=1)
    gw["w_up"] = _mm(u2, dup, ta=True, name="mm_dw_up")
    du2 = _mm(dup, W["w_up"], tb=True, name="mm_du2").reshape(B, S, D_MODEL)
    dh1, dsh2, dsc2, dg_ffn, dgt1, dmix = _norm_bwd(h1, du2, dh2, g_ffn, sc2, "norm_bwd2", mix3=mix3, gt=gt1)

    dmix = dmix.reshape(T, D_MODEL)
    gw["w_out"] = _mm(merged, dmix, ta=True, name="mm_dw_out")
    dmerged = _mm(dmix, W["w_out"], tb=True, name="mm_dmerged")
    dy_att, dy_ssm, dga, dgs, db_att, db_ssm = _merge_bwd(proj, y_att, y_ssm, b_gate, dmerged)

    gw["w_proj_ssm"] = _mm(s_out, dy_ssm, ta=True, name="mm_dw_proj_ssm")
    ds_out = _mm(dy_ssm, W["w_proj_ssm"], tb=True, name="mm_ds_out")
    dy5, dd_skip, db_glu, dw_glu = _ssm_post_bwd(y5, us, ds_out, d_skip, W["w_glu"], b_glu)
    gw["w_glu"] = dw_glu
    dxs = _mm(dy5, w_c, name="mm_dxs")
    dwc = _mm(dy5, xs, ta=True, name="mm_dwc")
    g3, dab = _scan_bwd(dxs.reshape(B, S, 2 * SSM_LANES), xs.reshape(B, S, 2 * SSM_LANES), abar)
    gs2 = g3.reshape(T, 2 * SSM_LANES)
    dwbu = _mm(us, gs2, ta=True, name="mm_dwbu")
    dus_core = _mm(gs2, w_bu, tb=True, name="mm_dus")
    dus = _add_scaled_cast(dus_core, dy5, d_skip)
    dlr, dli, dldt, dbr, dbi, dcr, dci = _ssm_params_bwd(lr, li, ldt, br, bi, dab, dwbu, dwc)

    gw["w_proj_att"] = _mm(o_att, dy_att, ta=True, name="mm_dw_proj_att")
    do_att = _mm(dy_att, W["w_proj_att"], tb=True, out_dtype=BF16, name="mm_do_att")
    dq3, dk3, dv3 = _attention_bwd(proj3, do_att.reshape(B, S, ATT_WIDTH), o_att3, lse4, seq_blocks)
    dproj = jnp.concatenate([t.reshape(T, ATT_WIDTH) for t in (dq3, dk3, dv3)] + [dus, dga, dgs], axis=1)
    gw["w_in"] = _mm(u1, dproj, ta=True, name="mm_dw_in")
    du1 = _mm(dproj, W["w_in"], tb=True, name="mm_du1").reshape(B, S, D_MODEL)
    dx, dsh1, dsc1, dg_mix = _norm_bwd(x3, du1, dh1, g_mix, sc1, "norm_bwd1")

    dmod = jnp.concatenate([t.reshape(B, D_MODEL) for t in (dsh1, dsc1, dgt1, dsh2, dsc2, dgt2)], axis=1)
    gs = dict(
        g_mix=dg_mix.reshape(D_MODEL), b_gate=jnp.concatenate([db_att, db_ssm], axis=1).reshape(2 * D_MODEL),
        a_re=dlr.reshape(SSM_GROUPS, SSM_STATE), a_im=dli.reshape(SSM_GROUPS, SSM_STATE), log_dt=dldt[0, :SSM_GROUPS],
        b_re=_groups_from_lanes(dbr), b_im=_groups_from_lanes(dbi),
        c_re=dcr.reshape(SSM_GROUP_CH, SSM_GROUPS, SSM_STATE).transpose(1, 0, 2),
        c_im=dci.reshape(SSM_GROUP_CH, SSM_GROUPS, SSM_STATE).transpose(1, 0, 2),
        d_skip=dd_skip.reshape(SSM_WIDTH), b_glu=db_glu.reshape(SSM_WIDTH), g_ffn=dg_ffn.reshape(D_MODEL),
        w_conv=dw_conv, b_conv=db_conv.reshape(D_FF), g_final=dg_final.reshape(D_MODEL))
    return loss, dx, dmod, gw, gs


WEIGHTS = ['w_ada', 'b_ada', 'g_mix', 'w_in', 'b_gate', 'a_re', 'a_im', 'log_dt', 'b_re', 'b_im', 'c_re', 'c_im', 'd_skip',
           'w_glu', 'b_glu', 'w_proj_att', 'w_proj_ssm', 'w_out', 'g_ffn', 'w_up', 'w_conv', 'b_conv', 'w_down', 'g_final']
SMALL = ['g_mix', 'b_gate', 'a_re', 'a_im', 'log_dt', 'b_re', 'b_im', 'c_re', 'c_im', 'd_skip', 'b_glu', 'g_ffn', 'w_conv',
         'b_conv', 'g_final']


def kernel(x, c, w_ada, b_ada, g_mix, w_in, b_gate, a_re, a_im, log_dt, b_re, b_im, c_re, c_im, d_skip, w_glu, b_glu, w_proj_att, w_proj_ssm, w_out, g_ffn, w_up, w_conv, b_conv, w_down, g_final, loss_target, m_w_ada, m_b_ada, m_g_mix, m_w_in, m_b_gate, m_a_re, m_a_im, m_log_dt, m_b_re, m_b_im, m_c_re, m_c_im, m_d_skip, m_w_glu, m_b_glu, m_w_proj_att, m_w_proj_ssm, m_w_out, m_g_ffn, m_w_up, m_w_conv, m_b_conv, m_w_down, m_g_final, v_w_ada, v_b_ada, v_g_mix, v_w_in, v_b_gate, v_a_re, v_a_im, v_log_dt, v_b_re, v_b_im, v_c_re, v_c_im, v_d_skip, v_w_glu, v_b_glu, v_w_proj_att, v_w_proj_ssm, v_w_out, v_g_ffn, v_w_up, v_w_conv, v_b_conv, v_w_down, v_g_final):
    args = dict(locals())
    w = {n: args[n] for n in WEIGHTS}
    m = {n: args["m_" + n] for n in WEIGHTS}
    v = {n: args["v_" + n] for n in WEIGHTS}
    B, S, _ = x.shape
    ix, iy, ic = lax.axis_index("x"), lax.axis_index("y"), lax.axis_index("c")
    chip = 2 * ix + iy
    half = FLAT_ROWS // 2
    ada_cols = w_ada.shape[2]

    c_all = _exchange(c, "all", "gather", "gather_c").reshape(8 * B, D_MODEL)
    b_cols = lax.dynamic_slice_in_dim(b_ada, chip * ada_cols, ada_cols, axis=1)
    mod_cols = _ada_fwd(c_all, w_ada[0], b_cols)
    mod_all = _exchange(mod_cols, "xy", "gather", "gather_mod")
    mod_all = mod_all.transpose(1, 0, 2).reshape(8 * B, 6 * D_MODEL)
    mod = lax.dynamic_slice_in_dim(mod_all, (4 * ix + 2 * iy + ic) * B, B, axis=0)

    flat = _flatten_shards({n: w[n][0] for n, _, _ in BIG}).astype(BF16)
    mine = lax.dynamic_slice_in_dim(flat, ic * half, half, axis=0)
    halves = _exchange(mine, "xy", "gather", "gather_w_chips")
    others = _exchange(halves, "c", "swap", "gather_w_cores")
    south = ic == 0
    W = _unflatten_full(jnp.concatenate([jnp.where(south, halves, others), jnp.where(south, others, halves)], axis=1))

    wc_all = _exchange(w_conv[0], "xy", "gather", "gather_w_conv")
    P = {n: w[n][0] for n in SMALL if n not in ("w_conv", "g_final")}
    P["w_conv"] = wc_all.transpose(1, 0, 2).reshape(3, D_FF)
    P["g_final"] = g_final

    loss, dx, dmod, gw, gs = _local_step(x, mod, loss_target, W, P)

    loss = lax.psum(loss[0, 0], MESH_AXES)

    small_shapes = [gs[n].shape for n in SMALL]
    packed, counts = _pack_rows([gs[n] for n in SMALL] + [dmod])
    n_small = sum(counts[:-1])
    gathered = _exchange(packed, "all", "gather", "gather_small")
    small_sum = _sum_slots(gathered[:, :n_small], "sum_small")
    g_small = dict(zip(SMALL, _unpack_rows(small_sum, small_shapes)))
    dmod_all = gathered[:, n_small:].reshape(8, -1)[:, :B * 6 * D_MODEL].reshape(8 * B, 6 * D_MODEL)
    dmod_cols = lax.dynamic_slice_in_dim(dmod_all, chip * ada_cols, ada_cols, axis=1)
    g_w_ada, g_b_ada = _ada_bwd(c_all, dmod_all, dmod_cols)

    G = _flatten_full(gw)
    theirs = _exchange(G, "c", "half", "reduce_cores")
    ours = lax.dynamic_slice_in_dim(G, ic * half, half, axis=1)
    pair = _add2(ours.reshape(N_XY * half, LANES), theirs.reshape(N_XY * half, LANES), BF16).reshape(N_XY, half, LANES)
    parts = _exchange(pair, "xy", "scatter", "reduce_chips")
    red = _sum_slots(parts, "sum_chips")
    red_sib = _exchange(red, "c", "swap", "share_cores")
    g_flat = jnp.concatenate([jnp.where(south, red, red_sib), jnp.where(south, red_sib, red)], axis=0)
    g_big = _unflatten_shard(g_flat)

    grads = {"w_ada": g_w_ada[None], "b_ada": g_b_ada}
    for n, _, _ in BIG:
        grads[n] = g_big[n][None]
    wc_cols = w_conv.shape[2]
    for n in SMALL:
        g = g_small[n]
        if n == "w_conv":
            g = lax.dynamic_slice_in_dim(g, chip * wc_cols, wc_cols, axis=1)
        grads[n] = g.reshape(w[n].shape)

    delta, new_m, new_v = {}, {}, {}
    for n in ["w_ada"] + [b for b, _, _ in BIG]:
        shp = w[n].shape
        d2, m2, v2 = _adamw(w[n][0], grads[n][0], m[n][0], v[n][0], "adamw_" + n)
        delta[n], new_m[n], new_v[n] = d2.reshape(shp), m2.reshape(shp), v2.reshape(shp)
    rest = ["b_ada"] + SMALL
    shapes = [w[n].shape for n in rest]
    pw, _ = _pack_rows([w[n] for n in rest])
    pg, _ = _pack_rows([grads[n] for n in rest])
    pm, _ = _pack_rows([m[n] for n in rest])
    pv, _ = _pack_rows([v[n] for n in rest])
    d2, m2, v2 = _adamw(pw, pg, pm, pv, "adamw_small")
    for n, dd, mm, vv in zip(rest, _unpack_rows(d2, shapes), _unpack_rows(m2, shapes), _unpack_rows(v2, shapes)):
        delta[n], new_m[n], new_v[n] = dd, mm, vv

    return (loss, dx, *[grads[n] for n in WEIGHTS], *[delta[n] for n in WEIGHTS], *[new_m[n] for n in WEIGHTS],
            *[new_v[n] for n in WEIGHTS])
```

```python
import functools
import math

import numpy as np
import jax
import jax.numpy as jnp
from jax import lax
from jax.experimental import pallas as pl
from jax.experimental.pallas import tpu as pltpu

F32, BF16 = jnp.float32, jnp.bfloat16

D_MODEL = 1024
N_HEADS = 8
HEAD_DIM = 64
ATT_WIDTH = 512
SSM_GROUPS = 16
SSM_GROUP_CH = 16
SSM_WIDTH = 256
SSM_STATE = 64
SSM_LANES = SSM_GROUPS * SSM_STATE
D_FF = 2048
IN_WIDTH = 3 * ATT_WIDTH + SSM_WIDTH + 2 * D_MODEL
ATT_BLOCK = 128
N_PATTERNS = 3
EPS = 1e-6
NEG_INF = -1e30

ADAM_LR, ADAM_B1, ADAM_B2, ADAM_EPS, ADAM_WD, ADAM_STEP = 0.001, 0.9, 0.999, 1e-08, 0.01, 10

V7X_VMEM_LIMIT_BYTES = 56 * 1024 * 1024
LANES = 1024

MESH_AXES = ("x", "y", "c")


def _pcall(body, *, name, out_shape, grid=(), in_specs=None, out_specs=None, scratch_shapes=(), dims=None):
    params = dict(vmem_limit_bytes=V7X_VMEM_LIMIT_BYTES)
    if dims is not None:
        params["dimension_semantics"] = dims
    specs = {}
    if in_specs is not None:
        specs = dict(grid=grid, in_specs=in_specs, out_specs=out_specs)
    return pl.pallas_call(body, name=name, out_shape=out_shape, scratch_shapes=scratch_shapes,
                          compiler_params=pltpu.CompilerParams(**params), **specs)


def _sds(shape, dtype):
    return jax.ShapeDtypeStruct(tuple(shape), dtype)


def _tile(n, target):
    if n <= target:
        return n
    for t in range(target - target % 128, 0, -128):
        if n % t == 0:
            return t
    raise ValueError((n, target))


def _sig(v):
    return 1.0 / (1.0 + jnp.exp(-v))


def _mm(a, b, *, name, ta=False, tb=False, out_dtype=F32, tm=2048, tn=1024, tk=1024):
    if ta:
        K, M = a.shape
    else:
        M, K = a.shape
    if tb:
        N, K2 = b.shape
    else:
        K2, N = b.shape
    assert K == K2, (a.shape, b.shape)
    tm, tn, tk = _tile(M, tm), _tile(N, tn), _tile(K, tk)
    nk = K // tk
    a_spec = pl.BlockSpec((tk, tm), lambda i, j, k: (k, i)) if ta else pl.BlockSpec((tm, tk), lambda i, j, k: (i, k))
    b_spec = pl.BlockSpec((tn, tk), lambda i, j, k: (j, k)) if tb else pl.BlockSpec((tk, tn), lambda i, j, k: (k, j))
    dn = (((0 if ta else 1,), (1 if tb else 0,)), ((), ()))

    def body(a_ref, b_ref, o_ref, acc_ref):
        k = pl.program_id(2)

        @pl.when(k == 0)
        def _():
            acc_ref[...] = jnp.zeros_like(acc_ref)

        acc_ref[...] += lax.dot_general(a_ref[...].astype(BF16), b_ref[...].astype(BF16), dn,
                                        preferred_element_type=F32)

        @pl.when(k == nk - 1)
        def _():
            o_ref[...] = acc_ref[...].astype(out_dtype)

    def body_single(a_ref, b_ref, o_ref):
        o_ref[...] = lax.dot_general(a_ref[...].astype(BF16), b_ref[...].astype(BF16), dn,
                                     preferred_element_type=F32).astype(out_dtype)

    return _pcall(body_single if nk == 1 else body, name=name, out_shape=_sds((M, N), out_dtype),
                  grid=(M // tm, N // tn, nk), in_specs=[a_spec, b_spec],
                  out_specs=pl.BlockSpec((tm, tn), lambda i, j, k: (i, j)),
                  scratch_shapes=[] if nk == 1 else [pltpu.VMEM((tm, tn), F32)],
                  dims=("parallel", "parallel", "arbitrary"))(a, b)


def _ada_fwd(c_all, w_ada, b_ada_cols):
    n = w_ada.shape[1]

    def body(c_ref, w_ref, b_ref, o_ref):
        c = c_ref[...]
        act = c * _sig(c)
        o_ref[...] = jnp.dot(act.astype(BF16), w_ref[...].astype(BF16), preferred_element_type=F32) + b_ref[...]

    return _pcall(body, name="ada_fwd", out_shape=_sds((c_all.shape[0], n), F32))(c_all, w_ada, b_ada_cols)


def _ada_bwd(c_all, dmod_all, dmod_cols):
    n = dmod_cols.shape[1]

    def body(c_ref, da_ref, dc_ref, gw_ref, gb_ref):
        c = c_ref[...]
        act = c * _sig(c)
        gw_ref[...] = lax.dot_general(act, dc_ref[...], (((0,), (0,)), ((), ())), preferred_element_type=F32,
                                      precision=lax.Precision.HIGHEST)
        gb_ref[...] = jnp.sum(da_ref[...], axis=0, keepdims=True)

    return _pcall(body, name="ada_bwd", out_shape=(_sds((D_MODEL, n), F32), _sds((1, dmod_all.shape[1]), F32)))(
        c_all, dmod_all, dmod_cols)


ROW_TILE = 512


def _row_specs(B, S):
    ts = min(S, ROW_TILE)
    row = pl.BlockSpec((1, ts, D_MODEL), lambda b, s: (b, s, 0))
    bvec = pl.BlockSpec((1, 1, D_MODEL), lambda b, s: (b, 0, 0))
    gvec = pl.BlockSpec((1, D_MODEL), lambda b, s: (0, 0))
    return ts, row, bvec, gvec


def _norm_mod(x3, g, sc, sh):
    B, S, _ = x3.shape
    ts, row, bvec, gvec = _row_specs(B, S)

    def body(x_ref, g_ref, sc_ref, sh_ref, u_ref):
        x = x_ref[0]
        r = lax.rsqrt(jnp.mean(x * x, axis=-1, keepdims=True) + EPS)
        u_ref[0] = ((x * r) * g_ref[...] * (1.0 + sc_ref[0]) + sh_ref[0]).astype(BF16)

    return _pcall(body, name="norm_mod1", out_shape=_sds(x3.shape, BF16), grid=(B, S // ts),
                  in_specs=[row, gvec, bvec, bvec], out_specs=row, dims=("parallel", "parallel"))(x3, g, sc, sh)


def _resid_norm_mod(x3, mix3, gt, g, sc, sh):
    B, S, _ = x3.shape
    ts, row, bvec, gvec = _row_specs(B, S)

    def body(x_ref, m_ref, gt_ref, g_ref, sc_ref, sh_ref, h_ref, u_ref):
        h = x_ref[0] + gt_ref[0] * m_ref[0]
        h_ref[0] = h
        r = lax.rsqrt(jnp.mean(h * h, axis=-1, keepdims=True) + EPS)
        u_ref[0] = ((h * r) * g_ref[...] * (1.0 + sc_ref[0]) + sh_ref[0]).astype(BF16)

    return _pcall(body, name="resid_norm_mod2", out_shape=(_sds(x3.shape, F32), _sds(x3.shape, BF16)),
                  grid=(B, S // ts), in_specs=[row, row, bvec, gvec, bvec, bvec], out_specs=(row, row),
                  dims=("parallel", "parallel"))(x3, mix3, gt, g, sc, sh)


def _norm_bwd(h3, du3, dres3, g, sc, name, mix3=None, gt=None):
    B, S, _ = h3.shape
    ts, row, bvec, gvec = _row_specs(B, S)
    with_gate = mix3 is not None

    def body(*refs):
        if with_gate:
            h_ref, du_ref, dr_ref, g_ref, sc_ref, m_ref, gt_ref, dh_ref, dsh_ref, dsc_ref, dg_ref, dgt_ref, dm_ref = refs
        else:
            h_ref, du_ref, dr_ref, g_ref, sc_ref, dh_ref, dsh_ref, dsc_ref, dg_ref = refs
        b, s = pl.program_id(0), pl.program_id(1)
        h = h_ref[0]
        r = lax.rsqrt(jnp.mean(h * h, axis=-1, keepdims=True) + EPS)
        xn = h * r
        du = du_ref[0].astype(F32)
        g = g_ref[...]
        sc1 = 1.0 + sc_ref[0]
        dxn = du * g * sc1
        dh = dr_ref[0] + r * (dxn - xn * jnp.mean(dxn * xn, axis=-1, keepdims=True))
        dh_ref[0] = dh

        @pl.when(s == 0)
        def _():
            dsh_ref[...] = jnp.zeros_like(dsh_ref)
            dsc_ref[...] = jnp.zeros_like(dsc_ref)
            if with_gate:
                dgt_ref[...] = jnp.zeros_like(dgt_ref)

        @pl.when((s == 0) & (b == 0))
        def _():
            dg_ref[...] = jnp.zeros_like(dg_ref)

        dux = du * xn
        dsh_ref[0] += jnp.sum(du, axis=0, keepdims=True)
        dsc_ref[0] += jnp.sum(dux * g, axis=0, keepdims=True)
        dg_ref[...] += jnp.sum(dux * sc1, axis=0, keepdims=True)
        if with_gate:
            dgt_ref[0] += jnp.sum(dh * m_ref[0], axis=0, keepdims=True)
            dm_ref[0] = (dh * gt_ref[0]).astype(BF16)

    bshape = _sds((B, 1, D_MODEL), F32)
    in_specs = [row, row, row, gvec, bvec]
    out_shape = [_sds(h3.shape, F32), bshape, bshape, _sds((1, D_MODEL), F32)]
    out_specs = [row, bvec, bvec, gvec]
    args = [h3, du3, dres3, g, sc]
    if with_gate:
        in_specs += [row, bvec]
        out_shape += [bshape, _sds(h3.shape, BF16)]
        out_specs += [bvec, row]
        args += [mix3, gt]
    return _pcall(body, name=name, out_shape=tuple(out_shape), grid=(B, S // ts), in_specs=in_specs,
                  out_specs=tuple(out_specs), dims=("arbitrary", "arbitrary"))(*args)


def _final_loss(h1, ffn3, tgt3, gt, gfin):
    B, S, _ = h1.shape
    ts, row, bvec, gvec = _row_specs(B, S)
    one = pl.BlockSpec((1, 1), lambda b, s: (0, 0))

    def body(h_ref, f_ref, t_ref, gt_ref, gf_ref, dh_ref, dff_ref, dgt_ref, dgf_ref, loss_ref):
        b, s = pl.program_id(0), pl.program_id(1)
        f = f_ref[0].astype(F32)
        gtv = gt_ref[0]
        gf = gf_ref[...]
        h2 = h_ref[0] + gtv * f
        r = lax.rsqrt(jnp.mean(h2 * h2, axis=-1, keepdims=True) + EPS)
        n = h2 * r
        e = n * gf - t_ref[0]
        dy = e * (1.0 / D_MODEL)
        dn = dy * gf
        dh2 = r * (dn - n * jnp.mean(dn * n, axis=-1, keepdims=True))
        dh_ref[0] = dh2
        dff_ref[0] = (dh2 * gtv).astype(BF16)

        @pl.when(s == 0)
        def _():
            dgt_ref[...] = jnp.zeros_like(dgt_ref)

        @pl.when((s == 0) & (b == 0))
        def _():
            dgf_ref[...] = jnp.zeros_like(dgf_ref)
            loss_ref[...] = jnp.zeros_like(loss_ref)

        dgt_ref[0] += jnp.sum(dh2 * f, axis=0, keepdims=True)
        dgf_ref[...] += jnp.sum(dy * n, axis=0, keepdims=True)
        rows = jnp.sum(e * e, axis=1, keepdims=True)
        loss_ref[...] += jnp.sum(rows, axis=0, keepdims=True) * (0.5 / D_MODEL)

    return _pcall(body, name="final_loss",
                  out_shape=(_sds(h1.shape, F32), _sds(h1.shape, BF16), _sds((B, 1, D_MODEL), F32),
                             _sds((1, D_MODEL), F32), _sds((1, 1), F32)),
                  grid=(B, S // ts), in_specs=[row, row, row, bvec, gvec], out_specs=(row, row, bvec, gvec, one),
                  dims=("arbitrary", "arbitrary"))(h1, ffn3, tgt3, gt, gfin)


def _att_scores(qh, kc, kp, h, dil, first, a_idx, j_idx):
    scale = HEAD_DIM ** -0.5
    nt = (((1,), (1,)), ((), ()))
    slope = (2.0 ** (-8.0 * (h + 1) / N_HEADS)) * dil
    dist_c = (a_idx - j_idx).astype(F32)
    s_c = lax.dot_general(qh, kc, nt, preferred_element_type=F32) * scale
    s_c = jnp.where(a_idx >= j_idx, s_c - slope * dist_c, NEG_INF)
    s_p = lax.dot_general(qh, kp, nt, preferred_element_type=F32) * scale
    s_p = jnp.where((j_idx >= a_idx) & jnp.logical_not(first), s_p - slope * (dist_c + float(ATT_BLOCK)), NEG_INF)
    return s_c, s_p


def _att_block_consts(seq_blocks):
    p = pl.program_id(0)
    j = pl.program_id(1)
    nb = lax.shift_right_logical(jnp.int32(seq_blocks), 2 * p)
    dil = lax.shift_left(jnp.int32(1), 2 * p).astype(F32)
    a_idx = lax.broadcasted_iota(jnp.int32, (ATT_BLOCK, ATT_BLOCK), 0)
    j_idx = lax.broadcasted_iota(jnp.int32, (ATT_BLOCK, ATT_BLOCK), 1)
    return j, nb, dil, a_idx, j_idx


def _attn_fwd(qb, kb, vb, seq_blocks):
    _, NB, _, _ = qb.shape
    cur = pl.BlockSpec((None, None, ATT_BLOCK, ATT_WIDTH), lambda p, j: (p, j, 0, 0))
    prev = pl.BlockSpec((None, None, ATT_BLOCK, ATT_WIDTH), lambda p, j: (p, jnp.maximum(j - 1, 0), 0, 0))
    lse_spec = pl.BlockSpec((None, None, ATT_BLOCK, N_HEADS), lambda p, j: (p, j, 0, 0))

    def body(q_ref, kc_ref, kp_ref, vc_ref, vp_ref, o_ref, lse_ref):
        j, nb, dil, a_idx, j_idx = _att_block_consts(seq_blocks)
        first = lax.rem(j, nb) == 0
        for h in range(N_HEADS):
            hs = slice(h * HEAD_DIM, (h + 1) * HEAD_DIM)
            s_c, s_p = _att_scores(q_ref[:, hs], kc_ref[:, hs], kp_ref[:, hs], h, dil, first, a_idx, j_idx)
            m = jnp.maximum(jnp.max(s_c, axis=1, keepdims=True), jnp.max(s_p, axis=1, keepdims=True))
            p_c = jnp.exp(s_c - m)
            p_p = jnp.exp(s_p - m)
            den = jnp.sum(p_c, axis=1, keepdims=True) + jnp.sum(p_p, axis=1, keepdims=True)
            o = (jnp.dot(p_c.astype(BF16), vc_ref[:, hs], preferred_element_type=F32)
                 + jnp.dot(p_p.astype(BF16), vp_ref[:, hs], preferred_element_type=F32))
            o_ref[:, hs] = o / den
            lse_ref[:, h:h + 1] = m + jnp.log(den)

    return _pcall(body, name="attn_fwd",
                  out_shape=(_sds(qb.shape, F32), _sds((N_PATTERNS, NB, ATT_BLOCK, N_HEADS), F32)),
                  grid=(N_PATTERNS, NB), in_specs=[cur, cur, prev, cur, prev], out_specs=(cur, lse_spec),
                  dims=("parallel", "parallel"))(qb, kb, kb, vb, vb)


def _attn_combine(o_p, lse_p):
    _, T, _ = o_p.shape
    tm = min(T, 1024)

    def body(o_ref, l_ref, out_ref, lse_ref):
        l0, l1, l2 = l_ref[0], l_ref[1], l_ref[2]
        m = jnp.maximum(jnp.maximum(l0, l1), l2)
        lse = m + jnp.log(jnp.exp(l0 - m) + jnp.exp(l1 - m) + jnp.exp(l2 - m))
        lse_ref[...] = lse
        w = [jnp.exp(l0 - lse), jnp.exp(l1 - lse), jnp.exp(l2 - lse)]
        for h in range(N_HEADS):
            hs = slice(h * HEAD_DIM, (h + 1) * HEAD_DIM)
            acc = w[0][:, h:h + 1] * o_ref[0, :, hs]
            acc = acc + w[1][:, h:h + 1] * o_ref[1, :, hs]
            acc = acc + w[2][:, h:h + 1] * o_ref[2, :, hs]
            out_ref[:, hs] = acc.astype(BF16)

    return _pcall(body, name="attn_combine", out_shape=(_sds((T, ATT_WIDTH), BF16), _sds((T, N_HEADS), F32)),
                  grid=(T // tm,),
                  in_specs=[pl.BlockSpec((N_PATTERNS, tm, ATT_WIDTH), lambda i: (0, i, 0)),
                            pl.BlockSpec((N_PATTERNS, tm, N_HEADS), lambda i: (0, i, 0))],
                  out_specs=(pl.BlockSpec((tm, ATT_WIDTH), lambda i: (i, 0)), pl.BlockSpec((tm, N_HEADS), lambda i: (i, 0))),
                  dims=("parallel",))(o_p, lse_p)


def _attn_bwd(qb, kb, vb, dob, ob, lseb, seq_blocks):
    _, NB, _, _ = qb.shape
    last = NB - 1
    cur = pl.BlockSpec((None, None, ATT_BLOCK, ATT_WIDTH), lambda p, j: (p, jnp.minimum(j, last), 0, 0))
    prev = pl.BlockSpec((None, None, ATT_BLOCK, ATT_WIDTH),
                        lambda p, j: (p, jnp.maximum(jnp.minimum(j, last) - 1, 0), 0, 0))
    lag = pl.BlockSpec((None, None, ATT_BLOCK, ATT_WIDTH), lambda p, j: (p, jnp.maximum(j - 1, 0), 0, 0))
    lse_spec = pl.BlockSpec((None, None, ATT_BLOCK, N_HEADS), lambda p, j: (p, jnp.minimum(j, last), 0, 0))
    scale = HEAD_DIM ** -0.5
    tn = (((0,), (0,)), ((), ()))
    nt = (((1,), (1,)), ((), ()))

    def body(q_ref, kc_ref, kp_ref, vc_ref, vp_ref, do_ref, o_ref, lse_ref, dq_ref, dk_ref, dv_ref, ck_ref, cv_ref):
        j, nb, dil, a_idx, j_idx = _att_block_consts(seq_blocks)

        @pl.when(j == 0)
        def _():
            ck_ref[...] = jnp.zeros_like(ck_ref)
            cv_ref[...] = jnp.zeros_like(cv_ref)

        @pl.when(j <= last)
        def _():
            first = lax.rem(j, nb) == 0
            for h in range(N_HEADS):
                hs = slice(h * HEAD_DIM, (h + 1) * HEAD_DIM)
                qh, kc, kp, vc, vp, doh = q_ref[:, hs], kc_ref[:, hs], kp_ref[:, hs], vc_ref[:, hs], vp_ref[:, hs], do_ref[:, hs]
                s_c, s_p = _att_scores(qh, kc, kp, h, dil, first, a_idx, j_idx)
                lse = lse_ref[:, h:h + 1]
                p_c = jnp.exp(s_c - lse)
                p_p = jnp.exp(s_p - lse)
                delta = jnp.sum(doh.astype(F32) * o_ref[:, hs].astype(F32), axis=1, keepdims=True)
                ds_c = (p_c * (lax.dot_general(doh, vc, nt, preferred_element_type=F32) - delta)).astype(BF16)
                ds_p = (p_p * (lax.dot_general(doh, vp, nt, preferred_element_type=F32) - delta)).astype(BF16)
                dq_ref[:, hs] = (jnp.dot(ds_c, kc, preferred_element_type=F32)
                                 + jnp.dot(ds_p, kp, preferred_element_type=F32)) * scale
                dk_ref[:, hs] = ck_ref[:, hs] + lax.dot_general(ds_p, qh, tn, preferred_element_type=F32) * scale
                dv_ref[:, hs] = cv_ref[:, hs] + lax.dot_general(p_p.astype(BF16), doh, tn, preferred_element_type=F32)
                ck_ref[:, hs] = lax.dot_general(ds_c, qh, tn, preferred_element_type=F32) * scale
                cv_ref[:, hs] = lax.dot_general(p_c.astype(BF16), doh, tn, preferred_element_type=F32)

        @pl.when(j == NB)
        def _():
            dk_ref[...] = ck_ref[...]
            dv_ref[...] = cv_ref[...]

    shp = _sds(qb.shape, F32)
    return _pcall(body, name="attn_bwd", out_shape=(shp, shp, shp), grid=(N_PATTERNS, NB + 1),
                  in_specs=[cur, cur, prev, cur, prev, cur, cur, lse_spec], out_specs=(cur, lag, lag),
                  scratch_shapes=[pltpu.VMEM((ATT_BLOCK, ATT_WIDTH), F32), pltpu.VMEM((ATT_BLOCK, ATT_WIDTH), F32)],
                  dims=("arbitrary", "arbitrary"))(qb, kb, kb, vb, vb, dob, ob, lseb)


def _sum3_cast(a, b, c):
    T, N = a.shape
    tm = min(T, 1024)
    spec = pl.BlockSpec((tm, N), lambda i: (i, 0))

    def body(a_ref, b_ref, c_ref, o_ref):
        o_ref[...] = (a_ref[...] + b_ref[...] + c_ref[...]).astype(BF16)

    return _pcall(body, name="sum3_cast", out_shape=_sds((T, N), BF16), grid=(T // tm,), in_specs=[spec] * 3,
                  out_specs=spec, dims=("parallel",))(a, b, c)


def _to_blocks(t, B, S):
    C = t.shape[-1]
    outs = []
    for p in range(N_PATTERNS):
        d = 4 ** p
        u = t.reshape(B, S // d, d, C).transpose(0, 2, 1, 3)
        outs.append(u.reshape(B * S // ATT_BLOCK, ATT_BLOCK, C))
    return jnp.stack(outs, axis=0)


def _from_blocks(tb, B, S):
    C = tb.shape[-1]
    outs = []
    for p in range(N_PATTERNS):
        d = 4 ** p
        u = tb[p].reshape(B, d, S // d, C).transpose(0, 2, 1, 3)
        outs.append(u.reshape(B * S, C))
    return jnp.stack(outs, axis=0)


ATT_GROUP = 4
ATT_GW = ATT_GROUP * HEAD_DIM
ATT_GROUPS = N_HEADS // ATT_GROUP
ATT_PAIRS = ATT_GW // ATT_BLOCK
ATT_UNROLL = 3
NT_DIMS = (((1,), (1,)), ((), ()))
TN_DIMS = (((0,), (0,)), ((), ()))


def _att_rows(start, d):
    if d == 1:
        return pl.ds(start if isinstance(start, int) else pl.multiple_of(start, ATT_BLOCK), ATT_BLOCK)
    return pl.ds(start, ATT_BLOCK, stride=d)


def _att_fill_bias(bias_ref, g, d):
    a = lax.broadcasted_iota(jnp.int32, (ATT_BLOCK, ATT_BLOCK), 0)
    j = lax.broadcasted_iota(jnp.int32, (ATT_BLOCK, ATT_BLOCK), 1)
    dist = (a - j).astype(F32)
    for hh in range(ATT_GROUP):
        t, e = divmod(hh, 2)
        rs = slice(e * ATT_BLOCK, (e + 1) * ATT_BLOCK)
        lo = 2.0 ** (-8.0 * (hh + 1) / N_HEADS) * d
        hi = 2.0 ** (-8.0 * (ATT_GROUP + hh + 1) / N_HEADS) * d
        slope = jnp.where(g == 0, lo, hi).astype(F32)
        bias_ref[t, rs, 0:ATT_BLOCK] = jnp.where(a >= j, -slope * dist, NEG_INF)
        bias_ref[t, rs, ATT_BLOCK:] = jnp.where(j >= a, -slope * (dist + float(ATT_BLOCK)), NEG_INF)


def _stack_heads(v2, low):
    return jnp.concatenate([jnp.where(low, v2, 0.0), jnp.where(low, 0.0, v2)], axis=0).astype(BF16)


def _unstack_heads(r2, low):
    return jnp.where(low, r2[0:ATT_BLOCK], r2[ATT_BLOCK:])


def _attention_fwd(proj3, seq_blocks):
    B, S, _ = proj3.shape
    scale = HEAD_DIM ** -0.5
    nq = ATT_WIDTH // ATT_GW

    def col(k):
        return pl.BlockSpec((1, S, ATT_GW), lambda b, g, k=k: (b, 0, k * nq + g))

    o_spec = pl.BlockSpec((1, S, ATT_GW), lambda b, g: (b, 0, g))
    l_spec = pl.BlockSpec((1, 1, S, ATT_BLOCK), lambda b, g: (b, g, 0, 0))

    def body(q_ref, k_ref, v_ref, o_ref, lse_ref, qf, kf, vf, os, ls, bias):
        g = pl.program_id(1)
        for t in range(ATT_PAIRS):
            ts = slice(t * ATT_BLOCK, (t + 1) * ATT_BLOCK)
            qf[t] = q_ref[0, :, ts].astype(F32) * scale
            kf[t] = k_ref[0, :, ts].astype(F32)
            vf[t] = v_ref[0, :, ts].astype(F32)
        lane = lax.broadcasted_iota(jnp.int32, (ATT_BLOCK, ATT_BLOCK), 1)
        low = lane < HEAD_DIM

        def block(p, d, r, n, has_prev):
            start = n * (ATT_BLOCK * d) + r
            rows = _att_rows(start, d)
            prows = _att_rows(start - ATT_BLOCK * d, d) if has_prev else None
            lse_t = jnp.zeros((ATT_BLOCK, ATT_BLOCK), F32)
            for t in range(ATT_PAIRS):
                q2 = _stack_heads(qf[t, rows, :], low)
                k2 = kf[t, rows, :].astype(BF16)
                v2 = vf[t, rows, :].astype(BF16)
                if has_prev:
                    k2 = jnp.concatenate([k2, kf[t, prows, :].astype(BF16)], axis=0)
                    v2 = jnp.concatenate([v2, vf[t, prows, :].astype(BF16)], axis=0)
                    b2 = bias[t]
                else:
                    b2 = bias[t, :, 0:ATT_BLOCK]
                s = lax.dot_general(q2, k2, NT_DIMS, preferred_element_type=F32) + b2
                m = jnp.max(s, axis=1, keepdims=True)
                pr = jnp.exp(s - m)
                den = jnp.sum(pr, axis=1, keepdims=True)
                o = jnp.dot(pr.astype(BF16), v2, preferred_element_type=F32) / den
                os[p, t, rows, :] = _unstack_heads(o, low)
                lse2 = m + jnp.log(den)
                lse_t = jnp.where(lane == 2 * t, lse2[0:ATT_BLOCK], lse_t)
                lse_t = jnp.where(lane == 2 * t + 1, lse2[ATT_BLOCK:], lse_t)
            ls[p, rows, :] = lse_t

        for p in range(N_PATTERNS):
            d = 4 ** p
            _att_fill_bias(bias, g, d)
            _att_one_pattern(block, p, d, seq_blocks // d)

        def combine(i, carry):
            rows = pl.ds(pl.multiple_of(i * ATT_BLOCK, ATT_BLOCK), ATT_BLOCK)
            l0, l1, l2 = ls[0, rows, :], ls[1, rows, :], ls[2, rows, :]
            m = jnp.maximum(jnp.maximum(l0, l1), l2)
            lse = m + jnp.log(jnp.exp(l0 - m) + jnp.exp(l1 - m) + jnp.exp(l2 - m))
            lse_ref[0, 0, rows, :] = lse
            w = [jnp.exp(l0 - lse), jnp.exp(l1 - lse), jnp.exp(l2 - lse)]
            for t in range(ATT_PAIRS):
                acc = jnp.zeros((ATT_BLOCK, ATT_BLOCK), F32)
                for p in range(N_PATTERNS):
                    wt = jnp.where(low, w[p][:, 2 * t:2 * t + 1], w[p][:, 2 * t + 1:2 * t + 2])
                    acc = acc + wt * os[p, t, rows, :]
                o_ref[0, rows, t * ATT_BLOCK:(t + 1) * ATT_BLOCK] = acc.astype(BF16)
            return carry

        lax.fori_loop(0, S // ATT_BLOCK, combine, 0)

    return _pcall(body, name="attention_fwd",
                  out_shape=(_sds((B, S, ATT_WIDTH), BF16), _sds((B, ATT_GROUPS, S, ATT_BLOCK), F32)),
                  grid=(B, ATT_GROUPS), in_specs=[col(0), col(1), col(2)], out_specs=(o_spec, l_spec),
                  scratch_shapes=[pltpu.VMEM((ATT_PAIRS, S, ATT_BLOCK), F32)] * 3
                  + [pltpu.VMEM((N_PATTERNS, ATT_PAIRS, S, ATT_BLOCK), F32), pltpu.VMEM((N_PATTERNS, S, ATT_BLOCK), F32),
                     pltpu.VMEM((ATT_PAIRS, 2 * ATT_BLOCK, 2 * ATT_BLOCK), F32)],
                  dims=("parallel", "parallel"))(proj3, proj3, proj3)


def _att_one_pattern(block, p, d, nb):
    def per_residue(r, carry):
        block(p, d, r, 0, False)
        if nb > 1:
            def per_block(n, c2):
                block(p, d, r, n, True)
                return c2
            lax.fori_loop(1, nb, per_block, 0, unroll=ATT_UNROLL)
        return carry

    if d == 1:
        per_residue(0, 0)
    else:
        lax.fori_loop(0, d, per_residue, 0, unroll=ATT_UNROLL + 1 if nb == 1 else 1)


def _attention_bwd(proj3, do3, o3, lse4, seq_blocks):
    B, S, _ = proj3.shape
    scale = HEAD_DIM ** -0.5
    nq = ATT_WIDTH // ATT_GW

    def col(k):
        return pl.BlockSpec((1, S, ATT_GW), lambda b, g, k=k: (b, 0, k * nq + g))

    o_spec = pl.BlockSpec((1, S, ATT_GW), lambda b, g: (b, 0, g))
    l_spec = pl.BlockSpec((1, 1, S, ATT_BLOCK), lambda b, g: (b, g, 0, 0))

    def body(q_ref, k_ref, v_ref, do_ref, o_ref, lse_ref, dq_ref, dk_ref, dv_ref,
             qf, kf, vf, dof, dl, aq, ak, av, bias):
        g = pl.program_id(1)
        for t in range(ATT_PAIRS):
            ts = slice(t * ATT_BLOCK, (t + 1) * ATT_BLOCK)
            qf[t] = q_ref[0, :, ts].astype(F32) * scale
            kf[t] = k_ref[0, :, ts].astype(F32)
            vf[t] = v_ref[0, :, ts].astype(F32)
            dof[t] = do_ref[0, :, ts].astype(F32)
        aq[...] = jnp.zeros_like(aq)
        ak[...] = jnp.zeros_like(ak)
        av[...] = jnp.zeros_like(av)
        lane = lax.broadcasted_iota(jnp.int32, (ATT_BLOCK, ATT_BLOCK), 1)
        low = lane < HEAD_DIM

        def fill_delta(i, carry):
            rows = pl.ds(pl.multiple_of(i * ATT_BLOCK, ATT_BLOCK), ATT_BLOCK)
            acc = jnp.zeros((ATT_BLOCK, ATT_BLOCK), F32)
            for t in range(ATT_PAIRS):
                prod = dof[t, rows, :] * o_ref[0, rows, t * ATT_BLOCK:(t + 1) * ATT_BLOCK].astype(F32)
                lo = jnp.sum(jnp.where(low, prod, 0.0), axis=1, keepdims=True)
                hi = jnp.sum(prod, axis=1, keepdims=True) - lo
                acc = jnp.where(lane == 2 * t, lo, acc)
                acc = jnp.where(lane == 2 * t + 1, hi, acc)
            dl[rows, :] = acc
            return carry

        lax.fori_loop(0, S // ATT_BLOCK, fill_delta, 0)

        def block(p, d, r, n, has_prev):
            start = n * (ATT_BLOCK * d) + r
            rows = _att_rows(start, d)
            prows = _att_rows(start - ATT_BLOCK * d, d) if has_prev else None
            lse_t = lse_ref[0, 0, rows, :]
            dl_t = dl[rows, :]
            for t in range(ATT_PAIRS):
                q2 = _stack_heads(qf[t, rows, :], low)
                do2 = _stack_heads(dof[t, rows, :], low)
                k2 = kf[t, rows, :].astype(BF16)
                v2 = vf[t, rows, :].astype(BF16)
                if has_prev:
                    k2 = jnp.concatenate([k2, kf[t, prows, :].astype(BF16)], axis=0)
                    v2 = jnp.concatenate([v2, vf[t, prows, :].astype(BF16)], axis=0)
                    b2 = bias[t]
                else:
                    b2 = bias[t, :, 0:ATT_BLOCK]
                lse2 = jnp.concatenate([lse_t[:, 2 * t:2 * t + 1], lse_t[:, 2 * t + 1:2 * t + 2]], axis=0)
                dl2 = jnp.concatenate([dl_t[:, 2 * t:2 * t + 1], dl_t[:, 2 * t + 1:2 * t + 2]], axis=0)
                s = lax.dot_general(q2, k2, NT_DIMS, preferred_element_type=F32) + b2
                pr = jnp.exp(s - lse2)
                ds = (pr * (lax.dot_general(do2, v2, NT_DIMS, preferred_element_type=F32) - dl2)).astype(BF16)
                dq = _unstack_heads(jnp.dot(ds, k2, preferred_element_type=F32), low)
                dk = lax.dot_general(ds, q2, TN_DIMS, preferred_element_type=F32)
                dv = lax.dot_general(pr.astype(BF16), do2, TN_DIMS, preferred_element_type=F32)
                aq[t, rows, :] = aq[t, rows, :] + dq * scale
                ak[t, rows, :] = ak[t, rows, :] + dk[0:ATT_BLOCK]
                av[t, rows, :] = av[t, rows, :] + dv[0:ATT_BLOCK]
                if has_prev:
                    ak[t, prows, :] = ak[t, prows, :] + dk[ATT_BLOCK:]
                    av[t, prows, :] = av[t, prows, :] + dv[ATT_BLOCK:]

        for p in range(N_PATTERNS):
            d = 4 ** p
            _att_fill_bias(bias, g, d)
            _att_one_pattern(block, p, d, seq_blocks // d)

        for t in range(ATT_PAIRS):
            ts = slice(t * ATT_BLOCK, (t + 1) * ATT_BLOCK)
            dq_ref[0, :, ts] = aq[t].astype(BF16)
            dk_ref[0, :, ts] = ak[t].astype(BF16)
            dv_ref[0, :, ts] = av[t].astype(BF16)

    shp = _sds((B, S, ATT_WIDTH), BF16)
    pair_buf = pltpu.VMEM((ATT_PAIRS, S, ATT_BLOCK), F32)
    return _pcall(body, name="attention_bwd", out_shape=(shp, shp, shp), grid=(B, ATT_GROUPS),
                  in_specs=[col(0), col(1), col(2), o_spec, o_spec, l_spec], out_specs=(o_spec, o_spec, o_spec),
                  scratch_shapes=[pair_buf] * 4 + [pltpu.VMEM((S, ATT_BLOCK), F32)] + [pair_buf] * 3
                  + [pltpu.VMEM((ATT_PAIRS, 2 * ATT_BLOCK, 2 * ATT_BLOCK), F32)],
                  dims=("parallel", "parallel"))(proj3, proj3, proj3, do3, o3, lse4)


def _expand_groups(m):
    rows = SSM_WIDTH
    t = jnp.concatenate([m] * SSM_GROUPS, axis=0)
    r = lax.broadcasted_iota(jnp.int32, (rows, SSM_LANES), 0)
    l = lax.broadcasted_iota(jnp.int32, (rows, SSM_LANES), 1)
    keep = lax.shift_right_logical(r, 4) == lax.shift_right_logical(l, 6)
    return jnp.where(keep, t, 0.0)


def _collapse_groups(m):
    rows = SSM_WIDTH
    r = lax.broadcasted_iota(jnp.int32, (rows, SSM_LANES), 0)
    l = lax.broadcasted_iota(jnp.int32, (rows, SSM_LANES), 1)
    keep = lax.shift_right_logical(r, 4) == lax.shift_right_logical(l, 6)
    t = jnp.where(keep, m, 0.0)
    acc = t[0:SSM_GROUP_CH]
    for g in range(1, SSM_GROUPS):
        acc = acc + t[g * SSM_GROUP_CH:(g + 1) * SSM_GROUP_CH]
    return acc


def _zoh(lr, li, ldt):
    dt = jnp.exp(ldt)
    mag = jnp.exp(lr * dt)
    ang = li * dt
    cs, sn = jnp.cos(ang), jnp.sin(ang)
    ab_re, ab_im = mag * cs, mag * sn
    nr, ni = ab_re - 1.0, ab_im
    den = lr * lr + li * li
    n_re = nr * lr + ni * li
    n_im = ni * lr - nr * li
    return dict(dt=dt, mag=mag, cs=cs, sn=sn, ab_re=ab_re, ab_im=ab_im, nr=nr, ni=ni, den=den, n_re=n_re, n_im=n_im,
                f_re=n_re / den, f_im=n_im / den)


def _ssm_params(lr, li, ldt, br, bi, cr, ci):
    def body(lr_ref, li_ref, ldt_ref, br_ref, bi_ref, cr_ref, ci_ref, ab_ref, w_ref, c_ref):
        z = _zoh(lr_ref[...], li_ref[...], ldt_ref[...])
        ab_ref[0:1, :] = z["ab_re"]
        ab_ref[1:2, :] = z["ab_im"]
        br, bi = br_ref[...], bi_ref[...]
        w_ref[:, 0:SSM_LANES] = _expand_groups(z["f_re"] * br - z["f_im"] * bi).astype(BF16)
        w_ref[:, SSM_LANES:] = _expand_groups(z["f_re"] * bi + z["f_im"] * br).astype(BF16)
        c_ref[:, 0:SSM_LANES] = _expand_groups(cr_ref[...]).astype(BF16)
        c_ref[:, SSM_LANES:] = _expand_groups(-ci_ref[...]).astype(BF16)

    return _pcall(body, name="ssm_params",
                  out_shape=(_sds((2, SSM_LANES), F32), _sds((SSM_WIDTH, 2 * SSM_LANES), BF16),
                             _sds((SSM_WIDTH, 2 * SSM_LANES), BF16)))(lr, li, ldt, br, bi, cr, ci)


def _ssm_params_bwd(lr, li, ldt, br, bi, dab, dw, dc):
    def body(lr_ref, li_ref, ldt_ref, br_ref, bi_ref, dab_ref, dw_ref, dc_ref,
             dlr_ref, dli_ref, dldt_ref, dbr_ref, dbi_ref, dcr_ref, dci_ref):
        lr, li = lr_ref[...], li_ref[...]
        z = _zoh(lr, li, ldt_ref[...])
        br, bi = br_ref[...], bi_ref[...]
        dbb_re = _collapse_groups(dw_ref[:, 0:SSM_LANES])
        dbb_im = _collapse_groups(dw_ref[:, SSM_LANES:])
        dcr_ref[...] = _collapse_groups(dc_ref[:, 0:SSM_LANES])
        dci_ref[...] = -_collapse_groups(dc_ref[:, SSM_LANES:])
        f_re, f_im = z["f_re"], z["f_im"]
        dbr_ref[...] = f_re * dbb_re + f_im * dbb_im
        dbi_ref[...] = f_re * dbb_im - f_im * dbb_re
        df_re = jnp.sum(dbb_re * br + dbb_im * bi, axis=0, keepdims=True)
        df_im = jnp.sum(dbb_im * br - dbb_re * bi, axis=0, keepdims=True)
        den = z["den"]
        dn_re, dn_im = df_re / den, df_im / den
        dden = -(df_re * z["n_re"] + df_im * z["n_im"]) / (den * den)
        dnr = dn_re * lr - dn_im * li
        dni = dn_re * li + dn_im * lr
        dlr = dn_re * z["nr"] + dn_im * z["ni"] + 2.0 * dden * lr
        dli = dn_re * z["ni"] - dn_im * z["nr"] + 2.0 * dden * li
        dab_re = dab_ref[0:1, :] + dnr
        dab_im = dab_ref[1:2, :] + dni
        mag, cs, sn, dt = z["mag"], z["cs"], z["sn"], z["dt"]
        dmag = dab_re * cs + dab_im * sn
        dang = mag * (dab_im * cs - dab_re * sn)
        dlr_ref[...] = dlr + dmag * mag * dt
        dli_ref[...] = dli + dang * dt
        ddt = dmag * mag * lr + dang * li
        per_lane = jnp.broadcast_to(ddt * dt, (8, SSM_LANES))
        lane = lax.broadcasted_iota(jnp.int32, (SSM_LANES, 128), 0)
        col = lax.broadcasted_iota(jnp.int32, (SSM_LANES, 128), 1)
        ind = jnp.where(lax.shift_right_logical(lane, 6) == col, 1.0, 0.0)
        dldt_ref[...] = jnp.dot(per_lane, ind, preferred_element_type=F32, precision=lax.Precision.HIGHEST)[0:1]

    vec = _sds((1, SSM_LANES), F32)
    mat = _sds((SSM_GROUP_CH, SSM_LANES), F32)
    return _pcall(body, name="ssm_params_bwd", out_shape=(vec, vec, _sds((1, 128), F32), mat, mat, mat, mat))(
        lr, li, ldt, br, bi, dab, dw, dc)


SCAN_CHUNK = 512


def _scan_consts(ar, ai, k_ref, reverse):
    row = lax.broadcasted_iota(jnp.int32, (8, SSM_LANES), 0)
    pw = [(ar, ai)]
    for _ in range(7):
        pr, pi = pw[-1]
        pw.append((pr * ar - pi * ai, pr * ai + pi * ar))
    for n, k in enumerate((1, 2, 4)):
        keep = (row < 8 - k) if reverse else (row >= k)
        k_ref[2 * n] = jnp.where(keep, jnp.broadcast_to(pw[k - 1][0], (8, SSM_LANES)), 0.0)
        k_ref[2 * n + 1] = jnp.where(keep, jnp.broadcast_to(pw[k - 1][1], (8, SSM_LANES)), 0.0)
    cr = jnp.zeros((8, SSM_LANES), F32)
    ci = jnp.zeros((8, SSM_LANES), F32)
    for r in range(8):
        e = (8 - r) if reverse else (r + 1)
        cr = jnp.where(row == r, jnp.broadcast_to(pw[e - 1][0], (8, SSM_LANES)), cr)
        ci = jnp.where(row == r, jnp.broadcast_to(pw[e - 1][1], (8, SSM_LANES)), ci)
    k_ref[6] = cr
    k_ref[7] = ci


def _scan_tile(xr, xi, k_ref, car, cai, reverse):
    for n, k in enumerate((1, 2, 4)):
        sh = (8 - k) if reverse else k
        sr = pltpu.roll(xr, sh, 0)
        si = pltpu.roll(xi, sh, 0)
        mr, mi = k_ref[2 * n], k_ref[2 * n + 1]
        xr, xi = xr + mr * sr - mi * si, xi + mr * si + mi * sr
    pr, pi = k_ref[6], k_ref[7]
    xr, xi = xr + pr * car - pi * cai, xi + pr * cai + pi * car
    return xr, xi


def _scan_fwd(bu3, abar):
    B, S, _ = bu3.shape
    ch = min(S, SCAN_CHUNK)
    blk = pl.BlockSpec((1, ch, 2 * SSM_LANES), lambda b, c: (b, c, 0))

    def body(ab_ref, bu_ref, x_ref, k_ref, carry_ref):
        _scan_consts(ab_ref[0:1, :], ab_ref[1:2, :], k_ref, False)

        @pl.when(pl.program_id(1) == 0)
        def _():
            carry_ref[...] = jnp.zeros_like(carry_ref)

        def step(i, carry):
            base = pl.multiple_of(i * 8, 8)
            xr = bu_ref[0, pl.ds(base, 8), 0:SSM_LANES]
            xi = bu_ref[0, pl.ds(base, 8), SSM_LANES:]
            xr, xi = _scan_tile(xr, xi, k_ref, carry[0], carry[1], False)
            x_ref[0, pl.ds(base, 8), 0:SSM_LANES] = xr
            x_ref[0, pl.ds(base, 8), SSM_LANES:] = xi
            return (jnp.broadcast_to(xr[7:8], (8, SSM_LANES)), jnp.broadcast_to(xi[7:8], (8, SSM_LANES)))

        cr, ci = lax.fori_loop(0, ch // 8, step, (carry_ref[0], carry_ref[1]))
        carry_ref[0] = cr
        carry_ref[1] = ci

    return _pcall(body, name="scan_fwd", out_shape=_sds(bu3.shape, F32), grid=(B, S // ch),
                  in_specs=[pl.BlockSpec((2, SSM_LANES), lambda b, c: (0, 0)), blk], out_specs=blk,
                  scratch_shapes=[pltpu.VMEM((8, 8, SSM_LANES), F32), pltpu.VMEM((2, 8, SSM_LANES), F32)],
                  dims=("arbitrary", "arbitrary"))(abar, bu3)


def _scan_bwd(dx3, xs3, abar):
    B, S, _ = dx3.shape
    ch = min(S, SCAN_CHUNK)
    nc = S // ch
    blk = pl.BlockSpec((1, ch, 2 * SSM_LANES), lambda b, c: (b, nc - 1 - c, 0))

    def body(ab_ref, dx_ref, xs_ref, g_ref, da_ref, k_ref, carry_ref, acc_ref):
        b, c = pl.program_id(0), pl.program_id(1)
        _scan_consts(ab_ref[0:1, :], -ab_ref[1:2, :], k_ref, True)
        row = lax.broadcasted_iota(jnp.int32, (8, SSM_LANES), 0)

        @pl.when(c == 0)
        def _():
            carry_ref[...] = jnp.zeros_like(carry_ref)

        @pl.when((c == 0) & (b == 0))
        def _():
            acc_ref[...] = jnp.zeros_like(acc_ref)

        def step(i, carry):
            car, cai, ar_acc, ai_acc = carry
            base = pl.multiple_of((ch // 8 - 1 - i) * 8, 8)
            gr = dx_ref[0, pl.ds(base, 8), 0:SSM_LANES]
            gi = dx_ref[0, pl.ds(base, 8), SSM_LANES:]
            gr, gi = _scan_tile(gr, gi, k_ref, car, cai, True)
            g_ref[0, pl.ds(base, 8), 0:SSM_LANES] = gr
            g_ref[0, pl.ds(base, 8), SSM_LANES:] = gi
            nr = jnp.where(row == 7, car, pltpu.roll(gr, 7, 0))
            ni = jnp.where(row == 7, cai, pltpu.roll(gi, 7, 0))
            xr = xs_ref[0, pl.ds(base, 8), 0:SSM_LANES]
            xi = xs_ref[0, pl.ds(base, 8), SSM_LANES:]
            ar_acc = ar_acc + nr * xr + ni * xi
            ai_acc = ai_acc + ni * xr - nr * xi
            return (jnp.broadcast_to(gr[0:1], (8, SSM_LANES)), jnp.broadcast_to(gi[0:1], (8, SSM_LANES)), ar_acc, ai_acc)

        cr, ci, ar_acc, ai_acc = lax.fori_loop(0, ch // 8, step, (carry_ref[0], carry_ref[1], acc_ref[0], acc_ref[1]))
        carry_ref[0] = cr
        carry_ref[1] = ci
        acc_ref[0] = ar_acc
        acc_ref[1] = ai_acc
        da_ref[0:1, :] = jnp.sum(ar_acc, axis=0, keepdims=True)
        da_ref[1:2, :] = jnp.sum(ai_acc, axis=0, keepdims=True)

    return _pcall(body, name="scan_bwd", out_shape=(_sds(dx3.shape, F32), _sds((2, SSM_LANES), F32)), grid=(B, nc),
                  in_specs=[pl.BlockSpec((2, SSM_LANES), lambda b, c: (0, 0)), blk, blk],
                  out_specs=(blk, pl.BlockSpec((2, SSM_LANES), lambda b, c: (0, 0))),
                  scratch_shapes=[pltpu.VMEM((8, 8, SSM_LANES), F32), pltpu.VMEM((2, 8, SSM_LANES), F32),
                                  pltpu.VMEM((2, 8, SSM_LANES), F32)],
                  dims=("arbitrary", "arbitrary"))(abar, dx3, xs3)


GELU_K = math.sqrt(2.0 / math.pi)
GELU_C = 0.044715


def _gelu_parts(y):
    t = jnp.tanh(GELU_K * (y + GELU_C * y * y * y))
    return 0.5 * y * (1.0 + t), t


def _ssm_post(yc, us, dsk, wglu, bglu):
    T, N = yc.shape
    tm = min(T, 1024)
    row = pl.BlockSpec((tm, N), lambda i: (i, 0))
    vec = pl.BlockSpec((1, N), lambda i: (0, 0))
    mat = pl.BlockSpec((N, N), lambda i: (0, 0))

    def body(yc_ref, us_ref, d_ref, w_ref, b_ref, y_ref, s_ref):
        y = yc_ref[...] + d_ref[...] * us_ref[...]
        y_ref[...] = y
        z, _ = _gelu_parts(y)
        gl = jnp.dot(z.astype(BF16), w_ref[...], preferred_element_type=F32) + b_ref[...]
        s_ref[...] = (z * _sig(gl)).astype(BF16)

    return _pcall(body, name="ssm_post", out_shape=(_sds((T, N), F32), _sds((T, N), BF16)), grid=(T // tm,),
                  in_specs=[row, row, vec, mat, vec], out_specs=(row, row), dims=("parallel",))(yc, us, dsk, wglu, bglu)


def _ssm_post_bwd(y5, us, ds, dsk, wglu, bglu):
    T, N = y5.shape
    tm = min(T, 1024)
    row = pl.BlockSpec((tm, N), lambda i: (i, 0))
    vec = pl.BlockSpec((1, N), lambda i: (0, 0))
    mat = pl.BlockSpec((N, N), lambda i: (0, 0))

    def body(y_ref, us_ref, ds_ref, d_ref, w_ref, b_ref, dy_ref, dd_ref, db_ref, dw_ref):
        @pl.when(pl.program_id(0) == 0)
        def _():
            dd_ref[...] = jnp.zeros_like(dd_ref)
            db_ref[...] = jnp.zeros_like(db_ref)
            dw_ref[...] = jnp.zeros_like(dw_ref)

        y = y_ref[...]
        z, t = _gelu_parts(y)
        zb = z.astype(BF16)
        gl = jnp.dot(zb, w_ref[...], preferred_element_type=F32) + b_ref[...]
        sg = _sig(gl)
        ds = ds_ref[...]
        dgl = ds * z * sg * (1.0 - sg)
        dglb = dgl.astype(BF16)
        dz = ds * sg + lax.dot_general(dglb, w_ref[...], (((1,), (1,)), ((), ())), preferred_element_type=F32)
        dgelu = 0.5 * (1.0 + t) + 0.5 * y * (1.0 - t * t) * GELU_K * (1.0 + 3.0 * GELU_C * y * y)
        dy = dz * dgelu
        dy_ref[...] = dy
        dd_ref[...] += jnp.sum(dy * us_ref[...], axis=0, keepdims=True)
        db_ref[...] += jnp.sum(dgl, axis=0, keepdims=True)
        dw_ref[...] += lax.dot_general(zb, dglb, (((0,), (0,)), ((), ())), preferred_element_type=F32)

    return _pcall(body, name="ssm_post_bwd",
                  out_shape=(_sds((T, N), F32), _sds((1, N), F32), _sds((1, N), F32), _sds((N, N), F32)),
                  grid=(T // tm,), in_specs=[row, row, row, vec, mat, vec], out_specs=(row, vec, vec, mat),
                  dims=("arbitrary",))(y5, us, ds, dsk, wglu, bglu)


def _add_scaled_cast(a, b, s):
    T, N = a.shape
    tm = min(T, 1024)
    row = pl.BlockSpec((tm, N), lambda i: (i, 0))

    def body(a_ref, b_ref, s_ref, o_ref):
        o_ref[...] = (a_ref[...] + s_ref[...] * b_ref[...]).astype(BF16)

    return _pcall(body, name="add_scaled_cast", out_shape=_sds((T, N), BF16), grid=(T // tm,),
                  in_specs=[row, row, pl.BlockSpec((1, N), lambda i: (0, 0))], out_specs=row, dims=("parallel",))(a, b, s)


GATE_TILE = 256
GATE_ATT_BLOCK0 = (3 * ATT_WIDTH + SSM_WIDTH) // GATE_TILE
GATE_SSM_BLOCK0 = (3 * ATT_WIDTH + SSM_WIDTH + D_MODEL) // GATE_TILE


def _merge(proj, y_att, y_ssm, b_gate):
    T = proj.shape[0]
    tm = min(T, 1024)
    nj = D_MODEL // GATE_TILE
    ga = pl.BlockSpec((tm, GATE_TILE), lambda i, j: (i, GATE_ATT_BLOCK0 + j))
    gs = pl.BlockSpec((tm, GATE_TILE), lambda i, j: (i, GATE_SSM_BLOCK0 + j))
    yy = pl.BlockSpec((tm, GATE_TILE), lambda i, j: (i, j))
    ba = pl.BlockSpec((1, GATE_TILE), lambda i, j: (0, j))
    bs = pl.BlockSpec((1, GATE_TILE), lambda i, j: (0, nj + j))

    def body(ga_ref, gs_ref, ya_ref, ys_ref, ba_ref, bs_ref, o_ref):
        o_ref[...] = (_sig(ga_ref[...] + ba_ref[...]) * ya_ref[...]
                      + _sig(gs_ref[...] + bs_ref[...]) * ys_ref[...]).astype(BF16)

    return _pcall(body, name="merge", out_shape=_sds((T, D_MODEL), BF16), grid=(T // tm, nj),
                  in_specs=[ga, gs, yy, yy, ba, bs], out_specs=yy, dims=("parallel", "parallel"))(
        proj, proj, y_att, y_ssm, b_gate, b_gate)


def _merge_bwd(proj, y_att, y_ssm, b_gate, dmerged):
    T = proj.shape[0]
    tm = min(T, 1024)
    nj = D_MODEL // GATE_TILE
    ga = pl.BlockSpec((tm, GATE_TILE), lambda j, i: (i, GATE_ATT_BLOCK0 + j))
    gs = pl.BlockSpec((tm, GATE_TILE), lambda j, i: (i, GATE_SSM_BLOCK0 + j))
    yy = pl.BlockSpec((tm, GATE_TILE), lambda j, i: (i, j))
    ba = pl.BlockSpec((1, GATE_TILE), lambda j, i: (0, j))
    bs = pl.BlockSpec((1, GATE_TILE), lambda j, i: (0, nj + j))

    def body(ga_ref, gs_ref, ya_ref, ys_ref, ba_ref, bs_ref, dm_ref, dya_ref, dys_ref, dga_ref, dgs_ref, dba_ref, dbs_ref):
        @pl.when(pl.program_id(1) == 0)
        def _():
            dba_ref[...] = jnp.zeros_like(dba_ref)
            dbs_ref[...] = jnp.zeros_like(dbs_ref)

        dm = dm_ref[...].astype(F32)
        sa = _sig(ga_ref[...] + ba_ref[...])
        ss = _sig(gs_ref[...] + bs_ref[...])
        dya_ref[...] = (dm * sa).astype(BF16)
        dys_ref[...] = (dm * ss).astype(BF16)
        dga = dm * ya_ref[...] * sa * (1.0 - sa)
        dgs = dm * ys_ref[...] * ss * (1.0 - ss)
        dga_ref[...] = dga.astype(BF16)
        dgs_ref[...] = dgs.astype(BF16)
        dba_ref[...] += jnp.sum(dga, axis=0, keepdims=True)
        dbs_ref[...] += jnp.sum(dgs, axis=0, keepdims=True)

    big = _sds((T, D_MODEL), BF16)
    vec = _sds((1, D_MODEL), F32)
    return _pcall(body, name="merge_bwd", out_shape=(big, big, big, big, vec, vec), grid=(nj, T // tm),
                  in_specs=[ga, gs, yy, yy, ba, bs, yy], out_specs=(yy, yy, yy, yy, ba, ba),
                  dims=("arbitrary", "arbitrary"))(proj, proj, y_att, y_ssm, b_gate, b_gate, dmerged)


CONV_TILE = 256


def _conv_pre(a, w_ref, b_ref, row):
    conv = b_ref[...] + w_ref[0:1, :] * a
    shifted = []
    for j in (1, 2):
        sh = jnp.where(row >= j, pltpu.roll(a, j, 0), 0.0)
        shifted.append(sh)
        conv = conv + w_ref[j:j + 1, :] * sh
    return conv, shifted


def _conv_act(up3, w_conv, b_conv):
    B, S, _ = up3.shape
    nj = D_FF // CONV_TILE
    a_spec = pl.BlockSpec((1, S, CONV_TILE), lambda b, j: (b, 0, j))
    v_spec = pl.BlockSpec((1, S, CONV_TILE), lambda b, j: (b, 0, nj + j))
    w_spec = pl.BlockSpec((3, CONV_TILE), lambda b, j: (0, j))
    b_spec = pl.BlockSpec((1, CONV_TILE), lambda b, j: (0, j))

    def body(a_ref, v_ref, w_ref, b_ref, o_ref):
        a = a_ref[0].astype(F32)
        row = lax.broadcasted_iota(jnp.int32, a.shape, 0)
        conv, _ = _conv_pre(a, w_ref, b_ref, row)
        o_ref[0] = (conv * _sig(conv) * v_ref[0]).astype(BF16)

    return _pcall(body, name="conv_act", out_shape=_sds((B, S, D_FF), BF16), grid=(B, nj),
                  in_specs=[a_spec, v_spec, w_spec, b_spec], out_specs=a_spec, dims=("parallel", "parallel"))(
        up3, up3, w_conv, b_conv)


def _conv_bwd(up3, dact3, w_conv, b_conv):
    B, S, _ = up3.shape
    nj = D_FF // CONV_TILE
    a_spec = pl.BlockSpec((1, S, CONV_TILE), lambda j, b: (b, 0, j))
    v_spec = pl.BlockSpec((1, S, CONV_TILE), lambda j, b: (b, 0, nj + j))
    w_spec = pl.BlockSpec((3, CONV_TILE), lambda j, b: (0, j))
    b_spec = pl.BlockSpec((1, CONV_TILE), lambda j, b: (0, j))

    def body(a_ref, v_ref, d_ref, w_ref, b_ref, da_ref, dv_ref, dw_ref, db_ref):
        @pl.when(pl.program_id(1) == 0)
        def _():
            dw_ref[...] = jnp.zeros_like(dw_ref)
            db_ref[...] = jnp.zeros_like(db_ref)

        a = a_ref[0].astype(F32)
        d = d_ref[0].astype(F32)
        row = lax.broadcasted_iota(jnp.int32, a.shape, 0)
        conv, shifted = _conv_pre(a, w_ref, b_ref, row)
        sg = _sig(conv)
        dv_ref[0] = (d * conv * sg).astype(BF16)
        dconv = d * v_ref[0] * (sg * (1.0 + conv * (1.0 - sg)))
        da = w_ref[0:1, :] * dconv
        for j in (1, 2):
            da = da + w_ref[j:j + 1, :] * jnp.where(row < S - j, pltpu.roll(dconv, S - j, 0), 0.0)
        da_ref[0] = da.astype(BF16)
        db_ref[...] += jnp.sum(dconv, axis=0, keepdims=True)
        dw_ref[0:1, :] += jnp.sum(dconv * a, axis=0, keepdims=True)
        dw_ref[1:2, :] += jnp.sum(dconv * shifted[0], axis=0, keepdims=True)
        dw_ref[2:3, :] += jnp.sum(dconv * shifted[1], axis=0, keepdims=True)

    big = _sds((B, S, D_FF), BF16)
    return _pcall(body, name="conv_bwd", out_shape=(big, big, _sds((3, D_FF), F32), _sds((1, D_FF), F32)),
                  grid=(nj, B), in_specs=[a_spec, v_spec, a_spec, w_spec, b_spec],
                  out_specs=(a_spec, a_spec, w_spec, b_spec), dims=("arbitrary", "arbitrary"))(
        up3, up3, dact3, w_conv, b_conv)


def _rows_tile(r, cap=640):
    for t in range(min(r, cap) - min(r, cap) % 8, 7, -8):
        if r % t == 0:
            return t
    return r


def _add2(a, b, out_dtype):
    R, N = a.shape
    tr = _rows_tile(R)
    spec = pl.BlockSpec((tr, N), lambda i: (i, 0))

    def body(a_ref, b_ref, o_ref):
        o_ref[...] = (a_ref[...] + b_ref[...]).astype(out_dtype)

    return _pcall(body, name="add2", out_shape=_sds((R, N), out_dtype), grid=(R // tr,), in_specs=[spec, spec],
                  out_specs=spec, dims=("parallel",))(a, b)


def _sum_slots(q, name):
    n, R, N = q.shape
    tr = _rows_tile(R)

    def body(q_ref, o_ref):
        acc = q_ref[0].astype(F32)
        for s in range(1, n):
            acc = acc + q_ref[s].astype(F32)
        o_ref[...] = acc

    return _pcall(body, name=name, out_shape=_sds((R, N), F32), grid=(R // tr,),
                  in_specs=[pl.BlockSpec((n, tr, N), lambda i: (0, i, 0))], out_specs=pl.BlockSpec((tr, N), lambda i: (i, 0)),
                  dims=("parallel",))(q)


def _adamw(w, g, m, v, name):
    R, N = w.shape
    tr = _rows_tile(R) if R * N * 4 > (1 << 20) else R
    tr = min(tr, 256) if R % 256 == 0 and R > 256 else tr
    spec = pl.BlockSpec((tr, N), lambda i: (i, 0))
    bc1 = 1.0 - ADAM_B1 ** ADAM_STEP
    bc2 = 1.0 - ADAM_B2 ** ADAM_STEP

    def body(w_ref, g_ref, m_ref, v_ref, d_ref, nm_ref, nv_ref):
        g = g_ref[...]
        m = ADAM_B1 * m_ref[...] + (1.0 - ADAM_B1) * g
        v = ADAM_B2 * v_ref[...] + (1.0 - ADAM_B2) * (g * g)
        nm_ref[...] = m
        nv_ref[...] = v
        d_ref[...] = -ADAM_LR * ((m / bc1) / (jnp.sqrt(v / bc2) + ADAM_EPS) + ADAM_WD * w_ref[...])

    shp = _sds((R, N), F32)
    return _pcall(body, name=name, out_shape=(shp, shp, shp), grid=(R // tr,), in_specs=[spec] * 4,
                  out_specs=(spec, spec, spec), dims=("parallel",))(w, g, m, v)


_GROUP_MASKS = {
    "all": [(dx, dy, dc) for dx in (0, 1) for dy in (0, 1) for dc in (0, 1) if (dx, dy, dc) != (0, 0, 0)],
    "xy": [(1, 0, 0), (0, 1, 0), (1, 1, 0)],
    "c": [(0, 0, 1)],
}
_GROUP_SLOTS = {"all": 8, "xy": 4, "c": 2}


def _group_slot(group, x, y, c):
    return {"all": 4 * x + 2 * y + c, "xy": 2 * x + y, "c": c}[group]


def _flip(v, d):
    return 1 - v if d else v


def _exchange(arr, group, mode, name):
    masks = _GROUP_MASKS[group]
    n = len(masks)
    if mode == "gather":
        out_shape = (_GROUP_SLOTS[group],) + arr.shape
    elif mode == "scatter":
        assert arr.shape[0] == _GROUP_SLOTS[group]
        out_shape = arr.shape
    elif mode == "swap":
        assert group == "c"
        out_shape = arr.shape
    else:
        assert group == "c"
        half = arr.shape[1] // 2
        out_shape = (arr.shape[0], half, arr.shape[2])

    def body(x_ref, o_ref, send_sems, recv_sems, local_sem):
        x, y, c = lax.axis_index("x"), lax.axis_index("y"), lax.axis_index("c")
        me = _group_slot(group, x, y, c)
        local = None
        if mode == "gather":
            local = pltpu.make_async_copy(x_ref, o_ref.at[me], local_sem)
        elif mode == "scatter":
            local = pltpu.make_async_copy(x_ref.at[me], o_ref.at[me], local_sem)
        if local is not None:
            local.start()
        copies = []
        for k, (dx, dy, dc) in enumerate(masks):
            px, py, pc = _flip(x, dx), _flip(y, dy), _flip(c, dc)
            if mode == "gather":
                src, dst = x_ref, o_ref.at[me]
            elif mode == "scatter":
                src, dst = x_ref.at[_group_slot(group, px, py, pc)], o_ref.at[me]
            elif mode == "swap":
                src, dst = x_ref, o_ref
            else:
                src, dst = x_ref.at[:, pl.ds(pl.multiple_of(pc * half, 8), half), :], o_ref
            cp = pltpu.make_async_remote_copy(src_ref=src, dst_ref=dst, send_sem=send_sems.at[k], recv_sem=recv_sems.at[k],
                                              device_id=(px, py, pc), device_id_type=pl.DeviceIdType.MESH)
            cp.start()
            copies.append(cp)
        for cp in copies:
            cp.wait()
        if local is not None:
            local.wait()

    anyspec = pl.BlockSpec(memory_space=pl.ANY)
    return pl.pallas_call(body, name=name, out_shape=_sds(out_shape, arr.dtype), in_specs=[anyspec], out_specs=anyspec,
                          scratch_shapes=[pltpu.SemaphoreType.DMA((n,)), pltpu.SemaphoreType.DMA((n,)),
                                          pltpu.SemaphoreType.DMA(())])(arr)


BIG = (("w_in", (D_MODEL, IN_WIDTH), 1), ("w_out", (D_MODEL, D_MODEL), 0), ("w_up", (D_MODEL, 2 * D_FF), 1),
       ("w_down", (D_FF, D_MODEL), 0), ("w_proj_att", (ATT_WIDTH, D_MODEL), 1), ("w_proj_ssm", (SSM_WIDTH, D_MODEL), 1),
       ("w_glu", (SSM_WIDTH, SSM_WIDTH), 0))
N_XY = 4


def _big_rows(shape):
    return shape[0] * shape[1] // N_XY // LANES


FLAT_ROWS = sum(_big_rows(s) for _, s, _ in BIG)


def _shard_shape(shape, axis):
    return (shape[0] // N_XY, shape[1]) if axis == 0 else (shape[0], shape[1] // N_XY)


def _flatten_shards(shards):
    return jnp.concatenate([shards[n].reshape(_big_rows(s), LANES) for n, s, _ in BIG], axis=0)


def _unflatten_shard(flat):
    out, r = {}, 0
    for n, s, ax in BIG:
        k = _big_rows(s)
        out[n] = flat[r:r + k].reshape(_shard_shape(s, ax))
        r += k
    return out


def _unflatten_full(flat4):
    out, r = {}, 0
    for n, s, ax in BIG:
        k = _big_rows(s)
        sh = _shard_shape(s, ax)
        t = flat4[:, r:r + k].reshape((N_XY,) + sh)
        out[n] = t.reshape(s) if ax == 0 else t.transpose(1, 0, 2).reshape(s)
        r += k
    return out


def _flatten_full(full):
    parts = []
    for n, s, ax in BIG:
        sh = _shard_shape(s, ax)
        t = full[n]
        t = t.reshape((N_XY,) + sh) if ax == 0 else t.reshape(s[0], N_XY, sh[1]).transpose(1, 0, 2)
        parts.append(t.reshape(N_XY, _big_rows(s), LANES))
    return jnp.concatenate(parts, axis=1)


def _pack_rows(arrs):
    rows, counts = [], []
    for a in arrs:
        f = a.reshape(-1)
        k = -(-f.shape[0] // LANES)
        rows.append(jnp.pad(f, (0, k * LANES - f.shape[0])).reshape(k, LANES))
        counts.append(k)
    return jnp.concatenate(rows, axis=0), counts


def _unpack_rows(buf, shapes):
    out, r = [], 0
    for s in shapes:
        size = int(np.prod(s))
        k = -(-size // LANES)
        out.append(buf[r:r + k].reshape(-1)[:size].reshape(s))
        r += k
    return out


def _lanes_from_groups(a):
    return a.transpose(2, 0, 1).reshape(SSM_GROUP_CH, SSM_LANES)


def _groups_from_lanes(a):
    return a.reshape(SSM_GROUP_CH, SSM_GROUPS, SSM_STATE).transpose(1, 2, 0)


def _local_step(x3, mod, tgt3, W, P):
    B, S, _ = x3.shape
    T = B * S
    seq_blocks = S // ATT_BLOCK
    sh1, sc1, gt1, sh2, sc2, gt2 = [m.reshape(B, 1, D_MODEL) for m in jnp.split(mod, 6, axis=-1)]
    g_mix, g_ffn, g_final = P["g_mix"].reshape(1, D_MODEL), P["g_ffn"].reshape(1, D_MODEL), P["g_final"].reshape(1, D_MODEL)
    b_gate = P["b_gate"].reshape(1, 2 * D_MODEL)
    d_skip, b_glu = P["d_skip"].reshape(1, SSM_WIDTH), P["b_glu"].reshape(1, SSM_WIDTH)
    w_conv, b_conv = P["w_conv"], P["b_conv"].reshape(1, D_FF)

    u1 = _norm_mod(x3, g_mix, sc1, sh1).reshape(T, D_MODEL)
    proj = _mm(u1, W["w_in"], name="mm_proj", out_dtype=BF16)
    proj3 = proj.reshape(B, S, IN_WIDTH)
    us = proj[:, 3 * ATT_WIDTH:3 * ATT_WIDTH + SSM_WIDTH]
    o_att3, lse4 = _attention_fwd(proj3, seq_blocks)
    o_att = o_att3.reshape(T, ATT_WIDTH)
    y_att = _mm(o_att, W["w_proj_att"], name="mm_proj_att", out_dtype=BF16)

    lr = P["a_re"].reshape(1, SSM_LANES)
    li = P["a_im"].reshape(1, SSM_LANES)
    ldt = jnp.repeat(P["log_dt"], SSM_STATE).reshape(1, SSM_LANES)
    br, bi = _lanes_from_groups(P["b_re"]), _lanes_from_groups(P["b_im"])
    cr = P["c_re"].transpose(1, 0, 2).reshape(SSM_GROUP_CH, SSM_LANES)
    ci = P["c_im"].transpose(1, 0, 2).reshape(SSM_GROUP_CH, SSM_LANES)
    abar, w_bu, w_c = _ssm_params(lr, li, ldt, br, bi, cr, ci)
    bu = _mm(us, w_bu, name="mm_bu")
    xs = _scan_fwd(bu.reshape(B, S, 2 * SSM_LANES), abar).reshape(T, 2 * SSM_LANES)
    y_core = _mm(xs, w_c, tb=True, name="mm_ssm_out")
    y5, s_out = _ssm_post(y_core, us, d_skip, W["w_glu"], b_glu)
    y_ssm = _mm(s_out, W["w_proj_ssm"], name="mm_proj_ssm", out_dtype=BF16)

    merged = _merge(proj, y_att, y_ssm, b_gate)
    mix = _mm(merged, W["w_out"], name="mm_out", out_dtype=BF16)
    mix3 = mix.reshape(B, S, D_MODEL)

    h1, u2 = _resid_norm_mod(x3, mix3, gt1, g_ffn, sc2, sh2)
    u2 = u2.reshape(T, D_MODEL)
    up3 = _mm(u2, W["w_up"], name="mm_up", out_dtype=BF16).reshape(B, S, 2 * D_FF)
    act = _conv_act(up3, w_conv, b_conv).reshape(T, D_FF)
    ffn3 = _mm(act, W["w_down"], name="mm_down", out_dtype=BF16).reshape(B, S, D_MODEL)
    dh2, dffn, dgt2, dg_final, loss = _final_loss(h1, ffn3, tgt3, gt2, g_final)

    dffn = dffn.reshape(T, D_MODEL)
    gw = {}
    gw["w_down"] = _mm(act, dffn, ta=True, name="mm_dw_down")
    dact3 = _mm(dffn, W["w_down"], tb=True, name="mm_dact", out_dtype=BF16).reshape(B, S, D_FF)
    da3, dval3, dw_conv, db_conv = _conv_bwd(up3, dact3, w_conv, b_conv)
    dup = jnp.concatenate([da3.reshape(T, D_FF), dval3.reshape(T, D_FF)], axis=1)
    gw["w_up"] = _mm(u2, dup, ta=True, name="mm_dw_up")
    du2 = _mm(dup, W["w_up"], tb=True, name="mm_du2", out_dtype=BF16).reshape(B, S, D_MODEL)
    dh1, dsh2, dsc2, dg_ffn, dgt1, dmix = _norm_bwd(h1, du2, dh2, g_ffn, sc2, "norm_bwd2", mix3=mix3, gt=gt1)

    dmix = dmix.reshape(T, D_MODEL)
    gw["w_out"] = _mm(merged, dmix, ta=True, name="mm_dw_out")
    dmerged = _mm(dmix, W["w_out"], tb=True, name="mm_dmerged", out_dtype=BF16)
    dy_att, dy_ssm, dga, dgs, db_att, db_ssm = _merge_bwd(proj, y_att, y_ssm, b_gate, dmerged)

    gw["w_proj_ssm"] = _mm(s_out, dy_ssm, ta=True, name="mm_dw_proj_ssm")
    ds_out = _mm(dy_ssm, W["w_proj_ssm"], tb=True, name="mm_ds_out")
    dy5, dd_skip, db_glu, dw_glu = _ssm_post_bwd(y5, us, ds_out, d_skip, W["w_glu"], b_glu)
    gw["w_glu"] = dw_glu
    dxs = _mm(dy5, w_c, name="mm_dxs")
    dwc = _mm(dy5, xs, ta=True, name="mm_dwc")
    g3, dab = _scan_bwd(dxs.reshape(B, S, 2 * SSM_LANES), xs.reshape(B, S, 2 * SSM_LANES), abar)
    gs2 = g3.reshape(T, 2 * SSM_LANES)
    dwbu = _mm(us, gs2, ta=True, name="mm_dwbu")
    dus_core = _mm(gs2, w_bu, tb=True, name="mm_dus")
    dus = _add_scaled_cast(dus_core, dy5, d_skip)
    dlr, dli, dldt, dbr, dbi, dcr, dci = _ssm_params_bwd(lr, li, ldt, br, bi, dab, dwbu, dwc)

    gw["w_proj_att"] = _mm(o_att, dy_att, ta=True, name="mm_dw_proj_att")
    do_att = _mm(dy_att, W["w_proj_att"], tb=True, out_dtype=BF16, name="mm_do_att")
    dq3, dk3, dv3 = _attention_bwd(proj3, do_att.reshape(B, S, ATT_WIDTH), o_att3, lse4, seq_blocks)
    dproj = jnp.concatenate([t.reshape(T, ATT_WIDTH) for t in (dq3, dk3, dv3)] + [dus, dga, dgs], axis=1)
    gw["w_in"] = _mm(u1, dproj, ta=True, name="mm_dw_in")
    du1 = _mm(dproj, W["w_in"], tb=True, name="mm_du1", out_dtype=BF16).reshape(B, S, D_MODEL)
    dx, dsh1, dsc1, dg_mix = _norm_bwd(x3, du1, dh1, g_mix, sc1, "norm_bwd1")

    dmod = jnp.concatenate([t.reshape(B, D_MODEL) for t in (dsh1, dsc1, dgt1, dsh2, dsc2, dgt2)], axis=1)
    gs = dict(
        g_mix=dg_mix.reshape(D_MODEL), b_gate=jnp.concatenate([db_att, db_ssm], axis=1).reshape(2 * D_MODEL),
        a_re=dlr.reshape(SSM_GROUPS, SSM_STATE), a_im=dli.reshape(SSM_GROUPS, SSM_STATE), log_dt=dldt[0, :SSM_GROUPS],
        b_re=_groups_from_lanes(dbr), b_im=_groups_from_lanes(dbi),
        c_re=dcr.reshape(SSM_GROUP_CH, SSM_GROUPS, SSM_STATE).transpose(1, 0, 2),
        c_im=dci.reshape(SSM_GROUP_CH, SSM_GROUPS, SSM_STATE).transpose(1, 0, 2),
        d_skip=dd_skip.reshape(SSM_WIDTH), b_glu=db_glu.reshape(SSM_WIDTH), g_ffn=dg_ffn.reshape(D_MODEL),
        w_conv=dw_conv, b_conv=db_conv.reshape(D_FF), g_final=dg_final.reshape(D_MODEL))
    return loss, dx, dmod, gw, gs


WEIGHTS = ['w_ada', 'b_ada', 'g_mix', 'w_in', 'b_gate', 'a_re', 'a_im', 'log_dt', 'b_re', 'b_im', 'c_re', 'c_im', 'd_skip',
           'w_glu', 'b_glu', 'w_proj_att', 'w_proj_ssm', 'w_out', 'g_ffn', 'w_up', 'w_conv', 'b_conv', 'w_down', 'g_final']
SMALL = ['g_mix', 'b_gate', 'a_re', 'a_im', 'log_dt', 'b_re', 'b_im', 'c_re', 'c_im', 'd_skip', 'b_glu', 'g_ffn', 'w_conv',
         'b_conv', 'g_final']


def kernel(x, c, w_ada, b_ada, g_mix, w_in, b_gate, a_re, a_im, log_dt, b_re, b_im, c_re, c_im, d_skip, w_glu, b_glu, w_proj_att, w_proj_ssm, w_out, g_ffn, w_up, w_conv, b_conv, w_down, g_final, loss_target, m_w_ada, m_b_ada, m_g_mix, m_w_in, m_b_gate, m_a_re, m_a_im, m_log_dt, m_b_re, m_b_im, m_c_re, m_c_im, m_d_skip, m_w_glu, m_b_glu, m_w_proj_att, m_w_proj_ssm, m_w_out, m_g_ffn, m_w_up, m_w_conv, m_b_conv, m_w_down, m_g_final, v_w_ada, v_b_ada, v_g_mix, v_w_in, v_b_gate, v_a_re, v_a_im, v_log_dt, v_b_re, v_b_im, v_c_re, v_c_im, v_d_skip, v_w_glu, v_b_glu, v_w_proj_att, v_w_proj_ssm, v_w_out, v_g_ffn, v_w_up, v_w_conv, v_b_conv, v_w_down, v_g_final):
    args = dict(locals())
    w = {n: args[n] for n in WEIGHTS}
    m = {n: args["m_" + n] for n in WEIGHTS}
    v = {n: args["v_" + n] for n in WEIGHTS}
    B, S, _ = x.shape
    ix, iy, ic = lax.axis_index("x"), lax.axis_index("y"), lax.axis_index("c")
    chip = 2 * ix + iy
    half = FLAT_ROWS // 2
    ada_cols = w_ada.shape[2]

    c_all = _exchange(c, "all", "gather", "gather_c").reshape(8 * B, D_MODEL)
    b_cols = lax.dynamic_slice_in_dim(b_ada, chip * ada_cols, ada_cols, axis=1)
    mod_cols = _ada_fwd(c_all, w_ada[0], b_cols)
    mod_all = _exchange(mod_cols, "xy", "gather", "gather_mod")
    mod_all = mod_all.transpose(1, 0, 2).reshape(8 * B, 6 * D_MODEL)
    mod = lax.dynamic_slice_in_dim(mod_all, (4 * ix + 2 * iy + ic) * B, B, axis=0)

    flat = _flatten_shards({n: w[n][0] for n, _, _ in BIG}).astype(BF16)
    mine = lax.dynamic_slice_in_dim(flat, ic * half, half, axis=0)
    halves = _exchange(mine, "xy", "gather", "gather_w_chips")
    others = _exchange(halves, "c", "swap", "gather_w_cores")
    south = ic == 0
    W = _unflatten_full(jnp.concatenate([jnp.where(south, halves, others), jnp.where(south, others, halves)], axis=1))

    wc_all = _exchange(w_conv[0], "xy", "gather", "gather_w_conv")
    P = {n: w[n][0] for n in SMALL if n not in ("w_conv", "g_final")}
    P["w_conv"] = wc_all.transpose(1, 0, 2).reshape(3, D_FF)
    P["g_final"] = g_final

    loss, dx, dmod, gw, gs = _local_step(x, mod, loss_target, W, P)

    loss = lax.psum(loss[0, 0], MESH_AXES)

    small_shapes = [gs[n].shape for n in SMALL]
    packed, counts = _pack_rows([gs[n] for n in SMALL] + [dmod])
    n_small = sum(counts[:-1])
    gathered = _exchange(packed, "all", "gather", "gather_small")
    small_sum = _sum_slots(gathered[:, :n_small], "sum_small")
    g_small = dict(zip(SMALL, _unpack_rows(small_sum, small_shapes)))
    dmod_all = gathered[:, n_small:].reshape(8, -1)[:, :B * 6 * D_MODEL].reshape(8 * B, 6 * D_MODEL)
    dmod_cols = lax.dynamic_slice_in_dim(dmod_all, chip * ada_cols, ada_cols, axis=1)
    g_w_ada, g_b_ada = _ada_bwd(c_all, dmod_all, dmod_cols)

    G = _flatten_full(gw)
    theirs = _exchange(G, "c", "half", "reduce_cores")
    ours = lax.dynamic_slice_in_dim(G, ic * half, half, axis=1)
    pair = _add2(ours.reshape(N_XY * half, LANES), theirs.reshape(N_XY * half, LANES), BF16).reshape(N_XY, half, LANES)
    parts = _exchange(pair, "xy", "scatter", "reduce_chips")
    red = _sum_slots(parts, "sum_chips")
    red_sib = _exchange(red, "c", "swap", "share_cores")
    g_flat = jnp.concatenate([jnp.where(south, red, red_sib), jnp.where(south, red_sib, red)], axis=0)
    g_big = _unflatten_shard(g_flat)

    grads = {"w_ada": g_w_ada[None], "b_ada": g_b_ada}
    for n, _, _ in BIG:
        grads[n] = g_big[n][None]
    wc_cols = w_conv.shape[2]
    for n in SMALL:
        g = g_small[n]
        if n == "w_conv":
            g = lax.dynamic_slice_in_dim(g, chip * wc_cols, wc_cols, axis=1)
        grads[n] = g.reshape(w[n].shape)

    delta, new_m, new_v = {}, {}, {}
    for n in ["w_ada"] + [b for b, _, _ in BIG]:
        shp = w[n].shape
        d2, m2, v2 = _adamw(w[n][0], grads[n][0], m[n][0], v[n][0], "adamw_" + n)
        delta[n], new_m[n], new_v[n] = d2.reshape(shp), m2.reshape(shp), v2.reshape(shp)
    rest = ["b_ada"] + SMALL
    shapes = [w[n].shape for n in rest]
    pw, _ = _pack_rows([w[n] for n in rest])
    pg, _ = _pack_rows([grads[n] for n in rest])
    pm, _ = _pack_rows([m[n] for n in rest])
    pv, _ = _pack_rows([v[n] for n in rest])
    d2, m2, v2 = _adamw(pw, pg, pm, pv, "adamw_small")
    for n, dd, mm, vv in zip(rest, _unpack_rows(d2, shapes), _unpack_rows(m2, shapes), _unpack_rows(v2, shapes)):
        delta[n], new_m[n], new_v[n] = dd, mm, vv

    return (loss, dx, *[grads[n] for n in WEIGHTS], *[delta[n] for n in WEIGHTS], *[new_m[n] for n in WEIGHTS],
            *[new_v[n] for n in WEIGHTS])
```

```python
import functools
import math

import numpy as np
import jax
import jax.numpy as jnp
from jax import lax
from jax.experimental import pallas as pl
from jax.experimental.pallas import tpu as pltpu

F32, BF16 = jnp.float32, jnp.bfloat16

D_MODEL = 1024
N_HEADS = 8
HEAD_DIM = 64
ATT_WIDTH = 512
SSM_GROUPS = 16
SSM_GROUP_CH = 16
SSM_WIDTH = 256
SSM_STATE = 64
SSM_LANES = SSM_GROUPS * SSM_STATE
D_FF = 2048
IN_WIDTH = 3 * ATT_WIDTH + SSM_WIDTH + 2 * D_MODEL
ATT_BLOCK = 128
N_PATTERNS = 3
EPS = 1e-6
NEG_INF = -1e30

ADAM_LR, ADAM_B1, ADAM_B2, ADAM_EPS, ADAM_WD, ADAM_STEP = 0.001, 0.9, 0.999, 1e-08, 0.01, 10

V7X_VMEM_LIMIT_BYTES = 56 * 1024 * 1024
LANES = 1024

MESH_AXES = ("x", "y", "c")


def _pcall(body, *, name, out_shape, grid=(), in_specs=None, out_specs=None, scratch_shapes=(), dims=None):
    params = dict(vmem_limit_bytes=V7X_VMEM_LIMIT_BYTES)
    if dims is not None:
        params["dimension_semantics"] = dims
    specs = {}
    if in_specs is not None:
        specs = dict(grid=grid, in_specs=in_specs, out_specs=out_specs)
    return pl.pallas_call(body, name=name, out_shape=out_shape, scratch_shapes=scratch_shapes,
                          compiler_params=pltpu.CompilerParams(**params), **specs)


def _sds(shape, dtype):
    return jax.ShapeDtypeStruct(tuple(shape), dtype)


def _tile(n, target):
    if n <= target:
        return n
    for t in range(target - target % 128, 0, -128):
        if n % t == 0:
            return t
    raise ValueError((n, target))


def _sig(v):
    return 1.0 / (1.0 + jnp.exp(-v))


def _mm(a, b, *, name, ta=False, tb=False, out_dtype=F32, tm=2048, tn=1024, tk=1024):
    if ta:
        K, M = a.shape
    else:
        M, K = a.shape
    if tb:
        N, K2 = b.shape
    else:
        K2, N = b.shape
    assert K == K2, (a.shape, b.shape)
    tm, tn, tk = _tile(M, tm), _tile(N, tn), _tile(K, tk)
    nk = K // tk
    a_spec = pl.BlockSpec((tk, tm), lambda i, j, k: (k, i)) if ta else pl.BlockSpec((tm, tk), lambda i, j, k: (i, k))
    b_spec = pl.BlockSpec((tn, tk), lambda i, j, k: (j, k)) if tb else pl.BlockSpec((tk, tn), lambda i, j, k: (k, j))
    dn = (((0 if ta else 1,), (1 if tb else 0,)), ((), ()))

    def body(a_ref, b_ref, o_ref, acc_ref):
        k = pl.program_id(2)

        @pl.when(k == 0)
        def _():
            acc_ref[...] = jnp.zeros_like(acc_ref)

        acc_ref[...] += lax.dot_general(a_ref[...].astype(BF16), b_ref[...].astype(BF16), dn,
                                        preferred_element_type=F32)

        @pl.when(k == nk - 1)
        def _():
            o_ref[...] = acc_ref[...].astype(out_dtype)

    def body_single(a_ref, b_ref, o_ref):
        o_ref[...] = lax.dot_general(a_ref[...].astype(BF16), b_ref[...].astype(BF16), dn,
                                     preferred_element_type=F32).astype(out_dtype)

    return _pcall(body_single if nk == 1 else body, name=name, out_shape=_sds((M, N), out_dtype),
                  grid=(M // tm, N // tn, nk), in_specs=[a_spec, b_spec],
                  out_specs=pl.BlockSpec((tm, tn), lambda i, j, k: (i, j)),
                  scratch_shapes=[] if nk == 1 else [pltpu.VMEM((tm, tn), F32)],
                  dims=("parallel", "parallel", "arbitrary"))(a, b)


def _ada_fwd(c_all, w_ada, b_ada_cols):
    n = w_ada.shape[1]

    def body(c_ref, w_ref, b_ref, o_ref):
        c = c_ref[...]
        act = c * _sig(c)
        o_ref[...] = jnp.dot(act.astype(BF16), w_ref[...].astype(BF16), preferred_element_type=F32) + b_ref[...]

    return _pcall(body, name="ada_fwd", out_shape=_sds((c_all.shape[0], n), F32))(c_all, w_ada, b_ada_cols)


def _ada_bwd(c_all, dmod_all, dmod_cols):
    n = dmod_cols.shape[1]

    def body(c_ref, da_ref, dc_ref, gw_ref, gb_ref):
        c = c_ref[...]
        act = c * _sig(c)
        gw_ref[...] = lax.dot_general(act, dc_ref[...], (((0,), (0,)), ((), ())), preferred_element_type=F32,
                                      precision=lax.Precision.HIGHEST)
        gb_ref[...] = jnp.sum(da_ref[...], axis=0, keepdims=True)

    return _pcall(body, name="ada_bwd", out_shape=(_sds((D_MODEL, n), F32), _sds((1, dmod_all.shape[1]), F32)))(
        c_all, dmod_all, dmod_cols)


ROW_TILE = 512


def _row_specs(B, S):
    ts = min(S, ROW_TILE)
    row = pl.BlockSpec((1, ts, D_MODEL), lambda b, s: (b, s, 0))
    bvec = pl.BlockSpec((1, 1, D_MODEL), lambda b, s: (b, 0, 0))
    gvec = pl.BlockSpec((1, D_MODEL), lambda b, s: (0, 0))
    return ts, row, bvec, gvec


def _norm_mod(x3, g, sc, sh):
    B, S, _ = x3.shape
    ts, row, bvec, gvec = _row_specs(B, S)

    def body(x_ref, g_ref, sc_ref, sh_ref, u_ref):
        x = x_ref[0]
        r = lax.rsqrt(jnp.mean(x * x, axis=-1, keepdims=True) + EPS)
        u_ref[0] = ((x * r) * g_ref[...] * (1.0 + sc_ref[0]) + sh_ref[0]).astype(BF16)

    return _pcall(body, name="norm_mod1", out_shape=_sds(x3.shape, BF16), grid=(B, S // ts),
                  in_specs=[row, gvec, bvec, bvec], out_specs=row, dims=("parallel", "parallel"))(x3, g, sc, sh)


def _resid_norm_mod(x3, mix3, gt, g, sc, sh):
    B, S, _ = x3.shape
    ts, row, bvec, gvec = _row_specs(B, S)

    def body(x_ref, m_ref, gt_ref, g_ref, sc_ref, sh_ref, h_ref, u_ref):
        h = x_ref[0] + gt_ref[0] * m_ref[0]
        h_ref[0] = h
        r = lax.rsqrt(jnp.mean(h * h, axis=-1, keepdims=True) + EPS)
        u_ref[0] = ((h * r) * g_ref[...] * (1.0 + sc_ref[0]) + sh_ref[0]).astype(BF16)

    return _pcall(body, name="resid_norm_mod2", out_shape=(_sds(x3.shape, F32), _sds(x3.shape, BF16)),
                  grid=(B, S // ts), in_specs=[row, row, bvec, gvec, bvec, bvec], out_specs=(row, row),
                  dims=("parallel", "parallel"))(x3, mix3, gt, g, sc, sh)


def _norm_bwd(h3, du3, dres3, g, sc, name, mix3=None, gt=None):
    B, S, _ = h3.shape
    ts, row, bvec, gvec = _row_specs(B, S)
    with_gate = mix3 is not None

    def body(*refs):
        if with_gate:
            h_ref, du_ref, dr_ref, g_ref, sc_ref, m_ref, gt_ref, dh_ref, dsh_ref, dsc_ref, dg_ref, dgt_ref, dm_ref = refs
        else:
            h_ref, du_ref, dr_ref, g_ref, sc_ref, dh_ref, dsh_ref, dsc_ref, dg_ref = refs
        b, s = pl.program_id(0), pl.program_id(1)
        h = h_ref[0]
        r = lax.rsqrt(jnp.mean(h * h, axis=-1, keepdims=True) + EPS)
        xn = h * r
        du = du_ref[0].astype(F32)
        g = g_ref[...]
        sc1 = 1.0 + sc_ref[0]
        dxn = du * g * sc1
        dh = dr_ref[0] + r * (dxn - xn * jnp.mean(dxn * xn, axis=-1, keepdims=True))
        dh_ref[0] = dh

        @pl.when(s == 0)
        def _():
            dsh_ref[...] = jnp.zeros_like(dsh_ref)
            dsc_ref[...] = jnp.zeros_like(dsc_ref)
            if with_gate:
                dgt_ref[...] = jnp.zeros_like(dgt_ref)

        @pl.when((s == 0) & (b == 0))
        def _():
            dg_ref[...] = jnp.zeros_like(dg_ref)

        dux = du * xn
        dsh_ref[0] += jnp.sum(du, axis=0, keepdims=True)
        dsc_ref[0] += jnp.sum(dux * g, axis=0, keepdims=True)
        dg_ref[...] += jnp.sum(dux * sc1, axis=0, keepdims=True)
        if with_gate:
            dgt_ref[0] += jnp.sum(dh * m_ref[0], axis=0, keepdims=True)
            dm_ref[0] = (dh * gt_ref[0]).astype(BF16)

    bshape = _sds((B, 1, D_MODEL), F32)
    in_specs = [row, row, row, gvec, bvec]
    out_shape = [_sds(h3.shape, F32), bshape, bshape, _sds((1, D_MODEL), F32)]
    out_specs = [row, bvec, bvec, gvec]
    args = [h3, du3, dres3, g, sc]
    if with_gate:
        in_specs += [row, bvec]
        out_shape += [bshape, _sds(h3.shape, BF16)]
        out_specs += [bvec, row]
        args += [mix3, gt]
    return _pcall(body, name=name, out_shape=tuple(out_shape), grid=(B, S // ts), in_specs=in_specs,
                  out_specs=tuple(out_specs), dims=("arbitrary", "arbitrary"))(*args)


def _final_loss(h1, ffn3, tgt3, gt, gfin):
    B, S, _ = h1.shape
    ts, row, bvec, gvec = _row_specs(B, S)
    one = pl.BlockSpec((1, 1), lambda b, s: (0, 0))

    def body(h_ref, f_ref, t_ref, gt_ref, gf_ref, dh_ref, dff_ref, dgt_ref, dgf_ref, loss_ref):
        b, s = pl.program_id(0), pl.program_id(1)
        f = f_ref[0].astype(F32)
        gtv = gt_ref[0]
        gf = gf_ref[...]
        h2 = h_ref[0] + gtv * f
        r = lax.rsqrt(jnp.mean(h2 * h2, axis=-1, keepdims=True) + EPS)
        n = h2 * r
        e = n * gf - t_ref[0]
        dy = e * (1.0 / D_MODEL)
        dn = dy * gf
        dh2 = r * (dn - n * jnp.mean(dn * n, axis=-1, keepdims=True))
        dh_ref[0] = dh2
        dff_ref[0] = (dh2 * gtv).astype(BF16)

        @pl.when(s == 0)
        def _():
            dgt_ref[...] = jnp.zeros_like(dgt_ref)

        @pl.when((s == 0) & (b == 0))
        def _():
            dgf_ref[...] = jnp.zeros_like(dgf_ref)
            loss_ref[...] = jnp.zeros_like(loss_ref)

        dgt_ref[0] += jnp.sum(dh2 * f, axis=0, keepdims=True)
        dgf_ref[...] += jnp.sum(dy * n, axis=0, keepdims=True)
        rows = jnp.sum(e * e, axis=1, keepdims=True)
        loss_ref[...] += jnp.sum(rows, axis=0, keepdims=True) * (0.5 / D_MODEL)

    return _pcall(body, name="final_loss",
                  out_shape=(_sds(h1.shape, F32), _sds(h1.shape, BF16), _sds((B, 1, D_MODEL), F32),
                             _sds((1, D_MODEL), F32), _sds((1, 1), F32)),
                  grid=(B, S // ts), in_specs=[row, row, row, bvec, gvec], out_specs=(row, row, bvec, gvec, one),
                  dims=("arbitrary", "arbitrary"))(h1, ffn3, tgt3, gt, gfin)


def _att_scores(qh, kc, kp, h, dil, first, a_idx, j_idx):
    scale = HEAD_DIM ** -0.5
    nt = (((1,), (1,)), ((), ()))
    slope = (2.0 ** (-8.0 * (h + 1) / N_HEADS)) * dil
    dist_c = (a_idx - j_idx).astype(F32)
    s_c = lax.dot_general(qh, kc, nt, preferred_element_type=F32) * scale
    s_c = jnp.where(a_idx >= j_idx, s_c - slope * dist_c, NEG_INF)
    s_p = lax.dot_general(qh, kp, nt, preferred_element_type=F32) * scale
    s_p = jnp.where((j_idx >= a_idx) & jnp.logical_not(first), s_p - slope * (dist_c + float(ATT_BLOCK)), NEG_INF)
    return s_c, s_p


def _att_block_consts(seq_blocks):
    p = pl.program_id(0)
    j = pl.program_id(1)
    nb = lax.shift_right_logical(jnp.int32(seq_blocks), 2 * p)
    dil = lax.shift_left(jnp.int32(1), 2 * p).astype(F32)
    a_idx = lax.broadcasted_iota(jnp.int32, (ATT_BLOCK, ATT_BLOCK), 0)
    j_idx = lax.broadcasted_iota(jnp.int32, (ATT_BLOCK, ATT_BLOCK), 1)
    return j, nb, dil, a_idx, j_idx


def _attn_fwd(qb, kb, vb, seq_blocks):
    _, NB, _, _ = qb.shape
    cur = pl.BlockSpec((None, None, ATT_BLOCK, ATT_WIDTH), lambda p, j: (p, j, 0, 0))
    prev = pl.BlockSpec((None, None, ATT_BLOCK, ATT_WIDTH), lambda p, j: (p, jnp.maximum(j - 1, 0), 0, 0))
    lse_spec = pl.BlockSpec((None, None, ATT_BLOCK, N_HEADS), lambda p, j: (p, j, 0, 0))

    def body(q_ref, kc_ref, kp_ref, vc_ref, vp_ref, o_ref, lse_ref):
        j, nb, dil, a_idx, j_idx = _att_block_consts(seq_blocks)
        first = lax.rem(j, nb) == 0
        for h in range(N_HEADS):
            hs = slice(h * HEAD_DIM, (h + 1) * HEAD_DIM)
            s_c, s_p = _att_scores(q_ref[:, hs], kc_ref[:, hs], kp_ref[:, hs], h, dil, first, a_idx, j_idx)
            m = jnp.maximum(jnp.max(s_c, axis=1, keepdims=True), jnp.max(s_p, axis=1, keepdims=True))
            p_c = jnp.exp(s_c - m)
            p_p = jnp.exp(s_p - m)
            den = jnp.sum(p_c, axis=1, keepdims=True) + jnp.sum(p_p, axis=1, keepdims=True)
            o = (jnp.dot(p_c.astype(BF16), vc_ref[:, hs], preferred_element_type=F32)
                 + jnp.dot(p_p.astype(BF16), vp_ref[:, hs], preferred_element_type=F32))
            o_ref[:, hs] = o / den
            lse_ref[:, h:h + 1] = m + jnp.log(den)

    return _pcall(body, name="attn_fwd",
                  out_shape=(_sds(qb.shape, F32), _sds((N_PATTERNS, NB, ATT_BLOCK, N_HEADS), F32)),
                  grid=(N_PATTERNS, NB), in_specs=[cur, cur, prev, cur, prev], out_specs=(cur, lse_spec),
                  dims=("parallel", "parallel"))(qb, kb, kb, vb, vb)


def _attn_combine(o_p, lse_p):
    _, T, _ = o_p.shape
    tm = min(T, 1024)

    def body(o_ref, l_ref, out_ref, lse_ref):
        l0, l1, l2 = l_ref[0], l_ref[1], l_ref[2]
        m = jnp.maximum(jnp.maximum(l0, l1), l2)
        lse = m + jnp.log(jnp.exp(l0 - m) + jnp.exp(l1 - m) + jnp.exp(l2 - m))
        lse_ref[...] = lse
        w = [jnp.exp(l0 - lse), jnp.exp(l1 - lse), jnp.exp(l2 - lse)]
        for h in range(N_HEADS):
            hs = slice(h * HEAD_DIM, (h + 1) * HEAD_DIM)
            acc = w[0][:, h:h + 1] * o_ref[0, :, hs]
            acc = acc + w[1][:, h:h + 1] * o_ref[1, :, hs]
            acc = acc + w[2][:, h:h + 1] * o_ref[2, :, hs]
            out_ref[:, hs] = acc.astype(BF16)

    return _pcall(body, name="attn_combine", out_shape=(_sds((T, ATT_WIDTH), BF16), _sds((T, N_HEADS), F32)),
                  grid=(T // tm,),
                  in_specs=[pl.BlockSpec((N_PATTERNS, tm, ATT_WIDTH), lambda i: (0, i, 0)),
                            pl.BlockSpec((N_PATTERNS, tm, N_HEADS), lambda i: (0, i, 0))],
                  out_specs=(pl.BlockSpec((tm, ATT_WIDTH), lambda i: (i, 0)), pl.BlockSpec((tm, N_HEADS), lambda i: (i, 0))),
                  dims=("parallel",))(o_p, lse_p)


def _attn_bwd(qb, kb, vb, dob, ob, lseb, seq_blocks):
    _, NB, _, _ = qb.shape
    last = NB - 1
    cur = pl.BlockSpec((None, None, ATT_BLOCK, ATT_WIDTH), lambda p, j: (p, jnp.minimum(j, last), 0, 0))
    prev = pl.BlockSpec((None, None, ATT_BLOCK, ATT_WIDTH),
                        lambda p, j: (p, jnp.maximum(jnp.minimum(j, last) - 1, 0), 0, 0))
    lag = pl.BlockSpec((None, None, ATT_BLOCK, ATT_WIDTH), lambda p, j: (p, jnp.maximum(j - 1, 0), 0, 0))
    lse_spec = pl.BlockSpec((None, None, ATT_BLOCK, N_HEADS), lambda p, j: (p, jnp.minimum(j, last), 0, 0))
    scale = HEAD_DIM ** -0.5
    tn = (((0,), (0,)), ((), ()))
    nt = (((1,), (1,)), ((), ()))

    def body(q_ref, kc_ref, kp_ref, vc_ref, vp_ref, do_ref, o_ref, lse_ref, dq_ref, dk_ref, dv_ref, ck_ref, cv_ref):
        j, nb, dil, a_idx, j_idx = _att_block_consts(seq_blocks)

        @pl.when(j == 0)
        def _():
            ck_ref[...] = jnp.zeros_like(ck_ref)
            cv_ref[...] = jnp.zeros_like(cv_ref)

        @pl.when(j <= last)
        def _():
            first = lax.rem(j, nb) == 0
            for h in range(N_HEADS):
                hs = slice(h * HEAD_DIM, (h + 1) * HEAD_DIM)
                qh, kc, kp, vc, vp, doh = q_ref[:, hs], kc_ref[:, hs], kp_ref[:, hs], vc_ref[:, hs], vp_ref[:, hs], do_ref[:, hs]
                s_c, s_p = _att_scores(qh, kc, kp, h, dil, first, a_idx, j_idx)
                lse = lse_ref[:, h:h + 1]
                p_c = jnp.exp(s_c - lse)
                p_p = jnp.exp(s_p - lse)
                delta = jnp.sum(doh.astype(F32) * o_ref[:, hs].astype(F32), axis=1, keepdims=True)
                ds_c = (p_c * (lax.dot_general(doh, vc, nt, preferred_element_type=F32) - delta)).astype(BF16)
                ds_p = (p_p * (lax.dot_general(doh, vp, nt, preferred_element_type=F32) - delta)).astype(BF16)
                dq_ref[:, hs] = (jnp.dot(ds_c, kc, preferred_element_type=F32)
                                 + jnp.dot(ds_p, kp, preferred_element_type=F32)) * scale
                dk_ref[:, hs] = ck_ref[:, hs] + lax.dot_general(ds_p, qh, tn, preferred_element_type=F32) * scale
                dv_ref[:, hs] = cv_ref[:, hs] + lax.dot_general(p_p.astype(BF16), doh, tn, preferred_element_type=F32)
                ck_ref[:, hs] = lax.dot_general(ds_c, qh, tn, preferred_element_type=F32) * scale
                cv_ref[:, hs] = lax.dot_general(p_c.astype(BF16), doh, tn, preferred_element_type=F32)

        @pl.when(j == NB)
        def _():
            dk_ref[...] = ck_ref[...]
            dv_ref[...] = cv_ref[...]

    shp = _sds(qb.shape, F32)
    return _pcall(body, name="attn_bwd", out_shape=(shp, shp, shp), grid=(N_PATTERNS, NB + 1),
                  in_specs=[cur, cur, prev, cur, prev, cur, cur, lse_spec], out_specs=(cur, lag, lag),
                  scratch_shapes=[pltpu.VMEM((ATT_BLOCK, ATT_WIDTH), F32), pltpu.VMEM((ATT_BLOCK, ATT_WIDTH), F32)],
                  dims=("arbitrary", "arbitrary"))(qb, kb, kb, vb, vb, dob, ob, lseb)


def _sum3_cast(a, b, c):
    T, N = a.shape
    tm = min(T, 1024)
    spec = pl.BlockSpec((tm, N), lambda i: (i, 0))

    def body(a_ref, b_ref, c_ref, o_ref):
        o_ref[...] = (a_ref[...] + b_ref[...] + c_ref[...]).astype(BF16)

    return _pcall(body, name="sum3_cast", out_shape=_sds((T, N), BF16), grid=(T // tm,), in_specs=[spec] * 3,
                  out_specs=spec, dims=("parallel",))(a, b, c)


def _to_blocks(t, B, S):
    C = t.shape[-1]
    outs = []
    for p in range(N_PATTERNS):
        d = 4 ** p
        u = t.reshape(B, S // d, d, C).transpose(0, 2, 1, 3)
        outs.append(u.reshape(B * S // ATT_BLOCK, ATT_BLOCK, C))
    return jnp.stack(outs, axis=0)


def _from_blocks(tb, B, S):
    C = tb.shape[-1]
    outs = []
    for p in range(N_PATTERNS):
        d = 4 ** p
        u = tb[p].reshape(B, d, S // d, C).transpose(0, 2, 1, 3)
        outs.append(u.reshape(B * S, C))
    return jnp.stack(outs, axis=0)


ATT_GROUP = 4
ATT_GW = ATT_GROUP * HEAD_DIM
ATT_GROUPS = N_HEADS // ATT_GROUP
ATT_PAIRS = ATT_GW // ATT_BLOCK
ATT_UNROLL = 3
NT_DIMS = (((1,), (1,)), ((), ()))
TN_DIMS = (((0,), (0,)), ((), ()))


def _att_rows(start, d):
    if d == 1:
        return pl.ds(start if isinstance(start, int) else pl.multiple_of(start, ATT_BLOCK), ATT_BLOCK)
    return pl.ds(start, ATT_BLOCK, stride=d)


def _att_fill_bias(bias_ref, g, d):
    a = lax.broadcasted_iota(jnp.int32, (ATT_BLOCK, ATT_BLOCK), 0)
    j = lax.broadcasted_iota(jnp.int32, (ATT_BLOCK, ATT_BLOCK), 1)
    dist = (a - j).astype(F32)
    for hh in range(ATT_GROUP):
        t, e = divmod(hh, 2)
        rs = slice(e * ATT_BLOCK, (e + 1) * ATT_BLOCK)
        lo = 2.0 ** (-8.0 * (hh + 1) / N_HEADS) * d
        hi = 2.0 ** (-8.0 * (ATT_GROUP + hh + 1) / N_HEADS) * d
        slope = jnp.where(g == 0, lo, hi).astype(F32)
        bias_ref[t, rs, 0:ATT_BLOCK] = jnp.where(a >= j, -slope * dist, NEG_INF)
        bias_ref[t, rs, ATT_BLOCK:] = jnp.where(j >= a, -slope * (dist + float(ATT_BLOCK)), NEG_INF)


def _stack_heads(v2, low):
    return jnp.concatenate([jnp.where(low, v2, 0.0), jnp.where(low, 0.0, v2)], axis=0).astype(BF16)


def _unstack_heads(r2, low):
    return jnp.where(low, r2[0:ATT_BLOCK], r2[ATT_BLOCK:])


def _attention_fwd(proj3, seq_blocks):
    B, S, _ = proj3.shape
    scale = HEAD_DIM ** -0.5
    nq = ATT_WIDTH // ATT_GW

    def col(k):
        return pl.BlockSpec((1, S, ATT_GW), lambda b, g, k=k: (b, 0, k * nq + g))

    o_spec = pl.BlockSpec((1, S, ATT_GW), lambda b, g: (b, 0, g))
    l_spec = pl.BlockSpec((1, 1, S, ATT_BLOCK), lambda b, g: (b, g, 0, 0))

    def body(q_ref, k_ref, v_ref, o_ref, lse_ref, qf, kf, vf, os, ls, bias):
        g = pl.program_id(1)
        for t in range(ATT_PAIRS):
            ts = slice(t * ATT_BLOCK, (t + 1) * ATT_BLOCK)
            qf[t] = q_ref[0, :, ts].astype(F32) * scale
            kf[t] = k_ref[0, :, ts].astype(F32)
            vf[t] = v_ref[0, :, ts].astype(F32)
        lane = lax.broadcasted_iota(jnp.int32, (ATT_BLOCK, ATT_BLOCK), 1)
        low = lane < HEAD_DIM

        def block(p, d, r, n, has_prev):
            start = n * (ATT_BLOCK * d) + r
            rows = _att_rows(start, d)
            prows = _att_rows(start - ATT_BLOCK * d, d) if has_prev else None
            lse_t = jnp.zeros((ATT_BLOCK, ATT_BLOCK), F32)
            for t in range(ATT_PAIRS):
                q2 = _stack_heads(qf[t, rows, :], low)
                k2 = kf[t, rows, :].astype(BF16)
                v2 = vf[t, rows, :].astype(BF16)
                if has_prev:
                    k2 = jnp.concatenate([k2, kf[t, prows, :].astype(BF16)], axis=0)
                    v2 = jnp.concatenate([v2, vf[t, prows, :].astype(BF16)], axis=0)
                    b2 = bias[t]
                else:
                    b2 = bias[t, :, 0:ATT_BLOCK]
                s = lax.dot_general(q2, k2, NT_DIMS, preferred_element_type=F32) + b2
                m = jnp.max(s, axis=1, keepdims=True)
                pr = jnp.exp(s - m)
                den = jnp.sum(pr, axis=1, keepdims=True)
                o = jnp.dot(pr.astype(BF16), v2, preferred_element_type=F32) / den
                os[p, t, rows, :] = _unstack_heads(o, low)
                lse2 = m + jnp.log(den)
                lse_t = jnp.where(lane == 2 * t, lse2[0:ATT_BLOCK], lse_t)
                lse_t = jnp.where(lane == 2 * t + 1, lse2[ATT_BLOCK:], lse_t)
            ls[p, rows, :] = lse_t

        for p in range(N_PATTERNS):
            d = 4 ** p
            _att_fill_bias(bias, g, d)
            _att_one_pattern(block, p, d, seq_blocks // d)

        def combine(i, carry):
            rows = pl.ds(pl.multiple_of(i * ATT_BLOCK, ATT_BLOCK), ATT_BLOCK)
            l0, l1, l2 = ls[0, rows, :], ls[1, rows, :], ls[2, rows, :]
            m = jnp.maximum(jnp.maximum(l0, l1), l2)
            lse = m + jnp.log(jnp.exp(l0 - m) + jnp.exp(l1 - m) + jnp.exp(l2 - m))
            lse_ref[0, 0, rows, :] = lse
            w = [jnp.exp(l0 - lse), jnp.exp(l1 - lse), jnp.exp(l2 - lse)]
            for t in range(ATT_PAIRS):
                acc = jnp.zeros((ATT_BLOCK, ATT_BLOCK), F32)
                for p in range(N_PATTERNS):
                    wt = jnp.where(low, w[p][:, 2 * t:2 * t + 1], w[p][:, 2 * t + 1:2 * t + 2])
                    acc = acc + wt * os[p, t, rows, :]
                o_ref[0, rows, t * ATT_BLOCK:(t + 1) * ATT_BLOCK] = acc.astype(BF16)
            return carry

        lax.fori_loop(0, S // ATT_BLOCK, combine, 0, unroll=2)

    return _pcall(body, name="attention_fwd",
                  out_shape=(_sds((B, S, ATT_WIDTH), BF16), _sds((B, ATT_GROUPS, S, ATT_BLOCK), F32)),
                  grid=(B, ATT_GROUPS), in_specs=[col(0), col(1), col(2)], out_specs=(o_spec, l_spec),
                  scratch_shapes=[pltpu.VMEM((ATT_PAIRS, S, ATT_BLOCK), F32)] * 3
                  + [pltpu.VMEM((N_PATTERNS, ATT_PAIRS, S, ATT_BLOCK), F32), pltpu.VMEM((N_PATTERNS, S, ATT_BLOCK), F32),
                     pltpu.VMEM((ATT_PAIRS, 2 * ATT_BLOCK, 2 * ATT_BLOCK), F32)],
                  dims=("parallel", "parallel"))(proj3, proj3, proj3)


def _att_one_pattern(block, p, d, nb):
    def per_residue(r, carry):
        block(p, d, r, 0, False)
        if nb > 1:
            def per_block(n, c2):
                block(p, d, r, n, True)
                return c2
            lax.fori_loop(1, nb, per_block, 0, unroll=ATT_UNROLL)
        return carry

    if d == 1:
        per_residue(0, 0)
    else:
        lax.fori_loop(0, d, per_residue, 0, unroll=ATT_UNROLL + 1 if nb == 1 else 1)


def _attention_bwd(proj3, do3, o3, lse4, seq_blocks):
    B, S, _ = proj3.shape
    scale = HEAD_DIM ** -0.5
    nq = ATT_WIDTH // ATT_GW

    def col(k):
        return pl.BlockSpec((1, S, ATT_GW), lambda b, g, k=k: (b, 0, k * nq + g))

    o_spec = pl.BlockSpec((1, S, ATT_GW), lambda b, g: (b, 0, g))
    l_spec = pl.BlockSpec((1, 1, S, ATT_BLOCK), lambda b, g: (b, g, 0, 0))

    def body(q_ref, k_ref, v_ref, do_ref, o_ref, lse_ref, dq_ref, dk_ref, dv_ref,
             qf, kf, vf, dof, dl, aq, ak, av, bias):
        g = pl.program_id(1)
        for t in range(ATT_PAIRS):
            ts = slice(t * ATT_BLOCK, (t + 1) * ATT_BLOCK)
            qf[t] = q_ref[0, :, ts].astype(F32) * scale
            kf[t] = k_ref[0, :, ts].astype(F32)
            vf[t] = v_ref[0, :, ts].astype(F32)
            dof[t] = do_ref[0, :, ts].astype(F32)
        aq[...] = jnp.zeros_like(aq)
        ak[...] = jnp.zeros_like(ak)
        av[...] = jnp.zeros_like(av)
        lane = lax.broadcasted_iota(jnp.int32, (ATT_BLOCK, ATT_BLOCK), 1)
        low = lane < HEAD_DIM

        def fill_delta(i, carry):
            rows = pl.ds(pl.multiple_of(i * ATT_BLOCK, ATT_BLOCK), ATT_BLOCK)
            acc = jnp.zeros((ATT_BLOCK, ATT_BLOCK), F32)
            for t in range(ATT_PAIRS):
                prod = dof[t, rows, :] * o_ref[0, rows, t * ATT_BLOCK:(t + 1) * ATT_BLOCK].astype(F32)
                lo = jnp.sum(jnp.where(low, prod, 0.0), axis=1, keepdims=True)
                hi = jnp.sum(prod, axis=1, keepdims=True) - lo
                acc = jnp.where(lane == 2 * t, lo, acc)
                acc = jnp.where(lane == 2 * t + 1, hi, acc)
            dl[rows, :] = acc
            return carry

        lax.fori_loop(0, S // ATT_BLOCK, fill_delta, 0, unroll=2)

        def block(p, d, r, n, has_prev):
            start = n * (ATT_BLOCK * d) + r
            rows = _att_rows(start, d)
            prows = _att_rows(start - ATT_BLOCK * d, d) if has_prev else None
            lse_t = lse_ref[0, 0, rows, :]
            dl_t = dl[rows, :]
            for t in range(ATT_PAIRS):
                q2 = _stack_heads(qf[t, rows, :], low)
                do2 = _stack_heads(dof[t, rows, :], low)
                k2 = kf[t, rows, :].astype(BF16)
                v2 = vf[t, rows, :].astype(BF16)
                if has_prev:
                    k2 = jnp.concatenate([k2, kf[t, prows, :].astype(BF16)], axis=0)
                    v2 = jnp.concatenate([v2, vf[t, prows, :].astype(BF16)], axis=0)
                    b2 = bias[t]
                else:
                    b2 = bias[t, :, 0:ATT_BLOCK]
                lse2 = jnp.concatenate([lse_t[:, 2 * t:2 * t + 1], lse_t[:, 2 * t + 1:2 * t + 2]], axis=0)
                dl2 = jnp.concatenate([dl_t[:, 2 * t:2 * t + 1], dl_t[:, 2 * t + 1:2 * t + 2]], axis=0)
                s = lax.dot_general(q2, k2, NT_DIMS, preferred_element_type=F32) + b2
                pr = jnp.exp(s - lse2)
                ds = (pr * (lax.dot_general(do2, v2, NT_DIMS, preferred_element_type=F32) - dl2)).astype(BF16)
                dq = _unstack_heads(jnp.dot(ds, k2, preferred_element_type=F32), low)
                dk = lax.dot_general(ds, q2, TN_DIMS, preferred_element_type=F32)
                dv = lax.dot_general(pr.astype(BF16), do2, TN_DIMS, preferred_element_type=F32)
                aq[t, rows, :] = aq[t, rows, :] + dq * scale
                ak[t, rows, :] = ak[t, rows, :] + dk[0:ATT_BLOCK]
                av[t, rows, :] = av[t, rows, :] + dv[0:ATT_BLOCK]
                if has_prev:
                    ak[t, prows, :] = ak[t, prows, :] + dk[ATT_BLOCK:]
                    av[t, prows, :] = av[t, prows, :] + dv[ATT_BLOCK:]

        for p in range(N_PATTERNS):
            d = 4 ** p
            _att_fill_bias(bias, g, d)
            _att_one_pattern(block, p, d, seq_blocks // d)

        for t in range(ATT_PAIRS):
            ts = slice(t * ATT_BLOCK, (t + 1) * ATT_BLOCK)
            dq_ref[0, :, ts] = aq[t].astype(BF16)
            dk_ref[0, :, ts] = ak[t].astype(BF16)
            dv_ref[0, :, ts] = av[t].astype(BF16)

    shp = _sds((B, S, ATT_WIDTH), BF16)
    pair_buf = pltpu.VMEM((ATT_PAIRS, S, ATT_BLOCK), F32)
    return _pcall(body, name="attention_bwd", out_shape=(shp, shp, shp), grid=(B, ATT_GROUPS),
                  in_specs=[col(0), col(1), col(2), o_spec, o_spec, l_spec], out_specs=(o_spec, o_spec, o_spec),
                  scratch_shapes=[pair_buf] * 4 + [pltpu.VMEM((S, ATT_BLOCK), F32)] + [pair_buf] * 3
                  + [pltpu.VMEM((ATT_PAIRS, 2 * ATT_BLOCK, 2 * ATT_BLOCK), F32)],
                  dims=("parallel", "parallel"))(proj3, proj3, proj3, do3, o3, lse4)


def _expand_groups(m):
    rows = SSM_WIDTH
    t = jnp.concatenate([m] * SSM_GROUPS, axis=0)
    r = lax.broadcasted_iota(jnp.int32, (rows, SSM_LANES), 0)
    l = lax.broadcasted_iota(jnp.int32, (rows, SSM_LANES), 1)
    keep = lax.shift_right_logical(r, 4) == lax.shift_right_logical(l, 6)
    return jnp.where(keep, t, 0.0)


def _collapse_groups(m):
    rows = SSM_WIDTH
    r = lax.broadcasted_iota(jnp.int32, (rows, SSM_LANES), 0)
    l = lax.broadcasted_iota(jnp.int32, (rows, SSM_LANES), 1)
    keep = lax.shift_right_logical(r, 4) == lax.shift_right_logical(l, 6)
    t = jnp.where(keep, m, 0.0)
    acc = t[0:SSM_GROUP_CH]
    for g in range(1, SSM_GROUPS):
        acc = acc + t[g * SSM_GROUP_CH:(g + 1) * SSM_GROUP_CH]
    return acc


def _zoh(lr, li, ldt):
    dt = jnp.exp(ldt)
    mag = jnp.exp(lr * dt)
    ang = li * dt
    cs, sn = jnp.cos(ang), jnp.sin(ang)
    ab_re, ab_im = mag * cs, mag * sn
    nr, ni = ab_re - 1.0, ab_im
    den = lr * lr + li * li
    n_re = nr * lr + ni * li
    n_im = ni * lr - nr * li
    return dict(dt=dt, mag=mag, cs=cs, sn=sn, ab_re=ab_re, ab_im=ab_im, nr=nr, ni=ni, den=den, n_re=n_re, n_im=n_im,
                f_re=n_re / den, f_im=n_im / den)


def _ssm_params(lr, li, ldt, br, bi, cr, ci):
    def body(lr_ref, li_ref, ldt_ref, br_ref, bi_ref, cr_ref, ci_ref, ab_ref, w_ref, c_ref):
        z = _zoh(lr_ref[...], li_ref[...], ldt_ref[...])
        ab_ref[0:1, :] = z["ab_re"]
        ab_ref[1:2, :] = z["ab_im"]
        br, bi = br_ref[...], bi_ref[...]
        w_ref[:, 0:SSM_LANES] = _expand_groups(z["f_re"] * br - z["f_im"] * bi).astype(BF16)
        w_ref[:, SSM_LANES:] = _expand_groups(z["f_re"] * bi + z["f_im"] * br).astype(BF16)
        c_ref[:, 0:SSM_LANES] = _expand_groups(cr_ref[...]).astype(BF16)
        c_ref[:, SSM_LANES:] = _expand_groups(-ci_ref[...]).astype(BF16)

    return _pcall(body, name="ssm_params",
                  out_shape=(_sds((2, SSM_LANES), F32), _sds((SSM_WIDTH, 2 * SSM_LANES), BF16),
                             _sds((SSM_WIDTH, 2 * SSM_LANES), BF16)))(lr, li, ldt, br, bi, cr, ci)


def _ssm_params_bwd(lr, li, ldt, br, bi, dab, dw, dc):
    def body(lr_ref, li_ref, ldt_ref, br_ref, bi_ref, dab_ref, dw_ref, dc_ref,
             dlr_ref, dli_ref, dldt_ref, dbr_ref, dbi_ref, dcr_ref, dci_ref):
        lr, li = lr_ref[...], li_ref[...]
        z = _zoh(lr, li, ldt_ref[...])
        br, bi = br_ref[...], bi_ref[...]
        dbb_re = _collapse_groups(dw_ref[:, 0:SSM_LANES])
        dbb_im = _collapse_groups(dw_ref[:, SSM_LANES:])
        dcr_ref[...] = _collapse_groups(dc_ref[:, 0:SSM_LANES])
        dci_ref[...] = -_collapse_groups(dc_ref[:, SSM_LANES:])
        f_re, f_im = z["f_re"], z["f_im"]
        dbr_ref[...] = f_re * dbb_re + f_im * dbb_im
        dbi_ref[...] = f_re * dbb_im - f_im * dbb_re
        df_re = jnp.sum(dbb_re * br + dbb_im * bi, axis=0, keepdims=True)
        df_im = jnp.sum(dbb_im * br - dbb_re * bi, axis=0, keepdims=True)
        den = z["den"]
        dn_re, dn_im = df_re / den, df_im / den
        dden = -(df_re * z["n_re"] + df_im * z["n_im"]) / (den * den)
        dnr = dn_re * lr - dn_im * li
        dni = dn_re * li + dn_im * lr
        dlr = dn_re * z["nr"] + dn_im * z["ni"] + 2.0 * dden * lr
        dli = dn_re * z["ni"] - dn_im * z["nr"] + 2.0 * dden * li
        dab_re = dab_ref[0:1, :] + dnr
        dab_im = dab_ref[1:2, :] + dni
        mag, cs, sn, dt = z["mag"], z["cs"], z["sn"], z["dt"]
        dmag = dab_re * cs + dab_im * sn
        dang = mag * (dab_im * cs - dab_re * sn)
        dlr_ref[...] = dlr + dmag * mag * dt
        dli_ref[...] = dli + dang * dt
        ddt = dmag * mag * lr + dang * li
        per_lane = jnp.broadcast_to(ddt * dt, (8, SSM_LANES))
        lane = lax.broadcasted_iota(jnp.int32, (SSM_LANES, 128), 0)
        col = lax.broadcasted_iota(jnp.int32, (SSM_LANES, 128), 1)
        ind = jnp.where(lax.shift_right_logical(lane, 6) == col, 1.0, 0.0)
        dldt_ref[...] = jnp.dot(per_lane, ind, preferred_element_type=F32, precision=lax.Precision.HIGHEST)[0:1]

    vec = _sds((1, SSM_LANES), F32)
    mat = _sds((SSM_GROUP_CH, SSM_LANES), F32)
    return _pcall(body, name="ssm_params_bwd", out_shape=(vec, vec, _sds((1, 128), F32), mat, mat, mat, mat))(
        lr, li, ldt, br, bi, dab, dw, dc)


SCAN_CHUNK = 512


def _scan_consts(ar, ai, k_ref, reverse):
    row = lax.broadcasted_iota(jnp.int32, (8, SSM_LANES), 0)
    pw = [(ar, ai)]
    for _ in range(7):
        pr, pi = pw[-1]
        pw.append((pr * ar - pi * ai, pr * ai + pi * ar))
    for n, k in enumerate((1, 2, 4)):
        keep = (row < 8 - k) if reverse else (row >= k)
        k_ref[2 * n] = jnp.where(keep, jnp.broadcast_to(pw[k - 1][0], (8, SSM_LANES)), 0.0)
        k_ref[2 * n + 1] = jnp.where(keep, jnp.broadcast_to(pw[k - 1][1], (8, SSM_LANES)), 0.0)
    cr = jnp.zeros((8, SSM_LANES), F32)
    ci = jnp.zeros((8, SSM_LANES), F32)
    for r in range(8):
        e = (8 - r) if reverse else (r + 1)
        cr = jnp.where(row == r, jnp.broadcast_to(pw[e - 1][0], (8, SSM_LANES)), cr)
        ci = jnp.where(row == r, jnp.broadcast_to(pw[e - 1][1], (8, SSM_LANES)), ci)
    k_ref[6] = cr
    k_ref[7] = ci


def _scan_tile(xr, xi, k_ref, car, cai, reverse):
    for n, k in enumerate((1, 2, 4)):
        sh = (8 - k) if reverse else k
        sr = pltpu.roll(xr, sh, 0)
        si = pltpu.roll(xi, sh, 0)
        mr, mi = k_ref[2 * n], k_ref[2 * n + 1]
        xr, xi = xr + mr * sr - mi * si, xi + mr * si + mi * sr
    pr, pi = k_ref[6], k_ref[7]
    xr, xi = xr + pr * car - pi * cai, xi + pr * cai + pi * car
    return xr, xi


def _scan_fwd(bu3, abar):
    B, S, _ = bu3.shape
    ch = min(S, SCAN_CHUNK)
    blk = pl.BlockSpec((1, ch, 2 * SSM_LANES), lambda b, c: (b, c, 0))

    def body(ab_ref, bu_ref, x_ref, k_ref, carry_ref):
        _scan_consts(ab_ref[0:1, :], ab_ref[1:2, :], k_ref, False)

        @pl.when(pl.program_id(1) == 0)
        def _():
            carry_ref[...] = jnp.zeros_like(carry_ref)

        def step(i, carry):
            base = pl.multiple_of(i * 8, 8)
            xr = bu_ref[0, pl.ds(base, 8), 0:SSM_LANES]
            xi = bu_ref[0, pl.ds(base, 8), SSM_LANES:]
            xr, xi = _scan_tile(xr, xi, k_ref, carry[0], carry[1], False)
            x_ref[0, pl.ds(base, 8), 0:SSM_LANES] = xr
            x_ref[0, pl.ds(base, 8), SSM_LANES:] = xi
            return (jnp.broadcast_to(xr[7:8], (8, SSM_LANES)), jnp.broadcast_to(xi[7:8], (8, SSM_LANES)))

        cr, ci = lax.fori_loop(0, ch // 8, step, (carry_ref[0], carry_ref[1]))
        carry_ref[0] = cr
        carry_ref[1] = ci

    return _pcall(body, name="scan_fwd", out_shape=_sds(bu3.shape, F32), grid=(B, S // ch),
                  in_specs=[pl.BlockSpec((2, SSM_LANES), lambda b, c: (0, 0)), blk], out_specs=blk,
                  scratch_shapes=[pltpu.VMEM((8, 8, SSM_LANES), F32), pltpu.VMEM((2, 8, SSM_LANES), F32)],
                  dims=("arbitrary", "arbitrary"))(abar, bu3)


def _scan_bwd(dx3, xs3, abar):
    B, S, _ = dx3.shape
    ch = min(S, SCAN_CHUNK)
    nc = S // ch
    blk = pl.BlockSpec((1, ch, 2 * SSM_LANES), lambda b, c: (b, nc - 1 - c, 0))

    def body(ab_ref, dx_ref, xs_ref, g_ref, da_ref, k_ref, carry_ref, acc_ref):
        b, c = pl.program_id(0), pl.program_id(1)
        _scan_consts(ab_ref[0:1, :], -ab_ref[1:2, :], k_ref, True)
        row = lax.broadcasted_iota(jnp.int32, (8, SSM_LANES), 0)

        @pl.when(c == 0)
        def _():
            carry_ref[...] = jnp.zeros_like(carry_ref)

        @pl.when((c == 0) & (b == 0))
        def _():
            acc_ref[...] = jnp.zeros_like(acc_ref)

        def step(i, carry):
            car, cai, ar_acc, ai_acc = carry
            base = pl.multiple_of((ch // 8 - 1 - i) * 8, 8)
            gr = dx_ref[0, pl.ds(base, 8), 0:SSM_LANES]
            gi = dx_ref[0, pl.ds(base, 8), SSM_LANES:]
            gr, gi = _scan_tile(gr, gi, k_ref, car, cai, True)
            g_ref[0, pl.ds(base, 8), 0:SSM_LANES] = gr
            g_ref[0, pl.ds(base, 8), SSM_LANES:] = gi
            nr = jnp.where(row == 7, car, pltpu.roll(gr, 7, 0))
            ni = jnp.where(row == 7, cai, pltpu.roll(gi, 7, 0))
            xr = xs_ref[0, pl.ds(base, 8), 0:SSM_LANES]
            xi = xs_ref[0, pl.ds(base, 8), SSM_LANES:]
            ar_acc = ar_acc + nr * xr + ni * xi
            ai_acc = ai_acc + ni * xr - nr * xi
            return (jnp.broadcast_to(gr[0:1], (8, SSM_LANES)), jnp.broadcast_to(gi[0:1], (8, SSM_LANES)), ar_acc, ai_acc)

        cr, ci, ar_acc, ai_acc = lax.fori_loop(0, ch // 8, step, (carry_ref[0], carry_ref[1], acc_ref[0], acc_ref[1]))
        carry_ref[0] = cr
        carry_ref[1] = ci
        acc_ref[0] = ar_acc
        acc_ref[1] = ai_acc
        da_ref[0:1, :] = jnp.sum(ar_acc, axis=0, keepdims=True)
        da_ref[1:2, :] = jnp.sum(ai_acc, axis=0, keepdims=True)

    return _pcall(body, name="scan_bwd", out_shape=(_sds(dx3.shape, F32), _sds((2, SSM_LANES), F32)), grid=(B, nc),
                  in_specs=[pl.BlockSpec((2, SSM_LANES), lambda b, c: (0, 0)), blk, blk],
                  out_specs=(blk, pl.BlockSpec((2, SSM_LANES), lambda b, c: (0, 0))),
                  scratch_shapes=[pltpu.VMEM((8, 8, SSM_LANES), F32), pltpu.VMEM((2, 8, SSM_LANES), F32),
                                  pltpu.VMEM((2, 8, SSM_LANES), F32)],
                  dims=("arbitrary", "arbitrary"))(abar, dx3, xs3)


GELU_K = math.sqrt(2.0 / math.pi)
GELU_C = 0.044715


def _gelu_parts(y):
    t = jnp.tanh(GELU_K * (y + GELU_C * y * y * y))
    return 0.5 * y * (1.0 + t), t


def _ssm_post(yc, us, dsk, wglu, bglu):
    T, N = yc.shape
    tm = min(T, 1024)
    row = pl.BlockSpec((tm, N), lambda i: (i, 0))
    vec = pl.BlockSpec((1, N), lambda i: (0, 0))
    mat = pl.BlockSpec((N, N), lambda i: (0, 0))

    def body(yc_ref, us_ref, d_ref, w_ref, b_ref, y_ref, s_ref):
        y = yc_ref[...] + d_ref[...] * us_ref[...]
        y_ref[...] = y
        z, _ = _gelu_parts(y)
        gl = jnp.dot(z.astype(BF16), w_ref[...], preferred_element_type=F32) + b_ref[...]
        s_ref[...] = (z * _sig(gl)).astype(BF16)

    return _pcall(body, name="ssm_post", out_shape=(_sds((T, N), F32), _sds((T, N), BF16)), grid=(T // tm,),
                  in_specs=[row, row, vec, mat, vec], out_specs=(row, row), dims=("parallel",))(yc, us, dsk, wglu, bglu)


def _ssm_post_bwd(y5, us, ds, dsk, wglu, bglu):
    T, N = y5.shape
    tm = min(T, 1024)
    row = pl.BlockSpec((tm, N), lambda i: (i, 0))
    vec = pl.BlockSpec((1, N), lambda i: (0, 0))
    mat = pl.BlockSpec((N, N), lambda i: (0, 0))

    def body(y_ref, us_ref, ds_ref, d_ref, w_ref, b_ref, dy_ref, dd_ref, db_ref, dw_ref):
        @pl.when(pl.program_id(0) == 0)
        def _():
            dd_ref[...] = jnp.zeros_like(dd_ref)
            db_ref[...] = jnp.zeros_like(db_ref)
            dw_ref[...] = jnp.zeros_like(dw_ref)

        y = y_ref[...]
        z, t = _gelu_parts(y)
        zb = z.astype(BF16)
        gl = jnp.dot(zb, w_ref[...], preferred_element_type=F32) + b_ref[...]
        sg = _sig(gl)
        ds = ds_ref[...]
        dgl = ds * z * sg * (1.0 - sg)
        dglb = dgl.astype(BF16)
        dz = ds * sg + lax.dot_general(dglb, w_ref[...], (((1,), (1,)), ((), ())), preferred_element_type=F32)
        dgelu = 0.5 * (1.0 + t) + 0.5 * y * (1.0 - t * t) * GELU_K * (1.0 + 3.0 * GELU_C * y * y)
        dy = dz * dgelu
        dy_ref[...] = dy
        dd_ref[...] += jnp.sum(dy * us_ref[...], axis=0, keepdims=True)
        db_ref[...] += jnp.sum(dgl, axis=0, keepdims=True)
        dw_ref[...] += lax.dot_general(zb, dglb, (((0,), (0,)), ((), ())), preferred_element_type=F32)

    return _pcall(body, name="ssm_post_bwd",
                  out_shape=(_sds((T, N), F32), _sds((1, N), F32), _sds((1, N), F32), _sds((N, N), F32)),
                  grid=(T // tm,), in_specs=[row, row, row, vec, mat, vec], out_specs=(row, vec, vec, mat),
                  dims=("arbitrary",))(y5, us, ds, dsk, wglu, bglu)


def _add_scaled_cast(a, b, s):
    T, N = a.shape
    tm = min(T, 1024)
    row = pl.BlockSpec((tm, N), lambda i: (i, 0))

    def body(a_ref, b_ref, s_ref, o_ref):
        o_ref[...] = (a_ref[...] + s_ref[...] * b_ref[...]).astype(BF16)

    return _pcall(body, name="add_scaled_cast", out_shape=_sds((T, N), BF16), grid=(T // tm,),
                  in_specs=[row, row, pl.BlockSpec((1, N), lambda i: (0, 0))], out_specs=row, dims=("parallel",))(a, b, s)


GATE_TILE = 256
GATE_ATT_BLOCK0 = (3 * ATT_WIDTH + SSM_WIDTH) // GATE_TILE
GATE_SSM_BLOCK0 = (3 * ATT_WIDTH + SSM_WIDTH + D_MODEL) // GATE_TILE


def _merge(proj, y_att, y_ssm, b_gate):
    T = proj.shape[0]
    tm = min(T, 1024)
    nj = D_MODEL // GATE_TILE
    ga = pl.BlockSpec((tm, GATE_TILE), lambda i, j: (i, GATE_ATT_BLOCK0 + j))
    gs = pl.BlockSpec((tm, GATE_TILE), lambda i, j: (i, GATE_SSM_BLOCK0 + j))
    yy = pl.BlockSpec((tm, GATE_TILE), lambda i, j: (i, j))
    ba = pl.BlockSpec((1, GATE_TILE), lambda i, j: (0, j))
    bs = pl.BlockSpec((1, GATE_TILE), lambda i, j: (0, nj + j))

    def body(ga_ref, gs_ref, ya_ref, ys_ref, ba_ref, bs_ref, o_ref):
        o_ref[...] = (_sig(ga_ref[...] + ba_ref[...]) * ya_ref[...]
                      + _sig(gs_ref[...] + bs_ref[...]) * ys_ref[...]).astype(BF16)

    return _pcall(body, name="merge", out_shape=_sds((T, D_MODEL), BF16), grid=(T // tm, nj),
                  in_specs=[ga, gs, yy, yy, ba, bs], out_specs=yy, dims=("parallel", "parallel"))(
        proj, proj, y_att, y_ssm, b_gate, b_gate)


def _merge_bwd(proj, y_att, y_ssm, b_gate, dmerged):
    T = proj.shape[0]
    tm = min(T, 1024)
    nj = D_MODEL // GATE_TILE
    ga = pl.BlockSpec((tm, GATE_TILE), lambda j, i: (i, GATE_ATT_BLOCK0 + j))
    gs = pl.BlockSpec((tm, GATE_TILE), lambda j, i: (i, GATE_SSM_BLOCK0 + j))
    yy = pl.BlockSpec((tm, GATE_TILE), lambda j, i: (i, j))
    ba = pl.BlockSpec((1, GATE_TILE), lambda j, i: (0, j))
    bs = pl.BlockSpec((1, GATE_TILE), lambda j, i: (0, nj + j))

    def body(ga_ref, gs_ref, ya_ref, ys_ref, ba_ref, bs_ref, dm_ref, dya_ref, dys_ref, dga_ref, dgs_ref, dba_ref, dbs_ref):
        @pl.when(pl.program_id(1) == 0)
        def _():
            dba_ref[...] = jnp.zeros_like(dba_ref)
            dbs_ref[...] = jnp.zeros_like(dbs_ref)

        dm = dm_ref[...].astype(F32)
        sa = _sig(ga_ref[...] + ba_ref[...])
        ss = _sig(gs_ref[...] + bs_ref[...])
        dya_ref[...] = (dm * sa).astype(BF16)
        dys_ref[...] = (dm * ss).astype(BF16)
        dga = dm * ya_ref[...] * sa * (1.0 - sa)
        dgs = dm * ys_ref[...] * ss * (1.0 - ss)
        dga_ref[...] = dga.astype(BF16)
        dgs_ref[...] = dgs.astype(BF16)
        dba_ref[...] += jnp.sum(dga, axis=0, keepdims=True)
        dbs_ref[...] += jnp.sum(dgs, axis=0, keepdims=True)

    big = _sds((T, D_MODEL), BF16)
    vec = _sds((1, D_MODEL), F32)
    return _pcall(body, name="merge_bwd", out_shape=(big, big, big, big, vec, vec), grid=(nj, T // tm),
                  in_specs=[ga, gs, yy, yy, ba, bs, yy], out_specs=(yy, yy, yy, yy, ba, ba),
                  dims=("arbitrary", "arbitrary"))(proj, proj, y_att, y_ssm, b_gate, b_gate, dmerged)


CONV_TILE = 256


def _conv_pre(a, w_ref, b_ref, row):
    conv = b_ref[...] + w_ref[0:1, :] * a
    shifted = []
    for j in (1, 2):
        sh = jnp.where(row >= j, pltpu.roll(a, j, 0), 0.0)
        shifted.append(sh)
        conv = conv + w_ref[j:j + 1, :] * sh
    return conv, shifted


def _conv_act(up3, w_conv, b_conv):
    B, S, _ = up3.shape
    nj = D_FF // CONV_TILE
    a_spec = pl.BlockSpec((1, S, CONV_TILE), lambda b, j: (b, 0, j))
    v_spec = pl.BlockSpec((1, S, CONV_TILE), lambda b, j: (b, 0, nj + j))
    w_spec = pl.BlockSpec((3, CONV_TILE), lambda b, j: (0, j))
    b_spec = pl.BlockSpec((1, CONV_TILE), lambda b, j: (0, j))

    def body(a_ref, v_ref, w_ref, b_ref, o_ref):
        a = a_ref[0].astype(F32)
        row = lax.broadcasted_iota(jnp.int32, a.shape, 0)
        conv, _ = _conv_pre(a, w_ref, b_ref, row)
        o_ref[0] = (conv * _sig(conv) * v_ref[0]).astype(BF16)

    return _pcall(body, name="conv_act", out_shape=_sds((B, S, D_FF), BF16), grid=(B, nj),
                  in_specs=[a_spec, v_spec, w_spec, b_spec], out_specs=a_spec, dims=("parallel", "parallel"))(
        up3, up3, w_conv, b_conv)


def _conv_bwd(up3, dact3, w_conv, b_conv):
    B, S, _ = up3.shape
    nj = D_FF // CONV_TILE
    a_spec = pl.BlockSpec((1, S, CONV_TILE), lambda j, b: (b, 0, j))
    v_spec = pl.BlockSpec((1, S, CONV_TILE), lambda j, b: (b, 0, nj + j))
    w_spec = pl.BlockSpec((3, CONV_TILE), lambda j, b: (0, j))
    b_spec = pl.BlockSpec((1, CONV_TILE), lambda j, b: (0, j))

    def body(a_ref, v_ref, d_ref, w_ref, b_ref, da_ref, dv_ref, dw_ref, db_ref):
        @pl.when(pl.program_id(1) == 0)
        def _():
            dw_ref[...] = jnp.zeros_like(dw_ref)
            db_ref[...] = jnp.zeros_like(db_ref)

        a = a_ref[0].astype(F32)
        d = d_ref[0].astype(F32)
        row = lax.broadcasted_iota(jnp.int32, a.shape, 0)
        conv, shifted = _conv_pre(a, w_ref, b_ref, row)
        sg = _sig(conv)
        dv_ref[0] = (d * conv * sg).astype(BF16)
        dconv = d * v_ref[0] * (sg * (1.0 + conv * (1.0 - sg)))
        da = w_ref[0:1, :] * dconv
        for j in (1, 2):
            da = da + w_ref[j:j + 1, :] * jnp.where(row < S - j, pltpu.roll(dconv, S - j, 0), 0.0)
        da_ref[0] = da.astype(BF16)
        db_ref[...] += jnp.sum(dconv, axis=0, keepdims=True)
        dw_ref[0:1, :] += jnp.sum(dconv * a, axis=0, keepdims=True)
        dw_ref[1:2, :] += jnp.sum(dconv * shifted[0], axis=0, keepdims=True)
        dw_ref[2:3, :] += jnp.sum(dconv * shifted[1], axis=0, keepdims=True)

    big = _sds((B, S, D_FF), BF16)
    return _pcall(body, name="conv_bwd", out_shape=(big, big, _sds((3, D_FF), F32), _sds((1, D_FF), F32)),
                  grid=(nj, B), in_specs=[a_spec, v_spec, a_spec, w_spec, b_spec],
                  out_specs=(a_spec, a_spec, w_spec, b_spec), dims=("arbitrary", "arbitrary"))(
        up3, up3, dact3, w_conv, b_conv)


def _rows_tile(r, cap=640):
    for t in range(min(r, cap) - min(r, cap) % 8, 7, -8):
        if r % t == 0:
            return t
    return r


def _add2(a, b, out_dtype):
    R, N = a.shape
    tr = _rows_tile(R)
    spec = pl.BlockSpec((tr, N), lambda i: (i, 0))

    def body(a_ref, b_ref, o_ref):
        o_ref[...] = (a_ref[...] + b_ref[...]).astype(out_dtype)

    return _pcall(body, name="add2", out_shape=_sds((R, N), out_dtype), grid=(R // tr,), in_specs=[spec, spec],
                  out_specs=spec, dims=("parallel",))(a, b)


def _sum_slots(q, name):
    n, R, N = q.shape
    tr = _rows_tile(R)

    def body(q_ref, o_ref):
        acc = q_ref[0].astype(F32)
        for s in range(1, n):
            acc = acc + q_ref[s].astype(F32)
        o_ref[...] = acc

    return _pcall(body, name=name, out_shape=_sds((R, N), F32), grid=(R // tr,),
                  in_specs=[pl.BlockSpec((n, tr, N), lambda i: (0, i, 0))], out_specs=pl.BlockSpec((tr, N), lambda i: (i, 0)),
                  dims=("parallel",))(q)


def _adamw(w, g, m, v, name):
    R, N = w.shape
    tr = _rows_tile(R) if R * N * 4 > (1 << 20) else R
    tr = min(tr, 256) if R % 256 == 0 and R > 256 else tr
    spec = pl.BlockSpec((tr, N), lambda i: (i, 0))
    bc1 = 1.0 - ADAM_B1 ** ADAM_STEP
    bc2 = 1.0 - ADAM_B2 ** ADAM_STEP

    def body(w_ref, g_ref, m_ref, v_ref, d_ref, nm_ref, nv_ref):
        g = g_ref[...]
        m = ADAM_B1 * m_ref[...] + (1.0 - ADAM_B1) * g
        v = ADAM_B2 * v_ref[...] + (1.0 - ADAM_B2) * (g * g)
        nm_ref[...] = m
        nv_ref[...] = v
        d_ref[...] = -ADAM_LR * ((m / bc1) / (jnp.sqrt(v / bc2) + ADAM_EPS) + ADAM_WD * w_ref[...])

    shp = _sds((R, N), F32)
    return _pcall(body, name=name, out_shape=(shp, shp, shp), grid=(R // tr,), in_specs=[spec] * 4,
                  out_specs=(spec, spec, spec), dims=("parallel",))(w, g, m, v)


_GROUP_MASKS = {
    "all": [(dx, dy, dc) for dx in (0, 1) for dy in (0, 1) for dc in (0, 1) if (dx, dy, dc) != (0, 0, 0)],
    "xy": [(1, 0, 0), (0, 1, 0), (1, 1, 0)],
    "c": [(0, 0, 1)],
}
_GROUP_SLOTS = {"all": 8, "xy": 4, "c": 2}


def _group_slot(group, x, y, c):
    return {"all": 4 * x + 2 * y + c, "xy": 2 * x + y, "c": c}[group]


def _flip(v, d):
    return 1 - v if d else v


def _exchange(arr, group, mode, name):
    return _exchange_list([arr], group, mode, name)[0]


def _exchange_list(arrs, group, mode, name):
    masks = _GROUP_MASKS[group]
    n = len(masks)
    na = len(arrs)
    out_shapes, halves, bounce = [], [], []
    for arr in arrs:
        if mode == "gather":
            out_shapes.append((_GROUP_SLOTS[group],) + arr.shape)
            bounce.append(pltpu.VMEM(arr.shape, arr.dtype))
        elif mode == "scatter":
            assert arr.shape[0] == _GROUP_SLOTS[group]
            out_shapes.append(arr.shape)
            bounce.append(pltpu.VMEM(arr.shape[1:], arr.dtype))
        elif mode == "swap":
            assert group == "c"
            out_shapes.append(arr.shape)
        else:
            assert group == "c"
            halves.append(arr.shape[1] // 2)
            out_shapes.append((arr.shape[0], arr.shape[1] // 2, arr.shape[2]))
    has_local = mode in ("gather", "scatter")

    def body(*refs):
        x_refs, o_refs = refs[:na], refs[na:2 * na]
        send_sems, recv_sems = refs[2 * na], refs[2 * na + 1]
        x, y, c = lax.axis_index("x"), lax.axis_index("y"), lax.axis_index("c")
        me = _group_slot(group, x, y, c)
        if has_local:
            local_sems = refs[2 * na + 2]
            bufs = refs[2 * na + 3:]
            loads = []
            for i in range(na):
                src = x_refs[i] if mode == "gather" else x_refs[i].at[me]
                loads.append(pltpu.make_async_copy(src, bufs[i], local_sems.at[2 * i]))
                loads[-1].start()
        copies = []
        for i in range(na):
            x_ref, o_ref = x_refs[i], o_refs[i]
            for k, (dx, dy, dc) in enumerate(masks):
                px, py, pc = _flip(x, dx), _flip(y, dy), _flip(c, dc)
                if mode == "gather":
                    src, dst = x_ref, o_ref.at[me]
                elif mode == "scatter":
                    src, dst = x_ref.at[_group_slot(group, px, py, pc)], o_ref.at[me]
                elif mode == "swap":
                    src, dst = x_ref, o_ref
                else:
                    src, dst = x_ref.at[:, pl.ds(pl.multiple_of(pc * halves[i], 8), halves[i]), :], o_ref
                cp = pltpu.make_async_remote_copy(src_ref=src, dst_ref=dst, send_sem=send_sems.at[i * n + k],
                                                  recv_sem=recv_sems.at[i * n + k], device_id=(px, py, pc),
                                                  device_id_type=pl.DeviceIdType.MESH)
                cp.start()
                copies.append(cp)
        if has_local:
            stores = []
            for i in range(na):
                loads[i].wait()
                stores.append(pltpu.make_async_copy(bufs[i], o_refs[i].at[me], local_sems.at[2 * i + 1]))
                stores[-1].start()
        for cp in copies:
            cp.wait()
        if has_local:
            for st in stores:
                st.wait()

    anyspec = pl.BlockSpec(memory_space=pl.ANY)
    scratch = [pltpu.SemaphoreType.DMA((n * na,)), pltpu.SemaphoreType.DMA((n * na,))]
    if has_local:
        scratch += [pltpu.SemaphoreType.DMA((2 * na,))] + bounce
    outs = pl.pallas_call(body, name=name, out_shape=tuple(_sds(s, a.dtype) for s, a in zip(out_shapes, arrs)),
                          in_specs=[anyspec] * na, out_specs=tuple([anyspec] * na), scratch_shapes=scratch,
                          compiler_params=pltpu.CompilerParams(vmem_limit_bytes=V7X_VMEM_LIMIT_BYTES))(*arrs)
    return list(outs)


def _gather_weights(shards, name):
    na = len(shards)
    masks = _GROUP_MASKS["xy"]
    n = len(masks)

    def body(*refs):
        x_refs, o_refs = refs[:na], refs[na:2 * na]
        send_sems, recv_sems, local_sems = refs[2 * na:2 * na + 3]
        bufs = refs[2 * na + 3:]
        x, y, c = lax.axis_index("x"), lax.axis_index("y"), lax.axis_index("c")
        me = 2 * x + y
        sibling = (x, y, 1 - c)
        loads = []
        for i in range(na):
            loads.append(pltpu.make_async_copy(x_refs[i], bufs[i], local_sems.at[2 * i]))
            loads[-1].start()

        def half_of(i, slot, cc):
            h = shards[i].shape[0] // 2
            return o_refs[i].at[slot, pl.ds(pl.multiple_of(cc * h, 8), h), :]

        def src_half(i, cc):
            h = shards[i].shape[0] // 2
            return x_refs[i].at[pl.ds(pl.multiple_of(cc * h, 8), h), :]

        sends = []
        for i in range(na):
            for k, (dx, dy, _) in enumerate(masks):
                cp = pltpu.make_async_remote_copy(src_ref=src_half(i, c), dst_ref=half_of(i, me, c),
                                                  send_sem=send_sems.at[i * 2 * n + k], recv_sem=recv_sems.at[i * 2 * n + k],
                                                  device_id=(_flip(x, dx), _flip(y, dy), c),
                                                  device_id_type=pl.DeviceIdType.MESH)
                cp.start()
                sends.append(cp)
        stores = []
        for i in range(na):
            loads[i].wait()
            stores.append(pltpu.make_async_copy(bufs[i], o_refs[i].at[me], local_sems.at[2 * i + 1]))
            stores[-1].start()
        for i in range(na):
            for k, (dx, dy, _) in enumerate(masks):
                slot = 2 * _flip(x, dx) + _flip(y, dy)
                landed = pltpu.make_async_remote_copy(src_ref=src_half(i, c), dst_ref=half_of(i, slot, c),
                                                      send_sem=send_sems.at[i * 2 * n + k],
                                                      recv_sem=recv_sems.at[i * 2 * n + k], device_id=sibling,
                                                      device_id_type=pl.DeviceIdType.MESH)
                landed.wait_recv()
                fwd = pltpu.make_async_remote_copy(src_ref=half_of(i, slot, c), dst_ref=half_of(i, slot, c),
                                                   send_sem=send_sems.at[i * 2 * n + n + k],
                                                   recv_sem=recv_sems.at[i * 2 * n + n + k], device_id=sibling,
                                                   device_id_type=pl.DeviceIdType.MESH)
                fwd.start()
                sends.append(fwd)
        for i in range(na):
            for k, (dx, dy, _) in enumerate(masks):
                slot = 2 * _flip(x, dx) + _flip(y, dy)
                pltpu.make_async_remote_copy(src_ref=half_of(i, slot, 1 - c), dst_ref=half_of(i, slot, 1 - c),
                                             send_sem=send_sems.at[i * 2 * n + n + k],
                                             recv_sem=recv_sems.at[i * 2 * n + n + k], device_id=sibling,
                                             device_id_type=pl.DeviceIdType.MESH).wait_recv()
        for cp in sends:
            cp.wait_send()
        for st in stores:
            st.wait()

    anyspec = pl.BlockSpec(memory_space=pl.ANY)
    scratch = [pltpu.SemaphoreType.DMA((2 * n * na,)), pltpu.SemaphoreType.DMA((2 * n * na,)),
               pltpu.SemaphoreType.DMA((2 * na,))] + [pltpu.VMEM(s.shape, s.dtype) for s in shards]
    outs = pl.pallas_call(body, name=name, out_shape=tuple(_sds((N_XY,) + s.shape, s.dtype) for s in shards),
                          in_specs=[anyspec] * na, out_specs=tuple([anyspec] * na), scratch_shapes=scratch,
                          compiler_params=pltpu.CompilerParams(vmem_limit_bytes=V7X_VMEM_LIMIT_BYTES))(*shards)
    return list(outs)


def _pair_add(g, theirs, core, name):
    n4, h2, w = g.shape
    h = h2 // 2
    tr = _rows_tile(h)
    nb = h // tr

    def body(c_ref, g_ref, t_ref, o_ref):
        o_ref[...] = (g_ref[...] + t_ref[...]).astype(BF16)

    grid_spec = pltpu.PrefetchScalarGridSpec(
        num_scalar_prefetch=1, grid=(n4, nb),
        in_specs=[pl.BlockSpec((None, tr, w), lambda j, i, c_ref: (j, c_ref[0] * nb + i, 0)),
                  pl.BlockSpec((None, tr, w), lambda j, i, c_ref: (j, i, 0))],
        out_specs=pl.BlockSpec((None, tr, w), lambda j, i, c_ref: (j, i, 0)))
    return pl.pallas_call(body, name=name, out_shape=_sds((n4, h, w), BF16), grid_spec=grid_spec,
                          compiler_params=pltpu.CompilerParams(vmem_limit_bytes=V7X_VMEM_LIMIT_BYTES,
                                                               dimension_semantics=("parallel", "parallel")))(core, g, theirs)


BIG = (("w_proj_att", (ATT_WIDTH, D_MODEL), 1), ("w_proj_ssm", (SSM_WIDTH, D_MODEL), 1),
       ("w_glu", (SSM_WIDTH, SSM_WIDTH), 0))
DIRECT = (("w_in", True), ("w_up", True), ("w_down", False), ("w_out", False))
N_XY = 4


def _big_rows(shape):
    return shape[0] * shape[1] // N_XY // LANES


FLAT_ROWS = sum(_big_rows(s) for _, s, _ in BIG)


def _shard_shape(shape, axis):
    return (shape[0] // N_XY, shape[1]) if axis == 0 else (shape[0], shape[1] // N_XY)


def _flatten_shards(shards):
    return jnp.concatenate([shards[n].reshape(_big_rows(s), LANES) for n, s, _ in BIG], axis=0)


def _unflatten_shard(flat):
    out, r = {}, 0
    for n, s, ax in BIG:
        k = _big_rows(s)
        out[n] = flat[r:r + k].reshape(_shard_shape(s, ax))
        r += k
    return out


def _unflatten_full(flat4):
    out, r = {}, 0
    for n, s, ax in BIG:
        k = _big_rows(s)
        sh = _shard_shape(s, ax)
        t = flat4[:, r:r + k].reshape((N_XY,) + sh)
        out[n] = t.reshape(s) if ax == 0 else t.transpose(1, 0, 2).reshape(s)
        r += k
    return out


def _flatten_full(full):
    parts = []
    for n, s, ax in BIG:
        sh = _shard_shape(s, ax)
        t = full[n]
        t = t.reshape((N_XY,) + sh) if ax == 0 else t.reshape(s[0], N_XY, sh[1]).transpose(1, 0, 2)
        parts.append(t.reshape(N_XY, _big_rows(s), LANES))
    return jnp.concatenate(parts, axis=1)


def _pack_rows(arrs):
    rows, counts = [], []
    for a in arrs:
        f = a.reshape(-1)
        k = -(-f.shape[0] // LANES)
        rows.append(jnp.pad(f, (0, k * LANES - f.shape[0])).reshape(k, LANES))
        counts.append(k)
    return jnp.concatenate(rows, axis=0), counts


def _unpack_rows(buf, shapes):
    out, r = [], 0
    for s in shapes:
        size = int(np.prod(s))
        k = -(-size // LANES)
        out.append(buf[r:r + k].reshape(-1)[:size].reshape(s))
        r += k
    return out


def _lanes_from_groups(a):
    return a.transpose(2, 0, 1).reshape(SSM_GROUP_CH, SSM_LANES)


def _groups_from_lanes(a):
    return a.reshape(SSM_GROUP_CH, SSM_GROUPS, SSM_STATE).transpose(1, 2, 0)


def _local_step(x3, mod, tgt3, W, P):
    B, S, _ = x3.shape
    T = B * S
    seq_blocks = S // ATT_BLOCK
    sh1, sc1, gt1, sh2, sc2, gt2 = [m.reshape(B, 1, D_MODEL) for m in jnp.split(mod, 6, axis=-1)]
    g_mix, g_ffn, g_final = P["g_mix"].reshape(1, D_MODEL), P["g_ffn"].reshape(1, D_MODEL), P["g_final"].reshape(1, D_MODEL)
    b_gate = P["b_gate"].reshape(1, 2 * D_MODEL)
    d_skip, b_glu = P["d_skip"].reshape(1, SSM_WIDTH), P["b_glu"].reshape(1, SSM_WIDTH)
    w_conv, b_conv = P["w_conv"], P["b_conv"].reshape(1, D_FF)

    u1 = _norm_mod(x3, g_mix, sc1, sh1).reshape(T, D_MODEL)
    proj = _mm(u1, W["w_in_t"], tb=True, name="mm_proj", out_dtype=BF16)
    proj3 = proj.reshape(B, S, IN_WIDTH)
    us = proj[:, 3 * ATT_WIDTH:3 * ATT_WIDTH + SSM_WIDTH]
    o_att3, lse4 = _attention_fwd(proj3, seq_blocks)
    o_att = o_att3.reshape(T, ATT_WIDTH)
    y_att = _mm(o_att, W["w_proj_att"], name="mm_proj_att", out_dtype=BF16)

    lr = P["a_re"].reshape(1, SSM_LANES)
    li = P["a_im"].reshape(1, SSM_LANES)
    ldt = jnp.repeat(P["log_dt"], SSM_STATE).reshape(1, SSM_LANES)
    br, bi = _lanes_from_groups(P["b_re"]), _lanes_from_groups(P["b_im"])
    cr = P["c_re"].transpose(1, 0, 2).reshape(SSM_GROUP_CH, SSM_LANES)
    ci = P["c_im"].transpose(1, 0, 2).reshape(SSM_GROUP_CH, SSM_LANES)
    abar, w_bu, w_c = _ssm_params(lr, li, ldt, br, bi, cr, ci)
    bu = _mm(us, w_bu, name="mm_bu")
    xs = _scan_fwd(bu.reshape(B, S, 2 * SSM_LANES), abar).reshape(T, 2 * SSM_LANES)
    y_core = _mm(xs, w_c, tb=True, name="mm_ssm_out")
    y5, s_out = _ssm_post(y_core, us, d_skip, W["w_glu"], b_glu)
    y_ssm = _mm(s_out, W["w_proj_ssm"], name="mm_proj_ssm", out_dtype=BF16)

    merged = _merge(proj, y_att, y_ssm, b_gate)
    mix = _mm(merged, W["w_out"], name="mm_out", out_dtype=BF16)
    mix3 = mix.reshape(B, S, D_MODEL)

    h1, u2 = _resid_norm_mod(x3, mix3, gt1, g_ffn, sc2, sh2)
    u2 = u2.reshape(T, D_MODEL)
    up3 = _mm(u2, W["w_up_t"], tb=True, name="mm_up", out_dtype=BF16).reshape(B, S, 2 * D_FF)
    act = _conv_act(up3, w_conv, b_conv).reshape(T, D_FF)
    ffn3 = _mm(act, W["w_down"], name="mm_down", out_dtype=BF16).reshape(B, S, D_MODEL)
    dh2, dffn, dgt2, dg_final, loss = _final_loss(h1, ffn3, tgt3, gt2, g_final)

    dffn = dffn.reshape(T, D_MODEL)
    gw = {}
    gw["w_down"] = _mm(act, dffn, ta=True, name="mm_dw_down")
    dact3 = _mm(dffn, W["w_down"], tb=True, name="mm_dact", out_dtype=BF16).reshape(B, S, D_FF)
    da3, dval3, dw_conv, db_conv = _conv_bwd(up3, dact3, w_conv, b_conv)
    dup = jnp.concatenate([da3.reshape(T, D_FF), dval3.reshape(T, D_FF)], axis=1)
    gw["w_up_t"] = _mm(dup, u2, ta=True, name="mm_dw_up")
    du2 = _mm(dup, W["w_up_t"], name="mm_du2", out_dtype=BF16).reshape(B, S, D_MODEL)
    dh1, dsh2, dsc2, dg_ffn, dgt1, dmix = _norm_bwd(h1, du2, dh2, g_ffn, sc2, "norm_bwd2", mix3=mix3, gt=gt1)

    dmix = dmix.reshape(T, D_MODEL)
    gw["w_out"] = _mm(merged, dmix, ta=True, name="mm_dw_out")
    dmerged = _mm(dmix, W["w_out"], tb=True, name="mm_dmerged", out_dtype=BF16)
    dy_att, dy_ssm, dga, dgs, db_att, db_ssm = _merge_bwd(proj, y_att, y_ssm, b_gate, dmerged)

    gw["w_proj_ssm"] = _mm(s_out, dy_ssm, ta=True, name="mm_dw_proj_ssm")
    ds_out = _mm(dy_ssm, W["w_proj_ssm"], tb=True, name="mm_ds_out")
    dy5, dd_skip, db_glu, dw_glu = _ssm_post_bwd(y5, us, ds_out, d_skip, W["w_glu"], b_glu)
    gw["w_glu"] = dw_glu
    dxs = _mm(dy5, w_c, name="mm_dxs")
    dwc = _mm(dy5, xs, ta=True, name="mm_dwc")
    g3, dab = _scan_bwd(dxs.reshape(B, S, 2 * SSM_LANES), xs.reshape(B, S, 2 * SSM_LANES), abar)
    gs2 = g3.reshape(T, 2 * SSM_LANES)
    dwbu = _mm(us, gs2, ta=True, name="mm_dwbu")
    dus_core = _mm(gs2, w_bu, tb=True, name="mm_dus")
    dus = _add_scaled_cast(dus_core, dy5, d_skip)
    dlr, dli, dldt, dbr, dbi, dcr, dci = _ssm_params_bwd(lr, li, ldt, br, bi, dab, dwbu, dwc)

    gw["w_proj_att"] = _mm(o_att, dy_att, ta=True, name="mm_dw_proj_att")
    do_att = _mm(dy_att, W["w_proj_att"], tb=True, out_dtype=BF16, name="mm_do_att")
    dq3, dk3, dv3 = _attention_bwd(proj3, do_att.reshape(B, S, ATT_WIDTH), o_att3, lse4, seq_blocks)
    dproj = jnp.concatenate([t.reshape(T, ATT_WIDTH) for t in (dq3, dk3, dv3)] + [dus, dga, dgs], axis=1)
    gw["w_in_t"] = _mm(dproj, u1, ta=True, name="mm_dw_in")
    du1 = _mm(dproj, W["w_in_t"], name="mm_du1", out_dtype=BF16).reshape(B, S, D_MODEL)
    dx, dsh1, dsc1, dg_mix = _norm_bwd(x3, du1, dh1, g_mix, sc1, "norm_bwd1")

    dmod = jnp.concatenate([t.reshape(B, D_MODEL) for t in (dsh1, dsc1, dgt1, dsh2, dsc2, dgt2)], axis=1)
    gs = dict(
        g_mix=dg_mix.reshape(D_MODEL), b_gate=jnp.concatenate([db_att, db_ssm], axis=1).reshape(2 * D_MODEL),
        a_re=dlr.reshape(SSM_GROUPS, SSM_STATE), a_im=dli.reshape(SSM_GROUPS, SSM_STATE), log_dt=dldt[0, :SSM_GROUPS],
        b_re=_groups_from_lanes(dbr), b_im=_groups_from_lanes(dbi),
        c_re=dcr.reshape(SSM_GROUP_CH, SSM_GROUPS, SSM_STATE).transpose(1, 0, 2),
        c_im=dci.reshape(SSM_GROUP_CH, SSM_GROUPS, SSM_STATE).transpose(1, 0, 2),
        d_skip=dd_skip.reshape(SSM_WIDTH), b_glu=db_glu.reshape(SSM_WIDTH), g_ffn=dg_ffn.reshape(D_MODEL),
        w_conv=dw_conv, b_conv=db_conv.reshape(D_FF), g_final=dg_final.reshape(D_MODEL))
    return loss, dx, dmod, gw, gs


WEIGHTS = ['w_ada', 'b_ada', 'g_mix', 'w_in', 'b_gate', 'a_re', 'a_im', 'log_dt', 'b_re', 'b_im', 'c_re', 'c_im', 'd_skip',
           'w_glu', 'b_glu', 'w_proj_att', 'w_proj_ssm', 'w_out', 'g_ffn', 'w_up', 'w_conv', 'b_conv', 'w_down', 'g_final']
SMALL = ['g_mix', 'b_gate', 'a_re', 'a_im', 'log_dt', 'b_re', 'b_im', 'c_re', 'c_im', 'd_skip', 'b_glu', 'g_ffn', 'w_conv',
         'b_conv', 'g_final']


def kernel(x, c, w_ada, b_ada, g_mix, w_in, b_gate, a_re, a_im, log_dt, b_re, b_im, c_re, c_im, d_skip, w_glu, b_glu, w_proj_att, w_proj_ssm, w_out, g_ffn, w_up, w_conv, b_conv, w_down, g_final, loss_target, m_w_ada, m_b_ada, m_g_mix, m_w_in, m_b_gate, m_a_re, m_a_im, m_log_dt, m_b_re, m_b_im, m_c_re, m_c_im, m_d_skip, m_w_glu, m_b_glu, m_w_proj_att, m_w_proj_ssm, m_w_out, m_g_ffn, m_w_up, m_w_conv, m_b_conv, m_w_down, m_g_final, v_w_ada, v_b_ada, v_g_mix, v_w_in, v_b_gate, v_a_re, v_a_im, v_log_dt, v_b_re, v_b_im, v_c_re, v_c_im, v_d_skip, v_w_glu, v_b_glu, v_w_proj_att, v_w_proj_ssm, v_w_out, v_g_ffn, v_w_up, v_w_conv, v_b_conv, v_w_down, v_g_final):
    args = dict(locals())
    w = {n: args[n] for n in WEIGHTS}
    m = {n: args["m_" + n] for n in WEIGHTS}
    v = {n: args["v_" + n] for n in WEIGHTS}
    B, S, _ = x.shape
    ix, iy, ic = lax.axis_index("x"), lax.axis_index("y"), lax.axis_index("c")
    chip = 2 * ix + iy
    half = FLAT_ROWS // 2
    ada_cols = w_ada.shape[2]

    c_all = _exchange(c, "all", "gather", "gather_c").reshape(8 * B, D_MODEL)
    b_cols = lax.dynamic_slice_in_dim(b_ada, chip * ada_cols, ada_cols, axis=1)
    mod_cols = _ada_fwd(c_all, w_ada[0], b_cols)
    mod_all = _exchange(mod_cols, "xy", "gather", "gather_mod")
    mod_all = mod_all.transpose(1, 0, 2).reshape(8 * B, 6 * D_MODEL)
    mod = lax.dynamic_slice_in_dim(mod_all, (4 * ix + 2 * iy + ic) * B, B, axis=0)

    south = ic == 0
    core = ic.astype(jnp.int32).reshape(1)
    shards = [(w[n][0].T if t else w[n][0]).astype(BF16) for n, t in DIRECT]
    shards.append(_flatten_shards({n: w[n][0] for n, _, _ in BIG}).astype(BF16))
    full = _gather_weights(shards, "gather_weights")
    W = {n + ("_t" if t else ""): f.reshape(-1, LANES) for (n, t), f in zip(DIRECT, full)}
    W.update(_unflatten_full(full[-1]))

    wc_all = _exchange(w_conv[0], "xy", "gather", "gather_w_conv")
    P = {n: w[n][0] for n in SMALL if n not in ("w_conv", "g_final")}
    P["w_conv"] = wc_all.transpose(1, 0, 2).reshape(3, D_FF)
    P["g_final"] = g_final

    loss, dx, dmod, gw, gs = _local_step(x, mod, loss_target, W, P)

    loss = lax.psum(loss[0, 0], MESH_AXES)

    small_shapes = [gs[n].shape for n in SMALL]
    packed, counts = _pack_rows([gs[n] for n in SMALL] + [dmod])
    n_small = sum(counts[:-1])
    gathered = _exchange(packed, "all", "gather", "gather_small")
    small_sum = _sum_slots(gathered[:, :n_small], "sum_small")
    g_small = dict(zip(SMALL, _unpack_rows(small_sum, small_shapes)))
    dmod_all = gathered[:, n_small:].reshape(8, -1)[:, :B * 6 * D_MODEL].reshape(8 * B, 6 * D_MODEL)
    dmod_cols = lax.dynamic_slice_in_dim(dmod_all, chip * ada_cols, ada_cols, axis=1)
    g_w_ada, g_b_ada = _ada_bwd(c_all, dmod_all, dmod_cols)

    G = [gw[n + ("_t" if t else "")].reshape(N_XY, -1, LANES) for n, t in DIRECT]
    G.append(_flatten_full({n: gw[n] for n, _, _ in BIG}))
    theirs = _exchange_list(G, "c", "half", "reduce_cores")
    pair = [_pair_add(g, t, core, "pair_add_%d" % i) for i, (g, t) in enumerate(zip(G, theirs))]
    parts = _exchange_list(pair, "xy", "scatter", "reduce_chips")
    red = [_sum_slots(p, "sum_chips_%d" % i) for i, p in enumerate(parts)]
    red_sib = _exchange_list(red, "c", "swap", "share_cores")
    reduced = [jnp.concatenate([jnp.where(south, r, s), jnp.where(south, s, r)], axis=0) for r, s in zip(red, red_sib)]

    grads = {"w_ada": g_w_ada[None], "b_ada": g_b_ada}
    for (n, t), g in zip(DIRECT, reduced):
        grads[n] = (g.T if t else g)[None]
    for n, g in _unflatten_shard(reduced[-1]).items():
        grads[n] = g[None]
    wc_cols = w_conv.shape[2]
    for n in SMALL:
        g = g_small[n]
        if n == "w_conv":
            g = lax.dynamic_slice_in_dim(g, chip * wc_cols, wc_cols, axis=1)
        grads[n] = g.reshape(w[n].shape)

    delta, new_m, new_v = {}, {}, {}
    for n in ["w_ada"] + [b for b, _ in DIRECT] + [b for b, _, _ in BIG]:
        shp = w[n].shape
        d2, m2, v2 = _adamw(w[n][0], grads[n][0], m[n][0], v[n][0], "adamw_" + n)
        delta[n], new_m[n], new_v[n] = d2.reshape(shp), m2.reshape(shp), v2.reshape(shp)
    rest = ["b_ada"] + SMALL
    shapes = [w[n].shape for n in rest]
    pw, _ = _pack_rows([w[n] for n in rest])
    pg, _ = _pack_rows([grads[n] for n in rest])
    pm, _ = _pack_rows([m[n] for n in rest])
    pv, _ = _pack_rows([v[n] for n in rest])
    d2, m2, v2 = _adamw(pw, pg, pm, pv, "adamw_small")
    for n, dd, mm, vv in zip(rest, _unpack_rows(d2, shapes), _unpack_rows(m2, shapes), _unpack_rows(v2, shapes)):
        delta[n], new_m[n], new_v[n] = dd, mm, vv

    return (loss, dx, *[grads[n] for n in WEIGHTS], *[delta[n] for n in WEIGHTS], *[new_m[n] for n in WEIGHTS],
            *[new_v[n] for n in WEIGHTS])
```

```python
import functools
import math

import numpy as np
import jax
import jax.numpy as jnp
from jax import lax
from jax.experimental import pallas as pl
from jax.experimental.pallas import tpu as pltpu

F32, BF16 = jnp.float32, jnp.bfloat16

D_MODEL = 1024
N_HEADS = 8
HEAD_DIM = 64
ATT_WIDTH = 512
SSM_GROUPS = 16
SSM_GROUP_CH = 16
SSM_WIDTH = 256
SSM_STATE = 64
SSM_LANES = SSM_GROUPS * SSM_STATE
D_FF = 2048
IN_WIDTH = 3 * ATT_WIDTH + SSM_WIDTH + 2 * D_MODEL
ATT_BLOCK = 128
N_PATTERNS = 3
EPS = 1e-6
NEG_INF = -1e30

ADAM_LR, ADAM_B1, ADAM_B2, ADAM_EPS, ADAM_WD, ADAM_STEP = 0.001, 0.9, 0.999, 1e-08, 0.01, 10

V7X_VMEM_LIMIT_BYTES = 56 * 1024 * 1024
LANES = 1024

MESH_AXES = ("x", "y", "c")


def _pcall(body, *, name, out_shape, grid=(), in_specs=None, out_specs=None, scratch_shapes=(), dims=None):
    params = dict(vmem_limit_bytes=V7X_VMEM_LIMIT_BYTES)
    if dims is not None:
        params["dimension_semantics"] = dims
    specs = {}
    if in_specs is not None:
        specs = dict(grid=grid, in_specs=in_specs, out_specs=out_specs)
    return pl.pallas_call(body, name=name, out_shape=out_shape, scratch_shapes=scratch_shapes,
                          compiler_params=pltpu.CompilerParams(**params), **specs)


def _sds(shape, dtype):
    return jax.ShapeDtypeStruct(tuple(shape), dtype)


def _tile(n, target):
    if n <= target:
        return n
    for t in range(target - target % 128, 0, -128):
        if n % t == 0:
            return t
    raise ValueError((n, target))


def _sig(v):
    return 1.0 / (1.0 + jnp.exp(-v))


def _mm(a, b, *, name, ta=False, tb=False, out_dtype=F32, tm=2048, tn=1024, tk=1024):
    if ta:
        K, M = a.shape
    else:
        M, K = a.shape
    if tb:
        N, K2 = b.shape
    else:
        K2, N = b.shape
    assert K == K2, (a.shape, b.shape)
    tm, tn, tk = _tile(M, tm), _tile(N, tn), _tile(K, tk)
    nk = K // tk
    a_spec = pl.BlockSpec((tk, tm), lambda i, j, k: (k, i)) if ta else pl.BlockSpec((tm, tk), lambda i, j, k: (i, k))
    b_spec = pl.BlockSpec((tn, tk), lambda i, j, k: (j, k)) if tb else pl.BlockSpec((tk, tn), lambda i, j, k: (k, j))
    dn = (((0 if ta else 1,), (1 if tb else 0,)), ((), ()))

    def body(a_ref, b_ref, o_ref, acc_ref):
        k = pl.program_id(2)

        @pl.when(k == 0)
        def _():
            acc_ref[...] = jnp.zeros_like(acc_ref)

        acc_ref[...] += lax.dot_general(a_ref[...].astype(BF16), b_ref[...].astype(BF16), dn,
                                        preferred_element_type=F32)

        @pl.when(k == nk - 1)
        def _():
            o_ref[...] = acc_ref[...].astype(out_dtype)

    def body_single(a_ref, b_ref, o_ref):
        o_ref[...] = lax.dot_general(a_ref[...].astype(BF16), b_ref[...].astype(BF16), dn,
                                     preferred_element_type=F32).astype(out_dtype)

    return _pcall(body_single if nk == 1 else body, name=name, out_shape=_sds((M, N), out_dtype),
                  grid=(M // tm, N // tn, nk), in_specs=[a_spec, b_spec],
                  out_specs=pl.BlockSpec((tm, tn), lambda i, j, k: (i, j)),
                  scratch_shapes=[] if nk == 1 else [pltpu.VMEM((tm, tn), F32)],
                  dims=("parallel", "parallel", "arbitrary"))(a, b)


def _ada_fwd(c_all, w_ada, b_ada_cols):
    n = w_ada.shape[1]

    def body(c_ref, w_ref, b_ref, o_ref):
        c = c_ref[...]
        act = c * _sig(c)
        o_ref[...] = jnp.dot(act.astype(BF16), w_ref[...].astype(BF16), preferred_element_type=F32) + b_ref[...]

    return _pcall(body, name="ada_fwd", out_shape=_sds((c_all.shape[0], n), F32))(c_all, w_ada, b_ada_cols)


def _ada_bwd(c_all, dmod_all, dmod_cols):
    n = dmod_cols.shape[1]

    def body(c_ref, da_ref, dc_ref, gw_ref, gb_ref):
        c = c_ref[...]
        act = c * _sig(c)
        gw_ref[...] = lax.dot_general(act, dc_ref[...], (((0,), (0,)), ((), ())), preferred_element_type=F32,
                                      precision=lax.Precision.HIGHEST)
        gb_ref[...] = jnp.sum(da_ref[...], axis=0, keepdims=True)

    return _pcall(body, name="ada_bwd", out_shape=(_sds((D_MODEL, n), F32), _sds((1, dmod_all.shape[1]), F32)))(
        c_all, dmod_all, dmod_cols)


ROW_TILE = 512


def _row_specs(B, S):
    ts = min(S, ROW_TILE)
    row = pl.BlockSpec((1, ts, D_MODEL), lambda b, s: (b, s, 0))
    bvec = pl.BlockSpec((1, 1, D_MODEL), lambda b, s: (b, 0, 0))
    gvec = pl.BlockSpec((1, D_MODEL), lambda b, s: (0, 0))
    return ts, row, bvec, gvec


def _norm_mod(x3, g, sc, sh):
    B, S, _ = x3.shape
    ts, row, bvec, gvec = _row_specs(B, S)

    def body(x_ref, g_ref, sc_ref, sh_ref, u_ref):
        x = x_ref[0]
        r = lax.rsqrt(jnp.mean(x * x, axis=-1, keepdims=True) + EPS)
        u_ref[0] = ((x * r) * g_ref[...] * (1.0 + sc_ref[0]) + sh_ref[0]).astype(BF16)

    return _pcall(body, name="norm_mod1", out_shape=_sds(x3.shape, BF16), grid=(B, S // ts),
                  in_specs=[row, gvec, bvec, bvec], out_specs=row, dims=("parallel", "parallel"))(x3, g, sc, sh)


def _resid_norm_mod(x3, mix3, gt, g, sc, sh):
    B, S, _ = x3.shape
    ts, row, bvec, gvec = _row_specs(B, S)

    def body(x_ref, m_ref, gt_ref, g_ref, sc_ref, sh_ref, h_ref, u_ref):
        h = x_ref[0] + gt_ref[0] * m_ref[0]
        h_ref[0] = h
        r = lax.rsqrt(jnp.mean(h * h, axis=-1, keepdims=True) + EPS)
        u_ref[0] = ((h * r) * g_ref[...] * (1.0 + sc_ref[0]) + sh_ref[0]).astype(BF16)

    return _pcall(body, name="resid_norm_mod2", out_shape=(_sds(x3.shape, F32), _sds(x3.shape, BF16)),
                  grid=(B, S // ts), in_specs=[row, row, bvec, gvec, bvec, bvec], out_specs=(row, row),
                  dims=("parallel", "parallel"))(x3, mix3, gt, g, sc, sh)


def _norm_bwd(h3, du3, dres3, g, sc, name, mix3=None, gt=None):
    B, S, _ = h3.shape
    ts, row, bvec, gvec = _row_specs(B, S)
    with_gate = mix3 is not None

    def body(*refs):
        if with_gate:
            h_ref, du_ref, dr_ref, g_ref, sc_ref, m_ref, gt_ref, dh_ref, dsh_ref, dsc_ref, dg_ref, dgt_ref, dm_ref = refs
        else:
            h_ref, du_ref, dr_ref, g_ref, sc_ref, dh_ref, dsh_ref, dsc_ref, dg_ref = refs
        b, s = pl.program_id(0), pl.program_id(1)
        h = h_ref[0]
        r = lax.rsqrt(jnp.mean(h * h, axis=-1, keepdims=True) + EPS)
        xn = h * r
        du = du_ref[0].astype(F32)
        g = g_ref[...]
        sc1 = 1.0 + sc_ref[0]
        dxn = du * g * sc1
        dh = dr_ref[0] + r * (dxn - xn * jnp.mean(dxn * xn, axis=-1, keepdims=True))
        dh_ref[0] = dh

        @pl.when(s == 0)
        def _():
            dsh_ref[...] = jnp.zeros_like(dsh_ref)
            dsc_ref[...] = jnp.zeros_like(dsc_ref)
            if with_gate:
                dgt_ref[...] = jnp.zeros_like(dgt_ref)

        @pl.when((s == 0) & (b == 0))
        def _():
            dg_ref[...] = jnp.zeros_like(dg_ref)

        dux = du * xn
        dsh_ref[0] += jnp.sum(du, axis=0, keepdims=True)
        dsc_ref[0] += jnp.sum(dux * g, axis=0, keepdims=True)
        dg_ref[...] += jnp.sum(dux * sc1, axis=0, keepdims=True)
        if with_gate:
            dgt_ref[0] += jnp.sum(dh * m_ref[0], axis=0, keepdims=True)
            dm_ref[0] = (dh * gt_ref[0]).astype(BF16)

    bshape = _sds((B, 1, D_MODEL), F32)
    in_specs = [row, row, row, gvec, bvec]
    out_shape = [_sds(h3.shape, F32), bshape, bshape, _sds((1, D_MODEL), F32)]
    out_specs = [row, bvec, bvec, gvec]
    args = [h3, du3, dres3, g, sc]
    if with_gate:
        in_specs += [row, bvec]
        out_shape += [bshape, _sds(h3.shape, BF16)]
        out_specs += [bvec, row]
        args += [mix3, gt]
    return _pcall(body, name=name, out_shape=tuple(out_shape), grid=(B, S // ts), in_specs=in_specs,
                  out_specs=tuple(out_specs), dims=("arbitrary", "arbitrary"))(*args)


def _final_loss(h1, ffn3, tgt3, gt, gfin):
    B, S, _ = h1.shape
    ts, row, bvec, gvec = _row_specs(B, S)
    one = pl.BlockSpec((1, 1), lambda b, s: (0, 0))

    def body(h_ref, f_ref, t_ref, gt_ref, gf_ref, dh_ref, dff_ref, dgt_ref, dgf_ref, loss_ref):
        b, s = pl.program_id(0), pl.program_id(1)
        f = f_ref[0].astype(F32)
        gtv = gt_ref[0]
        gf = gf_ref[...]
        h2 = h_ref[0] + gtv * f
        r = lax.rsqrt(jnp.mean(h2 * h2, axis=-1, keepdims=True) + EPS)
        n = h2 * r
        e = n * gf - t_ref[0]
        dy = e * (1.0 / D_MODEL)
        dn = dy * gf
        dh2 = r * (dn - n * jnp.mean(dn * n, axis=-1, keepdims=True))
        dh_ref[0] = dh2
        dff_ref[0] = (dh2 * gtv).astype(BF16)

        @pl.when(s == 0)
        def _():
            dgt_ref[...] = jnp.zeros_like(dgt_ref)

        @pl.when((s == 0) & (b == 0))
        def _():
            dgf_ref[...] = jnp.zeros_like(dgf_ref)
            loss_ref[...] = jnp.zeros_like(loss_ref)

        dgt_ref[0] += jnp.sum(dh2 * f, axis=0, keepdims=True)
        dgf_ref[...] += jnp.sum(dy * n, axis=0, keepdims=True)
        rows = jnp.sum(e * e, axis=1, keepdims=True)
        loss_ref[...] += jnp.sum(rows, axis=0, keepdims=True) * (0.5 / D_MODEL)

    return _pcall(body, name="final_loss",
                  out_shape=(_sds(h1.shape, F32), _sds(h1.shape, BF16), _sds((B, 1, D_MODEL), F32),
                             _sds((1, D_MODEL), F32), _sds((1, 1), F32)),
                  grid=(B, S // ts), in_specs=[row, row, row, bvec, gvec], out_specs=(row, row, bvec, gvec, one),
                  dims=("arbitrary", "arbitrary"))(h1, ffn3, tgt3, gt, gfin)


def _att_scores(qh, kc, kp, h, dil, first, a_idx, j_idx):
    scale = HEAD_DIM ** -0.5
    nt = (((1,), (1,)), ((), ()))
    slope = (2.0 ** (-8.0 * (h + 1) / N_HEADS)) * dil
    dist_c = (a_idx - j_idx).astype(F32)
    s_c = lax.dot_general(qh, kc, nt, preferred_element_type=F32) * scale
    s_c = jnp.where(a_idx >= j_idx, s_c - slope * dist_c, NEG_INF)
    s_p = lax.dot_general(qh, kp, nt, preferred_element_type=F32) * scale
    s_p = jnp.where((j_idx >= a_idx) & jnp.logical_not(first), s_p - slope * (dist_c + float(ATT_BLOCK)), NEG_INF)
    return s_c, s_p


def _att_block_consts(seq_blocks):
    p = pl.program_id(0)
    j = pl.program_id(1)
    nb = lax.shift_right_logical(jnp.int32(seq_blocks), 2 * p)
    dil = lax.shift_left(jnp.int32(1), 2 * p).astype(F32)
    a_idx = lax.broadcasted_iota(jnp.int32, (ATT_BLOCK, ATT_BLOCK), 0)
    j_idx = lax.broadcasted_iota(jnp.int32, (ATT_BLOCK, ATT_BLOCK), 1)
    return j, nb, dil, a_idx, j_idx


def _attn_fwd(qb, kb, vb, seq_blocks):
    _, NB, _, _ = qb.shape
    cur = pl.BlockSpec((None, None, ATT_BLOCK, ATT_WIDTH), lambda p, j: (p, j, 0, 0))
    prev = pl.BlockSpec((None, None, ATT_BLOCK, ATT_WIDTH), lambda p, j: (p, jnp.maximum(j - 1, 0), 0, 0))
    lse_spec = pl.BlockSpec((None, None, ATT_BLOCK, N_HEADS), lambda p, j: (p, j, 0, 0))

    def body(q_ref, kc_ref, kp_ref, vc_ref, vp_ref, o_ref, lse_ref):
        j, nb, dil, a_idx, j_idx = _att_block_consts(seq_blocks)
        first = lax.rem(j, nb) == 0
        for h in range(N_HEADS):
            hs = slice(h * HEAD_DIM, (h + 1) * HEAD_DIM)
            s_c, s_p = _att_scores(q_ref[:, hs], kc_ref[:, hs], kp_ref[:, hs], h, dil, first, a_idx, j_idx)
            m = jnp.maximum(jnp.max(s_c, axis=1, keepdims=True), jnp.max(s_p, axis=1, keepdims=True))
            p_c = jnp.exp(s_c - m)
            p_p = jnp.exp(s_p - m)
            den = jnp.sum(p_c, axis=1, keepdims=True) + jnp.sum(p_p, axis=1, keepdims=True)
            o = (jnp.dot(p_c.astype(BF16), vc_ref[:, hs], preferred_element_type=F32)
                 + jnp.dot(p_p.astype(BF16), vp_ref[:, hs], preferred_element_type=F32))
            o_ref[:, hs] = o / den
            lse_ref[:, h:h + 1] = m + jnp.log(den)

    return _pcall(body, name="attn_fwd",
                  out_shape=(_sds(qb.shape, F32), _sds((N_PATTERNS, NB, ATT_BLOCK, N_HEADS), F32)),
                  grid=(N_PATTERNS, NB), in_specs=[cur, cur, prev, cur, prev], out_specs=(cur, lse_spec),
                  dims=("parallel", "parallel"))(qb, kb, kb, vb, vb)


def _attn_combine(o_p, lse_p):
    _, T, _ = o_p.shape
    tm = min(T, 1024)

    def body(o_ref, l_ref, out_ref, lse_ref):
        l0, l1, l2 = l_ref[0], l_ref[1], l_ref[2]
        m = jnp.maximum(jnp.maximum(l0, l1), l2)
        lse = m + jnp.log(jnp.exp(l0 - m) + jnp.exp(l1 - m) + jnp.exp(l2 - m))
        lse_ref[...] = lse
        w = [jnp.exp(l0 - lse), jnp.exp(l1 - lse), jnp.exp(l2 - lse)]
        for h in range(N_HEADS):
            hs = slice(h * HEAD_DIM, (h + 1) * HEAD_DIM)
            acc = w[0][:, h:h + 1] * o_ref[0, :, hs]
            acc = acc + w[1][:, h:h + 1] * o_ref[1, :, hs]
            acc = acc + w[2][:, h:h + 1] * o_ref[2, :, hs]
            out_ref[:, hs] = acc.astype(BF16)

    return _pcall(body, name="attn_combine", out_shape=(_sds((T, ATT_WIDTH), BF16), _sds((T, N_HEADS), F32)),
                  grid=(T // tm,),
                  in_specs=[pl.BlockSpec((N_PATTERNS, tm, ATT_WIDTH), lambda i: (0, i, 0)),
                            pl.BlockSpec((N_PATTERNS, tm, N_HEADS), lambda i: (0, i, 0))],
                  out_specs=(pl.BlockSpec((tm, ATT_WIDTH), lambda i: (i, 0)), pl.BlockSpec((tm, N_HEADS), lambda i: (i, 0))),
                  dims=("parallel",))(o_p, lse_p)


def _attn_bwd(qb, kb, vb, dob, ob, lseb, seq_blocks):
    _, NB, _, _ = qb.shape
    last = NB - 1
    cur = pl.BlockSpec((None, None, ATT_BLOCK, ATT_WIDTH), lambda p, j: (p, jnp.minimum(j, last), 0, 0))
    prev = pl.BlockSpec((None, None, ATT_BLOCK, ATT_WIDTH),
                        lambda p, j: (p, jnp.maximum(jnp.minimum(j, last) - 1, 0), 0, 0))
    lag = pl.BlockSpec((None, None, ATT_BLOCK, ATT_WIDTH), lambda p, j: (p, jnp.maximum(j - 1, 0), 0, 0))
    lse_spec = pl.BlockSpec((None, None, ATT_BLOCK, N_HEADS), lambda p, j: (p, jnp.minimum(j, last), 0, 0))
    scale = HEAD_DIM ** -0.5
    tn = (((0,), (0,)), ((), ()))
    nt = (((1,), (1,)), ((), ()))

    def body(q_ref, kc_ref, kp_ref, vc_ref, vp_ref, do_ref, o_ref, lse_ref, dq_ref, dk_ref, dv_ref, ck_ref, cv_ref):
        j, nb, dil, a_idx, j_idx = _att_block_consts(seq_blocks)

        @pl.when(j == 0)
        def _():
            ck_ref[...] = jnp.zeros_like(ck_ref)
            cv_ref[...] = jnp.zeros_like(cv_ref)

        @pl.when(j <= last)
        def _():
            first = lax.rem(j, nb) == 0
            for h in range(N_HEADS):
                hs = slice(h * HEAD_DIM, (h + 1) * HEAD_DIM)
                qh, kc, kp, vc, vp, doh = q_ref[:, hs], kc_ref[:, hs], kp_ref[:, hs], vc_ref[:, hs], vp_ref[:, hs], do_ref[:, hs]
                s_c, s_p = _att_scores(qh, kc, kp, h, dil, first, a_idx, j_idx)
                lse = lse_ref[:, h:h + 1]
                p_c = jnp.exp(s_c - lse)
                p_p = jnp.exp(s_p - lse)
                delta = jnp.sum(doh.astype(F32) * o_ref[:, hs].astype(F32), axis=1, keepdims=True)
                ds_c = (p_c * (lax.dot_general(doh, vc, nt, preferred_element_type=F32) - delta)).astype(BF16)
                ds_p = (p_p * (lax.dot_general(doh, vp, nt, preferred_element_type=F32) - delta)).astype(BF16)
                dq_ref[:, hs] = (jnp.dot(ds_c, kc, preferred_element_type=F32)
                                 + jnp.dot(ds_p, kp, preferred_element_type=F32)) * scale
                dk_ref[:, hs] = ck_ref[:, hs] + lax.dot_general(ds_p, qh, tn, preferred_element_type=F32) * scale
                dv_ref[:, hs] = cv_ref[:, hs] + lax.dot_general(p_p.astype(BF16), doh, tn, preferred_element_type=F32)
                ck_ref[:, hs] = lax.dot_general(ds_c, qh, tn, preferred_element_type=F32) * scale
                cv_ref[:, hs] = lax.dot_general(p_c.astype(BF16), doh, tn, preferred_element_type=F32)

        @pl.when(j == NB)
        def _():
            dk_ref[...] = ck_ref[...]
            dv_ref[...] = cv_ref[...]

    shp = _sds(qb.shape, F32)
    return _pcall(body, name="attn_bwd", out_shape=(shp, shp, shp), grid=(N_PATTERNS, NB + 1),
                  in_specs=[cur, cur, prev, cur, prev, cur, cur, lse_spec], out_specs=(cur, lag, lag),
                  scratch_shapes=[pltpu.VMEM((ATT_BLOCK, ATT_WIDTH), F32), pltpu.VMEM((ATT_BLOCK, ATT_WIDTH), F32)],
                  dims=("arbitrary", "arbitrary"))(qb, kb, kb, vb, vb, dob, ob, lseb)


def _sum3_cast(a, b, c):
    T, N = a.shape
    tm = min(T, 1024)
    spec = pl.BlockSpec((tm, N), lambda i: (i, 0))

    def body(a_ref, b_ref, c_ref, o_ref):
        o_ref[...] = (a_ref[...] + b_ref[...] + c_ref[...]).astype(BF16)

    return _pcall(body, name="sum3_cast", out_shape=_sds((T, N), BF16), grid=(T // tm,), in_specs=[spec] * 3,
                  out_specs=spec, dims=("parallel",))(a, b, c)


def _to_blocks(t, B, S):
    C = t.shape[-1]
    outs = []
    for p in range(N_PATTERNS):
        d = 4 ** p
        u = t.reshape(B, S // d, d, C).transpose(0, 2, 1, 3)
        outs.append(u.reshape(B * S // ATT_BLOCK, ATT_BLOCK, C))
    return jnp.stack(outs, axis=0)


def _from_blocks(tb, B, S):
    C = tb.shape[-1]
    outs = []
    for p in range(N_PATTERNS):
        d = 4 ** p
        u = tb[p].reshape(B, d, S // d, C).transpose(0, 2, 1, 3)
        outs.append(u.reshape(B * S, C))
    return jnp.stack(outs, axis=0)


ATT_GROUP = 4
ATT_GW = ATT_GROUP * HEAD_DIM
ATT_GROUPS = N_HEADS // ATT_GROUP
ATT_PAIRS = ATT_GW // ATT_BLOCK
ATT_UNROLL = 3
NT_DIMS = (((1,), (1,)), ((), ()))
TN_DIMS = (((0,), (0,)), ((), ()))


def _att_rows(start, d):
    if d == 1:
        return pl.ds(start if isinstance(start, int) else pl.multiple_of(start, ATT_BLOCK), ATT_BLOCK)
    return pl.ds(start, ATT_BLOCK, stride=d)


def _att_fill_bias(bias_ref, g, d):
    a = lax.broadcasted_iota(jnp.int32, (ATT_BLOCK, ATT_BLOCK), 0)
    j = lax.broadcasted_iota(jnp.int32, (ATT_BLOCK, ATT_BLOCK), 1)
    dist = (a - j).astype(F32)
    for hh in range(ATT_GROUP):
        t, e = divmod(hh, 2)
        rs = slice(e * ATT_BLOCK, (e + 1) * ATT_BLOCK)
        lo = 2.0 ** (-8.0 * (hh + 1) / N_HEADS) * d
        hi = 2.0 ** (-8.0 * (ATT_GROUP + hh + 1) / N_HEADS) * d
        slope = jnp.where(g == 0, lo, hi).astype(F32)
        bias_ref[t, rs, 0:ATT_BLOCK] = jnp.where(a >= j, -slope * dist, NEG_INF)
        bias_ref[t, rs, ATT_BLOCK:] = jnp.where(j >= a, -slope * (dist + float(ATT_BLOCK)), NEG_INF)


def _stack_heads(v2, low):
    return jnp.concatenate([jnp.where(low, v2, 0.0), jnp.where(low, 0.0, v2)], axis=0).astype(BF16)


def _unstack_heads(r2, low):
    return jnp.where(low, r2[0:ATT_BLOCK], r2[ATT_BLOCK:])


def _attention_fwd(proj3, seq_blocks):
    B, S, _ = proj3.shape
    scale = HEAD_DIM ** -0.5
    nq = ATT_WIDTH // ATT_GW

    def col(k):
        return pl.BlockSpec((1, S, ATT_GW), lambda b, g, k=k: (b, 0, k * nq + g))

    o_spec = pl.BlockSpec((1, S, ATT_GW), lambda b, g: (b, 0, g))
    l_spec = pl.BlockSpec((1, 1, S, ATT_BLOCK), lambda b, g: (b, g, 0, 0))

    def body(q_ref, k_ref, v_ref, o_ref, lse_ref, qf, kf, vf, os, ls, bias):
        g = pl.program_id(1)
        for t in range(ATT_PAIRS):
            ts = slice(t * ATT_BLOCK, (t + 1) * ATT_BLOCK)
            qf[t] = q_ref[0, :, ts].astype(F32) * scale
            kf[t] = k_ref[0, :, ts].astype(F32)
            vf[t] = v_ref[0, :, ts].astype(F32)
        lane = lax.broadcasted_iota(jnp.int32, (ATT_BLOCK, ATT_BLOCK), 1)
        low = lane < HEAD_DIM

        def block(p, d, r, n, has_prev):
            start = n * (ATT_BLOCK * d) + r
            rows = _att_rows(start, d)
            prows = _att_rows(start - ATT_BLOCK * d, d) if has_prev else None
            lse_t = jnp.zeros((ATT_BLOCK, ATT_BLOCK), F32)
            for t in range(ATT_PAIRS):
                q2 = _stack_heads(qf[t, rows, :], low)
                k2 = kf[t, rows, :].astype(BF16)
                v2 = vf[t, rows, :].astype(BF16)
                if has_prev:
                    k2 = jnp.concatenate([k2, kf[t, prows, :].astype(BF16)], axis=0)
                    v2 = jnp.concatenate([v2, vf[t, prows, :].astype(BF16)], axis=0)
                    b2 = bias[t]
                else:
                    b2 = bias[t, :, 0:ATT_BLOCK]
                s = lax.dot_general(q2, k2, NT_DIMS, preferred_element_type=F32) + b2
                m = jnp.max(s, axis=1, keepdims=True)
                pr = jnp.exp(s - m)
                den = jnp.sum(pr, axis=1, keepdims=True)
                o = jnp.dot(pr.astype(BF16), v2, preferred_element_type=F32) / den
                os[p, t, rows, :] = _unstack_heads(o, low)
                lse2 = m + jnp.log(den)
                lse_t = jnp.where(lane == 2 * t, lse2[0:ATT_BLOCK], lse_t)
                lse_t = jnp.where(lane == 2 * t + 1, lse2[ATT_BLOCK:], lse_t)
            ls[p, rows, :] = lse_t

        for p in range(N_PATTERNS):
            d = 4 ** p
            _att_fill_bias(bias, g, d)
            _att_one_pattern(block, p, d, seq_blocks // d)

        def combine(i, carry):
            rows = pl.ds(pl.multiple_of(i * ATT_BLOCK, ATT_BLOCK), ATT_BLOCK)
            l0, l1, l2 = ls[0, rows, :], ls[1, rows, :], ls[2, rows, :]
            m = jnp.maximum(jnp.maximum(l0, l1), l2)
            lse = m + jnp.log(jnp.exp(l0 - m) + jnp.exp(l1 - m) + jnp.exp(l2 - m))
            lse_ref[0, 0, rows, :] = lse
            w = [jnp.exp(l0 - lse), jnp.exp(l1 - lse), jnp.exp(l2 - lse)]
            for t in range(ATT_PAIRS):
                acc = jnp.zeros((ATT_BLOCK, ATT_BLOCK), F32)
                for p in range(N_PATTERNS):
                    wt = jnp.where(low, w[p][:, 2 * t:2 * t + 1], w[p][:, 2 * t + 1:2 * t + 2])
                    acc = acc + wt * os[p, t, rows, :]
                o_ref[0, rows, t * ATT_BLOCK:(t + 1) * ATT_BLOCK] = acc.astype(BF16)
            return carry

        lax.fori_loop(0, S // ATT_BLOCK, combine, 0, unroll=2)

    return _pcall(body, name="attention_fwd",
                  out_shape=(_sds((B, S, ATT_WIDTH), BF16), _sds((B, ATT_GROUPS, S, ATT_BLOCK), F32)),
                  grid=(B, ATT_GROUPS), in_specs=[col(0), col(1), col(2)], out_specs=(o_spec, l_spec),
                  scratch_shapes=[pltpu.VMEM((ATT_PAIRS, S, ATT_BLOCK), F32)] * 3
                  + [pltpu.VMEM((N_PATTERNS, ATT_PAIRS, S, ATT_BLOCK), F32), pltpu.VMEM((N_PATTERNS, S, ATT_BLOCK), F32),
                     pltpu.VMEM((ATT_PAIRS, 2 * ATT_BLOCK, 2 * ATT_BLOCK), F32)],
                  dims=("parallel", "parallel"))(proj3, proj3, proj3)


def _att_one_pattern(block, p, d, nb):
    def per_residue(r, carry):
        block(p, d, r, 0, False)
        if nb > 1:
            def per_block(n, c2):
                block(p, d, r, n, True)
                return c2
            lax.fori_loop(1, nb, per_block, 0, unroll=ATT_UNROLL)
        return carry

    if d == 1:
        per_residue(0, 0)
    else:
        lax.fori_loop(0, d, per_residue, 0, unroll=ATT_UNROLL + 1 if nb == 1 else 1)


def _attention_bwd(proj3, do3, o3, lse4, seq_blocks):
    B, S, _ = proj3.shape
    scale = HEAD_DIM ** -0.5
    nq = ATT_WIDTH // ATT_GW

    def col(k):
        return pl.BlockSpec((1, S, ATT_GW), lambda b, g, k=k: (b, 0, k * nq + g))

    o_spec = pl.BlockSpec((1, S, ATT_GW), lambda b, g: (b, 0, g))
    l_spec = pl.BlockSpec((1, 1, S, ATT_BLOCK), lambda b, g: (b, g, 0, 0))

    def body(q_ref, k_ref, v_ref, do_ref, o_ref, lse_ref, dq_ref, dk_ref, dv_ref,
             qf, kf, vf, dof, dl, aq, ak, av, bias):
        g = pl.program_id(1)
        for t in range(ATT_PAIRS):
            ts = slice(t * ATT_BLOCK, (t + 1) * ATT_BLOCK)
            qf[t] = q_ref[0, :, ts].astype(F32) * scale
            kf[t] = k_ref[0, :, ts].astype(F32)
            vf[t] = v_ref[0, :, ts].astype(F32)
            dof[t] = do_ref[0, :, ts].astype(F32)
        aq[...] = jnp.zeros_like(aq)
        ak[...] = jnp.zeros_like(ak)
        av[...] = jnp.zeros_like(av)
        lane = lax.broadcasted_iota(jnp.int32, (ATT_BLOCK, ATT_BLOCK), 1)
        low = lane < HEAD_DIM

        def fill_delta(i, carry):
            rows = pl.ds(pl.multiple_of(i * ATT_BLOCK, ATT_BLOCK), ATT_BLOCK)
            acc = jnp.zeros((ATT_BLOCK, ATT_BLOCK), F32)
            for t in range(ATT_PAIRS):
                prod = dof[t, rows, :] * o_ref[0, rows, t * ATT_BLOCK:(t + 1) * ATT_BLOCK].astype(F32)
                lo = jnp.sum(jnp.where(low, prod, 0.0), axis=1, keepdims=True)
                hi = jnp.sum(prod, axis=1, keepdims=True) - lo
                acc = jnp.where(lane == 2 * t, lo, acc)
                acc = jnp.where(lane == 2 * t + 1, hi, acc)
            dl[rows, :] = acc
            return carry

        lax.fori_loop(0, S // ATT_BLOCK, fill_delta, 0, unroll=2)

        def block(p, d, r, n, has_prev):
            start = n * (ATT_BLOCK * d) + r
            rows = _att_rows(start, d)
            prows = _att_rows(start - ATT_BLOCK * d, d) if has_prev else None
            lse_t = lse_ref[0, 0, rows, :]
            dl_t = dl[rows, :]
            for t in range(ATT_PAIRS):
                q2 = _stack_heads(qf[t, rows, :], low)
                do2 = _stack_heads(dof[t, rows, :], low)
                k2 = kf[t, rows, :].astype(BF16)
                v2 = vf[t, rows, :].astype(BF16)
                if has_prev:
                    k2 = jnp.concatenate([k2, kf[t, prows, :].astype(BF16)], axis=0)
                    v2 = jnp.concatenate([v2, vf[t, prows, :].astype(BF16)], axis=0)
                    b2 = bias[t]
                else:
                    b2 = bias[t, :, 0:ATT_BLOCK]
                lse2 = jnp.concatenate([lse_t[:, 2 * t:2 * t + 1], lse_t[:, 2 * t + 1:2 * t + 2]], axis=0)
                dl2 = jnp.concatenate([dl_t[:, 2 * t:2 * t + 1], dl_t[:, 2 * t + 1:2 * t + 2]], axis=0)
                s = lax.dot_general(q2, k2, NT_DIMS, preferred_element_type=F32) + b2
                pr = jnp.exp(s - lse2)
                ds = (pr * (lax.dot_general(do2, v2, NT_DIMS, preferred_element_type=F32) - dl2)).astype(BF16)
                dq = _unstack_heads(jnp.dot(ds, k2, preferred_element_type=F32), low)
                dk = lax.dot_general(ds, q2, TN_DIMS, preferred_element_type=F32)
                dv = lax.dot_general(pr.astype(BF16), do2, TN_DIMS, preferred_element_type=F32)
                aq[t, rows, :] = aq[t, rows, :] + dq * scale
                ak[t, rows, :] = ak[t, rows, :] + dk[0:ATT_BLOCK]
                av[t, rows, :] = av[t, rows, :] + dv[0:ATT_BLOCK]
                if has_prev:
                    ak[t, prows, :] = ak[t, prows, :] + dk[ATT_BLOCK:]
                    av[t, prows, :] = av[t, prows, :] + dv[ATT_BLOCK:]

        for p in range(N_PATTERNS):
            d = 4 ** p
            _att_fill_bias(bias, g, d)
            _att_one_pattern(block, p, d, seq_blocks // d)

        for t in range(ATT_PAIRS):
            ts = slice(t * ATT_BLOCK, (t + 1) * ATT_BLOCK)
            dq_ref[0, :, ts] = aq[t].astype(BF16)
            dk_ref[0, :, ts] = ak[t].astype(BF16)
            dv_ref[0, :, ts] = av[t].astype(BF16)

    shp = _sds((B, S, ATT_WIDTH), BF16)
    pair_buf = pltpu.VMEM((ATT_PAIRS, S, ATT_BLOCK), F32)
    return _pcall(body, name="attention_bwd", out_shape=(shp, shp, shp), grid=(B, ATT_GROUPS),
                  in_specs=[col(0), col(1), col(2), o_spec, o_spec, l_spec], out_specs=(o_spec, o_spec, o_spec),
                  scratch_shapes=[pair_buf] * 4 + [pltpu.VMEM((S, ATT_BLOCK), F32)] + [pair_buf] * 3
                  + [pltpu.VMEM((ATT_PAIRS, 2 * ATT_BLOCK, 2 * ATT_BLOCK), F32)],
                  dims=("parallel", "parallel"))(proj3, proj3, proj3, do3, o3, lse4)


def _expand_groups(m):
    rows = SSM_WIDTH
    t = jnp.concatenate([m] * SSM_GROUPS, axis=0)
    r = lax.broadcasted_iota(jnp.int32, (rows, SSM_LANES), 0)
    l = lax.broadcasted_iota(jnp.int32, (rows, SSM_LANES), 1)
    keep = lax.shift_right_logical(r, 4) == lax.shift_right_logical(l, 6)
    return jnp.where(keep, t, 0.0)


def _collapse_groups(m):
    rows = SSM_WIDTH
    r = lax.broadcasted_iota(jnp.int32, (rows, SSM_LANES), 0)
    l = lax.broadcasted_iota(jnp.int32, (rows, SSM_LANES), 1)
    keep = lax.shift_right_logical(r, 4) == lax.shift_right_logical(l, 6)
    t = jnp.where(keep, m, 0.0)
    acc = t[0:SSM_GROUP_CH]
    for g in range(1, SSM_GROUPS):
        acc = acc + t[g * SSM_GROUP_CH:(g + 1) * SSM_GROUP_CH]
    return acc


def _zoh(lr, li, ldt):
    dt = jnp.exp(ldt)
    mag = jnp.exp(lr * dt)
    ang = li * dt
    cs, sn = jnp.cos(ang), jnp.sin(ang)
    ab_re, ab_im = mag * cs, mag * sn
    nr, ni = ab_re - 1.0, ab_im
    den = lr * lr + li * li
    n_re = nr * lr + ni * li
    n_im = ni * lr - nr * li
    return dict(dt=dt, mag=mag, cs=cs, sn=sn, ab_re=ab_re, ab_im=ab_im, nr=nr, ni=ni, den=den, n_re=n_re, n_im=n_im,
                f_re=n_re / den, f_im=n_im / den)


def _ssm_params(lr, li, ldt, br, bi, cr, ci):
    def body(lr_ref, li_ref, ldt_ref, br_ref, bi_ref, cr_ref, ci_ref, ab_ref, w_ref, c_ref):
        z = _zoh(lr_ref[...], li_ref[...], ldt_ref[...])
        ab_ref[0:1, :] = z["ab_re"]
        ab_ref[1:2, :] = z["ab_im"]
        br, bi = br_ref[...], bi_ref[...]
        w_ref[:, 0:SSM_LANES] = _expand_groups(z["f_re"] * br - z["f_im"] * bi).astype(BF16)
        w_ref[:, SSM_LANES:] = _expand_groups(z["f_re"] * bi + z["f_im"] * br).astype(BF16)
        c_ref[:, 0:SSM_LANES] = _expand_groups(cr_ref[...]).astype(BF16)
        c_ref[:, SSM_LANES:] = _expand_groups(-ci_ref[...]).astype(BF16)

    return _pcall(body, name="ssm_params",
                  out_shape=(_sds((2, SSM_LANES), F32), _sds((SSM_WIDTH, 2 * SSM_LANES), BF16),
                             _sds((SSM_WIDTH, 2 * SSM_LANES), BF16)))(lr, li, ldt, br, bi, cr, ci)


def _ssm_params_bwd(lr, li, ldt, br, bi, dab, dw, dc):
    def body(lr_ref, li_ref, ldt_ref, br_ref, bi_ref, dab_ref, dw_ref, dc_ref,
             dlr_ref, dli_ref, dldt_ref, dbr_ref, dbi_ref, dcr_ref, dci_ref):
        lr, li = lr_ref[...], li_ref[...]
        z = _zoh(lr, li, ldt_ref[...])
        br, bi = br_ref[...], bi_ref[...]
        dbb_re = _collapse_groups(dw_ref[:, 0:SSM_LANES])
        dbb_im = _collapse_groups(dw_ref[:, SSM_LANES:])
        dcr_ref[...] = _collapse_groups(dc_ref[:, 0:SSM_LANES])
        dci_ref[...] = -_collapse_groups(dc_ref[:, SSM_LANES:])
        f_re, f_im = z["f_re"], z["f_im"]
        dbr_ref[...] = f_re * dbb_re + f_im * dbb_im
        dbi_ref[...] = f_re * dbb_im - f_im * dbb_re
        df_re = jnp.sum(dbb_re * br + dbb_im * bi, axis=0, keepdims=True)
        df_im = jnp.sum(dbb_im * br - dbb_re * bi, axis=0, keepdims=True)
        den = z["den"]
        dn_re, dn_im = df_re / den, df_im / den
        dden = -(df_re * z["n_re"] + df_im * z["n_im"]) / (den * den)
        dnr = dn_re * lr - dn_im * li
        dni = dn_re * li + dn_im * lr
        dlr = dn_re * z["nr"] + dn_im * z["ni"] + 2.0 * dden * lr
        dli = dn_re * z["ni"] - dn_im * z["nr"] + 2.0 * dden * li
        dab_re = dab_ref[0:1, :] + dnr
        dab_im = dab_ref[1:2, :] + dni
        mag, cs, sn, dt = z["mag"], z["cs"], z["sn"], z["dt"]
        dmag = dab_re * cs + dab_im * sn
        dang = mag * (dab_im * cs - dab_re * sn)
        dlr_ref[...] = dlr + dmag * mag * dt
        dli_ref[...] = dli + dang * dt
        ddt = dmag * mag * lr + dang * li
        per_lane = jnp.broadcast_to(ddt * dt, (8, SSM_LANES))
        lane = lax.broadcasted_iota(jnp.int32, (SSM_LANES, 128), 0)
        col = lax.broadcasted_iota(jnp.int32, (SSM_LANES, 128), 1)
        ind = jnp.where(lax.shift_right_logical(lane, 6) == col, 1.0, 0.0)
        dldt_ref[...] = jnp.dot(per_lane, ind, preferred_element_type=F32, precision=lax.Precision.HIGHEST)[0:1]

    vec = _sds((1, SSM_LANES), F32)
    mat = _sds((SSM_GROUP_CH, SSM_LANES), F32)
    return _pcall(body, name="ssm_params_bwd", out_shape=(vec, vec, _sds((1, 128), F32), mat, mat, mat, mat))(
        lr, li, ldt, br, bi, dab, dw, dc)


SCAN_CHUNK = 512


def _scan_consts(ar, ai, k_ref, reverse):
    row = lax.broadcasted_iota(jnp.int32, (8, SSM_LANES), 0)
    pw = [(ar, ai)]
    for _ in range(7):
        pr, pi = pw[-1]
        pw.append((pr * ar - pi * ai, pr * ai + pi * ar))
    for n, k in enumerate((1, 2, 4)):
        keep = (row < 8 - k) if reverse else (row >= k)
        k_ref[2 * n] = jnp.where(keep, jnp.broadcast_to(pw[k - 1][0], (8, SSM_LANES)), 0.0)
        k_ref[2 * n + 1] = jnp.where(keep, jnp.broadcast_to(pw[k - 1][1], (8, SSM_LANES)), 0.0)
    cr = jnp.zeros((8, SSM_LANES), F32)
    ci = jnp.zeros((8, SSM_LANES), F32)
    for r in range(8):
        e = (8 - r) if reverse else (r + 1)
        cr = jnp.where(row == r, jnp.broadcast_to(pw[e - 1][0], (8, SSM_LANES)), cr)
        ci = jnp.where(row == r, jnp.broadcast_to(pw[e - 1][1], (8, SSM_LANES)), ci)
    k_ref[6] = cr
    k_ref[7] = ci


def _scan_tile(xr, xi, k_ref, car, cai, reverse):
    for n, k in enumerate((1, 2, 4)):
        sh = (8 - k) if reverse else k
        sr = pltpu.roll(xr, sh, 0)
        si = pltpu.roll(xi, sh, 0)
        mr, mi = k_ref[2 * n], k_ref[2 * n + 1]
        xr, xi = xr + mr * sr - mi * si, xi + mr * si + mi * sr
    pr, pi = k_ref[6], k_ref[7]
    xr, xi = xr + pr * car - pi * cai, xi + pr * cai + pi * car
    return xr, xi


def _scan_fwd(bu3, abar):
    B, S, _ = bu3.shape
    ch = min(S, SCAN_CHUNK)
    blk = pl.BlockSpec((1, ch, 2 * SSM_LANES), lambda b, c: (b, c, 0))

    def body(ab_ref, bu_ref, x_ref, k_ref, carry_ref):
        _scan_consts(ab_ref[0:1, :], ab_ref[1:2, :], k_ref, False)

        @pl.when(pl.program_id(1) == 0)
        def _():
            carry_ref[...] = jnp.zeros_like(carry_ref)

        def step(i, carry):
            base = pl.multiple_of(i * 8, 8)
            xr = bu_ref[0, pl.ds(base, 8), 0:SSM_LANES]
            xi = bu_ref[0, pl.ds(base, 8), SSM_LANES:]
            xr, xi = _scan_tile(xr, xi, k_ref, carry[0], carry[1], False)
            x_ref[0, pl.ds(base, 8), 0:SSM_LANES] = xr
            x_ref[0, pl.ds(base, 8), SSM_LANES:] = xi
            return (jnp.broadcast_to(xr[7:8], (8, SSM_LANES)), jnp.broadcast_to(xi[7:8], (8, SSM_LANES)))

        cr, ci = lax.fori_loop(0, ch // 8, step, (carry_ref[0], carry_ref[1]))
        carry_ref[0] = cr
        carry_ref[1] = ci

    return _pcall(body, name="scan_fwd", out_shape=_sds(bu3.shape, F32), grid=(B, S // ch),
                  in_specs=[pl.BlockSpec((2, SSM_LANES), lambda b, c: (0, 0)), blk], out_specs=blk,
                  scratch_shapes=[pltpu.VMEM((8, 8, SSM_LANES), F32), pltpu.VMEM((2, 8, SSM_LANES), F32)],
                  dims=("arbitrary", "arbitrary"))(abar, bu3)


def _scan_bwd(dx3, xs3, abar):
    B, S, _ = dx3.shape
    ch = min(S, SCAN_CHUNK)
    nc = S // ch
    blk = pl.BlockSpec((1, ch, 2 * SSM_LANES), lambda b, c: (b, nc - 1 - c, 0))

    def body(ab_ref, dx_ref, xs_ref, g_ref, da_ref, k_ref, carry_ref, acc_ref):
        b, c = pl.program_id(0), pl.program_id(1)
        _scan_consts(ab_ref[0:1, :], -ab_ref[1:2, :], k_ref, True)
        row = lax.broadcasted_iota(jnp.int32, (8, SSM_LANES), 0)

        @pl.when(c == 0)
        def _():
            carry_ref[...] = jnp.zeros_like(carry_ref)

        @pl.when((c == 0) & (b == 0))
        def _():
            acc_ref[...] = jnp.zeros_like(acc_ref)

        def step(i, carry):
            car, cai, ar_acc, ai_acc = carry
            base = pl.multiple_of((ch // 8 - 1 - i) * 8, 8)
            gr = dx_ref[0, pl.ds(base, 8), 0:SSM_LANES]
            gi = dx_ref[0, pl.ds(base, 8), SSM_LANES:]
            gr, gi = _scan_tile(gr, gi, k_ref, car, cai, True)
            g_ref[0, pl.ds(base, 8), 0:SSM_LANES] = gr
            g_ref[0, pl.ds(base, 8), SSM_LANES:] = gi
            nr = jnp.where(row == 7, car, pltpu.roll(gr, 7, 0))
            ni = jnp.where(row == 7, cai, pltpu.roll(gi, 7, 0))
            xr = xs_ref[0, pl.ds(base, 8), 0:SSM_LANES]
            xi = xs_ref[0, pl.ds(base, 8), SSM_LANES:]
            ar_acc = ar_acc + nr * xr + ni * xi
            ai_acc = ai_acc + ni * xr - nr * xi
            return (jnp.broadcast_to(gr[0:1], (8, SSM_LANES)), jnp.broadcast_to(gi[0:1], (8, SSM_LANES)), ar_acc, ai_acc)

        cr, ci, ar_acc, ai_acc = lax.fori_loop(0, ch // 8, step, (carry_ref[0], carry_ref[1], acc_ref[0], acc_ref[1]))
        carry_ref[0] = cr
        carry_ref[1] = ci
        acc_ref[0] = ar_acc
        acc_ref[1] = ai_acc
        da_ref[0:1, :] = jnp.sum(ar_acc, axis=0, keepdims=True)
        da_ref[1:2, :] = jnp.sum(ai_acc, axis=0, keepdims=True)

    return _pcall(body, name="scan_bwd", out_shape=(_sds(dx3.shape, F32), _sds((2, SSM_LANES), F32)), grid=(B, nc),
                  in_specs=[pl.BlockSpec((2, SSM_LANES), lambda b, c: (0, 0)), blk, blk],
                  out_specs=(blk, pl.BlockSpec((2, SSM_LANES), lambda b, c: (0, 0))),
                  scratch_shapes=[pltpu.VMEM((8, 8, SSM_LANES), F32), pltpu.VMEM((2, 8, SSM_LANES), F32),
                                  pltpu.VMEM((2, 8, SSM_LANES), F32)],
                  dims=("arbitrary", "arbitrary"))(abar, dx3, xs3)


US_BLOCK = (3 * ATT_WIDTH) // SSM_WIDTH


def _ssm_scan_fwd(proj3, abar, w_bu, w_c):
    B, S, _ = proj3.shape
    ch = min(S, SCAN_CHUNK)
    u_spec = pl.BlockSpec((1, ch, SSM_WIDTH), lambda b, c: (b, c, US_BLOCK))
    x_spec = pl.BlockSpec((1, ch, 2 * SSM_LANES), lambda b, c: (b, c, 0))
    y_spec = pl.BlockSpec((1, ch, SSM_WIDTH), lambda b, c: (b, c, 0))
    w_spec = pl.BlockSpec((SSM_WIDTH, 2 * SSM_LANES), lambda b, c: (0, 0))

    def body(ab_ref, u_ref, wb_ref, wc_ref, x_ref, y_ref, k_ref, carry_ref):
        _scan_consts(ab_ref[0:1, :], ab_ref[1:2, :], k_ref, False)

        @pl.when(pl.program_id(1) == 0)
        def _():
            carry_ref[...] = jnp.zeros_like(carry_ref)

        x_ref[0] = jnp.dot(u_ref[0], wb_ref[...], preferred_element_type=F32)

        def step(i, carry):
            base = pl.multiple_of(i * 8, 8)
            xr = x_ref[0, pl.ds(base, 8), 0:SSM_LANES]
            xi = x_ref[0, pl.ds(base, 8), SSM_LANES:]
            xr, xi = _scan_tile(xr, xi, k_ref, carry[0], carry[1], False)
            x_ref[0, pl.ds(base, 8), 0:SSM_LANES] = xr
            x_ref[0, pl.ds(base, 8), SSM_LANES:] = xi
            return (jnp.broadcast_to(xr[7:8], (8, SSM_LANES)), jnp.broadcast_to(xi[7:8], (8, SSM_LANES)))

        cr, ci = lax.fori_loop(0, ch // 8, step, (carry_ref[0], carry_ref[1]))
        carry_ref[0] = cr
        carry_ref[1] = ci
        y_ref[0] = lax.dot_general(x_ref[0].astype(BF16), wc_ref[...], NT_DIMS, preferred_element_type=F32)

    return _pcall(body, name="ssm_scan_fwd",
                  out_shape=(_sds((B, S, 2 * SSM_LANES), F32), _sds((B, S, SSM_WIDTH), F32)), grid=(B, S // ch),
                  in_specs=[pl.BlockSpec((2, SSM_LANES), lambda b, c: (0, 0)), u_spec, w_spec, w_spec],
                  out_specs=(x_spec, y_spec),
                  scratch_shapes=[pltpu.VMEM((8, 8, SSM_LANES), F32), pltpu.VMEM((2, 8, SSM_LANES), F32)],
                  dims=("arbitrary", "arbitrary"))(abar, proj3, w_bu, w_c)


def _ssm_scan_bwd(proj3, dy3, xs3, abar, w_bu, w_c, dsk):
    B, S, _ = proj3.shape
    ch = min(S, SCAN_CHUNK)
    nc = S // ch
    u_spec = pl.BlockSpec((1, ch, SSM_WIDTH), lambda b, c: (b, nc - 1 - c, US_BLOCK))
    x_spec = pl.BlockSpec((1, ch, 2 * SSM_LANES), lambda b, c: (b, nc - 1 - c, 0))
    y_spec = pl.BlockSpec((1, ch, SSM_WIDTH), lambda b, c: (b, nc - 1 - c, 0))
    w_spec = pl.BlockSpec((SSM_WIDTH, 2 * SSM_LANES), lambda b, c: (0, 0))
    ab_spec = pl.BlockSpec((2, SSM_LANES), lambda b, c: (0, 0))
    d_spec = pl.BlockSpec((1, SSM_WIDTH), lambda b, c: (0, 0))

    def body(ab_ref, u_ref, dy_ref, xs_ref, wb_ref, wc_ref, d_ref, du_ref, da_ref, dwb_ref, dwc_ref,
             g_ref, k_ref, carry_ref, acc_ref):
        b, c = pl.program_id(0), pl.program_id(1)
        _scan_consts(ab_ref[0:1, :], -ab_ref[1:2, :], k_ref, True)
        row = lax.broadcasted_iota(jnp.int32, (8, SSM_LANES), 0)

        @pl.when(c == 0)
        def _():
            carry_ref[...] = jnp.zeros_like(carry_ref)

        @pl.when((c == 0) & (b == 0))
        def _():
            acc_ref[...] = jnp.zeros_like(acc_ref)
            dwb_ref[...] = jnp.zeros_like(dwb_ref)
            dwc_ref[...] = jnp.zeros_like(dwc_ref)

        dy = dy_ref[0]
        dyb = dy.astype(BF16)
        g_ref[...] = jnp.dot(dyb, wc_ref[...], preferred_element_type=F32)

        def step(i, carry):
            car, cai, ar_acc, ai_acc = carry
            base = pl.multiple_of((ch // 8 - 1 - i) * 8, 8)
            gr = g_ref[pl.ds(base, 8), 0:SSM_LANES]
            gi = g_ref[pl.ds(base, 8), SSM_LANES:]
            gr, gi = _scan_tile(gr, gi, k_ref, car, cai, True)
            g_ref[pl.ds(base, 8), 0:SSM_LANES] = gr
            g_ref[pl.ds(base, 8), SSM_LANES:] = gi
            nr = jnp.where(row == 7, car, pltpu.roll(gr, 7, 0))
            ni = jnp.where(row == 7, cai, pltpu.roll(gi, 7, 0))
            xr = xs_ref[0, pl.ds(base, 8), 0:SSM_LANES]
            xi = xs_ref[0, pl.ds(base, 8), SSM_LANES:]
            ar_acc = ar_acc + nr * xr + ni * xi
            ai_acc = ai_acc + ni * xr - nr * xi
            return (jnp.broadcast_to(gr[0:1], (8, SSM_LANES)), jnp.broadcast_to(gi[0:1], (8, SSM_LANES)), ar_acc, ai_acc)

        cr, ci, ar_acc, ai_acc = lax.fori_loop(0, ch // 8, step, (carry_ref[0], carry_ref[1], acc_ref[0], acc_ref[1]))
        carry_ref[0] = cr
        carry_ref[1] = ci
        acc_ref[0] = ar_acc
        acc_ref[1] = ai_acc
        da_ref[0:1, :] = jnp.sum(ar_acc, axis=0, keepdims=True)
        da_ref[1:2, :] = jnp.sum(ai_acc, axis=0, keepdims=True)

        gb = g_ref[...].astype(BF16)
        du = lax.dot_general(gb, wb_ref[...], NT_DIMS, preferred_element_type=F32) + d_ref[...] * dy
        du_ref[0] = du.astype(BF16)
        dwb_ref[...] += lax.dot_general(u_ref[0], gb, TN_DIMS, preferred_element_type=F32)
        dwc_ref[...] += lax.dot_general(dyb, xs_ref[0].astype(BF16), TN_DIMS, preferred_element_type=F32)

    mat = _sds((SSM_WIDTH, 2 * SSM_LANES), F32)
    return _pcall(body, name="ssm_scan_bwd",
                  out_shape=(_sds((B, S, SSM_WIDTH), BF16), _sds((2, SSM_LANES), F32), mat, mat), grid=(B, nc),
                  in_specs=[ab_spec, u_spec, y_spec, x_spec, w_spec, w_spec, d_spec],
                  out_specs=(y_spec, ab_spec, w_spec, w_spec),
                  scratch_shapes=[pltpu.VMEM((ch, 2 * SSM_LANES), F32), pltpu.VMEM((8, 8, SSM_LANES), F32),
                                  pltpu.VMEM((2, 8, SSM_LANES), F32), pltpu.VMEM((2, 8, SSM_LANES), F32)],
                  dims=("arbitrary", "arbitrary"))(abar, proj3, dy3, xs3, w_bu, w_c, dsk)


GELU_K = math.sqrt(2.0 / math.pi)
GELU_C = 0.044715


def _gelu_parts(y):
    t = jnp.tanh(GELU_K * (y + GELU_C * y * y * y))
    return 0.5 * y * (1.0 + t), t


def _ssm_post(yc, us, dsk, wglu, bglu):
    T, N = yc.shape
    tm = min(T, 1024)
    row = pl.BlockSpec((tm, N), lambda i: (i, 0))
    vec = pl.BlockSpec((1, N), lambda i: (0, 0))
    mat = pl.BlockSpec((N, N), lambda i: (0, 0))

    def body(yc_ref, us_ref, d_ref, w_ref, b_ref, y_ref, s_ref):
        y = yc_ref[...] + d_ref[...] * us_ref[...]
        y_ref[...] = y
        z, _ = _gelu_parts(y)
        gl = jnp.dot(z.astype(BF16), w_ref[...], preferred_element_type=F32) + b_ref[...]
        s_ref[...] = (z * _sig(gl)).astype(BF16)

    return _pcall(body, name="ssm_post", out_shape=(_sds((T, N), F32), _sds((T, N), BF16)), grid=(T // tm,),
                  in_specs=[row, row, vec, mat, vec], out_specs=(row, row), dims=("parallel",))(yc, us, dsk, wglu, bglu)


def _ssm_post_bwd(y5, us, ds, dsk, wglu, bglu):
    T, N = y5.shape
    tm = min(T, 1024)
    row = pl.BlockSpec((tm, N), lambda i: (i, 0))
    vec = pl.BlockSpec((1, N), lambda i: (0, 0))
    mat = pl.BlockSpec((N, N), lambda i: (0, 0))

    def body(y_ref, us_ref, ds_ref, d_ref, w_ref, b_ref, dy_ref, dd_ref, db_ref, dw_ref):
        @pl.when(pl.program_id(0) == 0)
        def _():
            dd_ref[...] = jnp.zeros_like(dd_ref)
            db_ref[...] = jnp.zeros_like(db_ref)
            dw_ref[...] = jnp.zeros_like(dw_ref)

        y = y_ref[...]
        z, t = _gelu_parts(y)
        zb = z.astype(BF16)
        gl = jnp.dot(zb, w_ref[...], preferred_element_type=F32) + b_ref[...]
        sg = _sig(gl)
        ds = ds_ref[...]
        dgl = ds * z * sg * (1.0 - sg)
        dglb = dgl.astype(BF16)
        dz = ds * sg + lax.dot_general(dglb, w_ref[...], (((1,), (1,)), ((), ())), preferred_element_type=F32)
        dgelu = 0.5 * (1.0 + t) + 0.5 * y * (1.0 - t * t) * GELU_K * (1.0 + 3.0 * GELU_C * y * y)
        dy = dz * dgelu
        dy_ref[...] = dy
        dd_ref[...] += jnp.sum(dy * us_ref[...], axis=0, keepdims=True)
        db_ref[...] += jnp.sum(dgl, axis=0, keepdims=True)
        dw_ref[...] += lax.dot_general(zb, dglb, (((0,), (0,)), ((), ())), preferred_element_type=F32)

    return _pcall(body, name="ssm_post_bwd",
                  out_shape=(_sds((T, N), F32), _sds((1, N), F32), _sds((1, N), F32), _sds((N, N), F32)),
                  grid=(T // tm,), in_specs=[row, row, row, vec, mat, vec], out_specs=(row, vec, vec, mat),
                  dims=("arbitrary",))(y5, us, ds, dsk, wglu, bglu)


def _add_scaled_cast(a, b, s):
    T, N = a.shape
    tm = min(T, 1024)
    row = pl.BlockSpec((tm, N), lambda i: (i, 0))

    def body(a_ref, b_ref, s_ref, o_ref):
        o_ref[...] = (a_ref[...] + s_ref[...] * b_ref[...]).astype(BF16)

    return _pcall(body, name="add_scaled_cast", out_shape=_sds((T, N), BF16), grid=(T // tm,),
                  in_specs=[row, row, pl.BlockSpec((1, N), lambda i: (0, 0))], out_specs=row, dims=("parallel",))(a, b, s)


GATE_TILE = 256
GATE_ATT_BLOCK0 = (3 * ATT_WIDTH + SSM_WIDTH) // GATE_TILE
GATE_SSM_BLOCK0 = (3 * ATT_WIDTH + SSM_WIDTH + D_MODEL) // GATE_TILE


def _merge(proj, y_att, y_ssm, b_gate):
    T = proj.shape[0]
    tm = min(T, 1024)
    nj = D_MODEL // GATE_TILE
    ga = pl.BlockSpec((tm, GATE_TILE), lambda i, j: (i, GATE_ATT_BLOCK0 + j))
    gs = pl.BlockSpec((tm, GATE_TILE), lambda i, j: (i, GATE_SSM_BLOCK0 + j))
    yy = pl.BlockSpec((tm, GATE_TILE), lambda i, j: (i, j))
    ba = pl.BlockSpec((1, GATE_TILE), lambda i, j: (0, j))
    bs = pl.BlockSpec((1, GATE_TILE), lambda i, j: (0, nj + j))

    def body(ga_ref, gs_ref, ya_ref, ys_ref, ba_ref, bs_ref, o_ref):
        o_ref[...] = (_sig(ga_ref[...] + ba_ref[...]) * ya_ref[...]
                      + _sig(gs_ref[...] + bs_ref[...]) * ys_ref[...]).astype(BF16)

    return _pcall(body, name="merge", out_shape=_sds((T, D_MODEL), BF16), grid=(T // tm, nj),
                  in_specs=[ga, gs, yy, yy, ba, bs], out_specs=yy, dims=("parallel", "parallel"))(
        proj, proj, y_att, y_ssm, b_gate, b_gate)


def _merge_bwd(proj, y_att, y_ssm, b_gate, dmerged):
    T = proj.shape[0]
    tm = min(T, 1024)
    nj = D_MODEL // GATE_TILE
    ga = pl.BlockSpec((tm, GATE_TILE), lambda j, i: (i, GATE_ATT_BLOCK0 + j))
    gs = pl.BlockSpec((tm, GATE_TILE), lambda j, i: (i, GATE_SSM_BLOCK0 + j))
    yy = pl.BlockSpec((tm, GATE_TILE), lambda j, i: (i, j))
    ba = pl.BlockSpec((1, GATE_TILE), lambda j, i: (0, j))
    bs = pl.BlockSpec((1, GATE_TILE), lambda j, i: (0, nj + j))

    def body(ga_ref, gs_ref, ya_ref, ys_ref, ba_ref, bs_ref, dm_ref, dya_ref, dys_ref, dga_ref, dgs_ref, dba_ref, dbs_ref):
        @pl.when(pl.program_id(1) == 0)
        def _():
            dba_ref[...] = jnp.zeros_like(dba_ref)
            dbs_ref[...] = jnp.zeros_like(dbs_ref)

        dm = dm_ref[...].astype(F32)
        sa = _sig(ga_ref[...] + ba_ref[...])
        ss = _sig(gs_ref[...] + bs_ref[...])
        dya_ref[...] = (dm * sa).astype(BF16)
        dys_ref[...] = (dm * ss).astype(BF16)
        dga = dm * ya_ref[...] * sa * (1.0 - sa)
        dgs = dm * ys_ref[...] * ss * (1.0 - ss)
        dga_ref[...] = dga.astype(BF16)
        dgs_ref[...] = dgs.astype(BF16)
        dba_ref[...] += jnp.sum(dga, axis=0, keepdims=True)
        dbs_ref[...] += jnp.sum(dgs, axis=0, keepdims=True)

    big = _sds((T, D_MODEL), BF16)
    vec = _sds((1, D_MODEL), F32)
    return _pcall(body, name="merge_bwd", out_shape=(big, big, big, big, vec, vec), grid=(nj, T // tm),
                  in_specs=[ga, gs, yy, yy, ba, bs, yy], out_specs=(yy, yy, yy, yy, ba, ba),
                  dims=("arbitrary", "arbitrary"))(proj, proj, y_att, y_ssm, b_gate, b_gate, dmerged)


CONV_TILE = 256


def _conv_pre(a, w_ref, b_ref, row):
    conv = b_ref[...] + w_ref[0:1, :] * a
    shifted = []
    for j in (1, 2):
        sh = jnp.where(row >= j, pltpu.roll(a, j, 0), 0.0)
        shifted.append(sh)
        conv = conv + w_ref[j:j + 1, :] * sh
    return conv, shifted


def _conv_act(up3, w_conv, b_conv):
    B, S, _ = up3.shape
    nj = D_FF // CONV_TILE
    a_spec = pl.BlockSpec((1, S, CONV_TILE), lambda b, j: (b, 0, j))
    v_spec = pl.BlockSpec((1, S, CONV_TILE), lambda b, j: (b, 0, nj + j))
    w_spec = pl.BlockSpec((3, CONV_TILE), lambda b, j: (0, j))
    b_spec = pl.BlockSpec((1, CONV_TILE), lambda b, j: (0, j))

    def body(a_ref, v_ref, w_ref, b_ref, o_ref):
        a = a_ref[0].astype(F32)
        row = lax.broadcasted_iota(jnp.int32, a.shape, 0)
        conv, _ = _conv_pre(a, w_ref, b_ref, row)
        o_ref[0] = (conv * _sig(conv) * v_ref[0]).astype(BF16)

    return _pcall(body, name="conv_act", out_shape=_sds((B, S, D_FF), BF16), grid=(B, nj),
                  in_specs=[a_spec, v_spec, w_spec, b_spec], out_specs=a_spec, dims=("parallel", "parallel"))(
        up3, up3, w_conv, b_conv)


def _conv_bwd(up3, dact3, w_conv, b_conv):
    B, S, _ = up3.shape
    nj = D_FF // CONV_TILE
    a_spec = pl.BlockSpec((1, S, CONV_TILE), lambda j, b, e: (b, 0, j))
    v_spec = pl.BlockSpec((1, S, CONV_TILE), lambda j, b, e: (b, 0, nj + j))
    o_spec = pl.BlockSpec((1, S, CONV_TILE), lambda j, b, e: (b, 0, e * nj + j))
    w_spec = pl.BlockSpec((3, CONV_TILE), lambda j, b, e: (0, j))
    b_spec = pl.BlockSpec((1, CONV_TILE), lambda j, b, e: (0, j))

    def body(a_ref, v_ref, d_ref, w_ref, b_ref, dup_ref, dw_ref, db_ref, dv_keep):
        e = pl.program_id(2)

        @pl.when((pl.program_id(1) == 0) & (e == 0))
        def _():
            dw_ref[...] = jnp.zeros_like(dw_ref)
            db_ref[...] = jnp.zeros_like(db_ref)

        @pl.when(e == 0)
        def _():
            a = a_ref[0].astype(F32)
            d = d_ref[0].astype(F32)
            row = lax.broadcasted_iota(jnp.int32, a.shape, 0)
            conv, shifted = _conv_pre(a, w_ref, b_ref, row)
            sg = _sig(conv)
            dv_keep[...] = (d * conv * sg).astype(BF16)
            dconv = d * v_ref[0] * (sg * (1.0 + conv * (1.0 - sg)))
            da = w_ref[0:1, :] * dconv
            for j in (1, 2):
                da = da + w_ref[j:j + 1, :] * jnp.where(row < S - j, pltpu.roll(dconv, S - j, 0), 0.0)
            dup_ref[0] = da.astype(BF16)
            db_ref[...] += jnp.sum(dconv, axis=0, keepdims=True)
            dw_ref[0:1, :] += jnp.sum(dconv * a, axis=0, keepdims=True)
            dw_ref[1:2, :] += jnp.sum(dconv * shifted[0], axis=0, keepdims=True)
            dw_ref[2:3, :] += jnp.sum(dconv * shifted[1], axis=0, keepdims=True)

        @pl.when(e == 1)
        def _():
            dup_ref[0] = dv_keep[...]

    return _pcall(body, name="conv_bwd",
                  out_shape=(_sds((B, S, 2 * D_FF), BF16), _sds((3, D_FF), F32), _sds((1, D_FF), F32)),
                  grid=(nj, B, 2), in_specs=[a_spec, v_spec, a_spec, w_spec, b_spec],
                  out_specs=(o_spec, w_spec, b_spec), scratch_shapes=[pltpu.VMEM((S, CONV_TILE), BF16)],
                  dims=("arbitrary", "arbitrary", "arbitrary"))(up3, up3, dact3, w_conv, b_conv)


def _rows_tile(r, cap=640):
    for t in range(min(r, cap) - min(r, cap) % 8, 7, -8):
        if r % t == 0:
            return t
    return r


def _add2(a, b, out_dtype):
    R, N = a.shape
    tr = _rows_tile(R)
    spec = pl.BlockSpec((tr, N), lambda i: (i, 0))

    def body(a_ref, b_ref, o_ref):
        o_ref[...] = (a_ref[...] + b_ref[...]).astype(out_dtype)

    return _pcall(body, name="add2", out_shape=_sds((R, N), out_dtype), grid=(R // tr,), in_specs=[spec, spec],
                  out_specs=spec, dims=("parallel",))(a, b)


def _sum_slots(q, name):
    n, R, N = q.shape
    tr = _rows_tile(R)

    def body(q_ref, o_ref):
        acc = q_ref[0].astype(F32)
        for s in range(1, n):
            acc = acc + q_ref[s].astype(F32)
        o_ref[...] = acc

    return _pcall(body, name=name, out_shape=_sds((R, N), F32), grid=(R // tr,),
                  in_specs=[pl.BlockSpec((n, tr, N), lambda i: (0, i, 0))], out_specs=pl.BlockSpec((tr, N), lambda i: (i, 0)),
                  dims=("parallel",))(q)


def _adamw(w, g, m, v, name):
    R, N = w.shape
    tr = _rows_tile(R) if R * N * 4 > (1 << 20) else R
    tr = min(tr, 256) if R % 256 == 0 and R > 256 else tr
    spec = pl.BlockSpec((tr, N), lambda i: (i, 0))
    bc1 = 1.0 - ADAM_B1 ** ADAM_STEP
    bc2 = 1.0 - ADAM_B2 ** ADAM_STEP

    def body(w_ref, g_ref, m_ref, v_ref, d_ref, nm_ref, nv_ref):
        g = g_ref[...]
        m = ADAM_B1 * m_ref[...] + (1.0 - ADAM_B1) * g
        v = ADAM_B2 * v_ref[...] + (1.0 - ADAM_B2) * (g * g)
        nm_ref[...] = m
        nv_ref[...] = v
        d_ref[...] = -ADAM_LR * ((m / bc1) / (jnp.sqrt(v / bc2) + ADAM_EPS) + ADAM_WD * w_ref[...])

    shp = _sds((R, N), F32)
    return _pcall(body, name=name, out_shape=(shp, shp, shp), grid=(R // tr,), in_specs=[spec] * 4,
                  out_specs=(spec, spec, spec), dims=("parallel",))(w, g, m, v)


_GROUP_MASKS = {
    "all": [(dx, dy, dc) for dx in (0, 1) for dy in (0, 1) for dc in (0, 1) if (dx, dy, dc) != (0, 0, 0)],
    "xy": [(1, 0, 0), (0, 1, 0), (1, 1, 0)],
    "c": [(0, 0, 1)],
}
_GROUP_SLOTS = {"all": 8, "xy": 4, "c": 2}


def _group_slot(group, x, y, c):
    return {"all": 4 * x + 2 * y + c, "xy": 2 * x + y, "c": c}[group]


def _flip(v, d):
    return 1 - v if d else v


def _exchange(arr, group, mode, name):
    return _exchange_list([arr], group, mode, name)[0]


def _exchange_list(arrs, group, mode, name):
    masks = _GROUP_MASKS[group]
    n = len(masks)
    na = len(arrs)
    out_shapes, halves, bounce = [], [], []
    for arr in arrs:
        if mode == "gather":
            out_shapes.append((_GROUP_SLOTS[group],) + arr.shape)
            bounce.append(pltpu.VMEM(arr.shape, arr.dtype))
        elif mode == "scatter":
            assert arr.shape[0] == _GROUP_SLOTS[group]
            out_shapes.append(arr.shape)
            bounce.append(pltpu.VMEM(arr.shape[1:], arr.dtype))
        elif mode == "swap":
            assert group == "c"
            out_shapes.append(arr.shape)
        else:
            assert group == "c"
            halves.append(arr.shape[1] // 2)
            out_shapes.append((arr.shape[0], arr.shape[1] // 2, arr.shape[2]))
    has_local = mode in ("gather", "scatter")

    def body(*refs):
        x_refs, o_refs = refs[:na], refs[na:2 * na]
        send_sems, recv_sems = refs[2 * na], refs[2 * na + 1]
        x, y, c = lax.axis_index("x"), lax.axis_index("y"), lax.axis_index("c")
        me = _group_slot(group, x, y, c)
        if has_local:
            local_sems = refs[2 * na + 2]
            bufs = refs[2 * na + 3:]
            loads = []
            for i in range(na):
                src = x_refs[i] if mode == "gather" else x_refs[i].at[me]
                loads.append(pltpu.make_async_copy(src, bufs[i], local_sems.at[2 * i]))
                loads[-1].start()
        copies = []
        for i in range(na):
            x_ref, o_ref = x_refs[i], o_refs[i]
            for k, (dx, dy, dc) in enumerate(masks):
                px, py, pc = _flip(x, dx), _flip(y, dy), _flip(c, dc)
                if mode == "gather":
                    src, dst = x_ref, o_ref.at[me]
                elif mode == "scatter":
                    src, dst = x_ref.at[_group_slot(group, px, py, pc)], o_ref.at[me]
                elif mode == "swap":
                    src, dst = x_ref, o_ref
                else:
                    src, dst = x_ref.at[:, pl.ds(pl.multiple_of(pc * halves[i], 8), halves[i]), :], o_ref
                cp = pltpu.make_async_remote_copy(src_ref=src, dst_ref=dst, send_sem=send_sems.at[i * n + k],
                                                  recv_sem=recv_sems.at[i * n + k], device_id=(px, py, pc),
                                                  device_id_type=pl.DeviceIdType.MESH)
                cp.start()
                copies.append(cp)
        if has_local:
            stores = []
            for i in range(na):
                loads[i].wait()
                stores.append(pltpu.make_async_copy(bufs[i], o_refs[i].at[me], local_sems.at[2 * i + 1]))
                stores[-1].start()
        for cp in copies:
            cp.wait()
        if has_local:
            for st in stores:
                st.wait()

    anyspec = pl.BlockSpec(memory_space=pl.ANY)
    scratch = [pltpu.SemaphoreType.DMA((n * na,)), pltpu.SemaphoreType.DMA((n * na,))]
    if has_local:
        scratch += [pltpu.SemaphoreType.DMA((2 * na,))] + bounce
    outs = pl.pallas_call(body, name=name, out_shape=tuple(_sds(s, a.dtype) for s, a in zip(out_shapes, arrs)),
                          in_specs=[anyspec] * na, out_specs=tuple([anyspec] * na), scratch_shapes=scratch,
                          compiler_params=pltpu.CompilerParams(vmem_limit_bytes=V7X_VMEM_LIMIT_BYTES))(*arrs)
    return list(outs)


def _gather_weights(shards, name):
    na = len(shards)
    masks = _GROUP_MASKS["xy"]
    n = len(masks)

    def body(*refs):
        x_refs, o_refs = refs[:na], refs[na:2 * na]
        send_sems, recv_sems, local_sems = refs[2 * na:2 * na + 3]
        bufs = refs[2 * na + 3:]
        x, y, c = lax.axis_index("x"), lax.axis_index("y"), lax.axis_index("c")
        me = 2 * x + y
        sibling = (x, y, 1 - c)
        loads = []
        for i in range(na):
            loads.append(pltpu.make_async_copy(x_refs[i], bufs[i], local_sems.at[2 * i]))
            loads[-1].start()

        def half_of(i, slot, cc):
            h = shards[i].shape[0] // 2
            return o_refs[i].at[slot, pl.ds(pl.multiple_of(cc * h, 8), h), :]

        def src_half(i, cc):
            h = shards[i].shape[0] // 2
            return x_refs[i].at[pl.ds(pl.multiple_of(cc * h, 8), h), :]

        sends = []
        for i in range(na):
            for k, (dx, dy, _) in enumerate(masks):
                cp = pltpu.make_async_remote_copy(src_ref=src_half(i, c), dst_ref=half_of(i, me, c),
                                                  send_sem=send_sems.at[i * 2 * n + k], recv_sem=recv_sems.at[i * 2 * n + k],
                                                  device_id=(_flip(x, dx), _flip(y, dy), c),
                                                  device_id_type=pl.DeviceIdType.MESH)
                cp.start()
                sends.append(cp)
        stores = []
        for i in range(na):
            loads[i].wait()
            stores.append(pltpu.make_async_copy(bufs[i], o_refs[i].at[me], local_sems.at[2 * i + 1]))
            stores[-1].start()
        for i in range(na):
            for k, (dx, dy, _) in enumerate(masks):
                slot = 2 * _flip(x, dx) + _flip(y, dy)
                landed = pltpu.make_async_remote_copy(src_ref=src_half(i, c), dst_ref=half_of(i, slot, c),
                                                      send_sem=send_sems.at[i * 2 * n + k],
                                                      recv_sem=recv_sems.at[i * 2 * n + k], device_id=sibling,
                                                      device_id_type=pl.DeviceIdType.MESH)
                landed.wait_recv()
                fwd = pltpu.make_async_remote_copy(src_ref=half_of(i, slot, c), dst_ref=half_of(i, slot, c),
                                                   send_sem=send_sems.at[i * 2 * n + n + k],
                                                   recv_sem=recv_sems.at[i * 2 * n + n + k], device_id=sibling,
                                                   device_id_type=pl.DeviceIdType.MESH)
                fwd.start()
                sends.append(fwd)
        for i in range(na):
            for k, (dx, dy, _) in enumerate(masks):
                slot = 2 * _flip(x, dx) + _flip(y, dy)
                pltpu.make_async_remote_copy(src_ref=half_of(i, slot, 1 - c), dst_ref=half_of(i, slot, 1 - c),
                                             send_sem=send_sems.at[i * 2 * n + n + k],
                                             recv_sem=recv_sems.at[i * 2 * n + n + k], device_id=sibling,
                                             device_id_type=pl.DeviceIdType.MESH).wait_recv()
        for cp in sends:
            cp.wait_send()
        for st in stores:
            st.wait()

    anyspec = pl.BlockSpec(memory_space=pl.ANY)
    scratch = [pltpu.SemaphoreType.DMA((2 * n * na,)), pltpu.SemaphoreType.DMA((2 * n * na,)),
               pltpu.SemaphoreType.DMA((2 * na,))] + [pltpu.VMEM(s.shape, s.dtype) for s in shards]
    outs = pl.pallas_call(body, name=name, out_shape=tuple(_sds((N_XY,) + s.shape, s.dtype) for s in shards),
                          in_specs=[anyspec] * na, out_specs=tuple([anyspec] * na), scratch_shapes=scratch,
                          compiler_params=pltpu.CompilerParams(vmem_limit_bytes=V7X_VMEM_LIMIT_BYTES))(*shards)
    return list(outs)


def _pair_add(g, theirs, core, name):
    n4, h2, w = g.shape
    h = h2 // 2
    tr = _rows_tile(h)
    nb = h // tr

    def body(c_ref, g_ref, t_ref, o_ref):
        o_ref[...] = (g_ref[...] + t_ref[...]).astype(BF16)

    grid_spec = pltpu.PrefetchScalarGridSpec(
        num_scalar_prefetch=1, grid=(n4, nb),
        in_specs=[pl.BlockSpec((None, tr, w), lambda j, i, c_ref: (j, c_ref[0] * nb + i, 0)),
                  pl.BlockSpec((None, tr, w), lambda j, i, c_ref: (j, i, 0))],
        out_specs=pl.BlockSpec((None, tr, w), lambda j, i, c_ref: (j, i, 0)))
    return pl.pallas_call(body, name=name, out_shape=_sds((n4, h, w), BF16), grid_spec=grid_spec,
                          compiler_params=pltpu.CompilerParams(vmem_limit_bytes=V7X_VMEM_LIMIT_BYTES,
                                                               dimension_semantics=("parallel", "parallel")))(core, g, theirs)


BIG = (("w_proj_att", (ATT_WIDTH, D_MODEL), 1), ("w_proj_ssm", (SSM_WIDTH, D_MODEL), 1),
       ("w_glu", (SSM_WIDTH, SSM_WIDTH), 0))
DIRECT = (("w_in", True), ("w_up", True), ("w_down", False), ("w_out", False))
N_XY = 4


def _big_rows(shape):
    return shape[0] * shape[1] // N_XY // LANES


FLAT_ROWS = sum(_big_rows(s) for _, s, _ in BIG)


def _shard_shape(shape, axis):
    return (shape[0] // N_XY, shape[1]) if axis == 0 else (shape[0], shape[1] // N_XY)


def _flatten_shards(shards):
    return jnp.concatenate([shards[n].reshape(_big_rows(s), LANES) for n, s, _ in BIG], axis=0)


def _unflatten_shard(flat):
    out, r = {}, 0
    for n, s, ax in BIG:
        k = _big_rows(s)
        out[n] = flat[r:r + k].reshape(_shard_shape(s, ax))
        r += k
    return out


def _unflatten_full(flat4):
    out, r = {}, 0
    for n, s, ax in BIG:
        k = _big_rows(s)
        sh = _shard_shape(s, ax)
        t = flat4[:, r:r + k].reshape((N_XY,) + sh)
        out[n] = t.reshape(s) if ax == 0 else t.transpose(1, 0, 2).reshape(s)
        r += k
    return out


def _flatten_full(full):
    parts = []
    for n, s, ax in BIG:
        sh = _shard_shape(s, ax)
        t = full[n]
        t = t.reshape((N_XY,) + sh) if ax == 0 else t.reshape(s[0], N_XY, sh[1]).transpose(1, 0, 2)
        parts.append(t.reshape(N_XY, _big_rows(s), LANES))
    return jnp.concatenate(parts, axis=1)


def _pack_rows(arrs):
    rows, counts = [], []
    for a in arrs:
        f = a.reshape(-1)
        k = -(-f.shape[0] // LANES)
        rows.append(jnp.pad(f, (0, k * LANES - f.shape[0])).reshape(k, LANES))
        counts.append(k)
    return jnp.concatenate(rows, axis=0), counts


def _unpack_rows(buf, shapes):
    out, r = [], 0
    for s in shapes:
        size = int(np.prod(s))
        k = -(-size // LANES)
        out.append(buf[r:r + k].reshape(-1)[:size].reshape(s))
        r += k
    return out


def _lanes_from_groups(a):
    return a.transpose(2, 0, 1).reshape(SSM_GROUP_CH, SSM_LANES)


def _groups_from_lanes(a):
    return a.reshape(SSM_GROUP_CH, SSM_GROUPS, SSM_STATE).transpose(1, 2, 0)


def _local_step(x3, mod, tgt3, W, P):
    B, S, _ = x3.shape
    T = B * S
    seq_blocks = S // ATT_BLOCK
    sh1, sc1, gt1, sh2, sc2, gt2 = [m.reshape(B, 1, D_MODEL) for m in jnp.split(mod, 6, axis=-1)]
    g_mix, g_ffn, g_final = P["g_mix"].reshape(1, D_MODEL), P["g_ffn"].reshape(1, D_MODEL), P["g_final"].reshape(1, D_MODEL)
    b_gate = P["b_gate"].reshape(1, 2 * D_MODEL)
    d_skip, b_glu = P["d_skip"].reshape(1, SSM_WIDTH), P["b_glu"].reshape(1, SSM_WIDTH)
    w_conv, b_conv = P["w_conv"], P["b_conv"].reshape(1, D_FF)

    u1 = _norm_mod(x3, g_mix, sc1, sh1).reshape(T, D_MODEL)
    proj = _mm(u1, W["w_in_t"], tb=True, name="mm_proj", out_dtype=BF16)
    proj3 = proj.reshape(B, S, IN_WIDTH)
    us = proj[:, 3 * ATT_WIDTH:3 * ATT_WIDTH + SSM_WIDTH]
    o_att3, lse4 = _attention_fwd(proj3, seq_blocks)
    o_att = o_att3.reshape(T, ATT_WIDTH)
    y_att = _mm(o_att, W["w_proj_att"], name="mm_proj_att", out_dtype=BF16)

    lr = P["a_re"].reshape(1, SSM_LANES)
    li = P["a_im"].reshape(1, SSM_LANES)
    ldt = jnp.repeat(P["log_dt"], SSM_STATE).reshape(1, SSM_LANES)
    br, bi = _lanes_from_groups(P["b_re"]), _lanes_from_groups(P["b_im"])
    cr = P["c_re"].transpose(1, 0, 2).reshape(SSM_GROUP_CH, SSM_LANES)
    ci = P["c_im"].transpose(1, 0, 2).reshape(SSM_GROUP_CH, SSM_LANES)
    abar, w_bu, w_c = _ssm_params(lr, li, ldt, br, bi, cr, ci)
    xs3, y_core3 = _ssm_scan_fwd(proj3, abar, w_bu, w_c)
    y5, s_out = _ssm_post(y_core3.reshape(T, SSM_WIDTH), us, d_skip, W["w_glu"], b_glu)
    y_ssm = _mm(s_out, W["w_proj_ssm"], name="mm_proj_ssm", out_dtype=BF16)

    merged = _merge(proj, y_att, y_ssm, b_gate)
    mix = _mm(merged, W["w_out"], name="mm_out", out_dtype=BF16)
    mix3 = mix.reshape(B, S, D_MODEL)

    h1, u2 = _resid_norm_mod(x3, mix3, gt1, g_ffn, sc2, sh2)
    u2 = u2.reshape(T, D_MODEL)
    up3 = _mm(u2, W["w_up_t"], tb=True, name="mm_up", out_dtype=BF16).reshape(B, S, 2 * D_FF)
    act = _conv_act(up3, w_conv, b_conv).reshape(T, D_FF)
    ffn3 = _mm(act, W["w_down"], name="mm_down", out_dtype=BF16).reshape(B, S, D_MODEL)
    dh2, dffn, dgt2, dg_final, loss = _final_loss(h1, ffn3, tgt3, gt2, g_final)

    dffn = dffn.reshape(T, D_MODEL)
    gw = {}
    gw["w_down"] = _mm(act, dffn, ta=True, name="mm_dw_down")
    dact3 = _mm(dffn, W["w_down"], tb=True, name="mm_dact", out_dtype=BF16).reshape(B, S, D_FF)
    dup3, dw_conv, db_conv = _conv_bwd(up3, dact3, w_conv, b_conv)
    dup = dup3.reshape(T, 2 * D_FF)
    gw["w_up_t"] = _mm(dup, u2, ta=True, name="mm_dw_up")
    du2 = _mm(dup, W["w_up_t"], name="mm_du2", out_dtype=BF16).reshape(B, S, D_MODEL)
    dh1, dsh2, dsc2, dg_ffn, dgt1, dmix = _norm_bwd(h1, du2, dh2, g_ffn, sc2, "norm_bwd2", mix3=mix3, gt=gt1)

    dmix = dmix.reshape(T, D_MODEL)
    gw["w_out"] = _mm(merged, dmix, ta=True, name="mm_dw_out")
    dmerged = _mm(dmix, W["w_out"], tb=True, name="mm_dmerged", out_dtype=BF16)
    dy_att, dy_ssm, dga, dgs, db_att, db_ssm = _merge_bwd(proj, y_att, y_ssm, b_gate, dmerged)

    gw["w_proj_ssm"] = _mm(s_out, dy_ssm, ta=True, name="mm_dw_proj_ssm")
    ds_out = _mm(dy_ssm, W["w_proj_ssm"], tb=True, name="mm_ds_out")
    dy5, dd_skip, db_glu, dw_glu = _ssm_post_bwd(y5, us, ds_out, d_skip, W["w_glu"], b_glu)
    gw["w_glu"] = dw_glu
    dus3, dab, dwbu, dwc = _ssm_scan_bwd(proj3, dy5.reshape(B, S, SSM_WIDTH), xs3, abar, w_bu, w_c, d_skip)
    dus = dus3.reshape(T, SSM_WIDTH)
    dlr, dli, dldt, dbr, dbi, dcr, dci = _ssm_params_bwd(lr, li, ldt, br, bi, dab, dwbu, dwc)

    gw["w_proj_att"] = _mm(o_att, dy_att, ta=True, name="mm_dw_proj_att")
    do_att = _mm(dy_att, W["w_proj_att"], tb=True, out_dtype=BF16, name="mm_do_att")
    dq3, dk3, dv3 = _attention_bwd(proj3, do_att.reshape(B, S, ATT_WIDTH), o_att3, lse4, seq_blocks)
    dproj = jnp.concatenate([t.reshape(T, ATT_WIDTH) for t in (dq3, dk3, dv3)] + [dus, dga, dgs], axis=1)
    gw["w_in_t"] = _mm(dproj, u1, ta=True, name="mm_dw_in")
    du1 = _mm(dproj, W["w_in_t"], name="mm_du1", out_dtype=BF16).reshape(B, S, D_MODEL)
    dx, dsh1, dsc1, dg_mix = _norm_bwd(x3, du1, dh1, g_mix, sc1, "norm_bwd1")

    dmod = jnp.concatenate([t.reshape(B, D_MODEL) for t in (dsh1, dsc1, dgt1, dsh2, dsc2, dgt2)], axis=1)
    gs = dict(
        g_mix=dg_mix.reshape(D_MODEL), b_gate=jnp.concatenate([db_att, db_ssm], axis=1).reshape(2 * D_MODEL),
        a_re=dlr.reshape(SSM_GROUPS, SSM_STATE), a_im=dli.reshape(SSM_GROUPS, SSM_STATE), log_dt=dldt[0, :SSM_GROUPS],
        b_re=_groups_from_lanes(dbr), b_im=_groups_from_lanes(dbi),
        c_re=dcr.reshape(SSM_GROUP_CH, SSM_GROUPS, SSM_STATE).transpose(1, 0, 2),
        c_im=dci.reshape(SSM_GROUP_CH, SSM_GROUPS, SSM_STATE).transpose(1, 0, 2),
        d_skip=dd_skip.reshape(SSM_WIDTH), b_glu=db_glu.reshape(SSM_WIDTH), g_ffn=dg_ffn.reshape(D_MODEL),
        w_conv=dw_conv, b_conv=db_conv.reshape(D_FF), g_final=dg_final.reshape(D_MODEL))
    return loss, dx, dmod, gw, gs


WEIGHTS = ['w_ada', 'b_ada', 'g_mix', 'w_in', 'b_gate', 'a_re', 'a_im', 'log_dt', 'b_re', 'b_im', 'c_re', 'c_im', 'd_skip',
           'w_glu', 'b_glu', 'w_proj_att', 'w_proj_ssm', 'w_out', 'g_ffn', 'w_up', 'w_conv', 'b_conv', 'w_down', 'g_final']
SMALL = ['g_mix', 'b_gate', 'a_re', 'a_im', 'log_dt', 'b_re', 'b_im', 'c_re', 'c_im', 'd_skip', 'b_glu', 'g_ffn', 'w_conv',
         'b_conv', 'g_final']


def kernel(x, c, w_ada, b_ada, g_mix, w_in, b_gate, a_re, a_im, log_dt, b_re, b_im, c_re, c_im, d_skip, w_glu, b_glu, w_proj_att, w_proj_ssm, w_out, g_ffn, w_up, w_conv, b_conv, w_down, g_final, loss_target, m_w_ada, m_b_ada, m_g_mix, m_w_in, m_b_gate, m_a_re, m_a_im, m_log_dt, m_b_re, m_b_im, m_c_re, m_c_im, m_d_skip, m_w_glu, m_b_glu, m_w_proj_att, m_w_proj_ssm, m_w_out, m_g_ffn, m_w_up, m_w_conv, m_b_conv, m_w_down, m_g_final, v_w_ada, v_b_ada, v_g_mix, v_w_in, v_b_gate, v_a_re, v_a_im, v_log_dt, v_b_re, v_b_im, v_c_re, v_c_im, v_d_skip, v_w_glu, v_b_glu, v_w_proj_att, v_w_proj_ssm, v_w_out, v_g_ffn, v_w_up, v_w_conv, v_b_conv, v_w_down, v_g_final):
    args = dict(locals())
    w = {n: args[n] for n in WEIGHTS}
    m = {n: args["m_" + n] for n in WEIGHTS}
    v = {n: args["v_" + n] for n in WEIGHTS}
    B, S, _ = x.shape
    ix, iy, ic = lax.axis_index("x"), lax.axis_index("y"), lax.axis_index("c")
    chip = 2 * ix + iy
    half = FLAT_ROWS // 2
    ada_cols = w_ada.shape[2]

    c_all = _exchange(c, "all", "gather", "gather_c").reshape(8 * B, D_MODEL)
    b_cols = lax.dynamic_slice_in_dim(b_ada, chip * ada_cols, ada_cols, axis=1)
    mod_cols = _ada_fwd(c_all, w_ada[0], b_cols)
    mod_all = _exchange(mod_cols, "xy", "gather", "gather_mod")
    mod_all = mod_all.transpose(1, 0, 2).reshape(8 * B, 6 * D_MODEL)
    mod = lax.dynamic_slice_in_dim(mod_all, (4 * ix + 2 * iy + ic) * B, B, axis=0)

    south = ic == 0
    core = ic.astype(jnp.int32).reshape(1)
    shards = [(w[n][0].T if t else w[n][0]).astype(BF16) for n, t in DIRECT]
    shards.append(_flatten_shards({n: w[n][0] for n, _, _ in BIG}).astype(BF16))
    full = _gather_weights(shards, "gather_weights")
    W = {n + ("_t" if t else ""): f.reshape(-1, LANES) for (n, t), f in zip(DIRECT, full)}
    W.update(_unflatten_full(full[-1]))

    wc_all = _exchange(w_conv[0], "xy", "gather", "gather_w_conv")
    P = {n: w[n][0] for n in SMALL if n not in ("w_conv", "g_final")}
    P["w_conv"] = wc_all.transpose(1, 0, 2).reshape(3, D_FF)
    P["g_final"] = g_final

    loss, dx, dmod, gw, gs = _local_step(x, mod, loss_target, W, P)

    loss = lax.psum(loss[0, 0], MESH_AXES)

    small_shapes = [gs[n].shape for n in SMALL]
    packed, counts = _pack_rows([gs[n] for n in SMALL] + [dmod])
    n_small = sum(counts[:-1])
    gathered = _exchange(packed, "all", "gather", "gather_small")
    small_sum = _sum_slots(gathered[:, :n_small], "sum_small")
    g_small = dict(zip(SMALL, _unpack_rows(small_sum, small_shapes)))
    dmod_all = gathered[:, n_small:].reshape(8, -1)[:, :B * 6 * D_MODEL].reshape(8 * B, 6 * D_MODEL)
    dmod_cols = lax.dynamic_slice_in_dim(dmod_all, chip * ada_cols, ada_cols, axis=1)
    g_w_ada, g_b_ada = _ada_bwd(c_all, dmod_all, dmod_cols)

    G = [gw[n + ("_t" if t else "")].reshape(N_XY, -1, LANES) for n, t in DIRECT]
    G.append(_flatten_full({n: gw[n] for n, _, _ in BIG}))
    theirs = _exchange_list(G, "c", "half", "reduce_cores")
    pair = [_pair_add(g, t, core, "pair_add_%d" % i) for i, (g, t) in enumerate(zip(G, theirs))]
    parts = _exchange_list(pair, "xy", "scatter", "reduce_chips")
    red = [_sum_slots(p, "sum_chips_%d" % i) for i, p in enumerate(parts)]
    red_sib = _exchange_list(red, "c", "swap", "share_cores")
    reduced = [jnp.concatenate([jnp.where(south, r, s), jnp.where(south, s, r)], axis=0) for r, s in zip(red, red_sib)]

    grads = {"w_ada": g_w_ada[None], "b_ada": g_b_ada}
    for (n, t), g in zip(DIRECT, reduced):
        grads[n] = (g.T if t else g)[None]
    for n, g in _unflatten_shard(reduced[-1]).items():
        grads[n] = g[None]
    wc_cols = w_conv.shape[2]
    for n in SMALL:
        g = g_small[n]
        if n == "w_conv":
            g = lax.dynamic_slice_in_dim(g, chip * wc_cols, wc_cols, axis=1)
        grads[n] = g.reshape(w[n].shape)

    delta, new_m, new_v = {}, {}, {}
    for n in ["w_ada"] + [b for b, _ in DIRECT] + [b for b, _, _ in BIG]:
        shp = w[n].shape
        d2, m2, v2 = _adamw(w[n][0], grads[n][0], m[n][0], v[n][0], "adamw_" + n)
        delta[n], new_m[n], new_v[n] = d2.reshape(shp), m2.reshape(shp), v2.reshape(shp)
    rest = ["b_ada"] + SMALL
    shapes = [w[n].shape for n in rest]
    pw, _ = _pack_rows([w[n] for n in rest])
    pg, _ = _pack_rows([grads[n] for n in rest])
    pm, _ = _pack_rows([m[n] for n in rest])
    pv, _ = _pack_rows([v[n] for n in rest])
    d2, m2, v2 = _adamw(pw, pg, pm, pv, "adamw_small")
    for n, dd, mm, vv in zip(rest, _unpack_rows(d2, shapes), _unpack_rows(m2, shapes), _unpack_rows(v2, shapes)):
        delta[n], new_m[n], new_v[n] = dd, mm, vv

    return (loss, dx, *[grads[n] for n in WEIGHTS], *[delta[n] for n in WEIGHTS], *[new_m[n] for n in WEIGHTS],
            *[new_v[n] for n in WEIGHTS])
```

```python
import functools
import math

import numpy as np
import jax
import jax.numpy as jnp
from jax import lax
from jax.experimental import pallas as pl
from jax.experimental.pallas import tpu as pltpu

F32, BF16 = jnp.float32, jnp.bfloat16

D_MODEL = 1024
N_HEADS = 8
HEAD_DIM = 64
ATT_WIDTH = 512
SSM_GROUPS = 16
SSM_GROUP_CH = 16
SSM_WIDTH = 256
SSM_STATE = 64
SSM_LANES = SSM_GROUPS * SSM_STATE
D_FF = 2048
IN_WIDTH = 3 * ATT_WIDTH + SSM_WIDTH + 2 * D_MODEL
ATT_BLOCK = 128
N_PATTERNS = 3
EPS = 1e-6
NEG_INF = -1e30

ADAM_LR, ADAM_B1, ADAM_B2, ADAM_EPS, ADAM_WD, ADAM_STEP = 0.001, 0.9, 0.999, 1e-08, 0.01, 10

V7X_VMEM_LIMIT_BYTES = 56 * 1024 * 1024
LANES = 1024

MESH_AXES = ("x", "y", "c")


def _pcall(body, *, name, out_shape, grid=(), in_specs=None, out_specs=None, scratch_shapes=(), dims=None):
    params = dict(vmem_limit_bytes=V7X_VMEM_LIMIT_BYTES)
    if dims is not None:
        params["dimension_semantics"] = dims
    specs = {}
    if in_specs is not None:
        specs = dict(grid=grid, in_specs=in_specs, out_specs=out_specs)
    return pl.pallas_call(body, name=name, out_shape=out_shape, scratch_shapes=scratch_shapes,
                          compiler_params=pltpu.CompilerParams(**params), **specs)


def _sds(shape, dtype):
    return jax.ShapeDtypeStruct(tuple(shape), dtype)


def _tile(n, target):
    if n <= target:
        return n
    for t in range(target - target % 128, 0, -128):
        if n % t == 0:
            return t
    raise ValueError((n, target))


def _sig(v):
    return 1.0 / (1.0 + jnp.exp(-v))


def _mm(a, b, *, name, ta=False, tb=False, out_dtype=F32, tm=2048, tn=1024, tk=1024):
    halves = a.ndim == 3
    if halves:
        a_rows, a_cols = a.shape[1], 2 * a.shape[2]
    else:
        a_rows, a_cols = a.shape
    if ta:
        K, M = a_rows, a_cols
    else:
        M, K = a_rows, a_cols
    if tb:
        N, K2 = b.shape
    else:
        K2, N = b.shape
    assert K == K2, (a.shape, b.shape)
    if halves:
        tm, tk = (min(tm, M // 2), tk) if ta else (tm, min(tk, K // 2))
    tm, tn, tk = _tile(M, tm), _tile(N, tn), _tile(K, tk)
    nk = K // tk
    if halves and ta:
        per = a.shape[2] // tm
        a_spec = pl.BlockSpec((None, tk, tm), lambda i, j, k: (i // per, k, i % per))
    elif halves:
        per = a.shape[2] // tk
        a_spec = pl.BlockSpec((None, tm, tk), lambda i, j, k: (k // per, i, k % per))
    else:
        a_spec = pl.BlockSpec((tk, tm), lambda i, j, k: (k, i)) if ta else pl.BlockSpec((tm, tk), lambda i, j, k: (i, k))
    b_spec = pl.BlockSpec((tn, tk), lambda i, j, k: (j, k)) if tb else pl.BlockSpec((tk, tn), lambda i, j, k: (k, j))
    dn = (((0 if ta else 1,), (1 if tb else 0,)), ((), ()))

    def body(a_ref, b_ref, o_ref, acc_ref):
        k = pl.program_id(2)

        @pl.when(k == 0)
        def _():
            acc_ref[...] = jnp.zeros_like(acc_ref)

        acc_ref[...] += lax.dot_general(a_ref[...].astype(BF16), b_ref[...].astype(BF16), dn,
                                        preferred_element_type=F32)

        @pl.when(k == nk - 1)
        def _():
            o_ref[...] = acc_ref[...].astype(out_dtype)

    def body_single(a_ref, b_ref, o_ref):
        o_ref[...] = lax.dot_general(a_ref[...].astype(BF16), b_ref[...].astype(BF16), dn,
                                     preferred_element_type=F32).astype(out_dtype)

    return _pcall(body_single if nk == 1 else body, name=name, out_shape=_sds((M, N), out_dtype),
                  grid=(M // tm, N // tn, nk), in_specs=[a_spec, b_spec],
                  out_specs=pl.BlockSpec((tm, tn), lambda i, j, k: (i, j)),
                  scratch_shapes=[] if nk == 1 else [pltpu.VMEM((tm, tn), F32)],
                  dims=("parallel", "parallel", "arbitrary"))(a, b)


def _ada_fwd(c_all, w_ada, b_ada_cols):
    n = w_ada.shape[1]

    def body(c_ref, w_ref, b_ref, o_ref):
        c = c_ref[...]
        act = c * _sig(c)
        o_ref[...] = jnp.dot(act.astype(BF16), w_ref[...].astype(BF16), preferred_element_type=F32) + b_ref[...]

    return _pcall(body, name="ada_fwd", out_shape=_sds((c_all.shape[0], n), F32))(c_all, w_ada, b_ada_cols)


def _ada_bwd(c_all, dmod_all, dmod_cols):
    n = dmod_cols.shape[1]

    def body(c_ref, da_ref, dc_ref, gw_ref, gb_ref):
        c = c_ref[...]
        act = c * _sig(c)
        gw_ref[...] = lax.dot_general(act, dc_ref[...], (((0,), (0,)), ((), ())), preferred_element_type=F32,
                                      precision=lax.Precision.HIGHEST)
        gb_ref[...] = jnp.sum(da_ref[...], axis=0, keepdims=True)

    return _pcall(body, name="ada_bwd", out_shape=(_sds((D_MODEL, n), F32), _sds((1, dmod_all.shape[1]), F32)))(
        c_all, dmod_all, dmod_cols)


ROW_TILE = 512


def _row_specs(B, S):
    ts = min(S, ROW_TILE)
    row = pl.BlockSpec((1, ts, D_MODEL), lambda b, s: (b, s, 0))
    bvec = pl.BlockSpec((1, 1, D_MODEL), lambda b, s: (b, 0, 0))
    gvec = pl.BlockSpec((1, D_MODEL), lambda b, s: (0, 0))
    return ts, row, bvec, gvec


def _norm_mod(x3, g, sc, sh):
    B, S, _ = x3.shape
    ts, row, bvec, gvec = _row_specs(B, S)

    def body(x_ref, g_ref, sc_ref, sh_ref, u_ref):
        x = x_ref[0]
        r = lax.rsqrt(jnp.mean(x * x, axis=-1, keepdims=True) + EPS)
        u_ref[0] = ((x * r) * g_ref[...] * (1.0 + sc_ref[0]) + sh_ref[0]).astype(BF16)

    return _pcall(body, name="norm_mod1", out_shape=_sds(x3.shape, BF16), grid=(B, S // ts),
                  in_specs=[row, gvec, bvec, bvec], out_specs=row, dims=("parallel", "parallel"))(x3, g, sc, sh)


def _resid_norm_mod(x3, mix3, gt, g, sc, sh):
    B, S, _ = x3.shape
    ts, row, bvec, gvec = _row_specs(B, S)

    def body(x_ref, m_ref, gt_ref, g_ref, sc_ref, sh_ref, h_ref, u_ref):
        h = x_ref[0] + gt_ref[0] * m_ref[0]
        h_ref[0] = h
        r = lax.rsqrt(jnp.mean(h * h, axis=-1, keepdims=True) + EPS)
        u_ref[0] = ((h * r) * g_ref[...] * (1.0 + sc_ref[0]) + sh_ref[0]).astype(BF16)

    return _pcall(body, name="resid_norm_mod2", out_shape=(_sds(x3.shape, F32), _sds(x3.shape, BF16)),
                  grid=(B, S // ts), in_specs=[row, row, bvec, gvec, bvec, bvec], out_specs=(row, row),
                  dims=("parallel", "parallel"))(x3, mix3, gt, g, sc, sh)


def _norm_bwd(h3, du3, dres3, g, sc, name, mix3=None, gt=None):
    B, S, _ = h3.shape
    ts, row, bvec, gvec = _row_specs(B, S)
    with_gate = mix3 is not None

    def body(*refs):
        if with_gate:
            h_ref, du_ref, dr_ref, g_ref, sc_ref, m_ref, gt_ref, dh_ref, dsh_ref, dsc_ref, dg_ref, dgt_ref, dm_ref = refs
        else:
            h_ref, du_ref, dr_ref, g_ref, sc_ref, dh_ref, dsh_ref, dsc_ref, dg_ref = refs
        b, s = pl.program_id(0), pl.program_id(1)
        h = h_ref[0]
        r = lax.rsqrt(jnp.mean(h * h, axis=-1, keepdims=True) + EPS)
        xn = h * r
        du = du_ref[0].astype(F32)
        g = g_ref[...]
        sc1 = 1.0 + sc_ref[0]
        dxn = du * g * sc1
        dh = dr_ref[0] + r * (dxn - xn * jnp.mean(dxn * xn, axis=-1, keepdims=True))
        dh_ref[0] = dh

        @pl.when(s == 0)
        def _():
            dsh_ref[...] = jnp.zeros_like(dsh_ref)
            dsc_ref[...] = jnp.zeros_like(dsc_ref)
            if with_gate:
                dgt_ref[...] = jnp.zeros_like(dgt_ref)

        @pl.when((s == 0) & (b == 0))
        def _():
            dg_ref[...] = jnp.zeros_like(dg_ref)

        dux = du * xn
        dsh_ref[0] += jnp.sum(du, axis=0, keepdims=True)
        dsc_ref[0] += jnp.sum(dux * g, axis=0, keepdims=True)
        dg_ref[...] += jnp.sum(dux * sc1, axis=0, keepdims=True)
        if with_gate:
            dgt_ref[0] += jnp.sum(dh * m_ref[0], axis=0, keepdims=True)
            dm_ref[0] = (dh * gt_ref[0]).astype(BF16)

    bshape = _sds((B, 1, D_MODEL), F32)
    in_specs = [row, row, row, gvec, bvec]
    out_shape = [_sds(h3.shape, F32), bshape, bshape, _sds((1, D_MODEL), F32)]
    out_specs = [row, bvec, bvec, gvec]
    args = [h3, du3, dres3, g, sc]
    if with_gate:
        in_specs += [row, bvec]
        out_shape += [bshape, _sds(h3.shape, BF16)]
        out_specs += [bvec, row]
        args += [mix3, gt]
    return _pcall(body, name=name, out_shape=tuple(out_shape), grid=(B, S // ts), in_specs=in_specs,
                  out_specs=tuple(out_specs), dims=("arbitrary", "arbitrary"))(*args)


def _final_loss(h1, ffn3, tgt3, gt, gfin):
    B, S, _ = h1.shape
    ts, row, bvec, gvec = _row_specs(B, S)
    one = pl.BlockSpec((1, 1), lambda b, s: (0, 0))

    def body(h_ref, f_ref, t_ref, gt_ref, gf_ref, dh_ref, dff_ref, dgt_ref, dgf_ref, loss_ref):
        b, s = pl.program_id(0), pl.program_id(1)
        f = f_ref[0].astype(F32)
        gtv = gt_ref[0]
        gf = gf_ref[...]
        h2 = h_ref[0] + gtv * f
        r = lax.rsqrt(jnp.mean(h2 * h2, axis=-1, keepdims=True) + EPS)
        n = h2 * r
        e = n * gf - t_ref[0]
        dy = e * (1.0 / D_MODEL)
        dn = dy * gf
        dh2 = r * (dn - n * jnp.mean(dn * n, axis=-1, keepdims=True))
        dh_ref[0] = dh2
        dff_ref[0] = (dh2 * gtv).astype(BF16)

        @pl.when(s == 0)
        def _():
            dgt_ref[...] = jnp.zeros_like(dgt_ref)

        @pl.when((s == 0) & (b == 0))
        def _():
            dgf_ref[...] = jnp.zeros_like(dgf_ref)
            loss_ref[...] = jnp.zeros_like(loss_ref)

        dgt_ref[0] += jnp.sum(dh2 * f, axis=0, keepdims=True)
        dgf_ref[...] += jnp.sum(dy * n, axis=0, keepdims=True)
        rows = jnp.sum(e * e, axis=1, keepdims=True)
        loss_ref[...] += jnp.sum(rows, axis=0, keepdims=True) * (0.5 / D_MODEL)

    return _pcall(body, name="final_loss",
                  out_shape=(_sds(h1.shape, F32), _sds(h1.shape, BF16), _sds((B, 1, D_MODEL), F32),
                             _sds((1, D_MODEL), F32), _sds((1, 1), F32)),
                  grid=(B, S // ts), in_specs=[row, row, row, bvec, gvec], out_specs=(row, row, bvec, gvec, one),
                  dims=("arbitrary", "arbitrary"))(h1, ffn3, tgt3, gt, gfin)


def _att_scores(qh, kc, kp, h, dil, first, a_idx, j_idx):
    scale = HEAD_DIM ** -0.5
    nt = (((1,), (1,)), ((), ()))
    slope = (2.0 ** (-8.0 * (h + 1) / N_HEADS)) * dil
    dist_c = (a_idx - j_idx).astype(F32)
    s_c = lax.dot_general(qh, kc, nt, preferred_element_type=F32) * scale
    s_c = jnp.where(a_idx >= j_idx, s_c - slope * dist_c, NEG_INF)
    s_p = lax.dot_general(qh, kp, nt, preferred_element_type=F32) * scale
    s_p = jnp.where((j_idx >= a_idx) & jnp.logical_not(first), s_p - slope * (dist_c + float(ATT_BLOCK)), NEG_INF)
    return s_c, s_p


def _att_block_consts(seq_blocks):
    p = pl.program_id(0)
    j = pl.program_id(1)
    nb = lax.shift_right_logical(jnp.int32(seq_blocks), 2 * p)
    dil = lax.shift_left(jnp.int32(1), 2 * p).astype(F32)
    a_idx = lax.broadcasted_iota(jnp.int32, (ATT_BLOCK, ATT_BLOCK), 0)
    j_idx = lax.broadcasted_iota(jnp.int32, (ATT_BLOCK, ATT_BLOCK), 1)
    return j, nb, dil, a_idx, j_idx


def _attn_fwd(qb, kb, vb, seq_blocks):
    _, NB, _, _ = qb.shape
    cur = pl.BlockSpec((None, None, ATT_BLOCK, ATT_WIDTH), lambda p, j: (p, j, 0, 0))
    prev = pl.BlockSpec((None, None, ATT_BLOCK, ATT_WIDTH), lambda p, j: (p, jnp.maximum(j - 1, 0), 0, 0))
    lse_spec = pl.BlockSpec((None, None, ATT_BLOCK, N_HEADS), lambda p, j: (p, j, 0, 0))

    def body(q_ref, kc_ref, kp_ref, vc_ref, vp_ref, o_ref, lse_ref):
        j, nb, dil, a_idx, j_idx = _att_block_consts(seq_blocks)
        first = lax.rem(j, nb) == 0
        for h in range(N_HEADS):
            hs = slice(h * HEAD_DIM, (h + 1) * HEAD_DIM)
            s_c, s_p = _att_scores(q_ref[:, hs], kc_ref[:, hs], kp_ref[:, hs], h, dil, first, a_idx, j_idx)
            m = jnp.maximum(jnp.max(s_c, axis=1, keepdims=True), jnp.max(s_p, axis=1, keepdims=True))
            p_c = jnp.exp(s_c - m)
            p_p = jnp.exp(s_p - m)
            den = jnp.sum(p_c, axis=1, keepdims=True) + jnp.sum(p_p, axis=1, keepdims=True)
            o = (jnp.dot(p_c.astype(BF16), vc_ref[:, hs], preferred_element_type=F32)
                 + jnp.dot(p_p.astype(BF16), vp_ref[:, hs], preferred_element_type=F32))
            o_ref[:, hs] = o / den
            lse_ref[:, h:h + 1] = m + jnp.log(den)

    return _pcall(body, name="attn_fwd",
                  out_shape=(_sds(qb.shape, F32), _sds((N_PATTERNS, NB, ATT_BLOCK, N_HEADS), F32)),
                  grid=(N_PATTERNS, NB), in_specs=[cur, cur, prev, cur, prev], out_specs=(cur, lse_spec),
                  dims=("parallel", "parallel"))(qb, kb, kb, vb, vb)


def _attn_combine(o_p, lse_p):
    _, T, _ = o_p.shape
    tm = min(T, 1024)

    def body(o_ref, l_ref, out_ref, lse_ref):
        l0, l1, l2 = l_ref[0], l_ref[1], l_ref[2]
        m = jnp.maximum(jnp.maximum(l0, l1), l2)
        lse = m + jnp.log(jnp.exp(l0 - m) + jnp.exp(l1 - m) + jnp.exp(l2 - m))
        lse_ref[...] = lse
        w = [jnp.exp(l0 - lse), jnp.exp(l1 - lse), jnp.exp(l2 - lse)]
        for h in range(N_HEADS):
            hs = slice(h * HEAD_DIM, (h + 1) * HEAD_DIM)
            acc = w[0][:, h:h + 1] * o_ref[0, :, hs]
            acc = acc + w[1][:, h:h + 1] * o_ref[1, :, hs]
            acc = acc + w[2][:, h:h + 1] * o_ref[2, :, hs]
            out_ref[:, hs] = acc.astype(BF16)

    return _pcall(body, name="attn_combine", out_shape=(_sds((T, ATT_WIDTH), BF16), _sds((T, N_HEADS), F32)),
                  grid=(T // tm,),
                  in_specs=[pl.BlockSpec((N_PATTERNS, tm, ATT_WIDTH), lambda i: (0, i, 0)),
                            pl.BlockSpec((N_PATTERNS, tm, N_HEADS), lambda i: (0, i, 0))],
                  out_specs=(pl.BlockSpec((tm, ATT_WIDTH), lambda i: (i, 0)), pl.BlockSpec((tm, N_HEADS), lambda i: (i, 0))),
                  dims=("parallel",))(o_p, lse_p)


def _attn_bwd(qb, kb, vb, dob, ob, lseb, seq_blocks):
    _, NB, _, _ = qb.shape
    last = NB - 1
    cur = pl.BlockSpec((None, None, ATT_BLOCK, ATT_WIDTH), lambda p, j: (p, jnp.minimum(j, last), 0, 0))
    prev = pl.BlockSpec((None, None, ATT_BLOCK, ATT_WIDTH),
                        lambda p, j: (p, jnp.maximum(jnp.minimum(j, last) - 1, 0), 0, 0))
    lag = pl.BlockSpec((None, None, ATT_BLOCK, ATT_WIDTH), lambda p, j: (p, jnp.maximum(j - 1, 0), 0, 0))
    lse_spec = pl.BlockSpec((None, None, ATT_BLOCK, N_HEADS), lambda p, j: (p, jnp.minimum(j, last), 0, 0))
    scale = HEAD_DIM ** -0.5
    tn = (((0,), (0,)), ((), ()))
    nt = (((1,), (1,)), ((), ()))

    def body(q_ref, kc_ref, kp_ref, vc_ref, vp_ref, do_ref, o_ref, lse_ref, dq_ref, dk_ref, dv_ref, ck_ref, cv_ref):
        j, nb, dil, a_idx, j_idx = _att_block_consts(seq_blocks)

        @pl.when(j == 0)
        def _():
            ck_ref[...] = jnp.zeros_like(ck_ref)
            cv_ref[...] = jnp.zeros_like(cv_ref)

        @pl.when(j <= last)
        def _():
            first = lax.rem(j, nb) == 0
            for h in range(N_HEADS):
                hs = slice(h * HEAD_DIM, (h + 1) * HEAD_DIM)
                qh, kc, kp, vc, vp, doh = q_ref[:, hs], kc_ref[:, hs], kp_ref[:, hs], vc_ref[:, hs], vp_ref[:, hs], do_ref[:, hs]
                s_c, s_p = _att_scores(qh, kc, kp, h, dil, first, a_idx, j_idx)
                lse = lse_ref[:, h:h + 1]
                p_c = jnp.exp(s_c - lse)
                p_p = jnp.exp(s_p - lse)
                delta = jnp.sum(doh.astype(F32) * o_ref[:, hs].astype(F32), axis=1, keepdims=True)
                ds_c = (p_c * (lax.dot_general(doh, vc, nt, preferred_element_type=F32) - delta)).astype(BF16)
                ds_p = (p_p * (lax.dot_general(doh, vp, nt, preferred_element_type=F32) - delta)).astype(BF16)
                dq_ref[:, hs] = (jnp.dot(ds_c, kc, preferred_element_type=F32)
                                 + jnp.dot(ds_p, kp, preferred_element_type=F32)) * scale
                dk_ref[:, hs] = ck_ref[:, hs] + lax.dot_general(ds_p, qh, tn, preferred_element_type=F32) * scale
                dv_ref[:, hs] = cv_ref[:, hs] + lax.dot_general(p_p.astype(BF16), doh, tn, preferred_element_type=F32)
                ck_ref[:, hs] = lax.dot_general(ds_c, qh, tn, preferred_element_type=F32) * scale
                cv_ref[:, hs] = lax.dot_general(p_c.astype(BF16), doh, tn, preferred_element_type=F32)

        @pl.when(j == NB)
        def _():
            dk_ref[...] = ck_ref[...]
            dv_ref[...] = cv_ref[...]

    shp = _sds(qb.shape, F32)
    return _pcall(body, name="attn_bwd", out_shape=(shp, shp, shp), grid=(N_PATTERNS, NB + 1),
                  in_specs=[cur, cur, prev, cur, prev, cur, cur, lse_spec], out_specs=(cur, lag, lag),
                  scratch_shapes=[pltpu.VMEM((ATT_BLOCK, ATT_WIDTH), F32), pltpu.VMEM((ATT_BLOCK, ATT_WIDTH), F32)],
                  dims=("arbitrary", "arbitrary"))(qb, kb, kb, vb, vb, dob, ob, lseb)


def _sum3_cast(a, b, c):
    T, N = a.shape
    tm = min(T, 1024)
    spec = pl.BlockSpec((tm, N), lambda i: (i, 0))

    def body(a_ref, b_ref, c_ref, o_ref):
        o_ref[...] = (a_ref[...] + b_ref[...] + c_ref[...]).astype(BF16)

    return _pcall(body, name="sum3_cast", out_shape=_sds((T, N), BF16), grid=(T // tm,), in_specs=[spec] * 3,
                  out_specs=spec, dims=("parallel",))(a, b, c)


def _to_blocks(t, B, S):
    C = t.shape[-1]
    outs = []
    for p in range(N_PATTERNS):
        d = 4 ** p
        u = t.reshape(B, S // d, d, C).transpose(0, 2, 1, 3)
        outs.append(u.reshape(B * S // ATT_BLOCK, ATT_BLOCK, C))
    return jnp.stack(outs, axis=0)


def _from_blocks(tb, B, S):
    C = tb.shape[-1]
    outs = []
    for p in range(N_PATTERNS):
        d = 4 ** p
        u = tb[p].reshape(B, d, S // d, C).transpose(0, 2, 1, 3)
        outs.append(u.reshape(B * S, C))
    return jnp.stack(outs, axis=0)


ATT_GROUP = 4
ATT_GW = ATT_GROUP * HEAD_DIM
ATT_GROUPS = N_HEADS // ATT_GROUP
ATT_PAIRS = ATT_GW // ATT_BLOCK
ATT_UNROLL = 3
NT_DIMS = (((1,), (1,)), ((), ()))
TN_DIMS = (((0,), (0,)), ((), ()))


def _att_rows(start, d):
    if d == 1:
        return pl.ds(start if isinstance(start, int) else pl.multiple_of(start, ATT_BLOCK), ATT_BLOCK)
    return pl.ds(start, ATT_BLOCK, stride=d)


def _att_fill_bias(bias_ref, g, d):
    a = lax.broadcasted_iota(jnp.int32, (ATT_BLOCK, ATT_BLOCK), 0)
    j = lax.broadcasted_iota(jnp.int32, (ATT_BLOCK, ATT_BLOCK), 1)
    dist = (a - j).astype(F32)
    for hh in range(ATT_GROUP):
        t, e = divmod(hh, 2)
        rs = slice(e * ATT_BLOCK, (e + 1) * ATT_BLOCK)
        lo = 2.0 ** (-8.0 * (hh + 1) / N_HEADS) * d
        hi = 2.0 ** (-8.0 * (ATT_GROUP + hh + 1) / N_HEADS) * d
        slope = jnp.where(g == 0, lo, hi).astype(F32)
        bias_ref[t, rs, 0:ATT_BLOCK] = jnp.where(a >= j, -slope * dist, NEG_INF)
        bias_ref[t, rs, ATT_BLOCK:] = jnp.where(j >= a, -slope * (dist + float(ATT_BLOCK)), NEG_INF)


def _stack_heads(v2, low):
    return jnp.concatenate([jnp.where(low, v2, 0.0), jnp.where(low, 0.0, v2)], axis=0).astype(BF16)


def _unstack_heads(r2, low):
    return jnp.where(low, r2[0:ATT_BLOCK], r2[ATT_BLOCK:])


def _attention_fwd(proj3, seq_blocks):
    B, S, _ = proj3.shape
    scale = HEAD_DIM ** -0.5
    nq = ATT_WIDTH // ATT_GW

    def col(k):
        return pl.BlockSpec((1, S, ATT_GW), lambda b, g, k=k: (b, 0, k * nq + g))

    o_spec = pl.BlockSpec((1, S, ATT_GW), lambda b, g: (b, 0, g))
    l_spec = pl.BlockSpec((1, 1, S, ATT_BLOCK), lambda b, g: (b, g, 0, 0))

    def body(q_ref, k_ref, v_ref, o_ref, lse_ref, qf, kf, vf, os, ls, bias):
        g = pl.program_id(1)
        for t in range(ATT_PAIRS):
            ts = slice(t * ATT_BLOCK, (t + 1) * ATT_BLOCK)
            qf[t] = q_ref[0, :, ts].astype(F32) * scale
            kf[t] = k_ref[0, :, ts].astype(F32)
            vf[t] = v_ref[0, :, ts].astype(F32)
        lane = lax.broadcasted_iota(jnp.int32, (ATT_BLOCK, ATT_BLOCK), 1)
        low = lane < HEAD_DIM

        def block(p, d, r, n, has_prev):
            start = n * (ATT_BLOCK * d) + r
            rows = _att_rows(start, d)
            prows = _att_rows(start - ATT_BLOCK * d, d) if has_prev else None
            lse_t = jnp.zeros((ATT_BLOCK, ATT_BLOCK), F32)
            for t in range(ATT_PAIRS):
                q2 = _stack_heads(qf[t, rows, :], low)
                k2 = kf[t, rows, :].astype(BF16)
                v2 = vf[t, rows, :].astype(BF16)
                if has_prev:
                    k2 = jnp.concatenate([k2, kf[t, prows, :].astype(BF16)], axis=0)
                    v2 = jnp.concatenate([v2, vf[t, prows, :].astype(BF16)], axis=0)
                    b2 = bias[t]
                else:
                    b2 = bias[t, :, 0:ATT_BLOCK]
                s = lax.dot_general(q2, k2, NT_DIMS, preferred_element_type=F32) + b2
                m = jnp.max(s, axis=1, keepdims=True)
                pr = jnp.exp(s - m)
                den = jnp.sum(pr, axis=1, keepdims=True)
                o = jnp.dot(pr.astype(BF16), v2, preferred_element_type=F32) / den
                os[p, t, rows, :] = _unstack_heads(o, low)
                lse2 = m + jnp.log(den)
                lse_t = jnp.where(lane == 2 * t, lse2[0:ATT_BLOCK], lse_t)
                lse_t = jnp.where(lane == 2 * t + 1, lse2[ATT_BLOCK:], lse_t)
            ls[p, rows, :] = lse_t

        for p in range(N_PATTERNS):
            d = 4 ** p
            _att_fill_bias(bias, g, d)
            _att_one_pattern(block, p, d, seq_blocks // d)

        def combine(i, carry):
            rows = pl.ds(pl.multiple_of(i * ATT_BLOCK, ATT_BLOCK), ATT_BLOCK)
            l0, l1, l2 = ls[0, rows, :], ls[1, rows, :], ls[2, rows, :]
            m = jnp.maximum(jnp.maximum(l0, l1), l2)
            lse = m + jnp.log(jnp.exp(l0 - m) + jnp.exp(l1 - m) + jnp.exp(l2 - m))
            lse_ref[0, 0, rows, :] = lse
            w = [jnp.exp(l0 - lse), jnp.exp(l1 - lse), jnp.exp(l2 - lse)]
            for t in range(ATT_PAIRS):
                acc = jnp.zeros((ATT_BLOCK, ATT_BLOCK), F32)
                for p in range(N_PATTERNS):
                    wt = jnp.where(low, w[p][:, 2 * t:2 * t + 1], w[p][:, 2 * t + 1:2 * t + 2])
                    acc = acc + wt * os[p, t, rows, :]
                o_ref[0, rows, t * ATT_BLOCK:(t + 1) * ATT_BLOCK] = acc.astype(BF16)
            return carry

        lax.fori_loop(0, S // ATT_BLOCK, combine, 0, unroll=2)

    return _pcall(body, name="attention_fwd",
                  out_shape=(_sds((B, S, ATT_WIDTH), BF16), _sds((B, ATT_GROUPS, S, ATT_BLOCK), F32)),
                  grid=(B, ATT_GROUPS), in_specs=[col(0), col(1), col(2)], out_specs=(o_spec, l_spec),
                  scratch_shapes=[pltpu.VMEM((ATT_PAIRS, S, ATT_BLOCK), F32)] * 3
                  + [pltpu.VMEM((N_PATTERNS, ATT_PAIRS, S, ATT_BLOCK), F32), pltpu.VMEM((N_PATTERNS, S, ATT_BLOCK), F32),
                     pltpu.VMEM((ATT_PAIRS, 2 * ATT_BLOCK, 2 * ATT_BLOCK), F32)],
                  dims=("parallel", "parallel"))(proj3, proj3, proj3)


def _att_one_pattern(block, p, d, nb):
    def per_residue(r, carry):
        block(p, d, r, 0, False)
        if nb > 1:
            def per_block(n, c2):
                block(p, d, r, n, True)
                return c2
            lax.fori_loop(1, nb, per_block, 0, unroll=ATT_UNROLL)
        return carry

    if d == 1:
        per_residue(0, 0)
    else:
        lax.fori_loop(0, d, per_residue, 0, unroll=ATT_UNROLL + 1 if nb == 1 else 1)


def _attention_bwd(proj3, do3, o3, lse4, seq_blocks):
    B, S, _ = proj3.shape
    scale = HEAD_DIM ** -0.5
    nq = ATT_WIDTH // ATT_GW

    def col(k):
        return pl.BlockSpec((1, S, ATT_GW), lambda b, g, k=k: (b, 0, k * nq + g))

    o_spec = pl.BlockSpec((1, S, ATT_GW), lambda b, g: (b, 0, g))
    l_spec = pl.BlockSpec((1, 1, S, ATT_BLOCK), lambda b, g: (b, g, 0, 0))

    def body(q_ref, k_ref, v_ref, do_ref, o_ref, lse_ref, dq_ref, dk_ref, dv_ref,
             qf, kf, vf, dof, dl, aq, ak, av, bias):
        g = pl.program_id(1)
        for t in range(ATT_PAIRS):
            ts = slice(t * ATT_BLOCK, (t + 1) * ATT_BLOCK)
            qf[t] = q_ref[0, :, ts].astype(F32) * scale
            kf[t] = k_ref[0, :, ts].astype(F32)
            vf[t] = v_ref[0, :, ts].astype(F32)
            dof[t] = do_ref[0, :, ts].astype(F32)
        aq[...] = jnp.zeros_like(aq)
        ak[...] = jnp.zeros_like(ak)
        av[...] = jnp.zeros_like(av)
        lane = lax.broadcasted_iota(jnp.int32, (ATT_BLOCK, ATT_BLOCK), 1)
        low = lane < HEAD_DIM

        def fill_delta(i, carry):
            rows = pl.ds(pl.multiple_of(i * ATT_BLOCK, ATT_BLOCK), ATT_BLOCK)
            acc = jnp.zeros((ATT_BLOCK, ATT_BLOCK), F32)
            for t in range(ATT_PAIRS):
                prod = dof[t, rows, :] * o_ref[0, rows, t * ATT_BLOCK:(t + 1) * ATT_BLOCK].astype(F32)
                lo = jnp.sum(jnp.where(low, prod, 0.0), axis=1, keepdims=True)
                hi = jnp.sum(prod, axis=1, keepdims=True) - lo
                acc = jnp.where(lane == 2 * t, lo, acc)
                acc = jnp.where(lane == 2 * t + 1, hi, acc)
            dl[rows, :] = acc
            return carry

        lax.fori_loop(0, S // ATT_BLOCK, fill_delta, 0, unroll=2)

        def block(p, d, r, n, has_prev):
            start = n * (ATT_BLOCK * d) + r
            rows = _att_rows(start, d)
            prows = _att_rows(start - ATT_BLOCK * d, d) if has_prev else None
            lse_t = lse_ref[0, 0, rows, :]
            dl_t = dl[rows, :]
            for t in range(ATT_PAIRS):
                q2 = _stack_heads(qf[t, rows, :], low)
                do2 = _stack_heads(dof[t, rows, :], low)
                k2 = kf[t, rows, :].astype(BF16)
                v2 = vf[t, rows, :].astype(BF16)
                if has_prev:
                    k2 = jnp.concatenate([k2, kf[t, prows, :].astype(BF16)], axis=0)
                    v2 = jnp.concatenate([v2, vf[t, prows, :].astype(BF16)], axis=0)
                    b2 = bias[t]
                else:
                    b2 = bias[t, :, 0:ATT_BLOCK]
                lse2 = jnp.concatenate([lse_t[:, 2 * t:2 * t + 1], lse_t[:, 2 * t + 1:2 * t + 2]], axis=0)
                dl2 = jnp.concatenate([dl_t[:, 2 * t:2 * t + 1], dl_t[:, 2 * t + 1:2 * t + 2]], axis=0)
                s = lax.dot_general(q2, k2, NT_DIMS, preferred_element_type=F32) + b2
                pr = jnp.exp(s - lse2)
                ds = (pr * (lax.dot_general(do2, v2, NT_DIMS, preferred_element_type=F32) - dl2)).astype(BF16)
                dq = _unstack_heads(jnp.dot(ds, k2, preferred_element_type=F32), low)
                dk = lax.dot_general(ds, q2, TN_DIMS, preferred_element_type=F32)
                dv = lax.dot_general(pr.astype(BF16), do2, TN_DIMS, preferred_element_type=F32)
                aq[t, rows, :] = aq[t, rows, :] + dq * scale
                ak[t, rows, :] = ak[t, rows, :] + dk[0:ATT_BLOCK]
                av[t, rows, :] = av[t, rows, :] + dv[0:ATT_BLOCK]
                if has_prev:
                    ak[t, prows, :] = ak[t, prows, :] + dk[ATT_BLOCK:]
                    av[t, prows, :] = av[t, prows, :] + dv[ATT_BLOCK:]

        for p in range(N_PATTERNS):
            d = 4 ** p
            _att_fill_bias(bias, g, d)
            _att_one_pattern(block, p, d, seq_blocks // d)

        for t in range(ATT_PAIRS):
            ts = slice(t * ATT_BLOCK, (t + 1) * ATT_BLOCK)
            dq_ref[0, :, ts] = aq[t].astype(BF16)
            dk_ref[0, :, ts] = ak[t].astype(BF16)
            dv_ref[0, :, ts] = av[t].astype(BF16)

    shp = _sds((B, S, ATT_WIDTH), BF16)
    pair_buf = pltpu.VMEM((ATT_PAIRS, S, ATT_BLOCK), F32)
    return _pcall(body, name="attention_bwd", out_shape=(shp, shp, shp), grid=(B, ATT_GROUPS),
                  in_specs=[col(0), col(1), col(2), o_spec, o_spec, l_spec], out_specs=(o_spec, o_spec, o_spec),
                  scratch_shapes=[pair_buf] * 4 + [pltpu.VMEM((S, ATT_BLOCK), F32)] + [pair_buf] * 3
                  + [pltpu.VMEM((ATT_PAIRS, 2 * ATT_BLOCK, 2 * ATT_BLOCK), F32)],
                  dims=("parallel", "parallel"))(proj3, proj3, proj3, do3, o3, lse4)


def _expand_groups(m):
    rows = SSM_WIDTH
    t = jnp.concatenate([m] * SSM_GROUPS, axis=0)
    r = lax.broadcasted_iota(jnp.int32, (rows, SSM_LANES), 0)
    l = lax.broadcasted_iota(jnp.int32, (rows, SSM_LANES), 1)
    keep = lax.shift_right_logical(r, 4) == lax.shift_right_logical(l, 6)
    return jnp.where(keep, t, 0.0)


def _collapse_groups(m):
    rows = SSM_WIDTH
    r = lax.broadcasted_iota(jnp.int32, (rows, SSM_LANES), 0)
    l = lax.broadcasted_iota(jnp.int32, (rows, SSM_LANES), 1)
    keep = lax.shift_right_logical(r, 4) == lax.shift_right_logical(l, 6)
    t = jnp.where(keep, m, 0.0)
    acc = t[0:SSM_GROUP_CH]
    for g in range(1, SSM_GROUPS):
        acc = acc + t[g * SSM_GROUP_CH:(g + 1) * SSM_GROUP_CH]
    return acc


def _zoh(lr, li, ldt):
    dt = jnp.exp(ldt)
    mag = jnp.exp(lr * dt)
    ang = li * dt
    cs, sn = jnp.cos(ang), jnp.sin(ang)
    ab_re, ab_im = mag * cs, mag * sn
    nr, ni = ab_re - 1.0, ab_im
    den = lr * lr + li * li
    n_re = nr * lr + ni * li
    n_im = ni * lr - nr * li
    return dict(dt=dt, mag=mag, cs=cs, sn=sn, ab_re=ab_re, ab_im=ab_im, nr=nr, ni=ni, den=den, n_re=n_re, n_im=n_im,
                f_re=n_re / den, f_im=n_im / den)


def _ssm_params(lr, li, ldt, br, bi, cr, ci):
    def body(lr_ref, li_ref, ldt_ref, br_ref, bi_ref, cr_ref, ci_ref, ab_ref, w_ref, c_ref):
        z = _zoh(lr_ref[...], li_ref[...], ldt_ref[...])
        ab_ref[0:1, :] = z["ab_re"]
        ab_ref[1:2, :] = z["ab_im"]
        br, bi = br_ref[...], bi_ref[...]
        w_ref[:, 0:SSM_LANES] = _expand_groups(z["f_re"] * br - z["f_im"] * bi).astype(BF16)
        w_ref[:, SSM_LANES:] = _expand_groups(z["f_re"] * bi + z["f_im"] * br).astype(BF16)
        c_ref[:, 0:SSM_LANES] = _expand_groups(cr_ref[...]).astype(BF16)
        c_ref[:, SSM_LANES:] = _expand_groups(-ci_ref[...]).astype(BF16)

    return _pcall(body, name="ssm_params",
                  out_shape=(_sds((2, SSM_LANES), F32), _sds((SSM_WIDTH, 2 * SSM_LANES), BF16),
                             _sds((SSM_WIDTH, 2 * SSM_LANES), BF16)))(lr, li, ldt, br, bi, cr, ci)


def _ssm_params_bwd(lr, li, ldt, br, bi, dab, dw, dc):
    def body(lr_ref, li_ref, ldt_ref, br_ref, bi_ref, dab_ref, dw_ref, dc_ref,
             dlr_ref, dli_ref, dldt_ref, dbr_ref, dbi_ref, dcr_ref, dci_ref):
        lr, li = lr_ref[...], li_ref[...]
        z = _zoh(lr, li, ldt_ref[...])
        br, bi = br_ref[...], bi_ref[...]
        dbb_re = _collapse_groups(dw_ref[:, 0:SSM_LANES])
        dbb_im = _collapse_groups(dw_ref[:, SSM_LANES:])
        dcr_ref[...] = _collapse_groups(dc_ref[:, 0:SSM_LANES])
        dci_ref[...] = -_collapse_groups(dc_ref[:, SSM_LANES:])
        f_re, f_im = z["f_re"], z["f_im"]
        dbr_ref[...] = f_re * dbb_re + f_im * dbb_im
        dbi_ref[...] = f_re * dbb_im - f_im * dbb_re
        df_re = jnp.sum(dbb_re * br + dbb_im * bi, axis=0, keepdims=True)
        df_im = jnp.sum(dbb_im * br - dbb_re * bi, axis=0, keepdims=True)
        den = z["den"]
        dn_re, dn_im = df_re / den, df_im / den
        dden = -(df_re * z["n_re"] + df_im * z["n_im"]) / (den * den)
        dnr = dn_re * lr - dn_im * li
        dni = dn_re * li + dn_im * lr
        dlr = dn_re * z["nr"] + dn_im * z["ni"] + 2.0 * dden * lr
        dli = dn_re * z["ni"] - dn_im * z["nr"] + 2.0 * dden * li
        dab_re = dab_ref[0:1, :] + dnr
        dab_im = dab_ref[1:2, :] + dni
        mag, cs, sn, dt = z["mag"], z["cs"], z["sn"], z["dt"]
        dmag = dab_re * cs + dab_im * sn
        dang = mag * (dab_im * cs - dab_re * sn)
        dlr_ref[...] = dlr + dmag * mag * dt
        dli_ref[...] = dli + dang * dt
        ddt = dmag * mag * lr + dang * li
        per_lane = jnp.broadcast_to(ddt * dt, (8, SSM_LANES))
        lane = lax.broadcasted_iota(jnp.int32, (SSM_LANES, 128), 0)
        col = lax.broadcasted_iota(jnp.int32, (SSM_LANES, 128), 1)
        ind = jnp.where(lax.shift_right_logical(lane, 6) == col, 1.0, 0.0)
        dldt_ref[...] = jnp.dot(per_lane, ind, preferred_element_type=F32, precision=lax.Precision.HIGHEST)[0:1]

    vec = _sds((1, SSM_LANES), F32)
    mat = _sds((SSM_GROUP_CH, SSM_LANES), F32)
    return _pcall(body, name="ssm_params_bwd", out_shape=(vec, vec, _sds((1, 128), F32), mat, mat, mat, mat))(
        lr, li, ldt, br, bi, dab, dw, dc)


SCAN_CHUNK = 512


def _scan_consts(ar, ai, k_ref, reverse):
    row = lax.broadcasted_iota(jnp.int32, (8, SSM_LANES), 0)
    pw = [(ar, ai)]
    for _ in range(7):
        pr, pi = pw[-1]
        pw.append((pr * ar - pi * ai, pr * ai + pi * ar))
    for n, k in enumerate((1, 2, 4)):
        keep = (row < 8 - k) if reverse else (row >= k)
        k_ref[2 * n] = jnp.where(keep, jnp.broadcast_to(pw[k - 1][0], (8, SSM_LANES)), 0.0)
        k_ref[2 * n + 1] = jnp.where(keep, jnp.broadcast_to(pw[k - 1][1], (8, SSM_LANES)), 0.0)
    cr = jnp.zeros((8, SSM_LANES), F32)
    ci = jnp.zeros((8, SSM_LANES), F32)
    for r in range(8):
        e = (8 - r) if reverse else (r + 1)
        cr = jnp.where(row == r, jnp.broadcast_to(pw[e - 1][0], (8, SSM_LANES)), cr)
        ci = jnp.where(row == r, jnp.broadcast_to(pw[e - 1][1], (8, SSM_LANES)), ci)
    k_ref[6] = cr
    k_ref[7] = ci


def _scan_tile(xr, xi, k_ref, car, cai, reverse):
    for n, k in enumerate((1, 2, 4)):
        sh = (8 - k) if reverse else k
        sr = pltpu.roll(xr, sh, 0)
        si = pltpu.roll(xi, sh, 0)
        mr, mi = k_ref[2 * n], k_ref[2 * n + 1]
        xr, xi = xr + mr * sr - mi * si, xi + mr * si + mi * sr
    pr, pi = k_ref[6], k_ref[7]
    xr, xi = xr + pr * car - pi * cai, xi + pr * cai + pi * car
    return xr, xi


def _scan_fwd(bu3, abar):
    B, S, _ = bu3.shape
    ch = min(S, SCAN_CHUNK)
    blk = pl.BlockSpec((1, ch, 2 * SSM_LANES), lambda b, c: (b, c, 0))

    def body(ab_ref, bu_ref, x_ref, k_ref, carry_ref):
        _scan_consts(ab_ref[0:1, :], ab_ref[1:2, :], k_ref, False)

        @pl.when(pl.program_id(1) == 0)
        def _():
            carry_ref[...] = jnp.zeros_like(carry_ref)

        def step(i, carry):
            base = pl.multiple_of(i * 8, 8)
            xr = bu_ref[0, pl.ds(base, 8), 0:SSM_LANES]
            xi = bu_ref[0, pl.ds(base, 8), SSM_LANES:]
            xr, xi = _scan_tile(xr, xi, k_ref, carry[0], carry[1], False)
            x_ref[0, pl.ds(base, 8), 0:SSM_LANES] = xr
            x_ref[0, pl.ds(base, 8), SSM_LANES:] = xi
            return (jnp.broadcast_to(xr[7:8], (8, SSM_LANES)), jnp.broadcast_to(xi[7:8], (8, SSM_LANES)))

        cr, ci = lax.fori_loop(0, ch // 8, step, (carry_ref[0], carry_ref[1]))
        carry_ref[0] = cr
        carry_ref[1] = ci

    return _pcall(body, name="scan_fwd", out_shape=_sds(bu3.shape, F32), grid=(B, S // ch),
                  in_specs=[pl.BlockSpec((2, SSM_LANES), lambda b, c: (0, 0)), blk], out_specs=blk,
                  scratch_shapes=[pltpu.VMEM((8, 8, SSM_LANES), F32), pltpu.VMEM((2, 8, SSM_LANES), F32)],
                  dims=("arbitrary", "arbitrary"))(abar, bu3)


def _scan_bwd(dx3, xs3, abar):
    B, S, _ = dx3.shape
    ch = min(S, SCAN_CHUNK)
    nc = S // ch
    blk = pl.BlockSpec((1, ch, 2 * SSM_LANES), lambda b, c: (b, nc - 1 - c, 0))

    def body(ab_ref, dx_ref, xs_ref, g_ref, da_ref, k_ref, carry_ref, acc_ref):
        b, c = pl.program_id(0), pl.program_id(1)
        _scan_consts(ab_ref[0:1, :], -ab_ref[1:2, :], k_ref, True)
        row = lax.broadcasted_iota(jnp.int32, (8, SSM_LANES), 0)

        @pl.when(c == 0)
        def _():
            carry_ref[...] = jnp.zeros_like(carry_ref)

        @pl.when((c == 0) & (b == 0))
        def _():
            acc_ref[...] = jnp.zeros_like(acc_ref)

        def step(i, carry):
            car, cai, ar_acc, ai_acc = carry
            base = pl.multiple_of((ch // 8 - 1 - i) * 8, 8)
            gr = dx_ref[0, pl.ds(base, 8), 0:SSM_LANES]
            gi = dx_ref[0, pl.ds(base, 8), SSM_LANES:]
            gr, gi = _scan_tile(gr, gi, k_ref, car, cai, True)
            g_ref[0, pl.ds(base, 8), 0:SSM_LANES] = gr
            g_ref[0, pl.ds(base, 8), SSM_LANES:] = gi
            nr = jnp.where(row == 7, car, pltpu.roll(gr, 7, 0))
            ni = jnp.where(row == 7, cai, pltpu.roll(gi, 7, 0))
            xr = xs_ref[0, pl.ds(base, 8), 0:SSM_LANES]
            xi = xs_ref[0, pl.ds(base, 8), SSM_LANES:]
            ar_acc = ar_acc + nr * xr + ni * xi
            ai_acc = ai_acc + ni * xr - nr * xi
            return (jnp.broadcast_to(gr[0:1], (8, SSM_LANES)), jnp.broadcast_to(gi[0:1], (8, SSM_LANES)), ar_acc, ai_acc)

        cr, ci, ar_acc, ai_acc = lax.fori_loop(0, ch // 8, step, (carry_ref[0], carry_ref[1], acc_ref[0], acc_ref[1]))
        carry_ref[0] = cr
        carry_ref[1] = ci
        acc_ref[0] = ar_acc
        acc_ref[1] = ai_acc
        da_ref[0:1, :] = jnp.sum(ar_acc, axis=0, keepdims=True)
        da_ref[1:2, :] = jnp.sum(ai_acc, axis=0, keepdims=True)

    return _pcall(body, name="scan_bwd", out_shape=(_sds(dx3.shape, F32), _sds((2, SSM_LANES), F32)), grid=(B, nc),
                  in_specs=[pl.BlockSpec((2, SSM_LANES), lambda b, c: (0, 0)), blk, blk],
                  out_specs=(blk, pl.BlockSpec((2, SSM_LANES), lambda b, c: (0, 0))),
                  scratch_shapes=[pltpu.VMEM((8, 8, SSM_LANES), F32), pltpu.VMEM((2, 8, SSM_LANES), F32),
                                  pltpu.VMEM((2, 8, SSM_LANES), F32)],
                  dims=("arbitrary", "arbitrary"))(abar, dx3, xs3)


US_BLOCK = (3 * ATT_WIDTH) // SSM_WIDTH


def _ssm_scan_fwd(proj3, abar, w_bu, w_c):
    B, S, _ = proj3.shape
    ch = min(S, SCAN_CHUNK)
    u_spec = pl.BlockSpec((1, ch, SSM_WIDTH), lambda b, c: (b, c, US_BLOCK))
    x_spec = pl.BlockSpec((1, ch, 2 * SSM_LANES), lambda b, c: (b, c, 0))
    y_spec = pl.BlockSpec((1, ch, SSM_WIDTH), lambda b, c: (b, c, 0))
    w_spec = pl.BlockSpec((SSM_WIDTH, 2 * SSM_LANES), lambda b, c: (0, 0))

    def body(ab_ref, u_ref, wb_ref, wc_ref, x_ref, y_ref, k_ref, carry_ref):
        _scan_consts(ab_ref[0:1, :], ab_ref[1:2, :], k_ref, False)

        @pl.when(pl.program_id(1) == 0)
        def _():
            carry_ref[...] = jnp.zeros_like(carry_ref)

        x_ref[0] = jnp.dot(u_ref[0], wb_ref[...], preferred_element_type=F32)

        def step(i, carry):
            base = pl.multiple_of(i * 8, 8)
            xr = x_ref[0, pl.ds(base, 8), 0:SSM_LANES]
            xi = x_ref[0, pl.ds(base, 8), SSM_LANES:]
            xr, xi = _scan_tile(xr, xi, k_ref, carry[0], carry[1], False)
            x_ref[0, pl.ds(base, 8), 0:SSM_LANES] = xr
            x_ref[0, pl.ds(base, 8), SSM_LANES:] = xi
            return (jnp.broadcast_to(xr[7:8], (8, SSM_LANES)), jnp.broadcast_to(xi[7:8], (8, SSM_LANES)))

        cr, ci = lax.fori_loop(0, ch // 8, step, (carry_ref[0], carry_ref[1]))
        carry_ref[0] = cr
        carry_ref[1] = ci
        y_ref[0] = lax.dot_general(x_ref[0].astype(BF16), wc_ref[...], NT_DIMS, preferred_element_type=F32)

    return _pcall(body, name="ssm_scan_fwd",
                  out_shape=(_sds((B, S, 2 * SSM_LANES), F32), _sds((B, S, SSM_WIDTH), F32)), grid=(B, S // ch),
                  in_specs=[pl.BlockSpec((2, SSM_LANES), lambda b, c: (0, 0)), u_spec, w_spec, w_spec],
                  out_specs=(x_spec, y_spec),
                  scratch_shapes=[pltpu.VMEM((8, 8, SSM_LANES), F32), pltpu.VMEM((2, 8, SSM_LANES), F32)],
                  dims=("arbitrary", "arbitrary"))(abar, proj3, w_bu, w_c)


def _ssm_scan_bwd(proj3, dy3, xs3, abar, w_bu, w_c, dsk):
    B, S, _ = proj3.shape
    ch = min(S, SCAN_CHUNK)
    nc = S // ch
    u_spec = pl.BlockSpec((1, ch, SSM_WIDTH), lambda b, c: (b, nc - 1 - c, US_BLOCK))
    x_spec = pl.BlockSpec((1, ch, 2 * SSM_LANES), lambda b, c: (b, nc - 1 - c, 0))
    y_spec = pl.BlockSpec((1, ch, SSM_WIDTH), lambda b, c: (b, nc - 1 - c, 0))
    w_spec = pl.BlockSpec((SSM_WIDTH, 2 * SSM_LANES), lambda b, c: (0, 0))
    ab_spec = pl.BlockSpec((2, SSM_LANES), lambda b, c: (0, 0))
    d_spec = pl.BlockSpec((1, SSM_WIDTH), lambda b, c: (0, 0))

    def body(ab_ref, u_ref, dy_ref, xs_ref, wb_ref, wc_ref, d_ref, du_ref, da_ref, dwb_ref, dwc_ref,
             g_ref, k_ref, carry_ref, acc_ref):
        b, c = pl.program_id(0), pl.program_id(1)
        _scan_consts(ab_ref[0:1, :], -ab_ref[1:2, :], k_ref, True)
        row = lax.broadcasted_iota(jnp.int32, (8, SSM_LANES), 0)

        @pl.when(c == 0)
        def _():
            carry_ref[...] = jnp.zeros_like(carry_ref)

        @pl.when((c == 0) & (b == 0))
        def _():
            acc_ref[...] = jnp.zeros_like(acc_ref)
            dwb_ref[...] = jnp.zeros_like(dwb_ref)
            dwc_ref[...] = jnp.zeros_like(dwc_ref)

        dy = dy_ref[0]
        dyb = dy.astype(BF16)
        g_ref[...] = jnp.dot(dyb, wc_ref[...], preferred_element_type=F32)

        def step(i, carry):
            car, cai, ar_acc, ai_acc = carry
            base = pl.multiple_of((ch // 8 - 1 - i) * 8, 8)
            gr = g_ref[pl.ds(base, 8), 0:SSM_LANES]
            gi = g_ref[pl.ds(base, 8), SSM_LANES:]
            gr, gi = _scan_tile(gr, gi, k_ref, car, cai, True)
            g_ref[pl.ds(base, 8), 0:SSM_LANES] = gr
            g_ref[pl.ds(base, 8), SSM_LANES:] = gi
            nr = jnp.where(row == 7, car, pltpu.roll(gr, 7, 0))
            ni = jnp.where(row == 7, cai, pltpu.roll(gi, 7, 0))
            xr = xs_ref[0, pl.ds(base, 8), 0:SSM_LANES]
            xi = xs_ref[0, pl.ds(base, 8), SSM_LANES:]
            ar_acc = ar_acc + nr * xr + ni * xi
            ai_acc = ai_acc + ni * xr - nr * xi
            return (jnp.broadcast_to(gr[0:1], (8, SSM_LANES)), jnp.broadcast_to(gi[0:1], (8, SSM_LANES)), ar_acc, ai_acc)

        cr, ci, ar_acc, ai_acc = lax.fori_loop(0, ch // 8, step, (carry_ref[0], carry_ref[1], acc_ref[0], acc_ref[1]))
        carry_ref[0] = cr
        carry_ref[1] = ci
        acc_ref[0] = ar_acc
        acc_ref[1] = ai_acc
        da_ref[0:1, :] = jnp.sum(ar_acc, axis=0, keepdims=True)
        da_ref[1:2, :] = jnp.sum(ai_acc, axis=0, keepdims=True)

        gb = g_ref[...].astype(BF16)
        du = lax.dot_general(gb, wb_ref[...], NT_DIMS, preferred_element_type=F32) + d_ref[...] * dy
        du_ref[0] = du.astype(BF16)
        dwb_ref[...] += lax.dot_general(u_ref[0], gb, TN_DIMS, preferred_element_type=F32)
        dwc_ref[...] += lax.dot_general(dyb, xs_ref[0].astype(BF16), TN_DIMS, preferred_element_type=F32)

    mat = _sds((SSM_WIDTH, 2 * SSM_LANES), F32)
    return _pcall(body, name="ssm_scan_bwd",
                  out_shape=(_sds((B, S, SSM_WIDTH), BF16), _sds((2, SSM_LANES), F32), mat, mat), grid=(B, nc),
                  in_specs=[ab_spec, u_spec, y_spec, x_spec, w_spec, w_spec, d_spec],
                  out_specs=(y_spec, ab_spec, w_spec, w_spec),
                  scratch_shapes=[pltpu.VMEM((ch, 2 * SSM_LANES), F32), pltpu.VMEM((8, 8, SSM_LANES), F32),
                                  pltpu.VMEM((2, 8, SSM_LANES), F32), pltpu.VMEM((2, 8, SSM_LANES), F32)],
                  dims=("arbitrary", "arbitrary"))(abar, proj3, dy3, xs3, w_bu, w_c, dsk)


GELU_K = math.sqrt(2.0 / math.pi)
GELU_C = 0.044715


def _gelu_parts(y):
    t = jnp.tanh(GELU_K * (y + GELU_C * y * y * y))
    return 0.5 * y * (1.0 + t), t


def _ssm_post(yc, us, dsk, wglu, bglu):
    T, N = yc.shape
    tm = min(T, 1024)
    row = pl.BlockSpec((tm, N), lambda i: (i, 0))
    vec = pl.BlockSpec((1, N), lambda i: (0, 0))
    mat = pl.BlockSpec((N, N), lambda i: (0, 0))

    def body(yc_ref, us_ref, d_ref, w_ref, b_ref, y_ref, s_ref):
        y = yc_ref[...] + d_ref[...] * us_ref[...]
        y_ref[...] = y
        z, _ = _gelu_parts(y)
        gl = jnp.dot(z.astype(BF16), w_ref[...], preferred_element_type=F32) + b_ref[...]
        s_ref[...] = (z * _sig(gl)).astype(BF16)

    return _pcall(body, name="ssm_post", out_shape=(_sds((T, N), F32), _sds((T, N), BF16)), grid=(T // tm,),
                  in_specs=[row, row, vec, mat, vec], out_specs=(row, row), dims=("parallel",))(yc, us, dsk, wglu, bglu)


def _ssm_post_bwd(y5, us, ds, dsk, wglu, bglu):
    T, N = y5.shape
    tm = min(T, 1024)
    row = pl.BlockSpec((tm, N), lambda i: (i, 0))
    vec = pl.BlockSpec((1, N), lambda i: (0, 0))
    mat = pl.BlockSpec((N, N), lambda i: (0, 0))

    def body(y_ref, us_ref, ds_ref, d_ref, w_ref, b_ref, dy_ref, dd_ref, db_ref, dw_ref):
        @pl.when(pl.program_id(0) == 0)
        def _():
            dd_ref[...] = jnp.zeros_like(dd_ref)
            db_ref[...] = jnp.zeros_like(db_ref)
            dw_ref[...] = jnp.zeros_like(dw_ref)

        y = y_ref[...]
        z, t = _gelu_parts(y)
        zb = z.astype(BF16)
        gl = jnp.dot(zb, w_ref[...], preferred_element_type=F32) + b_ref[...]
        sg = _sig(gl)
        ds = ds_ref[...]
        dgl = ds * z * sg * (1.0 - sg)
        dglb = dgl.astype(BF16)
        dz = ds * sg + lax.dot_general(dglb, w_ref[...], (((1,), (1,)), ((), ())), preferred_element_type=F32)
        dgelu = 0.5 * (1.0 + t) + 0.5 * y * (1.0 - t * t) * GELU_K * (1.0 + 3.0 * GELU_C * y * y)
        dy = dz * dgelu
        dy_ref[...] = dy
        dd_ref[...] += jnp.sum(dy * us_ref[...], axis=0, keepdims=True)
        db_ref[...] += jnp.sum(dgl, axis=0, keepdims=True)
        dw_ref[...] += lax.dot_general(zb, dglb, (((0,), (0,)), ((), ())), preferred_element_type=F32)

    return _pcall(body, name="ssm_post_bwd",
                  out_shape=(_sds((T, N), F32), _sds((1, N), F32), _sds((1, N), F32), _sds((N, N), F32)),
                  grid=(T // tm,), in_specs=[row, row, row, vec, mat, vec], out_specs=(row, vec, vec, mat),
                  dims=("arbitrary",))(y5, us, ds, dsk, wglu, bglu)


def _add_scaled_cast(a, b, s):
    T, N = a.shape
    tm = min(T, 1024)
    row = pl.BlockSpec((tm, N), lambda i: (i, 0))

    def body(a_ref, b_ref, s_ref, o_ref):
        o_ref[...] = (a_ref[...] + s_ref[...] * b_ref[...]).astype(BF16)

    return _pcall(body, name="add_scaled_cast", out_shape=_sds((T, N), BF16), grid=(T // tm,),
                  in_specs=[row, row, pl.BlockSpec((1, N), lambda i: (0, 0))], out_specs=row, dims=("parallel",))(a, b, s)


GATE_TILE = 256
GATE_ATT_BLOCK0 = (3 * ATT_WIDTH + SSM_WIDTH) // GATE_TILE
GATE_SSM_BLOCK0 = (3 * ATT_WIDTH + SSM_WIDTH + D_MODEL) // GATE_TILE


def _merge(proj, y_att, y_ssm, b_gate):
    T = proj.shape[0]
    tm = min(T, 1024)
    nj = D_MODEL // GATE_TILE
    ga = pl.BlockSpec((tm, GATE_TILE), lambda i, j: (i, GATE_ATT_BLOCK0 + j))
    gs = pl.BlockSpec((tm, GATE_TILE), lambda i, j: (i, GATE_SSM_BLOCK0 + j))
    yy = pl.BlockSpec((tm, GATE_TILE), lambda i, j: (i, j))
    ba = pl.BlockSpec((1, GATE_TILE), lambda i, j: (0, j))
    bs = pl.BlockSpec((1, GATE_TILE), lambda i, j: (0, nj + j))

    def body(ga_ref, gs_ref, ya_ref, ys_ref, ba_ref, bs_ref, o_ref):
        o_ref[...] = (_sig(ga_ref[...] + ba_ref[...]) * ya_ref[...]
                      + _sig(gs_ref[...] + bs_ref[...]) * ys_ref[...]).astype(BF16)

    return _pcall(body, name="merge", out_shape=_sds((T, D_MODEL), BF16), grid=(T // tm, nj),
                  in_specs=[ga, gs, yy, yy, ba, bs], out_specs=yy, dims=("parallel", "parallel"))(
        proj, proj, y_att, y_ssm, b_gate, b_gate)


def _merge_bwd(proj, y_att, y_ssm, b_gate, dmerged):
    T = proj.shape[0]
    tm = min(T, 1024)
    nj = D_MODEL // GATE_TILE
    ga = pl.BlockSpec((tm, GATE_TILE), lambda j, i: (i, GATE_ATT_BLOCK0 + j))
    gs = pl.BlockSpec((tm, GATE_TILE), lambda j, i: (i, GATE_SSM_BLOCK0 + j))
    yy = pl.BlockSpec((tm, GATE_TILE), lambda j, i: (i, j))
    ba = pl.BlockSpec((1, GATE_TILE), lambda j, i: (0, j))
    bs = pl.BlockSpec((1, GATE_TILE), lambda j, i: (0, nj + j))

    def body(ga_ref, gs_ref, ya_ref, ys_ref, ba_ref, bs_ref, dm_ref, dya_ref, dys_ref, dga_ref, dgs_ref, dba_ref, dbs_ref):
        @pl.when(pl.program_id(1) == 0)
        def _():
            dba_ref[...] = jnp.zeros_like(dba_ref)
            dbs_ref[...] = jnp.zeros_like(dbs_ref)

        dm = dm_ref[...].astype(F32)
        sa = _sig(ga_ref[...] + ba_ref[...])
        ss = _sig(gs_ref[...] + bs_ref[...])
        dya_ref[...] = (dm * sa).astype(BF16)
        dys_ref[...] = (dm * ss).astype(BF16)
        dga = dm * ya_ref[...] * sa * (1.0 - sa)
        dgs = dm * ys_ref[...] * ss * (1.0 - ss)
        dga_ref[...] = dga.astype(BF16)
        dgs_ref[...] = dgs.astype(BF16)
        dba_ref[...] += jnp.sum(dga, axis=0, keepdims=True)
        dbs_ref[...] += jnp.sum(dgs, axis=0, keepdims=True)

    big = _sds((T, D_MODEL), BF16)
    vec = _sds((1, D_MODEL), F32)
    return _pcall(body, name="merge_bwd", out_shape=(big, big, big, big, vec, vec), grid=(nj, T // tm),
                  in_specs=[ga, gs, yy, yy, ba, bs, yy], out_specs=(yy, yy, yy, yy, ba, ba),
                  dims=("arbitrary", "arbitrary"))(proj, proj, y_att, y_ssm, b_gate, b_gate, dmerged)


CONV_TILE = 256


def _conv_pre(a, w_ref, b_ref, row):
    conv = b_ref[...] + w_ref[0:1, :] * a
    shifted = []
    for j in (1, 2):
        sh = jnp.where(row >= j, pltpu.roll(a, j, 0), 0.0)
        shifted.append(sh)
        conv = conv + w_ref[j:j + 1, :] * sh
    return conv, shifted


def _conv_act(up3, w_conv, b_conv):
    B, S, _ = up3.shape
    nj = D_FF // CONV_TILE
    a_spec = pl.BlockSpec((1, S, CONV_TILE), lambda b, j: (b, 0, j))
    v_spec = pl.BlockSpec((1, S, CONV_TILE), lambda b, j: (b, 0, nj + j))
    w_spec = pl.BlockSpec((3, CONV_TILE), lambda b, j: (0, j))
    b_spec = pl.BlockSpec((1, CONV_TILE), lambda b, j: (0, j))

    def body(a_ref, v_ref, w_ref, b_ref, o_ref):
        a = a_ref[0].astype(F32)
        row = lax.broadcasted_iota(jnp.int32, a.shape, 0)
        conv, _ = _conv_pre(a, w_ref, b_ref, row)
        o_ref[0] = (conv * _sig(conv) * v_ref[0]).astype(BF16)

    return _pcall(body, name="conv_act", out_shape=_sds((B, S, D_FF), BF16), grid=(B, nj),
                  in_specs=[a_spec, v_spec, w_spec, b_spec], out_specs=a_spec, dims=("parallel", "parallel"))(
        up3, up3, w_conv, b_conv)


def _conv_bwd(up3, dact3, w_conv, b_conv):
    B, S, _ = up3.shape
    nj = D_FF // CONV_TILE
    a_spec = pl.BlockSpec((1, S, CONV_TILE), lambda j, b: (b, 0, j))
    v_spec = pl.BlockSpec((1, S, CONV_TILE), lambda j, b: (b, 0, nj + j))
    o_spec = pl.BlockSpec((2, 1, S, CONV_TILE), lambda j, b: (0, b, 0, j))
    w_spec = pl.BlockSpec((3, CONV_TILE), lambda j, b: (0, j))
    b_spec = pl.BlockSpec((1, CONV_TILE), lambda j, b: (0, j))

    def body(a_ref, v_ref, d_ref, w_ref, b_ref, dup_ref, dw_ref, db_ref):
        @pl.when(pl.program_id(1) == 0)
        def _():
            dw_ref[...] = jnp.zeros_like(dw_ref)
            db_ref[...] = jnp.zeros_like(db_ref)

        a = a_ref[0].astype(F32)
        d = d_ref[0].astype(F32)
        row = lax.broadcasted_iota(jnp.int32, a.shape, 0)
        conv, shifted = _conv_pre(a, w_ref, b_ref, row)
        sg = _sig(conv)
        dup_ref[1, 0] = (d * conv * sg).astype(BF16)
        dconv = d * v_ref[0] * (sg * (1.0 + conv * (1.0 - sg)))
        da = w_ref[0:1, :] * dconv
        for j in (1, 2):
            da = da + w_ref[j:j + 1, :] * jnp.where(row < S - j, pltpu.roll(dconv, S - j, 0), 0.0)
        dup_ref[0, 0] = da.astype(BF16)
        db_ref[...] += jnp.sum(dconv, axis=0, keepdims=True)
        dw_ref[0:1, :] += jnp.sum(dconv * a, axis=0, keepdims=True)
        dw_ref[1:2, :] += jnp.sum(dconv * shifted[0], axis=0, keepdims=True)
        dw_ref[2:3, :] += jnp.sum(dconv * shifted[1], axis=0, keepdims=True)

    return _pcall(body, name="conv_bwd",
                  out_shape=(_sds((2, B, S, D_FF), BF16), _sds((3, D_FF), F32), _sds((1, D_FF), F32)),
                  grid=(nj, B), in_specs=[a_spec, v_spec, a_spec, w_spec, b_spec],
                  out_specs=(o_spec, w_spec, b_spec), dims=("arbitrary", "arbitrary"))(up3, up3, dact3, w_conv, b_conv)


def _rows_tile(r, cap=640):
    for t in range(min(r, cap) - min(r, cap) % 8, 7, -8):
        if r % t == 0:
            return t
    return r


def _add2(a, b, out_dtype):
    R, N = a.shape
    tr = _rows_tile(R)
    spec = pl.BlockSpec((tr, N), lambda i: (i, 0))

    def body(a_ref, b_ref, o_ref):
        o_ref[...] = (a_ref[...] + b_ref[...]).astype(out_dtype)

    return _pcall(body, name="add2", out_shape=_sds((R, N), out_dtype), grid=(R // tr,), in_specs=[spec, spec],
                  out_specs=spec, dims=("parallel",))(a, b)


def _sum_slots(q, name):
    n, R, N = q.shape
    tr = _rows_tile(R)

    def body(q_ref, o_ref):
        acc = q_ref[0].astype(F32)
        for s in range(1, n):
            acc = acc + q_ref[s].astype(F32)
        o_ref[...] = acc

    return _pcall(body, name=name, out_shape=_sds((R, N), F32), grid=(R // tr,),
                  in_specs=[pl.BlockSpec((n, tr, N), lambda i: (0, i, 0))], out_specs=pl.BlockSpec((tr, N), lambda i: (i, 0)),
                  dims=("parallel",))(q)


NATIVE = (("b_re", 16, 1024), ("b_im", 16, 1024), ("c_re", 16, 1024), ("c_im", 16, 1024), ("g_mix", 1, 1024),
          ("b_att", 1, 1024), ("b_ssm", 1, 1024), ("a_re", 1, 1024), ("a_im", 1, 1024), ("log_dt", 1, 128),
          ("d_skip", 1, 256), ("b_glu", 1, 256), ("g_ffn", 1, 1024), ("g_final", 1, 1024), ("b_conv", 1, 2048),
          ("w_conv", 3, 2048))
N_MOD = 6


def _native_rows():
    starts, r = {}, 0
    for name, rows, cols in NATIVE:
        starts[name] = r
        r += rows * (-(-cols // LANES))
    n_sum = -(-r // 8) * 8
    return starts, n_sum


def _pack_small(native, dmods):
    starts, n_sum = _native_rows()
    B = dmods[0].shape[0]
    total = n_sum + 8 * N_MOD

    def body(*refs):
        xs, ms, o_ref = refs[:len(NATIVE)], refs[len(NATIVE):len(NATIVE) + N_MOD], refs[-1]
        o_ref[...] = jnp.zeros_like(o_ref)
        for (name, rows, cols), x_ref in zip(NATIVE, xs):
            chunks = -(-cols // LANES)
            if chunks == 1 and rows % 8 == 0:
                o_ref[starts[name]:starts[name] + rows, 0:cols] = x_ref[...]
                continue
            for i in range(rows):
                for q in range(chunks):
                    wd = min(LANES, cols - q * LANES)
                    r = starts[name] + i * chunks + q
                    o_ref[r:r + 1, 0:wd] = x_ref[i:i + 1, q * LANES:q * LANES + wd]
        for k, m_ref in enumerate(ms):
            for b in range(B):
                o_ref[n_sum + 8 * k + b:n_sum + 8 * k + b + 1, :] = m_ref[b]

    return _pcall(body, name="pack_small", out_shape=_sds((total, LANES), F32))(
        *[native[n] for n, _, _ in NATIVE], *dmods)


def _sum_unpack_small(gathered, B):
    starts, n_sum = _native_rows()
    nd = gathered.shape[0]

    def body(*refs):
        g_ref, outs, dm_ref, acc = refs[0], refs[1:1 + len(NATIVE)], refs[1 + len(NATIVE)], refs[-1]
        s = g_ref[0, 0:n_sum, :]
        for d in range(1, nd):
            s = s + g_ref[d, 0:n_sum, :]
        acc[...] = s
        for (name, rows, cols), o_ref in zip(NATIVE, outs):
            chunks = -(-cols // LANES)
            if chunks == 1 and rows % 8 == 0:
                o_ref[...] = acc[starts[name]:starts[name] + rows, 0:cols]
                continue
            for i in range(rows):
                for q in range(chunks):
                    wd = min(LANES, cols - q * LANES)
                    r = starts[name] + i * chunks + q
                    o_ref[i:i + 1, q * LANES:q * LANES + wd] = acc[r:r + 1, 0:wd]
        for d in range(nd):
            for k in range(N_MOD):
                dm_ref[d, :, k * D_MODEL:(k + 1) * D_MODEL] = g_ref[d, n_sum + 8 * k:n_sum + 8 * k + B, :]

    out_shape = tuple(_sds((rows, cols), F32) for _, rows, cols in NATIVE) + (_sds((nd, B, N_MOD * D_MODEL), F32),)
    res = _pcall(body, name="sum_unpack_small", out_shape=out_shape,
                 scratch_shapes=[pltpu.VMEM((n_sum, LANES), F32)])(gathered)
    return {n: r for (n, _, _), r in zip(NATIVE, res[:-1])}, res[-1]


def _small_from_native(nat):
    lanes3 = lambda a: a.reshape(SSM_GROUP_CH, SSM_GROUPS, SSM_STATE)
    return dict(
        g_mix=nat["g_mix"].reshape(D_MODEL), b_gate=jnp.concatenate([nat["b_att"], nat["b_ssm"]], axis=1).reshape(2 * D_MODEL),
        a_re=nat["a_re"].reshape(SSM_GROUPS, SSM_STATE), a_im=nat["a_im"].reshape(SSM_GROUPS, SSM_STATE),
        log_dt=nat["log_dt"][0, :SSM_GROUPS], b_re=_groups_from_lanes(nat["b_re"]), b_im=_groups_from_lanes(nat["b_im"]),
        c_re=lanes3(nat["c_re"]).transpose(1, 0, 2), c_im=lanes3(nat["c_im"]).transpose(1, 0, 2),
        d_skip=nat["d_skip"].reshape(SSM_WIDTH), b_glu=nat["b_glu"].reshape(SSM_WIDTH), g_ffn=nat["g_ffn"].reshape(D_MODEL),
        w_conv=nat["w_conv"], b_conv=nat["b_conv"].reshape(D_FF), g_final=nat["g_final"].reshape(D_MODEL))


def _adamw_multi(params):
    n = len(params)
    bc1 = 1.0 - ADAM_B1 ** ADAM_STEP
    bc2 = 1.0 - ADAM_B2 ** ADAM_STEP

    def body(*refs):
        ins, outs = refs[:4 * n], refs[4 * n:]
        for i in range(n):
            w_ref, g_ref, m_ref, v_ref = ins[4 * i:4 * i + 4]
            d_ref, nm_ref, nv_ref = outs[3 * i:3 * i + 3]
            g = g_ref[...]
            m = ADAM_B1 * m_ref[...] + (1.0 - ADAM_B1) * g
            v = ADAM_B2 * v_ref[...] + (1.0 - ADAM_B2) * (g * g)
            nm_ref[...] = m
            nv_ref[...] = v
            d_ref[...] = -ADAM_LR * ((m / bc1) / (jnp.sqrt(v / bc2) + ADAM_EPS) + ADAM_WD * w_ref[...])

    flat = [a for p in params for a in p]
    out_shape = tuple(_sds(p[0].shape, F32) for p in params for _ in range(3))
    res = _pcall(body, name="adamw_small", out_shape=out_shape)(*flat)
    return [tuple(res[3 * i:3 * i + 3]) for i in range(n)]


def _adamw(w, g, m, v, name):
    R, N = w.shape
    tr = _rows_tile(R) if R * N * 4 > (1 << 20) else R
    tr = min(tr, 256) if R % 256 == 0 and R > 256 else tr
    spec = pl.BlockSpec((tr, N), lambda i: (i, 0))
    bc1 = 1.0 - ADAM_B1 ** ADAM_STEP
    bc2 = 1.0 - ADAM_B2 ** ADAM_STEP

    def body(w_ref, g_ref, m_ref, v_ref, d_ref, nm_ref, nv_ref):
        g = g_ref[...]
        m = ADAM_B1 * m_ref[...] + (1.0 - ADAM_B1) * g
        v = ADAM_B2 * v_ref[...] + (1.0 - ADAM_B2) * (g * g)
        nm_ref[...] = m
        nv_ref[...] = v
        d_ref[...] = -ADAM_LR * ((m / bc1) / (jnp.sqrt(v / bc2) + ADAM_EPS) + ADAM_WD * w_ref[...])

    shp = _sds((R, N), F32)
    return _pcall(body, name=name, out_shape=(shp, shp, shp), grid=(R // tr,), in_specs=[spec] * 4,
                  out_specs=(spec, spec, spec), dims=("parallel",))(w, g, m, v)


_GROUP_MASKS = {
    "all": [(dx, dy, dc) for dx in (0, 1) for dy in (0, 1) for dc in (0, 1) if (dx, dy, dc) != (0, 0, 0)],
    "xy": [(1, 0, 0), (0, 1, 0), (1, 1, 0)],
    "c": [(0, 0, 1)],
}
_GROUP_SLOTS = {"all": 8, "xy": 4, "c": 2}


def _group_slot(group, x, y, c):
    return {"all": 4 * x + 2 * y + c, "xy": 2 * x + y, "c": c}[group]


def _flip(v, d):
    return 1 - v if d else v


def _exchange(arr, group, mode, name):
    return _exchange_list([arr], group, mode, name)[0]


def _exchange_list(arrs, group, mode, name):
    masks = _GROUP_MASKS[group]
    n = len(masks)
    na = len(arrs)
    out_shapes, halves, bounce = [], [], []
    for arr in arrs:
        if mode == "gather":
            out_shapes.append((_GROUP_SLOTS[group],) + arr.shape)
            bounce.append(pltpu.VMEM(arr.shape, arr.dtype))
        elif mode == "scatter":
            assert arr.shape[0] == _GROUP_SLOTS[group]
            out_shapes.append(arr.shape)
            bounce.append(pltpu.VMEM(arr.shape[1:], arr.dtype))
        elif mode == "swap":
            assert group == "c"
            out_shapes.append(arr.shape)
        else:
            assert group == "c"
            halves.append(arr.shape[1] // 2)
            out_shapes.append((arr.shape[0], arr.shape[1] // 2, arr.shape[2]))
    has_local = mode in ("gather", "scatter")

    def body(*refs):
        x_refs, o_refs = refs[:na], refs[na:2 * na]
        send_sems, recv_sems = refs[2 * na], refs[2 * na + 1]
        x, y, c = lax.axis_index("x"), lax.axis_index("y"), lax.axis_index("c")
        me = _group_slot(group, x, y, c)
        if has_local:
            local_sems = refs[2 * na + 2]
            bufs = refs[2 * na + 3:]
            loads = []
            for i in range(na):
                src = x_refs[i] if mode == "gather" else x_refs[i].at[me]
                loads.append(pltpu.make_async_copy(src, bufs[i], local_sems.at[2 * i]))
                loads[-1].start()
        copies = []
        for i in range(na):
            x_ref, o_ref = x_refs[i], o_refs[i]
            for k, (dx, dy, dc) in enumerate(masks):
                px, py, pc = _flip(x, dx), _flip(y, dy), _flip(c, dc)
                if mode == "gather":
                    src, dst = x_ref, o_ref.at[me]
                elif mode == "scatter":
                    src, dst = x_ref.at[_group_slot(group, px, py, pc)], o_ref.at[me]
                elif mode == "swap":
                    src, dst = x_ref, o_ref
                else:
                    src, dst = x_ref.at[:, pl.ds(pl.multiple_of(pc * halves[i], 8), halves[i]), :], o_ref
                cp = pltpu.make_async_remote_copy(src_ref=src, dst_ref=dst, send_sem=send_sems.at[i * n + k],
                                                  recv_sem=recv_sems.at[i * n + k], device_id=(px, py, pc),
                                                  device_id_type=pl.DeviceIdType.MESH)
                cp.start()
                copies.append(cp)
        if has_local:
            stores = []
            for i in range(na):
                loads[i].wait()
                stores.append(pltpu.make_async_copy(bufs[i], o_refs[i].at[me], local_sems.at[2 * i + 1]))
                stores[-1].start()
        for cp in copies:
            cp.wait()
        if has_local:
            for st in stores:
                st.wait()

    anyspec = pl.BlockSpec(memory_space=pl.ANY)
    scratch = [pltpu.SemaphoreType.DMA((n * na,)), pltpu.SemaphoreType.DMA((n * na,))]
    if has_local:
        scratch += [pltpu.SemaphoreType.DMA((2 * na,))] + bounce
    outs = pl.pallas_call(body, name=name, out_shape=tuple(_sds(s, a.dtype) for s, a in zip(out_shapes, arrs)),
                          in_specs=[anyspec] * na, out_specs=tuple([anyspec] * na), scratch_shapes=scratch,
                          compiler_params=pltpu.CompilerParams(vmem_limit_bytes=V7X_VMEM_LIMIT_BYTES))(*arrs)
    return list(outs)


def _gather_weights(shards, name):
    na = len(shards)
    masks = _GROUP_MASKS["xy"]
    n = len(masks)

    def body(*refs):
        x_refs, o_refs = refs[:na], refs[na:2 * na]
        send_sems, recv_sems, local_sems = refs[2 * na:2 * na + 3]
        bufs = refs[2 * na + 3:]
        x, y, c = lax.axis_index("x"), lax.axis_index("y"), lax.axis_index("c")
        me = 2 * x + y
        sibling = (x, y, 1 - c)
        loads = []
        for i in range(na):
            loads.append(pltpu.make_async_copy(x_refs[i], bufs[i], local_sems.at[2 * i]))
            loads[-1].start()

        def half_of(i, slot, cc):
            h = shards[i].shape[0] // 2
            return o_refs[i].at[slot, pl.ds(pl.multiple_of(cc * h, 8), h), :]

        def src_half(i, cc):
            h = shards[i].shape[0] // 2
            return x_refs[i].at[pl.ds(pl.multiple_of(cc * h, 8), h), :]

        sends = []
        for i in range(na):
            for k, (dx, dy, _) in enumerate(masks):
                cp = pltpu.make_async_remote_copy(src_ref=src_half(i, c), dst_ref=half_of(i, me, c),
                                                  send_sem=send_sems.at[i * 2 * n + k], recv_sem=recv_sems.at[i * 2 * n + k],
                                                  device_id=(_flip(x, dx), _flip(y, dy), c),
                                                  device_id_type=pl.DeviceIdType.MESH)
                cp.start()
                sends.append(cp)
        stores = []
        for i in range(na):
            loads[i].wait()
            stores.append(pltpu.make_async_copy(bufs[i], o_refs[i].at[me], local_sems.at[2 * i + 1]))
            stores[-1].start()
        for i in range(na):
            for k, (dx, dy, _) in enumerate(masks):
                slot = 2 * _flip(x, dx) + _flip(y, dy)
                landed = pltpu.make_async_remote_copy(src_ref=src_half(i, c), dst_ref=half_of(i, slot, c),
                                                      send_sem=send_sems.at[i * 2 * n + k],
                                                      recv_sem=recv_sems.at[i * 2 * n + k], device_id=sibling,
                                                      device_id_type=pl.DeviceIdType.MESH)
                landed.wait_recv()
                fwd = pltpu.make_async_remote_copy(src_ref=half_of(i, slot, c), dst_ref=half_of(i, slot, c),
                                                   send_sem=send_sems.at[i * 2 * n + n + k],
                                                   recv_sem=recv_sems.at[i * 2 * n + n + k], device_id=sibling,
                                                   device_id_type=pl.DeviceIdType.MESH)
                fwd.start()
                sends.append(fwd)
        for i in range(na):
            for k, (dx, dy, _) in enumerate(masks):
                slot = 2 * _flip(x, dx) + _flip(y, dy)
                pltpu.make_async_remote_copy(src_ref=half_of(i, slot, 1 - c), dst_ref=half_of(i, slot, 1 - c),
                                             send_sem=send_sems.at[i * 2 * n + n + k],
                                             recv_sem=recv_sems.at[i * 2 * n + n + k], device_id=sibling,
                                             device_id_type=pl.DeviceIdType.MESH).wait_recv()
        for cp in sends:
            cp.wait_send()
        for st in stores:
            st.wait()

    anyspec = pl.BlockSpec(memory_space=pl.ANY)
    scratch = [pltpu.SemaphoreType.DMA((2 * n * na,)), pltpu.SemaphoreType.DMA((2 * n * na,)),
               pltpu.SemaphoreType.DMA((2 * na,))] + [pltpu.VMEM(s.shape, s.dtype) for s in shards]
    outs = pl.pallas_call(body, name=name, out_shape=tuple(_sds((N_XY,) + s.shape, s.dtype) for s in shards),
                          in_specs=[anyspec] * na, out_specs=tuple([anyspec] * na), scratch_shapes=scratch,
                          compiler_params=pltpu.CompilerParams(vmem_limit_bytes=V7X_VMEM_LIMIT_BYTES))(*shards)
    return list(outs)


def _pair_add(g, theirs, core, name):
    n4, h2, w = g.shape
    h = h2 // 2
    tr = _rows_tile(h)
    nb = h // tr

    def body(c_ref, g_ref, t_ref, o_ref):
        o_ref[...] = (g_ref[...] + t_ref[...]).astype(BF16)

    grid_spec = pltpu.PrefetchScalarGridSpec(
        num_scalar_prefetch=1, grid=(n4, nb),
        in_specs=[pl.BlockSpec((None, tr, w), lambda j, i, c_ref: (j, c_ref[0] * nb + i, 0)),
                  pl.BlockSpec((None, tr, w), lambda j, i, c_ref: (j, i, 0))],
        out_specs=pl.BlockSpec((None, tr, w), lambda j, i, c_ref: (j, i, 0)))
    return pl.pallas_call(body, name=name, out_shape=_sds((n4, h, w), BF16), grid_spec=grid_spec,
                          compiler_params=pltpu.CompilerParams(vmem_limit_bytes=V7X_VMEM_LIMIT_BYTES,
                                                               dimension_semantics=("parallel", "parallel")))(core, g, theirs)


BIG = (("w_proj_att", (ATT_WIDTH, D_MODEL), 1), ("w_proj_ssm", (SSM_WIDTH, D_MODEL), 1),
       ("w_glu", (SSM_WIDTH, SSM_WIDTH), 0))
DIRECT = (("w_in", True), ("w_up", True), ("w_down", False), ("w_out", False))
N_XY = 4


def _big_rows(shape):
    return shape[0] * shape[1] // N_XY // LANES


FLAT_ROWS = sum(_big_rows(s) for _, s, _ in BIG)


def _shard_shape(shape, axis):
    return (shape[0] // N_XY, shape[1]) if axis == 0 else (shape[0], shape[1] // N_XY)


def _flatten_shards(shards):
    return jnp.concatenate([shards[n].reshape(_big_rows(s), LANES) for n, s, _ in BIG], axis=0)


def _unflatten_shard(flat):
    out, r = {}, 0
    for n, s, ax in BIG:
        k = _big_rows(s)
        out[n] = flat[r:r + k].reshape(_shard_shape(s, ax))
        r += k
    return out


def _unflatten_full(flat4):
    out, r = {}, 0
    for n, s, ax in BIG:
        k = _big_rows(s)
        sh = _shard_shape(s, ax)
        t = flat4[:, r:r + k].reshape((N_XY,) + sh)
        out[n] = t.reshape(s) if ax == 0 else t.transpose(1, 0, 2).reshape(s)
        r += k
    return out


def _flatten_full(full):
    parts = []
    for n, s, ax in BIG:
        sh = _shard_shape(s, ax)
        t = full[n]
        t = t.reshape((N_XY,) + sh) if ax == 0 else t.reshape(s[0], N_XY, sh[1]).transpose(1, 0, 2)
        parts.append(t.reshape(N_XY, _big_rows(s), LANES))
    return jnp.concatenate(parts, axis=1)


def _pack_rows(arrs):
    rows, counts = [], []
    for a in arrs:
        f = a.reshape(-1)
        k = -(-f.shape[0] // LANES)
        rows.append(jnp.pad(f, (0, k * LANES - f.shape[0])).reshape(k, LANES))
        counts.append(k)
    return jnp.concatenate(rows, axis=0), counts


def _unpack_rows(buf, shapes):
    out, r = [], 0
    for s in shapes:
        size = int(np.prod(s))
        k = -(-size // LANES)
        out.append(buf[r:r + k].reshape(-1)[:size].reshape(s))
        r += k
    return out


def _lanes_from_groups(a):
    return a.transpose(2, 0, 1).reshape(SSM_GROUP_CH, SSM_LANES)


def _groups_from_lanes(a):
    return a.reshape(SSM_GROUP_CH, SSM_GROUPS, SSM_STATE).transpose(1, 2, 0)


def _local_step(x3, mod, tgt3, W, P):
    B, S, _ = x3.shape
    T = B * S
    seq_blocks = S // ATT_BLOCK
    sh1, sc1, gt1, sh2, sc2, gt2 = [m.reshape(B, 1, D_MODEL) for m in jnp.split(mod, 6, axis=-1)]
    g_mix, g_ffn, g_final = P["g_mix"].reshape(1, D_MODEL), P["g_ffn"].reshape(1, D_MODEL), P["g_final"].reshape(1, D_MODEL)
    b_gate = P["b_gate"].reshape(1, 2 * D_MODEL)
    d_skip, b_glu = P["d_skip"].reshape(1, SSM_WIDTH), P["b_glu"].reshape(1, SSM_WIDTH)
    w_conv, b_conv = P["w_conv"], P["b_conv"].reshape(1, D_FF)

    u1 = _norm_mod(x3, g_mix, sc1, sh1).reshape(T, D_MODEL)
    proj = _mm(u1, W["w_in_t"], tb=True, name="mm_proj", out_dtype=BF16)
    proj3 = proj.reshape(B, S, IN_WIDTH)
    us = proj[:, 3 * ATT_WIDTH:3 * ATT_WIDTH + SSM_WIDTH]
    o_att3, lse4 = _attention_fwd(proj3, seq_blocks)
    o_att = o_att3.reshape(T, ATT_WIDTH)
    y_att = _mm(o_att, W["w_proj_att"], name="mm_proj_att", out_dtype=BF16)

    lr = P["a_re"].reshape(1, SSM_LANES)
    li = P["a_im"].reshape(1, SSM_LANES)
    ldt = jnp.repeat(P["log_dt"], SSM_STATE).reshape(1, SSM_LANES)
    br, bi = _lanes_from_groups(P["b_re"]), _lanes_from_groups(P["b_im"])
    cr = P["c_re"].transpose(1, 0, 2).reshape(SSM_GROUP_CH, SSM_LANES)
    ci = P["c_im"].transpose(1, 0, 2).reshape(SSM_GROUP_CH, SSM_LANES)
    abar, w_bu, w_c = _ssm_params(lr, li, ldt, br, bi, cr, ci)
    xs3, y_core3 = _ssm_scan_fwd(proj3, abar, w_bu, w_c)
    y5, s_out = _ssm_post(y_core3.reshape(T, SSM_WIDTH), us, d_skip, W["w_glu"], b_glu)
    y_ssm = _mm(s_out, W["w_proj_ssm"], name="mm_proj_ssm", out_dtype=BF16)

    merged = _merge(proj, y_att, y_ssm, b_gate)
    mix = _mm(merged, W["w_out"], name="mm_out", out_dtype=BF16)
    mix3 = mix.reshape(B, S, D_MODEL)

    h1, u2 = _resid_norm_mod(x3, mix3, gt1, g_ffn, sc2, sh2)
    u2 = u2.reshape(T, D_MODEL)
    up3 = _mm(u2, W["w_up_t"], tb=True, name="mm_up", out_dtype=BF16).reshape(B, S, 2 * D_FF)
    act = _conv_act(up3, w_conv, b_conv).reshape(T, D_FF)
    ffn3 = _mm(act, W["w_down"], name="mm_down", out_dtype=BF16).reshape(B, S, D_MODEL)
    dh2, dffn, dgt2, dg_final, loss = _final_loss(h1, ffn3, tgt3, gt2, g_final)

    dffn = dffn.reshape(T, D_MODEL)
    gw = {}
    gw["w_down"] = _mm(act, dffn, ta=True, name="mm_dw_down")
    dact3 = _mm(dffn, W["w_down"], tb=True, name="mm_dact", out_dtype=BF16).reshape(B, S, D_FF)
    dup3, dw_conv, db_conv = _conv_bwd(up3, dact3, w_conv, b_conv)
    dup = dup3.reshape(2, T, D_FF)
    gw["w_up_t"] = _mm(dup, u2, ta=True, name="mm_dw_up")
    du2 = _mm(dup, W["w_up_t"], name="mm_du2", out_dtype=BF16).reshape(B, S, D_MODEL)
    dh1, dsh2, dsc2, dg_ffn, dgt1, dmix = _norm_bwd(h1, du2, dh2, g_ffn, sc2, "norm_bwd2", mix3=mix3, gt=gt1)

    dmix = dmix.reshape(T, D_MODEL)
    gw["w_out"] = _mm(merged, dmix, ta=True, name="mm_dw_out")
    dmerged = _mm(dmix, W["w_out"], tb=True, name="mm_dmerged", out_dtype=BF16)
    dy_att, dy_ssm, dga, dgs, db_att, db_ssm = _merge_bwd(proj, y_att, y_ssm, b_gate, dmerged)

    gw["w_proj_ssm"] = _mm(s_out, dy_ssm, ta=True, name="mm_dw_proj_ssm")
    ds_out = _mm(dy_ssm, W["w_proj_ssm"], tb=True, name="mm_ds_out")
    dy5, dd_skip, db_glu, dw_glu = _ssm_post_bwd(y5, us, ds_out, d_skip, W["w_glu"], b_glu)
    gw["w_glu"] = dw_glu
    dus3, dab, dwbu, dwc = _ssm_scan_bwd(proj3, dy5.reshape(B, S, SSM_WIDTH), xs3, abar, w_bu, w_c, d_skip)
    dus = dus3.reshape(T, SSM_WIDTH)
    dlr, dli, dldt, dbr, dbi, dcr, dci = _ssm_params_bwd(lr, li, ldt, br, bi, dab, dwbu, dwc)

    gw["w_proj_att"] = _mm(o_att, dy_att, ta=True, name="mm_dw_proj_att")
    do_att = _mm(dy_att, W["w_proj_att"], tb=True, out_dtype=BF16, name="mm_do_att")
    dq3, dk3, dv3 = _attention_bwd(proj3, do_att.reshape(B, S, ATT_WIDTH), o_att3, lse4, seq_blocks)
    dproj = jnp.concatenate([t.reshape(T, ATT_WIDTH) for t in (dq3, dk3, dv3)] + [dus, dga, dgs], axis=1)
    gw["w_in_t"] = _mm(dproj, u1, ta=True, name="mm_dw_in")
    du1 = _mm(dproj, W["w_in_t"], name="mm_du1", out_dtype=BF16).reshape(B, S, D_MODEL)
    dx, dsh1, dsc1, dg_mix = _norm_bwd(x3, du1, dh1, g_mix, sc1, "norm_bwd1")

    dmods = [dsh1, dsc1, dgt1, dsh2, dsc2, dgt2]
    native = dict(g_mix=dg_mix, b_att=db_att, b_ssm=db_ssm, a_re=dlr, a_im=dli, log_dt=dldt, b_re=dbr, b_im=dbi, c_re=dcr,
                  c_im=dci, d_skip=dd_skip, b_glu=db_glu, g_ffn=dg_ffn, w_conv=dw_conv, b_conv=db_conv, g_final=dg_final)
    return loss, dx, dmods, gw, native


WEIGHTS = ['w_ada', 'b_ada', 'g_mix', 'w_in', 'b_gate', 'a_re', 'a_im', 'log_dt', 'b_re', 'b_im', 'c_re', 'c_im', 'd_skip',
           'w_glu', 'b_glu', 'w_proj_att', 'w_proj_ssm', 'w_out', 'g_ffn', 'w_up', 'w_conv', 'b_conv', 'w_down', 'g_final']
SMALL = ['g_mix', 'b_gate', 'a_re', 'a_im', 'log_dt', 'b_re', 'b_im', 'c_re', 'c_im', 'd_skip', 'b_glu', 'g_ffn', 'w_conv',
         'b_conv', 'g_final']


def kernel(x, c, w_ada, b_ada, g_mix, w_in, b_gate, a_re, a_im, log_dt, b_re, b_im, c_re, c_im, d_skip, w_glu, b_glu, w_proj_att, w_proj_ssm, w_out, g_ffn, w_up, w_conv, b_conv, w_down, g_final, loss_target, m_w_ada, m_b_ada, m_g_mix, m_w_in, m_b_gate, m_a_re, m_a_im, m_log_dt, m_b_re, m_b_im, m_c_re, m_c_im, m_d_skip, m_w_glu, m_b_glu, m_w_proj_att, m_w_proj_ssm, m_w_out, m_g_ffn, m_w_up, m_w_conv, m_b_conv, m_w_down, m_g_final, v_w_ada, v_b_ada, v_g_mix, v_w_in, v_b_gate, v_a_re, v_a_im, v_log_dt, v_b_re, v_b_im, v_c_re, v_c_im, v_d_skip, v_w_glu, v_b_glu, v_w_proj_att, v_w_proj_ssm, v_w_out, v_g_ffn, v_w_up, v_w_conv, v_b_conv, v_w_down, v_g_final):
    args = dict(locals())
    w = {n: args[n] for n in WEIGHTS}
    m = {n: args["m_" + n] for n in WEIGHTS}
    v = {n: args["v_" + n] for n in WEIGHTS}
    B, S, _ = x.shape
    ix, iy, ic = lax.axis_index("x"), lax.axis_index("y"), lax.axis_index("c")
    chip = 2 * ix + iy
    half = FLAT_ROWS // 2
    ada_cols = w_ada.shape[2]

    c_all = _exchange(c, "all", "gather", "gather_c").reshape(8 * B, D_MODEL)
    b_cols = lax.dynamic_slice_in_dim(b_ada, chip * ada_cols, ada_cols, axis=1)
    mod_cols = _ada_fwd(c_all, w_ada[0], b_cols)
    mod_all = _exchange(mod_cols, "xy", "gather", "gather_mod")
    mod_all = mod_all.transpose(1, 0, 2).reshape(8 * B, 6 * D_MODEL)
    mod = lax.dynamic_slice_in_dim(mod_all, (4 * ix + 2 * iy + ic) * B, B, axis=0)

    south = ic == 0
    core = ic.astype(jnp.int32).reshape(1)
    shards = [(w[n][0].T if t else w[n][0]).astype(BF16) for n, t in DIRECT]
    shards.append(_flatten_shards({n: w[n][0] for n, _, _ in BIG}).astype(BF16))
    full = _gather_weights(shards, "gather_weights")
    W = {n + ("_t" if t else ""): f.reshape(-1, LANES) for (n, t), f in zip(DIRECT, full)}
    W.update(_unflatten_full(full[-1]))

    wc_all = _exchange(w_conv[0], "xy", "gather", "gather_w_conv")
    P = {n: w[n][0] for n in SMALL if n not in ("w_conv", "g_final")}
    P["w_conv"] = wc_all.transpose(1, 0, 2).reshape(3, D_FF)
    P["g_final"] = g_final

    loss, dx, dmods, gw, native = _local_step(x, mod, loss_target, W, P)

    loss = lax.psum(loss[0, 0], MESH_AXES)

    gathered = _exchange(_pack_small(native, dmods), "all", "gather", "gather_small")
    native_sum, dmod_all = _sum_unpack_small(gathered, B)
    g_small = _small_from_native(native_sum)
    dmod_all = dmod_all.reshape(8 * B, N_MOD * D_MODEL)
    dmod_cols = lax.dynamic_slice_in_dim(dmod_all, chip * ada_cols, ada_cols, axis=1)
    g_w_ada, g_b_ada = _ada_bwd(c_all, dmod_all, dmod_cols)

    G = [gw[n + ("_t" if t else "")].reshape(N_XY, -1, LANES) for n, t in DIRECT]
    G.append(_flatten_full({n: gw[n] for n, _, _ in BIG}))
    theirs = _exchange_list(G, "c", "half", "reduce_cores")
    pair = [_pair_add(g, t, core, "pair_add_%d" % i) for i, (g, t) in enumerate(zip(G, theirs))]
    parts = _exchange_list(pair, "xy", "scatter", "reduce_chips")
    red = [_sum_slots(p, "sum_chips_%d" % i) for i, p in enumerate(parts)]
    red_sib = _exchange_list(red, "c", "swap", "share_cores")
    reduced = [jnp.concatenate([jnp.where(south, r, s), jnp.where(south, s, r)], axis=0) for r, s in zip(red, red_sib)]

    grads = {"w_ada": g_w_ada[None], "b_ada": g_b_ada}
    for (n, t), g in zip(DIRECT, reduced):
        grads[n] = (g.T if t else g)[None]
    for n, g in _unflatten_shard(reduced[-1]).items():
        grads[n] = g[None]
    wc_cols = w_conv.shape[2]
    for n in SMALL:
        g = g_small[n]
        if n == "w_conv":
            g = lax.dynamic_slice_in_dim(g, chip * wc_cols, wc_cols, axis=1)
        grads[n] = g.reshape(w[n].shape)

    delta, new_m, new_v = {}, {}, {}
    for n in ["w_ada"] + [b for b, _ in DIRECT] + [b for b, _, _ in BIG]:
        shp = w[n].shape
        d2, m2, v2 = _adamw(w[n][0], grads[n][0], m[n][0], v[n][0], "adamw_" + n)
        delta[n], new_m[n], new_v[n] = d2.reshape(shp), m2.reshape(shp), v2.reshape(shp)
    rest = ["b_ada"] + SMALL

    def drop(a):
        return a.reshape(1, -1) if a.ndim == 1 else (a if a.ndim == 2 else a[0])

    upd = _adamw_multi([(drop(w[n]), drop(grads[n]), drop(m[n]), drop(v[n])) for n in rest])
    for n, (dd, mm, vv) in zip(rest, upd):
        delta[n], new_m[n], new_v[n] = dd.reshape(w[n].shape), mm.reshape(w[n].shape), vv.reshape(w[n].shape)

    return (loss, dx, *[grads[n] for n in WEIGHTS], *[delta[n] for n in WEIGHTS], *[new_m[n] for n in WEIGHTS],
            *[new_v[n] for n in WEIGHTS])
```

```python
import functools
import math

import numpy as np
import jax
import jax.numpy as jnp
from jax import lax
from jax.experimental import pallas as pl
from jax.experimental.pallas import tpu as pltpu

F32, BF16 = jnp.float32, jnp.bfloat16

D_MODEL = 1024
N_HEADS = 8
HEAD_DIM = 64
ATT_WIDTH = 512
SSM_GROUPS = 16
SSM_GROUP_CH = 16
SSM_WIDTH = 256
SSM_STATE = 64
SSM_LANES = SSM_GROUPS * SSM_STATE
D_FF = 2048
IN_WIDTH = 3 * ATT_WIDTH + SSM_WIDTH + 2 * D_MODEL
ATT_BLOCK = 128
N_PATTERNS = 3
EPS = 1e-6
NEG_INF = -1e30

ADAM_LR, ADAM_B1, ADAM_B2, ADAM_EPS, ADAM_WD, ADAM_STEP = 0.001, 0.9, 0.999, 1e-08, 0.01, 10

V7X_VMEM_LIMIT_BYTES = 56 * 1024 * 1024
LANES = 1024

MESH_AXES = ("x", "y", "c")


def _pcall(body, *, name, out_shape, grid=(), in_specs=None, out_specs=None, scratch_shapes=(), dims=None):
    params = dict(vmem_limit_bytes=V7X_VMEM_LIMIT_BYTES)
    if dims is not None:
        params["dimension_semantics"] = dims
    specs = {}
    if in_specs is not None:
        specs = dict(grid=grid, in_specs=in_specs, out_specs=out_specs)
    return pl.pallas_call(body, name=name, out_shape=out_shape, scratch_shapes=scratch_shapes,
                          compiler_params=pltpu.CompilerParams(**params), **specs)


def _sds(shape, dtype):
    return jax.ShapeDtypeStruct(tuple(shape), dtype)


def _tile(n, target):
    if n <= target:
        return n
    for t in range(target - target % 128, 0, -128):
        if n % t == 0:
            return t
    raise ValueError((n, target))


def _sig(v):
    return pl.reciprocal(1.0 + jnp.exp(-v), approx=True)


def _mm(a, b, *, name, ta=False, tb=False, out_dtype=F32, tm=2048, tn=1024, tk=1024):
    halves = a.ndim == 3
    if halves:
        a_rows, a_cols = a.shape[1], 2 * a.shape[2]
    else:
        a_rows, a_cols = a.shape
    if ta:
        K, M = a_rows, a_cols
    else:
        M, K = a_rows, a_cols
    if tb:
        N, K2 = b.shape
    else:
        K2, N = b.shape
    assert K == K2, (a.shape, b.shape)
    if halves:
        tm, tk = (min(tm, M // 2), tk) if ta else (tm, min(tk, K // 2))
    tm, tn, tk = _tile(M, tm), _tile(N, tn), _tile(K, tk)
    nk = K // tk
    if halves and ta:
        per = a.shape[2] // tm
        a_spec = pl.BlockSpec((None, tk, tm), lambda i, j, k: (i // per, k, i % per))
    elif halves:
        per = a.shape[2] // tk
        a_spec = pl.BlockSpec((None, tm, tk), lambda i, j, k: (k // per, i, k % per))
    else:
        a_spec = pl.BlockSpec((tk, tm), lambda i, j, k: (k, i)) if ta else pl.BlockSpec((tm, tk), lambda i, j, k: (i, k))
    b_spec = pl.BlockSpec((tn, tk), lambda i, j, k: (j, k)) if tb else pl.BlockSpec((tk, tn), lambda i, j, k: (k, j))
    dn = (((0 if ta else 1,), (1 if tb else 0,)), ((), ()))

    def body(a_ref, b_ref, o_ref, acc_ref):
        k = pl.program_id(2)

        @pl.when(k == 0)
        def _():
            acc_ref[...] = jnp.zeros_like(acc_ref)

        acc_ref[...] += lax.dot_general(a_ref[...].astype(BF16), b_ref[...].astype(BF16), dn,
                                        preferred_element_type=F32)

        @pl.when(k == nk - 1)
        def _():
            o_ref[...] = acc_ref[...].astype(out_dtype)

    def body_single(a_ref, b_ref, o_ref):
        o_ref[...] = lax.dot_general(a_ref[...].astype(BF16), b_ref[...].astype(BF16), dn,
                                     preferred_element_type=F32).astype(out_dtype)

    return _pcall(body_single if nk == 1 else body, name=name, out_shape=_sds((M, N), out_dtype),
                  grid=(M // tm, N // tn, nk), in_specs=[a_spec, b_spec],
                  out_specs=pl.BlockSpec((tm, tn), lambda i, j, k: (i, j)),
                  scratch_shapes=[] if nk == 1 else [pltpu.VMEM((tm, tn), F32)],
                  dims=("parallel", "parallel", "arbitrary"))(a, b)


def _ada_fwd(c_all, w_ada, b_ada_cols):
    n = w_ada.shape[1]

    def body(c_ref, w_ref, b_ref, o_ref):
        c = c_ref[...]
        act = c * _sig(c)
        o_ref[...] = jnp.dot(act.astype(BF16), w_ref[...].astype(BF16), preferred_element_type=F32) + b_ref[...]

    return _pcall(body, name="ada_fwd", out_shape=_sds((c_all.shape[0], n), F32))(c_all, w_ada, b_ada_cols)


def _ada_bwd(c_all, dmod_all, dmod_cols):
    n = dmod_cols.shape[1]

    def body(c_ref, da_ref, dc_ref, gw_ref, gb_ref):
        c = c_ref[...]
        act = c * _sig(c)
        gw_ref[...] = lax.dot_general(act, dc_ref[...], (((0,), (0,)), ((), ())), preferred_element_type=F32,
                                      precision=lax.Precision.HIGHEST)
        gb_ref[...] = jnp.sum(da_ref[...], axis=0, keepdims=True)

    return _pcall(body, name="ada_bwd", out_shape=(_sds((D_MODEL, n), F32), _sds((1, dmod_all.shape[1]), F32)))(
        c_all, dmod_all, dmod_cols)


ROW_TILE = 512


def _row_specs(B, S):
    ts = min(S, ROW_TILE)
    row = pl.BlockSpec((1, ts, D_MODEL), lambda b, s: (b, s, 0))
    bvec = pl.BlockSpec((1, 1, D_MODEL), lambda b, s: (b, 0, 0))
    gvec = pl.BlockSpec((1, D_MODEL), lambda b, s: (0, 0))
    return ts, row, bvec, gvec


def _norm_mod(x3, g, sc, sh):
    B, S, _ = x3.shape
    ts, row, bvec, gvec = _row_specs(B, S)

    def body(x_ref, g_ref, sc_ref, sh_ref, u_ref):
        x = x_ref[0]
        r = lax.rsqrt(jnp.mean(x * x, axis=-1, keepdims=True) + EPS)
        u_ref[0] = ((x * r) * g_ref[...] * (1.0 + sc_ref[0]) + sh_ref[0]).astype(BF16)

    return _pcall(body, name="norm_mod1", out_shape=_sds(x3.shape, BF16), grid=(B, S // ts),
                  in_specs=[row, gvec, bvec, bvec], out_specs=row, dims=("parallel", "parallel"))(x3, g, sc, sh)


def _resid_norm_mod(x3, mix3, gt, g, sc, sh):
    B, S, _ = x3.shape
    ts, row, bvec, gvec = _row_specs(B, S)

    def body(x_ref, m_ref, gt_ref, g_ref, sc_ref, sh_ref, h_ref, u_ref):
        h = x_ref[0] + gt_ref[0] * m_ref[0]
        h_ref[0] = h
        r = lax.rsqrt(jnp.mean(h * h, axis=-1, keepdims=True) + EPS)
        u_ref[0] = ((h * r) * g_ref[...] * (1.0 + sc_ref[0]) + sh_ref[0]).astype(BF16)

    return _pcall(body, name="resid_norm_mod2", out_shape=(_sds(x3.shape, F32), _sds(x3.shape, BF16)),
                  grid=(B, S // ts), in_specs=[row, row, bvec, gvec, bvec, bvec], out_specs=(row, row),
                  dims=("parallel", "parallel"))(x3, mix3, gt, g, sc, sh)


def _norm_bwd(h3, du3, dres3, g, sc, name, mix3=None, gt=None):
    B, S, _ = h3.shape
    ts, row, bvec, gvec = _row_specs(B, S)
    with_gate = mix3 is not None

    def body(*refs):
        if with_gate:
            h_ref, du_ref, dr_ref, g_ref, sc_ref, m_ref, gt_ref, dh_ref, dsh_ref, dsc_ref, dg_ref, dgt_ref, dm_ref = refs
        else:
            h_ref, du_ref, dr_ref, g_ref, sc_ref, dh_ref, dsh_ref, dsc_ref, dg_ref = refs
        b, s = pl.program_id(0), pl.program_id(1)
        h = h_ref[0]
        r = lax.rsqrt(jnp.mean(h * h, axis=-1, keepdims=True) + EPS)
        xn = h * r
        du = du_ref[0].astype(F32)
        g = g_ref[...]
        sc1 = 1.0 + sc_ref[0]
        dxn = du * g * sc1
        dh = dr_ref[0] + r * (dxn - xn * jnp.mean(dxn * xn, axis=-1, keepdims=True))
        dh_ref[0] = dh

        @pl.when(s == 0)
        def _():
            dsh_ref[...] = jnp.zeros_like(dsh_ref)
            dsc_ref[...] = jnp.zeros_like(dsc_ref)
            if with_gate:
                dgt_ref[...] = jnp.zeros_like(dgt_ref)

        @pl.when((s == 0) & (b == 0))
        def _():
            dg_ref[...] = jnp.zeros_like(dg_ref)

        dux = du * xn
        dsh_ref[0] += jnp.sum(du, axis=0, keepdims=True)
        dsc_ref[0] += jnp.sum(dux * g, axis=0, keepdims=True)
        dg_ref[...] += jnp.sum(dux * sc1, axis=0, keepdims=True)
        if with_gate:
            dgt_ref[0] += jnp.sum(dh * m_ref[0], axis=0, keepdims=True)
            dm_ref[0] = (dh * gt_ref[0]).astype(BF16)

    bshape = _sds((B, 1, D_MODEL), F32)
    in_specs = [row, row, row, gvec, bvec]
    out_shape = [_sds(h3.shape, F32), bshape, bshape, _sds((1, D_MODEL), F32)]
    out_specs = [row, bvec, bvec, gvec]
    args = [h3, du3, dres3, g, sc]
    if with_gate:
        in_specs += [row, bvec]
        out_shape += [bshape, _sds(h3.shape, BF16)]
        out_specs += [bvec, row]
        args += [mix3, gt]
    return _pcall(body, name=name, out_shape=tuple(out_shape), grid=(B, S // ts), in_specs=in_specs,
                  out_specs=tuple(out_specs), dims=("arbitrary", "arbitrary"))(*args)


def _final_loss(h1, ffn3, tgt3, gt, gfin):
    B, S, _ = h1.shape
    ts, row, bvec, gvec = _row_specs(B, S)
    one = pl.BlockSpec((1, 1), lambda b, s: (0, 0))

    def body(h_ref, f_ref, t_ref, gt_ref, gf_ref, dh_ref, dff_ref, dgt_ref, dgf_ref, loss_ref):
        b, s = pl.program_id(0), pl.program_id(1)
        f = f_ref[0].astype(F32)
        gtv = gt_ref[0]
        gf = gf_ref[...]
        h2 = h_ref[0] + gtv * f
        r = lax.rsqrt(jnp.mean(h2 * h2, axis=-1, keepdims=True) + EPS)
        n = h2 * r
        e = n * gf - t_ref[0]
        dy = e * (1.0 / D_MODEL)
        dn = dy * gf
        dh2 = r * (dn - n * jnp.mean(dn * n, axis=-1, keepdims=True))
        dh_ref[0] = dh2
        dff_ref[0] = (dh2 * gtv).astype(BF16)

        @pl.when(s == 0)
        def _():
            dgt_ref[...] = jnp.zeros_like(dgt_ref)

        @pl.when((s == 0) & (b == 0))
        def _():
            dgf_ref[...] = jnp.zeros_like(dgf_ref)
            loss_ref[...] = jnp.zeros_like(loss_ref)

        dgt_ref[0] += jnp.sum(dh2 * f, axis=0, keepdims=True)
        dgf_ref[...] += jnp.sum(dy * n, axis=0, keepdims=True)
        rows = jnp.sum(e * e, axis=1, keepdims=True)
        loss_ref[...] += jnp.sum(rows, axis=0, keepdims=True) * (0.5 / D_MODEL)

    return _pcall(body, name="final_loss",
                  out_shape=(_sds(h1.shape, F32), _sds(h1.shape, BF16), _sds((B, 1, D_MODEL), F32),
                             _sds((1, D_MODEL), F32), _sds((1, 1), F32)),
                  grid=(B, S // ts), in_specs=[row, row, row, bvec, gvec], out_specs=(row, row, bvec, gvec, one),
                  dims=("arbitrary", "arbitrary"))(h1, ffn3, tgt3, gt, gfin)


def _att_scores(qh, kc, kp, h, dil, first, a_idx, j_idx):
    scale = HEAD_DIM ** -0.5
    nt = (((1,), (1,)), ((), ()))
    slope = (2.0 ** (-8.0 * (h + 1) / N_HEADS)) * dil
    dist_c = (a_idx - j_idx).astype(F32)
    s_c = lax.dot_general(qh, kc, nt, preferred_element_type=F32) * scale
    s_c = jnp.where(a_idx >= j_idx, s_c - slope * dist_c, NEG_INF)
    s_p = lax.dot_general(qh, kp, nt, preferred_element_type=F32) * scale
    s_p = jnp.where((j_idx >= a_idx) & jnp.logical_not(first), s_p - slope * (dist_c + float(ATT_BLOCK)), NEG_INF)
    return s_c, s_p


def _att_block_consts(seq_blocks):
    p = pl.program_id(0)
    j = pl.program_id(1)
    nb = lax.shift_right_logical(jnp.int32(seq_blocks), 2 * p)
    dil = lax.shift_left(jnp.int32(1), 2 * p).astype(F32)
    a_idx = lax.broadcasted_iota(jnp.int32, (ATT_BLOCK, ATT_BLOCK), 0)
    j_idx = lax.broadcasted_iota(jnp.int32, (ATT_BLOCK, ATT_BLOCK), 1)
    return j, nb, dil, a_idx, j_idx


def _attn_fwd(qb, kb, vb, seq_blocks):
    _, NB, _, _ = qb.shape
    cur = pl.BlockSpec((None, None, ATT_BLOCK, ATT_WIDTH), lambda p, j: (p, j, 0, 0))
    prev = pl.BlockSpec((None, None, ATT_BLOCK, ATT_WIDTH), lambda p, j: (p, jnp.maximum(j - 1, 0), 0, 0))
    lse_spec = pl.BlockSpec((None, None, ATT_BLOCK, N_HEADS), lambda p, j: (p, j, 0, 0))

    def body(q_ref, kc_ref, kp_ref, vc_ref, vp_ref, o_ref, lse_ref):
        j, nb, dil, a_idx, j_idx = _att_block_consts(seq_blocks)
        first = lax.rem(j, nb) == 0
        for h in range(N_HEADS):
            hs = slice(h * HEAD_DIM, (h + 1) * HEAD_DIM)
            s_c, s_p = _att_scores(q_ref[:, hs], kc_ref[:, hs], kp_ref[:, hs], h, dil, first, a_idx, j_idx)
            m = jnp.maximum(jnp.max(s_c, axis=1, keepdims=True), jnp.max(s_p, axis=1, keepdims=True))
            p_c = jnp.exp(s_c - m)
            p_p = jnp.exp(s_p - m)
            den = jnp.sum(p_c, axis=1, keepdims=True) + jnp.sum(p_p, axis=1, keepdims=True)
            o = (jnp.dot(p_c.astype(BF16), vc_ref[:, hs], preferred_element_type=F32)
                 + jnp.dot(p_p.astype(BF16), vp_ref[:, hs], preferred_element_type=F32))
            o_ref[:, hs] = o / den
            lse_ref[:, h:h + 1] = m + jnp.log(den)

    return _pcall(body, name="attn_fwd",
                  out_shape=(_sds(qb.shape, F32), _sds((N_PATTERNS, NB, ATT_BLOCK, N_HEADS), F32)),
                  grid=(N_PATTERNS, NB), in_specs=[cur, cur, prev, cur, prev], out_specs=(cur, lse_spec),
                  dims=("parallel", "parallel"))(qb, kb, kb, vb, vb)


def _attn_combine(o_p, lse_p):
    _, T, _ = o_p.shape
    tm = min(T, 1024)

    def body(o_ref, l_ref, out_ref, lse_ref):
        l0, l1, l2 = l_ref[0], l_ref[1], l_ref[2]
        m = jnp.maximum(jnp.maximum(l0, l1), l2)
        lse = m + jnp.log(jnp.exp(l0 - m) + jnp.exp(l1 - m) + jnp.exp(l2 - m))
        lse_ref[...] = lse
        w = [jnp.exp(l0 - lse), jnp.exp(l1 - lse), jnp.exp(l2 - lse)]
        for h in range(N_HEADS):
            hs = slice(h * HEAD_DIM, (h + 1) * HEAD_DIM)
            acc = w[0][:, h:h + 1] * o_ref[0, :, hs]
            acc = acc + w[1][:, h:h + 1] * o_ref[1, :, hs]
            acc = acc + w[2][:, h:h + 1] * o_ref[2, :, hs]
            out_ref[:, hs] = acc.astype(BF16)

    return _pcall(body, name="attn_combine", out_shape=(_sds((T, ATT_WIDTH), BF16), _sds((T, N_HEADS), F32)),
                  grid=(T // tm,),
                  in_specs=[pl.BlockSpec((N_PATTERNS, tm, ATT_WIDTH), lambda i: (0, i, 0)),
                            pl.BlockSpec((N_PATTERNS, tm, N_HEADS), lambda i: (0, i, 0))],
                  out_specs=(pl.BlockSpec((tm, ATT_WIDTH), lambda i: (i, 0)), pl.BlockSpec((tm, N_HEADS), lambda i: (i, 0))),
                  dims=("parallel",))(o_p, lse_p)


def _attn_bwd(qb, kb, vb, dob, ob, lseb, seq_blocks):
    _, NB, _, _ = qb.shape
    last = NB - 1
    cur = pl.BlockSpec((None, None, ATT_BLOCK, ATT_WIDTH), lambda p, j: (p, jnp.minimum(j, last), 0, 0))
    prev = pl.BlockSpec((None, None, ATT_BLOCK, ATT_WIDTH),
                        lambda p, j: (p, jnp.maximum(jnp.minimum(j, last) - 1, 0), 0, 0))
    lag = pl.BlockSpec((None, None, ATT_BLOCK, ATT_WIDTH), lambda p, j: (p, jnp.maximum(j - 1, 0), 0, 0))
    lse_spec = pl.BlockSpec((None, None, ATT_BLOCK, N_HEADS), lambda p, j: (p, jnp.minimum(j, last), 0, 0))
    scale = HEAD_DIM ** -0.5
    tn = (((0,), (0,)), ((), ()))
    nt = (((1,), (1,)), ((), ()))

    def body(q_ref, kc_ref, kp_ref, vc_ref, vp_ref, do_ref, o_ref, lse_ref, dq_ref, dk_ref, dv_ref, ck_ref, cv_ref):
        j, nb, dil, a_idx, j_idx = _att_block_consts(seq_blocks)

        @pl.when(j == 0)
        def _():
            ck_ref[...] = jnp.zeros_like(ck_ref)
            cv_ref[...] = jnp.zeros_like(cv_ref)

        @pl.when(j <= last)
        def _():
            first = lax.rem(j, nb) == 0
            for h in range(N_HEADS):
                hs = slice(h * HEAD_DIM, (h + 1) * HEAD_DIM)
                qh, kc, kp, vc, vp, doh = q_ref[:, hs], kc_ref[:, hs], kp_ref[:, hs], vc_ref[:, hs], vp_ref[:, hs], do_ref[:, hs]
                s_c, s_p = _att_scores(qh, kc, kp, h, dil, first, a_idx, j_idx)
                lse = lse_ref[:, h:h + 1]
                p_c = jnp.exp(s_c - lse)
                p_p = jnp.exp(s_p - lse)
                delta = jnp.sum(doh.astype(F32) * o_ref[:, hs].astype(F32), axis=1, keepdims=True)
                ds_c = (p_c * (lax.dot_general(doh, vc, nt, preferred_element_type=F32) - delta)).astype(BF16)
                ds_p = (p_p * (lax.dot_general(doh, vp, nt, preferred_element_type=F32) - delta)).astype(BF16)
                dq_ref[:, hs] = (jnp.dot(ds_c, kc, preferred_element_type=F32)
                                 + jnp.dot(ds_p, kp, preferred_element_type=F32)) * scale
                dk_ref[:, hs] = ck_ref[:, hs] + lax.dot_general(ds_p, qh, tn, preferred_element_type=F32) * scale
                dv_ref[:, hs] = cv_ref[:, hs] + lax.dot_general(p_p.astype(BF16), doh, tn, preferred_element_type=F32)
                ck_ref[:, hs] = lax.dot_general(ds_c, qh, tn, preferred_element_type=F32) * scale
                cv_ref[:, hs] = lax.dot_general(p_c.astype(BF16), doh, tn, preferred_element_type=F32)

        @pl.when(j == NB)
        def _():
            dk_ref[...] = ck_ref[...]
            dv_ref[...] = cv_ref[...]

    shp = _sds(qb.shape, F32)
    return _pcall(body, name="attn_bwd", out_shape=(shp, shp, shp), grid=(N_PATTERNS, NB + 1),
                  in_specs=[cur, cur, prev, cur, prev, cur, cur, lse_spec], out_specs=(cur, lag, lag),
                  scratch_shapes=[pltpu.VMEM((ATT_BLOCK, ATT_WIDTH), F32), pltpu.VMEM((ATT_BLOCK, ATT_WIDTH), F32)],
                  dims=("arbitrary", "arbitrary"))(qb, kb, kb, vb, vb, dob, ob, lseb)


def _sum3_cast(a, b, c):
    T, N = a.shape
    tm = min(T, 1024)
    spec = pl.BlockSpec((tm, N), lambda i: (i, 0))

    def body(a_ref, b_ref, c_ref, o_ref):
        o_ref[...] = (a_ref[...] + b_ref[...] + c_ref[...]).astype(BF16)

    return _pcall(body, name="sum3_cast", out_shape=_sds((T, N), BF16), grid=(T // tm,), in_specs=[spec] * 3,
                  out_specs=spec, dims=("parallel",))(a, b, c)


def _to_blocks(t, B, S):
    C = t.shape[-1]
    outs = []
    for p in range(N_PATTERNS):
        d = 4 ** p
        u = t.reshape(B, S // d, d, C).transpose(0, 2, 1, 3)
        outs.append(u.reshape(B * S // ATT_BLOCK, ATT_BLOCK, C))
    return jnp.stack(outs, axis=0)


def _from_blocks(tb, B, S):
    C = tb.shape[-1]
    outs = []
    for p in range(N_PATTERNS):
        d = 4 ** p
        u = tb[p].reshape(B, d, S // d, C).transpose(0, 2, 1, 3)
        outs.append(u.reshape(B * S, C))
    return jnp.stack(outs, axis=0)


ATT_GROUP = 4
ATT_GW = ATT_GROUP * HEAD_DIM
ATT_GROUPS = N_HEADS // ATT_GROUP
ATT_PAIRS = ATT_GW // ATT_BLOCK
ATT_UNROLL = 3
NT_DIMS = (((1,), (1,)), ((), ()))
TN_DIMS = (((0,), (0,)), ((), ()))


def _att_rows(start, d):
    if d == 1:
        return pl.ds(start if isinstance(start, int) else pl.multiple_of(start, ATT_BLOCK), ATT_BLOCK)
    return pl.ds(start, ATT_BLOCK, stride=d)


def _att_fill_bias(bias_ref, g, d):
    a = lax.broadcasted_iota(jnp.int32, (ATT_BLOCK, ATT_BLOCK), 0)
    j = lax.broadcasted_iota(jnp.int32, (ATT_BLOCK, ATT_BLOCK), 1)
    dist = (a - j).astype(F32)
    for hh in range(ATT_GROUP):
        t, e = divmod(hh, 2)
        rs = slice(e * ATT_BLOCK, (e + 1) * ATT_BLOCK)
        lo = 2.0 ** (-8.0 * (hh + 1) / N_HEADS) * d
        hi = 2.0 ** (-8.0 * (ATT_GROUP + hh + 1) / N_HEADS) * d
        slope = jnp.where(g == 0, lo, hi).astype(F32)
        bias_ref[t, rs, 0:ATT_BLOCK] = jnp.where(a >= j, -slope * dist, NEG_INF)
        bias_ref[t, rs, ATT_BLOCK:] = jnp.where(j >= a, -slope * (dist + float(ATT_BLOCK)), NEG_INF)


def _stack_heads(v2, low):
    return jnp.concatenate([jnp.where(low, v2, 0.0), jnp.where(low, 0.0, v2)], axis=0).astype(BF16)


def _unstack_heads(r2, low):
    return jnp.where(low, r2[0:ATT_BLOCK], r2[ATT_BLOCK:])


class _Riders:
    def __init__(self, arrs, mode):
        self.arrs, self.mode, self.n = list(arrs), mode, len(arrs)
        slot_shapes = [a.shape if mode == "gather" else a.shape[1:] for a in self.arrs]
        self.out_shape = [_sds((N_XY,) + s, a.dtype) for s, a in zip(slot_shapes, self.arrs)]
        k = len(_GROUP_MASKS["xy"])
        self.scratch = [pltpu.SemaphoreType.DMA((k * self.n,)), pltpu.SemaphoreType.DMA((k * self.n,)),
                        pltpu.SemaphoreType.DMA((2 * self.n,))] + [pltpu.VMEM(s, a.dtype) for s, a in zip(slot_shapes, self.arrs)]
        self.specs = [pl.BlockSpec(memory_space=pl.ANY)] * self.n

    def _remote(self, x_refs, o_refs, send_sems, recv_sems):
        x, y, c = lax.axis_index("x"), lax.axis_index("y"), lax.axis_index("c")
        me = 2 * x + y
        cps = []
        for i in range(self.n):
            for k, (dx, dy, _) in enumerate(_GROUP_MASKS["xy"]):
                px, py = _flip(x, dx), _flip(y, dy)
                src = x_refs[i] if self.mode == "gather" else x_refs[i].at[2 * px + py]
                cps.append(pltpu.make_async_remote_copy(
                    src_ref=src, dst_ref=o_refs[i].at[me], send_sem=send_sems.at[3 * i + k], recv_sem=recv_sems.at[3 * i + k],
                    device_id=(px, py, c), device_id_type=pl.DeviceIdType.MESH))
        return cps, me

    def start(self, x_refs, o_refs, scratch):
        send_sems, recv_sems, local_sems, bufs = scratch[0], scratch[1], scratch[2], scratch[3:]
        cps, me = self._remote(x_refs, o_refs, send_sems, recv_sems)
        for cp in cps:
            cp.start()
        for i in range(self.n):
            src = x_refs[i] if self.mode == "gather" else x_refs[i].at[me]
            load = pltpu.make_async_copy(src, bufs[i], local_sems.at[2 * i])
            load.start()
            load.wait()
            pltpu.make_async_copy(bufs[i], o_refs[i].at[me], local_sems.at[2 * i + 1]).start()

    def wait(self, x_refs, o_refs, scratch):
        send_sems, recv_sems, local_sems, bufs = scratch[0], scratch[1], scratch[2], scratch[3:]
        cps, me = self._remote(x_refs, o_refs, send_sems, recv_sems)
        for cp in cps:
            cp.wait()
        for i in range(self.n):
            pltpu.make_async_copy(bufs[i], o_refs[i].at[me], local_sems.at[2 * i + 1]).wait()


def _with_riders(compute, riders, n_in, n_out, n_scratch, last_step):
    if riders is None:
        return compute
    n = riders.n

    def body(*refs):
        ins, x_refs = refs[:n_in], refs[n_in:n_in + n]
        outs, o_refs = refs[n_in + n:n_in + n + n_out], refs[n_in + n + n_out:n_in + 2 * n + n_out]
        scratch = refs[n_in + 2 * n + n_out:]
        own, ride = scratch[:n_scratch], scratch[n_scratch:]
        b, g = pl.program_id(0), pl.program_id(1)

        @pl.when((b == 0) & (g == 0))
        def _():
            riders.start(x_refs, o_refs, ride)

        compute(*ins, *outs, *own)

        @pl.when((b == last_step[0]) & (g == last_step[1]))
        def _():
            riders.wait(x_refs, o_refs, ride)

    return body


def _attention_fwd(proj3, seq_blocks, riders=None):
    B, S, _ = proj3.shape
    scale = HEAD_DIM ** -0.5
    nq = ATT_WIDTH // ATT_GW

    def col(k):
        return pl.BlockSpec((1, S, ATT_GW), lambda b, g, k=k: (b, 0, k * nq + g))

    o_spec = pl.BlockSpec((1, S, ATT_GW), lambda b, g: (b, 0, g))
    l_spec = pl.BlockSpec((1, 1, S, ATT_BLOCK), lambda b, g: (b, g, 0, 0))

    def compute(q_ref, k_ref, v_ref, o_ref, lse_ref, qf, kf, vf, os, ls, bias):
        g = pl.program_id(1)
        for t in range(ATT_PAIRS):
            ts = slice(t * ATT_BLOCK, (t + 1) * ATT_BLOCK)
            qf[t] = q_ref[0, :, ts].astype(F32) * scale
            kf[t] = k_ref[0, :, ts].astype(F32)
            vf[t] = v_ref[0, :, ts].astype(F32)
        lane = lax.broadcasted_iota(jnp.int32, (ATT_BLOCK, ATT_BLOCK), 1)
        low = lane < HEAD_DIM

        def block(p, d, r, n, has_prev):
            start = n * (ATT_BLOCK * d) + r
            rows = _att_rows(start, d)
            prows = _att_rows(start - ATT_BLOCK * d, d) if has_prev else None
            lse_t = jnp.zeros((ATT_BLOCK, ATT_BLOCK), F32)
            for t in range(ATT_PAIRS):
                q2 = _stack_heads(qf[t, rows, :], low)
                k2 = kf[t, rows, :].astype(BF16)
                v2 = vf[t, rows, :].astype(BF16)
                if has_prev:
                    k2 = jnp.concatenate([k2, kf[t, prows, :].astype(BF16)], axis=0)
                    v2 = jnp.concatenate([v2, vf[t, prows, :].astype(BF16)], axis=0)
                    b2 = bias[t]
                else:
                    b2 = bias[t, :, 0:ATT_BLOCK]
                s = lax.dot_general(q2, k2, NT_DIMS, preferred_element_type=F32) + b2
                m = jnp.max(s, axis=1, keepdims=True)
                pr = jnp.exp(s - m)
                den = jnp.sum(pr, axis=1, keepdims=True)
                o = jnp.dot(pr.astype(BF16), v2, preferred_element_type=F32) * (1.0 / den)
                os[p, t, rows, :] = _unstack_heads(o, low)
                lse2 = m + jnp.log(den)
                lse_t = jnp.where(lane == 2 * t, lse2[0:ATT_BLOCK], lse_t)
                lse_t = jnp.where(lane == 2 * t + 1, lse2[ATT_BLOCK:], lse_t)
            ls[p, rows, :] = lse_t

        for p in range(N_PATTERNS):
            d = 4 ** p
            _att_fill_bias(bias, g, d)
            _att_one_pattern(block, p, d, seq_blocks // d)

        def combine(i, carry):
            rows = pl.ds(pl.multiple_of(i * ATT_BLOCK, ATT_BLOCK), ATT_BLOCK)
            l0, l1, l2 = ls[0, rows, :], ls[1, rows, :], ls[2, rows, :]
            m = jnp.maximum(jnp.maximum(l0, l1), l2)
            lse = m + jnp.log(jnp.exp(l0 - m) + jnp.exp(l1 - m) + jnp.exp(l2 - m))
            lse_ref[0, 0, rows, :] = lse
            w = [jnp.exp(l0 - lse), jnp.exp(l1 - lse), jnp.exp(l2 - lse)]
            for t in range(ATT_PAIRS):
                acc = jnp.zeros((ATT_BLOCK, ATT_BLOCK), F32)
                for p in range(N_PATTERNS):
                    wt = jnp.where(low, w[p][:, 2 * t:2 * t + 1], w[p][:, 2 * t + 1:2 * t + 2])
                    acc = acc + wt * os[p, t, rows, :]
                o_ref[0, rows, t * ATT_BLOCK:(t + 1) * ATT_BLOCK] = acc.astype(BF16)
            return carry

        lax.fori_loop(0, S // ATT_BLOCK, combine, 0, unroll=2)

    scratch = ([pltpu.VMEM((ATT_PAIRS, S, ATT_BLOCK), F32)] * 3
               + [pltpu.VMEM((N_PATTERNS, ATT_PAIRS, S, ATT_BLOCK), F32), pltpu.VMEM((N_PATTERNS, S, ATT_BLOCK), F32),
                  pltpu.VMEM((ATT_PAIRS, 2 * ATT_BLOCK, 2 * ATT_BLOCK), F32)])
    rs = riders
    res = _pcall(_with_riders(compute, rs, 3, 2, len(scratch), (B - 1, ATT_GROUPS - 1)), name="attention_fwd",
                 out_shape=(_sds((B, S, ATT_WIDTH), BF16), _sds((B, ATT_GROUPS, S, ATT_BLOCK), F32))
                 + (tuple(rs.out_shape) if rs else ()),
                 grid=(B, ATT_GROUPS), in_specs=[col(0), col(1), col(2)] + (rs.specs if rs else []),
                 out_specs=(o_spec, l_spec) + (tuple(rs.specs) if rs else ()),
                 scratch_shapes=scratch + (rs.scratch if rs else []),
                 dims=("arbitrary", "arbitrary"))(proj3, proj3, proj3, *(rs.arrs if rs else []))
    return res[0], res[1], list(res[2:])


def _att_one_pattern(block, p, d, nb):
    def per_residue(r, carry):
        block(p, d, r, 0, False)
        if nb > 1:
            def per_block(n, c2):
                block(p, d, r, n, True)
                return c2
            lax.fori_loop(1, nb, per_block, 0, unroll=ATT_UNROLL)
        return carry

    if d == 1:
        per_residue(0, 0)
    else:
        lax.fori_loop(0, d, per_residue, 0, unroll=ATT_UNROLL + 1 if nb == 1 else 1)


def _attention_bwd(proj3, do3, o3, lse4, seq_blocks, riders=None):
    B, S, _ = proj3.shape
    scale = HEAD_DIM ** -0.5
    nq = ATT_WIDTH // ATT_GW

    def col(k):
        return pl.BlockSpec((1, S, ATT_GW), lambda b, g, k=k: (b, 0, k * nq + g))

    o_spec = pl.BlockSpec((1, S, ATT_GW), lambda b, g: (b, 0, g))
    l_spec = pl.BlockSpec((1, 1, S, ATT_BLOCK), lambda b, g: (b, g, 0, 0))

    def compute(q_ref, k_ref, v_ref, do_ref, o_ref, lse_ref, dq_ref, dk_ref, dv_ref,
                qf, kf, vf, dof, dl, aq, ak, av, bias):
        g = pl.program_id(1)
        for t in range(ATT_PAIRS):
            ts = slice(t * ATT_BLOCK, (t + 1) * ATT_BLOCK)
            qf[t] = q_ref[0, :, ts].astype(F32) * scale
            kf[t] = k_ref[0, :, ts].astype(F32)
            vf[t] = v_ref[0, :, ts].astype(F32)
            dof[t] = do_ref[0, :, ts].astype(F32)
        aq[...] = jnp.zeros_like(aq)
        ak[...] = jnp.zeros_like(ak)
        av[...] = jnp.zeros_like(av)
        lane = lax.broadcasted_iota(jnp.int32, (ATT_BLOCK, ATT_BLOCK), 1)
        low = lane < HEAD_DIM

        def fill_delta(i, carry):
            rows = pl.ds(pl.multiple_of(i * ATT_BLOCK, ATT_BLOCK), ATT_BLOCK)
            acc = jnp.zeros((ATT_BLOCK, ATT_BLOCK), F32)
            for t in range(ATT_PAIRS):
                prod = dof[t, rows, :] * o_ref[0, rows, t * ATT_BLOCK:(t + 1) * ATT_BLOCK].astype(F32)
                lo = jnp.sum(jnp.where(low, prod, 0.0), axis=1, keepdims=True)
                hi = jnp.sum(prod, axis=1, keepdims=True) - lo
                acc = jnp.where(lane == 2 * t, lo, acc)
                acc = jnp.where(lane == 2 * t + 1, hi, acc)
            dl[rows, :] = acc
            return carry

        lax.fori_loop(0, S // ATT_BLOCK, fill_delta, 0, unroll=2)

        def block(p, d, r, n, has_prev):
            start = n * (ATT_BLOCK * d) + r
            rows = _att_rows(start, d)
            prows = _att_rows(start - ATT_BLOCK * d, d) if has_prev else None
            lse_t = lse_ref[0, 0, rows, :]
            dl_t = dl[rows, :]
            for t in range(ATT_PAIRS):
                q2 = _stack_heads(qf[t, rows, :], low)
                do2 = _stack_heads(dof[t, rows, :], low)
                k2 = kf[t, rows, :].astype(BF16)
                v2 = vf[t, rows, :].astype(BF16)
                if has_prev:
                    k2 = jnp.concatenate([k2, kf[t, prows, :].astype(BF16)], axis=0)
                    v2 = jnp.concatenate([v2, vf[t, prows, :].astype(BF16)], axis=0)
                    b2 = bias[t]
                else:
                    b2 = bias[t, :, 0:ATT_BLOCK]
                lse2 = jnp.concatenate([lse_t[:, 2 * t:2 * t + 1], lse_t[:, 2 * t + 1:2 * t + 2]], axis=0)
                dl2 = jnp.concatenate([dl_t[:, 2 * t:2 * t + 1], dl_t[:, 2 * t + 1:2 * t + 2]], axis=0)
                s = lax.dot_general(q2, k2, NT_DIMS, preferred_element_type=F32) + b2
                pr = jnp.exp(s - lse2)
                ds = (pr * (lax.dot_general(do2, v2, NT_DIMS, preferred_element_type=F32) - dl2)).astype(BF16)
                dq = _unstack_heads(jnp.dot(ds, k2, preferred_element_type=F32), low)
                dk = lax.dot_general(ds, q2, TN_DIMS, preferred_element_type=F32)
                dv = lax.dot_general(pr.astype(BF16), do2, TN_DIMS, preferred_element_type=F32)
                aq[t, rows, :] = aq[t, rows, :] + dq * scale
                ak[t, rows, :] = ak[t, rows, :] + dk[0:ATT_BLOCK]
                av[t, rows, :] = av[t, rows, :] + dv[0:ATT_BLOCK]
                if has_prev:
                    ak[t, prows, :] = ak[t, prows, :] + dk[ATT_BLOCK:]
                    av[t, prows, :] = av[t, prows, :] + dv[ATT_BLOCK:]

        for p in range(N_PATTERNS):
            d = 4 ** p
            _att_fill_bias(bias, g, d)
            _att_one_pattern(block, p, d, seq_blocks // d)

        for t in range(ATT_PAIRS):
            ts = slice(t * ATT_BLOCK, (t + 1) * ATT_BLOCK)
            dq_ref[0, :, ts] = aq[t].astype(BF16)
            dk_ref[0, :, ts] = ak[t].astype(BF16)
            dv_ref[0, :, ts] = av[t].astype(BF16)

    shp = _sds((B, S, ATT_WIDTH), BF16)
    pair_buf = pltpu.VMEM((ATT_PAIRS, S, ATT_BLOCK), F32)
    scratch = ([pair_buf] * 4 + [pltpu.VMEM((S, ATT_BLOCK), F32)] + [pair_buf] * 3
               + [pltpu.VMEM((ATT_PAIRS, 2 * ATT_BLOCK, 2 * ATT_BLOCK), F32)])
    rs = riders
    res = _pcall(_with_riders(compute, rs, 6, 3, len(scratch), (B - 1, ATT_GROUPS - 1)), name="attention_bwd",
                 out_shape=(shp, shp, shp) + (tuple(rs.out_shape) if rs else ()), grid=(B, ATT_GROUPS),
                 in_specs=[col(0), col(1), col(2), o_spec, o_spec, l_spec] + (rs.specs if rs else []),
                 out_specs=(o_spec, o_spec, o_spec) + (tuple(rs.specs) if rs else ()),
                 scratch_shapes=scratch + (rs.scratch if rs else []),
                 dims=("arbitrary", "arbitrary"))(proj3, proj3, proj3, do3, o3, lse4, *(rs.arrs if rs else []))
    return res[0], res[1], res[2], list(res[3:])


def _expand_groups(m):
    rows = SSM_WIDTH
    t = jnp.concatenate([m] * SSM_GROUPS, axis=0)
    r = lax.broadcasted_iota(jnp.int32, (rows, SSM_LANES), 0)
    l = lax.broadcasted_iota(jnp.int32, (rows, SSM_LANES), 1)
    keep = lax.shift_right_logical(r, 4) == lax.shift_right_logical(l, 6)
    return jnp.where(keep, t, 0.0)


def _collapse_groups(m):
    rows = SSM_WIDTH
    r = lax.broadcasted_iota(jnp.int32, (rows, SSM_LANES), 0)
    l = lax.broadcasted_iota(jnp.int32, (rows, SSM_LANES), 1)
    keep = lax.shift_right_logical(r, 4) == lax.shift_right_logical(l, 6)
    t = jnp.where(keep, m, 0.0)
    acc = t[0:SSM_GROUP_CH]
    for g in range(1, SSM_GROUPS):
        acc = acc + t[g * SSM_GROUP_CH:(g + 1) * SSM_GROUP_CH]
    return acc


def _zoh(lr, li, ldt):
    dt = jnp.exp(ldt)
    mag = jnp.exp(lr * dt)
    ang = li * dt
    cs, sn = jnp.cos(ang), jnp.sin(ang)
    ab_re, ab_im = mag * cs, mag * sn
    nr, ni = ab_re - 1.0, ab_im
    den = lr * lr + li * li
    n_re = nr * lr + ni * li
    n_im = ni * lr - nr * li
    return dict(dt=dt, mag=mag, cs=cs, sn=sn, ab_re=ab_re, ab_im=ab_im, nr=nr, ni=ni, den=den, n_re=n_re, n_im=n_im,
                f_re=n_re / den, f_im=n_im / den)


def _ssm_params(lr, li, ldt, br, bi, cr, ci):
    def body(lr_ref, li_ref, ldt_ref, br_ref, bi_ref, cr_ref, ci_ref, ab_ref, w_ref, c_ref):
        z = _zoh(lr_ref[...], li_ref[...], ldt_ref[...])
        ab_ref[0:1, :] = z["ab_re"]
        ab_ref[1:2, :] = z["ab_im"]
        br, bi = br_ref[...], bi_ref[...]
        w_ref[:, 0:SSM_LANES] = _expand_groups(z["f_re"] * br - z["f_im"] * bi).astype(BF16)
        w_ref[:, SSM_LANES:] = _expand_groups(z["f_re"] * bi + z["f_im"] * br).astype(BF16)
        c_ref[:, 0:SSM_LANES] = _expand_groups(cr_ref[...]).astype(BF16)
        c_ref[:, SSM_LANES:] = _expand_groups(-ci_ref[...]).astype(BF16)

    return _pcall(body, name="ssm_params",
                  out_shape=(_sds((2, SSM_LANES), F32), _sds((SSM_WIDTH, 2 * SSM_LANES), BF16),
                             _sds((SSM_WIDTH, 2 * SSM_LANES), BF16)))(lr, li, ldt, br, bi, cr, ci)


def _ssm_params_bwd(lr, li, ldt, br, bi, dab, dw, dc):
    def body(lr_ref, li_ref, ldt_ref, br_ref, bi_ref, dab_ref, dw_ref, dc_ref,
             dlr_ref, dli_ref, dldt_ref, dbr_ref, dbi_ref, dcr_ref, dci_ref):
        lr, li = lr_ref[...], li_ref[...]
        z = _zoh(lr, li, ldt_ref[...])
        br, bi = br_ref[...], bi_ref[...]
        dbb_re = _collapse_groups(dw_ref[:, 0:SSM_LANES])
        dbb_im = _collapse_groups(dw_ref[:, SSM_LANES:])
        dcr_ref[...] = _collapse_groups(dc_ref[:, 0:SSM_LANES])
        dci_ref[...] = -_collapse_groups(dc_ref[:, SSM_LANES:])
        f_re, f_im = z["f_re"], z["f_im"]
        dbr_ref[...] = f_re * dbb_re + f_im * dbb_im
        dbi_ref[...] = f_re * dbb_im - f_im * dbb_re
        df_re = jnp.sum(dbb_re * br + dbb_im * bi, axis=0, keepdims=True)
        df_im = jnp.sum(dbb_im * br - dbb_re * bi, axis=0, keepdims=True)
        den = z["den"]
        dn_re, dn_im = df_re / den, df_im / den
        dden = -(df_re * z["n_re"] + df_im * z["n_im"]) / (den * den)
        dnr = dn_re * lr - dn_im * li
        dni = dn_re * li + dn_im * lr
        dlr = dn_re * z["nr"] + dn_im * z["ni"] + 2.0 * dden * lr
        dli = dn_re * z["ni"] - dn_im * z["nr"] + 2.0 * dden * li
        dab_re = dab_ref[0:1, :] + dnr
        dab_im = dab_ref[1:2, :] + dni
        mag, cs, sn, dt = z["mag"], z["cs"], z["sn"], z["dt"]
        dmag = dab_re * cs + dab_im * sn
        dang = mag * (dab_im * cs - dab_re * sn)
        dlr_ref[...] = dlr + dmag * mag * dt
        dli_ref[...] = dli + dang * dt
        ddt = dmag * mag * lr + dang * li
        per_lane = jnp.broadcast_to(ddt * dt, (8, SSM_LANES))
        lane = lax.broadcasted_iota(jnp.int32, (SSM_LANES, 128), 0)
        col = lax.broadcasted_iota(jnp.int32, (SSM_LANES, 128), 1)
        ind = jnp.where(lax.shift_right_logical(lane, 6) == col, 1.0, 0.0)
        dldt_ref[...] = jnp.dot(per_lane, ind, preferred_element_type=F32, precision=lax.Precision.HIGHEST)[0:1]

    vec = _sds((1, SSM_LANES), F32)
    mat = _sds((SSM_GROUP_CH, SSM_LANES), F32)
    return _pcall(body, name="ssm_params_bwd", out_shape=(vec, vec, _sds((1, 128), F32), mat, mat, mat, mat))(
        lr, li, ldt, br, bi, dab, dw, dc)


SCAN_CHUNK = 512


def _scan_consts(ar, ai, k_ref, reverse):
    row = lax.broadcasted_iota(jnp.int32, (8, SSM_LANES), 0)
    pw = [(ar, ai)]
    for _ in range(7):
        pr, pi = pw[-1]
        pw.append((pr * ar - pi * ai, pr * ai + pi * ar))
    for n, k in enumerate((1, 2, 4)):
        keep = (row < 8 - k) if reverse else (row >= k)
        k_ref[2 * n] = jnp.where(keep, jnp.broadcast_to(pw[k - 1][0], (8, SSM_LANES)), 0.0)
        k_ref[2 * n + 1] = jnp.where(keep, jnp.broadcast_to(pw[k - 1][1], (8, SSM_LANES)), 0.0)
    cr = jnp.zeros((8, SSM_LANES), F32)
    ci = jnp.zeros((8, SSM_LANES), F32)
    for r in range(8):
        e = (8 - r) if reverse else (r + 1)
        cr = jnp.where(row == r, jnp.broadcast_to(pw[e - 1][0], (8, SSM_LANES)), cr)
        ci = jnp.where(row == r, jnp.broadcast_to(pw[e - 1][1], (8, SSM_LANES)), ci)
    k_ref[6] = cr
    k_ref[7] = ci


def _scan_tile(xr, xi, k_ref, car, cai, reverse):
    for n, k in enumerate((1, 2, 4)):
        sh = (8 - k) if reverse else k
        sr = pltpu.roll(xr, sh, 0)
        si = pltpu.roll(xi, sh, 0)
        mr, mi = k_ref[2 * n], k_ref[2 * n + 1]
        xr, xi = xr + mr * sr - mi * si, xi + mr * si + mi * sr
    pr, pi = k_ref[6], k_ref[7]
    xr, xi = xr + pr * car - pi * cai, xi + pr * cai + pi * car
    return xr, xi


def _scan_fwd(bu3, abar):
    B, S, _ = bu3.shape
    ch = min(S, SCAN_CHUNK)
    blk = pl.BlockSpec((1, ch, 2 * SSM_LANES), lambda b, c: (b, c, 0))

    def body(ab_ref, bu_ref, x_ref, k_ref, carry_ref):
        _scan_consts(ab_ref[0:1, :], ab_ref[1:2, :], k_ref, False)

        @pl.when(pl.program_id(1) == 0)
        def _():
            carry_ref[...] = jnp.zeros_like(carry_ref)

        def step(i, carry):
            base = pl.multiple_of(i * 8, 8)
            xr = bu_ref[0, pl.ds(base, 8), 0:SSM_LANES]
            xi = bu_ref[0, pl.ds(base, 8), SSM_LANES:]
            xr, xi = _scan_tile(xr, xi, k_ref, carry[0], carry[1], False)
            x_ref[0, pl.ds(base, 8), 0:SSM_LANES] = xr
            x_ref[0, pl.ds(base, 8), SSM_LANES:] = xi
            return (jnp.broadcast_to(xr[7:8], (8, SSM_LANES)), jnp.broadcast_to(xi[7:8], (8, SSM_LANES)))

        cr, ci = lax.fori_loop(0, ch // 8, step, (carry_ref[0], carry_ref[1]))
        carry_ref[0] = cr
        carry_ref[1] = ci

    return _pcall(body, name="scan_fwd", out_shape=_sds(bu3.shape, F32), grid=(B, S // ch),
                  in_specs=[pl.BlockSpec((2, SSM_LANES), lambda b, c: (0, 0)), blk], out_specs=blk,
                  scratch_shapes=[pltpu.VMEM((8, 8, SSM_LANES), F32), pltpu.VMEM((2, 8, SSM_LANES), F32)],
                  dims=("arbitrary", "arbitrary"))(abar, bu3)


def _scan_bwd(dx3, xs3, abar):
    B, S, _ = dx3.shape
    ch = min(S, SCAN_CHUNK)
    nc = S // ch
    blk = pl.BlockSpec((1, ch, 2 * SSM_LANES), lambda b, c: (b, nc - 1 - c, 0))

    def body(ab_ref, dx_ref, xs_ref, g_ref, da_ref, k_ref, carry_ref, acc_ref):
        b, c = pl.program_id(0), pl.program_id(1)
        _scan_consts(ab_ref[0:1, :], -ab_ref[1:2, :], k_ref, True)
        row = lax.broadcasted_iota(jnp.int32, (8, SSM_LANES), 0)

        @pl.when(c == 0)
        def _():
            carry_ref[...] = jnp.zeros_like(carry_ref)

        @pl.when((c == 0) & (b == 0))
        def _():
            acc_ref[...] = jnp.zeros_like(acc_ref)

        def step(i, carry):
            car, cai, ar_acc, ai_acc = carry
            base = pl.multiple_of((ch // 8 - 1 - i) * 8, 8)
            gr = dx_ref[0, pl.ds(base, 8), 0:SSM_LANES]
            gi = dx_ref[0, pl.ds(base, 8), SSM_LANES:]
            gr, gi = _scan_tile(gr, gi, k_ref, car, cai, True)
            g_ref[0, pl.ds(base, 8), 0:SSM_LANES] = gr
            g_ref[0, pl.ds(base, 8), SSM_LANES:] = gi
            nr = jnp.where(row == 7, car, pltpu.roll(gr, 7, 0))
            ni = jnp.where(row == 7, cai, pltpu.roll(gi, 7, 0))
            xr = xs_ref[0, pl.ds(base, 8), 0:SSM_LANES]
            xi = xs_ref[0, pl.ds(base, 8), SSM_LANES:]
            ar_acc = ar_acc + nr * xr + ni * xi
            ai_acc = ai_acc + ni * xr - nr * xi
            return (jnp.broadcast_to(gr[0:1], (8, SSM_LANES)), jnp.broadcast_to(gi[0:1], (8, SSM_LANES)), ar_acc, ai_acc)

        cr, ci, ar_acc, ai_acc = lax.fori_loop(0, ch // 8, step, (carry_ref[0], carry_ref[1], acc_ref[0], acc_ref[1]))
        carry_ref[0] = cr
        carry_ref[1] = ci
        acc_ref[0] = ar_acc
        acc_ref[1] = ai_acc
        da_ref[0:1, :] = jnp.sum(ar_acc, axis=0, keepdims=True)
        da_ref[1:2, :] = jnp.sum(ai_acc, axis=0, keepdims=True)

    return _pcall(body, name="scan_bwd", out_shape=(_sds(dx3.shape, F32), _sds((2, SSM_LANES), F32)), grid=(B, nc),
                  in_specs=[pl.BlockSpec((2, SSM_LANES), lambda b, c: (0, 0)), blk, blk],
                  out_specs=(blk, pl.BlockSpec((2, SSM_LANES), lambda b, c: (0, 0))),
                  scratch_shapes=[pltpu.VMEM((8, 8, SSM_LANES), F32), pltpu.VMEM((2, 8, SSM_LANES), F32),
                                  pltpu.VMEM((2, 8, SSM_LANES), F32)],
                  dims=("arbitrary", "arbitrary"))(abar, dx3, xs3)


US_BLOCK = (3 * ATT_WIDTH) // SSM_WIDTH


def _ssm_scan_fwd(proj3, abar, w_bu, w_c):
    B, S, _ = proj3.shape
    ch = min(S, SCAN_CHUNK)
    u_spec = pl.BlockSpec((1, ch, SSM_WIDTH), lambda b, c: (b, c, US_BLOCK))
    x_spec = pl.BlockSpec((1, ch, 2 * SSM_LANES), lambda b, c: (b, c, 0))
    y_spec = pl.BlockSpec((1, ch, SSM_WIDTH), lambda b, c: (b, c, 0))
    w_spec = pl.BlockSpec((SSM_WIDTH, 2 * SSM_LANES), lambda b, c: (0, 0))

    def body(ab_ref, u_ref, wb_ref, wc_ref, x_ref, y_ref, k_ref, carry_ref):
        _scan_consts(ab_ref[0:1, :], ab_ref[1:2, :], k_ref, False)

        @pl.when(pl.program_id(1) == 0)
        def _():
            carry_ref[...] = jnp.zeros_like(carry_ref)

        x_ref[0] = jnp.dot(u_ref[0], wb_ref[...], preferred_element_type=F32)

        def step(i, carry):
            base = pl.multiple_of(i * 8, 8)
            xr = x_ref[0, pl.ds(base, 8), 0:SSM_LANES]
            xi = x_ref[0, pl.ds(base, 8), SSM_LANES:]
            xr, xi = _scan_tile(xr, xi, k_ref, carry[0], carry[1], False)
            x_ref[0, pl.ds(base, 8), 0:SSM_LANES] = xr
            x_ref[0, pl.ds(base, 8), SSM_LANES:] = xi
            return (jnp.broadcast_to(xr[7:8], (8, SSM_LANES)), jnp.broadcast_to(xi[7:8], (8, SSM_LANES)))

        cr, ci = lax.fori_loop(0, ch // 8, step, (carry_ref[0], carry_ref[1]))
        carry_ref[0] = cr
        carry_ref[1] = ci
        y_ref[0] = lax.dot_general(x_ref[0].astype(BF16), wc_ref[...], NT_DIMS, preferred_element_type=F32)

    return _pcall(body, name="ssm_scan_fwd",
                  out_shape=(_sds((B, S, 2 * SSM_LANES), F32), _sds((B, S, SSM_WIDTH), F32)), grid=(B, S // ch),
                  in_specs=[pl.BlockSpec((2, SSM_LANES), lambda b, c: (0, 0)), u_spec, w_spec, w_spec],
                  out_specs=(x_spec, y_spec),
                  scratch_shapes=[pltpu.VMEM((8, 8, SSM_LANES), F32), pltpu.VMEM((2, 8, SSM_LANES), F32)],
                  dims=("arbitrary", "arbitrary"))(abar, proj3, w_bu, w_c)


def _ssm_scan_bwd(proj3, dy3, xs3, abar, w_bu, w_c, dsk):
    B, S, _ = proj3.shape
    ch = min(S, SCAN_CHUNK)
    nc = S // ch
    u_spec = pl.BlockSpec((1, ch, SSM_WIDTH), lambda b, c: (b, nc - 1 - c, US_BLOCK))
    x_spec = pl.BlockSpec((1, ch, 2 * SSM_LANES), lambda b, c: (b, nc - 1 - c, 0))
    y_spec = pl.BlockSpec((1, ch, SSM_WIDTH), lambda b, c: (b, nc - 1 - c, 0))
    w_spec = pl.BlockSpec((SSM_WIDTH, 2 * SSM_LANES), lambda b, c: (0, 0))
    ab_spec = pl.BlockSpec((2, SSM_LANES), lambda b, c: (0, 0))
    d_spec = pl.BlockSpec((1, SSM_WIDTH), lambda b, c: (0, 0))

    def body(ab_ref, u_ref, dy_ref, xs_ref, wb_ref, wc_ref, d_ref, du_ref, da_ref, dwb_ref, dwc_ref,
             g_ref, k_ref, carry_ref, acc_ref):
        b, c = pl.program_id(0), pl.program_id(1)
        _scan_consts(ab_ref[0:1, :], -ab_ref[1:2, :], k_ref, True)
        row = lax.broadcasted_iota(jnp.int32, (8, SSM_LANES), 0)

        @pl.when(c == 0)
        def _():
            carry_ref[...] = jnp.zeros_like(carry_ref)

        @pl.when((c == 0) & (b == 0))
        def _():
            acc_ref[...] = jnp.zeros_like(acc_ref)
            dwb_ref[...] = jnp.zeros_like(dwb_ref)
            dwc_ref[...] = jnp.zeros_like(dwc_ref)

        dy = dy_ref[0]
        dyb = dy.astype(BF16)
        g_ref[...] = jnp.dot(dyb, wc_ref[...], preferred_element_type=F32)

        def step(i, carry):
            car, cai, ar_acc, ai_acc = carry
            base = pl.multiple_of((ch // 8 - 1 - i) * 8, 8)
            gr = g_ref[pl.ds(base, 8), 0:SSM_LANES]
            gi = g_ref[pl.ds(base, 8), SSM_LANES:]
            gr, gi = _scan_tile(gr, gi, k_ref, car, cai, True)
            g_ref[pl.ds(base, 8), 0:SSM_LANES] = gr
            g_ref[pl.ds(base, 8), SSM_LANES:] = gi
            nr = jnp.where(row == 7, car, pltpu.roll(gr, 7, 0))
            ni = jnp.where(row == 7, cai, pltpu.roll(gi, 7, 0))
            xr = xs_ref[0, pl.ds(base, 8), 0:SSM_LANES]
            xi = xs_ref[0, pl.ds(base, 8), SSM_LANES:]
            ar_acc = ar_acc + nr * xr + ni * xi
            ai_acc = ai_acc + ni * xr - nr * xi
            return (jnp.broadcast_to(gr[0:1], (8, SSM_LANES)), jnp.broadcast_to(gi[0:1], (8, SSM_LANES)), ar_acc, ai_acc)

        cr, ci, ar_acc, ai_acc = lax.fori_loop(0, ch // 8, step, (carry_ref[0], carry_ref[1], acc_ref[0], acc_ref[1]))
        carry_ref[0] = cr
        carry_ref[1] = ci
        acc_ref[0] = ar_acc
        acc_ref[1] = ai_acc
        da_ref[0:1, :] = jnp.sum(ar_acc, axis=0, keepdims=True)
        da_ref[1:2, :] = jnp.sum(ai_acc, axis=0, keepdims=True)

        gb = g_ref[...].astype(BF16)
        du = lax.dot_general(gb, wb_ref[...], NT_DIMS, preferred_element_type=F32) + d_ref[...] * dy
        du_ref[0] = du.astype(BF16)
        dwb_ref[...] += lax.dot_general(u_ref[0], gb, TN_DIMS, preferred_element_type=F32)
        dwc_ref[...] += lax.dot_general(dyb, xs_ref[0].astype(BF16), TN_DIMS, preferred_element_type=F32)

    mat = _sds((SSM_WIDTH, 2 * SSM_LANES), F32)
    return _pcall(body, name="ssm_scan_bwd",
                  out_shape=(_sds((B, S, SSM_WIDTH), BF16), _sds((2, SSM_LANES), F32), mat, mat), grid=(B, nc),
                  in_specs=[ab_spec, u_spec, y_spec, x_spec, w_spec, w_spec, d_spec],
                  out_specs=(y_spec, ab_spec, w_spec, w_spec),
                  scratch_shapes=[pltpu.VMEM((ch, 2 * SSM_LANES), F32), pltpu.VMEM((8, 8, SSM_LANES), F32),
                                  pltpu.VMEM((2, 8, SSM_LANES), F32), pltpu.VMEM((2, 8, SSM_LANES), F32)],
                  dims=("arbitrary", "arbitrary"))(abar, proj3, dy3, xs3, w_bu, w_c, dsk)


GELU_K = math.sqrt(2.0 / math.pi)
GELU_C = 0.044715


def _gelu_parts(y):
    t = jnp.tanh(GELU_K * (y + GELU_C * y * y * y))
    return 0.5 * y * (1.0 + t), t


def _ssm_post(yc, us, dsk, wglu, bglu):
    T, N = yc.shape
    tm = min(T, 1024)
    row = pl.BlockSpec((tm, N), lambda i: (i, 0))
    vec = pl.BlockSpec((1, N), lambda i: (0, 0))
    mat = pl.BlockSpec((N, N), lambda i: (0, 0))

    def body(yc_ref, us_ref, d_ref, w_ref, b_ref, y_ref, s_ref):
        y = yc_ref[...] + d_ref[...] * us_ref[...]
        y_ref[...] = y
        z, _ = _gelu_parts(y)
        gl = jnp.dot(z.astype(BF16), w_ref[...], preferred_element_type=F32) + b_ref[...]
        s_ref[...] = (z * _sig(gl)).astype(BF16)

    return _pcall(body, name="ssm_post", out_shape=(_sds((T, N), F32), _sds((T, N), BF16)), grid=(T // tm,),
                  in_specs=[row, row, vec, mat, vec], out_specs=(row, row), dims=("parallel",))(yc, us, dsk, wglu, bglu)


def _ssm_post_bwd(y5, us, ds, dsk, wglu, bglu):
    T, N = y5.shape
    tm = min(T, 1024)
    row = pl.BlockSpec((tm, N), lambda i: (i, 0))
    vec = pl.BlockSpec((1, N), lambda i: (0, 0))
    mat = pl.BlockSpec((N, N), lambda i: (0, 0))

    def body(y_ref, us_ref, ds_ref, d_ref, w_ref, b_ref, dy_ref, dd_ref, db_ref, dw_ref):
        @pl.when(pl.program_id(0) == 0)
        def _():
            dd_ref[...] = jnp.zeros_like(dd_ref)
            db_ref[...] = jnp.zeros_like(db_ref)
            dw_ref[...] = jnp.zeros_like(dw_ref)

        y = y_ref[...]
        z, t = _gelu_parts(y)
        zb = z.astype(BF16)
        gl = jnp.dot(zb, w_ref[...], preferred_element_type=F32) + b_ref[...]
        sg = _sig(gl)
        ds = ds_ref[...]
        dgl = ds * z * sg * (1.0 - sg)
        dglb = dgl.astype(BF16)
        dz = ds * sg + lax.dot_general(dglb, w_ref[...], (((1,), (1,)), ((), ())), preferred_element_type=F32)
        dgelu = 0.5 * (1.0 + t) + 0.5 * y * (1.0 - t * t) * GELU_K * (1.0 + 3.0 * GELU_C * y * y)
        dy = dz * dgelu
        dy_ref[...] = dy
        dd_ref[...] += jnp.sum(dy * us_ref[...], axis=0, keepdims=True)
        db_ref[...] += jnp.sum(dgl, axis=0, keepdims=True)
        dw_ref[...] += lax.dot_general(zb, dglb, (((0,), (0,)), ((), ())), preferred_element_type=F32)

    return _pcall(body, name="ssm_post_bwd",
                  out_shape=(_sds((T, N), F32), _sds((1, N), F32), _sds((1, N), F32), _sds((N, N), F32)),
                  grid=(T // tm,), in_specs=[row, row, row, vec, mat, vec], out_specs=(row, vec, vec, mat),
                  dims=("arbitrary",))(y5, us, ds, dsk, wglu, bglu)


def _add_scaled_cast(a, b, s):
    T, N = a.shape
    tm = min(T, 1024)
    row = pl.BlockSpec((tm, N), lambda i: (i, 0))

    def body(a_ref, b_ref, s_ref, o_ref):
        o_ref[...] = (a_ref[...] + s_ref[...] * b_ref[...]).astype(BF16)

    return _pcall(body, name="add_scaled_cast", out_shape=_sds((T, N), BF16), grid=(T // tm,),
                  in_specs=[row, row, pl.BlockSpec((1, N), lambda i: (0, 0))], out_specs=row, dims=("parallel",))(a, b, s)


GATE_TILE = 256
GATE_ATT_BLOCK0 = (3 * ATT_WIDTH + SSM_WIDTH) // GATE_TILE
GATE_SSM_BLOCK0 = (3 * ATT_WIDTH + SSM_WIDTH + D_MODEL) // GATE_TILE


def _merge(proj, y_att, y_ssm, b_gate):
    T = proj.shape[0]
    tm = min(T, 1024)
    nj = D_MODEL // GATE_TILE
    ga = pl.BlockSpec((tm, GATE_TILE), lambda i, j: (i, GATE_ATT_BLOCK0 + j))
    gs = pl.BlockSpec((tm, GATE_TILE), lambda i, j: (i, GATE_SSM_BLOCK0 + j))
    yy = pl.BlockSpec((tm, GATE_TILE), lambda i, j: (i, j))
    ba = pl.BlockSpec((1, GATE_TILE), lambda i, j: (0, j))
    bs = pl.BlockSpec((1, GATE_TILE), lambda i, j: (0, nj + j))

    def body(ga_ref, gs_ref, ya_ref, ys_ref, ba_ref, bs_ref, o_ref):
        o_ref[...] = (_sig(ga_ref[...] + ba_ref[...]) * ya_ref[...]
                      + _sig(gs_ref[...] + bs_ref[...]) * ys_ref[...]).astype(BF16)

    return _pcall(body, name="merge", out_shape=_sds((T, D_MODEL), BF16), grid=(T // tm, nj),
                  in_specs=[ga, gs, yy, yy, ba, bs], out_specs=yy, dims=("parallel", "parallel"))(
        proj, proj, y_att, y_ssm, b_gate, b_gate)


def _merge_bwd(proj, y_att, y_ssm, b_gate, dmerged):
    T = proj.shape[0]
    tm = min(T, 1024)
    nj = D_MODEL // GATE_TILE
    ga = pl.BlockSpec((tm, GATE_TILE), lambda j, i: (i, GATE_ATT_BLOCK0 + j))
    gs = pl.BlockSpec((tm, GATE_TILE), lambda j, i: (i, GATE_SSM_BLOCK0 + j))
    yy = pl.BlockSpec((tm, GATE_TILE), lambda j, i: (i, j))
    ba = pl.BlockSpec((1, GATE_TILE), lambda j, i: (0, j))
    bs = pl.BlockSpec((1, GATE_TILE), lambda j, i: (0, nj + j))

    def body(ga_ref, gs_ref, ya_ref, ys_ref, ba_ref, bs_ref, dm_ref, dya_ref, dys_ref, dga_ref, dgs_ref, dba_ref, dbs_ref):
        @pl.when(pl.program_id(1) == 0)
        def _():
            dba_ref[...] = jnp.zeros_like(dba_ref)
            dbs_ref[...] = jnp.zeros_like(dbs_ref)

        dm = dm_ref[...].astype(F32)
        sa = _sig(ga_ref[...] + ba_ref[...])
        ss = _sig(gs_ref[...] + bs_ref[...])
        dya_ref[...] = (dm * sa).astype(BF16)
        dys_ref[...] = (dm * ss).astype(BF16)
        dga = dm * ya_ref[...] * sa * (1.0 - sa)
        dgs = dm * ys_ref[...] * ss * (1.0 - ss)
        dga_ref[...] = dga.astype(BF16)
        dgs_ref[...] = dgs.astype(BF16)
        dba_ref[...] += jnp.sum(dga, axis=0, keepdims=True)
        dbs_ref[...] += jnp.sum(dgs, axis=0, keepdims=True)

    big = _sds((T, D_MODEL), BF16)
    vec = _sds((1, D_MODEL), F32)
    return _pcall(body, name="merge_bwd", out_shape=(big, big, big, big, vec, vec), grid=(nj, T // tm),
                  in_specs=[ga, gs, yy, yy, ba, bs, yy], out_specs=(yy, yy, yy, yy, ba, ba),
                  dims=("arbitrary", "arbitrary"))(proj, proj, y_att, y_ssm, b_gate, b_gate, dmerged)


CONV_TILE = 256


def _conv_pre(a, w_ref, b_ref, row):
    conv = b_ref[...] + w_ref[0:1, :] * a
    shifted = []
    for j in (1, 2):
        sh = jnp.where(row >= j, pltpu.roll(a, j, 0), 0.0)
        shifted.append(sh)
        conv = conv + w_ref[j:j + 1, :] * sh
    return conv, shifted


def _conv_act(up3, w_conv, b_conv):
    B, S, _ = up3.shape
    nj = D_FF // CONV_TILE
    a_spec = pl.BlockSpec((1, S, CONV_TILE), lambda b, j: (b, 0, j))
    v_spec = pl.BlockSpec((1, S, CONV_TILE), lambda b, j: (b, 0, nj + j))
    w_spec = pl.BlockSpec((3, CONV_TILE), lambda b, j: (0, j))
    b_spec = pl.BlockSpec((1, CONV_TILE), lambda b, j: (0, j))

    def body(a_ref, v_ref, w_ref, b_ref, o_ref):
        a = a_ref[0].astype(F32)
        row = lax.broadcasted_iota(jnp.int32, a.shape, 0)
        conv, _ = _conv_pre(a, w_ref, b_ref, row)
        o_ref[0] = (conv * _sig(conv) * v_ref[0]).astype(BF16)

    return _pcall(body, name="conv_act", out_shape=_sds((B, S, D_FF), BF16), grid=(B, nj),
                  in_specs=[a_spec, v_spec, w_spec, b_spec], out_specs=a_spec, dims=("parallel", "parallel"))(
        up3, up3, w_conv, b_conv)


def _conv_bwd(up3, dact3, w_conv, b_conv):
    B, S, _ = up3.shape
    nj = D_FF // CONV_TILE
    a_spec = pl.BlockSpec((1, S, CONV_TILE), lambda j, b: (b, 0, j))
    v_spec = pl.BlockSpec((1, S, CONV_TILE), lambda j, b: (b, 0, nj + j))
    o_spec = pl.BlockSpec((2, 1, S, CONV_TILE), lambda j, b: (0, b, 0, j))
    w_spec = pl.BlockSpec((3, CONV_TILE), lambda j, b: (0, j))
    b_spec = pl.BlockSpec((1, CONV_TILE), lambda j, b: (0, j))

    def body(a_ref, v_ref, d_ref, w_ref, b_ref, dup_ref, dw_ref, db_ref):
        @pl.when(pl.program_id(1) == 0)
        def _():
            dw_ref[...] = jnp.zeros_like(dw_ref)
            db_ref[...] = jnp.zeros_like(db_ref)

        a = a_ref[0].astype(F32)
        d = d_ref[0].astype(F32)
        row = lax.broadcasted_iota(jnp.int32, a.shape, 0)
        conv, shifted = _conv_pre(a, w_ref, b_ref, row)
        sg = _sig(conv)
        dup_ref[1, 0] = (d * conv * sg).astype(BF16)
        dconv = d * v_ref[0] * (sg * (1.0 + conv * (1.0 - sg)))
        da = w_ref[0:1, :] * dconv
        for j in (1, 2):
            da = da + w_ref[j:j + 1, :] * jnp.where(row < S - j, pltpu.roll(dconv, S - j, 0), 0.0)
        dup_ref[0, 0] = da.astype(BF16)
        db_ref[...] += jnp.sum(dconv, axis=0, keepdims=True)
        dw_ref[0:1, :] += jnp.sum(dconv * a, axis=0, keepdims=True)
        dw_ref[1:2, :] += jnp.sum(dconv * shifted[0], axis=0, keepdims=True)
        dw_ref[2:3, :] += jnp.sum(dconv * shifted[1], axis=0, keepdims=True)

    return _pcall(body, name="conv_bwd",
                  out_shape=(_sds((2, B, S, D_FF), BF16), _sds((3, D_FF), F32), _sds((1, D_FF), F32)),
                  grid=(nj, B), in_specs=[a_spec, v_spec, a_spec, w_spec, b_spec],
                  out_specs=(o_spec, w_spec, b_spec), dims=("arbitrary", "arbitrary"))(up3, up3, dact3, w_conv, b_conv)


def _rows_tile(r, cap=640):
    for t in range(min(r, cap) - min(r, cap) % 8, 7, -8):
        if r % t == 0:
            return t
    return r


def _add2(a, b, out_dtype):
    R, N = a.shape
    tr = _rows_tile(R)
    spec = pl.BlockSpec((tr, N), lambda i: (i, 0))

    def body(a_ref, b_ref, o_ref):
        o_ref[...] = (a_ref[...] + b_ref[...]).astype(out_dtype)

    return _pcall(body, name="add2", out_shape=_sds((R, N), out_dtype), grid=(R // tr,), in_specs=[spec, spec],
                  out_specs=spec, dims=("parallel",))(a, b)


def _sum_slots(q, name):
    n, R, N = q.shape
    tr = _rows_tile(R)

    def body(q_ref, o_ref):
        acc = q_ref[0].astype(F32)
        for s in range(1, n):
            acc = acc + q_ref[s].astype(F32)
        o_ref[...] = acc

    return _pcall(body, name=name, out_shape=_sds((R, N), F32), grid=(R // tr,),
                  in_specs=[pl.BlockSpec((n, tr, N), lambda i: (0, i, 0))], out_specs=pl.BlockSpec((tr, N), lambda i: (i, 0)),
                  dims=("parallel",))(q)


NATIVE = (("b_re", 16, 1024), ("b_im", 16, 1024), ("c_re", 16, 1024), ("c_im", 16, 1024), ("g_mix", 1, 1024),
          ("b_att", 1, 1024), ("b_ssm", 1, 1024), ("a_re", 1, 1024), ("a_im", 1, 1024), ("log_dt", 1, 128),
          ("d_skip", 1, 256), ("b_glu", 1, 256), ("g_ffn", 1, 1024), ("g_final", 1, 1024), ("b_conv", 1, 2048),
          ("w_conv", 3, 2048))
N_MOD = 6


def _native_rows():
    starts, r = {}, 0
    for name, rows, cols in NATIVE:
        starts[name] = r
        r += rows * (-(-cols // LANES))
    n_sum = -(-r // 8) * 8
    return starts, n_sum


def _pack_small(native, dmods):
    starts, n_sum = _native_rows()
    B = dmods[0].shape[0]
    total = n_sum + 8 * N_MOD

    def body(*refs):
        xs, ms, o_ref = refs[:len(NATIVE)], refs[len(NATIVE):len(NATIVE) + N_MOD], refs[-1]
        o_ref[...] = jnp.zeros_like(o_ref)
        for (name, rows, cols), x_ref in zip(NATIVE, xs):
            chunks = -(-cols // LANES)
            if chunks == 1 and rows % 8 == 0:
                o_ref[starts[name]:starts[name] + rows, 0:cols] = x_ref[...]
                continue
            for i in range(rows):
                for q in range(chunks):
                    wd = min(LANES, cols - q * LANES)
                    r = starts[name] + i * chunks + q
                    o_ref[r:r + 1, 0:wd] = x_ref[i:i + 1, q * LANES:q * LANES + wd]
        for k, m_ref in enumerate(ms):
            for b in range(B):
                o_ref[n_sum + 8 * k + b:n_sum + 8 * k + b + 1, :] = m_ref[b]

    return _pcall(body, name="pack_small", out_shape=_sds((total, LANES), F32))(
        *[native[n] for n, _, _ in NATIVE], *dmods)


def _sum_unpack_small(gathered, B):
    starts, n_sum = _native_rows()
    nd = gathered.shape[0]

    def body(*refs):
        g_ref, outs, dm_ref, acc = refs[0], refs[1:1 + len(NATIVE)], refs[1 + len(NATIVE)], refs[-1]
        s = g_ref[0, 0:n_sum, :]
        for d in range(1, nd):
            s = s + g_ref[d, 0:n_sum, :]
        acc[...] = s
        for (name, rows, cols), o_ref in zip(NATIVE, outs):
            chunks = -(-cols // LANES)
            if chunks == 1 and rows % 8 == 0:
                o_ref[...] = acc[starts[name]:starts[name] + rows, 0:cols]
                continue
            for i in range(rows):
                for q in range(chunks):
                    wd = min(LANES, cols - q * LANES)
                    r = starts[name] + i * chunks + q
                    o_ref[i:i + 1, q * LANES:q * LANES + wd] = acc[r:r + 1, 0:wd]
        for d in range(nd):
            for k in range(N_MOD):
                dm_ref[d, :, k * D_MODEL:(k + 1) * D_MODEL] = g_ref[d, n_sum + 8 * k:n_sum + 8 * k + B, :]

    out_shape = tuple(_sds((rows, cols), F32) for _, rows, cols in NATIVE) + (_sds((nd, B, N_MOD * D_MODEL), F32),)
    res = _pcall(body, name="sum_unpack_small", out_shape=out_shape,
                 scratch_shapes=[pltpu.VMEM((n_sum, LANES), F32)])(gathered)
    return {n: r for (n, _, _), r in zip(NATIVE, res[:-1])}, res[-1]


def _small_from_native(nat):
    lanes3 = lambda a: a.reshape(SSM_GROUP_CH, SSM_GROUPS, SSM_STATE)
    return dict(
        g_mix=nat["g_mix"].reshape(D_MODEL), b_gate=jnp.concatenate([nat["b_att"], nat["b_ssm"]], axis=1).reshape(2 * D_MODEL),
        a_re=nat["a_re"].reshape(SSM_GROUPS, SSM_STATE), a_im=nat["a_im"].reshape(SSM_GROUPS, SSM_STATE),
        log_dt=nat["log_dt"][0, :SSM_GROUPS], b_re=_groups_from_lanes(nat["b_re"]), b_im=_groups_from_lanes(nat["b_im"]),
        c_re=lanes3(nat["c_re"]).transpose(1, 0, 2), c_im=lanes3(nat["c_im"]).transpose(1, 0, 2),
        d_skip=nat["d_skip"].reshape(SSM_WIDTH), b_glu=nat["b_glu"].reshape(SSM_WIDTH), g_ffn=nat["g_ffn"].reshape(D_MODEL),
        w_conv=nat["w_conv"], b_conv=nat["b_conv"].reshape(D_FF), g_final=nat["g_final"].reshape(D_MODEL))


def _adamw_multi(params):
    n = len(params)
    bc1 = 1.0 - ADAM_B1 ** ADAM_STEP
    bc2 = 1.0 - ADAM_B2 ** ADAM_STEP

    def body(*refs):
        ins, outs = refs[:4 * n], refs[4 * n:]
        for i in range(n):
            w_ref, g_ref, m_ref, v_ref = ins[4 * i:4 * i + 4]
            d_ref, nm_ref, nv_ref = outs[3 * i:3 * i + 3]
            g = g_ref[...]
            m = ADAM_B1 * m_ref[...] + (1.0 - ADAM_B1) * g
            v = ADAM_B2 * v_ref[...] + (1.0 - ADAM_B2) * (g * g)
            nm_ref[...] = m
            nv_ref[...] = v
            d_ref[...] = -ADAM_LR * ((m / bc1) / (jnp.sqrt(v / bc2) + ADAM_EPS) + ADAM_WD * w_ref[...])

    flat = [a for p in params for a in p]
    out_shape = tuple(_sds(p[0].shape, F32) for p in params for _ in range(3))
    res = _pcall(body, name="adamw_small", out_shape=out_shape)(*flat)
    return [tuple(res[3 * i:3 * i + 3]) for i in range(n)]


def _adamw(w, g, m, v, name):
    R, N = w.shape
    tr = _rows_tile(R) if R * N * 4 > (1 << 20) else R
    tr = min(tr, 256) if R % 256 == 0 and R > 256 else tr
    spec = pl.BlockSpec((tr, N), lambda i: (i, 0))
    bc1 = 1.0 - ADAM_B1 ** ADAM_STEP
    bc2 = 1.0 - ADAM_B2 ** ADAM_STEP

    def body(w_ref, g_ref, m_ref, v_ref, d_ref, nm_ref, nv_ref):
        g = g_ref[...]
        m = ADAM_B1 * m_ref[...] + (1.0 - ADAM_B1) * g
        v = ADAM_B2 * v_ref[...] + (1.0 - ADAM_B2) * (g * g)
        nm_ref[...] = m
        nv_ref[...] = v
        d_ref[...] = -ADAM_LR * ((m / bc1) / (jnp.sqrt(v / bc2) + ADAM_EPS) + ADAM_WD * w_ref[...])

    shp = _sds((R, N), F32)
    return _pcall(body, name=name, out_shape=(shp, shp, shp), grid=(R // tr,), in_specs=[spec] * 4,
                  out_specs=(spec, spec, spec), dims=("parallel",))(w, g, m, v)


_GROUP_MASKS = {
    "all": [(dx, dy, dc) for dx in (0, 1) for dy in (0, 1) for dc in (0, 1) if (dx, dy, dc) != (0, 0, 0)],
    "xy": [(1, 0, 0), (0, 1, 0), (1, 1, 0)],
    "c": [(0, 0, 1)],
}
_GROUP_SLOTS = {"all": 8, "xy": 4, "c": 2}


def _group_slot(group, x, y, c):
    return {"all": 4 * x + 2 * y + c, "xy": 2 * x + y, "c": c}[group]


def _flip(v, d):
    return 1 - v if d else v


def _exchange(arr, group, mode, name):
    return _exchange_list([arr], group, mode, name)[0]


def _exchange_list(arrs, group, mode, name):
    masks = _GROUP_MASKS[group]
    n = len(masks)
    na = len(arrs)
    out_shapes, halves, bounce = [], [], []
    for arr in arrs:
        if mode == "gather":
            out_shapes.append((_GROUP_SLOTS[group],) + arr.shape)
            bounce.append(pltpu.VMEM(arr.shape, arr.dtype))
        elif mode == "scatter":
            assert arr.shape[0] == _GROUP_SLOTS[group]
            out_shapes.append(arr.shape)
            bounce.append(pltpu.VMEM(arr.shape[1:], arr.dtype))
        elif mode == "swap":
            assert group == "c"
            out_shapes.append(arr.shape)
        else:
            assert group == "c"
            halves.append(arr.shape[1] // 2)
            out_shapes.append((arr.shape[0], arr.shape[1] // 2, arr.shape[2]))
    has_local = mode in ("gather", "scatter")

    def body(*refs):
        x_refs, o_refs = refs[:na], refs[na:2 * na]
        send_sems, recv_sems = refs[2 * na], refs[2 * na + 1]
        x, y, c = lax.axis_index("x"), lax.axis_index("y"), lax.axis_index("c")
        me = _group_slot(group, x, y, c)
        if has_local:
            local_sems = refs[2 * na + 2]
            bufs = refs[2 * na + 3:]
            loads = []
            for i in range(na):
                src = x_refs[i] if mode == "gather" else x_refs[i].at[me]
                loads.append(pltpu.make_async_copy(src, bufs[i], local_sems.at[2 * i]))
                loads[-1].start()
        copies = []
        for i in range(na):
            x_ref, o_ref = x_refs[i], o_refs[i]
            for k, (dx, dy, dc) in enumerate(masks):
                px, py, pc = _flip(x, dx), _flip(y, dy), _flip(c, dc)
                if mode == "gather":
                    src, dst = x_ref, o_ref.at[me]
                elif mode == "scatter":
                    src, dst = x_ref.at[_group_slot(group, px, py, pc)], o_ref.at[me]
                elif mode == "swap":
                    src, dst = x_ref, o_ref
                else:
                    src, dst = x_ref.at[:, pl.ds(pl.multiple_of(pc * halves[i], 8), halves[i]), :], o_ref
                cp = pltpu.make_async_remote_copy(src_ref=src, dst_ref=dst, send_sem=send_sems.at[i * n + k],
                                                  recv_sem=recv_sems.at[i * n + k], device_id=(px, py, pc),
                                                  device_id_type=pl.DeviceIdType.MESH)
                cp.start()
                copies.append(cp)
        if has_local:
            stores = []
            for i in range(na):
                loads[i].wait()
                stores.append(pltpu.make_async_copy(bufs[i], o_refs[i].at[me], local_sems.at[2 * i + 1]))
                stores[-1].start()
        for cp in copies:
            cp.wait()
        if has_local:
            for st in stores:
                st.wait()

    anyspec = pl.BlockSpec(memory_space=pl.ANY)
    scratch = [pltpu.SemaphoreType.DMA((n * na,)), pltpu.SemaphoreType.DMA((n * na,))]
    if has_local:
        scratch += [pltpu.SemaphoreType.DMA((2 * na,))] + bounce
    outs = pl.pallas_call(body, name=name, out_shape=tuple(_sds(s, a.dtype) for s, a in zip(out_shapes, arrs)),
                          in_specs=[anyspec] * na, out_specs=tuple([anyspec] * na), scratch_shapes=scratch,
                          compiler_params=pltpu.CompilerParams(vmem_limit_bytes=V7X_VMEM_LIMIT_BYTES))(*arrs)
    return list(outs)


def _gather_weights(shards, name):
    na = len(shards)
    masks = _GROUP_MASKS["xy"]
    n = len(masks)

    def body(*refs):
        x_refs, o_refs = refs[:na], refs[na:2 * na]
        send_sems, recv_sems, local_sems = refs[2 * na:2 * na + 3]
        bufs = refs[2 * na + 3:]
        x, y, c = lax.axis_index("x"), lax.axis_index("y"), lax.axis_index("c")
        me = 2 * x + y
        sibling = (x, y, 1 - c)
        loads = []
        for i in range(na):
            loads.append(pltpu.make_async_copy(x_refs[i], bufs[i], local_sems.at[2 * i]))
            loads[-1].start()

        def half_of(i, slot, cc):
            h = shards[i].shape[0] // 2
            return o_refs[i].at[slot, pl.ds(pl.multiple_of(cc * h, 8), h), :]

        def src_half(i, cc):
            h = shards[i].shape[0] // 2
            return x_refs[i].at[pl.ds(pl.multiple_of(cc * h, 8), h), :]

        sends = []
        for i in range(na):
            for k, (dx, dy, _) in enumerate(masks):
                cp = pltpu.make_async_remote_copy(src_ref=src_half(i, c), dst_ref=half_of(i, me, c),
                                                  send_sem=send_sems.at[i * 2 * n + k], recv_sem=recv_sems.at[i * 2 * n + k],
                                                  device_id=(_flip(x, dx), _flip(y, dy), c),
                                                  device_id_type=pl.DeviceIdType.MESH)
                cp.start()
                sends.append(cp)
        stores = []
        for i in range(na):
            loads[i].wait()
            stores.append(pltpu.make_async_copy(bufs[i], o_refs[i].at[me], local_sems.at[2 * i + 1]))
            stores[-1].start()
        for i in range(na):
            for k, (dx, dy, _) in enumerate(masks):
                slot = 2 * _flip(x, dx) + _flip(y, dy)
                landed = pltpu.make_async_remote_copy(src_ref=src_half(i, c), dst_ref=half_of(i, slot, c),
                                                      send_sem=send_sems.at[i * 2 * n + k],
                                                      recv_sem=recv_sems.at[i * 2 * n + k], device_id=sibling,
                                                      device_id_type=pl.DeviceIdType.MESH)
                landed.wait_recv()
                fwd = pltpu.make_async_remote_copy(src_ref=half_of(i, slot, c), dst_ref=half_of(i, slot, c),
                                                   send_sem=send_sems.at[i * 2 * n + n + k],
                                                   recv_sem=recv_sems.at[i * 2 * n + n + k], device_id=sibling,
                                                   device_id_type=pl.DeviceIdType.MESH)
                fwd.start()
                sends.append(fwd)
        for i in range(na):
            for k, (dx, dy, _) in enumerate(masks):
                slot = 2 * _flip(x, dx) + _flip(y, dy)
                pltpu.make_async_remote_copy(src_ref=half_of(i, slot, 1 - c), dst_ref=half_of(i, slot, 1 - c),
                                             send_sem=send_sems.at[i * 2 * n + n + k],
                                             recv_sem=recv_sems.at[i * 2 * n + n + k], device_id=sibling,
                                             device_id_type=pl.DeviceIdType.MESH).wait_recv()
        for cp in sends:
            cp.wait_send()
        for st in stores:
            st.wait()

    anyspec = pl.BlockSpec(memory_space=pl.ANY)
    scratch = [pltpu.SemaphoreType.DMA((2 * n * na,)), pltpu.SemaphoreType.DMA((2 * n * na,)),
               pltpu.SemaphoreType.DMA((2 * na,))] + [pltpu.VMEM(s.shape, s.dtype) for s in shards]
    outs = pl.pallas_call(body, name=name, out_shape=tuple(_sds((N_XY,) + s.shape, s.dtype) for s in shards),
                          in_specs=[anyspec] * na, out_specs=tuple([anyspec] * na), scratch_shapes=scratch,
                          compiler_params=pltpu.CompilerParams(vmem_limit_bytes=V7X_VMEM_LIMIT_BYTES))(*shards)
    return list(outs)


def _pair_add(g, theirs, core, name):
    n4, h2, w = g.shape
    h = h2 // 2
    tr = _rows_tile(h)
    nb = h // tr

    def body(c_ref, g_ref, t_ref, o_ref):
        o_ref[...] = (g_ref[...] + t_ref[...]).astype(BF16)

    grid_spec = pltpu.PrefetchScalarGridSpec(
        num_scalar_prefetch=1, grid=(n4, nb),
        in_specs=[pl.BlockSpec((None, tr, w), lambda j, i, c_ref: (j, c_ref[0] * nb + i, 0)),
                  pl.BlockSpec((None, tr, w), lambda j, i, c_ref: (j, i, 0))],
        out_specs=pl.BlockSpec((None, tr, w), lambda j, i, c_ref: (j, i, 0)))
    return pl.pallas_call(body, name=name, out_shape=_sds((n4, h, w), BF16), grid_spec=grid_spec,
                          compiler_params=pltpu.CompilerParams(vmem_limit_bytes=V7X_VMEM_LIMIT_BYTES,
                                                               dimension_semantics=("parallel", "parallel")))(core, g, theirs)


BIG = (("w_proj_att", (ATT_WIDTH, D_MODEL), 1), ("w_proj_ssm", (SSM_WIDTH, D_MODEL), 1),
       ("w_glu", (SSM_WIDTH, SSM_WIDTH), 0))
DIRECT = (("w_in", True), ("w_up", True), ("w_down", False), ("w_out", False))
N_XY = 4


def _big_rows(shape):
    return shape[0] * shape[1] // N_XY // LANES


FLAT_ROWS = sum(_big_rows(s) for _, s, _ in BIG)


def _shard_shape(shape, axis):
    return (shape[0] // N_XY, shape[1]) if axis == 0 else (shape[0], shape[1] // N_XY)


def _flatten_shards(shards):
    return jnp.concatenate([shards[n].reshape(_big_rows(s), LANES) for n, s, _ in BIG], axis=0)


def _unflatten_shard(flat):
    out, r = {}, 0
    for n, s, ax in BIG:
        k = _big_rows(s)
        out[n] = flat[r:r + k].reshape(_shard_shape(s, ax))
        r += k
    return out


def _unflatten_full(flat4):
    out, r = {}, 0
    for n, s, ax in BIG:
        k = _big_rows(s)
        sh = _shard_shape(s, ax)
        t = flat4[:, r:r + k].reshape((N_XY,) + sh)
        out[n] = t.reshape(s) if ax == 0 else t.transpose(1, 0, 2).reshape(s)
        r += k
    return out


def _flatten_full(full):
    parts = []
    for n, s, ax in BIG:
        sh = _shard_shape(s, ax)
        t = full[n]
        t = t.reshape((N_XY,) + sh) if ax == 0 else t.reshape(s[0], N_XY, sh[1]).transpose(1, 0, 2)
        parts.append(t.reshape(N_XY, _big_rows(s), LANES))
    return jnp.concatenate(parts, axis=1)


def _pack_rows(arrs):
    rows, counts = [], []
    for a in arrs:
        f = a.reshape(-1)
        k = -(-f.shape[0] // LANES)
        rows.append(jnp.pad(f, (0, k * LANES - f.shape[0])).reshape(k, LANES))
        counts.append(k)
    return jnp.concatenate(rows, axis=0), counts


def _unpack_rows(buf, shapes):
    out, r = [], 0
    for s in shapes:
        size = int(np.prod(s))
        k = -(-size // LANES)
        out.append(buf[r:r + k].reshape(-1)[:size].reshape(s))
        r += k
    return out


def _lanes_from_groups(a):
    return a.transpose(2, 0, 1).reshape(SSM_GROUP_CH, SSM_LANES)


def _groups_from_lanes(a):
    return a.reshape(SSM_GROUP_CH, SSM_GROUPS, SSM_STATE).transpose(1, 2, 0)


LATE = ("w_up_t", "w_down", "w_out")
EARLY_GRADS = ("w_up_t", "w_down")


def _local_step(x3, mod, tgt3, W, P, late_shards=None, pair_fn=None):
    B, S, _ = x3.shape
    T = B * S
    seq_blocks = S // ATT_BLOCK
    sh1, sc1, gt1, sh2, sc2, gt2 = [m.reshape(B, 1, D_MODEL) for m in jnp.split(mod, 6, axis=-1)]
    g_mix, g_ffn, g_final = P["g_mix"].reshape(1, D_MODEL), P["g_ffn"].reshape(1, D_MODEL), P["g_final"].reshape(1, D_MODEL)
    b_gate = P["b_gate"].reshape(1, 2 * D_MODEL)
    d_skip, b_glu = P["d_skip"].reshape(1, SSM_WIDTH), P["b_glu"].reshape(1, SSM_WIDTH)
    w_conv, b_conv = P["w_conv"], P["b_conv"].reshape(1, D_FF)

    u1 = _norm_mod(x3, g_mix, sc1, sh1).reshape(T, D_MODEL)
    proj = _mm(u1, W["w_in_t"], tb=True, name="mm_proj", out_dtype=BF16)
    proj3 = proj.reshape(B, S, IN_WIDTH)
    us = proj[:, 3 * ATT_WIDTH:3 * ATT_WIDTH + SSM_WIDTH]
    o_att3, lse4, late = _attention_fwd(proj3, seq_blocks, _Riders(late_shards, "gather") if late_shards else None)
    if late_shards:
        W = dict(W, **{n: f.reshape(-1, LANES) for n, f in zip(LATE, late)})
    o_att = o_att3.reshape(T, ATT_WIDTH)
    y_att = _mm(o_att, W["w_proj_att"], name="mm_proj_att", out_dtype=BF16)

    lr = P["a_re"].reshape(1, SSM_LANES)
    li = P["a_im"].reshape(1, SSM_LANES)
    ldt = jnp.repeat(P["log_dt"], SSM_STATE).reshape(1, SSM_LANES)
    br, bi = _lanes_from_groups(P["b_re"]), _lanes_from_groups(P["b_im"])
    cr = P["c_re"].transpose(1, 0, 2).reshape(SSM_GROUP_CH, SSM_LANES)
    ci = P["c_im"].transpose(1, 0, 2).reshape(SSM_GROUP_CH, SSM_LANES)
    abar, w_bu, w_c = _ssm_params(lr, li, ldt, br, bi, cr, ci)
    xs3, y_core3 = _ssm_scan_fwd(proj3, abar, w_bu, w_c)
    y5, s_out = _ssm_post(y_core3.reshape(T, SSM_WIDTH), us, d_skip, W["w_glu"], b_glu)
    y_ssm = _mm(s_out, W["w_proj_ssm"], name="mm_proj_ssm", out_dtype=BF16)

    merged = _merge(proj, y_att, y_ssm, b_gate)
    mix = _mm(merged, W["w_out"], name="mm_out", out_dtype=BF16)
    mix3 = mix.reshape(B, S, D_MODEL)

    h1, u2 = _resid_norm_mod(x3, mix3, gt1, g_ffn, sc2, sh2)
    u2 = u2.reshape(T, D_MODEL)
    up3 = _mm(u2, W["w_up_t"], tb=True, name="mm_up", out_dtype=BF16).reshape(B, S, 2 * D_FF)
    act = _conv_act(up3, w_conv, b_conv).reshape(T, D_FF)
    ffn3 = _mm(act, W["w_down"], name="mm_down", out_dtype=BF16).reshape(B, S, D_MODEL)
    dh2, dffn, dgt2, dg_final, loss = _final_loss(h1, ffn3, tgt3, gt2, g_final)

    dffn = dffn.reshape(T, D_MODEL)
    gw = {}
    gw["w_down"] = _mm(act, dffn, ta=True, name="mm_dw_down")
    dact3 = _mm(dffn, W["w_down"], tb=True, name="mm_dact", out_dtype=BF16).reshape(B, S, D_FF)
    dup3, dw_conv, db_conv = _conv_bwd(up3, dact3, w_conv, b_conv)
    dup = dup3.reshape(2, T, D_FF)
    gw["w_up_t"] = _mm(dup, u2, ta=True, name="mm_dw_up")
    du2 = _mm(dup, W["w_up_t"], name="mm_du2", out_dtype=BF16).reshape(B, S, D_MODEL)
    dh1, dsh2, dsc2, dg_ffn, dgt1, dmix = _norm_bwd(h1, du2, dh2, g_ffn, sc2, "norm_bwd2", mix3=mix3, gt=gt1)

    dmix = dmix.reshape(T, D_MODEL)
    gw["w_out"] = _mm(merged, dmix, ta=True, name="mm_dw_out")
    dmerged = _mm(dmix, W["w_out"], tb=True, name="mm_dmerged", out_dtype=BF16)
    dy_att, dy_ssm, dga, dgs, db_att, db_ssm = _merge_bwd(proj, y_att, y_ssm, b_gate, dmerged)

    gw["w_proj_ssm"] = _mm(s_out, dy_ssm, ta=True, name="mm_dw_proj_ssm")
    ds_out = _mm(dy_ssm, W["w_proj_ssm"], tb=True, name="mm_ds_out")
    dy5, dd_skip, db_glu, dw_glu = _ssm_post_bwd(y5, us, ds_out, d_skip, W["w_glu"], b_glu)
    gw["w_glu"] = dw_glu
    dus3, dab, dwbu, dwc = _ssm_scan_bwd(proj3, dy5.reshape(B, S, SSM_WIDTH), xs3, abar, w_bu, w_c, d_skip)
    dus = dus3.reshape(T, SSM_WIDTH)
    dlr, dli, dldt, dbr, dbi, dcr, dci = _ssm_params_bwd(lr, li, ldt, br, bi, dab, dwbu, dwc)

    gw["w_proj_att"] = _mm(o_att, dy_att, ta=True, name="mm_dw_proj_att")
    do_att = _mm(dy_att, W["w_proj_att"], tb=True, out_dtype=BF16, name="mm_do_att")
    pairs = pair_fn([gw[n] for n in EARLY_GRADS]) if pair_fn else None
    dq3, dk3, dv3, early_parts = _attention_bwd(proj3, do_att.reshape(B, S, ATT_WIDTH), o_att3, lse4, seq_blocks,
                                                _Riders(pairs, "scatter") if pairs else None)
    dproj = jnp.concatenate([t.reshape(T, ATT_WIDTH) for t in (dq3, dk3, dv3)] + [dus, dga, dgs], axis=1)
    gw["w_in_t"] = _mm(dproj, u1, ta=True, name="mm_dw_in")
    du1 = _mm(dproj, W["w_in_t"], name="mm_du1", out_dtype=BF16).reshape(B, S, D_MODEL)
    dx, dsh1, dsc1, dg_mix = _norm_bwd(x3, du1, dh1, g_mix, sc1, "norm_bwd1")

    dmods = [dsh1, dsc1, dgt1, dsh2, dsc2, dgt2]
    native = dict(g_mix=dg_mix, b_att=db_att, b_ssm=db_ssm, a_re=dlr, a_im=dli, log_dt=dldt, b_re=dbr, b_im=dbi, c_re=dcr,
                  c_im=dci, d_skip=dd_skip, b_glu=db_glu, g_ffn=dg_ffn, w_conv=dw_conv, b_conv=db_conv, g_final=dg_final)
    return loss, dx, dmods, gw, native, early_parts


WEIGHTS = ['w_ada', 'b_ada', 'g_mix', 'w_in', 'b_gate', 'a_re', 'a_im', 'log_dt', 'b_re', 'b_im', 'c_re', 'c_im', 'd_skip',
           'w_glu', 'b_glu', 'w_proj_att', 'w_proj_ssm', 'w_out', 'g_ffn', 'w_up', 'w_conv', 'b_conv', 'w_down', 'g_final']
SMALL = ['g_mix', 'b_gate', 'a_re', 'a_im', 'log_dt', 'b_re', 'b_im', 'c_re', 'c_im', 'd_skip', 'b_glu', 'g_ffn', 'w_conv',
         'b_conv', 'g_final']


def kernel(x, c, w_ada, b_ada, g_mix, w_in, b_gate, a_re, a_im, log_dt, b_re, b_im, c_re, c_im, d_skip, w_glu, b_glu, w_proj_att, w_proj_ssm, w_out, g_ffn, w_up, w_conv, b_conv, w_down, g_final, loss_target, m_w_ada, m_b_ada, m_g_mix, m_w_in, m_b_gate, m_a_re, m_a_im, m_log_dt, m_b_re, m_b_im, m_c_re, m_c_im, m_d_skip, m_w_glu, m_b_glu, m_w_proj_att, m_w_proj_ssm, m_w_out, m_g_ffn, m_w_up, m_w_conv, m_b_conv, m_w_down, m_g_final, v_w_ada, v_b_ada, v_g_mix, v_w_in, v_b_gate, v_a_re, v_a_im, v_log_dt, v_b_re, v_b_im, v_c_re, v_c_im, v_d_skip, v_w_glu, v_b_glu, v_w_proj_att, v_w_proj_ssm, v_w_out, v_g_ffn, v_w_up, v_w_conv, v_b_conv, v_w_down, v_g_final):
    args = dict(locals())
    w = {n: args[n] for n in WEIGHTS}
    m = {n: args["m_" + n] for n in WEIGHTS}
    v = {n: args["v_" + n] for n in WEIGHTS}
    B, S, _ = x.shape
    ix, iy, ic = lax.axis_index("x"), lax.axis_index("y"), lax.axis_index("c")
    chip = 2 * ix + iy
    half = FLAT_ROWS // 2
    ada_cols = w_ada.shape[2]

    c_all = _exchange(c, "all", "gather", "gather_c").reshape(8 * B, D_MODEL)
    b_cols = lax.dynamic_slice_in_dim(b_ada, chip * ada_cols, ada_cols, axis=1)
    mod_cols = _ada_fwd(c_all, w_ada[0], b_cols)
    mod_all = _exchange(mod_cols, "xy", "gather", "gather_mod")
    mod_all = mod_all.transpose(1, 0, 2).reshape(8 * B, 6 * D_MODEL)
    mod = lax.dynamic_slice_in_dim(mod_all, (4 * ix + 2 * iy + ic) * B, B, axis=0)

    south = ic == 0
    core = ic.astype(jnp.int32).reshape(1)
    shard = {n + ("_t" if t else ""): (w[n][0].T if t else w[n][0]).astype(BF16) for n, t in DIRECT}
    misc = _flatten_shards({n: w[n][0] for n, _, _ in BIG}).astype(BF16)
    w_in_full, misc_full = _gather_weights([shard["w_in_t"], misc], "gather_weights")
    W = {"w_in_t": w_in_full.reshape(-1, LANES)}
    W.update(_unflatten_full(misc_full))

    def pair_fn(gs):
        slots = [g.reshape(N_XY, -1, LANES) for g in gs]
        theirs = _exchange_list(slots, "c", "half", "reduce_cores_early")
        return [_pair_add(g, t, core, "pair_add_early_%d" % i) for i, (g, t) in enumerate(zip(slots, theirs))]

    wc_all = _exchange(w_conv[0], "xy", "gather", "gather_w_conv")
    P = {n: w[n][0] for n in SMALL if n not in ("w_conv", "g_final")}
    P["w_conv"] = wc_all.transpose(1, 0, 2).reshape(3, D_FF)
    P["g_final"] = g_final

    loss, dx, dmods, gw, native, early_parts = _local_step(x, mod, loss_target, W, P, [shard[n] for n in LATE], pair_fn)

    loss = lax.psum(loss[0, 0], MESH_AXES)

    gathered = _exchange(_pack_small(native, dmods), "all", "gather", "gather_small")
    native_sum, dmod_all = _sum_unpack_small(gathered, B)
    g_small = _small_from_native(native_sum)
    dmod_all = dmod_all.reshape(8 * B, N_MOD * D_MODEL)
    dmod_cols = lax.dynamic_slice_in_dim(dmod_all, chip * ada_cols, ada_cols, axis=1)
    g_w_ada, g_b_ada = _ada_bwd(c_all, dmod_all, dmod_cols)

    end_names = [n for n, t in DIRECT if n + ("_t" if t else "") not in EARLY_GRADS]
    G = [gw[n + ("_t" if t else "")].reshape(N_XY, -1, LANES) for n, t in DIRECT if n in end_names]
    G.append(_flatten_full({n: gw[n] for n, _, _ in BIG}))
    theirs = _exchange_list(G, "c", "half", "reduce_cores")
    pair = [_pair_add(g, t, core, "pair_add_%d" % i) for i, (g, t) in enumerate(zip(G, theirs))]
    parts = early_parts + _exchange_list(pair, "xy", "scatter", "reduce_chips")
    red = [_sum_slots(p, "sum_chips_%d" % i) for i, p in enumerate(parts)]
    red_sib = _exchange_list(red, "c", "swap", "share_cores")
    reduced = [jnp.concatenate([jnp.where(south, r, s), jnp.where(south, s, r)], axis=0) for r, s in zip(red, red_sib)]

    grads = {"w_ada": g_w_ada[None], "b_ada": g_b_ada}
    order = [n[:-2] if n.endswith("_t") else n for n in EARLY_GRADS] + end_names
    transposed = dict(DIRECT)
    for n, g in zip(order, reduced):
        grads[n] = (g.T if transposed[n] else g)[None]
    for n, g in _unflatten_shard(reduced[-1]).items():
        grads[n] = g[None]
    wc_cols = w_conv.shape[2]
    for n in SMALL:
        g = g_small[n]
        if n == "w_conv":
            g = lax.dynamic_slice_in_dim(g, chip * wc_cols, wc_cols, axis=1)
        grads[n] = g.reshape(w[n].shape)

    delta, new_m, new_v = {}, {}, {}
    for n in ["w_ada"] + [b for b, _ in DIRECT] + [b for b, _, _ in BIG]:
        shp = w[n].shape
        d2, m2, v2 = _adamw(w[n][0], grads[n][0], m[n][0], v[n][0], "adamw_" + n)
        delta[n], new_m[n], new_v[n] = d2.reshape(shp), m2.reshape(shp), v2.reshape(shp)
    rest = ["b_ada"] + SMALL

    def drop(a):
        return a.reshape(1, -1) if a.ndim == 1 else (a if a.ndim == 2 else a[0])

    upd = _adamw_multi([(drop(w[n]), drop(grads[n]), drop(m[n]), drop(v[n])) for n in rest])
    for n, (dd, mm, vv) in zip(rest, upd):
        delta[n], new_m[n], new_v[n] = dd.reshape(w[n].shape), mm.reshape(w[n].shape), vv.reshape(w[n].shape)

    return (loss, dx, *[grads[n] for n in WEIGHTS], *[delta[n] for n in WEIGHTS], *[new_m[n] for n in WEIGHTS],
            *[new_v[n] for n in WEIGHTS])
```

```python
import functools
import math

import numpy as np
import jax
import jax.numpy as jnp
from jax import lax
from jax.experimental import pallas as pl
from jax.experimental.pallas import tpu as pltpu

F32, BF16 = jnp.float32, jnp.bfloat16

D_MODEL = 1024
N_HEADS = 8
HEAD_DIM = 64
ATT_WIDTH = 512
SSM_GROUPS = 16
SSM_GROUP_CH = 16
SSM_WIDTH = 256
SSM_STATE = 64
SSM_LANES = SSM_GROUPS * SSM_STATE
D_FF = 2048
IN_WIDTH = 3 * ATT_WIDTH + SSM_WIDTH + 2 * D_MODEL
ATT_BLOCK = 128
N_PATTERNS = 3
EPS = 1e-6
NEG_INF = -1e30

ADAM_LR, ADAM_B1, ADAM_B2, ADAM_EPS, ADAM_WD, ADAM_STEP = 0.001, 0.9, 0.999, 1e-08, 0.01, 10

V7X_VMEM_LIMIT_BYTES = 56 * 1024 * 1024
LANES = 1024

MESH_AXES = ("x", "y", "c")


def _pcall(body, *, name, out_shape, grid=(), in_specs=None, out_specs=None, scratch_shapes=(), dims=None):
    params = dict(vmem_limit_bytes=V7X_VMEM_LIMIT_BYTES)
    if dims is not None:
        params["dimension_semantics"] = dims
    specs = {}
    if in_specs is not None:
        specs = dict(grid=grid, in_specs=in_specs, out_specs=out_specs)
    return pl.pallas_call(body, name=name, out_shape=out_shape, scratch_shapes=scratch_shapes,
                          compiler_params=pltpu.CompilerParams(**params), **specs)


def _sds(shape, dtype):
    return jax.ShapeDtypeStruct(tuple(shape), dtype)


def _tile(n, target):
    if n <= target:
        return n
    for t in range(target - target % 128, 0, -128):
        if n % t == 0:
            return t
    raise ValueError((n, target))


def _sig(v):
    return pl.reciprocal(1.0 + jnp.exp(-v), approx=True)


def _mm(a, b, *, name, ta=False, tb=False, out_dtype=F32, tm=2048, tn=1024, tk=1024, riders=None):
    halves = a.ndim == 3
    if halves:
        a_rows, a_cols = a.shape[1], 2 * a.shape[2]
    else:
        a_rows, a_cols = a.shape
    if ta:
        K, M = a_rows, a_cols
    else:
        M, K = a_rows, a_cols
    if tb:
        N, K2 = b.shape
    else:
        K2, N = b.shape
    assert K == K2, (a.shape, b.shape)
    if halves:
        tm, tk = (min(tm, M // 2), tk) if ta else (tm, min(tk, K // 2))
    tm, tn, tk = _tile(M, tm), _tile(N, tn), _tile(K, tk)
    nk = K // tk
    if halves and ta:
        per = a.shape[2] // tm
        a_spec = pl.BlockSpec((None, tk, tm), lambda i, j, k: (i // per, k, i % per))
    elif halves:
        per = a.shape[2] // tk
        a_spec = pl.BlockSpec((None, tm, tk), lambda i, j, k: (k // per, i, k % per))
    else:
        a_spec = pl.BlockSpec((tk, tm), lambda i, j, k: (k, i)) if ta else pl.BlockSpec((tm, tk), lambda i, j, k: (i, k))
    b_spec = pl.BlockSpec((tn, tk), lambda i, j, k: (j, k)) if tb else pl.BlockSpec((tk, tn), lambda i, j, k: (k, j))
    dn = (((0 if ta else 1,), (1 if tb else 0,)), ((), ()))

    def body(a_ref, b_ref, o_ref, acc_ref):
        k = pl.program_id(2)

        @pl.when(k == 0)
        def _():
            acc_ref[...] = jnp.zeros_like(acc_ref)

        acc_ref[...] += lax.dot_general(a_ref[...].astype(BF16), b_ref[...].astype(BF16), dn,
                                        preferred_element_type=F32)

        @pl.when(k == nk - 1)
        def _():
            o_ref[...] = acc_ref[...].astype(out_dtype)

    def body_single(a_ref, b_ref, o_ref):
        o_ref[...] = lax.dot_general(a_ref[...].astype(BF16), b_ref[...].astype(BF16), dn,
                                     preferred_element_type=F32).astype(out_dtype)

    grid = (M // tm, N // tn, nk)
    scratch = [] if nk == 1 else [pltpu.VMEM((tm, tn), F32)]
    o_spec = pl.BlockSpec((tm, tn), lambda i, j, k: (i, j))
    if riders is None:
        return _pcall(body_single if nk == 1 else body, name=name, out_shape=_sds((M, N), out_dtype), grid=grid,
                      in_specs=[a_spec, b_spec], out_specs=o_spec, scratch_shapes=scratch,
                      dims=("parallel", "parallel", "arbitrary"))(a, b)
    rs = riders
    res = _pcall(_with_riders(body_single if nk == 1 else body, rs, 2, 1, len(scratch), tuple(g - 1 for g in grid)),
                 name=name, out_shape=(_sds((M, N), out_dtype),) + tuple(rs.out_shape), grid=grid,
                 in_specs=[a_spec, b_spec] + rs.specs, out_specs=(o_spec,) + tuple(rs.specs),
                 scratch_shapes=scratch + rs.scratch, dims=("arbitrary", "arbitrary", "arbitrary"))(a, b, *rs.arrs)
    return res[0], list(res[1:])


def _ada_fwd(c_all, w_ada, b_ada_cols):
    n = w_ada.shape[1]

    def body(c_ref, w_ref, b_ref, o_ref):
        c = c_ref[...]
        act = c * _sig(c)
        o_ref[...] = jnp.dot(act.astype(BF16), w_ref[...].astype(BF16), preferred_element_type=F32) + b_ref[...]

    return _pcall(body, name="ada_fwd", out_shape=_sds((c_all.shape[0], n), F32))(c_all, w_ada, b_ada_cols)


def _ada_bwd(c_all, dmod_all, dmod_cols):
    n = dmod_cols.shape[1]

    def body(c_ref, da_ref, dc_ref, gw_ref, gb_ref):
        c = c_ref[...]
        act = c * _sig(c)
        gw_ref[...] = lax.dot_general(act, dc_ref[...], (((0,), (0,)), ((), ())), preferred_element_type=F32,
                                      precision=lax.Precision.HIGHEST)
        gb_ref[...] = jnp.sum(da_ref[...], axis=0, keepdims=True)

    return _pcall(body, name="ada_bwd", out_shape=(_sds((D_MODEL, n), F32), _sds((1, dmod_all.shape[1]), F32)))(
        c_all, dmod_all, dmod_cols)


ROW_TILE = 512


def _row_specs(B, S):
    ts = min(S, ROW_TILE)
    row = pl.BlockSpec((1, ts, D_MODEL), lambda b, s: (b, s, 0))
    bvec = pl.BlockSpec((1, 1, D_MODEL), lambda b, s: (b, 0, 0))
    gvec = pl.BlockSpec((1, D_MODEL), lambda b, s: (0, 0))
    return ts, row, bvec, gvec


def _norm_mod(x3, g, sc, sh):
    B, S, _ = x3.shape
    ts, row, bvec, gvec = _row_specs(B, S)

    def body(x_ref, g_ref, sc_ref, sh_ref, u_ref):
        x = x_ref[0]
        r = lax.rsqrt(jnp.mean(x * x, axis=-1, keepdims=True) + EPS)
        u_ref[0] = ((x * r) * g_ref[...] * (1.0 + sc_ref[0]) + sh_ref[0]).astype(BF16)

    return _pcall(body, name="norm_mod1", out_shape=_sds(x3.shape, BF16), grid=(B, S // ts),
                  in_specs=[row, gvec, bvec, bvec], out_specs=row, dims=("parallel", "parallel"))(x3, g, sc, sh)


def _resid_norm_mod(x3, mix3, gt, g, sc, sh):
    B, S, _ = x3.shape
    ts, row, bvec, gvec = _row_specs(B, S)

    def body(x_ref, m_ref, gt_ref, g_ref, sc_ref, sh_ref, h_ref, u_ref):
        h = x_ref[0] + gt_ref[0] * m_ref[0]
        h_ref[0] = h
        r = lax.rsqrt(jnp.mean(h * h, axis=-1, keepdims=True) + EPS)
        u_ref[0] = ((h * r) * g_ref[...] * (1.0 + sc_ref[0]) + sh_ref[0]).astype(BF16)

    return _pcall(body, name="resid_norm_mod2", out_shape=(_sds(x3.shape, F32), _sds(x3.shape, BF16)),
                  grid=(B, S // ts), in_specs=[row, row, bvec, gvec, bvec, bvec], out_specs=(row, row),
                  dims=("parallel", "parallel"))(x3, mix3, gt, g, sc, sh)


def _norm_bwd(h3, du3, dres3, g, sc, name, mix3=None, gt=None):
    B, S, _ = h3.shape
    ts, row, bvec, gvec = _row_specs(B, S)
    with_gate = mix3 is not None

    def body(*refs):
        if with_gate:
            h_ref, du_ref, dr_ref, g_ref, sc_ref, m_ref, gt_ref, dh_ref, dsh_ref, dsc_ref, dg_ref, dgt_ref, dm_ref = refs
        else:
            h_ref, du_ref, dr_ref, g_ref, sc_ref, dh_ref, dsh_ref, dsc_ref, dg_ref = refs
        b, s = pl.program_id(0), pl.program_id(1)
        h = h_ref[0]
        r = lax.rsqrt(jnp.mean(h * h, axis=-1, keepdims=True) + EPS)
        xn = h * r
        du = du_ref[0].astype(F32)
        g = g_ref[...]
        sc1 = 1.0 + sc_ref[0]
        dxn = du * g * sc1
        dh = dr_ref[0] + r * (dxn - xn * jnp.mean(dxn * xn, axis=-1, keepdims=True))
        dh_ref[0] = dh

        @pl.when(s == 0)
        def _():
            dsh_ref[...] = jnp.zeros_like(dsh_ref)
            dsc_ref[...] = jnp.zeros_like(dsc_ref)
            if with_gate:
                dgt_ref[...] = jnp.zeros_like(dgt_ref)

        @pl.when((s == 0) & (b == 0))
        def _():
            dg_ref[...] = jnp.zeros_like(dg_ref)

        dux = du * xn
        dsh_ref[0] += jnp.sum(du, axis=0, keepdims=True)
        dsc_ref[0] += jnp.sum(dux * g, axis=0, keepdims=True)
        dg_ref[...] += jnp.sum(dux * sc1, axis=0, keepdims=True)
        if with_gate:
            dgt_ref[0] += jnp.sum(dh * m_ref[0], axis=0, keepdims=True)
            dm_ref[0] = (dh * gt_ref[0]).astype(BF16)

    bshape = _sds((B, 1, D_MODEL), F32)
    in_specs = [row, row, row, gvec, bvec]
    out_shape = [_sds(h3.shape, F32), bshape, bshape, _sds((1, D_MODEL), F32)]
    out_specs = [row, bvec, bvec, gvec]
    args = [h3, du3, dres3, g, sc]
    if with_gate:
        in_specs += [row, bvec]
        out_shape += [bshape, _sds(h3.shape, BF16)]
        out_specs += [bvec, row]
        args += [mix3, gt]
    return _pcall(body, name=name, out_shape=tuple(out_shape), grid=(B, S // ts), in_specs=in_specs,
                  out_specs=tuple(out_specs), dims=("arbitrary", "arbitrary"))(*args)


def _final_loss(h1, ffn3, tgt3, gt, gfin):
    B, S, _ = h1.shape
    ts, row, bvec, gvec = _row_specs(B, S)
    one = pl.BlockSpec((1, 1), lambda b, s: (0, 0))

    def body(h_ref, f_ref, t_ref, gt_ref, gf_ref, dh_ref, dff_ref, dgt_ref, dgf_ref, loss_ref):
        b, s = pl.program_id(0), pl.program_id(1)
        f = f_ref[0].astype(F32)
        gtv = gt_ref[0]
        gf = gf_ref[...]
        h2 = h_ref[0] + gtv * f
        r = lax.rsqrt(jnp.mean(h2 * h2, axis=-1, keepdims=True) + EPS)
        n = h2 * r
        e = n * gf - t_ref[0]
        dy = e * (1.0 / D_MODEL)
        dn = dy * gf
        dh2 = r * (dn - n * jnp.mean(dn * n, axis=-1, keepdims=True))
        dh_ref[0] = dh2
        dff_ref[0] = (dh2 * gtv).astype(BF16)

        @pl.when(s == 0)
        def _():
            dgt_ref[...] = jnp.zeros_like(dgt_ref)

        @pl.when((s == 0) & (b == 0))
        def _():
            dgf_ref[...] = jnp.zeros_like(dgf_ref)
            loss_ref[...] = jnp.zeros_like(loss_ref)

        dgt_ref[0] += jnp.sum(dh2 * f, axis=0, keepdims=True)
        dgf_ref[...] += jnp.sum(dy * n, axis=0, keepdims=True)
        rows = jnp.sum(e * e, axis=1, keepdims=True)
        loss_ref[...] += jnp.sum(rows, axis=0, keepdims=True) * (0.5 / D_MODEL)

    return _pcall(body, name="final_loss",
                  out_shape=(_sds(h1.shape, F32), _sds(h1.shape, BF16), _sds((B, 1, D_MODEL), F32),
                             _sds((1, D_MODEL), F32), _sds((1, 1), F32)),
                  grid=(B, S // ts), in_specs=[row, row, row, bvec, gvec], out_specs=(row, row, bvec, gvec, one),
                  dims=("arbitrary", "arbitrary"))(h1, ffn3, tgt3, gt, gfin)


def _att_scores(qh, kc, kp, h, dil, first, a_idx, j_idx):
    scale = HEAD_DIM ** -0.5
    nt = (((1,), (1,)), ((), ()))
    slope = (2.0 ** (-8.0 * (h + 1) / N_HEADS)) * dil
    dist_c = (a_idx - j_idx).astype(F32)
    s_c = lax.dot_general(qh, kc, nt, preferred_element_type=F32) * scale
    s_c = jnp.where(a_idx >= j_idx, s_c - slope * dist_c, NEG_INF)
    s_p = lax.dot_general(qh, kp, nt, preferred_element_type=F32) * scale
    s_p = jnp.where((j_idx >= a_idx) & jnp.logical_not(first), s_p - slope * (dist_c + float(ATT_BLOCK)), NEG_INF)
    return s_c, s_p


def _att_block_consts(seq_blocks):
    p = pl.program_id(0)
    j = pl.program_id(1)
    nb = lax.shift_right_logical(jnp.int32(seq_blocks), 2 * p)
    dil = lax.shift_left(jnp.int32(1), 2 * p).astype(F32)
    a_idx = lax.broadcasted_iota(jnp.int32, (ATT_BLOCK, ATT_BLOCK), 0)
    j_idx = lax.broadcasted_iota(jnp.int32, (ATT_BLOCK, ATT_BLOCK), 1)
    return j, nb, dil, a_idx, j_idx


def _attn_fwd(qb, kb, vb, seq_blocks):
    _, NB, _, _ = qb.shape
    cur = pl.BlockSpec((None, None, ATT_BLOCK, ATT_WIDTH), lambda p, j: (p, j, 0, 0))
    prev = pl.BlockSpec((None, None, ATT_BLOCK, ATT_WIDTH), lambda p, j: (p, jnp.maximum(j - 1, 0), 0, 0))
    lse_spec = pl.BlockSpec((None, None, ATT_BLOCK, N_HEADS), lambda p, j: (p, j, 0, 0))

    def body(q_ref, kc_ref, kp_ref, vc_ref, vp_ref, o_ref, lse_ref):
        j, nb, dil, a_idx, j_idx = _att_block_consts(seq_blocks)
        first = lax.rem(j, nb) == 0
        for h in range(N_HEADS):
            hs = slice(h * HEAD_DIM, (h + 1) * HEAD_DIM)
            s_c, s_p = _att_scores(q_ref[:, hs], kc_ref[:, hs], kp_ref[:, hs], h, dil, first, a_idx, j_idx)
            m = jnp.maximum(jnp.max(s_c, axis=1, keepdims=True), jnp.max(s_p, axis=1, keepdims=True))
            p_c = jnp.exp(s_c - m)
            p_p = jnp.exp(s_p - m)
            den = jnp.sum(p_c, axis=1, keepdims=True) + jnp.sum(p_p, axis=1, keepdims=True)
            o = (jnp.dot(p_c.astype(BF16), vc_ref[:, hs], preferred_element_type=F32)
                 + jnp.dot(p_p.astype(BF16), vp_ref[:, hs], preferred_element_type=F32))
            o_ref[:, hs] = o / den
            lse_ref[:, h:h + 1] = m + jnp.log(den)

    return _pcall(body, name="attn_fwd",
                  out_shape=(_sds(qb.shape, F32), _sds((N_PATTERNS, NB, ATT_BLOCK, N_HEADS), F32)),
                  grid=(N_PATTERNS, NB), in_specs=[cur, cur, prev, cur, prev], out_specs=(cur, lse_spec),
                  dims=("parallel", "parallel"))(qb, kb, kb, vb, vb)


def _attn_combine(o_p, lse_p):
    _, T, _ = o_p.shape
    tm = min(T, 1024)

    def body(o_ref, l_ref, out_ref, lse_ref):
        l0, l1, l2 = l_ref[0], l_ref[1], l_ref[2]
        m = jnp.maximum(jnp.maximum(l0, l1), l2)
        lse = m + jnp.log(jnp.exp(l0 - m) + jnp.exp(l1 - m) + jnp.exp(l2 - m))
        lse_ref[...] = lse
        w = [jnp.exp(l0 - lse), jnp.exp(l1 - lse), jnp.exp(l2 - lse)]
        for h in range(N_HEADS):
            hs = slice(h * HEAD_DIM, (h + 1) * HEAD_DIM)
            acc = w[0][:, h:h + 1] * o_ref[0, :, hs]
            acc = acc + w[1][:, h:h + 1] * o_ref[1, :, hs]
            acc = acc + w[2][:, h:h + 1] * o_ref[2, :, hs]
            out_ref[:, hs] = acc.astype(BF16)

    return _pcall(body, name="attn_combine", out_shape=(_sds((T, ATT_WIDTH), BF16), _sds((T, N_HEADS), F32)),
                  grid=(T // tm,),
                  in_specs=[pl.BlockSpec((N_PATTERNS, tm, ATT_WIDTH), lambda i: (0, i, 0)),
                            pl.BlockSpec((N_PATTERNS, tm, N_HEADS), lambda i: (0, i, 0))],
                  out_specs=(pl.BlockSpec((tm, ATT_WIDTH), lambda i: (i, 0)), pl.BlockSpec((tm, N_HEADS), lambda i: (i, 0))),
                  dims=("parallel",))(o_p, lse_p)


def _attn_bwd(qb, kb, vb, dob, ob, lseb, seq_blocks):
    _, NB, _, _ = qb.shape
    last = NB - 1
    cur = pl.BlockSpec((None, None, ATT_BLOCK, ATT_WIDTH), lambda p, j: (p, jnp.minimum(j, last), 0, 0))
    prev = pl.BlockSpec((None, None, ATT_BLOCK, ATT_WIDTH),
                        lambda p, j: (p, jnp.maximum(jnp.minimum(j, last) - 1, 0), 0, 0))
    lag = pl.BlockSpec((None, None, ATT_BLOCK, ATT_WIDTH), lambda p, j: (p, jnp.maximum(j - 1, 0), 0, 0))
    lse_spec = pl.BlockSpec((None, None, ATT_BLOCK, N_HEADS), lambda p, j: (p, jnp.minimum(j, last), 0, 0))
    scale = HEAD_DIM ** -0.5
    tn = (((0,), (0,)), ((), ()))
    nt = (((1,), (1,)), ((), ()))

    def body(q_ref, kc_ref, kp_ref, vc_ref, vp_ref, do_ref, o_ref, lse_ref, dq_ref, dk_ref, dv_ref, ck_ref, cv_ref):
        j, nb, dil, a_idx, j_idx = _att_block_consts(seq_blocks)

        @pl.when(j == 0)
        def _():
            ck_ref[...] = jnp.zeros_like(ck_ref)
            cv_ref[...] = jnp.zeros_like(cv_ref)

        @pl.when(j <= last)
        def _():
            first = lax.rem(j, nb) == 0
            for h in range(N_HEADS):
                hs = slice(h * HEAD_DIM, (h + 1) * HEAD_DIM)
                qh, kc, kp, vc, vp, doh = q_ref[:, hs], kc_ref[:, hs], kp_ref[:, hs], vc_ref[:, hs], vp_ref[:, hs], do_ref[:, hs]
                s_c, s_p = _att_scores(qh, kc, kp, h, dil, first, a_idx, j_idx)
                lse = lse_ref[:, h:h + 1]
                p_c = jnp.exp(s_c - lse)
                p_p = jnp.exp(s_p - lse)
                delta = jnp.sum(doh.astype(F32) * o_ref[:, hs].astype(F32), axis=1, keepdims=True)
                ds_c = (p_c * (lax.dot_general(doh, vc, nt, preferred_element_type=F32) - delta)).astype(BF16)
                ds_p = (p_p * (lax.dot_general(doh, vp, nt, preferred_element_type=F32) - delta)).astype(BF16)
                dq_ref[:, hs] = (jnp.dot(ds_c, kc, preferred_element_type=F32)
                                 + jnp.dot(ds_p, kp, preferred_element_type=F32)) * scale
                dk_ref[:, hs] = ck_ref[:, hs] + lax.dot_general(ds_p, qh, tn, preferred_element_type=F32) * scale
                dv_ref[:, hs] = cv_ref[:, hs] + lax.dot_general(p_p.astype(BF16), doh, tn, preferred_element_type=F32)
                ck_ref[:, hs] = lax.dot_general(ds_c, qh, tn, preferred_element_type=F32) * scale
                cv_ref[:, hs] = lax.dot_general(p_c.astype(BF16), doh, tn, preferred_element_type=F32)

        @pl.when(j == NB)
        def _():
            dk_ref[...] = ck_ref[...]
            dv_ref[...] = cv_ref[...]

    shp = _sds(qb.shape, F32)
    return _pcall(body, name="attn_bwd", out_shape=(shp, shp, shp), grid=(N_PATTERNS, NB + 1),
                  in_specs=[cur, cur, prev, cur, prev, cur, cur, lse_spec], out_specs=(cur, lag, lag),
                  scratch_shapes=[pltpu.VMEM((ATT_BLOCK, ATT_WIDTH), F32), pltpu.VMEM((ATT_BLOCK, ATT_WIDTH), F32)],
                  dims=("arbitrary", "arbitrary"))(qb, kb, kb, vb, vb, dob, ob, lseb)


def _sum3_cast(a, b, c):
    T, N = a.shape
    tm = min(T, 1024)
    spec = pl.BlockSpec((tm, N), lambda i: (i, 0))

    def body(a_ref, b_ref, c_ref, o_ref):
        o_ref[...] = (a_ref[...] + b_ref[...] + c_ref[...]).astype(BF16)

    return _pcall(body, name="sum3_cast", out_shape=_sds((T, N), BF16), grid=(T // tm,), in_specs=[spec] * 3,
                  out_specs=spec, dims=("parallel",))(a, b, c)


def _to_blocks(t, B, S):
    C = t.shape[-1]
    outs = []
    for p in range(N_PATTERNS):
        d = 4 ** p
        u = t.reshape(B, S // d, d, C).transpose(0, 2, 1, 3)
        outs.append(u.reshape(B * S // ATT_BLOCK, ATT_BLOCK, C))
    return jnp.stack(outs, axis=0)


def _from_blocks(tb, B, S):
    C = tb.shape[-1]
    outs = []
    for p in range(N_PATTERNS):
        d = 4 ** p
        u = tb[p].reshape(B, d, S // d, C).transpose(0, 2, 1, 3)
        outs.append(u.reshape(B * S, C))
    return jnp.stack(outs, axis=0)


ATT_GROUP = 4
ATT_GW = ATT_GROUP * HEAD_DIM
ATT_GROUPS = N_HEADS // ATT_GROUP
ATT_PAIRS = ATT_GW // ATT_BLOCK
ATT_UNROLL = 3
NT_DIMS = (((1,), (1,)), ((), ()))
TN_DIMS = (((0,), (0,)), ((), ()))


def _att_rows(start, d):
    if d == 1:
        return pl.ds(start if isinstance(start, int) else pl.multiple_of(start, ATT_BLOCK), ATT_BLOCK)
    return pl.ds(start, ATT_BLOCK, stride=d)


def _att_fill_bias(bias_ref, g, d):
    a = lax.broadcasted_iota(jnp.int32, (ATT_BLOCK, ATT_BLOCK), 0)
    j = lax.broadcasted_iota(jnp.int32, (ATT_BLOCK, ATT_BLOCK), 1)
    dist = (a - j).astype(F32)
    for hh in range(ATT_GROUP):
        t, e = divmod(hh, 2)
        rs = slice(e * ATT_BLOCK, (e + 1) * ATT_BLOCK)
        lo = 2.0 ** (-8.0 * (hh + 1) / N_HEADS) * d
        hi = 2.0 ** (-8.0 * (ATT_GROUP + hh + 1) / N_HEADS) * d
        slope = jnp.where(g == 0, lo, hi).astype(F32)
        bias_ref[t, rs, 0:ATT_BLOCK] = jnp.where(a >= j, -slope * dist, NEG_INF)
        bias_ref[t, rs, ATT_BLOCK:] = jnp.where(j >= a, -slope * (dist + float(ATT_BLOCK)), NEG_INF)


def _stack_heads(v2, low):
    return jnp.concatenate([jnp.where(low, v2, 0.0), jnp.where(low, 0.0, v2)], axis=0).astype(BF16)


def _unstack_heads(r2, low):
    return jnp.where(low, r2[0:ATT_BLOCK], r2[ATT_BLOCK:])


class _Riders:
    def __init__(self, arrs, mode):
        self.arrs, self.mode, self.n = list(arrs), mode, len(arrs)
        slot_shapes = [a.shape if mode == "gather" else a.shape[1:] for a in self.arrs]
        self.out_shape = [_sds((N_XY,) + s, a.dtype) for s, a in zip(slot_shapes, self.arrs)]
        k = len(_GROUP_MASKS["xy"])
        self.scratch = [pltpu.SemaphoreType.DMA((k * self.n,)), pltpu.SemaphoreType.DMA((k * self.n,)),
                        pltpu.SemaphoreType.DMA((2 * self.n,))] + [pltpu.VMEM(s, a.dtype) for s, a in zip(slot_shapes, self.arrs)]
        self.specs = [pl.BlockSpec(memory_space=pl.ANY)] * self.n

    def _remote(self, x_refs, o_refs, send_sems, recv_sems):
        x, y, c = lax.axis_index("x"), lax.axis_index("y"), lax.axis_index("c")
        me = 2 * x + y
        cps = []
        for i in range(self.n):
            for k, (dx, dy, _) in enumerate(_GROUP_MASKS["xy"]):
                px, py = _flip(x, dx), _flip(y, dy)
                src = x_refs[i] if self.mode == "gather" else x_refs[i].at[2 * px + py]
                cps.append(pltpu.make_async_remote_copy(
                    src_ref=src, dst_ref=o_refs[i].at[me], send_sem=send_sems.at[3 * i + k], recv_sem=recv_sems.at[3 * i + k],
                    device_id=(px, py, c), device_id_type=pl.DeviceIdType.MESH))
        return cps, me

    def start(self, x_refs, o_refs, scratch):
        send_sems, recv_sems, local_sems, bufs = scratch[0], scratch[1], scratch[2], scratch[3:]
        cps, me = self._remote(x_refs, o_refs, send_sems, recv_sems)
        for cp in cps:
            cp.start()
        for i in range(self.n):
            src = x_refs[i] if self.mode == "gather" else x_refs[i].at[me]
            load = pltpu.make_async_copy(src, bufs[i], local_sems.at[2 * i])
            load.start()
            load.wait()
            pltpu.make_async_copy(bufs[i], o_refs[i].at[me], local_sems.at[2 * i + 1]).start()

    def wait(self, x_refs, o_refs, scratch):
        send_sems, recv_sems, local_sems, bufs = scratch[0], scratch[1], scratch[2], scratch[3:]
        cps, me = self._remote(x_refs, o_refs, send_sems, recv_sems)
        for cp in cps:
            cp.wait()
        for i in range(self.n):
            pltpu.make_async_copy(bufs[i], o_refs[i].at[me], local_sems.at[2 * i + 1]).wait()


def _with_riders(compute, riders, n_in, n_out, n_scratch, last_step):
    if riders is None:
        return compute
    n = riders.n

    def body(*refs):
        ins, x_refs = refs[:n_in], refs[n_in:n_in + n]
        outs, o_refs = refs[n_in + n:n_in + n + n_out], refs[n_in + n + n_out:n_in + 2 * n + n_out]
        scratch = refs[n_in + 2 * n + n_out:]
        own, ride = scratch[:n_scratch], scratch[n_scratch:]
        ids = [pl.program_id(i) for i in range(len(last_step))]
        first = functools.reduce(jnp.logical_and, [i == 0 for i in ids])
        last = functools.reduce(jnp.logical_and, [i == l for i, l in zip(ids, last_step)])

        @pl.when(first)
        def _():
            riders.start(x_refs, o_refs, ride)

        compute(*ins, *outs, *own)

        @pl.when(last)
        def _():
            riders.wait(x_refs, o_refs, ride)

    return body


def _attention_fwd(proj3, seq_blocks, riders=None):
    B, S, _ = proj3.shape
    scale = HEAD_DIM ** -0.5
    nq = ATT_WIDTH // ATT_GW

    def col(k):
        return pl.BlockSpec((1, S, ATT_GW), lambda b, g, k=k: (b, 0, k * nq + g))

    o_spec = pl.BlockSpec((1, S, ATT_GW), lambda b, g: (b, 0, g))
    l_spec = pl.BlockSpec((1, 1, S, ATT_BLOCK), lambda b, g: (b, g, 0, 0))

    def compute(q_ref, k_ref, v_ref, o_ref, lse_ref, qf, kf, vf, os, ls, bias):
        g = pl.program_id(1)
        for t in range(ATT_PAIRS):
            ts = slice(t * ATT_BLOCK, (t + 1) * ATT_BLOCK)
            qf[t] = q_ref[0, :, ts].astype(F32) * scale
            kf[t] = k_ref[0, :, ts].astype(F32)
            vf[t] = v_ref[0, :, ts].astype(F32)
        lane = lax.broadcasted_iota(jnp.int32, (ATT_BLOCK, ATT_BLOCK), 1)
        low = lane < HEAD_DIM

        def block(p, d, r, n, has_prev):
            start = n * (ATT_BLOCK * d) + r
            rows = _att_rows(start, d)
            prows = _att_rows(start - ATT_BLOCK * d, d) if has_prev else None
            lse_t = jnp.zeros((ATT_BLOCK, ATT_BLOCK), F32)
            for t in range(ATT_PAIRS):
                q2 = _stack_heads(qf[t, rows, :], low)
                k2 = kf[t, rows, :].astype(BF16)
                v2 = vf[t, rows, :].astype(BF16)
                if has_prev:
                    k2 = jnp.concatenate([k2, kf[t, prows, :].astype(BF16)], axis=0)
                    v2 = jnp.concatenate([v2, vf[t, prows, :].astype(BF16)], axis=0)
                    b2 = bias[t]
                else:
                    b2 = bias[t, :, 0:ATT_BLOCK]
                s = lax.dot_general(q2, k2, NT_DIMS, preferred_element_type=F32) + b2
                m = jnp.max(s, axis=1, keepdims=True)
                pr = jnp.exp(s - m)
                den = jnp.sum(pr, axis=1, keepdims=True)
                o = jnp.dot(pr.astype(BF16), v2, preferred_element_type=F32) * (1.0 / den)
                os[p, t, rows, :] = _unstack_heads(o, low)
                lse2 = m + jnp.log(den)
                lse_t = jnp.where(lane == 2 * t, lse2[0:ATT_BLOCK], lse_t)
                lse_t = jnp.where(lane == 2 * t + 1, lse2[ATT_BLOCK:], lse_t)
            ls[p, rows, :] = lse_t

        for p in range(N_PATTERNS):
            d = 4 ** p
            _att_fill_bias(bias, g, d)
            _att_one_pattern(block, p, d, seq_blocks // d)

        def combine(i, carry):
            rows = pl.ds(pl.multiple_of(i * ATT_BLOCK, ATT_BLOCK), ATT_BLOCK)
            l0, l1, l2 = ls[0, rows, :], ls[1, rows, :], ls[2, rows, :]
            m = jnp.maximum(jnp.maximum(l0, l1), l2)
            lse = m + jnp.log(jnp.exp(l0 - m) + jnp.exp(l1 - m) + jnp.exp(l2 - m))
            lse_ref[0, 0, rows, :] = lse
            w = [jnp.exp(l0 - lse), jnp.exp(l1 - lse), jnp.exp(l2 - lse)]
            for t in range(ATT_PAIRS):
                acc = jnp.zeros((ATT_BLOCK, ATT_BLOCK), F32)
                for p in range(N_PATTERNS):
                    wt = jnp.where(low, w[p][:, 2 * t:2 * t + 1], w[p][:, 2 * t + 1:2 * t + 2])
                    acc = acc + wt * os[p, t, rows, :]
                o_ref[0, rows, t * ATT_BLOCK:(t + 1) * ATT_BLOCK] = acc.astype(BF16)
            return carry

        lax.fori_loop(0, S // ATT_BLOCK, combine, 0, unroll=2)

    scratch = ([pltpu.VMEM((ATT_PAIRS, S, ATT_BLOCK), F32)] * 3
               + [pltpu.VMEM((N_PATTERNS, ATT_PAIRS, S, ATT_BLOCK), F32), pltpu.VMEM((N_PATTERNS, S, ATT_BLOCK), F32),
                  pltpu.VMEM((ATT_PAIRS, 2 * ATT_BLOCK, 2 * ATT_BLOCK), F32)])
    rs = riders
    res = _pcall(_with_riders(compute, rs, 3, 2, len(scratch), (B - 1, ATT_GROUPS - 1)), name="attention_fwd",
                 out_shape=(_sds((B, S, ATT_WIDTH), BF16), _sds((B, ATT_GROUPS, S, ATT_BLOCK), F32))
                 + (tuple(rs.out_shape) if rs else ()),
                 grid=(B, ATT_GROUPS), in_specs=[col(0), col(1), col(2)] + (rs.specs if rs else []),
                 out_specs=(o_spec, l_spec) + (tuple(rs.specs) if rs else ()),
                 scratch_shapes=scratch + (rs.scratch if rs else []),
                 dims=("arbitrary", "arbitrary"))(proj3, proj3, proj3, *(rs.arrs if rs else []))
    return res[0], res[1], list(res[2:])


def _att_one_pattern(block, p, d, nb):
    def per_residue(r, carry):
        block(p, d, r, 0, False)
        if nb > 1:
            def per_block(n, c2):
                block(p, d, r, n, True)
                return c2
            lax.fori_loop(1, nb, per_block, 0, unroll=ATT_UNROLL)
        return carry

    if d == 1:
        per_residue(0, 0)
    else:
        lax.fori_loop(0, d, per_residue, 0, unroll=ATT_UNROLL + 1 if nb == 1 else 1)


def _attention_bwd(proj3, do3, o3, lse4, seq_blocks, riders=None):
    B, S, _ = proj3.shape
    scale = HEAD_DIM ** -0.5
    nq = ATT_WIDTH // ATT_GW

    def col(k):
        return pl.BlockSpec((1, S, ATT_GW), lambda b, g, k=k: (b, 0, k * nq + g))

    o_spec = pl.BlockSpec((1, S, ATT_GW), lambda b, g: (b, 0, g))
    l_spec = pl.BlockSpec((1, 1, S, ATT_BLOCK), lambda b, g: (b, g, 0, 0))

    def compute(q_ref, k_ref, v_ref, do_ref, o_ref, lse_ref, dq_ref, dk_ref, dv_ref,
                qf, kf, vf, dof, dl, aq, ak, av, bias):
        g = pl.program_id(1)
        for t in range(ATT_PAIRS):
            ts = slice(t * ATT_BLOCK, (t + 1) * ATT_BLOCK)
            qf[t] = q_ref[0, :, ts].astype(F32) * scale
            kf[t] = k_ref[0, :, ts].astype(F32)
            vf[t] = v_ref[0, :, ts].astype(F32)
            dof[t] = do_ref[0, :, ts].astype(F32)
        aq[...] = jnp.zeros_like(aq)
        ak[...] = jnp.zeros_like(ak)
        av[...] = jnp.zeros_like(av)
        lane = lax.broadcasted_iota(jnp.int32, (ATT_BLOCK, ATT_BLOCK), 1)
        low = lane < HEAD_DIM

        def fill_delta(i, carry):
            rows = pl.ds(pl.multiple_of(i * ATT_BLOCK, ATT_BLOCK), ATT_BLOCK)
            acc = jnp.zeros((ATT_BLOCK, ATT_BLOCK), F32)
            for t in range(ATT_PAIRS):
                prod = dof[t, rows, :] * o_ref[0, rows, t * ATT_BLOCK:(t + 1) * ATT_BLOCK].astype(F32)
                lo = jnp.sum(jnp.where(low, prod, 0.0), axis=1, keepdims=True)
                hi = jnp.sum(prod, axis=1, keepdims=True) - lo
                acc = jnp.where(lane == 2 * t, lo, acc)
                acc = jnp.where(lane == 2 * t + 1, hi, acc)
            dl[rows, :] = acc
            return carry

        lax.fori_loop(0, S // ATT_BLOCK, fill_delta, 0, unroll=2)

        def block(p, d, r, n, has_prev):
            start = n * (ATT_BLOCK * d) + r
            rows = _att_rows(start, d)
            prows = _att_rows(start - ATT_BLOCK * d, d) if has_prev else None
            lse_t = lse_ref[0, 0, rows, :]
            dl_t = dl[rows, :]
            for t in range(ATT_PAIRS):
                q2 = _stack_heads(qf[t, rows, :], low)
                do2 = _stack_heads(dof[t, rows, :], low)
                k2 = kf[t, rows, :].astype(BF16)
                v2 = vf[t, rows, :].astype(BF16)
                if has_prev:
                    k2 = jnp.concatenate([k2, kf[t, prows, :].astype(BF16)], axis=0)
                    v2 = jnp.concatenate([v2, vf[t, prows, :].astype(BF16)], axis=0)
                    b2 = bias[t]
                else:
                    b2 = bias[t, :, 0:ATT_BLOCK]
                lse2 = jnp.concatenate([lse_t[:, 2 * t:2 * t + 1], lse_t[:, 2 * t + 1:2 * t + 2]], axis=0)
                dl2 = jnp.concatenate([dl_t[:, 2 * t:2 * t + 1], dl_t[:, 2 * t + 1:2 * t + 2]], axis=0)
                s = lax.dot_general(q2, k2, NT_DIMS, preferred_element_type=F32) + b2
                pr = jnp.exp(s - lse2)
                ds = (pr * (lax.dot_general(do2, v2, NT_DIMS, preferred_element_type=F32) - dl2)).astype(BF16)
                dq = _unstack_heads(jnp.dot(ds, k2, preferred_element_type=F32), low)
                dk = lax.dot_general(ds, q2, TN_DIMS, preferred_element_type=F32)
                dv = lax.dot_general(pr.astype(BF16), do2, TN_DIMS, preferred_element_type=F32)
                aq[t, rows, :] = aq[t, rows, :] + dq * scale
                ak[t, rows, :] = ak[t, rows, :] + dk[0:ATT_BLOCK]
                av[t, rows, :] = av[t, rows, :] + dv[0:ATT_BLOCK]
                if has_prev:
                    ak[t, prows, :] = ak[t, prows, :] + dk[ATT_BLOCK:]
                    av[t, prows, :] = av[t, prows, :] + dv[ATT_BLOCK:]

        for p in range(N_PATTERNS):
            d = 4 ** p
            _att_fill_bias(bias, g, d)
            _att_one_pattern(block, p, d, seq_blocks // d)

        for t in range(ATT_PAIRS):
            ts = slice(t * ATT_BLOCK, (t + 1) * ATT_BLOCK)
            dq_ref[0, :, ts] = aq[t].astype(BF16)
            dk_ref[0, :, ts] = ak[t].astype(BF16)
            dv_ref[0, :, ts] = av[t].astype(BF16)

    shp = _sds((B, S, ATT_WIDTH), BF16)
    pair_buf = pltpu.VMEM((ATT_PAIRS, S, ATT_BLOCK), F32)
    scratch = ([pair_buf] * 4 + [pltpu.VMEM((S, ATT_BLOCK), F32)] + [pair_buf] * 3
               + [pltpu.VMEM((ATT_PAIRS, 2 * ATT_BLOCK, 2 * ATT_BLOCK), F32)])
    rs = riders
    res = _pcall(_with_riders(compute, rs, 6, 3, len(scratch), (B - 1, ATT_GROUPS - 1)), name="attention_bwd",
                 out_shape=(shp, shp, shp) + (tuple(rs.out_shape) if rs else ()), grid=(B, ATT_GROUPS),
                 in_specs=[col(0), col(1), col(2), o_spec, o_spec, l_spec] + (rs.specs if rs else []),
                 out_specs=(o_spec, o_spec, o_spec) + (tuple(rs.specs) if rs else ()),
                 scratch_shapes=scratch + (rs.scratch if rs else []),
                 dims=("arbitrary", "arbitrary"))(proj3, proj3, proj3, do3, o3, lse4, *(rs.arrs if rs else []))
    return res[0], res[1], res[2], list(res[3:])


def _expand_groups(m):
    rows = SSM_WIDTH
    t = jnp.concatenate([m] * SSM_GROUPS, axis=0)
    r = lax.broadcasted_iota(jnp.int32, (rows, SSM_LANES), 0)
    l = lax.broadcasted_iota(jnp.int32, (rows, SSM_LANES), 1)
    keep = lax.shift_right_logical(r, 4) == lax.shift_right_logical(l, 6)
    return jnp.where(keep, t, 0.0)


def _collapse_groups(m):
    rows = SSM_WIDTH
    r = lax.broadcasted_iota(jnp.int32, (rows, SSM_LANES), 0)
    l = lax.broadcasted_iota(jnp.int32, (rows, SSM_LANES), 1)
    keep = lax.shift_right_logical(r, 4) == lax.shift_right_logical(l, 6)
    t = jnp.where(keep, m, 0.0)
    acc = t[0:SSM_GROUP_CH]
    for g in range(1, SSM_GROUPS):
        acc = acc + t[g * SSM_GROUP_CH:(g + 1) * SSM_GROUP_CH]
    return acc


def _zoh(lr, li, ldt):
    dt = jnp.exp(ldt)
    mag = jnp.exp(lr * dt)
    ang = li * dt
    cs, sn = jnp.cos(ang), jnp.sin(ang)
    ab_re, ab_im = mag * cs, mag * sn
    nr, ni = ab_re - 1.0, ab_im
    den = lr * lr + li * li
    n_re = nr * lr + ni * li
    n_im = ni * lr - nr * li
    return dict(dt=dt, mag=mag, cs=cs, sn=sn, ab_re=ab_re, ab_im=ab_im, nr=nr, ni=ni, den=den, n_re=n_re, n_im=n_im,
                f_re=n_re / den, f_im=n_im / den)


def _ssm_params(lr, li, ldt, br, bi, cr, ci):
    def body(lr_ref, li_ref, ldt_ref, br_ref, bi_ref, cr_ref, ci_ref, ab_ref, w_ref, c_ref):
        z = _zoh(lr_ref[...], li_ref[...], ldt_ref[...])
        ab_ref[0:1, :] = z["ab_re"]
        ab_ref[1:2, :] = z["ab_im"]
        br, bi = br_ref[...], bi_ref[...]
        w_ref[:, 0:SSM_LANES] = _expand_groups(z["f_re"] * br - z["f_im"] * bi).astype(BF16)
        w_ref[:, SSM_LANES:] = _expand_groups(z["f_re"] * bi + z["f_im"] * br).astype(BF16)
        c_ref[:, 0:SSM_LANES] = _expand_groups(cr_ref[...]).astype(BF16)
        c_ref[:, SSM_LANES:] = _expand_groups(-ci_ref[...]).astype(BF16)

    return _pcall(body, name="ssm_params",
                  out_shape=(_sds((2, SSM_LANES), F32), _sds((SSM_WIDTH, 2 * SSM_LANES), BF16),
                             _sds((SSM_WIDTH, 2 * SSM_LANES), BF16)))(lr, li, ldt, br, bi, cr, ci)


def _ssm_params_bwd(lr, li, ldt, br, bi, dab, dw, dc):
    def body(lr_ref, li_ref, ldt_ref, br_ref, bi_ref, dab_ref, dw_ref, dc_ref,
             dlr_ref, dli_ref, dldt_ref, dbr_ref, dbi_ref, dcr_ref, dci_ref):
        lr, li = lr_ref[...], li_ref[...]
        z = _zoh(lr, li, ldt_ref[...])
        br, bi = br_ref[...], bi_ref[...]
        dbb_re = _collapse_groups(dw_ref[:, 0:SSM_LANES])
        dbb_im = _collapse_groups(dw_ref[:, SSM_LANES:])
        dcr_ref[...] = _collapse_groups(dc_ref[:, 0:SSM_LANES])
        dci_ref[...] = -_collapse_groups(dc_ref[:, SSM_LANES:])
        f_re, f_im = z["f_re"], z["f_im"]
        dbr_ref[...] = f_re * dbb_re + f_im * dbb_im
        dbi_ref[...] = f_re * dbb_im - f_im * dbb_re
        df_re = jnp.sum(dbb_re * br + dbb_im * bi, axis=0, keepdims=True)
        df_im = jnp.sum(dbb_im * br - dbb_re * bi, axis=0, keepdims=True)
        den = z["den"]
        dn_re, dn_im = df_re / den, df_im / den
        dden = -(df_re * z["n_re"] + df_im * z["n_im"]) / (den * den)
        dnr = dn_re * lr - dn_im * li
        dni = dn_re * li + dn_im * lr
        dlr = dn_re * z["nr"] + dn_im * z["ni"] + 2.0 * dden * lr
        dli = dn_re * z["ni"] - dn_im * z["nr"] + 2.0 * dden * li
        dab_re = dab_ref[0:1, :] + dnr
        dab_im = dab_ref[1:2, :] + dni
        mag, cs, sn, dt = z["mag"], z["cs"], z["sn"], z["dt"]
        dmag = dab_re * cs + dab_im * sn
        dang = mag * (dab_im * cs - dab_re * sn)
        dlr_ref[...] = dlr + dmag * mag * dt
        dli_ref[...] = dli + dang * dt
        ddt = dmag * mag * lr + dang * li
        per_lane = jnp.broadcast_to(ddt * dt, (8, SSM_LANES))
        lane = lax.broadcasted_iota(jnp.int32, (SSM_LANES, 128), 0)
        col = lax.broadcasted_iota(jnp.int32, (SSM_LANES, 128), 1)
        ind = jnp.where(lax.shift_right_logical(lane, 6) == col, 1.0, 0.0)
        dldt_ref[...] = jnp.dot(per_lane, ind, preferred_element_type=F32, precision=lax.Precision.HIGHEST)[0:1]

    vec = _sds((1, SSM_LANES), F32)
    mat = _sds((SSM_GROUP_CH, SSM_LANES), F32)
    return _pcall(body, name="ssm_params_bwd", out_shape=(vec, vec, _sds((1, 128), F32), mat, mat, mat, mat))(
        lr, li, ldt, br, bi, dab, dw, dc)


SCAN_CHUNK = 512


def _scan_consts(ar, ai, k_ref, reverse):
    row = lax.broadcasted_iota(jnp.int32, (8, SSM_LANES), 0)
    pw = [(ar, ai)]
    for _ in range(7):
        pr, pi = pw[-1]
        pw.append((pr * ar - pi * ai, pr * ai + pi * ar))
    for n, k in enumerate((1, 2, 4)):
        keep = (row < 8 - k) if reverse else (row >= k)
        k_ref[2 * n] = jnp.where(keep, jnp.broadcast_to(pw[k - 1][0], (8, SSM_LANES)), 0.0)
        k_ref[2 * n + 1] = jnp.where(keep, jnp.broadcast_to(pw[k - 1][1], (8, SSM_LANES)), 0.0)
    cr = jnp.zeros((8, SSM_LANES), F32)
    ci = jnp.zeros((8, SSM_LANES), F32)
    for r in range(8):
        e = (8 - r) if reverse else (r + 1)
        cr = jnp.where(row == r, jnp.broadcast_to(pw[e - 1][0], (8, SSM_LANES)), cr)
        ci = jnp.where(row == r, jnp.broadcast_to(pw[e - 1][1], (8, SSM_LANES)), ci)
    k_ref[6] = cr
    k_ref[7] = ci


def _scan_tile(xr, xi, k_ref, car, cai, reverse):
    for n, k in enumerate((1, 2, 4)):
        sh = (8 - k) if reverse else k
        sr = pltpu.roll(xr, sh, 0)
        si = pltpu.roll(xi, sh, 0)
        mr, mi = k_ref[2 * n], k_ref[2 * n + 1]
        xr, xi = xr + mr * sr - mi * si, xi + mr * si + mi * sr
    pr, pi = k_ref[6], k_ref[7]
    xr, xi = xr + pr * car - pi * cai, xi + pr * cai + pi * car
    return xr, xi


def _scan_fwd(bu3, abar):
    B, S, _ = bu3.shape
    ch = min(S, SCAN_CHUNK)
    blk = pl.BlockSpec((1, ch, 2 * SSM_LANES), lambda b, c: (b, c, 0))

    def body(ab_ref, bu_ref, x_ref, k_ref, carry_ref):
        _scan_consts(ab_ref[0:1, :], ab_ref[1:2, :], k_ref, False)

        @pl.when(pl.program_id(1) == 0)
        def _():
            carry_ref[...] = jnp.zeros_like(carry_ref)

        def step(i, carry):
            base = pl.multiple_of(i * 8, 8)
            xr = bu_ref[0, pl.ds(base, 8), 0:SSM_LANES]
            xi = bu_ref[0, pl.ds(base, 8), SSM_LANES:]
            xr, xi = _scan_tile(xr, xi, k_ref, carry[0], carry[1], False)
            x_ref[0, pl.ds(base, 8), 0:SSM_LANES] = xr
            x_ref[0, pl.ds(base, 8), SSM_LANES:] = xi
            return (jnp.broadcast_to(xr[7:8], (8, SSM_LANES)), jnp.broadcast_to(xi[7:8], (8, SSM_LANES)))

        cr, ci = lax.fori_loop(0, ch // 8, step, (carry_ref[0], carry_ref[1]))
        carry_ref[0] = cr
        carry_ref[1] = ci

    return _pcall(body, name="scan_fwd", out_shape=_sds(bu3.shape, F32), grid=(B, S // ch),
                  in_specs=[pl.BlockSpec((2, SSM_LANES), lambda b, c: (0, 0)), blk], out_specs=blk,
                  scratch_shapes=[pltpu.VMEM((8, 8, SSM_LANES), F32), pltpu.VMEM((2, 8, SSM_LANES), F32)],
                  dims=("arbitrary", "arbitrary"))(abar, bu3)


def _scan_bwd(dx3, xs3, abar):
    B, S, _ = dx3.shape
    ch = min(S, SCAN_CHUNK)
    nc = S // ch
    blk = pl.BlockSpec((1, ch, 2 * SSM_LANES), lambda b, c: (b, nc - 1 - c, 0))

    def body(ab_ref, dx_ref, xs_ref, g_ref, da_ref, k_ref, carry_ref, acc_ref):
        b, c = pl.program_id(0), pl.program_id(1)
        _scan_consts(ab_ref[0:1, :], -ab_ref[1:2, :], k_ref, True)
        row = lax.broadcasted_iota(jnp.int32, (8, SSM_LANES), 0)

        @pl.when(c == 0)
        def _():
            carry_ref[...] = jnp.zeros_like(carry_ref)

        @pl.when((c == 0) & (b == 0))
        def _():
            acc_ref[...] = jnp.zeros_like(acc_ref)

        def step(i, carry):
            car, cai, ar_acc, ai_acc = carry
            base = pl.multiple_of((ch // 8 - 1 - i) * 8, 8)
            gr = dx_ref[0, pl.ds(base, 8), 0:SSM_LANES]
            gi = dx_ref[0, pl.ds(base, 8), SSM_LANES:]
            gr, gi = _scan_tile(gr, gi, k_ref, car, cai, True)
            g_ref[0, pl.ds(base, 8), 0:SSM_LANES] = gr
            g_ref[0, pl.ds(base, 8), SSM_LANES:] = gi
            nr = jnp.where(row == 7, car, pltpu.roll(gr, 7, 0))
            ni = jnp.where(row == 7, cai, pltpu.roll(gi, 7, 0))
            xr = xs_ref[0, pl.ds(base, 8), 0:SSM_LANES]
            xi = xs_ref[0, pl.ds(base, 8), SSM_LANES:]
            ar_acc = ar_acc + nr * xr + ni * xi
            ai_acc = ai_acc + ni * xr - nr * xi
            return (jnp.broadcast_to(gr[0:1], (8, SSM_LANES)), jnp.broadcast_to(gi[0:1], (8, SSM_LANES)), ar_acc, ai_acc)

        cr, ci, ar_acc, ai_acc = lax.fori_loop(0, ch // 8, step, (carry_ref[0], carry_ref[1], acc_ref[0], acc_ref[1]))
        carry_ref[0] = cr
        carry_ref[1] = ci
        acc_ref[0] = ar_acc
        acc_ref[1] = ai_acc
        da_ref[0:1, :] = jnp.sum(ar_acc, axis=0, keepdims=True)
        da_ref[1:2, :] = jnp.sum(ai_acc, axis=0, keepdims=True)

    return _pcall(body, name="scan_bwd", out_shape=(_sds(dx3.shape, F32), _sds((2, SSM_LANES), F32)), grid=(B, nc),
                  in_specs=[pl.BlockSpec((2, SSM_LANES), lambda b, c: (0, 0)), blk, blk],
                  out_specs=(blk, pl.BlockSpec((2, SSM_LANES), lambda b, c: (0, 0))),
                  scratch_shapes=[pltpu.VMEM((8, 8, SSM_LANES), F32), pltpu.VMEM((2, 8, SSM_LANES), F32),
                                  pltpu.VMEM((2, 8, SSM_LANES), F32)],
                  dims=("arbitrary", "arbitrary"))(abar, dx3, xs3)


US_BLOCK = (3 * ATT_WIDTH) // SSM_WIDTH


def _ssm_scan_fwd(proj3, abar, w_bu, w_c):
    B, S, _ = proj3.shape
    ch = min(S, SCAN_CHUNK)
    u_spec = pl.BlockSpec((1, ch, SSM_WIDTH), lambda b, c: (b, c, US_BLOCK))
    x_spec = pl.BlockSpec((1, ch, 2 * SSM_LANES), lambda b, c: (b, c, 0))
    y_spec = pl.BlockSpec((1, ch, SSM_WIDTH), lambda b, c: (b, c, 0))
    w_spec = pl.BlockSpec((SSM_WIDTH, 2 * SSM_LANES), lambda b, c: (0, 0))

    def body(ab_ref, u_ref, wb_ref, wc_ref, x_ref, y_ref, k_ref, carry_ref):
        _scan_consts(ab_ref[0:1, :], ab_ref[1:2, :], k_ref, False)

        @pl.when(pl.program_id(1) == 0)
        def _():
            carry_ref[...] = jnp.zeros_like(carry_ref)

        x_ref[0] = jnp.dot(u_ref[0], wb_ref[...], preferred_element_type=F32)

        def step(i, carry):
            base = pl.multiple_of(i * 8, 8)
            xr = x_ref[0, pl.ds(base, 8), 0:SSM_LANES]
            xi = x_ref[0, pl.ds(base, 8), SSM_LANES:]
            xr, xi = _scan_tile(xr, xi, k_ref, carry[0], carry[1], False)
            x_ref[0, pl.ds(base, 8), 0:SSM_LANES] = xr
            x_ref[0, pl.ds(base, 8), SSM_LANES:] = xi
            return (jnp.broadcast_to(xr[7:8], (8, SSM_LANES)), jnp.broadcast_to(xi[7:8], (8, SSM_LANES)))

        cr, ci = lax.fori_loop(0, ch // 8, step, (carry_ref[0], carry_ref[1]))
        carry_ref[0] = cr
        carry_ref[1] = ci
        y_ref[0] = lax.dot_general(x_ref[0].astype(BF16), wc_ref[...], NT_DIMS, preferred_element_type=F32)

    return _pcall(body, name="ssm_scan_fwd",
                  out_shape=(_sds((B, S, 2 * SSM_LANES), F32), _sds((B, S, SSM_WIDTH), F32)), grid=(B, S // ch),
                  in_specs=[pl.BlockSpec((2, SSM_LANES), lambda b, c: (0, 0)), u_spec, w_spec, w_spec],
                  out_specs=(x_spec, y_spec),
                  scratch_shapes=[pltpu.VMEM((8, 8, SSM_LANES), F32), pltpu.VMEM((2, 8, SSM_LANES), F32)],
                  dims=("arbitrary", "arbitrary"))(abar, proj3, w_bu, w_c)


def _ssm_scan_bwd(proj3, dy3, xs3, abar, w_bu, w_c, dsk):
    B, S, _ = proj3.shape
    ch = min(S, SCAN_CHUNK)
    nc = S // ch
    u_spec = pl.BlockSpec((1, ch, SSM_WIDTH), lambda b, c: (b, nc - 1 - c, US_BLOCK))
    x_spec = pl.BlockSpec((1, ch, 2 * SSM_LANES), lambda b, c: (b, nc - 1 - c, 0))
    y_spec = pl.BlockSpec((1, ch, SSM_WIDTH), lambda b, c: (b, nc - 1 - c, 0))
    w_spec = pl.BlockSpec((SSM_WIDTH, 2 * SSM_LANES), lambda b, c: (0, 0))
    ab_spec = pl.BlockSpec((2, SSM_LANES), lambda b, c: (0, 0))
    d_spec = pl.BlockSpec((1, SSM_WIDTH), lambda b, c: (0, 0))

    def body(ab_ref, u_ref, dy_ref, xs_ref, wb_ref, wc_ref, d_ref, du_ref, da_ref, dwb_ref, dwc_ref,
             g_ref, k_ref, carry_ref, acc_ref):
        b, c = pl.program_id(0), pl.program_id(1)
        _scan_consts(ab_ref[0:1, :], -ab_ref[1:2, :], k_ref, True)
        row = lax.broadcasted_iota(jnp.int32, (8, SSM_LANES), 0)

        @pl.when(c == 0)
        def _():
            carry_ref[...] = jnp.zeros_like(carry_ref)

        @pl.when((c == 0) & (b == 0))
        def _():
            acc_ref[...] = jnp.zeros_like(acc_ref)
            dwb_ref[...] = jnp.zeros_like(dwb_ref)
            dwc_ref[...] = jnp.zeros_like(dwc_ref)

        dy = dy_ref[0]
        dyb = dy.astype(BF16)
        g_ref[...] = jnp.dot(dyb, wc_ref[...], preferred_element_type=F32)

        def step(i, carry):
            car, cai, ar_acc, ai_acc = carry
            base = pl.multiple_of((ch // 8 - 1 - i) * 8, 8)
            gr = g_ref[pl.ds(base, 8), 0:SSM_LANES]
            gi = g_ref[pl.ds(base, 8), SSM_LANES:]
            gr, gi = _scan_tile(gr, gi, k_ref, car, cai, True)
            g_ref[pl.ds(base, 8), 0:SSM_LANES] = gr
            g_ref[pl.ds(base, 8), SSM_LANES:] = gi
            nr = jnp.where(row == 7, car, pltpu.roll(gr, 7, 0))
            ni = jnp.where(row == 7, cai, pltpu.roll(gi, 7, 0))
            xr = xs_ref[0, pl.ds(base, 8), 0:SSM_LANES]
            xi = xs_ref[0, pl.ds(base, 8), SSM_LANES:]
            ar_acc = ar_acc + nr * xr + ni * xi
            ai_acc = ai_acc + ni * xr - nr * xi
            return (jnp.broadcast_to(gr[0:1], (8, SSM_LANES)), jnp.broadcast_to(gi[0:1], (8, SSM_LANES)), ar_acc, ai_acc)

        cr, ci, ar_acc, ai_acc = lax.fori_loop(0, ch // 8, step, (carry_ref[0], carry_ref[1], acc_ref[0], acc_ref[1]))
        carry_ref[0] = cr
        carry_ref[1] = ci
        acc_ref[0] = ar_acc
        acc_ref[1] = ai_acc
        da_ref[0:1, :] = jnp.sum(ar_acc, axis=0, keepdims=True)
        da_ref[1:2, :] = jnp.sum(ai_acc, axis=0, keepdims=True)

        gb = g_ref[...].astype(BF16)
        du = lax.dot_general(gb, wb_ref[...], NT_DIMS, preferred_element_type=F32) + d_ref[...] * dy
        du_ref[0] = du.astype(BF16)
        dwb_ref[...] += lax.dot_general(u_ref[0], gb, TN_DIMS, preferred_element_type=F32)
        dwc_ref[...] += lax.dot_general(dyb, xs_ref[0].astype(BF16), TN_DIMS, preferred_element_type=F32)

    mat = _sds((SSM_WIDTH, 2 * SSM_LANES), F32)
    return _pcall(body, name="ssm_scan_bwd",
                  out_shape=(_sds((B, S, SSM_WIDTH), BF16), _sds((2, SSM_LANES), F32), mat, mat), grid=(B, nc),
                  in_specs=[ab_spec, u_spec, y_spec, x_spec, w_spec, w_spec, d_spec],
                  out_specs=(y_spec, ab_spec, w_spec, w_spec),
                  scratch_shapes=[pltpu.VMEM((ch, 2 * SSM_LANES), F32), pltpu.VMEM((8, 8, SSM_LANES), F32),
                                  pltpu.VMEM((2, 8, SSM_LANES), F32), pltpu.VMEM((2, 8, SSM_LANES), F32)],
                  dims=("arbitrary", "arbitrary"))(abar, proj3, dy3, xs3, w_bu, w_c, dsk)


GELU_K = math.sqrt(2.0 / math.pi)
GELU_C = 0.044715


def _gelu_parts(y):
    t = jnp.tanh(GELU_K * (y + GELU_C * y * y * y))
    return 0.5 * y * (1.0 + t), t


def _ssm_post(yc, us, dsk, wglu, bglu):
    T, N = yc.shape
    tm = min(T, 1024)
    row = pl.BlockSpec((tm, N), lambda i: (i, 0))
    vec = pl.BlockSpec((1, N), lambda i: (0, 0))
    mat = pl.BlockSpec((N, N), lambda i: (0, 0))

    def body(yc_ref, us_ref, d_ref, w_ref, b_ref, y_ref, s_ref):
        y = yc_ref[...] + d_ref[...] * us_ref[...]
        y_ref[...] = y
        z, _ = _gelu_parts(y)
        gl = jnp.dot(z.astype(BF16), w_ref[...], preferred_element_type=F32) + b_ref[...]
        s_ref[...] = (z * _sig(gl)).astype(BF16)

    return _pcall(body, name="ssm_post", out_shape=(_sds((T, N), F32), _sds((T, N), BF16)), grid=(T // tm,),
                  in_specs=[row, row, vec, mat, vec], out_specs=(row, row), dims=("parallel",))(yc, us, dsk, wglu, bglu)


def _ssm_post_bwd(y5, us, ds, dsk, wglu, bglu):
    T, N = y5.shape
    tm = min(T, 1024)
    row = pl.BlockSpec((tm, N), lambda i: (i, 0))
    vec = pl.BlockSpec((1, N), lambda i: (0, 0))
    mat = pl.BlockSpec((N, N), lambda i: (0, 0))

    def body(y_ref, us_ref, ds_ref, d_ref, w_ref, b_ref, dy_ref, dd_ref, db_ref, dw_ref):
        @pl.when(pl.program_id(0) == 0)
        def _():
            dd_ref[...] = jnp.zeros_like(dd_ref)
            db_ref[...] = jnp.zeros_like(db_ref)
            dw_ref[...] = jnp.zeros_like(dw_ref)

        y = y_ref[...]
        z, t = _gelu_parts(y)
        zb = z.astype(BF16)
        gl = jnp.dot(zb, w_ref[...], preferred_element_type=F32) + b_ref[...]
        sg = _sig(gl)
        ds = ds_ref[...]
        dgl = ds * z * sg * (1.0 - sg)
        dglb = dgl.astype(BF16)
        dz = ds * sg + lax.dot_general(dglb, w_ref[...], (((1,), (1,)), ((), ())), preferred_element_type=F32)
        dgelu = 0.5 * (1.0 + t) + 0.5 * y * (1.0 - t * t) * GELU_K * (1.0 + 3.0 * GELU_C * y * y)
        dy = dz * dgelu
        dy_ref[...] = dy
        dd_ref[...] += jnp.sum(dy * us_ref[...], axis=0, keepdims=True)
        db_ref[...] += jnp.sum(dgl, axis=0, keepdims=True)
        dw_ref[...] += lax.dot_general(zb, dglb, (((0,), (0,)), ((), ())), preferred_element_type=F32)

    return _pcall(body, name="ssm_post_bwd",
                  out_shape=(_sds((T, N), F32), _sds((1, N), F32), _sds((1, N), F32), _sds((N, N), F32)),
                  grid=(T // tm,), in_specs=[row, row, row, vec, mat, vec], out_specs=(row, vec, vec, mat),
                  dims=("arbitrary",))(y5, us, ds, dsk, wglu, bglu)


def _add_scaled_cast(a, b, s):
    T, N = a.shape
    tm = min(T, 1024)
    row = pl.BlockSpec((tm, N), lambda i: (i, 0))

    def body(a_ref, b_ref, s_ref, o_ref):
        o_ref[...] = (a_ref[...] + s_ref[...] * b_ref[...]).astype(BF16)

    return _pcall(body, name="add_scaled_cast", out_shape=_sds((T, N), BF16), grid=(T // tm,),
                  in_specs=[row, row, pl.BlockSpec((1, N), lambda i: (0, 0))], out_specs=row, dims=("parallel",))(a, b, s)


GATE_TILE = 256
GATE_ATT_BLOCK0 = (3 * ATT_WIDTH + SSM_WIDTH) // GATE_TILE
GATE_SSM_BLOCK0 = (3 * ATT_WIDTH + SSM_WIDTH + D_MODEL) // GATE_TILE


def _merge(proj, y_att, y_ssm, b_gate):
    T = proj.shape[0]
    tm = min(T, 1024)
    nj = D_MODEL // GATE_TILE
    ga = pl.BlockSpec((tm, GATE_TILE), lambda i, j: (i, GATE_ATT_BLOCK0 + j))
    gs = pl.BlockSpec((tm, GATE_TILE), lambda i, j: (i, GATE_SSM_BLOCK0 + j))
    yy = pl.BlockSpec((tm, GATE_TILE), lambda i, j: (i, j))
    ba = pl.BlockSpec((1, GATE_TILE), lambda i, j: (0, j))
    bs = pl.BlockSpec((1, GATE_TILE), lambda i, j: (0, nj + j))

    def body(ga_ref, gs_ref, ya_ref, ys_ref, ba_ref, bs_ref, o_ref):
        o_ref[...] = (_sig(ga_ref[...] + ba_ref[...]) * ya_ref[...]
                      + _sig(gs_ref[...] + bs_ref[...]) * ys_ref[...]).astype(BF16)

    return _pcall(body, name="merge", out_shape=_sds((T, D_MODEL), BF16), grid=(T // tm, nj),
                  in_specs=[ga, gs, yy, yy, ba, bs], out_specs=yy, dims=("parallel", "parallel"))(
        proj, proj, y_att, y_ssm, b_gate, b_gate)


def _merge_bwd(proj, y_att, y_ssm, b_gate, dmerged):
    T = proj.shape[0]
    tm = min(T, 1024)
    nj = D_MODEL // GATE_TILE
    ga = pl.BlockSpec((tm, GATE_TILE), lambda j, i: (i, GATE_ATT_BLOCK0 + j))
    gs = pl.BlockSpec((tm, GATE_TILE), lambda j, i: (i, GATE_SSM_BLOCK0 + j))
    yy = pl.BlockSpec((tm, GATE_TILE), lambda j, i: (i, j))
    ba = pl.BlockSpec((1, GATE_TILE), lambda j, i: (0, j))
    bs = pl.BlockSpec((1, GATE_TILE), lambda j, i: (0, nj + j))

    def body(ga_ref, gs_ref, ya_ref, ys_ref, ba_ref, bs_ref, dm_ref, dya_ref, dys_ref, dga_ref, dgs_ref, dba_ref, dbs_ref):
        @pl.when(pl.program_id(1) == 0)
        def _():
            dba_ref[...] = jnp.zeros_like(dba_ref)
            dbs_ref[...] = jnp.zeros_like(dbs_ref)

        dm = dm_ref[...].astype(F32)
        sa = _sig(ga_ref[...] + ba_ref[...])
        ss = _sig(gs_ref[...] + bs_ref[...])
        dya_ref[...] = (dm * sa).astype(BF16)
        dys_ref[...] = (dm * ss).astype(BF16)
        dga = dm * ya_ref[...] * sa * (1.0 - sa)
        dgs = dm * ys_ref[...] * ss * (1.0 - ss)
        dga_ref[...] = dga.astype(BF16)
        dgs_ref[...] = dgs.astype(BF16)
        dba_ref[...] += jnp.sum(dga, axis=0, keepdims=True)
        dbs_ref[...] += jnp.sum(dgs, axis=0, keepdims=True)

    big = _sds((T, D_MODEL), BF16)
    vec = _sds((1, D_MODEL), F32)
    return _pcall(body, name="merge_bwd", out_shape=(big, big, big, big, vec, vec), grid=(nj, T // tm),
                  in_specs=[ga, gs, yy, yy, ba, bs, yy], out_specs=(yy, yy, yy, yy, ba, ba),
                  dims=("arbitrary", "arbitrary"))(proj, proj, y_att, y_ssm, b_gate, b_gate, dmerged)


CONV_TILE = 256


def _conv_pre(a, w_ref, b_ref, row):
    conv = b_ref[...] + w_ref[0:1, :] * a
    shifted = []
    for j in (1, 2):
        sh = jnp.where(row >= j, pltpu.roll(a, j, 0), 0.0)
        shifted.append(sh)
        conv = conv + w_ref[j:j + 1, :] * sh
    return conv, shifted


def _conv_act(up3, w_conv, b_conv):
    B, S, _ = up3.shape
    nj = D_FF // CONV_TILE
    a_spec = pl.BlockSpec((1, S, CONV_TILE), lambda b, j: (b, 0, j))
    v_spec = pl.BlockSpec((1, S, CONV_TILE), lambda b, j: (b, 0, nj + j))
    w_spec = pl.BlockSpec((3, CONV_TILE), lambda b, j: (0, j))
    b_spec = pl.BlockSpec((1, CONV_TILE), lambda b, j: (0, j))

    def body(a_ref, v_ref, w_ref, b_ref, o_ref):
        a = a_ref[0].astype(F32)
        row = lax.broadcasted_iota(jnp.int32, a.shape, 0)
        conv, _ = _conv_pre(a, w_ref, b_ref, row)
        o_ref[0] = (conv * _sig(conv) * v_ref[0]).astype(BF16)

    return _pcall(body, name="conv_act", out_shape=_sds((B, S, D_FF), BF16), grid=(B, nj),
                  in_specs=[a_spec, v_spec, w_spec, b_spec], out_specs=a_spec, dims=("parallel", "parallel"))(
        up3, up3, w_conv, b_conv)


def _conv_bwd(up3, dact3, w_conv, b_conv):
    B, S, _ = up3.shape
    nj = D_FF // CONV_TILE
    a_spec = pl.BlockSpec((1, S, CONV_TILE), lambda j, b: (b, 0, j))
    v_spec = pl.BlockSpec((1, S, CONV_TILE), lambda j, b: (b, 0, nj + j))
    o_spec = pl.BlockSpec((2, 1, S, CONV_TILE), lambda j, b: (0, b, 0, j))
    w_spec = pl.BlockSpec((3, CONV_TILE), lambda j, b: (0, j))
    b_spec = pl.BlockSpec((1, CONV_TILE), lambda j, b: (0, j))

    def body(a_ref, v_ref, d_ref, w_ref, b_ref, dup_ref, dw_ref, db_ref):
        @pl.when(pl.program_id(1) == 0)
        def _():
            dw_ref[...] = jnp.zeros_like(dw_ref)
            db_ref[...] = jnp.zeros_like(db_ref)

        a = a_ref[0].astype(F32)
        d = d_ref[0].astype(F32)
        row = lax.broadcasted_iota(jnp.int32, a.shape, 0)
        conv, shifted = _conv_pre(a, w_ref, b_ref, row)
        sg = _sig(conv)
        dup_ref[1, 0] = (d * conv * sg).astype(BF16)
        dconv = d * v_ref[0] * (sg * (1.0 + conv * (1.0 - sg)))
        da = w_ref[0:1, :] * dconv
        for j in (1, 2):
            da = da + w_ref[j:j + 1, :] * jnp.where(row < S - j, pltpu.roll(dconv, S - j, 0), 0.0)
        dup_ref[0, 0] = da.astype(BF16)
        db_ref[...] += jnp.sum(dconv, axis=0, keepdims=True)
        dw_ref[0:1, :] += jnp.sum(dconv * a, axis=0, keepdims=True)
        dw_ref[1:2, :] += jnp.sum(dconv * shifted[0], axis=0, keepdims=True)
        dw_ref[2:3, :] += jnp.sum(dconv * shifted[1], axis=0, keepdims=True)

    return _pcall(body, name="conv_bwd",
                  out_shape=(_sds((2, B, S, D_FF), BF16), _sds((3, D_FF), F32), _sds((1, D_FF), F32)),
                  grid=(nj, B), in_specs=[a_spec, v_spec, a_spec, w_spec, b_spec],
                  out_specs=(o_spec, w_spec, b_spec), dims=("arbitrary", "arbitrary"))(up3, up3, dact3, w_conv, b_conv)


def _rows_tile(r, cap=640):
    for t in range(min(r, cap) - min(r, cap) % 8, 7, -8):
        if r % t == 0:
            return t
    return r


def _add2(a, b, out_dtype):
    R, N = a.shape
    tr = _rows_tile(R)
    spec = pl.BlockSpec((tr, N), lambda i: (i, 0))

    def body(a_ref, b_ref, o_ref):
        o_ref[...] = (a_ref[...] + b_ref[...]).astype(out_dtype)

    return _pcall(body, name="add2", out_shape=_sds((R, N), out_dtype), grid=(R // tr,), in_specs=[spec, spec],
                  out_specs=spec, dims=("parallel",))(a, b)


def _sum_slots(q, name):
    n, R, N = q.shape
    tr = _rows_tile(R)

    def body(q_ref, o_ref):
        acc = q_ref[0].astype(F32)
        for s in range(1, n):
            acc = acc + q_ref[s].astype(F32)
        o_ref[...] = acc

    return _pcall(body, name=name, out_shape=_sds((R, N), F32), grid=(R // tr,),
                  in_specs=[pl.BlockSpec((n, tr, N), lambda i: (0, i, 0))], out_specs=pl.BlockSpec((tr, N), lambda i: (i, 0)),
                  dims=("parallel",))(q)


NATIVE = (("b_re", 16, 1024), ("b_im", 16, 1024), ("c_re", 16, 1024), ("c_im", 16, 1024), ("g_mix", 1, 1024),
          ("b_att", 1, 1024), ("b_ssm", 1, 1024), ("a_re", 1, 1024), ("a_im", 1, 1024), ("log_dt", 1, 128),
          ("d_skip", 1, 256), ("b_glu", 1, 256), ("g_ffn", 1, 1024), ("g_final", 1, 1024), ("b_conv", 1, 2048),
          ("w_conv", 3, 2048))
N_MOD = 6


def _native_rows():
    starts, r = {}, 0
    for name, rows, cols in NATIVE:
        starts[name] = r
        r += rows * (-(-cols // LANES))
    n_sum = -(-r // 8) * 8
    return starts, n_sum


def _pack_small(native, dmods):
    starts, n_sum = _native_rows()
    B = dmods[0].shape[0]
    total = n_sum + 8 * N_MOD

    def body(*refs):
        xs, ms, o_ref = refs[:len(NATIVE)], refs[len(NATIVE):len(NATIVE) + N_MOD], refs[-1]
        o_ref[...] = jnp.zeros_like(o_ref)
        for (name, rows, cols), x_ref in zip(NATIVE, xs):
            chunks = -(-cols // LANES)
            if chunks == 1 and rows % 8 == 0:
                o_ref[starts[name]:starts[name] + rows, 0:cols] = x_ref[...]
                continue
            for i in range(rows):
                for q in range(chunks):
                    wd = min(LANES, cols - q * LANES)
                    r = starts[name] + i * chunks + q
                    o_ref[r:r + 1, 0:wd] = x_ref[i:i + 1, q * LANES:q * LANES + wd]
        for k, m_ref in enumerate(ms):
            for b in range(B):
                o_ref[n_sum + 8 * k + b:n_sum + 8 * k + b + 1, :] = m_ref[b]

    return _pcall(body, name="pack_small", out_shape=_sds((total, LANES), F32))(
        *[native[n] for n, _, _ in NATIVE], *dmods)


def _sum_unpack_small(gathered, B):
    starts, n_sum = _native_rows()
    nd = gathered.shape[0]

    def body(*refs):
        g_ref, outs, dm_ref, acc = refs[0], refs[1:1 + len(NATIVE)], refs[1 + len(NATIVE)], refs[-1]
        s = g_ref[0, 0:n_sum, :]
        for d in range(1, nd):
            s = s + g_ref[d, 0:n_sum, :]
        acc[...] = s
        for (name, rows, cols), o_ref in zip(NATIVE, outs):
            chunks = -(-cols // LANES)
            if chunks == 1 and rows % 8 == 0:
                o_ref[...] = acc[starts[name]:starts[name] + rows, 0:cols]
                continue
            for i in range(rows):
                for q in range(chunks):
                    wd = min(LANES, cols - q * LANES)
                    r = starts[name] + i * chunks + q
                    o_ref[i:i + 1, q * LANES:q * LANES + wd] = acc[r:r + 1, 0:wd]
        for d in range(nd):
            for k in range(N_MOD):
                dm_ref[d, :, k * D_MODEL:(k + 1) * D_MODEL] = g_ref[d, n_sum + 8 * k:n_sum + 8 * k + B, :]

    out_shape = tuple(_sds((rows, cols), F32) for _, rows, cols in NATIVE) + (_sds((nd, B, N_MOD * D_MODEL), F32),)
    res = _pcall(body, name="sum_unpack_small", out_shape=out_shape,
                 scratch_shapes=[pltpu.VMEM((n_sum, LANES), F32)])(gathered)
    return {n: r for (n, _, _), r in zip(NATIVE, res[:-1])}, res[-1]


def _small_from_native(nat):
    lanes3 = lambda a: a.reshape(SSM_GROUP_CH, SSM_GROUPS, SSM_STATE)
    return dict(
        g_mix=nat["g_mix"].reshape(D_MODEL), b_gate=jnp.concatenate([nat["b_att"], nat["b_ssm"]], axis=1).reshape(2 * D_MODEL),
        a_re=nat["a_re"].reshape(SSM_GROUPS, SSM_STATE), a_im=nat["a_im"].reshape(SSM_GROUPS, SSM_STATE),
        log_dt=nat["log_dt"][0, :SSM_GROUPS], b_re=_groups_from_lanes(nat["b_re"]), b_im=_groups_from_lanes(nat["b_im"]),
        c_re=lanes3(nat["c_re"]).transpose(1, 0, 2), c_im=lanes3(nat["c_im"]).transpose(1, 0, 2),
        d_skip=nat["d_skip"].reshape(SSM_WIDTH), b_glu=nat["b_glu"].reshape(SSM_WIDTH), g_ffn=nat["g_ffn"].reshape(D_MODEL),
        w_conv=nat["w_conv"], b_conv=nat["b_conv"].reshape(D_FF), g_final=nat["g_final"].reshape(D_MODEL))


def _adamw_multi(params):
    n = len(params)
    bc1 = 1.0 - ADAM_B1 ** ADAM_STEP
    bc2 = 1.0 - ADAM_B2 ** ADAM_STEP

    def body(*refs):
        ins, outs = refs[:4 * n], refs[4 * n:]
        for i in range(n):
            w_ref, g_ref, m_ref, v_ref = ins[4 * i:4 * i + 4]
            d_ref, nm_ref, nv_ref = outs[3 * i:3 * i + 3]
            g = g_ref[...]
            m = ADAM_B1 * m_ref[...] + (1.0 - ADAM_B1) * g
            v = ADAM_B2 * v_ref[...] + (1.0 - ADAM_B2) * (g * g)
            nm_ref[...] = m
            nv_ref[...] = v
            d_ref[...] = -ADAM_LR * ((m / bc1) / (jnp.sqrt(v / bc2) + ADAM_EPS) + ADAM_WD * w_ref[...])

    flat = [a for p in params for a in p]
    out_shape = tuple(_sds(p[0].shape, F32) for p in params for _ in range(3))
    res = _pcall(body, name="adamw_small", out_shape=out_shape)(*flat)
    return [tuple(res[3 * i:3 * i + 3]) for i in range(n)]


def _adamw(w, g, m, v, name):
    R, N = w.shape
    tr = _rows_tile(R) if R * N * 4 > (1 << 20) else R
    tr = min(tr, 256) if R % 256 == 0 and R > 256 else tr
    spec = pl.BlockSpec((tr, N), lambda i: (i, 0))
    bc1 = 1.0 - ADAM_B1 ** ADAM_STEP
    bc2 = 1.0 - ADAM_B2 ** ADAM_STEP

    def body(w_ref, g_ref, m_ref, v_ref, d_ref, nm_ref, nv_ref):
        g = g_ref[...]
        m = ADAM_B1 * m_ref[...] + (1.0 - ADAM_B1) * g
        v = ADAM_B2 * v_ref[...] + (1.0 - ADAM_B2) * (g * g)
        nm_ref[...] = m
        nv_ref[...] = v
        d_ref[...] = -ADAM_LR * ((m / bc1) / (jnp.sqrt(v / bc2) + ADAM_EPS) + ADAM_WD * w_ref[...])

    shp = _sds((R, N), F32)
    return _pcall(body, name=name, out_shape=(shp, shp, shp), grid=(R // tr,), in_specs=[spec] * 4,
                  out_specs=(spec, spec, spec), dims=("parallel",))(w, g, m, v)


_GROUP_MASKS = {
    "all": [(dx, dy, dc) for dx in (0, 1) for dy in (0, 1) for dc in (0, 1) if (dx, dy, dc) != (0, 0, 0)],
    "xy": [(1, 0, 0), (0, 1, 0), (1, 1, 0)],
    "c": [(0, 0, 1)],
}
_GROUP_SLOTS = {"all": 8, "xy": 4, "c": 2}


def _group_slot(group, x, y, c):
    return {"all": 4 * x + 2 * y + c, "xy": 2 * x + y, "c": c}[group]


def _flip(v, d):
    return 1 - v if d else v


def _exchange(arr, group, mode, name):
    return _exchange_list([arr], group, mode, name)[0]


def _exchange_list(arrs, group, mode, name):
    masks = _GROUP_MASKS[group]
    n = len(masks)
    na = len(arrs)
    out_shapes, halves, bounce = [], [], []
    for arr in arrs:
        if mode == "gather":
            out_shapes.append((_GROUP_SLOTS[group],) + arr.shape)
            bounce.append(pltpu.VMEM(arr.shape, arr.dtype))
        elif mode == "scatter":
            assert arr.shape[0] == _GROUP_SLOTS[group]
            out_shapes.append(arr.shape)
            bounce.append(pltpu.VMEM(arr.shape[1:], arr.dtype))
        elif mode == "swap":
            assert group == "c"
            out_shapes.append(arr.shape)
        else:
            assert group == "c"
            halves.append(arr.shape[1] // 2)
            out_shapes.append((arr.shape[0], arr.shape[1] // 2, arr.shape[2]))
    has_local = mode in ("gather", "scatter")

    def body(*refs):
        x_refs, o_refs = refs[:na], refs[na:2 * na]
        send_sems, recv_sems = refs[2 * na], refs[2 * na + 1]
        x, y, c = lax.axis_index("x"), lax.axis_index("y"), lax.axis_index("c")
        me = _group_slot(group, x, y, c)
        if has_local:
            local_sems = refs[2 * na + 2]
            bufs = refs[2 * na + 3:]
            loads = []
            for i in range(na):
                src = x_refs[i] if mode == "gather" else x_refs[i].at[me]
                loads.append(pltpu.make_async_copy(src, bufs[i], local_sems.at[2 * i]))
                loads[-1].start()
        copies = []
        for i in range(na):
            x_ref, o_ref = x_refs[i], o_refs[i]
            for k, (dx, dy, dc) in enumerate(masks):
                px, py, pc = _flip(x, dx), _flip(y, dy), _flip(c, dc)
                if mode == "gather":
                    src, dst = x_ref, o_ref.at[me]
                elif mode == "scatter":
                    src, dst = x_ref.at[_group_slot(group, px, py, pc)], o_ref.at[me]
                elif mode == "swap":
                    src, dst = x_ref, o_ref
                else:
                    src, dst = x_ref.at[:, pl.ds(pl.multiple_of(pc * halves[i], 8), halves[i]), :], o_ref
                cp = pltpu.make_async_remote_copy(src_ref=src, dst_ref=dst, send_sem=send_sems.at[i * n + k],
                                                  recv_sem=recv_sems.at[i * n + k], device_id=(px, py, pc),
                                                  device_id_type=pl.DeviceIdType.MESH)
                cp.start()
                copies.append(cp)
        if has_local:
            stores = []
            for i in range(na):
                loads[i].wait()
                stores.append(pltpu.make_async_copy(bufs[i], o_refs[i].at[me], local_sems.at[2 * i + 1]))
                stores[-1].start()
        for cp in copies:
            cp.wait()
        if has_local:
            for st in stores:
                st.wait()

    anyspec = pl.BlockSpec(memory_space=pl.ANY)
    scratch = [pltpu.SemaphoreType.DMA((n * na,)), pltpu.SemaphoreType.DMA((n * na,))]
    if has_local:
        scratch += [pltpu.SemaphoreType.DMA((2 * na,))] + bounce
    outs = pl.pallas_call(body, name=name, out_shape=tuple(_sds(s, a.dtype) for s, a in zip(out_shapes, arrs)),
                          in_specs=[anyspec] * na, out_specs=tuple([anyspec] * na), scratch_shapes=scratch,
                          compiler_params=pltpu.CompilerParams(vmem_limit_bytes=V7X_VMEM_LIMIT_BYTES))(*arrs)
    return list(outs)


def _gather_weights(shards, name):
    na = len(shards)
    masks = _GROUP_MASKS["xy"]
    n = len(masks)

    def body(*refs):
        x_refs, o_refs = refs[:na], refs[na:2 * na]
        send_sems, recv_sems, local_sems = refs[2 * na:2 * na + 3]
        bufs = refs[2 * na + 3:]
        x, y, c = lax.axis_index("x"), lax.axis_index("y"), lax.axis_index("c")
        me = 2 * x + y
        sibling = (x, y, 1 - c)
        loads = []
        for i in range(na):
            loads.append(pltpu.make_async_copy(x_refs[i], bufs[i], local_sems.at[2 * i]))
            loads[-1].start()

        def half_of(i, slot, cc):
            h = shards[i].shape[0] // 2
            return o_refs[i].at[slot, pl.ds(pl.multiple_of(cc * h, 8), h), :]

        def src_half(i, cc):
            h = shards[i].shape[0] // 2
            return x_refs[i].at[pl.ds(pl.multiple_of(cc * h, 8), h), :]

        sends = []
        for i in range(na):
            for k, (dx, dy, _) in enumerate(masks):
                cp = pltpu.make_async_remote_copy(src_ref=src_half(i, c), dst_ref=half_of(i, me, c),
                                                  send_sem=send_sems.at[i * 2 * n + k], recv_sem=recv_sems.at[i * 2 * n + k],
                                                  device_id=(_flip(x, dx), _flip(y, dy), c),
                                                  device_id_type=pl.DeviceIdType.MESH)
                cp.start()
                sends.append(cp)
        stores = []
        for i in range(na):
            loads[i].wait()
            stores.append(pltpu.make_async_copy(bufs[i], o_refs[i].at[me], local_sems.at[2 * i + 1]))
            stores[-1].start()
        for i in range(na):
            for k, (dx, dy, _) in enumerate(masks):
                slot = 2 * _flip(x, dx) + _flip(y, dy)
                landed = pltpu.make_async_remote_copy(src_ref=src_half(i, c), dst_ref=half_of(i, slot, c),
                                                      send_sem=send_sems.at[i * 2 * n + k],
                                                      recv_sem=recv_sems.at[i * 2 * n + k], device_id=sibling,
                                                      device_id_type=pl.DeviceIdType.MESH)
                landed.wait_recv()
                fwd = pltpu.make_async_remote_copy(src_ref=half_of(i, slot, c), dst_ref=half_of(i, slot, c),
                                                   send_sem=send_sems.at[i * 2 * n + n + k],
                                                   recv_sem=recv_sems.at[i * 2 * n + n + k], device_id=sibling,
                                                   device_id_type=pl.DeviceIdType.MESH)
                fwd.start()
                sends.append(fwd)
        for i in range(na):
            for k, (dx, dy, _) in enumerate(masks):
                slot = 2 * _flip(x, dx) + _flip(y, dy)
                pltpu.make_async_remote_copy(src_ref=half_of(i, slot, 1 - c), dst_ref=half_of(i, slot, 1 - c),
                                             send_sem=send_sems.at[i * 2 * n + n + k],
                                             recv_sem=recv_sems.at[i * 2 * n + n + k], device_id=sibling,
                                             device_id_type=pl.DeviceIdType.MESH).wait_recv()
        for cp in sends:
            cp.wait_send()
        for st in stores:
            st.wait()

    anyspec = pl.BlockSpec(memory_space=pl.ANY)
    scratch = [pltpu.SemaphoreType.DMA((2 * n * na,)), pltpu.SemaphoreType.DMA((2 * n * na,)),
               pltpu.SemaphoreType.DMA((2 * na,))] + [pltpu.VMEM(s.shape, s.dtype) for s in shards]
    outs = pl.pallas_call(body, name=name, out_shape=tuple(_sds((N_XY,) + s.shape, s.dtype) for s in shards),
                          in_specs=[anyspec] * na, out_specs=tuple([anyspec] * na), scratch_shapes=scratch,
                          compiler_params=pltpu.CompilerParams(vmem_limit_bytes=V7X_VMEM_LIMIT_BYTES))(*shards)
    return list(outs)


def _pair_add(g, theirs, core, name):
    n4, h2, w = g.shape
    h = h2 // 2
    tr = _rows_tile(h)
    nb = h // tr

    def body(c_ref, g_ref, t_ref, o_ref):
        o_ref[...] = (g_ref[...] + t_ref[...]).astype(BF16)

    grid_spec = pltpu.PrefetchScalarGridSpec(
        num_scalar_prefetch=1, grid=(n4, nb),
        in_specs=[pl.BlockSpec((None, tr, w), lambda j, i, c_ref: (j, c_ref[0] * nb + i, 0)),
                  pl.BlockSpec((None, tr, w), lambda j, i, c_ref: (j, i, 0))],
        out_specs=pl.BlockSpec((None, tr, w), lambda j, i, c_ref: (j, i, 0)))
    return pl.pallas_call(body, name=name, out_shape=_sds((n4, h, w), BF16), grid_spec=grid_spec,
                          compiler_params=pltpu.CompilerParams(vmem_limit_bytes=V7X_VMEM_LIMIT_BYTES,
                                                               dimension_semantics=("parallel", "parallel")))(core, g, theirs)


BIG = (("w_proj_att", (ATT_WIDTH, D_MODEL), 1), ("w_proj_ssm", (SSM_WIDTH, D_MODEL), 1),
       ("w_glu", (SSM_WIDTH, SSM_WIDTH), 0))
DIRECT = (("w_in", True), ("w_up", True), ("w_down", False), ("w_out", False))
N_XY = 4


def _big_rows(shape):
    return shape[0] * shape[1] // N_XY // LANES


FLAT_ROWS = sum(_big_rows(s) for _, s, _ in BIG)


def _shard_shape(shape, axis):
    return (shape[0] // N_XY, shape[1]) if axis == 0 else (shape[0], shape[1] // N_XY)


def _flatten_shards(shards):
    return jnp.concatenate([shards[n].reshape(_big_rows(s), LANES) for n, s, _ in BIG], axis=0)


def _unflatten_shard(flat):
    out, r = {}, 0
    for n, s, ax in BIG:
        k = _big_rows(s)
        out[n] = flat[r:r + k].reshape(_shard_shape(s, ax))
        r += k
    return out


def _unflatten_full(flat4):
    out, r = {}, 0
    for n, s, ax in BIG:
        k = _big_rows(s)
        sh = _shard_shape(s, ax)
        t = flat4[:, r:r + k].reshape((N_XY,) + sh)
        out[n] = t.reshape(s) if ax == 0 else t.transpose(1, 0, 2).reshape(s)
        r += k
    return out


def _flatten_full(full):
    parts = []
    for n, s, ax in BIG:
        sh = _shard_shape(s, ax)
        t = full[n]
        t = t.reshape((N_XY,) + sh) if ax == 0 else t.reshape(s[0], N_XY, sh[1]).transpose(1, 0, 2)
        parts.append(t.reshape(N_XY, _big_rows(s), LANES))
    return jnp.concatenate(parts, axis=1)


def _pack_rows(arrs):
    rows, counts = [], []
    for a in arrs:
        f = a.reshape(-1)
        k = -(-f.shape[0] // LANES)
        rows.append(jnp.pad(f, (0, k * LANES - f.shape[0])).reshape(k, LANES))
        counts.append(k)
    return jnp.concatenate(rows, axis=0), counts


def _unpack_rows(buf, shapes):
    out, r = [], 0
    for s in shapes:
        size = int(np.prod(s))
        k = -(-size // LANES)
        out.append(buf[r:r + k].reshape(-1)[:size].reshape(s))
        r += k
    return out


def _lanes_from_groups(a):
    return a.transpose(2, 0, 1).reshape(SSM_GROUP_CH, SSM_LANES)


def _groups_from_lanes(a):
    return a.reshape(SSM_GROUP_CH, SSM_GROUPS, SSM_STATE).transpose(1, 2, 0)


LATE = ("w_up_t", "w_down", "w_out")
EARLY_GRADS = ("w_up_t", "w_down", "w_out")


def _local_step(x3, mod, tgt3, W, P, late_shards=None, pair_fn=None):
    B, S, _ = x3.shape
    T = B * S
    seq_blocks = S // ATT_BLOCK
    sh1, sc1, gt1, sh2, sc2, gt2 = [m.reshape(B, 1, D_MODEL) for m in jnp.split(mod, 6, axis=-1)]
    g_mix, g_ffn, g_final = P["g_mix"].reshape(1, D_MODEL), P["g_ffn"].reshape(1, D_MODEL), P["g_final"].reshape(1, D_MODEL)
    b_gate = P["b_gate"].reshape(1, 2 * D_MODEL)
    d_skip, b_glu = P["d_skip"].reshape(1, SSM_WIDTH), P["b_glu"].reshape(1, SSM_WIDTH)
    w_conv, b_conv = P["w_conv"], P["b_conv"].reshape(1, D_FF)

    u1 = _norm_mod(x3, g_mix, sc1, sh1).reshape(T, D_MODEL)
    proj = _mm(u1, W["w_in_t"], tb=True, name="mm_proj", out_dtype=BF16)
    proj3 = proj.reshape(B, S, IN_WIDTH)
    us = proj[:, 3 * ATT_WIDTH:3 * ATT_WIDTH + SSM_WIDTH]
    o_att3, lse4, late = _attention_fwd(proj3, seq_blocks, _Riders(late_shards, "gather") if late_shards else None)
    if late_shards:
        W = dict(W, **{n: f.reshape(-1, LANES) for n, f in zip(LATE, late)})
        w_conv = late[len(LATE)].transpose(1, 0, 2).reshape(3, D_FF)
    o_att = o_att3.reshape(T, ATT_WIDTH)
    y_att = _mm(o_att, W["w_proj_att"], name="mm_proj_att", out_dtype=BF16)

    lr = P["a_re"].reshape(1, SSM_LANES)
    li = P["a_im"].reshape(1, SSM_LANES)
    ldt = jnp.repeat(P["log_dt"], SSM_STATE).reshape(1, SSM_LANES)
    br, bi = _lanes_from_groups(P["b_re"]), _lanes_from_groups(P["b_im"])
    cr = P["c_re"].transpose(1, 0, 2).reshape(SSM_GROUP_CH, SSM_LANES)
    ci = P["c_im"].transpose(1, 0, 2).reshape(SSM_GROUP_CH, SSM_LANES)
    abar, w_bu, w_c = _ssm_params(lr, li, ldt, br, bi, cr, ci)
    xs3, y_core3 = _ssm_scan_fwd(proj3, abar, w_bu, w_c)
    y5, s_out = _ssm_post(y_core3.reshape(T, SSM_WIDTH), us, d_skip, W["w_glu"], b_glu)
    y_ssm = _mm(s_out, W["w_proj_ssm"], name="mm_proj_ssm", out_dtype=BF16)

    merged = _merge(proj, y_att, y_ssm, b_gate)
    mix = _mm(merged, W["w_out"], name="mm_out", out_dtype=BF16)
    mix3 = mix.reshape(B, S, D_MODEL)

    h1, u2 = _resid_norm_mod(x3, mix3, gt1, g_ffn, sc2, sh2)
    u2 = u2.reshape(T, D_MODEL)
    up3 = _mm(u2, W["w_up_t"], tb=True, name="mm_up", out_dtype=BF16).reshape(B, S, 2 * D_FF)
    act = _conv_act(up3, w_conv, b_conv).reshape(T, D_FF)
    ffn3 = _mm(act, W["w_down"], name="mm_down", out_dtype=BF16).reshape(B, S, D_MODEL)
    dh2, dffn, dgt2, dg_final, loss = _final_loss(h1, ffn3, tgt3, gt2, g_final)

    dffn = dffn.reshape(T, D_MODEL)
    gw = {}
    gw["w_down"] = _mm(act, dffn, ta=True, name="mm_dw_down")
    dact3 = _mm(dffn, W["w_down"], tb=True, name="mm_dact", out_dtype=BF16).reshape(B, S, D_FF)
    dup3, dw_conv, db_conv = _conv_bwd(up3, dact3, w_conv, b_conv)
    dup = dup3.reshape(2, T, D_FF)
    gw["w_up_t"] = _mm(dup, u2, ta=True, name="mm_dw_up")
    du2 = _mm(dup, W["w_up_t"], name="mm_du2", out_dtype=BF16).reshape(B, S, D_MODEL)
    dh1, dsh2, dsc2, dg_ffn, dgt1, dmix = _norm_bwd(h1, du2, dh2, g_ffn, sc2, "norm_bwd2", mix3=mix3, gt=gt1)

    dmix = dmix.reshape(T, D_MODEL)
    gw["w_out"] = _mm(merged, dmix, ta=True, name="mm_dw_out")
    dmerged = _mm(dmix, W["w_out"], tb=True, name="mm_dmerged", out_dtype=BF16)
    dy_att, dy_ssm, dga, dgs, db_att, db_ssm = _merge_bwd(proj, y_att, y_ssm, b_gate, dmerged)

    gw["w_proj_ssm"] = _mm(s_out, dy_ssm, ta=True, name="mm_dw_proj_ssm")
    ds_out = _mm(dy_ssm, W["w_proj_ssm"], tb=True, name="mm_ds_out")
    dy5, dd_skip, db_glu, dw_glu = _ssm_post_bwd(y5, us, ds_out, d_skip, W["w_glu"], b_glu)
    gw["w_glu"] = dw_glu
    dus3, dab, dwbu, dwc = _ssm_scan_bwd(proj3, dy5.reshape(B, S, SSM_WIDTH), xs3, abar, w_bu, w_c, d_skip)
    dus = dus3.reshape(T, SSM_WIDTH)
    dlr, dli, dldt, dbr, dbi, dcr, dci = _ssm_params_bwd(lr, li, ldt, br, bi, dab, dwbu, dwc)

    gw["w_proj_att"] = _mm(o_att, dy_att, ta=True, name="mm_dw_proj_att")
    do_att = _mm(dy_att, W["w_proj_att"], tb=True, out_dtype=BF16, name="mm_do_att")
    early = [gw[n].reshape(N_XY, -1, LANES) for n in EARLY_GRADS] + [_flatten_full({n: gw[n] for n, _, _ in BIG})]
    pairs = pair_fn(early, "early") if pair_fn else None
    dq3, dk3, dv3, parts = _attention_bwd(proj3, do_att.reshape(B, S, ATT_WIDTH), o_att3, lse4, seq_blocks,
                                          _Riders(pairs, "scatter") if pairs else None)
    dproj = jnp.concatenate([t.reshape(T, ATT_WIDTH) for t in (dq3, dk3, dv3)] + [dus, dga, dgs], axis=1)
    gw["w_in_t"] = _mm(dproj, u1, ta=True, name="mm_dw_in")
    if pair_fn:
        pairs = pair_fn([gw["w_in_t"].reshape(N_XY, -1, LANES)], "last")
        du1, last_parts = _mm(dproj, W["w_in_t"], name="mm_du1", out_dtype=BF16, riders=_Riders(pairs, "scatter"))
        parts = parts + last_parts
    else:
        du1 = _mm(dproj, W["w_in_t"], name="mm_du1", out_dtype=BF16)
    du1 = du1.reshape(B, S, D_MODEL)
    dx, dsh1, dsc1, dg_mix = _norm_bwd(x3, du1, dh1, g_mix, sc1, "norm_bwd1")

    dmods = [dsh1, dsc1, dgt1, dsh2, dsc2, dgt2]
    native = dict(g_mix=dg_mix, b_att=db_att, b_ssm=db_ssm, a_re=dlr, a_im=dli, log_dt=dldt, b_re=dbr, b_im=dbi, c_re=dcr,
                  c_im=dci, d_skip=dd_skip, b_glu=db_glu, g_ffn=dg_ffn, w_conv=dw_conv, b_conv=db_conv, g_final=dg_final)
    return loss, dx, dmods, gw, native, parts


WEIGHTS = ['w_ada', 'b_ada', 'g_mix', 'w_in', 'b_gate', 'a_re', 'a_im', 'log_dt', 'b_re', 'b_im', 'c_re', 'c_im', 'd_skip',
           'w_glu', 'b_glu', 'w_proj_att', 'w_proj_ssm', 'w_out', 'g_ffn', 'w_up', 'w_conv', 'b_conv', 'w_down', 'g_final']
SMALL = ['g_mix', 'b_gate', 'a_re', 'a_im', 'log_dt', 'b_re', 'b_im', 'c_re', 'c_im', 'd_skip', 'b_glu', 'g_ffn', 'w_conv',
         'b_conv', 'g_final']


def kernel(x, c, w_ada, b_ada, g_mix, w_in, b_gate, a_re, a_im, log_dt, b_re, b_im, c_re, c_im, d_skip, w_glu, b_glu, w_proj_att, w_proj_ssm, w_out, g_ffn, w_up, w_conv, b_conv, w_down, g_final, loss_target, m_w_ada, m_b_ada, m_g_mix, m_w_in, m_b_gate, m_a_re, m_a_im, m_log_dt, m_b_re, m_b_im, m_c_re, m_c_im, m_d_skip, m_w_glu, m_b_glu, m_w_proj_att, m_w_proj_ssm, m_w_out, m_g_ffn, m_w_up, m_w_conv, m_b_conv, m_w_down, m_g_final, v_w_ada, v_b_ada, v_g_mix, v_w_in, v_b_gate, v_a_re, v_a_im, v_log_dt, v_b_re, v_b_im, v_c_re, v_c_im, v_d_skip, v_w_glu, v_b_glu, v_w_proj_att, v_w_proj_ssm, v_w_out, v_g_ffn, v_w_up, v_w_conv, v_b_conv, v_w_down, v_g_final):
    args = dict(locals())
    w = {n: args[n] for n in WEIGHTS}
    m = {n: args["m_" + n] for n in WEIGHTS}
    v = {n: args["v_" + n] for n in WEIGHTS}
    B, S, _ = x.shape
    ix, iy, ic = lax.axis_index("x"), lax.axis_index("y"), lax.axis_index("c")
    chip = 2 * ix + iy
    half = FLAT_ROWS // 2
    ada_cols = w_ada.shape[2]

    c_all = _exchange(c, "all", "gather", "gather_c").reshape(8 * B, D_MODEL)
    b_cols = lax.dynamic_slice_in_dim(b_ada, chip * ada_cols, ada_cols, axis=1)
    mod_cols = _ada_fwd(c_all, w_ada[0], b_cols)
    mod_all = _exchange(mod_cols, "xy", "gather", "gather_mod")
    mod_all = mod_all.transpose(1, 0, 2).reshape(8 * B, 6 * D_MODEL)
    mod = lax.dynamic_slice_in_dim(mod_all, (4 * ix + 2 * iy + ic) * B, B, axis=0)

    south = ic == 0
    core = ic.astype(jnp.int32).reshape(1)
    shard = {n + ("_t" if t else ""): (w[n][0].T if t else w[n][0]).astype(BF16) for n, t in DIRECT}
    misc = _flatten_shards({n: w[n][0] for n, _, _ in BIG}).astype(BF16)
    w_in_full, misc_full = _gather_weights([shard["w_in_t"], misc], "gather_weights")
    W = {"w_in_t": w_in_full.reshape(-1, LANES)}
    W.update(_unflatten_full(misc_full))

    def pair_fn(slots, tag):
        theirs = _exchange_list(slots, "c", "half", "reduce_cores_" + tag)
        return [_pair_add(g, t, core, "pair_add_%s_%d" % (tag, i)) for i, (g, t) in enumerate(zip(slots, theirs))]

    P = {n: w[n][0] for n in SMALL if n not in ("w_conv", "g_final")}
    P["w_conv"] = None
    P["g_final"] = g_final

    loss, dx, dmods, gw, native, parts = _local_step(x, mod, loss_target, W, P, [shard[n] for n in LATE] + [w_conv[0]],
                                                     pair_fn)

    loss = lax.psum(loss[0, 0], MESH_AXES)

    gathered = _exchange(_pack_small(native, dmods), "all", "gather", "gather_small")
    native_sum, dmod_all = _sum_unpack_small(gathered, B)
    g_small = _small_from_native(native_sum)
    dmod_all = dmod_all.reshape(8 * B, N_MOD * D_MODEL)
    dmod_cols = lax.dynamic_slice_in_dim(dmod_all, chip * ada_cols, ada_cols, axis=1)
    g_w_ada, g_b_ada = _ada_bwd(c_all, dmod_all, dmod_cols)

    red = [_sum_slots(p, "sum_chips_%d" % i) for i, p in enumerate(parts)]
    red_sib = _exchange_list(red, "c", "swap", "share_cores")
    reduced = [jnp.concatenate([jnp.where(south, r, s), jnp.where(south, s, r)], axis=0) for r, s in zip(red, red_sib)]

    grads = {"w_ada": g_w_ada[None], "b_ada": g_b_ada}
    order = list(EARLY_GRADS) + ["misc", "w_in_t"]
    for n, g in zip(order, reduced):
        if n == "misc":
            for k, gk in _unflatten_shard(g).items():
                grads[k] = gk[None]
        else:
            grads[n[:-2] if n.endswith("_t") else n] = (g.T if n.endswith("_t") else g)[None]
    wc_cols = w_conv.shape[2]
    for n in SMALL:
        g = g_small[n]
        if n == "w_conv":
            g = lax.dynamic_slice_in_dim(g, chip * wc_cols, wc_cols, axis=1)
        grads[n] = g.reshape(w[n].shape)

    delta, new_m, new_v = {}, {}, {}
    for n in ["w_ada"] + [b for b, _ in DIRECT] + [b for b, _, _ in BIG]:
        shp = w[n].shape
        d2, m2, v2 = _adamw(w[n][0], grads[n][0], m[n][0], v[n][0], "adamw_" + n)
        delta[n], new_m[n], new_v[n] = d2.reshape(shp), m2.reshape(shp), v2.reshape(shp)
    rest = ["b_ada"] + SMALL

    def drop(a):
        return a.reshape(1, -1) if a.ndim == 1 else (a if a.ndim == 2 else a[0])

    upd = _adamw_multi([(drop(w[n]), drop(grads[n]), drop(m[n]), drop(v[n])) for n in rest])
    for n, (dd, mm, vv) in zip(rest, upd):
        delta[n], new_m[n], new_v[n] = dd.reshape(w[n].shape), mm.reshape(w[n].shape), vv.reshape(w[n].shape)

    return (loss, dx, *[grads[n] for n in WEIGHTS], *[delta[n] for n in WEIGHTS], *[new_m[n] for n in WEIGHTS],
            *[new_v[n] for n in WEIGHTS])
```

```python
import functools
import math

import numpy as np
import jax
import jax.numpy as jnp
from jax import lax
from jax.experimental import pallas as pl
from jax.experimental.pallas import tpu as pltpu

F32, BF16 = jnp.float32, jnp.bfloat16

D_MODEL = 1024
N_HEADS = 8
HEAD_DIM = 64
ATT_WIDTH = 512
SSM_GROUPS = 16
SSM_GROUP_CH = 16
SSM_WIDTH = 256
SSM_STATE = 64
SSM_LANES = SSM_GROUPS * SSM_STATE
D_FF = 2048
IN_WIDTH = 3 * ATT_WIDTH + SSM_WIDTH + 2 * D_MODEL
ATT_BLOCK = 128
N_PATTERNS = 3
EPS = 1e-6
NEG_INF = -1e30

ADAM_LR, ADAM_B1, ADAM_B2, ADAM_EPS, ADAM_WD, ADAM_STEP = 0.001, 0.9, 0.999, 1e-08, 0.01, 10

V7X_VMEM_LIMIT_BYTES = 56 * 1024 * 1024
LANES = 1024

MESH_AXES = ("x", "y", "c")


def _pcall(body, *, name, out_shape, grid=(), in_specs=None, out_specs=None, scratch_shapes=(), dims=None):
    params = dict(vmem_limit_bytes=V7X_VMEM_LIMIT_BYTES)
    if dims is not None:
        params["dimension_semantics"] = dims
    specs = {}
    if in_specs is not None:
        specs = dict(grid=grid, in_specs=in_specs, out_specs=out_specs)
    return pl.pallas_call(body, name=name, out_shape=out_shape, scratch_shapes=scratch_shapes,
                          compiler_params=pltpu.CompilerParams(**params), **specs)


def _sds(shape, dtype):
    return jax.ShapeDtypeStruct(tuple(shape), dtype)


def _tile(n, target):
    if n <= target:
        return n
    for t in range(target - target % 128, 0, -128):
        if n % t == 0:
            return t
    raise ValueError((n, target))


def _sig(v):
    return pl.reciprocal(1.0 + jnp.exp(-v), approx=True)


def _mm(a, b, *, name, ta=False, tb=False, out_dtype=F32, tm=2048, tn=1024, tk=1024, riders=None):
    halves = a.ndim == 3
    if halves:
        a_rows, a_cols = a.shape[1], 2 * a.shape[2]
    else:
        a_rows, a_cols = a.shape
    if ta:
        K, M = a_rows, a_cols
    else:
        M, K = a_rows, a_cols
    if tb:
        N, K2 = b.shape
    else:
        K2, N = b.shape
    assert K == K2, (a.shape, b.shape)
    if halves:
        tm, tk = (min(tm, M // 2), tk) if ta else (tm, min(tk, K // 2))
    tm, tn, tk = _tile(M, tm), _tile(N, tn), _tile(K, tk)
    nk = K // tk
    if halves and ta:
        per = a.shape[2] // tm
        a_spec = pl.BlockSpec((None, tk, tm), lambda i, j, k: (i // per, k, i % per))
    elif halves:
        per = a.shape[2] // tk
        a_spec = pl.BlockSpec((None, tm, tk), lambda i, j, k: (k // per, i, k % per))
    else:
        a_spec = pl.BlockSpec((tk, tm), lambda i, j, k: (k, i)) if ta else pl.BlockSpec((tm, tk), lambda i, j, k: (i, k))
    b_spec = pl.BlockSpec((tn, tk), lambda i, j, k: (j, k)) if tb else pl.BlockSpec((tk, tn), lambda i, j, k: (k, j))
    dn = (((0 if ta else 1,), (1 if tb else 0,)), ((), ()))

    def body(a_ref, b_ref, o_ref, acc_ref):
        k = pl.program_id(2)

        @pl.when(k == 0)
        def _():
            acc_ref[...] = jnp.zeros_like(acc_ref)

        acc_ref[...] += lax.dot_general(a_ref[...].astype(BF16), b_ref[...].astype(BF16), dn,
                                        preferred_element_type=F32)

        @pl.when(k == nk - 1)
        def _():
            o_ref[...] = acc_ref[...].astype(out_dtype)

    def body_single(a_ref, b_ref, o_ref):
        o_ref[...] = lax.dot_general(a_ref[...].astype(BF16), b_ref[...].astype(BF16), dn,
                                     preferred_element_type=F32).astype(out_dtype)

    grid = (M // tm, N // tn, nk)
    scratch = [] if nk == 1 else [pltpu.VMEM((tm, tn), F32)]
    o_spec = pl.BlockSpec((tm, tn), lambda i, j, k: (i, j))
    if riders is None:
        return _pcall(body_single if nk == 1 else body, name=name, out_shape=_sds((M, N), out_dtype), grid=grid,
                      in_specs=[a_spec, b_spec], out_specs=o_spec, scratch_shapes=scratch,
                      dims=("parallel", "parallel", "arbitrary"))(a, b)
    rs = riders
    res = _pcall(_with_riders(body_single if nk == 1 else body, rs, 2, 1, len(scratch), tuple(g - 1 for g in grid)),
                 name=name, out_shape=(_sds((M, N), out_dtype),) + tuple(rs.out_shape), grid=grid,
                 in_specs=[a_spec, b_spec] + rs.specs, out_specs=(o_spec,) + tuple(rs.specs),
                 scratch_shapes=scratch + rs.scratch, dims=("arbitrary", "arbitrary", "arbitrary"))(a, b, *rs.arrs)
    return res[0], list(res[1:])


def _ada_fwd(c_all, w_ada, b_ada_cols):
    n = w_ada.shape[1]

    def body(c_ref, w_ref, b_ref, o_ref):
        c = c_ref[...]
        act = c * _sig(c)
        o_ref[...] = jnp.dot(act.astype(BF16), w_ref[...].astype(BF16), preferred_element_type=F32) + b_ref[...]

    return _pcall(body, name="ada_fwd", out_shape=_sds((c_all.shape[0], n), F32))(c_all, w_ada, b_ada_cols)


def _ada_bwd(c_all, dmod_all, dmod_cols):
    n = dmod_cols.shape[1]

    def body(c_ref, da_ref, dc_ref, gw_ref, gb_ref):
        c = c_ref[...]
        act = c * _sig(c)
        gw_ref[...] = lax.dot_general(act, dc_ref[...], (((0,), (0,)), ((), ())), preferred_element_type=F32,
                                      precision=lax.Precision.HIGHEST)
        gb_ref[...] = jnp.sum(da_ref[...], axis=0, keepdims=True)

    return _pcall(body, name="ada_bwd", out_shape=(_sds((D_MODEL, n), F32), _sds((1, dmod_all.shape[1]), F32)))(
        c_all, dmod_all, dmod_cols)


ROW_TILE = 512


def _row_specs(B, S):
    ts = min(S, ROW_TILE)
    row = pl.BlockSpec((1, ts, D_MODEL), lambda b, s: (b, s, 0))
    bvec = pl.BlockSpec((1, 1, D_MODEL), lambda b, s: (b, 0, 0))
    gvec = pl.BlockSpec((1, D_MODEL), lambda b, s: (0, 0))
    return ts, row, bvec, gvec


def _norm_mod(x3, g, sc, sh):
    B, S, _ = x3.shape
    ts, row, bvec, gvec = _row_specs(B, S)

    def body(x_ref, g_ref, sc_ref, sh_ref, u_ref):
        x = x_ref[0]
        r = lax.rsqrt(jnp.mean(x * x, axis=-1, keepdims=True) + EPS)
        u_ref[0] = ((x * r) * g_ref[...] * (1.0 + sc_ref[0]) + sh_ref[0]).astype(BF16)

    return _pcall(body, name="norm_mod1", out_shape=_sds(x3.shape, BF16), grid=(B, S // ts),
                  in_specs=[row, gvec, bvec, bvec], out_specs=row, dims=("parallel", "parallel"))(x3, g, sc, sh)


def _resid_norm_mod(x3, mix3, gt, g, sc, sh):
    B, S, _ = x3.shape
    ts, row, bvec, gvec = _row_specs(B, S)

    def body(x_ref, m_ref, gt_ref, g_ref, sc_ref, sh_ref, h_ref, u_ref):
        h = x_ref[0] + gt_ref[0] * m_ref[0]
        h_ref[0] = h
        r = lax.rsqrt(jnp.mean(h * h, axis=-1, keepdims=True) + EPS)
        u_ref[0] = ((h * r) * g_ref[...] * (1.0 + sc_ref[0]) + sh_ref[0]).astype(BF16)

    return _pcall(body, name="resid_norm_mod2", out_shape=(_sds(x3.shape, F32), _sds(x3.shape, BF16)),
                  grid=(B, S // ts), in_specs=[row, row, bvec, gvec, bvec, bvec], out_specs=(row, row),
                  dims=("parallel", "parallel"))(x3, mix3, gt, g, sc, sh)


def _norm_bwd(h3, du3, dres3, g, sc, name, mix3=None, gt=None):
    B, S, _ = h3.shape
    ts, row, bvec, gvec = _row_specs(B, S)
    with_gate = mix3 is not None

    def body(*refs):
        if with_gate:
            h_ref, du_ref, dr_ref, g_ref, sc_ref, m_ref, gt_ref, dh_ref, dsh_ref, dsc_ref, dg_ref, dgt_ref, dm_ref = refs
        else:
            h_ref, du_ref, dr_ref, g_ref, sc_ref, dh_ref, dsh_ref, dsc_ref, dg_ref = refs
        b, s = pl.program_id(0), pl.program_id(1)
        h = h_ref[0]
        r = lax.rsqrt(jnp.mean(h * h, axis=-1, keepdims=True) + EPS)
        xn = h * r
        du = du_ref[0].astype(F32)
        g = g_ref[...]
        sc1 = 1.0 + sc_ref[0]
        dxn = du * g * sc1
        dh = dr_ref[0] + r * (dxn - xn * jnp.mean(dxn * xn, axis=-1, keepdims=True))
        dh_ref[0] = dh

        @pl.when(s == 0)
        def _():
            dsh_ref[...] = jnp.zeros_like(dsh_ref)
            dsc_ref[...] = jnp.zeros_like(dsc_ref)
            if with_gate:
                dgt_ref[...] = jnp.zeros_like(dgt_ref)

        @pl.when((s == 0) & (b == 0))
        def _():
            dg_ref[...] = jnp.zeros_like(dg_ref)

        dux = du * xn
        dsh_ref[0] += jnp.sum(du, axis=0, keepdims=True)
        dsc_ref[0] += jnp.sum(dux * g, axis=0, keepdims=True)
        dg_ref[...] += jnp.sum(dux * sc1, axis=0, keepdims=True)
        if with_gate:
            dgt_ref[0] += jnp.sum(dh * m_ref[0], axis=0, keepdims=True)
            dm_ref[0] = (dh * gt_ref[0]).astype(BF16)

    bshape = _sds((B, 1, D_MODEL), F32)
    in_specs = [row, row, row, gvec, bvec]
    out_shape = [_sds(h3.shape, F32), bshape, bshape, _sds((1, D_MODEL), F32)]
    out_specs = [row, bvec, bvec, gvec]
    args = [h3, du3, dres3, g, sc]
    if with_gate:
        in_specs += [row, bvec]
        out_shape += [bshape, _sds(h3.shape, BF16)]
        out_specs += [bvec, row]
        args += [mix3, gt]
    return _pcall(body, name=name, out_shape=tuple(out_shape), grid=(B, S // ts), in_specs=in_specs,
                  out_specs=tuple(out_specs), dims=("arbitrary", "arbitrary"))(*args)


def _final_loss(h1, ffn3, tgt3, gt, gfin):
    B, S, _ = h1.shape
    ts, row, bvec, gvec = _row_specs(B, S)
    one = pl.BlockSpec((1, 1), lambda b, s: (0, 0))

    def body(h_ref, f_ref, t_ref, gt_ref, gf_ref, dh_ref, dff_ref, dgt_ref, dgf_ref, loss_ref):
        b, s = pl.program_id(0), pl.program_id(1)
        f = f_ref[0].astype(F32)
        gtv = gt_ref[0]
        gf = gf_ref[...]
        h2 = h_ref[0] + gtv * f
        r = lax.rsqrt(jnp.mean(h2 * h2, axis=-1, keepdims=True) + EPS)
        n = h2 * r
        e = n * gf - t_ref[0]
        dy = e * (1.0 / D_MODEL)
        dn = dy * gf
        dh2 = r * (dn - n * jnp.mean(dn * n, axis=-1, keepdims=True))
        dh_ref[0] = dh2
        dff_ref[0] = (dh2 * gtv).astype(BF16)

        @pl.when(s == 0)
        def _():
            dgt_ref[...] = jnp.zeros_like(dgt_ref)

        @pl.when((s == 0) & (b == 0))
        def _():
            dgf_ref[...] = jnp.zeros_like(dgf_ref)
            loss_ref[...] = jnp.zeros_like(loss_ref)

        dgt_ref[0] += jnp.sum(dh2 * f, axis=0, keepdims=True)
        dgf_ref[...] += jnp.sum(dy * n, axis=0, keepdims=True)
        rows = jnp.sum(e * e, axis=1, keepdims=True)
        loss_ref[...] += jnp.sum(rows, axis=0, keepdims=True) * (0.5 / D_MODEL)

    return _pcall(body, name="final_loss",
                  out_shape=(_sds(h1.shape, F32), _sds(h1.shape, BF16), _sds((B, 1, D_MODEL), F32),
                             _sds((1, D_MODEL), F32), _sds((1, 1), F32)),
                  grid=(B, S // ts), in_specs=[row, row, row, bvec, gvec], out_specs=(row, row, bvec, gvec, one),
                  dims=("arbitrary", "arbitrary"))(h1, ffn3, tgt3, gt, gfin)


def _att_scores(qh, kc, kp, h, dil, first, a_idx, j_idx):
    scale = HEAD_DIM ** -0.5
    nt = (((1,), (1,)), ((), ()))
    slope = (2.0 ** (-8.0 * (h + 1) / N_HEADS)) * dil
    dist_c = (a_idx - j_idx).astype(F32)
    s_c = lax.dot_general(qh, kc, nt, preferred_element_type=F32) * scale
    s_c = jnp.where(a_idx >= j_idx, s_c - slope * dist_c, NEG_INF)
    s_p = lax.dot_general(qh, kp, nt, preferred_element_type=F32) * scale
    s_p = jnp.where((j_idx >= a_idx) & jnp.logical_not(first), s_p - slope * (dist_c + float(ATT_BLOCK)), NEG_INF)
    return s_c, s_p


def _att_block_consts(seq_blocks):
    p = pl.program_id(0)
    j = pl.program_id(1)
    nb = lax.shift_right_logical(jnp.int32(seq_blocks), 2 * p)
    dil = lax.shift_left(jnp.int32(1), 2 * p).astype(F32)
    a_idx = lax.broadcasted_iota(jnp.int32, (ATT_BLOCK, ATT_BLOCK), 0)
    j_idx = lax.broadcasted_iota(jnp.int32, (ATT_BLOCK, ATT_BLOCK), 1)
    return j, nb, dil, a_idx, j_idx


def _attn_fwd(qb, kb, vb, seq_blocks):
    _, NB, _, _ = qb.shape
    cur = pl.BlockSpec((None, None, ATT_BLOCK, ATT_WIDTH), lambda p, j: (p, j, 0, 0))
    prev = pl.BlockSpec((None, None, ATT_BLOCK, ATT_WIDTH), lambda p, j: (p, jnp.maximum(j - 1, 0), 0, 0))
    lse_spec = pl.BlockSpec((None, None, ATT_BLOCK, N_HEADS), lambda p, j: (p, j, 0, 0))

    def body(q_ref, kc_ref, kp_ref, vc_ref, vp_ref, o_ref, lse_ref):
        j, nb, dil, a_idx, j_idx = _att_block_consts(seq_blocks)
        first = lax.rem(j, nb) == 0
        for h in range(N_HEADS):
            hs = slice(h * HEAD_DIM, (h + 1) * HEAD_DIM)
            s_c, s_p = _att_scores(q_ref[:, hs], kc_ref[:, hs], kp_ref[:, hs], h, dil, first, a_idx, j_idx)
            m = jnp.maximum(jnp.max(s_c, axis=1, keepdims=True), jnp.max(s_p, axis=1, keepdims=True))
            p_c = jnp.exp(s_c - m)
            p_p = jnp.exp(s_p - m)
            den = jnp.sum(p_c, axis=1, keepdims=True) + jnp.sum(p_p, axis=1, keepdims=True)
            o = (jnp.dot(p_c.astype(BF16), vc_ref[:, hs], preferred_element_type=F32)
                 + jnp.dot(p_p.astype(BF16), vp_ref[:, hs], preferred_element_type=F32))
            o_ref[:, hs] = o / den
            lse_ref[:, h:h + 1] = m + jnp.log(den)

    return _pcall(body, name="attn_fwd",
                  out_shape=(_sds(qb.shape, F32), _sds((N_PATTERNS, NB, ATT_BLOCK, N_HEADS), F32)),
                  grid=(N_PATTERNS, NB), in_specs=[cur, cur, prev, cur, prev], out_specs=(cur, lse_spec),
                  dims=("parallel", "parallel"))(qb, kb, kb, vb, vb)


def _attn_combine(o_p, lse_p):
    _, T, _ = o_p.shape
    tm = min(T, 1024)

    def body(o_ref, l_ref, out_ref, lse_ref):
        l0, l1, l2 = l_ref[0], l_ref[1], l_ref[2]
        m = jnp.maximum(jnp.maximum(l0, l1), l2)
        lse = m + jnp.log(jnp.exp(l0 - m) + jnp.exp(l1 - m) + jnp.exp(l2 - m))
        lse_ref[...] = lse
        w = [jnp.exp(l0 - lse), jnp.exp(l1 - lse), jnp.exp(l2 - lse)]
        for h in range(N_HEADS):
            hs = slice(h * HEAD_DIM, (h + 1) * HEAD_DIM)
            acc = w[0][:, h:h + 1] * o_ref[0, :, hs]
            acc = acc + w[1][:, h:h + 1] * o_ref[1, :, hs]
            acc = acc + w[2][:, h:h + 1] * o_ref[2, :, hs]
            out_ref[:, hs] = acc.astype(BF16)

    return _pcall(body, name="attn_combine", out_shape=(_sds((T, ATT_WIDTH), BF16), _sds((T, N_HEADS), F32)),
                  grid=(T // tm,),
                  in_specs=[pl.BlockSpec((N_PATTERNS, tm, ATT_WIDTH), lambda i: (0, i, 0)),
                            pl.BlockSpec((N_PATTERNS, tm, N_HEADS), lambda i: (0, i, 0))],
                  out_specs=(pl.BlockSpec((tm, ATT_WIDTH), lambda i: (i, 0)), pl.BlockSpec((tm, N_HEADS), lambda i: (i, 0))),
                  dims=("parallel",))(o_p, lse_p)


def _attn_bwd(qb, kb, vb, dob, ob, lseb, seq_blocks):
    _, NB, _, _ = qb.shape
    last = NB - 1
    cur = pl.BlockSpec((None, None, ATT_BLOCK, ATT_WIDTH), lambda p, j: (p, jnp.minimum(j, last), 0, 0))
    prev = pl.BlockSpec((None, None, ATT_BLOCK, ATT_WIDTH),
                        lambda p, j: (p, jnp.maximum(jnp.minimum(j, last) - 1, 0), 0, 0))
    lag = pl.BlockSpec((None, None, ATT_BLOCK, ATT_WIDTH), lambda p, j: (p, jnp.maximum(j - 1, 0), 0, 0))
    lse_spec = pl.BlockSpec((None, None, ATT_BLOCK, N_HEADS), lambda p, j: (p, jnp.minimum(j, last), 0, 0))
    scale = HEAD_DIM ** -0.5
    tn = (((0,), (0,)), ((), ()))
    nt = (((1,), (1,)), ((), ()))

    def body(q_ref, kc_ref, kp_ref, vc_ref, vp_ref, do_ref, o_ref, lse_ref, dq_ref, dk_ref, dv_ref, ck_ref, cv_ref):
        j, nb, dil, a_idx, j_idx = _att_block_consts(seq_blocks)

        @pl.when(j == 0)
        def _():
            ck_ref[...] = jnp.zeros_like(ck_ref)
            cv_ref[...] = jnp.zeros_like(cv_ref)

        @pl.when(j <= last)
        def _():
            first = lax.rem(j, nb) == 0
            for h in range(N_HEADS):
                hs = slice(h * HEAD_DIM, (h + 1) * HEAD_DIM)
                qh, kc, kp, vc, vp, doh = q_ref[:, hs], kc_ref[:, hs], kp_ref[:, hs], vc_ref[:, hs], vp_ref[:, hs], do_ref[:, hs]
                s_c, s_p = _att_scores(qh, kc, kp, h, dil, first, a_idx, j_idx)
                lse = lse_ref[:, h:h + 1]
                p_c = jnp.exp(s_c - lse)
                p_p = jnp.exp(s_p - lse)
                delta = jnp.sum(doh.astype(F32) * o_ref[:, hs].astype(F32), axis=1, keepdims=True)
                ds_c = (p_c * (lax.dot_general(doh, vc, nt, preferred_element_type=F32) - delta)).astype(BF16)
                ds_p = (p_p * (lax.dot_general(doh, vp, nt, preferred_element_type=F32) - delta)).astype(BF16)
                dq_ref[:, hs] = (jnp.dot(ds_c, kc, preferred_element_type=F32)
                                 + jnp.dot(ds_p, kp, preferred_element_type=F32)) * scale
                dk_ref[:, hs] = ck_ref[:, hs] + lax.dot_general(ds_p, qh, tn, preferred_element_type=F32) * scale
                dv_ref[:, hs] = cv_ref[:, hs] + lax.dot_general(p_p.astype(BF16), doh, tn, preferred_element_type=F32)
                ck_ref[:, hs] = lax.dot_general(ds_c, qh, tn, preferred_element_type=F32) * scale
                cv_ref[:, hs] = lax.dot_general(p_c.astype(BF16), doh, tn, preferred_element_type=F32)

        @pl.when(j == NB)
        def _():
            dk_ref[...] = ck_ref[...]
            dv_ref[...] = cv_ref[...]

    shp = _sds(qb.shape, F32)
    return _pcall(body, name="attn_bwd", out_shape=(shp, shp, shp), grid=(N_PATTERNS, NB + 1),
                  in_specs=[cur, cur, prev, cur, prev, cur, cur, lse_spec], out_specs=(cur, lag, lag),
                  scratch_shapes=[pltpu.VMEM((ATT_BLOCK, ATT_WIDTH), F32), pltpu.VMEM((ATT_BLOCK, ATT_WIDTH), F32)],
                  dims=("arbitrary", "arbitrary"))(qb, kb, kb, vb, vb, dob, ob, lseb)


def _sum3_cast(a, b, c):
    T, N = a.shape
    tm = min(T, 1024)
    spec = pl.BlockSpec((tm, N), lambda i: (i, 0))

    def body(a_ref, b_ref, c_ref, o_ref):
        o_ref[...] = (a_ref[...] + b_ref[...] + c_ref[...]).astype(BF16)

    return _pcall(body, name="sum3_cast", out_shape=_sds((T, N), BF16), grid=(T // tm,), in_specs=[spec] * 3,
                  out_specs=spec, dims=("parallel",))(a, b, c)


def _to_blocks(t, B, S):
    C = t.shape[-1]
    outs = []
    for p in range(N_PATTERNS):
        d = 4 ** p
        u = t.reshape(B, S // d, d, C).transpose(0, 2, 1, 3)
        outs.append(u.reshape(B * S // ATT_BLOCK, ATT_BLOCK, C))
    return jnp.stack(outs, axis=0)


def _from_blocks(tb, B, S):
    C = tb.shape[-1]
    outs = []
    for p in range(N_PATTERNS):
        d = 4 ** p
        u = tb[p].reshape(B, d, S // d, C).transpose(0, 2, 1, 3)
        outs.append(u.reshape(B * S, C))
    return jnp.stack(outs, axis=0)


ATT_GROUP = 4
ATT_GW = ATT_GROUP * HEAD_DIM
ATT_GROUPS = N_HEADS // ATT_GROUP
ATT_PAIRS = ATT_GW // ATT_BLOCK
ATT_UNROLL = 3
NT_DIMS = (((1,), (1,)), ((), ()))
TN_DIMS = (((0,), (0,)), ((), ()))


def _att_rows(start, d):
    if d == 1:
        return pl.ds(start if isinstance(start, int) else pl.multiple_of(start, ATT_BLOCK), ATT_BLOCK)
    return pl.ds(start, ATT_BLOCK, stride=d)


def _att_fill_bias(bias_ref, g, d):
    a = lax.broadcasted_iota(jnp.int32, (ATT_BLOCK, ATT_BLOCK), 0)
    j = lax.broadcasted_iota(jnp.int32, (ATT_BLOCK, ATT_BLOCK), 1)
    dist = (a - j).astype(F32)
    for hh in range(ATT_GROUP):
        t, e = divmod(hh, 2)
        rs = slice(e * ATT_BLOCK, (e + 1) * ATT_BLOCK)
        lo = 2.0 ** (-8.0 * (hh + 1) / N_HEADS) * d
        hi = 2.0 ** (-8.0 * (ATT_GROUP + hh + 1) / N_HEADS) * d
        slope = jnp.where(g == 0, lo, hi).astype(F32)
        bias_ref[t, rs, 0:ATT_BLOCK] = jnp.where(a >= j, -slope * dist, NEG_INF)
        bias_ref[t, rs, ATT_BLOCK:] = jnp.where(j >= a, -slope * (dist + float(ATT_BLOCK)), NEG_INF)


def _stack_heads(v2, low):
    return jnp.concatenate([jnp.where(low, v2, 0.0), jnp.where(low, 0.0, v2)], axis=0).astype(BF16)


def _unstack_heads(r2, low):
    return jnp.where(low, r2[0:ATT_BLOCK], r2[ATT_BLOCK:])


class _Riders:
    def __init__(self, arrs, mode):
        self.arrs, self.mode, self.n = list(arrs), mode, len(arrs)
        slot_shapes = [a.shape if mode == "gather" else a.shape[1:] for a in self.arrs]
        self.out_shape = [_sds((N_XY,) + s, a.dtype) for s, a in zip(slot_shapes, self.arrs)]
        k = len(_GROUP_MASKS["xy"])
        self.scratch = [pltpu.SemaphoreType.DMA((k * self.n,)), pltpu.SemaphoreType.DMA((k * self.n,)),
                        pltpu.SemaphoreType.DMA((2 * self.n,))] + [pltpu.VMEM(s, a.dtype) for s, a in zip(slot_shapes, self.arrs)]
        self.specs = [pl.BlockSpec(memory_space=pl.ANY)] * self.n

    def _remote(self, x_refs, o_refs, send_sems, recv_sems):
        x, y, c = lax.axis_index("x"), lax.axis_index("y"), lax.axis_index("c")
        me = 2 * x + y
        cps = []
        for i in range(self.n):
            for k, (dx, dy, _) in enumerate(_GROUP_MASKS["xy"]):
                px, py = _flip(x, dx), _flip(y, dy)
                src = x_refs[i] if self.mode == "gather" else x_refs[i].at[2 * px + py]
                cps.append(pltpu.make_async_remote_copy(
                    src_ref=src, dst_ref=o_refs[i].at[me], send_sem=send_sems.at[3 * i + k], recv_sem=recv_sems.at[3 * i + k],
                    device_id=(px, py, c), device_id_type=pl.DeviceIdType.MESH))
        return cps, me

    def start(self, x_refs, o_refs, scratch):
        send_sems, recv_sems, local_sems, bufs = scratch[0], scratch[1], scratch[2], scratch[3:]
        cps, me = self._remote(x_refs, o_refs, send_sems, recv_sems)
        for cp in cps:
            cp.start()
        for i in range(self.n):
            src = x_refs[i] if self.mode == "gather" else x_refs[i].at[me]
            load = pltpu.make_async_copy(src, bufs[i], local_sems.at[2 * i])
            load.start()
            load.wait()
            pltpu.make_async_copy(bufs[i], o_refs[i].at[me], local_sems.at[2 * i + 1]).start()

    def wait(self, x_refs, o_refs, scratch):
        send_sems, recv_sems, local_sems, bufs = scratch[0], scratch[1], scratch[2], scratch[3:]
        cps, me = self._remote(x_refs, o_refs, send_sems, recv_sems)
        for cp in cps:
            cp.wait()
        for i in range(self.n):
            pltpu.make_async_copy(bufs[i], o_refs[i].at[me], local_sems.at[2 * i + 1]).wait()


def _with_riders(compute, riders, n_in, n_out, n_scratch, last_step):
    if riders is None:
        return compute
    n = riders.n

    def body(*refs):
        ins, x_refs = refs[:n_in], refs[n_in:n_in + n]
        outs, o_refs = refs[n_in + n:n_in + n + n_out], refs[n_in + n + n_out:n_in + 2 * n + n_out]
        scratch = refs[n_in + 2 * n + n_out:]
        own, ride = scratch[:n_scratch], scratch[n_scratch:]
        ids = [pl.program_id(i) for i in range(len(last_step))]
        first = functools.reduce(jnp.logical_and, [i == 0 for i in ids])
        last = functools.reduce(jnp.logical_and, [i == l for i, l in zip(ids, last_step)])

        @pl.when(first)
        def _():
            riders.start(x_refs, o_refs, ride)

        compute(*ins, *outs, *own)

        @pl.when(last)
        def _():
            riders.wait(x_refs, o_refs, ride)

    return body


def _attention_fwd(proj3, seq_blocks, riders=None):
    B, S, _ = proj3.shape
    scale = HEAD_DIM ** -0.5
    nq = ATT_WIDTH // ATT_GW

    def col(k):
        return pl.BlockSpec((1, S, ATT_GW), lambda b, g, k=k: (b, 0, k * nq + g))

    o_spec = pl.BlockSpec((1, S, ATT_GW), lambda b, g: (b, 0, g))
    l_spec = pl.BlockSpec((1, 1, S, ATT_BLOCK), lambda b, g: (b, g, 0, 0))

    def compute(q_ref, k_ref, v_ref, o_ref, lse_ref, qf, kf, vf, os, ls, bias):
        g = pl.program_id(1)
        for t in range(ATT_PAIRS):
            ts = slice(t * ATT_BLOCK, (t + 1) * ATT_BLOCK)
            qf[t] = q_ref[0, :, ts].astype(F32) * scale
            kf[t] = k_ref[0, :, ts].astype(F32)
            vf[t] = v_ref[0, :, ts].astype(F32)
        lane = lax.broadcasted_iota(jnp.int32, (ATT_BLOCK, ATT_BLOCK), 1)
        low = lane < HEAD_DIM

        def block(p, d, r, n, has_prev):
            start = n * (ATT_BLOCK * d) + r
            rows = _att_rows(start, d)
            prows = _att_rows(start - ATT_BLOCK * d, d) if has_prev else None
            lse_t = jnp.zeros((ATT_BLOCK, ATT_BLOCK), F32)
            for t in range(ATT_PAIRS):
                q2 = _stack_heads(qf[t, rows, :], low)
                k2 = kf[t, rows, :].astype(BF16)
                v2 = vf[t, rows, :].astype(BF16)
                if has_prev:
                    k2 = jnp.concatenate([k2, kf[t, prows, :].astype(BF16)], axis=0)
                    v2 = jnp.concatenate([v2, vf[t, prows, :].astype(BF16)], axis=0)
                    b2 = bias[t]
                else:
                    b2 = bias[t, :, 0:ATT_BLOCK]
                s = lax.dot_general(q2, k2, NT_DIMS, preferred_element_type=F32) + b2
                m = jnp.max(s, axis=1, keepdims=True)
                pr = jnp.exp(s - m)
                den = jnp.sum(pr, axis=1, keepdims=True)
                o = jnp.dot(pr.astype(BF16), v2, preferred_element_type=F32) * (1.0 / den)
                os[p, t, rows, :] = _unstack_heads(o, low)
                lse2 = m + jnp.log(den)
                lse_t = jnp.where(lane == 2 * t, lse2[0:ATT_BLOCK], lse_t)
                lse_t = jnp.where(lane == 2 * t + 1, lse2[ATT_BLOCK:], lse_t)
            ls[p, rows, :] = lse_t

        for p in range(N_PATTERNS):
            d = 4 ** p
            _att_fill_bias(bias, g, d)
            _att_one_pattern(block, p, d, seq_blocks // d)

        def combine(i, carry):
            rows = pl.ds(pl.multiple_of(i * ATT_BLOCK, ATT_BLOCK), ATT_BLOCK)
            l0, l1, l2 = ls[0, rows, :], ls[1, rows, :], ls[2, rows, :]
            m = jnp.maximum(jnp.maximum(l0, l1), l2)
            lse = m + jnp.log(jnp.exp(l0 - m) + jnp.exp(l1 - m) + jnp.exp(l2 - m))
            lse_ref[0, 0, rows, :] = lse
            w = [jnp.exp(l0 - lse), jnp.exp(l1 - lse), jnp.exp(l2 - lse)]
            for t in range(ATT_PAIRS):
                acc = jnp.zeros((ATT_BLOCK, ATT_BLOCK), F32)
                for p in range(N_PATTERNS):
                    wt = jnp.where(low, w[p][:, 2 * t:2 * t + 1], w[p][:, 2 * t + 1:2 * t + 2])
                    acc = acc + wt * os[p, t, rows, :]
                o_ref[0, rows, t * ATT_BLOCK:(t + 1) * ATT_BLOCK] = acc.astype(BF16)
            return carry

        lax.fori_loop(0, S // ATT_BLOCK, combine, 0, unroll=2)

    scratch = ([pltpu.VMEM((ATT_PAIRS, S, ATT_BLOCK), F32)] * 3
               + [pltpu.VMEM((N_PATTERNS, ATT_PAIRS, S, ATT_BLOCK), F32), pltpu.VMEM((N_PATTERNS, S, ATT_BLOCK), F32),
                  pltpu.VMEM((ATT_PAIRS, 2 * ATT_BLOCK, 2 * ATT_BLOCK), F32)])
    rs = riders
    res = _pcall(_with_riders(compute, rs, 3, 2, len(scratch), (B - 1, ATT_GROUPS - 1)), name="attention_fwd",
                 out_shape=(_sds((B, S, ATT_WIDTH), BF16), _sds((B, ATT_GROUPS, S, ATT_BLOCK), F32))
                 + (tuple(rs.out_shape) if rs else ()),
                 grid=(B, ATT_GROUPS), in_specs=[col(0), col(1), col(2)] + (rs.specs if rs else []),
                 out_specs=(o_spec, l_spec) + (tuple(rs.specs) if rs else ()),
                 scratch_shapes=scratch + (rs.scratch if rs else []),
                 dims=("arbitrary", "arbitrary"))(proj3, proj3, proj3, *(rs.arrs if rs else []))
    return res[0], res[1], list(res[2:])


def _att_one_pattern(block, p, d, nb):
    def per_residue(r, carry):
        block(p, d, r, 0, False)
        if nb > 1:
            def per_block(n, c2):
                block(p, d, r, n, True)
                return c2
            lax.fori_loop(1, nb, per_block, 0, unroll=ATT_UNROLL)
        return carry

    if d == 1:
        per_residue(0, 0)
    else:
        lax.fori_loop(0, d, per_residue, 0, unroll=ATT_UNROLL + 1 if nb == 1 else 1)


def _attention_bwd(proj3, do3, o3, lse4, seq_blocks, riders=None):
    B, S, _ = proj3.shape
    scale = HEAD_DIM ** -0.5
    nq = ATT_WIDTH // ATT_GW

    def col(k):
        return pl.BlockSpec((1, S, ATT_GW), lambda b, g, k=k: (b, 0, k * nq + g))

    o_spec = pl.BlockSpec((1, S, ATT_GW), lambda b, g: (b, 0, g))
    l_spec = pl.BlockSpec((1, 1, S, ATT_BLOCK), lambda b, g: (b, g, 0, 0))

    def compute(q_ref, k_ref, v_ref, do_ref, o_ref, lse_ref, dq_ref, dk_ref, dv_ref,
                qf, kf, vf, dof, dl, aq, ak, av, bias):
        g = pl.program_id(1)
        for t in range(ATT_PAIRS):
            ts = slice(t * ATT_BLOCK, (t + 1) * ATT_BLOCK)
            qf[t] = q_ref[0, :, ts].astype(F32) * scale
            kf[t] = k_ref[0, :, ts].astype(F32)
            vf[t] = v_ref[0, :, ts].astype(F32)
            dof[t] = do_ref[0, :, ts].astype(F32)
        aq[...] = jnp.zeros_like(aq)
        ak[...] = jnp.zeros_like(ak)
        av[...] = jnp.zeros_like(av)
        lane = lax.broadcasted_iota(jnp.int32, (ATT_BLOCK, ATT_BLOCK), 1)
        low = lane < HEAD_DIM

        def fill_delta(i, carry):
            rows = pl.ds(pl.multiple_of(i * ATT_BLOCK, ATT_BLOCK), ATT_BLOCK)
            acc = jnp.zeros((ATT_BLOCK, ATT_BLOCK), F32)
            for t in range(ATT_PAIRS):
                prod = dof[t, rows, :] * o_ref[0, rows, t * ATT_BLOCK:(t + 1) * ATT_BLOCK].astype(F32)
                lo = jnp.sum(jnp.where(low, prod, 0.0), axis=1, keepdims=True)
                hi = jnp.sum(prod, axis=1, keepdims=True) - lo
                acc = jnp.where(lane == 2 * t, lo, acc)
                acc = jnp.where(lane == 2 * t + 1, hi, acc)
            dl[rows, :] = acc
            return carry

        lax.fori_loop(0, S // ATT_BLOCK, fill_delta, 0, unroll=2)

        def block(p, d, r, n, has_prev):
            start = n * (ATT_BLOCK * d) + r
            rows = _att_rows(start, d)
            prows = _att_rows(start - ATT_BLOCK * d, d) if has_prev else None
            lse_t = lse_ref[0, 0, rows, :]
            dl_t = dl[rows, :]
            for t in range(ATT_PAIRS):
                q2 = _stack_heads(qf[t, rows, :], low)
                do2 = _stack_heads(dof[t, rows, :], low)
                k2 = kf[t, rows, :].astype(BF16)
                v2 = vf[t, rows, :].astype(BF16)
                if has_prev:
                    k2 = jnp.concatenate([k2, kf[t, prows, :].astype(BF16)], axis=0)
                    v2 = jnp.concatenate([v2, vf[t, prows, :].astype(BF16)], axis=0)
                    b2 = bias[t]
                else:
                    b2 = bias[t, :, 0:ATT_BLOCK]
                lse2 = jnp.concatenate([lse_t[:, 2 * t:2 * t + 1], lse_t[:, 2 * t + 1:2 * t + 2]], axis=0)
                dl2 = jnp.concatenate([dl_t[:, 2 * t:2 * t + 1], dl_t[:, 2 * t + 1:2 * t + 2]], axis=0)
                s = lax.dot_general(q2, k2, NT_DIMS, preferred_element_type=F32) + b2
                pr = jnp.exp(s - lse2)
                ds = (pr * (lax.dot_general(do2, v2, NT_DIMS, preferred_element_type=F32) - dl2)).astype(BF16)
                dq = _unstack_heads(jnp.dot(ds, k2, preferred_element_type=F32), low)
                dk = lax.dot_general(ds, q2, TN_DIMS, preferred_element_type=F32)
                dv = lax.dot_general(pr.astype(BF16), do2, TN_DIMS, preferred_element_type=F32)
                aq[t, rows, :] = aq[t, rows, :] + dq * scale
                ak[t, rows, :] = ak[t, rows, :] + dk[0:ATT_BLOCK]
                av[t, rows, :] = av[t, rows, :] + dv[0:ATT_BLOCK]
                if has_prev:
                    ak[t, prows, :] = ak[t, prows, :] + dk[ATT_BLOCK:]
                    av[t, prows, :] = av[t, prows, :] + dv[ATT_BLOCK:]

        for p in range(N_PATTERNS):
            d = 4 ** p
            _att_fill_bias(bias, g, d)
            _att_one_pattern(block, p, d, seq_blocks // d)

        for t in range(ATT_PAIRS):
            ts = slice(t * ATT_BLOCK, (t + 1) * ATT_BLOCK)
            dq_ref[0, :, ts] = aq[t].astype(BF16)
            dk_ref[0, :, ts] = ak[t].astype(BF16)
            dv_ref[0, :, ts] = av[t].astype(BF16)

    shp = _sds((B, S, ATT_WIDTH), BF16)
    pair_buf = pltpu.VMEM((ATT_PAIRS, S, ATT_BLOCK), F32)
    scratch = ([pair_buf] * 4 + [pltpu.VMEM((S, ATT_BLOCK), F32)] + [pair_buf] * 3
               + [pltpu.VMEM((ATT_PAIRS, 2 * ATT_BLOCK, 2 * ATT_BLOCK), F32)])
    rs = riders
    res = _pcall(_with_riders(compute, rs, 6, 3, len(scratch), (B - 1, ATT_GROUPS - 1)), name="attention_bwd",
                 out_shape=(shp, shp, shp) + (tuple(rs.out_shape) if rs else ()), grid=(B, ATT_GROUPS),
                 in_specs=[col(0), col(1), col(2), o_spec, o_spec, l_spec] + (rs.specs if rs else []),
                 out_specs=(o_spec, o_spec, o_spec) + (tuple(rs.specs) if rs else ()),
                 scratch_shapes=scratch + (rs.scratch if rs else []),
                 dims=("arbitrary", "arbitrary"))(proj3, proj3, proj3, do3, o3, lse4, *(rs.arrs if rs else []))
    return res[0], res[1], res[2], list(res[3:])


def _expand_groups(m):
    rows = SSM_WIDTH
    t = jnp.concatenate([m] * SSM_GROUPS, axis=0)
    r = lax.broadcasted_iota(jnp.int32, (rows, SSM_LANES), 0)
    l = lax.broadcasted_iota(jnp.int32, (rows, SSM_LANES), 1)
    keep = lax.shift_right_logical(r, 4) == lax.shift_right_logical(l, 6)
    return jnp.where(keep, t, 0.0)


def _collapse_groups(m):
    rows = SSM_WIDTH
    r = lax.broadcasted_iota(jnp.int32, (rows, SSM_LANES), 0)
    l = lax.broadcasted_iota(jnp.int32, (rows, SSM_LANES), 1)
    keep = lax.shift_right_logical(r, 4) == lax.shift_right_logical(l, 6)
    t = jnp.where(keep, m, 0.0)
    acc = t[0:SSM_GROUP_CH]
    for g in range(1, SSM_GROUPS):
        acc = acc + t[g * SSM_GROUP_CH:(g + 1) * SSM_GROUP_CH]
    return acc


def _zoh(lr, li, ldt):
    dt = jnp.exp(ldt)
    mag = jnp.exp(lr * dt)
    ang = li * dt
    cs, sn = jnp.cos(ang), jnp.sin(ang)
    ab_re, ab_im = mag * cs, mag * sn
    nr, ni = ab_re - 1.0, ab_im
    den = lr * lr + li * li
    n_re = nr * lr + ni * li
    n_im = ni * lr - nr * li
    return dict(dt=dt, mag=mag, cs=cs, sn=sn, ab_re=ab_re, ab_im=ab_im, nr=nr, ni=ni, den=den, n_re=n_re, n_im=n_im,
                f_re=n_re / den, f_im=n_im / den)


def _ssm_params(lr, li, ldt, br, bi, cr, ci):
    def body(lr_ref, li_ref, ldt_ref, br_ref, bi_ref, cr_ref, ci_ref, ab_ref, w_ref, c_ref):
        z = _zoh(lr_ref[...], li_ref[...], ldt_ref[...])
        ab_ref[0:1, :] = z["ab_re"]
        ab_ref[1:2, :] = z["ab_im"]
        br, bi = br_ref[...], bi_ref[...]
        w_ref[:, 0:SSM_LANES] = _expand_groups(z["f_re"] * br - z["f_im"] * bi).astype(BF16)
        w_ref[:, SSM_LANES:] = _expand_groups(z["f_re"] * bi + z["f_im"] * br).astype(BF16)
        c_ref[:, 0:SSM_LANES] = _expand_groups(cr_ref[...]).astype(BF16)
        c_ref[:, SSM_LANES:] = _expand_groups(-ci_ref[...]).astype(BF16)

    return _pcall(body, name="ssm_params",
                  out_shape=(_sds((2, SSM_LANES), F32), _sds((SSM_WIDTH, 2 * SSM_LANES), BF16),
                             _sds((SSM_WIDTH, 2 * SSM_LANES), BF16)))(lr, li, ldt, br, bi, cr, ci)


def _ssm_params_bwd(lr, li, ldt, br, bi, dab, dw, dc):
    def body(lr_ref, li_ref, ldt_ref, br_ref, bi_ref, dab_ref, dw_ref, dc_ref,
             dlr_ref, dli_ref, dldt_ref, dbr_ref, dbi_ref, dcr_ref, dci_ref):
        lr, li = lr_ref[...], li_ref[...]
        z = _zoh(lr, li, ldt_ref[...])
        br, bi = br_ref[...], bi_ref[...]
        dbb_re = _collapse_groups(dw_ref[:, 0:SSM_LANES])
        dbb_im = _collapse_groups(dw_ref[:, SSM_LANES:])
        dcr_ref[...] = _collapse_groups(dc_ref[:, 0:SSM_LANES])
        dci_ref[...] = -_collapse_groups(dc_ref[:, SSM_LANES:])
        f_re, f_im = z["f_re"], z["f_im"]
        dbr_ref[...] = f_re * dbb_re + f_im * dbb_im
        dbi_ref[...] = f_re * dbb_im - f_im * dbb_re
        df_re = jnp.sum(dbb_re * br + dbb_im * bi, axis=0, keepdims=True)
        df_im = jnp.sum(dbb_im * br - dbb_re * bi, axis=0, keepdims=True)
        den = z["den"]
        dn_re, dn_im = df_re / den, df_im / den
        dden = -(df_re * z["n_re"] + df_im * z["n_im"]) / (den * den)
        dnr = dn_re * lr - dn_im * li
        dni = dn_re * li + dn_im * lr
        dlr = dn_re * z["nr"] + dn_im * z["ni"] + 2.0 * dden * lr
        dli = dn_re * z["ni"] - dn_im * z["nr"] + 2.0 * dden * li
        dab_re = dab_ref[0:1, :] + dnr
        dab_im = dab_ref[1:2, :] + dni
        mag, cs, sn, dt = z["mag"], z["cs"], z["sn"], z["dt"]
        dmag = dab_re * cs + dab_im * sn
        dang = mag * (dab_im * cs - dab_re * sn)
        dlr_ref[...] = dlr + dmag * mag * dt
        dli_ref[...] = dli + dang * dt
        ddt = dmag * mag * lr + dang * li
        per_lane = jnp.broadcast_to(ddt * dt, (8, SSM_LANES))
        lane = lax.broadcasted_iota(jnp.int32, (SSM_LANES, 128), 0)
        col = lax.broadcasted_iota(jnp.int32, (SSM_LANES, 128), 1)
        ind = jnp.where(lax.shift_right_logical(lane, 6) == col, 1.0, 0.0)
        dldt_ref[...] = jnp.dot(per_lane, ind, preferred_element_type=F32, precision=lax.Precision.HIGHEST)[0:1]

    vec = _sds((1, SSM_LANES), F32)
    mat = _sds((SSM_GROUP_CH, SSM_LANES), F32)
    return _pcall(body, name="ssm_params_bwd", out_shape=(vec, vec, _sds((1, 128), F32), mat, mat, mat, mat))(
        lr, li, ldt, br, bi, dab, dw, dc)


SCAN_CHUNK = 512


def _scan_consts(ar, ai, k_ref, reverse):
    row = lax.broadcasted_iota(jnp.int32, (8, SSM_LANES), 0)
    pw = [(ar, ai)]
    for _ in range(7):
        pr, pi = pw[-1]
        pw.append((pr * ar - pi * ai, pr * ai + pi * ar))
    for n, k in enumerate((1, 2, 4)):
        keep = (row < 8 - k) if reverse else (row >= k)
        k_ref[2 * n] = jnp.where(keep, jnp.broadcast_to(pw[k - 1][0], (8, SSM_LANES)), 0.0)
        k_ref[2 * n + 1] = jnp.where(keep, jnp.broadcast_to(pw[k - 1][1], (8, SSM_LANES)), 0.0)
    cr = jnp.zeros((8, SSM_LANES), F32)
    ci = jnp.zeros((8, SSM_LANES), F32)
    for r in range(8):
        e = (8 - r) if reverse else (r + 1)
        cr = jnp.where(row == r, jnp.broadcast_to(pw[e - 1][0], (8, SSM_LANES)), cr)
        ci = jnp.where(row == r, jnp.broadcast_to(pw[e - 1][1], (8, SSM_LANES)), ci)
    k_ref[6] = cr
    k_ref[7] = ci


def _scan_tile(xr, xi, k_ref, car, cai, reverse):
    for n, k in enumerate((1, 2, 4)):
        sh = (8 - k) if reverse else k
        sr = pltpu.roll(xr, sh, 0)
        si = pltpu.roll(xi, sh, 0)
        mr, mi = k_ref[2 * n], k_ref[2 * n + 1]
        xr, xi = xr + mr * sr - mi * si, xi + mr * si + mi * sr
    pr, pi = k_ref[6], k_ref[7]
    xr, xi = xr + pr * car - pi * cai, xi + pr * cai + pi * car
    return xr, xi


def _scan_fwd(bu3, abar):
    B, S, _ = bu3.shape
    ch = min(S, SCAN_CHUNK)
    blk = pl.BlockSpec((1, ch, 2 * SSM_LANES), lambda b, c: (b, c, 0))

    def body(ab_ref, bu_ref, x_ref, k_ref, carry_ref):
        _scan_consts(ab_ref[0:1, :], ab_ref[1:2, :], k_ref, False)

        @pl.when(pl.program_id(1) == 0)
        def _():
            carry_ref[...] = jnp.zeros_like(carry_ref)

        def step(i, carry):
            base = pl.multiple_of(i * 8, 8)
            xr = bu_ref[0, pl.ds(base, 8), 0:SSM_LANES]
            xi = bu_ref[0, pl.ds(base, 8), SSM_LANES:]
            xr, xi = _scan_tile(xr, xi, k_ref, carry[0], carry[1], False)
            x_ref[0, pl.ds(base, 8), 0:SSM_LANES] = xr
            x_ref[0, pl.ds(base, 8), SSM_LANES:] = xi
            return (jnp.broadcast_to(xr[7:8], (8, SSM_LANES)), jnp.broadcast_to(xi[7:8], (8, SSM_LANES)))

        cr, ci = lax.fori_loop(0, ch // 8, step, (carry_ref[0], carry_ref[1]))
        carry_ref[0] = cr
        carry_ref[1] = ci

    return _pcall(body, name="scan_fwd", out_shape=_sds(bu3.shape, F32), grid=(B, S // ch),
                  in_specs=[pl.BlockSpec((2, SSM_LANES), lambda b, c: (0, 0)), blk], out_specs=blk,
                  scratch_shapes=[pltpu.VMEM((8, 8, SSM_LANES), F32), pltpu.VMEM((2, 8, SSM_LANES), F32)],
                  dims=("arbitrary", "arbitrary"))(abar, bu3)


def _scan_bwd(dx3, xs3, abar):
    B, S, _ = dx3.shape
    ch = min(S, SCAN_CHUNK)
    nc = S // ch
    blk = pl.BlockSpec((1, ch, 2 * SSM_LANES), lambda b, c: (b, nc - 1 - c, 0))

    def body(ab_ref, dx_ref, xs_ref, g_ref, da_ref, k_ref, carry_ref, acc_ref):
        b, c = pl.program_id(0), pl.program_id(1)
        _scan_consts(ab_ref[0:1, :], -ab_ref[1:2, :], k_ref, True)
        row = lax.broadcasted_iota(jnp.int32, (8, SSM_LANES), 0)

        @pl.when(c == 0)
        def _():
            carry_ref[...] = jnp.zeros_like(carry_ref)

        @pl.when((c == 0) & (b == 0))
        def _():
            acc_ref[...] = jnp.zeros_like(acc_ref)

        def step(i, carry):
            car, cai, ar_acc, ai_acc = carry
            base = pl.multiple_of((ch // 8 - 1 - i) * 8, 8)
            gr = dx_ref[0, pl.ds(base, 8), 0:SSM_LANES]
            gi = dx_ref[0, pl.ds(base, 8), SSM_LANES:]
            gr, gi = _scan_tile(gr, gi, k_ref, car, cai, True)
            g_ref[0, pl.ds(base, 8), 0:SSM_LANES] = gr
            g_ref[0, pl.ds(base, 8), SSM_LANES:] = gi
            nr = jnp.where(row == 7, car, pltpu.roll(gr, 7, 0))
            ni = jnp.where(row == 7, cai, pltpu.roll(gi, 7, 0))
            xr = xs_ref[0, pl.ds(base, 8), 0:SSM_LANES]
            xi = xs_ref[0, pl.ds(base, 8), SSM_LANES:]
            ar_acc = ar_acc + nr * xr + ni * xi
            ai_acc = ai_acc + ni * xr - nr * xi
            return (jnp.broadcast_to(gr[0:1], (8, SSM_LANES)), jnp.broadcast_to(gi[0:1], (8, SSM_LANES)), ar_acc, ai_acc)

        cr, ci, ar_acc, ai_acc = lax.fori_loop(0, ch // 8, step, (carry_ref[0], carry_ref[1], acc_ref[0], acc_ref[1]))
        carry_ref[0] = cr
        carry_ref[1] = ci
        acc_ref[0] = ar_acc
        acc_ref[1] = ai_acc
        da_ref[0:1, :] = jnp.sum(ar_acc, axis=0, keepdims=True)
        da_ref[1:2, :] = jnp.sum(ai_acc, axis=0, keepdims=True)

    return _pcall(body, name="scan_bwd", out_shape=(_sds(dx3.shape, F32), _sds((2, SSM_LANES), F32)), grid=(B, nc),
                  in_specs=[pl.BlockSpec((2, SSM_LANES), lambda b, c: (0, 0)), blk, blk],
                  out_specs=(blk, pl.BlockSpec((2, SSM_LANES), lambda b, c: (0, 0))),
                  scratch_shapes=[pltpu.VMEM((8, 8, SSM_LANES), F32), pltpu.VMEM((2, 8, SSM_LANES), F32),
                                  pltpu.VMEM((2, 8, SSM_LANES), F32)],
                  dims=("arbitrary", "arbitrary"))(abar, dx3, xs3)


US_BLOCK = (3 * ATT_WIDTH) // SSM_WIDTH


def _ssm_scan_fwd(proj3, abar, w_bu, w_c):
    B, S, _ = proj3.shape
    ch = min(S, SCAN_CHUNK)
    u_spec = pl.BlockSpec((1, ch, SSM_WIDTH), lambda b, c: (b, c, US_BLOCK))
    x_spec = pl.BlockSpec((1, ch, 2 * SSM_LANES), lambda b, c: (b, c, 0))
    y_spec = pl.BlockSpec((1, ch, SSM_WIDTH), lambda b, c: (b, c, 0))
    w_spec = pl.BlockSpec((SSM_WIDTH, 2 * SSM_LANES), lambda b, c: (0, 0))

    def body(ab_ref, u_ref, wb_ref, wc_ref, x_ref, y_ref, k_ref, carry_ref):
        _scan_consts(ab_ref[0:1, :], ab_ref[1:2, :], k_ref, False)

        @pl.when(pl.program_id(1) == 0)
        def _():
            carry_ref[...] = jnp.zeros_like(carry_ref)

        x_ref[0] = jnp.dot(u_ref[0], wb_ref[...], preferred_element_type=F32)

        def step(i, carry):
            base = pl.multiple_of(i * 8, 8)
            xr = x_ref[0, pl.ds(base, 8), 0:SSM_LANES]
            xi = x_ref[0, pl.ds(base, 8), SSM_LANES:]
            xr, xi = _scan_tile(xr, xi, k_ref, carry[0], carry[1], False)
            x_ref[0, pl.ds(base, 8), 0:SSM_LANES] = xr
            x_ref[0, pl.ds(base, 8), SSM_LANES:] = xi
            return (jnp.broadcast_to(xr[7:8], (8, SSM_LANES)), jnp.broadcast_to(xi[7:8], (8, SSM_LANES)))

        cr, ci = lax.fori_loop(0, ch // 8, step, (carry_ref[0], carry_ref[1]))
        carry_ref[0] = cr
        carry_ref[1] = ci
        y_ref[0] = lax.dot_general(x_ref[0].astype(BF16), wc_ref[...], NT_DIMS, preferred_element_type=F32)

    return _pcall(body, name="ssm_scan_fwd",
                  out_shape=(_sds((B, S, 2 * SSM_LANES), F32), _sds((B, S, SSM_WIDTH), F32)), grid=(B, S // ch),
                  in_specs=[pl.BlockSpec((2, SSM_LANES), lambda b, c: (0, 0)), u_spec, w_spec, w_spec],
                  out_specs=(x_spec, y_spec),
                  scratch_shapes=[pltpu.VMEM((8, 8, SSM_LANES), F32), pltpu.VMEM((2, 8, SSM_LANES), F32)],
                  dims=("arbitrary", "arbitrary"))(abar, proj3, w_bu, w_c)


def _ssm_scan_bwd(proj3, dy3, xs3, abar, w_bu, w_c, dsk):
    B, S, _ = proj3.shape
    ch = min(S, SCAN_CHUNK)
    nc = S // ch
    u_spec = pl.BlockSpec((1, ch, SSM_WIDTH), lambda b, c: (b, nc - 1 - c, US_BLOCK))
    x_spec = pl.BlockSpec((1, ch, 2 * SSM_LANES), lambda b, c: (b, nc - 1 - c, 0))
    y_spec = pl.BlockSpec((1, ch, SSM_WIDTH), lambda b, c: (b, nc - 1 - c, 0))
    w_spec = pl.BlockSpec((SSM_WIDTH, 2 * SSM_LANES), lambda b, c: (0, 0))
    ab_spec = pl.BlockSpec((2, SSM_LANES), lambda b, c: (0, 0))
    d_spec = pl.BlockSpec((1, SSM_WIDTH), lambda b, c: (0, 0))

    def body(ab_ref, u_ref, dy_ref, xs_ref, wb_ref, wc_ref, d_ref, du_ref, da_ref, dwb_ref, dwc_ref,
             g_ref, k_ref, carry_ref, acc_ref):
        b, c = pl.program_id(0), pl.program_id(1)
        _scan_consts(ab_ref[0:1, :], -ab_ref[1:2, :], k_ref, True)
        row = lax.broadcasted_iota(jnp.int32, (8, SSM_LANES), 0)

        @pl.when(c == 0)
        def _():
            carry_ref[...] = jnp.zeros_like(carry_ref)

        @pl.when((c == 0) & (b == 0))
        def _():
            acc_ref[...] = jnp.zeros_like(acc_ref)
            dwb_ref[...] = jnp.zeros_like(dwb_ref)
            dwc_ref[...] = jnp.zeros_like(dwc_ref)

        dy = dy_ref[0]
        dyb = dy.astype(BF16)
        g_ref[...] = jnp.dot(dyb, wc_ref[...], preferred_element_type=F32)

        def step(i, carry):
            car, cai, ar_acc, ai_acc = carry
            base = pl.multiple_of((ch // 8 - 1 - i) * 8, 8)
            gr = g_ref[pl.ds(base, 8), 0:SSM_LANES]
            gi = g_ref[pl.ds(base, 8), SSM_LANES:]
            gr, gi = _scan_tile(gr, gi, k_ref, car, cai, True)
            g_ref[pl.ds(base, 8), 0:SSM_LANES] = gr
            g_ref[pl.ds(base, 8), SSM_LANES:] = gi
            nr = jnp.where(row == 7, car, pltpu.roll(gr, 7, 0))
            ni = jnp.where(row == 7, cai, pltpu.roll(gi, 7, 0))
            xr = xs_ref[0, pl.ds(base, 8), 0:SSM_LANES]
            xi = xs_ref[0, pl.ds(base, 8), SSM_LANES:]
            ar_acc = ar_acc + nr * xr + ni * xi
            ai_acc = ai_acc + ni * xr - nr * xi
            return (jnp.broadcast_to(gr[0:1], (8, SSM_LANES)), jnp.broadcast_to(gi[0:1], (8, SSM_LANES)), ar_acc, ai_acc)

        cr, ci, ar_acc, ai_acc = lax.fori_loop(0, ch // 8, step, (carry_ref[0], carry_ref[1], acc_ref[0], acc_ref[1]))
        carry_ref[0] = cr
        carry_ref[1] = ci
        acc_ref[0] = ar_acc
        acc_ref[1] = ai_acc
        da_ref[0:1, :] = jnp.sum(ar_acc, axis=0, keepdims=True)
        da_ref[1:2, :] = jnp.sum(ai_acc, axis=0, keepdims=True)

        gb = g_ref[...].astype(BF16)
        du = lax.dot_general(gb, wb_ref[...], NT_DIMS, preferred_element_type=F32) + d_ref[...] * dy
        du_ref[0] = du.astype(BF16)
        dwb_ref[...] += lax.dot_general(u_ref[0], gb, TN_DIMS, preferred_element_type=F32)
        dwc_ref[...] += lax.dot_general(dyb, xs_ref[0].astype(BF16), TN_DIMS, preferred_element_type=F32)

    mat = _sds((SSM_WIDTH, 2 * SSM_LANES), F32)
    return _pcall(body, name="ssm_scan_bwd",
                  out_shape=(_sds((B, S, SSM_WIDTH), BF16), _sds((2, SSM_LANES), F32), mat, mat), grid=(B, nc),
                  in_specs=[ab_spec, u_spec, y_spec, x_spec, w_spec, w_spec, d_spec],
                  out_specs=(y_spec, ab_spec, w_spec, w_spec),
                  scratch_shapes=[pltpu.VMEM((ch, 2 * SSM_LANES), F32), pltpu.VMEM((8, 8, SSM_LANES), F32),
                                  pltpu.VMEM((2, 8, SSM_LANES), F32), pltpu.VMEM((2, 8, SSM_LANES), F32)],
                  dims=("arbitrary", "arbitrary"))(abar, proj3, dy3, xs3, w_bu, w_c, dsk)


GELU_K = math.sqrt(2.0 / math.pi)
GELU_C = 0.044715


def _gelu_parts(y):
    t = jnp.tanh(GELU_K * (y + GELU_C * y * y * y))
    return 0.5 * y * (1.0 + t), t


def _ssm_post(yc, us, dsk, wglu, bglu):
    T, N = yc.shape
    tm = min(T, 1024)
    row = pl.BlockSpec((tm, N), lambda i: (i, 0))
    vec = pl.BlockSpec((1, N), lambda i: (0, 0))
    mat = pl.BlockSpec((N, N), lambda i: (0, 0))

    def body(yc_ref, us_ref, d_ref, w_ref, b_ref, y_ref, s_ref):
        y = yc_ref[...] + d_ref[...] * us_ref[...]
        y_ref[...] = y
        z, _ = _gelu_parts(y)
        gl = jnp.dot(z.astype(BF16), w_ref[...], preferred_element_type=F32) + b_ref[...]
        s_ref[...] = (z * _sig(gl)).astype(BF16)

    return _pcall(body, name="ssm_post", out_shape=(_sds((T, N), F32), _sds((T, N), BF16)), grid=(T // tm,),
                  in_specs=[row, row, vec, mat, vec], out_specs=(row, row), dims=("parallel",))(yc, us, dsk, wglu, bglu)


def _ssm_post_bwd(y5, us, ds, dsk, wglu, bglu):
    T, N = y5.shape
    tm = min(T, 1024)
    row = pl.BlockSpec((tm, N), lambda i: (i, 0))
    vec = pl.BlockSpec((1, N), lambda i: (0, 0))
    mat = pl.BlockSpec((N, N), lambda i: (0, 0))

    def body(y_ref, us_ref, ds_ref, d_ref, w_ref, b_ref, dy_ref, dd_ref, db_ref, dw_ref):
        @pl.when(pl.program_id(0) == 0)
        def _():
            dd_ref[...] = jnp.zeros_like(dd_ref)
            db_ref[...] = jnp.zeros_like(db_ref)
            dw_ref[...] = jnp.zeros_like(dw_ref)

        y = y_ref[...]
        z, t = _gelu_parts(y)
        zb = z.astype(BF16)
        gl = jnp.dot(zb, w_ref[...], preferred_element_type=F32) + b_ref[...]
        sg = _sig(gl)
        ds = ds_ref[...]
        dgl = ds * z * sg * (1.0 - sg)
        dglb = dgl.astype(BF16)
        dz = ds * sg + lax.dot_general(dglb, w_ref[...], (((1,), (1,)), ((), ())), preferred_element_type=F32)
        dgelu = 0.5 * (1.0 + t) + 0.5 * y * (1.0 - t * t) * GELU_K * (1.0 + 3.0 * GELU_C * y * y)
        dy = dz * dgelu
        dy_ref[...] = dy
        dd_ref[...] += jnp.sum(dy * us_ref[...], axis=0, keepdims=True)
        db_ref[...] += jnp.sum(dgl, axis=0, keepdims=True)
        dw_ref[...] += lax.dot_general(zb, dglb, (((0,), (0,)), ((), ())), preferred_element_type=F32)

    return _pcall(body, name="ssm_post_bwd",
                  out_shape=(_sds((T, N), F32), _sds((1, N), F32), _sds((1, N), F32), _sds((N, N), F32)),
                  grid=(T // tm,), in_specs=[row, row, row, vec, mat, vec], out_specs=(row, vec, vec, mat),
                  dims=("arbitrary",))(y5, us, ds, dsk, wglu, bglu)


def _add_scaled_cast(a, b, s):
    T, N = a.shape
    tm = min(T, 1024)
    row = pl.BlockSpec((tm, N), lambda i: (i, 0))

    def body(a_ref, b_ref, s_ref, o_ref):
        o_ref[...] = (a_ref[...] + s_ref[...] * b_ref[...]).astype(BF16)

    return _pcall(body, name="add_scaled_cast", out_shape=_sds((T, N), BF16), grid=(T // tm,),
                  in_specs=[row, row, pl.BlockSpec((1, N), lambda i: (0, 0))], out_specs=row, dims=("parallel",))(a, b, s)


GATE_TILE = 256
GATE_ATT_BLOCK0 = (3 * ATT_WIDTH + SSM_WIDTH) // GATE_TILE
GATE_SSM_BLOCK0 = (3 * ATT_WIDTH + SSM_WIDTH + D_MODEL) // GATE_TILE


def _merge(proj, y_att, y_ssm, b_gate):
    T = proj.shape[0]
    tm = min(T, 1024)
    nj = D_MODEL // GATE_TILE
    ga = pl.BlockSpec((tm, GATE_TILE), lambda i, j: (i, GATE_ATT_BLOCK0 + j))
    gs = pl.BlockSpec((tm, GATE_TILE), lambda i, j: (i, GATE_SSM_BLOCK0 + j))
    yy = pl.BlockSpec((tm, GATE_TILE), lambda i, j: (i, j))
    ba = pl.BlockSpec((1, GATE_TILE), lambda i, j: (0, j))
    bs = pl.BlockSpec((1, GATE_TILE), lambda i, j: (0, nj + j))

    def body(ga_ref, gs_ref, ya_ref, ys_ref, ba_ref, bs_ref, o_ref):
        o_ref[...] = (_sig(ga_ref[...] + ba_ref[...]) * ya_ref[...]
                      + _sig(gs_ref[...] + bs_ref[...]) * ys_ref[...]).astype(BF16)

    return _pcall(body, name="merge", out_shape=_sds((T, D_MODEL), BF16), grid=(T // tm, nj),
                  in_specs=[ga, gs, yy, yy, ba, bs], out_specs=yy, dims=("parallel", "parallel"))(
        proj, proj, y_att, y_ssm, b_gate, b_gate)


def _merge_bwd(proj, y_att, y_ssm, b_gate, dmerged):
    T = proj.shape[0]
    tm = min(T, 1024)
    nj = D_MODEL // GATE_TILE
    ga = pl.BlockSpec((tm, GATE_TILE), lambda j, i: (i, GATE_ATT_BLOCK0 + j))
    gs = pl.BlockSpec((tm, GATE_TILE), lambda j, i: (i, GATE_SSM_BLOCK0 + j))
    yy = pl.BlockSpec((tm, GATE_TILE), lambda j, i: (i, j))
    ba = pl.BlockSpec((1, GATE_TILE), lambda j, i: (0, j))
    bs = pl.BlockSpec((1, GATE_TILE), lambda j, i: (0, nj + j))

    def body(ga_ref, gs_ref, ya_ref, ys_ref, ba_ref, bs_ref, dm_ref, dya_ref, dys_ref, dga_ref, dgs_ref, dba_ref, dbs_ref):
        @pl.when(pl.program_id(1) == 0)
        def _():
            dba_ref[...] = jnp.zeros_like(dba_ref)
            dbs_ref[...] = jnp.zeros_like(dbs_ref)

        dm = dm_ref[...].astype(F32)
        sa = _sig(ga_ref[...] + ba_ref[...])
        ss = _sig(gs_ref[...] + bs_ref[...])
        dya_ref[...] = (dm * sa).astype(BF16)
        dys_ref[...] = (dm * ss).astype(BF16)
        dga = dm * ya_ref[...] * sa * (1.0 - sa)
        dgs = dm * ys_ref[...] * ss * (1.0 - ss)
        dga_ref[...] = dga.astype(BF16)
        dgs_ref[...] = dgs.astype(BF16)
        dba_ref[...] += jnp.sum(dga, axis=0, keepdims=True)
        dbs_ref[...] += jnp.sum(dgs, axis=0, keepdims=True)

    big = _sds((T, D_MODEL), BF16)
    vec = _sds((1, D_MODEL), F32)
    return _pcall(body, name="merge_bwd", out_shape=(big, big, big, big, vec, vec), grid=(nj, T // tm),
                  in_specs=[ga, gs, yy, yy, ba, bs, yy], out_specs=(yy, yy, yy, yy, ba, ba),
                  dims=("arbitrary", "arbitrary"))(proj, proj, y_att, y_ssm, b_gate, b_gate, dmerged)


CONV_TILE = 256


def _conv_pre(a, w_ref, b_ref, row):
    conv = b_ref[...] + w_ref[0:1, :] * a
    shifted = []
    for j in (1, 2):
        sh = jnp.where(row >= j, pltpu.roll(a, j, 0), 0.0)
        shifted.append(sh)
        conv = conv + w_ref[j:j + 1, :] * sh
    return conv, shifted


def _conv_act(up3, w_conv, b_conv):
    B, S, _ = up3.shape
    nj = D_FF // CONV_TILE
    a_spec = pl.BlockSpec((1, S, CONV_TILE), lambda b, j: (b, 0, j))
    v_spec = pl.BlockSpec((1, S, CONV_TILE), lambda b, j: (b, 0, nj + j))
    w_spec = pl.BlockSpec((3, CONV_TILE), lambda b, j: (0, j))
    b_spec = pl.BlockSpec((1, CONV_TILE), lambda b, j: (0, j))

    def body(a_ref, v_ref, w_ref, b_ref, o_ref):
        a = a_ref[0].astype(F32)
        row = lax.broadcasted_iota(jnp.int32, a.shape, 0)
        conv, _ = _conv_pre(a, w_ref, b_ref, row)
        o_ref[0] = (conv * _sig(conv) * v_ref[0]).astype(BF16)

    return _pcall(body, name="conv_act", out_shape=_sds((B, S, D_FF), BF16), grid=(B, nj),
                  in_specs=[a_spec, v_spec, w_spec, b_spec], out_specs=a_spec, dims=("parallel", "parallel"))(
        up3, up3, w_conv, b_conv)


def _conv_bwd(up3, dact3, w_conv, b_conv):
    B, S, _ = up3.shape
    nj = D_FF // CONV_TILE
    a_spec = pl.BlockSpec((1, S, CONV_TILE), lambda j, b: (b, 0, j))
    v_spec = pl.BlockSpec((1, S, CONV_TILE), lambda j, b: (b, 0, nj + j))
    o_spec = pl.BlockSpec((2, 1, S, CONV_TILE), lambda j, b: (0, b, 0, j))
    w_spec = pl.BlockSpec((3, CONV_TILE), lambda j, b: (0, j))
    b_spec = pl.BlockSpec((1, CONV_TILE), lambda j, b: (0, j))

    def body(a_ref, v_ref, d_ref, w_ref, b_ref, dup_ref, dw_ref, db_ref):
        @pl.when(pl.program_id(1) == 0)
        def _():
            dw_ref[...] = jnp.zeros_like(dw_ref)
            db_ref[...] = jnp.zeros_like(db_ref)

        a = a_ref[0].astype(F32)
        d = d_ref[0].astype(F32)
        row = lax.broadcasted_iota(jnp.int32, a.shape, 0)
        conv, shifted = _conv_pre(a, w_ref, b_ref, row)
        sg = _sig(conv)
        dup_ref[1, 0] = (d * conv * sg).astype(BF16)
        dconv = d * v_ref[0] * (sg * (1.0 + conv * (1.0 - sg)))
        da = w_ref[0:1, :] * dconv
        for j in (1, 2):
            da = da + w_ref[j:j + 1, :] * jnp.where(row < S - j, pltpu.roll(dconv, S - j, 0), 0.0)
        dup_ref[0, 0] = da.astype(BF16)
        db_ref[...] += jnp.sum(dconv, axis=0, keepdims=True)
        dw_ref[0:1, :] += jnp.sum(dconv * a, axis=0, keepdims=True)
        dw_ref[1:2, :] += jnp.sum(dconv * shifted[0], axis=0, keepdims=True)
        dw_ref[2:3, :] += jnp.sum(dconv * shifted[1], axis=0, keepdims=True)

    return _pcall(body, name="conv_bwd",
                  out_shape=(_sds((2, B, S, D_FF), BF16), _sds((3, D_FF), F32), _sds((1, D_FF), F32)),
                  grid=(nj, B), in_specs=[a_spec, v_spec, a_spec, w_spec, b_spec],
                  out_specs=(o_spec, w_spec, b_spec), dims=("arbitrary", "arbitrary"))(up3, up3, dact3, w_conv, b_conv)


def _rows_tile(r, cap=640):
    for t in range(min(r, cap) - min(r, cap) % 8, 7, -8):
        if r % t == 0:
            return t
    return r


def _add2(a, b, out_dtype, name):
    R, N = a.shape
    tr = _rows_tile(R)
    spec = pl.BlockSpec((tr, N), lambda i: (i, 0))

    def body(a_ref, b_ref, o_ref):
        o_ref[...] = (a_ref[...] + b_ref[...]).astype(out_dtype)

    return _pcall(body, name=name, out_shape=_sds((R, N), out_dtype), grid=(R // tr,), in_specs=[spec, spec],
                  out_specs=spec, dims=("parallel",))(a, b)


def _sum_slots(q, name):
    n, R, N = q.shape
    tr = _rows_tile(R)

    def body(q_ref, o_ref):
        acc = q_ref[0].astype(F32)
        for s in range(1, n):
            acc = acc + q_ref[s].astype(F32)
        o_ref[...] = acc

    return _pcall(body, name=name, out_shape=_sds((R, N), F32), grid=(R // tr,),
                  in_specs=[pl.BlockSpec((n, tr, N), lambda i: (0, i, 0))], out_specs=pl.BlockSpec((tr, N), lambda i: (i, 0)),
                  dims=("parallel",))(q)


NATIVE = (("b_re", 16, 1024), ("b_im", 16, 1024), ("c_re", 16, 1024), ("c_im", 16, 1024), ("g_mix", 1, 1024),
          ("b_att", 1, 1024), ("b_ssm", 1, 1024), ("a_re", 1, 1024), ("a_im", 1, 1024), ("log_dt", 1, 128),
          ("d_skip", 1, 256), ("b_glu", 1, 256), ("g_ffn", 1, 1024), ("g_final", 1, 1024), ("b_conv", 1, 2048),
          ("w_conv", 3, 2048), ("loss", 1, 1))
N_MOD = 6


def _native_rows():
    starts, r = {}, 0
    for name, rows, cols in NATIVE:
        starts[name] = r
        r += rows * (-(-cols // LANES))
    n_sum = -(-r // 8) * 8
    return starts, n_sum


def _pack_small(native, dmods):
    starts, n_sum = _native_rows()
    B = dmods[0].shape[0]
    total = n_sum + 8 * N_MOD

    def body(*refs):
        xs, ms, o_ref = refs[:len(NATIVE)], refs[len(NATIVE):len(NATIVE) + N_MOD], refs[-1]
        o_ref[...] = jnp.zeros_like(o_ref)
        for (name, rows, cols), x_ref in zip(NATIVE, xs):
            chunks = -(-cols // LANES)
            if chunks == 1 and rows % 8 == 0:
                o_ref[starts[name]:starts[name] + rows, 0:cols] = x_ref[...]
                continue
            for i in range(rows):
                for q in range(chunks):
                    wd = min(LANES, cols - q * LANES)
                    r = starts[name] + i * chunks + q
                    o_ref[r:r + 1, 0:wd] = x_ref[i:i + 1, q * LANES:q * LANES + wd]
        for k, m_ref in enumerate(ms):
            for b in range(B):
                o_ref[n_sum + 8 * k + b:n_sum + 8 * k + b + 1, :] = m_ref[b]

    return _pcall(body, name="pack_small", out_shape=_sds((total, LANES), F32))(
        *[native[n] for n, _, _ in NATIVE], *dmods)


def _sum_unpack_small(gathered, B):
    starts, n_sum = _native_rows()
    nd = gathered.shape[0]

    def body(*refs):
        g_ref, outs, dm_ref, acc = refs[0], refs[1:1 + len(NATIVE)], refs[1 + len(NATIVE)], refs[-1]
        s = g_ref[0, 0:n_sum, :]
        for d in range(1, nd):
            s = s + g_ref[d, 0:n_sum, :]
        acc[...] = s
        for (name, rows, cols), o_ref in zip(NATIVE, outs):
            chunks = -(-cols // LANES)
            if chunks == 1 and rows % 8 == 0:
                o_ref[...] = acc[starts[name]:starts[name] + rows, 0:cols]
                continue
            for i in range(rows):
                for q in range(chunks):
                    wd = min(LANES, cols - q * LANES)
                    r = starts[name] + i * chunks + q
                    o_ref[i:i + 1, q * LANES:q * LANES + wd] = acc[r:r + 1, 0:wd]
        for d in range(nd):
            for k in range(N_MOD):
                dm_ref[d, :, k * D_MODEL:(k + 1) * D_MODEL] = g_ref[d, n_sum + 8 * k:n_sum + 8 * k + B, :]

    out_shape = tuple(_sds((rows, cols), F32) for _, rows, cols in NATIVE) + (_sds((nd, B, N_MOD * D_MODEL), F32),)
    res = _pcall(body, name="sum_unpack_small", out_shape=out_shape,
                 scratch_shapes=[pltpu.VMEM((n_sum, LANES), F32)])(gathered)
    return {n: r for (n, _, _), r in zip(NATIVE, res[:-1])}, res[-1]


def _small_from_native(nat):
    lanes3 = lambda a: a.reshape(SSM_GROUP_CH, SSM_GROUPS, SSM_STATE)
    return dict(
        g_mix=nat["g_mix"].reshape(D_MODEL), b_gate=jnp.concatenate([nat["b_att"], nat["b_ssm"]], axis=1).reshape(2 * D_MODEL),
        a_re=nat["a_re"].reshape(SSM_GROUPS, SSM_STATE), a_im=nat["a_im"].reshape(SSM_GROUPS, SSM_STATE),
        log_dt=nat["log_dt"][0, :SSM_GROUPS], b_re=_groups_from_lanes(nat["b_re"]), b_im=_groups_from_lanes(nat["b_im"]),
        c_re=lanes3(nat["c_re"]).transpose(1, 0, 2), c_im=lanes3(nat["c_im"]).transpose(1, 0, 2),
        d_skip=nat["d_skip"].reshape(SSM_WIDTH), b_glu=nat["b_glu"].reshape(SSM_WIDTH), g_ffn=nat["g_ffn"].reshape(D_MODEL),
        w_conv=nat["w_conv"], b_conv=nat["b_conv"].reshape(D_FF), g_final=nat["g_final"].reshape(D_MODEL))


def _adamw_multi(params):
    n = len(params)
    bc1 = 1.0 - ADAM_B1 ** ADAM_STEP
    bc2 = 1.0 - ADAM_B2 ** ADAM_STEP

    def body(*refs):
        ins, outs = refs[:4 * n], refs[4 * n:]
        for i in range(n):
            w_ref, g_ref, m_ref, v_ref = ins[4 * i:4 * i + 4]
            d_ref, nm_ref, nv_ref = outs[3 * i:3 * i + 3]
            g = g_ref[...]
            m = ADAM_B1 * m_ref[...] + (1.0 - ADAM_B1) * g
            v = ADAM_B2 * v_ref[...] + (1.0 - ADAM_B2) * (g * g)
            nm_ref[...] = m
            nv_ref[...] = v
            d_ref[...] = -ADAM_LR * ((m / bc1) / (jnp.sqrt(v / bc2) + ADAM_EPS) + ADAM_WD * w_ref[...])

    flat = [a for p in params for a in p]
    out_shape = tuple(_sds(p[0].shape, F32) for p in params for _ in range(3))
    res = _pcall(body, name="adamw_small", out_shape=out_shape)(*flat)
    return [tuple(res[3 * i:3 * i + 3]) for i in range(n)]


def _adamw(w, g, m, v, name):
    R, N = w.shape
    tr = _rows_tile(R) if R * N * 4 > (1 << 20) else R
    tr = min(tr, 256) if R % 256 == 0 and R > 256 else tr
    spec = pl.BlockSpec((tr, N), lambda i: (i, 0))
    bc1 = 1.0 - ADAM_B1 ** ADAM_STEP
    bc2 = 1.0 - ADAM_B2 ** ADAM_STEP

    def body(w_ref, g_ref, m_ref, v_ref, d_ref, nm_ref, nv_ref):
        g = g_ref[...]
        m = ADAM_B1 * m_ref[...] + (1.0 - ADAM_B1) * g
        v = ADAM_B2 * v_ref[...] + (1.0 - ADAM_B2) * (g * g)
        nm_ref[...] = m
        nv_ref[...] = v
        d_ref[...] = -ADAM_LR * ((m / bc1) / (jnp.sqrt(v / bc2) + ADAM_EPS) + ADAM_WD * w_ref[...])

    shp = _sds((R, N), F32)
    return _pcall(body, name=name, out_shape=(shp, shp, shp), grid=(R // tr,), in_specs=[spec] * 4,
                  out_specs=(spec, spec, spec), dims=("parallel",))(w, g, m, v)


_GROUP_MASKS = {
    "all": [(dx, dy, dc) for dx in (0, 1) for dy in (0, 1) for dc in (0, 1) if (dx, dy, dc) != (0, 0, 0)],
    "xy": [(1, 0, 0), (0, 1, 0), (1, 1, 0)],
    "c": [(0, 0, 1)],
}
_GROUP_SLOTS = {"all": 8, "xy": 4, "c": 2}


def _group_slot(group, x, y, c):
    return {"all": 4 * x + 2 * y + c, "xy": 2 * x + y, "c": c}[group]


def _flip(v, d):
    return 1 - v if d else v


def _exchange(arr, group, mode, name):
    return _exchange_list([arr], group, mode, name)[0]


def _exchange_list(arrs, group, mode, name):
    masks = _GROUP_MASKS[group]
    n = len(masks)
    na = len(arrs)
    out_shapes, halves, bounce = [], [], []
    for arr in arrs:
        if mode == "gather":
            out_shapes.append((_GROUP_SLOTS[group],) + arr.shape)
            bounce.append(pltpu.VMEM(arr.shape, arr.dtype))
        elif mode == "scatter":
            assert arr.shape[0] == _GROUP_SLOTS[group]
            out_shapes.append(arr.shape)
            bounce.append(pltpu.VMEM(arr.shape[1:], arr.dtype))
        elif mode == "swap":
            assert group == "c"
            out_shapes.append(arr.shape)
        else:
            assert group == "c"
            halves.append(arr.shape[1] // 2)
            out_shapes.append((arr.shape[0], arr.shape[1] // 2, arr.shape[2]))
    has_local = mode in ("gather", "scatter")

    def body(*refs):
        x_refs, o_refs = refs[:na], refs[na:2 * na]
        send_sems, recv_sems = refs[2 * na], refs[2 * na + 1]
        x, y, c = lax.axis_index("x"), lax.axis_index("y"), lax.axis_index("c")
        me = _group_slot(group, x, y, c)
        if has_local:
            local_sems = refs[2 * na + 2]
            bufs = refs[2 * na + 3:]
            loads = []
            for i in range(na):
                src = x_refs[i] if mode == "gather" else x_refs[i].at[me]
                loads.append(pltpu.make_async_copy(src, bufs[i], local_sems.at[2 * i]))
                loads[-1].start()
        copies = []
        for i in range(na):
            x_ref, o_ref = x_refs[i], o_refs[i]
            for k, (dx, dy, dc) in enumerate(masks):
                px, py, pc = _flip(x, dx), _flip(y, dy), _flip(c, dc)
                if mode == "gather":
                    src, dst = x_ref, o_ref.at[me]
                elif mode == "scatter":
                    src, dst = x_ref.at[_group_slot(group, px, py, pc)], o_ref.at[me]
                elif mode == "swap":
                    src, dst = x_ref, o_ref
                else:
                    src, dst = x_ref.at[:, pl.ds(pl.multiple_of(pc * halves[i], 8), halves[i]), :], o_ref
                cp = pltpu.make_async_remote_copy(src_ref=src, dst_ref=dst, send_sem=send_sems.at[i * n + k],
                                                  recv_sem=recv_sems.at[i * n + k], device_id=(px, py, pc),
                                                  device_id_type=pl.DeviceIdType.MESH)
                cp.start()
                copies.append(cp)
        if has_local:
            stores = []
            for i in range(na):
                loads[i].wait()
                stores.append(pltpu.make_async_copy(bufs[i], o_refs[i].at[me], local_sems.at[2 * i + 1]))
                stores[-1].start()
        for cp in copies:
            cp.wait()
        if has_local:
            for st in stores:
                st.wait()

    anyspec = pl.BlockSpec(memory_space=pl.ANY)
    scratch = [pltpu.SemaphoreType.DMA((n * na,)), pltpu.SemaphoreType.DMA((n * na,))]
    if has_local:
        scratch += [pltpu.SemaphoreType.DMA((2 * na,))] + bounce
    outs = pl.pallas_call(body, name=name, out_shape=tuple(_sds(s, a.dtype) for s, a in zip(out_shapes, arrs)),
                          in_specs=[anyspec] * na, out_specs=tuple([anyspec] * na), scratch_shapes=scratch,
                          compiler_params=pltpu.CompilerParams(vmem_limit_bytes=V7X_VMEM_LIMIT_BYTES))(*arrs)
    return list(outs)


def _gather_weights(shards, name):
    na = len(shards)
    masks = _GROUP_MASKS["xy"]
    n = len(masks)

    def body(*refs):
        x_refs, o_refs = refs[:na], refs[na:2 * na]
        send_sems, recv_sems, local_sems = refs[2 * na:2 * na + 3]
        bufs = refs[2 * na + 3:]
        x, y, c = lax.axis_index("x"), lax.axis_index("y"), lax.axis_index("c")
        me = 2 * x + y
        sibling = (x, y, 1 - c)
        loads = []
        for i in range(na):
            loads.append(pltpu.make_async_copy(x_refs[i], bufs[i], local_sems.at[2 * i]))
            loads[-1].start()

        def half_of(i, slot, cc):
            h = shards[i].shape[0] // 2
            return o_refs[i].at[slot, pl.ds(pl.multiple_of(cc * h, 8), h), :]

        def src_half(i, cc):
            h = shards[i].shape[0] // 2
            return x_refs[i].at[pl.ds(pl.multiple_of(cc * h, 8), h), :]

        sends = []
        for i in range(na):
            for k, (dx, dy, _) in enumerate(masks):
                cp = pltpu.make_async_remote_copy(src_ref=src_half(i, c), dst_ref=half_of(i, me, c),
                                                  send_sem=send_sems.at[i * 2 * n + k], recv_sem=recv_sems.at[i * 2 * n + k],
                                                  device_id=(_flip(x, dx), _flip(y, dy), c),
                                                  device_id_type=pl.DeviceIdType.MESH)
                cp.start()
                sends.append(cp)
        stores = []
        for i in range(na):
            loads[i].wait()
            stores.append(pltpu.make_async_copy(bufs[i], o_refs[i].at[me], local_sems.at[2 * i + 1]))
            stores[-1].start()
        for i in range(na):
            for k, (dx, dy, _) in enumerate(masks):
                slot = 2 * _flip(x, dx) + _flip(y, dy)
                landed = pltpu.make_async_remote_copy(src_ref=src_half(i, c), dst_ref=half_of(i, slot, c),
                                                      send_sem=send_sems.at[i * 2 * n + k],
                                                      recv_sem=recv_sems.at[i * 2 * n + k], device_id=sibling,
                                                      device_id_type=pl.DeviceIdType.MESH)
                landed.wait_recv()
                fwd = pltpu.make_async_remote_copy(src_ref=half_of(i, slot, c), dst_ref=half_of(i, slot, c),
                                                   send_sem=send_sems.at[i * 2 * n + n + k],
                                                   recv_sem=recv_sems.at[i * 2 * n + n + k], device_id=sibling,
                                                   device_id_type=pl.DeviceIdType.MESH)
                fwd.start()
                sends.append(fwd)
        for i in range(na):
            for k, (dx, dy, _) in enumerate(masks):
                slot = 2 * _flip(x, dx) + _flip(y, dy)
                pltpu.make_async_remote_copy(src_ref=half_of(i, slot, 1 - c), dst_ref=half_of(i, slot, 1 - c),
                                             send_sem=send_sems.at[i * 2 * n + n + k],
                                             recv_sem=recv_sems.at[i * 2 * n + n + k], device_id=sibling,
                                             device_id_type=pl.DeviceIdType.MESH).wait_recv()
        for cp in sends:
            cp.wait_send()
        for st in stores:
            st.wait()

    anyspec = pl.BlockSpec(memory_space=pl.ANY)
    scratch = [pltpu.SemaphoreType.DMA((2 * n * na,)), pltpu.SemaphoreType.DMA((2 * n * na,)),
               pltpu.SemaphoreType.DMA((2 * na,))] + [pltpu.VMEM(s.shape, s.dtype) for s in shards]
    outs = pl.pallas_call(body, name=name, out_shape=tuple(_sds((N_XY,) + s.shape, s.dtype) for s in shards),
                          in_specs=[anyspec] * na, out_specs=tuple([anyspec] * na), scratch_shapes=scratch,
                          compiler_params=pltpu.CompilerParams(vmem_limit_bytes=V7X_VMEM_LIMIT_BYTES))(*shards)
    return list(outs)


def _pair_add(g, theirs, core, name):
    n4, h2, w = g.shape
    h = h2 // 2
    tr = _rows_tile(h)
    nb = h // tr

    def body(c_ref, g_ref, t_ref, o_ref):
        o_ref[...] = (g_ref[...] + t_ref[...]).astype(BF16)

    grid_spec = pltpu.PrefetchScalarGridSpec(
        num_scalar_prefetch=1, grid=(n4, nb),
        in_specs=[pl.BlockSpec((None, tr, w), lambda j, i, c_ref: (j, c_ref[0] * nb + i, 0)),
                  pl.BlockSpec((None, tr, w), lambda j, i, c_ref: (j, i, 0))],
        out_specs=pl.BlockSpec((None, tr, w), lambda j, i, c_ref: (j, i, 0)))
    return pl.pallas_call(body, name=name, out_shape=_sds((n4, h, w), BF16), grid_spec=grid_spec,
                          compiler_params=pltpu.CompilerParams(vmem_limit_bytes=V7X_VMEM_LIMIT_BYTES,
                                                               dimension_semantics=("parallel", "parallel")))(core, g, theirs)


BIG = (("w_proj_att", (ATT_WIDTH, D_MODEL), 1), ("w_proj_ssm", (SSM_WIDTH, D_MODEL), 1),
       ("w_glu", (SSM_WIDTH, SSM_WIDTH), 0))
DIRECT = (("w_in", True), ("w_up", True), ("w_down", False), ("w_out", False))
N_XY = 4


def _big_rows(shape):
    return shape[0] * shape[1] // N_XY // LANES


FLAT_ROWS = sum(_big_rows(s) for _, s, _ in BIG)


def _shard_shape(shape, axis):
    return (shape[0] // N_XY, shape[1]) if axis == 0 else (shape[0], shape[1] // N_XY)


def _flatten_shards(shards):
    return jnp.concatenate([shards[n].reshape(_big_rows(s), LANES) for n, s, _ in BIG], axis=0)


def _unflatten_shard(flat):
    out, r = {}, 0
    for n, s, ax in BIG:
        k = _big_rows(s)
        out[n] = flat[r:r + k].reshape(_shard_shape(s, ax))
        r += k
    return out


def _unflatten_full(flat4):
    out, r = {}, 0
    for n, s, ax in BIG:
        k = _big_rows(s)
        sh = _shard_shape(s, ax)
        t = flat4[:, r:r + k].reshape((N_XY,) + sh)
        out[n] = t.reshape(s) if ax == 0 else t.transpose(1, 0, 2).reshape(s)
        r += k
    return out


def _flatten_full(full):
    parts = []
    for n, s, ax in BIG:
        sh = _shard_shape(s, ax)
        t = full[n]
        t = t.reshape((N_XY,) + sh) if ax == 0 else t.reshape(s[0], N_XY, sh[1]).transpose(1, 0, 2)
        parts.append(t.reshape(N_XY, _big_rows(s), LANES))
    return jnp.concatenate(parts, axis=1)


def _pack_rows(arrs):
    rows, counts = [], []
    for a in arrs:
        f = a.reshape(-1)
        k = -(-f.shape[0] // LANES)
        rows.append(jnp.pad(f, (0, k * LANES - f.shape[0])).reshape(k, LANES))
        counts.append(k)
    return jnp.concatenate(rows, axis=0), counts


def _unpack_rows(buf, shapes):
    out, r = [], 0
    for s in shapes:
        size = int(np.prod(s))
        k = -(-size // LANES)
        out.append(buf[r:r + k].reshape(-1)[:size].reshape(s))
        r += k
    return out


def _lanes_from_groups(a):
    return a.transpose(2, 0, 1).reshape(SSM_GROUP_CH, SSM_LANES)


def _groups_from_lanes(a):
    return a.reshape(SSM_GROUP_CH, SSM_GROUPS, SSM_STATE).transpose(1, 2, 0)


LATE = ("w_up_t", "w_down", "w_out")
EARLY_GRADS = ("w_up_t", "w_down", "w_out")


def _local_step(x3, mod, tgt3, W, P, late_shards=None, scatter_grads=False):
    B, S, _ = x3.shape
    T = B * S
    seq_blocks = S // ATT_BLOCK
    sh1, sc1, gt1, sh2, sc2, gt2 = [m.reshape(B, 1, D_MODEL) for m in jnp.split(mod, 6, axis=-1)]
    g_mix, g_ffn, g_final = P["g_mix"].reshape(1, D_MODEL), P["g_ffn"].reshape(1, D_MODEL), P["g_final"].reshape(1, D_MODEL)
    b_gate = P["b_gate"].reshape(1, 2 * D_MODEL)
    d_skip, b_glu = P["d_skip"].reshape(1, SSM_WIDTH), P["b_glu"].reshape(1, SSM_WIDTH)
    w_conv, b_conv = P["w_conv"], P["b_conv"].reshape(1, D_FF)

    u1 = _norm_mod(x3, g_mix, sc1, sh1).reshape(T, D_MODEL)
    proj = _mm(u1, W["w_in_t"], tb=True, name="mm_proj", out_dtype=BF16)
    proj3 = proj.reshape(B, S, IN_WIDTH)
    us = proj[:, 3 * ATT_WIDTH:3 * ATT_WIDTH + SSM_WIDTH]
    o_att3, lse4, late = _attention_fwd(proj3, seq_blocks, _Riders(late_shards, "gather") if late_shards else None)
    if late_shards:
        W = dict(W, **{n: f.reshape(-1, LANES) for n, f in zip(LATE, late)})
        w_conv = late[len(LATE)].transpose(1, 0, 2).reshape(3, D_FF)
    o_att = o_att3.reshape(T, ATT_WIDTH)
    y_att = _mm(o_att, W["w_proj_att"], name="mm_proj_att", out_dtype=BF16)

    lr = P["a_re"].reshape(1, SSM_LANES)
    li = P["a_im"].reshape(1, SSM_LANES)
    ldt = jnp.repeat(P["log_dt"], SSM_STATE).reshape(1, SSM_LANES)
    br, bi = _lanes_from_groups(P["b_re"]), _lanes_from_groups(P["b_im"])
    cr = P["c_re"].transpose(1, 0, 2).reshape(SSM_GROUP_CH, SSM_LANES)
    ci = P["c_im"].transpose(1, 0, 2).reshape(SSM_GROUP_CH, SSM_LANES)
    abar, w_bu, w_c = _ssm_params(lr, li, ldt, br, bi, cr, ci)
    xs3, y_core3 = _ssm_scan_fwd(proj3, abar, w_bu, w_c)
    y5, s_out = _ssm_post(y_core3.reshape(T, SSM_WIDTH), us, d_skip, W["w_glu"], b_glu)
    y_ssm = _mm(s_out, W["w_proj_ssm"], name="mm_proj_ssm", out_dtype=BF16)

    merged = _merge(proj, y_att, y_ssm, b_gate)
    mix = _mm(merged, W["w_out"], name="mm_out", out_dtype=BF16)
    mix3 = mix.reshape(B, S, D_MODEL)

    h1, u2 = _resid_norm_mod(x3, mix3, gt1, g_ffn, sc2, sh2)
    u2 = u2.reshape(T, D_MODEL)
    up3 = _mm(u2, W["w_up_t"], tb=True, name="mm_up", out_dtype=BF16).reshape(B, S, 2 * D_FF)
    act = _conv_act(up3, w_conv, b_conv).reshape(T, D_FF)
    ffn3 = _mm(act, W["w_down"], name="mm_down", out_dtype=BF16).reshape(B, S, D_MODEL)
    dh2, dffn, dgt2, dg_final, loss = _final_loss(h1, ffn3, tgt3, gt2, g_final)

    dffn = dffn.reshape(T, D_MODEL)
    gw = {}
    gw["w_down"] = _mm(act, dffn, ta=True, out_dtype=BF16, name="mm_dw_down")
    dact3 = _mm(dffn, W["w_down"], tb=True, name="mm_dact", out_dtype=BF16).reshape(B, S, D_FF)
    dup3, dw_conv, db_conv = _conv_bwd(up3, dact3, w_conv, b_conv)
    dup = dup3.reshape(2, T, D_FF)
    gw["w_up_t"] = _mm(dup, u2, ta=True, out_dtype=BF16, name="mm_dw_up")
    du2 = _mm(dup, W["w_up_t"], name="mm_du2", out_dtype=BF16).reshape(B, S, D_MODEL)
    dh1, dsh2, dsc2, dg_ffn, dgt1, dmix = _norm_bwd(h1, du2, dh2, g_ffn, sc2, "norm_bwd2", mix3=mix3, gt=gt1)

    dmix = dmix.reshape(T, D_MODEL)
    gw["w_out"] = _mm(merged, dmix, ta=True, out_dtype=BF16, name="mm_dw_out")
    dmerged = _mm(dmix, W["w_out"], tb=True, name="mm_dmerged", out_dtype=BF16)
    dy_att, dy_ssm, dga, dgs, db_att, db_ssm = _merge_bwd(proj, y_att, y_ssm, b_gate, dmerged)

    gw["w_proj_ssm"] = _mm(s_out, dy_ssm, ta=True, name="mm_dw_proj_ssm")
    ds_out = _mm(dy_ssm, W["w_proj_ssm"], tb=True, name="mm_ds_out")
    dy5, dd_skip, db_glu, dw_glu = _ssm_post_bwd(y5, us, ds_out, d_skip, W["w_glu"], b_glu)
    gw["w_glu"] = dw_glu
    dus3, dab, dwbu, dwc = _ssm_scan_bwd(proj3, dy5.reshape(B, S, SSM_WIDTH), xs3, abar, w_bu, w_c, d_skip)
    dus = dus3.reshape(T, SSM_WIDTH)
    dlr, dli, dldt, dbr, dbi, dcr, dci = _ssm_params_bwd(lr, li, ldt, br, bi, dab, dwbu, dwc)

    gw["w_proj_att"] = _mm(o_att, dy_att, ta=True, name="mm_dw_proj_att")
    do_att = _mm(dy_att, W["w_proj_att"], tb=True, out_dtype=BF16, name="mm_do_att")
    early = [gw[n].reshape(N_XY, -1, LANES) for n in EARLY_GRADS]
    early.append(_flatten_full({n: gw[n].astype(BF16) for n, _, _ in BIG}))
    dq3, dk3, dv3, parts = _attention_bwd(proj3, do_att.reshape(B, S, ATT_WIDTH), o_att3, lse4, seq_blocks,
                                          _Riders(early, "scatter") if scatter_grads else None)
    dproj = jnp.concatenate([t.reshape(T, ATT_WIDTH) for t in (dq3, dk3, dv3)] + [dus, dga, dgs], axis=1)
    gw["w_in_t"] = _mm(dproj, u1, ta=True, out_dtype=BF16, name="mm_dw_in")
    if scatter_grads:
        du1, last_parts = _mm(dproj, W["w_in_t"], name="mm_du1", out_dtype=BF16,
                              riders=_Riders([gw["w_in_t"].reshape(N_XY, -1, LANES)], "scatter"))
        parts = parts + last_parts
    else:
        du1 = _mm(dproj, W["w_in_t"], name="mm_du1", out_dtype=BF16)
    du1 = du1.reshape(B, S, D_MODEL)
    dx, dsh1, dsc1, dg_mix = _norm_bwd(x3, du1, dh1, g_mix, sc1, "norm_bwd1")

    dmods = [dsh1, dsc1, dgt1, dsh2, dsc2, dgt2]
    native = dict(g_mix=dg_mix, b_att=db_att, b_ssm=db_ssm, a_re=dlr, a_im=dli, log_dt=dldt, b_re=dbr, b_im=dbi, c_re=dcr,
                  c_im=dci, d_skip=dd_skip, b_glu=db_glu, g_ffn=dg_ffn, w_conv=dw_conv, b_conv=db_conv, g_final=dg_final)
    return loss, dx, dmods, gw, native, parts


WEIGHTS = ['w_ada', 'b_ada', 'g_mix', 'w_in', 'b_gate', 'a_re', 'a_im', 'log_dt', 'b_re', 'b_im', 'c_re', 'c_im', 'd_skip',
           'w_glu', 'b_glu', 'w_proj_att', 'w_proj_ssm', 'w_out', 'g_ffn', 'w_up', 'w_conv', 'b_conv', 'w_down', 'g_final']
SMALL = ['g_mix', 'b_gate', 'a_re', 'a_im', 'log_dt', 'b_re', 'b_im', 'c_re', 'c_im', 'd_skip', 'b_glu', 'g_ffn', 'w_conv',
         'b_conv', 'g_final']


def kernel(x, c, w_ada, b_ada, g_mix, w_in, b_gate, a_re, a_im, log_dt, b_re, b_im, c_re, c_im, d_skip, w_glu, b_glu, w_proj_att, w_proj_ssm, w_out, g_ffn, w_up, w_conv, b_conv, w_down, g_final, loss_target, m_w_ada, m_b_ada, m_g_mix, m_w_in, m_b_gate, m_a_re, m_a_im, m_log_dt, m_b_re, m_b_im, m_c_re, m_c_im, m_d_skip, m_w_glu, m_b_glu, m_w_proj_att, m_w_proj_ssm, m_w_out, m_g_ffn, m_w_up, m_w_conv, m_b_conv, m_w_down, m_g_final, v_w_ada, v_b_ada, v_g_mix, v_w_in, v_b_gate, v_a_re, v_a_im, v_log_dt, v_b_re, v_b_im, v_c_re, v_c_im, v_d_skip, v_w_glu, v_b_glu, v_w_proj_att, v_w_proj_ssm, v_w_out, v_g_ffn, v_w_up, v_w_conv, v_b_conv, v_w_down, v_g_final):
    args = dict(locals())
    w = {n: args[n] for n in WEIGHTS}
    m = {n: args["m_" + n] for n in WEIGHTS}
    v = {n: args["v_" + n] for n in WEIGHTS}
    B, S, _ = x.shape
    ix, iy, ic = lax.axis_index("x"), lax.axis_index("y"), lax.axis_index("c")
    chip = 2 * ix + iy
    half = FLAT_ROWS // 2
    ada_cols = w_ada.shape[2]

    c_all = _exchange(c, "all", "gather", "gather_c").reshape(8 * B, D_MODEL)
    b_cols = lax.dynamic_slice_in_dim(b_ada, chip * ada_cols, ada_cols, axis=1)
    mod_cols = _ada_fwd(c_all, w_ada[0], b_cols)
    mod_all = _exchange(mod_cols, "xy", "gather", "gather_mod")
    mod_all = mod_all.transpose(1, 0, 2).reshape(8 * B, 6 * D_MODEL)
    mod = lax.dynamic_slice_in_dim(mod_all, (4 * ix + 2 * iy + ic) * B, B, axis=0)

    south = ic == 0
    core = ic.astype(jnp.int32).reshape(1)
    shard = {n + ("_t" if t else ""): (w[n][0].T if t else w[n][0]).astype(BF16) for n, t in DIRECT}
    misc = _flatten_shards({n: w[n][0] for n, _, _ in BIG}).astype(BF16)
    w_in_full, misc_full = _gather_weights([shard["w_in_t"], misc], "gather_weights")
    W = {"w_in_t": w_in_full.reshape(-1, LANES)}
    W.update(_unflatten_full(misc_full))

    P = {n: w[n][0] for n in SMALL if n not in ("w_conv", "g_final")}
    P["w_conv"] = None
    P["g_final"] = g_final

    loss, dx, dmods, gw, native, parts = _local_step(x, mod, loss_target, W, P, [shard[n] for n in LATE] + [w_conv[0]],
                                                     True)

    native["loss"] = loss
    gathered = _exchange(_pack_small(native, dmods), "all", "gather", "gather_small")
    native_sum, dmod_all = _sum_unpack_small(gathered, B)
    loss = native_sum["loss"][0, 0]
    g_small = _small_from_native(native_sum)
    dmod_all = dmod_all.reshape(8 * B, N_MOD * D_MODEL)
    dmod_cols = lax.dynamic_slice_in_dim(dmod_all, chip * ada_cols, ada_cols, axis=1)
    g_w_ada, g_b_ada = _ada_bwd(c_all, dmod_all, dmod_cols)

    red = [_sum_slots(p, "sum_chips_%d" % i) for i, p in enumerate(parts)]
    red_sib = _exchange_list(red, "c", "swap", "share_cores")
    reduced = [_add2(r, s, F32, "add_cores_%d" % i) for i, (r, s) in enumerate(zip(red, red_sib))]

    grads = {"w_ada": g_w_ada[None], "b_ada": g_b_ada}
    order = list(EARLY_GRADS) + ["misc", "w_in_t"]
    for n, g in zip(order, reduced):
        if n == "misc":
            for k, gk in _unflatten_shard(g).items():
                grads[k] = gk[None]
        else:
            grads[n[:-2] if n.endswith("_t") else n] = (g.T if n.endswith("_t") else g)[None]
    wc_cols = w_conv.shape[2]
    for n in SMALL:
        g = g_small[n]
        if n == "w_conv":
            g = lax.dynamic_slice_in_dim(g, chip * wc_cols, wc_cols, axis=1)
        grads[n] = g.reshape(w[n].shape)

    delta, new_m, new_v = {}, {}, {}
    for n in ["w_ada"] + [b for b, _ in DIRECT] + [b for b, _, _ in BIG]:
        shp = w[n].shape
        d2, m2, v2 = _adamw(w[n][0], grads[n][0], m[n][0], v[n][0], "adamw_" + n)
        delta[n], new_m[n], new_v[n] = d2.reshape(shp), m2.reshape(shp), v2.reshape(shp)
    rest = ["b_ada"] + SMALL

    def drop(a):
        return a.reshape(1, -1) if a.ndim == 1 else (a if a.ndim == 2 else a[0])

    upd = _adamw_multi([(drop(w[n]), drop(grads[n]), drop(m[n]), drop(v[n])) for n in rest])
    for n, (dd, mm, vv) in zip(rest, upd):
        delta[n], new_m[n], new_v[n] = dd.reshape(w[n].shape), mm.reshape(w[n].shape), vv.reshape(w[n].shape)

    return (loss, dx, *[grads[n] for n in WEIGHTS], *[delta[n] for n in WEIGHTS], *[new_m[n] for n in WEIGHTS],
            *[new_v[n] for n in WEIGHTS])
```

```python
import functools
import math

import numpy as np
import jax
import jax.numpy as jnp
from jax import lax
from jax.experimental import pallas as pl
from jax.experimental.pallas import tpu as pltpu

F32, BF16 = jnp.float32, jnp.bfloat16

D_MODEL = 1024
N_HEADS = 8
HEAD_DIM = 64
ATT_WIDTH = 512
SSM_GROUPS = 16
SSM_GROUP_CH = 16
SSM_WIDTH = 256
SSM_STATE = 64
SSM_LANES = SSM_GROUPS * SSM_STATE
D_FF = 2048
IN_WIDTH = 3 * ATT_WIDTH + SSM_WIDTH + 2 * D_MODEL
ATT_BLOCK = 128
N_PATTERNS = 3
EPS = 1e-6
NEG_INF = -1e30

ADAM_LR, ADAM_B1, ADAM_B2, ADAM_EPS, ADAM_WD, ADAM_STEP = 0.001, 0.9, 0.999, 1e-08, 0.01, 10

V7X_VMEM_LIMIT_BYTES = 56 * 1024 * 1024
LANES = 1024

MESH_AXES = ("x", "y", "c")


def _pcall(body, *, name, out_shape, grid=(), in_specs=None, out_specs=None, scratch_shapes=(), dims=None):
    params = dict(vmem_limit_bytes=V7X_VMEM_LIMIT_BYTES)
    if dims is not None:
        params["dimension_semantics"] = dims
    specs = {}
    if in_specs is not None:
        specs = dict(grid=grid, in_specs=in_specs, out_specs=out_specs)
    return pl.pallas_call(body, name=name, out_shape=out_shape, scratch_shapes=scratch_shapes,
                          compiler_params=pltpu.CompilerParams(**params), **specs)


def _sds(shape, dtype):
    return jax.ShapeDtypeStruct(tuple(shape), dtype)


def _tile(n, target):
    if n <= target:
        return n
    for t in range(target - target % 128, 0, -128):
        if n % t == 0:
            return t
    raise ValueError((n, target))


def _sig(v):
    return pl.reciprocal(1.0 + jnp.exp(-v), approx=True)


def _mm(a, b, *, name, ta=False, tb=False, out_dtype=F32, tm=2048, tn=1024, tk=1024, riders=None):
    halves = a.ndim == 3
    if halves:
        a_rows, a_cols = a.shape[1], 2 * a.shape[2]
    else:
        a_rows, a_cols = a.shape
    if ta:
        K, M = a_rows, a_cols
    else:
        M, K = a_rows, a_cols
    if tb:
        N, K2 = b.shape
    else:
        K2, N = b.shape
    assert K == K2, (a.shape, b.shape)
    if halves:
        tm, tk = (min(tm, M // 2), tk) if ta else (tm, min(tk, K // 2))
    tm, tn, tk = _tile(M, tm), _tile(N, tn), _tile(K, tk)
    nk = K // tk
    if halves and ta:
        per = a.shape[2] // tm
        a_spec = pl.BlockSpec((None, tk, tm), lambda i, j, k: (i // per, k, i % per))
    elif halves:
        per = a.shape[2] // tk
        a_spec = pl.BlockSpec((None, tm, tk), lambda i, j, k: (k // per, i, k % per))
    else:
        a_spec = pl.BlockSpec((tk, tm), lambda i, j, k: (k, i)) if ta else pl.BlockSpec((tm, tk), lambda i, j, k: (i, k))
    b_spec = pl.BlockSpec((tn, tk), lambda i, j, k: (j, k)) if tb else pl.BlockSpec((tk, tn), lambda i, j, k: (k, j))
    dn = (((0 if ta else 1,), (1 if tb else 0,)), ((), ()))

    def body(a_ref, b_ref, o_ref, acc_ref):
        k = pl.program_id(2)

        @pl.when(k == 0)
        def _():
            acc_ref[...] = jnp.zeros_like(acc_ref)

        acc_ref[...] += lax.dot_general(a_ref[...].astype(BF16), b_ref[...].astype(BF16), dn,
                                        preferred_element_type=F32)

        @pl.when(k == nk - 1)
        def _():
            o_ref[...] = acc_ref[...].astype(out_dtype)

    def body_single(a_ref, b_ref, o_ref):
        o_ref[...] = lax.dot_general(a_ref[...].astype(BF16), b_ref[...].astype(BF16), dn,
                                     preferred_element_type=F32).astype(out_dtype)

    grid = (M // tm, N // tn, nk)
    scratch = [] if nk == 1 else [pltpu.VMEM((tm, tn), F32)]
    o_spec = pl.BlockSpec((tm, tn), lambda i, j, k: (i, j))
    if riders is None:
        return _pcall(body_single if nk == 1 else body, name=name, out_shape=_sds((M, N), out_dtype), grid=grid,
                      in_specs=[a_spec, b_spec], out_specs=o_spec, scratch_shapes=scratch,
                      dims=("parallel", "parallel", "arbitrary"))(a, b)
    rs = riders
    res = _pcall(_with_riders(body_single if nk == 1 else body, rs, 2, 1, len(scratch), tuple(g - 1 for g in grid)),
                 name=name, out_shape=(_sds((M, N), out_dtype),) + tuple(rs.out_shape), grid=grid,
                 in_specs=[a_spec, b_spec] + rs.specs, out_specs=(o_spec,) + tuple(rs.specs),
                 scratch_shapes=scratch + rs.scratch, dims=("arbitrary", "arbitrary", "arbitrary"))(a, b, *rs.arrs)
    return res[0], list(res[1:])


def _ada_fwd(c_all, w_ada, b_ada_cols):
    n = w_ada.shape[1]

    def body(c_ref, w_ref, b_ref, o_ref):
        c = c_ref[...]
        act = c * _sig(c)
        o_ref[...] = jnp.dot(act.astype(BF16), w_ref[...].astype(BF16), preferred_element_type=F32) + b_ref[...]

    return _pcall(body, name="ada_fwd", out_shape=_sds((c_all.shape[0], n), F32))(c_all, w_ada, b_ada_cols)


def _ada_bwd(c_all, dmod_all, dmod_cols):
    n = dmod_cols.shape[1]

    def body(c_ref, da_ref, dc_ref, gw_ref, gb_ref):
        c = c_ref[...]
        act = c * _sig(c)
        gw_ref[...] = lax.dot_general(act, dc_ref[...], (((0,), (0,)), ((), ())), preferred_element_type=F32,
                                      precision=lax.Precision.HIGHEST)
        gb_ref[...] = jnp.sum(da_ref[...], axis=0, keepdims=True)

    return _pcall(body, name="ada_bwd", out_shape=(_sds((D_MODEL, n), F32), _sds((1, dmod_all.shape[1]), F32)))(
        c_all, dmod_all, dmod_cols)


ROW_TILE = 512


def _row_specs(B, S):
    ts = min(S, ROW_TILE)
    row = pl.BlockSpec((1, ts, D_MODEL), lambda b, s: (b, s, 0))
    bvec = pl.BlockSpec((1, 1, D_MODEL), lambda b, s: (b, 0, 0))
    gvec = pl.BlockSpec((1, D_MODEL), lambda b, s: (0, 0))
    return ts, row, bvec, gvec


def _norm_mod(x3, g, sc, sh):
    B, S, _ = x3.shape
    ts, row, bvec, gvec = _row_specs(B, S)

    def body(x_ref, g_ref, sc_ref, sh_ref, u_ref):
        x = x_ref[0]
        r = lax.rsqrt(jnp.mean(x * x, axis=-1, keepdims=True) + EPS)
        u_ref[0] = ((x * r) * g_ref[...] * (1.0 + sc_ref[0]) + sh_ref[0]).astype(BF16)

    return _pcall(body, name="norm_mod1", out_shape=_sds(x3.shape, BF16), grid=(B, S // ts),
                  in_specs=[row, gvec, bvec, bvec], out_specs=row, dims=("parallel", "parallel"))(x3, g, sc, sh)


def _resid_norm_mod(x3, mix3, gt, g, sc, sh):
    B, S, _ = x3.shape
    ts, row, bvec, gvec = _row_specs(B, S)

    def body(x_ref, m_ref, gt_ref, g_ref, sc_ref, sh_ref, h_ref, u_ref):
        h = x_ref[0] + gt_ref[0] * m_ref[0]
        h_ref[0] = h
        r = lax.rsqrt(jnp.mean(h * h, axis=-1, keepdims=True) + EPS)
        u_ref[0] = ((h * r) * g_ref[...] * (1.0 + sc_ref[0]) + sh_ref[0]).astype(BF16)

    return _pcall(body, name="resid_norm_mod2", out_shape=(_sds(x3.shape, F32), _sds(x3.shape, BF16)),
                  grid=(B, S // ts), in_specs=[row, row, bvec, gvec, bvec, bvec], out_specs=(row, row),
                  dims=("parallel", "parallel"))(x3, mix3, gt, g, sc, sh)


def _norm_bwd(h3, du3, dres3, g, sc, name, mix3=None, gt=None):
    B, S, _ = h3.shape
    ts, row, bvec, gvec = _row_specs(B, S)
    with_gate = mix3 is not None

    def body(*refs):
        if with_gate:
            h_ref, du_ref, dr_ref, g_ref, sc_ref, m_ref, gt_ref, dh_ref, dsh_ref, dsc_ref, dg_ref, dgt_ref, dm_ref = refs
        else:
            h_ref, du_ref, dr_ref, g_ref, sc_ref, dh_ref, dsh_ref, dsc_ref, dg_ref = refs
        b, s = pl.program_id(0), pl.program_id(1)
        h = h_ref[0]
        r = lax.rsqrt(jnp.mean(h * h, axis=-1, keepdims=True) + EPS)
        xn = h * r
        du = du_ref[0].astype(F32)
        g = g_ref[...]
        sc1 = 1.0 + sc_ref[0]
        dxn = du * g * sc1
        dh = dr_ref[0] + r * (dxn - xn * jnp.mean(dxn * xn, axis=-1, keepdims=True))
        dh_ref[0] = dh

        @pl.when(s == 0)
        def _():
            dsh_ref[...] = jnp.zeros_like(dsh_ref)
            dsc_ref[...] = jnp.zeros_like(dsc_ref)
            if with_gate:
                dgt_ref[...] = jnp.zeros_like(dgt_ref)

        @pl.when((s == 0) & (b == 0))
        def _():
            dg_ref[...] = jnp.zeros_like(dg_ref)

        dux = du * xn
        dsh_ref[0] += jnp.sum(du, axis=0, keepdims=True)
        dsc_ref[0] += jnp.sum(dux * g, axis=0, keepdims=True)
        dg_ref[...] += jnp.sum(dux * sc1, axis=0, keepdims=True)
        if with_gate:
            dgt_ref[0] += jnp.sum(dh * m_ref[0], axis=0, keepdims=True)
            dm_ref[0] = (dh * gt_ref[0]).astype(BF16)

    bshape = _sds((B, 1, D_MODEL), F32)
    in_specs = [row, row, row, gvec, bvec]
    out_shape = [_sds(h3.shape, F32), bshape, bshape, _sds((1, D_MODEL), F32)]
    out_specs = [row, bvec, bvec, gvec]
    args = [h3, du3, dres3, g, sc]
    if with_gate:
        in_specs += [row, bvec]
        out_shape += [bshape, _sds(h3.shape, BF16)]
        out_specs += [bvec, row]
        args += [mix3, gt]
    return _pcall(body, name=name, out_shape=tuple(out_shape), grid=(B, S // ts), in_specs=in_specs,
                  out_specs=tuple(out_specs), dims=("arbitrary", "arbitrary"))(*args)


def _final_loss(h1, ffn3, tgt3, gt, gfin):
    B, S, _ = h1.shape
    ts, row, bvec, gvec = _row_specs(B, S)
    one = pl.BlockSpec((1, 1), lambda b, s: (0, 0))

    def body(h_ref, f_ref, t_ref, gt_ref, gf_ref, dh_ref, dff_ref, dgt_ref, dgf_ref, loss_ref):
        b, s = pl.program_id(0), pl.program_id(1)
        f = f_ref[0].astype(F32)
        gtv = gt_ref[0]
        gf = gf_ref[...]
        h2 = h_ref[0] + gtv * f
        r = lax.rsqrt(jnp.mean(h2 * h2, axis=-1, keepdims=True) + EPS)
        n = h2 * r
        e = n * gf - t_ref[0]
        dy = e * (1.0 / D_MODEL)
        dn = dy * gf
        dh2 = r * (dn - n * jnp.mean(dn * n, axis=-1, keepdims=True))
        dh_ref[0] = dh2
        dff_ref[0] = (dh2 * gtv).astype(BF16)

        @pl.when(s == 0)
        def _():
            dgt_ref[...] = jnp.zeros_like(dgt_ref)

        @pl.when((s == 0) & (b == 0))
        def _():
            dgf_ref[...] = jnp.zeros_like(dgf_ref)
            loss_ref[...] = jnp.zeros_like(loss_ref)

        dgt_ref[0] += jnp.sum(dh2 * f, axis=0, keepdims=True)
        dgf_ref[...] += jnp.sum(dy * n, axis=0, keepdims=True)
        rows = jnp.sum(e * e, axis=1, keepdims=True)
        loss_ref[...] += jnp.sum(rows, axis=0, keepdims=True) * (0.5 / D_MODEL)

    return _pcall(body, name="final_loss",
                  out_shape=(_sds(h1.shape, F32), _sds(h1.shape, BF16), _sds((B, 1, D_MODEL), F32),
                             _sds((1, D_MODEL), F32), _sds((1, 1), F32)),
                  grid=(B, S // ts), in_specs=[row, row, row, bvec, gvec], out_specs=(row, row, bvec, gvec, one),
                  dims=("arbitrary", "arbitrary"))(h1, ffn3, tgt3, gt, gfin)


def _att_scores(qh, kc, kp, h, dil, first, a_idx, j_idx):
    scale = HEAD_DIM ** -0.5
    nt = (((1,), (1,)), ((), ()))
    slope = (2.0 ** (-8.0 * (h + 1) / N_HEADS)) * dil
    dist_c = (a_idx - j_idx).astype(F32)
    s_c = lax.dot_general(qh, kc, nt, preferred_element_type=F32) * scale
    s_c = jnp.where(a_idx >= j_idx, s_c - slope * dist_c, NEG_INF)
    s_p = lax.dot_general(qh, kp, nt, preferred_element_type=F32) * scale
    s_p = jnp.where((j_idx >= a_idx) & jnp.logical_not(first), s_p - slope * (dist_c + float(ATT_BLOCK)), NEG_INF)
    return s_c, s_p


def _att_block_consts(seq_blocks):
    p = pl.program_id(0)
    j = pl.program_id(1)
    nb = lax.shift_right_logical(jnp.int32(seq_blocks), 2 * p)
    dil = lax.shift_left(jnp.int32(1), 2 * p).astype(F32)
    a_idx = lax.broadcasted_iota(jnp.int32, (ATT_BLOCK, ATT_BLOCK), 0)
    j_idx = lax.broadcasted_iota(jnp.int32, (ATT_BLOCK, ATT_BLOCK), 1)
    return j, nb, dil, a_idx, j_idx


def _attn_fwd(qb, kb, vb, seq_blocks):
    _, NB, _, _ = qb.shape
    cur = pl.BlockSpec((None, None, ATT_BLOCK, ATT_WIDTH), lambda p, j: (p, j, 0, 0))
    prev = pl.BlockSpec((None, None, ATT_BLOCK, ATT_WIDTH), lambda p, j: (p, jnp.maximum(j - 1, 0), 0, 0))
    lse_spec = pl.BlockSpec((None, None, ATT_BLOCK, N_HEADS), lambda p, j: (p, j, 0, 0))

    def body(q_ref, kc_ref, kp_ref, vc_ref, vp_ref, o_ref, lse_ref):
        j, nb, dil, a_idx, j_idx = _att_block_consts(seq_blocks)
        first = lax.rem(j, nb) == 0
        for h in range(N_HEADS):
            hs = slice(h * HEAD_DIM, (h + 1) * HEAD_DIM)
            s_c, s_p = _att_scores(q_ref[:, hs], kc_ref[:, hs], kp_ref[:, hs], h, dil, first, a_idx, j_idx)
            m = jnp.maximum(jnp.max(s_c, axis=1, keepdims=True), jnp.max(s_p, axis=1, keepdims=True))
            p_c = jnp.exp(s_c - m)
            p_p = jnp.exp(s_p - m)
            den = jnp.sum(p_c, axis=1, keepdims=True) + jnp.sum(p_p, axis=1, keepdims=True)
            o = (jnp.dot(p_c.astype(BF16), vc_ref[:, hs], preferred_element_type=F32)
                 + jnp.dot(p_p.astype(BF16), vp_ref[:, hs], preferred_element_type=F32))
            o_ref[:, hs] = o / den
            lse_ref[:, h:h + 1] = m + jnp.log(den)

    return _pcall(body, name="attn_fwd",
                  out_shape=(_sds(qb.shape, F32), _sds((N_PATTERNS, NB, ATT_BLOCK, N_HEADS), F32)),
                  grid=(N_PATTERNS, NB), in_specs=[cur, cur, prev, cur, prev], out_specs=(cur, lse_spec),
                  dims=("parallel", "parallel"))(qb, kb, kb, vb, vb)


def _attn_combine(o_p, lse_p):
    _, T, _ = o_p.shape
    tm = min(T, 1024)

    def body(o_ref, l_ref, out_ref, lse_ref):
        l0, l1, l2 = l_ref[0], l_ref[1], l_ref[2]
        m = jnp.maximum(jnp.maximum(l0, l1), l2)
        lse = m + jnp.log(jnp.exp(l0 - m) + jnp.exp(l1 - m) + jnp.exp(l2 - m))
        lse_ref[...] = lse
        w = [jnp.exp(l0 - lse), jnp.exp(l1 - lse), jnp.exp(l2 - lse)]
        for h in range(N_HEADS):
            hs = slice(h * HEAD_DIM, (h + 1) * HEAD_DIM)
            acc = w[0][:, h:h + 1] * o_ref[0, :, hs]
            acc = acc + w[1][:, h:h + 1] * o_ref[1, :, hs]
            acc = acc + w[2][:, h:h + 1] * o_ref[2, :, hs]
            out_ref[:, hs] = acc.astype(BF16)

    return _pcall(body, name="attn_combine", out_shape=(_sds((T, ATT_WIDTH), BF16), _sds((T, N_HEADS), F32)),
                  grid=(T // tm,),
                  in_specs=[pl.BlockSpec((N_PATTERNS, tm, ATT_WIDTH), lambda i: (0, i, 0)),
                            pl.BlockSpec((N_PATTERNS, tm, N_HEADS), lambda i: (0, i, 0))],
                  out_specs=(pl.BlockSpec((tm, ATT_WIDTH), lambda i: (i, 0)), pl.BlockSpec((tm, N_HEADS), lambda i: (i, 0))),
                  dims=("parallel",))(o_p, lse_p)


def _attn_bwd(qb, kb, vb, dob, ob, lseb, seq_blocks):
    _, NB, _, _ = qb.shape
    last = NB - 1
    cur = pl.BlockSpec((None, None, ATT_BLOCK, ATT_WIDTH), lambda p, j: (p, jnp.minimum(j, last), 0, 0))
    prev = pl.BlockSpec((None, None, ATT_BLOCK, ATT_WIDTH),
                        lambda p, j: (p, jnp.maximum(jnp.minimum(j, last) - 1, 0), 0, 0))
    lag = pl.BlockSpec((None, None, ATT_BLOCK, ATT_WIDTH), lambda p, j: (p, jnp.maximum(j - 1, 0), 0, 0))
    lse_spec = pl.BlockSpec((None, None, ATT_BLOCK, N_HEADS), lambda p, j: (p, jnp.minimum(j, last), 0, 0))
    scale = HEAD_DIM ** -0.5
    tn = (((0,), (0,)), ((), ()))
    nt = (((1,), (1,)), ((), ()))

    def body(q_ref, kc_ref, kp_ref, vc_ref, vp_ref, do_ref, o_ref, lse_ref, dq_ref, dk_ref, dv_ref, ck_ref, cv_ref):
        j, nb, dil, a_idx, j_idx = _att_block_consts(seq_blocks)

        @pl.when(j == 0)
        def _():
            ck_ref[...] = jnp.zeros_like(ck_ref)
            cv_ref[...] = jnp.zeros_like(cv_ref)

        @pl.when(j <= last)
        def _():
            first = lax.rem(j, nb) == 0
            for h in range(N_HEADS):
                hs = slice(h * HEAD_DIM, (h + 1) * HEAD_DIM)
                qh, kc, kp, vc, vp, doh = q_ref[:, hs], kc_ref[:, hs], kp_ref[:, hs], vc_ref[:, hs], vp_ref[:, hs], do_ref[:, hs]
                s_c, s_p = _att_scores(qh, kc, kp, h, dil, first, a_idx, j_idx)
                lse = lse_ref[:, h:h + 1]
                p_c = jnp.exp(s_c - lse)
                p_p = jnp.exp(s_p - lse)
                delta = jnp.sum(doh.astype(F32) * o_ref[:, hs].astype(F32), axis=1, keepdims=True)
                ds_c = (p_c * (lax.dot_general(doh, vc, nt, preferred_element_type=F32) - delta)).astype(BF16)
                ds_p = (p_p * (lax.dot_general(doh, vp, nt, preferred_element_type=F32) - delta)).astype(BF16)
                dq_ref[:, hs] = (jnp.dot(ds_c, kc, preferred_element_type=F32)
                                 + jnp.dot(ds_p, kp, preferred_element_type=F32)) * scale
                dk_ref[:, hs] = ck_ref[:, hs] + lax.dot_general(ds_p, qh, tn, preferred_element_type=F32) * scale
                dv_ref[:, hs] = cv_ref[:, hs] + lax.dot_general(p_p.astype(BF16), doh, tn, preferred_element_type=F32)
                ck_ref[:, hs] = lax.dot_general(ds_c, qh, tn, preferred_element_type=F32) * scale
                cv_ref[:, hs] = lax.dot_general(p_c.astype(BF16), doh, tn, preferred_element_type=F32)

        @pl.when(j == NB)
        def _():
            dk_ref[...] = ck_ref[...]
            dv_ref[...] = cv_ref[...]

    shp = _sds(qb.shape, F32)
    return _pcall(body, name="attn_bwd", out_shape=(shp, shp, shp), grid=(N_PATTERNS, NB + 1),
                  in_specs=[cur, cur, prev, cur, prev, cur, cur, lse_spec], out_specs=(cur, lag, lag),
                  scratch_shapes=[pltpu.VMEM((ATT_BLOCK, ATT_WIDTH), F32), pltpu.VMEM((ATT_BLOCK, ATT_WIDTH), F32)],
                  dims=("arbitrary", "arbitrary"))(qb, kb, kb, vb, vb, dob, ob, lseb)


def _sum3_cast(a, b, c):
    T, N = a.shape
    tm = min(T, 1024)
    spec = pl.BlockSpec((tm, N), lambda i: (i, 0))

    def body(a_ref, b_ref, c_ref, o_ref):
        o_ref[...] = (a_ref[...] + b_ref[...] + c_ref[...]).astype(BF16)

    return _pcall(body, name="sum3_cast", out_shape=_sds((T, N), BF16), grid=(T // tm,), in_specs=[spec] * 3,
                  out_specs=spec, dims=("parallel",))(a, b, c)


def _to_blocks(t, B, S):
    C = t.shape[-1]
    outs = []
    for p in range(N_PATTERNS):
        d = 4 ** p
        u = t.reshape(B, S // d, d, C).transpose(0, 2, 1, 3)
        outs.append(u.reshape(B * S // ATT_BLOCK, ATT_BLOCK, C))
    return jnp.stack(outs, axis=0)


def _from_blocks(tb, B, S):
    C = tb.shape[-1]
    outs = []
    for p in range(N_PATTERNS):
        d = 4 ** p
        u = tb[p].reshape(B, d, S // d, C).transpose(0, 2, 1, 3)
        outs.append(u.reshape(B * S, C))
    return jnp.stack(outs, axis=0)


ATT_GROUP = 4
ATT_GW = ATT_GROUP * HEAD_DIM
ATT_GROUPS = N_HEADS // ATT_GROUP
ATT_PAIRS = ATT_GW // ATT_BLOCK
ATT_UNROLL = 5
ATT_RESIDUE_UNROLL = 4
NT_DIMS = (((1,), (1,)), ((), ()))
TN_DIMS = (((0,), (0,)), ((), ()))


def _att_rows(start, d):
    if d == 1:
        return pl.ds(start if isinstance(start, int) else pl.multiple_of(start, ATT_BLOCK), ATT_BLOCK)
    return pl.ds(start, ATT_BLOCK, stride=d)


def _att_fill_bias(bias_ref, g, d):
    a = lax.broadcasted_iota(jnp.int32, (ATT_BLOCK, ATT_BLOCK), 0)
    j = lax.broadcasted_iota(jnp.int32, (ATT_BLOCK, ATT_BLOCK), 1)
    dist = (a - j).astype(F32)
    for hh in range(ATT_GROUP):
        t, e = divmod(hh, 2)
        rs = slice(e * ATT_BLOCK, (e + 1) * ATT_BLOCK)
        lo = 2.0 ** (-8.0 * (hh + 1) / N_HEADS) * d
        hi = 2.0 ** (-8.0 * (ATT_GROUP + hh + 1) / N_HEADS) * d
        slope = jnp.where(g == 0, lo, hi).astype(F32)
        bias_ref[t, rs, 0:ATT_BLOCK] = jnp.where(a >= j, -slope * dist, NEG_INF)
        bias_ref[t, rs, ATT_BLOCK:] = jnp.where(j >= a, -slope * (dist + float(ATT_BLOCK)), NEG_INF)


def _stack_heads(v2, low):
    return jnp.concatenate([jnp.where(low, v2, 0.0), jnp.where(low, 0.0, v2)], axis=0).astype(BF16)


def _unstack_heads(r2, low):
    return jnp.where(low, r2[0:ATT_BLOCK], r2[ATT_BLOCK:])


class _Riders:
    def __init__(self, arrs, mode, group="xy"):
        self.arrs, self.mode, self.n, self.group = list(arrs), mode, len(arrs), group
        slot_shapes = [a.shape if mode == "gather" else a.shape[1:] for a in self.arrs]
        self.out_shape = [_sds((_GROUP_SLOTS[group],) + s, a.dtype) for s, a in zip(slot_shapes, self.arrs)]
        k = len(_GROUP_MASKS[group])
        self.scratch = [pltpu.SemaphoreType.DMA((k * self.n,)), pltpu.SemaphoreType.DMA((k * self.n,)),
                        pltpu.SemaphoreType.DMA((2 * self.n,))] + [pltpu.VMEM(s, a.dtype) for s, a in zip(slot_shapes, self.arrs)]
        self.specs = [pl.BlockSpec(memory_space=pl.ANY)] * self.n

    def _remote(self, x_refs, o_refs, send_sems, recv_sems):
        x, y, c = lax.axis_index("x"), lax.axis_index("y"), lax.axis_index("c")
        me = _group_slot(self.group, x, y, c)
        masks = _GROUP_MASKS[self.group]
        cps = []
        for i in range(self.n):
            for k, (dx, dy, dc) in enumerate(masks):
                px, py, pc = _flip(x, dx), _flip(y, dy), _flip(c, dc)
                src = x_refs[i] if self.mode == "gather" else x_refs[i].at[_group_slot(self.group, px, py, pc)]
                cps.append(pltpu.make_async_remote_copy(
                    src_ref=src, dst_ref=o_refs[i].at[me], send_sem=send_sems.at[len(masks) * i + k],
                    recv_sem=recv_sems.at[len(masks) * i + k], device_id=(px, py, pc),
                    device_id_type=pl.DeviceIdType.MESH))
        return cps, me

    def start(self, x_refs, o_refs, scratch):
        send_sems, recv_sems, local_sems, bufs = scratch[0], scratch[1], scratch[2], scratch[3:]
        cps, me = self._remote(x_refs, o_refs, send_sems, recv_sems)
        for cp in cps:
            cp.start()
        for i in range(self.n):
            src = x_refs[i] if self.mode == "gather" else x_refs[i].at[me]
            load = pltpu.make_async_copy(src, bufs[i], local_sems.at[2 * i])
            load.start()
            load.wait()
            pltpu.make_async_copy(bufs[i], o_refs[i].at[me], local_sems.at[2 * i + 1]).start()

    def wait(self, x_refs, o_refs, scratch):
        send_sems, recv_sems, local_sems, bufs = scratch[0], scratch[1], scratch[2], scratch[3:]
        cps, me = self._remote(x_refs, o_refs, send_sems, recv_sems)
        for cp in cps:
            cp.wait()
        for i in range(self.n):
            pltpu.make_async_copy(bufs[i], o_refs[i].at[me], local_sems.at[2 * i + 1]).wait()


def _with_riders(compute, riders, n_in, n_out, n_scratch, last_step):
    if riders is None:
        return compute
    n = riders.n

    def body(*refs):
        ins, x_refs = refs[:n_in], refs[n_in:n_in + n]
        outs, o_refs = refs[n_in + n:n_in + n + n_out], refs[n_in + n + n_out:n_in + 2 * n + n_out]
        scratch = refs[n_in + 2 * n + n_out:]
        own, ride = scratch[:n_scratch], scratch[n_scratch:]
        ids = [pl.program_id(i) for i in range(len(last_step))]
        first = functools.reduce(jnp.logical_and, [i == 0 for i in ids])
        last = functools.reduce(jnp.logical_and, [i == l for i, l in zip(ids, last_step)])

        @pl.when(first)
        def _():
            riders.start(x_refs, o_refs, ride)

        compute(*ins, *outs, *own)

        @pl.when(last)
        def _():
            riders.wait(x_refs, o_refs, ride)

    return body


def _attention_fwd(proj3, seq_blocks, riders=None):
    B, S, _ = proj3.shape
    scale = HEAD_DIM ** -0.5
    nq = ATT_WIDTH // ATT_GW

    def col(k):
        return pl.BlockSpec((1, S, ATT_GW), lambda b, g, k=k: (b, 0, k * nq + g))

    o_spec = pl.BlockSpec((1, S, ATT_GW), lambda b, g: (b, 0, g))
    l_spec = pl.BlockSpec((1, 1, S, ATT_BLOCK), lambda b, g: (b, g, 0, 0))

    def compute(q_ref, k_ref, v_ref, o_ref, lse_ref, qf, kf, vf, os, ls, bias):
        g = pl.program_id(1)
        for t in range(ATT_PAIRS):
            ts = slice(t * ATT_BLOCK, (t + 1) * ATT_BLOCK)
            qf[t] = q_ref[0, :, ts].astype(F32) * scale
            kf[t] = k_ref[0, :, ts].astype(F32)
            vf[t] = v_ref[0, :, ts].astype(F32)
        lane = lax.broadcasted_iota(jnp.int32, (ATT_BLOCK, ATT_BLOCK), 1)
        low = lane < HEAD_DIM

        def block(p, d, r, n, has_prev):
            start = n * (ATT_BLOCK * d) + r
            rows = _att_rows(start, d)
            prows = _att_rows(start - ATT_BLOCK * d, d) if has_prev else None
            lse_t = jnp.zeros((ATT_BLOCK, ATT_BLOCK), F32)
            for t in range(ATT_PAIRS):
                q2 = _stack_heads(qf[t, rows, :], low)
                k2 = kf[t, rows, :].astype(BF16)
                v2 = vf[t, rows, :].astype(BF16)
                if has_prev:
                    k2 = jnp.concatenate([k2, kf[t, prows, :].astype(BF16)], axis=0)
                    v2 = jnp.concatenate([v2, vf[t, prows, :].astype(BF16)], axis=0)
                    b2 = bias[t]
                else:
                    b2 = bias[t, :, 0:ATT_BLOCK]
                s = lax.dot_general(q2, k2, NT_DIMS, preferred_element_type=F32) + b2
                m = jnp.max(s, axis=1, keepdims=True)
                pr = jnp.exp(s - m)
                den = jnp.sum(pr, axis=1, keepdims=True)
                o = jnp.dot(pr.astype(BF16), v2, preferred_element_type=F32) * (1.0 / den)
                os[p, t, rows, :] = _unstack_heads(o, low)
                lse2 = m + jnp.log(den)
                lse_t = jnp.where(lane == 2 * t, lse2[0:ATT_BLOCK], lse_t)
                lse_t = jnp.where(lane == 2 * t + 1, lse2[ATT_BLOCK:], lse_t)
            ls[p, rows, :] = lse_t

        for p in range(N_PATTERNS):
            d = 4 ** p
            _att_fill_bias(bias, g, d)
            _att_one_pattern(block, p, d, seq_blocks // d)

        def combine(i, carry):
            rows = pl.ds(pl.multiple_of(i * ATT_BLOCK, ATT_BLOCK), ATT_BLOCK)
            l0, l1, l2 = ls[0, rows, :], ls[1, rows, :], ls[2, rows, :]
            m = jnp.maximum(jnp.maximum(l0, l1), l2)
            lse = m + jnp.log(jnp.exp(l0 - m) + jnp.exp(l1 - m) + jnp.exp(l2 - m))
            lse_ref[0, 0, rows, :] = lse
            w = [jnp.exp(l0 - lse), jnp.exp(l1 - lse), jnp.exp(l2 - lse)]
            for t in range(ATT_PAIRS):
                acc = jnp.zeros((ATT_BLOCK, ATT_BLOCK), F32)
                for p in range(N_PATTERNS):
                    wt = jnp.where(low, w[p][:, 2 * t:2 * t + 1], w[p][:, 2 * t + 1:2 * t + 2])
                    acc = acc + wt * os[p, t, rows, :]
                o_ref[0, rows, t * ATT_BLOCK:(t + 1) * ATT_BLOCK] = acc.astype(BF16)
            return carry

        lax.fori_loop(0, S // ATT_BLOCK, combine, 0, unroll=2)

    scratch = ([pltpu.VMEM((ATT_PAIRS, S, ATT_BLOCK), F32)] * 3
               + [pltpu.VMEM((N_PATTERNS, ATT_PAIRS, S, ATT_BLOCK), F32), pltpu.VMEM((N_PATTERNS, S, ATT_BLOCK), F32),
                  pltpu.VMEM((ATT_PAIRS, 2 * ATT_BLOCK, 2 * ATT_BLOCK), F32)])
    rs = riders
    res = _pcall(_with_riders(compute, rs, 3, 2, len(scratch), (B - 1, ATT_GROUPS - 1)), name="attention_fwd",
                 out_shape=(_sds((B, S, ATT_WIDTH), BF16), _sds((B, ATT_GROUPS, S, ATT_BLOCK), F32))
                 + (tuple(rs.out_shape) if rs else ()),
                 grid=(B, ATT_GROUPS), in_specs=[col(0), col(1), col(2)] + (rs.specs if rs else []),
                 out_specs=(o_spec, l_spec) + (tuple(rs.specs) if rs else ()),
                 scratch_shapes=scratch + (rs.scratch if rs else []),
                 dims=("arbitrary", "arbitrary"))(proj3, proj3, proj3, *(rs.arrs if rs else []))
    return res[0], res[1], list(res[2:])


def _att_one_pattern(block, p, d, nb):
    def per_residue(r, carry):
        block(p, d, r, 0, False)
        if nb > 1:
            def per_block(n, c2):
                block(p, d, r, n, True)
                return c2
            lax.fori_loop(1, nb, per_block, 0, unroll=ATT_UNROLL if (nb - 1) % ATT_UNROLL == 0 else nb - 1)
        return carry

    if d == 1:
        per_residue(0, 0)
    else:
        lax.fori_loop(0, d, per_residue, 0, unroll=ATT_RESIDUE_UNROLL if nb == 1 else 1)


def _attention_bwd(proj3, do3, o3, lse4, seq_blocks, riders=None):
    B, S, _ = proj3.shape
    scale = HEAD_DIM ** -0.5
    nq = ATT_WIDTH // ATT_GW

    def col(k):
        return pl.BlockSpec((1, S, ATT_GW), lambda b, g, k=k: (b, 0, k * nq + g))

    o_spec = pl.BlockSpec((1, S, ATT_GW), lambda b, g: (b, 0, g))
    l_spec = pl.BlockSpec((1, 1, S, ATT_BLOCK), lambda b, g: (b, g, 0, 0))

    def compute(q_ref, k_ref, v_ref, do_ref, o_ref, lse_ref, dq_ref, dk_ref, dv_ref,
                qf, kf, vf, dof, dl, aq, ak, av, bias):
        g = pl.program_id(1)
        for t in range(ATT_PAIRS):
            ts = slice(t * ATT_BLOCK, (t + 1) * ATT_BLOCK)
            qf[t] = q_ref[0, :, ts].astype(F32) * scale
            kf[t] = k_ref[0, :, ts].astype(F32)
            vf[t] = v_ref[0, :, ts].astype(F32)
            dof[t] = do_ref[0, :, ts].astype(F32)
        aq[...] = jnp.zeros_like(aq)
        ak[...] = jnp.zeros_like(ak)
        av[...] = jnp.zeros_like(av)
        lane = lax.broadcasted_iota(jnp.int32, (ATT_BLOCK, ATT_BLOCK), 1)
        low = lane < HEAD_DIM

        def fill_delta(i, carry):
            rows = pl.ds(pl.multiple_of(i * ATT_BLOCK, ATT_BLOCK), ATT_BLOCK)
            acc = jnp.zeros((ATT_BLOCK, ATT_BLOCK), F32)
            for t in range(ATT_PAIRS):
                prod = dof[t, rows, :] * o_ref[0, rows, t * ATT_BLOCK:(t + 1) * ATT_BLOCK].astype(F32)
                lo = jnp.sum(jnp.where(low, prod, 0.0), axis=1, keepdims=True)
                hi = jnp.sum(prod, axis=1, keepdims=True) - lo
                acc = jnp.where(lane == 2 * t, lo, acc)
                acc = jnp.where(lane == 2 * t + 1, hi, acc)
            dl[rows, :] = acc
            return carry

        lax.fori_loop(0, S // ATT_BLOCK, fill_delta, 0, unroll=2)

        def block(p, d, r, n, has_prev):
            start = n * (ATT_BLOCK * d) + r
            rows = _att_rows(start, d)
            prows = _att_rows(start - ATT_BLOCK * d, d) if has_prev else None
            lse_t = lse_ref[0, 0, rows, :]
            dl_t = dl[rows, :]
            for t in range(ATT_PAIRS):
                q2 = _stack_heads(qf[t, rows, :], low)
                do2 = _stack_heads(dof[t, rows, :], low)
                k2 = kf[t, rows, :].astype(BF16)
                v2 = vf[t, rows, :].astype(BF16)
                if has_prev:
                    k2 = jnp.concatenate([k2, kf[t, prows, :].astype(BF16)], axis=0)
                    v2 = jnp.concatenate([v2, vf[t, prows, :].astype(BF16)], axis=0)
                    b2 = bias[t]
                else:
                    b2 = bias[t, :, 0:ATT_BLOCK]
                lse2 = jnp.concatenate([lse_t[:, 2 * t:2 * t + 1], lse_t[:, 2 * t + 1:2 * t + 2]], axis=0)
                dl2 = jnp.concatenate([dl_t[:, 2 * t:2 * t + 1], dl_t[:, 2 * t + 1:2 * t + 2]], axis=0)
                s = lax.dot_general(q2, k2, NT_DIMS, preferred_element_type=F32) + b2
                pr = jnp.exp(s - lse2)
                ds = (pr * (lax.dot_general(do2, v2, NT_DIMS, preferred_element_type=F32) - dl2)).astype(BF16)
                dq = _unstack_heads(jnp.dot(ds, k2, preferred_element_type=F32), low)
                dk = lax.dot_general(ds, q2, TN_DIMS, preferred_element_type=F32)
                dv = lax.dot_general(pr.astype(BF16), do2, TN_DIMS, preferred_element_type=F32)
                aq[t, rows, :] = aq[t, rows, :] + dq * scale
                ak[t, rows, :] = ak[t, rows, :] + dk[0:ATT_BLOCK]
                av[t, rows, :] = av[t, rows, :] + dv[0:ATT_BLOCK]
                if has_prev:
                    ak[t, prows, :] = ak[t, prows, :] + dk[ATT_BLOCK:]
                    av[t, prows, :] = av[t, prows, :] + dv[ATT_BLOCK:]

        for p in range(N_PATTERNS):
            d = 4 ** p
            _att_fill_bias(bias, g, d)
            _att_one_pattern(block, p, d, seq_blocks // d)

        for t in range(ATT_PAIRS):
            ts = slice(t * ATT_BLOCK, (t + 1) * ATT_BLOCK)
            dq_ref[0, :, ts] = aq[t].astype(BF16)
            dk_ref[0, :, ts] = ak[t].astype(BF16)
            dv_ref[0, :, ts] = av[t].astype(BF16)

    shp = _sds((B, S, ATT_WIDTH), BF16)
    pair_buf = pltpu.VMEM((ATT_PAIRS, S, ATT_BLOCK), F32)
    scratch = ([pair_buf] * 4 + [pltpu.VMEM((S, ATT_BLOCK), F32)] + [pair_buf] * 3
               + [pltpu.VMEM((ATT_PAIRS, 2 * ATT_BLOCK, 2 * ATT_BLOCK), F32)])
    rs = riders
    res = _pcall(_with_riders(compute, rs, 6, 3, len(scratch), (B - 1, ATT_GROUPS - 1)), name="attention_bwd",
                 out_shape=(shp, shp, shp) + (tuple(rs.out_shape) if rs else ()), grid=(B, ATT_GROUPS),
                 in_specs=[col(0), col(1), col(2), o_spec, o_spec, l_spec] + (rs.specs if rs else []),
                 out_specs=(o_spec, o_spec, o_spec) + (tuple(rs.specs) if rs else ()),
                 scratch_shapes=scratch + (rs.scratch if rs else []),
                 dims=("arbitrary", "arbitrary"))(proj3, proj3, proj3, do3, o3, lse4, *(rs.arrs if rs else []))
    return res[0], res[1], res[2], list(res[3:])


def _expand_groups(m):
    rows = SSM_WIDTH
    t = jnp.concatenate([m] * SSM_GROUPS, axis=0)
    r = lax.broadcasted_iota(jnp.int32, (rows, SSM_LANES), 0)
    l = lax.broadcasted_iota(jnp.int32, (rows, SSM_LANES), 1)
    keep = lax.shift_right_logical(r, 4) == lax.shift_right_logical(l, 6)
    return jnp.where(keep, t, 0.0)


def _collapse_groups(m):
    rows = SSM_WIDTH
    r = lax.broadcasted_iota(jnp.int32, (rows, SSM_LANES), 0)
    l = lax.broadcasted_iota(jnp.int32, (rows, SSM_LANES), 1)
    keep = lax.shift_right_logical(r, 4) == lax.shift_right_logical(l, 6)
    t = jnp.where(keep, m, 0.0)
    acc = t[0:SSM_GROUP_CH]
    for g in range(1, SSM_GROUPS):
        acc = acc + t[g * SSM_GROUP_CH:(g + 1) * SSM_GROUP_CH]
    return acc


def _zoh(lr, li, ldt):
    dt = jnp.exp(ldt)
    mag = jnp.exp(lr * dt)
    ang = li * dt
    cs, sn = jnp.cos(ang), jnp.sin(ang)
    ab_re, ab_im = mag * cs, mag * sn
    nr, ni = ab_re - 1.0, ab_im
    den = lr * lr + li * li
    n_re = nr * lr + ni * li
    n_im = ni * lr - nr * li
    return dict(dt=dt, mag=mag, cs=cs, sn=sn, ab_re=ab_re, ab_im=ab_im, nr=nr, ni=ni, den=den, n_re=n_re, n_im=n_im,
                f_re=n_re / den, f_im=n_im / den)


def _ssm_params(lr, li, ldt, br, bi, cr, ci):
    def body(lr_ref, li_ref, ldt_ref, br_ref, bi_ref, cr_ref, ci_ref, ab_ref, w_ref, c_ref):
        z = _zoh(lr_ref[...], li_ref[...], ldt_ref[...])
        ab_ref[0:1, :] = z["ab_re"]
        ab_ref[1:2, :] = z["ab_im"]
        br, bi = br_ref[...], bi_ref[...]
        w_ref[:, 0:SSM_LANES] = _expand_groups(z["f_re"] * br - z["f_im"] * bi).astype(BF16)
        w_ref[:, SSM_LANES:] = _expand_groups(z["f_re"] * bi + z["f_im"] * br).astype(BF16)
        c_ref[:, 0:SSM_LANES] = _expand_groups(cr_ref[...]).astype(BF16)
        c_ref[:, SSM_LANES:] = _expand_groups(-ci_ref[...]).astype(BF16)

    return _pcall(body, name="ssm_params",
                  out_shape=(_sds((2, SSM_LANES), F32), _sds((SSM_WIDTH, 2 * SSM_LANES), BF16),
                             _sds((SSM_WIDTH, 2 * SSM_LANES), BF16)))(lr, li, ldt, br, bi, cr, ci)


def _ssm_params_bwd(lr, li, ldt, br, bi, dab, dw, dc):
    def body(lr_ref, li_ref, ldt_ref, br_ref, bi_ref, dab_ref, dw_ref, dc_ref,
             dlr_ref, dli_ref, dldt_ref, dbr_ref, dbi_ref, dcr_ref, dci_ref):
        lr, li = lr_ref[...], li_ref[...]
        z = _zoh(lr, li, ldt_ref[...])
        br, bi = br_ref[...], bi_ref[...]
        dbb_re = _collapse_groups(dw_ref[:, 0:SSM_LANES])
        dbb_im = _collapse_groups(dw_ref[:, SSM_LANES:])
        dcr_ref[...] = _collapse_groups(dc_ref[:, 0:SSM_LANES])
        dci_ref[...] = -_collapse_groups(dc_ref[:, SSM_LANES:])
        f_re, f_im = z["f_re"], z["f_im"]
        dbr_ref[...] = f_re * dbb_re + f_im * dbb_im
        dbi_ref[...] = f_re * dbb_im - f_im * dbb_re
        df_re = jnp.sum(dbb_re * br + dbb_im * bi, axis=0, keepdims=True)
        df_im = jnp.sum(dbb_im * br - dbb_re * bi, axis=0, keepdims=True)
        den = z["den"]
        dn_re, dn_im = df_re / den, df_im / den
        dden = -(df_re * z["n_re"] + df_im * z["n_im"]) / (den * den)
        dnr = dn_re * lr - dn_im * li
        dni = dn_re * li + dn_im * lr
        dlr = dn_re * z["nr"] + dn_im * z["ni"] + 2.0 * dden * lr
        dli = dn_re * z["ni"] - dn_im * z["nr"] + 2.0 * dden * li
        dab_re = dab_ref[0:1, :] + dnr
        dab_im = dab_ref[1:2, :] + dni
        mag, cs, sn, dt = z["mag"], z["cs"], z["sn"], z["dt"]
        dmag = dab_re * cs + dab_im * sn
        dang = mag * (dab_im * cs - dab_re * sn)
        dlr_ref[...] = dlr + dmag * mag * dt
        dli_ref[...] = dli + dang * dt
        ddt = dmag * mag * lr + dang * li
        per_lane = jnp.broadcast_to(ddt * dt, (8, SSM_LANES))
        lane = lax.broadcasted_iota(jnp.int32, (SSM_LANES, 128), 0)
        col = lax.broadcasted_iota(jnp.int32, (SSM_LANES, 128), 1)
        ind = jnp.where(lax.shift_right_logical(lane, 6) == col, 1.0, 0.0)
        dldt_ref[...] = jnp.dot(per_lane, ind, preferred_element_type=F32, precision=lax.Precision.HIGHEST)[0:1]

    vec = _sds((1, SSM_LANES), F32)
    mat = _sds((SSM_GROUP_CH, SSM_LANES), F32)
    return _pcall(body, name="ssm_params_bwd", out_shape=(vec, vec, _sds((1, 128), F32), mat, mat, mat, mat))(
        lr, li, ldt, br, bi, dab, dw, dc)


SCAN_CHUNK = 512


def _scan_consts(ar, ai, k_ref, reverse):
    row = lax.broadcasted_iota(jnp.int32, (8, SSM_LANES), 0)
    pw = [(ar, ai)]
    for _ in range(7):
        pr, pi = pw[-1]
        pw.append((pr * ar - pi * ai, pr * ai + pi * ar))
    for n, k in enumerate((1, 2, 4)):
        keep = (row < 8 - k) if reverse else (row >= k)
        k_ref[2 * n] = jnp.where(keep, jnp.broadcast_to(pw[k - 1][0], (8, SSM_LANES)), 0.0)
        k_ref[2 * n + 1] = jnp.where(keep, jnp.broadcast_to(pw[k - 1][1], (8, SSM_LANES)), 0.0)
    cr = jnp.zeros((8, SSM_LANES), F32)
    ci = jnp.zeros((8, SSM_LANES), F32)
    for r in range(8):
        e = (8 - r) if reverse else (r + 1)
        cr = jnp.where(row == r, jnp.broadcast_to(pw[e - 1][0], (8, SSM_LANES)), cr)
        ci = jnp.where(row == r, jnp.broadcast_to(pw[e - 1][1], (8, SSM_LANES)), ci)
    k_ref[6] = cr
    k_ref[7] = ci


def _scan_tile(xr, xi, k_ref, car, cai, reverse):
    for n, k in enumerate((1, 2, 4)):
        sh = (8 - k) if reverse else k
        sr = pltpu.roll(xr, sh, 0)
        si = pltpu.roll(xi, sh, 0)
        mr, mi = k_ref[2 * n], k_ref[2 * n + 1]
        xr, xi = xr + mr * sr - mi * si, xi + mr * si + mi * sr
    pr, pi = k_ref[6], k_ref[7]
    xr, xi = xr + pr * car - pi * cai, xi + pr * cai + pi * car
    return xr, xi


def _scan_fwd(bu3, abar):
    B, S, _ = bu3.shape
    ch = min(S, SCAN_CHUNK)
    blk = pl.BlockSpec((1, ch, 2 * SSM_LANES), lambda b, c: (b, c, 0))

    def body(ab_ref, bu_ref, x_ref, k_ref, carry_ref):
        _scan_consts(ab_ref[0:1, :], ab_ref[1:2, :], k_ref, False)

        @pl.when(pl.program_id(1) == 0)
        def _():
            carry_ref[...] = jnp.zeros_like(carry_ref)

        def step(i, carry):
            base = pl.multiple_of(i * 8, 8)
            xr = bu_ref[0, pl.ds(base, 8), 0:SSM_LANES]
            xi = bu_ref[0, pl.ds(base, 8), SSM_LANES:]
            xr, xi = _scan_tile(xr, xi, k_ref, carry[0], carry[1], False)
            x_ref[0, pl.ds(base, 8), 0:SSM_LANES] = xr
            x_ref[0, pl.ds(base, 8), SSM_LANES:] = xi
            return (jnp.broadcast_to(xr[7:8], (8, SSM_LANES)), jnp.broadcast_to(xi[7:8], (8, SSM_LANES)))

        cr, ci = lax.fori_loop(0, ch // 8, step, (carry_ref[0], carry_ref[1]))
        carry_ref[0] = cr
        carry_ref[1] = ci

    return _pcall(body, name="scan_fwd", out_shape=_sds(bu3.shape, F32), grid=(B, S // ch),
                  in_specs=[pl.BlockSpec((2, SSM_LANES), lambda b, c: (0, 0)), blk], out_specs=blk,
                  scratch_shapes=[pltpu.VMEM((8, 8, SSM_LANES), F32), pltpu.VMEM((2, 8, SSM_LANES), F32)],
                  dims=("arbitrary", "arbitrary"))(abar, bu3)


def _scan_bwd(dx3, xs3, abar):
    B, S, _ = dx3.shape
    ch = min(S, SCAN_CHUNK)
    nc = S // ch
    blk = pl.BlockSpec((1, ch, 2 * SSM_LANES), lambda b, c: (b, nc - 1 - c, 0))

    def body(ab_ref, dx_ref, xs_ref, g_ref, da_ref, k_ref, carry_ref, acc_ref):
        b, c = pl.program_id(0), pl.program_id(1)
        _scan_consts(ab_ref[0:1, :], -ab_ref[1:2, :], k_ref, True)
        row = lax.broadcasted_iota(jnp.int32, (8, SSM_LANES), 0)

        @pl.when(c == 0)
        def _():
            carry_ref[...] = jnp.zeros_like(carry_ref)

        @pl.when((c == 0) & (b == 0))
        def _():
            acc_ref[...] = jnp.zeros_like(acc_ref)

        def step(i, carry):
            car, cai, ar_acc, ai_acc = carry
            base = pl.multiple_of((ch // 8 - 1 - i) * 8, 8)
            gr = dx_ref[0, pl.ds(base, 8), 0:SSM_LANES]
            gi = dx_ref[0, pl.ds(base, 8), SSM_LANES:]
            gr, gi = _scan_tile(gr, gi, k_ref, car, cai, True)
            g_ref[0, pl.ds(base, 8), 0:SSM_LANES] = gr
            g_ref[0, pl.ds(base, 8), SSM_LANES:] = gi
            nr = jnp.where(row == 7, car, pltpu.roll(gr, 7, 0))
            ni = jnp.where(row == 7, cai, pltpu.roll(gi, 7, 0))
            xr = xs_ref[0, pl.ds(base, 8), 0:SSM_LANES]
            xi = xs_ref[0, pl.ds(base, 8), SSM_LANES:]
            ar_acc = ar_acc + nr * xr + ni * xi
            ai_acc = ai_acc + ni * xr - nr * xi
            return (jnp.broadcast_to(gr[0:1], (8, SSM_LANES)), jnp.broadcast_to(gi[0:1], (8, SSM_LANES)), ar_acc, ai_acc)

        cr, ci, ar_acc, ai_acc = lax.fori_loop(0, ch // 8, step, (carry_ref[0], carry_ref[1], acc_ref[0], acc_ref[1]))
        carry_ref[0] = cr
        carry_ref[1] = ci
        acc_ref[0] = ar_acc
        acc_ref[1] = ai_acc
        da_ref[0:1, :] = jnp.sum(ar_acc, axis=0, keepdims=True)
        da_ref[1:2, :] = jnp.sum(ai_acc, axis=0, keepdims=True)

    return _pcall(body, name="scan_bwd", out_shape=(_sds(dx3.shape, F32), _sds((2, SSM_LANES), F32)), grid=(B, nc),
                  in_specs=[pl.BlockSpec((2, SSM_LANES), lambda b, c: (0, 0)), blk, blk],
                  out_specs=(blk, pl.BlockSpec((2, SSM_LANES), lambda b, c: (0, 0))),
                  scratch_shapes=[pltpu.VMEM((8, 8, SSM_LANES), F32), pltpu.VMEM((2, 8, SSM_LANES), F32),
                                  pltpu.VMEM((2, 8, SSM_LANES), F32)],
                  dims=("arbitrary", "arbitrary"))(abar, dx3, xs3)


US_BLOCK = (3 * ATT_WIDTH) // SSM_WIDTH


def _ssm_scan_fwd(proj3, abar, w_bu, w_c):
    B, S, _ = proj3.shape
    ch = min(S, SCAN_CHUNK)
    u_spec = pl.BlockSpec((1, ch, SSM_WIDTH), lambda b, c: (b, c, US_BLOCK))
    x_spec = pl.BlockSpec((1, ch, 2 * SSM_LANES), lambda b, c: (b, c, 0))
    y_spec = pl.BlockSpec((1, ch, SSM_WIDTH), lambda b, c: (b, c, 0))
    w_spec = pl.BlockSpec((SSM_WIDTH, 2 * SSM_LANES), lambda b, c: (0, 0))

    def body(ab_ref, u_ref, wb_ref, wc_ref, x_ref, y_ref, k_ref, carry_ref):
        _scan_consts(ab_ref[0:1, :], ab_ref[1:2, :], k_ref, False)

        @pl.when(pl.program_id(1) == 0)
        def _():
            carry_ref[...] = jnp.zeros_like(carry_ref)

        x_ref[0] = jnp.dot(u_ref[0], wb_ref[...], preferred_element_type=F32)

        def step(i, carry):
            base = pl.multiple_of(i * 8, 8)
            xr = x_ref[0, pl.ds(base, 8), 0:SSM_LANES]
            xi = x_ref[0, pl.ds(base, 8), SSM_LANES:]
            xr, xi = _scan_tile(xr, xi, k_ref, carry[0], carry[1], False)
            x_ref[0, pl.ds(base, 8), 0:SSM_LANES] = xr
            x_ref[0, pl.ds(base, 8), SSM_LANES:] = xi
            return (jnp.broadcast_to(xr[7:8], (8, SSM_LANES)), jnp.broadcast_to(xi[7:8], (8, SSM_LANES)))

        cr, ci = lax.fori_loop(0, ch // 8, step, (carry_ref[0], carry_ref[1]))
        carry_ref[0] = cr
        carry_ref[1] = ci
        y_ref[0] = lax.dot_general(x_ref[0].astype(BF16), wc_ref[...], NT_DIMS, preferred_element_type=F32)

    return _pcall(body, name="ssm_scan_fwd",
                  out_shape=(_sds((B, S, 2 * SSM_LANES), F32), _sds((B, S, SSM_WIDTH), F32)), grid=(B, S // ch),
                  in_specs=[pl.BlockSpec((2, SSM_LANES), lambda b, c: (0, 0)), u_spec, w_spec, w_spec],
                  out_specs=(x_spec, y_spec),
                  scratch_shapes=[pltpu.VMEM((8, 8, SSM_LANES), F32), pltpu.VMEM((2, 8, SSM_LANES), F32)],
                  dims=("arbitrary", "arbitrary"))(abar, proj3, w_bu, w_c)


def _ssm_scan_bwd(proj3, dy3, xs3, abar, w_bu, w_c, dsk):
    B, S, _ = proj3.shape
    ch = min(S, SCAN_CHUNK)
    nc = S // ch
    u_spec = pl.BlockSpec((1, ch, SSM_WIDTH), lambda b, c: (b, nc - 1 - c, US_BLOCK))
    x_spec = pl.BlockSpec((1, ch, 2 * SSM_LANES), lambda b, c: (b, nc - 1 - c, 0))
    y_spec = pl.BlockSpec((1, ch, SSM_WIDTH), lambda b, c: (b, nc - 1 - c, 0))
    w_spec = pl.BlockSpec((SSM_WIDTH, 2 * SSM_LANES), lambda b, c: (0, 0))
    ab_spec = pl.BlockSpec((2, SSM_LANES), lambda b, c: (0, 0))
    d_spec = pl.BlockSpec((1, SSM_WIDTH), lambda b, c: (0, 0))

    def body(ab_ref, u_ref, dy_ref, xs_ref, wb_ref, wc_ref, d_ref, du_ref, da_ref, dwb_ref, dwc_ref,
             g_ref, k_ref, carry_ref, acc_ref):
        b, c = pl.program_id(0), pl.program_id(1)
        _scan_consts(ab_ref[0:1, :], -ab_ref[1:2, :], k_ref, True)
        row = lax.broadcasted_iota(jnp.int32, (8, SSM_LANES), 0)

        @pl.when(c == 0)
        def _():
            carry_ref[...] = jnp.zeros_like(carry_ref)

        @pl.when((c == 0) & (b == 0))
        def _():
            acc_ref[...] = jnp.zeros_like(acc_ref)
            dwb_ref[...] = jnp.zeros_like(dwb_ref)
            dwc_ref[...] = jnp.zeros_like(dwc_ref)

        dy = dy_ref[0]
        dyb = dy.astype(BF16)
        g_ref[...] = jnp.dot(dyb, wc_ref[...], preferred_element_type=F32)

        def step(i, carry):
            car, cai, ar_acc, ai_acc = carry
            base = pl.multiple_of((ch // 8 - 1 - i) * 8, 8)
            gr = g_ref[pl.ds(base, 8), 0:SSM_LANES]
            gi = g_ref[pl.ds(base, 8), SSM_LANES:]
            gr, gi = _scan_tile(gr, gi, k_ref, car, cai, True)
            g_ref[pl.ds(base, 8), 0:SSM_LANES] = gr
            g_ref[pl.ds(base, 8), SSM_LANES:] = gi
            nr = jnp.where(row == 7, car, pltpu.roll(gr, 7, 0))
            ni = jnp.where(row == 7, cai, pltpu.roll(gi, 7, 0))
            xr = xs_ref[0, pl.ds(base, 8), 0:SSM_LANES]
            xi = xs_ref[0, pl.ds(base, 8), SSM_LANES:]
            ar_acc = ar_acc + nr * xr + ni * xi
            ai_acc = ai_acc + ni * xr - nr * xi
            return (jnp.broadcast_to(gr[0:1], (8, SSM_LANES)), jnp.broadcast_to(gi[0:1], (8, SSM_LANES)), ar_acc, ai_acc)

        cr, ci, ar_acc, ai_acc = lax.fori_loop(0, ch // 8, step, (carry_ref[0], carry_ref[1], acc_ref[0], acc_ref[1]))
        carry_ref[0] = cr
        carry_ref[1] = ci
        acc_ref[0] = ar_acc
        acc_ref[1] = ai_acc
        da_ref[0:1, :] = jnp.sum(ar_acc, axis=0, keepdims=True)
        da_ref[1:2, :] = jnp.sum(ai_acc, axis=0, keepdims=True)

        gb = g_ref[...].astype(BF16)
        du = lax.dot_general(gb, wb_ref[...], NT_DIMS, preferred_element_type=F32) + d_ref[...] * dy
        du_ref[0] = du.astype(BF16)
        dwb_ref[...] += lax.dot_general(u_ref[0], gb, TN_DIMS, preferred_element_type=F32)
        dwc_ref[...] += lax.dot_general(dyb, xs_ref[0].astype(BF16), TN_DIMS, preferred_element_type=F32)

    mat = _sds((SSM_WIDTH, 2 * SSM_LANES), F32)
    return _pcall(body, name="ssm_scan_bwd",
                  out_shape=(_sds((B, S, SSM_WIDTH), BF16), _sds((2, SSM_LANES), F32), mat, mat), grid=(B, nc),
                  in_specs=[ab_spec, u_spec, y_spec, x_spec, w_spec, w_spec, d_spec],
                  out_specs=(y_spec, ab_spec, w_spec, w_spec),
                  scratch_shapes=[pltpu.VMEM((ch, 2 * SSM_LANES), F32), pltpu.VMEM((8, 8, SSM_LANES), F32),
                                  pltpu.VMEM((2, 8, SSM_LANES), F32), pltpu.VMEM((2, 8, SSM_LANES), F32)],
                  dims=("arbitrary", "arbitrary"))(abar, proj3, dy3, xs3, w_bu, w_c, dsk)


GELU_K = math.sqrt(2.0 / math.pi)
GELU_C = 0.044715


def _gelu_parts(y):
    t = jnp.tanh(GELU_K * (y + GELU_C * y * y * y))
    return 0.5 * y * (1.0 + t), t


def _ssm_post(yc, us, dsk, wglu, bglu):
    T, N = yc.shape
    tm = min(T, 1024)
    row = pl.BlockSpec((tm, N), lambda i: (i, 0))
    vec = pl.BlockSpec((1, N), lambda i: (0, 0))
    mat = pl.BlockSpec((N, N), lambda i: (0, 0))

    def body(yc_ref, us_ref, d_ref, w_ref, b_ref, y_ref, s_ref):
        y = yc_ref[...] + d_ref[...] * us_ref[...]
        y_ref[...] = y
        z, _ = _gelu_parts(y)
        gl = jnp.dot(z.astype(BF16), w_ref[...], preferred_element_type=F32) + b_ref[...]
        s_ref[...] = (z * _sig(gl)).astype(BF16)

    return _pcall(body, name="ssm_post", out_shape=(_sds((T, N), F32), _sds((T, N), BF16)), grid=(T // tm,),
                  in_specs=[row, row, vec, mat, vec], out_specs=(row, row), dims=("parallel",))(yc, us, dsk, wglu, bglu)


def _ssm_post_bwd(y5, us, ds, dsk, wglu, bglu):
    T, N = y5.shape
    tm = min(T, 1024)
    row = pl.BlockSpec((tm, N), lambda i: (i, 0))
    vec = pl.BlockSpec((1, N), lambda i: (0, 0))
    mat = pl.BlockSpec((N, N), lambda i: (0, 0))

    def body(y_ref, us_ref, ds_ref, d_ref, w_ref, b_ref, dy_ref, dd_ref, db_ref, dw_ref):
        @pl.when(pl.program_id(0) == 0)
        def _():
            dd_ref[...] = jnp.zeros_like(dd_ref)
            db_ref[...] = jnp.zeros_like(db_ref)
            dw_ref[...] = jnp.zeros_like(dw_ref)

        y = y_ref[...]
        z, t = _gelu_parts(y)
        zb = z.astype(BF16)
        gl = jnp.dot(zb, w_ref[...], preferred_element_type=F32) + b_ref[...]
        sg = _sig(gl)
        ds = ds_ref[...]
        dgl = ds * z * sg * (1.0 - sg)
        dglb = dgl.astype(BF16)
        dz = ds * sg + lax.dot_general(dglb, w_ref[...], (((1,), (1,)), ((), ())), preferred_element_type=F32)
        dgelu = 0.5 * (1.0 + t) + 0.5 * y * (1.0 - t * t) * GELU_K * (1.0 + 3.0 * GELU_C * y * y)
        dy = dz * dgelu
        dy_ref[...] = dy
        dd_ref[...] += jnp.sum(dy * us_ref[...], axis=0, keepdims=True)
        db_ref[...] += jnp.sum(dgl, axis=0, keepdims=True)
        dw_ref[...] += lax.dot_general(zb, dglb, (((0,), (0,)), ((), ())), preferred_element_type=F32)

    return _pcall(body, name="ssm_post_bwd",
                  out_shape=(_sds((T, N), F32), _sds((1, N), F32), _sds((1, N), F32), _sds((N, N), F32)),
                  grid=(T // tm,), in_specs=[row, row, row, vec, mat, vec], out_specs=(row, vec, vec, mat),
                  dims=("arbitrary",))(y5, us, ds, dsk, wglu, bglu)


def _add_scaled_cast(a, b, s):
    T, N = a.shape
    tm = min(T, 1024)
    row = pl.BlockSpec((tm, N), lambda i: (i, 0))

    def body(a_ref, b_ref, s_ref, o_ref):
        o_ref[...] = (a_ref[...] + s_ref[...] * b_ref[...]).astype(BF16)

    return _pcall(body, name="add_scaled_cast", out_shape=_sds((T, N), BF16), grid=(T // tm,),
                  in_specs=[row, row, pl.BlockSpec((1, N), lambda i: (0, 0))], out_specs=row, dims=("parallel",))(a, b, s)


GATE_TILE = 256
GATE_ATT_BLOCK0 = (3 * ATT_WIDTH + SSM_WIDTH) // GATE_TILE
GATE_SSM_BLOCK0 = (3 * ATT_WIDTH + SSM_WIDTH + D_MODEL) // GATE_TILE


def _merge(proj, y_att, y_ssm, b_gate):
    T = proj.shape[0]
    tm = min(T, 1024)
    nj = D_MODEL // GATE_TILE
    ga = pl.BlockSpec((tm, GATE_TILE), lambda i, j: (i, GATE_ATT_BLOCK0 + j))
    gs = pl.BlockSpec((tm, GATE_TILE), lambda i, j: (i, GATE_SSM_BLOCK0 + j))
    yy = pl.BlockSpec((tm, GATE_TILE), lambda i, j: (i, j))
    ba = pl.BlockSpec((1, GATE_TILE), lambda i, j: (0, j))
    bs = pl.BlockSpec((1, GATE_TILE), lambda i, j: (0, nj + j))

    def body(ga_ref, gs_ref, ya_ref, ys_ref, ba_ref, bs_ref, o_ref):
        o_ref[...] = (_sig(ga_ref[...] + ba_ref[...]) * ya_ref[...]
                      + _sig(gs_ref[...] + bs_ref[...]) * ys_ref[...]).astype(BF16)

    return _pcall(body, name="merge", out_shape=_sds((T, D_MODEL), BF16), grid=(T // tm, nj),
                  in_specs=[ga, gs, yy, yy, ba, bs], out_specs=yy, dims=("parallel", "parallel"))(
        proj, proj, y_att, y_ssm, b_gate, b_gate)


def _merge_bwd(proj, y_att, y_ssm, b_gate, dmerged):
    T = proj.shape[0]
    tm = min(T, 1024)
    nj = D_MODEL // GATE_TILE
    ga = pl.BlockSpec((tm, GATE_TILE), lambda j, i: (i, GATE_ATT_BLOCK0 + j))
    gs = pl.BlockSpec((tm, GATE_TILE), lambda j, i: (i, GATE_SSM_BLOCK0 + j))
    yy = pl.BlockSpec((tm, GATE_TILE), lambda j, i: (i, j))
    ba = pl.BlockSpec((1, GATE_TILE), lambda j, i: (0, j))
    bs = pl.BlockSpec((1, GATE_TILE), lambda j, i: (0, nj + j))

    def body(ga_ref, gs_ref, ya_ref, ys_ref, ba_ref, bs_ref, dm_ref, dya_ref, dys_ref, dga_ref, dgs_ref, dba_ref, dbs_ref):
        @pl.when(pl.program_id(1) == 0)
        def _():
            dba_ref[...] = jnp.zeros_like(dba_ref)
            dbs_ref[...] = jnp.zeros_like(dbs_ref)

        dm = dm_ref[...].astype(F32)
        sa = _sig(ga_ref[...] + ba_ref[...])
        ss = _sig(gs_ref[...] + bs_ref[...])
        dya_ref[...] = (dm * sa).astype(BF16)
        dys_ref[...] = (dm * ss).astype(BF16)
        dga = dm * ya_ref[...] * sa * (1.0 - sa)
        dgs = dm * ys_ref[...] * ss * (1.0 - ss)
        dga_ref[...] = dga.astype(BF16)
        dgs_ref[...] = dgs.astype(BF16)
        dba_ref[...] += jnp.sum(dga, axis=0, keepdims=True)
        dbs_ref[...] += jnp.sum(dgs, axis=0, keepdims=True)

    big = _sds((T, D_MODEL), BF16)
    vec = _sds((1, D_MODEL), F32)
    return _pcall(body, name="merge_bwd", out_shape=(big, big, big, big, vec, vec), grid=(nj, T // tm),
                  in_specs=[ga, gs, yy, yy, ba, bs, yy], out_specs=(yy, yy, yy, yy, ba, ba),
                  dims=("arbitrary", "arbitrary"))(proj, proj, y_att, y_ssm, b_gate, b_gate, dmerged)


CONV_TILE = 256


def _conv_pre(a, w_ref, b_ref, row):
    conv = b_ref[...] + w_ref[0:1, :] * a
    shifted = []
    for j in (1, 2):
        sh = jnp.where(row >= j, pltpu.roll(a, j, 0), 0.0)
        shifted.append(sh)
        conv = conv + w_ref[j:j + 1, :] * sh
    return conv, shifted


def _conv_act(up3, w_conv, b_conv):
    B, S, _ = up3.shape
    nj = D_FF // CONV_TILE
    a_spec = pl.BlockSpec((1, S, CONV_TILE), lambda b, j: (b, 0, j))
    v_spec = pl.BlockSpec((1, S, CONV_TILE), lambda b, j: (b, 0, nj + j))
    w_spec = pl.BlockSpec((3, CONV_TILE), lambda b, j: (0, j))
    b_spec = pl.BlockSpec((1, CONV_TILE), lambda b, j: (0, j))

    def body(a_ref, v_ref, w_ref, b_ref, o_ref):
        a = a_ref[0].astype(F32)
        row = lax.broadcasted_iota(jnp.int32, a.shape, 0)
        conv, _ = _conv_pre(a, w_ref, b_ref, row)
        o_ref[0] = (conv * _sig(conv) * v_ref[0]).astype(BF16)

    return _pcall(body, name="conv_act", out_shape=_sds((B, S, D_FF), BF16), grid=(B, nj),
                  in_specs=[a_spec, v_spec, w_spec, b_spec], out_specs=a_spec, dims=("parallel", "parallel"))(
        up3, up3, w_conv, b_conv)


def _conv_bwd(up3, dact3, w_conv, b_conv):
    B, S, _ = up3.shape
    nj = D_FF // CONV_TILE
    a_spec = pl.BlockSpec((1, S, CONV_TILE), lambda j, b: (b, 0, j))
    v_spec = pl.BlockSpec((1, S, CONV_TILE), lambda j, b: (b, 0, nj + j))
    o_spec = pl.BlockSpec((2, 1, S, CONV_TILE), lambda j, b: (0, b, 0, j))
    w_spec = pl.BlockSpec((3, CONV_TILE), lambda j, b: (0, j))
    b_spec = pl.BlockSpec((1, CONV_TILE), lambda j, b: (0, j))

    def body(a_ref, v_ref, d_ref, w_ref, b_ref, dup_ref, dw_ref, db_ref):
        @pl.when(pl.program_id(1) == 0)
        def _():
            dw_ref[...] = jnp.zeros_like(dw_ref)
            db_ref[...] = jnp.zeros_like(db_ref)

        a = a_ref[0].astype(F32)
        d = d_ref[0].astype(F32)
        row = lax.broadcasted_iota(jnp.int32, a.shape, 0)
        conv, shifted = _conv_pre(a, w_ref, b_ref, row)
        sg = _sig(conv)
        dup_ref[1, 0] = (d * conv * sg).astype(BF16)
        dconv = d * v_ref[0] * (sg * (1.0 + conv * (1.0 - sg)))
        da = w_ref[0:1, :] * dconv
        for j in (1, 2):
            da = da + w_ref[j:j + 1, :] * jnp.where(row < S - j, pltpu.roll(dconv, S - j, 0), 0.0)
        dup_ref[0, 0] = da.astype(BF16)
        db_ref[...] += jnp.sum(dconv, axis=0, keepdims=True)
        dw_ref[0:1, :] += jnp.sum(dconv * a, axis=0, keepdims=True)
        dw_ref[1:2, :] += jnp.sum(dconv * shifted[0], axis=0, keepdims=True)
        dw_ref[2:3, :] += jnp.sum(dconv * shifted[1], axis=0, keepdims=True)

    return _pcall(body, name="conv_bwd",
                  out_shape=(_sds((2, B, S, D_FF), BF16), _sds((3, D_FF), F32), _sds((1, D_FF), F32)),
                  grid=(nj, B), in_specs=[a_spec, v_spec, a_spec, w_spec, b_spec],
                  out_specs=(o_spec, w_spec, b_spec), dims=("arbitrary", "arbitrary"))(up3, up3, dact3, w_conv, b_conv)


def _rows_tile(r, cap=640):
    for t in range(min(r, cap) - min(r, cap) % 8, 7, -8):
        if r % t == 0:
            return t
    return r


def _add2(a, b, out_dtype, name):
    R, N = a.shape
    tr = _rows_tile(R)
    spec = pl.BlockSpec((tr, N), lambda i: (i, 0))

    def body(a_ref, b_ref, o_ref):
        o_ref[...] = (a_ref[...] + b_ref[...]).astype(out_dtype)

    return _pcall(body, name=name, out_shape=_sds((R, N), out_dtype), grid=(R // tr,), in_specs=[spec, spec],
                  out_specs=spec, dims=("parallel",))(a, b)


def _sum_slots(q, name):
    n, R, N = q.shape
    tr = _rows_tile(R)

    def body(q_ref, o_ref):
        acc = q_ref[0].astype(F32)
        for s in range(1, n):
            acc = acc + q_ref[s].astype(F32)
        o_ref[...] = acc

    return _pcall(body, name=name, out_shape=_sds((R, N), F32), grid=(R // tr,),
                  in_specs=[pl.BlockSpec((n, tr, N), lambda i: (0, i, 0))], out_specs=pl.BlockSpec((tr, N), lambda i: (i, 0)),
                  dims=("parallel",))(q)


NATIVE = (("b_re", 16, 1024), ("b_im", 16, 1024), ("c_re", 16, 1024), ("c_im", 16, 1024), ("g_mix", 1, 1024),
          ("b_att", 1, 1024), ("b_ssm", 1, 1024), ("a_re", 1, 1024), ("a_im", 1, 1024), ("log_dt", 1, 128),
          ("d_skip", 1, 256), ("b_glu", 1, 256), ("g_ffn", 1, 1024), ("g_final", 1, 1024), ("b_conv", 1, 2048),
          ("w_conv", 3, 2048), ("loss", 1, 1))
N_MOD = 6
NATIVE_LATE = ("g_mix",)
MODS_LATE = (0, 1)


def _small_plan(late):
    pieces = [p for p in NATIVE if (p[0] in NATIVE_LATE) == late]
    mods = [k for k in range(N_MOD) if (k in MODS_LATE) == late]
    starts, r = {}, 0
    for name, rows, cols in pieces:
        starts[name] = r
        r += rows * (-(-cols // LANES))
    return pieces, mods, starts, -(-r // 8) * 8


def _pack_small(native, dmods, late):
    pieces, mods, starts, n_sum = _small_plan(late)
    B = dmods[mods[0]].shape[0]
    total = n_sum + 8 * len(mods)

    def body(*refs):
        xs, ms, o_ref = refs[:len(pieces)], refs[len(pieces):-1], refs[-1]
        o_ref[...] = jnp.zeros_like(o_ref)
        for (name, rows, cols), x_ref in zip(pieces, xs):
            chunks = -(-cols // LANES)
            if chunks == 1 and rows % 8 == 0:
                o_ref[starts[name]:starts[name] + rows, 0:cols] = x_ref[...]
                continue
            for i in range(rows):
                for q in range(chunks):
                    wd = min(LANES, cols - q * LANES)
                    r = starts[name] + i * chunks + q
                    o_ref[r:r + 1, 0:wd] = x_ref[i:i + 1, q * LANES:q * LANES + wd]
        for k, m_ref in enumerate(ms):
            for b in range(B):
                o_ref[n_sum + 8 * k + b:n_sum + 8 * k + b + 1, :] = m_ref[b]

    return _pcall(body, name="pack_small_late" if late else "pack_small_early", out_shape=_sds((total, LANES), F32))(
        *[native[n] for n, _, _ in pieces], *[dmods[k] for k in mods])


def _sum_unpack_small(gathered_early, gathered_late, B):
    plans = [_small_plan(False), _small_plan(True)]
    nd = gathered_early.shape[0]
    n_out = len(NATIVE)

    def body(*refs):
        g_refs, outs, dm_ref, accs = refs[0:2], refs[2:2 + n_out], refs[2 + n_out], refs[3 + n_out:]
        o = 0
        for g_ref, acc, (pieces, mods, starts, n_sum) in zip(g_refs, accs, plans):
            s = g_ref[0, 0:n_sum, :]
            for d in range(1, nd):
                s = s + g_ref[d, 0:n_sum, :]
            acc[...] = s
            for name, rows, cols in pieces:
                o_ref = outs[o]
                o += 1
                chunks = -(-cols // LANES)
                if chunks == 1 and rows % 8 == 0:
                    o_ref[...] = acc[starts[name]:starts[name] + rows, 0:cols]
                    continue
                for i in range(rows):
                    for q in range(chunks):
                        wd = min(LANES, cols - q * LANES)
                        r = starts[name] + i * chunks + q
                        o_ref[i:i + 1, q * LANES:q * LANES + wd] = acc[r:r + 1, 0:wd]
            for d in range(nd):
                for j, k in enumerate(mods):
                    dm_ref[d, :, k * D_MODEL:(k + 1) * D_MODEL] = g_ref[d, n_sum + 8 * j:n_sum + 8 * j + B, :]

    ordered = [p for pieces, _, _, _ in plans for p in pieces]
    out_shape = tuple(_sds((rows, cols), F32) for _, rows, cols in ordered) + (_sds((nd, B, N_MOD * D_MODEL), F32),)
    res = _pcall(body, name="sum_unpack_small", out_shape=out_shape,
                 scratch_shapes=[pltpu.VMEM((n_sum, LANES), F32) for _, _, _, n_sum in plans])(gathered_early, gathered_late)
    return {n: r for (n, _, _), r in zip(ordered, res[:-1])}, res[-1]


def _small_from_native(nat):
    lanes3 = lambda a: a.reshape(SSM_GROUP_CH, SSM_GROUPS, SSM_STATE)
    return dict(
        g_mix=nat["g_mix"].reshape(D_MODEL), b_gate=jnp.concatenate([nat["b_att"], nat["b_ssm"]], axis=1).reshape(2 * D_MODEL),
        a_re=nat["a_re"].reshape(SSM_GROUPS, SSM_STATE), a_im=nat["a_im"].reshape(SSM_GROUPS, SSM_STATE),
        log_dt=nat["log_dt"][0, :SSM_GROUPS], b_re=_groups_from_lanes(nat["b_re"]), b_im=_groups_from_lanes(nat["b_im"]),
        c_re=lanes3(nat["c_re"]).transpose(1, 0, 2), c_im=lanes3(nat["c_im"]).transpose(1, 0, 2),
        d_skip=nat["d_skip"].reshape(SSM_WIDTH), b_glu=nat["b_glu"].reshape(SSM_WIDTH), g_ffn=nat["g_ffn"].reshape(D_MODEL),
        w_conv=nat["w_conv"], b_conv=nat["b_conv"].reshape(D_FF), g_final=nat["g_final"].reshape(D_MODEL))


def _adamw_multi(params):
    n = len(params)
    bc1 = 1.0 - ADAM_B1 ** ADAM_STEP
    bc2 = 1.0 - ADAM_B2 ** ADAM_STEP

    def body(*refs):
        ins, outs = refs[:4 * n], refs[4 * n:]
        for i in range(n):
            w_ref, g_ref, m_ref, v_ref = ins[4 * i:4 * i + 4]
            d_ref, nm_ref, nv_ref = outs[3 * i:3 * i + 3]
            g = g_ref[...]
            m = ADAM_B1 * m_ref[...] + (1.0 - ADAM_B1) * g
            v = ADAM_B2 * v_ref[...] + (1.0 - ADAM_B2) * (g * g)
            nm_ref[...] = m
            nv_ref[...] = v
            d_ref[...] = -ADAM_LR * ((m / bc1) / (jnp.sqrt(v / bc2) + ADAM_EPS) + ADAM_WD * w_ref[...])

    flat = [a for p in params for a in p]
    out_shape = tuple(_sds(p[0].shape, F32) for p in params for _ in range(3))
    res = _pcall(body, name="adamw_small", out_shape=out_shape)(*flat)
    return [tuple(res[3 * i:3 * i + 3]) for i in range(n)]


def _adamw(w, g, m, v, name):
    R, N = w.shape
    tr = _rows_tile(R) if R * N * 4 > (1 << 20) else R
    tr = min(tr, 256) if R % 256 == 0 and R > 256 else tr
    spec = pl.BlockSpec((tr, N), lambda i: (i, 0))
    bc1 = 1.0 - ADAM_B1 ** ADAM_STEP
    bc2 = 1.0 - ADAM_B2 ** ADAM_STEP

    def body(w_ref, g_ref, m_ref, v_ref, d_ref, nm_ref, nv_ref):
        g = g_ref[...]
        m = ADAM_B1 * m_ref[...] + (1.0 - ADAM_B1) * g
        v = ADAM_B2 * v_ref[...] + (1.0 - ADAM_B2) * (g * g)
        nm_ref[...] = m
        nv_ref[...] = v
        d_ref[...] = -ADAM_LR * ((m / bc1) / (jnp.sqrt(v / bc2) + ADAM_EPS) + ADAM_WD * w_ref[...])

    shp = _sds((R, N), F32)
    return _pcall(body, name=name, out_shape=(shp, shp, shp), grid=(R // tr,), in_specs=[spec] * 4,
                  out_specs=(spec, spec, spec), dims=("parallel",))(w, g, m, v)


_GROUP_MASKS = {
    "all": [(dx, dy, dc) for dx in (0, 1) for dy in (0, 1) for dc in (0, 1) if (dx, dy, dc) != (0, 0, 0)],
    "xy": [(1, 0, 0), (0, 1, 0), (1, 1, 0)],
    "c": [(0, 0, 1)],
}
_GROUP_SLOTS = {"all": 8, "xy": 4, "c": 2}


def _group_slot(group, x, y, c):
    return {"all": 4 * x + 2 * y + c, "xy": 2 * x + y, "c": c}[group]


def _flip(v, d):
    return 1 - v if d else v


def _exchange(arr, group, mode, name):
    return _exchange_list([arr], group, mode, name)[0]


def _exchange_list(arrs, group, mode, name):
    masks = _GROUP_MASKS[group]
    n = len(masks)
    na = len(arrs)
    out_shapes, halves, bounce = [], [], []
    for arr in arrs:
        if mode == "gather":
            out_shapes.append((_GROUP_SLOTS[group],) + arr.shape)
            bounce.append(pltpu.VMEM(arr.shape, arr.dtype))
        elif mode == "scatter":
            assert arr.shape[0] == _GROUP_SLOTS[group]
            out_shapes.append(arr.shape)
            bounce.append(pltpu.VMEM(arr.shape[1:], arr.dtype))
        elif mode == "swap":
            assert group == "c"
            out_shapes.append(arr.shape)
        else:
            assert group == "c"
            halves.append(arr.shape[1] // 2)
            out_shapes.append((arr.shape[0], arr.shape[1] // 2, arr.shape[2]))
    has_local = mode in ("gather", "scatter")

    def body(*refs):
        x_refs, o_refs = refs[:na], refs[na:2 * na]
        send_sems, recv_sems = refs[2 * na], refs[2 * na + 1]
        x, y, c = lax.axis_index("x"), lax.axis_index("y"), lax.axis_index("c")
        me = _group_slot(group, x, y, c)
        if has_local:
            local_sems = refs[2 * na + 2]
            bufs = refs[2 * na + 3:]
            loads = []
            for i in range(na):
                src = x_refs[i] if mode == "gather" else x_refs[i].at[me]
                loads.append(pltpu.make_async_copy(src, bufs[i], local_sems.at[2 * i]))
                loads[-1].start()
        copies = []
        for i in range(na):
            x_ref, o_ref = x_refs[i], o_refs[i]
            for k, (dx, dy, dc) in enumerate(masks):
                px, py, pc = _flip(x, dx), _flip(y, dy), _flip(c, dc)
                if mode == "gather":
                    src, dst = x_ref, o_ref.at[me]
                elif mode == "scatter":
                    src, dst = x_ref.at[_group_slot(group, px, py, pc)], o_ref.at[me]
                elif mode == "swap":
                    src, dst = x_ref, o_ref
                else:
                    src, dst = x_ref.at[:, pl.ds(pl.multiple_of(pc * halves[i], 8), halves[i]), :], o_ref
                cp = pltpu.make_async_remote_copy(src_ref=src, dst_ref=dst, send_sem=send_sems.at[i * n + k],
                                                  recv_sem=recv_sems.at[i * n + k], device_id=(px, py, pc),
                                                  device_id_type=pl.DeviceIdType.MESH)
                cp.start()
                copies.append(cp)
        if has_local:
            stores = []
            for i in range(na):
                loads[i].wait()
                stores.append(pltpu.make_async_copy(bufs[i], o_refs[i].at[me], local_sems.at[2 * i + 1]))
                stores[-1].start()
        for cp in copies:
            cp.wait()
        if has_local:
            for st in stores:
                st.wait()

    anyspec = pl.BlockSpec(memory_space=pl.ANY)
    scratch = [pltpu.SemaphoreType.DMA((n * na,)), pltpu.SemaphoreType.DMA((n * na,))]
    if has_local:
        scratch += [pltpu.SemaphoreType.DMA((2 * na,))] + bounce
    outs = pl.pallas_call(body, name=name, out_shape=tuple(_sds(s, a.dtype) for s, a in zip(out_shapes, arrs)),
                          in_specs=[anyspec] * na, out_specs=tuple([anyspec] * na), scratch_shapes=scratch,
                          compiler_params=pltpu.CompilerParams(vmem_limit_bytes=V7X_VMEM_LIMIT_BYTES))(*arrs)
    return list(outs)


def _gather_weights(shards, name):
    na = len(shards)
    masks = _GROUP_MASKS["xy"]
    n = len(masks)

    def body(*refs):
        x_refs, o_refs = refs[:na], refs[na:2 * na]
        send_sems, recv_sems, local_sems = refs[2 * na:2 * na + 3]
        bufs = refs[2 * na + 3:]
        x, y, c = lax.axis_index("x"), lax.axis_index("y"), lax.axis_index("c")
        me = 2 * x + y
        sibling = (x, y, 1 - c)
        loads = []
        for i in range(na):
            loads.append(pltpu.make_async_copy(x_refs[i], bufs[i], local_sems.at[2 * i]))
            loads[-1].start()

        def half_of(i, slot, cc):
            h = shards[i].shape[0] // 2
            return o_refs[i].at[slot, pl.ds(pl.multiple_of(cc * h, 8), h), :]

        def src_half(i, cc):
            h = shards[i].shape[0] // 2
            return x_refs[i].at[pl.ds(pl.multiple_of(cc * h, 8), h), :]

        sends = []
        for i in range(na):
            for k, (dx, dy, _) in enumerate(masks):
                cp = pltpu.make_async_remote_copy(src_ref=src_half(i, c), dst_ref=half_of(i, me, c),
                                                  send_sem=send_sems.at[i * 2 * n + k], recv_sem=recv_sems.at[i * 2 * n + k],
                                                  device_id=(_flip(x, dx), _flip(y, dy), c),
                                                  device_id_type=pl.DeviceIdType.MESH)
                cp.start()
                sends.append(cp)
        stores = []
        for i in range(na):
            loads[i].wait()
            stores.append(pltpu.make_async_copy(bufs[i], o_refs[i].at[me], local_sems.at[2 * i + 1]))
            stores[-1].start()
        for i in range(na):
            for k, (dx, dy, _) in enumerate(masks):
                slot = 2 * _flip(x, dx) + _flip(y, dy)
                landed = pltpu.make_async_remote_copy(src_ref=src_half(i, c), dst_ref=half_of(i, slot, c),
                                                      send_sem=send_sems.at[i * 2 * n + k],
                                                      recv_sem=recv_sems.at[i * 2 * n + k], device_id=sibling,
                                                      device_id_type=pl.DeviceIdType.MESH)
                landed.wait_recv()
                fwd = pltpu.make_async_remote_copy(src_ref=half_of(i, slot, c), dst_ref=half_of(i, slot, c),
                                                   send_sem=send_sems.at[i * 2 * n + n + k],
                                                   recv_sem=recv_sems.at[i * 2 * n + n + k], device_id=sibling,
                                                   device_id_type=pl.DeviceIdType.MESH)
                fwd.start()
                sends.append(fwd)
        for i in range(na):
            for k, (dx, dy, _) in enumerate(masks):
                slot = 2 * _flip(x, dx) + _flip(y, dy)
                pltpu.make_async_remote_copy(src_ref=half_of(i, slot, 1 - c), dst_ref=half_of(i, slot, 1 - c),
                                             send_sem=send_sems.at[i * 2 * n + n + k],
                                             recv_sem=recv_sems.at[i * 2 * n + n + k], device_id=sibling,
                                             device_id_type=pl.DeviceIdType.MESH).wait_recv()
        for cp in sends:
            cp.wait_send()
        for st in stores:
            st.wait()

    anyspec = pl.BlockSpec(memory_space=pl.ANY)
    scratch = [pltpu.SemaphoreType.DMA((2 * n * na,)), pltpu.SemaphoreType.DMA((2 * n * na,)),
               pltpu.SemaphoreType.DMA((2 * na,))] + [pltpu.VMEM(s.shape, s.dtype) for s in shards]
    outs = pl.pallas_call(body, name=name, out_shape=tuple(_sds((N_XY,) + s.shape, s.dtype) for s in shards),
                          in_specs=[anyspec] * na, out_specs=tuple([anyspec] * na), scratch_shapes=scratch,
                          compiler_params=pltpu.CompilerParams(vmem_limit_bytes=V7X_VMEM_LIMIT_BYTES))(*shards)
    return list(outs)


def _pair_add(g, theirs, core, name):
    n4, h2, w = g.shape
    h = h2 // 2
    tr = _rows_tile(h)
    nb = h // tr

    def body(c_ref, g_ref, t_ref, o_ref):
        o_ref[...] = (g_ref[...] + t_ref[...]).astype(BF16)

    grid_spec = pltpu.PrefetchScalarGridSpec(
        num_scalar_prefetch=1, grid=(n4, nb),
        in_specs=[pl.BlockSpec((None, tr, w), lambda j, i, c_ref: (j, c_ref[0] * nb + i, 0)),
                  pl.BlockSpec((None, tr, w), lambda j, i, c_ref: (j, i, 0))],
        out_specs=pl.BlockSpec((None, tr, w), lambda j, i, c_ref: (j, i, 0)))
    return pl.pallas_call(body, name=name, out_shape=_sds((n4, h, w), BF16), grid_spec=grid_spec,
                          compiler_params=pltpu.CompilerParams(vmem_limit_bytes=V7X_VMEM_LIMIT_BYTES,
                                                               dimension_semantics=("parallel", "parallel")))(core, g, theirs)


BIG = (("w_proj_att", (ATT_WIDTH, D_MODEL), 1), ("w_proj_ssm", (SSM_WIDTH, D_MODEL), 1),
       ("w_glu", (SSM_WIDTH, SSM_WIDTH), 0))
DIRECT = (("w_in", True), ("w_up", True), ("w_down", False), ("w_out", False))
N_XY = 4


def _big_rows(shape):
    return shape[0] * shape[1] // N_XY // LANES


FLAT_ROWS = sum(_big_rows(s) for _, s, _ in BIG)


def _shard_shape(shape, axis):
    return (shape[0] // N_XY, shape[1]) if axis == 0 else (shape[0], shape[1] // N_XY)


def _flatten_shards(shards):
    return jnp.concatenate([shards[n].reshape(_big_rows(s), LANES) for n, s, _ in BIG], axis=0)


def _unflatten_shard(flat):
    out, r = {}, 0
    for n, s, ax in BIG:
        k = _big_rows(s)
        out[n] = flat[r:r + k].reshape(_shard_shape(s, ax))
        r += k
    return out


def _unflatten_full(flat4):
    out, r = {}, 0
    for n, s, ax in BIG:
        k = _big_rows(s)
        sh = _shard_shape(s, ax)
        t = flat4[:, r:r + k].reshape((N_XY,) + sh)
        out[n] = t.reshape(s) if ax == 0 else t.transpose(1, 0, 2).reshape(s)
        r += k
    return out


def _flatten_full(full):
    parts = []
    for n, s, ax in BIG:
        sh = _shard_shape(s, ax)
        t = full[n]
        t = t.reshape((N_XY,) + sh) if ax == 0 else t.reshape(s[0], N_XY, sh[1]).transpose(1, 0, 2)
        parts.append(t.reshape(N_XY, _big_rows(s), LANES))
    return jnp.concatenate(parts, axis=1)


def _pack_rows(arrs):
    rows, counts = [], []
    for a in arrs:
        f = a.reshape(-1)
        k = -(-f.shape[0] // LANES)
        rows.append(jnp.pad(f, (0, k * LANES - f.shape[0])).reshape(k, LANES))
        counts.append(k)
    return jnp.concatenate(rows, axis=0), counts


def _unpack_rows(buf, shapes):
    out, r = [], 0
    for s in shapes:
        size = int(np.prod(s))
        k = -(-size // LANES)
        out.append(buf[r:r + k].reshape(-1)[:size].reshape(s))
        r += k
    return out


def _lanes_from_groups(a):
    return a.transpose(2, 0, 1).reshape(SSM_GROUP_CH, SSM_LANES)


def _groups_from_lanes(a):
    return a.reshape(SSM_GROUP_CH, SSM_GROUPS, SSM_STATE).transpose(1, 2, 0)


LATE = ("w_up_t", "w_down", "w_out")
EARLY_GRADS = ("w_up_t", "w_down", "w_out")


def _local_step(x3, mod, tgt3, W, P, late_shards=None, scatter_grads=False):
    B, S, _ = x3.shape
    T = B * S
    seq_blocks = S // ATT_BLOCK
    sh1, sc1, gt1, sh2, sc2, gt2 = [m.reshape(B, 1, D_MODEL) for m in jnp.split(mod, 6, axis=-1)]
    g_mix, g_ffn, g_final = P["g_mix"].reshape(1, D_MODEL), P["g_ffn"].reshape(1, D_MODEL), P["g_final"].reshape(1, D_MODEL)
    b_gate = P["b_gate"].reshape(1, 2 * D_MODEL)
    d_skip, b_glu = P["d_skip"].reshape(1, SSM_WIDTH), P["b_glu"].reshape(1, SSM_WIDTH)
    w_conv, b_conv = P["w_conv"], P["b_conv"].reshape(1, D_FF)

    u1 = _norm_mod(x3, g_mix, sc1, sh1).reshape(T, D_MODEL)
    proj = _mm(u1, W["w_in_t"], tb=True, name="mm_proj", out_dtype=BF16)
    proj3 = proj.reshape(B, S, IN_WIDTH)
    us = proj[:, 3 * ATT_WIDTH:3 * ATT_WIDTH + SSM_WIDTH]
    o_att3, lse4, late = _attention_fwd(proj3, seq_blocks, _Riders(late_shards, "gather") if late_shards else None)
    if late_shards:
        W = dict(W, **{n: f.reshape(-1, LANES) for n, f in zip(LATE, late)})
        w_conv = late[len(LATE)].transpose(1, 0, 2).reshape(3, D_FF)
    o_att = o_att3.reshape(T, ATT_WIDTH)
    y_att = _mm(o_att, W["w_proj_att"], name="mm_proj_att", out_dtype=BF16)

    lr = P["a_re"].reshape(1, SSM_LANES)
    li = P["a_im"].reshape(1, SSM_LANES)
    ldt = jnp.repeat(P["log_dt"], SSM_STATE).reshape(1, SSM_LANES)
    br, bi = _lanes_from_groups(P["b_re"]), _lanes_from_groups(P["b_im"])
    cr = P["c_re"].transpose(1, 0, 2).reshape(SSM_GROUP_CH, SSM_LANES)
    ci = P["c_im"].transpose(1, 0, 2).reshape(SSM_GROUP_CH, SSM_LANES)
    abar, w_bu, w_c = _ssm_params(lr, li, ldt, br, bi, cr, ci)
    xs3, y_core3 = _ssm_scan_fwd(proj3, abar, w_bu, w_c)
    y5, s_out = _ssm_post(y_core3.reshape(T, SSM_WIDTH), us, d_skip, W["w_glu"], b_glu)
    y_ssm = _mm(s_out, W["w_proj_ssm"], name="mm_proj_ssm", out_dtype=BF16)

    merged = _merge(proj, y_att, y_ssm, b_gate)
    mix = _mm(merged, W["w_out"], name="mm_out", out_dtype=BF16)
    mix3 = mix.reshape(B, S, D_MODEL)

    h1, u2 = _resid_norm_mod(x3, mix3, gt1, g_ffn, sc2, sh2)
    u2 = u2.reshape(T, D_MODEL)
    up3 = _mm(u2, W["w_up_t"], tb=True, name="mm_up", out_dtype=BF16).reshape(B, S, 2 * D_FF)
    act = _conv_act(up3, w_conv, b_conv).reshape(T, D_FF)
    ffn3 = _mm(act, W["w_down"], name="mm_down", out_dtype=BF16).reshape(B, S, D_MODEL)
    dh2, dffn, dgt2, dg_final, loss = _final_loss(h1, ffn3, tgt3, gt2, g_final)

    dffn = dffn.reshape(T, D_MODEL)
    gw = {}
    gw["w_down"] = _mm(act, dffn, ta=True, out_dtype=BF16, name="mm_dw_down")
    dact3 = _mm(dffn, W["w_down"], tb=True, name="mm_dact", out_dtype=BF16).reshape(B, S, D_FF)
    dup3, dw_conv, db_conv = _conv_bwd(up3, dact3, w_conv, b_conv)
    dup = dup3.reshape(2, T, D_FF)
    gw["w_up_t"] = _mm(dup, u2, ta=True, out_dtype=BF16, name="mm_dw_up")
    du2 = _mm(dup, W["w_up_t"], name="mm_du2", out_dtype=BF16).reshape(B, S, D_MODEL)
    dh1, dsh2, dsc2, dg_ffn, dgt1, dmix = _norm_bwd(h1, du2, dh2, g_ffn, sc2, "norm_bwd2", mix3=mix3, gt=gt1)

    dmix = dmix.reshape(T, D_MODEL)
    gw["w_out"] = _mm(merged, dmix, ta=True, out_dtype=BF16, name="mm_dw_out")
    dmerged = _mm(dmix, W["w_out"], tb=True, name="mm_dmerged", out_dtype=BF16)
    dy_att, dy_ssm, dga, dgs, db_att, db_ssm = _merge_bwd(proj, y_att, y_ssm, b_gate, dmerged)

    gw["w_proj_ssm"] = _mm(s_out, dy_ssm, ta=True, name="mm_dw_proj_ssm")
    ds_out = _mm(dy_ssm, W["w_proj_ssm"], tb=True, name="mm_ds_out")
    dy5, dd_skip, db_glu, dw_glu = _ssm_post_bwd(y5, us, ds_out, d_skip, W["w_glu"], b_glu)
    gw["w_glu"] = dw_glu
    dus3, dab, dwbu, dwc = _ssm_scan_bwd(proj3, dy5.reshape(B, S, SSM_WIDTH), xs3, abar, w_bu, w_c, d_skip)
    dus = dus3.reshape(T, SSM_WIDTH)
    dlr, dli, dldt, dbr, dbi, dcr, dci = _ssm_params_bwd(lr, li, ldt, br, bi, dab, dwbu, dwc)

    gw["w_proj_att"] = _mm(o_att, dy_att, ta=True, name="mm_dw_proj_att")
    do_att = _mm(dy_att, W["w_proj_att"], tb=True, out_dtype=BF16, name="mm_do_att")
    early = [gw[n].reshape(N_XY, -1, LANES) for n in EARLY_GRADS]
    early.append(_flatten_full({n: gw[n].astype(BF16) for n, _, _ in BIG}))
    dq3, dk3, dv3, parts = _attention_bwd(proj3, do_att.reshape(B, S, ATT_WIDTH), o_att3, lse4, seq_blocks,
                                          _Riders(early, "scatter") if scatter_grads else None)
    dproj = jnp.concatenate([t.reshape(T, ATT_WIDTH) for t in (dq3, dk3, dv3)] + [dus, dga, dgs], axis=1)
    dmods = [None, None, dgt1, dsh2, dsc2, dgt2]
    native = dict(b_att=db_att, b_ssm=db_ssm, a_re=dlr, a_im=dli, log_dt=dldt, b_re=dbr, b_im=dbi, c_re=dcr, c_im=dci,
                  d_skip=dd_skip, b_glu=db_glu, g_ffn=dg_ffn, w_conv=dw_conv, b_conv=db_conv, g_final=dg_final, loss=loss)
    small_early = _pack_small(native, dmods, False)
    if scatter_grads:
        gw["w_in_t"], (small_early,) = _mm(dproj, u1, ta=True, out_dtype=BF16, name="mm_dw_in",
                                           riders=_Riders([small_early], "gather", "all"))
        du1, last_parts = _mm(dproj, W["w_in_t"], name="mm_du1", out_dtype=BF16,
                              riders=_Riders([gw["w_in_t"].reshape(N_XY, -1, LANES)], "scatter"))
        parts = parts + last_parts
    else:
        gw["w_in_t"] = _mm(dproj, u1, ta=True, out_dtype=BF16, name="mm_dw_in")
        du1 = _mm(dproj, W["w_in_t"], name="mm_du1", out_dtype=BF16)
    du1 = du1.reshape(B, S, D_MODEL)
    dx, dsh1, dsc1, dg_mix = _norm_bwd(x3, du1, dh1, g_mix, sc1, "norm_bwd1")
    dmods[0], dmods[1] = dsh1, dsc1
    native["g_mix"] = dg_mix
    return loss, dx, dmods, gw, native, parts, small_early


WEIGHTS = ['w_ada', 'b_ada', 'g_mix', 'w_in', 'b_gate', 'a_re', 'a_im', 'log_dt', 'b_re', 'b_im', 'c_re', 'c_im', 'd_skip',
           'w_glu', 'b_glu', 'w_proj_att', 'w_proj_ssm', 'w_out', 'g_ffn', 'w_up', 'w_conv', 'b_conv', 'w_down', 'g_final']
SMALL = ['g_mix', 'b_gate', 'a_re', 'a_im', 'log_dt', 'b_re', 'b_im', 'c_re', 'c_im', 'd_skip', 'b_glu', 'g_ffn', 'w_conv',
         'b_conv', 'g_final']


def kernel(x, c, w_ada, b_ada, g_mix, w_in, b_gate, a_re, a_im, log_dt, b_re, b_im, c_re, c_im, d_skip, w_glu, b_glu, w_proj_att, w_proj_ssm, w_out, g_ffn, w_up, w_conv, b_conv, w_down, g_final, loss_target, m_w_ada, m_b_ada, m_g_mix, m_w_in, m_b_gate, m_a_re, m_a_im, m_log_dt, m_b_re, m_b_im, m_c_re, m_c_im, m_d_skip, m_w_glu, m_b_glu, m_w_proj_att, m_w_proj_ssm, m_w_out, m_g_ffn, m_w_up, m_w_conv, m_b_conv, m_w_down, m_g_final, v_w_ada, v_b_ada, v_g_mix, v_w_in, v_b_gate, v_a_re, v_a_im, v_log_dt, v_b_re, v_b_im, v_c_re, v_c_im, v_d_skip, v_w_glu, v_b_glu, v_w_proj_att, v_w_proj_ssm, v_w_out, v_g_ffn, v_w_up, v_w_conv, v_b_conv, v_w_down, v_g_final):
    args = dict(locals())
    w = {n: args[n] for n in WEIGHTS}
    m = {n: args["m_" + n] for n in WEIGHTS}
    v = {n: args["v_" + n] for n in WEIGHTS}
    B, S, _ = x.shape
    ix, iy, ic = lax.axis_index("x"), lax.axis_index("y"), lax.axis_index("c")
    chip = 2 * ix + iy
    half = FLAT_ROWS // 2
    ada_cols = w_ada.shape[2]

    c_all = _exchange(c, "all", "gather", "gather_c").reshape(8 * B, D_MODEL)
    b_cols = lax.dynamic_slice_in_dim(b_ada, chip * ada_cols, ada_cols, axis=1)
    mod_cols = _ada_fwd(c_all, w_ada[0], b_cols)
    mod_all = _exchange(mod_cols, "xy", "gather", "gather_mod")
    mod_all = mod_all.transpose(1, 0, 2).reshape(8 * B, 6 * D_MODEL)
    mod = lax.dynamic_slice_in_dim(mod_all, (4 * ix + 2 * iy + ic) * B, B, axis=0)

    south = ic == 0
    core = ic.astype(jnp.int32).reshape(1)
    shard = {n + ("_t" if t else ""): (w[n][0].T if t else w[n][0]).astype(BF16) for n, t in DIRECT}
    misc = _flatten_shards({n: w[n][0] for n, _, _ in BIG}).astype(BF16)
    w_in_full, misc_full = _gather_weights([shard["w_in_t"], misc], "gather_weights")
    W = {"w_in_t": w_in_full.reshape(-1, LANES)}
    W.update(_unflatten_full(misc_full))

    P = {n: w[n][0] for n in SMALL if n not in ("w_conv", "g_final")}
    P["w_conv"] = None
    P["g_final"] = g_final

    loss, dx, dmods, gw, native, parts, small_early = _local_step(x, mod, loss_target, W, P,
                                                                  [shard[n] for n in LATE] + [w_conv[0]], True)

    small_late = _exchange(_pack_small(native, dmods, True), "all", "gather", "gather_small")
    native_sum, dmod_all = _sum_unpack_small(small_early, small_late, B)
    loss = native_sum["loss"][0, 0]
    g_small = _small_from_native(native_sum)
    dmod_all = dmod_all.reshape(8 * B, N_MOD * D_MODEL)
    dmod_cols = lax.dynamic_slice_in_dim(dmod_all, chip * ada_cols, ada_cols, axis=1)
    g_w_ada, g_b_ada = _ada_bwd(c_all, dmod_all, dmod_cols)

    red = [_sum_slots(p, "sum_chips_%d" % i) for i, p in enumerate(parts)]
    red_sib = _exchange_list(red, "c", "swap", "share_cores")
    reduced = [_add2(r, s, F32, "add_cores_%d" % i) for i, (r, s) in enumerate(zip(red, red_sib))]

    grads = {"w_ada": g_w_ada[None], "b_ada": g_b_ada}
    order = list(EARLY_GRADS) + ["misc", "w_in_t"]
    for n, g in zip(order, reduced):
        if n == "misc":
            for k, gk in _unflatten_shard(g).items():
                grads[k] = gk[None]
        else:
            grads[n[:-2] if n.endswith("_t") else n] = (g.T if n.endswith("_t") else g)[None]
    wc_cols = w_conv.shape[2]
    for n in SMALL:
        g = g_small[n]
        if n == "w_conv":
            g = lax.dynamic_slice_in_dim(g, chip * wc_cols, wc_cols, axis=1)
        grads[n] = g.reshape(w[n].shape)

    delta, new_m, new_v = {}, {}, {}
    for n in ["w_ada"] + [b for b, _ in DIRECT] + [b for b, _, _ in BIG]:
        shp = w[n].shape
        d2, m2, v2 = _adamw(w[n][0], grads[n][0], m[n][0], v[n][0], "adamw_" + n)
        delta[n], new_m[n], new_v[n] = d2.reshape(shp), m2.reshape(shp), v2.reshape(shp)
    rest = ["b_ada"] + SMALL

    def drop(a):
        return a.reshape(1, -1) if a.ndim == 1 else (a if a.ndim == 2 else a[0])

    upd = _adamw_multi([(drop(w[n]), drop(grads[n]), drop(m[n]), drop(v[n])) for n in rest])
    for n, (dd, mm, vv) in zip(rest, upd):
        delta[n], new_m[n], new_v[n] = dd.reshape(w[n].shape), mm.reshape(w[n].shape), vv.reshape(w[n].shape)

    return (loss, dx, *[grads[n] for n in WEIGHTS], *[delta[n] for n in WEIGHTS], *[new_m[n] for n in WEIGHTS],
            *[new_v[n] for n in WEIGHTS])
```

```python
import functools
import math

import numpy as np
import jax
import jax.numpy as jnp
from jax import lax
from jax.experimental import pallas as pl
from jax.experimental.pallas import tpu as pltpu

F32, BF16 = jnp.float32, jnp.bfloat16

D_MODEL = 1024
N_HEADS = 8
HEAD_DIM = 64
ATT_WIDTH = 512
SSM_GROUPS = 16
SSM_GROUP_CH = 16
SSM_WIDTH = 256
SSM_STATE = 64
SSM_LANES = SSM_GROUPS * SSM_STATE
D_FF = 2048
IN_WIDTH = 3 * ATT_WIDTH + SSM_WIDTH + 2 * D_MODEL
ATT_BLOCK = 128
N_PATTERNS = 3
EPS = 1e-6
NEG_INF = -1e30

ADAM_LR, ADAM_B1, ADAM_B2, ADAM_EPS, ADAM_WD, ADAM_STEP = 0.001, 0.9, 0.999, 1e-08, 0.01, 10

V7X_VMEM_LIMIT_BYTES = 56 * 1024 * 1024
LANES = 1024

MESH_AXES = ("x", "y", "c")


def _pcall(body, *, name, out_shape, grid=(), in_specs=None, out_specs=None, scratch_shapes=(), dims=None):
    params = dict(vmem_limit_bytes=V7X_VMEM_LIMIT_BYTES)
    if dims is not None:
        params["dimension_semantics"] = dims
    specs = {}
    if in_specs is not None:
        specs = dict(grid=grid, in_specs=in_specs, out_specs=out_specs)
    return pl.pallas_call(body, name=name, out_shape=out_shape, scratch_shapes=scratch_shapes,
                          compiler_params=pltpu.CompilerParams(**params), **specs)


def _sds(shape, dtype):
    return jax.ShapeDtypeStruct(tuple(shape), dtype)


def _tile(n, target):
    if n <= target:
        return n
    for t in range(target - target % 128, 0, -128):
        if n % t == 0:
            return t
    raise ValueError((n, target))


def _sig(v):
    return pl.reciprocal(1.0 + jnp.exp(-v), approx=True)


def _mm(a, b, *, name, ta=False, tb=False, out_dtype=F32, tm=2048, tn=1024, tk=1024, riders=None):
    halves = a.ndim == 3
    if halves:
        a_rows, a_cols = a.shape[1], 2 * a.shape[2]
    else:
        a_rows, a_cols = a.shape
    if ta:
        K, M = a_rows, a_cols
    else:
        M, K = a_rows, a_cols
    if tb:
        N, K2 = b.shape
    else:
        K2, N = b.shape
    assert K == K2, (a.shape, b.shape)
    if halves:
        tm, tk = (min(tm, M // 2), tk) if ta else (tm, min(tk, K // 2))
    tm, tn, tk = _tile(M, tm), _tile(N, tn), _tile(K, tk)
    nk = K // tk
    if halves and ta:
        per = a.shape[2] // tm
        a_spec = pl.BlockSpec((None, tk, tm), lambda i, j, k: (i // per, k, i % per))
    elif halves:
        per = a.shape[2] // tk
        a_spec = pl.BlockSpec((None, tm, tk), lambda i, j, k: (k // per, i, k % per))
    else:
        a_spec = pl.BlockSpec((tk, tm), lambda i, j, k: (k, i)) if ta else pl.BlockSpec((tm, tk), lambda i, j, k: (i, k))
    b_spec = pl.BlockSpec((tn, tk), lambda i, j, k: (j, k)) if tb else pl.BlockSpec((tk, tn), lambda i, j, k: (k, j))
    dn = (((0 if ta else 1,), (1 if tb else 0,)), ((), ()))

    def body(a_ref, b_ref, o_ref, acc_ref):
        k = pl.program_id(2)

        @pl.when(k == 0)
        def _():
            acc_ref[...] = jnp.zeros_like(acc_ref)

        acc_ref[...] += lax.dot_general(a_ref[...].astype(BF16), b_ref[...].astype(BF16), dn,
                                        preferred_element_type=F32)

        @pl.when(k == nk - 1)
        def _():
            o_ref[...] = acc_ref[...].astype(out_dtype)

    def body_single(a_ref, b_ref, o_ref):
        o_ref[...] = lax.dot_general(a_ref[...].astype(BF16), b_ref[...].astype(BF16), dn,
                                     preferred_element_type=F32).astype(out_dtype)

    grid = (M // tm, N // tn, nk)
    scratch = [] if nk == 1 else [pltpu.VMEM((tm, tn), F32)]
    o_spec = pl.BlockSpec((tm, tn), lambda i, j, k: (i, j))
    if riders is None:
        return _pcall(body_single if nk == 1 else body, name=name, out_shape=_sds((M, N), out_dtype), grid=grid,
                      in_specs=[a_spec, b_spec], out_specs=o_spec, scratch_shapes=scratch,
                      dims=("parallel", "parallel", "arbitrary"))(a, b)
    rs = riders
    res = _pcall(_with_riders(body_single if nk == 1 else body, rs, 2, 1, len(scratch), tuple(g - 1 for g in grid)),
                 name=name, out_shape=(_sds((M, N), out_dtype),) + tuple(rs.out_shape), grid=grid,
                 in_specs=[a_spec, b_spec] + rs.specs, out_specs=(o_spec,) + tuple(rs.specs),
                 scratch_shapes=scratch + rs.scratch, dims=("arbitrary", "arbitrary", "arbitrary"))(a, b, *rs.arrs)
    return res[0], list(res[1:])


def _ada_fwd(c_all, w_ada, b_ada_cols):
    n = w_ada.shape[1]

    def body(c_ref, w_ref, b_ref, o_ref):
        c = c_ref[...]
        act = c * _sig(c)
        o_ref[...] = jnp.dot(act.astype(BF16), w_ref[...].astype(BF16), preferred_element_type=F32) + b_ref[...]

    return _pcall(body, name="ada_fwd", out_shape=_sds((c_all.shape[0], n), F32))(c_all, w_ada, b_ada_cols)


def _ada_bwd(c_all, dmod_all, dmod_cols):
    n = dmod_cols.shape[1]

    def body(c_ref, da_ref, dc_ref, gw_ref, gb_ref):
        c = c_ref[...]
        act = c * _sig(c)
        gw_ref[...] = lax.dot_general(act, dc_ref[...], (((0,), (0,)), ((), ())), preferred_element_type=F32,
                                      precision=lax.Precision.HIGHEST)
        gb_ref[...] = jnp.sum(da_ref[...], axis=0, keepdims=True)

    return _pcall(body, name="ada_bwd", out_shape=(_sds((D_MODEL, n), F32), _sds((1, dmod_all.shape[1]), F32)))(
        c_all, dmod_all, dmod_cols)


ROW_TILE = 512


def _row_specs(B, S):
    ts = min(S, ROW_TILE)
    row = pl.BlockSpec((1, ts, D_MODEL), lambda b, s: (b, s, 0))
    bvec = pl.BlockSpec((1, 1, D_MODEL), lambda b, s: (b, 0, 0))
    gvec = pl.BlockSpec((1, D_MODEL), lambda b, s: (0, 0))
    return ts, row, bvec, gvec


def _norm_mod(x3, g, sc, sh):
    B, S, _ = x3.shape
    ts, row, bvec, gvec = _row_specs(B, S)

    def body(x_ref, g_ref, sc_ref, sh_ref, u_ref):
        x = x_ref[0]
        r = lax.rsqrt(jnp.mean(x * x, axis=-1, keepdims=True) + EPS)
        u_ref[0] = ((x * r) * g_ref[...] * (1.0 + sc_ref[0]) + sh_ref[0]).astype(BF16)

    return _pcall(body, name="norm_mod1", out_shape=_sds(x3.shape, BF16), grid=(B, S // ts),
                  in_specs=[row, gvec, bvec, bvec], out_specs=row, dims=("parallel", "parallel"))(x3, g, sc, sh)


def _resid_norm_mod(x3, mix3, gt, g, sc, sh):
    B, S, _ = x3.shape
    ts, row, bvec, gvec = _row_specs(B, S)

    def body(x_ref, m_ref, gt_ref, g_ref, sc_ref, sh_ref, h_ref, u_ref):
        h = x_ref[0] + gt_ref[0] * m_ref[0]
        h_ref[0] = h
        r = lax.rsqrt(jnp.mean(h * h, axis=-1, keepdims=True) + EPS)
        u_ref[0] = ((h * r) * g_ref[...] * (1.0 + sc_ref[0]) + sh_ref[0]).astype(BF16)

    return _pcall(body, name="resid_norm_mod2", out_shape=(_sds(x3.shape, F32), _sds(x3.shape, BF16)),
                  grid=(B, S // ts), in_specs=[row, row, bvec, gvec, bvec, bvec], out_specs=(row, row),
                  dims=("parallel", "parallel"))(x3, mix3, gt, g, sc, sh)


def _norm_bwd(h3, du3, dres3, g, sc, name, mix3=None, gt=None):
    B, S, _ = h3.shape
    ts, row, bvec, gvec = _row_specs(B, S)
    with_gate = mix3 is not None

    def body(*refs):
        if with_gate:
            h_ref, du_ref, dr_ref, g_ref, sc_ref, m_ref, gt_ref, dh_ref, dsh_ref, dsc_ref, dg_ref, dgt_ref, dm_ref = refs
        else:
            h_ref, du_ref, dr_ref, g_ref, sc_ref, dh_ref, dsh_ref, dsc_ref, dg_ref = refs
        b, s = pl.program_id(0), pl.program_id(1)
        h = h_ref[0]
        r = lax.rsqrt(jnp.mean(h * h, axis=-1, keepdims=True) + EPS)
        xn = h * r
        du = du_ref[0].astype(F32)
        g = g_ref[...]
        sc1 = 1.0 + sc_ref[0]
        dxn = du * g * sc1
        dh = dr_ref[0] + r * (dxn - xn * jnp.mean(dxn * xn, axis=-1, keepdims=True))
        dh_ref[0] = dh

        @pl.when(s == 0)
        def _():
            dsh_ref[...] = jnp.zeros_like(dsh_ref)
            dsc_ref[...] = jnp.zeros_like(dsc_ref)
            if with_gate:
                dgt_ref[...] = jnp.zeros_like(dgt_ref)

        @pl.when((s == 0) & (b == 0))
        def _():
            dg_ref[...] = jnp.zeros_like(dg_ref)

        dux = du * xn
        dsh_ref[0] += jnp.sum(du, axis=0, keepdims=True)
        dsc_ref[0] += jnp.sum(dux * g, axis=0, keepdims=True)
        dg_ref[...] += jnp.sum(dux * sc1, axis=0, keepdims=True)
        if with_gate:
            dgt_ref[0] += jnp.sum(dh * m_ref[0], axis=0, keepdims=True)
            dm_ref[0] = (dh * gt_ref[0]).astype(BF16)

    bshape = _sds((B, 1, D_MODEL), F32)
    in_specs = [row, row, row, gvec, bvec]
    out_shape = [_sds(h3.shape, F32), bshape, bshape, _sds((1, D_MODEL), F32)]
    out_specs = [row, bvec, bvec, gvec]
    args = [h3, du3, dres3, g, sc]
    if with_gate:
        in_specs += [row, bvec]
        out_shape += [bshape, _sds(h3.shape, BF16)]
        out_specs += [bvec, row]
        args += [mix3, gt]
    return _pcall(body, name=name, out_shape=tuple(out_shape), grid=(B, S // ts), in_specs=in_specs,
                  out_specs=tuple(out_specs), dims=("arbitrary", "arbitrary"))(*args)


def _final_loss(h1, ffn3, tgt3, gt, gfin):
    B, S, _ = h1.shape
    ts, row, bvec, gvec = _row_specs(B, S)
    one = pl.BlockSpec((1, 1), lambda b, s: (0, 0))

    def body(h_ref, f_ref, t_ref, gt_ref, gf_ref, dh_ref, dff_ref, dgt_ref, dgf_ref, loss_ref):
        b, s = pl.program_id(0), pl.program_id(1)
        f = f_ref[0].astype(F32)
        gtv = gt_ref[0]
        gf = gf_ref[...]
        h2 = h_ref[0] + gtv * f
        r = lax.rsqrt(jnp.mean(h2 * h2, axis=-1, keepdims=True) + EPS)
        n = h2 * r
        e = n * gf - t_ref[0]
        dy = e * (1.0 / D_MODEL)
        dn = dy * gf
        dh2 = r * (dn - n * jnp.mean(dn * n, axis=-1, keepdims=True))
        dh_ref[0] = dh2
        dff_ref[0] = (dh2 * gtv).astype(BF16)

        @pl.when(s == 0)
        def _():
            dgt_ref[...] = jnp.zeros_like(dgt_ref)

        @pl.when((s == 0) & (b == 0))
        def _():
            dgf_ref[...] = jnp.zeros_like(dgf_ref)
            loss_ref[...] = jnp.zeros_like(loss_ref)

        dgt_ref[0] += jnp.sum(dh2 * f, axis=0, keepdims=True)
        dgf_ref[...] += jnp.sum(dy * n, axis=0, keepdims=True)
        rows = jnp.sum(e * e, axis=1, keepdims=True)
        loss_ref[...] += jnp.sum(rows, axis=0, keepdims=True) * (0.5 / D_MODEL)

    return _pcall(body, name="final_loss",
                  out_shape=(_sds(h1.shape, F32), _sds(h1.shape, BF16), _sds((B, 1, D_MODEL), F32),
                             _sds((1, D_MODEL), F32), _sds((1, 1), F32)),
                  grid=(B, S // ts), in_specs=[row, row, row, bvec, gvec], out_specs=(row, row, bvec, gvec, one),
                  dims=("arbitrary", "arbitrary"))(h1, ffn3, tgt3, gt, gfin)


def _att_scores(qh, kc, kp, h, dil, first, a_idx, j_idx):
    scale = HEAD_DIM ** -0.5
    nt = (((1,), (1,)), ((), ()))
    slope = (2.0 ** (-8.0 * (h + 1) / N_HEADS)) * dil
    dist_c = (a_idx - j_idx).astype(F32)
    s_c = lax.dot_general(qh, kc, nt, preferred_element_type=F32) * scale
    s_c = jnp.where(a_idx >= j_idx, s_c - slope * dist_c, NEG_INF)
    s_p = lax.dot_general(qh, kp, nt, preferred_element_type=F32) * scale
    s_p = jnp.where((j_idx >= a_idx) & jnp.logical_not(first), s_p - slope * (dist_c + float(ATT_BLOCK)), NEG_INF)
    return s_c, s_p


def _att_block_consts(seq_blocks):
    p = pl.program_id(0)
    j = pl.program_id(1)
    nb = lax.shift_right_logical(jnp.int32(seq_blocks), 2 * p)
    dil = lax.shift_left(jnp.int32(1), 2 * p).astype(F32)
    a_idx = lax.broadcasted_iota(jnp.int32, (ATT_BLOCK, ATT_BLOCK), 0)
    j_idx = lax.broadcasted_iota(jnp.int32, (ATT_BLOCK, ATT_BLOCK), 1)
    return j, nb, dil, a_idx, j_idx


def _attn_fwd(qb, kb, vb, seq_blocks):
    _, NB, _, _ = qb.shape
    cur = pl.BlockSpec((None, None, ATT_BLOCK, ATT_WIDTH), lambda p, j: (p, j, 0, 0))
    prev = pl.BlockSpec((None, None, ATT_BLOCK, ATT_WIDTH), lambda p, j: (p, jnp.maximum(j - 1, 0), 0, 0))
    lse_spec = pl.BlockSpec((None, None, ATT_BLOCK, N_HEADS), lambda p, j: (p, j, 0, 0))

    def body(q_ref, kc_ref, kp_ref, vc_ref, vp_ref, o_ref, lse_ref):
        j, nb, dil, a_idx, j_idx = _att_block_consts(seq_blocks)
        first = lax.rem(j, nb) == 0
        for h in range(N_HEADS):
            hs = slice(h * HEAD_DIM, (h + 1) * HEAD_DIM)
            s_c, s_p = _att_scores(q_ref[:, hs], kc_ref[:, hs], kp_ref[:, hs], h, dil, first, a_idx, j_idx)
            m = jnp.maximum(jnp.max(s_c, axis=1, keepdims=True), jnp.max(s_p, axis=1, keepdims=True))
            p_c = jnp.exp(s_c - m)
            p_p = jnp.exp(s_p - m)
            den = jnp.sum(p_c, axis=1, keepdims=True) + jnp.sum(p_p, axis=1, keepdims=True)
            o = (jnp.dot(p_c.astype(BF16), vc_ref[:, hs], preferred_element_type=F32)
                 + jnp.dot(p_p.astype(BF16), vp_ref[:, hs], preferred_element_type=F32))
            o_ref[:, hs] = o / den
            lse_ref[:, h:h + 1] = m + jnp.log(den)

    return _pcall(body, name="attn_fwd",
                  out_shape=(_sds(qb.shape, F32), _sds((N_PATTERNS, NB, ATT_BLOCK, N_HEADS), F32)),
                  grid=(N_PATTERNS, NB), in_specs=[cur, cur, prev, cur, prev], out_specs=(cur, lse_spec),
                  dims=("parallel", "parallel"))(qb, kb, kb, vb, vb)


def _attn_combine(o_p, lse_p):
    _, T, _ = o_p.shape
    tm = min(T, 1024)

    def body(o_ref, l_ref, out_ref, lse_ref):
        l0, l1, l2 = l_ref[0], l_ref[1], l_ref[2]
        m = jnp.maximum(jnp.maximum(l0, l1), l2)
        lse = m + jnp.log(jnp.exp(l0 - m) + jnp.exp(l1 - m) + jnp.exp(l2 - m))
        lse_ref[...] = lse
        w = [jnp.exp(l0 - lse), jnp.exp(l1 - lse), jnp.exp(l2 - lse)]
        for h in range(N_HEADS):
            hs = slice(h * HEAD_DIM, (h + 1) * HEAD_DIM)
            acc = w[0][:, h:h + 1] * o_ref[0, :, hs]
            acc = acc + w[1][:, h:h + 1] * o_ref[1, :, hs]
            acc = acc + w[2][:, h:h + 1] * o_ref[2, :, hs]
            out_ref[:, hs] = acc.astype(BF16)

    return _pcall(body, name="attn_combine", out_shape=(_sds((T, ATT_WIDTH), BF16), _sds((T, N_HEADS), F32)),
                  grid=(T // tm,),
                  in_specs=[pl.BlockSpec((N_PATTERNS, tm, ATT_WIDTH), lambda i: (0, i, 0)),
                            pl.BlockSpec((N_PATTERNS, tm, N_HEADS), lambda i: (0, i, 0))],
                  out_specs=(pl.BlockSpec((tm, ATT_WIDTH), lambda i: (i, 0)), pl.BlockSpec((tm, N_HEADS), lambda i: (i, 0))),
                  dims=("parallel",))(o_p, lse_p)


def _attn_bwd(qb, kb, vb, dob, ob, lseb, seq_blocks):
    _, NB, _, _ = qb.shape
    last = NB - 1
    cur = pl.BlockSpec((None, None, ATT_BLOCK, ATT_WIDTH), lambda p, j: (p, jnp.minimum(j, last), 0, 0))
    prev = pl.BlockSpec((None, None, ATT_BLOCK, ATT_WIDTH),
                        lambda p, j: (p, jnp.maximum(jnp.minimum(j, last) - 1, 0), 0, 0))
    lag = pl.BlockSpec((None, None, ATT_BLOCK, ATT_WIDTH), lambda p, j: (p, jnp.maximum(j - 1, 0), 0, 0))
    lse_spec = pl.BlockSpec((None, None, ATT_BLOCK, N_HEADS), lambda p, j: (p, jnp.minimum(j, last), 0, 0))
    scale = HEAD_DIM ** -0.5
    tn = (((0,), (0,)), ((), ()))
    nt = (((1,), (1,)), ((), ()))

    def body(q_ref, kc_ref, kp_ref, vc_ref, vp_ref, do_ref, o_ref, lse_ref, dq_ref, dk_ref, dv_ref, ck_ref, cv_ref):
        j, nb, dil, a_idx, j_idx = _att_block_consts(seq_blocks)

        @pl.when(j == 0)
        def _():
            ck_ref[...] = jnp.zeros_like(ck_ref)
            cv_ref[...] = jnp.zeros_like(cv_ref)

        @pl.when(j <= last)
        def _():
            first = lax.rem(j, nb) == 0
            for h in range(N_HEADS):
                hs = slice(h * HEAD_DIM, (h + 1) * HEAD_DIM)
                qh, kc, kp, vc, vp, doh = q_ref[:, hs], kc_ref[:, hs], kp_ref[:, hs], vc_ref[:, hs], vp_ref[:, hs], do_ref[:, hs]
                s_c, s_p = _att_scores(qh, kc, kp, h, dil, first, a_idx, j_idx)
                lse = lse_ref[:, h:h + 1]
                p_c = jnp.exp(s_c - lse)
                p_p = jnp.exp(s_p - lse)
                delta = jnp.sum(doh.astype(F32) * o_ref[:, hs].astype(F32), axis=1, keepdims=True)
                ds_c = (p_c * (lax.dot_general(doh, vc, nt, preferred_element_type=F32) - delta)).astype(BF16)
                ds_p = (p_p * (lax.dot_general(doh, vp, nt, preferred_element_type=F32) - delta)).astype(BF16)
                dq_ref[:, hs] = (jnp.dot(ds_c, kc, preferred_element_type=F32)
                                 + jnp.dot(ds_p, kp, preferred_element_type=F32)) * scale
                dk_ref[:, hs] = ck_ref[:, hs] + lax.dot_general(ds_p, qh, tn, preferred_element_type=F32) * scale
                dv_ref[:, hs] = cv_ref[:, hs] + lax.dot_general(p_p.astype(BF16), doh, tn, preferred_element_type=F32)
                ck_ref[:, hs] = lax.dot_general(ds_c, qh, tn, preferred_element_type=F32) * scale
                cv_ref[:, hs] = lax.dot_general(p_c.astype(BF16), doh, tn, preferred_element_type=F32)

        @pl.when(j == NB)
        def _():
            dk_ref[...] = ck_ref[...]
            dv_ref[...] = cv_ref[...]

    shp = _sds(qb.shape, F32)
    return _pcall(body, name="attn_bwd", out_shape=(shp, shp, shp), grid=(N_PATTERNS, NB + 1),
                  in_specs=[cur, cur, prev, cur, prev, cur, cur, lse_spec], out_specs=(cur, lag, lag),
                  scratch_shapes=[pltpu.VMEM((ATT_BLOCK, ATT_WIDTH), F32), pltpu.VMEM((ATT_BLOCK, ATT_WIDTH), F32)],
                  dims=("arbitrary", "arbitrary"))(qb, kb, kb, vb, vb, dob, ob, lseb)


def _sum3_cast(a, b, c):
    T, N = a.shape
    tm = min(T, 1024)
    spec = pl.BlockSpec((tm, N), lambda i: (i, 0))

    def body(a_ref, b_ref, c_ref, o_ref):
        o_ref[...] = (a_ref[...] + b_ref[...] + c_ref[...]).astype(BF16)

    return _pcall(body, name="sum3_cast", out_shape=_sds((T, N), BF16), grid=(T // tm,), in_specs=[spec] * 3,
                  out_specs=spec, dims=("parallel",))(a, b, c)


def _to_blocks(t, B, S):
    C = t.shape[-1]
    outs = []
    for p in range(N_PATTERNS):
        d = 4 ** p
        u = t.reshape(B, S // d, d, C).transpose(0, 2, 1, 3)
        outs.append(u.reshape(B * S // ATT_BLOCK, ATT_BLOCK, C))
    return jnp.stack(outs, axis=0)


def _from_blocks(tb, B, S):
    C = tb.shape[-1]
    outs = []
    for p in range(N_PATTERNS):
        d = 4 ** p
        u = tb[p].reshape(B, d, S // d, C).transpose(0, 2, 1, 3)
        outs.append(u.reshape(B * S, C))
    return jnp.stack(outs, axis=0)


ATT_GROUP = 4
ATT_GW = ATT_GROUP * HEAD_DIM
ATT_GROUPS = N_HEADS // ATT_GROUP
ATT_PAIRS = ATT_GW // ATT_BLOCK
ATT_UNROLL = 5
ATT_RESIDUE_UNROLL = 4
NT_DIMS = (((1,), (1,)), ((), ()))
TN_DIMS = (((0,), (0,)), ((), ()))


def _att_rows(start, d):
    if d == 1:
        return pl.ds(start if isinstance(start, int) else pl.multiple_of(start, ATT_BLOCK), ATT_BLOCK)
    return pl.ds(start, ATT_BLOCK, stride=d)


def _att_fill_bias(bias_ref, g, d):
    a = lax.broadcasted_iota(jnp.int32, (ATT_BLOCK, ATT_BLOCK), 0)
    j = lax.broadcasted_iota(jnp.int32, (ATT_BLOCK, ATT_BLOCK), 1)
    dist = (a - j).astype(F32)
    for hh in range(ATT_GROUP):
        t, e = divmod(hh, 2)
        rs = slice(e * ATT_BLOCK, (e + 1) * ATT_BLOCK)
        lo = 2.0 ** (-8.0 * (hh + 1) / N_HEADS) * d
        hi = 2.0 ** (-8.0 * (ATT_GROUP + hh + 1) / N_HEADS) * d
        slope = jnp.where(g == 0, lo, hi).astype(F32)
        bias_ref[t, rs, 0:ATT_BLOCK] = jnp.where(a >= j, -slope * dist, NEG_INF)
        bias_ref[t, rs, ATT_BLOCK:] = jnp.where(j >= a, -slope * (dist + float(ATT_BLOCK)), NEG_INF)


def _stack_heads(v2, low):
    return jnp.concatenate([jnp.where(low, v2, 0.0), jnp.where(low, 0.0, v2)], axis=0).astype(BF16)


def _unstack_heads(r2, low):
    return jnp.where(low, r2[0:ATT_BLOCK], r2[ATT_BLOCK:])


class _Riders:
    def __init__(self, arrs, mode, group="xy"):
        self.arrs, self.mode, self.n, self.group = list(arrs), mode, len(arrs), group
        slot_shapes = [a.shape if mode == "gather" else a.shape[1:] for a in self.arrs]
        self.out_shape = [_sds((_GROUP_SLOTS[group],) + s, a.dtype) for s, a in zip(slot_shapes, self.arrs)]
        k = len(_GROUP_MASKS[group])
        self.scratch = [pltpu.SemaphoreType.DMA((k * self.n,)), pltpu.SemaphoreType.DMA((k * self.n,)),
                        pltpu.SemaphoreType.DMA((2 * self.n,))] + [pltpu.VMEM(s, a.dtype) for s, a in zip(slot_shapes, self.arrs)]
        self.specs = [pl.BlockSpec(memory_space=pl.ANY)] * self.n

    def _remote(self, x_refs, o_refs, send_sems, recv_sems):
        x, y, c = lax.axis_index("x"), lax.axis_index("y"), lax.axis_index("c")
        me = _group_slot(self.group, x, y, c)
        masks = _GROUP_MASKS[self.group]
        cps = []
        for i in range(self.n):
            for k, (dx, dy, dc) in enumerate(masks):
                px, py, pc = _flip(x, dx), _flip(y, dy), _flip(c, dc)
                src = x_refs[i] if self.mode == "gather" else x_refs[i].at[_group_slot(self.group, px, py, pc)]
                cps.append(pltpu.make_async_remote_copy(
                    src_ref=src, dst_ref=o_refs[i].at[me], send_sem=send_sems.at[len(masks) * i + k],
                    recv_sem=recv_sems.at[len(masks) * i + k], device_id=(px, py, pc),
                    device_id_type=pl.DeviceIdType.MESH))
        return cps, me

    def start(self, x_refs, o_refs, scratch):
        send_sems, recv_sems, local_sems, bufs = scratch[0], scratch[1], scratch[2], scratch[3:]
        cps, me = self._remote(x_refs, o_refs, send_sems, recv_sems)
        for cp in cps:
            cp.start()
        for i in range(self.n):
            src = x_refs[i] if self.mode == "gather" else x_refs[i].at[me]
            load = pltpu.make_async_copy(src, bufs[i], local_sems.at[2 * i])
            load.start()
            load.wait()
            pltpu.make_async_copy(bufs[i], o_refs[i].at[me], local_sems.at[2 * i + 1]).start()

    def wait(self, x_refs, o_refs, scratch):
        send_sems, recv_sems, local_sems, bufs = scratch[0], scratch[1], scratch[2], scratch[3:]
        cps, me = self._remote(x_refs, o_refs, send_sems, recv_sems)
        for cp in cps:
            cp.wait()
        for i in range(self.n):
            pltpu.make_async_copy(bufs[i], o_refs[i].at[me], local_sems.at[2 * i + 1]).wait()


def _with_riders(compute, riders, n_in, n_out, n_scratch, last_step):
    if riders is None:
        return compute
    n = riders.n

    def body(*refs):
        ins, x_refs = refs[:n_in], refs[n_in:n_in + n]
        outs, o_refs = refs[n_in + n:n_in + n + n_out], refs[n_in + n + n_out:n_in + 2 * n + n_out]
        scratch = refs[n_in + 2 * n + n_out:]
        own, ride = scratch[:n_scratch], scratch[n_scratch:]
        ids = [pl.program_id(i) for i in range(len(last_step))]
        first = functools.reduce(jnp.logical_and, [i == 0 for i in ids])
        last = functools.reduce(jnp.logical_and, [i == l for i, l in zip(ids, last_step)])

        @pl.when(first)
        def _():
            riders.start(x_refs, o_refs, ride)

        compute(*ins, *outs, *own)

        @pl.when(last)
        def _():
            riders.wait(x_refs, o_refs, ride)

    return body


def _attention_fwd(proj3, seq_blocks, riders=None):
    B, S, _ = proj3.shape
    scale = HEAD_DIM ** -0.5
    nq = ATT_WIDTH // ATT_GW

    def col(k):
        return pl.BlockSpec((1, S, ATT_GW), lambda b, g, k=k: (b, 0, k * nq + g))

    o_spec = pl.BlockSpec((1, S, ATT_GW), lambda b, g: (b, 0, g))
    l_spec = pl.BlockSpec((1, 1, S, ATT_BLOCK), lambda b, g: (b, g, 0, 0))

    def compute(q_ref, k_ref, v_ref, o_ref, lse_ref, qf, kf, vf, os, ls, bias):
        g = pl.program_id(1)
        for t in range(ATT_PAIRS):
            ts = slice(t * ATT_BLOCK, (t + 1) * ATT_BLOCK)
            qf[t] = q_ref[0, :, ts].astype(F32) * scale
            kf[t] = k_ref[0, :, ts].astype(F32)
            vf[t] = v_ref[0, :, ts].astype(F32)
        lane = lax.broadcasted_iota(jnp.int32, (ATT_BLOCK, ATT_BLOCK), 1)
        low = lane < HEAD_DIM

        def block(p, d, r, n, has_prev):
            start = n * (ATT_BLOCK * d) + r
            rows = _att_rows(start, d)
            prows = _att_rows(start - ATT_BLOCK * d, d) if has_prev else None
            lse_t = jnp.zeros((ATT_BLOCK, ATT_BLOCK), F32)
            for t in range(ATT_PAIRS):
                q2 = _stack_heads(qf[t, rows, :], low)
                k2 = kf[t, rows, :].astype(BF16)
                v2 = vf[t, rows, :].astype(BF16)
                if has_prev:
                    k2 = jnp.concatenate([k2, kf[t, prows, :].astype(BF16)], axis=0)
                    v2 = jnp.concatenate([v2, vf[t, prows, :].astype(BF16)], axis=0)
                    b2 = bias[t]
                else:
                    b2 = bias[t, :, 0:ATT_BLOCK]
                s = lax.dot_general(q2, k2, NT_DIMS, preferred_element_type=F32) + b2
                m = jnp.max(s, axis=1, keepdims=True)
                pr = jnp.exp(s - m)
                den = jnp.sum(pr, axis=1, keepdims=True)
                o = jnp.dot(pr.astype(BF16), v2, preferred_element_type=F32) * (1.0 / den)
                os[p, t, rows, :] = _unstack_heads(o, low)
                lse2 = m + jnp.log(den)
                lse_t = jnp.where(lane == 2 * t, lse2[0:ATT_BLOCK], lse_t)
                lse_t = jnp.where(lane == 2 * t + 1, lse2[ATT_BLOCK:], lse_t)
            ls[p, rows, :] = lse_t

        for p in range(N_PATTERNS):
            d = 4 ** p
            _att_fill_bias(bias, g, d)
            _att_one_pattern(block, p, d, seq_blocks // d)

        def combine(i, carry):
            rows = pl.ds(pl.multiple_of(i * ATT_BLOCK, ATT_BLOCK), ATT_BLOCK)
            l0, l1, l2 = ls[0, rows, :], ls[1, rows, :], ls[2, rows, :]
            m = jnp.maximum(jnp.maximum(l0, l1), l2)
            lse = m + jnp.log(jnp.exp(l0 - m) + jnp.exp(l1 - m) + jnp.exp(l2 - m))
            lse_ref[0, 0, rows, :] = lse
            w = [jnp.exp(l0 - lse), jnp.exp(l1 - lse), jnp.exp(l2 - lse)]
            for t in range(ATT_PAIRS):
                acc = jnp.zeros((ATT_BLOCK, ATT_BLOCK), F32)
                for p in range(N_PATTERNS):
                    wt = jnp.where(low, w[p][:, 2 * t:2 * t + 1], w[p][:, 2 * t + 1:2 * t + 2])
                    acc = acc + wt * os[p, t, rows, :]
                o_ref[0, rows, t * ATT_BLOCK:(t + 1) * ATT_BLOCK] = acc.astype(BF16)
            return carry

        lax.fori_loop(0, S // ATT_BLOCK, combine, 0, unroll=2)

    scratch = ([pltpu.VMEM((ATT_PAIRS, S, ATT_BLOCK), F32)] * 3
               + [pltpu.VMEM((N_PATTERNS, ATT_PAIRS, S, ATT_BLOCK), F32), pltpu.VMEM((N_PATTERNS, S, ATT_BLOCK), F32),
                  pltpu.VMEM((ATT_PAIRS, 2 * ATT_BLOCK, 2 * ATT_BLOCK), F32)])
    rs = riders
    res = _pcall(_with_riders(compute, rs, 3, 2, len(scratch), (B - 1, ATT_GROUPS - 1)), name="attention_fwd",
                 out_shape=(_sds((B, S, ATT_WIDTH), BF16), _sds((B, ATT_GROUPS, S, ATT_BLOCK), F32))
                 + (tuple(rs.out_shape) if rs else ()),
                 grid=(B, ATT_GROUPS), in_specs=[col(0), col(1), col(2)] + (rs.specs if rs else []),
                 out_specs=(o_spec, l_spec) + (tuple(rs.specs) if rs else ()),
                 scratch_shapes=scratch + (rs.scratch if rs else []),
                 dims=("arbitrary", "arbitrary"))(proj3, proj3, proj3, *(rs.arrs if rs else []))
    return res[0], res[1], list(res[2:])


def _att_one_pattern(block, p, d, nb):
    def per_residue(r, carry):
        block(p, d, r, 0, False)
        if nb > 1:
            def per_block(n, c2):
                block(p, d, r, n, True)
                return c2
            lax.fori_loop(1, nb, per_block, 0, unroll=ATT_UNROLL if (nb - 1) % ATT_UNROLL == 0 else nb - 1)
        return carry

    if d == 1:
        per_residue(0, 0)
    else:
        lax.fori_loop(0, d, per_residue, 0, unroll=ATT_RESIDUE_UNROLL if nb == 1 else 1)


def _attention_bwd(proj3, do3, o3, lse4, seq_blocks, riders=None):
    B, S, _ = proj3.shape
    scale = HEAD_DIM ** -0.5
    nq = ATT_WIDTH // ATT_GW

    def col(k):
        return pl.BlockSpec((1, S, ATT_GW), lambda b, g, k=k: (b, 0, k * nq + g))

    o_spec = pl.BlockSpec((1, S, ATT_GW), lambda b, g: (b, 0, g))
    l_spec = pl.BlockSpec((1, 1, S, ATT_BLOCK), lambda b, g: (b, g, 0, 0))

    def compute(q_ref, k_ref, v_ref, do_ref, o_ref, lse_ref, dq_ref, dk_ref, dv_ref,
                qf, kf, vf, dof, dl, aq, ak, av, bias):
        g = pl.program_id(1)
        for t in range(ATT_PAIRS):
            ts = slice(t * ATT_BLOCK, (t + 1) * ATT_BLOCK)
            qf[t] = q_ref[0, :, ts].astype(F32) * scale
            kf[t] = k_ref[0, :, ts].astype(F32)
            vf[t] = v_ref[0, :, ts].astype(F32)
            dof[t] = do_ref[0, :, ts].astype(F32)
        aq[...] = jnp.zeros_like(aq)
        ak[...] = jnp.zeros_like(ak)
        av[...] = jnp.zeros_like(av)
        lane = lax.broadcasted_iota(jnp.int32, (ATT_BLOCK, ATT_BLOCK), 1)
        low = lane < HEAD_DIM

        def fill_delta(i, carry):
            rows = pl.ds(pl.multiple_of(i * ATT_BLOCK, ATT_BLOCK), ATT_BLOCK)
            acc = jnp.zeros((ATT_BLOCK, ATT_BLOCK), F32)
            for t in range(ATT_PAIRS):
                prod = dof[t, rows, :] * o_ref[0, rows, t * ATT_BLOCK:(t + 1) * ATT_BLOCK].astype(F32)
                lo = jnp.sum(jnp.where(low, prod, 0.0), axis=1, keepdims=True)
                hi = jnp.sum(prod, axis=1, keepdims=True) - lo
                acc = jnp.where(lane == 2 * t, lo, acc)
                acc = jnp.where(lane == 2 * t + 1, hi, acc)
            dl[rows, :] = acc
            return carry

        lax.fori_loop(0, S // ATT_BLOCK, fill_delta, 0, unroll=2)

        def block(p, d, r, n, has_prev):
            start = n * (ATT_BLOCK * d) + r
            rows = _att_rows(start, d)
            prows = _att_rows(start - ATT_BLOCK * d, d) if has_prev else None
            lse_t = lse_ref[0, 0, rows, :]
            dl_t = dl[rows, :]
            for t in range(ATT_PAIRS):
                q2 = _stack_heads(qf[t, rows, :], low)
                do2 = _stack_heads(dof[t, rows, :], low)
                k2 = kf[t, rows, :].astype(BF16)
                v2 = vf[t, rows, :].astype(BF16)
                if has_prev:
                    k2 = jnp.concatenate([k2, kf[t, prows, :].astype(BF16)], axis=0)
                    v2 = jnp.concatenate([v2, vf[t, prows, :].astype(BF16)], axis=0)
                    b2 = bias[t]
                else:
                    b2 = bias[t, :, 0:ATT_BLOCK]
                lse2 = jnp.concatenate([lse_t[:, 2 * t:2 * t + 1], lse_t[:, 2 * t + 1:2 * t + 2]], axis=0)
                dl2 = jnp.concatenate([dl_t[:, 2 * t:2 * t + 1], dl_t[:, 2 * t + 1:2 * t + 2]], axis=0)
                s = lax.dot_general(q2, k2, NT_DIMS, preferred_element_type=F32) + b2
                pr = jnp.exp(s - lse2)
                ds = (pr * (lax.dot_general(do2, v2, NT_DIMS, preferred_element_type=F32) - dl2)).astype(BF16)
                dq = _unstack_heads(jnp.dot(ds, k2, preferred_element_type=F32), low)
                dk = lax.dot_general(ds, q2, TN_DIMS, preferred_element_type=F32)
                dv = lax.dot_general(pr.astype(BF16), do2, TN_DIMS, preferred_element_type=F32)
                aq[t, rows, :] = aq[t, rows, :] + dq * scale
                ak[t, rows, :] = ak[t, rows, :] + dk[0:ATT_BLOCK]
                av[t, rows, :] = av[t, rows, :] + dv[0:ATT_BLOCK]
                if has_prev:
                    ak[t, prows, :] = ak[t, prows, :] + dk[ATT_BLOCK:]
                    av[t, prows, :] = av[t, prows, :] + dv[ATT_BLOCK:]

        for p in range(N_PATTERNS):
            d = 4 ** p
            _att_fill_bias(bias, g, d)
            _att_one_pattern(block, p, d, seq_blocks // d)

        for t in range(ATT_PAIRS):
            ts = slice(t * ATT_BLOCK, (t + 1) * ATT_BLOCK)
            dq_ref[0, :, ts] = aq[t].astype(BF16)
            dk_ref[0, :, ts] = ak[t].astype(BF16)
            dv_ref[0, :, ts] = av[t].astype(BF16)

    shp = _sds((B, S, ATT_WIDTH), BF16)
    pair_buf = pltpu.VMEM((ATT_PAIRS, S, ATT_BLOCK), F32)
    scratch = ([pair_buf] * 4 + [pltpu.VMEM((S, ATT_BLOCK), F32)] + [pair_buf] * 3
               + [pltpu.VMEM((ATT_PAIRS, 2 * ATT_BLOCK, 2 * ATT_BLOCK), F32)])
    rs = riders
    res = _pcall(_with_riders(compute, rs, 6, 3, len(scratch), (B - 1, ATT_GROUPS - 1)), name="attention_bwd",
                 out_shape=(shp, shp, shp) + (tuple(rs.out_shape) if rs else ()), grid=(B, ATT_GROUPS),
                 in_specs=[col(0), col(1), col(2), o_spec, o_spec, l_spec] + (rs.specs if rs else []),
                 out_specs=(o_spec, o_spec, o_spec) + (tuple(rs.specs) if rs else ()),
                 scratch_shapes=scratch + (rs.scratch if rs else []),
                 dims=("arbitrary", "arbitrary"))(proj3, proj3, proj3, do3, o3, lse4, *(rs.arrs if rs else []))
    return res[0], res[1], res[2], list(res[3:])


def _expand_groups(m):
    rows = SSM_WIDTH
    t = jnp.concatenate([m] * SSM_GROUPS, axis=0)
    r = lax.broadcasted_iota(jnp.int32, (rows, SSM_LANES), 0)
    l = lax.broadcasted_iota(jnp.int32, (rows, SSM_LANES), 1)
    keep = lax.shift_right_logical(r, 4) == lax.shift_right_logical(l, 6)
    return jnp.where(keep, t, 0.0)


def _collapse_groups(m):
    rows = SSM_WIDTH
    r = lax.broadcasted_iota(jnp.int32, (rows, SSM_LANES), 0)
    l = lax.broadcasted_iota(jnp.int32, (rows, SSM_LANES), 1)
    keep = lax.shift_right_logical(r, 4) == lax.shift_right_logical(l, 6)
    t = jnp.where(keep, m, 0.0)
    acc = t[0:SSM_GROUP_CH]
    for g in range(1, SSM_GROUPS):
        acc = acc + t[g * SSM_GROUP_CH:(g + 1) * SSM_GROUP_CH]
    return acc


def _zoh(lr, li, ldt):
    dt = jnp.exp(ldt)
    mag = jnp.exp(lr * dt)
    ang = li * dt
    cs, sn = jnp.cos(ang), jnp.sin(ang)
    ab_re, ab_im = mag * cs, mag * sn
    nr, ni = ab_re - 1.0, ab_im
    den = lr * lr + li * li
    n_re = nr * lr + ni * li
    n_im = ni * lr - nr * li
    return dict(dt=dt, mag=mag, cs=cs, sn=sn, ab_re=ab_re, ab_im=ab_im, nr=nr, ni=ni, den=den, n_re=n_re, n_im=n_im,
                f_re=n_re / den, f_im=n_im / den)


def _ssm_params(lr, li, ldt, br, bi, cr, ci):
    def body(lr_ref, li_ref, ldt_ref, br_ref, bi_ref, cr_ref, ci_ref, ab_ref, w_ref, c_ref):
        z = _zoh(lr_ref[...], li_ref[...], ldt_ref[...])
        ab_ref[0:1, :] = z["ab_re"]
        ab_ref[1:2, :] = z["ab_im"]
        br, bi = br_ref[...], bi_ref[...]
        w_ref[:, 0:SSM_LANES] = _expand_groups(z["f_re"] * br - z["f_im"] * bi).astype(BF16)
        w_ref[:, SSM_LANES:] = _expand_groups(z["f_re"] * bi + z["f_im"] * br).astype(BF16)
        c_ref[:, 0:SSM_LANES] = _expand_groups(cr_ref[...]).astype(BF16)
        c_ref[:, SSM_LANES:] = _expand_groups(-ci_ref[...]).astype(BF16)

    return _pcall(body, name="ssm_params",
                  out_shape=(_sds((2, SSM_LANES), F32), _sds((SSM_WIDTH, 2 * SSM_LANES), BF16),
                             _sds((SSM_WIDTH, 2 * SSM_LANES), BF16)))(lr, li, ldt, br, bi, cr, ci)


def _ssm_params_bwd(lr, li, ldt, br, bi, dab, dw, dc):
    def body(lr_ref, li_ref, ldt_ref, br_ref, bi_ref, dab_ref, dw_ref, dc_ref,
             dlr_ref, dli_ref, dldt_ref, dbr_ref, dbi_ref, dcr_ref, dci_ref):
        lr, li = lr_ref[...], li_ref[...]
        z = _zoh(lr, li, ldt_ref[...])
        br, bi = br_ref[...], bi_ref[...]
        dbb_re = _collapse_groups(dw_ref[:, 0:SSM_LANES])
        dbb_im = _collapse_groups(dw_ref[:, SSM_LANES:])
        dcr_ref[...] = _collapse_groups(dc_ref[:, 0:SSM_LANES])
        dci_ref[...] = -_collapse_groups(dc_ref[:, SSM_LANES:])
        f_re, f_im = z["f_re"], z["f_im"]
        dbr_ref[...] = f_re * dbb_re + f_im * dbb_im
        dbi_ref[...] = f_re * dbb_im - f_im * dbb_re
        df_re = jnp.sum(dbb_re * br + dbb_im * bi, axis=0, keepdims=True)
        df_im = jnp.sum(dbb_im * br - dbb_re * bi, axis=0, keepdims=True)
        den = z["den"]
        dn_re, dn_im = df_re / den, df_im / den
        dden = -(df_re * z["n_re"] + df_im * z["n_im"]) / (den * den)
        dnr = dn_re * lr - dn_im * li
        dni = dn_re * li + dn_im * lr
        dlr = dn_re * z["nr"] + dn_im * z["ni"] + 2.0 * dden * lr
        dli = dn_re * z["ni"] - dn_im * z["nr"] + 2.0 * dden * li
        dab_re = dab_ref[0:1, :] + dnr
        dab_im = dab_ref[1:2, :] + dni
        mag, cs, sn, dt = z["mag"], z["cs"], z["sn"], z["dt"]
        dmag = dab_re * cs + dab_im * sn
        dang = mag * (dab_im * cs - dab_re * sn)
        dlr_ref[...] = dlr + dmag * mag * dt
        dli_ref[...] = dli + dang * dt
        ddt = dmag * mag * lr + dang * li
        per_lane = jnp.broadcast_to(ddt * dt, (8, SSM_LANES))
        lane = lax.broadcasted_iota(jnp.int32, (SSM_LANES, 128), 0)
        col = lax.broadcasted_iota(jnp.int32, (SSM_LANES, 128), 1)
        ind = jnp.where(lax.shift_right_logical(lane, 6) == col, 1.0, 0.0)
        dldt_ref[...] = jnp.dot(per_lane, ind, preferred_element_type=F32, precision=lax.Precision.HIGHEST)[0:1]

    vec = _sds((1, SSM_LANES), F32)
    mat = _sds((SSM_GROUP_CH, SSM_LANES), F32)
    return _pcall(body, name="ssm_params_bwd", out_shape=(vec, vec, _sds((1, 128), F32), mat, mat, mat, mat))(
        lr, li, ldt, br, bi, dab, dw, dc)


SCAN_CHUNK = 512


def _scan_consts(ar, ai, k_ref, reverse):
    row = lax.broadcasted_iota(jnp.int32, (8, SSM_LANES), 0)
    pw = [(ar, ai)]
    for _ in range(7):
        pr, pi = pw[-1]
        pw.append((pr * ar - pi * ai, pr * ai + pi * ar))
    for n, k in enumerate((1, 2, 4)):
        keep = (row < 8 - k) if reverse else (row >= k)
        k_ref[2 * n] = jnp.where(keep, jnp.broadcast_to(pw[k - 1][0], (8, SSM_LANES)), 0.0)
        k_ref[2 * n + 1] = jnp.where(keep, jnp.broadcast_to(pw[k - 1][1], (8, SSM_LANES)), 0.0)
    cr = jnp.zeros((8, SSM_LANES), F32)
    ci = jnp.zeros((8, SSM_LANES), F32)
    for r in range(8):
        e = (8 - r) if reverse else (r + 1)
        cr = jnp.where(row == r, jnp.broadcast_to(pw[e - 1][0], (8, SSM_LANES)), cr)
        ci = jnp.where(row == r, jnp.broadcast_to(pw[e - 1][1], (8, SSM_LANES)), ci)
    k_ref[6] = cr
    k_ref[7] = ci


def _scan_tile(xr, xi, k_ref, car, cai, reverse):
    for n, k in enumerate((1, 2, 4)):
        sh = (8 - k) if reverse else k
        sr = pltpu.roll(xr, sh, 0)
        si = pltpu.roll(xi, sh, 0)
        mr, mi = k_ref[2 * n], k_ref[2 * n + 1]
        xr, xi = xr + mr * sr - mi * si, xi + mr * si + mi * sr
    pr, pi = k_ref[6], k_ref[7]
    xr, xi = xr + pr * car - pi * cai, xi + pr * cai + pi * car
    return xr, xi


def _scan_fwd(bu3, abar):
    B, S, _ = bu3.shape
    ch = min(S, SCAN_CHUNK)
    blk = pl.BlockSpec((1, ch, 2 * SSM_LANES), lambda b, c: (b, c, 0))

    def body(ab_ref, bu_ref, x_ref, k_ref, carry_ref):
        _scan_consts(ab_ref[0:1, :], ab_ref[1:2, :], k_ref, False)

        @pl.when(pl.program_id(1) == 0)
        def _():
            carry_ref[...] = jnp.zeros_like(carry_ref)

        def step(i, carry):
            base = pl.multiple_of(i * 8, 8)
            xr = bu_ref[0, pl.ds(base, 8), 0:SSM_LANES]
            xi = bu_ref[0, pl.ds(base, 8), SSM_LANES:]
            xr, xi = _scan_tile(xr, xi, k_ref, carry[0], carry[1], False)
            x_ref[0, pl.ds(base, 8), 0:SSM_LANES] = xr
            x_ref[0, pl.ds(base, 8), SSM_LANES:] = xi
            return (jnp.broadcast_to(xr[7:8], (8, SSM_LANES)), jnp.broadcast_to(xi[7:8], (8, SSM_LANES)))

        cr, ci = lax.fori_loop(0, ch // 8, step, (carry_ref[0], carry_ref[1]))
        carry_ref[0] = cr
        carry_ref[1] = ci

    return _pcall(body, name="scan_fwd", out_shape=_sds(bu3.shape, F32), grid=(B, S // ch),
                  in_specs=[pl.BlockSpec((2, SSM_LANES), lambda b, c: (0, 0)), blk], out_specs=blk,
                  scratch_shapes=[pltpu.VMEM((8, 8, SSM_LANES), F32), pltpu.VMEM((2, 8, SSM_LANES), F32)],
                  dims=("arbitrary", "arbitrary"))(abar, bu3)


def _scan_bwd(dx3, xs3, abar):
    B, S, _ = dx3.shape
    ch = min(S, SCAN_CHUNK)
    nc = S // ch
    blk = pl.BlockSpec((1, ch, 2 * SSM_LANES), lambda b, c: (b, nc - 1 - c, 0))

    def body(ab_ref, dx_ref, xs_ref, g_ref, da_ref, k_ref, carry_ref, acc_ref):
        b, c = pl.program_id(0), pl.program_id(1)
        _scan_consts(ab_ref[0:1, :], -ab_ref[1:2, :], k_ref, True)
        row = lax.broadcasted_iota(jnp.int32, (8, SSM_LANES), 0)

        @pl.when(c == 0)
        def _():
            carry_ref[...] = jnp.zeros_like(carry_ref)

        @pl.when((c == 0) & (b == 0))
        def _():
            acc_ref[...] = jnp.zeros_like(acc_ref)

        def step(i, carry):
            car, cai, ar_acc, ai_acc = carry
            base = pl.multiple_of((ch // 8 - 1 - i) * 8, 8)
            gr = dx_ref[0, pl.ds(base, 8), 0:SSM_LANES]
            gi = dx_ref[0, pl.ds(base, 8), SSM_LANES:]
            gr, gi = _scan_tile(gr, gi, k_ref, car, cai, True)
            g_ref[0, pl.ds(base, 8), 0:SSM_LANES] = gr
            g_ref[0, pl.ds(base, 8), SSM_LANES:] = gi
            nr = jnp.where(row == 7, car, pltpu.roll(gr, 7, 0))
            ni = jnp.where(row == 7, cai, pltpu.roll(gi, 7, 0))
            xr = xs_ref[0, pl.ds(base, 8), 0:SSM_LANES]
            xi = xs_ref[0, pl.ds(base, 8), SSM_LANES:]
            ar_acc = ar_acc + nr * xr + ni * xi
            ai_acc = ai_acc + ni * xr - nr * xi
            return (jnp.broadcast_to(gr[0:1], (8, SSM_LANES)), jnp.broadcast_to(gi[0:1], (8, SSM_LANES)), ar_acc, ai_acc)

        cr, ci, ar_acc, ai_acc = lax.fori_loop(0, ch // 8, step, (carry_ref[0], carry_ref[1], acc_ref[0], acc_ref[1]))
        carry_ref[0] = cr
        carry_ref[1] = ci
        acc_ref[0] = ar_acc
        acc_ref[1] = ai_acc
        da_ref[0:1, :] = jnp.sum(ar_acc, axis=0, keepdims=True)
        da_ref[1:2, :] = jnp.sum(ai_acc, axis=0, keepdims=True)

    return _pcall(body, name="scan_bwd", out_shape=(_sds(dx3.shape, F32), _sds((2, SSM_LANES), F32)), grid=(B, nc),
                  in_specs=[pl.BlockSpec((2, SSM_LANES), lambda b, c: (0, 0)), blk, blk],
                  out_specs=(blk, pl.BlockSpec((2, SSM_LANES), lambda b, c: (0, 0))),
                  scratch_shapes=[pltpu.VMEM((8, 8, SSM_LANES), F32), pltpu.VMEM((2, 8, SSM_LANES), F32),
                                  pltpu.VMEM((2, 8, SSM_LANES), F32)],
                  dims=("arbitrary", "arbitrary"))(abar, dx3, xs3)


US_BLOCK = (3 * ATT_WIDTH) // SSM_WIDTH


def _ssm_scan_fwd(proj3, abar, w_bu, w_c):
    B, S, _ = proj3.shape
    ch = min(S, SCAN_CHUNK)
    u_spec = pl.BlockSpec((1, ch, SSM_WIDTH), lambda b, c: (b, c, US_BLOCK))
    x_spec = pl.BlockSpec((1, ch, 2 * SSM_LANES), lambda b, c: (b, c, 0))
    y_spec = pl.BlockSpec((1, ch, SSM_WIDTH), lambda b, c: (b, c, 0))
    w_spec = pl.BlockSpec((SSM_WIDTH, 2 * SSM_LANES), lambda b, c: (0, 0))

    def body(ab_ref, u_ref, wb_ref, wc_ref, x_ref, y_ref, k_ref, carry_ref):
        _scan_consts(ab_ref[0:1, :], ab_ref[1:2, :], k_ref, False)

        @pl.when(pl.program_id(1) == 0)
        def _():
            carry_ref[...] = jnp.zeros_like(carry_ref)

        x_ref[0] = jnp.dot(u_ref[0], wb_ref[...], preferred_element_type=F32)

        def step(i, carry):
            base = pl.multiple_of(i * 8, 8)
            xr = x_ref[0, pl.ds(base, 8), 0:SSM_LANES]
            xi = x_ref[0, pl.ds(base, 8), SSM_LANES:]
            xr, xi = _scan_tile(xr, xi, k_ref, carry[0], carry[1], False)
            x_ref[0, pl.ds(base, 8), 0:SSM_LANES] = xr
            x_ref[0, pl.ds(base, 8), SSM_LANES:] = xi
            return (jnp.broadcast_to(xr[7:8], (8, SSM_LANES)), jnp.broadcast_to(xi[7:8], (8, SSM_LANES)))

        cr, ci = lax.fori_loop(0, ch // 8, step, (carry_ref[0], carry_ref[1]))
        carry_ref[0] = cr
        carry_ref[1] = ci
        y_ref[0] = lax.dot_general(x_ref[0].astype(BF16), wc_ref[...], NT_DIMS, preferred_element_type=F32)

    return _pcall(body, name="ssm_scan_fwd",
                  out_shape=(_sds((B, S, 2 * SSM_LANES), F32), _sds((B, S, SSM_WIDTH), F32)), grid=(B, S // ch),
                  in_specs=[pl.BlockSpec((2, SSM_LANES), lambda b, c: (0, 0)), u_spec, w_spec, w_spec],
                  out_specs=(x_spec, y_spec),
                  scratch_shapes=[pltpu.VMEM((8, 8, SSM_LANES), F32), pltpu.VMEM((2, 8, SSM_LANES), F32)],
                  dims=("arbitrary", "arbitrary"))(abar, proj3, w_bu, w_c)


def _ssm_scan_bwd(proj3, dy3, xs3, abar, w_bu, w_c, dsk):
    B, S, _ = proj3.shape
    ch = min(S, SCAN_CHUNK)
    nc = S // ch
    u_spec = pl.BlockSpec((1, ch, SSM_WIDTH), lambda b, c: (b, nc - 1 - c, US_BLOCK))
    x_spec = pl.BlockSpec((1, ch, 2 * SSM_LANES), lambda b, c: (b, nc - 1 - c, 0))
    y_spec = pl.BlockSpec((1, ch, SSM_WIDTH), lambda b, c: (b, nc - 1 - c, 0))
    w_spec = pl.BlockSpec((SSM_WIDTH, 2 * SSM_LANES), lambda b, c: (0, 0))
    ab_spec = pl.BlockSpec((2, SSM_LANES), lambda b, c: (0, 0))
    d_spec = pl.BlockSpec((1, SSM_WIDTH), lambda b, c: (0, 0))

    def body(ab_ref, u_ref, dy_ref, xs_ref, wb_ref, wc_ref, d_ref, du_ref, da_ref, dwb_ref, dwc_ref,
             g_ref, k_ref, carry_ref, acc_ref):
        b, c = pl.program_id(0), pl.program_id(1)
        _scan_consts(ab_ref[0:1, :], -ab_ref[1:2, :], k_ref, True)
        row = lax.broadcasted_iota(jnp.int32, (8, SSM_LANES), 0)

        @pl.when(c == 0)
        def _():
            carry_ref[...] = jnp.zeros_like(carry_ref)

        @pl.when((c == 0) & (b == 0))
        def _():
            acc_ref[...] = jnp.zeros_like(acc_ref)
            dwb_ref[...] = jnp.zeros_like(dwb_ref)
            dwc_ref[...] = jnp.zeros_like(dwc_ref)

        dy = dy_ref[0]
        dyb = dy.astype(BF16)
        g_ref[...] = jnp.dot(dyb, wc_ref[...], preferred_element_type=F32)

        def step(i, carry):
            car, cai, ar_acc, ai_acc = carry
            base = pl.multiple_of((ch // 8 - 1 - i) * 8, 8)
            gr = g_ref[pl.ds(base, 8), 0:SSM_LANES]
            gi = g_ref[pl.ds(base, 8), SSM_LANES:]
            gr, gi = _scan_tile(gr, gi, k_ref, car, cai, True)
            g_ref[pl.ds(base, 8), 0:SSM_LANES] = gr
            g_ref[pl.ds(base, 8), SSM_LANES:] = gi
            nr = jnp.where(row == 7, car, pltpu.roll(gr, 7, 0))
            ni = jnp.where(row == 7, cai, pltpu.roll(gi, 7, 0))
            xr = xs_ref[0, pl.ds(base, 8), 0:SSM_LANES]
            xi = xs_ref[0, pl.ds(base, 8), SSM_LANES:]
            ar_acc = ar_acc + nr * xr + ni * xi
            ai_acc = ai_acc + ni * xr - nr * xi
            return (jnp.broadcast_to(gr[0:1], (8, SSM_LANES)), jnp.broadcast_to(gi[0:1], (8, SSM_LANES)), ar_acc, ai_acc)

        cr, ci, ar_acc, ai_acc = lax.fori_loop(0, ch // 8, step, (carry_ref[0], carry_ref[1], acc_ref[0], acc_ref[1]))
        carry_ref[0] = cr
        carry_ref[1] = ci
        acc_ref[0] = ar_acc
        acc_ref[1] = ai_acc
        da_ref[0:1, :] = jnp.sum(ar_acc, axis=0, keepdims=True)
        da_ref[1:2, :] = jnp.sum(ai_acc, axis=0, keepdims=True)

        gb = g_ref[...].astype(BF16)
        du = lax.dot_general(gb, wb_ref[...], NT_DIMS, preferred_element_type=F32) + d_ref[...] * dy
        du_ref[0] = du.astype(BF16)
        dwb_ref[...] += lax.dot_general(u_ref[0], gb, TN_DIMS, preferred_element_type=F32)
        dwc_ref[...] += lax.dot_general(dyb, xs_ref[0].astype(BF16), TN_DIMS, preferred_element_type=F32)

    mat = _sds((SSM_WIDTH, 2 * SSM_LANES), F32)
    return _pcall(body, name="ssm_scan_bwd",
                  out_shape=(_sds((B, S, SSM_WIDTH), BF16), _sds((2, SSM_LANES), F32), mat, mat), grid=(B, nc),
                  in_specs=[ab_spec, u_spec, y_spec, x_spec, w_spec, w_spec, d_spec],
                  out_specs=(y_spec, ab_spec, w_spec, w_spec),
                  scratch_shapes=[pltpu.VMEM((ch, 2 * SSM_LANES), F32), pltpu.VMEM((8, 8, SSM_LANES), F32),
                                  pltpu.VMEM((2, 8, SSM_LANES), F32), pltpu.VMEM((2, 8, SSM_LANES), F32)],
                  dims=("arbitrary", "arbitrary"))(abar, proj3, dy3, xs3, w_bu, w_c, dsk)


GELU_K = math.sqrt(2.0 / math.pi)
GELU_C = 0.044715


def _gelu_parts(y):
    t = jnp.tanh(GELU_K * (y + GELU_C * y * y * y))
    return 0.5 * y * (1.0 + t), t


def _ssm_post(yc, us, dsk, wglu, bglu):
    T, N = yc.shape
    tm = min(T, 1024)
    row = pl.BlockSpec((tm, N), lambda i: (i, 0))
    vec = pl.BlockSpec((1, N), lambda i: (0, 0))
    mat = pl.BlockSpec((N, N), lambda i: (0, 0))

    def body(yc_ref, us_ref, d_ref, w_ref, b_ref, y_ref, s_ref):
        y = yc_ref[...] + d_ref[...] * us_ref[...]
        y_ref[...] = y
        z, _ = _gelu_parts(y)
        gl = jnp.dot(z.astype(BF16), w_ref[...], preferred_element_type=F32) + b_ref[...]
        s_ref[...] = (z * _sig(gl)).astype(BF16)

    return _pcall(body, name="ssm_post", out_shape=(_sds((T, N), F32), _sds((T, N), BF16)), grid=(T // tm,),
                  in_specs=[row, row, vec, mat, vec], out_specs=(row, row), dims=("parallel",))(yc, us, dsk, wglu, bglu)


def _ssm_post_bwd(y5, us, ds, dsk, wglu, bglu):
    T, N = y5.shape
    tm = min(T, 1024)
    row = pl.BlockSpec((tm, N), lambda i: (i, 0))
    vec = pl.BlockSpec((1, N), lambda i: (0, 0))
    mat = pl.BlockSpec((N, N), lambda i: (0, 0))

    def body(y_ref, us_ref, ds_ref, d_ref, w_ref, b_ref, dy_ref, dd_ref, db_ref, dw_ref):
        @pl.when(pl.program_id(0) == 0)
        def _():
            dd_ref[...] = jnp.zeros_like(dd_ref)
            db_ref[...] = jnp.zeros_like(db_ref)
            dw_ref[...] = jnp.zeros_like(dw_ref)

        y = y_ref[...]
        z, t = _gelu_parts(y)
        zb = z.astype(BF16)
        gl = jnp.dot(zb, w_ref[...], preferred_element_type=F32) + b_ref[...]
        sg = _sig(gl)
        ds = ds_ref[...]
        dgl = ds * z * sg * (1.0 - sg)
        dglb = dgl.astype(BF16)
        dz = ds * sg + lax.dot_general(dglb, w_ref[...], (((1,), (1,)), ((), ())), preferred_element_type=F32)
        dgelu = 0.5 * (1.0 + t) + 0.5 * y * (1.0 - t * t) * GELU_K * (1.0 + 3.0 * GELU_C * y * y)
        dy = dz * dgelu
        dy_ref[...] = dy
        dd_ref[...] += jnp.sum(dy * us_ref[...], axis=0, keepdims=True)
        db_ref[...] += jnp.sum(dgl, axis=0, keepdims=True)
        dw_ref[...] += lax.dot_general(zb, dglb, (((0,), (0,)), ((), ())), preferred_element_type=F32)

    return _pcall(body, name="ssm_post_bwd",
                  out_shape=(_sds((T, N), F32), _sds((1, N), F32), _sds((1, N), F32), _sds((N, N), F32)),
                  grid=(T // tm,), in_specs=[row, row, row, vec, mat, vec], out_specs=(row, vec, vec, mat),
                  dims=("arbitrary",))(y5, us, ds, dsk, wglu, bglu)


def _add_scaled_cast(a, b, s):
    T, N = a.shape
    tm = min(T, 1024)
    row = pl.BlockSpec((tm, N), lambda i: (i, 0))

    def body(a_ref, b_ref, s_ref, o_ref):
        o_ref[...] = (a_ref[...] + s_ref[...] * b_ref[...]).astype(BF16)

    return _pcall(body, name="add_scaled_cast", out_shape=_sds((T, N), BF16), grid=(T // tm,),
                  in_specs=[row, row, pl.BlockSpec((1, N), lambda i: (0, 0))], out_specs=row, dims=("parallel",))(a, b, s)


GATE_TILE = 256
GATE_ATT_BLOCK0 = (3 * ATT_WIDTH + SSM_WIDTH) // GATE_TILE
GATE_SSM_BLOCK0 = (3 * ATT_WIDTH + SSM_WIDTH + D_MODEL) // GATE_TILE


def _merge(proj, y_att, y_ssm, b_gate):
    T = proj.shape[0]
    tm = min(T, 1024)
    nj = D_MODEL // GATE_TILE
    ga = pl.BlockSpec((tm, GATE_TILE), lambda i, j: (i, GATE_ATT_BLOCK0 + j))
    gs = pl.BlockSpec((tm, GATE_TILE), lambda i, j: (i, GATE_SSM_BLOCK0 + j))
    yy = pl.BlockSpec((tm, GATE_TILE), lambda i, j: (i, j))
    ba = pl.BlockSpec((1, GATE_TILE), lambda i, j: (0, j))
    bs = pl.BlockSpec((1, GATE_TILE), lambda i, j: (0, nj + j))

    def body(ga_ref, gs_ref, ya_ref, ys_ref, ba_ref, bs_ref, o_ref):
        o_ref[...] = (_sig(ga_ref[...] + ba_ref[...]) * ya_ref[...]
                      + _sig(gs_ref[...] + bs_ref[...]) * ys_ref[...]).astype(BF16)

    return _pcall(body, name="merge", out_shape=_sds((T, D_MODEL), BF16), grid=(T // tm, nj),
                  in_specs=[ga, gs, yy, yy, ba, bs], out_specs=yy, dims=("parallel", "parallel"))(
        proj, proj, y_att, y_ssm, b_gate, b_gate)


def _merge_bwd(proj, y_att, y_ssm, b_gate, dmerged):
    T = proj.shape[0]
    tm = min(T, 1024)
    nj = D_MODEL // GATE_TILE
    ga = pl.BlockSpec((tm, GATE_TILE), lambda j, i: (i, GATE_ATT_BLOCK0 + j))
    gs = pl.BlockSpec((tm, GATE_TILE), lambda j, i: (i, GATE_SSM_BLOCK0 + j))
    yy = pl.BlockSpec((tm, GATE_TILE), lambda j, i: (i, j))
    ba = pl.BlockSpec((1, GATE_TILE), lambda j, i: (0, j))
    bs = pl.BlockSpec((1, GATE_TILE), lambda j, i: (0, nj + j))

    def body(ga_ref, gs_ref, ya_ref, ys_ref, ba_ref, bs_ref, dm_ref, dya_ref, dys_ref, dga_ref, dgs_ref, dba_ref, dbs_ref):
        @pl.when(pl.program_id(1) == 0)
        def _():
            dba_ref[...] = jnp.zeros_like(dba_ref)
            dbs_ref[...] = jnp.zeros_like(dbs_ref)

        dm = dm_ref[...].astype(F32)
        sa = _sig(ga_ref[...] + ba_ref[...])
        ss = _sig(gs_ref[...] + bs_ref[...])
        dya_ref[...] = (dm * sa).astype(BF16)
        dys_ref[...] = (dm * ss).astype(BF16)
        dga = dm * ya_ref[...] * sa * (1.0 - sa)
        dgs = dm * ys_ref[...] * ss * (1.0 - ss)
        dga_ref[...] = dga.astype(BF16)
        dgs_ref[...] = dgs.astype(BF16)
        dba_ref[...] += jnp.sum(dga, axis=0, keepdims=True)
        dbs_ref[...] += jnp.sum(dgs, axis=0, keepdims=True)

    big = _sds((T, D_MODEL), BF16)
    vec = _sds((1, D_MODEL), F32)
    return _pcall(body, name="merge_bwd", out_shape=(big, big, big, big, vec, vec), grid=(nj, T // tm),
                  in_specs=[ga, gs, yy, yy, ba, bs, yy], out_specs=(yy, yy, yy, yy, ba, ba),
                  dims=("arbitrary", "arbitrary"))(proj, proj, y_att, y_ssm, b_gate, b_gate, dmerged)


CONV_TILE = 256


def _shift_rows(a, j, up=False):
    n = a.shape[0]
    r = pltpu.roll(a, n - j if up else j, 0)
    row = lax.broadcasted_iota(jnp.int32, (8, a.shape[1]), 0)
    if up:
        return jnp.concatenate([r[:n - 8], jnp.where(row < 8 - j, r[n - 8:], 0.0)], axis=0)
    return jnp.concatenate([jnp.where(row >= j, r[:8], 0.0), r[8:]], axis=0)


def _conv_pre(a, w_ref, b_ref):
    conv = b_ref[...] + w_ref[0:1, :] * a
    shifted = []
    for j in (1, 2):
        sh = _shift_rows(a, j)
        shifted.append(sh)
        conv = conv + w_ref[j:j + 1, :] * sh
    return conv, shifted


def _conv_act(up3, w_conv, b_conv):
    B, S, _ = up3.shape
    nj = D_FF // CONV_TILE
    a_spec = pl.BlockSpec((1, S, CONV_TILE), lambda b, j: (b, 0, j))
    v_spec = pl.BlockSpec((1, S, CONV_TILE), lambda b, j: (b, 0, nj + j))
    w_spec = pl.BlockSpec((3, CONV_TILE), lambda b, j: (0, j))
    b_spec = pl.BlockSpec((1, CONV_TILE), lambda b, j: (0, j))

    def body(a_ref, v_ref, w_ref, b_ref, o_ref):
        a = a_ref[0].astype(F32)
        conv, _ = _conv_pre(a, w_ref, b_ref)
        o_ref[0] = (conv * _sig(conv) * v_ref[0]).astype(BF16)

    return _pcall(body, name="conv_act", out_shape=_sds((B, S, D_FF), BF16), grid=(B, nj),
                  in_specs=[a_spec, v_spec, w_spec, b_spec], out_specs=a_spec, dims=("parallel", "parallel"))(
        up3, up3, w_conv, b_conv)


def _conv_bwd(up3, dact3, w_conv, b_conv):
    B, S, _ = up3.shape
    nj = D_FF // CONV_TILE
    a_spec = pl.BlockSpec((1, S, CONV_TILE), lambda j, b: (b, 0, j))
    v_spec = pl.BlockSpec((1, S, CONV_TILE), lambda j, b: (b, 0, nj + j))
    o_spec = pl.BlockSpec((2, 1, S, CONV_TILE), lambda j, b: (0, b, 0, j))
    w_spec = pl.BlockSpec((3, CONV_TILE), lambda j, b: (0, j))
    b_spec = pl.BlockSpec((1, CONV_TILE), lambda j, b: (0, j))

    def body(a_ref, v_ref, d_ref, w_ref, b_ref, dup_ref, dw_ref, db_ref):
        @pl.when(pl.program_id(1) == 0)
        def _():
            dw_ref[...] = jnp.zeros_like(dw_ref)
            db_ref[...] = jnp.zeros_like(db_ref)

        a = a_ref[0].astype(F32)
        d = d_ref[0].astype(F32)
        conv, shifted = _conv_pre(a, w_ref, b_ref)
        sg = _sig(conv)
        dup_ref[1, 0] = (d * conv * sg).astype(BF16)
        dconv = d * v_ref[0] * (sg * (1.0 + conv * (1.0 - sg)))
        da = w_ref[0:1, :] * dconv
        for j in (1, 2):
            da = da + w_ref[j:j + 1, :] * _shift_rows(dconv, j, up=True)
        dup_ref[0, 0] = da.astype(BF16)
        db_ref[...] += jnp.sum(dconv, axis=0, keepdims=True)
        dw_ref[0:1, :] += jnp.sum(dconv * a, axis=0, keepdims=True)
        dw_ref[1:2, :] += jnp.sum(dconv * shifted[0], axis=0, keepdims=True)
        dw_ref[2:3, :] += jnp.sum(dconv * shifted[1], axis=0, keepdims=True)

    return _pcall(body, name="conv_bwd",
                  out_shape=(_sds((2, B, S, D_FF), BF16), _sds((3, D_FF), F32), _sds((1, D_FF), F32)),
                  grid=(nj, B), in_specs=[a_spec, v_spec, a_spec, w_spec, b_spec],
                  out_specs=(o_spec, w_spec, b_spec), dims=("arbitrary", "arbitrary"))(up3, up3, dact3, w_conv, b_conv)


def _rows_tile(r, cap=640):
    for t in range(min(r, cap) - min(r, cap) % 8, 7, -8):
        if r % t == 0:
            return t
    return r


def _add2(a, b, out_dtype, name):
    R, N = a.shape
    tr = _rows_tile(R)
    spec = pl.BlockSpec((tr, N), lambda i: (i, 0))

    def body(a_ref, b_ref, o_ref):
        o_ref[...] = (a_ref[...] + b_ref[...]).astype(out_dtype)

    return _pcall(body, name=name, out_shape=_sds((R, N), out_dtype), grid=(R // tr,), in_specs=[spec, spec],
                  out_specs=spec, dims=("parallel",))(a, b)


def _sum_slots(q, name):
    n, R, N = q.shape
    tr = _rows_tile(R)

    def body(q_ref, o_ref):
        acc = q_ref[0].astype(F32)
        for s in range(1, n):
            acc = acc + q_ref[s].astype(F32)
        o_ref[...] = acc

    return _pcall(body, name=name, out_shape=_sds((R, N), F32), grid=(R // tr,),
                  in_specs=[pl.BlockSpec((n, tr, N), lambda i: (0, i, 0))], out_specs=pl.BlockSpec((tr, N), lambda i: (i, 0)),
                  dims=("parallel",))(q)


NATIVE = (("b_re", 16, 1024), ("b_im", 16, 1024), ("c_re", 16, 1024), ("c_im", 16, 1024), ("g_mix", 1, 1024),
          ("b_att", 1, 1024), ("b_ssm", 1, 1024), ("a_re", 1, 1024), ("a_im", 1, 1024), ("log_dt", 1, 128),
          ("d_skip", 1, 256), ("b_glu", 1, 256), ("g_ffn", 1, 1024), ("g_final", 1, 1024), ("b_conv", 1, 2048),
          ("w_conv", 3, 2048), ("loss", 1, 1))
N_MOD = 6
NATIVE_LATE = ("g_mix",)
MODS_LATE = (0, 1)


def _small_plan(late):
    pieces = [p for p in NATIVE if (p[0] in NATIVE_LATE) == late]
    mods = [k for k in range(N_MOD) if (k in MODS_LATE) == late]
    starts, r = {}, 0
    for name, rows, cols in pieces:
        starts[name] = r
        r += rows * (-(-cols // LANES))
    return pieces, mods, starts, -(-r // 8) * 8


def _pack_small(native, dmods, late):
    pieces, mods, starts, n_sum = _small_plan(late)
    B = dmods[mods[0]].shape[0]
    total = n_sum + 8 * len(mods)

    def body(*refs):
        xs, ms, o_ref = refs[:len(pieces)], refs[len(pieces):-1], refs[-1]
        o_ref[...] = jnp.zeros_like(o_ref)
        for (name, rows, cols), x_ref in zip(pieces, xs):
            chunks = -(-cols // LANES)
            if chunks == 1 and rows % 8 == 0:
                o_ref[starts[name]:starts[name] + rows, 0:cols] = x_ref[...]
                continue
            for i in range(rows):
                for q in range(chunks):
                    wd = min(LANES, cols - q * LANES)
                    r = starts[name] + i * chunks + q
                    o_ref[r:r + 1, 0:wd] = x_ref[i:i + 1, q * LANES:q * LANES + wd]
        for k, m_ref in enumerate(ms):
            for b in range(B):
                o_ref[n_sum + 8 * k + b:n_sum + 8 * k + b + 1, :] = m_ref[b]

    return _pcall(body, name="pack_small_late" if late else "pack_small_early", out_shape=_sds((total, LANES), F32))(
        *[native[n] for n, _, _ in pieces], *[dmods[k] for k in mods])


def _sum_unpack_small(gathered_early, gathered_late, B):
    plans = [_small_plan(False), _small_plan(True)]
    nd = gathered_early.shape[0]
    n_out = len(NATIVE)

    def body(*refs):
        g_refs, outs, dm_ref, accs = refs[0:2], refs[2:2 + n_out], refs[2 + n_out], refs[3 + n_out:]
        o = 0
        for g_ref, acc, (pieces, mods, starts, n_sum) in zip(g_refs, accs, plans):
            s = g_ref[0, 0:n_sum, :]
            for d in range(1, nd):
                s = s + g_ref[d, 0:n_sum, :]
            acc[...] = s
            for name, rows, cols in pieces:
                o_ref = outs[o]
                o += 1
                chunks = -(-cols // LANES)
                if chunks == 1 and rows % 8 == 0:
                    o_ref[...] = acc[starts[name]:starts[name] + rows, 0:cols]
                    continue
                for i in range(rows):
                    for q in range(chunks):
                        wd = min(LANES, cols - q * LANES)
                        r = starts[name] + i * chunks + q
                        o_ref[i:i + 1, q * LANES:q * LANES + wd] = acc[r:r + 1, 0:wd]
            for d in range(nd):
                for j, k in enumerate(mods):
                    dm_ref[d, :, k * D_MODEL:(k + 1) * D_MODEL] = g_ref[d, n_sum + 8 * j:n_sum + 8 * j + B, :]

    ordered = [p for pieces, _, _, _ in plans for p in pieces]
    out_shape = tuple(_sds((rows, cols), F32) for _, rows, cols in ordered) + (_sds((nd, B, N_MOD * D_MODEL), F32),)
    res = _pcall(body, name="sum_unpack_small", out_shape=out_shape,
                 scratch_shapes=[pltpu.VMEM((n_sum, LANES), F32) for _, _, _, n_sum in plans])(gathered_early, gathered_late)
    return {n: r for (n, _, _), r in zip(ordered, res[:-1])}, res[-1]


def _small_from_native(nat):
    lanes3 = lambda a: a.reshape(SSM_GROUP_CH, SSM_GROUPS, SSM_STATE)
    return dict(
        g_mix=nat["g_mix"].reshape(D_MODEL), b_gate=jnp.concatenate([nat["b_att"], nat["b_ssm"]], axis=1).reshape(2 * D_MODEL),
        a_re=nat["a_re"].reshape(SSM_GROUPS, SSM_STATE), a_im=nat["a_im"].reshape(SSM_GROUPS, SSM_STATE),
        log_dt=nat["log_dt"][0, :SSM_GROUPS], b_re=_groups_from_lanes(nat["b_re"]), b_im=_groups_from_lanes(nat["b_im"]),
        c_re=lanes3(nat["c_re"]).transpose(1, 0, 2), c_im=lanes3(nat["c_im"]).transpose(1, 0, 2),
        d_skip=nat["d_skip"].reshape(SSM_WIDTH), b_glu=nat["b_glu"].reshape(SSM_WIDTH), g_ffn=nat["g_ffn"].reshape(D_MODEL),
        w_conv=nat["w_conv"], b_conv=nat["b_conv"].reshape(D_FF), g_final=nat["g_final"].reshape(D_MODEL))


def _adamw_multi(params):
    n = len(params)
    bc1 = 1.0 - ADAM_B1 ** ADAM_STEP
    bc2 = 1.0 - ADAM_B2 ** ADAM_STEP

    def body(*refs):
        ins, outs = refs[:4 * n], refs[4 * n:]
        for i in range(n):
            w_ref, g_ref, m_ref, v_ref = ins[4 * i:4 * i + 4]
            d_ref, nm_ref, nv_ref = outs[3 * i:3 * i + 3]
            g = g_ref[...]
            m = ADAM_B1 * m_ref[...] + (1.0 - ADAM_B1) * g
            v = ADAM_B2 * v_ref[...] + (1.0 - ADAM_B2) * (g * g)
            nm_ref[...] = m
            nv_ref[...] = v
            d_ref[...] = -ADAM_LR * ((m / bc1) / (jnp.sqrt(v / bc2) + ADAM_EPS) + ADAM_WD * w_ref[...])

    flat = [a for p in params for a in p]
    out_shape = tuple(_sds(p[0].shape, F32) for p in params for _ in range(3))
    res = _pcall(body, name="adamw_small", out_shape=out_shape)(*flat)
    return [tuple(res[3 * i:3 * i + 3]) for i in range(n)]


def _adamw(w, g, m, v, name):
    R, N = w.shape
    tr = _rows_tile(R) if R * N * 4 > (1 << 20) else R
    tr = min(tr, 256) if R % 256 == 0 and R > 256 else tr
    spec = pl.BlockSpec((tr, N), lambda i: (i, 0))
    bc1 = 1.0 - ADAM_B1 ** ADAM_STEP
    bc2 = 1.0 - ADAM_B2 ** ADAM_STEP

    def body(w_ref, g_ref, m_ref, v_ref, d_ref, nm_ref, nv_ref):
        g = g_ref[...]
        m = ADAM_B1 * m_ref[...] + (1.0 - ADAM_B1) * g
        v = ADAM_B2 * v_ref[...] + (1.0 - ADAM_B2) * (g * g)
        nm_ref[...] = m
        nv_ref[...] = v
        d_ref[...] = -ADAM_LR * ((m / bc1) / (jnp.sqrt(v / bc2) + ADAM_EPS) + ADAM_WD * w_ref[...])

    shp = _sds((R, N), F32)
    return _pcall(body, name=name, out_shape=(shp, shp, shp), grid=(R // tr,), in_specs=[spec] * 4,
                  out_specs=(spec, spec, spec), dims=("parallel",))(w, g, m, v)


_GROUP_MASKS = {
    "all": [(dx, dy, dc) for dx in (0, 1) for dy in (0, 1) for dc in (0, 1) if (dx, dy, dc) != (0, 0, 0)],
    "xy": [(1, 0, 0), (0, 1, 0), (1, 1, 0)],
    "c": [(0, 0, 1)],
}
_GROUP_SLOTS = {"all": 8, "xy": 4, "c": 2}


def _group_slot(group, x, y, c):
    return {"all": 4 * x + 2 * y + c, "xy": 2 * x + y, "c": c}[group]


def _flip(v, d):
    return 1 - v if d else v


def _exchange(arr, group, mode, name):
    return _exchange_list([arr], group, mode, name)[0]


def _exchange_list(arrs, group, mode, name):
    masks = _GROUP_MASKS[group]
    n = len(masks)
    na = len(arrs)
    out_shapes, halves, bounce = [], [], []
    for arr in arrs:
        if mode == "gather":
            out_shapes.append((_GROUP_SLOTS[group],) + arr.shape)
            bounce.append(pltpu.VMEM(arr.shape, arr.dtype))
        elif mode == "scatter":
            assert arr.shape[0] == _GROUP_SLOTS[group]
            out_shapes.append(arr.shape)
            bounce.append(pltpu.VMEM(arr.shape[1:], arr.dtype))
        elif mode == "swap":
            assert group == "c"
            out_shapes.append(arr.shape)
        else:
            assert group == "c"
            halves.append(arr.shape[1] // 2)
            out_shapes.append((arr.shape[0], arr.shape[1] // 2, arr.shape[2]))
    has_local = mode in ("gather", "scatter")

    def body(*refs):
        x_refs, o_refs = refs[:na], refs[na:2 * na]
        send_sems, recv_sems = refs[2 * na], refs[2 * na + 1]
        x, y, c = lax.axis_index("x"), lax.axis_index("y"), lax.axis_index("c")
        me = _group_slot(group, x, y, c)
        if has_local:
            local_sems = refs[2 * na + 2]
            bufs = refs[2 * na + 3:]
            loads = []
            for i in range(na):
                src = x_refs[i] if mode == "gather" else x_refs[i].at[me]
                loads.append(pltpu.make_async_copy(src, bufs[i], local_sems.at[2 * i]))
                loads[-1].start()
        copies = []
        for i in range(na):
            x_ref, o_ref = x_refs[i], o_refs[i]
            for k, (dx, dy, dc) in enumerate(masks):
                px, py, pc = _flip(x, dx), _flip(y, dy), _flip(c, dc)
                if mode == "gather":
                    src, dst = x_ref, o_ref.at[me]
                elif mode == "scatter":
                    src, dst = x_ref.at[_group_slot(group, px, py, pc)], o_ref.at[me]
                elif mode == "swap":
                    src, dst = x_ref, o_ref
                else:
                    src, dst = x_ref.at[:, pl.ds(pl.multiple_of(pc * halves[i], 8), halves[i]), :], o_ref
                cp = pltpu.make_async_remote_copy(src_ref=src, dst_ref=dst, send_sem=send_sems.at[i * n + k],
                                                  recv_sem=recv_sems.at[i * n + k], device_id=(px, py, pc),
                                                  device_id_type=pl.DeviceIdType.MESH)
                cp.start()
                copies.append(cp)
        if has_local:
            stores = []
            for i in range(na):
                loads[i].wait()
                stores.append(pltpu.make_async_copy(bufs[i], o_refs[i].at[me], local_sems.at[2 * i + 1]))
                stores[-1].start()
        for cp in copies:
            cp.wait()
        if has_local:
            for st in stores:
                st.wait()

    anyspec = pl.BlockSpec(memory_space=pl.ANY)
    scratch = [pltpu.SemaphoreType.DMA((n * na,)), pltpu.SemaphoreType.DMA((n * na,))]
    if has_local:
        scratch += [pltpu.SemaphoreType.DMA((2 * na,))] + bounce
    outs = pl.pallas_call(body, name=name, out_shape=tuple(_sds(s, a.dtype) for s, a in zip(out_shapes, arrs)),
                          in_specs=[anyspec] * na, out_specs=tuple([anyspec] * na), scratch_shapes=scratch,
                          compiler_params=pltpu.CompilerParams(vmem_limit_bytes=V7X_VMEM_LIMIT_BYTES))(*arrs)
    return list(outs)


def _gather_weights(shards, name):
    na = len(shards)
    masks = _GROUP_MASKS["xy"]
    n = len(masks)

    def body(*refs):
        x_refs, o_refs = refs[:na], refs[na:2 * na]
        send_sems, recv_sems, local_sems = refs[2 * na:2 * na + 3]
        bufs = refs[2 * na + 3:]
        x, y, c = lax.axis_index("x"), lax.axis_index("y"), lax.axis_index("c")
        me = 2 * x + y
        sibling = (x, y, 1 - c)
        loads = []
        for i in range(na):
            loads.append(pltpu.make_async_copy(x_refs[i], bufs[i], local_sems.at[2 * i]))
            loads[-1].start()

        def half_of(i, slot, cc):
            h = shards[i].shape[0] // 2
            return o_refs[i].at[slot, pl.ds(pl.multiple_of(cc * h, 8), h), :]

        def src_half(i, cc):
            h = shards[i].shape[0] // 2
            return x_refs[i].at[pl.ds(pl.multiple_of(cc * h, 8), h), :]

        sends = []
        for i in range(na):
            for k, (dx, dy, _) in enumerate(masks):
                cp = pltpu.make_async_remote_copy(src_ref=src_half(i, c), dst_ref=half_of(i, me, c),
                                                  send_sem=send_sems.at[i * 2 * n + k], recv_sem=recv_sems.at[i * 2 * n + k],
                                                  device_id=(_flip(x, dx), _flip(y, dy), c),
                                                  device_id_type=pl.DeviceIdType.MESH)
                cp.start()
                sends.append(cp)
        stores = []
        for i in range(na):
            loads[i].wait()
            stores.append(pltpu.make_async_copy(bufs[i], o_refs[i].at[me], local_sems.at[2 * i + 1]))
            stores[-1].start()
        for i in range(na):
            for k, (dx, dy, _) in enumerate(masks):
                slot = 2 * _flip(x, dx) + _flip(y, dy)
                landed = pltpu.make_async_remote_copy(src_ref=src_half(i, c), dst_ref=half_of(i, slot, c),
                                                      send_sem=send_sems.at[i * 2 * n + k],
                                                      recv_sem=recv_sems.at[i * 2 * n + k], device_id=sibling,
                                                      device_id_type=pl.DeviceIdType.MESH)
                landed.wait_recv()
                fwd = pltpu.make_async_remote_copy(src_ref=half_of(i, slot, c), dst_ref=half_of(i, slot, c),
                                                   send_sem=send_sems.at[i * 2 * n + n + k],
                                                   recv_sem=recv_sems.at[i * 2 * n + n + k], device_id=sibling,
                                                   device_id_type=pl.DeviceIdType.MESH)
                fwd.start()
                sends.append(fwd)
        for i in range(na):
            for k, (dx, dy, _) in enumerate(masks):
                slot = 2 * _flip(x, dx) + _flip(y, dy)
                pltpu.make_async_remote_copy(src_ref=half_of(i, slot, 1 - c), dst_ref=half_of(i, slot, 1 - c),
                                             send_sem=send_sems.at[i * 2 * n + n + k],
                                             recv_sem=recv_sems.at[i * 2 * n + n + k], device_id=sibling,
                                             device_id_type=pl.DeviceIdType.MESH).wait_recv()
        for cp in sends:
            cp.wait_send()
        for st in stores:
            st.wait()

    anyspec = pl.BlockSpec(memory_space=pl.ANY)
    scratch = [pltpu.SemaphoreType.DMA((2 * n * na,)), pltpu.SemaphoreType.DMA((2 * n * na,)),
               pltpu.SemaphoreType.DMA((2 * na,))] + [pltpu.VMEM(s.shape, s.dtype) for s in shards]
    outs = pl.pallas_call(body, name=name, out_shape=tuple(_sds((N_XY,) + s.shape, s.dtype) for s in shards),
                          in_specs=[anyspec] * na, out_specs=tuple([anyspec] * na), scratch_shapes=scratch,
                          compiler_params=pltpu.CompilerParams(vmem_limit_bytes=V7X_VMEM_LIMIT_BYTES))(*shards)
    return list(outs)


def _pair_add(g, theirs, core, name):
    n4, h2, w = g.shape
    h = h2 // 2
    tr = _rows_tile(h)
    nb = h // tr

    def body(c_ref, g_ref, t_ref, o_ref):
        o_ref[...] = (g_ref[...] + t_ref[...]).astype(BF16)

    grid_spec = pltpu.PrefetchScalarGridSpec(
        num_scalar_prefetch=1, grid=(n4, nb),
        in_specs=[pl.BlockSpec((None, tr, w), lambda j, i, c_ref: (j, c_ref[0] * nb + i, 0)),
                  pl.BlockSpec((None, tr, w), lambda j, i, c_ref: (j, i, 0))],
        out_specs=pl.BlockSpec((None, tr, w), lambda j, i, c_ref: (j, i, 0)))
    return pl.pallas_call(body, name=name, out_shape=_sds((n4, h, w), BF16), grid_spec=grid_spec,
                          compiler_params=pltpu.CompilerParams(vmem_limit_bytes=V7X_VMEM_LIMIT_BYTES,
                                                               dimension_semantics=("parallel", "parallel")))(core, g, theirs)


BIG = (("w_proj_att", (ATT_WIDTH, D_MODEL), 1), ("w_proj_ssm", (SSM_WIDTH, D_MODEL), 1),
       ("w_glu", (SSM_WIDTH, SSM_WIDTH), 0))
DIRECT = (("w_in", True), ("w_up", True), ("w_down", False), ("w_out", False))
N_XY = 4


def _big_rows(shape):
    return shape[0] * shape[1] // N_XY // LANES


FLAT_ROWS = sum(_big_rows(s) for _, s, _ in BIG)


def _shard_shape(shape, axis):
    return (shape[0] // N_XY, shape[1]) if axis == 0 else (shape[0], shape[1] // N_XY)


def _flatten_shards(shards):
    return jnp.concatenate([shards[n].reshape(_big_rows(s), LANES) for n, s, _ in BIG], axis=0)


def _unflatten_shard(flat):
    out, r = {}, 0
    for n, s, ax in BIG:
        k = _big_rows(s)
        out[n] = flat[r:r + k].reshape(_shard_shape(s, ax))
        r += k
    return out


def _unflatten_full(flat4):
    out, r = {}, 0
    for n, s, ax in BIG:
        k = _big_rows(s)
        sh = _shard_shape(s, ax)
        t = flat4[:, r:r + k].reshape((N_XY,) + sh)
        out[n] = t.reshape(s) if ax == 0 else t.transpose(1, 0, 2).reshape(s)
        r += k
    return out


def _flatten_full(full):
    parts = []
    for n, s, ax in BIG:
        sh = _shard_shape(s, ax)
        t = full[n]
        t = t.reshape((N_XY,) + sh) if ax == 0 else t.reshape(s[0], N_XY, sh[1]).transpose(1, 0, 2)
        parts.append(t.reshape(N_XY, _big_rows(s), LANES))
    return jnp.concatenate(parts, axis=1)


def _pack_rows(arrs):
    rows, counts = [], []
    for a in arrs:
        f = a.reshape(-1)
        k = -(-f.shape[0] // LANES)
        rows.append(jnp.pad(f, (0, k * LANES - f.shape[0])).reshape(k, LANES))
        counts.append(k)
    return jnp.concatenate(rows, axis=0), counts


def _unpack_rows(buf, shapes):
    out, r = [], 0
    for s in shapes:
        size = int(np.prod(s))
        k = -(-size // LANES)
        out.append(buf[r:r + k].reshape(-1)[:size].reshape(s))
        r += k
    return out


def _lanes_from_groups(a):
    return a.transpose(2, 0, 1).reshape(SSM_GROUP_CH, SSM_LANES)


def _groups_from_lanes(a):
    return a.reshape(SSM_GROUP_CH, SSM_GROUPS, SSM_STATE).transpose(1, 2, 0)


LATE = ("w_up_t", "w_down", "w_out")
EARLY_GRADS = ("w_up_t", "w_down", "w_out")


def _local_step(x3, mod, tgt3, W, P, late_shards=None, scatter_grads=False):
    B, S, _ = x3.shape
    T = B * S
    seq_blocks = S // ATT_BLOCK
    sh1, sc1, gt1, sh2, sc2, gt2 = [m.reshape(B, 1, D_MODEL) for m in jnp.split(mod, 6, axis=-1)]
    g_mix, g_ffn, g_final = P["g_mix"].reshape(1, D_MODEL), P["g_ffn"].reshape(1, D_MODEL), P["g_final"].reshape(1, D_MODEL)
    b_gate = P["b_gate"].reshape(1, 2 * D_MODEL)
    d_skip, b_glu = P["d_skip"].reshape(1, SSM_WIDTH), P["b_glu"].reshape(1, SSM_WIDTH)
    w_conv, b_conv = P["w_conv"], P["b_conv"].reshape(1, D_FF)

    u1 = _norm_mod(x3, g_mix, sc1, sh1).reshape(T, D_MODEL)
    proj = _mm(u1, W["w_in_t"], tb=True, name="mm_proj", out_dtype=BF16)
    proj3 = proj.reshape(B, S, IN_WIDTH)
    us = proj[:, 3 * ATT_WIDTH:3 * ATT_WIDTH + SSM_WIDTH]
    o_att3, lse4, late = _attention_fwd(proj3, seq_blocks, _Riders(late_shards, "gather") if late_shards else None)
    if late_shards:
        W = dict(W, **{n: f.reshape(-1, LANES) for n, f in zip(LATE, late)})
        w_conv = late[len(LATE)].transpose(1, 0, 2).reshape(3, D_FF)
    o_att = o_att3.reshape(T, ATT_WIDTH)
    y_att = _mm(o_att, W["w_proj_att"], name="mm_proj_att", out_dtype=BF16)

    lr = P["a_re"].reshape(1, SSM_LANES)
    li = P["a_im"].reshape(1, SSM_LANES)
    ldt = jnp.repeat(P["log_dt"], SSM_STATE).reshape(1, SSM_LANES)
    br, bi = _lanes_from_groups(P["b_re"]), _lanes_from_groups(P["b_im"])
    cr = P["c_re"].transpose(1, 0, 2).reshape(SSM_GROUP_CH, SSM_LANES)
    ci = P["c_im"].transpose(1, 0, 2).reshape(SSM_GROUP_CH, SSM_LANES)
    abar, w_bu, w_c = _ssm_params(lr, li, ldt, br, bi, cr, ci)
    xs3, y_core3 = _ssm_scan_fwd(proj3, abar, w_bu, w_c)
    y5, s_out = _ssm_post(y_core3.reshape(T, SSM_WIDTH), us, d_skip, W["w_glu"], b_glu)
    y_ssm = _mm(s_out, W["w_proj_ssm"], name="mm_proj_ssm", out_dtype=BF16)

    merged = _merge(proj, y_att, y_ssm, b_gate)
    mix = _mm(merged, W["w_out"], name="mm_out", out_dtype=BF16)
    mix3 = mix.reshape(B, S, D_MODEL)

    h1, u2 = _resid_norm_mod(x3, mix3, gt1, g_ffn, sc2, sh2)
    u2 = u2.reshape(T, D_MODEL)
    up3 = _mm(u2, W["w_up_t"], tb=True, name="mm_up", out_dtype=BF16).reshape(B, S, 2 * D_FF)
    act = _conv_act(up3, w_conv, b_conv).reshape(T, D_FF)
    ffn3 = _mm(act, W["w_down"], name="mm_down", out_dtype=BF16).reshape(B, S, D_MODEL)
    dh2, dffn, dgt2, dg_final, loss = _final_loss(h1, ffn3, tgt3, gt2, g_final)

    dffn = dffn.reshape(T, D_MODEL)
    gw = {}
    gw["w_down"] = _mm(act, dffn, ta=True, out_dtype=BF16, name="mm_dw_down")
    dact3 = _mm(dffn, W["w_down"], tb=True, name="mm_dact", out_dtype=BF16).reshape(B, S, D_FF)
    dup3, dw_conv, db_conv = _conv_bwd(up3, dact3, w_conv, b_conv)
    dup = dup3.reshape(2, T, D_FF)
    gw["w_up_t"] = _mm(dup, u2, ta=True, out_dtype=BF16, name="mm_dw_up")
    du2 = _mm(dup, W["w_up_t"], name="mm_du2", out_dtype=BF16).reshape(B, S, D_MODEL)
    dh1, dsh2, dsc2, dg_ffn, dgt1, dmix = _norm_bwd(h1, du2, dh2, g_ffn, sc2, "norm_bwd2", mix3=mix3, gt=gt1)

    dmix = dmix.reshape(T, D_MODEL)
    gw["w_out"] = _mm(merged, dmix, ta=True, out_dtype=BF16, name="mm_dw_out")
    dmerged = _mm(dmix, W["w_out"], tb=True, name="mm_dmerged", out_dtype=BF16)
    dy_att, dy_ssm, dga, dgs, db_att, db_ssm = _merge_bwd(proj, y_att, y_ssm, b_gate, dmerged)

    gw["w_proj_ssm"] = _mm(s_out, dy_ssm, ta=True, name="mm_dw_proj_ssm")
    ds_out = _mm(dy_ssm, W["w_proj_ssm"], tb=True, name="mm_ds_out")
    dy5, dd_skip, db_glu, dw_glu = _ssm_post_bwd(y5, us, ds_out, d_skip, W["w_glu"], b_glu)
    gw["w_glu"] = dw_glu
    dus3, dab, dwbu, dwc = _ssm_scan_bwd(proj3, dy5.reshape(B, S, SSM_WIDTH), xs3, abar, w_bu, w_c, d_skip)
    dus = dus3.reshape(T, SSM_WIDTH)
    dlr, dli, dldt, dbr, dbi, dcr, dci = _ssm_params_bwd(lr, li, ldt, br, bi, dab, dwbu, dwc)

    gw["w_proj_att"] = _mm(o_att, dy_att, ta=True, name="mm_dw_proj_att")
    do_att = _mm(dy_att, W["w_proj_att"], tb=True, out_dtype=BF16, name="mm_do_att")
    early = [gw[n].reshape(N_XY, -1, LANES) for n in EARLY_GRADS]
    early.append(_flatten_full({n: gw[n].astype(BF16) for n, _, _ in BIG}))
    dq3, dk3, dv3, parts = _attention_bwd(proj3, do_att.reshape(B, S, ATT_WIDTH), o_att3, lse4, seq_blocks,
                                          _Riders(early, "scatter") if scatter_grads else None)
    dproj = jnp.concatenate([t.reshape(T, ATT_WIDTH) for t in (dq3, dk3, dv3)] + [dus, dga, dgs], axis=1)
    dmods = [None, None, dgt1, dsh2, dsc2, dgt2]
    native = dict(b_att=db_att, b_ssm=db_ssm, a_re=dlr, a_im=dli, log_dt=dldt, b_re=dbr, b_im=dbi, c_re=dcr, c_im=dci,
                  d_skip=dd_skip, b_glu=db_glu, g_ffn=dg_ffn, w_conv=dw_conv, b_conv=db_conv, g_final=dg_final, loss=loss)
    small_early = _pack_small(native, dmods, False)
    if scatter_grads:
        gw["w_in_t"], (small_early,) = _mm(dproj, u1, ta=True, out_dtype=BF16, name="mm_dw_in",
                                           riders=_Riders([small_early], "gather", "all"))
        du1, last_parts = _mm(dproj, W["w_in_t"], name="mm_du1", out_dtype=BF16,
                              riders=_Riders([gw["w_in_t"].reshape(N_XY, -1, LANES)], "scatter"))
        parts = parts + last_parts
    else:
        gw["w_in_t"] = _mm(dproj, u1, ta=True, out_dtype=BF16, name="mm_dw_in")
        du1 = _mm(dproj, W["w_in_t"], name="mm_du1", out_dtype=BF16)
    du1 = du1.reshape(B, S, D_MODEL)
    dx, dsh1, dsc1, dg_mix = _norm_bwd(x3, du1, dh1, g_mix, sc1, "norm_bwd1")
    dmods[0], dmods[1] = dsh1, dsc1
    native["g_mix"] = dg_mix
    return loss, dx, dmods, gw, native, parts, small_early


WEIGHTS = ['w_ada', 'b_ada', 'g_mix', 'w_in', 'b_gate', 'a_re', 'a_im', 'log_dt', 'b_re', 'b_im', 'c_re', 'c_im', 'd_skip',
           'w_glu', 'b_glu', 'w_proj_att', 'w_proj_ssm', 'w_out', 'g_ffn', 'w_up', 'w_conv', 'b_conv', 'w_down', 'g_final']
SMALL = ['g_mix', 'b_gate', 'a_re', 'a_im', 'log_dt', 'b_re', 'b_im', 'c_re', 'c_im', 'd_skip', 'b_glu', 'g_ffn', 'w_conv',
         'b_conv', 'g_final']


def kernel(x, c, w_ada, b_ada, g_mix, w_in, b_gate, a_re, a_im, log_dt, b_re, b_im, c_re, c_im, d_skip, w_glu, b_glu, w_proj_att, w_proj_ssm, w_out, g_ffn, w_up, w_conv, b_conv, w_down, g_final, loss_target, m_w_ada, m_b_ada, m_g_mix, m_w_in, m_b_gate, m_a_re, m_a_im, m_log_dt, m_b_re, m_b_im, m_c_re, m_c_im, m_d_skip, m_w_glu, m_b_glu, m_w_proj_att, m_w_proj_ssm, m_w_out, m_g_ffn, m_w_up, m_w_conv, m_b_conv, m_w_down, m_g_final, v_w_ada, v_b_ada, v_g_mix, v_w_in, v_b_gate, v_a_re, v_a_im, v_log_dt, v_b_re, v_b_im, v_c_re, v_c_im, v_d_skip, v_w_glu, v_b_glu, v_w_proj_att, v_w_proj_ssm, v_w_out, v_g_ffn, v_w_up, v_w_conv, v_b_conv, v_w_down, v_g_final):
    args = dict(locals())
    w = {n: args[n] for n in WEIGHTS}
    m = {n: args["m_" + n] for n in WEIGHTS}
    v = {n: args["v_" + n] for n in WEIGHTS}
    B, S, _ = x.shape
    ix, iy, ic = lax.axis_index("x"), lax.axis_index("y"), lax.axis_index("c")
    chip = 2 * ix + iy
    half = FLAT_ROWS // 2
    ada_cols = w_ada.shape[2]

    c_all = _exchange(c, "all", "gather", "gather_c").reshape(8 * B, D_MODEL)
    b_cols = lax.dynamic_slice_in_dim(b_ada, chip * ada_cols, ada_cols, axis=1)
    mod_cols = _ada_fwd(c_all, w_ada[0], b_cols)
    mod_all = _exchange(mod_cols, "xy", "gather", "gather_mod")
    mod_all = mod_all.transpose(1, 0, 2).reshape(8 * B, 6 * D_MODEL)
    mod = lax.dynamic_slice_in_dim(mod_all, (4 * ix + 2 * iy + ic) * B, B, axis=0)

    south = ic == 0
    core = ic.astype(jnp.int32).reshape(1)
    shard = {n + ("_t" if t else ""): (w[n][0].T if t else w[n][0]).astype(BF16) for n, t in DIRECT}
    misc = _flatten_shards({n: w[n][0] for n, _, _ in BIG}).astype(BF16)
    w_in_full, misc_full = _gather_weights([shard["w_in_t"], misc], "gather_weights")
    W = {"w_in_t": w_in_full.reshape(-1, LANES)}
    W.update(_unflatten_full(misc_full))

    P = {n: w[n][0] for n in SMALL if n not in ("w_conv", "g_final")}
    P["w_conv"] = None
    P["g_final"] = g_final

    loss, dx, dmods, gw, native, parts, small_early = _local_step(x, mod, loss_target, W, P,
                                                                  [shard[n] for n in LATE] + [w_conv[0]], True)

    small_late = _exchange(_pack_small(native, dmods, True), "all", "gather", "gather_small")
    native_sum, dmod_all = _sum_unpack_small(small_early, small_late, B)
    loss = native_sum["loss"][0, 0]
    g_small = _small_from_native(native_sum)
    dmod_all = dmod_all.reshape(8 * B, N_MOD * D_MODEL)
    dmod_cols = lax.dynamic_slice_in_dim(dmod_all, chip * ada_cols, ada_cols, axis=1)
    g_w_ada, g_b_ada = _ada_bwd(c_all, dmod_all, dmod_cols)

    red = [_sum_slots(p, "sum_chips_%d" % i) for i, p in enumerate(parts)]
    red_sib = _exchange_list(red, "c", "swap", "share_cores")
    reduced = [_add2(r, s, F32, "add_cores_%d" % i) for i, (r, s) in enumerate(zip(red, red_sib))]

    grads = {"w_ada": g_w_ada[None], "b_ada": g_b_ada}
    order = list(EARLY_GRADS) + ["misc", "w_in_t"]
    for n, g in zip(order, reduced):
        if n == "misc":
            for k, gk in _unflatten_shard(g).items():
                grads[k] = gk[None]
        else:
            grads[n[:-2] if n.endswith("_t") else n] = (g.T if n.endswith("_t") else g)[None]
    wc_cols = w_conv.shape[2]
    for n in SMALL:
        g = g_small[n]
        if n == "w_conv":
            g = lax.dynamic_slice_in_dim(g, chip * wc_cols, wc_cols, axis=1)
        grads[n] = g.reshape(w[n].shape)

    delta, new_m, new_v = {}, {}, {}
    reduced_t = dict(zip(order, reduced))
    for n in ["w_ada"] + [b for b, _ in DIRECT] + [b for b, _, _ in BIG]:
        shp = w[n].shape
        if n == "w_in":
            d2, m2, v2 = _adamw(w[n][0].T, reduced_t["w_in_t"], m[n][0].T, v[n][0].T, "adamw_" + n)
            d2, m2, v2 = d2.T, m2.T, v2.T
        else:
            d2, m2, v2 = _adamw(w[n][0], grads[n][0], m[n][0], v[n][0], "adamw_" + n)
        delta[n], new_m[n], new_v[n] = d2.reshape(shp), m2.reshape(shp), v2.reshape(shp)
    rest = ["b_ada"] + SMALL

    def drop(a):
        return a.reshape(1, -1) if a.ndim == 1 else (a if a.ndim == 2 else a[0])

    upd = _adamw_multi([(drop(w[n]), drop(grads[n]), drop(m[n]), drop(v[n])) for n in rest])
    for n, (dd, mm, vv) in zip(rest, upd):
        delta[n], new_m[n], new_v[n] = dd.reshape(w[n].shape), mm.reshape(w[n].shape), vv.reshape(w[n].shape)

    return (loss, dx, *[grads[n] for n in WEIGHTS], *[delta[n] for n in WEIGHTS], *[new_m[n] for n in WEIGHTS],
            *[new_v[n] for n in WEIGHTS])
```

```python
import functools
import math

import numpy as np
import jax
import jax.numpy as jnp
from jax import lax
from jax.experimental import pallas as pl
from jax.experimental.pallas import tpu as pltpu

F32, BF16 = jnp.float32, jnp.bfloat16

D_MODEL = 1024
N_HEADS = 8
HEAD_DIM = 64
ATT_WIDTH = 512
SSM_GROUPS = 16
SSM_GROUP_CH = 16
SSM_WIDTH = 256
SSM_STATE = 64
SSM_LANES = SSM_GROUPS * SSM_STATE
D_FF = 2048
IN_WIDTH = 3 * ATT_WIDTH + SSM_WIDTH + 2 * D_MODEL
ATT_BLOCK = 128
N_PATTERNS = 3
EPS = 1e-6
NEG_INF = -1e30

ADAM_LR, ADAM_B1, ADAM_B2, ADAM_EPS, ADAM_WD, ADAM_STEP = 0.001, 0.9, 0.999, 1e-08, 0.01, 10

V7X_VMEM_LIMIT_BYTES = 56 * 1024 * 1024
LANES = 1024

MESH_AXES = ("x", "y", "c")


def _pcall(body, *, name, out_shape, grid=(), in_specs=None, out_specs=None, scratch_shapes=(), dims=None):
    params = dict(vmem_limit_bytes=V7X_VMEM_LIMIT_BYTES)
    if dims is not None:
        params["dimension_semantics"] = dims
    specs = {}
    if in_specs is not None:
        specs = dict(grid=grid, in_specs=in_specs, out_specs=out_specs)
    return pl.pallas_call(body, name=name, out_shape=out_shape, scratch_shapes=scratch_shapes,
                          compiler_params=pltpu.CompilerParams(**params), **specs)


def _sds(shape, dtype):
    return jax.ShapeDtypeStruct(tuple(shape), dtype)


def _tile(n, target):
    if n <= target:
        return n
    for t in range(target - target % 128, 0, -128):
        if n % t == 0:
            return t
    raise ValueError((n, target))


def _sig(v):
    return pl.reciprocal(1.0 + jnp.exp(-v), approx=True)


def _mm(a, b, *, name, ta=False, tb=False, out_dtype=F32, tm=2048, tn=1024, tk=1024, riders=None):
    halves = a.ndim == 3
    if halves:
        a_rows, a_cols = a.shape[1], 2 * a.shape[2]
    else:
        a_rows, a_cols = a.shape
    if ta:
        K, M = a_rows, a_cols
    else:
        M, K = a_rows, a_cols
    if tb:
        N, K2 = b.shape
    else:
        K2, N = b.shape
    assert K == K2, (a.shape, b.shape)
    if halves:
        tm, tk = (min(tm, M // 2), tk) if ta else (tm, min(tk, K // 2))
    tm, tn, tk = _tile(M, tm), _tile(N, tn), _tile(K, tk)
    nk = K // tk
    if halves and ta:
        per = a.shape[2] // tm
        a_spec = pl.BlockSpec((None, tk, tm), lambda i, j, k: (i // per, k, i % per))
    elif halves:
        per = a.shape[2] // tk
        a_spec = pl.BlockSpec((None, tm, tk), lambda i, j, k: (k // per, i, k % per))
    else:
        a_spec = pl.BlockSpec((tk, tm), lambda i, j, k: (k, i)) if ta else pl.BlockSpec((tm, tk), lambda i, j, k: (i, k))
    b_spec = pl.BlockSpec((tn, tk), lambda i, j, k: (j, k)) if tb else pl.BlockSpec((tk, tn), lambda i, j, k: (k, j))
    dn = (((0 if ta else 1,), (1 if tb else 0,)), ((), ()))

    def body(a_ref, b_ref, o_ref, acc_ref):
        k = pl.program_id(2)

        @pl.when(k == 0)
        def _():
            acc_ref[...] = jnp.zeros_like(acc_ref)

        acc_ref[...] += lax.dot_general(a_ref[...].astype(BF16), b_ref[...].astype(BF16), dn,
                                        preferred_element_type=F32)

        @pl.when(k == nk - 1)
        def _():
            o_ref[...] = acc_ref[...].astype(out_dtype)

    def body_single(a_ref, b_ref, o_ref):
        o_ref[...] = lax.dot_general(a_ref[...].astype(BF16), b_ref[...].astype(BF16), dn,
                                     preferred_element_type=F32).astype(out_dtype)

    grid = (M // tm, N // tn, nk)
    scratch = [] if nk == 1 else [pltpu.VMEM((tm, tn), F32)]
    o_spec = pl.BlockSpec((tm, tn), lambda i, j, k: (i, j))
    if riders is None:
        return _pcall(body_single if nk == 1 else body, name=name, out_shape=_sds((M, N), out_dtype), grid=grid,
                      in_specs=[a_spec, b_spec], out_specs=o_spec, scratch_shapes=scratch,
                      dims=("parallel", "parallel", "arbitrary"))(a, b)
    rs = riders
    res = _pcall(_with_riders(body_single if nk == 1 else body, rs, 2, 1, len(scratch), tuple(g - 1 for g in grid)),
                 name=name, out_shape=(_sds((M, N), out_dtype),) + tuple(rs.out_shape), grid=grid,
                 in_specs=[a_spec, b_spec] + rs.specs, out_specs=(o_spec,) + tuple(rs.specs),
                 scratch_shapes=scratch + rs.scratch, dims=("arbitrary", "arbitrary", "arbitrary"))(a, b, *rs.arrs)
    return res[0], list(res[1:])


def _ada_fwd(c_all, w_ada, b_ada_cols):
    n = w_ada.shape[1]

    def body(c_ref, w_ref, b_ref, o_ref):
        c = c_ref[...]
        act = c * _sig(c)
        o_ref[...] = jnp.dot(act.astype(BF16), w_ref[...].astype(BF16), preferred_element_type=F32) + b_ref[...]

    return _pcall(body, name="ada_fwd", out_shape=_sds((c_all.shape[0], n), F32))(c_all, w_ada, b_ada_cols)


def _ada_bwd(c_all, dmod_all, dmod_cols):
    n = dmod_cols.shape[1]

    def body(c_ref, da_ref, dc_ref, gw_ref, gb_ref):
        c = c_ref[...]
        act = c * _sig(c)
        gw_ref[...] = lax.dot_general(act, dc_ref[...], (((0,), (0,)), ((), ())), preferred_element_type=F32,
                                      precision=lax.Precision.HIGHEST)
        gb_ref[...] = jnp.sum(da_ref[...], axis=0, keepdims=True)

    return _pcall(body, name="ada_bwd", out_shape=(_sds((D_MODEL, n), F32), _sds((1, dmod_all.shape[1]), F32)))(
        c_all, dmod_all, dmod_cols)


ROW_TILE = 512


def _row_specs(B, S):
    ts = min(S, ROW_TILE)
    row = pl.BlockSpec((1, ts, D_MODEL), lambda b, s: (b, s, 0))
    bvec = pl.BlockSpec((1, 1, D_MODEL), lambda b, s: (b, 0, 0))
    gvec = pl.BlockSpec((1, D_MODEL), lambda b, s: (0, 0))
    return ts, row, bvec, gvec


def _norm_mod(x3, g, sc, sh):
    B, S, _ = x3.shape
    ts, row, bvec, gvec = _row_specs(B, S)

    def body(x_ref, g_ref, sc_ref, sh_ref, u_ref):
        x = x_ref[0]
        r = lax.rsqrt(jnp.mean(x * x, axis=-1, keepdims=True) + EPS)
        u_ref[0] = ((x * r) * g_ref[...] * (1.0 + sc_ref[0]) + sh_ref[0]).astype(BF16)

    return _pcall(body, name="norm_mod1", out_shape=_sds(x3.shape, BF16), grid=(B, S // ts),
                  in_specs=[row, gvec, bvec, bvec], out_specs=row, dims=("parallel", "parallel"))(x3, g, sc, sh)


def _resid_norm_mod(x3, mix3, gt, g, sc, sh):
    B, S, _ = x3.shape
    ts, row, bvec, gvec = _row_specs(B, S)

    def body(x_ref, m_ref, gt_ref, g_ref, sc_ref, sh_ref, h_ref, u_ref):
        h = x_ref[0] + gt_ref[0] * m_ref[0]
        h_ref[0] = h
        r = lax.rsqrt(jnp.mean(h * h, axis=-1, keepdims=True) + EPS)
        u_ref[0] = ((h * r) * g_ref[...] * (1.0 + sc_ref[0]) + sh_ref[0]).astype(BF16)

    return _pcall(body, name="resid_norm_mod2", out_shape=(_sds(x3.shape, F32), _sds(x3.shape, BF16)),
                  grid=(B, S // ts), in_specs=[row, row, bvec, gvec, bvec, bvec], out_specs=(row, row),
                  dims=("parallel", "parallel"))(x3, mix3, gt, g, sc, sh)


def _norm_bwd(h3, du3, dres3, g, sc, name, mix3=None, gt=None):
    B, S, _ = h3.shape
    ts, row, bvec, gvec = _row_specs(B, S)
    with_gate = mix3 is not None

    def body(*refs):
        if with_gate:
            h_ref, du_ref, dr_ref, g_ref, sc_ref, m_ref, gt_ref, dh_ref, dsh_ref, dsc_ref, dg_ref, dgt_ref, dm_ref = refs
        else:
            h_ref, du_ref, dr_ref, g_ref, sc_ref, dh_ref, dsh_ref, dsc_ref, dg_ref = refs
        b, s = pl.program_id(0), pl.program_id(1)
        h = h_ref[0]
        r = lax.rsqrt(jnp.mean(h * h, axis=-1, keepdims=True) + EPS)
        xn = h * r
        du = du_ref[0].astype(F32)
        g = g_ref[...]
        sc1 = 1.0 + sc_ref[0]
        dxn = du * g * sc1
        dh = dr_ref[0].astype(F32) + r * (dxn - xn * jnp.mean(dxn * xn, axis=-1, keepdims=True))
        dh_ref[0] = dh.astype(dh_ref.dtype)

        @pl.when(s == 0)
        def _():
            dsh_ref[...] = jnp.zeros_like(dsh_ref)
            dsc_ref[...] = jnp.zeros_like(dsc_ref)
            if with_gate:
                dgt_ref[...] = jnp.zeros_like(dgt_ref)

        @pl.when((s == 0) & (b == 0))
        def _():
            dg_ref[...] = jnp.zeros_like(dg_ref)

        dux = du * xn
        dsh_ref[0] += jnp.sum(du, axis=0, keepdims=True)
        dsc_ref[0] += jnp.sum(dux * g, axis=0, keepdims=True)
        dg_ref[...] += jnp.sum(dux * sc1, axis=0, keepdims=True)
        if with_gate:
            dgt_ref[0] += jnp.sum(dh * m_ref[0], axis=0, keepdims=True)
            dm_ref[0] = (dh * gt_ref[0]).astype(BF16)

    bshape = _sds((B, 1, D_MODEL), F32)
    in_specs = [row, row, row, gvec, bvec]
    out_shape = [_sds(h3.shape, BF16 if with_gate else F32), bshape, bshape, _sds((1, D_MODEL), F32)]
    out_specs = [row, bvec, bvec, gvec]
    args = [h3, du3, dres3, g, sc]
    if with_gate:
        in_specs += [row, bvec]
        out_shape += [bshape, _sds(h3.shape, BF16)]
        out_specs += [bvec, row]
        args += [mix3, gt]
    return _pcall(body, name=name, out_shape=tuple(out_shape), grid=(B, S // ts), in_specs=in_specs,
                  out_specs=tuple(out_specs), dims=("arbitrary", "arbitrary"))(*args)


def _final_loss(h1, ffn3, tgt3, gt, gfin):
    B, S, _ = h1.shape
    ts, row, bvec, gvec = _row_specs(B, S)
    one = pl.BlockSpec((1, 1), lambda b, s: (0, 0))

    def body(h_ref, f_ref, t_ref, gt_ref, gf_ref, dh_ref, dff_ref, dgt_ref, dgf_ref, loss_ref):
        b, s = pl.program_id(0), pl.program_id(1)
        f = f_ref[0].astype(F32)
        gtv = gt_ref[0]
        gf = gf_ref[...]
        h2 = h_ref[0] + gtv * f
        r = lax.rsqrt(jnp.mean(h2 * h2, axis=-1, keepdims=True) + EPS)
        n = h2 * r
        e = n * gf - t_ref[0]
        dy = e * (1.0 / D_MODEL)
        dn = dy * gf
        dh2 = r * (dn - n * jnp.mean(dn * n, axis=-1, keepdims=True))
        dh_ref[0] = dh2.astype(BF16)
        dff_ref[0] = (dh2 * gtv).astype(BF16)

        @pl.when(s == 0)
        def _():
            dgt_ref[...] = jnp.zeros_like(dgt_ref)

        @pl.when((s == 0) & (b == 0))
        def _():
            dgf_ref[...] = jnp.zeros_like(dgf_ref)
            loss_ref[...] = jnp.zeros_like(loss_ref)

        dgt_ref[0] += jnp.sum(dh2 * f, axis=0, keepdims=True)
        dgf_ref[...] += jnp.sum(dy * n, axis=0, keepdims=True)
        rows = jnp.sum(e * e, axis=1, keepdims=True)
        loss_ref[...] += jnp.sum(rows, axis=0, keepdims=True) * (0.5 / D_MODEL)

    return _pcall(body, name="final_loss",
                  out_shape=(_sds(h1.shape, BF16), _sds(h1.shape, BF16), _sds((B, 1, D_MODEL), F32),
                             _sds((1, D_MODEL), F32), _sds((1, 1), F32)),
                  grid=(B, S // ts), in_specs=[row, row, row, bvec, gvec], out_specs=(row, row, bvec, gvec, one),
                  dims=("arbitrary", "arbitrary"))(h1, ffn3, tgt3, gt, gfin)


def _att_scores(qh, kc, kp, h, dil, first, a_idx, j_idx):
    scale = HEAD_DIM ** -0.5
    nt = (((1,), (1,)), ((), ()))
    slope = (2.0 ** (-8.0 * (h + 1) / N_HEADS)) * dil
    dist_c = (a_idx - j_idx).astype(F32)
    s_c = lax.dot_general(qh, kc, nt, preferred_element_type=F32) * scale
    s_c = jnp.where(a_idx >= j_idx, s_c - slope * dist_c, NEG_INF)
    s_p = lax.dot_general(qh, kp, nt, preferred_element_type=F32) * scale
    s_p = jnp.where((j_idx >= a_idx) & jnp.logical_not(first), s_p - slope * (dist_c + float(ATT_BLOCK)), NEG_INF)
    return s_c, s_p


def _att_block_consts(seq_blocks):
    p = pl.program_id(0)
    j = pl.program_id(1)
    nb = lax.shift_right_logical(jnp.int32(seq_blocks), 2 * p)
    dil = lax.shift_left(jnp.int32(1), 2 * p).astype(F32)
    a_idx = lax.broadcasted_iota(jnp.int32, (ATT_BLOCK, ATT_BLOCK), 0)
    j_idx = lax.broadcasted_iota(jnp.int32, (ATT_BLOCK, ATT_BLOCK), 1)
    return j, nb, dil, a_idx, j_idx


def _attn_fwd(qb, kb, vb, seq_blocks):
    _, NB, _, _ = qb.shape
    cur = pl.BlockSpec((None, None, ATT_BLOCK, ATT_WIDTH), lambda p, j: (p, j, 0, 0))
    prev = pl.BlockSpec((None, None, ATT_BLOCK, ATT_WIDTH), lambda p, j: (p, jnp.maximum(j - 1, 0), 0, 0))
    lse_spec = pl.BlockSpec((None, None, ATT_BLOCK, N_HEADS), lambda p, j: (p, j, 0, 0))

    def body(q_ref, kc_ref, kp_ref, vc_ref, vp_ref, o_ref, lse_ref):
        j, nb, dil, a_idx, j_idx = _att_block_consts(seq_blocks)
        first = lax.rem(j, nb) == 0
        for h in range(N_HEADS):
            hs = slice(h * HEAD_DIM, (h + 1) * HEAD_DIM)
            s_c, s_p = _att_scores(q_ref[:, hs], kc_ref[:, hs], kp_ref[:, hs], h, dil, first, a_idx, j_idx)
            m = jnp.maximum(jnp.max(s_c, axis=1, keepdims=True), jnp.max(s_p, axis=1, keepdims=True))
            p_c = jnp.exp(s_c - m)
            p_p = jnp.exp(s_p - m)
            den = jnp.sum(p_c, axis=1, keepdims=True) + jnp.sum(p_p, axis=1, keepdims=True)
            o = (jnp.dot(p_c.astype(BF16), vc_ref[:, hs], preferred_element_type=F32)
                 + jnp.dot(p_p.astype(BF16), vp_ref[:, hs], preferred_element_type=F32))
            o_ref[:, hs] = o / den
            lse_ref[:, h:h + 1] = m + jnp.log(den)

    return _pcall(body, name="attn_fwd",
                  out_shape=(_sds(qb.shape, F32), _sds((N_PATTERNS, NB, ATT_BLOCK, N_HEADS), F32)),
                  grid=(N_PATTERNS, NB), in_specs=[cur, cur, prev, cur, prev], out_specs=(cur, lse_spec),
                  dims=("parallel", "parallel"))(qb, kb, kb, vb, vb)


def _attn_combine(o_p, lse_p):
    _, T, _ = o_p.shape
    tm = min(T, 1024)

    def body(o_ref, l_ref, out_ref, lse_ref):
        l0, l1, l2 = l_ref[0], l_ref[1], l_ref[2]
        m = jnp.maximum(jnp.maximum(l0, l1), l2)
        lse = m + jnp.log(jnp.exp(l0 - m) + jnp.exp(l1 - m) + jnp.exp(l2 - m))
        lse_ref[...] = lse
        w = [jnp.exp(l0 - lse), jnp.exp(l1 - lse), jnp.exp(l2 - lse)]
        for h in range(N_HEADS):
            hs = slice(h * HEAD_DIM, (h + 1) * HEAD_DIM)
            acc = w[0][:, h:h + 1] * o_ref[0, :, hs]
            acc = acc + w[1][:, h:h + 1] * o_ref[1, :, hs]
            acc = acc + w[2][:, h:h + 1] * o_ref[2, :, hs]
            out_ref[:, hs] = acc.astype(BF16)

    return _pcall(body, name="attn_combine", out_shape=(_sds((T, ATT_WIDTH), BF16), _sds((T, N_HEADS), F32)),
                  grid=(T // tm,),
                  in_specs=[pl.BlockSpec((N_PATTERNS, tm, ATT_WIDTH), lambda i: (0, i, 0)),
                            pl.BlockSpec((N_PATTERNS, tm, N_HEADS), lambda i: (0, i, 0))],
                  out_specs=(pl.BlockSpec((tm, ATT_WIDTH), lambda i: (i, 0)), pl.BlockSpec((tm, N_HEADS), lambda i: (i, 0))),
                  dims=("parallel",))(o_p, lse_p)


def _attn_bwd(qb, kb, vb, dob, ob, lseb, seq_blocks):
    _, NB, _, _ = qb.shape
    last = NB - 1
    cur = pl.BlockSpec((None, None, ATT_BLOCK, ATT_WIDTH), lambda p, j: (p, jnp.minimum(j, last), 0, 0))
    prev = pl.BlockSpec((None, None, ATT_BLOCK, ATT_WIDTH),
                        lambda p, j: (p, jnp.maximum(jnp.minimum(j, last) - 1, 0), 0, 0))
    lag = pl.BlockSpec((None, None, ATT_BLOCK, ATT_WIDTH), lambda p, j: (p, jnp.maximum(j - 1, 0), 0, 0))
    lse_spec = pl.BlockSpec((None, None, ATT_BLOCK, N_HEADS), lambda p, j: (p, jnp.minimum(j, last), 0, 0))
    scale = HEAD_DIM ** -0.5
    tn = (((0,), (0,)), ((), ()))
    nt = (((1,), (1,)), ((), ()))

    def body(q_ref, kc_ref, kp_ref, vc_ref, vp_ref, do_ref, o_ref, lse_ref, dq_ref, dk_ref, dv_ref, ck_ref, cv_ref):
        j, nb, dil, a_idx, j_idx = _att_block_consts(seq_blocks)

        @pl.when(j == 0)
        def _():
            ck_ref[...] = jnp.zeros_like(ck_ref)
            cv_ref[...] = jnp.zeros_like(cv_ref)

        @pl.when(j <= last)
        def _():
            first = lax.rem(j, nb) == 0
            for h in range(N_HEADS):
                hs = slice(h * HEAD_DIM, (h + 1) * HEAD_DIM)
                qh, kc, kp, vc, vp, doh = q_ref[:, hs], kc_ref[:, hs], kp_ref[:, hs], vc_ref[:, hs], vp_ref[:, hs], do_ref[:, hs]
                s_c, s_p = _att_scores(qh, kc, kp, h, dil, first, a_idx, j_idx)
                lse = lse_ref[:, h:h + 1]
                p_c = jnp.exp(s_c - lse)
                p_p = jnp.exp(s_p - lse)
                delta = jnp.sum(doh.astype(F32) * o_ref[:, hs].astype(F32), axis=1, keepdims=True)
                ds_c = (p_c * (lax.dot_general(doh, vc, nt, preferred_element_type=F32) - delta)).astype(BF16)
                ds_p = (p_p * (lax.dot_general(doh, vp, nt, preferred_element_type=F32) - delta)).astype(BF16)
                dq_ref[:, hs] = (jnp.dot(ds_c, kc, preferred_element_type=F32)
                                 + jnp.dot(ds_p, kp, preferred_element_type=F32)) * scale
                dk_ref[:, hs] = ck_ref[:, hs] + lax.dot_general(ds_p, qh, tn, preferred_element_type=F32) * scale
                dv_ref[:, hs] = cv_ref[:, hs] + lax.dot_general(p_p.astype(BF16), doh, tn, preferred_element_type=F32)
                ck_ref[:, hs] = lax.dot_general(ds_c, qh, tn, preferred_element_type=F32) * scale
                cv_ref[:, hs] = lax.dot_general(p_c.astype(BF16), doh, tn, preferred_element_type=F32)

        @pl.when(j == NB)
        def _():
            dk_ref[...] = ck_ref[...]
            dv_ref[...] = cv_ref[...]

    shp = _sds(qb.shape, F32)
    return _pcall(body, name="attn_bwd", out_shape=(shp, shp, shp), grid=(N_PATTERNS, NB + 1),
                  in_specs=[cur, cur, prev, cur, prev, cur, cur, lse_spec], out_specs=(cur, lag, lag),
                  scratch_shapes=[pltpu.VMEM((ATT_BLOCK, ATT_WIDTH), F32), pltpu.VMEM((ATT_BLOCK, ATT_WIDTH), F32)],
                  dims=("arbitrary", "arbitrary"))(qb, kb, kb, vb, vb, dob, ob, lseb)


def _sum3_cast(a, b, c):
    T, N = a.shape
    tm = min(T, 1024)
    spec = pl.BlockSpec((tm, N), lambda i: (i, 0))

    def body(a_ref, b_ref, c_ref, o_ref):
        o_ref[...] = (a_ref[...] + b_ref[...] + c_ref[...]).astype(BF16)

    return _pcall(body, name="sum3_cast", out_shape=_sds((T, N), BF16), grid=(T // tm,), in_specs=[spec] * 3,
                  out_specs=spec, dims=("parallel",))(a, b, c)


def _to_blocks(t, B, S):
    C = t.shape[-1]
    outs = []
    for p in range(N_PATTERNS):
        d = 4 ** p
        u = t.reshape(B, S // d, d, C).transpose(0, 2, 1, 3)
        outs.append(u.reshape(B * S // ATT_BLOCK, ATT_BLOCK, C))
    return jnp.stack(outs, axis=0)


def _from_blocks(tb, B, S):
    C = tb.shape[-1]
    outs = []
    for p in range(N_PATTERNS):
        d = 4 ** p
        u = tb[p].reshape(B, d, S // d, C).transpose(0, 2, 1, 3)
        outs.append(u.reshape(B * S, C))
    return jnp.stack(outs, axis=0)


ATT_GROUP = 4
ATT_GW = ATT_GROUP * HEAD_DIM
ATT_GROUPS = N_HEADS // ATT_GROUP
ATT_PAIRS = ATT_GW // ATT_BLOCK
ATT_UNROLL = 5
ATT_RESIDUE_UNROLL = 4
NT_DIMS = (((1,), (1,)), ((), ()))
TN_DIMS = (((0,), (0,)), ((), ()))


def _att_rows(start, d):
    if d == 1:
        return pl.ds(start if isinstance(start, int) else pl.multiple_of(start, ATT_BLOCK), ATT_BLOCK)
    return pl.ds(start, ATT_BLOCK, stride=d)


def _att_fill_bias(bias_ref, g, d):
    a = lax.broadcasted_iota(jnp.int32, (ATT_BLOCK, ATT_BLOCK), 0)
    j = lax.broadcasted_iota(jnp.int32, (ATT_BLOCK, ATT_BLOCK), 1)
    dist = (a - j).astype(F32)
    for hh in range(ATT_GROUP):
        t, e = divmod(hh, 2)
        rs = slice(e * ATT_BLOCK, (e + 1) * ATT_BLOCK)
        lo = 2.0 ** (-8.0 * (hh + 1) / N_HEADS) * d
        hi = 2.0 ** (-8.0 * (ATT_GROUP + hh + 1) / N_HEADS) * d
        slope = jnp.where(g == 0, lo, hi).astype(F32)
        bias_ref[t, rs, 0:ATT_BLOCK] = jnp.where(a >= j, -slope * dist, NEG_INF)
        bias_ref[t, rs, ATT_BLOCK:] = jnp.where(j >= a, -slope * (dist + float(ATT_BLOCK)), NEG_INF)


def _stack_heads(v2, low):
    return jnp.concatenate([jnp.where(low, v2, 0.0), jnp.where(low, 0.0, v2)], axis=0).astype(BF16)


def _unstack_heads(r2, low):
    return jnp.where(low, r2[0:ATT_BLOCK], r2[ATT_BLOCK:])


class _Riders:
    def __init__(self, arrs, mode, group="xy"):
        self.arrs, self.mode, self.n, self.group = list(arrs), mode, len(arrs), group
        slot_shapes = [a.shape if mode == "gather" else a.shape[1:] for a in self.arrs]
        self.out_shape = [_sds((_GROUP_SLOTS[group],) + s, a.dtype) for s, a in zip(slot_shapes, self.arrs)]
        k = len(_GROUP_MASKS[group])
        self.scratch = [pltpu.SemaphoreType.DMA((k * self.n,)), pltpu.SemaphoreType.DMA((k * self.n,)),
                        pltpu.SemaphoreType.DMA((2 * self.n,))] + [pltpu.VMEM(s, a.dtype) for s, a in zip(slot_shapes, self.arrs)]
        self.specs = [pl.BlockSpec(memory_space=pl.ANY)] * self.n

    def _remote(self, x_refs, o_refs, send_sems, recv_sems):
        x, y, c = lax.axis_index("x"), lax.axis_index("y"), lax.axis_index("c")
        me = _group_slot(self.group, x, y, c)
        masks = _GROUP_MASKS[self.group]
        cps = []
        for i in range(self.n):
            for k, (dx, dy, dc) in enumerate(masks):
                px, py, pc = _flip(x, dx), _flip(y, dy), _flip(c, dc)
                src = x_refs[i] if self.mode == "gather" else x_refs[i].at[_group_slot(self.group, px, py, pc)]
                cps.append(pltpu.make_async_remote_copy(
                    src_ref=src, dst_ref=o_refs[i].at[me], send_sem=send_sems.at[len(masks) * i + k],
                    recv_sem=recv_sems.at[len(masks) * i + k], device_id=(px, py, pc),
                    device_id_type=pl.DeviceIdType.MESH))
        return cps, me

    def start(self, x_refs, o_refs, scratch):
        send_sems, recv_sems, local_sems, bufs = scratch[0], scratch[1], scratch[2], scratch[3:]
        cps, me = self._remote(x_refs, o_refs, send_sems, recv_sems)
        for cp in cps:
            cp.start()
        for i in range(self.n):
            src = x_refs[i] if self.mode == "gather" else x_refs[i].at[me]
            load = pltpu.make_async_copy(src, bufs[i], local_sems.at[2 * i])
            load.start()
            load.wait()
            pltpu.make_async_copy(bufs[i], o_refs[i].at[me], local_sems.at[2 * i + 1]).start()

    def wait(self, x_refs, o_refs, scratch):
        send_sems, recv_sems, local_sems, bufs = scratch[0], scratch[1], scratch[2], scratch[3:]
        cps, me = self._remote(x_refs, o_refs, send_sems, recv_sems)
        for cp in cps:
            cp.wait()
        for i in range(self.n):
            pltpu.make_async_copy(bufs[i], o_refs[i].at[me], local_sems.at[2 * i + 1]).wait()


def _with_riders(compute, riders, n_in, n_out, n_scratch, last_step):
    if riders is None:
        return compute
    n = riders.n

    def body(*refs):
        ins, x_refs = refs[:n_in], refs[n_in:n_in + n]
        outs, o_refs = refs[n_in + n:n_in + n + n_out], refs[n_in + n + n_out:n_in + 2 * n + n_out]
        scratch = refs[n_in + 2 * n + n_out:]
        own, ride = scratch[:n_scratch], scratch[n_scratch:]
        ids = [pl.program_id(i) for i in range(len(last_step))]
        first = functools.reduce(jnp.logical_and, [i == 0 for i in ids])
        last = functools.reduce(jnp.logical_and, [i == l for i, l in zip(ids, last_step)])

        @pl.when(first)
        def _():
            riders.start(x_refs, o_refs, ride)

        compute(*ins, *outs, *own)

        @pl.when(last)
        def _():
            riders.wait(x_refs, o_refs, ride)

    return body


def _attention_fwd(proj3, seq_blocks, riders=None):
    B, S, _ = proj3.shape
    scale = HEAD_DIM ** -0.5
    nq = ATT_WIDTH // ATT_GW

    def col(k):
        return pl.BlockSpec((1, S, ATT_GW), lambda b, g, k=k: (b, 0, k * nq + g))

    o_spec = pl.BlockSpec((1, S, ATT_GW), lambda b, g: (b, 0, g))
    l_spec = pl.BlockSpec((1, 1, S, ATT_BLOCK), lambda b, g: (b, g, 0, 0))

    def compute(q_ref, k_ref, v_ref, o_ref, lse_ref, qf, kf, vf, os, ls, bias):
        g = pl.program_id(1)
        for t in range(ATT_PAIRS):
            ts = slice(t * ATT_BLOCK, (t + 1) * ATT_BLOCK)
            qf[t] = q_ref[0, :, ts].astype(F32) * scale
            kf[t] = k_ref[0, :, ts].astype(F32)
            vf[t] = v_ref[0, :, ts].astype(F32)
        lane = lax.broadcasted_iota(jnp.int32, (ATT_BLOCK, ATT_BLOCK), 1)
        low = lane < HEAD_DIM

        def block(p, d, r, n, has_prev):
            start = n * (ATT_BLOCK * d) + r
            rows = _att_rows(start, d)
            prows = _att_rows(start - ATT_BLOCK * d, d) if has_prev else None
            lse_t = jnp.zeros((ATT_BLOCK, ATT_BLOCK), F32)
            for t in range(ATT_PAIRS):
                q2 = _stack_heads(qf[t, rows, :], low)
                k2 = kf[t, rows, :].astype(BF16)
                v2 = vf[t, rows, :].astype(BF16)
                if has_prev:
                    k2 = jnp.concatenate([k2, kf[t, prows, :].astype(BF16)], axis=0)
                    v2 = jnp.concatenate([v2, vf[t, prows, :].astype(BF16)], axis=0)
                    b2 = bias[t]
                else:
                    b2 = bias[t, :, 0:ATT_BLOCK]
                s = lax.dot_general(q2, k2, NT_DIMS, preferred_element_type=F32) + b2
                m = jnp.max(s, axis=1, keepdims=True)
                pr = jnp.exp(s - m)
                den = jnp.sum(pr, axis=1, keepdims=True)
                o = jnp.dot(pr.astype(BF16), v2, preferred_element_type=F32) * (1.0 / den)
                os[p, t, rows, :] = _unstack_heads(o, low)
                lse2 = m + jnp.log(den)
                lse_t = jnp.where(lane == 2 * t, lse2[0:ATT_BLOCK], lse_t)
                lse_t = jnp.where(lane == 2 * t + 1, lse2[ATT_BLOCK:], lse_t)
            ls[p, rows, :] = lse_t

        for p in range(N_PATTERNS):
            d = 4 ** p
            _att_fill_bias(bias, g, d)
            _att_one_pattern(block, p, d, seq_blocks // d)

        def combine(i, carry):
            rows = pl.ds(pl.multiple_of(i * ATT_BLOCK, ATT_BLOCK), ATT_BLOCK)
            l0, l1, l2 = ls[0, rows, :], ls[1, rows, :], ls[2, rows, :]
            m = jnp.maximum(jnp.maximum(l0, l1), l2)
            lse = m + jnp.log(jnp.exp(l0 - m) + jnp.exp(l1 - m) + jnp.exp(l2 - m))
            lse_ref[0, 0, rows, :] = lse
            w = [jnp.exp(l0 - lse), jnp.exp(l1 - lse), jnp.exp(l2 - lse)]
            for t in range(ATT_PAIRS):
                acc = jnp.zeros((ATT_BLOCK, ATT_BLOCK), F32)
                for p in range(N_PATTERNS):
                    wt = jnp.where(low, w[p][:, 2 * t:2 * t + 1], w[p][:, 2 * t + 1:2 * t + 2])
                    acc = acc + wt * os[p, t, rows, :]
                o_ref[0, rows, t * ATT_BLOCK:(t + 1) * ATT_BLOCK] = acc.astype(BF16)
            return carry

        lax.fori_loop(0, S // ATT_BLOCK, combine, 0, unroll=2)

    scratch = ([pltpu.VMEM((ATT_PAIRS, S, ATT_BLOCK), F32)] * 3
               + [pltpu.VMEM((N_PATTERNS, ATT_PAIRS, S, ATT_BLOCK), F32), pltpu.VMEM((N_PATTERNS, S, ATT_BLOCK), F32),
                  pltpu.VMEM((ATT_PAIRS, 2 * ATT_BLOCK, 2 * ATT_BLOCK), F32)])
    rs = riders
    res = _pcall(_with_riders(compute, rs, 3, 2, len(scratch), (B - 1, ATT_GROUPS - 1)), name="attention_fwd",
                 out_shape=(_sds((B, S, ATT_WIDTH), BF16), _sds((B, ATT_GROUPS, S, ATT_BLOCK), F32))
                 + (tuple(rs.out_shape) if rs else ()),
                 grid=(B, ATT_GROUPS), in_specs=[col(0), col(1), col(2)] + (rs.specs if rs else []),
                 out_specs=(o_spec, l_spec) + (tuple(rs.specs) if rs else ()),
                 scratch_shapes=scratch + (rs.scratch if rs else []),
                 dims=("arbitrary", "arbitrary"))(proj3, proj3, proj3, *(rs.arrs if rs else []))
    return res[0], res[1], list(res[2:])


def _att_one_pattern(block, p, d, nb):
    def per_residue(r, carry):
        block(p, d, r, 0, False)
        if nb > 1:
            def per_block(n, c2):
                block(p, d, r, n, True)
                return c2
            lax.fori_loop(1, nb, per_block, 0, unroll=ATT_UNROLL if (nb - 1) % ATT_UNROLL == 0 else nb - 1)
        return carry

    if d == 1:
        per_residue(0, 0)
    else:
        lax.fori_loop(0, d, per_residue, 0, unroll=ATT_RESIDUE_UNROLL if nb == 1 else 1)


def _attention_bwd(proj3, do3, o3, lse4, seq_blocks, riders=None):
    B, S, _ = proj3.shape
    scale = HEAD_DIM ** -0.5
    nq = ATT_WIDTH // ATT_GW

    def col(k):
        return pl.BlockSpec((1, S, ATT_GW), lambda b, g, k=k: (b, 0, k * nq + g))

    o_spec = pl.BlockSpec((1, S, ATT_GW), lambda b, g: (b, 0, g))
    l_spec = pl.BlockSpec((1, 1, S, ATT_BLOCK), lambda b, g: (b, g, 0, 0))

    def compute(q_ref, k_ref, v_ref, do_ref, o_ref, lse_ref, dq_ref, dk_ref, dv_ref,
                qf, kf, vf, dof, dl, aq, ak, av, bias):
        g = pl.program_id(1)
        for t in range(ATT_PAIRS):
            ts = slice(t * ATT_BLOCK, (t + 1) * ATT_BLOCK)
            qf[t] = q_ref[0, :, ts].astype(F32) * scale
            kf[t] = k_ref[0, :, ts].astype(F32)
            vf[t] = v_ref[0, :, ts].astype(F32)
            dof[t] = do_ref[0, :, ts].astype(F32)
        aq[...] = jnp.zeros_like(aq)
        ak[...] = jnp.zeros_like(ak)
        av[...] = jnp.zeros_like(av)
        lane = lax.broadcasted_iota(jnp.int32, (ATT_BLOCK, ATT_BLOCK), 1)
        low = lane < HEAD_DIM

        def fill_delta(i, carry):
            rows = pl.ds(pl.multiple_of(i * ATT_BLOCK, ATT_BLOCK), ATT_BLOCK)
            acc = jnp.zeros((ATT_BLOCK, ATT_BLOCK), F32)
            for t in range(ATT_PAIRS):
                prod = dof[t, rows, :] * o_ref[0, rows, t * ATT_BLOCK:(t + 1) * ATT_BLOCK].astype(F32)
                lo = jnp.sum(jnp.where(low, prod, 0.0), axis=1, keepdims=True)
                hi = jnp.sum(prod, axis=1, keepdims=True) - lo
                acc = jnp.where(lane == 2 * t, lo, acc)
                acc = jnp.where(lane == 2 * t + 1, hi, acc)
            dl[rows, :] = acc
            return carry

        lax.fori_loop(0, S // ATT_BLOCK, fill_delta, 0, unroll=2)

        def block(p, d, r, n, has_prev):
            start = n * (ATT_BLOCK * d) + r
            rows = _att_rows(start, d)
            prows = _att_rows(start - ATT_BLOCK * d, d) if has_prev else None
            lse_t = lse_ref[0, 0, rows, :]
            dl_t = dl[rows, :]
            for t in range(ATT_PAIRS):
                q2 = _stack_heads(qf[t, rows, :], low)
                do2 = _stack_heads(dof[t, rows, :], low)
                k2 = kf[t, rows, :].astype(BF16)
                v2 = vf[t, rows, :].astype(BF16)
                if has_prev:
                    k2 = jnp.concatenate([k2, kf[t, prows, :].astype(BF16)], axis=0)
                    v2 = jnp.concatenate([v2, vf[t, prows, :].astype(BF16)], axis=0)
                    b2 = bias[t]
                else:
                    b2 = bias[t, :, 0:ATT_BLOCK]
                lse2 = jnp.concatenate([lse_t[:, 2 * t:2 * t + 1], lse_t[:, 2 * t + 1:2 * t + 2]], axis=0)
                dl2 = jnp.concatenate([dl_t[:, 2 * t:2 * t + 1], dl_t[:, 2 * t + 1:2 * t + 2]], axis=0)
                s = lax.dot_general(q2, k2, NT_DIMS, preferred_element_type=F32) + b2
                pr = jnp.exp(s - lse2)
                ds = (pr * (lax.dot_general(do2, v2, NT_DIMS, preferred_element_type=F32) - dl2)).astype(BF16)
                dq = _unstack_heads(jnp.dot(ds, k2, preferred_element_type=F32), low)
                dk = lax.dot_general(ds, q2, TN_DIMS, preferred_element_type=F32)
                dv = lax.dot_general(pr.astype(BF16), do2, TN_DIMS, preferred_element_type=F32)
                aq[t, rows, :] = aq[t, rows, :] + dq * scale
                ak[t, rows, :] = ak[t, rows, :] + dk[0:ATT_BLOCK]
                av[t, rows, :] = av[t, rows, :] + dv[0:ATT_BLOCK]
                if has_prev:
                    ak[t, prows, :] = ak[t, prows, :] + dk[ATT_BLOCK:]
                    av[t, prows, :] = av[t, prows, :] + dv[ATT_BLOCK:]

        for p in range(N_PATTERNS):
            d = 4 ** p
            _att_fill_bias(bias, g, d)
            _att_one_pattern(block, p, d, seq_blocks // d)

        for t in range(ATT_PAIRS):
            ts = slice(t * ATT_BLOCK, (t + 1) * ATT_BLOCK)
            dq_ref[0, :, ts] = aq[t].astype(BF16)
            dk_ref[0, :, ts] = ak[t].astype(BF16)
            dv_ref[0, :, ts] = av[t].astype(BF16)

    shp = _sds((B, S, ATT_WIDTH), BF16)
    pair_buf = pltpu.VMEM((ATT_PAIRS, S, ATT_BLOCK), F32)
    scratch = ([pair_buf] * 4 + [pltpu.VMEM((S, ATT_BLOCK), F32)] + [pair_buf] * 3
               + [pltpu.VMEM((ATT_PAIRS, 2 * ATT_BLOCK, 2 * ATT_BLOCK), F32)])
    rs = riders
    res = _pcall(_with_riders(compute, rs, 6, 3, len(scratch), (B - 1, ATT_GROUPS - 1)), name="attention_bwd",
                 out_shape=(shp, shp, shp) + (tuple(rs.out_shape) if rs else ()), grid=(B, ATT_GROUPS),
                 in_specs=[col(0), col(1), col(2), o_spec, o_spec, l_spec] + (rs.specs if rs else []),
                 out_specs=(o_spec, o_spec, o_spec) + (tuple(rs.specs) if rs else ()),
                 scratch_shapes=scratch + (rs.scratch if rs else []),
                 dims=("arbitrary", "arbitrary"))(proj3, proj3, proj3, do3, o3, lse4, *(rs.arrs if rs else []))
    return res[0], res[1], res[2], list(res[3:])


def _expand_groups(m):
    rows = SSM_WIDTH
    t = jnp.concatenate([m] * SSM_GROUPS, axis=0)
    r = lax.broadcasted_iota(jnp.int32, (rows, SSM_LANES), 0)
    l = lax.broadcasted_iota(jnp.int32, (rows, SSM_LANES), 1)
    keep = lax.shift_right_logical(r, 4) == lax.shift_right_logical(l, 6)
    return jnp.where(keep, t, 0.0)


def _collapse_groups(m):
    rows = SSM_WIDTH
    r = lax.broadcasted_iota(jnp.int32, (rows, SSM_LANES), 0)
    l = lax.broadcasted_iota(jnp.int32, (rows, SSM_LANES), 1)
    keep = lax.shift_right_logical(r, 4) == lax.shift_right_logical(l, 6)
    t = jnp.where(keep, m, 0.0)
    acc = t[0:SSM_GROUP_CH]
    for g in range(1, SSM_GROUPS):
        acc = acc + t[g * SSM_GROUP_CH:(g + 1) * SSM_GROUP_CH]
    return acc


def _zoh(lr, li, ldt):
    dt = jnp.exp(ldt)
    mag = jnp.exp(lr * dt)
    ang = li * dt
    cs, sn = jnp.cos(ang), jnp.sin(ang)
    ab_re, ab_im = mag * cs, mag * sn
    nr, ni = ab_re - 1.0, ab_im
    den = lr * lr + li * li
    n_re = nr * lr + ni * li
    n_im = ni * lr - nr * li
    return dict(dt=dt, mag=mag, cs=cs, sn=sn, ab_re=ab_re, ab_im=ab_im, nr=nr, ni=ni, den=den, n_re=n_re, n_im=n_im,
                f_re=n_re / den, f_im=n_im / den)


def _ssm_params(lr, li, ldt, br, bi, cr, ci):
    def body(lr_ref, li_ref, ldt_ref, br_ref, bi_ref, cr_ref, ci_ref, ab_ref, w_ref, c_ref):
        z = _zoh(lr_ref[...], li_ref[...], ldt_ref[...])
        ab_ref[0:1, :] = z["ab_re"]
        ab_ref[1:2, :] = z["ab_im"]
        br, bi = br_ref[...], bi_ref[...]
        w_ref[:, 0:SSM_LANES] = _expand_groups(z["f_re"] * br - z["f_im"] * bi).astype(BF16)
        w_ref[:, SSM_LANES:] = _expand_groups(z["f_re"] * bi + z["f_im"] * br).astype(BF16)
        c_ref[:, 0:SSM_LANES] = _expand_groups(cr_ref[...]).astype(BF16)
        c_ref[:, SSM_LANES:] = _expand_groups(-ci_ref[...]).astype(BF16)

    return _pcall(body, name="ssm_params",
                  out_shape=(_sds((2, SSM_LANES), F32), _sds((SSM_WIDTH, 2 * SSM_LANES), BF16),
                             _sds((SSM_WIDTH, 2 * SSM_LANES), BF16)))(lr, li, ldt, br, bi, cr, ci)


def _ssm_params_bwd(lr, li, ldt, br, bi, dab, dw, dc):
    def body(lr_ref, li_ref, ldt_ref, br_ref, bi_ref, dab_ref, dw_ref, dc_ref,
             dlr_ref, dli_ref, dldt_ref, dbr_ref, dbi_ref, dcr_ref, dci_ref):
        lr, li = lr_ref[...], li_ref[...]
        z = _zoh(lr, li, ldt_ref[...])
        br, bi = br_ref[...], bi_ref[...]
        dbb_re = _collapse_groups(dw_ref[:, 0:SSM_LANES])
        dbb_im = _collapse_groups(dw_ref[:, SSM_LANES:])
        dcr_ref[...] = _collapse_groups(dc_ref[:, 0:SSM_LANES])
        dci_ref[...] = -_collapse_groups(dc_ref[:, SSM_LANES:])
        f_re, f_im = z["f_re"], z["f_im"]
        dbr_ref[...] = f_re * dbb_re + f_im * dbb_im
        dbi_ref[...] = f_re * dbb_im - f_im * dbb_re
        df_re = jnp.sum(dbb_re * br + dbb_im * bi, axis=0, keepdims=True)
        df_im = jnp.sum(dbb_im * br - dbb_re * bi, axis=0, keepdims=True)
        den = z["den"]
        dn_re, dn_im = df_re / den, df_im / den
        dden = -(df_re * z["n_re"] + df_im * z["n_im"]) / (den * den)
        dnr = dn_re * lr - dn_im * li
        dni = dn_re * li + dn_im * lr
        dlr = dn_re * z["nr"] + dn_im * z["ni"] + 2.0 * dden * lr
        dli = dn_re * z["ni"] - dn_im * z["nr"] + 2.0 * dden * li
        dab_re = dab_ref[0:1, :] + dnr
        dab_im = dab_ref[1:2, :] + dni
        mag, cs, sn, dt = z["mag"], z["cs"], z["sn"], z["dt"]
        dmag = dab_re * cs + dab_im * sn
        dang = mag * (dab_im * cs - dab_re * sn)
        dlr_ref[...] = dlr + dmag * mag * dt
        dli_ref[...] = dli + dang * dt
        ddt = dmag * mag * lr + dang * li
        per_lane = jnp.broadcast_to(ddt * dt, (8, SSM_LANES))
        lane = lax.broadcasted_iota(jnp.int32, (SSM_LANES, 128), 0)
        col = lax.broadcasted_iota(jnp.int32, (SSM_LANES, 128), 1)
        ind = jnp.where(lax.shift_right_logical(lane, 6) == col, 1.0, 0.0)
        dldt_ref[...] = jnp.dot(per_lane, ind, preferred_element_type=F32, precision=lax.Precision.HIGHEST)[0:1]

    vec = _sds((1, SSM_LANES), F32)
    mat = _sds((SSM_GROUP_CH, SSM_LANES), F32)
    return _pcall(body, name="ssm_params_bwd", out_shape=(vec, vec, _sds((1, 128), F32), mat, mat, mat, mat))(
        lr, li, ldt, br, bi, dab, dw, dc)


SCAN_CHUNK = 512


def _scan_consts(ar, ai, k_ref, reverse):
    row = lax.broadcasted_iota(jnp.int32, (8, SSM_LANES), 0)
    pw = [(ar, ai)]
    for _ in range(7):
        pr, pi = pw[-1]
        pw.append((pr * ar - pi * ai, pr * ai + pi * ar))
    for n, k in enumerate((1, 2, 4)):
        keep = (row < 8 - k) if reverse else (row >= k)
        k_ref[2 * n] = jnp.where(keep, jnp.broadcast_to(pw[k - 1][0], (8, SSM_LANES)), 0.0)
        k_ref[2 * n + 1] = jnp.where(keep, jnp.broadcast_to(pw[k - 1][1], (8, SSM_LANES)), 0.0)
    cr = jnp.zeros((8, SSM_LANES), F32)
    ci = jnp.zeros((8, SSM_LANES), F32)
    for r in range(8):
        e = (8 - r) if reverse else (r + 1)
        cr = jnp.where(row == r, jnp.broadcast_to(pw[e - 1][0], (8, SSM_LANES)), cr)
        ci = jnp.where(row == r, jnp.broadcast_to(pw[e - 1][1], (8, SSM_LANES)), ci)
    k_ref[6] = cr
    k_ref[7] = ci


def _scan_tile(xr, xi, k_ref, car, cai, reverse):
    for n, k in enumerate((1, 2, 4)):
        sh = (8 - k) if reverse else k
        sr = pltpu.roll(xr, sh, 0)
        si = pltpu.roll(xi, sh, 0)
        mr, mi = k_ref[2 * n], k_ref[2 * n + 1]
        xr, xi = xr + mr * sr - mi * si, xi + mr * si + mi * sr
    pr, pi = k_ref[6], k_ref[7]
    xr, xi = xr + pr * car - pi * cai, xi + pr * cai + pi * car
    return xr, xi


def _scan_fwd(bu3, abar):
    B, S, _ = bu3.shape
    ch = min(S, SCAN_CHUNK)
    blk = pl.BlockSpec((1, ch, 2 * SSM_LANES), lambda b, c: (b, c, 0))

    def body(ab_ref, bu_ref, x_ref, k_ref, carry_ref):
        _scan_consts(ab_ref[0:1, :], ab_ref[1:2, :], k_ref, False)

        @pl.when(pl.program_id(1) == 0)
        def _():
            carry_ref[...] = jnp.zeros_like(carry_ref)

        def step(i, carry):
            base = pl.multiple_of(i * 8, 8)
            xr = bu_ref[0, pl.ds(base, 8), 0:SSM_LANES]
            xi = bu_ref[0, pl.ds(base, 8), SSM_LANES:]
            xr, xi = _scan_tile(xr, xi, k_ref, carry[0], carry[1], False)
            x_ref[0, pl.ds(base, 8), 0:SSM_LANES] = xr
            x_ref[0, pl.ds(base, 8), SSM_LANES:] = xi
            return (jnp.broadcast_to(xr[7:8], (8, SSM_LANES)), jnp.broadcast_to(xi[7:8], (8, SSM_LANES)))

        cr, ci = lax.fori_loop(0, ch // 8, step, (carry_ref[0], carry_ref[1]))
        carry_ref[0] = cr
        carry_ref[1] = ci

    return _pcall(body, name="scan_fwd", out_shape=_sds(bu3.shape, F32), grid=(B, S // ch),
                  in_specs=[pl.BlockSpec((2, SSM_LANES), lambda b, c: (0, 0)), blk], out_specs=blk,
                  scratch_shapes=[pltpu.VMEM((8, 8, SSM_LANES), F32), pltpu.VMEM((2, 8, SSM_LANES), F32)],
                  dims=("arbitrary", "arbitrary"))(abar, bu3)


def _scan_bwd(dx3, xs3, abar):
    B, S, _ = dx3.shape
    ch = min(S, SCAN_CHUNK)
    nc = S // ch
    blk = pl.BlockSpec((1, ch, 2 * SSM_LANES), lambda b, c: (b, nc - 1 - c, 0))

    def body(ab_ref, dx_ref, xs_ref, g_ref, da_ref, k_ref, carry_ref, acc_ref):
        b, c = pl.program_id(0), pl.program_id(1)
        _scan_consts(ab_ref[0:1, :], -ab_ref[1:2, :], k_ref, True)
        row = lax.broadcasted_iota(jnp.int32, (8, SSM_LANES), 0)

        @pl.when(c == 0)
        def _():
            carry_ref[...] = jnp.zeros_like(carry_ref)

        @pl.when((c == 0) & (b == 0))
        def _():
            acc_ref[...] = jnp.zeros_like(acc_ref)

        def step(i, carry):
            car, cai, ar_acc, ai_acc = carry
            base = pl.multiple_of((ch // 8 - 1 - i) * 8, 8)
            gr = dx_ref[0, pl.ds(base, 8), 0:SSM_LANES]
            gi = dx_ref[0, pl.ds(base, 8), SSM_LANES:]
            gr, gi = _scan_tile(gr, gi, k_ref, car, cai, True)
            g_ref[0, pl.ds(base, 8), 0:SSM_LANES] = gr
            g_ref[0, pl.ds(base, 8), SSM_LANES:] = gi
            nr = jnp.where(row == 7, car, pltpu.roll(gr, 7, 0))
            ni = jnp.where(row == 7, cai, pltpu.roll(gi, 7, 0))
            xr = xs_ref[0, pl.ds(base, 8), 0:SSM_LANES]
            xi = xs_ref[0, pl.ds(base, 8), SSM_LANES:]
            ar_acc = ar_acc + nr * xr + ni * xi
            ai_acc = ai_acc + ni * xr - nr * xi
            return (jnp.broadcast_to(gr[0:1], (8, SSM_LANES)), jnp.broadcast_to(gi[0:1], (8, SSM_LANES)), ar_acc, ai_acc)

        cr, ci, ar_acc, ai_acc = lax.fori_loop(0, ch // 8, step, (carry_ref[0], carry_ref[1], acc_ref[0], acc_ref[1]))
        carry_ref[0] = cr
        carry_ref[1] = ci
        acc_ref[0] = ar_acc
        acc_ref[1] = ai_acc
        da_ref[0:1, :] = jnp.sum(ar_acc, axis=0, keepdims=True)
        da_ref[1:2, :] = jnp.sum(ai_acc, axis=0, keepdims=True)

    return _pcall(body, name="scan_bwd", out_shape=(_sds(dx3.shape, F32), _sds((2, SSM_LANES), F32)), grid=(B, nc),
                  in_specs=[pl.BlockSpec((2, SSM_LANES), lambda b, c: (0, 0)), blk, blk],
                  out_specs=(blk, pl.BlockSpec((2, SSM_LANES), lambda b, c: (0, 0))),
                  scratch_shapes=[pltpu.VMEM((8, 8, SSM_LANES), F32), pltpu.VMEM((2, 8, SSM_LANES), F32),
                                  pltpu.VMEM((2, 8, SSM_LANES), F32)],
                  dims=("arbitrary", "arbitrary"))(abar, dx3, xs3)


US_BLOCK = (3 * ATT_WIDTH) // SSM_WIDTH


def _ssm_scan_fwd(proj3, abar, w_bu, w_c):
    B, S, _ = proj3.shape
    ch = min(S, SCAN_CHUNK)
    u_spec = pl.BlockSpec((1, ch, SSM_WIDTH), lambda b, c: (b, c, US_BLOCK))
    x_spec = pl.BlockSpec((1, ch, 2 * SSM_LANES), lambda b, c: (b, c, 0))
    y_spec = pl.BlockSpec((1, ch, SSM_WIDTH), lambda b, c: (b, c, 0))
    w_spec = pl.BlockSpec((SSM_WIDTH, 2 * SSM_LANES), lambda b, c: (0, 0))

    def body(ab_ref, u_ref, wb_ref, wc_ref, x_ref, y_ref, k_ref, carry_ref):
        _scan_consts(ab_ref[0:1, :], ab_ref[1:2, :], k_ref, False)

        @pl.when(pl.program_id(1) == 0)
        def _():
            carry_ref[...] = jnp.zeros_like(carry_ref)

        x_ref[0] = jnp.dot(u_ref[0], wb_ref[...], preferred_element_type=F32)

        def step(i, carry):
            base = pl.multiple_of(i * 8, 8)
            xr = x_ref[0, pl.ds(base, 8), 0:SSM_LANES]
            xi = x_ref[0, pl.ds(base, 8), SSM_LANES:]
            xr, xi = _scan_tile(xr, xi, k_ref, carry[0], carry[1], False)
            x_ref[0, pl.ds(base, 8), 0:SSM_LANES] = xr
            x_ref[0, pl.ds(base, 8), SSM_LANES:] = xi
            return (jnp.broadcast_to(xr[7:8], (8, SSM_LANES)), jnp.broadcast_to(xi[7:8], (8, SSM_LANES)))

        cr, ci = lax.fori_loop(0, ch // 8, step, (carry_ref[0], carry_ref[1]))
        carry_ref[0] = cr
        carry_ref[1] = ci
        y_ref[0] = lax.dot_general(x_ref[0].astype(BF16), wc_ref[...], NT_DIMS, preferred_element_type=F32)

    return _pcall(body, name="ssm_scan_fwd",
                  out_shape=(_sds((B, S, 2 * SSM_LANES), F32), _sds((B, S, SSM_WIDTH), F32)), grid=(B, S // ch),
                  in_specs=[pl.BlockSpec((2, SSM_LANES), lambda b, c: (0, 0)), u_spec, w_spec, w_spec],
                  out_specs=(x_spec, y_spec),
                  scratch_shapes=[pltpu.VMEM((8, 8, SSM_LANES), F32), pltpu.VMEM((2, 8, SSM_LANES), F32)],
                  dims=("arbitrary", "arbitrary"))(abar, proj3, w_bu, w_c)


def _ssm_scan_bwd(proj3, dy3, xs3, abar, w_bu, w_c, dsk):
    B, S, _ = proj3.shape
    ch = min(S, SCAN_CHUNK)
    nc = S // ch
    u_spec = pl.BlockSpec((1, ch, SSM_WIDTH), lambda b, c: (b, nc - 1 - c, US_BLOCK))
    x_spec = pl.BlockSpec((1, ch, 2 * SSM_LANES), lambda b, c: (b, nc - 1 - c, 0))
    y_spec = pl.BlockSpec((1, ch, SSM_WIDTH), lambda b, c: (b, nc - 1 - c, 0))
    w_spec = pl.BlockSpec((SSM_WIDTH, 2 * SSM_LANES), lambda b, c: (0, 0))
    ab_spec = pl.BlockSpec((2, SSM_LANES), lambda b, c: (0, 0))
    d_spec = pl.BlockSpec((1, SSM_WIDTH), lambda b, c: (0, 0))

    def body(ab_ref, u_ref, dy_ref, xs_ref, wb_ref, wc_ref, d_ref, du_ref, da_ref, dwb_ref, dwc_ref,
             g_ref, k_ref, carry_ref, acc_ref):
        b, c = pl.program_id(0), pl.program_id(1)
        _scan_consts(ab_ref[0:1, :], -ab_ref[1:2, :], k_ref, True)
        row = lax.broadcasted_iota(jnp.int32, (8, SSM_LANES), 0)

        @pl.when(c == 0)
        def _():
            carry_ref[...] = jnp.zeros_like(carry_ref)

        @pl.when((c == 0) & (b == 0))
        def _():
            acc_ref[...] = jnp.zeros_like(acc_ref)
            dwb_ref[...] = jnp.zeros_like(dwb_ref)
            dwc_ref[...] = jnp.zeros_like(dwc_ref)

        dy = dy_ref[0]
        dyb = dy.astype(BF16)
        g_ref[...] = jnp.dot(dyb, wc_ref[...], preferred_element_type=F32)

        def step(i, carry):
            car, cai, ar_acc, ai_acc = carry
            base = pl.multiple_of((ch // 8 - 1 - i) * 8, 8)
            gr = g_ref[pl.ds(base, 8), 0:SSM_LANES]
            gi = g_ref[pl.ds(base, 8), SSM_LANES:]
            gr, gi = _scan_tile(gr, gi, k_ref, car, cai, True)
            g_ref[pl.ds(base, 8), 0:SSM_LANES] = gr
            g_ref[pl.ds(base, 8), SSM_LANES:] = gi
            nr = jnp.where(row == 7, car, pltpu.roll(gr, 7, 0))
            ni = jnp.where(row == 7, cai, pltpu.roll(gi, 7, 0))
            xr = xs_ref[0, pl.ds(base, 8), 0:SSM_LANES]
            xi = xs_ref[0, pl.ds(base, 8), SSM_LANES:]
            ar_acc = ar_acc + nr * xr + ni * xi
            ai_acc = ai_acc + ni * xr - nr * xi
            return (jnp.broadcast_to(gr[0:1], (8, SSM_LANES)), jnp.broadcast_to(gi[0:1], (8, SSM_LANES)), ar_acc, ai_acc)

        cr, ci, ar_acc, ai_acc = lax.fori_loop(0, ch // 8, step, (carry_ref[0], carry_ref[1], acc_ref[0], acc_ref[1]))
        carry_ref[0] = cr
        carry_ref[1] = ci
        acc_ref[0] = ar_acc
        acc_ref[1] = ai_acc
        da_ref[0:1, :] = jnp.sum(ar_acc, axis=0, keepdims=True)
        da_ref[1:2, :] = jnp.sum(ai_acc, axis=0, keepdims=True)

        gb = g_ref[...].astype(BF16)
        du = lax.dot_general(gb, wb_ref[...], NT_DIMS, preferred_element_type=F32) + d_ref[...] * dy
        du_ref[0] = du.astype(BF16)
        dwb_ref[...] += lax.dot_general(u_ref[0], gb, TN_DIMS, preferred_element_type=F32)
        dwc_ref[...] += lax.dot_general(dyb, xs_ref[0].astype(BF16), TN_DIMS, preferred_element_type=F32)

    mat = _sds((SSM_WIDTH, 2 * SSM_LANES), F32)
    return _pcall(body, name="ssm_scan_bwd",
                  out_shape=(_sds((B, S, SSM_WIDTH), BF16), _sds((2, SSM_LANES), F32), mat, mat), grid=(B, nc),
                  in_specs=[ab_spec, u_spec, y_spec, x_spec, w_spec, w_spec, d_spec],
                  out_specs=(y_spec, ab_spec, w_spec, w_spec),
                  scratch_shapes=[pltpu.VMEM((ch, 2 * SSM_LANES), F32), pltpu.VMEM((8, 8, SSM_LANES), F32),
                                  pltpu.VMEM((2, 8, SSM_LANES), F32), pltpu.VMEM((2, 8, SSM_LANES), F32)],
                  dims=("arbitrary", "arbitrary"))(abar, proj3, dy3, xs3, w_bu, w_c, dsk)


GELU_K = math.sqrt(2.0 / math.pi)
GELU_C = 0.044715


def _gelu_parts(y):
    t = jnp.tanh(GELU_K * (y + GELU_C * y * y * y))
    return 0.5 * y * (1.0 + t), t


def _ssm_post(yc, us, dsk, wglu, bglu):
    T, N = yc.shape
    tm = min(T, 1024)
    row = pl.BlockSpec((tm, N), lambda i: (i, 0))
    vec = pl.BlockSpec((1, N), lambda i: (0, 0))
    mat = pl.BlockSpec((N, N), lambda i: (0, 0))

    def body(yc_ref, us_ref, d_ref, w_ref, b_ref, y_ref, s_ref):
        y = yc_ref[...] + d_ref[...] * us_ref[...]
        y_ref[...] = y
        z, _ = _gelu_parts(y)
        gl = jnp.dot(z.astype(BF16), w_ref[...], preferred_element_type=F32) + b_ref[...]
        s_ref[...] = (z * _sig(gl)).astype(BF16)

    return _pcall(body, name="ssm_post", out_shape=(_sds((T, N), F32), _sds((T, N), BF16)), grid=(T // tm,),
                  in_specs=[row, row, vec, mat, vec], out_specs=(row, row), dims=("parallel",))(yc, us, dsk, wglu, bglu)


def _ssm_post_bwd(y5, us, ds, dsk, wglu, bglu):
    T, N = y5.shape
    tm = min(T, 1024)
    row = pl.BlockSpec((tm, N), lambda i: (i, 0))
    vec = pl.BlockSpec((1, N), lambda i: (0, 0))
    mat = pl.BlockSpec((N, N), lambda i: (0, 0))

    def body(y_ref, us_ref, ds_ref, d_ref, w_ref, b_ref, dy_ref, dd_ref, db_ref, dw_ref):
        @pl.when(pl.program_id(0) == 0)
        def _():
            dd_ref[...] = jnp.zeros_like(dd_ref)
            db_ref[...] = jnp.zeros_like(db_ref)
            dw_ref[...] = jnp.zeros_like(dw_ref)

        y = y_ref[...]
        z, t = _gelu_parts(y)
        zb = z.astype(BF16)
        gl = jnp.dot(zb, w_ref[...], preferred_element_type=F32) + b_ref[...]
        sg = _sig(gl)
        ds = ds_ref[...]
        dgl = ds * z * sg * (1.0 - sg)
        dglb = dgl.astype(BF16)
        dz = ds * sg + lax.dot_general(dglb, w_ref[...], (((1,), (1,)), ((), ())), preferred_element_type=F32)
        dgelu = 0.5 * (1.0 + t) + 0.5 * y * (1.0 - t * t) * GELU_K * (1.0 + 3.0 * GELU_C * y * y)
        dy = dz * dgelu
        dy_ref[...] = dy
        dd_ref[...] += jnp.sum(dy * us_ref[...], axis=0, keepdims=True)
        db_ref[...] += jnp.sum(dgl, axis=0, keepdims=True)
        dw_ref[...] += lax.dot_general(zb, dglb, (((0,), (0,)), ((), ())), preferred_element_type=F32)

    return _pcall(body, name="ssm_post_bwd",
                  out_shape=(_sds((T, N), F32), _sds((1, N), F32), _sds((1, N), F32), _sds((N, N), F32)),
                  grid=(T // tm,), in_specs=[row, row, row, vec, mat, vec], out_specs=(row, vec, vec, mat),
                  dims=("arbitrary",))(y5, us, ds, dsk, wglu, bglu)


def _add_scaled_cast(a, b, s):
    T, N = a.shape
    tm = min(T, 1024)
    row = pl.BlockSpec((tm, N), lambda i: (i, 0))

    def body(a_ref, b_ref, s_ref, o_ref):
        o_ref[...] = (a_ref[...] + s_ref[...] * b_ref[...]).astype(BF16)

    return _pcall(body, name="add_scaled_cast", out_shape=_sds((T, N), BF16), grid=(T // tm,),
                  in_specs=[row, row, pl.BlockSpec((1, N), lambda i: (0, 0))], out_specs=row, dims=("parallel",))(a, b, s)


GATE_TILE = 256
GATE_ATT_BLOCK0 = (3 * ATT_WIDTH + SSM_WIDTH) // GATE_TILE
GATE_SSM_BLOCK0 = (3 * ATT_WIDTH + SSM_WIDTH + D_MODEL) // GATE_TILE


def _merge(proj, y_att, y_ssm, b_gate):
    T = proj.shape[0]
    tm = min(T, 1024)
    nj = D_MODEL // GATE_TILE
    ga = pl.BlockSpec((tm, GATE_TILE), lambda i, j: (i, GATE_ATT_BLOCK0 + j))
    gs = pl.BlockSpec((tm, GATE_TILE), lambda i, j: (i, GATE_SSM_BLOCK0 + j))
    yy = pl.BlockSpec((tm, GATE_TILE), lambda i, j: (i, j))
    ba = pl.BlockSpec((1, GATE_TILE), lambda i, j: (0, j))
    bs = pl.BlockSpec((1, GATE_TILE), lambda i, j: (0, nj + j))

    def body(ga_ref, gs_ref, ya_ref, ys_ref, ba_ref, bs_ref, o_ref):
        o_ref[...] = (_sig(ga_ref[...] + ba_ref[...]) * ya_ref[...]
                      + _sig(gs_ref[...] + bs_ref[...]) * ys_ref[...]).astype(BF16)

    return _pcall(body, name="merge", out_shape=_sds((T, D_MODEL), BF16), grid=(T // tm, nj),
                  in_specs=[ga, gs, yy, yy, ba, bs], out_specs=yy, dims=("parallel", "parallel"))(
        proj, proj, y_att, y_ssm, b_gate, b_gate)


def _merge_bwd(proj, y_att, y_ssm, b_gate, dmerged):
    T = proj.shape[0]
    tm = min(T, 1024)
    nj = D_MODEL // GATE_TILE
    ga = pl.BlockSpec((tm, GATE_TILE), lambda j, i: (i, GATE_ATT_BLOCK0 + j))
    gs = pl.BlockSpec((tm, GATE_TILE), lambda j, i: (i, GATE_SSM_BLOCK0 + j))
    yy = pl.BlockSpec((tm, GATE_TILE), lambda j, i: (i, j))
    ba = pl.BlockSpec((1, GATE_TILE), lambda j, i: (0, j))
    bs = pl.BlockSpec((1, GATE_TILE), lambda j, i: (0, nj + j))

    def body(ga_ref, gs_ref, ya_ref, ys_ref, ba_ref, bs_ref, dm_ref, dya_ref, dys_ref, dga_ref, dgs_ref, dba_ref, dbs_ref):
        @pl.when(pl.program_id(1) == 0)
        def _():
            dba_ref[...] = jnp.zeros_like(dba_ref)
            dbs_ref[...] = jnp.zeros_like(dbs_ref)

        dm = dm_ref[...].astype(F32)
        sa = _sig(ga_ref[...] + ba_ref[...])
        ss = _sig(gs_ref[...] + bs_ref[...])
        dya_ref[...] = (dm * sa).astype(BF16)
        dys_ref[...] = (dm * ss).astype(BF16)
        dga = dm * ya_ref[...] * sa * (1.0 - sa)
        dgs = dm * ys_ref[...] * ss * (1.0 - ss)
        dga_ref[...] = dga.astype(BF16)
        dgs_ref[...] = dgs.astype(BF16)
        dba_ref[...] += jnp.sum(dga, axis=0, keepdims=True)
        dbs_ref[...] += jnp.sum(dgs, axis=0, keepdims=True)

    big = _sds((T, D_MODEL), BF16)
    vec = _sds((1, D_MODEL), F32)
    return _pcall(body, name="merge_bwd", out_shape=(big, big, big, big, vec, vec), grid=(nj, T // tm),
                  in_specs=[ga, gs, yy, yy, ba, bs, yy], out_specs=(yy, yy, yy, yy, ba, ba),
                  dims=("arbitrary", "arbitrary"))(proj, proj, y_att, y_ssm, b_gate, b_gate, dmerged)


CONV_TILE = 256


def _shift_rows(a, j, up=False):
    n = a.shape[0]
    r = pltpu.roll(a, n - j if up else j, 0)
    row = lax.broadcasted_iota(jnp.int32, (8, a.shape[1]), 0)
    if up:
        return jnp.concatenate([r[:n - 8], jnp.where(row < 8 - j, r[n - 8:], 0.0)], axis=0)
    return jnp.concatenate([jnp.where(row >= j, r[:8], 0.0), r[8:]], axis=0)


def _conv_pre(a, w_ref, b_ref):
    conv = b_ref[...] + w_ref[0:1, :] * a
    shifted = []
    for j in (1, 2):
        sh = _shift_rows(a, j)
        shifted.append(sh)
        conv = conv + w_ref[j:j + 1, :] * sh
    return conv, shifted


def _conv_act(up3, w_conv, b_conv):
    B, S, _ = up3.shape
    nj = D_FF // CONV_TILE
    a_spec = pl.BlockSpec((1, S, CONV_TILE), lambda b, j: (b, 0, j))
    v_spec = pl.BlockSpec((1, S, CONV_TILE), lambda b, j: (b, 0, nj + j))
    w_spec = pl.BlockSpec((3, CONV_TILE), lambda b, j: (0, j))
    b_spec = pl.BlockSpec((1, CONV_TILE), lambda b, j: (0, j))

    def body(a_ref, v_ref, w_ref, b_ref, o_ref):
        a = a_ref[0].astype(F32)
        conv, _ = _conv_pre(a, w_ref, b_ref)
        o_ref[0] = (conv * _sig(conv) * v_ref[0]).astype(BF16)

    return _pcall(body, name="conv_act", out_shape=_sds((B, S, D_FF), BF16), grid=(B, nj),
                  in_specs=[a_spec, v_spec, w_spec, b_spec], out_specs=a_spec, dims=("parallel", "parallel"))(
        up3, up3, w_conv, b_conv)


def _conv_bwd(up3, dact3, w_conv, b_conv):
    B, S, _ = up3.shape
    nj = D_FF // CONV_TILE
    a_spec = pl.BlockSpec((1, S, CONV_TILE), lambda j, b: (b, 0, j))
    v_spec = pl.BlockSpec((1, S, CONV_TILE), lambda j, b: (b, 0, nj + j))
    o_spec = pl.BlockSpec((2, 1, S, CONV_TILE), lambda j, b: (0, b, 0, j))
    w_spec = pl.BlockSpec((3, CONV_TILE), lambda j, b: (0, j))
    b_spec = pl.BlockSpec((1, CONV_TILE), lambda j, b: (0, j))

    def body(a_ref, v_ref, d_ref, w_ref, b_ref, dup_ref, dw_ref, db_ref):
        @pl.when(pl.program_id(1) == 0)
        def _():
            dw_ref[...] = jnp.zeros_like(dw_ref)
            db_ref[...] = jnp.zeros_like(db_ref)

        a = a_ref[0].astype(F32)
        d = d_ref[0].astype(F32)
        conv, shifted = _conv_pre(a, w_ref, b_ref)
        sg = _sig(conv)
        dup_ref[1, 0] = (d * conv * sg).astype(BF16)
        dconv = d * v_ref[0] * (sg * (1.0 + conv * (1.0 - sg)))
        da = w_ref[0:1, :] * dconv
        for j in (1, 2):
            da = da + w_ref[j:j + 1, :] * _shift_rows(dconv, j, up=True)
        dup_ref[0, 0] = da.astype(BF16)
        db_ref[...] += jnp.sum(dconv, axis=0, keepdims=True)
        dw_ref[0:1, :] += jnp.sum(dconv * a, axis=0, keepdims=True)
        dw_ref[1:2, :] += jnp.sum(dconv * shifted[0], axis=0, keepdims=True)
        dw_ref[2:3, :] += jnp.sum(dconv * shifted[1], axis=0, keepdims=True)

    return _pcall(body, name="conv_bwd",
                  out_shape=(_sds((2, B, S, D_FF), BF16), _sds((3, D_FF), F32), _sds((1, D_FF), F32)),
                  grid=(nj, B), in_specs=[a_spec, v_spec, a_spec, w_spec, b_spec],
                  out_specs=(o_spec, w_spec, b_spec), dims=("arbitrary", "arbitrary"))(up3, up3, dact3, w_conv, b_conv)


def _rows_tile(r, cap=640):
    for t in range(min(r, cap) - min(r, cap) % 8, 7, -8):
        if r % t == 0:
            return t
    return r


def _add2(a, b, out_dtype, name):
    R, N = a.shape
    tr = _rows_tile(R)
    spec = pl.BlockSpec((tr, N), lambda i: (i, 0))

    def body(a_ref, b_ref, o_ref):
        o_ref[...] = (a_ref[...] + b_ref[...]).astype(out_dtype)

    return _pcall(body, name=name, out_shape=_sds((R, N), out_dtype), grid=(R // tr,), in_specs=[spec, spec],
                  out_specs=spec, dims=("parallel",))(a, b)


def _sum_slots(q, name):
    n, R, N = q.shape
    tr = _rows_tile(R)

    def body(q_ref, o_ref):
        acc = q_ref[0].astype(F32)
        for s in range(1, n):
            acc = acc + q_ref[s].astype(F32)
        o_ref[...] = acc

    return _pcall(body, name=name, out_shape=_sds((R, N), F32), grid=(R // tr,),
                  in_specs=[pl.BlockSpec((n, tr, N), lambda i: (0, i, 0))], out_specs=pl.BlockSpec((tr, N), lambda i: (i, 0)),
                  dims=("parallel",))(q)


NATIVE = (("b_re", 16, 1024), ("b_im", 16, 1024), ("c_re", 16, 1024), ("c_im", 16, 1024), ("g_mix", 1, 1024),
          ("b_att", 1, 1024), ("b_ssm", 1, 1024), ("a_re", 1, 1024), ("a_im", 1, 1024), ("log_dt", 1, 128),
          ("d_skip", 1, 256), ("b_glu", 1, 256), ("g_ffn", 1, 1024), ("g_final", 1, 1024), ("b_conv", 1, 2048),
          ("w_conv", 3, 2048), ("loss", 1, 1))
N_MOD = 6
NATIVE_LATE = ("g_mix",)
MODS_LATE = (0, 1)


def _small_plan(late):
    pieces = [p for p in NATIVE if (p[0] in NATIVE_LATE) == late]
    mods = [k for k in range(N_MOD) if (k in MODS_LATE) == late]
    starts, r = {}, 0
    for name, rows, cols in pieces:
        starts[name] = r
        r += rows * (-(-cols // LANES))
    return pieces, mods, starts, -(-r // 8) * 8


def _pack_small(native, dmods, late):
    pieces, mods, starts, n_sum = _small_plan(late)
    B = dmods[mods[0]].shape[0]
    total = n_sum + 8 * len(mods)

    def body(*refs):
        xs, ms, o_ref = refs[:len(pieces)], refs[len(pieces):-1], refs[-1]
        o_ref[...] = jnp.zeros_like(o_ref)
        for (name, rows, cols), x_ref in zip(pieces, xs):
            chunks = -(-cols // LANES)
            if chunks == 1 and rows % 8 == 0:
                o_ref[starts[name]:starts[name] + rows, 0:cols] = x_ref[...]
                continue
            for i in range(rows):
                for q in range(chunks):
                    wd = min(LANES, cols - q * LANES)
                    r = starts[name] + i * chunks + q
                    o_ref[r:r + 1, 0:wd] = x_ref[i:i + 1, q * LANES:q * LANES + wd]
        for k, m_ref in enumerate(ms):
            for b in range(B):
                o_ref[n_sum + 8 * k + b:n_sum + 8 * k + b + 1, :] = m_ref[b]

    return _pcall(body, name="pack_small_late" if late else "pack_small_early", out_shape=_sds((total, LANES), F32))(
        *[native[n] for n, _, _ in pieces], *[dmods[k] for k in mods])


def _sum_unpack_small(gathered_early, gathered_late, B):
    plans = [_small_plan(False), _small_plan(True)]
    nd = gathered_early.shape[0]
    n_out = len(NATIVE)

    def body(*refs):
        g_refs, outs, dm_ref, accs = refs[0:2], refs[2:2 + n_out], refs[2 + n_out], refs[3 + n_out:]
        o = 0
        for g_ref, acc, (pieces, mods, starts, n_sum) in zip(g_refs, accs, plans):
            s = g_ref[0, 0:n_sum, :]
            for d in range(1, nd):
                s = s + g_ref[d, 0:n_sum, :]
            acc[...] = s
            for name, rows, cols in pieces:
                o_ref = outs[o]
                o += 1
                chunks = -(-cols // LANES)
                if chunks == 1 and rows % 8 == 0:
                    o_ref[...] = acc[starts[name]:starts[name] + rows, 0:cols]
                    continue
                for i in range(rows):
                    for q in range(chunks):
                        wd = min(LANES, cols - q * LANES)
                        r = starts[name] + i * chunks + q
                        o_ref[i:i + 1, q * LANES:q * LANES + wd] = acc[r:r + 1, 0:wd]
            for d in range(nd):
                for j, k in enumerate(mods):
                    dm_ref[d, :, k * D_MODEL:(k + 1) * D_MODEL] = g_ref[d, n_sum + 8 * j:n_sum + 8 * j + B, :]

    ordered = [p for pieces, _, _, _ in plans for p in pieces]
    out_shape = tuple(_sds((rows, cols), F32) for _, rows, cols in ordered) + (_sds((nd, B, N_MOD * D_MODEL), F32),)
    res = _pcall(body, name="sum_unpack_small", out_shape=out_shape,
                 scratch_shapes=[pltpu.VMEM((n_sum, LANES), F32) for _, _, _, n_sum in plans])(gathered_early, gathered_late)
    return {n: r for (n, _, _), r in zip(ordered, res[:-1])}, res[-1]


def _small_from_native(nat):
    lanes3 = lambda a: a.reshape(SSM_GROUP_CH, SSM_GROUPS, SSM_STATE)
    return dict(
        g_mix=nat["g_mix"].reshape(D_MODEL), b_gate=jnp.concatenate([nat["b_att"], nat["b_ssm"]], axis=1).reshape(2 * D_MODEL),
        a_re=nat["a_re"].reshape(SSM_GROUPS, SSM_STATE), a_im=nat["a_im"].reshape(SSM_GROUPS, SSM_STATE),
        log_dt=nat["log_dt"][0, :SSM_GROUPS], b_re=_groups_from_lanes(nat["b_re"]), b_im=_groups_from_lanes(nat["b_im"]),
        c_re=lanes3(nat["c_re"]).transpose(1, 0, 2), c_im=lanes3(nat["c_im"]).transpose(1, 0, 2),
        d_skip=nat["d_skip"].reshape(SSM_WIDTH), b_glu=nat["b_glu"].reshape(SSM_WIDTH), g_ffn=nat["g_ffn"].reshape(D_MODEL),
        w_conv=nat["w_conv"], b_conv=nat["b_conv"].reshape(D_FF), g_final=nat["g_final"].reshape(D_MODEL))


def _adamw_multi(params):
    n = len(params)
    bc1 = 1.0 - ADAM_B1 ** ADAM_STEP
    bc2 = 1.0 - ADAM_B2 ** ADAM_STEP

    def body(*refs):
        ins, outs = refs[:4 * n], refs[4 * n:]
        for i in range(n):
            w_ref, g_ref, m_ref, v_ref = ins[4 * i:4 * i + 4]
            d_ref, nm_ref, nv_ref = outs[3 * i:3 * i + 3]
            g = g_ref[...]
            m = ADAM_B1 * m_ref[...] + (1.0 - ADAM_B1) * g
            v = ADAM_B2 * v_ref[...] + (1.0 - ADAM_B2) * (g * g)
            nm_ref[...] = m
            nv_ref[...] = v
            d_ref[...] = -ADAM_LR * ((m / bc1) / (jnp.sqrt(v / bc2) + ADAM_EPS) + ADAM_WD * w_ref[...])

    flat = [a for p in params for a in p]
    out_shape = tuple(_sds(p[0].shape, F32) for p in params for _ in range(3))
    res = _pcall(body, name="adamw_small", out_shape=out_shape)(*flat)
    return [tuple(res[3 * i:3 * i + 3]) for i in range(n)]


def _adamw(w, g, m, v, name, g_other=None):
    R, N = w.shape
    tr = _rows_tile(R, 256)
    spec = pl.BlockSpec((tr, N), lambda i: (i, 0))
    bc1 = 1.0 - ADAM_B1 ** ADAM_STEP
    bc2 = 1.0 - ADAM_B2 ** ADAM_STEP
    two = g_other is not None

    def body(*refs):
        w_ref, g_ref, m_ref, v_ref = refs[:4]
        d_ref, nm_ref, nv_ref = refs[4 + two:7 + two]
        g = g_ref[...]
        if two:
            g = g + refs[4][...]
            refs[8][...] = g
        m = ADAM_B1 * m_ref[...] + (1.0 - ADAM_B1) * g
        v = ADAM_B2 * v_ref[...] + (1.0 - ADAM_B2) * (g * g)
        nm_ref[...] = m
        nv_ref[...] = v
        d_ref[...] = -ADAM_LR * ((m / bc1) / (jnp.sqrt(v / bc2) + ADAM_EPS) + ADAM_WD * w_ref[...])

    shp = _sds((R, N), F32)
    args = (w, g, m, v) + ((g_other,) if two else ())
    return _pcall(body, name=name, out_shape=(shp,) * (3 + two), grid=(R // tr,), in_specs=[spec] * len(args),
                  out_specs=(spec,) * (3 + two), dims=("parallel",))(*args)


_GROUP_MASKS = {
    "all": [(dx, dy, dc) for dx in (0, 1) for dy in (0, 1) for dc in (0, 1) if (dx, dy, dc) != (0, 0, 0)],
    "xy": [(1, 0, 0), (0, 1, 0), (1, 1, 0)],
    "c": [(0, 0, 1)],
}
_GROUP_SLOTS = {"all": 8, "xy": 4, "c": 2}


def _group_slot(group, x, y, c):
    return {"all": 4 * x + 2 * y + c, "xy": 2 * x + y, "c": c}[group]


def _flip(v, d):
    return 1 - v if d else v


def _exchange(arr, group, mode, name):
    return _exchange_list([arr], group, mode, name)[0]


def _exchange_list(arrs, group, mode, name):
    masks = _GROUP_MASKS[group]
    n = len(masks)
    na = len(arrs)
    out_shapes, halves, bounce = [], [], []
    for arr in arrs:
        if mode == "gather":
            out_shapes.append((_GROUP_SLOTS[group],) + arr.shape)
            bounce.append(pltpu.VMEM(arr.shape, arr.dtype))
        elif mode == "scatter":
            assert arr.shape[0] == _GROUP_SLOTS[group]
            out_shapes.append(arr.shape)
            bounce.append(pltpu.VMEM(arr.shape[1:], arr.dtype))
        elif mode == "swap":
            assert group == "c"
            out_shapes.append(arr.shape)
        else:
            assert group == "c"
            halves.append(arr.shape[1] // 2)
            out_shapes.append((arr.shape[0], arr.shape[1] // 2, arr.shape[2]))
    has_local = mode in ("gather", "scatter")

    def body(*refs):
        x_refs, o_refs = refs[:na], refs[na:2 * na]
        send_sems, recv_sems = refs[2 * na], refs[2 * na + 1]
        x, y, c = lax.axis_index("x"), lax.axis_index("y"), lax.axis_index("c")
        me = _group_slot(group, x, y, c)
        if has_local:
            local_sems = refs[2 * na + 2]
            bufs = refs[2 * na + 3:]
            loads = []
            for i in range(na):
                src = x_refs[i] if mode == "gather" else x_refs[i].at[me]
                loads.append(pltpu.make_async_copy(src, bufs[i], local_sems.at[2 * i]))
                loads[-1].start()
        copies = []
        for i in range(na):
            x_ref, o_ref = x_refs[i], o_refs[i]
            for k, (dx, dy, dc) in enumerate(masks):
                px, py, pc = _flip(x, dx), _flip(y, dy), _flip(c, dc)
                if mode == "gather":
                    src, dst = x_ref, o_ref.at[me]
                elif mode == "scatter":
                    src, dst = x_ref.at[_group_slot(group, px, py, pc)], o_ref.at[me]
                elif mode == "swap":
                    src, dst = x_ref, o_ref
                else:
                    src, dst = x_ref.at[:, pl.ds(pl.multiple_of(pc * halves[i], 8), halves[i]), :], o_ref
                cp = pltpu.make_async_remote_copy(src_ref=src, dst_ref=dst, send_sem=send_sems.at[i * n + k],
                                                  recv_sem=recv_sems.at[i * n + k], device_id=(px, py, pc),
                                                  device_id_type=pl.DeviceIdType.MESH)
                cp.start()
                copies.append(cp)
        if has_local:
            stores = []
            for i in range(na):
                loads[i].wait()
                stores.append(pltpu.make_async_copy(bufs[i], o_refs[i].at[me], local_sems.at[2 * i + 1]))
                stores[-1].start()
        for cp in copies:
            cp.wait()
        if has_local:
            for st in stores:
                st.wait()

    anyspec = pl.BlockSpec(memory_space=pl.ANY)
    scratch = [pltpu.SemaphoreType.DMA((n * na,)), pltpu.SemaphoreType.DMA((n * na,))]
    if has_local:
        scratch += [pltpu.SemaphoreType.DMA((2 * na,))] + bounce
    outs = pl.pallas_call(body, name=name, out_shape=tuple(_sds(s, a.dtype) for s, a in zip(out_shapes, arrs)),
                          in_specs=[anyspec] * na, out_specs=tuple([anyspec] * na), scratch_shapes=scratch,
                          compiler_params=pltpu.CompilerParams(vmem_limit_bytes=V7X_VMEM_LIMIT_BYTES))(*arrs)
    return list(outs)


def _gather_weights(shards, name):
    na = len(shards)
    masks = _GROUP_MASKS["xy"]
    n = len(masks)

    def body(*refs):
        x_refs, o_refs = refs[:na], refs[na:2 * na]
        send_sems, recv_sems, local_sems = refs[2 * na:2 * na + 3]
        bufs = refs[2 * na + 3:]
        x, y, c = lax.axis_index("x"), lax.axis_index("y"), lax.axis_index("c")
        me = 2 * x + y
        sibling = (x, y, 1 - c)
        loads = []
        for i in range(na):
            loads.append(pltpu.make_async_copy(x_refs[i], bufs[i], local_sems.at[2 * i]))
            loads[-1].start()

        def half_of(i, slot, cc):
            h = shards[i].shape[0] // 2
            return o_refs[i].at[slot, pl.ds(pl.multiple_of(cc * h, 8), h), :]

        def src_half(i, cc):
            h = shards[i].shape[0] // 2
            return x_refs[i].at[pl.ds(pl.multiple_of(cc * h, 8), h), :]

        sends = []
        for i in range(na):
            for k, (dx, dy, _) in enumerate(masks):
                cp = pltpu.make_async_remote_copy(src_ref=src_half(i, c), dst_ref=half_of(i, me, c),
                                                  send_sem=send_sems.at[i * 2 * n + k], recv_sem=recv_sems.at[i * 2 * n + k],
                                                  device_id=(_flip(x, dx), _flip(y, dy), c),
                                                  device_id_type=pl.DeviceIdType.MESH)
                cp.start()
                sends.append(cp)
        stores = []
        for i in range(na):
            loads[i].wait()
            stores.append(pltpu.make_async_copy(bufs[i], o_refs[i].at[me], local_sems.at[2 * i + 1]))
            stores[-1].start()
        for i in range(na):
            for k, (dx, dy, _) in enumerate(masks):
                slot = 2 * _flip(x, dx) + _flip(y, dy)
                landed = pltpu.make_async_remote_copy(src_ref=src_half(i, c), dst_ref=half_of(i, slot, c),
                                                      send_sem=send_sems.at[i * 2 * n + k],
                                                      recv_sem=recv_sems.at[i * 2 * n + k], device_id=sibling,
                                                      device_id_type=pl.DeviceIdType.MESH)
                landed.wait_recv()
                fwd = pltpu.make_async_remote_copy(src_ref=half_of(i, slot, c), dst_ref=half_of(i, slot, c),
                                                   send_sem=send_sems.at[i * 2 * n + n + k],
                                                   recv_sem=recv_sems.at[i * 2 * n + n + k], device_id=sibling,
                                                   device_id_type=pl.DeviceIdType.MESH)
                fwd.start()
                sends.append(fwd)
        for i in range(na):
            for k, (dx, dy, _) in enumerate(masks):
                slot = 2 * _flip(x, dx) + _flip(y, dy)
                pltpu.make_async_remote_copy(src_ref=half_of(i, slot, 1 - c), dst_ref=half_of(i, slot, 1 - c),
                                             send_sem=send_sems.at[i * 2 * n + n + k],
                                             recv_sem=recv_sems.at[i * 2 * n + n + k], device_id=sibling,
                                             device_id_type=pl.DeviceIdType.MESH).wait_recv()
        for cp in sends:
            cp.wait_send()
        for st in stores:
            st.wait()

    anyspec = pl.BlockSpec(memory_space=pl.ANY)
    scratch = [pltpu.SemaphoreType.DMA((2 * n * na,)), pltpu.SemaphoreType.DMA((2 * n * na,)),
               pltpu.SemaphoreType.DMA((2 * na,))] + [pltpu.VMEM(s.shape, s.dtype) for s in shards]
    outs = pl.pallas_call(body, name=name, out_shape=tuple(_sds((N_XY,) + s.shape, s.dtype) for s in shards),
                          in_specs=[anyspec] * na, out_specs=tuple([anyspec] * na), scratch_shapes=scratch,
                          compiler_params=pltpu.CompilerParams(vmem_limit_bytes=V7X_VMEM_LIMIT_BYTES))(*shards)
    return list(outs)


def _pair_add(g, theirs, core, name):
    n4, h2, w = g.shape
    h = h2 // 2
    tr = _rows_tile(h)
    nb = h // tr

    def body(c_ref, g_ref, t_ref, o_ref):
        o_ref[...] = (g_ref[...] + t_ref[...]).astype(BF16)

    grid_spec = pltpu.PrefetchScalarGridSpec(
        num_scalar_prefetch=1, grid=(n4, nb),
        in_specs=[pl.BlockSpec((None, tr, w), lambda j, i, c_ref: (j, c_ref[0] * nb + i, 0)),
                  pl.BlockSpec((None, tr, w), lambda j, i, c_ref: (j, i, 0))],
        out_specs=pl.BlockSpec((None, tr, w), lambda j, i, c_ref: (j, i, 0)))
    return pl.pallas_call(body, name=name, out_shape=_sds((n4, h, w), BF16), grid_spec=grid_spec,
                          compiler_params=pltpu.CompilerParams(vmem_limit_bytes=V7X_VMEM_LIMIT_BYTES,
                                                               dimension_semantics=("parallel", "parallel")))(core, g, theirs)


BIG = (("w_proj_att", (ATT_WIDTH, D_MODEL), 1), ("w_proj_ssm", (SSM_WIDTH, D_MODEL), 1),
       ("w_glu", (SSM_WIDTH, SSM_WIDTH), 0))
DIRECT = (("w_in", True), ("w_up", True), ("w_down", False), ("w_out", False))
N_XY = 4


def _big_rows(shape):
    return shape[0] * shape[1] // N_XY // LANES


FLAT_ROWS = sum(_big_rows(s) for _, s, _ in BIG)


def _shard_shape(shape, axis):
    return (shape[0] // N_XY, shape[1]) if axis == 0 else (shape[0], shape[1] // N_XY)


def _flatten_shards(shards):
    return jnp.concatenate([shards[n].reshape(_big_rows(s), LANES) for n, s, _ in BIG], axis=0)


def _unflatten_shard(flat):
    out, r = {}, 0
    for n, s, ax in BIG:
        k = _big_rows(s)
        out[n] = flat[r:r + k].reshape(_shard_shape(s, ax))
        r += k
    return out


def _unflatten_full(flat4):
    out, r = {}, 0
    for n, s, ax in BIG:
        k = _big_rows(s)
        sh = _shard_shape(s, ax)
        t = flat4[:, r:r + k].reshape((N_XY,) + sh)
        out[n] = t.reshape(s) if ax == 0 else t.transpose(1, 0, 2).reshape(s)
        r += k
    return out


def _flatten_full(full):
    parts = []
    for n, s, ax in BIG:
        sh = _shard_shape(s, ax)
        t = full[n]
        t = t.reshape((N_XY,) + sh) if ax == 0 else t.reshape(s[0], N_XY, sh[1]).transpose(1, 0, 2)
        parts.append(t.reshape(N_XY, _big_rows(s), LANES))
    return jnp.concatenate(parts, axis=1)


def _pack_rows(arrs):
    rows, counts = [], []
    for a in arrs:
        f = a.reshape(-1)
        k = -(-f.shape[0] // LANES)
        rows.append(jnp.pad(f, (0, k * LANES - f.shape[0])).reshape(k, LANES))
        counts.append(k)
    return jnp.concatenate(rows, axis=0), counts


def _unpack_rows(buf, shapes):
    out, r = [], 0
    for s in shapes:
        size = int(np.prod(s))
        k = -(-size // LANES)
        out.append(buf[r:r + k].reshape(-1)[:size].reshape(s))
        r += k
    return out


def _lanes_from_groups(a):
    return a.transpose(2, 0, 1).reshape(SSM_GROUP_CH, SSM_LANES)


def _groups_from_lanes(a):
    return a.reshape(SSM_GROUP_CH, SSM_GROUPS, SSM_STATE).transpose(1, 2, 0)


LATE = ("w_up_t", "w_down", "w_out")
EARLY_GRADS = ("w_up_t", "w_down", "w_out")


def _local_step(x3, mod, tgt3, W, P, late_shards=None, scatter_grads=False):
    B, S, _ = x3.shape
    T = B * S
    seq_blocks = S // ATT_BLOCK
    sh1, sc1, gt1, sh2, sc2, gt2 = [m.reshape(B, 1, D_MODEL) for m in jnp.split(mod, 6, axis=-1)]
    g_mix, g_ffn, g_final = P["g_mix"].reshape(1, D_MODEL), P["g_ffn"].reshape(1, D_MODEL), P["g_final"].reshape(1, D_MODEL)
    b_gate = P["b_gate"].reshape(1, 2 * D_MODEL)
    d_skip, b_glu = P["d_skip"].reshape(1, SSM_WIDTH), P["b_glu"].reshape(1, SSM_WIDTH)
    w_conv, b_conv = P["w_conv"], P["b_conv"].reshape(1, D_FF)

    u1 = _norm_mod(x3, g_mix, sc1, sh1).reshape(T, D_MODEL)
    proj = _mm(u1, W["w_in_t"], tb=True, name="mm_proj", out_dtype=BF16)
    proj3 = proj.reshape(B, S, IN_WIDTH)
    us = proj[:, 3 * ATT_WIDTH:3 * ATT_WIDTH + SSM_WIDTH]
    o_att3, lse4, late = _attention_fwd(proj3, seq_blocks, _Riders(late_shards, "gather") if late_shards else None)
    if late_shards:
        W = dict(W, **{n: f.reshape(-1, LANES) for n, f in zip(LATE, late)})
        w_conv = late[len(LATE)].transpose(1, 0, 2).reshape(3, D_FF)
    o_att = o_att3.reshape(T, ATT_WIDTH)
    y_att = _mm(o_att, W["w_proj_att"], name="mm_proj_att", out_dtype=BF16)

    lr = P["a_re"].reshape(1, SSM_LANES)
    li = P["a_im"].reshape(1, SSM_LANES)
    ldt = jnp.repeat(P["log_dt"], SSM_STATE).reshape(1, SSM_LANES)
    br, bi = _lanes_from_groups(P["b_re"]), _lanes_from_groups(P["b_im"])
    cr = P["c_re"].transpose(1, 0, 2).reshape(SSM_GROUP_CH, SSM_LANES)
    ci = P["c_im"].transpose(1, 0, 2).reshape(SSM_GROUP_CH, SSM_LANES)
    abar, w_bu, w_c = _ssm_params(lr, li, ldt, br, bi, cr, ci)
    xs3, y_core3 = _ssm_scan_fwd(proj3, abar, w_bu, w_c)
    y5, s_out = _ssm_post(y_core3.reshape(T, SSM_WIDTH), us, d_skip, W["w_glu"], b_glu)
    y_ssm = _mm(s_out, W["w_proj_ssm"], name="mm_proj_ssm", out_dtype=BF16)

    merged = _merge(proj, y_att, y_ssm, b_gate)
    mix = _mm(merged, W["w_out"], name="mm_out", out_dtype=BF16)
    mix3 = mix.reshape(B, S, D_MODEL)

    h1, u2 = _resid_norm_mod(x3, mix3, gt1, g_ffn, sc2, sh2)
    u2 = u2.reshape(T, D_MODEL)
    up3 = _mm(u2, W["w_up_t"], tb=True, name="mm_up", out_dtype=BF16).reshape(B, S, 2 * D_FF)
    act = _conv_act(up3, w_conv, b_conv).reshape(T, D_FF)
    ffn3 = _mm(act, W["w_down"], name="mm_down", out_dtype=BF16).reshape(B, S, D_MODEL)
    dh2, dffn, dgt2, dg_final, loss = _final_loss(h1, ffn3, tgt3, gt2, g_final)

    dffn = dffn.reshape(T, D_MODEL)
    gw = {}
    gw["w_down"] = _mm(act, dffn, ta=True, out_dtype=BF16, name="mm_dw_down")
    dact3 = _mm(dffn, W["w_down"], tb=True, name="mm_dact", out_dtype=BF16).reshape(B, S, D_FF)
    dup3, dw_conv, db_conv = _conv_bwd(up3, dact3, w_conv, b_conv)
    dup = dup3.reshape(2, T, D_FF)
    gw["w_up_t"] = _mm(dup, u2, ta=True, out_dtype=BF16, name="mm_dw_up")
    du2 = _mm(dup, W["w_up_t"], name="mm_du2", out_dtype=BF16).reshape(B, S, D_MODEL)
    dh1, dsh2, dsc2, dg_ffn, dgt1, dmix = _norm_bwd(h1, du2, dh2, g_ffn, sc2, "norm_bwd2", mix3=mix3, gt=gt1)

    dmix = dmix.reshape(T, D_MODEL)
    gw["w_out"] = _mm(merged, dmix, ta=True, out_dtype=BF16, name="mm_dw_out")
    dmerged = _mm(dmix, W["w_out"], tb=True, name="mm_dmerged", out_dtype=BF16)
    dy_att, dy_ssm, dga, dgs, db_att, db_ssm = _merge_bwd(proj, y_att, y_ssm, b_gate, dmerged)

    gw["w_proj_ssm"] = _mm(s_out, dy_ssm, ta=True, name="mm_dw_proj_ssm")
    ds_out = _mm(dy_ssm, W["w_proj_ssm"], tb=True, name="mm_ds_out")
    dy5, dd_skip, db_glu, dw_glu = _ssm_post_bwd(y5, us, ds_out, d_skip, W["w_glu"], b_glu)
    gw["w_glu"] = dw_glu
    dus3, dab, dwbu, dwc = _ssm_scan_bwd(proj3, dy5.reshape(B, S, SSM_WIDTH), xs3, abar, w_bu, w_c, d_skip)
    dus = dus3.reshape(T, SSM_WIDTH)
    dlr, dli, dldt, dbr, dbi, dcr, dci = _ssm_params_bwd(lr, li, ldt, br, bi, dab, dwbu, dwc)

    gw["w_proj_att"] = _mm(o_att, dy_att, ta=True, name="mm_dw_proj_att")
    do_att = _mm(dy_att, W["w_proj_att"], tb=True, out_dtype=BF16, name="mm_do_att")
    early = [gw[n].reshape(N_XY, -1, LANES) for n in EARLY_GRADS]
    early.append(_flatten_full({n: gw[n].astype(BF16) for n, _, _ in BIG}))
    dq3, dk3, dv3, parts = _attention_bwd(proj3, do_att.reshape(B, S, ATT_WIDTH), o_att3, lse4, seq_blocks,
                                          _Riders(early, "scatter") if scatter_grads else None)
    dproj = jnp.concatenate([t.reshape(T, ATT_WIDTH) for t in (dq3, dk3, dv3)] + [dus, dga, dgs], axis=1)
    dmods = [None, None, dgt1, dsh2, dsc2, dgt2]
    native = dict(b_att=db_att, b_ssm=db_ssm, a_re=dlr, a_im=dli, log_dt=dldt, b_re=dbr, b_im=dbi, c_re=dcr, c_im=dci,
                  d_skip=dd_skip, b_glu=db_glu, g_ffn=dg_ffn, w_conv=dw_conv, b_conv=db_conv, g_final=dg_final, loss=loss)
    small_early = _pack_small(native, dmods, False)
    if scatter_grads:
        gw["w_in_t"], (small_early,) = _mm(dproj, u1, ta=True, out_dtype=BF16, name="mm_dw_in",
                                           riders=_Riders([small_early], "gather", "all"))
        du1, last_parts = _mm(dproj, W["w_in_t"], name="mm_du1", out_dtype=BF16,
                              riders=_Riders([gw["w_in_t"].reshape(N_XY, -1, LANES)], "scatter"))
        parts = parts + last_parts
    else:
        gw["w_in_t"] = _mm(dproj, u1, ta=True, out_dtype=BF16, name="mm_dw_in")
        du1 = _mm(dproj, W["w_in_t"], name="mm_du1", out_dtype=BF16)
    du1 = du1.reshape(B, S, D_MODEL)
    dx, dsh1, dsc1, dg_mix = _norm_bwd(x3, du1, dh1, g_mix, sc1, "norm_bwd1")
    dmods[0], dmods[1] = dsh1, dsc1
    native["g_mix"] = dg_mix
    return loss, dx, dmods, gw, native, parts, small_early


WEIGHTS = ['w_ada', 'b_ada', 'g_mix', 'w_in', 'b_gate', 'a_re', 'a_im', 'log_dt', 'b_re', 'b_im', 'c_re', 'c_im', 'd_skip',
           'w_glu', 'b_glu', 'w_proj_att', 'w_proj_ssm', 'w_out', 'g_ffn', 'w_up', 'w_conv', 'b_conv', 'w_down', 'g_final']
SMALL = ['g_mix', 'b_gate', 'a_re', 'a_im', 'log_dt', 'b_re', 'b_im', 'c_re', 'c_im', 'd_skip', 'b_glu', 'g_ffn', 'w_conv',
         'b_conv', 'g_final']


def kernel(x, c, w_ada, b_ada, g_mix, w_in, b_gate, a_re, a_im, log_dt, b_re, b_im, c_re, c_im, d_skip, w_glu, b_glu, w_proj_att, w_proj_ssm, w_out, g_ffn, w_up, w_conv, b_conv, w_down, g_final, loss_target, m_w_ada, m_b_ada, m_g_mix, m_w_in, m_b_gate, m_a_re, m_a_im, m_log_dt, m_b_re, m_b_im, m_c_re, m_c_im, m_d_skip, m_w_glu, m_b_glu, m_w_proj_att, m_w_proj_ssm, m_w_out, m_g_ffn, m_w_up, m_w_conv, m_b_conv, m_w_down, m_g_final, v_w_ada, v_b_ada, v_g_mix, v_w_in, v_b_gate, v_a_re, v_a_im, v_log_dt, v_b_re, v_b_im, v_c_re, v_c_im, v_d_skip, v_w_glu, v_b_glu, v_w_proj_att, v_w_proj_ssm, v_w_out, v_g_ffn, v_w_up, v_w_conv, v_b_conv, v_w_down, v_g_final):
    args = dict(locals())
    w = {n: args[n] for n in WEIGHTS}
    m = {n: args["m_" + n] for n in WEIGHTS}
    v = {n: args["v_" + n] for n in WEIGHTS}
    B, S, _ = x.shape
    ix, iy, ic = lax.axis_index("x"), lax.axis_index("y"), lax.axis_index("c")
    chip = 2 * ix + iy
    half = FLAT_ROWS // 2
    ada_cols = w_ada.shape[2]

    c_all = _exchange(c, "all", "gather", "gather_c").reshape(8 * B, D_MODEL)
    b_cols = lax.dynamic_slice_in_dim(b_ada, chip * ada_cols, ada_cols, axis=1)
    mod_cols = _ada_fwd(c_all, w_ada[0], b_cols)
    mod_all = _exchange(mod_cols, "xy", "gather", "gather_mod")
    mod_all = mod_all.transpose(1, 0, 2).reshape(8 * B, 6 * D_MODEL)
    mod = lax.dynamic_slice_in_dim(mod_all, (4 * ix + 2 * iy + ic) * B, B, axis=0)

    south = ic == 0
    core = ic.astype(jnp.int32).reshape(1)
    shard = {n + ("_t" if t else ""): (w[n][0].T if t else w[n][0]).astype(BF16) for n, t in DIRECT}
    misc = _flatten_shards({n: w[n][0] for n, _, _ in BIG}).astype(BF16)
    w_in_full, misc_full = _gather_weights([shard["w_in_t"], misc], "gather_weights")
    W = {"w_in_t": w_in_full.reshape(-1, LANES)}
    W.update(_unflatten_full(misc_full))

    P = {n: w[n][0] for n in SMALL if n not in ("w_conv", "g_final")}
    P["w_conv"] = None
    P["g_final"] = g_final

    loss, dx, dmods, gw, native, parts, small_early = _local_step(x, mod, loss_target, W, P,
                                                                  [shard[n] for n in LATE] + [w_conv[0]], True)

    small_late = _exchange(_pack_small(native, dmods, True), "all", "gather", "gather_small")
    native_sum, dmod_all = _sum_unpack_small(small_early, small_late, B)
    loss = native_sum["loss"][0, 0]
    g_small = _small_from_native(native_sum)
    dmod_all = dmod_all.reshape(8 * B, N_MOD * D_MODEL)
    dmod_cols = lax.dynamic_slice_in_dim(dmod_all, chip * ada_cols, ada_cols, axis=1)
    g_w_ada, g_b_ada = _ada_bwd(c_all, dmod_all, dmod_cols)

    red = [_sum_slots(p, "sum_chips_%d" % i) for i, p in enumerate(parts)]
    red_sib = _exchange_list(red, "c", "swap", "share_cores")
    order = list(EARLY_GRADS) + ["misc", "w_in_t"]
    halves = dict(zip(order, zip(red, red_sib)))

    grads = {"w_ada": g_w_ada[None], "b_ada": g_b_ada}
    grads["w_up"] = _add2(*halves["w_up_t"], F32, "add_cores_w_up").T[None]
    for k, gk in _unflatten_shard(_add2(*halves["misc"], F32, "add_cores_misc")).items():
        grads[k] = gk[None]
    wc_cols = w_conv.shape[2]
    for n in SMALL:
        g = g_small[n]
        if n == "w_conv":
            g = lax.dynamic_slice_in_dim(g, chip * wc_cols, wc_cols, axis=1)
        grads[n] = g.reshape(w[n].shape)

    delta, new_m, new_v = {}, {}, {}
    for n in ["w_ada"] + [b for b, _ in DIRECT] + [b for b, _, _ in BIG]:
        shp = w[n].shape
        if n == "w_in":
            r, s = halves["w_in_t"]
            d2, m2, v2, g2 = _adamw(w[n][0].T, r, m[n][0].T, v[n][0].T, "adamw_" + n, g_other=s)
            d2, m2, v2, grads[n] = d2.T, m2.T, v2.T, g2.T[None]
        elif n in ("w_down", "w_out"):
            r, s = halves[n]
            d2, m2, v2, g2 = _adamw(w[n][0], r, m[n][0], v[n][0], "adamw_" + n, g_other=s)
            grads[n] = g2[None]
        else:
            d2, m2, v2 = _adamw(w[n][0], grads[n][0], m[n][0], v[n][0], "adamw_" + n)
        delta[n], new_m[n], new_v[n] = d2.reshape(shp), m2.reshape(shp), v2.reshape(shp)
    rest = ["b_ada"] + SMALL

    def drop(a):
        return a.reshape(1, -1) if a.ndim == 1 else (a if a.ndim == 2 else a[0])

    upd = _adamw_multi([(drop(w[n]), drop(grads[n]), drop(m[n]), drop(v[n])) for n in rest])
    for n, (dd, mm, vv) in zip(rest, upd):
        delta[n], new_m[n], new_v[n] = dd.reshape(w[n].shape), mm.reshape(w[n].shape), vv.reshape(w[n].shape)

    return (loss, dx, *[grads[n] for n in WEIGHTS], *[delta[n] for n in WEIGHTS], *[new_m[n] for n in WEIGHTS],
            *[new_v[n] for n in WEIGHTS])
```

```python
import functools
import math

import jax
import jax.numpy as jnp
from jax import lax
from jax.experimental import pallas as pl
from jax.experimental.pallas import tpu as pltpu

F32, BF16 = jnp.float32, jnp.bfloat16

D_MODEL = 1024
N_HEADS = 8
HEAD_DIM = 64
ATT_WIDTH = 512
SSM_GROUPS = 16
SSM_GROUP_CH = 16
SSM_WIDTH = 256
SSM_STATE = 64
SSM_LANES = SSM_GROUPS * SSM_STATE
D_FF = 2048
IN_WIDTH = 3 * ATT_WIDTH + SSM_WIDTH + 2 * D_MODEL
ATT_BLOCK = 128
N_PATTERNS = 3
EPS = 1e-6
NEG_INF = -1e30

ADAM_LR, ADAM_B1, ADAM_B2, ADAM_EPS, ADAM_WD, ADAM_STEP = 0.001, 0.9, 0.999, 1e-08, 0.01, 10

V7X_VMEM_LIMIT_BYTES = 56 * 1024 * 1024
LANES = 1024


def _pcall(body, *, name, out_shape, grid=(), in_specs=None, out_specs=None, scratch_shapes=(), dims=None):
    params = dict(vmem_limit_bytes=V7X_VMEM_LIMIT_BYTES)
    if dims is not None:
        params["dimension_semantics"] = dims
    specs = {}
    if in_specs is not None:
        specs = dict(grid=grid, in_specs=in_specs, out_specs=out_specs)
    return pl.pallas_call(body, name=name, out_shape=out_shape, scratch_shapes=scratch_shapes,
                          compiler_params=pltpu.CompilerParams(**params), **specs)


def _sds(shape, dtype):
    return jax.ShapeDtypeStruct(tuple(shape), dtype)


def _tile(n, target):
    if n <= target:
        return n
    for t in range(target - target % 128, 0, -128):
        if n % t == 0:
            return t
    raise ValueError((n, target))


def _sig(v):
    return pl.reciprocal(1.0 + jnp.exp(-v), approx=True)


def _mm(a, b, *, name, ta=False, tb=False, out_dtype=F32, tm=2048, tn=1024, tk=1024, riders=None):
    halves = a.ndim == 3
    if halves:
        a_rows, a_cols = a.shape[1], 2 * a.shape[2]
    else:
        a_rows, a_cols = a.shape
    if ta:
        K, M = a_rows, a_cols
    else:
        M, K = a_rows, a_cols
    if tb:
        N, K2 = b.shape
    else:
        K2, N = b.shape
    assert K == K2, (a.shape, b.shape)
    if halves:
        tm, tk = (min(tm, M // 2), tk) if ta else (tm, min(tk, K // 2))
    tm, tn, tk = _tile(M, tm), _tile(N, tn), _tile(K, tk)
    nk = K // tk
    if halves and ta:
        per = a.shape[2] // tm
        a_spec = pl.BlockSpec((None, tk, tm), lambda i, j, k: (i // per, k, i % per))
    elif halves:
        per = a.shape[2] // tk
        a_spec = pl.BlockSpec((None, tm, tk), lambda i, j, k: (k // per, i, k % per))
    else:
        a_spec = pl.BlockSpec((tk, tm), lambda i, j, k: (k, i)) if ta else pl.BlockSpec((tm, tk), lambda i, j, k: (i, k))
    b_spec = pl.BlockSpec((tn, tk), lambda i, j, k: (j, k)) if tb else pl.BlockSpec((tk, tn), lambda i, j, k: (k, j))
    dn = (((0 if ta else 1,), (1 if tb else 0,)), ((), ()))

    def body(a_ref, b_ref, o_ref, acc_ref):
        k = pl.program_id(2)

        @pl.when(k == 0)
        def _():
            acc_ref[...] = jnp.zeros_like(acc_ref)

        acc_ref[...] += lax.dot_general(a_ref[...].astype(BF16), b_ref[...].astype(BF16), dn,
                                        preferred_element_type=F32)

        @pl.when(k == nk - 1)
        def _():
            o_ref[...] = acc_ref[...].astype(out_dtype)

    def body_single(a_ref, b_ref, o_ref):
        o_ref[...] = lax.dot_general(a_ref[...].astype(BF16), b_ref[...].astype(BF16), dn,
                                     preferred_element_type=F32).astype(out_dtype)

    grid = (M // tm, N // tn, nk)
    scratch = [] if nk == 1 else [pltpu.VMEM((tm, tn), F32)]
    o_spec = pl.BlockSpec((tm, tn), lambda i, j, k: (i, j))
    if riders is None:
        return _pcall(body_single if nk == 1 else body, name=name, out_shape=_sds((M, N), out_dtype), grid=grid,
                      in_specs=[a_spec, b_spec], out_specs=o_spec, scratch_shapes=scratch,
                      dims=("parallel", "parallel", "arbitrary"))(a, b)
    rs = riders
    res = _pcall(_with_riders(body_single if nk == 1 else body, rs, 2, 1, len(scratch), tuple(g - 1 for g in grid)),
                 name=name, out_shape=(_sds((M, N), out_dtype),) + tuple(rs.out_shape), grid=grid,
                 in_specs=[a_spec, b_spec] + rs.specs, out_specs=(o_spec,) + tuple(rs.specs),
                 scratch_shapes=scratch + rs.scratch, dims=("arbitrary", "arbitrary", "arbitrary"))(a, b, *rs.arrs)
    return res[0], list(res[1:])


def _ada_fwd(c_all, w_ada, b_ada_cols):
    n = w_ada.shape[1]

    def body(c_ref, w_ref, b_ref, o_ref):
        c = c_ref[...]
        act = c * _sig(c)
        o_ref[...] = jnp.dot(act.astype(BF16), w_ref[...].astype(BF16), preferred_element_type=F32) + b_ref[...]

    return _pcall(body, name="ada_fwd", out_shape=_sds((c_all.shape[0], n), F32))(c_all, w_ada, b_ada_cols)


def _ada_bwd(c_all, dmod_all, dmod_cols):
    n = dmod_cols.shape[1]

    def body(c_ref, da_ref, dc_ref, gw_ref, gb_ref):
        c = c_ref[...]
        act = c * _sig(c)
        gw_ref[...] = lax.dot_general(act, dc_ref[...], (((0,), (0,)), ((), ())), preferred_element_type=F32,
                                      precision=lax.Precision.HIGHEST)
        gb_ref[...] = jnp.sum(da_ref[...], axis=0, keepdims=True)

    return _pcall(body, name="ada_bwd", out_shape=(_sds((D_MODEL, n), F32), _sds((1, dmod_all.shape[1]), F32)))(
        c_all, dmod_all, dmod_cols)


ROW_TILE = 512


def _row_specs(B, S):
    ts = min(S, ROW_TILE)
    row = pl.BlockSpec((1, ts, D_MODEL), lambda b, s: (b, s, 0))
    bvec = pl.BlockSpec((1, 1, D_MODEL), lambda b, s: (b, 0, 0))
    gvec = pl.BlockSpec((1, D_MODEL), lambda b, s: (0, 0))
    return ts, row, bvec, gvec


def _norm_mod(x3, g, sc, sh):
    B, S, _ = x3.shape
    ts, row, bvec, gvec = _row_specs(B, S)

    def body(x_ref, g_ref, sc_ref, sh_ref, u_ref):
        x = x_ref[0]
        r = lax.rsqrt(jnp.mean(x * x, axis=-1, keepdims=True) + EPS)
        u_ref[0] = ((x * r) * g_ref[...] * (1.0 + sc_ref[0]) + sh_ref[0]).astype(BF16)

    return _pcall(body, name="norm_mod1", out_shape=_sds(x3.shape, BF16), grid=(B, S // ts),
                  in_specs=[row, gvec, bvec, bvec], out_specs=row, dims=("parallel", "parallel"))(x3, g, sc, sh)


def _resid_norm_mod(x3, mix3, gt, g, sc, sh):
    B, S, _ = x3.shape
    ts, row, bvec, gvec = _row_specs(B, S)

    def body(x_ref, m_ref, gt_ref, g_ref, sc_ref, sh_ref, h_ref, u_ref):
        h = x_ref[0] + gt_ref[0] * m_ref[0]
        h_ref[0] = h
        r = lax.rsqrt(jnp.mean(h * h, axis=-1, keepdims=True) + EPS)
        u_ref[0] = ((h * r) * g_ref[...] * (1.0 + sc_ref[0]) + sh_ref[0]).astype(BF16)

    return _pcall(body, name="resid_norm_mod2", out_shape=(_sds(x3.shape, F32), _sds(x3.shape, BF16)),
                  grid=(B, S // ts), in_specs=[row, row, bvec, gvec, bvec, bvec], out_specs=(row, row),
                  dims=("parallel", "parallel"))(x3, mix3, gt, g, sc, sh)


def _norm_bwd(h3, du3, dres3, g, sc, name, mix3=None, gt=None):
    B, S, _ = h3.shape
    ts, row, bvec, gvec = _row_specs(B, S)
    with_gate = mix3 is not None

    def body(*refs):
        if with_gate:
            h_ref, du_ref, dr_ref, g_ref, sc_ref, m_ref, gt_ref, dh_ref, dsh_ref, dsc_ref, dg_ref, dgt_ref, dm_ref = refs
        else:
            h_ref, du_ref, dr_ref, g_ref, sc_ref, dh_ref, dsh_ref, dsc_ref, dg_ref = refs
        b, s = pl.program_id(0), pl.program_id(1)
        h = h_ref[0]
        r = lax.rsqrt(jnp.mean(h * h, axis=-1, keepdims=True) + EPS)
        xn = h * r
        du = du_ref[0].astype(F32)
        g = g_ref[...]
        sc1 = 1.0 + sc_ref[0]
        dxn = du * g * sc1
        dh = dr_ref[0].astype(F32) + r * (dxn - xn * jnp.mean(dxn * xn, axis=-1, keepdims=True))
        dh_ref[0] = dh.astype(dh_ref.dtype)

        @pl.when(s == 0)
        def _():
            dsh_ref[...] = jnp.zeros_like(dsh_ref)
            dsc_ref[...] = jnp.zeros_like(dsc_ref)
            if with_gate:
                dgt_ref[...] = jnp.zeros_like(dgt_ref)

        @pl.when((s == 0) & (b == 0))
        def _():
            dg_ref[...] = jnp.zeros_like(dg_ref)

        dux = du * xn
        dsh_ref[0] += jnp.sum(du, axis=0, keepdims=True)
        dsc_ref[0] += jnp.sum(dux * g, axis=0, keepdims=True)
        dg_ref[...] += jnp.sum(dux * sc1, axis=0, keepdims=True)
        if with_gate:
            dgt_ref[0] += jnp.sum(dh * m_ref[0], axis=0, keepdims=True)
            dm_ref[0] = (dh * gt_ref[0]).astype(BF16)

    bshape = _sds((B, 1, D_MODEL), F32)
    in_specs = [row, row, row, gvec, bvec]
    out_shape = [_sds(h3.shape, BF16 if with_gate else F32), bshape, bshape, _sds((1, D_MODEL), F32)]
    out_specs = [row, bvec, bvec, gvec]
    args = [h3, du3, dres3, g, sc]
    if with_gate:
        in_specs += [row, bvec]
        out_shape += [bshape, _sds(h3.shape, BF16)]
        out_specs += [bvec, row]
        args += [mix3, gt]
    return _pcall(body, name=name, out_shape=tuple(out_shape), grid=(B, S // ts), in_specs=in_specs,
                  out_specs=tuple(out_specs), dims=("arbitrary", "arbitrary"))(*args)


def _final_loss(h1, ffn3, tgt3, gt, gfin):
    B, S, _ = h1.shape
    ts, row, bvec, gvec = _row_specs(B, S)
    one = pl.BlockSpec((1, 1), lambda b, s: (0, 0))

    def body(h_ref, f_ref, t_ref, gt_ref, gf_ref, dh_ref, dff_ref, dgt_ref, dgf_ref, loss_ref):
        b, s = pl.program_id(0), pl.program_id(1)
        f = f_ref[0].astype(F32)
        gtv = gt_ref[0]
        gf = gf_ref[...]
        h2 = h_ref[0] + gtv * f
        r = lax.rsqrt(jnp.mean(h2 * h2, axis=-1, keepdims=True) + EPS)
        n = h2 * r
        e = n * gf - t_ref[0]
        dy = e * (1.0 / D_MODEL)
        dn = dy * gf
        dh2 = r * (dn - n * jnp.mean(dn * n, axis=-1, keepdims=True))
        dh_ref[0] = dh2.astype(BF16)
        dff_ref[0] = (dh2 * gtv).astype(BF16)

        @pl.when(s == 0)
        def _():
            dgt_ref[...] = jnp.zeros_like(dgt_ref)

        @pl.when((s == 0) & (b == 0))
        def _():
            dgf_ref[...] = jnp.zeros_like(dgf_ref)
            loss_ref[...] = jnp.zeros_like(loss_ref)

        dgt_ref[0] += jnp.sum(dh2 * f, axis=0, keepdims=True)
        dgf_ref[...] += jnp.sum(dy * n, axis=0, keepdims=True)
        rows = jnp.sum(e * e, axis=1, keepdims=True)
        loss_ref[...] += jnp.sum(rows, axis=0, keepdims=True) * (0.5 / D_MODEL)

    return _pcall(body, name="final_loss",
                  out_shape=(_sds(h1.shape, BF16), _sds(h1.shape, BF16), _sds((B, 1, D_MODEL), F32),
                             _sds((1, D_MODEL), F32), _sds((1, 1), F32)),
                  grid=(B, S // ts), in_specs=[row, row, row, bvec, gvec], out_specs=(row, row, bvec, gvec, one),
                  dims=("arbitrary", "arbitrary"))(h1, ffn3, tgt3, gt, gfin)


ATT_GROUP = 4
ATT_GW = ATT_GROUP * HEAD_DIM
ATT_GROUPS = N_HEADS // ATT_GROUP
ATT_PAIRS = ATT_GW // ATT_BLOCK
ATT_UNROLL = 5
ATT_RESIDUE_UNROLL = 4
NT_DIMS = (((1,), (1,)), ((), ()))
TN_DIMS = (((0,), (0,)), ((), ()))


def _att_rows(start, d):
    if d == 1:
        return pl.ds(start if isinstance(start, int) else pl.multiple_of(start, ATT_BLOCK), ATT_BLOCK)
    return pl.ds(start, ATT_BLOCK, stride=d)


def _att_fill_bias(bias_ref, g, d):
    a = lax.broadcasted_iota(jnp.int32, (ATT_BLOCK, ATT_BLOCK), 0)
    j = lax.broadcasted_iota(jnp.int32, (ATT_BLOCK, ATT_BLOCK), 1)
    dist = (a - j).astype(F32)
    for hh in range(ATT_GROUP):
        t, e = divmod(hh, 2)
        rs = slice(e * ATT_BLOCK, (e + 1) * ATT_BLOCK)
        lo = 2.0 ** (-8.0 * (hh + 1) / N_HEADS) * d
        hi = 2.0 ** (-8.0 * (ATT_GROUP + hh + 1) / N_HEADS) * d
        slope = jnp.where(g == 0, lo, hi).astype(F32)
        bias_ref[t, rs, 0:ATT_BLOCK] = jnp.where(a >= j, -slope * dist, NEG_INF)
        bias_ref[t, rs, ATT_BLOCK:] = jnp.where(j >= a, -slope * (dist + float(ATT_BLOCK)), NEG_INF)


def _stack_heads(v2, low):
    return jnp.concatenate([jnp.where(low, v2, 0.0), jnp.where(low, 0.0, v2)], axis=0).astype(BF16)


def _unstack_heads(r2, low):
    return jnp.where(low, r2[0:ATT_BLOCK], r2[ATT_BLOCK:])


class _Riders:
    def __init__(self, arrs, mode, group="xy"):
        self.arrs, self.mode, self.n, self.group = list(arrs), mode, len(arrs), group
        slot_shapes = [a.shape if mode == "gather" else a.shape[1:] for a in self.arrs]
        self.out_shape = [_sds((_GROUP_SLOTS[group],) + s, a.dtype) for s, a in zip(slot_shapes, self.arrs)]
        k = len(_GROUP_MASKS[group])
        self.scratch = [pltpu.SemaphoreType.DMA((k * self.n,)), pltpu.SemaphoreType.DMA((k * self.n,)),
                        pltpu.SemaphoreType.DMA((2 * self.n,))] + [pltpu.VMEM(s, a.dtype) for s, a in zip(slot_shapes, self.arrs)]
        self.specs = [pl.BlockSpec(memory_space=pl.ANY)] * self.n

    def _remote(self, x_refs, o_refs, send_sems, recv_sems):
        x, y, c = lax.axis_index("x"), lax.axis_index("y"), lax.axis_index("c")
        me = _group_slot(self.group, x, y, c)
        masks = _GROUP_MASKS[self.group]
        cps = []
        for i in range(self.n):
            for k, (dx, dy, dc) in enumerate(masks):
                px, py, pc = _flip(x, dx), _flip(y, dy), _flip(c, dc)
                src = x_refs[i] if self.mode == "gather" else x_refs[i].at[_group_slot(self.group, px, py, pc)]
                cps.append(pltpu.make_async_remote_copy(
                    src_ref=src, dst_ref=o_refs[i].at[me], send_sem=send_sems.at[len(masks) * i + k],
                    recv_sem=recv_sems.at[len(masks) * i + k], device_id=(px, py, pc),
                    device_id_type=pl.DeviceIdType.MESH))
        return cps, me

    def start(self, x_refs, o_refs, scratch):
        send_sems, recv_sems, local_sems, bufs = scratch[0], scratch[1], scratch[2], scratch[3:]
        cps, me = self._remote(x_refs, o_refs, send_sems, recv_sems)
        for cp in cps:
            cp.start()
        for i in range(self.n):
            src = x_refs[i] if self.mode == "gather" else x_refs[i].at[me]
            load = pltpu.make_async_copy(src, bufs[i], local_sems.at[2 * i])
            load.start()
            load.wait()
            pltpu.make_async_copy(bufs[i], o_refs[i].at[me], local_sems.at[2 * i + 1]).start()

    def wait(self, x_refs, o_refs, scratch):
        send_sems, recv_sems, local_sems, bufs = scratch[0], scratch[1], scratch[2], scratch[3:]
        cps, me = self._remote(x_refs, o_refs, send_sems, recv_sems)
        for cp in cps:
            cp.wait()
        for i in range(self.n):
            pltpu.make_async_copy(bufs[i], o_refs[i].at[me], local_sems.at[2 * i + 1]).wait()


def _with_riders(compute, riders, n_in, n_out, n_scratch, last_step):
    if riders is None:
        return compute
    n = riders.n

    def body(*refs):
        ins, x_refs = refs[:n_in], refs[n_in:n_in + n]
        outs, o_refs = refs[n_in + n:n_in + n + n_out], refs[n_in + n + n_out:n_in + 2 * n + n_out]
        scratch = refs[n_in + 2 * n + n_out:]
        own, ride = scratch[:n_scratch], scratch[n_scratch:]
        ids = [pl.program_id(i) for i in range(len(last_step))]
        first = functools.reduce(jnp.logical_and, [i == 0 for i in ids])
        last = functools.reduce(jnp.logical_and, [i == l for i, l in zip(ids, last_step)])

        @pl.when(first)
        def _():
            riders.start(x_refs, o_refs, ride)

        compute(*ins, *outs, *own)

        @pl.when(last)
        def _():
            riders.wait(x_refs, o_refs, ride)

    return body


def _attention_fwd(proj3, seq_blocks, riders=None):
    B, S, _ = proj3.shape
    scale = HEAD_DIM ** -0.5
    nq = ATT_WIDTH // ATT_GW

    def col(k):
        return pl.BlockSpec((1, S, ATT_GW), lambda b, g, k=k: (b, 0, k * nq + g))

    o_spec = pl.BlockSpec((1, S, ATT_GW), lambda b, g: (b, 0, g))
    l_spec = pl.BlockSpec((1, 1, S, ATT_BLOCK), lambda b, g: (b, g, 0, 0))

    def compute(q_ref, k_ref, v_ref, o_ref, lse_ref, qf, kf, vf, os, ls, bias):
        g = pl.program_id(1)
        for t in range(ATT_PAIRS):
            ts = slice(t * ATT_BLOCK, (t + 1) * ATT_BLOCK)
            qf[t] = q_ref[0, :, ts].astype(F32) * scale
            kf[t] = k_ref[0, :, ts].astype(F32)
            vf[t] = v_ref[0, :, ts].astype(F32)
        lane = lax.broadcasted_iota(jnp.int32, (ATT_BLOCK, ATT_BLOCK), 1)
        low = lane < HEAD_DIM

        def block(p, d, r, n, has_prev):
            start = n * (ATT_BLOCK * d) + r
            rows = _att_rows(start, d)
            prows = _att_rows(start - ATT_BLOCK * d, d) if has_prev else None
            lse_t = jnp.zeros((ATT_BLOCK, ATT_BLOCK), F32)
            for t in range(ATT_PAIRS):
                q2 = _stack_heads(qf[t, rows, :], low)
                k2 = kf[t, rows, :].astype(BF16)
                v2 = vf[t, rows, :].astype(BF16)
                if has_prev:
                    k2 = jnp.concatenate([k2, kf[t, prows, :].astype(BF16)], axis=0)
                    v2 = jnp.concatenate([v2, vf[t, prows, :].astype(BF16)], axis=0)
                    b2 = bias[t]
                else:
                    b2 = bias[t, :, 0:ATT_BLOCK]
                s = lax.dot_general(q2, k2, NT_DIMS, preferred_element_type=F32) + b2
                m = jnp.max(s, axis=1, keepdims=True)
                pr = jnp.exp(s - m)
                den = jnp.sum(pr, axis=1, keepdims=True)
                o = jnp.dot(pr.astype(BF16), v2, preferred_element_type=F32) * (1.0 / den)
                os[p, t, rows, :] = _unstack_heads(o, low)
                lse2 = m + jnp.log(den)
                lse_t = jnp.where(lane == 2 * t, lse2[0:ATT_BLOCK], lse_t)
                lse_t = jnp.where(lane == 2 * t + 1, lse2[ATT_BLOCK:], lse_t)
            ls[p, rows, :] = lse_t

        for p in range(N_PATTERNS):
            d = 4 ** p
            _att_fill_bias(bias, g, d)
            _att_one_pattern(block, p, d, seq_blocks // d)

        def combine(i, carry):
            rows = pl.ds(pl.multiple_of(i * ATT_BLOCK, ATT_BLOCK), ATT_BLOCK)
            l0, l1, l2 = ls[0, rows, :], ls[1, rows, :], ls[2, rows, :]
            m = jnp.maximum(jnp.maximum(l0, l1), l2)
            lse = m + jnp.log(jnp.exp(l0 - m) + jnp.exp(l1 - m) + jnp.exp(l2 - m))
            lse_ref[0, 0, rows, :] = lse
            w = [jnp.exp(l0 - lse), jnp.exp(l1 - lse), jnp.exp(l2 - lse)]
            for t in range(ATT_PAIRS):
                acc = jnp.zeros((ATT_BLOCK, ATT_BLOCK), F32)
                for p in range(N_PATTERNS):
                    wt = jnp.where(low, w[p][:, 2 * t:2 * t + 1], w[p][:, 2 * t + 1:2 * t + 2])
                    acc = acc + wt * os[p, t, rows, :]
                o_ref[0, rows, t * ATT_BLOCK:(t + 1) * ATT_BLOCK] = acc.astype(BF16)
            return carry

        lax.fori_loop(0, S // ATT_BLOCK, combine, 0, unroll=2)

    scratch = ([pltpu.VMEM((ATT_PAIRS, S, ATT_BLOCK), F32)] * 3
               + [pltpu.VMEM((N_PATTERNS, ATT_PAIRS, S, ATT_BLOCK), F32), pltpu.VMEM((N_PATTERNS, S, ATT_BLOCK), F32),
                  pltpu.VMEM((ATT_PAIRS, 2 * ATT_BLOCK, 2 * ATT_BLOCK), F32)])
    rs = riders
    res = _pcall(_with_riders(compute, rs, 3, 2, len(scratch), (B - 1, ATT_GROUPS - 1)), name="attention_fwd",
                 out_shape=(_sds((B, S, ATT_WIDTH), BF16), _sds((B, ATT_GROUPS, S, ATT_BLOCK), F32))
                 + (tuple(rs.out_shape) if rs else ()),
                 grid=(B, ATT_GROUPS), in_specs=[col(0), col(1), col(2)] + (rs.specs if rs else []),
                 out_specs=(o_spec, l_spec) + (tuple(rs.specs) if rs else ()),
                 scratch_shapes=scratch + (rs.scratch if rs else []),
                 dims=("arbitrary", "arbitrary"))(proj3, proj3, proj3, *(rs.arrs if rs else []))
    return res[0], res[1], list(res[2:])


def _att_one_pattern(block, p, d, nb):
    def per_residue(r, carry):
        block(p, d, r, 0, False)
        if nb > 1:
            def per_block(n, c2):
                block(p, d, r, n, True)
                return c2
            lax.fori_loop(1, nb, per_block, 0, unroll=ATT_UNROLL if (nb - 1) % ATT_UNROLL == 0 else nb - 1)
        return carry

    if d == 1:
        per_residue(0, 0)
    else:
        lax.fori_loop(0, d, per_residue, 0, unroll=ATT_RESIDUE_UNROLL if nb == 1 else 1)


def _attention_bwd(proj3, do3, o3, lse4, seq_blocks, riders=None):
    B, S, _ = proj3.shape
    scale = HEAD_DIM ** -0.5
    nq = ATT_WIDTH // ATT_GW

    def col(k):
        return pl.BlockSpec((1, S, ATT_GW), lambda b, g, k=k: (b, 0, k * nq + g))

    o_spec = pl.BlockSpec((1, S, ATT_GW), lambda b, g: (b, 0, g))
    l_spec = pl.BlockSpec((1, 1, S, ATT_BLOCK), lambda b, g: (b, g, 0, 0))

    def compute(q_ref, k_ref, v_ref, do_ref, o_ref, lse_ref, dq_ref, dk_ref, dv_ref,
                qf, kf, vf, dof, dl, aq, ak, av, bias):
        g = pl.program_id(1)
        for t in range(ATT_PAIRS):
            ts = slice(t * ATT_BLOCK, (t + 1) * ATT_BLOCK)
            qf[t] = q_ref[0, :, ts].astype(F32) * scale
            kf[t] = k_ref[0, :, ts].astype(F32)
            vf[t] = v_ref[0, :, ts].astype(F32)
            dof[t] = do_ref[0, :, ts].astype(F32)
        aq[...] = jnp.zeros_like(aq)
        ak[...] = jnp.zeros_like(ak)
        av[...] = jnp.zeros_like(av)
        lane = lax.broadcasted_iota(jnp.int32, (ATT_BLOCK, ATT_BLOCK), 1)
        low = lane < HEAD_DIM

        def fill_delta(i, carry):
            rows = pl.ds(pl.multiple_of(i * ATT_BLOCK, ATT_BLOCK), ATT_BLOCK)
            acc = jnp.zeros((ATT_BLOCK, ATT_BLOCK), F32)
            for t in range(ATT_PAIRS):
                prod = dof[t, rows, :] * o_ref[0, rows, t * ATT_BLOCK:(t + 1) * ATT_BLOCK].astype(F32)
                lo = jnp.sum(jnp.where(low, prod, 0.0), axis=1, keepdims=True)
                hi = jnp.sum(prod, axis=1, keepdims=True) - lo
                acc = jnp.where(lane == 2 * t, lo, acc)
                acc = jnp.where(lane == 2 * t + 1, hi, acc)
            dl[rows, :] = acc
            return carry

        lax.fori_loop(0, S // ATT_BLOCK, fill_delta, 0, unroll=2)

        def block(p, d, r, n, has_prev):
            start = n * (ATT_BLOCK * d) + r
            rows = _att_rows(start, d)
            prows = _att_rows(start - ATT_BLOCK * d, d) if has_prev else None
            lse_t = lse_ref[0, 0, rows, :]
            dl_t = dl[rows, :]
            for t in range(ATT_PAIRS):
                q2 = _stack_heads(qf[t, rows, :], low)
                do2 = _stack_heads(dof[t, rows, :], low)
                k2 = kf[t, rows, :].astype(BF16)
                v2 = vf[t, rows, :].astype(BF16)
                if has_prev:
                    k2 = jnp.concatenate([k2, kf[t, prows, :].astype(BF16)], axis=0)
                    v2 = jnp.concatenate([v2, vf[t, prows, :].astype(BF16)], axis=0)
                    b2 = bias[t]
                else:
                    b2 = bias[t, :, 0:ATT_BLOCK]
                lse2 = jnp.concatenate([lse_t[:, 2 * t:2 * t + 1], lse_t[:, 2 * t + 1:2 * t + 2]], axis=0)
                dl2 = jnp.concatenate([dl_t[:, 2 * t:2 * t + 1], dl_t[:, 2 * t + 1:2 * t + 2]], axis=0)
                s = lax.dot_general(q2, k2, NT_DIMS, preferred_element_type=F32) + b2
                pr = jnp.exp(s - lse2)
                ds = (pr * (lax.dot_general(do2, v2, NT_DIMS, preferred_element_type=F32) - dl2)).astype(BF16)
                dq = _unstack_heads(jnp.dot(ds, k2, preferred_element_type=F32), low)
                dk = lax.dot_general(ds, q2, TN_DIMS, preferred_element_type=F32)
                dv = lax.dot_general(pr.astype(BF16), do2, TN_DIMS, preferred_element_type=F32)
                aq[t, rows, :] = aq[t, rows, :] + dq * scale
                ak[t, rows, :] = ak[t, rows, :] + dk[0:ATT_BLOCK]
                av[t, rows, :] = av[t, rows, :] + dv[0:ATT_BLOCK]
                if has_prev:
                    ak[t, prows, :] = ak[t, prows, :] + dk[ATT_BLOCK:]
                    av[t, prows, :] = av[t, prows, :] + dv[ATT_BLOCK:]

        for p in range(N_PATTERNS):
            d = 4 ** p
            _att_fill_bias(bias, g, d)
            _att_one_pattern(block, p, d, seq_blocks // d)

        for t in range(ATT_PAIRS):
            ts = slice(t * ATT_BLOCK, (t + 1) * ATT_BLOCK)
            dq_ref[0, :, ts] = aq[t].astype(BF16)
            dk_ref[0, :, ts] = ak[t].astype(BF16)
            dv_ref[0, :, ts] = av[t].astype(BF16)

    shp = _sds((B, S, ATT_WIDTH), BF16)
    pair_buf = pltpu.VMEM((ATT_PAIRS, S, ATT_BLOCK), F32)
    scratch = ([pair_buf] * 4 + [pltpu.VMEM((S, ATT_BLOCK), F32)] + [pair_buf] * 3
               + [pltpu.VMEM((ATT_PAIRS, 2 * ATT_BLOCK, 2 * ATT_BLOCK), F32)])
    rs = riders
    res = _pcall(_with_riders(compute, rs, 6, 3, len(scratch), (B - 1, ATT_GROUPS - 1)), name="attention_bwd",
                 out_shape=(shp, shp, shp) + (tuple(rs.out_shape) if rs else ()), grid=(B, ATT_GROUPS),
                 in_specs=[col(0), col(1), col(2), o_spec, o_spec, l_spec] + (rs.specs if rs else []),
                 out_specs=(o_spec, o_spec, o_spec) + (tuple(rs.specs) if rs else ()),
                 scratch_shapes=scratch + (rs.scratch if rs else []),
                 dims=("arbitrary", "arbitrary"))(proj3, proj3, proj3, do3, o3, lse4, *(rs.arrs if rs else []))
    return res[0], res[1], res[2], list(res[3:])


def _expand_groups(m):
    rows = SSM_WIDTH
    t = jnp.concatenate([m] * SSM_GROUPS, axis=0)
    r = lax.broadcasted_iota(jnp.int32, (rows, SSM_LANES), 0)
    l = lax.broadcasted_iota(jnp.int32, (rows, SSM_LANES), 1)
    keep = lax.shift_right_logical(r, 4) == lax.shift_right_logical(l, 6)
    return jnp.where(keep, t, 0.0)


def _collapse_groups(m):
    rows = SSM_WIDTH
    r = lax.broadcasted_iota(jnp.int32, (rows, SSM_LANES), 0)
    l = lax.broadcasted_iota(jnp.int32, (rows, SSM_LANES), 1)
    keep = lax.shift_right_logical(r, 4) == lax.shift_right_logical(l, 6)
    t = jnp.where(keep, m, 0.0)
    acc = t[0:SSM_GROUP_CH]
    for g in range(1, SSM_GROUPS):
        acc = acc + t[g * SSM_GROUP_CH:(g + 1) * SSM_GROUP_CH]
    return acc


def _zoh(lr, li, ldt):
    dt = jnp.exp(ldt)
    mag = jnp.exp(lr * dt)
    ang = li * dt
    cs, sn = jnp.cos(ang), jnp.sin(ang)
    ab_re, ab_im = mag * cs, mag * sn
    nr, ni = ab_re - 1.0, ab_im
    den = lr * lr + li * li
    n_re = nr * lr + ni * li
    n_im = ni * lr - nr * li
    return dict(dt=dt, mag=mag, cs=cs, sn=sn, ab_re=ab_re, ab_im=ab_im, nr=nr, ni=ni, den=den, n_re=n_re, n_im=n_im,
                f_re=n_re / den, f_im=n_im / den)


def _ssm_params(lr, li, ldt, br, bi, cr, ci):
    def body(lr_ref, li_ref, ldt_ref, br_ref, bi_ref, cr_ref, ci_ref, ab_ref, w_ref, c_ref):
        z = _zoh(lr_ref[...], li_ref[...], ldt_ref[...])
        ab_ref[0:1, :] = z["ab_re"]
        ab_ref[1:2, :] = z["ab_im"]
        br, bi = br_ref[...], bi_ref[...]
        w_ref[:, 0:SSM_LANES] = _expand_groups(z["f_re"] * br - z["f_im"] * bi).astype(BF16)
        w_ref[:, SSM_LANES:] = _expand_groups(z["f_re"] * bi + z["f_im"] * br).astype(BF16)
        c_ref[:, 0:SSM_LANES] = _expand_groups(cr_ref[...]).astype(BF16)
        c_ref[:, SSM_LANES:] = _expand_groups(-ci_ref[...]).astype(BF16)

    return _pcall(body, name="ssm_params",
                  out_shape=(_sds((2, SSM_LANES), F32), _sds((SSM_WIDTH, 2 * SSM_LANES), BF16),
                             _sds((SSM_WIDTH, 2 * SSM_LANES), BF16)))(lr, li, ldt, br, bi, cr, ci)


def _ssm_params_bwd(lr, li, ldt, br, bi, dab, dw, dc):
    def body(lr_ref, li_ref, ldt_ref, br_ref, bi_ref, dab_ref, dw_ref, dc_ref,
             dlr_ref, dli_ref, dldt_ref, dbr_ref, dbi_ref, dcr_ref, dci_ref):
        lr, li = lr_ref[...], li_ref[...]
        z = _zoh(lr, li, ldt_ref[...])
        br, bi = br_ref[...], bi_ref[...]
        dbb_re = _collapse_groups(dw_ref[:, 0:SSM_LANES])
        dbb_im = _collapse_groups(dw_ref[:, SSM_LANES:])
        dcr_ref[...] = _collapse_groups(dc_ref[:, 0:SSM_LANES])
        dci_ref[...] = -_collapse_groups(dc_ref[:, SSM_LANES:])
        f_re, f_im = z["f_re"], z["f_im"]
        dbr_ref[...] = f_re * dbb_re + f_im * dbb_im
        dbi_ref[...] = f_re * dbb_im - f_im * dbb_re
        df_re = jnp.sum(dbb_re * br + dbb_im * bi, axis=0, keepdims=True)
        df_im = jnp.sum(dbb_im * br - dbb_re * bi, axis=0, keepdims=True)
        den = z["den"]
        dn_re, dn_im = df_re / den, df_im / den
        dden = -(df_re * z["n_re"] + df_im * z["n_im"]) / (den * den)
        dnr = dn_re * lr - dn_im * li
        dni = dn_re * li + dn_im * lr
        dlr = dn_re * z["nr"] + dn_im * z["ni"] + 2.0 * dden * lr
        dli = dn_re * z["ni"] - dn_im * z["nr"] + 2.0 * dden * li
        dab_re = dab_ref[0:1, :] + dnr
        dab_im = dab_ref[1:2, :] + dni
        mag, cs, sn, dt = z["mag"], z["cs"], z["sn"], z["dt"]
        dmag = dab_re * cs + dab_im * sn
        dang = mag * (dab_im * cs - dab_re * sn)
        dlr_ref[...] = dlr + dmag * mag * dt
        dli_ref[...] = dli + dang * dt
        ddt = dmag * mag * lr + dang * li
        per_lane = jnp.broadcast_to(ddt * dt, (8, SSM_LANES))
        lane = lax.broadcasted_iota(jnp.int32, (SSM_LANES, 128), 0)
        col = lax.broadcasted_iota(jnp.int32, (SSM_LANES, 128), 1)
        ind = jnp.where(lax.shift_right_logical(lane, 6) == col, 1.0, 0.0)
        dldt_ref[...] = jnp.dot(per_lane, ind, preferred_element_type=F32, precision=lax.Precision.HIGHEST)[0:1]

    vec = _sds((1, SSM_LANES), F32)
    mat = _sds((SSM_GROUP_CH, SSM_LANES), F32)
    return _pcall(body, name="ssm_params_bwd", out_shape=(vec, vec, _sds((1, 128), F32), mat, mat, mat, mat))(
        lr, li, ldt, br, bi, dab, dw, dc)


SCAN_CHUNK = 512


def _scan_consts(ar, ai, k_ref, reverse):
    row = lax.broadcasted_iota(jnp.int32, (8, SSM_LANES), 0)
    pw = [(ar, ai)]
    for _ in range(7):
        pr, pi = pw[-1]
        pw.append((pr * ar - pi * ai, pr * ai + pi * ar))
    for n, k in enumerate((1, 2, 4)):
        keep = (row < 8 - k) if reverse else (row >= k)
        k_ref[2 * n] = jnp.where(keep, jnp.broadcast_to(pw[k - 1][0], (8, SSM_LANES)), 0.0)
        k_ref[2 * n + 1] = jnp.where(keep, jnp.broadcast_to(pw[k - 1][1], (8, SSM_LANES)), 0.0)
    cr = jnp.zeros((8, SSM_LANES), F32)
    ci = jnp.zeros((8, SSM_LANES), F32)
    for r in range(8):
        e = (8 - r) if reverse else (r + 1)
        cr = jnp.where(row == r, jnp.broadcast_to(pw[e - 1][0], (8, SSM_LANES)), cr)
        ci = jnp.where(row == r, jnp.broadcast_to(pw[e - 1][1], (8, SSM_LANES)), ci)
    k_ref[6] = cr
    k_ref[7] = ci


def _scan_tile(xr, xi, k_ref, car, cai, reverse):
    for n, k in enumerate((1, 2, 4)):
        sh = (8 - k) if reverse else k
        sr = pltpu.roll(xr, sh, 0)
        si = pltpu.roll(xi, sh, 0)
        mr, mi = k_ref[2 * n], k_ref[2 * n + 1]
        xr, xi = xr + mr * sr - mi * si, xi + mr * si + mi * sr
    pr, pi = k_ref[6], k_ref[7]
    xr, xi = xr + pr * car - pi * cai, xi + pr * cai + pi * car
    return xr, xi


US_BLOCK = (3 * ATT_WIDTH) // SSM_WIDTH


def _ssm_scan_fwd(proj3, abar, w_bu, w_c):
    B, S, _ = proj3.shape
    ch = min(S, SCAN_CHUNK)
    u_spec = pl.BlockSpec((1, ch, SSM_WIDTH), lambda b, c: (b, c, US_BLOCK))
    x_spec = pl.BlockSpec((1, ch, 2 * SSM_LANES), lambda b, c: (b, c, 0))
    y_spec = pl.BlockSpec((1, ch, SSM_WIDTH), lambda b, c: (b, c, 0))
    w_spec = pl.BlockSpec((SSM_WIDTH, 2 * SSM_LANES), lambda b, c: (0, 0))

    def body(ab_ref, u_ref, wb_ref, wc_ref, x_ref, y_ref, k_ref, carry_ref):
        _scan_consts(ab_ref[0:1, :], ab_ref[1:2, :], k_ref, False)

        @pl.when(pl.program_id(1) == 0)
        def _():
            carry_ref[...] = jnp.zeros_like(carry_ref)

        x_ref[0] = jnp.dot(u_ref[0], wb_ref[...], preferred_element_type=F32)

        def step(i, carry):
            base = pl.multiple_of(i * 8, 8)
            xr = x_ref[0, pl.ds(base, 8), 0:SSM_LANES]
            xi = x_ref[0, pl.ds(base, 8), SSM_LANES:]
            xr, xi = _scan_tile(xr, xi, k_ref, carry[0], carry[1], False)
            x_ref[0, pl.ds(base, 8), 0:SSM_LANES] = xr
            x_ref[0, pl.ds(base, 8), SSM_LANES:] = xi
            return (jnp.broadcast_to(xr[7:8], (8, SSM_LANES)), jnp.broadcast_to(xi[7:8], (8, SSM_LANES)))

        cr, ci = lax.fori_loop(0, ch // 8, step, (carry_ref[0], carry_ref[1]))
        carry_ref[0] = cr
        carry_ref[1] = ci
        y_ref[0] = lax.dot_general(x_ref[0].astype(BF16), wc_ref[...], NT_DIMS, preferred_element_type=F32)

    return _pcall(body, name="ssm_scan_fwd",
                  out_shape=(_sds((B, S, 2 * SSM_LANES), F32), _sds((B, S, SSM_WIDTH), F32)), grid=(B, S // ch),
                  in_specs=[pl.BlockSpec((2, SSM_LANES), lambda b, c: (0, 0)), u_spec, w_spec, w_spec],
                  out_specs=(x_spec, y_spec),
                  scratch_shapes=[pltpu.VMEM((8, 8, SSM_LANES), F32), pltpu.VMEM((2, 8, SSM_LANES), F32)],
                  dims=("arbitrary", "arbitrary"))(abar, proj3, w_bu, w_c)


def _ssm_scan_bwd(proj3, dy3, xs3, abar, w_bu, w_c, dsk):
    B, S, _ = proj3.shape
    ch = min(S, SCAN_CHUNK)
    nc = S // ch
    u_spec = pl.BlockSpec((1, ch, SSM_WIDTH), lambda b, c: (b, nc - 1 - c, US_BLOCK))
    x_spec = pl.BlockSpec((1, ch, 2 * SSM_LANES), lambda b, c: (b, nc - 1 - c, 0))
    y_spec = pl.BlockSpec((1, ch, SSM_WIDTH), lambda b, c: (b, nc - 1 - c, 0))
    w_spec = pl.BlockSpec((SSM_WIDTH, 2 * SSM_LANES), lambda b, c: (0, 0))
    ab_spec = pl.BlockSpec((2, SSM_LANES), lambda b, c: (0, 0))
    d_spec = pl.BlockSpec((1, SSM_WIDTH), lambda b, c: (0, 0))

    def body(ab_ref, u_ref, dy_ref, xs_ref, wb_ref, wc_ref, d_ref, du_ref, da_ref, dwb_ref, dwc_ref,
             g_ref, k_ref, carry_ref, acc_ref):
        b, c = pl.program_id(0), pl.program_id(1)
        _scan_consts(ab_ref[0:1, :], -ab_ref[1:2, :], k_ref, True)
        row = lax.broadcasted_iota(jnp.int32, (8, SSM_LANES), 0)

        @pl.when(c == 0)
        def _():
            carry_ref[...] = jnp.zeros_like(carry_ref)

        @pl.when((c == 0) & (b == 0))
        def _():
            acc_ref[...] = jnp.zeros_like(acc_ref)
            dwb_ref[...] = jnp.zeros_like(dwb_ref)
            dwc_ref[...] = jnp.zeros_like(dwc_ref)

        dy = dy_ref[0]
        dyb = dy.astype(BF16)
        g_ref[...] = jnp.dot(dyb, wc_ref[...], preferred_element_type=F32)

        def step(i, carry):
            car, cai, ar_acc, ai_acc = carry
            base = pl.multiple_of((ch // 8 - 1 - i) * 8, 8)
            gr = g_ref[pl.ds(base, 8), 0:SSM_LANES]
            gi = g_ref[pl.ds(base, 8), SSM_LANES:]
            gr, gi = _scan_tile(gr, gi, k_ref, car, cai, True)
            g_ref[pl.ds(base, 8), 0:SSM_LANES] = gr
            g_ref[pl.ds(base, 8), SSM_LANES:] = gi
            nr = jnp.where(row == 7, car, pltpu.roll(gr, 7, 0))
            ni = jnp.where(row == 7, cai, pltpu.roll(gi, 7, 0))
            xr = xs_ref[0, pl.ds(base, 8), 0:SSM_LANES]
            xi = xs_ref[0, pl.ds(base, 8), SSM_LANES:]
            ar_acc = ar_acc + nr * xr + ni * xi
            ai_acc = ai_acc + ni * xr - nr * xi
            return (jnp.broadcast_to(gr[0:1], (8, SSM_LANES)), jnp.broadcast_to(gi[0:1], (8, SSM_LANES)), ar_acc, ai_acc)

        cr, ci, ar_acc, ai_acc = lax.fori_loop(0, ch // 8, step, (carry_ref[0], carry_ref[1], acc_ref[0], acc_ref[1]))
        carry_ref[0] = cr
        carry_ref[1] = ci
        acc_ref[0] = ar_acc
        acc_ref[1] = ai_acc
        da_ref[0:1, :] = jnp.sum(ar_acc, axis=0, keepdims=True)
        da_ref[1:2, :] = jnp.sum(ai_acc, axis=0, keepdims=True)

        gb = g_ref[...].astype(BF16)
        du = lax.dot_general(gb, wb_ref[...], NT_DIMS, preferred_element_type=F32) + d_ref[...] * dy
        du_ref[0] = du.astype(BF16)
        dwb_ref[...] += lax.dot_general(u_ref[0], gb, TN_DIMS, preferred_element_type=F32)
        dwc_ref[...] += lax.dot_general(dyb, xs_ref[0].astype(BF16), TN_DIMS, preferred_element_type=F32)

    mat = _sds((SSM_WIDTH, 2 * SSM_LANES), F32)
    return _pcall(body, name="ssm_scan_bwd",
                  out_shape=(_sds((B, S, SSM_WIDTH), BF16), _sds((2, SSM_LANES), F32), mat, mat), grid=(B, nc),
                  in_specs=[ab_spec, u_spec, y_spec, x_spec, w_spec, w_spec, d_spec],
                  out_specs=(y_spec, ab_spec, w_spec, w_spec),
                  scratch_shapes=[pltpu.VMEM((ch, 2 * SSM_LANES), F32), pltpu.VMEM((8, 8, SSM_LANES), F32),
                                  pltpu.VMEM((2, 8, SSM_LANES), F32), pltpu.VMEM((2, 8, SSM_LANES), F32)],
                  dims=("arbitrary", "arbitrary"))(abar, proj3, dy3, xs3, w_bu, w_c, dsk)


GELU_K = math.sqrt(2.0 / math.pi)
GELU_C = 0.044715


def _gelu_parts(y):
    t = jnp.tanh(GELU_K * (y + GELU_C * y * y * y))
    return 0.5 * y * (1.0 + t), t


def _ssm_post(yc, us, dsk, wglu, bglu):
    T, N = yc.shape
    tm = min(T, 1024)
    row = pl.BlockSpec((tm, N), lambda i: (i, 0))
    vec = pl.BlockSpec((1, N), lambda i: (0, 0))
    mat = pl.BlockSpec((N, N), lambda i: (0, 0))

    def body(yc_ref, us_ref, d_ref, w_ref, b_ref, y_ref, s_ref):
        y = yc_ref[...] + d_ref[...] * us_ref[...]
        y_ref[...] = y
        z, _ = _gelu_parts(y)
        gl = jnp.dot(z.astype(BF16), w_ref[...], preferred_element_type=F32) + b_ref[...]
        s_ref[...] = (z * _sig(gl)).astype(BF16)

    return _pcall(body, name="ssm_post", out_shape=(_sds((T, N), F32), _sds((T, N), BF16)), grid=(T // tm,),
                  in_specs=[row, row, vec, mat, vec], out_specs=(row, row), dims=("parallel",))(yc, us, dsk, wglu, bglu)


def _ssm_post_bwd(y5, us, ds, dsk, wglu, bglu):
    T, N = y5.shape
    tm = min(T, 1024)
    row = pl.BlockSpec((tm, N), lambda i: (i, 0))
    vec = pl.BlockSpec((1, N), lambda i: (0, 0))
    mat = pl.BlockSpec((N, N), lambda i: (0, 0))

    def body(y_ref, us_ref, ds_ref, d_ref, w_ref, b_ref, dy_ref, dd_ref, db_ref, dw_ref):
        @pl.when(pl.program_id(0) == 0)
        def _():
            dd_ref[...] = jnp.zeros_like(dd_ref)
            db_ref[...] = jnp.zeros_like(db_ref)
            dw_ref[...] = jnp.zeros_like(dw_ref)

        y = y_ref[...]
        z, t = _gelu_parts(y)
        zb = z.astype(BF16)
        gl = jnp.dot(zb, w_ref[...], preferred_element_type=F32) + b_ref[...]
        sg = _sig(gl)
        ds = ds_ref[...]
        dgl = ds * z * sg * (1.0 - sg)
        dglb = dgl.astype(BF16)
        dz = ds * sg + lax.dot_general(dglb, w_ref[...], (((1,), (1,)), ((), ())), preferred_element_type=F32)
        dgelu = 0.5 * (1.0 + t) + 0.5 * y * (1.0 - t * t) * GELU_K * (1.0 + 3.0 * GELU_C * y * y)
        dy = dz * dgelu
        dy_ref[...] = dy
        dd_ref[...] += jnp.sum(dy * us_ref[...], axis=0, keepdims=True)
        db_ref[...] += jnp.sum(dgl, axis=0, keepdims=True)
        dw_ref[...] += lax.dot_general(zb, dglb, (((0,), (0,)), ((), ())), preferred_element_type=F32)

    return _pcall(body, name="ssm_post_bwd",
                  out_shape=(_sds((T, N), F32), _sds((1, N), F32), _sds((1, N), F32), _sds((N, N), F32)),
                  grid=(T // tm,), in_specs=[row, row, row, vec, mat, vec], out_specs=(row, vec, vec, mat),
                  dims=("arbitrary",))(y5, us, ds, dsk, wglu, bglu)


GATE_TILE = 256
GATE_ATT_BLOCK0 = (3 * ATT_WIDTH + SSM_WIDTH) // GATE_TILE
GATE_SSM_BLOCK0 = (3 * ATT_WIDTH + SSM_WIDTH + D_MODEL) // GATE_TILE


def _merge(proj, y_att, y_ssm, b_gate):
    T = proj.shape[0]
    tm = min(T, 1024)
    nj = D_MODEL // GATE_TILE
    ga = pl.BlockSpec((tm, GATE_TILE), lambda i, j: (i, GATE_ATT_BLOCK0 + j))
    gs = pl.BlockSpec((tm, GATE_TILE), lambda i, j: (i, GATE_SSM_BLOCK0 + j))
    yy = pl.BlockSpec((tm, GATE_TILE), lambda i, j: (i, j))
    ba = pl.BlockSpec((1, GATE_TILE), lambda i, j: (0, j))
    bs = pl.BlockSpec((1, GATE_TILE), lambda i, j: (0, nj + j))

    def body(ga_ref, gs_ref, ya_ref, ys_ref, ba_ref, bs_ref, o_ref):
        o_ref[...] = (_sig(ga_ref[...] + ba_ref[...]) * ya_ref[...]
                      + _sig(gs_ref[...] + bs_ref[...]) * ys_ref[...]).astype(BF16)

    return _pcall(body, name="merge", out_shape=_sds((T, D_MODEL), BF16), grid=(T // tm, nj),
                  in_specs=[ga, gs, yy, yy, ba, bs], out_specs=yy, dims=("parallel", "parallel"))(
        proj, proj, y_att, y_ssm, b_gate, b_gate)


def _merge_bwd(proj, y_att, y_ssm, b_gate, dmerged):
    T = proj.shape[0]
    tm = min(T, 1024)
    nj = D_MODEL // GATE_TILE
    ga = pl.BlockSpec((tm, GATE_TILE), lambda j, i: (i, GATE_ATT_BLOCK0 + j))
    gs = pl.BlockSpec((tm, GATE_TILE), lambda j, i: (i, GATE_SSM_BLOCK0 + j))
    yy = pl.BlockSpec((tm, GATE_TILE), lambda j, i: (i, j))
    ba = pl.BlockSpec((1, GATE_TILE), lambda j, i: (0, j))
    bs = pl.BlockSpec((1, GATE_TILE), lambda j, i: (0, nj + j))

    def body(ga_ref, gs_ref, ya_ref, ys_ref, ba_ref, bs_ref, dm_ref, dya_ref, dys_ref, dga_ref, dgs_ref, dba_ref, dbs_ref):
        @pl.when(pl.program_id(1) == 0)
        def _():
            dba_ref[...] = jnp.zeros_like(dba_ref)
            dbs_ref[...] = jnp.zeros_like(dbs_ref)

        dm = dm_ref[...].astype(F32)
        sa = _sig(ga_ref[...] + ba_ref[...])
        ss = _sig(gs_ref[...] + bs_ref[...])
        dya_ref[...] = (dm * sa).astype(BF16)
        dys_ref[...] = (dm * ss).astype(BF16)
        dga = dm * ya_ref[...] * sa * (1.0 - sa)
        dgs = dm * ys_ref[...] * ss * (1.0 - ss)
        dga_ref[...] = dga.astype(BF16)
        dgs_ref[...] = dgs.astype(BF16)
        dba_ref[...] += jnp.sum(dga, axis=0, keepdims=True)
        dbs_ref[...] += jnp.sum(dgs, axis=0, keepdims=True)

    big = _sds((T, D_MODEL), BF16)
    vec = _sds((1, D_MODEL), F32)
    return _pcall(body, name="merge_bwd", out_shape=(big, big, big, big, vec, vec), grid=(nj, T // tm),
                  in_specs=[ga, gs, yy, yy, ba, bs, yy], out_specs=(yy, yy, yy, yy, ba, ba),
                  dims=("arbitrary", "arbitrary"))(proj, proj, y_att, y_ssm, b_gate, b_gate, dmerged)


CONV_TILE = 256


def _shift_rows(a, j, up=False):
    n = a.shape[0]
    r = pltpu.roll(a, n - j if up else j, 0)
    row = lax.broadcasted_iota(jnp.int32, (8, a.shape[1]), 0)
    if up:
        return jnp.concatenate([r[:n - 8], jnp.where(row < 8 - j, r[n - 8:], 0.0)], axis=0)
    return jnp.concatenate([jnp.where(row >= j, r[:8], 0.0), r[8:]], axis=0)


def _conv_pre(a, w_ref, b_ref):
    conv = b_ref[...] + w_ref[0:1, :] * a
    shifted = []
    for j in (1, 2):
        sh = _shift_rows(a, j)
        shifted.append(sh)
        conv = conv + w_ref[j:j + 1, :] * sh
    return conv, shifted


def _conv_act(up3, w_conv, b_conv):
    B, S, _ = up3.shape
    nj = D_FF // CONV_TILE
    a_spec = pl.BlockSpec((1, S, CONV_TILE), lambda b, j: (b, 0, j))
    v_spec = pl.BlockSpec((1, S, CONV_TILE), lambda b, j: (b, 0, nj + j))
    w_spec = pl.BlockSpec((3, CONV_TILE), lambda b, j: (0, j))
    b_spec = pl.BlockSpec((1, CONV_TILE), lambda b, j: (0, j))

    def body(a_ref, v_ref, w_ref, b_ref, o_ref):
        a = a_ref[0].astype(F32)
        conv, _ = _conv_pre(a, w_ref, b_ref)
        o_ref[0] = (conv * _sig(conv) * v_ref[0]).astype(BF16)

    return _pcall(body, name="conv_act", out_shape=_sds((B, S, D_FF), BF16), grid=(B, nj),
                  in_specs=[a_spec, v_spec, w_spec, b_spec], out_specs=a_spec, dims=("parallel", "parallel"))(
        up3, up3, w_conv, b_conv)


def _conv_bwd(up3, dact3, w_conv, b_conv):
    B, S, _ = up3.shape
    nj = D_FF // CONV_TILE
    a_spec = pl.BlockSpec((1, S, CONV_TILE), lambda j, b: (b, 0, j))
    v_spec = pl.BlockSpec((1, S, CONV_TILE), lambda j, b: (b, 0, nj + j))
    o_spec = pl.BlockSpec((2, 1, S, CONV_TILE), lambda j, b: (0, b, 0, j))
    w_spec = pl.BlockSpec((3, CONV_TILE), lambda j, b: (0, j))
    b_spec = pl.BlockSpec((1, CONV_TILE), lambda j, b: (0, j))

    def body(a_ref, v_ref, d_ref, w_ref, b_ref, dup_ref, dw_ref, db_ref):
        @pl.when(pl.program_id(1) == 0)
        def _():
            dw_ref[...] = jnp.zeros_like(dw_ref)
            db_ref[...] = jnp.zeros_like(db_ref)

        a = a_ref[0].astype(F32)
        d = d_ref[0].astype(F32)
        conv, shifted = _conv_pre(a, w_ref, b_ref)
        sg = _sig(conv)
        dup_ref[1, 0] = (d * conv * sg).astype(BF16)
        dconv = d * v_ref[0] * (sg * (1.0 + conv * (1.0 - sg)))
        da = w_ref[0:1, :] * dconv
        for j in (1, 2):
            da = da + w_ref[j:j + 1, :] * _shift_rows(dconv, j, up=True)
        dup_ref[0, 0] = da.astype(BF16)
        db_ref[...] += jnp.sum(dconv, axis=0, keepdims=True)
        dw_ref[0:1, :] += jnp.sum(dconv * a, axis=0, keepdims=True)
        dw_ref[1:2, :] += jnp.sum(dconv * shifted[0], axis=0, keepdims=True)
        dw_ref[2:3, :] += jnp.sum(dconv * shifted[1], axis=0, keepdims=True)

    return _pcall(body, name="conv_bwd",
                  out_shape=(_sds((2, B, S, D_FF), BF16), _sds((3, D_FF), F32), _sds((1, D_FF), F32)),
                  grid=(nj, B), in_specs=[a_spec, v_spec, a_spec, w_spec, b_spec],
                  out_specs=(o_spec, w_spec, b_spec), dims=("arbitrary", "arbitrary"))(up3, up3, dact3, w_conv, b_conv)


def _rows_tile(r, cap=640):
    for t in range(min(r, cap) - min(r, cap) % 8, 7, -8):
        if r % t == 0:
            return t
    return r


def _add2(a, b, out_dtype, name):
    R, N = a.shape
    tr = _rows_tile(R)
    spec = pl.BlockSpec((tr, N), lambda i: (i, 0))

    def body(a_ref, b_ref, o_ref):
        o_ref[...] = (a_ref[...] + b_ref[...]).astype(out_dtype)

    return _pcall(body, name=name, out_shape=_sds((R, N), out_dtype), grid=(R // tr,), in_specs=[spec, spec],
                  out_specs=spec, dims=("parallel",))(a, b)


def _sum_slots(q, name):
    n, R, N = q.shape
    tr = _rows_tile(R)

    def body(q_ref, o_ref):
        acc = q_ref[0].astype(F32)
        for s in range(1, n):
            acc = acc + q_ref[s].astype(F32)
        o_ref[...] = acc

    return _pcall(body, name=name, out_shape=_sds((R, N), F32), grid=(R // tr,),
                  in_specs=[pl.BlockSpec((n, tr, N), lambda i: (0, i, 0))], out_specs=pl.BlockSpec((tr, N), lambda i: (i, 0)),
                  dims=("parallel",))(q)


NATIVE = (("b_re", 16, 1024), ("b_im", 16, 1024), ("c_re", 16, 1024), ("c_im", 16, 1024), ("g_mix", 1, 1024),
          ("b_att", 1, 1024), ("b_ssm", 1, 1024), ("a_re", 1, 1024), ("a_im", 1, 1024), ("log_dt", 1, 128),
          ("d_skip", 1, 256), ("b_glu", 1, 256), ("g_ffn", 1, 1024), ("g_final", 1, 1024), ("b_conv", 1, 2048),
          ("w_conv", 3, 2048), ("loss", 1, 1))
N_MOD = 6
NATIVE_LATE = ("g_mix",)
MODS_LATE = (0, 1)


def _small_plan(late):
    pieces = [p for p in NATIVE if (p[0] in NATIVE_LATE) == late]
    mods = [k for k in range(N_MOD) if (k in MODS_LATE) == late]
    starts, r = {}, 0
    for name, rows, cols in pieces:
        starts[name] = r
        r += rows * (-(-cols // LANES))
    return pieces, mods, starts, -(-r // 8) * 8


def _pack_small(native, dmods, late):
    pieces, mods, starts, n_sum = _small_plan(late)
    B = dmods[mods[0]].shape[0]
    total = n_sum + 8 * len(mods)

    def body(*refs):
        xs, ms, o_ref = refs[:len(pieces)], refs[len(pieces):-1], refs[-1]
        o_ref[...] = jnp.zeros_like(o_ref)
        for (name, rows, cols), x_ref in zip(pieces, xs):
            chunks = -(-cols // LANES)
            if chunks == 1 and rows % 8 == 0:
                o_ref[starts[name]:starts[name] + rows, 0:cols] = x_ref[...]
                continue
            for i in range(rows):
                for q in range(chunks):
                    wd = min(LANES, cols - q * LANES)
                    r = starts[name] + i * chunks + q
                    o_ref[r:r + 1, 0:wd] = x_ref[i:i + 1, q * LANES:q * LANES + wd]
        for k, m_ref in enumerate(ms):
            for b in range(B):
                o_ref[n_sum + 8 * k + b:n_sum + 8 * k + b + 1, :] = m_ref[b]

    return _pcall(body, name="pack_small_late" if late else "pack_small_early", out_shape=_sds((total, LANES), F32))(
        *[native[n] for n, _, _ in pieces], *[dmods[k] for k in mods])


def _sum_unpack_small(gathered_early, gathered_late, B):
    plans = [_small_plan(False), _small_plan(True)]
    nd = gathered_early.shape[0]
    n_out = len(NATIVE)

    def body(*refs):
        g_refs, outs, dm_ref, accs = refs[0:2], refs[2:2 + n_out], refs[2 + n_out], refs[3 + n_out:]
        o = 0
        for g_ref, acc, (pieces, mods, starts, n_sum) in zip(g_refs, accs, plans):
            s = g_ref[0, 0:n_sum, :]
            for d in range(1, nd):
                s = s + g_ref[d, 0:n_sum, :]
            acc[...] = s
            for name, rows, cols in pieces:
                o_ref = outs[o]
                o += 1
                chunks = -(-cols // LANES)
                if chunks == 1 and rows % 8 == 0:
                    o_ref[...] = acc[starts[name]:starts[name] + rows, 0:cols]
                    continue
                for i in range(rows):
                    for q in range(chunks):
                        wd = min(LANES, cols - q * LANES)
                        r = starts[name] + i * chunks + q
                        o_ref[i:i + 1, q * LANES:q * LANES + wd] = acc[r:r + 1, 0:wd]
            for d in range(nd):
                for j, k in enumerate(mods):
                    dm_ref[d, :, k * D_MODEL:(k + 1) * D_MODEL] = g_ref[d, n_sum + 8 * j:n_sum + 8 * j + B, :]

    ordered = [p for pieces, _, _, _ in plans for p in pieces]
    out_shape = tuple(_sds((rows, cols), F32) for _, rows, cols in ordered) + (_sds((nd, B, N_MOD * D_MODEL), F32),)
    res = _pcall(body, name="sum_unpack_small", out_shape=out_shape,
                 scratch_shapes=[pltpu.VMEM((n_sum, LANES), F32) for _, _, _, n_sum in plans])(gathered_early, gathered_late)
    return {n: r for (n, _, _), r in zip(ordered, res[:-1])}, res[-1]


def _small_from_native(nat):
    lanes3 = lambda a: a.reshape(SSM_GROUP_CH, SSM_GROUPS, SSM_STATE)
    return dict(
        g_mix=nat["g_mix"].reshape(D_MODEL), b_gate=jnp.concatenate([nat["b_att"], nat["b_ssm"]], axis=1).reshape(2 * D_MODEL),
        a_re=nat["a_re"].reshape(SSM_GROUPS, SSM_STATE), a_im=nat["a_im"].reshape(SSM_GROUPS, SSM_STATE),
        log_dt=nat["log_dt"][0, :SSM_GROUPS], b_re=_groups_from_lanes(nat["b_re"]), b_im=_groups_from_lanes(nat["b_im"]),
        c_re=lanes3(nat["c_re"]).transpose(1, 0, 2), c_im=lanes3(nat["c_im"]).transpose(1, 0, 2),
        d_skip=nat["d_skip"].reshape(SSM_WIDTH), b_glu=nat["b_glu"].reshape(SSM_WIDTH), g_ffn=nat["g_ffn"].reshape(D_MODEL),
        w_conv=nat["w_conv"], b_conv=nat["b_conv"].reshape(D_FF), g_final=nat["g_final"].reshape(D_MODEL))


def _adamw_multi(params):
    n = len(params)
    bc1 = 1.0 - ADAM_B1 ** ADAM_STEP
    bc2 = 1.0 - ADAM_B2 ** ADAM_STEP

    def body(*refs):
        ins, outs = refs[:4 * n], refs[4 * n:]
        for i in range(n):
            w_ref, g_ref, m_ref, v_ref = ins[4 * i:4 * i + 4]
            d_ref, nm_ref, nv_ref = outs[3 * i:3 * i + 3]
            g = g_ref[...]
            m = ADAM_B1 * m_ref[...] + (1.0 - ADAM_B1) * g
            v = ADAM_B2 * v_ref[...] + (1.0 - ADAM_B2) * (g * g)
            nm_ref[...] = m
            nv_ref[...] = v
            d_ref[...] = -ADAM_LR * ((m / bc1) / (jnp.sqrt(v / bc2) + ADAM_EPS) + ADAM_WD * w_ref[...])

    flat = [a for p in params for a in p]
    out_shape = tuple(_sds(p[0].shape, F32) for p in params for _ in range(3))
    res = _pcall(body, name="adamw_small", out_shape=out_shape)(*flat)
    return [tuple(res[3 * i:3 * i + 3]) for i in range(n)]


def _adamw(w, g, m, v, name, g_other=None):
    R, N = w.shape
    tr = _rows_tile(R, 256)
    spec = pl.BlockSpec((tr, N), lambda i: (i, 0))
    bc1 = 1.0 - ADAM_B1 ** ADAM_STEP
    bc2 = 1.0 - ADAM_B2 ** ADAM_STEP
    two = g_other is not None

    def body(*refs):
        w_ref, g_ref, m_ref, v_ref = refs[:4]
        d_ref, nm_ref, nv_ref = refs[4 + two:7 + two]
        g = g_ref[...]
        if two:
            g = g + refs[4][...]
            refs[8][...] = g
        m = ADAM_B1 * m_ref[...] + (1.0 - ADAM_B1) * g
        v = ADAM_B2 * v_ref[...] + (1.0 - ADAM_B2) * (g * g)
        nm_ref[...] = m
        nv_ref[...] = v
        d_ref[...] = -ADAM_LR * ((m / bc1) / (jnp.sqrt(v / bc2) + ADAM_EPS) + ADAM_WD * w_ref[...])

    shp = _sds((R, N), F32)
    args = (w, g, m, v) + ((g_other,) if two else ())
    return _pcall(body, name=name, out_shape=(shp,) * (3 + two), grid=(R // tr,), in_specs=[spec] * len(args),
                  out_specs=(spec,) * (3 + two), dims=("parallel",))(*args)


_GROUP_MASKS = {
    "all": [(dx, dy, dc) for dx in (0, 1) for dy in (0, 1) for dc in (0, 1) if (dx, dy, dc) != (0, 0, 0)],
    "xy": [(1, 0, 0), (0, 1, 0), (1, 1, 0)],
    "c": [(0, 0, 1)],
}
_GROUP_SLOTS = {"all": 8, "xy": 4, "c": 2}


def _group_slot(group, x, y, c):
    return {"all": 4 * x + 2 * y + c, "xy": 2 * x + y, "c": c}[group]


def _flip(v, d):
    return 1 - v if d else v


def _exchange(arr, group, mode, name):
    return _exchange_list([arr], group, mode, name)[0]


def _exchange_list(arrs, group, mode, name):
    masks = _GROUP_MASKS[group]
    n = len(masks)
    na = len(arrs)
    assert mode in ("gather", "swap") and (mode == "gather" or group == "c")
    has_local = mode == "gather"
    out_shapes = [((_GROUP_SLOTS[group],) if has_local else ()) + arr.shape for arr in arrs]
    bounce = [pltpu.VMEM(arr.shape, arr.dtype) for arr in arrs] if has_local else []

    def body(*refs):
        x_refs, o_refs = refs[:na], refs[na:2 * na]
        send_sems, recv_sems = refs[2 * na], refs[2 * na + 1]
        x, y, c = lax.axis_index("x"), lax.axis_index("y"), lax.axis_index("c")
        me = _group_slot(group, x, y, c)
        if has_local:
            local_sems = refs[2 * na + 2]
            bufs = refs[2 * na + 3:]
            loads = []
            for i in range(na):
                loads.append(pltpu.make_async_copy(x_refs[i], bufs[i], local_sems.at[2 * i]))
                loads[-1].start()
        copies = []
        for i in range(na):
            x_ref, o_ref = x_refs[i], o_refs[i]
            for k, (dx, dy, dc) in enumerate(masks):
                px, py, pc = _flip(x, dx), _flip(y, dy), _flip(c, dc)
                src, dst = (x_ref, o_ref.at[me]) if has_local else (x_ref, o_ref)
                cp =pltpu.make_async_remote_copy(src_ref=src, dst_ref=dst, send_sem=send_sems.at[i * n + k],
                                                  recv_sem=recv_sems.at[i * n + k], device_id=(px, py, pc),
                                                  device_id_type=pl.DeviceIdType.MESH)
                cp.start()
                copies.append(cp)
        if has_local:
            stores = []
            for i in range(na):
                loads[i].wait()
                stores.append(pltpu.make_async_copy(bufs[i], o_refs[i].at[me], local_sems.at[2 * i + 1]))
                stores[-1].start()
        for cp in copies:
            cp.wait()
        if has_local:
            for st in stores:
                st.wait()

    anyspec = pl.BlockSpec(memory_space=pl.ANY)
    scratch = [pltpu.SemaphoreType.DMA((n * na,)), pltpu.SemaphoreType.DMA((n * na,))]
    if has_local:
        scratch += [pltpu.SemaphoreType.DMA((2 * na,))] + bounce
    outs = pl.pallas_call(body, name=name, out_shape=tuple(_sds(s, a.dtype) for s, a in zip(out_shapes, arrs)),
                          in_specs=[anyspec] * na, out_specs=tuple([anyspec] * na), scratch_shapes=scratch,
                          compiler_params=pltpu.CompilerParams(vmem_limit_bytes=V7X_VMEM_LIMIT_BYTES))(*arrs)
    return list(outs)


def _gather_weights(shards, name):
    na = len(shards)
    masks = _GROUP_MASKS["xy"]
    n = len(masks)

    def body(*refs):
        x_refs, o_refs = refs[:na], refs[na:2 * na]
        send_sems, recv_sems, local_sems = refs[2 * na:2 * na + 3]
        bufs = refs[2 * na + 3:]
        x, y, c = lax.axis_index("x"), lax.axis_index("y"), lax.axis_index("c")
        me = 2 * x + y
        sibling = (x, y, 1 - c)
        loads = []
        for i in range(na):
            loads.append(pltpu.make_async_copy(x_refs[i], bufs[i], local_sems.at[2 * i]))
            loads[-1].start()

        def half_of(i, slot, cc):
            h = shards[i].shape[0] // 2
            return o_refs[i].at[slot, pl.ds(pl.multiple_of(cc * h, 8), h), :]

        def src_half(i, cc):
            h = shards[i].shape[0] // 2
            return x_refs[i].at[pl.ds(pl.multiple_of(cc * h, 8), h), :]

        sends = []
        for i in range(na):
            for k, (dx, dy, _) in enumerate(masks):
                cp = pltpu.make_async_remote_copy(src_ref=src_half(i, c), dst_ref=half_of(i, me, c),
                                                  send_sem=send_sems.at[i * 2 * n + k], recv_sem=recv_sems.at[i * 2 * n + k],
                                                  device_id=(_flip(x, dx), _flip(y, dy), c),
                                                  device_id_type=pl.DeviceIdType.MESH)
                cp.start()
                sends.append(cp)
        stores = []
        for i in range(na):
            loads[i].wait()
            stores.append(pltpu.make_async_copy(bufs[i], o_refs[i].at[me], local_sems.at[2 * i + 1]))
            stores[-1].start()
        for i in range(na):
            for k, (dx, dy, _) in enumerate(masks):
                slot = 2 * _flip(x, dx) + _flip(y, dy)
                landed = pltpu.make_async_remote_copy(src_ref=src_half(i, c), dst_ref=half_of(i, slot, c),
                                                      send_sem=send_sems.at[i * 2 * n + k],
                                                      recv_sem=recv_sems.at[i * 2 * n + k], device_id=sibling,
                                                      device_id_type=pl.DeviceIdType.MESH)
                landed.wait_recv()
                fwd = pltpu.make_async_remote_copy(src_ref=half_of(i, slot, c), dst_ref=half_of(i, slot, c),
                                                   send_sem=send_sems.at[i * 2 * n + n + k],
                                                   recv_sem=recv_sems.at[i * 2 * n + n + k], device_id=sibling,
                                                   device_id_type=pl.DeviceIdType.MESH)
                fwd.start()
                sends.append(fwd)
        for i in range(na):
            for k, (dx, dy, _) in enumerate(masks):
                slot = 2 * _flip(x, dx) + _flip(y, dy)
                pltpu.make_async_remote_copy(src_ref=half_of(i, slot, 1 - c), dst_ref=half_of(i, slot, 1 - c),
                                             send_sem=send_sems.at[i * 2 * n + n + k],
                                             recv_sem=recv_sems.at[i * 2 * n + n + k], device_id=sibling,
                                             device_id_type=pl.DeviceIdType.MESH).wait_recv()
        for cp in sends:
            cp.wait_send()
        for st in stores:
            st.wait()

    anyspec = pl.BlockSpec(memory_space=pl.ANY)
    scratch = [pltpu.SemaphoreType.DMA((2 * n * na,)), pltpu.SemaphoreType.DMA((2 * n * na,)),
               pltpu.SemaphoreType.DMA((2 * na,))] + [pltpu.VMEM(s.shape, s.dtype) for s in shards]
    outs = pl.pallas_call(body, name=name, out_shape=tuple(_sds((N_XY,) + s.shape, s.dtype) for s in shards),
                          in_specs=[anyspec] * na, out_specs=tuple([anyspec] * na), scratch_shapes=scratch,
                          compiler_params=pltpu.CompilerParams(vmem_limit_bytes=V7X_VMEM_LIMIT_BYTES))(*shards)
    return list(outs)


BIG = (("w_proj_att", (ATT_WIDTH, D_MODEL), 1), ("w_proj_ssm", (SSM_WIDTH, D_MODEL), 1),
       ("w_glu", (SSM_WIDTH, SSM_WIDTH), 0))
DIRECT = (("w_in", True), ("w_up", True), ("w_down", False), ("w_out", False))
N_XY = 4


def _big_rows(shape):
    return shape[0] * shape[1] // N_XY // LANES


FLAT_ROWS = sum(_big_rows(s) for _, s, _ in BIG)


def _shard_shape(shape, axis):
    return (shape[0] // N_XY, shape[1]) if axis == 0 else (shape[0], shape[1] // N_XY)


def _flatten_shards(shards):
    return jnp.concatenate([shards[n].reshape(_big_rows(s), LANES) for n, s, _ in BIG], axis=0)


def _unflatten_shard(flat):
    out, r = {}, 0
    for n, s, ax in BIG:
        k = _big_rows(s)
        out[n] = flat[r:r + k].reshape(_shard_shape(s, ax))
        r += k
    return out


def _unflatten_full(flat4):
    out, r = {}, 0
    for n, s, ax in BIG:
        k = _big_rows(s)
        sh = _shard_shape(s, ax)
        t = flat4[:, r:r + k].reshape((N_XY,) + sh)
        out[n] = t.reshape(s) if ax == 0 else t.transpose(1, 0, 2).reshape(s)
        r += k
    return out


def _flatten_full(full):
    parts = []
    for n, s, ax in BIG:
        sh = _shard_shape(s, ax)
        t = full[n]
        t = t.reshape((N_XY,) + sh) if ax == 0 else t.reshape(s[0], N_XY, sh[1]).transpose(1, 0, 2)
        parts.append(t.reshape(N_XY, _big_rows(s), LANES))
    return jnp.concatenate(parts, axis=1)


def _lanes_from_groups(a):
    return a.transpose(2, 0, 1).reshape(SSM_GROUP_CH, SSM_LANES)


def _groups_from_lanes(a):
    return a.reshape(SSM_GROUP_CH, SSM_GROUPS, SSM_STATE).transpose(1, 2, 0)


LATE = ("w_up_t", "w_down", "w_out")
EARLY_GRADS = ("w_up_t", "w_down", "w_out")


def _local_step(x3, mod, tgt3, W, P, late_shards=None, scatter_grads=False):
    B, S, _ = x3.shape
    T = B * S
    seq_blocks = S // ATT_BLOCK
    sh1, sc1, gt1, sh2, sc2, gt2 = [m.reshape(B, 1, D_MODEL) for m in jnp.split(mod, 6, axis=-1)]
    g_mix, g_ffn, g_final = P["g_mix"].reshape(1, D_MODEL), P["g_ffn"].reshape(1, D_MODEL), P["g_final"].reshape(1, D_MODEL)
    b_gate = P["b_gate"].reshape(1, 2 * D_MODEL)
    d_skip, b_glu = P["d_skip"].reshape(1, SSM_WIDTH), P["b_glu"].reshape(1, SSM_WIDTH)
    w_conv, b_conv = P["w_conv"], P["b_conv"].reshape(1, D_FF)

    u1 = _norm_mod(x3, g_mix, sc1, sh1).reshape(T, D_MODEL)
    proj = _mm(u1, W["w_in_t"], tb=True, name="mm_proj", out_dtype=BF16)
    proj3 = proj.reshape(B, S, IN_WIDTH)
    us = proj[:, 3 * ATT_WIDTH:3 * ATT_WIDTH + SSM_WIDTH]
    o_att3, lse4, late = _attention_fwd(proj3, seq_blocks, _Riders(late_shards, "gather") if late_shards else None)
    if late_shards:
        W = dict(W, **{n: f.reshape(-1, LANES) for n, f in zip(LATE, late)})
        w_conv = late[len(LATE)].transpose(1, 0, 2).reshape(3, D_FF)
        W.update(_unflatten_full(late[len(LATE) + 1]))
    o_att = o_att3.reshape(T, ATT_WIDTH)
    y_att = _mm(o_att, W["w_proj_att"], name="mm_proj_att", out_dtype=BF16)

    lr = P["a_re"].reshape(1, SSM_LANES)
    li = P["a_im"].reshape(1, SSM_LANES)
    ldt = jnp.repeat(P["log_dt"], SSM_STATE).reshape(1, SSM_LANES)
    br, bi = _lanes_from_groups(P["b_re"]), _lanes_from_groups(P["b_im"])
    cr = P["c_re"].transpose(1, 0, 2).reshape(SSM_GROUP_CH, SSM_LANES)
    ci = P["c_im"].transpose(1, 0, 2).reshape(SSM_GROUP_CH, SSM_LANES)
    abar, w_bu, w_c = _ssm_params(lr, li, ldt, br, bi, cr, ci)
    xs3, y_core3 = _ssm_scan_fwd(proj3, abar, w_bu, w_c)
    y5, s_out = _ssm_post(y_core3.reshape(T, SSM_WIDTH), us, d_skip, W["w_glu"], b_glu)
    y_ssm = _mm(s_out, W["w_proj_ssm"], name="mm_proj_ssm", out_dtype=BF16)

    merged = _merge(proj, y_att, y_ssm, b_gate)
    mix = _mm(merged, W["w_out"], name="mm_out", out_dtype=BF16)
    mix3 = mix.reshape(B, S, D_MODEL)

    h1, u2 = _resid_norm_mod(x3, mix3, gt1, g_ffn, sc2, sh2)
    u2 = u2.reshape(T, D_MODEL)
    up3 = _mm(u2, W["w_up_t"], tb=True, name="mm_up", out_dtype=BF16).reshape(B, S, 2 * D_FF)
    act = _conv_act(up3, w_conv, b_conv).reshape(T, D_FF)
    ffn3 = _mm(act, W["w_down"], name="mm_down", out_dtype=BF16).reshape(B, S, D_MODEL)
    dh2, dffn, dgt2, dg_final, loss = _final_loss(h1, ffn3, tgt3, gt2, g_final)

    dffn = dffn.reshape(T, D_MODEL)
    gw = {}
    gw["w_down"] = _mm(act, dffn, ta=True, out_dtype=BF16, name="mm_dw_down")
    dact3 = _mm(dffn, W["w_down"], tb=True, name="mm_dact", out_dtype=BF16).reshape(B, S, D_FF)
    dup3, dw_conv, db_conv = _conv_bwd(up3, dact3, w_conv, b_conv)
    dup = dup3.reshape(2, T, D_FF)
    gw["w_up_t"] = _mm(dup, u2, ta=True, out_dtype=BF16, name="mm_dw_up")
    du2 = _mm(dup, W["w_up_t"], name="mm_du2", out_dtype=BF16).reshape(B, S, D_MODEL)
    dh1, dsh2, dsc2, dg_ffn, dgt1, dmix = _norm_bwd(h1, du2, dh2, g_ffn, sc2, "norm_bwd2", mix3=mix3, gt=gt1)

    dmix = dmix.reshape(T, D_MODEL)
    gw["w_out"] = _mm(merged, dmix, ta=True, out_dtype=BF16, name="mm_dw_out")
    dmerged = _mm(dmix, W["w_out"], tb=True, name="mm_dmerged", out_dtype=BF16)
    dy_att, dy_ssm, dga, dgs, db_att, db_ssm = _merge_bwd(proj, y_att, y_ssm, b_gate, dmerged)

    gw["w_proj_ssm"] = _mm(s_out, dy_ssm, ta=True, name="mm_dw_proj_ssm")
    ds_out = _mm(dy_ssm, W["w_proj_ssm"], tb=True, name="mm_ds_out")
    dy5, dd_skip, db_glu, dw_glu = _ssm_post_bwd(y5, us, ds_out, d_skip, W["w_glu"], b_glu)
    gw["w_glu"] = dw_glu
    dus3, dab, dwbu, dwc = _ssm_scan_bwd(proj3, dy5.reshape(B, S, SSM_WIDTH), xs3, abar, w_bu, w_c, d_skip)
    dus = dus3.reshape(T, SSM_WIDTH)
    dlr, dli, dldt, dbr, dbi, dcr, dci = _ssm_params_bwd(lr, li, ldt, br, bi, dab, dwbu, dwc)

    gw["w_proj_att"] = _mm(o_att, dy_att, ta=True, name="mm_dw_proj_att")
    do_att = _mm(dy_att, W["w_proj_att"], tb=True, out_dtype=BF16, name="mm_do_att")
    early = [gw[n].reshape(N_XY, -1, LANES) for n in EARLY_GRADS]
    early.append(_flatten_full({n: gw[n].astype(BF16) for n, _, _ in BIG}))
    dq3, dk3, dv3, parts = _attention_bwd(proj3, do_att.reshape(B, S, ATT_WIDTH), o_att3, lse4, seq_blocks,
                                          _Riders(early, "scatter") if scatter_grads else None)
    dproj = jnp.concatenate([t.reshape(T, ATT_WIDTH) for t in (dq3, dk3, dv3)] + [dus, dga, dgs], axis=1)
    dmods = [None, None, dgt1, dsh2, dsc2, dgt2]
    native = dict(b_att=db_att, b_ssm=db_ssm, a_re=dlr, a_im=dli, log_dt=dldt, b_re=dbr, b_im=dbi, c_re=dcr, c_im=dci,
                  d_skip=dd_skip, b_glu=db_glu, g_ffn=dg_ffn, w_conv=dw_conv, b_conv=db_conv, g_final=dg_final, loss=loss)
    small_early = _pack_small(native, dmods, False)
    if scatter_grads:
        gw["w_in_t"], (small_early,) = _mm(dproj, u1, ta=True, out_dtype=BF16, name="mm_dw_in",
                                           riders=_Riders([small_early], "gather", "all"))
        du1, last_parts = _mm(dproj, W["w_in_t"], name="mm_du1", out_dtype=BF16,
                              riders=_Riders([gw["w_in_t"].reshape(N_XY, -1, LANES)], "scatter"))
        parts = parts + last_parts
    else:
        gw["w_in_t"] = _mm(dproj, u1, ta=True, out_dtype=BF16, name="mm_dw_in")
        du1 = _mm(dproj, W["w_in_t"], name="mm_du1", out_dtype=BF16)
    du1 = du1.reshape(B, S, D_MODEL)
    dx, dsh1, dsc1, dg_mix = _norm_bwd(x3, du1, dh1, g_mix, sc1, "norm_bwd1")
    dmods[0], dmods[1] = dsh1, dsc1
    native["g_mix"] = dg_mix
    return loss, dx, dmods, gw, native, parts, small_early


WEIGHTS = ['w_ada', 'b_ada', 'g_mix', 'w_in', 'b_gate', 'a_re', 'a_im', 'log_dt', 'b_re', 'b_im', 'c_re', 'c_im', 'd_skip',
           'w_glu', 'b_glu', 'w_proj_att', 'w_proj_ssm', 'w_out', 'g_ffn', 'w_up', 'w_conv', 'b_conv', 'w_down', 'g_final']
SMALL = ['g_mix', 'b_gate', 'a_re', 'a_im', 'log_dt', 'b_re', 'b_im', 'c_re', 'c_im', 'd_skip', 'b_glu', 'g_ffn', 'w_conv',
         'b_conv', 'g_final']


def kernel(x, c, w_ada, b_ada, g_mix, w_in, b_gate, a_re, a_im, log_dt, b_re, b_im, c_re, c_im, d_skip, w_glu, b_glu, w_proj_att, w_proj_ssm, w_out, g_ffn, w_up, w_conv, b_conv, w_down, g_final, loss_target, m_w_ada, m_b_ada, m_g_mix, m_w_in, m_b_gate, m_a_re, m_a_im, m_log_dt, m_b_re, m_b_im, m_c_re, m_c_im, m_d_skip, m_w_glu, m_b_glu, m_w_proj_att, m_w_proj_ssm, m_w_out, m_g_ffn, m_w_up, m_w_conv, m_b_conv, m_w_down, m_g_final, v_w_ada, v_b_ada, v_g_mix, v_w_in, v_b_gate, v_a_re, v_a_im, v_log_dt, v_b_re, v_b_im, v_c_re, v_c_im, v_d_skip, v_w_glu, v_b_glu, v_w_proj_att, v_w_proj_ssm, v_w_out, v_g_ffn, v_w_up, v_w_conv, v_b_conv, v_w_down, v_g_final):
    args = dict(locals())
    w = {n: args[n] for n in WEIGHTS}
    m = {n: args["m_" + n] for n in WEIGHTS}
    v = {n: args["v_" + n] for n in WEIGHTS}
    B, S, _ = x.shape
    ix, iy, ic = lax.axis_index("x"), lax.axis_index("y"), lax.axis_index("c")
    chip = 2 * ix + iy
    ada_cols = w_ada.shape[2]

    c_all = _exchange(c, "all", "gather", "gather_c").reshape(8 * B, D_MODEL)
    b_cols = lax.dynamic_slice_in_dim(b_ada, chip * ada_cols, ada_cols, axis=1)
    mod_cols = _ada_fwd(c_all, w_ada[0], b_cols)
    mod_all = _exchange(mod_cols, "xy", "gather", "gather_mod")
    mod_all = mod_all.transpose(1, 0, 2).reshape(8 * B, 6 * D_MODEL)
    mod = lax.dynamic_slice_in_dim(mod_all, (4 * ix + 2 * iy + ic) * B, B, axis=0)

    shard = {n + ("_t" if t else ""): (w[n][0].T if t else w[n][0]).astype(BF16) for n, t in DIRECT}
    misc = _flatten_shards({n: w[n][0] for n, _, _ in BIG}).astype(BF16)
    (w_in_full,) = _gather_weights([shard["w_in_t"]], "gather_weights")
    W = {"w_in_t": w_in_full.reshape(-1, LANES)}

    P = {n: w[n][0] for n in SMALL if n not in ("w_conv", "g_final")}
    P["w_conv"] = None
    P["g_final"] = g_final

    loss, dx, dmods, gw, native, parts, small_early = _local_step(x, mod, loss_target, W, P,
                                                                  [shard[n] for n in LATE] + [w_conv[0], misc], True)

    small_late = _exchange(_pack_small(native, dmods, True), "all", "gather", "gather_small")
    native_sum, dmod_all = _sum_unpack_small(small_early, small_late, B)
    loss = native_sum["loss"][0, 0]
    g_small = _small_from_native(native_sum)
    dmod_all = dmod_all.reshape(8 * B, N_MOD * D_MODEL)
    dmod_cols = lax.dynamic_slice_in_dim(dmod_all, chip * ada_cols, ada_cols, axis=1)
    g_w_ada, g_b_ada = _ada_bwd(c_all, dmod_all, dmod_cols)

    red = [_sum_slots(p, "sum_chips_%d" % i) for i, p in enumerate(parts)]
    red_sib = _exchange_list(red, "c", "swap", "share_cores")
    order = list(EARLY_GRADS) + ["misc", "w_in_t"]
    halves = dict(zip(order, zip(red, red_sib)))

    grads = {"w_ada": g_w_ada[None], "b_ada": g_b_ada}
    grads["w_up"] = _add2(*halves["w_up_t"], F32, "add_cores_w_up").T[None]
    for k, gk in _unflatten_shard(_add2(*halves["misc"], F32, "add_cores_misc")).items():
        grads[k] = gk[None]
    wc_cols = w_conv.shape[2]
    for n in SMALL:
        g = g_small[n]
        if n == "w_conv":
            g = lax.dynamic_slice_in_dim(g, chip * wc_cols, wc_cols, axis=1)
        grads[n] = g.reshape(w[n].shape)

    delta, new_m, new_v = {}, {}, {}
    for n in ["w_ada"] + [b for b, _ in DIRECT] + [b for b, _, _ in BIG]:
        shp = w[n].shape
        if n == "w_in":
            r, s = halves["w_in_t"]
            d2, m2, v2, g2 = _adamw(w[n][0].T, r, m[n][0].T, v[n][0].T, "adamw_" + n, g_other=s)
            d2, m2, v2, grads[n] = d2.T, m2.T, v2.T, g2.T[None]
        elif n in ("w_down", "w_out"):
            r, s = halves[n]
            d2, m2, v2, g2 = _adamw(w[n][0], r, m[n][0], v[n][0], "adamw_" + n, g_other=s)
            grads[n] = g2[None]
        else:
            d2, m2, v2 = _adamw(w[n][0], grads[n][0], m[n][0], v[n][0], "adamw_" + n)
        delta[n], new_m[n], new_v[n] = d2.reshape(shp), m2.reshape(shp), v2.reshape(shp)
    rest = ["b_ada"] + SMALL

    def drop(a):
        return a.reshape(1, -1) if a.ndim == 1 else (a if a.ndim == 2 else a[0])

    upd = _adamw_multi([(drop(w[n]), drop(grads[n]), drop(m[n]), drop(v[n])) for n in rest])
    for n, (dd, mm, vv) in zip(rest, upd):
        delta[n], new_m[n], new_v[n] = dd.reshape(w[n].shape), mm.reshape(w[n].shape), vv.reshape(w[n].shape)

    return (loss, dx, *[grads[n] for n in WEIGHTS], *[delta[n] for n in WEIGHTS], *[new_m[n] for n in WEIGHTS],
            *[new_v[n] for n in WEIGHTS])
```

```python
import functools
import math

import jax
import jax.numpy as jnp
from jax import lax
from jax.experimental import pallas as pl
from jax.experimental.pallas import tpu as pltpu

F32, BF16 = jnp.float32, jnp.bfloat16

D_MODEL = 1024
N_HEADS = 8
HEAD_DIM = 64
ATT_WIDTH = 512
SSM_GROUPS = 16
SSM_GROUP_CH = 16
SSM_WIDTH = 256
SSM_STATE = 64
SSM_LANES = SSM_GROUPS * SSM_STATE
D_FF = 2048
IN_WIDTH = 3 * ATT_WIDTH + SSM_WIDTH + 2 * D_MODEL
ATT_BLOCK = 128
N_PATTERNS = 3
EPS = 1e-6
NEG_INF = -1e30

ADAM_LR, ADAM_B1, ADAM_B2, ADAM_EPS, ADAM_WD, ADAM_STEP = 0.001, 0.9, 0.999, 1e-08, 0.01, 10

V7X_VMEM_LIMIT_BYTES = 56 * 1024 * 1024
LANES = 1024


def _pcall(body, *, name, out_shape, grid=(), in_specs=None, out_specs=None, scratch_shapes=(), dims=None):
    params = dict(vmem_limit_bytes=V7X_VMEM_LIMIT_BYTES)
    if dims is not None:
        params["dimension_semantics"] = dims
    specs = {}
    if in_specs is not None:
        specs = dict(grid=grid, in_specs=in_specs, out_specs=out_specs)
    return pl.pallas_call(body, name=name, out_shape=out_shape, scratch_shapes=scratch_shapes,
                          compiler_params=pltpu.CompilerParams(**params), **specs)


def _sds(shape, dtype):
    return jax.ShapeDtypeStruct(tuple(shape), dtype)


def _tile(n, target):
    if n <= target:
        return n
    for t in range(target - target % 128, 0, -128):
        if n % t == 0:
            return t
    raise ValueError((n, target))


def _sig(v):
    return pl.reciprocal(1.0 + jnp.exp(-v), approx=True)


def _mm(a, b, *, name, ta=False, tb=False, out_dtype=F32, tm=2048, tn=1024, tk=1024, riders=None):
    halves = a.ndim == 3
    if halves:
        a_rows, a_cols = a.shape[1], 2 * a.shape[2]
    else:
        a_rows, a_cols = a.shape
    if ta:
        K, M = a_rows, a_cols
    else:
        M, K = a_rows, a_cols
    if tb:
        N, K2 = b.shape
    else:
        K2, N = b.shape
    assert K == K2, (a.shape, b.shape)
    if halves:
        tm, tk = (min(tm, M // 2), tk) if ta else (tm, min(tk, K // 2))
    tm, tn, tk = _tile(M, tm), _tile(N, tn), _tile(K, tk)
    nk = K // tk
    if halves and ta:
        per = a.shape[2] // tm
        a_spec = pl.BlockSpec((None, tk, tm), lambda i, j, k: (i // per, k, i % per))
    elif halves:
        per = a.shape[2] // tk
        a_spec = pl.BlockSpec((None, tm, tk), lambda i, j, k: (k // per, i, k % per))
    else:
        a_spec = pl.BlockSpec((tk, tm), lambda i, j, k: (k, i)) if ta else pl.BlockSpec((tm, tk), lambda i, j, k: (i, k))
    b_spec = pl.BlockSpec((tn, tk), lambda i, j, k: (j, k)) if tb else pl.BlockSpec((tk, tn), lambda i, j, k: (k, j))
    dn = (((0 if ta else 1,), (1 if tb else 0,)), ((), ()))

    def body(a_ref, b_ref, o_ref, acc_ref):
        k = pl.program_id(2)

        @pl.when(k == 0)
        def _():
            acc_ref[...] = jnp.zeros_like(acc_ref)

        acc_ref[...] += lax.dot_general(a_ref[...].astype(BF16), b_ref[...].astype(BF16), dn,
                                        preferred_element_type=F32)

        @pl.when(k == nk - 1)
        def _():
            o_ref[...] = acc_ref[...].astype(out_dtype)

    def body_single(a_ref, b_ref, o_ref):
        o_ref[...] = lax.dot_general(a_ref[...].astype(BF16), b_ref[...].astype(BF16), dn,
                                     preferred_element_type=F32).astype(out_dtype)

    grid = (M // tm, N // tn, nk)
    scratch = [] if nk == 1 else [pltpu.VMEM((tm, tn), F32)]
    o_spec = pl.BlockSpec((tm, tn), lambda i, j, k: (i, j))
    if riders is None:
        return _pcall(body_single if nk == 1 else body, name=name, out_shape=_sds((M, N), out_dtype), grid=grid,
                      in_specs=[a_spec, b_spec], out_specs=o_spec, scratch_shapes=scratch,
                      dims=("parallel", "parallel", "arbitrary"))(a, b)
    rs = riders
    res = _pcall(_with_riders(body_single if nk == 1 else body, rs, 2, 1, len(scratch), tuple(g - 1 for g in grid)),
                 name=name, out_shape=(_sds((M, N), out_dtype),) + tuple(rs.out_shape), grid=grid,
                 in_specs=[a_spec, b_spec] + rs.specs, out_specs=(o_spec,) + tuple(rs.specs),
                 scratch_shapes=scratch + rs.scratch, dims=("arbitrary", "arbitrary", "arbitrary"))(a, b, *rs.arrs)
    return res[0], list(res[1:])


def _ada_fwd(c_all, w_ada, b_ada_cols):
    n = w_ada.shape[1]

    def body(c_ref, w_ref, b_ref, o_ref):
        c = c_ref[...]
        act = c * _sig(c)
        o_ref[...] = jnp.dot(act.astype(BF16), w_ref[...].astype(BF16), preferred_element_type=F32) + b_ref[...]

    return _pcall(body, name="ada_fwd", out_shape=_sds((c_all.shape[0], n), F32))(c_all, w_ada, b_ada_cols)


def _ada_bwd(c_all, dmod_all, dmod_cols):
    n = dmod_cols.shape[1]

    def body(c_ref, da_ref, dc_ref, gw_ref, gb_ref):
        c = c_ref[...]
        act = c * _sig(c)
        gw_ref[...] = lax.dot_general(act, dc_ref[...], (((0,), (0,)), ((), ())), preferred_element_type=F32,
                                      precision=lax.Precision.HIGHEST)
        gb_ref[...] = jnp.sum(da_ref[...], axis=0, keepdims=True)

    return _pcall(body, name="ada_bwd", out_shape=(_sds((D_MODEL, n), F32), _sds((1, dmod_all.shape[1]), F32)))(
        c_all, dmod_all, dmod_cols)


ROW_TILE = 512


def _row_specs(B, S):
    ts = min(S, ROW_TILE)
    row = pl.BlockSpec((1, ts, D_MODEL), lambda b, s: (b, s, 0))
    bvec = pl.BlockSpec((1, 1, D_MODEL), lambda b, s: (b, 0, 0))
    gvec = pl.BlockSpec((1, D_MODEL), lambda b, s: (0, 0))
    return ts, row, bvec, gvec


def _norm_mod(x3, g, sc, sh):
    B, S, _ = x3.shape
    ts, row, bvec, gvec = _row_specs(B, S)

    def body(x_ref, g_ref, sc_ref, sh_ref, u_ref):
        x = x_ref[0]
        r = lax.rsqrt(jnp.mean(x * x, axis=-1, keepdims=True) + EPS)
        u_ref[0] = ((x * r) * g_ref[...] * (1.0 + sc_ref[0]) + sh_ref[0]).astype(BF16)

    return _pcall(body, name="norm_mod1", out_shape=_sds(x3.shape, BF16), grid=(B, S // ts),
                  in_specs=[row, gvec, bvec, bvec], out_specs=row, dims=("parallel", "parallel"))(x3, g, sc, sh)


def _resid_norm_mod(x3, mix3, gt, g, sc, sh):
    B, S, _ = x3.shape
    ts, row, bvec, gvec = _row_specs(B, S)

    def body(x_ref, m_ref, gt_ref, g_ref, sc_ref, sh_ref, h_ref, u_ref):
        h = x_ref[0] + gt_ref[0] * m_ref[0]
        h_ref[0] = h
        r = lax.rsqrt(jnp.mean(h * h, axis=-1, keepdims=True) + EPS)
        u_ref[0] = ((h * r) * g_ref[...] * (1.0 + sc_ref[0]) + sh_ref[0]).astype(BF16)

    return _pcall(body, name="resid_norm_mod2", out_shape=(_sds(x3.shape, F32), _sds(x3.shape, BF16)),
                  grid=(B, S // ts), in_specs=[row, row, bvec, gvec, bvec, bvec], out_specs=(row, row),
                  dims=("parallel", "parallel"))(x3, mix3, gt, g, sc, sh)


def _norm_bwd(h3, du3, dres3, g, sc, name, mix3=None, gt=None):
    B, S, _ = h3.shape
    ts, row, bvec, gvec = _row_specs(B, S)
    with_gate = mix3 is not None

    def body(*refs):
        if with_gate:
            h_ref, du_ref, dr_ref, g_ref, sc_ref, m_ref, gt_ref, dh_ref, dsh_ref, dsc_ref, dg_ref, dgt_ref, dm_ref = refs
        else:
            h_ref, du_ref, dr_ref, g_ref, sc_ref, dh_ref, dsh_ref, dsc_ref, dg_ref = refs
        b, s = pl.program_id(0), pl.program_id(1)
        h = h_ref[0]
        r = lax.rsqrt(jnp.mean(h * h, axis=-1, keepdims=True) + EPS)
        xn = h * r
        du = du_ref[0].astype(F32)
        g = g_ref[...]
        sc1 = 1.0 + sc_ref[0]
        dxn = du * g * sc1
        dh = dr_ref[0].astype(F32) + r * (dxn - xn * jnp.mean(dxn * xn, axis=-1, keepdims=True))
        dh_ref[0] = dh.astype(dh_ref.dtype)

        @pl.when(s == 0)
        def _():
            dsh_ref[...] = jnp.zeros_like(dsh_ref)
            dsc_ref[...] = jnp.zeros_like(dsc_ref)
            if with_gate:
                dgt_ref[...] = jnp.zeros_like(dgt_ref)

        @pl.when((s == 0) & (b == 0))
        def _():
            dg_ref[...] = jnp.zeros_like(dg_ref)

        dux = du * xn
        dsh_ref[0] += jnp.sum(du, axis=0, keepdims=True)
        dsc_ref[0] += jnp.sum(dux * g, axis=0, keepdims=True)
        dg_ref[...] += jnp.sum(dux * sc1, axis=0, keepdims=True)
        if with_gate:
            dgt_ref[0] += jnp.sum(dh * m_ref[0], axis=0, keepdims=True)
            dm_ref[0] = (dh * gt_ref[0]).astype(BF16)

    bshape = _sds((B, 1, D_MODEL), F32)
    in_specs = [row, row, row, gvec, bvec]
    out_shape = [_sds(h3.shape, BF16 if with_gate else F32), bshape, bshape, _sds((1, D_MODEL), F32)]
    out_specs = [row, bvec, bvec, gvec]
    args = [h3, du3, dres3, g, sc]
    if with_gate:
        in_specs += [row, bvec]
        out_shape += [bshape, _sds(h3.shape, BF16)]
        out_specs += [bvec, row]
        args += [mix3, gt]
    return _pcall(body, name=name, out_shape=tuple(out_shape), grid=(B, S // ts), in_specs=in_specs,
                  out_specs=tuple(out_specs), dims=("arbitrary", "arbitrary"))(*args)


def _final_loss(h1, ffn3, tgt3, gt, gfin):
    B, S, _ = h1.shape
    ts, row, bvec, gvec = _row_specs(B, S)
    one = pl.BlockSpec((1, 1), lambda b, s: (0, 0))

    def body(h_ref, f_ref, t_ref, gt_ref, gf_ref, dh_ref, dff_ref, dgt_ref, dgf_ref, loss_ref):
        b, s = pl.program_id(0), pl.program_id(1)
        f = f_ref[0].astype(F32)
        gtv = gt_ref[0]
        gf = gf_ref[...]
        h2 = h_ref[0] + gtv * f
        r = lax.rsqrt(jnp.mean(h2 * h2, axis=-1, keepdims=True) + EPS)
        n = h2 * r
        e = n * gf - t_ref[0]
        dy = e * (1.0 / D_MODEL)
        dn = dy * gf
        dh2 = r * (dn - n * jnp.mean(dn * n, axis=-1, keepdims=True))
        dh_ref[0] = dh2.astype(BF16)
        dff_ref[0] = (dh2 * gtv).astype(BF16)

        @pl.when(s == 0)
        def _():
            dgt_ref[...] = jnp.zeros_like(dgt_ref)

        @pl.when((s == 0) & (b == 0))
        def _():
            dgf_ref[...] = jnp.zeros_like(dgf_ref)
            loss_ref[...] = jnp.zeros_like(loss_ref)

        dgt_ref[0] += jnp.sum(dh2 * f, axis=0, keepdims=True)
        dgf_ref[...] += jnp.sum(dy * n, axis=0, keepdims=True)
        rows = jnp.sum(e * e, axis=1, keepdims=True)
        loss_ref[...] += jnp.sum(rows, axis=0, keepdims=True) * (0.5 / D_MODEL)

    return _pcall(body, name="final_loss",
                  out_shape=(_sds(h1.shape, BF16), _sds(h1.shape, BF16), _sds((B, 1, D_MODEL), F32),
                             _sds((1, D_MODEL), F32), _sds((1, 1), F32)),
                  grid=(B, S // ts), in_specs=[row, row, row, bvec, gvec], out_specs=(row, row, bvec, gvec, one),
                  dims=("arbitrary", "arbitrary"))(h1, ffn3, tgt3, gt, gfin)


ATT_GROUP = 4
ATT_GW = ATT_GROUP * HEAD_DIM
ATT_GROUPS = N_HEADS // ATT_GROUP
ATT_PAIRS = ATT_GW // ATT_BLOCK
ATT_UNROLL = 5
ATT_RESIDUE_UNROLL = 4
NT_DIMS = (((1,), (1,)), ((), ()))
TN_DIMS = (((0,), (0,)), ((), ()))


def _att_rows(start, d):
    if d == 1:
        return pl.ds(start if isinstance(start, int) else pl.multiple_of(start, ATT_BLOCK), ATT_BLOCK)
    return pl.ds(start, ATT_BLOCK, stride=d)


def _att_fill_bias(bias_ref, g, d):
    a = lax.broadcasted_iota(jnp.int32, (ATT_BLOCK, ATT_BLOCK), 0)
    j = lax.broadcasted_iota(jnp.int32, (ATT_BLOCK, ATT_BLOCK), 1)
    dist = (a - j).astype(F32)
    for hh in range(ATT_GROUP):
        t, e = divmod(hh, 2)
        rs = slice(e * ATT_BLOCK, (e + 1) * ATT_BLOCK)
        lo = 2.0 ** (-8.0 * (hh + 1) / N_HEADS) * d
        hi = 2.0 ** (-8.0 * (ATT_GROUP + hh + 1) / N_HEADS) * d
        slope = jnp.where(g == 0, lo, hi).astype(F32)
        bias_ref[t, rs, 0:ATT_BLOCK] = jnp.where(a >= j, -slope * dist, NEG_INF)
        bias_ref[t, rs, ATT_BLOCK:] = jnp.where(j >= a, -slope * (dist + float(ATT_BLOCK)), NEG_INF)


def _stack_heads(v2, low):
    return jnp.concatenate([jnp.where(low, v2, 0.0), jnp.where(low, 0.0, v2)], axis=0).astype(BF16)


def _unstack_heads(r2, low):
    return jnp.where(low, r2[0:ATT_BLOCK], r2[ATT_BLOCK:])


class _Riders:
    def __init__(self, arrs, mode, group="xy"):
        self.arrs, self.mode, self.n, self.group = list(arrs), mode, len(arrs), group
        slot_shapes = [a.shape if mode == "gather" else a.shape[1:] for a in self.arrs]
        self.out_shape = [_sds((_GROUP_SLOTS[group],) + s, a.dtype) for s, a in zip(slot_shapes, self.arrs)]
        k = len(_GROUP_MASKS[group])
        self.scratch = [pltpu.SemaphoreType.DMA((k * self.n,)), pltpu.SemaphoreType.DMA((k * self.n,)),
                        pltpu.SemaphoreType.DMA((2 * self.n,))] + [pltpu.VMEM(s, a.dtype) for s, a in zip(slot_shapes, self.arrs)]
        self.specs = [pl.BlockSpec(memory_space=pl.ANY)] * self.n

    def _remote(self, x_refs, o_refs, send_sems, recv_sems):
        x, y, c = lax.axis_index("x"), lax.axis_index("y"), lax.axis_index("c")
        me = _group_slot(self.group, x, y, c)
        masks = _GROUP_MASKS[self.group]
        cps = []
        for i in range(self.n):
            for k, (dx, dy, dc) in enumerate(masks):
                px, py, pc = _flip(x, dx), _flip(y, dy), _flip(c, dc)
                src = x_refs[i] if self.mode == "gather" else x_refs[i].at[_group_slot(self.group, px, py, pc)]
                cps.append(pltpu.make_async_remote_copy(
                    src_ref=src, dst_ref=o_refs[i].at[me], send_sem=send_sems.at[len(masks) * i + k],
                    recv_sem=recv_sems.at[len(masks) * i + k], device_id=(px, py, pc),
                    device_id_type=pl.DeviceIdType.MESH))
        return cps, me

    def start(self, x_refs, o_refs, scratch):
        send_sems, recv_sems, local_sems, bufs = scratch[0], scratch[1], scratch[2], scratch[3:]
        cps, me = self._remote(x_refs, o_refs, send_sems, recv_sems)
        for cp in cps:
            cp.start()
        for i in range(self.n):
            src = x_refs[i] if self.mode == "gather" else x_refs[i].at[me]
            load = pltpu.make_async_copy(src, bufs[i], local_sems.at[2 * i])
            load.start()
            load.wait()
            pltpu.make_async_copy(bufs[i], o_refs[i].at[me], local_sems.at[2 * i + 1]).start()

    def wait(self, x_refs, o_refs, scratch):
        send_sems, recv_sems, local_sems, bufs = scratch[0], scratch[1], scratch[2], scratch[3:]
        cps, me = self._remote(x_refs, o_refs, send_sems, recv_sems)
        for cp in cps:
            cp.wait()
        for i in range(self.n):
            pltpu.make_async_copy(bufs[i], o_refs[i].at[me], local_sems.at[2 * i + 1]).wait()


def _with_riders(compute, riders, n_in, n_out, n_scratch, last_step):
    if riders is None:
        return compute
    n = riders.n

    def body(*refs):
        ins, x_refs = refs[:n_in], refs[n_in:n_in + n]
        outs, o_refs = refs[n_in + n:n_in + n + n_out], refs[n_in + n + n_out:n_in + 2 * n + n_out]
        scratch = refs[n_in + 2 * n + n_out:]
        own, ride = scratch[:n_scratch], scratch[n_scratch:]
        ids = [pl.program_id(i) for i in range(len(last_step))]
        first = functools.reduce(jnp.logical_and, [i == 0 for i in ids])
        last = functools.reduce(jnp.logical_and, [i == l for i, l in zip(ids, last_step)])

        @pl.when(first)
        def _():
            riders.start(x_refs, o_refs, ride)

        compute(*ins, *outs, *own)

        @pl.when(last)
        def _():
            riders.wait(x_refs, o_refs, ride)

    return body


def _attention_fwd(proj3, seq_blocks, riders=None):
    B, S, _ = proj3.shape
    scale = HEAD_DIM ** -0.5
    nq = ATT_WIDTH // ATT_GW

    def col(k):
        return pl.BlockSpec((1, S, ATT_GW), lambda b, g, k=k: (b, 0, k * nq + g))

    o_spec = pl.BlockSpec((1, S, ATT_GW), lambda b, g: (b, 0, g))
    l_spec = pl.BlockSpec((1, 1, S, ATT_BLOCK), lambda b, g: (b, g, 0, 0))

    def compute(q_ref, k_ref, v_ref, o_ref, lse_ref, qf, kf, vf, os, ls, bias):
        g = pl.program_id(1)
        for t in range(ATT_PAIRS):
            ts = slice(t * ATT_BLOCK, (t + 1) * ATT_BLOCK)
            qf[t] = q_ref[0, :, ts].astype(F32) * scale
            kf[t] = k_ref[0, :, ts].astype(F32)
            vf[t] = v_ref[0, :, ts].astype(F32)
        lane = lax.broadcasted_iota(jnp.int32, (ATT_BLOCK, ATT_BLOCK), 1)
        low = lane < HEAD_DIM

        def block(p, d, r, n, has_prev):
            start = n * (ATT_BLOCK * d) + r
            rows = _att_rows(start, d)
            prows = _att_rows(start - ATT_BLOCK * d, d) if has_prev else None
            lse_t = jnp.zeros((ATT_BLOCK, ATT_BLOCK), F32)
            for t in range(ATT_PAIRS):
                q2 = _stack_heads(qf[t, rows, :], low)
                k2 = kf[t, rows, :].astype(BF16)
                v2 = vf[t, rows, :].astype(BF16)
                if has_prev:
                    k2 = jnp.concatenate([k2, kf[t, prows, :].astype(BF16)], axis=0)
                    v2 = jnp.concatenate([v2, vf[t, prows, :].astype(BF16)], axis=0)
                    b2 = bias[t]
                else:
                    b2 = bias[t, :, 0:ATT_BLOCK]
                s = lax.dot_general(q2, k2, NT_DIMS, preferred_element_type=F32) + b2
                m = jnp.max(s, axis=1, keepdims=True)
                pr = jnp.exp(s - m)
                den = jnp.sum(pr, axis=1, keepdims=True)
                o = jnp.dot(pr.astype(BF16), v2, preferred_element_type=F32) * (1.0 / den)
                os[p, t, rows, :] = _unstack_heads(o, low)
                lse2 = m + jnp.log(den)
                lse_t = jnp.where(lane == 2 * t, lse2[0:ATT_BLOCK], lse_t)
                lse_t = jnp.where(lane == 2 * t + 1, lse2[ATT_BLOCK:], lse_t)
            ls[p, rows, :] = lse_t

        for p in range(N_PATTERNS):
            d = 4 ** p
            _att_fill_bias(bias, g, d)
            _att_one_pattern(block, p, d, seq_blocks // d)

        def combine(i, carry):
            rows = pl.ds(pl.multiple_of(i * ATT_BLOCK, ATT_BLOCK), ATT_BLOCK)
            l0, l1, l2 = ls[0, rows, :], ls[1, rows, :], ls[2, rows, :]
            m = jnp.maximum(jnp.maximum(l0, l1), l2)
            lse = m + jnp.log(jnp.exp(l0 - m) + jnp.exp(l1 - m) + jnp.exp(l2 - m))
            lse_ref[0, 0, rows, :] = lse
            w = [jnp.exp(l0 - lse), jnp.exp(l1 - lse), jnp.exp(l2 - lse)]
            for t in range(ATT_PAIRS):
                acc = jnp.zeros((ATT_BLOCK, ATT_BLOCK), F32)
                for p in range(N_PATTERNS):
                    wt = jnp.where(low, w[p][:, 2 * t:2 * t + 1], w[p][:, 2 * t + 1:2 * t + 2])
                    acc = acc + wt * os[p, t, rows, :]
                o_ref[0, rows, t * ATT_BLOCK:(t + 1) * ATT_BLOCK] = acc.astype(BF16)
            return carry

        lax.fori_loop(0, S // ATT_BLOCK, combine, 0, unroll=2)

    scratch = ([pltpu.VMEM((ATT_PAIRS, S, ATT_BLOCK), F32)] * 3
               + [pltpu.VMEM((N_PATTERNS, ATT_PAIRS, S, ATT_BLOCK), F32), pltpu.VMEM((N_PATTERNS, S, ATT_BLOCK), F32),
                  pltpu.VMEM((ATT_PAIRS, 2 * ATT_BLOCK, 2 * ATT_BLOCK), F32)])
    rs = riders
    res = _pcall(_with_riders(compute, rs, 3, 2, len(scratch), (B - 1, ATT_GROUPS - 1)), name="attention_fwd",
                 out_shape=(_sds((B, S, ATT_WIDTH), BF16), _sds((B, ATT_GROUPS, S, ATT_BLOCK), F32))
                 + (tuple(rs.out_shape) if rs else ()),
                 grid=(B, ATT_GROUPS), in_specs=[col(0), col(1), col(2)] + (rs.specs if rs else []),
                 out_specs=(o_spec, l_spec) + (tuple(rs.specs) if rs else ()),
                 scratch_shapes=scratch + (rs.scratch if rs else []),
                 dims=("arbitrary", "arbitrary"))(proj3, proj3, proj3, *(rs.arrs if rs else []))
    return res[0], res[1], list(res[2:])


def _att_one_pattern(block, p, d, nb):
    def per_residue(r, carry):
        block(p, d, r, 0, False)
        if nb > 1:
            def per_block(n, c2):
                block(p, d, r, n, True)
                return c2
            lax.fori_loop(1, nb, per_block, 0, unroll=ATT_UNROLL if (nb - 1) % ATT_UNROLL == 0 else nb - 1)
        return carry

    if d == 1:
        per_residue(0, 0)
    else:
        lax.fori_loop(0, d, per_residue, 0, unroll=ATT_RESIDUE_UNROLL if nb == 1 else 1)


def _attention_bwd(proj3, do3, o3, lse4, seq_blocks, riders=None):
    B, S, _ = proj3.shape
    scale = HEAD_DIM ** -0.5
    nq = ATT_WIDTH // ATT_GW

    def col(k):
        return pl.BlockSpec((1, S, ATT_GW), lambda b, g, k=k: (b, 0, k * nq + g))

    o_spec = pl.BlockSpec((1, S, ATT_GW), lambda b, g: (b, 0, g))
    l_spec = pl.BlockSpec((1, 1, S, ATT_BLOCK), lambda b, g: (b, g, 0, 0))

    def compute(q_ref, k_ref, v_ref, do_ref, o_ref, lse_ref, dq_ref, dk_ref, dv_ref,
                qf, kf, vf, dof, dl, aq, ak, av, bias):
        g = pl.program_id(1)
        for t in range(ATT_PAIRS):
            ts = slice(t * ATT_BLOCK, (t + 1) * ATT_BLOCK)
            qf[t] = q_ref[0, :, ts].astype(F32) * scale
            kf[t] = k_ref[0, :, ts].astype(F32)
            vf[t] = v_ref[0, :, ts].astype(F32)
            dof[t] = do_ref[0, :, ts].astype(F32)
        aq[...] = jnp.zeros_like(aq)
        ak[...] = jnp.zeros_like(ak)
        av[...] = jnp.zeros_like(av)
        lane = lax.broadcasted_iota(jnp.int32, (ATT_BLOCK, ATT_BLOCK), 1)
        low = lane < HEAD_DIM

        def fill_delta(i, carry):
            rows = pl.ds(pl.multiple_of(i * ATT_BLOCK, ATT_BLOCK), ATT_BLOCK)
            acc = jnp.zeros((ATT_BLOCK, ATT_BLOCK), F32)
            for t in range(ATT_PAIRS):
                prod = dof[t, rows, :] * o_ref[0, rows, t * ATT_BLOCK:(t + 1) * ATT_BLOCK].astype(F32)
                lo = jnp.sum(jnp.where(low, prod, 0.0), axis=1, keepdims=True)
                hi = jnp.sum(prod, axis=1, keepdims=True) - lo
                acc = jnp.where(lane == 2 * t, lo, acc)
                acc = jnp.where(lane == 2 * t + 1, hi, acc)
            dl[rows, :] = acc
            return carry

        lax.fori_loop(0, S // ATT_BLOCK, fill_delta, 0, unroll=2)

        def block(p, d, r, n, has_prev):
            start = n * (ATT_BLOCK * d) + r
            rows = _att_rows(start, d)
            prows = _att_rows(start - ATT_BLOCK * d, d) if has_prev else None
            lse_t = lse_ref[0, 0, rows, :]
            dl_t = dl[rows, :]
            for t in range(ATT_PAIRS):
                q2 = _stack_heads(qf[t, rows, :], low)
                do2 = _stack_heads(dof[t, rows, :], low)
                k2 = kf[t, rows, :].astype(BF16)
                v2 = vf[t, rows, :].astype(BF16)
                if has_prev:
                    k2 = jnp.concatenate([k2, kf[t, prows, :].astype(BF16)], axis=0)
                    v2 = jnp.concatenate([v2, vf[t, prows, :].astype(BF16)], axis=0)
                    b2 = bias[t]
                else:
                    b2 = bias[t, :, 0:ATT_BLOCK]
                lse2 = jnp.concatenate([lse_t[:, 2 * t:2 * t + 1], lse_t[:, 2 * t + 1:2 * t + 2]], axis=0)
                dl2 = jnp.concatenate([dl_t[:, 2 * t:2 * t + 1], dl_t[:, 2 * t + 1:2 * t + 2]], axis=0)
                s = lax.dot_general(q2, k2, NT_DIMS, preferred_element_type=F32) + b2
                pr = jnp.exp(s - lse2)
                ds = (pr * (lax.dot_general(do2, v2, NT_DIMS, preferred_element_type=F32) - dl2)).astype(BF16)
                dq = _unstack_heads(jnp.dot(ds, k2, preferred_element_type=F32), low)
                dk = lax.dot_general(ds, q2, TN_DIMS, preferred_element_type=F32)
                dv = lax.dot_general(pr.astype(BF16), do2, TN_DIMS, preferred_element_type=F32)
                aq[t, rows, :] = aq[t, rows, :] + dq * scale
                ak[t, rows, :] = ak[t, rows, :] + dk[0:ATT_BLOCK]
                av[t, rows, :] = av[t, rows, :] + dv[0:ATT_BLOCK]
                if has_prev:
                    ak[t, prows, :] = ak[t, prows, :] + dk[ATT_BLOCK:]
                    av[t, prows, :] = av[t, prows, :] + dv[ATT_BLOCK:]

        for p in range(N_PATTERNS):
            d = 4 ** p
            _att_fill_bias(bias, g, d)
            _att_one_pattern(block, p, d, seq_blocks // d)

        for t in range(ATT_PAIRS):
            ts = slice(t * ATT_BLOCK, (t + 1) * ATT_BLOCK)
            dq_ref[0, :, ts] = aq[t].astype(BF16)
            dk_ref[0, :, ts] = ak[t].astype(BF16)
            dv_ref[0, :, ts] = av[t].astype(BF16)

    shp = _sds((B, S, ATT_WIDTH), BF16)
    pair_buf = pltpu.VMEM((ATT_PAIRS, S, ATT_BLOCK), F32)
    scratch = ([pair_buf] * 4 + [pltpu.VMEM((S, ATT_BLOCK), F32)] + [pair_buf] * 3
               + [pltpu.VMEM((ATT_PAIRS, 2 * ATT_BLOCK, 2 * ATT_BLOCK), F32)])
    rs = riders
    res = _pcall(_with_riders(compute, rs, 6, 3, len(scratch), (B - 1, ATT_GROUPS - 1)), name="attention_bwd",
                 out_shape=(shp, shp, shp) + (tuple(rs.out_shape) if rs else ()), grid=(B, ATT_GROUPS),
                 in_specs=[col(0), col(1), col(2), o_spec, o_spec, l_spec] + (rs.specs if rs else []),
                 out_specs=(o_spec, o_spec, o_spec) + (tuple(rs.specs) if rs else ()),
                 scratch_shapes=scratch + (rs.scratch if rs else []),
                 dims=("arbitrary", "arbitrary"))(proj3, proj3, proj3, do3, o3, lse4, *(rs.arrs if rs else []))
    return res[0], res[1], res[2], list(res[3:])


def _expand_groups(m):
    rows = SSM_WIDTH
    t = jnp.concatenate([m] * SSM_GROUPS, axis=0)
    r = lax.broadcasted_iota(jnp.int32, (rows, SSM_LANES), 0)
    l = lax.broadcasted_iota(jnp.int32, (rows, SSM_LANES), 1)
    keep = lax.shift_right_logical(r, 4) == lax.shift_right_logical(l, 6)
    return jnp.where(keep, t, 0.0)


def _collapse_groups(m):
    rows = SSM_WIDTH
    r = lax.broadcasted_iota(jnp.int32, (rows, SSM_LANES), 0)
    l = lax.broadcasted_iota(jnp.int32, (rows, SSM_LANES), 1)
    keep = lax.shift_right_logical(r, 4) == lax.shift_right_logical(l, 6)
    t = jnp.where(keep, m, 0.0)
    acc = t[0:SSM_GROUP_CH]
    for g in range(1, SSM_GROUPS):
        acc = acc + t[g * SSM_GROUP_CH:(g + 1) * SSM_GROUP_CH]
    return acc


def _zoh(lr, li, ldt):
    dt = jnp.exp(ldt)
    mag = jnp.exp(lr * dt)
    ang = li * dt
    cs, sn = jnp.cos(ang), jnp.sin(ang)
    ab_re, ab_im = mag * cs, mag * sn
    nr, ni = ab_re - 1.0, ab_im
    den = lr * lr + li * li
    n_re = nr * lr + ni * li
    n_im = ni * lr - nr * li
    return dict(dt=dt, mag=mag, cs=cs, sn=sn, ab_re=ab_re, ab_im=ab_im, nr=nr, ni=ni, den=den, n_re=n_re, n_im=n_im,
                f_re=n_re / den, f_im=n_im / den)


def _ssm_params(lr, li, ldt, br, bi, cr, ci):
    def body(lr_ref, li_ref, ldt_ref, br_ref, bi_ref, cr_ref, ci_ref, ab_ref, w_ref, c_ref):
        z = _zoh(lr_ref[...], li_ref[...], ldt_ref[...])
        ab_ref[0:1, :] = z["ab_re"]
        ab_ref[1:2, :] = z["ab_im"]
        br, bi = br_ref[...], bi_ref[...]
        w_ref[:, 0:SSM_LANES] = _expand_groups(z["f_re"] * br - z["f_im"] * bi).astype(BF16)
        w_ref[:, SSM_LANES:] = _expand_groups(z["f_re"] * bi + z["f_im"] * br).astype(BF16)
        c_ref[:, 0:SSM_LANES] = _expand_groups(cr_ref[...]).astype(BF16)
        c_ref[:, SSM_LANES:] = _expand_groups(-ci_ref[...]).astype(BF16)

    return _pcall(body, name="ssm_params",
                  out_shape=(_sds((2, SSM_LANES), F32), _sds((SSM_WIDTH, 2 * SSM_LANES), BF16),
                             _sds((SSM_WIDTH, 2 * SSM_LANES), BF16)))(lr, li, ldt, br, bi, cr, ci)


def _ssm_params_bwd(lr, li, ldt, br, bi, dab, dw, dc):
    def body(lr_ref, li_ref, ldt_ref, br_ref, bi_ref, dab_ref, dw_ref, dc_ref,
             dlr_ref, dli_ref, dldt_ref, dbr_ref, dbi_ref, dcr_ref, dci_ref):
        lr, li = lr_ref[...], li_ref[...]
        z = _zoh(lr, li, ldt_ref[...])
        br, bi = br_ref[...], bi_ref[...]
        dbb_re = _collapse_groups(dw_ref[:, 0:SSM_LANES])
        dbb_im = _collapse_groups(dw_ref[:, SSM_LANES:])
        dcr_ref[...] = _collapse_groups(dc_ref[:, 0:SSM_LANES])
        dci_ref[...] = -_collapse_groups(dc_ref[:, SSM_LANES:])
        f_re, f_im = z["f_re"], z["f_im"]
        dbr_ref[...] = f_re * dbb_re + f_im * dbb_im
        dbi_ref[...] = f_re * dbb_im - f_im * dbb_re
        df_re = jnp.sum(dbb_re * br + dbb_im * bi, axis=0, keepdims=True)
        df_im = jnp.sum(dbb_im * br - dbb_re * bi, axis=0, keepdims=True)
        den = z["den"]
        dn_re, dn_im = df_re / den, df_im / den
        dden = -(df_re * z["n_re"] + df_im * z["n_im"]) / (den * den)
        dnr = dn_re * lr - dn_im * li
        dni = dn_re * li + dn_im * lr
        dlr = dn_re * z["nr"] + dn_im * z["ni"] + 2.0 * dden * lr
        dli = dn_re * z["ni"] - dn_im * z["nr"] + 2.0 * dden * li
        dab_re = dab_ref[0:1, :] + dnr
        dab_im = dab_ref[1:2, :] + dni
        mag, cs, sn, dt = z["mag"], z["cs"], z["sn"], z["dt"]
        dmag = dab_re * cs + dab_im * sn
        dang = mag * (dab_im * cs - dab_re * sn)
        dlr_ref[...] = dlr + dmag * mag * dt
        dli_ref[...] = dli + dang * dt
        ddt = dmag * mag * lr + dang * li
        per_lane = jnp.broadcast_to(ddt * dt, (8, SSM_LANES))
        lane = lax.broadcasted_iota(jnp.int32, (SSM_LANES, 128), 0)
        col = lax.broadcasted_iota(jnp.int32, (SSM_LANES, 128), 1)
        ind = jnp.where(lax.shift_right_logical(lane, 6) == col, 1.0, 0.0)
        dldt_ref[...] = jnp.dot(per_lane, ind, preferred_element_type=F32, precision=lax.Precision.HIGHEST)[0:1]

    vec = _sds((1, SSM_LANES), F32)
    mat = _sds((SSM_GROUP_CH, SSM_LANES), F32)
    return _pcall(body, name="ssm_params_bwd", out_shape=(vec, vec, _sds((1, 128), F32), mat, mat, mat, mat))(
        lr, li, ldt, br, bi, dab, dw, dc)


SCAN_CHUNK = 512


def _scan_consts(ar, ai, k_ref, reverse):
    row = lax.broadcasted_iota(jnp.int32, (8, SSM_LANES), 0)
    pw = [(ar, ai)]
    for _ in range(7):
        pr, pi = pw[-1]
        pw.append((pr * ar - pi * ai, pr * ai + pi * ar))
    for n, k in enumerate((1, 2, 4)):
        keep = (row < 8 - k) if reverse else (row >= k)
        k_ref[2 * n] = jnp.where(keep, jnp.broadcast_to(pw[k - 1][0], (8, SSM_LANES)), 0.0)
        k_ref[2 * n + 1] = jnp.where(keep, jnp.broadcast_to(pw[k - 1][1], (8, SSM_LANES)), 0.0)
    cr = jnp.zeros((8, SSM_LANES), F32)
    ci = jnp.zeros((8, SSM_LANES), F32)
    for r in range(8):
        e = (8 - r) if reverse else (r + 1)
        cr = jnp.where(row == r, jnp.broadcast_to(pw[e - 1][0], (8, SSM_LANES)), cr)
        ci = jnp.where(row == r, jnp.broadcast_to(pw[e - 1][1], (8, SSM_LANES)), ci)
    k_ref[6] = cr
    k_ref[7] = ci


def _scan_tile(xr, xi, k_ref, car, cai, reverse):
    for n, k in enumerate((1, 2, 4)):
        sh = (8 - k) if reverse else k
        sr = pltpu.roll(xr, sh, 0)
        si = pltpu.roll(xi, sh, 0)
        mr, mi = k_ref[2 * n], k_ref[2 * n + 1]
        xr, xi = xr + mr * sr - mi * si, xi + mr * si + mi * sr
    pr, pi = k_ref[6], k_ref[7]
    xr, xi = xr + pr * car - pi * cai, xi + pr * cai + pi * car
    return xr, xi


US_BLOCK = (3 * ATT_WIDTH) // SSM_WIDTH


def _ssm_scan_fwd(proj3, abar, w_bu, w_c):
    B, S, _ = proj3.shape
    ch = min(S, SCAN_CHUNK)
    u_spec = pl.BlockSpec((1, ch, SSM_WIDTH), lambda b, c: (b, c, US_BLOCK))
    x_spec = pl.BlockSpec((1, ch, 2 * SSM_LANES), lambda b, c: (b, c, 0))
    y_spec = pl.BlockSpec((1, ch, SSM_WIDTH), lambda b, c: (b, c, 0))
    w_spec = pl.BlockSpec((SSM_WIDTH, 2 * SSM_LANES), lambda b, c: (0, 0))

    def body(ab_ref, u_ref, wb_ref, wc_ref, x_ref, y_ref, k_ref, carry_ref):
        _scan_consts(ab_ref[0:1, :], ab_ref[1:2, :], k_ref, False)

        @pl.when(pl.program_id(1) == 0)
        def _():
            carry_ref[...] = jnp.zeros_like(carry_ref)

        x_ref[0] = jnp.dot(u_ref[0], wb_ref[...], preferred_element_type=F32)

        def step(i, carry):
            base = pl.multiple_of(i * 8, 8)
            xr = x_ref[0, pl.ds(base, 8), 0:SSM_LANES]
            xi = x_ref[0, pl.ds(base, 8), SSM_LANES:]
            xr, xi = _scan_tile(xr, xi, k_ref, carry[0], carry[1], False)
            x_ref[0, pl.ds(base, 8), 0:SSM_LANES] = xr
            x_ref[0, pl.ds(base, 8), SSM_LANES:] = xi
            return (jnp.broadcast_to(xr[7:8], (8, SSM_LANES)), jnp.broadcast_to(xi[7:8], (8, SSM_LANES)))

        cr, ci = lax.fori_loop(0, ch // 8, step, (carry_ref[0], carry_ref[1]))
        carry_ref[0] = cr
        carry_ref[1] = ci
        y_ref[0] = lax.dot_general(x_ref[0].astype(BF16), wc_ref[...], NT_DIMS, preferred_element_type=F32)

    return _pcall(body, name="ssm_scan_fwd",
                  out_shape=(_sds((B, S, 2 * SSM_LANES), F32), _sds((B, S, SSM_WIDTH), F32)), grid=(B, S // ch),
                  in_specs=[pl.BlockSpec((2, SSM_LANES), lambda b, c: (0, 0)), u_spec, w_spec, w_spec],
                  out_specs=(x_spec, y_spec),
                  scratch_shapes=[pltpu.VMEM((8, 8, SSM_LANES), F32), pltpu.VMEM((2, 8, SSM_LANES), F32)],
                  dims=("arbitrary", "arbitrary"))(abar, proj3, w_bu, w_c)


def _ssm_scan_bwd(proj3, dy3, xs3, abar, w_bu, w_c, dsk):
    B, S, _ = proj3.shape
    ch = min(S, SCAN_CHUNK)
    nc = S // ch
    u_spec = pl.BlockSpec((1, ch, SSM_WIDTH), lambda b, c: (b, nc - 1 - c, US_BLOCK))
    x_spec = pl.BlockSpec((1, ch, 2 * SSM_LANES), lambda b, c: (b, nc - 1 - c, 0))
    y_spec = pl.BlockSpec((1, ch, SSM_WIDTH), lambda b, c: (b, nc - 1 - c, 0))
    w_spec = pl.BlockSpec((SSM_WIDTH, 2 * SSM_LANES), lambda b, c: (0, 0))
    ab_spec = pl.BlockSpec((2, SSM_LANES), lambda b, c: (0, 0))
    d_spec = pl.BlockSpec((1, SSM_WIDTH), lambda b, c: (0, 0))

    def body(ab_ref, u_ref, dy_ref, xs_ref, wb_ref, wc_ref, d_ref, du_ref, da_ref, dwb_ref, dwc_ref,
             g_ref, k_ref, carry_ref, acc_ref):
        b, c = pl.program_id(0), pl.program_id(1)
        _scan_consts(ab_ref[0:1, :], -ab_ref[1:2, :], k_ref, True)
        row = lax.broadcasted_iota(jnp.int32, (8, SSM_LANES), 0)

        @pl.when(c == 0)
        def _():
            carry_ref[...] = jnp.zeros_like(carry_ref)

        @pl.when((c == 0) & (b == 0))
        def _():
            acc_ref[...] = jnp.zeros_like(acc_ref)
            dwb_ref[...] = jnp.zeros_like(dwb_ref)
            dwc_ref[...] = jnp.zeros_like(dwc_ref)

        dy = dy_ref[0]
        dyb = dy.astype(BF16)
        g_ref[...] = jnp.dot(dyb, wc_ref[...], preferred_element_type=F32)

        def step(i, carry):
            car, cai, ar_acc, ai_acc = carry
            base = pl.multiple_of((ch // 8 - 1 - i) * 8, 8)
            gr = g_ref[pl.ds(base, 8), 0:SSM_LANES]
            gi = g_ref[pl.ds(base, 8), SSM_LANES:]
            gr, gi = _scan_tile(gr, gi, k_ref, car, cai, True)
            g_ref[pl.ds(base, 8), 0:SSM_LANES] = gr
            g_ref[pl.ds(base, 8), SSM_LANES:] = gi
            nr = jnp.where(row == 7, car, pltpu.roll(gr, 7, 0))
            ni = jnp.where(row == 7, cai, pltpu.roll(gi, 7, 0))
            xr = xs_ref[0, pl.ds(base, 8), 0:SSM_LANES]
            xi = xs_ref[0, pl.ds(base, 8), SSM_LANES:]
            ar_acc = ar_acc + nr * xr + ni * xi
            ai_acc = ai_acc + ni * xr - nr * xi
            return (jnp.broadcast_to(gr[0:1], (8, SSM_LANES)), jnp.broadcast_to(gi[0:1], (8, SSM_LANES)), ar_acc, ai_acc)

        cr, ci, ar_acc, ai_acc = lax.fori_loop(0, ch // 8, step, (carry_ref[0], carry_ref[1], acc_ref[0], acc_ref[1]))
        carry_ref[0] = cr
        carry_ref[1] = ci
        acc_ref[0] = ar_acc
        acc_ref[1] = ai_acc
        da_ref[0:1, :] = jnp.sum(ar_acc, axis=0, keepdims=True)
        da_ref[1:2, :] = jnp.sum(ai_acc, axis=0, keepdims=True)

        gb = g_ref[...].astype(BF16)
        du = lax.dot_general(gb, wb_ref[...], NT_DIMS, preferred_element_type=F32) + d_ref[...] * dy
        du_ref[0] = du.astype(BF16)
        xb = xs_ref[0].astype(BF16)
        u = u_ref[0]
        for j in range(2 * SSM_LANES // SSM_WIDTH):
            rows = slice((j % (SSM_LANES // SSM_WIDTH)) * 64, (j % (SSM_LANES // SSM_WIDTH)) * 64 + 64)
            cols = slice(j * SSM_WIDTH, (j + 1) * SSM_WIDTH)
            dwb_ref[rows, cols] += lax.dot_general(u[:, rows], gb[:, cols], TN_DIMS, preferred_element_type=F32)
            dwc_ref[rows, cols] += lax.dot_general(dyb[:, rows], xb[:, cols], TN_DIMS, preferred_element_type=F32)

    mat = _sds((SSM_WIDTH, 2 * SSM_LANES), F32)
    return _pcall(body, name="ssm_scan_bwd",
                  out_shape=(_sds((B, S, SSM_WIDTH), BF16), _sds((2, SSM_LANES), F32), mat, mat), grid=(B, nc),
                  in_specs=[ab_spec, u_spec, y_spec, x_spec, w_spec, w_spec, d_spec],
                  out_specs=(y_spec, ab_spec, w_spec, w_spec),
                  scratch_shapes=[pltpu.VMEM((ch, 2 * SSM_LANES), F32), pltpu.VMEM((8, 8, SSM_LANES), F32),
                                  pltpu.VMEM((2, 8, SSM_LANES), F32), pltpu.VMEM((2, 8, SSM_LANES), F32)],
                  dims=("arbitrary", "arbitrary"))(abar, proj3, dy3, xs3, w_bu, w_c, dsk)


GELU_K = math.sqrt(2.0 / math.pi)
GELU_C = 0.044715


def _gelu_parts(y):
    t = jnp.tanh(GELU_K * (y + GELU_C * y * y * y))
    return 0.5 * y * (1.0 + t), t


def _ssm_post(yc, us, dsk, wglu, bglu):
    T, N = yc.shape
    tm = min(T, 1024)
    row = pl.BlockSpec((tm, N), lambda i: (i, 0))
    vec = pl.BlockSpec((1, N), lambda i: (0, 0))
    mat = pl.BlockSpec((N, N), lambda i: (0, 0))

    def body(yc_ref, us_ref, d_ref, w_ref, b_ref, y_ref, s_ref):
        y = yc_ref[...] + d_ref[...] * us_ref[...]
        y_ref[...] = y
        z, _ = _gelu_parts(y)
        gl = jnp.dot(z.astype(BF16), w_ref[...], preferred_element_type=F32) + b_ref[...]
        s_ref[...] = (z * _sig(gl)).astype(BF16)

    return _pcall(body, name="ssm_post", out_shape=(_sds((T, N), F32), _sds((T, N), BF16)), grid=(T // tm,),
                  in_specs=[row, row, vec, mat, vec], out_specs=(row, row), dims=("parallel",))(yc, us, dsk, wglu, bglu)


def _ssm_post_bwd(y5, us, ds, dsk, wglu, bglu):
    T, N = y5.shape
    tm = min(T, 1024)
    row = pl.BlockSpec((tm, N), lambda i: (i, 0))
    vec = pl.BlockSpec((1, N), lambda i: (0, 0))
    mat = pl.BlockSpec((N, N), lambda i: (0, 0))

    def body(y_ref, us_ref, ds_ref, d_ref, w_ref, b_ref, dy_ref, dd_ref, db_ref, dw_ref):
        @pl.when(pl.program_id(0) == 0)
        def _():
            dd_ref[...] = jnp.zeros_like(dd_ref)
            db_ref[...] = jnp.zeros_like(db_ref)
            dw_ref[...] = jnp.zeros_like(dw_ref)

        y = y_ref[...]
        z, t = _gelu_parts(y)
        zb = z.astype(BF16)
        gl = jnp.dot(zb, w_ref[...], preferred_element_type=F32) + b_ref[...]
        sg = _sig(gl)
        ds = ds_ref[...]
        dgl = ds * z * sg * (1.0 - sg)
        dglb = dgl.astype(BF16)
        dz = ds * sg + lax.dot_general(dglb, w_ref[...], (((1,), (1,)), ((), ())), preferred_element_type=F32)
        dgelu = 0.5 * (1.0 + t) + 0.5 * y * (1.0 - t * t) * GELU_K * (1.0 + 3.0 * GELU_C * y * y)
        dy = dz * dgelu
        dy_ref[...] = dy
        dd_ref[...] += jnp.sum(dy * us_ref[...], axis=0, keepdims=True)
        db_ref[...] += jnp.sum(dgl, axis=0, keepdims=True)
        dw_ref[...] += lax.dot_general(zb, dglb, (((0,), (0,)), ((), ())), preferred_element_type=F32)

    return _pcall(body, name="ssm_post_bwd",
                  out_shape=(_sds((T, N), F32), _sds((1, N), F32), _sds((1, N), F32), _sds((N, N), F32)),
                  grid=(T // tm,), in_specs=[row, row, row, vec, mat, vec], out_specs=(row, vec, vec, mat),
                  dims=("arbitrary",))(y5, us, ds, dsk, wglu, bglu)


GATE_TILE = 256
GATE_ATT_BLOCK0 = (3 * ATT_WIDTH + SSM_WIDTH) // GATE_TILE
GATE_SSM_BLOCK0 = (3 * ATT_WIDTH + SSM_WIDTH + D_MODEL) // GATE_TILE


def _merge(proj, y_att, y_ssm, b_gate):
    T = proj.shape[0]
    tm = min(T, 1024)
    nj = D_MODEL // GATE_TILE
    ga = pl.BlockSpec((tm, GATE_TILE), lambda i, j: (i, GATE_ATT_BLOCK0 + j))
    gs = pl.BlockSpec((tm, GATE_TILE), lambda i, j: (i, GATE_SSM_BLOCK0 + j))
    yy = pl.BlockSpec((tm, GATE_TILE), lambda i, j: (i, j))
    ba = pl.BlockSpec((1, GATE_TILE), lambda i, j: (0, j))
    bs = pl.BlockSpec((1, GATE_TILE), lambda i, j: (0, nj + j))

    def body(ga_ref, gs_ref, ya_ref, ys_ref, ba_ref, bs_ref, o_ref):
        o_ref[...] = (_sig(ga_ref[...] + ba_ref[...]) * ya_ref[...]
                      + _sig(gs_ref[...] + bs_ref[...]) * ys_ref[...]).astype(BF16)

    return _pcall(body, name="merge", out_shape=_sds((T, D_MODEL), BF16), grid=(T // tm, nj),
                  in_specs=[ga, gs, yy, yy, ba, bs], out_specs=yy, dims=("parallel", "parallel"))(
        proj, proj, y_att, y_ssm, b_gate, b_gate)


def _merge_bwd(proj, y_att, y_ssm, b_gate, dmerged):
    T = proj.shape[0]
    tm = min(T, 1024)
    nj = D_MODEL // GATE_TILE
    ga = pl.BlockSpec((tm, GATE_TILE), lambda j, i: (i, GATE_ATT_BLOCK0 + j))
    gs = pl.BlockSpec((tm, GATE_TILE), lambda j, i: (i, GATE_SSM_BLOCK0 + j))
    yy = pl.BlockSpec((tm, GATE_TILE), lambda j, i: (i, j))
    ba = pl.BlockSpec((1, GATE_TILE), lambda j, i: (0, j))
    bs = pl.BlockSpec((1, GATE_TILE), lambda j, i: (0, nj + j))

    def body(ga_ref, gs_ref, ya_ref, ys_ref, ba_ref, bs_ref, dm_ref, dya_ref, dys_ref, dga_ref, dgs_ref, dba_ref, dbs_ref):
        @pl.when(pl.program_id(1) == 0)
        def _():
            dba_ref[...] = jnp.zeros_like(dba_ref)
            dbs_ref[...] = jnp.zeros_like(dbs_ref)

        dm = dm_ref[...].astype(F32)
        sa = _sig(ga_ref[...] + ba_ref[...])
        ss = _sig(gs_ref[...] + bs_ref[...])
        dya_ref[...] = (dm * sa).astype(BF16)
        dys_ref[...] = (dm * ss).astype(BF16)
        dga = dm * ya_ref[...] * sa * (1.0 - sa)
        dgs = dm * ys_ref[...] * ss * (1.0 - ss)
        dga_ref[...] = dga.astype(BF16)
        dgs_ref[...] = dgs.astype(BF16)
        dba_ref[...] += jnp.sum(dga, axis=0, keepdims=True)
        dbs_ref[...] += jnp.sum(dgs, axis=0, keepdims=True)

    big = _sds((T, D_MODEL), BF16)
    vec = _sds((1, D_MODEL), F32)
    return _pcall(body, name="merge_bwd", out_shape=(big, big, big, big, vec, vec), grid=(nj, T // tm),
                  in_specs=[ga, gs, yy, yy, ba, bs, yy], out_specs=(yy, yy, yy, yy, ba, ba),
                  dims=("arbitrary", "arbitrary"))(proj, proj, y_att, y_ssm, b_gate, b_gate, dmerged)


CONV_TILE = 256


def _shift_rows(a, j, up=False):
    n = a.shape[0]
    r = pltpu.roll(a, n - j if up else j, 0)
    row = lax.broadcasted_iota(jnp.int32, (8, a.shape[1]), 0)
    if up:
        return jnp.concatenate([r[:n - 8], jnp.where(row < 8 - j, r[n - 8:], 0.0)], axis=0)
    return jnp.concatenate([jnp.where(row >= j, r[:8], 0.0), r[8:]], axis=0)


def _conv_pre(a, w_ref, b_ref):
    conv = b_ref[...] + w_ref[0:1, :] * a
    shifted = []
    for j in (1, 2):
        sh = _shift_rows(a, j)
        shifted.append(sh)
        conv = conv + w_ref[j:j + 1, :] * sh
    return conv, shifted


def _conv_act(up3, w_conv, b_conv):
    B, S, _ = up3.shape
    nj = D_FF // CONV_TILE
    a_spec = pl.BlockSpec((1, S, CONV_TILE), lambda b, j: (b, 0, j))
    v_spec = pl.BlockSpec((1, S, CONV_TILE), lambda b, j: (b, 0, nj + j))
    w_spec = pl.BlockSpec((3, CONV_TILE), lambda b, j: (0, j))
    b_spec = pl.BlockSpec((1, CONV_TILE), lambda b, j: (0, j))

    def body(a_ref, v_ref, w_ref, b_ref, o_ref):
        a = a_ref[0].astype(F32)
        conv, _ = _conv_pre(a, w_ref, b_ref)
        o_ref[0] = (conv * _sig(conv) * v_ref[0]).astype(BF16)

    return _pcall(body, name="conv_act", out_shape=_sds((B, S, D_FF), BF16), grid=(B, nj),
                  in_specs=[a_spec, v_spec, w_spec, b_spec], out_specs=a_spec, dims=("parallel", "parallel"))(
        up3, up3, w_conv, b_conv)


def _conv_bwd(up3, dact3, w_conv, b_conv):
    B, S, _ = up3.shape
    nj = D_FF // CONV_TILE
    a_spec = pl.BlockSpec((1, S, CONV_TILE), lambda j, b: (b, 0, j))
    v_spec = pl.BlockSpec((1, S, CONV_TILE), lambda j, b: (b, 0, nj + j))
    o_spec = pl.BlockSpec((2, 1, S, CONV_TILE), lambda j, b: (0, b, 0, j))
    w_spec = pl.BlockSpec((3, CONV_TILE), lambda j, b: (0, j))
    b_spec = pl.BlockSpec((1, CONV_TILE), lambda j, b: (0, j))

    def body(a_ref, v_ref, d_ref, w_ref, b_ref, dup_ref, dw_ref, db_ref):
        @pl.when(pl.program_id(1) == 0)
        def _():
            dw_ref[...] = jnp.zeros_like(dw_ref)
            db_ref[...] = jnp.zeros_like(db_ref)

        a = a_ref[0].astype(F32)
        d = d_ref[0].astype(F32)
        conv, shifted = _conv_pre(a, w_ref, b_ref)
        sg = _sig(conv)
        dup_ref[1, 0] = (d * conv * sg).astype(BF16)
        dconv = d * v_ref[0] * (sg * (1.0 + conv * (1.0 - sg)))
        da = w_ref[0:1, :] * dconv
        for j in (1, 2):
            da = da + w_ref[j:j + 1, :] * _shift_rows(dconv, j, up=True)
        dup_ref[0, 0] = da.astype(BF16)
        db_ref[...] += jnp.sum(dconv, axis=0, keepdims=True)
        dw_ref[0:1, :] += jnp.sum(dconv * a, axis=0, keepdims=True)
        dw_ref[1:2, :] += jnp.sum(dconv * shifted[0], axis=0, keepdims=True)
        dw_ref[2:3, :] += jnp.sum(dconv * shifted[1], axis=0, keepdims=True)

    return _pcall(body, name="conv_bwd",
                  out_shape=(_sds((2, B, S, D_FF), BF16), _sds((3, D_FF), F32), _sds((1, D_FF), F32)),
                  grid=(nj, B), in_specs=[a_spec, v_spec, a_spec, w_spec, b_spec],
                  out_specs=(o_spec, w_spec, b_spec), dims=("arbitrary", "arbitrary"))(up3, up3, dact3, w_conv, b_conv)


def _rows_tile(r, cap=640):
    for t in range(min(r, cap) - min(r, cap) % 8, 7, -8):
        if r % t == 0:
            return t
    return r


def _add2(a, b, out_dtype, name):
    R, N = a.shape
    tr = _rows_tile(R)
    spec = pl.BlockSpec((tr, N), lambda i: (i, 0))

    def body(a_ref, b_ref, o_ref):
        o_ref[...] = (a_ref[...] + b_ref[...]).astype(out_dtype)

    return _pcall(body, name=name, out_shape=_sds((R, N), out_dtype), grid=(R // tr,), in_specs=[spec, spec],
                  out_specs=spec, dims=("parallel",))(a, b)


def _sum_slots(q, name):
    n, R, N = q.shape
    tr = _rows_tile(R)

    def body(q_ref, o_ref):
        acc = q_ref[0].astype(F32)
        for s in range(1, n):
            acc = acc + q_ref[s].astype(F32)
        o_ref[...] = acc

    return _pcall(body, name=name, out_shape=_sds((R, N), F32), grid=(R // tr,),
                  in_specs=[pl.BlockSpec((n, tr, N), lambda i: (0, i, 0))], out_specs=pl.BlockSpec((tr, N), lambda i: (i, 0)),
                  dims=("parallel",))(q)


NATIVE = (("b_re", 16, 1024), ("b_im", 16, 1024), ("c_re", 16, 1024), ("c_im", 16, 1024), ("g_mix", 1, 1024),
          ("b_att", 1, 1024), ("b_ssm", 1, 1024), ("a_re", 1, 1024), ("a_im", 1, 1024), ("log_dt", 1, 128),
          ("d_skip", 1, 256), ("b_glu", 1, 256), ("g_ffn", 1, 1024), ("g_final", 1, 1024), ("b_conv", 1, 2048),
          ("w_conv", 3, 2048), ("loss", 1, 1))
N_MOD = 6
NATIVE_LATE = ("g_mix",)
MODS_LATE = (0, 1)


def _small_plan(late):
    pieces = [p for p in NATIVE if (p[0] in NATIVE_LATE) == late]
    mods = [k for k in range(N_MOD) if (k in MODS_LATE) == late]
    starts, r = {}, 0
    for name, rows, cols in pieces:
        starts[name] = r
        r += rows * (-(-cols // LANES))
    return pieces, mods, starts, -(-r // 8) * 8


def _pack_small(native, dmods, late):
    pieces, mods, starts, n_sum = _small_plan(late)
    B = dmods[mods[0]].shape[0]
    total = n_sum + 8 * len(mods)

    def body(*refs):
        xs, ms, o_ref = refs[:len(pieces)], refs[len(pieces):-1], refs[-1]
        o_ref[...] = jnp.zeros_like(o_ref)
        for (name, rows, cols), x_ref in zip(pieces, xs):
            chunks = -(-cols // LANES)
            if chunks == 1 and rows % 8 == 0:
                o_ref[starts[name]:starts[name] + rows, 0:cols] = x_ref[...]
                continue
            for i in range(rows):
                for q in range(chunks):
                    wd = min(LANES, cols - q * LANES)
                    r = starts[name] + i * chunks + q
                    o_ref[r:r + 1, 0:wd] = x_ref[i:i + 1, q * LANES:q * LANES + wd]
        for k, m_ref in enumerate(ms):
            for b in range(B):
                o_ref[n_sum + 8 * k + b:n_sum + 8 * k + b + 1, :] = m_ref[b]

    return _pcall(body, name="pack_small_late" if late else "pack_small_early", out_shape=_sds((total, LANES), F32))(
        *[native[n] for n, _, _ in pieces], *[dmods[k] for k in mods])


def _sum_unpack_small(gathered_early, gathered_late, B):
    plans = [_small_plan(False), _small_plan(True)]
    nd = gathered_early.shape[0]
    n_out = len(NATIVE)

    def body(*refs):
        g_refs, outs, dm_ref, accs = refs[0:2], refs[2:2 + n_out], refs[2 + n_out], refs[3 + n_out:]
        o = 0
        for g_ref, acc, (pieces, mods, starts, n_sum) in zip(g_refs, accs, plans):
            s = g_ref[0, 0:n_sum, :]
            for d in range(1, nd):
                s = s + g_ref[d, 0:n_sum, :]
            acc[...] = s
            for name, rows, cols in pieces:
                o_ref = outs[o]
                o += 1
                chunks = -(-cols // LANES)
                if chunks == 1 and rows % 8 == 0:
                    o_ref[...] = acc[starts[name]:starts[name] + rows, 0:cols]
                    continue
                for i in range(rows):
                    for q in range(chunks):
                        wd = min(LANES, cols - q * LANES)
                        r = starts[name] + i * chunks + q
                        o_ref[i:i + 1, q * LANES:q * LANES + wd] = acc[r:r + 1, 0:wd]
            for d in range(nd):
                for j, k in enumerate(mods):
                    dm_ref[d, :, k * D_MODEL:(k + 1) * D_MODEL] = g_ref[d, n_sum + 8 * j:n_sum + 8 * j + B, :]

    ordered = [p for pieces, _, _, _ in plans for p in pieces]
    out_shape = tuple(_sds((rows, cols), F32) for _, rows, cols in ordered) + (_sds((nd, B, N_MOD * D_MODEL), F32),)
    res = _pcall(body, name="sum_unpack_small", out_shape=out_shape,
                 scratch_shapes=[pltpu.VMEM((n_sum, LANES), F32) for _, _, _, n_sum in plans])(gathered_early, gathered_late)
    return {n: r for (n, _, _), r in zip(ordered, res[:-1])}, res[-1]


def _small_from_native(nat):
    lanes3 = lambda a: a.reshape(SSM_GROUP_CH, SSM_GROUPS, SSM_STATE)
    return dict(
        g_mix=nat["g_mix"].reshape(D_MODEL), b_gate=jnp.concatenate([nat["b_att"], nat["b_ssm"]], axis=1).reshape(2 * D_MODEL),
        a_re=nat["a_re"].reshape(SSM_GROUPS, SSM_STATE), a_im=nat["a_im"].reshape(SSM_GROUPS, SSM_STATE),
        log_dt=nat["log_dt"][0, :SSM_GROUPS], b_re=_groups_from_lanes(nat["b_re"]), b_im=_groups_from_lanes(nat["b_im"]),
        c_re=lanes3(nat["c_re"]).transpose(1, 0, 2), c_im=lanes3(nat["c_im"]).transpose(1, 0, 2),
        d_skip=nat["d_skip"].reshape(SSM_WIDTH), b_glu=nat["b_glu"].reshape(SSM_WIDTH), g_ffn=nat["g_ffn"].reshape(D_MODEL),
        w_conv=nat["w_conv"], b_conv=nat["b_conv"].reshape(D_FF), g_final=nat["g_final"].reshape(D_MODEL))


def _adamw_multi(params):
    n = len(params)
    bc1 = 1.0 - ADAM_B1 ** ADAM_STEP
    bc2 = 1.0 - ADAM_B2 ** ADAM_STEP

    def body(*refs):
        ins, outs = refs[:4 * n], refs[4 * n:]
        for i in range(n):
            w_ref, g_ref, m_ref, v_ref = ins[4 * i:4 * i + 4]
            d_ref, nm_ref, nv_ref = outs[3 * i:3 * i + 3]
            g = g_ref[...]
            m = ADAM_B1 * m_ref[...] + (1.0 - ADAM_B1) * g
            v = ADAM_B2 * v_ref[...] + (1.0 - ADAM_B2) * (g * g)
            nm_ref[...] = m
            nv_ref[...] = v
            d_ref[...] = -ADAM_LR * ((m / bc1) / (jnp.sqrt(v / bc2) + ADAM_EPS) + ADAM_WD * w_ref[...])

    flat = [a for p in params for a in p]
    out_shape = tuple(_sds(p[0].shape, F32) for p in params for _ in range(3))
    res = _pcall(body, name="adamw_small", out_shape=out_shape)(*flat)
    return [tuple(res[3 * i:3 * i + 3]) for i in range(n)]


def _adamw(w, g, m, v, name, g_other=None):
    R, N = w.shape
    tr = _rows_tile(R, 256)
    spec = pl.BlockSpec((tr, N), lambda i: (i, 0))
    bc1 = 1.0 - ADAM_B1 ** ADAM_STEP
    bc2 = 1.0 - ADAM_B2 ** ADAM_STEP
    two = g_other is not None

    def body(*refs):
        w_ref, g_ref, m_ref, v_ref = refs[:4]
        d_ref, nm_ref, nv_ref = refs[4 + two:7 + two]
        g = g_ref[...]
        if two:
            g = g + refs[4][...]
            refs[8][...] = g
        m = ADAM_B1 * m_ref[...] + (1.0 - ADAM_B1) * g
        v = ADAM_B2 * v_ref[...] + (1.0 - ADAM_B2) * (g * g)
        nm_ref[...] = m
        nv_ref[...] = v
        d_ref[...] = -ADAM_LR * ((m / bc1) / (jnp.sqrt(v / bc2) + ADAM_EPS) + ADAM_WD * w_ref[...])

    shp = _sds((R, N), F32)
    args = (w, g, m, v) + ((g_other,) if two else ())
    return _pcall(body, name=name, out_shape=(shp,) * (3 + two), grid=(R // tr,), in_specs=[spec] * len(args),
                  out_specs=(spec,) * (3 + two), dims=("parallel",))(*args)


_GROUP_MASKS = {
    "all": [(dx, dy, dc) for dx in (0, 1) for dy in (0, 1) for dc in (0, 1) if (dx, dy, dc) != (0, 0, 0)],
    "xy": [(1, 0, 0), (0, 1, 0), (1, 1, 0)],
    "c": [(0, 0, 1)],
}
_GROUP_SLOTS = {"all": 8, "xy": 4, "c": 2}


def _group_slot(group, x, y, c):
    return {"all": 4 * x + 2 * y + c, "xy": 2 * x + y, "c": c}[group]


def _flip(v, d):
    return 1 - v if d else v


def _exchange(arr, group, mode, name):
    return _exchange_list([arr], group, mode, name)[0]


def _exchange_list(arrs, group, mode, name):
    masks = _GROUP_MASKS[group]
    n = len(masks)
    na = len(arrs)
    assert mode in ("gather", "swap") and (mode == "gather" or group == "c")
    has_local = mode == "gather"
    out_shapes = [((_GROUP_SLOTS[group],) if has_local else ()) + arr.shape for arr in arrs]
    bounce = [pltpu.VMEM(arr.shape, arr.dtype) for arr in arrs] if has_local else []

    def body(*refs):
        x_refs, o_refs = refs[:na], refs[na:2 * na]
        send_sems, recv_sems = refs[2 * na], refs[2 * na + 1]
        x, y, c = lax.axis_index("x"), lax.axis_index("y"), lax.axis_index("c")
        me = _group_slot(group, x, y, c)
        if has_local:
            local_sems = refs[2 * na + 2]
            bufs = refs[2 * na + 3:]
            loads = []
            for i in range(na):
                loads.append(pltpu.make_async_copy(x_refs[i], bufs[i], local_sems.at[2 * i]))
                loads[-1].start()
        copies = []
        for i in range(na):
            x_ref, o_ref = x_refs[i], o_refs[i]
            for k, (dx, dy, dc) in enumerate(masks):
                px, py, pc = _flip(x, dx), _flip(y, dy), _flip(c, dc)
                src, dst = (x_ref, o_ref.at[me]) if has_local else (x_ref, o_ref)
                cp =pltpu.make_async_remote_copy(src_ref=src, dst_ref=dst, send_sem=send_sems.at[i * n + k],
                                                  recv_sem=recv_sems.at[i * n + k], device_id=(px, py, pc),
                                                  device_id_type=pl.DeviceIdType.MESH)
                cp.start()
                copies.append(cp)
        if has_local:
            stores = []
            for i in range(na):
                loads[i].wait()
                stores.append(pltpu.make_async_copy(bufs[i], o_refs[i].at[me], local_sems.at[2 * i + 1]))
                stores[-1].start()
        for cp in copies:
            cp.wait()
        if has_local:
            for st in stores:
                st.wait()

    anyspec = pl.BlockSpec(memory_space=pl.ANY)
    scratch = [pltpu.SemaphoreType.DMA((n * na,)), pltpu.SemaphoreType.DMA((n * na,))]
    if has_local:
        scratch += [pltpu.SemaphoreType.DMA((2 * na,))] + bounce
    outs = pl.pallas_call(body, name=name, out_shape=tuple(_sds(s, a.dtype) for s, a in zip(out_shapes, arrs)),
                          in_specs=[anyspec] * na, out_specs=tuple([anyspec] * na), scratch_shapes=scratch,
                          compiler_params=pltpu.CompilerParams(vmem_limit_bytes=V7X_VMEM_LIMIT_BYTES))(*arrs)
    return list(outs)


def _gather_weights(shards, name):
    na = len(shards)
    masks = _GROUP_MASKS["xy"]
    n = len(masks)

    def body(*refs):
        x_refs, o_refs = refs[:na], refs[na:2 * na]
        send_sems, recv_sems, local_sems = refs[2 * na:2 * na + 3]
        bufs = refs[2 * na + 3:]
        x, y, c = lax.axis_index("x"), lax.axis_index("y"), lax.axis_index("c")
        me = 2 * x + y
        sibling = (x, y, 1 - c)
        loads = []
        for i in range(na):
            loads.append(pltpu.make_async_copy(x_refs[i], bufs[i], local_sems.at[2 * i]))
            loads[-1].start()

        def half_of(i, slot, cc):
            h = shards[i].shape[0] // 2
            return o_refs[i].at[slot, pl.ds(pl.multiple_of(cc * h, 8), h), :]

        def src_half(i, cc):
            h = shards[i].shape[0] // 2
            return x_refs[i].at[pl.ds(pl.multiple_of(cc * h, 8), h), :]

        sends = []
        for i in range(na):
            for k, (dx, dy, _) in enumerate(masks):
                cp = pltpu.make_async_remote_copy(src_ref=src_half(i, c), dst_ref=half_of(i, me, c),
                                                  send_sem=send_sems.at[i * 2 * n + k], recv_sem=recv_sems.at[i * 2 * n + k],
                                                  device_id=(_flip(x, dx), _flip(y, dy), c),
                                                  device_id_type=pl.DeviceIdType.MESH)
                cp.start()
                sends.append(cp)
        stores = []
        for i in range(na):
            loads[i].wait()
            stores.append(pltpu.make_async_copy(bufs[i], o_refs[i].at[me], local_sems.at[2 * i + 1]))
            stores[-1].start()
        for i in range(na):
            for k, (dx, dy, _) in enumerate(masks):
                slot = 2 * _flip(x, dx) + _flip(y, dy)
                landed = pltpu.make_async_remote_copy(src_ref=src_half(i, c), dst_ref=half_of(i, slot, c),
                                                      send_sem=send_sems.at[i * 2 * n + k],
                                                      recv_sem=recv_sems.at[i * 2 * n + k], device_id=sibling,
                                                      device_id_type=pl.DeviceIdType.MESH)
                landed.wait_recv()
                fwd = pltpu.make_async_remote_copy(src_ref=half_of(i, slot, c), dst_ref=half_of(i, slot, c),
                                                   send_sem=send_sems.at[i * 2 * n + n + k],
                                                   recv_sem=recv_sems.at[i * 2 * n + n + k], device_id=sibling,
                                                   device_id_type=pl.DeviceIdType.MESH)
                fwd.start()
                sends.append(fwd)
        for i in range(na):
            for k, (dx, dy, _) in enumerate(masks):
                slot = 2 * _flip(x, dx) + _flip(y, dy)
                pltpu.make_async_remote_copy(src_ref=half_of(i, slot, 1 - c), dst_ref=half_of(i, slot, 1 - c),
                                             send_sem=send_sems.at[i * 2 * n + n + k],
                                             recv_sem=recv_sems.at[i * 2 * n + n + k], device_id=sibling,
                                             device_id_type=pl.DeviceIdType.MESH).wait_recv()
        for cp in sends:
            cp.wait_send()
        for st in stores:
            st.wait()

    anyspec = pl.BlockSpec(memory_space=pl.ANY)
    scratch = [pltpu.SemaphoreType.DMA((2 * n * na,)), pltpu.SemaphoreType.DMA((2 * n * na,)),
               pltpu.SemaphoreType.DMA((2 * na,))] + [pltpu.VMEM(s.shape, s.dtype) for s in shards]
    outs = pl.pallas_call(body, name=name, out_shape=tuple(_sds((N_XY,) + s.shape, s.dtype) for s in shards),
                          in_specs=[anyspec] * na, out_specs=tuple([anyspec] * na), scratch_shapes=scratch,
                          compiler_params=pltpu.CompilerParams(vmem_limit_bytes=V7X_VMEM_LIMIT_BYTES))(*shards)
    return list(outs)


BIG = (("w_proj_att", (ATT_WIDTH, D_MODEL), 1), ("w_proj_ssm", (SSM_WIDTH, D_MODEL), 1),
       ("w_glu", (SSM_WIDTH, SSM_WIDTH), 0))
DIRECT = (("w_in", True), ("w_up", True), ("w_down", False), ("w_out", False))
N_XY = 4


def _big_rows(shape):
    return shape[0] * shape[1] // N_XY // LANES


FLAT_ROWS = sum(_big_rows(s) for _, s, _ in BIG)


def _shard_shape(shape, axis):
    return (shape[0] // N_XY, shape[1]) if axis == 0 else (shape[0], shape[1] // N_XY)


def _flatten_shards(shards):
    return jnp.concatenate([shards[n].reshape(_big_rows(s), LANES) for n, s, _ in BIG], axis=0)


def _unflatten_shard(flat):
    out, r = {}, 0
    for n, s, ax in BIG:
        k = _big_rows(s)
        out[n] = flat[r:r + k].reshape(_shard_shape(s, ax))
        r += k
    return out


def _unflatten_full(flat4):
    out, r = {}, 0
    for n, s, ax in BIG:
        k = _big_rows(s)
        sh = _shard_shape(s, ax)
        t = flat4[:, r:r + k].reshape((N_XY,) + sh)
        out[n] = t.reshape(s) if ax == 0 else t.transpose(1, 0, 2).reshape(s)
        r += k
    return out


def _flatten_full(full):
    parts = []
    for n, s, ax in BIG:
        sh = _shard_shape(s, ax)
        t = full[n]
        t = t.reshape((N_XY,) + sh) if ax == 0 else t.reshape(s[0], N_XY, sh[1]).transpose(1, 0, 2)
        parts.append(t.reshape(N_XY, _big_rows(s), LANES))
    return jnp.concatenate(parts, axis=1)


def _lanes_from_groups(a):
    return a.transpose(2, 0, 1).reshape(SSM_GROUP_CH, SSM_LANES)


def _groups_from_lanes(a):
    return a.reshape(SSM_GROUP_CH, SSM_GROUPS, SSM_STATE).transpose(1, 2, 0)


LATE = ("w_up_t", "w_down", "w_out")
EARLY_GRADS = ("w_up_t", "w_down", "w_out")


def _local_step(x3, mod, tgt3, W, P, late_shards=None, scatter_grads=False):
    B, S, _ = x3.shape
    T = B * S
    seq_blocks = S // ATT_BLOCK
    sh1, sc1, gt1, sh2, sc2, gt2 = [m.reshape(B, 1, D_MODEL) for m in jnp.split(mod, 6, axis=-1)]
    g_mix, g_ffn, g_final = P["g_mix"].reshape(1, D_MODEL), P["g_ffn"].reshape(1, D_MODEL), P["g_final"].reshape(1, D_MODEL)
    b_gate = P["b_gate"].reshape(1, 2 * D_MODEL)
    d_skip, b_glu = P["d_skip"].reshape(1, SSM_WIDTH), P["b_glu"].reshape(1, SSM_WIDTH)
    w_conv, b_conv = P["w_conv"], P["b_conv"].reshape(1, D_FF)

    u1 = _norm_mod(x3, g_mix, sc1, sh1).reshape(T, D_MODEL)
    proj = _mm(u1, W["w_in_t"], tb=True, name="mm_proj", out_dtype=BF16)
    proj3 = proj.reshape(B, S, IN_WIDTH)
    us = proj[:, 3 * ATT_WIDTH:3 * ATT_WIDTH + SSM_WIDTH]
    o_att3, lse4, late = _attention_fwd(proj3, seq_blocks, _Riders(late_shards, "gather") if late_shards else None)
    if late_shards:
        W = dict(W, **{n: f.reshape(-1, LANES) for n, f in zip(LATE, late)})
        w_conv = late[len(LATE)].transpose(1, 0, 2).reshape(3, D_FF)
        W.update(_unflatten_full(late[len(LATE) + 1]))
    o_att = o_att3.reshape(T, ATT_WIDTH)
    y_att = _mm(o_att, W["w_proj_att"], name="mm_proj_att", out_dtype=BF16)

    lr = P["a_re"].reshape(1, SSM_LANES)
    li = P["a_im"].reshape(1, SSM_LANES)
    ldt = jnp.repeat(P["log_dt"], SSM_STATE).reshape(1, SSM_LANES)
    br, bi = _lanes_from_groups(P["b_re"]), _lanes_from_groups(P["b_im"])
    cr = P["c_re"].transpose(1, 0, 2).reshape(SSM_GROUP_CH, SSM_LANES)
    ci = P["c_im"].transpose(1, 0, 2).reshape(SSM_GROUP_CH, SSM_LANES)
    abar, w_bu, w_c = _ssm_params(lr, li, ldt, br, bi, cr, ci)
    xs3, y_core3 = _ssm_scan_fwd(proj3, abar, w_bu, w_c)
    y5, s_out = _ssm_post(y_core3.reshape(T, SSM_WIDTH), us, d_skip, W["w_glu"], b_glu)
    y_ssm = _mm(s_out, W["w_proj_ssm"], name="mm_proj_ssm", out_dtype=BF16)

    merged = _merge(proj, y_att, y_ssm, b_gate)
    mix = _mm(merged, W["w_out"], name="mm_out", out_dtype=BF16)
    mix3 = mix.reshape(B, S, D_MODEL)

    h1, u2 = _resid_norm_mod(x3, mix3, gt1, g_ffn, sc2, sh2)
    u2 = u2.reshape(T, D_MODEL)
    up3 = _mm(u2, W["w_up_t"], tb=True, name="mm_up", out_dtype=BF16).reshape(B, S, 2 * D_FF)
    act = _conv_act(up3, w_conv, b_conv).reshape(T, D_FF)
    ffn3 = _mm(act, W["w_down"], name="mm_down", out_dtype=BF16).reshape(B, S, D_MODEL)
    dh2, dffn, dgt2, dg_final, loss = _final_loss(h1, ffn3, tgt3, gt2, g_final)

    dffn = dffn.reshape(T, D_MODEL)
    gw = {}
    gw["w_down"] = _mm(act, dffn, ta=True, out_dtype=BF16, name="mm_dw_down")
    dact3 = _mm(dffn, W["w_down"], tb=True, name="mm_dact", out_dtype=BF16).reshape(B, S, D_FF)
    dup3, dw_conv, db_conv = _conv_bwd(up3, dact3, w_conv, b_conv)
    dup = dup3.reshape(2, T, D_FF)
    gw["w_up_t"] = _mm(dup, u2, ta=True, out_dtype=BF16, name="mm_dw_up")
    du2 = _mm(dup, W["w_up_t"], name="mm_du2", out_dtype=BF16).reshape(B, S, D_MODEL)
    dh1, dsh2, dsc2, dg_ffn, dgt1, dmix = _norm_bwd(h1, du2, dh2, g_ffn, sc2, "norm_bwd2", mix3=mix3, gt=gt1)

    dmix = dmix.reshape(T, D_MODEL)
    gw["w_out"] = _mm(merged, dmix, ta=True, out_dtype=BF16, name="mm_dw_out")
    dmerged = _mm(dmix, W["w_out"], tb=True, name="mm_dmerged", out_dtype=BF16)
    dy_att, dy_ssm, dga, dgs, db_att, db_ssm = _merge_bwd(proj, y_att, y_ssm, b_gate, dmerged)

    gw["w_proj_ssm"] = _mm(s_out, dy_ssm, ta=True, name="mm_dw_proj_ssm")
    ds_out = _mm(dy_ssm, W["w_proj_ssm"], tb=True, name="mm_ds_out")
    dy5, dd_skip, db_glu, dw_glu = _ssm_post_bwd(y5, us, ds_out, d_skip, W["w_glu"], b_glu)
    gw["w_glu"] = dw_glu
    dus3, dab, dwbu, dwc = _ssm_scan_bwd(proj3, dy5.reshape(B, S, SSM_WIDTH), xs3, abar, w_bu, w_c, d_skip)
    dus = dus3.reshape(T, SSM_WIDTH)
    dlr, dli, dldt, dbr, dbi, dcr, dci = _ssm_params_bwd(lr, li, ldt, br, bi, dab, dwbu, dwc)

    gw["w_proj_att"] = _mm(o_att, dy_att, ta=True, name="mm_dw_proj_att")
    do_att = _mm(dy_att, W["w_proj_att"], tb=True, out_dtype=BF16, name="mm_do_att")
    early = [gw[n].reshape(N_XY, -1, LANES) for n in EARLY_GRADS]
    early.append(_flatten_full({n: gw[n].astype(BF16) for n, _, _ in BIG}))
    dq3, dk3, dv3, parts = _attention_bwd(proj3, do_att.reshape(B, S, ATT_WIDTH), o_att3, lse4, seq_blocks,
                                          _Riders(early, "scatter") if scatter_grads else None)
    dproj = jnp.concatenate([t.reshape(T, ATT_WIDTH) for t in (dq3, dk3, dv3)] + [dus, dga, dgs], axis=1)
    dmods = [None, None, dgt1, dsh2, dsc2, dgt2]
    native = dict(b_att=db_att, b_ssm=db_ssm, a_re=dlr, a_im=dli, log_dt=dldt, b_re=dbr, b_im=dbi, c_re=dcr, c_im=dci,
                  d_skip=dd_skip, b_glu=db_glu, g_ffn=dg_ffn, w_conv=dw_conv, b_conv=db_conv, g_final=dg_final, loss=loss)
    small_early = _pack_small(native, dmods, False)
    if scatter_grads:
        gw["w_in_t"], (small_early,) = _mm(dproj, u1, ta=True, out_dtype=BF16, name="mm_dw_in",
                                           riders=_Riders([small_early], "gather", "all"))
        du1, last_parts = _mm(dproj, W["w_in_t"], name="mm_du1", out_dtype=BF16,
                              riders=_Riders([gw["w_in_t"].reshape(N_XY, -1, LANES)], "scatter"))
        parts = parts + last_parts
    else:
        gw["w_in_t"] = _mm(dproj, u1, ta=True, out_dtype=BF16, name="mm_dw_in")
        du1 = _mm(dproj, W["w_in_t"], name="mm_du1", out_dtype=BF16)
    du1 = du1.reshape(B, S, D_MODEL)
    dx, dsh1, dsc1, dg_mix = _norm_bwd(x3, du1, dh1, g_mix, sc1, "norm_bwd1")
    dmods[0], dmods[1] = dsh1, dsc1
    native["g_mix"] = dg_mix
    return loss, dx, dmods, gw, native, parts, small_early


WEIGHTS = ['w_ada', 'b_ada', 'g_mix', 'w_in', 'b_gate', 'a_re', 'a_im', 'log_dt', 'b_re', 'b_im', 'c_re', 'c_im', 'd_skip',
           'w_glu', 'b_glu', 'w_proj_att', 'w_proj_ssm', 'w_out', 'g_ffn', 'w_up', 'w_conv', 'b_conv', 'w_down', 'g_final']
SMALL = ['g_mix', 'b_gate', 'a_re', 'a_im', 'log_dt', 'b_re', 'b_im', 'c_re', 'c_im', 'd_skip', 'b_glu', 'g_ffn', 'w_conv',
         'b_conv', 'g_final']


def kernel(x, c, w_ada, b_ada, g_mix, w_in, b_gate, a_re, a_im, log_dt, b_re, b_im, c_re, c_im, d_skip, w_glu, b_glu, w_proj_att, w_proj_ssm, w_out, g_ffn, w_up, w_conv, b_conv, w_down, g_final, loss_target, m_w_ada, m_b_ada, m_g_mix, m_w_in, m_b_gate, m_a_re, m_a_im, m_log_dt, m_b_re, m_b_im, m_c_re, m_c_im, m_d_skip, m_w_glu, m_b_glu, m_w_proj_att, m_w_proj_ssm, m_w_out, m_g_ffn, m_w_up, m_w_conv, m_b_conv, m_w_down, m_g_final, v_w_ada, v_b_ada, v_g_mix, v_w_in, v_b_gate, v_a_re, v_a_im, v_log_dt, v_b_re, v_b_im, v_c_re, v_c_im, v_d_skip, v_w_glu, v_b_glu, v_w_proj_att, v_w_proj_ssm, v_w_out, v_g_ffn, v_w_up, v_w_conv, v_b_conv, v_w_down, v_g_final):
    args = dict(locals())
    w = {n: args[n] for n in WEIGHTS}
    m = {n: args["m_" + n] for n in WEIGHTS}
    v = {n: args["v_" + n] for n in WEIGHTS}
    B, S, _ = x.shape
    ix, iy, ic = lax.axis_index("x"), lax.axis_index("y"), lax.axis_index("c")
    chip = 2 * ix + iy
    ada_cols = w_ada.shape[2]

    c_all = _exchange(c, "all", "gather", "gather_c").reshape(8 * B, D_MODEL)
    b_cols = lax.dynamic_slice_in_dim(b_ada, chip * ada_cols, ada_cols, axis=1)
    mod_cols = _ada_fwd(c_all, w_ada[0], b_cols)
    mod_all = _exchange(mod_cols, "xy", "gather", "gather_mod")
    mod_all = mod_all.transpose(1, 0, 2).reshape(8 * B, 6 * D_MODEL)
    mod = lax.dynamic_slice_in_dim(mod_all, (4 * ix + 2 * iy + ic) * B, B, axis=0)

    shard = {n + ("_t" if t else ""): (w[n][0].T if t else w[n][0]).astype(BF16) for n, t in DIRECT}
    misc = _flatten_shards({n: w[n][0] for n, _, _ in BIG}).astype(BF16)
    (w_in_full,) = _gather_weights([shard["w_in_t"]], "gather_weights")
    W = {"w_in_t": w_in_full.reshape(-1, LANES)}

    P = {n: w[n][0] for n in SMALL if n not in ("w_conv", "g_final")}
    P["w_conv"] = None
    P["g_final"] = g_final

    loss, dx, dmods, gw, native, parts, small_early = _local_step(x, mod, loss_target, W, P,
                                                                  [shard[n] for n in LATE] + [w_conv[0], misc], True)

    small_late = _exchange(_pack_small(native, dmods, True), "all", "gather", "gather_small")
    native_sum, dmod_all = _sum_unpack_small(small_early, small_late, B)
    loss = native_sum["loss"][0, 0]
    g_small = _small_from_native(native_sum)
    dmod_all = dmod_all.reshape(8 * B, N_MOD * D_MODEL)
    dmod_cols = lax.dynamic_slice_in_dim(dmod_all, chip * ada_cols, ada_cols, axis=1)
    g_w_ada, g_b_ada = _ada_bwd(c_all, dmod_all, dmod_cols)

    red = [_sum_slots(p, "sum_chips_%d" % i) for i, p in enumerate(parts)]
    red_sib = _exchange_list(red, "c", "swap", "share_cores")
    order = list(EARLY_GRADS) + ["misc", "w_in_t"]
    halves = dict(zip(order, zip(red, red_sib)))

    grads = {"w_ada": g_w_ada[None], "b_ada": g_b_ada}
    grads["w_up"] = _add2(*halves["w_up_t"], F32, "add_cores_w_up").T[None]
    for k, gk in _unflatten_shard(_add2(*halves["misc"], F32, "add_cores_misc")).items():
        grads[k] = gk[None]
    wc_cols = w_conv.shape[2]
    for n in SMALL:
        g = g_small[n]
        if n == "w_conv":
            g = lax.dynamic_slice_in_dim(g, chip * wc_cols, wc_cols, axis=1)
        grads[n] = g.reshape(w[n].shape)

    delta, new_m, new_v = {}, {}, {}
    for n in ["w_ada"] + [b for b, _ in DIRECT] + [b for b, _, _ in BIG]:
        shp = w[n].shape
        if n == "w_in":
            r, s = halves["w_in_t"]
            d2, m2, v2, g2 = _adamw(w[n][0].T, r, m[n][0].T, v[n][0].T, "adamw_" + n, g_other=s)
            d2, m2, v2, grads[n] = d2.T, m2.T, v2.T, g2.T[None]
        elif n in ("w_down", "w_out"):
            r, s = halves[n]
            d2, m2, v2, g2 = _adamw(w[n][0], r, m[n][0], v[n][0], "adamw_" + n, g_other=s)
            grads[n] = g2[None]
        else:
            d2, m2, v2 = _adamw(w[n][0], grads[n][0], m[n][0], v[n][0], "adamw_" + n)
        delta[n], new_m[n], new_v[n] = d2.reshape(shp), m2.reshape(shp), v2.reshape(shp)
    rest = ["b_ada"] + SMALL

    def drop(a):
        return a.reshape(1, -1) if a.ndim == 1 else (a if a.ndim == 2 else a[0])

    upd = _adamw_multi([(drop(w[n]), drop(grads[n]), drop(m[n]), drop(v[n])) for n in rest])
    for n, (dd, mm, vv) in zip(rest, upd):
        delta[n], new_m[n], new_v[n] = dd.reshape(w[n].shape), mm.reshape(w[n].shape), vv.reshape(w[n].shape)

    return (loss, dx, *[grads[n] for n in WEIGHTS], *[delta[n] for n in WEIGHTS], *[new_m[n] for n in WEIGHTS],
            *[new_v[n] for n in WEIGHTS])
```

```python
import functools
import math

import jax
import jax.numpy as jnp
from jax import lax
from jax.experimental import pallas as pl
from jax.experimental.pallas import tpu as pltpu

F32, BF16 = jnp.float32, jnp.bfloat16

D_MODEL = 1024
N_HEADS = 8
HEAD_DIM = 64
ATT_WIDTH = 512
SSM_GROUPS = 16
SSM_GROUP_CH = 16
SSM_WIDTH = 256
SSM_STATE = 64
SSM_LANES = SSM_GROUPS * SSM_STATE
D_FF = 2048
IN_WIDTH = 3 * ATT_WIDTH + SSM_WIDTH + 2 * D_MODEL
ATT_BLOCK = 128
N_PATTERNS = 3
EPS = 1e-6
NEG_INF = -1e30

ADAM_LR, ADAM_B1, ADAM_B2, ADAM_EPS, ADAM_WD, ADAM_STEP = 0.001, 0.9, 0.999, 1e-08, 0.01, 10

V7X_VMEM_LIMIT_BYTES = 56 * 1024 * 1024
LANES = 1024


def _pcall(body, *, name, out_shape, grid=(), in_specs=None, out_specs=None, scratch_shapes=(), dims=None):
    params = dict(vmem_limit_bytes=V7X_VMEM_LIMIT_BYTES)
    if dims is not None:
        params["dimension_semantics"] = dims
    specs = {}
    if in_specs is not None:
        specs = dict(grid=grid, in_specs=in_specs, out_specs=out_specs)
    return pl.pallas_call(body, name=name, out_shape=out_shape, scratch_shapes=scratch_shapes,
                          compiler_params=pltpu.CompilerParams(**params), **specs)


def _sds(shape, dtype):
    return jax.ShapeDtypeStruct(tuple(shape), dtype)


def _tile(n, target):
    if n <= target:
        return n
    for t in range(target - target % 128, 0, -128):
        if n % t == 0:
            return t
    raise ValueError((n, target))


def _sig(v):
    return pl.reciprocal(1.0 + jnp.exp(-v), approx=True)


def _mm(a, b, *, name, ta=False, tb=False, out_dtype=F32, tm=2048, tn=1024, tk=1024, riders=None):
    halves = a.ndim == 3
    if halves:
        a_rows, a_cols = a.shape[1], 2 * a.shape[2]
    else:
        a_rows, a_cols = a.shape
    if ta:
        K, M = a_rows, a_cols
    else:
        M, K = a_rows, a_cols
    if tb:
        N, K2 = b.shape
    else:
        K2, N = b.shape
    assert K == K2, (a.shape, b.shape)
    if halves:
        tm, tk = (min(tm, M // 2), tk) if ta else (tm, min(tk, K // 2))
    tm, tn, tk = _tile(M, tm), _tile(N, tn), _tile(K, tk)
    nk = K // tk
    if halves and ta:
        per = a.shape[2] // tm
        a_spec = pl.BlockSpec((None, tk, tm), lambda i, j, k: (i // per, k, i % per))
    elif halves:
        per = a.shape[2] // tk
        a_spec = pl.BlockSpec((None, tm, tk), lambda i, j, k: (k // per, i, k % per))
    else:
        a_spec = pl.BlockSpec((tk, tm), lambda i, j, k: (k, i)) if ta else pl.BlockSpec((tm, tk), lambda i, j, k: (i, k))
    b_spec = pl.BlockSpec((tn, tk), lambda i, j, k: (j, k)) if tb else pl.BlockSpec((tk, tn), lambda i, j, k: (k, j))
    dn = (((0 if ta else 1,), (1 if tb else 0,)), ((), ()))

    def body(a_ref, b_ref, o_ref, acc_ref):
        k = pl.program_id(2)

        @pl.when(k == 0)
        def _():
            acc_ref[...] = jnp.zeros_like(acc_ref)

        acc_ref[...] += lax.dot_general(a_ref[...].astype(BF16), b_ref[...].astype(BF16), dn,
                                        preferred_element_type=F32)

        @pl.when(k == nk - 1)
        def _():
            o_ref[...] = acc_ref[...].astype(out_dtype)

    def body_single(a_ref, b_ref, o_ref):
        o_ref[...] = lax.dot_general(a_ref[...].astype(BF16), b_ref[...].astype(BF16), dn,
                                     preferred_element_type=F32).astype(out_dtype)

    grid = (M // tm, N // tn, nk)
    scratch = [] if nk == 1 else [pltpu.VMEM((tm, tn), F32)]
    o_spec = pl.BlockSpec((tm, tn), lambda i, j, k: (i, j))
    if riders is None:
        return _pcall(body_single if nk == 1 else body, name=name, out_shape=_sds((M, N), out_dtype), grid=grid,
                      in_specs=[a_spec, b_spec], out_specs=o_spec, scratch_shapes=scratch,
                      dims=("parallel", "parallel", "arbitrary"))(a, b)
    rs = riders
    res = _pcall(_with_riders(body_single if nk == 1 else body, rs, 2, 1, len(scratch), tuple(g - 1 for g in grid)),
                 name=name, out_shape=(_sds((M, N), out_dtype),) + tuple(rs.out_shape), grid=grid,
                 in_specs=[a_spec, b_spec] + rs.specs, out_specs=(o_spec,) + tuple(rs.specs),
                 scratch_shapes=scratch + rs.scratch, dims=("arbitrary", "arbitrary", "arbitrary"))(a, b, *rs.arrs)
    return res[0], list(res[1:])


def _ada_fwd(c_all, w_ada, b_ada_cols):
    n = w_ada.shape[1]

    def body(c_ref, w_ref, b_ref, o_ref):
        c = c_ref[...]
        act = c * _sig(c)
        o_ref[...] = jnp.dot(act.astype(BF16), w_ref[...].astype(BF16), preferred_element_type=F32) + b_ref[...]

    return _pcall(body, name="ada_fwd", out_shape=_sds((c_all.shape[0], n), F32))(c_all, w_ada, b_ada_cols)


def _ada_bwd(c_all, dmod_all, dmod_cols):
    n = dmod_cols.shape[1]

    def body(c_ref, da_ref, dc_ref, gw_ref, gb_ref):
        c = c_ref[...]
        act = c * _sig(c)
        gw_ref[...] = lax.dot_general(act, dc_ref[...], (((0,), (0,)), ((), ())), preferred_element_type=F32,
                                      precision=lax.Precision.HIGHEST)
        gb_ref[...] = jnp.sum(da_ref[...], axis=0, keepdims=True)

    return _pcall(body, name="ada_bwd", out_shape=(_sds((D_MODEL, n), F32), _sds((1, dmod_all.shape[1]), F32)))(
        c_all, dmod_all, dmod_cols)


ROW_TILE = 1024


def _row_specs(B, S):
    ts = min(S, ROW_TILE)
    row = pl.BlockSpec((1, ts, D_MODEL), lambda b, s: (b, s, 0))
    bvec = pl.BlockSpec((1, 1, D_MODEL), lambda b, s: (b, 0, 0))
    gvec = pl.BlockSpec((1, D_MODEL), lambda b, s: (0, 0))
    return ts, row, bvec, gvec


def _norm_mod(x3, g, sc, sh):
    B, S, _ = x3.shape
    ts, row, bvec, gvec = _row_specs(B, S)

    def body(x_ref, g_ref, sc_ref, sh_ref, u_ref):
        x = x_ref[0]
        r = lax.rsqrt(jnp.mean(x * x, axis=-1, keepdims=True) + EPS)
        u_ref[0] = ((x * r) * g_ref[...] * (1.0 + sc_ref[0]) + sh_ref[0]).astype(BF16)

    return _pcall(body, name="norm_mod1", out_shape=_sds(x3.shape, BF16), grid=(B, S // ts),
                  in_specs=[row, gvec, bvec, bvec], out_specs=row, dims=("parallel", "parallel"))(x3, g, sc, sh)


def _resid_norm_mod(x3, mix3, gt, g, sc, sh):
    B, S, _ = x3.shape
    ts, row, bvec, gvec = _row_specs(B, S)

    def body(x_ref, m_ref, gt_ref, g_ref, sc_ref, sh_ref, h_ref, u_ref):
        h = x_ref[0] + gt_ref[0] * m_ref[0]
        h_ref[0] = h
        r = lax.rsqrt(jnp.mean(h * h, axis=-1, keepdims=True) + EPS)
        u_ref[0] = ((h * r) * g_ref[...] * (1.0 + sc_ref[0]) + sh_ref[0]).astype(BF16)

    return _pcall(body, name="resid_norm_mod2", out_shape=(_sds(x3.shape, F32), _sds(x3.shape, BF16)),
                  grid=(B, S // ts), in_specs=[row, row, bvec, gvec, bvec, bvec], out_specs=(row, row),
                  dims=("parallel", "parallel"))(x3, mix3, gt, g, sc, sh)


def _norm_bwd(h3, du3, dres3, g, sc, name, mix3=None, gt=None):
    B, S, _ = h3.shape
    ts, row, bvec, gvec = _row_specs(B, S)
    with_gate = mix3 is not None

    def body(*refs):
        if with_gate:
            h_ref, du_ref, dr_ref, g_ref, sc_ref, m_ref, gt_ref, dh_ref, dsh_ref, dsc_ref, dg_ref, dgt_ref, dm_ref = refs
        else:
            h_ref, du_ref, dr_ref, g_ref, sc_ref, dh_ref, dsh_ref, dsc_ref, dg_ref = refs
        b, s = pl.program_id(0), pl.program_id(1)
        h = h_ref[0]
        r = lax.rsqrt(jnp.mean(h * h, axis=-1, keepdims=True) + EPS)
        xn = h * r
        du = du_ref[0].astype(F32)
        g = g_ref[...]
        sc1 = 1.0 + sc_ref[0]
        dxn = du * g * sc1
        dh = dr_ref[0].astype(F32) + r * (dxn - xn * jnp.mean(dxn * xn, axis=-1, keepdims=True))
        dh_ref[0] = dh.astype(dh_ref.dtype)

        @pl.when(s == 0)
        def _():
            dsh_ref[...] = jnp.zeros_like(dsh_ref)
            dsc_ref[...] = jnp.zeros_like(dsc_ref)
            if with_gate:
                dgt_ref[...] = jnp.zeros_like(dgt_ref)

        @pl.when((s == 0) & (b == 0))
        def _():
            dg_ref[...] = jnp.zeros_like(dg_ref)

        dux = du * xn
        dsh_ref[0] += jnp.sum(du, axis=0, keepdims=True)
        dsc_ref[0] += jnp.sum(dux * g, axis=0, keepdims=True)
        dg_ref[...] += jnp.sum(dux * sc1, axis=0, keepdims=True)
        if with_gate:
            dgt_ref[0] += jnp.sum(dh * m_ref[0], axis=0, keepdims=True)
            dm_ref[0] = (dh * gt_ref[0]).astype(BF16)

    bshape = _sds((B, 1, D_MODEL), F32)
    in_specs = [row, row, row, gvec, bvec]
    out_shape = [_sds(h3.shape, BF16 if with_gate else F32), bshape, bshape, _sds((1, D_MODEL), F32)]
    out_specs = [row, bvec, bvec, gvec]
    args = [h3, du3, dres3, g, sc]
    if with_gate:
        in_specs += [row, bvec]
        out_shape += [bshape, _sds(h3.shape, BF16)]
        out_specs += [bvec, row]
        args += [mix3, gt]
    return _pcall(body, name=name, out_shape=tuple(out_shape), grid=(B, S // ts), in_specs=in_specs,
                  out_specs=tuple(out_specs), dims=("arbitrary", "arbitrary"))(*args)


def _final_loss(h1, ffn3, tgt3, gt, gfin):
    B, S, _ = h1.shape
    ts, row, bvec, gvec = _row_specs(B, S)
    one = pl.BlockSpec((1, 1), lambda b, s: (0, 0))

    def body(h_ref, f_ref, t_ref, gt_ref, gf_ref, dh_ref, dff_ref, dgt_ref, dgf_ref, loss_ref):
        b, s = pl.program_id(0), pl.program_id(1)
        f = f_ref[0].astype(F32)
        gtv = gt_ref[0]
        gf = gf_ref[...]
        h2 = h_ref[0] + gtv * f
        r = lax.rsqrt(jnp.mean(h2 * h2, axis=-1, keepdims=True) + EPS)
        n = h2 * r
        e = n * gf - t_ref[0]
        dy = e * (1.0 / D_MODEL)
        dn = dy * gf
        dh2 = r * (dn - n * jnp.mean(dn * n, axis=-1, keepdims=True))
        dh_ref[0] = dh2.astype(BF16)
        dff_ref[0] = (dh2 * gtv).astype(BF16)

        @pl.when(s == 0)
        def _():
            dgt_ref[...] = jnp.zeros_like(dgt_ref)

        @pl.when((s == 0) & (b == 0))
        def _():
            dgf_ref[...] = jnp.zeros_like(dgf_ref)
            loss_ref[...] = jnp.zeros_like(loss_ref)

        dgt_ref[0] += jnp.sum(dh2 * f, axis=0, keepdims=True)
        dgf_ref[...] += jnp.sum(dy * n, axis=0, keepdims=True)
        rows = jnp.sum(e * e, axis=1, keepdims=True)
        loss_ref[...] += jnp.sum(rows, axis=0, keepdims=True) * (0.5 / D_MODEL)

    return _pcall(body, name="final_loss",
                  out_shape=(_sds(h1.shape, BF16), _sds(h1.shape, BF16), _sds((B, 1, D_MODEL), F32),
                             _sds((1, D_MODEL), F32), _sds((1, 1), F32)),
                  grid=(B, S // ts), in_specs=[row, row, row, bvec, gvec], out_specs=(row, row, bvec, gvec, one),
                  dims=("arbitrary", "arbitrary"))(h1, ffn3, tgt3, gt, gfin)


ATT_GROUP = 4
ATT_GW = ATT_GROUP * HEAD_DIM
ATT_GROUPS = N_HEADS // ATT_GROUP
ATT_PAIRS = ATT_GW // ATT_BLOCK
ATT_UNROLL = 5
ATT_RESIDUE_UNROLL = 4
NT_DIMS = (((1,), (1,)), ((), ()))
TN_DIMS = (((0,), (0,)), ((), ()))


def _att_rows(start, d):
    if d == 1:
        return pl.ds(start if isinstance(start, int) else pl.multiple_of(start, ATT_BLOCK), ATT_BLOCK)
    return pl.ds(start, ATT_BLOCK, stride=d)


def _att_fill_bias(bias_ref, g, d):
    a = lax.broadcasted_iota(jnp.int32, (ATT_BLOCK, ATT_BLOCK), 0)
    j = lax.broadcasted_iota(jnp.int32, (ATT_BLOCK, ATT_BLOCK), 1)
    dist = (a - j).astype(F32)
    for hh in range(ATT_GROUP):
        t, e = divmod(hh, 2)
        rs = slice(e * ATT_BLOCK, (e + 1) * ATT_BLOCK)
        lo = 2.0 ** (-8.0 * (hh + 1) / N_HEADS) * d
        hi = 2.0 ** (-8.0 * (ATT_GROUP + hh + 1) / N_HEADS) * d
        slope = jnp.where(g == 0, lo, hi).astype(F32)
        bias_ref[t, rs, 0:ATT_BLOCK] = jnp.where(a >= j, -slope * dist, NEG_INF)
        bias_ref[t, rs, ATT_BLOCK:] = jnp.where(j >= a, -slope * (dist + float(ATT_BLOCK)), NEG_INF)


def _stack_heads(v2, low):
    return jnp.concatenate([jnp.where(low, v2, 0.0), jnp.where(low, 0.0, v2)], axis=0).astype(BF16)


def _unstack_heads(r2, low):
    return jnp.where(low, r2[0:ATT_BLOCK], r2[ATT_BLOCK:])


class _Riders:
    def __init__(self, arrs, mode, group="xy"):
        self.arrs, self.mode, self.n, self.group = list(arrs), mode, len(arrs), group
        slot_shapes = [a.shape if mode == "gather" else a.shape[1:] for a in self.arrs]
        self.out_shape = [_sds((_GROUP_SLOTS[group],) + s, a.dtype) for s, a in zip(slot_shapes, self.arrs)]
        k = len(_GROUP_MASKS[group])
        self.scratch = [pltpu.SemaphoreType.DMA((k * self.n,)), pltpu.SemaphoreType.DMA((k * self.n,)),
                        pltpu.SemaphoreType.DMA((2 * self.n,))] + [pltpu.VMEM(s, a.dtype) for s, a in zip(slot_shapes, self.arrs)]
        self.specs = [pl.BlockSpec(memory_space=pl.ANY)] * self.n

    def _remote(self, x_refs, o_refs, send_sems, recv_sems):
        x, y, c = lax.axis_index("x"), lax.axis_index("y"), lax.axis_index("c")
        me = _group_slot(self.group, x, y, c)
        masks = _GROUP_MASKS[self.group]
        cps = []
        for i in range(self.n):
            for k, (dx, dy, dc) in enumerate(masks):
                px, py, pc = _flip(x, dx), _flip(y, dy), _flip(c, dc)
                src = x_refs[i] if self.mode == "gather" else x_refs[i].at[_group_slot(self.group, px, py, pc)]
                cps.append(pltpu.make_async_remote_copy(
                    src_ref=src, dst_ref=o_refs[i].at[me], send_sem=send_sems.at[len(masks) * i + k],
                    recv_sem=recv_sems.at[len(masks) * i + k], device_id=(px, py, pc),
                    device_id_type=pl.DeviceIdType.MESH))
        return cps, me

    def start(self, x_refs, o_refs, scratch):
        send_sems, recv_sems, local_sems, bufs = scratch[0], scratch[1], scratch[2], scratch[3:]
        cps, me = self._remote(x_refs, o_refs, send_sems, recv_sems)
        for cp in cps:
            cp.start()
        for i in range(self.n):
            src = x_refs[i] if self.mode == "gather" else x_refs[i].at[me]
            load = pltpu.make_async_copy(src, bufs[i], local_sems.at[2 * i])
            load.start()
            load.wait()
            pltpu.make_async_copy(bufs[i], o_refs[i].at[me], local_sems.at[2 * i + 1]).start()

    def wait(self, x_refs, o_refs, scratch):
        send_sems, recv_sems, local_sems, bufs = scratch[0], scratch[1], scratch[2], scratch[3:]
        cps, me = self._remote(x_refs, o_refs, send_sems, recv_sems)
        for cp in cps:
            cp.wait()
        for i in range(self.n):
            pltpu.make_async_copy(bufs[i], o_refs[i].at[me], local_sems.at[2 * i + 1]).wait()


def _with_riders(compute, riders, n_in, n_out, n_scratch, last_step):
    if riders is None:
        return compute
    n = riders.n

    def body(*refs):
        ins, x_refs = refs[:n_in], refs[n_in:n_in + n]
        outs, o_refs = refs[n_in + n:n_in + n + n_out], refs[n_in + n + n_out:n_in + 2 * n + n_out]
        scratch = refs[n_in + 2 * n + n_out:]
        own, ride = scratch[:n_scratch], scratch[n_scratch:]
        ids = [pl.program_id(i) for i in range(len(last_step))]
        first = functools.reduce(jnp.logical_and, [i == 0 for i in ids])
        last = functools.reduce(jnp.logical_and, [i == l for i, l in zip(ids, last_step)])

        @pl.when(first)
        def _():
            riders.start(x_refs, o_refs, ride)

        compute(*ins, *outs, *own)

        @pl.when(last)
        def _():
            riders.wait(x_refs, o_refs, ride)

    return body


def _attention_fwd(proj3, seq_blocks, riders=None):
    B, S, _ = proj3.shape
    scale = HEAD_DIM ** -0.5
    nq = ATT_WIDTH // ATT_GW

    def col(k):
        return pl.BlockSpec((1, S, ATT_GW), lambda b, g, k=k: (b, 0, k * nq + g))

    o_spec = pl.BlockSpec((1, S, ATT_GW), lambda b, g: (b, 0, g))
    l_spec = pl.BlockSpec((1, 1, S, ATT_BLOCK), lambda b, g: (b, g, 0, 0))

    def compute(q_ref, k_ref, v_ref, o_ref, lse_ref, qf, kf, vf, os, ls, bias):
        g = pl.program_id(1)
        for t in range(ATT_PAIRS):
            ts = slice(t * ATT_BLOCK, (t + 1) * ATT_BLOCK)
            qf[t] = q_ref[0, :, ts].astype(F32) * scale
            kf[t] = k_ref[0, :, ts].astype(F32)
            vf[t] = v_ref[0, :, ts].astype(F32)
        lane = lax.broadcasted_iota(jnp.int32, (ATT_BLOCK, ATT_BLOCK), 1)
        low = lane < HEAD_DIM

        def block(p, d, r, n, has_prev):
            start = n * (ATT_BLOCK * d) + r
            rows = _att_rows(start, d)
            prows = _att_rows(start - ATT_BLOCK * d, d) if has_prev else None
            lse_t = jnp.zeros((ATT_BLOCK, ATT_BLOCK), F32)
            for t in range(ATT_PAIRS):
                q2 = _stack_heads(qf[t, rows, :], low)
                k2 = kf[t, rows, :].astype(BF16)
                v2 = vf[t, rows, :].astype(BF16)
                if has_prev:
                    k2 = jnp.concatenate([k2, kf[t, prows, :].astype(BF16)], axis=0)
                    v2 = jnp.concatenate([v2, vf[t, prows, :].astype(BF16)], axis=0)
                    b2 = bias[t]
                else:
                    b2 = bias[t, :, 0:ATT_BLOCK]
                s = lax.dot_general(q2, k2, NT_DIMS, preferred_element_type=F32) + b2
                m = jnp.max(s, axis=1, keepdims=True)
                pr = jnp.exp(s - m)
                den = jnp.sum(pr, axis=1, keepdims=True)
                o = jnp.dot(pr.astype(BF16), v2, preferred_element_type=F32) * (1.0 / den)
                os[p, t, rows, :] = _unstack_heads(o, low)
                lse2 = m + jnp.log(den)
                lse_t = jnp.where(lane == 2 * t, lse2[0:ATT_BLOCK], lse_t)
                lse_t = jnp.where(lane == 2 * t + 1, lse2[ATT_BLOCK:], lse_t)
            ls[p, rows, :] = lse_t

        for p in range(N_PATTERNS):
            d = 4 ** p
            _att_fill_bias(bias, g, d)
            _att_one_pattern(block, p, d, seq_blocks // d)

        def combine(i, carry):
            rows = pl.ds(pl.multiple_of(i * ATT_BLOCK, ATT_BLOCK), ATT_BLOCK)
            l0, l1, l2 = ls[0, rows, :], ls[1, rows, :], ls[2, rows, :]
            m = jnp.maximum(jnp.maximum(l0, l1), l2)
            lse = m + jnp.log(jnp.exp(l0 - m) + jnp.exp(l1 - m) + jnp.exp(l2 - m))
            lse_ref[0, 0, rows, :] = lse
            w = [jnp.exp(l0 - lse), jnp.exp(l1 - lse), jnp.exp(l2 - lse)]
            for t in range(ATT_PAIRS):
                acc = jnp.zeros((ATT_BLOCK, ATT_BLOCK), F32)
                for p in range(N_PATTERNS):
                    wt = jnp.where(low, w[p][:, 2 * t:2 * t + 1], w[p][:, 2 * t + 1:2 * t + 2])
                    acc = acc + wt * os[p, t, rows, :]
                o_ref[0, rows, t * ATT_BLOCK:(t + 1) * ATT_BLOCK] = acc.astype(BF16)
            return carry

        lax.fori_loop(0, S // ATT_BLOCK, combine, 0, unroll=2)

    scratch = ([pltpu.VMEM((ATT_PAIRS, S, ATT_BLOCK), F32)] * 3
               + [pltpu.VMEM((N_PATTERNS, ATT_PAIRS, S, ATT_BLOCK), F32), pltpu.VMEM((N_PATTERNS, S, ATT_BLOCK), F32),
                  pltpu.VMEM((ATT_PAIRS, 2 * ATT_BLOCK, 2 * ATT_BLOCK), F32)])
    rs = riders
    res = _pcall(_with_riders(compute, rs, 3, 2, len(scratch), (B - 1, ATT_GROUPS - 1)), name="attention_fwd",
                 out_shape=(_sds((B, S, ATT_WIDTH), BF16), _sds((B, ATT_GROUPS, S, ATT_BLOCK), F32))
                 + (tuple(rs.out_shape) if rs else ()),
                 grid=(B, ATT_GROUPS), in_specs=[col(0), col(1), col(2)] + (rs.specs if rs else []),
                 out_specs=(o_spec, l_spec) + (tuple(rs.specs) if rs else ()),
                 scratch_shapes=scratch + (rs.scratch if rs else []),
                 dims=("arbitrary", "arbitrary"))(proj3, proj3, proj3, *(rs.arrs if rs else []))
    return res[0], res[1], list(res[2:])


def _att_one_pattern(block, p, d, nb):
    def per_residue(r, carry):
        block(p, d, r, 0, False)
        if nb > 1:
            def per_block(n, c2):
                block(p, d, r, n, True)
                return c2
            lax.fori_loop(1, nb, per_block, 0, unroll=ATT_UNROLL if (nb - 1) % ATT_UNROLL == 0 else nb - 1)
        return carry

    if d == 1:
        per_residue(0, 0)
    else:
        lax.fori_loop(0, d, per_residue, 0, unroll=ATT_RESIDUE_UNROLL if nb == 1 else 1)


def _attention_bwd(proj3, do3, o3, lse4, seq_blocks, riders=None):
    B, S, _ = proj3.shape
    scale = HEAD_DIM ** -0.5
    nq = ATT_WIDTH // ATT_GW

    def col(k):
        return pl.BlockSpec((1, S, ATT_GW), lambda b, g, k=k: (b, 0, k * nq + g))

    o_spec = pl.BlockSpec((1, S, ATT_GW), lambda b, g: (b, 0, g))
    l_spec = pl.BlockSpec((1, 1, S, ATT_BLOCK), lambda b, g: (b, g, 0, 0))

    def compute(q_ref, k_ref, v_ref, do_ref, o_ref, lse_ref, dq_ref, dk_ref, dv_ref,
                qf, kf, vf, dof, dl, aq, ak, av, bias):
        g = pl.program_id(1)
        for t in range(ATT_PAIRS):
            ts = slice(t * ATT_BLOCK, (t + 1) * ATT_BLOCK)
            qf[t] = q_ref[0, :, ts].astype(F32) * scale
            kf[t] = k_ref[0, :, ts].astype(F32)
            vf[t] = v_ref[0, :, ts].astype(F32)
            dof[t] = do_ref[0, :, ts].astype(F32)
        aq[...] = jnp.zeros_like(aq)
        ak[...] = jnp.zeros_like(ak)
        av[...] = jnp.zeros_like(av)
        lane = lax.broadcasted_iota(jnp.int32, (ATT_BLOCK, ATT_BLOCK), 1)
        low = lane < HEAD_DIM

        def fill_delta(i, carry):
            rows = pl.ds(pl.multiple_of(i * ATT_BLOCK, ATT_BLOCK), ATT_BLOCK)
            acc = jnp.zeros((ATT_BLOCK, ATT_BLOCK), F32)
            for t in range(ATT_PAIRS):
                prod = dof[t, rows, :] * o_ref[0, rows, t * ATT_BLOCK:(t + 1) * ATT_BLOCK].astype(F32)
                lo = jnp.sum(jnp.where(low, prod, 0.0), axis=1, keepdims=True)
                hi = jnp.sum(prod, axis=1, keepdims=True) - lo
                acc = jnp.where(lane == 2 * t, lo, acc)
                acc = jnp.where(lane == 2 * t + 1, hi, acc)
            dl[rows, :] = acc
            return carry

        lax.fori_loop(0, S // ATT_BLOCK, fill_delta, 0, unroll=2)

        def block(p, d, r, n, has_prev):
            start = n * (ATT_BLOCK * d) + r
            rows = _att_rows(start, d)
            prows = _att_rows(start - ATT_BLOCK * d, d) if has_prev else None
            lse_t = lse_ref[0, 0, rows, :]
            dl_t = dl[rows, :]
            for t in range(ATT_PAIRS):
                q2 = _stack_heads(qf[t, rows, :], low)
                do2 = _stack_heads(dof[t, rows, :], low)
                k2 = kf[t, rows, :].astype(BF16)
                v2 = vf[t, rows, :].astype(BF16)
                if has_prev:
                    k2 = jnp.concatenate([k2, kf[t, prows, :].astype(BF16)], axis=0)
                    v2 = jnp.concatenate([v2, vf[t, prows, :].astype(BF16)], axis=0)
                    b2 = bias[t]
                else:
                    b2 = bias[t, :, 0:ATT_BLOCK]
                lse2 = jnp.concatenate([lse_t[:, 2 * t:2 * t + 1], lse_t[:, 2 * t + 1:2 * t + 2]], axis=0)
                dl2 = jnp.concatenate([dl_t[:, 2 * t:2 * t + 1], dl_t[:, 2 * t + 1:2 * t + 2]], axis=0)
                s = lax.dot_general(q2, k2, NT_DIMS, preferred_element_type=F32) + b2
                pr = jnp.exp(s - lse2)
                ds = (pr * (lax.dot_general(do2, v2, NT_DIMS, preferred_element_type=F32) - dl2)).astype(BF16)
                dq = _unstack_heads(jnp.dot(ds, k2, preferred_element_type=F32), low)
                dk = lax.dot_general(ds, q2, TN_DIMS, preferred_element_type=F32)
                dv = lax.dot_general(pr.astype(BF16), do2, TN_DIMS, preferred_element_type=F32)
                aq[t, rows, :] = aq[t, rows, :] + dq * scale
                ak[t, rows, :] = ak[t, rows, :] + dk[0:ATT_BLOCK]
                av[t, rows, :] = av[t, rows, :] + dv[0:ATT_BLOCK]
                if has_prev:
                    ak[t, prows, :] = ak[t, prows, :] + dk[ATT_BLOCK:]
                    av[t, prows, :] = av[t, prows, :] + dv[ATT_BLOCK:]

        for p in range(N_PATTERNS):
            d = 4 ** p
            _att_fill_bias(bias, g, d)
            _att_one_pattern(block, p, d, seq_blocks // d)

        for t in range(ATT_PAIRS):
            ts = slice(t * ATT_BLOCK, (t + 1) * ATT_BLOCK)
            dq_ref[0, :, ts] = aq[t].astype(BF16)
            dk_ref[0, :, ts] = ak[t].astype(BF16)
            dv_ref[0, :, ts] = av[t].astype(BF16)

    shp = _sds((B, S, ATT_WIDTH), BF16)
    pair_buf = pltpu.VMEM((ATT_PAIRS, S, ATT_BLOCK), F32)
    scratch = ([pair_buf] * 4 + [pltpu.VMEM((S, ATT_BLOCK), F32)] + [pair_buf] * 3
               + [pltpu.VMEM((ATT_PAIRS, 2 * ATT_BLOCK, 2 * ATT_BLOCK), F32)])
    rs = riders
    res = _pcall(_with_riders(compute, rs, 6, 3, len(scratch), (B - 1, ATT_GROUPS - 1)), name="attention_bwd",
                 out_shape=(shp, shp, shp) + (tuple(rs.out_shape) if rs else ()), grid=(B, ATT_GROUPS),
                 in_specs=[col(0), col(1), col(2), o_spec, o_spec, l_spec] + (rs.specs if rs else []),
                 out_specs=(o_spec, o_spec, o_spec) + (tuple(rs.specs) if rs else ()),
                 scratch_shapes=scratch + (rs.scratch if rs else []),
                 dims=("arbitrary", "arbitrary"))(proj3, proj3, proj3, do3, o3, lse4, *(rs.arrs if rs else []))
    return res[0], res[1], res[2], list(res[3:])


def _expand_groups(m):
    rows = SSM_WIDTH
    t = jnp.concatenate([m] * SSM_GROUPS, axis=0)
    r = lax.broadcasted_iota(jnp.int32, (rows, SSM_LANES), 0)
    l = lax.broadcasted_iota(jnp.int32, (rows, SSM_LANES), 1)
    keep = lax.shift_right_logical(r, 4) == lax.shift_right_logical(l, 6)
    return jnp.where(keep, t, 0.0)


def _collapse_groups(m):
    rows = SSM_WIDTH
    r = lax.broadcasted_iota(jnp.int32, (rows, SSM_LANES), 0)
    l = lax.broadcasted_iota(jnp.int32, (rows, SSM_LANES), 1)
    keep = lax.shift_right_logical(r, 4) == lax.shift_right_logical(l, 6)
    t = jnp.where(keep, m, 0.0)
    acc = t[0:SSM_GROUP_CH]
    for g in range(1, SSM_GROUPS):
        acc = acc + t[g * SSM_GROUP_CH:(g + 1) * SSM_GROUP_CH]
    return acc


def _zoh(lr, li, ldt):
    dt = jnp.exp(ldt)
    mag = jnp.exp(lr * dt)
    ang = li * dt
    cs, sn = jnp.cos(ang), jnp.sin(ang)
    ab_re, ab_im = mag * cs, mag * sn
    nr, ni = ab_re - 1.0, ab_im
    den = lr * lr + li * li
    n_re = nr * lr + ni * li
    n_im = ni * lr - nr * li
    return dict(dt=dt, mag=mag, cs=cs, sn=sn, ab_re=ab_re, ab_im=ab_im, nr=nr, ni=ni, den=den, n_re=n_re, n_im=n_im,
                f_re=n_re / den, f_im=n_im / den)


def _ssm_params(lr, li, ldt, br, bi, cr, ci):
    def body(lr_ref, li_ref, ldt_ref, br_ref, bi_ref, cr_ref, ci_ref, ab_ref, w_ref, c_ref):
        z = _zoh(lr_ref[...], li_ref[...], ldt_ref[...])
        ab_ref[0:1, :] = z["ab_re"]
        ab_ref[1:2, :] = z["ab_im"]
        br, bi = br_ref[...], bi_ref[...]
        w_ref[:, 0:SSM_LANES] = _expand_groups(z["f_re"] * br - z["f_im"] * bi).astype(BF16)
        w_ref[:, SSM_LANES:] = _expand_groups(z["f_re"] * bi + z["f_im"] * br).astype(BF16)
        c_ref[:, 0:SSM_LANES] = _expand_groups(cr_ref[...]).astype(BF16)
        c_ref[:, SSM_LANES:] = _expand_groups(-ci_ref[...]).astype(BF16)

    return _pcall(body, name="ssm_params",
                  out_shape=(_sds((2, SSM_LANES), F32), _sds((SSM_WIDTH, 2 * SSM_LANES), BF16),
                             _sds((SSM_WIDTH, 2 * SSM_LANES), BF16)))(lr, li, ldt, br, bi, cr, ci)


def _ssm_params_bwd(lr, li, ldt, br, bi, dab, dw, dc):
    def body(lr_ref, li_ref, ldt_ref, br_ref, bi_ref, dab_ref, dw_ref, dc_ref,
             dlr_ref, dli_ref, dldt_ref, dbr_ref, dbi_ref, dcr_ref, dci_ref):
        lr, li = lr_ref[...], li_ref[...]
        z = _zoh(lr, li, ldt_ref[...])
        br, bi = br_ref[...], bi_ref[...]
        dbb_re = _collapse_groups(dw_ref[:, 0:SSM_LANES])
        dbb_im = _collapse_groups(dw_ref[:, SSM_LANES:])
        dcr_ref[...] = _collapse_groups(dc_ref[:, 0:SSM_LANES])
        dci_ref[...] = -_collapse_groups(dc_ref[:, SSM_LANES:])
        f_re, f_im = z["f_re"], z["f_im"]
        dbr_ref[...] = f_re * dbb_re + f_im * dbb_im
        dbi_ref[...] = f_re * dbb_im - f_im * dbb_re
        df_re = jnp.sum(dbb_re * br + dbb_im * bi, axis=0, keepdims=True)
        df_im = jnp.sum(dbb_im * br - dbb_re * bi, axis=0, keepdims=True)
        den = z["den"]
        dn_re, dn_im = df_re / den, df_im / den
        dden = -(df_re * z["n_re"] + df_im * z["n_im"]) / (den * den)
        dnr = dn_re * lr - dn_im * li
        dni = dn_re * li + dn_im * lr
        dlr = dn_re * z["nr"] + dn_im * z["ni"] + 2.0 * dden * lr
        dli = dn_re * z["ni"] - dn_im * z["nr"] + 2.0 * dden * li
        dab_re = dab_ref[0:1, :] + dnr
        dab_im = dab_ref[1:2, :] + dni
        mag, cs, sn, dt = z["mag"], z["cs"], z["sn"], z["dt"]
        dmag = dab_re * cs + dab_im * sn
        dang = mag * (dab_im * cs - dab_re * sn)
        dlr_ref[...] = dlr + dmag * mag * dt
        dli_ref[...] = dli + dang * dt
        ddt = dmag * mag * lr + dang * li
        per_lane = jnp.broadcast_to(ddt * dt, (8, SSM_LANES))
        lane = lax.broadcasted_iota(jnp.int32, (SSM_LANES, 128), 0)
        col = lax.broadcasted_iota(jnp.int32, (SSM_LANES, 128), 1)
        ind = jnp.where(lax.shift_right_logical(lane, 6) == col, 1.0, 0.0)
        dldt_ref[...] = jnp.dot(per_lane, ind, preferred_element_type=F32, precision=lax.Precision.HIGHEST)[0:1]

    vec = _sds((1, SSM_LANES), F32)
    mat = _sds((SSM_GROUP_CH, SSM_LANES), F32)
    return _pcall(body, name="ssm_params_bwd", out_shape=(vec, vec, _sds((1, 128), F32), mat, mat, mat, mat))(
        lr, li, ldt, br, bi, dab, dw, dc)


SCAN_CHUNK = 512


def _scan_consts(ar, ai, k_ref, reverse):
    row = lax.broadcasted_iota(jnp.int32, (8, SSM_LANES), 0)
    pw = [(ar, ai)]
    for _ in range(7):
        pr, pi = pw[-1]
        pw.append((pr * ar - pi * ai, pr * ai + pi * ar))
    for n, k in enumerate((1, 2, 4)):
        keep = (row < 8 - k) if reverse else (row >= k)
        k_ref[2 * n] = jnp.where(keep, jnp.broadcast_to(pw[k - 1][0], (8, SSM_LANES)), 0.0)
        k_ref[2 * n + 1] = jnp.where(keep, jnp.broadcast_to(pw[k - 1][1], (8, SSM_LANES)), 0.0)
    cr = jnp.zeros((8, SSM_LANES), F32)
    ci = jnp.zeros((8, SSM_LANES), F32)
    for r in range(8):
        e = (8 - r) if reverse else (r + 1)
        cr = jnp.where(row == r, jnp.broadcast_to(pw[e - 1][0], (8, SSM_LANES)), cr)
        ci = jnp.where(row == r, jnp.broadcast_to(pw[e - 1][1], (8, SSM_LANES)), ci)
    k_ref[6] = cr
    k_ref[7] = ci


def _scan_tile(xr, xi, k_ref, car, cai, reverse):
    for n, k in enumerate((1, 2, 4)):
        sh = (8 - k) if reverse else k
        sr = pltpu.roll(xr, sh, 0)
        si = pltpu.roll(xi, sh, 0)
        mr, mi = k_ref[2 * n], k_ref[2 * n + 1]
        xr, xi = xr + mr * sr - mi * si, xi + mr * si + mi * sr
    pr, pi = k_ref[6], k_ref[7]
    xr, xi = xr + pr * car - pi * cai, xi + pr * cai + pi * car
    return xr, xi


US_BLOCK = (3 * ATT_WIDTH) // SSM_WIDTH


def _ssm_scan_fwd(proj3, abar, w_bu, w_c):
    B, S, _ = proj3.shape
    ch = min(S, SCAN_CHUNK)
    u_spec = pl.BlockSpec((1, ch, SSM_WIDTH), lambda b, c: (b, c, US_BLOCK))
    x_spec = pl.BlockSpec((1, ch, 2 * SSM_LANES), lambda b, c: (b, c, 0))
    y_spec = pl.BlockSpec((1, ch, SSM_WIDTH), lambda b, c: (b, c, 0))
    w_spec = pl.BlockSpec((SSM_WIDTH, 2 * SSM_LANES), lambda b, c: (0, 0))

    def body(ab_ref, u_ref, wb_ref, wc_ref, x_ref, y_ref, k_ref, carry_ref):
        _scan_consts(ab_ref[0:1, :], ab_ref[1:2, :], k_ref, False)

        @pl.when(pl.program_id(1) == 0)
        def _():
            carry_ref[...] = jnp.zeros_like(carry_ref)

        x_ref[0] = jnp.dot(u_ref[0], wb_ref[...], preferred_element_type=F32)

        def step(i, carry):
            base = pl.multiple_of(i * 8, 8)
            xr = x_ref[0, pl.ds(base, 8), 0:SSM_LANES]
            xi = x_ref[0, pl.ds(base, 8), SSM_LANES:]
            xr, xi = _scan_tile(xr, xi, k_ref, carry[0], carry[1], False)
            x_ref[0, pl.ds(base, 8), 0:SSM_LANES] = xr
            x_ref[0, pl.ds(base, 8), SSM_LANES:] = xi
            return (jnp.broadcast_to(xr[7:8], (8, SSM_LANES)), jnp.broadcast_to(xi[7:8], (8, SSM_LANES)))

        cr, ci = lax.fori_loop(0, ch // 8, step, (carry_ref[0], carry_ref[1]))
        carry_ref[0] = cr
        carry_ref[1] = ci
        y_ref[0] = lax.dot_general(x_ref[0].astype(BF16), wc_ref[...], NT_DIMS, preferred_element_type=F32)

    return _pcall(body, name="ssm_scan_fwd",
                  out_shape=(_sds((B, S, 2 * SSM_LANES), F32), _sds((B, S, SSM_WIDTH), F32)), grid=(B, S // ch),
                  in_specs=[pl.BlockSpec((2, SSM_LANES), lambda b, c: (0, 0)), u_spec, w_spec, w_spec],
                  out_specs=(x_spec, y_spec),
                  scratch_shapes=[pltpu.VMEM((8, 8, SSM_LANES), F32), pltpu.VMEM((2, 8, SSM_LANES), F32)],
                  dims=("arbitrary", "arbitrary"))(abar, proj3, w_bu, w_c)


def _ssm_scan_bwd(proj3, dy3, xs3, abar, w_bu, w_c, dsk):
    B, S, _ = proj3.shape
    ch = min(S, SCAN_CHUNK)
    nc = S // ch
    u_spec = pl.BlockSpec((1, ch, SSM_WIDTH), lambda b, c: (b, nc - 1 - c, US_BLOCK))
    x_spec = pl.BlockSpec((1, ch, 2 * SSM_LANES), lambda b, c: (b, nc - 1 - c, 0))
    y_spec = pl.BlockSpec((1, ch, SSM_WIDTH), lambda b, c: (b, nc - 1 - c, 0))
    w_spec = pl.BlockSpec((SSM_WIDTH, 2 * SSM_LANES), lambda b, c: (0, 0))
    ab_spec = pl.BlockSpec((2, SSM_LANES), lambda b, c: (0, 0))
    d_spec = pl.BlockSpec((1, SSM_WIDTH), lambda b, c: (0, 0))

    def body(ab_ref, u_ref, dy_ref, xs_ref, wb_ref, wc_ref, d_ref, du_ref, da_ref, dwb_ref, dwc_ref,
             g_ref, k_ref, carry_ref, acc_ref):
        b, c = pl.program_id(0), pl.program_id(1)
        _scan_consts(ab_ref[0:1, :], -ab_ref[1:2, :], k_ref, True)
        row = lax.broadcasted_iota(jnp.int32, (8, SSM_LANES), 0)

        @pl.when(c == 0)
        def _():
            carry_ref[...] = jnp.zeros_like(carry_ref)

        @pl.when((c == 0) & (b == 0))
        def _():
            acc_ref[...] = jnp.zeros_like(acc_ref)
            dwb_ref[...] = jnp.zeros_like(dwb_ref)
            dwc_ref[...] = jnp.zeros_like(dwc_ref)

        dy = dy_ref[0]
        dyb = dy.astype(BF16)
        g_ref[...] = jnp.dot(dyb, wc_ref[...], preferred_element_type=F32)

        def step(i, carry):
            car, cai, ar_acc, ai_acc = carry
            base = pl.multiple_of((ch // 8 - 1 - i) * 8, 8)
            gr = g_ref[pl.ds(base, 8), 0:SSM_LANES]
            gi = g_ref[pl.ds(base, 8), SSM_LANES:]
            gr, gi = _scan_tile(gr, gi, k_ref, car, cai, True)
            g_ref[pl.ds(base, 8), 0:SSM_LANES] = gr
            g_ref[pl.ds(base, 8), SSM_LANES:] = gi
            nr = jnp.where(row == 7, car, pltpu.roll(gr, 7, 0))
            ni = jnp.where(row == 7, cai, pltpu.roll(gi, 7, 0))
            xr = xs_ref[0, pl.ds(base, 8), 0:SSM_LANES]
            xi = xs_ref[0, pl.ds(base, 8), SSM_LANES:]
            ar_acc = ar_acc + nr * xr + ni * xi
            ai_acc = ai_acc + ni * xr - nr * xi
            return (jnp.broadcast_to(gr[0:1], (8, SSM_LANES)), jnp.broadcast_to(gi[0:1], (8, SSM_LANES)), ar_acc, ai_acc)

        cr, ci, ar_acc, ai_acc = lax.fori_loop(0, ch // 8, step, (carry_ref[0], carry_ref[1], acc_ref[0], acc_ref[1]))
        carry_ref[0] = cr
        carry_ref[1] = ci
        acc_ref[0] = ar_acc
        acc_ref[1] = ai_acc
        da_ref[0:1, :] = jnp.sum(ar_acc, axis=0, keepdims=True)
        da_ref[1:2, :] = jnp.sum(ai_acc, axis=0, keepdims=True)

        gb = g_ref[...].astype(BF16)
        du = lax.dot_general(gb, wb_ref[...], NT_DIMS, preferred_element_type=F32) + d_ref[...] * dy
        du_ref[0] = du.astype(BF16)
        xb = xs_ref[0].astype(BF16)
        u = u_ref[0]
        for j in range(2 * SSM_LANES // SSM_WIDTH):
            rows = slice((j % (SSM_LANES // SSM_WIDTH)) * 64, (j % (SSM_LANES // SSM_WIDTH)) * 64 + 64)
            cols = slice(j * SSM_WIDTH, (j + 1) * SSM_WIDTH)
            dwb_ref[rows, cols] += lax.dot_general(u[:, rows], gb[:, cols], TN_DIMS, preferred_element_type=F32)
            dwc_ref[rows, cols] += lax.dot_general(dyb[:, rows], xb[:, cols], TN_DIMS, preferred_element_type=F32)

    mat = _sds((SSM_WIDTH, 2 * SSM_LANES), F32)
    return _pcall(body, name="ssm_scan_bwd",
                  out_shape=(_sds((B, S, SSM_WIDTH), BF16), _sds((2, SSM_LANES), F32), mat, mat), grid=(B, nc),
                  in_specs=[ab_spec, u_spec, y_spec, x_spec, w_spec, w_spec, d_spec],
                  out_specs=(y_spec, ab_spec, w_spec, w_spec),
                  scratch_shapes=[pltpu.VMEM((ch, 2 * SSM_LANES), F32), pltpu.VMEM((8, 8, SSM_LANES), F32),
                                  pltpu.VMEM((2, 8, SSM_LANES), F32), pltpu.VMEM((2, 8, SSM_LANES), F32)],
                  dims=("arbitrary", "arbitrary"))(abar, proj3, dy3, xs3, w_bu, w_c, dsk)


GELU_K = math.sqrt(2.0 / math.pi)
GELU_C = 0.044715


def _gelu_parts(y):
    t = jnp.tanh(GELU_K * (y + GELU_C * y * y * y))
    return 0.5 * y * (1.0 + t), t


def _ssm_post(yc, us, dsk, wglu, bglu):
    T, N = yc.shape
    tm = min(T, 1024)
    row = pl.BlockSpec((tm, N), lambda i: (i, 0))
    vec = pl.BlockSpec((1, N), lambda i: (0, 0))
    mat = pl.BlockSpec((N, N), lambda i: (0, 0))

    def body(yc_ref, us_ref, d_ref, w_ref, b_ref, y_ref, s_ref):
        y = yc_ref[...] + d_ref[...] * us_ref[...]
        y_ref[...] = y
        z, _ = _gelu_parts(y)
        gl = jnp.dot(z.astype(BF16), w_ref[...], preferred_element_type=F32) + b_ref[...]
        s_ref[...] = (z * _sig(gl)).astype(BF16)

    return _pcall(body, name="ssm_post", out_shape=(_sds((T, N), F32), _sds((T, N), BF16)), grid=(T // tm,),
                  in_specs=[row, row, vec, mat, vec], out_specs=(row, row), dims=("parallel",))(yc, us, dsk, wglu, bglu)


def _ssm_post_bwd(y5, us, ds, dsk, wglu, bglu):
    T, N = y5.shape
    tm = min(T, 1024)
    row = pl.BlockSpec((tm, N), lambda i: (i, 0))
    vec = pl.BlockSpec((1, N), lambda i: (0, 0))
    mat = pl.BlockSpec((N, N), lambda i: (0, 0))

    def body(y_ref, us_ref, ds_ref, d_ref, w_ref, b_ref, dy_ref, dd_ref, db_ref, dw_ref):
        @pl.when(pl.program_id(0) == 0)
        def _():
            dd_ref[...] = jnp.zeros_like(dd_ref)
            db_ref[...] = jnp.zeros_like(db_ref)
            dw_ref[...] = jnp.zeros_like(dw_ref)

        y = y_ref[...]
        z, t = _gelu_parts(y)
        zb = z.astype(BF16)
        gl = jnp.dot(zb, w_ref[...], preferred_element_type=F32) + b_ref[...]
        sg = _sig(gl)
        ds = ds_ref[...]
        dgl = ds * z * sg * (1.0 - sg)
        dglb = dgl.astype(BF16)
        dz = ds * sg + lax.dot_general(dglb, w_ref[...], (((1,), (1,)), ((), ())), preferred_element_type=F32)
        dgelu = 0.5 * (1.0 + t) + 0.5 * y * (1.0 - t * t) * GELU_K * (1.0 + 3.0 * GELU_C * y * y)
        dy = dz * dgelu
        dy_ref[...] = dy
        dd_ref[...] += jnp.sum(dy * us_ref[...], axis=0, keepdims=True)
        db_ref[...] += jnp.sum(dgl, axis=0, keepdims=True)
        dw_ref[...] += lax.dot_general(zb, dglb, (((0,), (0,)), ((), ())), preferred_element_type=F32)

    return _pcall(body, name="ssm_post_bwd",
                  out_shape=(_sds((T, N), F32), _sds((1, N), F32), _sds((1, N), F32), _sds((N, N), F32)),
                  grid=(T // tm,), in_specs=[row, row, row, vec, mat, vec], out_specs=(row, vec, vec, mat),
                  dims=("arbitrary",))(y5, us, ds, dsk, wglu, bglu)


GATE_TILE = 256
GATE_ATT_BLOCK0 = (3 * ATT_WIDTH + SSM_WIDTH) // GATE_TILE
GATE_SSM_BLOCK0 = (3 * ATT_WIDTH + SSM_WIDTH + D_MODEL) // GATE_TILE


def _merge(proj, y_att, y_ssm, b_gate):
    T = proj.shape[0]
    tm = min(T, 2048)
    nj = D_MODEL // GATE_TILE
    ga = pl.BlockSpec((tm, GATE_TILE), lambda i, j: (i, GATE_ATT_BLOCK0 + j))
    gs = pl.BlockSpec((tm, GATE_TILE), lambda i, j: (i, GATE_SSM_BLOCK0 + j))
    yy = pl.BlockSpec((tm, GATE_TILE), lambda i, j: (i, j))
    ba = pl.BlockSpec((1, GATE_TILE), lambda i, j: (0, j))
    bs = pl.BlockSpec((1, GATE_TILE), lambda i, j: (0, nj + j))

    def body(ga_ref, gs_ref, ya_ref, ys_ref, ba_ref, bs_ref, o_ref):
        o_ref[...] = (_sig(ga_ref[...] + ba_ref[...]) * ya_ref[...]
                      + _sig(gs_ref[...] + bs_ref[...]) * ys_ref[...]).astype(BF16)

    return _pcall(body, name="merge", out_shape=_sds((T, D_MODEL), BF16), grid=(T // tm, nj),
                  in_specs=[ga, gs, yy, yy, ba, bs], out_specs=yy, dims=("parallel", "parallel"))(
        proj, proj, y_att, y_ssm, b_gate, b_gate)


def _merge_bwd(proj, y_att, y_ssm, b_gate, dmerged):
    T = proj.shape[0]
    tm = min(T, 2048)
    nj = D_MODEL // GATE_TILE
    ga = pl.BlockSpec((tm, GATE_TILE), lambda j, i: (i, GATE_ATT_BLOCK0 + j))
    gs = pl.BlockSpec((tm, GATE_TILE), lambda j, i: (i, GATE_SSM_BLOCK0 + j))
    yy = pl.BlockSpec((tm, GATE_TILE), lambda j, i: (i, j))
    ba = pl.BlockSpec((1, GATE_TILE), lambda j, i: (0, j))
    bs = pl.BlockSpec((1, GATE_TILE), lambda j, i: (0, nj + j))

    def body(ga_ref, gs_ref, ya_ref, ys_ref, ba_ref, bs_ref, dm_ref, dya_ref, dys_ref, dga_ref, dgs_ref, dba_ref, dbs_ref):
        @pl.when(pl.program_id(1) == 0)
        def _():
            dba_ref[...] = jnp.zeros_like(dba_ref)
            dbs_ref[...] = jnp.zeros_like(dbs_ref)

        dm = dm_ref[...].astype(F32)
        sa = _sig(ga_ref[...] + ba_ref[...])
        ss = _sig(gs_ref[...] + bs_ref[...])
        dya_ref[...] = (dm * sa).astype(BF16)
        dys_ref[...] = (dm * ss).astype(BF16)
        dga = dm * ya_ref[...] * sa * (1.0 - sa)
        dgs = dm * ys_ref[...] * ss * (1.0 - ss)
        dga_ref[...] = dga.astype(BF16)
        dgs_ref[...] = dgs.astype(BF16)
        dba_ref[...] += jnp.sum(dga, axis=0, keepdims=True)
        dbs_ref[...] += jnp.sum(dgs, axis=0, keepdims=True)

    big = _sds((T, D_MODEL), BF16)
    vec = _sds((1, D_MODEL), F32)
    return _pcall(body, name="merge_bwd", out_shape=(big, big, big, big, vec, vec), grid=(nj, T // tm),
                  in_specs=[ga, gs, yy, yy, ba, bs, yy], out_specs=(yy, yy, yy, yy, ba, ba),
                  dims=("arbitrary", "arbitrary"))(proj, proj, y_att, y_ssm, b_gate, b_gate, dmerged)


CONV_TILE = 256


def _shift_rows(a, j, up=False):
    n = a.shape[0]
    r = pltpu.roll(a, n - j if up else j, 0)
    row = lax.broadcasted_iota(jnp.int32, (8, a.shape[1]), 0)
    if up:
        return jnp.concatenate([r[:n - 8], jnp.where(row < 8 - j, r[n - 8:], 0.0)], axis=0)
    return jnp.concatenate([jnp.where(row >= j, r[:8], 0.0), r[8:]], axis=0)


def _conv_pre(a, w_ref, b_ref):
    conv = b_ref[...] + w_ref[0:1, :] * a
    shifted = []
    for j in (1, 2):
        sh = _shift_rows(a, j)
        shifted.append(sh)
        conv = conv + w_ref[j:j + 1, :] * sh
    return conv, shifted


def _conv_act(up3, w_conv, b_conv):
    B, S, _ = up3.shape
    nj = D_FF // CONV_TILE
    a_spec = pl.BlockSpec((1, S, CONV_TILE), lambda b, j: (b, 0, j))
    v_spec = pl.BlockSpec((1, S, CONV_TILE), lambda b, j: (b, 0, nj + j))
    w_spec = pl.BlockSpec((3, CONV_TILE), lambda b, j: (0, j))
    b_spec = pl.BlockSpec((1, CONV_TILE), lambda b, j: (0, j))

    def body(a_ref, v_ref, w_ref, b_ref, o_ref):
        a = a_ref[0].astype(F32)
        conv, _ = _conv_pre(a, w_ref, b_ref)
        o_ref[0] = (conv * _sig(conv) * v_ref[0]).astype(BF16)

    return _pcall(body, name="conv_act", out_shape=_sds((B, S, D_FF), BF16), grid=(B, nj),
                  in_specs=[a_spec, v_spec, w_spec, b_spec], out_specs=a_spec, dims=("parallel", "parallel"))(
        up3, up3, w_conv, b_conv)


def _conv_bwd(up3, dact3, w_conv, b_conv):
    B, S, _ = up3.shape
    nj = D_FF // CONV_TILE
    a_spec = pl.BlockSpec((1, S, CONV_TILE), lambda j, b: (b, 0, j))
    v_spec = pl.BlockSpec((1, S, CONV_TILE), lambda j, b: (b, 0, nj + j))
    o_spec = pl.BlockSpec((2, 1, S, CONV_TILE), lambda j, b: (0, b, 0, j))
    w_spec = pl.BlockSpec((3, CONV_TILE), lambda j, b: (0, j))
    b_spec = pl.BlockSpec((1, CONV_TILE), lambda j, b: (0, j))

    def body(a_ref, v_ref, d_ref, w_ref, b_ref, dup_ref, dw_ref, db_ref):
        @pl.when(pl.program_id(1) == 0)
        def _():
            dw_ref[...] = jnp.zeros_like(dw_ref)
            db_ref[...] = jnp.zeros_like(db_ref)

        a = a_ref[0].astype(F32)
        d = d_ref[0].astype(F32)
        conv, shifted = _conv_pre(a, w_ref, b_ref)
        sg = _sig(conv)
        dup_ref[1, 0] = (d * conv * sg).astype(BF16)
        dconv = d * v_ref[0] * (sg * (1.0 + conv * (1.0 - sg)))
        da = w_ref[0:1, :] * dconv
        for j in (1, 2):
            da = da + w_ref[j:j + 1, :] * _shift_rows(dconv, j, up=True)
        dup_ref[0, 0] = da.astype(BF16)
        db_ref[...] += jnp.sum(dconv, axis=0, keepdims=True)
        dw_ref[0:1, :] += jnp.sum(dconv * a, axis=0, keepdims=True)
        dw_ref[1:2, :] += jnp.sum(dconv * shifted[0], axis=0, keepdims=True)
        dw_ref[2:3, :] += jnp.sum(dconv * shifted[1], axis=0, keepdims=True)

    return _pcall(body, name="conv_bwd",
                  out_shape=(_sds((2, B, S, D_FF), BF16), _sds((3, D_FF), F32), _sds((1, D_FF), F32)),
                  grid=(nj, B), in_specs=[a_spec, v_spec, a_spec, w_spec, b_spec],
                  out_specs=(o_spec, w_spec, b_spec), dims=("arbitrary", "arbitrary"))(up3, up3, dact3, w_conv, b_conv)


def _rows_tile(r, cap=640):
    for t in range(min(r, cap) - min(r, cap) % 8, 7, -8):
        if r % t == 0:
            return t
    return r


def _add2(a, b, out_dtype, name):
    R, N = a.shape
    tr = _rows_tile(R)
    spec = pl.BlockSpec((tr, N), lambda i: (i, 0))

    def body(a_ref, b_ref, o_ref):
        o_ref[...] = (a_ref[...] + b_ref[...]).astype(out_dtype)

    return _pcall(body, name=name, out_shape=_sds((R, N), out_dtype), grid=(R // tr,), in_specs=[spec, spec],
                  out_specs=spec, dims=("parallel",))(a, b)


def _sum_slots(q, name):
    n, R, N = q.shape
    tr = _rows_tile(R)

    def body(q_ref, o_ref):
        acc = q_ref[0].astype(F32)
        for s in range(1, n):
            acc = acc + q_ref[s].astype(F32)
        o_ref[...] = acc

    return _pcall(body, name=name, out_shape=_sds((R, N), F32), grid=(R // tr,),
                  in_specs=[pl.BlockSpec((n, tr, N), lambda i: (0, i, 0))], out_specs=pl.BlockSpec((tr, N), lambda i: (i, 0)),
                  dims=("parallel",))(q)


NATIVE = (("b_re", 16, 1024), ("b_im", 16, 1024), ("c_re", 16, 1024), ("c_im", 16, 1024), ("g_mix", 1, 1024),
          ("b_att", 1, 1024), ("b_ssm", 1, 1024), ("a_re", 1, 1024), ("a_im", 1, 1024), ("log_dt", 1, 128),
          ("d_skip", 1, 256), ("b_glu", 1, 256), ("g_ffn", 1, 1024), ("g_final", 1, 1024), ("b_conv", 1, 2048),
          ("w_conv", 3, 2048), ("loss", 1, 1))
N_MOD = 6
NATIVE_LATE = ("g_mix",)
MODS_LATE = (0, 1)


def _small_plan(late):
    pieces = [p for p in NATIVE if (p[0] in NATIVE_LATE) == late]
    mods = [k for k in range(N_MOD) if (k in MODS_LATE) == late]
    starts, r = {}, 0
    for name, rows, cols in pieces:
        starts[name] = r
        r += rows * (-(-cols // LANES))
    return pieces, mods, starts, -(-r // 8) * 8


def _pack_small(native, dmods, late):
    pieces, mods, starts, n_sum = _small_plan(late)
    B = dmods[mods[0]].shape[0]
    total = n_sum + 8 * len(mods)

    def body(*refs):
        xs, ms, o_ref = refs[:len(pieces)], refs[len(pieces):-1], refs[-1]
        o_ref[...] = jnp.zeros_like(o_ref)
        for (name, rows, cols), x_ref in zip(pieces, xs):
            chunks = -(-cols // LANES)
            if chunks == 1 and rows % 8 == 0:
                o_ref[starts[name]:starts[name] + rows, 0:cols] = x_ref[...]
                continue
            for i in range(rows):
                for q in range(chunks):
                    wd = min(LANES, cols - q * LANES)
                    r = starts[name] + i * chunks + q
                    o_ref[r:r + 1, 0:wd] = x_ref[i:i + 1, q * LANES:q * LANES + wd]
        for k, m_ref in enumerate(ms):
            for b in range(B):
                o_ref[n_sum + 8 * k + b:n_sum + 8 * k + b + 1, :] = m_ref[b]

    return _pcall(body, name="pack_small_late" if late else "pack_small_early", out_shape=_sds((total, LANES), F32))(
        *[native[n] for n, _, _ in pieces], *[dmods[k] for k in mods])


def _sum_unpack_small(gathered_early, gathered_late, B):
    plans = [_small_plan(False), _small_plan(True)]
    nd = gathered_early.shape[0]
    n_out = len(NATIVE)

    def body(*refs):
        g_refs, outs, dm_ref, accs = refs[0:2], refs[2:2 + n_out], refs[2 + n_out], refs[3 + n_out:]
        o = 0
        for g_ref, acc, (pieces, mods, starts, n_sum) in zip(g_refs, accs, plans):
            s = g_ref[0, 0:n_sum, :]
            for d in range(1, nd):
                s = s + g_ref[d, 0:n_sum, :]
            acc[...] = s
            for name, rows, cols in pieces:
                o_ref = outs[o]
                o += 1
                chunks = -(-cols // LANES)
                if chunks == 1 and rows % 8 == 0:
                    o_ref[...] = acc[starts[name]:starts[name] + rows, 0:cols]
                    continue
                for i in range(rows):
                    for q in range(chunks):
                        wd = min(LANES, cols - q * LANES)
                        r = starts[name] + i * chunks + q
                        o_ref[i:i + 1, q * LANES:q * LANES + wd] = acc[r:r + 1, 0:wd]
            for d in range(nd):
                for j, k in enumerate(mods):
                    dm_ref[d, :, k * D_MODEL:(k + 1) * D_MODEL] = g_ref[d, n_sum + 8 * j:n_sum + 8 * j + B, :]

    ordered = [p for pieces, _, _, _ in plans for p in pieces]
    out_shape = tuple(_sds((rows, cols), F32) for _, rows, cols in ordered) + (_sds((nd, B, N_MOD * D_MODEL), F32),)
    res = _pcall(body, name="sum_unpack_small", out_shape=out_shape,
                 scratch_shapes=[pltpu.VMEM((n_sum, LANES), F32) for _, _, _, n_sum in plans])(gathered_early, gathered_late)
    return {n: r for (n, _, _), r in zip(ordered, res[:-1])}, res[-1]


def _small_from_native(nat):
    lanes3 = lambda a: a.reshape(SSM_GROUP_CH, SSM_GROUPS, SSM_STATE)
    return dict(
        g_mix=nat["g_mix"].reshape(D_MODEL), b_gate=jnp.concatenate([nat["b_att"], nat["b_ssm"]], axis=1).reshape(2 * D_MODEL),
        a_re=nat["a_re"].reshape(SSM_GROUPS, SSM_STATE), a_im=nat["a_im"].reshape(SSM_GROUPS, SSM_STATE),
        log_dt=nat["log_dt"][0, :SSM_GROUPS], b_re=_groups_from_lanes(nat["b_re"]), b_im=_groups_from_lanes(nat["b_im"]),
        c_re=lanes3(nat["c_re"]).transpose(1, 0, 2), c_im=lanes3(nat["c_im"]).transpose(1, 0, 2),
        d_skip=nat["d_skip"].reshape(SSM_WIDTH), b_glu=nat["b_glu"].reshape(SSM_WIDTH), g_ffn=nat["g_ffn"].reshape(D_MODEL),
        w_conv=nat["w_conv"], b_conv=nat["b_conv"].reshape(D_FF), g_final=nat["g_final"].reshape(D_MODEL))


def _adamw_multi(params):
    n = len(params)
    bc1 = 1.0 - ADAM_B1 ** ADAM_STEP
    bc2 = 1.0 - ADAM_B2 ** ADAM_STEP

    def body(*refs):
        ins, outs = refs[:4 * n], refs[4 * n:]
        for i in range(n):
            w_ref, g_ref, m_ref, v_ref = ins[4 * i:4 * i + 4]
            d_ref, nm_ref, nv_ref = outs[3 * i:3 * i + 3]
            g = g_ref[...]
            m = ADAM_B1 * m_ref[...] + (1.0 - ADAM_B1) * g
            v = ADAM_B2 * v_ref[...] + (1.0 - ADAM_B2) * (g * g)
            nm_ref[...] = m
            nv_ref[...] = v
            d_ref[...] = -ADAM_LR * ((m / bc1) / (jnp.sqrt(v / bc2) + ADAM_EPS) + ADAM_WD * w_ref[...])

    flat = [a for p in params for a in p]
    out_shape = tuple(_sds(p[0].shape, F32) for p in params for _ in range(3))
    res = _pcall(body, name="adamw_small", out_shape=out_shape)(*flat)
    return [tuple(res[3 * i:3 * i + 3]) for i in range(n)]


def _adamw(w, g, m, v, name, g_other=None):
    R, N = w.shape
    tr = _rows_tile(R, 256)
    spec = pl.BlockSpec((tr, N), lambda i: (i, 0))
    bc1 = 1.0 - ADAM_B1 ** ADAM_STEP
    bc2 = 1.0 - ADAM_B2 ** ADAM_STEP
    two = g_other is not None

    def body(*refs):
        w_ref, g_ref, m_ref, v_ref = refs[:4]
        d_ref, nm_ref, nv_ref = refs[4 + two:7 + two]
        g = g_ref[...]
        if two:
            g = g + refs[4][...]
            refs[8][...] = g
        m = ADAM_B1 * m_ref[...] + (1.0 - ADAM_B1) * g
        v = ADAM_B2 * v_ref[...] + (1.0 - ADAM_B2) * (g * g)
        nm_ref[...] = m
        nv_ref[...] = v
        d_ref[...] = -ADAM_LR * ((m / bc1) / (jnp.sqrt(v / bc2) + ADAM_EPS) + ADAM_WD * w_ref[...])

    shp = _sds((R, N), F32)
    args = (w, g, m, v) + ((g_other,) if two else ())
    return _pcall(body, name=name, out_shape=(shp,) * (3 + two), grid=(R // tr,), in_specs=[spec] * len(args),
                  out_specs=(spec,) * (3 + two), dims=("parallel",))(*args)


_GROUP_MASKS = {
    "all": [(dx, dy, dc) for dx in (0, 1) for dy in (0, 1) for dc in (0, 1) if (dx, dy, dc) != (0, 0, 0)],
    "xy": [(1, 0, 0), (0, 1, 0), (1, 1, 0)],
    "c": [(0, 0, 1)],
}
_GROUP_SLOTS = {"all": 8, "xy": 4, "c": 2}


def _group_slot(group, x, y, c):
    return {"all": 4 * x + 2 * y + c, "xy": 2 * x + y, "c": c}[group]


def _flip(v, d):
    return 1 - v if d else v


def _exchange(arr, group, mode, name):
    return _exchange_list([arr], group, mode, name)[0]


def _exchange_list(arrs, group, mode, name):
    masks = _GROUP_MASKS[group]
    n = len(masks)
    na = len(arrs)
    assert mode in ("gather", "swap") and (mode == "gather" or group == "c")
    has_local = mode == "gather"
    out_shapes = [((_GROUP_SLOTS[group],) if has_local else ()) + arr.shape for arr in arrs]
    bounce = [pltpu.VMEM(arr.shape, arr.dtype) for arr in arrs] if has_local else []

    def body(*refs):
        x_refs, o_refs = refs[:na], refs[na:2 * na]
        send_sems, recv_sems = refs[2 * na], refs[2 * na + 1]
        x, y, c = lax.axis_index("x"), lax.axis_index("y"), lax.axis_index("c")
        me = _group_slot(group, x, y, c)
        if has_local:
            local_sems = refs[2 * na + 2]
            bufs = refs[2 * na + 3:]
            loads = []
            for i in range(na):
                loads.append(pltpu.make_async_copy(x_refs[i], bufs[i], local_sems.at[2 * i]))
                loads[-1].start()
        copies = []
        for i in range(na):
            x_ref, o_ref = x_refs[i], o_refs[i]
            for k, (dx, dy, dc) in enumerate(masks):
                px, py, pc = _flip(x, dx), _flip(y, dy), _flip(c, dc)
                src, dst = (x_ref, o_ref.at[me]) if has_local else (x_ref, o_ref)
                cp =pltpu.make_async_remote_copy(src_ref=src, dst_ref=dst, send_sem=send_sems.at[i * n + k],
                                                  recv_sem=recv_sems.at[i * n + k], device_id=(px, py, pc),
                                                  device_id_type=pl.DeviceIdType.MESH)
                cp.start()
                copies.append(cp)
        if has_local:
            stores = []
            for i in range(na):
                loads[i].wait()
                stores.append(pltpu.make_async_copy(bufs[i], o_refs[i].at[me], local_sems.at[2 * i + 1]))
                stores[-1].start()
        for cp in copies:
            cp.wait()
        if has_local:
            for st in stores:
                st.wait()

    anyspec = pl.BlockSpec(memory_space=pl.ANY)
    scratch = [pltpu.SemaphoreType.DMA((n * na,)), pltpu.SemaphoreType.DMA((n * na,))]
    if has_local:
        scratch += [pltpu.SemaphoreType.DMA((2 * na,))] + bounce
    outs = pl.pallas_call(body, name=name, out_shape=tuple(_sds(s, a.dtype) for s, a in zip(out_shapes, arrs)),
                          in_specs=[anyspec] * na, out_specs=tuple([anyspec] * na), scratch_shapes=scratch,
                          compiler_params=pltpu.CompilerParams(vmem_limit_bytes=V7X_VMEM_LIMIT_BYTES))(*arrs)
    return list(outs)


def _gather_weights(shards, name):
    na = len(shards)
    masks = _GROUP_MASKS["xy"]
    n = len(masks)

    def body(*refs):
        x_refs, o_refs = refs[:na], refs[na:2 * na]
        send_sems, recv_sems, local_sems = refs[2 * na:2 * na + 3]
        bufs = refs[2 * na + 3:]
        x, y, c = lax.axis_index("x"), lax.axis_index("y"), lax.axis_index("c")
        me = 2 * x + y
        sibling = (x, y, 1 - c)
        loads = []
        for i in range(na):
            loads.append(pltpu.make_async_copy(x_refs[i], bufs[i], local_sems.at[2 * i]))
            loads[-1].start()

        def half_of(i, slot, cc):
            h = shards[i].shape[0] // 2
            return o_refs[i].at[slot, pl.ds(pl.multiple_of(cc * h, 8), h), :]

        def src_half(i, cc):
            h = shards[i].shape[0] // 2
            return x_refs[i].at[pl.ds(pl.multiple_of(cc * h, 8), h), :]

        sends = []
        for i in range(na):
            for k, (dx, dy, _) in enumerate(masks):
                cp = pltpu.make_async_remote_copy(src_ref=src_half(i, c), dst_ref=half_of(i, me, c),
                                                  send_sem=send_sems.at[i * 2 * n + k], recv_sem=recv_sems.at[i * 2 * n + k],
                                                  device_id=(_flip(x, dx), _flip(y, dy), c),
                                                  device_id_type=pl.DeviceIdType.MESH)
                cp.start()
                sends.append(cp)
        stores = []
        for i in range(na):
            loads[i].wait()
            stores.append(pltpu.make_async_copy(bufs[i], o_refs[i].at[me], local_sems.at[2 * i + 1]))
            stores[-1].start()
        for i in range(na):
            for k, (dx, dy, _) in enumerate(masks):
                slot = 2 * _flip(x, dx) + _flip(y, dy)
                landed = pltpu.make_async_remote_copy(src_ref=src_half(i, c), dst_ref=half_of(i, slot, c),
                                                      send_sem=send_sems.at[i * 2 * n + k],
                                                      recv_sem=recv_sems.at[i * 2 * n + k], device_id=sibling,
                                                      device_id_type=pl.DeviceIdType.MESH)
                landed.wait_recv()
                fwd = pltpu.make_async_remote_copy(src_ref=half_of(i, slot, c), dst_ref=half_of(i, slot, c),
                                                   send_sem=send_sems.at[i * 2 * n + n + k],
                                                   recv_sem=recv_sems.at[i * 2 * n + n + k], device_id=sibling,
                                                   device_id_type=pl.DeviceIdType.MESH)
                fwd.start()
                sends.append(fwd)
        for i in range(na):
            for k, (dx, dy, _) in enumerate(masks):
                slot = 2 * _flip(x, dx) + _flip(y, dy)
                pltpu.make_async_remote_copy(src_ref=half_of(i, slot, 1 - c), dst_ref=half_of(i, slot, 1 - c),
                                             send_sem=send_sems.at[i * 2 * n + n + k],
                                             recv_sem=recv_sems.at[i * 2 * n + n + k], device_id=sibling,
                                             device_id_type=pl.DeviceIdType.MESH).wait_recv()
        for cp in sends:
            cp.wait_send()
        for st in stores:
            st.wait()

    anyspec = pl.BlockSpec(memory_space=pl.ANY)
    scratch = [pltpu.SemaphoreType.DMA((2 * n * na,)), pltpu.SemaphoreType.DMA((2 * n * na,)),
               pltpu.SemaphoreType.DMA((2 * na,))] + [pltpu.VMEM(s.shape, s.dtype) for s in shards]
    outs = pl.pallas_call(body, name=name, out_shape=tuple(_sds((N_XY,) + s.shape, s.dtype) for s in shards),
                          in_specs=[anyspec] * na, out_specs=tuple([anyspec] * na), scratch_shapes=scratch,
                          compiler_params=pltpu.CompilerParams(vmem_limit_bytes=V7X_VMEM_LIMIT_BYTES))(*shards)
    return list(outs)


BIG = (("w_proj_att", (ATT_WIDTH, D_MODEL), 1), ("w_proj_ssm", (SSM_WIDTH, D_MODEL), 1),
       ("w_glu", (SSM_WIDTH, SSM_WIDTH), 0))
DIRECT = (("w_in", True), ("w_up", True), ("w_down", False), ("w_out", False))
N_XY = 4


def _big_rows(shape):
    return shape[0] * shape[1] // N_XY // LANES


FLAT_ROWS = sum(_big_rows(s) for _, s, _ in BIG)


def _shard_shape(shape, axis):
    return (shape[0] // N_XY, shape[1]) if axis == 0 else (shape[0], shape[1] // N_XY)


def _flatten_shards(shards):
    return jnp.concatenate([shards[n].reshape(_big_rows(s), LANES) for n, s, _ in BIG], axis=0)


def _unflatten_shard(flat):
    out, r = {}, 0
    for n, s, ax in BIG:
        k = _big_rows(s)
        out[n] = flat[r:r + k].reshape(_shard_shape(s, ax))
        r += k
    return out


def _unflatten_full(flat4):
    out, r = {}, 0
    for n, s, ax in BIG:
        k = _big_rows(s)
        sh = _shard_shape(s, ax)
        t = flat4[:, r:r + k].reshape((N_XY,) + sh)
        out[n] = t.reshape(s) if ax == 0 else t.transpose(1, 0, 2).reshape(s)
        r += k
    return out


def _flatten_full(full):
    parts = []
    for n, s, ax in BIG:
        sh = _shard_shape(s, ax)
        t = full[n]
        t = t.reshape((N_XY,) + sh) if ax == 0 else t.reshape(s[0], N_XY, sh[1]).transpose(1, 0, 2)
        parts.append(t.reshape(N_XY, _big_rows(s), LANES))
    return jnp.concatenate(parts, axis=1)


def _lanes_from_groups(a):
    return a.transpose(2, 0, 1).reshape(SSM_GROUP_CH, SSM_LANES)


def _groups_from_lanes(a):
    return a.reshape(SSM_GROUP_CH, SSM_GROUPS, SSM_STATE).transpose(1, 2, 0)


LATE = ("w_up_t", "w_down", "w_out")
EARLY_GRADS = ("w_up_t", "w_down", "w_out")


def _local_step(x3, mod, tgt3, W, P, late_shards=None, scatter_grads=False):
    B, S, _ = x3.shape
    T = B * S
    seq_blocks = S // ATT_BLOCK
    sh1, sc1, gt1, sh2, sc2, gt2 = [m.reshape(B, 1, D_MODEL) for m in jnp.split(mod, 6, axis=-1)]
    g_mix, g_ffn, g_final = P["g_mix"].reshape(1, D_MODEL), P["g_ffn"].reshape(1, D_MODEL), P["g_final"].reshape(1, D_MODEL)
    b_gate = P["b_gate"].reshape(1, 2 * D_MODEL)
    d_skip, b_glu = P["d_skip"].reshape(1, SSM_WIDTH), P["b_glu"].reshape(1, SSM_WIDTH)
    w_conv, b_conv = P["w_conv"], P["b_conv"].reshape(1, D_FF)

    u1 = _norm_mod(x3, g_mix, sc1, sh1).reshape(T, D_MODEL)
    proj = _mm(u1, W["w_in_t"], tb=True, name="mm_proj", out_dtype=BF16)
    proj3 = proj.reshape(B, S, IN_WIDTH)
    us = proj[:, 3 * ATT_WIDTH:3 * ATT_WIDTH + SSM_WIDTH]
    o_att3, lse4, late = _attention_fwd(proj3, seq_blocks, _Riders(late_shards, "gather") if late_shards else None)
    if late_shards:
        W = dict(W, **{n: f.reshape(-1, LANES) for n, f in zip(LATE, late)})
        w_conv = late[len(LATE)].transpose(1, 0, 2).reshape(3, D_FF)
        W.update(_unflatten_full(late[len(LATE) + 1]))
    o_att = o_att3.reshape(T, ATT_WIDTH)
    y_att = _mm(o_att, W["w_proj_att"], name="mm_proj_att", out_dtype=BF16)

    lr = P["a_re"].reshape(1, SSM_LANES)
    li = P["a_im"].reshape(1, SSM_LANES)
    ldt = jnp.repeat(P["log_dt"], SSM_STATE).reshape(1, SSM_LANES)
    br, bi = _lanes_from_groups(P["b_re"]), _lanes_from_groups(P["b_im"])
    cr = P["c_re"].transpose(1, 0, 2).reshape(SSM_GROUP_CH, SSM_LANES)
    ci = P["c_im"].transpose(1, 0, 2).reshape(SSM_GROUP_CH, SSM_LANES)
    abar, w_bu, w_c = _ssm_params(lr, li, ldt, br, bi, cr, ci)
    xs3, y_core3 = _ssm_scan_fwd(proj3, abar, w_bu, w_c)
    y5, s_out = _ssm_post(y_core3.reshape(T, SSM_WIDTH), us, d_skip, W["w_glu"], b_glu)
    y_ssm = _mm(s_out, W["w_proj_ssm"], name="mm_proj_ssm", out_dtype=BF16)

    merged = _merge(proj, y_att, y_ssm, b_gate)
    mix = _mm(merged, W["w_out"], name="mm_out", out_dtype=BF16)
    mix3 = mix.reshape(B, S, D_MODEL)

    h1, u2 = _resid_norm_mod(x3, mix3, gt1, g_ffn, sc2, sh2)
    u2 = u2.reshape(T, D_MODEL)
    up3 = _mm(u2, W["w_up_t"], tb=True, name="mm_up", out_dtype=BF16).reshape(B, S, 2 * D_FF)
    act = _conv_act(up3, w_conv, b_conv).reshape(T, D_FF)
    ffn3 = _mm(act, W["w_down"], name="mm_down", out_dtype=BF16).reshape(B, S, D_MODEL)
    dh2, dffn, dgt2, dg_final, loss = _final_loss(h1, ffn3, tgt3, gt2, g_final)

    dffn = dffn.reshape(T, D_MODEL)
    gw = {}
    gw["w_down"] = _mm(act, dffn, ta=True, out_dtype=BF16, name="mm_dw_down")
    dact3 = _mm(dffn, W["w_down"], tb=True, name="mm_dact", out_dtype=BF16).reshape(B, S, D_FF)
    dup3, dw_conv, db_conv = _conv_bwd(up3, dact3, w_conv, b_conv)
    dup = dup3.reshape(2, T, D_FF)
    gw["w_up_t"] = _mm(dup, u2, ta=True, out_dtype=BF16, name="mm_dw_up")
    du2 = _mm(dup, W["w_up_t"], name="mm_du2", out_dtype=BF16).reshape(B, S, D_MODEL)
    dh1, dsh2, dsc2, dg_ffn, dgt1, dmix = _norm_bwd(h1, du2, dh2, g_ffn, sc2, "norm_bwd2", mix3=mix3, gt=gt1)

    dmix = dmix.reshape(T, D_MODEL)
    gw["w_out"] = _mm(merged, dmix, ta=True, out_dtype=BF16, name="mm_dw_out")
    dmerged = _mm(dmix, W["w_out"], tb=True, name="mm_dmerged", out_dtype=BF16)
    dy_att, dy_ssm, dga, dgs, db_att, db_ssm = _merge_bwd(proj, y_att, y_ssm, b_gate, dmerged)

    gw["w_proj_ssm"] = _mm(s_out, dy_ssm, ta=True, name="mm_dw_proj_ssm")
    ds_out = _mm(dy_ssm, W["w_proj_ssm"], tb=True, name="mm_ds_out")
    dy5, dd_skip, db_glu, dw_glu = _ssm_post_bwd(y5, us, ds_out, d_skip, W["w_glu"], b_glu)
    gw["w_glu"] = dw_glu
    dus3, dab, dwbu, dwc = _ssm_scan_bwd(proj3, dy5.reshape(B, S, SSM_WIDTH), xs3, abar, w_bu, w_c, d_skip)
    dus = dus3.reshape(T, SSM_WIDTH)
    dlr, dli, dldt, dbr, dbi, dcr, dci = _ssm_params_bwd(lr, li, ldt, br, bi, dab, dwbu, dwc)

    gw["w_proj_att"] = _mm(o_att, dy_att, ta=True, name="mm_dw_proj_att")
    do_att = _mm(dy_att, W["w_proj_att"], tb=True, out_dtype=BF16, name="mm_do_att")
    early = [gw[n].reshape(N_XY, -1, LANES) for n in EARLY_GRADS]
    early.append(_flatten_full({n: gw[n].astype(BF16) for n, _, _ in BIG}))
    dq3, dk3, dv3, parts = _attention_bwd(proj3, do_att.reshape(B, S, ATT_WIDTH), o_att3, lse4, seq_blocks,
                                          _Riders(early, "scatter") if scatter_grads else None)
    dproj = jnp.concatenate([t.reshape(T, ATT_WIDTH) for t in (dq3, dk3, dv3)] + [dus, dga, dgs], axis=1)
    dmods = [None, None, dgt1, dsh2, dsc2, dgt2]
    native = dict(b_att=db_att, b_ssm=db_ssm, a_re=dlr, a_im=dli, log_dt=dldt, b_re=dbr, b_im=dbi, c_re=dcr, c_im=dci,
                  d_skip=dd_skip, b_glu=db_glu, g_ffn=dg_ffn, w_conv=dw_conv, b_conv=db_conv, g_final=dg_final, loss=loss)
    small_early = _pack_small(native, dmods, False)
    if scatter_grads:
        gw["w_in_t"], (small_early,) = _mm(dproj, u1, ta=True, out_dtype=BF16, name="mm_dw_in",
                                           riders=_Riders([small_early], "gather", "all"))
        du1, last_parts = _mm(dproj, W["w_in_t"], name="mm_du1", out_dtype=BF16,
                              riders=_Riders([gw["w_in_t"].reshape(N_XY, -1, LANES)], "scatter"))
        parts = parts + last_parts
    else:
        gw["w_in_t"] = _mm(dproj, u1, ta=True, out_dtype=BF16, name="mm_dw_in")
        du1 = _mm(dproj, W["w_in_t"], name="mm_du1", out_dtype=BF16)
    du1 = du1.reshape(B, S, D_MODEL)
    dx, dsh1, dsc1, dg_mix = _norm_bwd(x3, du1, dh1, g_mix, sc1, "norm_bwd1")
    dmods[0], dmods[1] = dsh1, dsc1
    native["g_mix"] = dg_mix
    return loss, dx, dmods, gw, native, parts, small_early


WEIGHTS = ['w_ada', 'b_ada', 'g_mix', 'w_in', 'b_gate', 'a_re', 'a_im', 'log_dt', 'b_re', 'b_im', 'c_re', 'c_im', 'd_skip',
           'w_glu', 'b_glu', 'w_proj_att', 'w_proj_ssm', 'w_out', 'g_ffn', 'w_up', 'w_conv', 'b_conv', 'w_down', 'g_final']
SMALL = ['g_mix', 'b_gate', 'a_re', 'a_im', 'log_dt', 'b_re', 'b_im', 'c_re', 'c_im', 'd_skip', 'b_glu', 'g_ffn', 'w_conv',
         'b_conv', 'g_final']


def kernel(x, c, w_ada, b_ada, g_mix, w_in, b_gate, a_re, a_im, log_dt, b_re, b_im, c_re, c_im, d_skip, w_glu, b_glu, w_proj_att, w_proj_ssm, w_out, g_ffn, w_up, w_conv, b_conv, w_down, g_final, loss_target, m_w_ada, m_b_ada, m_g_mix, m_w_in, m_b_gate, m_a_re, m_a_im, m_log_dt, m_b_re, m_b_im, m_c_re, m_c_im, m_d_skip, m_w_glu, m_b_glu, m_w_proj_att, m_w_proj_ssm, m_w_out, m_g_ffn, m_w_up, m_w_conv, m_b_conv, m_w_down, m_g_final, v_w_ada, v_b_ada, v_g_mix, v_w_in, v_b_gate, v_a_re, v_a_im, v_log_dt, v_b_re, v_b_im, v_c_re, v_c_im, v_d_skip, v_w_glu, v_b_glu, v_w_proj_att, v_w_proj_ssm, v_w_out, v_g_ffn, v_w_up, v_w_conv, v_b_conv, v_w_down, v_g_final):
    args = dict(locals())
    w = {n: args[n] for n in WEIGHTS}
    m = {n: args["m_" + n] for n in WEIGHTS}
    v = {n: args["v_" + n] for n in WEIGHTS}
    B, S, _ = x.shape
    ix, iy, ic = lax.axis_index("x"), lax.axis_index("y"), lax.axis_index("c")
    chip = 2 * ix + iy
    ada_cols = w_ada.shape[2]

    c_all = _exchange(c, "all", "gather", "gather_c").reshape(8 * B, D_MODEL)
    b_cols = lax.dynamic_slice_in_dim(b_ada, chip * ada_cols, ada_cols, axis=1)
    mod_cols = _ada_fwd(c_all, w_ada[0], b_cols)
    mod_all = _exchange(mod_cols, "xy", "gather", "gather_mod")
    mod_all = mod_all.transpose(1, 0, 2).reshape(8 * B, 6 * D_MODEL)
    mod = lax.dynamic_slice_in_dim(mod_all, (4 * ix + 2 * iy + ic) * B, B, axis=0)

    shard = {n + ("_t" if t else ""): (w[n][0].T if t else w[n][0]).astype(BF16) for n, t in DIRECT}
    misc = _flatten_shards({n: w[n][0] for n, _, _ in BIG}).astype(BF16)
    (w_in_full,) = _gather_weights([shard["w_in_t"]], "gather_weights")
    W = {"w_in_t": w_in_full.reshape(-1, LANES)}

    P = {n: w[n][0] for n in SMALL if n not in ("w_conv", "g_final")}
    P["w_conv"] = None
    P["g_final"] = g_final

    loss, dx, dmods, gw, native, parts, small_early = _local_step(x, mod, loss_target, W, P,
                                                                  [shard[n] for n in LATE] + [w_conv[0], misc], True)

    small_late = _exchange(_pack_small(native, dmods, True), "all", "gather", "gather_small")
    native_sum, dmod_all = _sum_unpack_small(small_early, small_late, B)
    loss = native_sum["loss"][0, 0]
    g_small = _small_from_native(native_sum)
    dmod_all = dmod_all.reshape(8 * B, N_MOD * D_MODEL)
    dmod_cols = lax.dynamic_slice_in_dim(dmod_all, chip * ada_cols, ada_cols, axis=1)
    g_w_ada, g_b_ada = _ada_bwd(c_all, dmod_all, dmod_cols)

    red = [_sum_slots(p, "sum_chips_%d" % i) for i, p in enumerate(parts)]
    red_sib = _exchange_list(red, "c", "swap", "share_cores")
    order = list(EARLY_GRADS) + ["misc", "w_in_t"]
    halves = dict(zip(order, zip(red, red_sib)))

    grads = {"w_ada": g_w_ada[None], "b_ada": g_b_ada}
    grads["w_up"] = _add2(*halves["w_up_t"], F32, "add_cores_w_up").T[None]
    for k, gk in _unflatten_shard(_add2(*halves["misc"], F32, "add_cores_misc")).items():
        grads[k] = gk[None]
    wc_cols = w_conv.shape[2]
    for n in SMALL:
        g = g_small[n]
        if n == "w_conv":
            g = lax.dynamic_slice_in_dim(g, chip * wc_cols, wc_cols, axis=1)
        grads[n] = g.reshape(w[n].shape)

    delta, new_m, new_v = {}, {}, {}
    for n in ["w_ada"] + [b for b, _ in DIRECT] + [b for b, _, _ in BIG]:
        shp = w[n].shape
        if n == "w_in":
            r, s = halves["w_in_t"]
            d2, m2, v2, g2 = _adamw(w[n][0].T, r, m[n][0].T, v[n][0].T, "adamw_" + n, g_other=s)
            d2, m2, v2, grads[n] = d2.T, m2.T, v2.T, g2.T[None]
        elif n in ("w_down", "w_out"):
            r, s = halves[n]
            d2, m2, v2, g2 = _adamw(w[n][0], r, m[n][0], v[n][0], "adamw_" + n, g_other=s)
            grads[n] = g2[None]
        else:
            d2, m2, v2 = _adamw(w[n][0], grads[n][0], m[n][0], v[n][0], "adamw_" + n)
        delta[n], new_m[n], new_v[n] = d2.reshape(shp), m2.reshape(shp), v2.reshape(shp)
    rest = ["b_ada"] + SMALL

    def drop(a):
        return a.reshape(1, -1) if a.ndim == 1 else (a if a.ndim == 2 else a[0])

    upd = _adamw_multi([(drop(w[n]), drop(grads[n]), drop(m[n]), drop(v[n])) for n in rest])
    for n, (dd, mm, vv) in zip(rest, upd):
        delta[n], new_m[n], new_v[n] = dd.reshape(w[n].shape), mm.reshape(w[n].shape), vv.reshape(w[n].shape)

    return (loss, dx, *[grads[n] for n in WEIGHTS], *[delta[n] for n in WEIGHTS], *[new_m[n] for n in WEIGHTS],
            *[new_v[n] for n in WEIGHTS])
```

```python
import functools
import math

import jax
import jax.numpy as jnp
from jax import lax
from jax.experimental import pallas as pl
from jax.experimental.pallas import tpu as pltpu

F32, BF16 = jnp.float32, jnp.bfloat16

D_MODEL = 1024
N_HEADS = 8
HEAD_DIM = 64
ATT_WIDTH = 512
SSM_GROUPS = 16
SSM_GROUP_CH = 16
SSM_WIDTH = 256
SSM_STATE = 64
SSM_LANES = SSM_GROUPS * SSM_STATE
D_FF = 2048
IN_WIDTH = 3 * ATT_WIDTH + SSM_WIDTH + 2 * D_MODEL
ATT_BLOCK = 128
N_PATTERNS = 3
EPS = 1e-6
NEG_INF = -1e30

ADAM_LR, ADAM_B1, ADAM_B2, ADAM_EPS, ADAM_WD, ADAM_STEP = 0.001, 0.9, 0.999, 1e-08, 0.01, 10

V7X_VMEM_LIMIT_BYTES = 56 * 1024 * 1024
LANES = 1024


def _pcall(body, *, name, out_shape, grid=(), in_specs=None, out_specs=None, scratch_shapes=(), dims=None):
    params = dict(vmem_limit_bytes=V7X_VMEM_LIMIT_BYTES)
    if dims is not None:
        params["dimension_semantics"] = dims
    specs = {}
    if in_specs is not None:
        specs = dict(grid=grid, in_specs=in_specs, out_specs=out_specs)
    return pl.pallas_call(body, name=name, out_shape=out_shape, scratch_shapes=scratch_shapes,
                          compiler_params=pltpu.CompilerParams(**params), **specs)


def _sds(shape, dtype):
    return jax.ShapeDtypeStruct(tuple(shape), dtype)


def _tile(n, target):
    if n <= target:
        return n
    for t in range(target - target % 128, 0, -128):
        if n % t == 0:
            return t
    raise ValueError((n, target))


def _sig(v):
    return pl.reciprocal(1.0 + jnp.exp(-v), approx=True)


def _mm(a, b, *, name, ta=False, tb=False, out_dtype=F32, tm=2048, tn=1024, tk=1024, riders=None):
    halves = a.ndim == 3
    if halves:
        a_rows, a_cols = a.shape[1], 2 * a.shape[2]
    else:
        a_rows, a_cols = a.shape
    if ta:
        K, M = a_rows, a_cols
    else:
        M, K = a_rows, a_cols
    if tb:
        N, K2 = b.shape
    else:
        K2, N = b.shape
    assert K == K2, (a.shape, b.shape)
    if halves:
        tm, tk = (min(tm, M // 2), tk) if ta else (tm, min(tk, K // 2))
    tm, tn, tk = _tile(M, tm), _tile(N, tn), _tile(K, tk)
    nk = K // tk
    if halves and ta:
        per = a.shape[2] // tm
        a_spec = pl.BlockSpec((None, tk, tm), lambda i, j, k: (i // per, k, i % per))
    elif halves:
        per = a.shape[2] // tk
        a_spec = pl.BlockSpec((None, tm, tk), lambda i, j, k: (k // per, i, k % per))
    else:
        a_spec = pl.BlockSpec((tk, tm), lambda i, j, k: (k, i)) if ta else pl.BlockSpec((tm, tk), lambda i, j, k: (i, k))
    b_spec = pl.BlockSpec((tn, tk), lambda i, j, k: (j, k)) if tb else pl.BlockSpec((tk, tn), lambda i, j, k: (k, j))
    dn = (((0 if ta else 1,), (1 if tb else 0,)), ((), ()))

    def body(a_ref, b_ref, o_ref, acc_ref):
        k = pl.program_id(2)

        @pl.when(k == 0)
        def _():
            acc_ref[...] = jnp.zeros_like(acc_ref)

        acc_ref[...] += lax.dot_general(a_ref[...].astype(BF16), b_ref[...].astype(BF16), dn,
                                        preferred_element_type=F32)

        @pl.when(k == nk - 1)
        def _():
            o_ref[...] = acc_ref[...].astype(out_dtype)

    def body_single(a_ref, b_ref, o_ref):
        o_ref[...] = lax.dot_general(a_ref[...].astype(BF16), b_ref[...].astype(BF16), dn,
                                     preferred_element_type=F32).astype(out_dtype)

    grid = (M // tm, N // tn, nk)
    scratch = [] if nk == 1 else [pltpu.VMEM((tm, tn), F32)]
    o_spec = pl.BlockSpec((tm, tn), lambda i, j, k: (i, j))
    if riders is None:
        return _pcall(body_single if nk == 1 else body, name=name, out_shape=_sds((M, N), out_dtype), grid=grid,
                      in_specs=[a_spec, b_spec], out_specs=o_spec, scratch_shapes=scratch,
                      dims=("parallel", "parallel", "arbitrary"))(a, b)
    rs = riders
    res = _pcall(_with_riders(body_single if nk == 1 else body, rs, 2, 1, len(scratch), tuple(g - 1 for g in grid)),
                 name=name, out_shape=(_sds((M, N), out_dtype),) + tuple(rs.out_shape), grid=grid,
                 in_specs=[a_spec, b_spec] + rs.specs, out_specs=(o_spec,) + tuple(rs.specs),
                 scratch_shapes=scratch + rs.scratch, dims=("arbitrary", "arbitrary", "arbitrary"))(a, b, *rs.arrs)
    return res[0], list(res[1:])


def _ada_fwd(c_all, w_ada, b_ada_cols):
    n = w_ada.shape[1]

    def body(c_ref, w_ref, b_ref, o_ref):
        c = c_ref[...]
        act = c * _sig(c)
        o_ref[...] = jnp.dot(act.astype(BF16), w_ref[...].astype(BF16), preferred_element_type=F32) + b_ref[...]

    return _pcall(body, name="ada_fwd", out_shape=_sds((c_all.shape[0], n), F32))(c_all, w_ada, b_ada_cols)


def _ada_bwd(c_all, dmod_all, dmod_cols):
    n = dmod_cols.shape[1]

    def body(c_ref, da_ref, dc_ref, gw_ref, gb_ref):
        c = c_ref[...]
        act = c * _sig(c)
        gw_ref[...] = lax.dot_general(act, dc_ref[...], (((0,), (0,)), ((), ())), preferred_element_type=F32,
                                      precision=lax.Precision.HIGHEST)
        gb_ref[...] = jnp.sum(da_ref[...], axis=0, keepdims=True)

    return _pcall(body, name="ada_bwd", out_shape=(_sds((D_MODEL, n), F32), _sds((1, dmod_all.shape[1]), F32)))(
        c_all, dmod_all, dmod_cols)


ROW_TILE = 1024


def _row_specs(B, S):
    ts = min(S, ROW_TILE)
    row = pl.BlockSpec((1, ts, D_MODEL), lambda b, s: (b, s, 0))
    bvec = pl.BlockSpec((1, 1, D_MODEL), lambda b, s: (b, 0, 0))
    gvec = pl.BlockSpec((1, D_MODEL), lambda b, s: (0, 0))
    return ts, row, bvec, gvec


def _norm_mod(x3, g, sc, sh):
    B, S, _ = x3.shape
    ts, row, bvec, gvec = _row_specs(B, S)

    def body(x_ref, g_ref, sc_ref, sh_ref, u_ref):
        x = x_ref[0]
        r = lax.rsqrt(jnp.mean(x * x, axis=-1, keepdims=True) + EPS)
        u_ref[0] = ((x * r) * g_ref[...] * (1.0 + sc_ref[0]) + sh_ref[0]).astype(BF16)

    return _pcall(body, name="norm_mod1", out_shape=_sds(x3.shape, BF16), grid=(B, S // ts),
                  in_specs=[row, gvec, bvec, bvec], out_specs=row, dims=("parallel", "parallel"))(x3, g, sc, sh)


def _resid_norm_mod(x3, mix3, gt, g, sc, sh):
    B, S, _ = x3.shape
    ts, row, bvec, gvec = _row_specs(B, S)

    def body(x_ref, m_ref, gt_ref, g_ref, sc_ref, sh_ref, h_ref, u_ref):
        h = x_ref[0] + gt_ref[0] * m_ref[0]
        h_ref[0] = h
        r = lax.rsqrt(jnp.mean(h * h, axis=-1, keepdims=True) + EPS)
        u_ref[0] = ((h * r) * g_ref[...] * (1.0 + sc_ref[0]) + sh_ref[0]).astype(BF16)

    return _pcall(body, name="resid_norm_mod2", out_shape=(_sds(x3.shape, F32), _sds(x3.shape, BF16)),
                  grid=(B, S // ts), in_specs=[row, row, bvec, gvec, bvec, bvec], out_specs=(row, row),
                  dims=("parallel", "parallel"))(x3, mix3, gt, g, sc, sh)


def _norm_bwd(h3, du3, dres3, g, sc, name, mix3=None, gt=None, riders=None):
    B, S, _ = h3.shape
    ts, row, bvec, gvec = _row_specs(B, S)
    with_gate = mix3 is not None

    def body(*refs):
        if with_gate:
            h_ref, du_ref, dr_ref, g_ref, sc_ref, m_ref, gt_ref, dh_ref, dsh_ref, dsc_ref, dg_ref, dgt_ref, dm_ref = refs
        else:
            h_ref, du_ref, dr_ref, g_ref, sc_ref, dh_ref, dsh_ref, dsc_ref, dg_ref = refs
        b, s = pl.program_id(0), pl.program_id(1)
        h = h_ref[0]
        r = lax.rsqrt(jnp.mean(h * h, axis=-1, keepdims=True) + EPS)
        xn = h * r
        du = du_ref[0].astype(F32)
        g = g_ref[...]
        sc1 = 1.0 + sc_ref[0]
        dxn = du * g * sc1
        dh = dr_ref[0].astype(F32) + r * (dxn - xn * jnp.mean(dxn * xn, axis=-1, keepdims=True))
        dh_ref[0] = dh.astype(dh_ref.dtype)

        @pl.when(s == 0)
        def _():
            dsh_ref[...] = jnp.zeros_like(dsh_ref)
            dsc_ref[...] = jnp.zeros_like(dsc_ref)
            if with_gate:
                dgt_ref[...] = jnp.zeros_like(dgt_ref)

        @pl.when((s == 0) & (b == 0))
        def _():
            dg_ref[...] = jnp.zeros_like(dg_ref)

        dux = du * xn
        dsh_ref[0] += jnp.sum(du, axis=0, keepdims=True)
        dsc_ref[0] += jnp.sum(dux * g, axis=0, keepdims=True)
        dg_ref[...] += jnp.sum(dux * sc1, axis=0, keepdims=True)
        if with_gate:
            dgt_ref[0] += jnp.sum(dh * m_ref[0], axis=0, keepdims=True)
            dm_ref[0] = (dh * gt_ref[0]).astype(BF16)

    bshape = _sds((B, 1, D_MODEL), F32)
    in_specs = [row, row, row, gvec, bvec]
    out_shape = [_sds(h3.shape, BF16 if with_gate else F32), bshape, bshape, _sds((1, D_MODEL), F32)]
    out_specs = [row, bvec, bvec, gvec]
    args = [h3, du3, dres3, g, sc]
    if with_gate:
        in_specs += [row, bvec]
        out_shape += [bshape, _sds(h3.shape, BF16)]
        out_specs += [bvec, row]
        args += [mix3, gt]
    if riders is None:
        return _pcall(body, name=name, out_shape=tuple(out_shape), grid=(B, S // ts), in_specs=in_specs,
                      out_specs=tuple(out_specs), dims=("arbitrary", "arbitrary"))(*args)
    rs = riders
    res = _pcall(_with_riders(body, rs, len(args), len(out_shape), 0, (B - 1, S // ts - 1)), name=name,
                 out_shape=tuple(out_shape) + tuple(rs.out_shape), grid=(B, S // ts), in_specs=in_specs + rs.specs,
                 out_specs=tuple(out_specs) + tuple(rs.specs), scratch_shapes=rs.scratch,
                 dims=("arbitrary", "arbitrary"))(*args, *rs.arrs)
    return tuple(res[:len(out_shape)]) + (list(res[len(out_shape):]),)


def _final_loss(h1, ffn3, tgt3, gt, gfin):
    B, S, _ = h1.shape
    ts, row, bvec, gvec = _row_specs(B, S)
    one = pl.BlockSpec((1, 1), lambda b, s: (0, 0))

    def body(h_ref, f_ref, t_ref, gt_ref, gf_ref, dh_ref, dff_ref, dgt_ref, dgf_ref, loss_ref):
        b, s = pl.program_id(0), pl.program_id(1)
        f = f_ref[0].astype(F32)
        gtv = gt_ref[0]
        gf = gf_ref[...]
        h2 = h_ref[0] + gtv * f
        r = lax.rsqrt(jnp.mean(h2 * h2, axis=-1, keepdims=True) + EPS)
        n = h2 * r
        e = n * gf - t_ref[0]
        dy = e * (1.0 / D_MODEL)
        dn = dy * gf
        dh2 = r * (dn - n * jnp.mean(dn * n, axis=-1, keepdims=True))
        dh_ref[0] = dh2.astype(BF16)
        dff_ref[0] = (dh2 * gtv).astype(BF16)

        @pl.when(s == 0)
        def _():
            dgt_ref[...] = jnp.zeros_like(dgt_ref)

        @pl.when((s == 0) & (b == 0))
        def _():
            dgf_ref[...] = jnp.zeros_like(dgf_ref)
            loss_ref[...] = jnp.zeros_like(loss_ref)

        dgt_ref[0] += jnp.sum(dh2 * f, axis=0, keepdims=True)
        dgf_ref[...] += jnp.sum(dy * n, axis=0, keepdims=True)
        rows = jnp.sum(e * e, axis=1, keepdims=True)
        loss_ref[...] += jnp.sum(rows, axis=0, keepdims=True) * (0.5 / D_MODEL)

    return _pcall(body, name="final_loss",
                  out_shape=(_sds(h1.shape, BF16), _sds(h1.shape, BF16), _sds((B, 1, D_MODEL), F32),
                             _sds((1, D_MODEL), F32), _sds((1, 1), F32)),
                  grid=(B, S // ts), in_specs=[row, row, row, bvec, gvec], out_specs=(row, row, bvec, gvec, one),
                  dims=("arbitrary", "arbitrary"))(h1, ffn3, tgt3, gt, gfin)


ATT_GROUP = 4
ATT_GW = ATT_GROUP * HEAD_DIM
ATT_GROUPS = N_HEADS // ATT_GROUP
ATT_PAIRS = ATT_GW // ATT_BLOCK
ATT_UNROLL = 5
ATT_RESIDUE_UNROLL = 4
NT_DIMS = (((1,), (1,)), ((), ()))
TN_DIMS = (((0,), (0,)), ((), ()))


def _att_rows(start, d):
    if d == 1:
        return pl.ds(start if isinstance(start, int) else pl.multiple_of(start, ATT_BLOCK), ATT_BLOCK)
    return pl.ds(start, ATT_BLOCK, stride=d)


def _att_fill_bias(bias_ref, g, d):
    a = lax.broadcasted_iota(jnp.int32, (ATT_BLOCK, ATT_BLOCK), 0)
    j = lax.broadcasted_iota(jnp.int32, (ATT_BLOCK, ATT_BLOCK), 1)
    dist = (a - j).astype(F32)
    for hh in range(ATT_GROUP):
        t, e = divmod(hh, 2)
        rs = slice(e * ATT_BLOCK, (e + 1) * ATT_BLOCK)
        lo = 2.0 ** (-8.0 * (hh + 1) / N_HEADS) * d
        hi = 2.0 ** (-8.0 * (ATT_GROUP + hh + 1) / N_HEADS) * d
        slope = jnp.where(g == 0, lo, hi).astype(F32)
        bias_ref[t, rs, 0:ATT_BLOCK] = jnp.where(a >= j, -slope * dist, NEG_INF)
        bias_ref[t, rs, ATT_BLOCK:] = jnp.where(j >= a, -slope * (dist + float(ATT_BLOCK)), NEG_INF)


def _stack_heads(v2, low):
    return jnp.concatenate([jnp.where(low, v2, 0.0), jnp.where(low, 0.0, v2)], axis=0).astype(BF16)


def _unstack_heads(r2, low):
    return jnp.where(low, r2[0:ATT_BLOCK], r2[ATT_BLOCK:])


class _Riders:
    def __init__(self, arrs, mode, group="xy"):
        self.arrs, self.mode, self.n, self.group = list(arrs), mode, len(arrs), group
        k = len(_GROUP_MASKS[group])
        self.scratch = [pltpu.SemaphoreType.DMA((k * self.n,)), pltpu.SemaphoreType.DMA((k * self.n,))]
        if mode == "swap":
            assert group == "c"
            self.out_shape = [_sds(a.shape, a.dtype) for a in self.arrs]
        else:
            slot_shapes = [a.shape if mode == "gather" else a.shape[1:] for a in self.arrs]
            self.out_shape = [_sds((_GROUP_SLOTS[group],) + s, a.dtype) for s, a in zip(slot_shapes, self.arrs)]
            self.scratch += [pltpu.SemaphoreType.DMA((2 * self.n,))] + [pltpu.VMEM(s, a.dtype)
                                                                        for s, a in zip(slot_shapes, self.arrs)]
        self.specs = [pl.BlockSpec(memory_space=pl.ANY)] * self.n

    def _remote(self, x_refs, o_refs, send_sems, recv_sems):
        x, y, c = lax.axis_index("x"), lax.axis_index("y"), lax.axis_index("c")
        me = _group_slot(self.group, x, y, c)
        masks = _GROUP_MASKS[self.group]
        cps = []
        for i in range(self.n):
            for k, (dx, dy, dc) in enumerate(masks):
                px, py, pc = _flip(x, dx), _flip(y, dy), _flip(c, dc)
                src = x_refs[i].at[_group_slot(self.group, px, py, pc)] if self.mode == "scatter" else x_refs[i]
                dst = o_refs[i] if self.mode == "swap" else o_refs[i].at[me]
                cps.append(pltpu.make_async_remote_copy(
                    src_ref=src, dst_ref=dst, send_sem=send_sems.at[len(masks) * i + k],
                    recv_sem=recv_sems.at[len(masks) * i + k], device_id=(px, py, pc),
                    device_id_type=pl.DeviceIdType.MESH))
        return cps, me

    def start(self, x_refs, o_refs, scratch):
        cps, me = self._remote(x_refs, o_refs, scratch[0], scratch[1])
        for cp in cps:
            cp.start()
        if self.mode == "swap":
            return
        local_sems, bufs = scratch[2], scratch[3:]
        for i in range(self.n):
            src = x_refs[i] if self.mode == "gather" else x_refs[i].at[me]
            load = pltpu.make_async_copy(src, bufs[i], local_sems.at[2 * i])
            load.start()
            load.wait()
            pltpu.make_async_copy(bufs[i], o_refs[i].at[me], local_sems.at[2 * i + 1]).start()

    def wait(self, x_refs, o_refs, scratch):
        cps, me = self._remote(x_refs, o_refs, scratch[0], scratch[1])
        for cp in cps:
            cp.wait()
        if self.mode == "swap":
            return
        local_sems, bufs = scratch[2], scratch[3:]
        for i in range(self.n):
            pltpu.make_async_copy(bufs[i], o_refs[i].at[me], local_sems.at[2 * i + 1]).wait()


def _with_riders(compute, riders, n_in, n_out, n_scratch, last_step):
    if riders is None:
        return compute
    n = riders.n

    def body(*refs):
        ins, x_refs = refs[:n_in], refs[n_in:n_in + n]
        outs, o_refs = refs[n_in + n:n_in + n + n_out], refs[n_in + n + n_out:n_in + 2 * n + n_out]
        scratch = refs[n_in + 2 * n + n_out:]
        own, ride = scratch[:n_scratch], scratch[n_scratch:]
        ids = [pl.program_id(i) for i in range(len(last_step))]
        first = functools.reduce(jnp.logical_and, [i == 0 for i in ids])
        last = functools.reduce(jnp.logical_and, [i == l for i, l in zip(ids, last_step)])

        @pl.when(first)
        def _():
            riders.start(x_refs, o_refs, ride)

        compute(*ins, *outs, *own)

        @pl.when(last)
        def _():
            riders.wait(x_refs, o_refs, ride)

    return body


def _attention_fwd(proj3, seq_blocks, riders=None):
    B, S, _ = proj3.shape
    scale = HEAD_DIM ** -0.5
    nq = ATT_WIDTH // ATT_GW

    def col(k):
        return pl.BlockSpec((1, S, ATT_GW), lambda b, g, k=k: (b, 0, k * nq + g))

    o_spec = pl.BlockSpec((1, S, ATT_GW), lambda b, g: (b, 0, g))
    l_spec = pl.BlockSpec((1, 1, S, ATT_BLOCK), lambda b, g: (b, g, 0, 0))

    def compute(q_ref, k_ref, v_ref, o_ref, lse_ref, qf, kf, vf, os, ls, bias):
        g = pl.program_id(1)
        for t in range(ATT_PAIRS):
            ts = slice(t * ATT_BLOCK, (t + 1) * ATT_BLOCK)
            qf[t] = q_ref[0, :, ts].astype(F32) * scale
            kf[t] = k_ref[0, :, ts].astype(F32)
            vf[t] = v_ref[0, :, ts].astype(F32)
        lane = lax.broadcasted_iota(jnp.int32, (ATT_BLOCK, ATT_BLOCK), 1)
        low = lane < HEAD_DIM

        def block(p, d, r, n, has_prev):
            start = n * (ATT_BLOCK * d) + r
            rows = _att_rows(start, d)
            prows = _att_rows(start - ATT_BLOCK * d, d) if has_prev else None
            lse_t = jnp.zeros((ATT_BLOCK, ATT_BLOCK), F32)
            for t in range(ATT_PAIRS):
                q2 = _stack_heads(qf[t, rows, :], low)
                k2 = kf[t, rows, :].astype(BF16)
                v2 = vf[t, rows, :].astype(BF16)
                if has_prev:
                    k2 = jnp.concatenate([k2, kf[t, prows, :].astype(BF16)], axis=0)
                    v2 = jnp.concatenate([v2, vf[t, prows, :].astype(BF16)], axis=0)
                    b2 = bias[t]
                else:
                    b2 = bias[t, :, 0:ATT_BLOCK]
                s = lax.dot_general(q2, k2, NT_DIMS, preferred_element_type=F32) + b2
                m = jnp.max(s, axis=1, keepdims=True)
                pr = jnp.exp(s - m)
                den = jnp.sum(pr, axis=1, keepdims=True)
                o = jnp.dot(pr.astype(BF16), v2, preferred_element_type=F32) * (1.0 / den)
                os[p, t, rows, :] = _unstack_heads(o, low)
                lse2 = m + jnp.log(den)
                lse_t = jnp.where(lane == 2 * t, lse2[0:ATT_BLOCK], lse_t)
                lse_t = jnp.where(lane == 2 * t + 1, lse2[ATT_BLOCK:], lse_t)
            ls[p, rows, :] = lse_t

        for p in range(N_PATTERNS):
            d = 4 ** p
            _att_fill_bias(bias, g, d)
            _att_one_pattern(block, p, d, seq_blocks // d)

        def combine(i, carry):
            rows = pl.ds(pl.multiple_of(i * ATT_BLOCK, ATT_BLOCK), ATT_BLOCK)
            l0, l1, l2 = ls[0, rows, :], ls[1, rows, :], ls[2, rows, :]
            m = jnp.maximum(jnp.maximum(l0, l1), l2)
            lse = m + jnp.log(jnp.exp(l0 - m) + jnp.exp(l1 - m) + jnp.exp(l2 - m))
            lse_ref[0, 0, rows, :] = lse
            w = [jnp.exp(l0 - lse), jnp.exp(l1 - lse), jnp.exp(l2 - lse)]
            for t in range(ATT_PAIRS):
                acc = jnp.zeros((ATT_BLOCK, ATT_BLOCK), F32)
                for p in range(N_PATTERNS):
                    wt = jnp.where(low, w[p][:, 2 * t:2 * t + 1], w[p][:, 2 * t + 1:2 * t + 2])
                    acc = acc + wt * os[p, t, rows, :]
                o_ref[0, rows, t * ATT_BLOCK:(t + 1) * ATT_BLOCK] = acc.astype(BF16)
            return carry

        lax.fori_loop(0, S // ATT_BLOCK, combine, 0, unroll=2)

    scratch = ([pltpu.VMEM((ATT_PAIRS, S, ATT_BLOCK), F32)] * 3
               + [pltpu.VMEM((N_PATTERNS, ATT_PAIRS, S, ATT_BLOCK), F32), pltpu.VMEM((N_PATTERNS, S, ATT_BLOCK), F32),
                  pltpu.VMEM((ATT_PAIRS, 2 * ATT_BLOCK, 2 * ATT_BLOCK), F32)])
    rs = riders
    res = _pcall(_with_riders(compute, rs, 3, 2, len(scratch), (B - 1, ATT_GROUPS - 1)), name="attention_fwd",
                 out_shape=(_sds((B, S, ATT_WIDTH), BF16), _sds((B, ATT_GROUPS, S, ATT_BLOCK), F32))
                 + (tuple(rs.out_shape) if rs else ()),
                 grid=(B, ATT_GROUPS), in_specs=[col(0), col(1), col(2)] + (rs.specs if rs else []),
                 out_specs=(o_spec, l_spec) + (tuple(rs.specs) if rs else ()),
                 scratch_shapes=scratch + (rs.scratch if rs else []),
                 dims=("arbitrary", "arbitrary"))(proj3, proj3, proj3, *(rs.arrs if rs else []))
    return res[0], res[1], list(res[2:])


def _att_one_pattern(block, p, d, nb):
    def per_residue(r, carry):
        block(p, d, r, 0, False)
        if nb > 1:
            def per_block(n, c2):
                block(p, d, r, n, True)
                return c2
            lax.fori_loop(1, nb, per_block, 0, unroll=ATT_UNROLL if (nb - 1) % ATT_UNROLL == 0 else nb - 1)
        return carry

    if d == 1:
        per_residue(0, 0)
    else:
        lax.fori_loop(0, d, per_residue, 0, unroll=ATT_RESIDUE_UNROLL if nb == 1 else 1)


def _attention_bwd(proj3, do3, o3, lse4, seq_blocks, riders=None):
    B, S, _ = proj3.shape
    scale = HEAD_DIM ** -0.5
    nq = ATT_WIDTH // ATT_GW

    def col(k):
        return pl.BlockSpec((1, S, ATT_GW), lambda b, g, k=k: (b, 0, k * nq + g))

    o_spec = pl.BlockSpec((1, S, ATT_GW), lambda b, g: (b, 0, g))
    l_spec = pl.BlockSpec((1, 1, S, ATT_BLOCK), lambda b, g: (b, g, 0, 0))

    def compute(q_ref, k_ref, v_ref, do_ref, o_ref, lse_ref, dq_ref, dk_ref, dv_ref,
                qf, kf, vf, dof, dl, aq, ak, av, bias):
        g = pl.program_id(1)
        for t in range(ATT_PAIRS):
            ts = slice(t * ATT_BLOCK, (t + 1) * ATT_BLOCK)
            qf[t] = q_ref[0, :, ts].astype(F32) * scale
            kf[t] = k_ref[0, :, ts].astype(F32)
            vf[t] = v_ref[0, :, ts].astype(F32)
            dof[t] = do_ref[0, :, ts].astype(F32)
        aq[...] = jnp.zeros_like(aq)
        ak[...] = jnp.zeros_like(ak)
        av[...] = jnp.zeros_like(av)
        lane = lax.broadcasted_iota(jnp.int32, (ATT_BLOCK, ATT_BLOCK), 1)
        low = lane < HEAD_DIM

        def fill_delta(i, carry):
            rows = pl.ds(pl.multiple_of(i * ATT_BLOCK, ATT_BLOCK), ATT_BLOCK)
            acc = jnp.zeros((ATT_BLOCK, ATT_BLOCK), F32)
            for t in range(ATT_PAIRS):
                prod = dof[t, rows, :] * o_ref[0, rows, t * ATT_BLOCK:(t + 1) * ATT_BLOCK].astype(F32)
                lo = jnp.sum(jnp.where(low, prod, 0.0), axis=1, keepdims=True)
                hi = jnp.sum(prod, axis=1, keepdims=True) - lo
                acc = jnp.where(lane == 2 * t, lo, acc)
                acc = jnp.where(lane == 2 * t + 1, hi, acc)
            dl[rows, :] = acc
            return carry

        lax.fori_loop(0, S // ATT_BLOCK, fill_delta, 0, unroll=2)

        def block(p, d, r, n, has_prev):
            start = n * (ATT_BLOCK * d) + r
            rows = _att_rows(start, d)
            prows = _att_rows(start - ATT_BLOCK * d, d) if has_prev else None
            lse_t = lse_ref[0, 0, rows, :]
            dl_t = dl[rows, :]
            for t in range(ATT_PAIRS):
                q2 = _stack_heads(qf[t, rows, :], low)
                do2 = _stack_heads(dof[t, rows, :], low)
                k2 = kf[t, rows, :].astype(BF16)
                v2 = vf[t, rows, :].astype(BF16)
                if has_prev:
                    k2 = jnp.concatenate([k2, kf[t, prows, :].astype(BF16)], axis=0)
                    v2 = jnp.concatenate([v2, vf[t, prows, :].astype(BF16)], axis=0)
                    b2 = bias[t]
                else:
                    b2 = bias[t, :, 0:ATT_BLOCK]
                lse2 = jnp.concatenate([lse_t[:, 2 * t:2 * t + 1], lse_t[:, 2 * t + 1:2 * t + 2]], axis=0)
                dl2 = jnp.concatenate([dl_t[:, 2 * t:2 * t + 1], dl_t[:, 2 * t + 1:2 * t + 2]], axis=0)
                s = lax.dot_general(q2, k2, NT_DIMS, preferred_element_type=F32) + b2
                pr = jnp.exp(s - lse2)
                ds = (pr * (lax.dot_general(do2, v2, NT_DIMS, preferred_element_type=F32) - dl2)).astype(BF16)
                dq = _unstack_heads(jnp.dot(ds, k2, preferred_element_type=F32), low)
                dk = lax.dot_general(ds, q2, TN_DIMS, preferred_element_type=F32)
                dv = lax.dot_general(pr.astype(BF16), do2, TN_DIMS, preferred_element_type=F32)
                aq[t, rows, :] = aq[t, rows, :] + dq * scale
                ak[t, rows, :] = ak[t, rows, :] + dk[0:ATT_BLOCK]
                av[t, rows, :] = av[t, rows, :] + dv[0:ATT_BLOCK]
                if has_prev:
                    ak[t, prows, :] = ak[t, prows, :] + dk[ATT_BLOCK:]
                    av[t, prows, :] = av[t, prows, :] + dv[ATT_BLOCK:]

        for p in range(N_PATTERNS):
            d = 4 ** p
            _att_fill_bias(bias, g, d)
            _att_one_pattern(block, p, d, seq_blocks // d)

        for t in range(ATT_PAIRS):
            ts = slice(t * ATT_BLOCK, (t + 1) * ATT_BLOCK)
            dq_ref[0, :, ts] = aq[t].astype(BF16)
            dk_ref[0, :, ts] = ak[t].astype(BF16)
            dv_ref[0, :, ts] = av[t].astype(BF16)

    shp = _sds((B, S, ATT_WIDTH), BF16)
    pair_buf = pltpu.VMEM((ATT_PAIRS, S, ATT_BLOCK), F32)
    scratch = ([pair_buf] * 4 + [pltpu.VMEM((S, ATT_BLOCK), F32)] + [pair_buf] * 3
               + [pltpu.VMEM((ATT_PAIRS, 2 * ATT_BLOCK, 2 * ATT_BLOCK), F32)])
    rs = riders
    res = _pcall(_with_riders(compute, rs, 6, 3, len(scratch), (B - 1, ATT_GROUPS - 1)), name="attention_bwd",
                 out_shape=(shp, shp, shp) + (tuple(rs.out_shape) if rs else ()), grid=(B, ATT_GROUPS),
                 in_specs=[col(0), col(1), col(2), o_spec, o_spec, l_spec] + (rs.specs if rs else []),
                 out_specs=(o_spec, o_spec, o_spec) + (tuple(rs.specs) if rs else ()),
                 scratch_shapes=scratch + (rs.scratch if rs else []),
                 dims=("arbitrary", "arbitrary"))(proj3, proj3, proj3, do3, o3, lse4, *(rs.arrs if rs else []))
    return res[0], res[1], res[2], list(res[3:])


def _expand_groups(m):
    rows = SSM_WIDTH
    t = jnp.concatenate([m] * SSM_GROUPS, axis=0)
    r = lax.broadcasted_iota(jnp.int32, (rows, SSM_LANES), 0)
    l = lax.broadcasted_iota(jnp.int32, (rows, SSM_LANES), 1)
    keep = lax.shift_right_logical(r, 4) == lax.shift_right_logical(l, 6)
    return jnp.where(keep, t, 0.0)


def _collapse_groups(m):
    rows = SSM_WIDTH
    r = lax.broadcasted_iota(jnp.int32, (rows, SSM_LANES), 0)
    l = lax.broadcasted_iota(jnp.int32, (rows, SSM_LANES), 1)
    keep = lax.shift_right_logical(r, 4) == lax.shift_right_logical(l, 6)
    t = jnp.where(keep, m, 0.0)
    acc = t[0:SSM_GROUP_CH]
    for g in range(1, SSM_GROUPS):
        acc = acc + t[g * SSM_GROUP_CH:(g + 1) * SSM_GROUP_CH]
    return acc


def _zoh(lr, li, ldt):
    dt = jnp.exp(ldt)
    mag = jnp.exp(lr * dt)
    ang = li * dt
    cs, sn = jnp.cos(ang), jnp.sin(ang)
    ab_re, ab_im = mag * cs, mag * sn
    nr, ni = ab_re - 1.0, ab_im
    den = lr * lr + li * li
    n_re = nr * lr + ni * li
    n_im = ni * lr - nr * li
    return dict(dt=dt, mag=mag, cs=cs, sn=sn, ab_re=ab_re, ab_im=ab_im, nr=nr, ni=ni, den=den, n_re=n_re, n_im=n_im,
                f_re=n_re / den, f_im=n_im / den)


def _ssm_params(lr, li, ldt, br, bi, cr, ci):
    def body(lr_ref, li_ref, ldt_ref, br_ref, bi_ref, cr_ref, ci_ref, ab_ref, w_ref, c_ref):
        z = _zoh(lr_ref[...], li_ref[...], ldt_ref[...])
        ab_ref[0:1, :] = z["ab_re"]
        ab_ref[1:2, :] = z["ab_im"]
        br, bi = br_ref[...], bi_ref[...]
        w_ref[:, 0:SSM_LANES] = _expand_groups(z["f_re"] * br - z["f_im"] * bi).astype(BF16)
        w_ref[:, SSM_LANES:] = _expand_groups(z["f_re"] * bi + z["f_im"] * br).astype(BF16)
        c_ref[:, 0:SSM_LANES] = _expand_groups(cr_ref[...]).astype(BF16)
        c_ref[:, SSM_LANES:] = _expand_groups(-ci_ref[...]).astype(BF16)

    return _pcall(body, name="ssm_params",
                  out_shape=(_sds((2, SSM_LANES), F32), _sds((SSM_WIDTH, 2 * SSM_LANES), BF16),
                             _sds((SSM_WIDTH, 2 * SSM_LANES), BF16)))(lr, li, ldt, br, bi, cr, ci)


def _ssm_params_bwd(lr, li, ldt, br, bi, dab, dw, dc):
    def body(lr_ref, li_ref, ldt_ref, br_ref, bi_ref, dab_ref, dw_ref, dc_ref,
             dlr_ref, dli_ref, dldt_ref, dbr_ref, dbi_ref, dcr_ref, dci_ref):
        lr, li = lr_ref[...], li_ref[...]
        z = _zoh(lr, li, ldt_ref[...])
        br, bi = br_ref[...], bi_ref[...]
        dbb_re = _collapse_groups(dw_ref[:, 0:SSM_LANES])
        dbb_im = _collapse_groups(dw_ref[:, SSM_LANES:])
        dcr_ref[...] = _collapse_groups(dc_ref[:, 0:SSM_LANES])
        dci_ref[...] = -_collapse_groups(dc_ref[:, SSM_LANES:])
        f_re, f_im = z["f_re"], z["f_im"]
        dbr_ref[...] = f_re * dbb_re + f_im * dbb_im
        dbi_ref[...] = f_re * dbb_im - f_im * dbb_re
        df_re = jnp.sum(dbb_re * br + dbb_im * bi, axis=0, keepdims=True)
        df_im = jnp.sum(dbb_im * br - dbb_re * bi, axis=0, keepdims=True)
        den = z["den"]
        dn_re, dn_im = df_re / den, df_im / den
        dden = -(df_re * z["n_re"] + df_im * z["n_im"]) / (den * den)
        dnr = dn_re * lr - dn_im * li
        dni = dn_re * li + dn_im * lr
        dlr = dn_re * z["nr"] + dn_im * z["ni"] + 2.0 * dden * lr
        dli = dn_re * z["ni"] - dn_im * z["nr"] + 2.0 * dden * li
        dab_re = dab_ref[0:1, :] + dnr
        dab_im = dab_ref[1:2, :] + dni
        mag, cs, sn, dt = z["mag"], z["cs"], z["sn"], z["dt"]
        dmag = dab_re * cs + dab_im * sn
        dang = mag * (dab_im * cs - dab_re * sn)
        dlr_ref[...] = dlr + dmag * mag * dt
        dli_ref[...] = dli + dang * dt
        ddt = dmag * mag * lr + dang * li
        per_lane = jnp.broadcast_to(ddt * dt, (8, SSM_LANES))
        lane = lax.broadcasted_iota(jnp.int32, (SSM_LANES, 128), 0)
        col = lax.broadcasted_iota(jnp.int32, (SSM_LANES, 128), 1)
        ind = jnp.where(lax.shift_right_logical(lane, 6) == col, 1.0, 0.0)
        dldt_ref[...] = jnp.dot(per_lane, ind, preferred_element_type=F32, precision=lax.Precision.HIGHEST)[0:1]

    vec = _sds((1, SSM_LANES), F32)
    mat = _sds((SSM_GROUP_CH, SSM_LANES), F32)
    return _pcall(body, name="ssm_params_bwd", out_shape=(vec, vec, _sds((1, 128), F32), mat, mat, mat, mat))(
        lr, li, ldt, br, bi, dab, dw, dc)


SCAN_CHUNK = 512


def _scan_consts(ar, ai, k_ref, reverse):
    row = lax.broadcasted_iota(jnp.int32, (8, SSM_LANES), 0)
    pw = [(ar, ai)]
    for _ in range(7):
        pr, pi = pw[-1]
        pw.append((pr * ar - pi * ai, pr * ai + pi * ar))
    for n, k in enumerate((1, 2, 4)):
        keep = (row < 8 - k) if reverse else (row >= k)
        k_ref[2 * n] = jnp.where(keep, jnp.broadcast_to(pw[k - 1][0], (8, SSM_LANES)), 0.0)
        k_ref[2 * n + 1] = jnp.where(keep, jnp.broadcast_to(pw[k - 1][1], (8, SSM_LANES)), 0.0)
    cr = jnp.zeros((8, SSM_LANES), F32)
    ci = jnp.zeros((8, SSM_LANES), F32)
    for r in range(8):
        e = (8 - r) if reverse else (r + 1)
        cr = jnp.where(row == r, jnp.broadcast_to(pw[e - 1][0], (8, SSM_LANES)), cr)
        ci = jnp.where(row == r, jnp.broadcast_to(pw[e - 1][1], (8, SSM_LANES)), ci)
    k_ref[6] = cr
    k_ref[7] = ci


def _scan_tile(xr, xi, k_ref, car, cai, reverse):
    for n, k in enumerate((1, 2, 4)):
        sh = (8 - k) if reverse else k
        sr = pltpu.roll(xr, sh, 0)
        si = pltpu.roll(xi, sh, 0)
        mr, mi = k_ref[2 * n], k_ref[2 * n + 1]
        xr, xi = xr + mr * sr - mi * si, xi + mr * si + mi * sr
    pr, pi = k_ref[6], k_ref[7]
    xr, xi = xr + pr * car - pi * cai, xi + pr * cai + pi * car
    return xr, xi


US_BLOCK = (3 * ATT_WIDTH) // SSM_WIDTH


def _ssm_scan_fwd(proj3, abar, w_bu, w_c):
    B, S, _ = proj3.shape
    ch = min(S, SCAN_CHUNK)
    u_spec = pl.BlockSpec((1, ch, SSM_WIDTH), lambda b, c: (b, c, US_BLOCK))
    x_spec = pl.BlockSpec((1, ch, 2 * SSM_LANES), lambda b, c: (b, c, 0))
    y_spec = pl.BlockSpec((1, ch, SSM_WIDTH), lambda b, c: (b, c, 0))
    w_spec = pl.BlockSpec((SSM_WIDTH, 2 * SSM_LANES), lambda b, c: (0, 0))

    def body(ab_ref, u_ref, wb_ref, wc_ref, x_ref, y_ref, k_ref, carry_ref):
        _scan_consts(ab_ref[0:1, :], ab_ref[1:2, :], k_ref, False)

        @pl.when(pl.program_id(1) == 0)
        def _():
            carry_ref[...] = jnp.zeros_like(carry_ref)

        x_ref[0] = jnp.dot(u_ref[0], wb_ref[...], preferred_element_type=F32)

        def step(i, carry):
            base = pl.multiple_of(i * 8, 8)
            xr = x_ref[0, pl.ds(base, 8), 0:SSM_LANES]
            xi = x_ref[0, pl.ds(base, 8), SSM_LANES:]
            xr, xi = _scan_tile(xr, xi, k_ref, carry[0], carry[1], False)
            x_ref[0, pl.ds(base, 8), 0:SSM_LANES] = xr
            x_ref[0, pl.ds(base, 8), SSM_LANES:] = xi
            return (jnp.broadcast_to(xr[7:8], (8, SSM_LANES)), jnp.broadcast_to(xi[7:8], (8, SSM_LANES)))

        cr, ci = lax.fori_loop(0, ch // 8, step, (carry_ref[0], carry_ref[1]))
        carry_ref[0] = cr
        carry_ref[1] = ci
        y_ref[0] = lax.dot_general(x_ref[0].astype(BF16), wc_ref[...], NT_DIMS, preferred_element_type=F32)

    return _pcall(body, name="ssm_scan_fwd",
                  out_shape=(_sds((B, S, 2 * SSM_LANES), F32), _sds((B, S, SSM_WIDTH), F32)), grid=(B, S // ch),
                  in_specs=[pl.BlockSpec((2, SSM_LANES), lambda b, c: (0, 0)), u_spec, w_spec, w_spec],
                  out_specs=(x_spec, y_spec),
                  scratch_shapes=[pltpu.VMEM((8, 8, SSM_LANES), F32), pltpu.VMEM((2, 8, SSM_LANES), F32)],
                  dims=("arbitrary", "arbitrary"))(abar, proj3, w_bu, w_c)


def _ssm_scan_bwd(proj3, dy3, xs3, abar, w_bu, w_c, dsk):
    B, S, _ = proj3.shape
    ch = min(S, SCAN_CHUNK)
    nc = S // ch
    u_spec = pl.BlockSpec((1, ch, SSM_WIDTH), lambda b, c: (b, nc - 1 - c, US_BLOCK))
    x_spec = pl.BlockSpec((1, ch, 2 * SSM_LANES), lambda b, c: (b, nc - 1 - c, 0))
    y_spec = pl.BlockSpec((1, ch, SSM_WIDTH), lambda b, c: (b, nc - 1 - c, 0))
    w_spec = pl.BlockSpec((SSM_WIDTH, 2 * SSM_LANES), lambda b, c: (0, 0))
    ab_spec = pl.BlockSpec((2, SSM_LANES), lambda b, c: (0, 0))
    d_spec = pl.BlockSpec((1, SSM_WIDTH), lambda b, c: (0, 0))

    def body(ab_ref, u_ref, dy_ref, xs_ref, wb_ref, wc_ref, d_ref, du_ref, da_ref, dwb_ref, dwc_ref,
             g_ref, k_ref, carry_ref, acc_ref):
        b, c = pl.program_id(0), pl.program_id(1)
        _scan_consts(ab_ref[0:1, :], -ab_ref[1:2, :], k_ref, True)
        row = lax.broadcasted_iota(jnp.int32, (8, SSM_LANES), 0)

        @pl.when(c == 0)
        def _():
            carry_ref[...] = jnp.zeros_like(carry_ref)

        @pl.when((c == 0) & (b == 0))
        def _():
            acc_ref[...] = jnp.zeros_like(acc_ref)
            dwb_ref[...] = jnp.zeros_like(dwb_ref)
            dwc_ref[...] = jnp.zeros_like(dwc_ref)

        dy = dy_ref[0]
        dyb = dy.astype(BF16)
        g_ref[...] = jnp.dot(dyb, wc_ref[...], preferred_element_type=F32)

        def step(i, carry):
            car, cai, ar_acc, ai_acc = carry
            base = pl.multiple_of((ch // 8 - 1 - i) * 8, 8)
            gr = g_ref[pl.ds(base, 8), 0:SSM_LANES]
            gi = g_ref[pl.ds(base, 8), SSM_LANES:]
            gr, gi = _scan_tile(gr, gi, k_ref, car, cai, True)
            g_ref[pl.ds(base, 8), 0:SSM_LANES] = gr
            g_ref[pl.ds(base, 8), SSM_LANES:] = gi
            nr = jnp.where(row == 7, car, pltpu.roll(gr, 7, 0))
            ni = jnp.where(row == 7, cai, pltpu.roll(gi, 7, 0))
            xr = xs_ref[0, pl.ds(base, 8), 0:SSM_LANES]
            xi = xs_ref[0, pl.ds(base, 8), SSM_LANES:]
            ar_acc = ar_acc + nr * xr + ni * xi
            ai_acc = ai_acc + ni * xr - nr * xi
            return (jnp.broadcast_to(gr[0:1], (8, SSM_LANES)), jnp.broadcast_to(gi[0:1], (8, SSM_LANES)), ar_acc, ai_acc)

        cr, ci, ar_acc, ai_acc = lax.fori_loop(0, ch // 8, step, (carry_ref[0], carry_ref[1], acc_ref[0], acc_ref[1]))
        carry_ref[0] = cr
        carry_ref[1] = ci
        acc_ref[0] = ar_acc
        acc_ref[1] = ai_acc
        da_ref[0:1, :] = jnp.sum(ar_acc, axis=0, keepdims=True)
        da_ref[1:2, :] = jnp.sum(ai_acc, axis=0, keepdims=True)

        gb = g_ref[...].astype(BF16)
        du = lax.dot_general(gb, wb_ref[...], NT_DIMS, preferred_element_type=F32) + d_ref[...] * dy
        du_ref[0] = du.astype(BF16)
        xb = xs_ref[0].astype(BF16)
        u = u_ref[0]
        for j in range(2 * SSM_LANES // SSM_WIDTH):
            rows = slice((j % (SSM_LANES // SSM_WIDTH)) * 64, (j % (SSM_LANES // SSM_WIDTH)) * 64 + 64)
            cols = slice(j * SSM_WIDTH, (j + 1) * SSM_WIDTH)
            dwb_ref[rows, cols] += lax.dot_general(u[:, rows], gb[:, cols], TN_DIMS, preferred_element_type=F32)
            dwc_ref[rows, cols] += lax.dot_general(dyb[:, rows], xb[:, cols], TN_DIMS, preferred_element_type=F32)

    mat = _sds((SSM_WIDTH, 2 * SSM_LANES), F32)
    return _pcall(body, name="ssm_scan_bwd",
                  out_shape=(_sds((B, S, SSM_WIDTH), BF16), _sds((2, SSM_LANES), F32), mat, mat), grid=(B, nc),
                  in_specs=[ab_spec, u_spec, y_spec, x_spec, w_spec, w_spec, d_spec],
                  out_specs=(y_spec, ab_spec, w_spec, w_spec),
                  scratch_shapes=[pltpu.VMEM((ch, 2 * SSM_LANES), F32), pltpu.VMEM((8, 8, SSM_LANES), F32),
                                  pltpu.VMEM((2, 8, SSM_LANES), F32), pltpu.VMEM((2, 8, SSM_LANES), F32)],
                  dims=("arbitrary", "arbitrary"))(abar, proj3, dy3, xs3, w_bu, w_c, dsk)


GELU_K = math.sqrt(2.0 / math.pi)
GELU_C = 0.044715


def _gelu_parts(y):
    t = jnp.tanh(GELU_K * (y + GELU_C * y * y * y))
    return 0.5 * y * (1.0 + t), t


def _ssm_post(yc, us, dsk, wglu, bglu):
    T, N = yc.shape
    tm = min(T, 1024)
    row = pl.BlockSpec((tm, N), lambda i: (i, 0))
    vec = pl.BlockSpec((1, N), lambda i: (0, 0))
    mat = pl.BlockSpec((N, N), lambda i: (0, 0))

    def body(yc_ref, us_ref, d_ref, w_ref, b_ref, y_ref, s_ref):
        y = yc_ref[...] + d_ref[...] * us_ref[...]
        y_ref[...] = y
        z, _ = _gelu_parts(y)
        gl = jnp.dot(z.astype(BF16), w_ref[...], preferred_element_type=F32) + b_ref[...]
        s_ref[...] = (z * _sig(gl)).astype(BF16)

    return _pcall(body, name="ssm_post", out_shape=(_sds((T, N), F32), _sds((T, N), BF16)), grid=(T // tm,),
                  in_specs=[row, row, vec, mat, vec], out_specs=(row, row), dims=("parallel",))(yc, us, dsk, wglu, bglu)


def _ssm_post_bwd(y5, us, ds, dsk, wglu, bglu):
    T, N = y5.shape
    tm = min(T, 1024)
    row = pl.BlockSpec((tm, N), lambda i: (i, 0))
    vec = pl.BlockSpec((1, N), lambda i: (0, 0))
    mat = pl.BlockSpec((N, N), lambda i: (0, 0))

    def body(y_ref, us_ref, ds_ref, d_ref, w_ref, b_ref, dy_ref, dd_ref, db_ref, dw_ref):
        @pl.when(pl.program_id(0) == 0)
        def _():
            dd_ref[...] = jnp.zeros_like(dd_ref)
            db_ref[...] = jnp.zeros_like(db_ref)
            dw_ref[...] = jnp.zeros_like(dw_ref)

        y = y_ref[...]
        z, t = _gelu_parts(y)
        zb = z.astype(BF16)
        gl = jnp.dot(zb, w_ref[...], preferred_element_type=F32) + b_ref[...]
        sg = _sig(gl)
        ds = ds_ref[...]
        dgl = ds * z * sg * (1.0 - sg)
        dglb = dgl.astype(BF16)
        dz = ds * sg + lax.dot_general(dglb, w_ref[...], (((1,), (1,)), ((), ())), preferred_element_type=F32)
        dgelu = 0.5 * (1.0 + t) + 0.5 * y * (1.0 - t * t) * GELU_K * (1.0 + 3.0 * GELU_C * y * y)
        dy = dz * dgelu
        dy_ref[...] = dy
        dd_ref[...] += jnp.sum(dy * us_ref[...], axis=0, keepdims=True)
        db_ref[...] += jnp.sum(dgl, axis=0, keepdims=True)
        dw_ref[...] += lax.dot_general(zb, dglb, (((0,), (0,)), ((), ())), preferred_element_type=F32)

    return _pcall(body, name="ssm_post_bwd",
                  out_shape=(_sds((T, N), F32), _sds((1, N), F32), _sds((1, N), F32), _sds((N, N), F32)),
                  grid=(T // tm,), in_specs=[row, row, row, vec, mat, vec], out_specs=(row, vec, vec, mat),
                  dims=("arbitrary",))(y5, us, ds, dsk, wglu, bglu)


GATE_TILE = 256
GATE_ATT_BLOCK0 = (3 * ATT_WIDTH + SSM_WIDTH) // GATE_TILE
GATE_SSM_BLOCK0 = (3 * ATT_WIDTH + SSM_WIDTH + D_MODEL) // GATE_TILE


def _merge(proj, y_att, y_ssm, b_gate):
    T = proj.shape[0]
    tm = min(T, 4096)
    nj = D_MODEL // GATE_TILE
    ga = pl.BlockSpec((tm, GATE_TILE), lambda i, j: (i, GATE_ATT_BLOCK0 + j))
    gs = pl.BlockSpec((tm, GATE_TILE), lambda i, j: (i, GATE_SSM_BLOCK0 + j))
    yy = pl.BlockSpec((tm, GATE_TILE), lambda i, j: (i, j))
    ba = pl.BlockSpec((1, GATE_TILE), lambda i, j: (0, j))
    bs = pl.BlockSpec((1, GATE_TILE), lambda i, j: (0, nj + j))

    def body(ga_ref, gs_ref, ya_ref, ys_ref, ba_ref, bs_ref, o_ref):
        o_ref[...] = (_sig(ga_ref[...] + ba_ref[...]) * ya_ref[...]
                      + _sig(gs_ref[...] + bs_ref[...]) * ys_ref[...]).astype(BF16)

    return _pcall(body, name="merge", out_shape=_sds((T, D_MODEL), BF16), grid=(T // tm, nj),
                  in_specs=[ga, gs, yy, yy, ba, bs], out_specs=yy, dims=("parallel", "parallel"))(
        proj, proj, y_att, y_ssm, b_gate, b_gate)


def _merge_bwd(proj, y_att, y_ssm, b_gate, dmerged):
    T = proj.shape[0]
    tm = min(T, 2048)
    nj = D_MODEL // GATE_TILE
    ga = pl.BlockSpec((tm, GATE_TILE), lambda j, i: (i, GATE_ATT_BLOCK0 + j))
    gs = pl.BlockSpec((tm, GATE_TILE), lambda j, i: (i, GATE_SSM_BLOCK0 + j))
    yy = pl.BlockSpec((tm, GATE_TILE), lambda j, i: (i, j))
    ba = pl.BlockSpec((1, GATE_TILE), lambda j, i: (0, j))
    bs = pl.BlockSpec((1, GATE_TILE), lambda j, i: (0, nj + j))

    def body(ga_ref, gs_ref, ya_ref, ys_ref, ba_ref, bs_ref, dm_ref, dya_ref, dys_ref, dga_ref, dgs_ref, dba_ref, dbs_ref):
        @pl.when(pl.program_id(1) == 0)
        def _():
            dba_ref[...] = jnp.zeros_like(dba_ref)
            dbs_ref[...] = jnp.zeros_like(dbs_ref)

        dm = dm_ref[...].astype(F32)
        sa = _sig(ga_ref[...] + ba_ref[...])
        ss = _sig(gs_ref[...] + bs_ref[...])
        dya_ref[...] = (dm * sa).astype(BF16)
        dys_ref[...] = (dm * ss).astype(BF16)
        dga = dm * ya_ref[...] * sa * (1.0 - sa)
        dgs = dm * ys_ref[...] * ss * (1.0 - ss)
        dga_ref[...] = dga.astype(BF16)
        dgs_ref[...] = dgs.astype(BF16)
        dba_ref[...] += jnp.sum(dga, axis=0, keepdims=True)
        dbs_ref[...] += jnp.sum(dgs, axis=0, keepdims=True)

    big = _sds((T, D_MODEL), BF16)
    vec = _sds((1, D_MODEL), F32)
    return _pcall(body, name="merge_bwd", out_shape=(big, big, big, big, vec, vec), grid=(nj, T // tm),
                  in_specs=[ga, gs, yy, yy, ba, bs, yy], out_specs=(yy, yy, yy, yy, ba, ba),
                  dims=("arbitrary", "arbitrary"))(proj, proj, y_att, y_ssm, b_gate, b_gate, dmerged)


CONV_TILE = 256


def _shift_rows(a, j, up=False):
    n = a.shape[0]
    r = pltpu.roll(a, n - j if up else j, 0)
    row = lax.broadcasted_iota(jnp.int32, (8, a.shape[1]), 0)
    if up:
        return jnp.concatenate([r[:n - 8], jnp.where(row < 8 - j, r[n - 8:], 0.0)], axis=0)
    return jnp.concatenate([jnp.where(row >= j, r[:8], 0.0), r[8:]], axis=0)


def _conv_pre(a, w_ref, b_ref):
    conv = b_ref[...] + w_ref[0:1, :] * a
    shifted = []
    for j in (1, 2):
        sh = _shift_rows(a, j)
        shifted.append(sh)
        conv = conv + w_ref[j:j + 1, :] * sh
    return conv, shifted


def _conv_act(up3, w_conv, b_conv):
    B, S, _ = up3.shape
    nj = D_FF // CONV_TILE
    a_spec = pl.BlockSpec((1, S, CONV_TILE), lambda b, j: (b, 0, j))
    v_spec = pl.BlockSpec((1, S, CONV_TILE), lambda b, j: (b, 0, nj + j))
    w_spec = pl.BlockSpec((3, CONV_TILE), lambda b, j: (0, j))
    b_spec = pl.BlockSpec((1, CONV_TILE), lambda b, j: (0, j))

    def body(a_ref, v_ref, w_ref, b_ref, o_ref):
        a = a_ref[0].astype(F32)
        conv, _ = _conv_pre(a, w_ref, b_ref)
        o_ref[0] = (conv * _sig(conv) * v_ref[0]).astype(BF16)

    return _pcall(body, name="conv_act", out_shape=_sds((B, S, D_FF), BF16), grid=(B, nj),
                  in_specs=[a_spec, v_spec, w_spec, b_spec], out_specs=a_spec, dims=("parallel", "parallel"))(
        up3, up3, w_conv, b_conv)


def _conv_bwd(up3, dact3, w_conv, b_conv):
    B, S, _ = up3.shape
    nj = D_FF // CONV_TILE
    a_spec = pl.BlockSpec((1, S, CONV_TILE), lambda j, b: (b, 0, j))
    v_spec = pl.BlockSpec((1, S, CONV_TILE), lambda j, b: (b, 0, nj + j))
    o_spec = pl.BlockSpec((2, 1, S, CONV_TILE), lambda j, b: (0, b, 0, j))
    w_spec = pl.BlockSpec((3, CONV_TILE), lambda j, b: (0, j))
    b_spec = pl.BlockSpec((1, CONV_TILE), lambda j, b: (0, j))

    def body(a_ref, v_ref, d_ref, w_ref, b_ref, dup_ref, dw_ref, db_ref):
        @pl.when(pl.program_id(1) == 0)
        def _():
            dw_ref[...] = jnp.zeros_like(dw_ref)
            db_ref[...] = jnp.zeros_like(db_ref)

        a = a_ref[0].astype(F32)
        d = d_ref[0].astype(F32)
        conv, shifted = _conv_pre(a, w_ref, b_ref)
        sg = _sig(conv)
        dup_ref[1, 0] = (d * conv * sg).astype(BF16)
        dconv = d * v_ref[0] * (sg * (1.0 + conv * (1.0 - sg)))
        da = w_ref[0:1, :] * dconv
        for j in (1, 2):
            da = da + w_ref[j:j + 1, :] * _shift_rows(dconv, j, up=True)
        dup_ref[0, 0] = da.astype(BF16)
        db_ref[...] += jnp.sum(dconv, axis=0, keepdims=True)
        dw_ref[0:1, :] += jnp.sum(dconv * a, axis=0, keepdims=True)
        dw_ref[1:2, :] += jnp.sum(dconv * shifted[0], axis=0, keepdims=True)
        dw_ref[2:3, :] += jnp.sum(dconv * shifted[1], axis=0, keepdims=True)

    return _pcall(body, name="conv_bwd",
                  out_shape=(_sds((2, B, S, D_FF), BF16), _sds((3, D_FF), F32), _sds((1, D_FF), F32)),
                  grid=(nj, B), in_specs=[a_spec, v_spec, a_spec, w_spec, b_spec],
                  out_specs=(o_spec, w_spec, b_spec), dims=("arbitrary", "arbitrary"))(up3, up3, dact3, w_conv, b_conv)


def _rows_tile(r, cap=640):
    for t in range(min(r, cap) - min(r, cap) % 8, 7, -8):
        if r % t == 0:
            return t
    return r


def _add2(a, b, out_dtype, name):
    R, N = a.shape
    tr = _rows_tile(R)
    spec = pl.BlockSpec((tr, N), lambda i: (i, 0))

    def body(a_ref, b_ref, o_ref):
        o_ref[...] = (a_ref[...] + b_ref[...]).astype(out_dtype)

    return _pcall(body, name=name, out_shape=_sds((R, N), out_dtype), grid=(R // tr,), in_specs=[spec, spec],
                  out_specs=spec, dims=("parallel",))(a, b)


def _sum_slots(q, name):
    n, R, N = q.shape
    tr = _rows_tile(R)

    def body(q_ref, o_ref):
        acc = q_ref[0].astype(F32)
        for s in range(1, n):
            acc = acc + q_ref[s].astype(F32)
        o_ref[...] = acc

    return _pcall(body, name=name, out_shape=_sds((R, N), F32), grid=(R // tr,),
                  in_specs=[pl.BlockSpec((n, tr, N), lambda i: (0, i, 0))], out_specs=pl.BlockSpec((tr, N), lambda i: (i, 0)),
                  dims=("parallel",))(q)


NATIVE = (("b_re", 16, 1024), ("b_im", 16, 1024), ("c_re", 16, 1024), ("c_im", 16, 1024), ("g_mix", 1, 1024),
          ("b_att", 1, 1024), ("b_ssm", 1, 1024), ("a_re", 1, 1024), ("a_im", 1, 1024), ("log_dt", 1, 128),
          ("d_skip", 1, 256), ("b_glu", 1, 256), ("g_ffn", 1, 1024), ("g_final", 1, 1024), ("b_conv", 1, 2048),
          ("w_conv", 3, 2048), ("loss", 1, 1))
N_MOD = 6
NATIVE_LATE = ("g_mix",)
MODS_LATE = (0, 1)


def _small_plan(late):
    pieces = [p for p in NATIVE if (p[0] in NATIVE_LATE) == late]
    mods = [k for k in range(N_MOD) if (k in MODS_LATE) == late]
    starts, r = {}, 0
    for name, rows, cols in pieces:
        starts[name] = r
        r += rows * (-(-cols // LANES))
    return pieces, mods, starts, -(-r // 8) * 8


def _pack_small(native, dmods, late):
    pieces, mods, starts, n_sum = _small_plan(late)
    B = dmods[mods[0]].shape[0]
    total = n_sum + 8 * len(mods)

    def body(*refs):
        xs, ms, o_ref = refs[:len(pieces)], refs[len(pieces):-1], refs[-1]
        o_ref[...] = jnp.zeros_like(o_ref)
        for (name, rows, cols), x_ref in zip(pieces, xs):
            chunks = -(-cols // LANES)
            if chunks == 1 and rows % 8 == 0:
                o_ref[starts[name]:starts[name] + rows, 0:cols] = x_ref[...]
                continue
            for i in range(rows):
                for q in range(chunks):
                    wd = min(LANES, cols - q * LANES)
                    r = starts[name] + i * chunks + q
                    o_ref[r:r + 1, 0:wd] = x_ref[i:i + 1, q * LANES:q * LANES + wd]
        for k, m_ref in enumerate(ms):
            for b in range(B):
                o_ref[n_sum + 8 * k + b:n_sum + 8 * k + b + 1, :] = m_ref[b]

    return _pcall(body, name="pack_small_late" if late else "pack_small_early", out_shape=_sds((total, LANES), F32))(
        *[native[n] for n, _, _ in pieces], *[dmods[k] for k in mods])


def _sum_unpack_small(gathered_early, gathered_late, B):
    plans = [_small_plan(False), _small_plan(True)]
    nd = gathered_early.shape[0]
    n_out = len(NATIVE)

    def body(*refs):
        g_refs, outs, dm_ref, accs = refs[0:2], refs[2:2 + n_out], refs[2 + n_out], refs[3 + n_out:]
        o = 0
        for g_ref, acc, (pieces, mods, starts, n_sum) in zip(g_refs, accs, plans):
            s = g_ref[0, 0:n_sum, :]
            for d in range(1, nd):
                s = s + g_ref[d, 0:n_sum, :]
            acc[...] = s
            for name, rows, cols in pieces:
                o_ref = outs[o]
                o += 1
                chunks = -(-cols // LANES)
                if chunks == 1 and rows % 8 == 0:
                    o_ref[...] = acc[starts[name]:starts[name] + rows, 0:cols]
                    continue
                for i in range(rows):
                    for q in range(chunks):
                        wd = min(LANES, cols - q * LANES)
                        r = starts[name] + i * chunks + q
                        o_ref[i:i + 1, q * LANES:q * LANES + wd] = acc[r:r + 1, 0:wd]
            for d in range(nd):
                for j, k in enumerate(mods):
                    dm_ref[d, :, k * D_MODEL:(k + 1) * D_MODEL] = g_ref[d, n_sum + 8 * j:n_sum + 8 * j + B, :]

    ordered = [p for pieces, _, _, _ in plans for p in pieces]
    out_shape = tuple(_sds((rows, cols), F32) for _, rows, cols in ordered) + (_sds((nd, B, N_MOD * D_MODEL), F32),)
    res = _pcall(body, name="sum_unpack_small", out_shape=out_shape,
                 scratch_shapes=[pltpu.VMEM((n_sum, LANES), F32) for _, _, _, n_sum in plans])(gathered_early, gathered_late)
    return {n: r for (n, _, _), r in zip(ordered, res[:-1])}, res[-1]


def _small_from_native(nat):
    lanes3 = lambda a: a.reshape(SSM_GROUP_CH, SSM_GROUPS, SSM_STATE)
    return dict(
        g_mix=nat["g_mix"].reshape(D_MODEL), b_gate=jnp.concatenate([nat["b_att"], nat["b_ssm"]], axis=1).reshape(2 * D_MODEL),
        a_re=nat["a_re"].reshape(SSM_GROUPS, SSM_STATE), a_im=nat["a_im"].reshape(SSM_GROUPS, SSM_STATE),
        log_dt=nat["log_dt"][0, :SSM_GROUPS], b_re=_groups_from_lanes(nat["b_re"]), b_im=_groups_from_lanes(nat["b_im"]),
        c_re=lanes3(nat["c_re"]).transpose(1, 0, 2), c_im=lanes3(nat["c_im"]).transpose(1, 0, 2),
        d_skip=nat["d_skip"].reshape(SSM_WIDTH), b_glu=nat["b_glu"].reshape(SSM_WIDTH), g_ffn=nat["g_ffn"].reshape(D_MODEL),
        w_conv=nat["w_conv"], b_conv=nat["b_conv"].reshape(D_FF), g_final=nat["g_final"].reshape(D_MODEL))


def _adamw_multi(params):
    n = len(params)
    bc1 = 1.0 - ADAM_B1 ** ADAM_STEP
    bc2 = 1.0 - ADAM_B2 ** ADAM_STEP

    def body(*refs):
        ins, outs = refs[:4 * n], refs[4 * n:]
        for i in range(n):
            w_ref, g_ref, m_ref, v_ref = ins[4 * i:4 * i + 4]
            d_ref, nm_ref, nv_ref = outs[3 * i:3 * i + 3]
            g = g_ref[...]
            m = ADAM_B1 * m_ref[...] + (1.0 - ADAM_B1) * g
            v = ADAM_B2 * v_ref[...] + (1.0 - ADAM_B2) * (g * g)
            nm_ref[...] = m
            nv_ref[...] = v
            d_ref[...] = -ADAM_LR * ((m / bc1) / (jnp.sqrt(v / bc2) + ADAM_EPS) + ADAM_WD * w_ref[...])

    flat = [a for p in params for a in p]
    out_shape = tuple(_sds(p[0].shape, F32) for p in params for _ in range(3))
    res = _pcall(body, name="adamw_small", out_shape=out_shape)(*flat)
    return [tuple(res[3 * i:3 * i + 3]) for i in range(n)]


def _adamw(w, g, m, v, name, g_other=None):
    R, N = w.shape
    tr = _rows_tile(R, 256)
    spec = pl.BlockSpec((tr, N), lambda i: (i, 0))
    bc1 = 1.0 - ADAM_B1 ** ADAM_STEP
    bc2 = 1.0 - ADAM_B2 ** ADAM_STEP
    two = g_other is not None

    def body(*refs):
        w_ref, g_ref, m_ref, v_ref = refs[:4]
        d_ref, nm_ref, nv_ref = refs[4 + two:7 + two]
        g = g_ref[...]
        if two:
            g = g + refs[4][...]
            refs[8][...] = g
        m = ADAM_B1 * m_ref[...] + (1.0 - ADAM_B1) * g
        v = ADAM_B2 * v_ref[...] + (1.0 - ADAM_B2) * (g * g)
        nm_ref[...] = m
        nv_ref[...] = v
        d_ref[...] = -ADAM_LR * ((m / bc1) / (jnp.sqrt(v / bc2) + ADAM_EPS) + ADAM_WD * w_ref[...])

    shp = _sds((R, N), F32)
    args = (w, g, m, v) + ((g_other,) if two else ())
    return _pcall(body, name=name, out_shape=(shp,) * (3 + two), grid=(R // tr,), in_specs=[spec] * len(args),
                  out_specs=(spec,) * (3 + two), dims=("parallel",))(*args)


_GROUP_MASKS = {
    "all": [(dx, dy, dc) for dx in (0, 1) for dy in (0, 1) for dc in (0, 1) if (dx, dy, dc) != (0, 0, 0)],
    "xy": [(1, 0, 0), (0, 1, 0), (1, 1, 0)],
    "c": [(0, 0, 1)],
}
_GROUP_SLOTS = {"all": 8, "xy": 4, "c": 2}


def _group_slot(group, x, y, c):
    return {"all": 4 * x + 2 * y + c, "xy": 2 * x + y, "c": c}[group]


def _flip(v, d):
    return 1 - v if d else v


def _exchange(arr, group, mode, name):
    return _exchange_list([arr], group, mode, name)[0]


def _exchange_list(arrs, group, mode, name):
    masks = _GROUP_MASKS[group]
    n = len(masks)
    na = len(arrs)
    assert mode in ("gather", "swap") and (mode == "gather" or group == "c")
    has_local = mode == "gather"
    out_shapes = [((_GROUP_SLOTS[group],) if has_local else ()) + arr.shape for arr in arrs]
    bounce = [pltpu.VMEM(arr.shape, arr.dtype) for arr in arrs] if has_local else []

    def body(*refs):
        x_refs, o_refs = refs[:na], refs[na:2 * na]
        send_sems, recv_sems = refs[2 * na], refs[2 * na + 1]
        x, y, c = lax.axis_index("x"), lax.axis_index("y"), lax.axis_index("c")
        me = _group_slot(group, x, y, c)
        if has_local:
            local_sems = refs[2 * na + 2]
            bufs = refs[2 * na + 3:]
            loads = []
            for i in range(na):
                loads.append(pltpu.make_async_copy(x_refs[i], bufs[i], local_sems.at[2 * i]))
                loads[-1].start()
        copies = []
        for i in range(na):
            x_ref, o_ref = x_refs[i], o_refs[i]
            for k, (dx, dy, dc) in enumerate(masks):
                px, py, pc = _flip(x, dx), _flip(y, dy), _flip(c, dc)
                src, dst = (x_ref, o_ref.at[me]) if has_local else (x_ref, o_ref)
                cp =pltpu.make_async_remote_copy(src_ref=src, dst_ref=dst, send_sem=send_sems.at[i * n + k],
                                                  recv_sem=recv_sems.at[i * n + k], device_id=(px, py, pc),
                                                  device_id_type=pl.DeviceIdType.MESH)
                cp.start()
                copies.append(cp)
        if has_local:
            stores = []
            for i in range(na):
                loads[i].wait()
                stores.append(pltpu.make_async_copy(bufs[i], o_refs[i].at[me], local_sems.at[2 * i + 1]))
                stores[-1].start()
        for cp in copies:
            cp.wait()
        if has_local:
            for st in stores:
                st.wait()

    anyspec = pl.BlockSpec(memory_space=pl.ANY)
    scratch = [pltpu.SemaphoreType.DMA((n * na,)), pltpu.SemaphoreType.DMA((n * na,))]
    if has_local:
        scratch += [pltpu.SemaphoreType.DMA((2 * na,))] + bounce
    outs = pl.pallas_call(body, name=name, out_shape=tuple(_sds(s, a.dtype) for s, a in zip(out_shapes, arrs)),
                          in_specs=[anyspec] * na, out_specs=tuple([anyspec] * na), scratch_shapes=scratch,
                          compiler_params=pltpu.CompilerParams(vmem_limit_bytes=V7X_VMEM_LIMIT_BYTES))(*arrs)
    return list(outs)


def _gather_weights(shards, name):
    na = len(shards)
    masks = _GROUP_MASKS["xy"]
    n = len(masks)

    def body(*refs):
        x_refs, o_refs = refs[:na], refs[na:2 * na]
        send_sems, recv_sems, local_sems = refs[2 * na:2 * na + 3]
        bufs = refs[2 * na + 3:]
        x, y, c = lax.axis_index("x"), lax.axis_index("y"), lax.axis_index("c")
        me = 2 * x + y
        sibling = (x, y, 1 - c)
        loads = []
        for i in range(na):
            loads.append(pltpu.make_async_copy(x_refs[i], bufs[i], local_sems.at[2 * i]))
            loads[-1].start()

        def half_of(i, slot, cc):
            h = shards[i].shape[0] // 2
            return o_refs[i].at[slot, pl.ds(pl.multiple_of(cc * h, 8), h), :]

        def src_half(i, cc):
            h = shards[i].shape[0] // 2
            return x_refs[i].at[pl.ds(pl.multiple_of(cc * h, 8), h), :]

        sends = []
        for i in range(na):
            for k, (dx, dy, _) in enumerate(masks):
                cp = pltpu.make_async_remote_copy(src_ref=src_half(i, c), dst_ref=half_of(i, me, c),
                                                  send_sem=send_sems.at[i * 2 * n + k], recv_sem=recv_sems.at[i * 2 * n + k],
                                                  device_id=(_flip(x, dx), _flip(y, dy), c),
                                                  device_id_type=pl.DeviceIdType.MESH)
                cp.start()
                sends.append(cp)
        stores = []
        for i in range(na):
            loads[i].wait()
            stores.append(pltpu.make_async_copy(bufs[i], o_refs[i].at[me], local_sems.at[2 * i + 1]))
            stores[-1].start()
        for i in range(na):
            for k, (dx, dy, _) in enumerate(masks):
                slot = 2 * _flip(x, dx) + _flip(y, dy)
                landed = pltpu.make_async_remote_copy(src_ref=src_half(i, c), dst_ref=half_of(i, slot, c),
                                                      send_sem=send_sems.at[i * 2 * n + k],
                                                      recv_sem=recv_sems.at[i * 2 * n + k], device_id=sibling,
                                                      device_id_type=pl.DeviceIdType.MESH)
                landed.wait_recv()
                fwd = pltpu.make_async_remote_copy(src_ref=half_of(i, slot, c), dst_ref=half_of(i, slot, c),
                                                   send_sem=send_sems.at[i * 2 * n + n + k],
                                                   recv_sem=recv_sems.at[i * 2 * n + n + k], device_id=sibling,
                                                   device_id_type=pl.DeviceIdType.MESH)
                fwd.start()
                sends.append(fwd)
        for i in range(na):
            for k, (dx, dy, _) in enumerate(masks):
                slot = 2 * _flip(x, dx) + _flip(y, dy)
                pltpu.make_async_remote_copy(src_ref=half_of(i, slot, 1 - c), dst_ref=half_of(i, slot, 1 - c),
                                             send_sem=send_sems.at[i * 2 * n + n + k],
                                             recv_sem=recv_sems.at[i * 2 * n + n + k], device_id=sibling,
                                             device_id_type=pl.DeviceIdType.MESH).wait_recv()
        for cp in sends:
            cp.wait_send()
        for st in stores:
            st.wait()

    anyspec = pl.BlockSpec(memory_space=pl.ANY)
    scratch = [pltpu.SemaphoreType.DMA((2 * n * na,)), pltpu.SemaphoreType.DMA((2 * n * na,)),
               pltpu.SemaphoreType.DMA((2 * na,))] + [pltpu.VMEM(s.shape, s.dtype) for s in shards]
    outs = pl.pallas_call(body, name=name, out_shape=tuple(_sds((N_XY,) + s.shape, s.dtype) for s in shards),
                          in_specs=[anyspec] * na, out_specs=tuple([anyspec] * na), scratch_shapes=scratch,
                          compiler_params=pltpu.CompilerParams(vmem_limit_bytes=V7X_VMEM_LIMIT_BYTES))(*shards)
    return list(outs)


BIG = (("w_proj_att", (ATT_WIDTH, D_MODEL), 1), ("w_proj_ssm", (SSM_WIDTH, D_MODEL), 1),
       ("w_glu", (SSM_WIDTH, SSM_WIDTH), 0))
DIRECT = (("w_in", True), ("w_up", True), ("w_down", False), ("w_out", False))
N_XY = 4


def _big_rows(shape):
    return shape[0] * shape[1] // N_XY // LANES


FLAT_ROWS = sum(_big_rows(s) for _, s, _ in BIG)


def _shard_shape(shape, axis):
    return (shape[0] // N_XY, shape[1]) if axis == 0 else (shape[0], shape[1] // N_XY)


def _flatten_shards(shards):
    return jnp.concatenate([shards[n].reshape(_big_rows(s), LANES) for n, s, _ in BIG], axis=0)


def _unflatten_shard(flat):
    out, r = {}, 0
    for n, s, ax in BIG:
        k = _big_rows(s)
        out[n] = flat[r:r + k].reshape(_shard_shape(s, ax))
        r += k
    return out


def _unflatten_full(flat4):
    out, r = {}, 0
    for n, s, ax in BIG:
        k = _big_rows(s)
        sh = _shard_shape(s, ax)
        t = flat4[:, r:r + k].reshape((N_XY,) + sh)
        out[n] = t.reshape(s) if ax == 0 else t.transpose(1, 0, 2).reshape(s)
        r += k
    return out


def _flatten_full(full):
    parts = []
    for n, s, ax in BIG:
        sh = _shard_shape(s, ax)
        t = full[n]
        t = t.reshape((N_XY,) + sh) if ax == 0 else t.reshape(s[0], N_XY, sh[1]).transpose(1, 0, 2)
        parts.append(t.reshape(N_XY, _big_rows(s), LANES))
    return jnp.concatenate(parts, axis=1)


def _lanes_from_groups(a):
    return a.transpose(2, 0, 1).reshape(SSM_GROUP_CH, SSM_LANES)


def _groups_from_lanes(a):
    return a.reshape(SSM_GROUP_CH, SSM_GROUPS, SSM_STATE).transpose(1, 2, 0)


LATE = ("w_up_t", "w_down", "w_out")
EARLY_GRADS = ("w_up_t", "w_down", "w_out")


def _local_step(x3, mod, tgt3, W, P, late_shards=None, scatter_grads=False):
    B, S, _ = x3.shape
    T = B * S
    seq_blocks = S // ATT_BLOCK
    sh1, sc1, gt1, sh2, sc2, gt2 = [m.reshape(B, 1, D_MODEL) for m in jnp.split(mod, 6, axis=-1)]
    g_mix, g_ffn, g_final = P["g_mix"].reshape(1, D_MODEL), P["g_ffn"].reshape(1, D_MODEL), P["g_final"].reshape(1, D_MODEL)
    b_gate = P["b_gate"].reshape(1, 2 * D_MODEL)
    d_skip, b_glu = P["d_skip"].reshape(1, SSM_WIDTH), P["b_glu"].reshape(1, SSM_WIDTH)
    w_conv, b_conv = P["w_conv"], P["b_conv"].reshape(1, D_FF)

    u1 = _norm_mod(x3, g_mix, sc1, sh1).reshape(T, D_MODEL)
    proj = _mm(u1, W["w_in_t"], tb=True, name="mm_proj", out_dtype=BF16)
    proj3 = proj.reshape(B, S, IN_WIDTH)
    us = proj[:, 3 * ATT_WIDTH:3 * ATT_WIDTH + SSM_WIDTH]
    o_att3, lse4, late = _attention_fwd(proj3, seq_blocks, _Riders(late_shards, "gather") if late_shards else None)
    if late_shards:
        W = dict(W, **{n: f.reshape(-1, LANES) for n, f in zip(LATE, late)})
        w_conv = late[len(LATE)].transpose(1, 0, 2).reshape(3, D_FF)
        W.update(_unflatten_full(late[len(LATE) + 1]))
    o_att = o_att3.reshape(T, ATT_WIDTH)
    y_att = _mm(o_att, W["w_proj_att"], name="mm_proj_att", out_dtype=BF16)

    lr = P["a_re"].reshape(1, SSM_LANES)
    li = P["a_im"].reshape(1, SSM_LANES)
    ldt = jnp.repeat(P["log_dt"], SSM_STATE).reshape(1, SSM_LANES)
    br, bi = _lanes_from_groups(P["b_re"]), _lanes_from_groups(P["b_im"])
    cr = P["c_re"].transpose(1, 0, 2).reshape(SSM_GROUP_CH, SSM_LANES)
    ci = P["c_im"].transpose(1, 0, 2).reshape(SSM_GROUP_CH, SSM_LANES)
    abar, w_bu, w_c = _ssm_params(lr, li, ldt, br, bi, cr, ci)
    xs3, y_core3 = _ssm_scan_fwd(proj3, abar, w_bu, w_c)
    y5, s_out = _ssm_post(y_core3.reshape(T, SSM_WIDTH), us, d_skip, W["w_glu"], b_glu)
    y_ssm = _mm(s_out, W["w_proj_ssm"], name="mm_proj_ssm", out_dtype=BF16)

    merged = _merge(proj, y_att, y_ssm, b_gate)
    mix = _mm(merged, W["w_out"], name="mm_out", out_dtype=BF16)
    mix3 = mix.reshape(B, S, D_MODEL)

    h1, u2 = _resid_norm_mod(x3, mix3, gt1, g_ffn, sc2, sh2)
    u2 = u2.reshape(T, D_MODEL)
    up3 = _mm(u2, W["w_up_t"], tb=True, name="mm_up", out_dtype=BF16).reshape(B, S, 2 * D_FF)
    act = _conv_act(up3, w_conv, b_conv).reshape(T, D_FF)
    ffn3 = _mm(act, W["w_down"], name="mm_down", out_dtype=BF16).reshape(B, S, D_MODEL)
    dh2, dffn, dgt2, dg_final, loss = _final_loss(h1, ffn3, tgt3, gt2, g_final)

    dffn = dffn.reshape(T, D_MODEL)
    gw = {}
    gw["w_down"] = _mm(act, dffn, ta=True, out_dtype=BF16, name="mm_dw_down")
    dact3 = _mm(dffn, W["w_down"], tb=True, name="mm_dact", out_dtype=BF16).reshape(B, S, D_FF)
    dup3, dw_conv, db_conv = _conv_bwd(up3, dact3, w_conv, b_conv)
    dup = dup3.reshape(2, T, D_FF)
    gw["w_up_t"] = _mm(dup, u2, ta=True, out_dtype=BF16, name="mm_dw_up")
    du2 = _mm(dup, W["w_up_t"], name="mm_du2", out_dtype=BF16).reshape(B, S, D_MODEL)
    dh1, dsh2, dsc2, dg_ffn, dgt1, dmix = _norm_bwd(h1, du2, dh2, g_ffn, sc2, "norm_bwd2", mix3=mix3, gt=gt1)

    dmix = dmix.reshape(T, D_MODEL)
    gw["w_out"] = _mm(merged, dmix, ta=True, out_dtype=BF16, name="mm_dw_out")
    dmerged = _mm(dmix, W["w_out"], tb=True, name="mm_dmerged", out_dtype=BF16)
    dy_att, dy_ssm, dga, dgs, db_att, db_ssm = _merge_bwd(proj, y_att, y_ssm, b_gate, dmerged)

    gw["w_proj_ssm"] = _mm(s_out, dy_ssm, ta=True, name="mm_dw_proj_ssm")
    ds_out = _mm(dy_ssm, W["w_proj_ssm"], tb=True, name="mm_ds_out")
    dy5, dd_skip, db_glu, dw_glu = _ssm_post_bwd(y5, us, ds_out, d_skip, W["w_glu"], b_glu)
    gw["w_glu"] = dw_glu
    dus3, dab, dwbu, dwc = _ssm_scan_bwd(proj3, dy5.reshape(B, S, SSM_WIDTH), xs3, abar, w_bu, w_c, d_skip)
    dus = dus3.reshape(T, SSM_WIDTH)
    dlr, dli, dldt, dbr, dbi, dcr, dci = _ssm_params_bwd(lr, li, ldt, br, bi, dab, dwbu, dwc)

    gw["w_proj_att"] = _mm(o_att, dy_att, ta=True, name="mm_dw_proj_att")
    do_att = _mm(dy_att, W["w_proj_att"], tb=True, out_dtype=BF16, name="mm_do_att")
    early = [gw[n].reshape(N_XY, -1, LANES) for n in EARLY_GRADS]
    early.append(_flatten_full({n: gw[n].astype(BF16) for n, _, _ in BIG}))
    dq3, dk3, dv3, parts = _attention_bwd(proj3, do_att.reshape(B, S, ATT_WIDTH), o_att3, lse4, seq_blocks,
                                          _Riders(early, "scatter") if scatter_grads else None)
    dproj = jnp.concatenate([t.reshape(T, ATT_WIDTH) for t in (dq3, dk3, dv3)] + [dus, dga, dgs], axis=1)
    dmods = [None, None, dgt1, dsh2, dsc2, dgt2]
    native = dict(b_att=db_att, b_ssm=db_ssm, a_re=dlr, a_im=dli, log_dt=dldt, b_re=dbr, b_im=dbi, c_re=dcr, c_im=dci,
                  d_skip=dd_skip, b_glu=db_glu, g_ffn=dg_ffn, w_conv=dw_conv, b_conv=db_conv, g_final=dg_final, loss=loss)
    small_early = _pack_small(native, dmods, False)
    if scatter_grads:
        gw["w_in_t"], (small_early,) = _mm(dproj, u1, ta=True, out_dtype=BF16, name="mm_dw_in",
                                           riders=_Riders([small_early], "gather", "all"))
        du1, last_parts = _mm(dproj, W["w_in_t"], name="mm_du1", out_dtype=BF16,
                              riders=_Riders([gw["w_in_t"].reshape(N_XY, -1, LANES)], "scatter"))
        parts = parts + last_parts
    else:
        gw["w_in_t"] = _mm(dproj, u1, ta=True, out_dtype=BF16, name="mm_dw_in")
        du1 = _mm(dproj, W["w_in_t"], name="mm_du1", out_dtype=BF16)
    du1 = du1.reshape(B, S, D_MODEL)
    sums = [_sum_slots(p, "sum_chips_%d" % i) for i, p in enumerate(parts)]
    if sums:
        dx, dsh1, dsc1, dg_mix, sums_sib = _norm_bwd(x3, du1, dh1, g_mix, sc1, "norm_bwd1",
                                                     riders=_Riders(sums, "swap", "c"))
    else:
        dx, dsh1, dsc1, dg_mix = _norm_bwd(x3, du1, dh1, g_mix, sc1, "norm_bwd1")
        sums_sib = []
    dmods[0], dmods[1] = dsh1, dsc1
    native["g_mix"] = dg_mix
    return loss, dx, dmods, gw, native, (sums, sums_sib), small_early


WEIGHTS = ['w_ada', 'b_ada', 'g_mix', 'w_in', 'b_gate', 'a_re', 'a_im', 'log_dt', 'b_re', 'b_im', 'c_re', 'c_im', 'd_skip',
           'w_glu', 'b_glu', 'w_proj_att', 'w_proj_ssm', 'w_out', 'g_ffn', 'w_up', 'w_conv', 'b_conv', 'w_down', 'g_final']
SMALL = ['g_mix', 'b_gate', 'a_re', 'a_im', 'log_dt', 'b_re', 'b_im', 'c_re', 'c_im', 'd_skip', 'b_glu', 'g_ffn', 'w_conv',
         'b_conv', 'g_final']


def kernel(x, c, w_ada, b_ada, g_mix, w_in, b_gate, a_re, a_im, log_dt, b_re, b_im, c_re, c_im, d_skip, w_glu, b_glu, w_proj_att, w_proj_ssm, w_out, g_ffn, w_up, w_conv, b_conv, w_down, g_final, loss_target, m_w_ada, m_b_ada, m_g_mix, m_w_in, m_b_gate, m_a_re, m_a_im, m_log_dt, m_b_re, m_b_im, m_c_re, m_c_im, m_d_skip, m_w_glu, m_b_glu, m_w_proj_att, m_w_proj_ssm, m_w_out, m_g_ffn, m_w_up, m_w_conv, m_b_conv, m_w_down, m_g_final, v_w_ada, v_b_ada, v_g_mix, v_w_in, v_b_gate, v_a_re, v_a_im, v_log_dt, v_b_re, v_b_im, v_c_re, v_c_im, v_d_skip, v_w_glu, v_b_glu, v_w_proj_att, v_w_proj_ssm, v_w_out, v_g_ffn, v_w_up, v_w_conv, v_b_conv, v_w_down, v_g_final):
    args = dict(locals())
    w = {n: args[n] for n in WEIGHTS}
    m = {n: args["m_" + n] for n in WEIGHTS}
    v = {n: args["v_" + n] for n in WEIGHTS}
    B, S, _ = x.shape
    ix, iy, ic = lax.axis_index("x"), lax.axis_index("y"), lax.axis_index("c")
    chip = 2 * ix + iy
    ada_cols = w_ada.shape[2]

    c_all = _exchange(c, "all", "gather", "gather_c").reshape(8 * B, D_MODEL)
    b_cols = lax.dynamic_slice_in_dim(b_ada, chip * ada_cols, ada_cols, axis=1)
    mod_cols = _ada_fwd(c_all, w_ada[0], b_cols)
    mod_all = _exchange(mod_cols, "xy", "gather", "gather_mod")
    mod_all = mod_all.transpose(1, 0, 2).reshape(8 * B, 6 * D_MODEL)
    mod = lax.dynamic_slice_in_dim(mod_all, (4 * ix + 2 * iy + ic) * B, B, axis=0)

    shard = {n + ("_t" if t else ""): (w[n][0].T if t else w[n][0]).astype(BF16) for n, t in DIRECT}
    misc = _flatten_shards({n: w[n][0] for n, _, _ in BIG}).astype(BF16)
    (w_in_full,) = _gather_weights([shard["w_in_t"]], "gather_weights")
    W = {"w_in_t": w_in_full.reshape(-1, LANES)}

    P = {n: w[n][0] for n in SMALL if n not in ("w_conv", "g_final")}
    P["w_conv"] = None
    P["g_final"] = g_final

    loss, dx, dmods, gw, native, parts, small_early = _local_step(x, mod, loss_target, W, P,
                                                                  [shard[n] for n in LATE] + [w_conv[0], misc], True)

    small_late = _exchange(_pack_small(native, dmods, True), "all", "gather", "gather_small")
    native_sum, dmod_all = _sum_unpack_small(small_early, small_late, B)
    loss = native_sum["loss"][0, 0]
    g_small = _small_from_native(native_sum)
    dmod_all = dmod_all.reshape(8 * B, N_MOD * D_MODEL)
    dmod_cols = lax.dynamic_slice_in_dim(dmod_all, chip * ada_cols, ada_cols, axis=1)
    g_w_ada, g_b_ada = _ada_bwd(c_all, dmod_all, dmod_cols)

    red, red_sib = parts
    order = list(EARLY_GRADS) + ["misc", "w_in_t"]
    halves = dict(zip(order, zip(red, red_sib)))

    grads = {"w_ada": g_w_ada[None], "b_ada": g_b_ada}
    grads["w_up"] = _add2(*halves["w_up_t"], F32, "add_cores_w_up").T[None]
    for k, gk in _unflatten_shard(_add2(*halves["misc"], F32, "add_cores_misc")).items():
        grads[k] = gk[None]
    wc_cols = w_conv.shape[2]
    for n in SMALL:
        g = g_small[n]
        if n == "w_conv":
            g = lax.dynamic_slice_in_dim(g, chip * wc_cols, wc_cols, axis=1)
        grads[n] = g.reshape(w[n].shape)

    delta, new_m, new_v = {}, {}, {}
    for n in ["w_ada"] + [b for b, _ in DIRECT] + [b for b, _, _ in BIG]:
        shp = w[n].shape
        if n == "w_in":
            r, s = halves["w_in_t"]
            d2, m2, v2, g2 = _adamw(w[n][0].T, r, m[n][0].T, v[n][0].T, "adamw_" + n, g_other=s)
            d2, m2, v2, grads[n] = d2.T, m2.T, v2.T, g2.T[None]
        elif n in ("w_down", "w_out"):
            r, s = halves[n]
            d2, m2, v2, g2 = _adamw(w[n][0], r, m[n][0], v[n][0], "adamw_" + n, g_other=s)
            grads[n] = g2[None]
        else:
            d2, m2, v2 = _adamw(w[n][0], grads[n][0], m[n][0], v[n][0], "adamw_" + n)
        delta[n], new_m[n], new_v[n] = d2.reshape(shp), m2.reshape(shp), v2.reshape(shp)
    rest = ["b_ada"] + SMALL

    def drop(a):
        return a.reshape(1, -1) if a.ndim == 1 else (a if a.ndim == 2 else a[0])

    upd = _adamw_multi([(drop(w[n]), drop(grads[n]), drop(m[n]), drop(v[n])) for n in rest])
    for n, (dd, mm, vv) in zip(rest, upd):
        delta[n], new_m[n], new_v[n] = dd.reshape(w[n].shape), mm.reshape(w[n].shape), vv.reshape(w[n].shape)

    return (loss, dx, *[grads[n] for n in WEIGHTS], *[delta[n] for n in WEIGHTS], *[new_m[n] for n in WEIGHTS],
            *[new_v[n] for n in WEIGHTS])
```

```python
import functools
import math

import jax
import jax.numpy as jnp
from jax import lax
from jax.experimental import pallas as pl
from jax.experimental.pallas import tpu as pltpu

F32, BF16 = jnp.float32, jnp.bfloat16

D_MODEL = 1024
N_HEADS = 8
HEAD_DIM = 64
ATT_WIDTH = 512
SSM_GROUPS = 16
SSM_GROUP_CH = 16
SSM_WIDTH = 256
SSM_STATE = 64
SSM_LANES = SSM_GROUPS * SSM_STATE
D_FF = 2048
IN_WIDTH = 3 * ATT_WIDTH + SSM_WIDTH + 2 * D_MODEL
ATT_BLOCK = 128
N_PATTERNS = 3
EPS = 1e-6
NEG_INF = -1e30

ADAM_LR, ADAM_B1, ADAM_B2, ADAM_EPS, ADAM_WD, ADAM_STEP = 0.001, 0.9, 0.999, 1e-08, 0.01, 10

V7X_VMEM_LIMIT_BYTES = 56 * 1024 * 1024
LANES = 1024


def _pcall(body, *, name, out_shape, grid=(), in_specs=None, out_specs=None, scratch_shapes=(), dims=None):
    params = dict(vmem_limit_bytes=V7X_VMEM_LIMIT_BYTES)
    if dims is not None:
        params["dimension_semantics"] = dims
    specs = {}
    if in_specs is not None:
        specs = dict(grid=grid, in_specs=in_specs, out_specs=out_specs)
    return pl.pallas_call(body, name=name, out_shape=out_shape, scratch_shapes=scratch_shapes,
                          compiler_params=pltpu.CompilerParams(**params), **specs)


def _sds(shape, dtype):
    return jax.ShapeDtypeStruct(tuple(shape), dtype)


def _tile(n, target):
    if n <= target:
        return n
    for t in range(target - target % 128, 0, -128):
        if n % t == 0:
            return t
    raise ValueError((n, target))


def _sig(v):
    return pl.reciprocal(1.0 + jnp.exp(-v), approx=True)


def _mm(a, b, *, name, ta=False, tb=False, out_dtype=F32, tm=2048, tn=1024, tk=1024, riders=None):
    halves = a.ndim == 3
    if halves:
        a_rows, a_cols = a.shape[1], 2 * a.shape[2]
    else:
        a_rows, a_cols = a.shape
    if ta:
        K, M = a_rows, a_cols
    else:
        M, K = a_rows, a_cols
    if tb:
        N, K2 = b.shape
    else:
        K2, N = b.shape
    assert K == K2, (a.shape, b.shape)
    if halves:
        tm, tk = (min(tm, M // 2), tk) if ta else (tm, min(tk, K // 2))
    tm, tn, tk = _tile(M, tm), _tile(N, tn), _tile(K, tk)
    nk = K // tk
    if halves and ta:
        per = a.shape[2] // tm
        a_spec = pl.BlockSpec((None, tk, tm), lambda i, j, k: (i // per, k, i % per))
    elif halves:
        per = a.shape[2] // tk
        a_spec = pl.BlockSpec((None, tm, tk), lambda i, j, k: (k // per, i, k % per))
    else:
        a_spec = pl.BlockSpec((tk, tm), lambda i, j, k: (k, i)) if ta else pl.BlockSpec((tm, tk), lambda i, j, k: (i, k))
    b_spec = pl.BlockSpec((tn, tk), lambda i, j, k: (j, k)) if tb else pl.BlockSpec((tk, tn), lambda i, j, k: (k, j))
    dn = (((0 if ta else 1,), (1 if tb else 0,)), ((), ()))

    def body(a_ref, b_ref, o_ref, acc_ref):
        k = pl.program_id(2)

        @pl.when(k == 0)
        def _():
            acc_ref[...] = jnp.zeros_like(acc_ref)

        acc_ref[...] += lax.dot_general(a_ref[...].astype(BF16), b_ref[...].astype(BF16), dn,
                                        preferred_element_type=F32)

        @pl.when(k == nk - 1)
        def _():
            o_ref[...] = acc_ref[...].astype(out_dtype)

    def body_single(a_ref, b_ref, o_ref):
        o_ref[...] = lax.dot_general(a_ref[...].astype(BF16), b_ref[...].astype(BF16), dn,
                                     preferred_element_type=F32).astype(out_dtype)

    grid = (M // tm, N // tn, nk)
    scratch = [] if nk == 1 else [pltpu.VMEM((tm, tn), F32)]
    o_spec = pl.BlockSpec((tm, tn), lambda i, j, k: (i, j))
    if riders is None:
        return _pcall(body_single if nk == 1 else body, name=name, out_shape=_sds((M, N), out_dtype), grid=grid,
                      in_specs=[a_spec, b_spec], out_specs=o_spec, scratch_shapes=scratch,
                      dims=("parallel", "parallel", "arbitrary"))(a, b)
    rs = riders
    res = _pcall(_with_riders(body_single if nk == 1 else body, rs, 2, 1, len(scratch), tuple(g - 1 for g in grid)),
                 name=name, out_shape=(_sds((M, N), out_dtype),) + tuple(rs.out_shape), grid=grid,
                 in_specs=[a_spec, b_spec] + rs.specs, out_specs=(o_spec,) + tuple(rs.specs),
                 scratch_shapes=scratch + rs.scratch, dims=("arbitrary", "arbitrary", "arbitrary"))(a, b, *rs.arrs)
    return res[0], list(res[1:])


def _ada_fwd(c_all, w_ada, b_ada_cols):
    n = w_ada.shape[1]

    def body(c_ref, w_ref, b_ref, o_ref):
        c = c_ref[...]
        act = c * _sig(c)
        o_ref[...] = jnp.dot(act.astype(BF16), w_ref[...].astype(BF16), preferred_element_type=F32) + b_ref[...]

    return _pcall(body, name="ada_fwd", out_shape=_sds((c_all.shape[0], n), F32))(c_all, w_ada, b_ada_cols)


def _ada_bwd(c_all, dmod_all, dmod_cols):
    n = dmod_cols.shape[1]

    def body(c_ref, da_ref, dc_ref, gw_ref, gb_ref):
        c = c_ref[...]
        act = c * _sig(c)
        gw_ref[...] = lax.dot_general(act, dc_ref[...], (((0,), (0,)), ((), ())), preferred_element_type=F32,
                                      precision=lax.Precision.HIGHEST)
        gb_ref[...] = jnp.sum(da_ref[...], axis=0, keepdims=True)

    return _pcall(body, name="ada_bwd", out_shape=(_sds((D_MODEL, n), F32), _sds((1, dmod_all.shape[1]), F32)))(
        c_all, dmod_all, dmod_cols)


ROW_TILE = 1024


def _row_specs(B, S):
    ts = min(S, ROW_TILE)
    row = pl.BlockSpec((1, ts, D_MODEL), lambda b, s: (b, s, 0))
    bvec = pl.BlockSpec((1, 1, D_MODEL), lambda b, s: (b, 0, 0))
    gvec = pl.BlockSpec((1, D_MODEL), lambda b, s: (0, 0))
    return ts, row, bvec, gvec


def _norm_mod(x3, g, sc, sh):
    B, S, _ = x3.shape
    ts, row, bvec, gvec = _row_specs(B, S)

    def body(x_ref, g_ref, sc_ref, sh_ref, u_ref):
        x = x_ref[0]
        r = lax.rsqrt(jnp.mean(x * x, axis=-1, keepdims=True) + EPS)
        u_ref[0] = ((x * r) * g_ref[...] * (1.0 + sc_ref[0]) + sh_ref[0]).astype(BF16)

    return _pcall(body, name="norm_mod1", out_shape=_sds(x3.shape, BF16), grid=(B, S // ts),
                  in_specs=[row, gvec, bvec, bvec], out_specs=row, dims=("parallel", "parallel"))(x3, g, sc, sh)


def _resid_norm_mod(x3, mix3, gt, g, sc, sh):
    B, S, _ = x3.shape
    ts, row, bvec, gvec = _row_specs(B, S)

    def body(x_ref, m_ref, gt_ref, g_ref, sc_ref, sh_ref, h_ref, u_ref):
        h = x_ref[0] + gt_ref[0] * m_ref[0]
        h_ref[0] = h
        r = lax.rsqrt(jnp.mean(h * h, axis=-1, keepdims=True) + EPS)
        u_ref[0] = ((h * r) * g_ref[...] * (1.0 + sc_ref[0]) + sh_ref[0]).astype(BF16)

    return _pcall(body, name="resid_norm_mod2", out_shape=(_sds(x3.shape, F32), _sds(x3.shape, BF16)),
                  grid=(B, S // ts), in_specs=[row, row, bvec, gvec, bvec, bvec], out_specs=(row, row),
                  dims=("parallel", "parallel"))(x3, mix3, gt, g, sc, sh)


def _norm_bwd(h3, du3, dres3, g, sc, name, mix3=None, gt=None, riders=None):
    B, S, _ = h3.shape
    ts, row, bvec, gvec = _row_specs(B, S)
    with_gate = mix3 is not None

    def body(*refs):
        if with_gate:
            h_ref, du_ref, dr_ref, g_ref, sc_ref, m_ref, gt_ref, dh_ref, dsh_ref, dsc_ref, dg_ref, dgt_ref, dm_ref = refs
        else:
            h_ref, du_ref, dr_ref, g_ref, sc_ref, dh_ref, dsh_ref, dsc_ref, dg_ref = refs
        b, s = pl.program_id(0), pl.program_id(1)
        h = h_ref[0]
        r = lax.rsqrt(jnp.mean(h * h, axis=-1, keepdims=True) + EPS)
        xn = h * r
        du = du_ref[0].astype(F32)
        g = g_ref[...]
        sc1 = 1.0 + sc_ref[0]
        dxn = du * g * sc1
        dh = dr_ref[0].astype(F32) + r * (dxn - xn * jnp.mean(dxn * xn, axis=-1, keepdims=True))
        dh_ref[0] = dh.astype(dh_ref.dtype)

        @pl.when(s == 0)
        def _():
            dsh_ref[...] = jnp.zeros_like(dsh_ref)
            dsc_ref[...] = jnp.zeros_like(dsc_ref)
            if with_gate:
                dgt_ref[...] = jnp.zeros_like(dgt_ref)

        @pl.when((s == 0) & (b == 0))
        def _():
            dg_ref[...] = jnp.zeros_like(dg_ref)

        dux = du * xn
        dsh_ref[0] += jnp.sum(du, axis=0, keepdims=True)
        dsc_ref[0] += jnp.sum(dux * g, axis=0, keepdims=True)
        dg_ref[...] += jnp.sum(dux * sc1, axis=0, keepdims=True)
        if with_gate:
            dgt_ref[0] += jnp.sum(dh * m_ref[0], axis=0, keepdims=True)
            dm_ref[0] = (dh * gt_ref[0]).astype(BF16)

    bshape = _sds((B, 1, D_MODEL), F32)
    in_specs = [row, row, row, gvec, bvec]
    out_shape = [_sds(h3.shape, BF16 if with_gate else F32), bshape, bshape, _sds((1, D_MODEL), F32)]
    out_specs = [row, bvec, bvec, gvec]
    args = [h3, du3, dres3, g, sc]
    if with_gate:
        in_specs += [row, bvec]
        out_shape += [bshape, _sds(h3.shape, BF16)]
        out_specs += [bvec, row]
        args += [mix3, gt]
    if riders is None:
        return _pcall(body, name=name, out_shape=tuple(out_shape), grid=(B, S // ts), in_specs=in_specs,
                      out_specs=tuple(out_specs), dims=("arbitrary", "arbitrary"))(*args)
    rs = riders
    res = _pcall(_with_riders(body, rs, len(args), len(out_shape), 0, (B - 1, S // ts - 1)), name=name,
                 out_shape=tuple(out_shape) + tuple(rs.out_shape), grid=(B, S // ts), in_specs=in_specs + rs.specs,
                 out_specs=tuple(out_specs) + tuple(rs.specs), scratch_shapes=rs.scratch,
                 dims=("arbitrary", "arbitrary"))(*args, *rs.arrs)
    return tuple(res[:len(out_shape)]) + (list(res[len(out_shape):]),)


def _final_loss(h1, ffn3, tgt3, gt, gfin):
    B, S, _ = h1.shape
    ts, row, bvec, gvec = _row_specs(B, S)
    one = pl.BlockSpec((1, 1), lambda b, s: (0, 0))

    def body(h_ref, f_ref, t_ref, gt_ref, gf_ref, dh_ref, dff_ref, dgt_ref, dgf_ref, loss_ref):
        b, s = pl.program_id(0), pl.program_id(1)
        f = f_ref[0].astype(F32)
        gtv = gt_ref[0]
        gf = gf_ref[...]
        h2 = h_ref[0] + gtv * f
        r = lax.rsqrt(jnp.mean(h2 * h2, axis=-1, keepdims=True) + EPS)
        n = h2 * r
        e = n * gf - t_ref[0]
        dy = e * (1.0 / D_MODEL)
        dn = dy * gf
        dh2 = r * (dn - n * jnp.mean(dn * n, axis=-1, keepdims=True))
        dh_ref[0] = dh2.astype(BF16)
        dff_ref[0] = (dh2 * gtv).astype(BF16)

        @pl.when(s == 0)
        def _():
            dgt_ref[...] = jnp.zeros_like(dgt_ref)

        @pl.when((s == 0) & (b == 0))
        def _():
            dgf_ref[...] = jnp.zeros_like(dgf_ref)
            loss_ref[...] = jnp.zeros_like(loss_ref)

        dgt_ref[0] += jnp.sum(dh2 * f, axis=0, keepdims=True)
        dgf_ref[...] += jnp.sum(dy * n, axis=0, keepdims=True)
        rows = jnp.sum(e * e, axis=1, keepdims=True)
        loss_ref[...] += jnp.sum(rows, axis=0, keepdims=True) * (0.5 / D_MODEL)

    return _pcall(body, name="final_loss",
                  out_shape=(_sds(h1.shape, BF16), _sds(h1.shape, BF16), _sds((B, 1, D_MODEL), F32),
                             _sds((1, D_MODEL), F32), _sds((1, 1), F32)),
                  grid=(B, S // ts), in_specs=[row, row, row, bvec, gvec], out_specs=(row, row, bvec, gvec, one),
                  dims=("arbitrary", "arbitrary"))(h1, ffn3, tgt3, gt, gfin)


ATT_GROUP = 4
ATT_GW = ATT_GROUP * HEAD_DIM
ATT_GROUPS = N_HEADS // ATT_GROUP
ATT_PAIRS = ATT_GW // ATT_BLOCK
ATT_UNROLL = 5
ATT_RESIDUE_UNROLL = 4
NT_DIMS = (((1,), (1,)), ((), ()))
TN_DIMS = (((0,), (0,)), ((), ()))


def _att_rows(start, d):
    if d == 1:
        return pl.ds(start if isinstance(start, int) else pl.multiple_of(start, ATT_BLOCK), ATT_BLOCK)
    return pl.ds(start, ATT_BLOCK, stride=d)


def _att_fill_bias(bias_ref, g, d):
    a = lax.broadcasted_iota(jnp.int32, (ATT_BLOCK, ATT_BLOCK), 0)
    j = lax.broadcasted_iota(jnp.int32, (ATT_BLOCK, ATT_BLOCK), 1)
    dist = (a - j).astype(F32)
    for hh in range(ATT_GROUP):
        t, e = divmod(hh, 2)
        rs = slice(e * ATT_BLOCK, (e + 1) * ATT_BLOCK)
        lo = 2.0 ** (-8.0 * (hh + 1) / N_HEADS) * d
        hi = 2.0 ** (-8.0 * (ATT_GROUP + hh + 1) / N_HEADS) * d
        slope = jnp.where(g == 0, lo, hi).astype(F32)
        bias_ref[t, rs, 0:ATT_BLOCK] = jnp.where(a >= j, -slope * dist, NEG_INF)
        bias_ref[t, rs, ATT_BLOCK:] = jnp.where(j >= a, -slope * (dist + float(ATT_BLOCK)), NEG_INF)


def _stack_heads(v2, low):
    return jnp.concatenate([jnp.where(low, v2, 0.0), jnp.where(low, 0.0, v2)], axis=0).astype(BF16)


def _unstack_heads(r2, low):
    return jnp.where(low, r2[0:ATT_BLOCK], r2[ATT_BLOCK:])


class _Riders:
    def __init__(self, arrs, mode, group="xy"):
        self.arrs, self.mode, self.n, self.group = list(arrs), mode, len(arrs), group
        k = len(_GROUP_MASKS[group])
        self.scratch = [pltpu.SemaphoreType.DMA((k * self.n,)), pltpu.SemaphoreType.DMA((k * self.n,))]
        if mode == "swap":
            assert group == "c"
            self.out_shape = [_sds(a.shape, a.dtype) for a in self.arrs]
        else:
            slot_shapes = [a.shape if mode == "gather" else a.shape[1:] for a in self.arrs]
            self.out_shape = [_sds((_GROUP_SLOTS[group],) + s, a.dtype) for s, a in zip(slot_shapes, self.arrs)]
            self.scratch += [pltpu.SemaphoreType.DMA((2 * self.n,))] + [pltpu.VMEM(s, a.dtype)
                                                                        for s, a in zip(slot_shapes, self.arrs)]
        self.specs = [pl.BlockSpec(memory_space=pl.ANY)] * self.n

    def _remote(self, x_refs, o_refs, send_sems, recv_sems):
        x, y, c = lax.axis_index("x"), lax.axis_index("y"), lax.axis_index("c")
        me = _group_slot(self.group, x, y, c)
        masks = _GROUP_MASKS[self.group]
        cps = []
        for i in range(self.n):
            for k, (dx, dy, dc) in enumerate(masks):
                px, py, pc = _flip(x, dx), _flip(y, dy), _flip(c, dc)
                src = x_refs[i].at[_group_slot(self.group, px, py, pc)] if self.mode == "scatter" else x_refs[i]
                dst = o_refs[i] if self.mode == "swap" else o_refs[i].at[me]
                cps.append(pltpu.make_async_remote_copy(
                    src_ref=src, dst_ref=dst, send_sem=send_sems.at[len(masks) * i + k],
                    recv_sem=recv_sems.at[len(masks) * i + k], device_id=(px, py, pc),
                    device_id_type=pl.DeviceIdType.MESH))
        return cps, me

    def start(self, x_refs, o_refs, scratch):
        cps, me = self._remote(x_refs, o_refs, scratch[0], scratch[1])
        for cp in cps:
            cp.start()
        if self.mode == "swap":
            return
        local_sems, bufs = scratch[2], scratch[3:]
        for i in range(self.n):
            src = x_refs[i] if self.mode == "gather" else x_refs[i].at[me]
            load = pltpu.make_async_copy(src, bufs[i], local_sems.at[2 * i])
            load.start()
            load.wait()
            pltpu.make_async_copy(bufs[i], o_refs[i].at[me], local_sems.at[2 * i + 1]).start()

    def wait(self, x_refs, o_refs, scratch):
        cps, me = self._remote(x_refs, o_refs, scratch[0], scratch[1])
        for cp in cps:
            cp.wait()
        if self.mode == "swap":
            return
        local_sems, bufs = scratch[2], scratch[3:]
        for i in range(self.n):
            pltpu.make_async_copy(bufs[i], o_refs[i].at[me], local_sems.at[2 * i + 1]).wait()


class _RiderGroup:
    def __init__(self, members):
        self.members = list(members)
        self.n = sum(m.n for m in self.members)
        self.arrs = [a for m in self.members for a in m.arrs]
        self.out_shape = [s for m in self.members for s in m.out_shape]
        self.specs = [s for m in self.members for s in m.specs]
        self.scratch = [s for m in self.members for s in m.scratch]

    def _each(self, x_refs, o_refs, scratch):
        i = j = 0
        for m in self.members:
            yield m, x_refs[i:i + m.n], o_refs[i:i + m.n], scratch[j:j + len(m.scratch)]
            i, j = i + m.n, j + len(m.scratch)

    def start(self, x_refs, o_refs, scratch):
        for m, xs, os, sc in self._each(x_refs, o_refs, scratch):
            m.start(xs, os, sc)

    def wait(self, x_refs, o_refs, scratch):
        for m, xs, os, sc in self._each(x_refs, o_refs, scratch):
            m.wait(xs, os, sc)


def _with_riders(compute, riders, n_in, n_out, n_scratch, last_step):
    if riders is None:
        return compute
    n = riders.n

    def body(*refs):
        ins, x_refs = refs[:n_in], refs[n_in:n_in + n]
        outs, o_refs = refs[n_in + n:n_in + n + n_out], refs[n_in + n + n_out:n_in + 2 * n + n_out]
        scratch = refs[n_in + 2 * n + n_out:]
        own, ride = scratch[:n_scratch], scratch[n_scratch:]
        ids = [pl.program_id(i) for i in range(len(last_step))]
        first = functools.reduce(jnp.logical_and, [i == 0 for i in ids])
        last = functools.reduce(jnp.logical_and, [i == l for i, l in zip(ids, last_step)])

        @pl.when(first)
        def _():
            riders.start(x_refs, o_refs, ride)

        compute(*ins, *outs, *own)

        @pl.when(last)
        def _():
            riders.wait(x_refs, o_refs, ride)

    return body


def _attention_fwd(proj3, seq_blocks, riders=None):
    B, S, _ = proj3.shape
    scale = HEAD_DIM ** -0.5
    nq = ATT_WIDTH // ATT_GW

    def col(k):
        return pl.BlockSpec((1, S, ATT_GW), lambda b, g, k=k: (b, 0, k * nq + g))

    o_spec = pl.BlockSpec((1, S, ATT_GW), lambda b, g: (b, 0, g))
    l_spec = pl.BlockSpec((1, 1, S, ATT_BLOCK), lambda b, g: (b, g, 0, 0))

    def compute(q_ref, k_ref, v_ref, o_ref, lse_ref, qf, kf, vf, os, ls, bias):
        g = pl.program_id(1)
        for t in range(ATT_PAIRS):
            ts = slice(t * ATT_BLOCK, (t + 1) * ATT_BLOCK)
            qf[t] = q_ref[0, :, ts].astype(F32) * scale
            kf[t] = k_ref[0, :, ts].astype(F32)
            vf[t] = v_ref[0, :, ts].astype(F32)
        lane = lax.broadcasted_iota(jnp.int32, (ATT_BLOCK, ATT_BLOCK), 1)
        low = lane < HEAD_DIM

        def block(p, d, r, n, has_prev):
            start = n * (ATT_BLOCK * d) + r
            rows = _att_rows(start, d)
            prows = _att_rows(start - ATT_BLOCK * d, d) if has_prev else None
            lse_t = jnp.zeros((ATT_BLOCK, ATT_BLOCK), F32)
            for t in range(ATT_PAIRS):
                q2 = _stack_heads(qf[t, rows, :], low)
                k2 = kf[t, rows, :].astype(BF16)
                v2 = vf[t, rows, :].astype(BF16)
                if has_prev:
                    k2 = jnp.concatenate([k2, kf[t, prows, :].astype(BF16)], axis=0)
                    v2 = jnp.concatenate([v2, vf[t, prows, :].astype(BF16)], axis=0)
                    b2 = bias[t]
                else:
                    b2 = bias[t, :, 0:ATT_BLOCK]
                s = lax.dot_general(q2, k2, NT_DIMS, preferred_element_type=F32) + b2
                m = jnp.max(s, axis=1, keepdims=True)
                pr = jnp.exp(s - m)
                den = jnp.sum(pr, axis=1, keepdims=True)
                o = jnp.dot(pr.astype(BF16), v2, preferred_element_type=F32) * (1.0 / den)
                os[p, t, rows, :] = _unstack_heads(o, low)
                lse2 = m + jnp.log(den)
                lse_t = jnp.where(lane == 2 * t, lse2[0:ATT_BLOCK], lse_t)
                lse_t = jnp.where(lane == 2 * t + 1, lse2[ATT_BLOCK:], lse_t)
            ls[p, rows, :] = lse_t

        for p in range(N_PATTERNS):
            d = 4 ** p
            _att_fill_bias(bias, g, d)
            _att_one_pattern(block, p, d, seq_blocks // d)

        def combine(i, carry):
            rows = pl.ds(pl.multiple_of(i * ATT_BLOCK, ATT_BLOCK), ATT_BLOCK)
            l0, l1, l2 = ls[0, rows, :], ls[1, rows, :], ls[2, rows, :]
            m = jnp.maximum(jnp.maximum(l0, l1), l2)
            lse = m + jnp.log(jnp.exp(l0 - m) + jnp.exp(l1 - m) + jnp.exp(l2 - m))
            lse_ref[0, 0, rows, :] = lse
            w = [jnp.exp(l0 - lse), jnp.exp(l1 - lse), jnp.exp(l2 - lse)]
            for t in range(ATT_PAIRS):
                acc = jnp.zeros((ATT_BLOCK, ATT_BLOCK), F32)
                for p in range(N_PATTERNS):
                    wt = jnp.where(low, w[p][:, 2 * t:2 * t + 1], w[p][:, 2 * t + 1:2 * t + 2])
                    acc = acc + wt * os[p, t, rows, :]
                o_ref[0, rows, t * ATT_BLOCK:(t + 1) * ATT_BLOCK] = acc.astype(BF16)
            return carry

        lax.fori_loop(0, S // ATT_BLOCK, combine, 0, unroll=2)

    scratch = ([pltpu.VMEM((ATT_PAIRS, S, ATT_BLOCK), F32)] * 3
               + [pltpu.VMEM((N_PATTERNS, ATT_PAIRS, S, ATT_BLOCK), F32), pltpu.VMEM((N_PATTERNS, S, ATT_BLOCK), F32),
                  pltpu.VMEM((ATT_PAIRS, 2 * ATT_BLOCK, 2 * ATT_BLOCK), F32)])
    rs = riders
    res = _pcall(_with_riders(compute, rs, 3, 2, len(scratch), (B - 1, ATT_GROUPS - 1)), name="attention_fwd",
                 out_shape=(_sds((B, S, ATT_WIDTH), BF16), _sds((B, ATT_GROUPS, S, ATT_BLOCK), F32))
                 + (tuple(rs.out_shape) if rs else ()),
                 grid=(B, ATT_GROUPS), in_specs=[col(0), col(1), col(2)] + (rs.specs if rs else []),
                 out_specs=(o_spec, l_spec) + (tuple(rs.specs) if rs else ()),
                 scratch_shapes=scratch + (rs.scratch if rs else []),
                 dims=("arbitrary", "arbitrary"))(proj3, proj3, proj3, *(rs.arrs if rs else []))
    return res[0], res[1], list(res[2:])


def _att_one_pattern(block, p, d, nb):
    def per_residue(r, carry):
        block(p, d, r, 0, False)
        if nb > 1:
            def per_block(n, c2):
                block(p, d, r, n, True)
                return c2
            lax.fori_loop(1, nb, per_block, 0, unroll=ATT_UNROLL if (nb - 1) % ATT_UNROLL == 0 else nb - 1)
        return carry

    if d == 1:
        per_residue(0, 0)
    else:
        lax.fori_loop(0, d, per_residue, 0, unroll=ATT_RESIDUE_UNROLL if nb == 1 else 1)


def _attention_bwd(proj3, do3, o3, lse4, seq_blocks, riders=None):
    B, S, _ = proj3.shape
    scale = HEAD_DIM ** -0.5
    nq = ATT_WIDTH // ATT_GW

    def col(k):
        return pl.BlockSpec((1, S, ATT_GW), lambda b, g, k=k: (b, 0, k * nq + g))

    o_spec = pl.BlockSpec((1, S, ATT_GW), lambda b, g: (b, 0, g))
    l_spec = pl.BlockSpec((1, 1, S, ATT_BLOCK), lambda b, g: (b, g, 0, 0))

    def compute(q_ref, k_ref, v_ref, do_ref, o_ref, lse_ref, dq_ref, dk_ref, dv_ref,
                qf, kf, vf, dof, dl, aq, ak, av, bias):
        g = pl.program_id(1)
        for t in range(ATT_PAIRS):
            ts = slice(t * ATT_BLOCK, (t + 1) * ATT_BLOCK)
            qf[t] = q_ref[0, :, ts].astype(F32) * scale
            kf[t] = k_ref[0, :, ts].astype(F32)
            vf[t] = v_ref[0, :, ts].astype(F32)
            dof[t] = do_ref[0, :, ts].astype(F32)
        aq[...] = jnp.zeros_like(aq)
        ak[...] = jnp.zeros_like(ak)
        av[...] = jnp.zeros_like(av)
        lane = lax.broadcasted_iota(jnp.int32, (ATT_BLOCK, ATT_BLOCK), 1)
        low = lane < HEAD_DIM

        def fill_delta(i, carry):
            rows = pl.ds(pl.multiple_of(i * ATT_BLOCK, ATT_BLOCK), ATT_BLOCK)
            acc = jnp.zeros((ATT_BLOCK, ATT_BLOCK), F32)
            for t in range(ATT_PAIRS):
                prod = dof[t, rows, :] * o_ref[0, rows, t * ATT_BLOCK:(t + 1) * ATT_BLOCK].astype(F32)
                lo = jnp.sum(jnp.where(low, prod, 0.0), axis=1, keepdims=True)
                hi = jnp.sum(prod, axis=1, keepdims=True) - lo
                acc = jnp.where(lane == 2 * t, lo, acc)
                acc = jnp.where(lane == 2 * t + 1, hi, acc)
            dl[rows, :] = acc
            return carry

        lax.fori_loop(0, S // ATT_BLOCK, fill_delta, 0, unroll=2)

        def block(p, d, r, n, has_prev):
            start = n * (ATT_BLOCK * d) + r
            rows = _att_rows(start, d)
            prows = _att_rows(start - ATT_BLOCK * d, d) if has_prev else None
            lse_t = lse_ref[0, 0, rows, :]
            dl_t = dl[rows, :]
            for t in range(ATT_PAIRS):
                q2 = _stack_heads(qf[t, rows, :], low)
                do2 = _stack_heads(dof[t, rows, :], low)
                k2 = kf[t, rows, :].astype(BF16)
                v2 = vf[t, rows, :].astype(BF16)
                if has_prev:
                    k2 = jnp.concatenate([k2, kf[t, prows, :].astype(BF16)], axis=0)
                    v2 = jnp.concatenate([v2, vf[t, prows, :].astype(BF16)], axis=0)
                    b2 = bias[t]
                else:
                    b2 = bias[t, :, 0:ATT_BLOCK]
                lse2 = jnp.concatenate([lse_t[:, 2 * t:2 * t + 1], lse_t[:, 2 * t + 1:2 * t + 2]], axis=0)
                dl2 = jnp.concatenate([dl_t[:, 2 * t:2 * t + 1], dl_t[:, 2 * t + 1:2 * t + 2]], axis=0)
                s = lax.dot_general(q2, k2, NT_DIMS, preferred_element_type=F32) + b2
                pr = jnp.exp(s - lse2)
                ds = (pr * (lax.dot_general(do2, v2, NT_DIMS, preferred_element_type=F32) - dl2)).astype(BF16)
                dq = _unstack_heads(jnp.dot(ds, k2, preferred_element_type=F32), low)
                dk = lax.dot_general(ds, q2, TN_DIMS, preferred_element_type=F32)
                dv = lax.dot_general(pr.astype(BF16), do2, TN_DIMS, preferred_element_type=F32)
                aq[t, rows, :] = aq[t, rows, :] + dq * scale
                ak[t, rows, :] = ak[t, rows, :] + dk[0:ATT_BLOCK]
                av[t, rows, :] = av[t, rows, :] + dv[0:ATT_BLOCK]
                if has_prev:
                    ak[t, prows, :] = ak[t, prows, :] + dk[ATT_BLOCK:]
                    av[t, prows, :] = av[t, prows, :] + dv[ATT_BLOCK:]

        for p in range(N_PATTERNS):
            d = 4 ** p
            _att_fill_bias(bias, g, d)
            _att_one_pattern(block, p, d, seq_blocks // d)

        for t in range(ATT_PAIRS):
            ts = slice(t * ATT_BLOCK, (t + 1) * ATT_BLOCK)
            dq_ref[0, :, ts] = aq[t].astype(BF16)
            dk_ref[0, :, ts] = ak[t].astype(BF16)
            dv_ref[0, :, ts] = av[t].astype(BF16)

    shp = _sds((B, S, ATT_WIDTH), BF16)
    pair_buf = pltpu.VMEM((ATT_PAIRS, S, ATT_BLOCK), F32)
    scratch = ([pair_buf] * 4 + [pltpu.VMEM((S, ATT_BLOCK), F32)] + [pair_buf] * 3
               + [pltpu.VMEM((ATT_PAIRS, 2 * ATT_BLOCK, 2 * ATT_BLOCK), F32)])
    rs = riders
    res = _pcall(_with_riders(compute, rs, 6, 3, len(scratch), (B - 1, ATT_GROUPS - 1)), name="attention_bwd",
                 out_shape=(shp, shp, shp) + (tuple(rs.out_shape) if rs else ()), grid=(B, ATT_GROUPS),
                 in_specs=[col(0), col(1), col(2), o_spec, o_spec, l_spec] + (rs.specs if rs else []),
                 out_specs=(o_spec, o_spec, o_spec) + (tuple(rs.specs) if rs else ()),
                 scratch_shapes=scratch + (rs.scratch if rs else []),
                 dims=("arbitrary", "arbitrary"))(proj3, proj3, proj3, do3, o3, lse4, *(rs.arrs if rs else []))
    return res[0], res[1], res[2], list(res[3:])


def _expand_groups(m):
    rows = SSM_WIDTH
    t = jnp.concatenate([m] * SSM_GROUPS, axis=0)
    r = lax.broadcasted_iota(jnp.int32, (rows, SSM_LANES), 0)
    l = lax.broadcasted_iota(jnp.int32, (rows, SSM_LANES), 1)
    keep = lax.shift_right_logical(r, 4) == lax.shift_right_logical(l, 6)
    return jnp.where(keep, t, 0.0)


def _collapse_groups(m):
    rows = SSM_WIDTH
    r = lax.broadcasted_iota(jnp.int32, (rows, SSM_LANES), 0)
    l = lax.broadcasted_iota(jnp.int32, (rows, SSM_LANES), 1)
    keep = lax.shift_right_logical(r, 4) == lax.shift_right_logical(l, 6)
    t = jnp.where(keep, m, 0.0)
    acc = t[0:SSM_GROUP_CH]
    for g in range(1, SSM_GROUPS):
        acc = acc + t[g * SSM_GROUP_CH:(g + 1) * SSM_GROUP_CH]
    return acc


def _zoh(lr, li, ldt):
    dt = jnp.exp(ldt)
    mag = jnp.exp(lr * dt)
    ang = li * dt
    cs, sn = jnp.cos(ang), jnp.sin(ang)
    ab_re, ab_im = mag * cs, mag * sn
    nr, ni = ab_re - 1.0, ab_im
    den = lr * lr + li * li
    n_re = nr * lr + ni * li
    n_im = ni * lr - nr * li
    return dict(dt=dt, mag=mag, cs=cs, sn=sn, ab_re=ab_re, ab_im=ab_im, nr=nr, ni=ni, den=den, n_re=n_re, n_im=n_im,
                f_re=n_re / den, f_im=n_im / den)


def _ssm_params(lr, li, ldt, br, bi, cr, ci):
    def body(lr_ref, li_ref, ldt_ref, br_ref, bi_ref, cr_ref, ci_ref, ab_ref, w_ref, c_ref):
        z = _zoh(lr_ref[...], li_ref[...], ldt_ref[...])
        ab_ref[0:1, :] = z["ab_re"]
        ab_ref[1:2, :] = z["ab_im"]
        br, bi = br_ref[...], bi_ref[...]
        w_ref[:, 0:SSM_LANES] = _expand_groups(z["f_re"] * br - z["f_im"] * bi).astype(BF16)
        w_ref[:, SSM_LANES:] = _expand_groups(z["f_re"] * bi + z["f_im"] * br).astype(BF16)
        c_ref[:, 0:SSM_LANES] = _expand_groups(cr_ref[...]).astype(BF16)
        c_ref[:, SSM_LANES:] = _expand_groups(-ci_ref[...]).astype(BF16)

    return _pcall(body, name="ssm_params",
                  out_shape=(_sds((2, SSM_LANES), F32), _sds((SSM_WIDTH, 2 * SSM_LANES), BF16),
                             _sds((SSM_WIDTH, 2 * SSM_LANES), BF16)))(lr, li, ldt, br, bi, cr, ci)


def _ssm_params_bwd(lr, li, ldt, br, bi, dab, dw, dc):
    def body(lr_ref, li_ref, ldt_ref, br_ref, bi_ref, dab_ref, dw_ref, dc_ref,
             dlr_ref, dli_ref, dldt_ref, dbr_ref, dbi_ref, dcr_ref, dci_ref):
        lr, li = lr_ref[...], li_ref[...]
        z = _zoh(lr, li, ldt_ref[...])
        br, bi = br_ref[...], bi_ref[...]
        dbb_re = _collapse_groups(dw_ref[:, 0:SSM_LANES])
        dbb_im = _collapse_groups(dw_ref[:, SSM_LANES:])
        dcr_ref[...] = _collapse_groups(dc_ref[:, 0:SSM_LANES])
        dci_ref[...] = -_collapse_groups(dc_ref[:, SSM_LANES:])
        f_re, f_im = z["f_re"], z["f_im"]
        dbr_ref[...] = f_re * dbb_re + f_im * dbb_im
        dbi_ref[...] = f_re * dbb_im - f_im * dbb_re
        df_re = jnp.sum(dbb_re * br + dbb_im * bi, axis=0, keepdims=True)
        df_im = jnp.sum(dbb_im * br - dbb_re * bi, axis=0, keepdims=True)
        den = z["den"]
        dn_re, dn_im = df_re / den, df_im / den
        dden = -(df_re * z["n_re"] + df_im * z["n_im"]) / (den * den)
        dnr = dn_re * lr - dn_im * li
        dni = dn_re * li + dn_im * lr
        dlr = dn_re * z["nr"] + dn_im * z["ni"] + 2.0 * dden * lr
        dli = dn_re * z["ni"] - dn_im * z["nr"] + 2.0 * dden * li
        dab_re = dab_ref[0:1, :] + dnr
        dab_im = dab_ref[1:2, :] + dni
        mag, cs, sn, dt = z["mag"], z["cs"], z["sn"], z["dt"]
        dmag = dab_re * cs + dab_im * sn
        dang = mag * (dab_im * cs - dab_re * sn)
        dlr_ref[...] = dlr + dmag * mag * dt
        dli_ref[...] = dli + dang * dt
        ddt = dmag * mag * lr + dang * li
        per_lane = jnp.broadcast_to(ddt * dt, (8, SSM_LANES))
        lane = lax.broadcasted_iota(jnp.int32, (SSM_LANES, 128), 0)
        col = lax.broadcasted_iota(jnp.int32, (SSM_LANES, 128), 1)
        ind = jnp.where(lax.shift_right_logical(lane, 6) == col, 1.0, 0.0)
        dldt_ref[...] = jnp.dot(per_lane, ind, preferred_element_type=F32, precision=lax.Precision.HIGHEST)[0:1]

    vec = _sds((1, SSM_LANES), F32)
    mat = _sds((SSM_GROUP_CH, SSM_LANES), F32)
    return _pcall(body, name="ssm_params_bwd", out_shape=(vec, vec, _sds((1, 128), F32), mat, mat, mat, mat))(
        lr, li, ldt, br, bi, dab, dw, dc)


SCAN_CHUNK = 512


def _scan_consts(ar, ai, k_ref, reverse):
    row = lax.broadcasted_iota(jnp.int32, (8, SSM_LANES), 0)
    pw = [(ar, ai)]
    for _ in range(7):
        pr, pi = pw[-1]
        pw.append((pr * ar - pi * ai, pr * ai + pi * ar))
    for n, k in enumerate((1, 2, 4)):
        keep = (row < 8 - k) if reverse else (row >= k)
        k_ref[2 * n] = jnp.where(keep, jnp.broadcast_to(pw[k - 1][0], (8, SSM_LANES)), 0.0)
        k_ref[2 * n + 1] = jnp.where(keep, jnp.broadcast_to(pw[k - 1][1], (8, SSM_LANES)), 0.0)
    cr = jnp.zeros((8, SSM_LANES), F32)
    ci = jnp.zeros((8, SSM_LANES), F32)
    for r in range(8):
        e = (8 - r) if reverse else (r + 1)
        cr = jnp.where(row == r, jnp.broadcast_to(pw[e - 1][0], (8, SSM_LANES)), cr)
        ci = jnp.where(row == r, jnp.broadcast_to(pw[e - 1][1], (8, SSM_LANES)), ci)
    k_ref[6] = cr
    k_ref[7] = ci


def _scan_tile(xr, xi, k_ref, car, cai, reverse):
    for n, k in enumerate((1, 2, 4)):
        sh = (8 - k) if reverse else k
        sr = pltpu.roll(xr, sh, 0)
        si = pltpu.roll(xi, sh, 0)
        mr, mi = k_ref[2 * n], k_ref[2 * n + 1]
        xr, xi = xr + mr * sr - mi * si, xi + mr * si + mi * sr
    pr, pi = k_ref[6], k_ref[7]
    xr, xi = xr + pr * car - pi * cai, xi + pr * cai + pi * car
    return xr, xi


US_BLOCK = (3 * ATT_WIDTH) // SSM_WIDTH


def _ssm_scan_fwd(proj3, abar, w_bu, w_c):
    B, S, _ = proj3.shape
    ch = min(S, SCAN_CHUNK)
    u_spec = pl.BlockSpec((1, ch, SSM_WIDTH), lambda b, c: (b, c, US_BLOCK))
    x_spec = pl.BlockSpec((1, ch, 2 * SSM_LANES), lambda b, c: (b, c, 0))
    y_spec = pl.BlockSpec((1, ch, SSM_WIDTH), lambda b, c: (b, c, 0))
    w_spec = pl.BlockSpec((SSM_WIDTH, 2 * SSM_LANES), lambda b, c: (0, 0))

    def body(ab_ref, u_ref, wb_ref, wc_ref, x_ref, y_ref, k_ref, carry_ref):
        _scan_consts(ab_ref[0:1, :], ab_ref[1:2, :], k_ref, False)

        @pl.when(pl.program_id(1) == 0)
        def _():
            carry_ref[...] = jnp.zeros_like(carry_ref)

        x_ref[0] = jnp.dot(u_ref[0], wb_ref[...], preferred_element_type=F32)

        def step(i, carry):
            base = pl.multiple_of(i * 8, 8)
            xr = x_ref[0, pl.ds(base, 8), 0:SSM_LANES]
            xi = x_ref[0, pl.ds(base, 8), SSM_LANES:]
            xr, xi = _scan_tile(xr, xi, k_ref, carry[0], carry[1], False)
            x_ref[0, pl.ds(base, 8), 0:SSM_LANES] = xr
            x_ref[0, pl.ds(base, 8), SSM_LANES:] = xi
            return (jnp.broadcast_to(xr[7:8], (8, SSM_LANES)), jnp.broadcast_to(xi[7:8], (8, SSM_LANES)))

        cr, ci = lax.fori_loop(0, ch // 8, step, (carry_ref[0], carry_ref[1]))
        carry_ref[0] = cr
        carry_ref[1] = ci
        y_ref[0] = lax.dot_general(x_ref[0].astype(BF16), wc_ref[...], NT_DIMS, preferred_element_type=F32)

    return _pcall(body, name="ssm_scan_fwd",
                  out_shape=(_sds((B, S, 2 * SSM_LANES), F32), _sds((B, S, SSM_WIDTH), F32)), grid=(B, S // ch),
                  in_specs=[pl.BlockSpec((2, SSM_LANES), lambda b, c: (0, 0)), u_spec, w_spec, w_spec],
                  out_specs=(x_spec, y_spec),
                  scratch_shapes=[pltpu.VMEM((8, 8, SSM_LANES), F32), pltpu.VMEM((2, 8, SSM_LANES), F32)],
                  dims=("arbitrary", "arbitrary"))(abar, proj3, w_bu, w_c)


def _ssm_scan_bwd(proj3, dy3, xs3, abar, w_bu, w_c, dsk):
    B, S, _ = proj3.shape
    ch = min(S, SCAN_CHUNK)
    nc = S // ch
    u_spec = pl.BlockSpec((1, ch, SSM_WIDTH), lambda b, c: (b, nc - 1 - c, US_BLOCK))
    x_spec = pl.BlockSpec((1, ch, 2 * SSM_LANES), lambda b, c: (b, nc - 1 - c, 0))
    y_spec = pl.BlockSpec((1, ch, SSM_WIDTH), lambda b, c: (b, nc - 1 - c, 0))
    w_spec = pl.BlockSpec((SSM_WIDTH, 2 * SSM_LANES), lambda b, c: (0, 0))
    ab_spec = pl.BlockSpec((2, SSM_LANES), lambda b, c: (0, 0))
    d_spec = pl.BlockSpec((1, SSM_WIDTH), lambda b, c: (0, 0))

    def body(ab_ref, u_ref, dy_ref, xs_ref, wb_ref, wc_ref, d_ref, du_ref, da_ref, dwb_ref, dwc_ref,
             g_ref, k_ref, carry_ref, acc_ref):
        b, c = pl.program_id(0), pl.program_id(1)
        _scan_consts(ab_ref[0:1, :], -ab_ref[1:2, :], k_ref, True)
        row = lax.broadcasted_iota(jnp.int32, (8, SSM_LANES), 0)

        @pl.when(c == 0)
        def _():
            carry_ref[...] = jnp.zeros_like(carry_ref)

        @pl.when((c == 0) & (b == 0))
        def _():
            acc_ref[...] = jnp.zeros_like(acc_ref)
            dwb_ref[...] = jnp.zeros_like(dwb_ref)
            dwc_ref[...] = jnp.zeros_like(dwc_ref)

        dy = dy_ref[0]
        dyb = dy.astype(BF16)
        g_ref[...] = jnp.dot(dyb, wc_ref[...], preferred_element_type=F32)

        def step(i, carry):
            car, cai, ar_acc, ai_acc = carry
            base = pl.multiple_of((ch // 8 - 1 - i) * 8, 8)
            gr = g_ref[pl.ds(base, 8), 0:SSM_LANES]
            gi = g_ref[pl.ds(base, 8), SSM_LANES:]
            gr, gi = _scan_tile(gr, gi, k_ref, car, cai, True)
            g_ref[pl.ds(base, 8), 0:SSM_LANES] = gr
            g_ref[pl.ds(base, 8), SSM_LANES:] = gi
            nr = jnp.where(row == 7, car, pltpu.roll(gr, 7, 0))
            ni = jnp.where(row == 7, cai, pltpu.roll(gi, 7, 0))
            xr = xs_ref[0, pl.ds(base, 8), 0:SSM_LANES]
            xi = xs_ref[0, pl.ds(base, 8), SSM_LANES:]
            ar_acc = ar_acc + nr * xr + ni * xi
            ai_acc = ai_acc + ni * xr - nr * xi
            return (jnp.broadcast_to(gr[0:1], (8, SSM_LANES)), jnp.broadcast_to(gi[0:1], (8, SSM_LANES)), ar_acc, ai_acc)

        cr, ci, ar_acc, ai_acc = lax.fori_loop(0, ch // 8, step, (carry_ref[0], carry_ref[1], acc_ref[0], acc_ref[1]))
        carry_ref[0] = cr
        carry_ref[1] = ci
        acc_ref[0] = ar_acc
        acc_ref[1] = ai_acc
        da_ref[0:1, :] = jnp.sum(ar_acc, axis=0, keepdims=True)
        da_ref[1:2, :] = jnp.sum(ai_acc, axis=0, keepdims=True)

        gb = g_ref[...].astype(BF16)
        du = lax.dot_general(gb, wb_ref[...], NT_DIMS, preferred_element_type=F32) + d_ref[...] * dy
        du_ref[0] = du.astype(BF16)
        xb = xs_ref[0].astype(BF16)
        u = u_ref[0]
        for j in range(2 * SSM_LANES // SSM_WIDTH):
            rows = slice((j % (SSM_LANES // SSM_WIDTH)) * 64, (j % (SSM_LANES // SSM_WIDTH)) * 64 + 64)
            cols = slice(j * SSM_WIDTH, (j + 1) * SSM_WIDTH)
            dwb_ref[rows, cols] += lax.dot_general(u[:, rows], gb[:, cols], TN_DIMS, preferred_element_type=F32)
            dwc_ref[rows, cols] += lax.dot_general(dyb[:, rows], xb[:, cols], TN_DIMS, preferred_element_type=F32)

    mat = _sds((SSM_WIDTH, 2 * SSM_LANES), F32)
    return _pcall(body, name="ssm_scan_bwd",
                  out_shape=(_sds((B, S, SSM_WIDTH), BF16), _sds((2, SSM_LANES), F32), mat, mat), grid=(B, nc),
                  in_specs=[ab_spec, u_spec, y_spec, x_spec, w_spec, w_spec, d_spec],
                  out_specs=(y_spec, ab_spec, w_spec, w_spec),
                  scratch_shapes=[pltpu.VMEM((ch, 2 * SSM_LANES), F32), pltpu.VMEM((8, 8, SSM_LANES), F32),
                                  pltpu.VMEM((2, 8, SSM_LANES), F32), pltpu.VMEM((2, 8, SSM_LANES), F32)],
                  dims=("arbitrary", "arbitrary"))(abar, proj3, dy3, xs3, w_bu, w_c, dsk)


GELU_K = math.sqrt(2.0 / math.pi)
GELU_C = 0.044715


def _gelu_parts(y):
    t = jnp.tanh(GELU_K * (y + GELU_C * y * y * y))
    return 0.5 * y * (1.0 + t), t


def _ssm_post(yc, us, dsk, wglu, bglu):
    T, N = yc.shape
    tm = min(T, 1024)
    row = pl.BlockSpec((tm, N), lambda i: (i, 0))
    vec = pl.BlockSpec((1, N), lambda i: (0, 0))
    mat = pl.BlockSpec((N, N), lambda i: (0, 0))

    def body(yc_ref, us_ref, d_ref, w_ref, b_ref, y_ref, s_ref):
        y = yc_ref[...] + d_ref[...] * us_ref[...]
        y_ref[...] = y
        z, _ = _gelu_parts(y)
        gl = jnp.dot(z.astype(BF16), w_ref[...], preferred_element_type=F32) + b_ref[...]
        s_ref[...] = (z * _sig(gl)).astype(BF16)

    return _pcall(body, name="ssm_post", out_shape=(_sds((T, N), F32), _sds((T, N), BF16)), grid=(T // tm,),
                  in_specs=[row, row, vec, mat, vec], out_specs=(row, row), dims=("parallel",))(yc, us, dsk, wglu, bglu)


def _ssm_post_bwd(y5, us, ds, dsk, wglu, bglu):
    T, N = y5.shape
    tm = min(T, 1024)
    row = pl.BlockSpec((tm, N), lambda i: (i, 0))
    vec = pl.BlockSpec((1, N), lambda i: (0, 0))
    mat = pl.BlockSpec((N, N), lambda i: (0, 0))

    def body(y_ref, us_ref, ds_ref, d_ref, w_ref, b_ref, dy_ref, dd_ref, db_ref, dw_ref):
        @pl.when(pl.program_id(0) == 0)
        def _():
            dd_ref[...] = jnp.zeros_like(dd_ref)
            db_ref[...] = jnp.zeros_like(db_ref)
            dw_ref[...] = jnp.zeros_like(dw_ref)

        y = y_ref[...]
        z, t = _gelu_parts(y)
        zb = z.astype(BF16)
        gl = jnp.dot(zb, w_ref[...], preferred_element_type=F32) + b_ref[...]
        sg = _sig(gl)
        ds = ds_ref[...]
        dgl = ds * z * sg * (1.0 - sg)
        dglb = dgl.astype(BF16)
        dz = ds * sg + lax.dot_general(dglb, w_ref[...], (((1,), (1,)), ((), ())), preferred_element_type=F32)
        dgelu = 0.5 * (1.0 + t) + 0.5 * y * (1.0 - t * t) * GELU_K * (1.0 + 3.0 * GELU_C * y * y)
        dy = dz * dgelu
        dy_ref[...] = dy
        dd_ref[...] += jnp.sum(dy * us_ref[...], axis=0, keepdims=True)
        db_ref[...] += jnp.sum(dgl, axis=0, keepdims=True)
        dw_ref[...] += lax.dot_general(zb, dglb, (((0,), (0,)), ((), ())), preferred_element_type=F32)

    return _pcall(body, name="ssm_post_bwd",
                  out_shape=(_sds((T, N), F32), _sds((1, N), F32), _sds((1, N), F32), _sds((N, N), F32)),
                  grid=(T // tm,), in_specs=[row, row, row, vec, mat, vec], out_specs=(row, vec, vec, mat),
                  dims=("arbitrary",))(y5, us, ds, dsk, wglu, bglu)


GATE_TILE = 256
GATE_ATT_BLOCK0 = (3 * ATT_WIDTH + SSM_WIDTH) // GATE_TILE
GATE_SSM_BLOCK0 = (3 * ATT_WIDTH + SSM_WIDTH + D_MODEL) // GATE_TILE


def _merge(proj, y_att, y_ssm, b_gate):
    T = proj.shape[0]
    tm = min(T, 4096)
    nj = D_MODEL // GATE_TILE
    ga = pl.BlockSpec((tm, GATE_TILE), lambda i, j: (i, GATE_ATT_BLOCK0 + j))
    gs = pl.BlockSpec((tm, GATE_TILE), lambda i, j: (i, GATE_SSM_BLOCK0 + j))
    yy = pl.BlockSpec((tm, GATE_TILE), lambda i, j: (i, j))
    ba = pl.BlockSpec((1, GATE_TILE), lambda i, j: (0, j))
    bs = pl.BlockSpec((1, GATE_TILE), lambda i, j: (0, nj + j))

    def body(ga_ref, gs_ref, ya_ref, ys_ref, ba_ref, bs_ref, o_ref):
        o_ref[...] = (_sig(ga_ref[...] + ba_ref[...]) * ya_ref[...]
                      + _sig(gs_ref[...] + bs_ref[...]) * ys_ref[...]).astype(BF16)

    return _pcall(body, name="merge", out_shape=_sds((T, D_MODEL), BF16), grid=(T // tm, nj),
                  in_specs=[ga, gs, yy, yy, ba, bs], out_specs=yy, dims=("parallel", "parallel"))(
        proj, proj, y_att, y_ssm, b_gate, b_gate)


def _merge_bwd(proj, y_att, y_ssm, b_gate, dmerged):
    T = proj.shape[0]
    tm = min(T, 2048)
    nj = D_MODEL // GATE_TILE
    ga = pl.BlockSpec((tm, GATE_TILE), lambda j, i: (i, GATE_ATT_BLOCK0 + j))
    gs = pl.BlockSpec((tm, GATE_TILE), lambda j, i: (i, GATE_SSM_BLOCK0 + j))
    yy = pl.BlockSpec((tm, GATE_TILE), lambda j, i: (i, j))
    ba = pl.BlockSpec((1, GATE_TILE), lambda j, i: (0, j))
    bs = pl.BlockSpec((1, GATE_TILE), lambda j, i: (0, nj + j))

    def body(ga_ref, gs_ref, ya_ref, ys_ref, ba_ref, bs_ref, dm_ref, dya_ref, dys_ref, dga_ref, dgs_ref, dba_ref, dbs_ref):
        @pl.when(pl.program_id(1) == 0)
        def _():
            dba_ref[...] = jnp.zeros_like(dba_ref)
            dbs_ref[...] = jnp.zeros_like(dbs_ref)

        dm = dm_ref[...].astype(F32)
        sa = _sig(ga_ref[...] + ba_ref[...])
        ss = _sig(gs_ref[...] + bs_ref[...])
        dya_ref[...] = (dm * sa).astype(BF16)
        dys_ref[...] = (dm * ss).astype(BF16)
        dga = dm * ya_ref[...] * sa * (1.0 - sa)
        dgs = dm * ys_ref[...] * ss * (1.0 - ss)
        dga_ref[...] = dga.astype(BF16)
        dgs_ref[...] = dgs.astype(BF16)
        dba_ref[...] += jnp.sum(dga, axis=0, keepdims=True)
        dbs_ref[...] += jnp.sum(dgs, axis=0, keepdims=True)

    big = _sds((T, D_MODEL), BF16)
    vec = _sds((1, D_MODEL), F32)
    return _pcall(body, name="merge_bwd", out_shape=(big, big, big, big, vec, vec), grid=(nj, T // tm),
                  in_specs=[ga, gs, yy, yy, ba, bs, yy], out_specs=(yy, yy, yy, yy, ba, ba),
                  dims=("arbitrary", "arbitrary"))(proj, proj, y_att, y_ssm, b_gate, b_gate, dmerged)


CONV_TILE = 256


def _shift_rows(a, j, up=False):
    n = a.shape[0]
    r = pltpu.roll(a, n - j if up else j, 0)
    row = lax.broadcasted_iota(jnp.int32, (8, a.shape[1]), 0)
    if up:
        return jnp.concatenate([r[:n - 8], jnp.where(row < 8 - j, r[n - 8:], 0.0)], axis=0)
    return jnp.concatenate([jnp.where(row >= j, r[:8], 0.0), r[8:]], axis=0)


def _conv_pre(a, w_ref, b_ref):
    conv = b_ref[...] + w_ref[0:1, :] * a
    shifted = []
    for j in (1, 2):
        sh = _shift_rows(a, j)
        shifted.append(sh)
        conv = conv + w_ref[j:j + 1, :] * sh
    return conv, shifted


def _conv_act(up3, w_conv, b_conv):
    B, S, _ = up3.shape
    nj = D_FF // CONV_TILE
    a_spec = pl.BlockSpec((1, S, CONV_TILE), lambda b, j: (b, 0, j))
    v_spec = pl.BlockSpec((1, S, CONV_TILE), lambda b, j: (b, 0, nj + j))
    w_spec = pl.BlockSpec((3, CONV_TILE), lambda b, j: (0, j))
    b_spec = pl.BlockSpec((1, CONV_TILE), lambda b, j: (0, j))

    def body(a_ref, v_ref, w_ref, b_ref, o_ref):
        a = a_ref[0].astype(F32)
        conv, _ = _conv_pre(a, w_ref, b_ref)
        o_ref[0] = (conv * _sig(conv) * v_ref[0]).astype(BF16)

    return _pcall(body, name="conv_act", out_shape=_sds((B, S, D_FF), BF16), grid=(B, nj),
                  in_specs=[a_spec, v_spec, w_spec, b_spec], out_specs=a_spec, dims=("parallel", "parallel"))(
        up3, up3, w_conv, b_conv)


def _conv_bwd(up3, dact3, w_conv, b_conv):
    B, S, _ = up3.shape
    nj = D_FF // CONV_TILE
    a_spec = pl.BlockSpec((1, S, CONV_TILE), lambda j, b: (b, 0, j))
    v_spec = pl.BlockSpec((1, S, CONV_TILE), lambda j, b: (b, 0, nj + j))
    o_spec = pl.BlockSpec((2, 1, S, CONV_TILE), lambda j, b: (0, b, 0, j))
    w_spec = pl.BlockSpec((3, CONV_TILE), lambda j, b: (0, j))
    b_spec = pl.BlockSpec((1, CONV_TILE), lambda j, b: (0, j))

    def body(a_ref, v_ref, d_ref, w_ref, b_ref, dup_ref, dw_ref, db_ref):
        @pl.when(pl.program_id(1) == 0)
        def _():
            dw_ref[...] = jnp.zeros_like(dw_ref)
            db_ref[...] = jnp.zeros_like(db_ref)

        a = a_ref[0].astype(F32)
        d = d_ref[0].astype(F32)
        conv, shifted = _conv_pre(a, w_ref, b_ref)
        sg = _sig(conv)
        dup_ref[1, 0] = (d * conv * sg).astype(BF16)
        dconv = d * v_ref[0] * (sg * (1.0 + conv * (1.0 - sg)))
        da = w_ref[0:1, :] * dconv
        for j in (1, 2):
            da = da + w_ref[j:j + 1, :] * _shift_rows(dconv, j, up=True)
        dup_ref[0, 0] = da.astype(BF16)
        db_ref[...] += jnp.sum(dconv, axis=0, keepdims=True)
        dw_ref[0:1, :] += jnp.sum(dconv * a, axis=0, keepdims=True)
        dw_ref[1:2, :] += jnp.sum(dconv * shifted[0], axis=0, keepdims=True)
        dw_ref[2:3, :] += jnp.sum(dconv * shifted[1], axis=0, keepdims=True)

    return _pcall(body, name="conv_bwd",
                  out_shape=(_sds((2, B, S, D_FF), BF16), _sds((3, D_FF), F32), _sds((1, D_FF), F32)),
                  grid=(nj, B), in_specs=[a_spec, v_spec, a_spec, w_spec, b_spec],
                  out_specs=(o_spec, w_spec, b_spec), dims=("arbitrary", "arbitrary"))(up3, up3, dact3, w_conv, b_conv)


def _rows_tile(r, cap=640):
    for t in range(min(r, cap) - min(r, cap) % 8, 7, -8):
        if r % t == 0:
            return t
    return r


def _add2(a, b, out_dtype, name):
    R, N = a.shape
    tr = _rows_tile(R)
    spec = pl.BlockSpec((tr, N), lambda i: (i, 0))

    def body(a_ref, b_ref, o_ref):
        o_ref[...] = (a_ref[...] + b_ref[...]).astype(out_dtype)

    return _pcall(body, name=name, out_shape=_sds((R, N), out_dtype), grid=(R // tr,), in_specs=[spec, spec],
                  out_specs=spec, dims=("parallel",))(a, b)


def _sum_slots(q, name):
    n, R, N = q.shape
    tr = _rows_tile(R)

    def body(q_ref, o_ref):
        acc = q_ref[0].astype(F32)
        for s in range(1, n):
            acc = acc + q_ref[s].astype(F32)
        o_ref[...] = acc

    return _pcall(body, name=name, out_shape=_sds((R, N), F32), grid=(R // tr,),
                  in_specs=[pl.BlockSpec((n, tr, N), lambda i: (0, i, 0))], out_specs=pl.BlockSpec((tr, N), lambda i: (i, 0)),
                  dims=("parallel",))(q)


NATIVE = (("b_re", 16, 1024), ("b_im", 16, 1024), ("c_re", 16, 1024), ("c_im", 16, 1024), ("g_mix", 1, 1024),
          ("b_att", 1, 1024), ("b_ssm", 1, 1024), ("a_re", 1, 1024), ("a_im", 1, 1024), ("log_dt", 1, 128),
          ("d_skip", 1, 256), ("b_glu", 1, 256), ("g_ffn", 1, 1024), ("g_final", 1, 1024), ("b_conv", 1, 2048),
          ("w_conv", 3, 2048), ("loss", 1, 1))
N_MOD = 6
NATIVE_LATE = ("g_mix",)
MODS_LATE = (0, 1)


def _small_plan(late):
    pieces = [p for p in NATIVE if (p[0] in NATIVE_LATE) == late]
    mods = [k for k in range(N_MOD) if (k in MODS_LATE) == late]
    starts, r = {}, 0
    for name, rows, cols in pieces:
        starts[name] = r
        r += rows * (-(-cols // LANES))
    return pieces, mods, starts, -(-r // 8) * 8


def _pack_small(native, dmods, late):
    pieces, mods, starts, n_sum = _small_plan(late)
    B = dmods[mods[0]].shape[0]
    total = n_sum + 8 * len(mods)

    def body(*refs):
        xs, ms, o_ref = refs[:len(pieces)], refs[len(pieces):-1], refs[-1]
        o_ref[...] = jnp.zeros_like(o_ref)
        for (name, rows, cols), x_ref in zip(pieces, xs):
            chunks = -(-cols // LANES)
            if chunks == 1 and rows % 8 == 0:
                o_ref[starts[name]:starts[name] + rows, 0:cols] = x_ref[...]
                continue
            for i in range(rows):
                for q in range(chunks):
                    wd = min(LANES, cols - q * LANES)
                    r = starts[name] + i * chunks + q
                    o_ref[r:r + 1, 0:wd] = x_ref[i:i + 1, q * LANES:q * LANES + wd]
        for k, m_ref in enumerate(ms):
            for b in range(B):
                o_ref[n_sum + 8 * k + b:n_sum + 8 * k + b + 1, :] = m_ref[b]

    return _pcall(body, name="pack_small_late" if late else "pack_small_early", out_shape=_sds((total, LANES), F32))(
        *[native[n] for n, _, _ in pieces], *[dmods[k] for k in mods])


def _sum_unpack_small(gathered_early, gathered_late, B):
    plans = [_small_plan(False), _small_plan(True)]
    nd = gathered_early.shape[0]
    n_out = len(NATIVE)

    def body(*refs):
        g_refs, outs, dm_ref, accs = refs[0:2], refs[2:2 + n_out], refs[2 + n_out], refs[3 + n_out:]
        o = 0
        for g_ref, acc, (pieces, mods, starts, n_sum) in zip(g_refs, accs, plans):
            s = g_ref[0, 0:n_sum, :]
            for d in range(1, nd):
                s = s + g_ref[d, 0:n_sum, :]
            acc[...] = s
            for name, rows, cols in pieces:
                o_ref = outs[o]
                o += 1
                chunks = -(-cols // LANES)
                if chunks == 1 and rows % 8 == 0:
                    o_ref[...] = acc[starts[name]:starts[name] + rows, 0:cols]
                    continue
                for i in range(rows):
                    for q in range(chunks):
                        wd = min(LANES, cols - q * LANES)
                        r = starts[name] + i * chunks + q
                        o_ref[i:i + 1, q * LANES:q * LANES + wd] = acc[r:r + 1, 0:wd]
            for d in range(nd):
                for j, k in enumerate(mods):
                    dm_ref[d, :, k * D_MODEL:(k + 1) * D_MODEL] = g_ref[d, n_sum + 8 * j:n_sum + 8 * j + B, :]

    ordered = [p for pieces, _, _, _ in plans for p in pieces]
    out_shape = tuple(_sds((rows, cols), F32) for _, rows, cols in ordered) + (_sds((nd, B, N_MOD * D_MODEL), F32),)
    res = _pcall(body, name="sum_unpack_small", out_shape=out_shape,
                 scratch_shapes=[pltpu.VMEM((n_sum, LANES), F32) for _, _, _, n_sum in plans])(gathered_early, gathered_late)
    return {n: r for (n, _, _), r in zip(ordered, res[:-1])}, res[-1]


def _small_from_native(nat):
    lanes3 = lambda a: a.reshape(SSM_GROUP_CH, SSM_GROUPS, SSM_STATE)
    return dict(
        g_mix=nat["g_mix"].reshape(D_MODEL), b_gate=jnp.concatenate([nat["b_att"], nat["b_ssm"]], axis=1).reshape(2 * D_MODEL),
        a_re=nat["a_re"].reshape(SSM_GROUPS, SSM_STATE), a_im=nat["a_im"].reshape(SSM_GROUPS, SSM_STATE),
        log_dt=nat["log_dt"][0, :SSM_GROUPS], b_re=_groups_from_lanes(nat["b_re"]), b_im=_groups_from_lanes(nat["b_im"]),
        c_re=lanes3(nat["c_re"]).transpose(1, 0, 2), c_im=lanes3(nat["c_im"]).transpose(1, 0, 2),
        d_skip=nat["d_skip"].reshape(SSM_WIDTH), b_glu=nat["b_glu"].reshape(SSM_WIDTH), g_ffn=nat["g_ffn"].reshape(D_MODEL),
        w_conv=nat["w_conv"], b_conv=nat["b_conv"].reshape(D_FF), g_final=nat["g_final"].reshape(D_MODEL))


def _adamw_multi(params):
    n = len(params)
    bc1 = 1.0 - ADAM_B1 ** ADAM_STEP
    bc2 = 1.0 - ADAM_B2 ** ADAM_STEP

    def body(*refs):
        ins, outs = refs[:4 * n], refs[4 * n:]
        for i in range(n):
            w_ref, g_ref, m_ref, v_ref = ins[4 * i:4 * i + 4]
            d_ref, nm_ref, nv_ref = outs[3 * i:3 * i + 3]
            g = g_ref[...]
            m = ADAM_B1 * m_ref[...] + (1.0 - ADAM_B1) * g
            v = ADAM_B2 * v_ref[...] + (1.0 - ADAM_B2) * (g * g)
            nm_ref[...] = m
            nv_ref[...] = v
            d_ref[...] = -ADAM_LR * ((m / bc1) / (jnp.sqrt(v / bc2) + ADAM_EPS) + ADAM_WD * w_ref[...])

    flat = [a for p in params for a in p]
    out_shape = tuple(_sds(p[0].shape, F32) for p in params for _ in range(3))
    res = _pcall(body, name="adamw_small", out_shape=out_shape)(*flat)
    return [tuple(res[3 * i:3 * i + 3]) for i in range(n)]


def _adamw(w, g, m, v, name, g_other=None):
    R, N = w.shape
    tr = _rows_tile(R, 256)
    spec = pl.BlockSpec((tr, N), lambda i: (i, 0))
    bc1 = 1.0 - ADAM_B1 ** ADAM_STEP
    bc2 = 1.0 - ADAM_B2 ** ADAM_STEP
    two = g_other is not None

    def body(*refs):
        w_ref, g_ref, m_ref, v_ref = refs[:4]
        d_ref, nm_ref, nv_ref = refs[4 + two:7 + two]
        g = g_ref[...]
        if two:
            g = g + refs[4][...]
            refs[8][...] = g
        m = ADAM_B1 * m_ref[...] + (1.0 - ADAM_B1) * g
        v = ADAM_B2 * v_ref[...] + (1.0 - ADAM_B2) * (g * g)
        nm_ref[...] = m
        nv_ref[...] = v
        d_ref[...] = -ADAM_LR * ((m / bc1) / (jnp.sqrt(v / bc2) + ADAM_EPS) + ADAM_WD * w_ref[...])

    shp = _sds((R, N), F32)
    args = (w, g, m, v) + ((g_other,) if two else ())
    return _pcall(body, name=name, out_shape=(shp,) * (3 + two), grid=(R // tr,), in_specs=[spec] * len(args),
                  out_specs=(spec,) * (3 + two), dims=("parallel",))(*args)


_GROUP_MASKS = {
    "all": [(dx, dy, dc) for dx in (0, 1) for dy in (0, 1) for dc in (0, 1) if (dx, dy, dc) != (0, 0, 0)],
    "xy": [(1, 0, 0), (0, 1, 0), (1, 1, 0)],
    "c": [(0, 0, 1)],
}
_GROUP_SLOTS = {"all": 8, "xy": 4, "c": 2}


def _group_slot(group, x, y, c):
    return {"all": 4 * x + 2 * y + c, "xy": 2 * x + y, "c": c}[group]


def _flip(v, d):
    return 1 - v if d else v


def _exchange(arr, group, mode, name):
    return _exchange_list([arr], group, mode, name)[0]


def _exchange_list(arrs, group, mode, name):
    masks = _GROUP_MASKS[group]
    n = len(masks)
    na = len(arrs)
    assert mode in ("gather", "swap") and (mode == "gather" or group == "c")
    has_local = mode == "gather"
    out_shapes = [((_GROUP_SLOTS[group],) if has_local else ()) + arr.shape for arr in arrs]
    bounce = [pltpu.VMEM(arr.shape, arr.dtype) for arr in arrs] if has_local else []

    def body(*refs):
        x_refs, o_refs = refs[:na], refs[na:2 * na]
        send_sems, recv_sems = refs[2 * na], refs[2 * na + 1]
        x, y, c = lax.axis_index("x"), lax.axis_index("y"), lax.axis_index("c")
        me = _group_slot(group, x, y, c)
        if has_local:
            local_sems = refs[2 * na + 2]
            bufs = refs[2 * na + 3:]
            loads = []
            for i in range(na):
                loads.append(pltpu.make_async_copy(x_refs[i], bufs[i], local_sems.at[2 * i]))
                loads[-1].start()
        copies = []
        for i in range(na):
            x_ref, o_ref = x_refs[i], o_refs[i]
            for k, (dx, dy, dc) in enumerate(masks):
                px, py, pc = _flip(x, dx), _flip(y, dy), _flip(c, dc)
                src, dst = (x_ref, o_ref.at[me]) if has_local else (x_ref, o_ref)
                cp =pltpu.make_async_remote_copy(src_ref=src, dst_ref=dst, send_sem=send_sems.at[i * n + k],
                                                  recv_sem=recv_sems.at[i * n + k], device_id=(px, py, pc),
                                                  device_id_type=pl.DeviceIdType.MESH)
                cp.start()
                copies.append(cp)
        if has_local:
            stores = []
            for i in range(na):
                loads[i].wait()
                stores.append(pltpu.make_async_copy(bufs[i], o_refs[i].at[me], local_sems.at[2 * i + 1]))
                stores[-1].start()
        for cp in copies:
            cp.wait()
        if has_local:
            for st in stores:
                st.wait()

    anyspec = pl.BlockSpec(memory_space=pl.ANY)
    scratch = [pltpu.SemaphoreType.DMA((n * na,)), pltpu.SemaphoreType.DMA((n * na,))]
    if has_local:
        scratch += [pltpu.SemaphoreType.DMA((2 * na,))] + bounce
    outs = pl.pallas_call(body, name=name, out_shape=tuple(_sds(s, a.dtype) for s, a in zip(out_shapes, arrs)),
                          in_specs=[anyspec] * na, out_specs=tuple([anyspec] * na), scratch_shapes=scratch,
                          compiler_params=pltpu.CompilerParams(vmem_limit_bytes=V7X_VMEM_LIMIT_BYTES))(*arrs)
    return list(outs)


def _gather_weights(shards, name):
    na = len(shards)
    masks = _GROUP_MASKS["xy"]
    n = len(masks)

    def body(*refs):
        x_refs, o_refs = refs[:na], refs[na:2 * na]
        send_sems, recv_sems, local_sems = refs[2 * na:2 * na + 3]
        bufs = refs[2 * na + 3:]
        x, y, c = lax.axis_index("x"), lax.axis_index("y"), lax.axis_index("c")
        me = 2 * x + y
        sibling = (x, y, 1 - c)
        loads = []
        for i in range(na):
            loads.append(pltpu.make_async_copy(x_refs[i], bufs[i], local_sems.at[2 * i]))
            loads[-1].start()

        def half_of(i, slot, cc):
            h = shards[i].shape[0] // 2
            return o_refs[i].at[slot, pl.ds(pl.multiple_of(cc * h, 8), h), :]

        def src_half(i, cc):
            h = shards[i].shape[0] // 2
            return x_refs[i].at[pl.ds(pl.multiple_of(cc * h, 8), h), :]

        sends = []
        for i in range(na):
            for k, (dx, dy, _) in enumerate(masks):
                cp = pltpu.make_async_remote_copy(src_ref=src_half(i, c), dst_ref=half_of(i, me, c),
                                                  send_sem=send_sems.at[i * 2 * n + k], recv_sem=recv_sems.at[i * 2 * n + k],
                                                  device_id=(_flip(x, dx), _flip(y, dy), c),
                                                  device_id_type=pl.DeviceIdType.MESH)
                cp.start()
                sends.append(cp)
        stores = []
        for i in range(na):
            loads[i].wait()
            stores.append(pltpu.make_async_copy(bufs[i], o_refs[i].at[me], local_sems.at[2 * i + 1]))
            stores[-1].start()
        for i in range(na):
            for k, (dx, dy, _) in enumerate(masks):
                slot = 2 * _flip(x, dx) + _flip(y, dy)
                landed = pltpu.make_async_remote_copy(src_ref=src_half(i, c), dst_ref=half_of(i, slot, c),
                                                      send_sem=send_sems.at[i * 2 * n + k],
                                                      recv_sem=recv_sems.at[i * 2 * n + k], device_id=sibling,
                                                      device_id_type=pl.DeviceIdType.MESH)
                landed.wait_recv()
                fwd = pltpu.make_async_remote_copy(src_ref=half_of(i, slot, c), dst_ref=half_of(i, slot, c),
                                                   send_sem=send_sems.at[i * 2 * n + n + k],
                                                   recv_sem=recv_sems.at[i * 2 * n + n + k], device_id=sibling,
                                                   device_id_type=pl.DeviceIdType.MESH)
                fwd.start()
                sends.append(fwd)
        for i in range(na):
            for k, (dx, dy, _) in enumerate(masks):
                slot = 2 * _flip(x, dx) + _flip(y, dy)
                pltpu.make_async_remote_copy(src_ref=half_of(i, slot, 1 - c), dst_ref=half_of(i, slot, 1 - c),
                                             send_sem=send_sems.at[i * 2 * n + n + k],
                                             recv_sem=recv_sems.at[i * 2 * n + n + k], device_id=sibling,
                                             device_id_type=pl.DeviceIdType.MESH).wait_recv()
        for cp in sends:
            cp.wait_send()
        for st in stores:
            st.wait()

    anyspec = pl.BlockSpec(memory_space=pl.ANY)
    scratch = [pltpu.SemaphoreType.DMA((2 * n * na,)), pltpu.SemaphoreType.DMA((2 * n * na,)),
               pltpu.SemaphoreType.DMA((2 * na,))] + [pltpu.VMEM(s.shape, s.dtype) for s in shards]
    outs = pl.pallas_call(body, name=name, out_shape=tuple(_sds((N_XY,) + s.shape, s.dtype) for s in shards),
                          in_specs=[anyspec] * na, out_specs=tuple([anyspec] * na), scratch_shapes=scratch,
                          compiler_params=pltpu.CompilerParams(vmem_limit_bytes=V7X_VMEM_LIMIT_BYTES))(*shards)
    return list(outs)


BIG = (("w_proj_att", (ATT_WIDTH, D_MODEL), 1), ("w_proj_ssm", (SSM_WIDTH, D_MODEL), 1),
       ("w_glu", (SSM_WIDTH, SSM_WIDTH), 0))
DIRECT = (("w_in", True), ("w_up", True), ("w_down", False), ("w_out", False))
N_XY = 4


def _big_rows(shape):
    return shape[0] * shape[1] // N_XY // LANES


FLAT_ROWS = sum(_big_rows(s) for _, s, _ in BIG)


def _shard_shape(shape, axis):
    return (shape[0] // N_XY, shape[1]) if axis == 0 else (shape[0], shape[1] // N_XY)


def _flatten_shards(shards):
    return jnp.concatenate([shards[n].reshape(_big_rows(s), LANES) for n, s, _ in BIG], axis=0)


def _unflatten_shard(flat):
    out, r = {}, 0
    for n, s, ax in BIG:
        k = _big_rows(s)
        out[n] = flat[r:r + k].reshape(_shard_shape(s, ax))
        r += k
    return out


def _unflatten_full(flat4):
    out, r = {}, 0
    for n, s, ax in BIG:
        k = _big_rows(s)
        sh = _shard_shape(s, ax)
        t = flat4[:, r:r + k].reshape((N_XY,) + sh)
        out[n] = t.reshape(s) if ax == 0 else t.transpose(1, 0, 2).reshape(s)
        r += k
    return out


def _flatten_full(full):
    parts = []
    for n, s, ax in BIG:
        sh = _shard_shape(s, ax)
        t = full[n]
        t = t.reshape((N_XY,) + sh) if ax == 0 else t.reshape(s[0], N_XY, sh[1]).transpose(1, 0, 2)
        parts.append(t.reshape(N_XY, _big_rows(s), LANES))
    return jnp.concatenate(parts, axis=1)


def _lanes_from_groups(a):
    return a.transpose(2, 0, 1).reshape(SSM_GROUP_CH, SSM_LANES)


def _groups_from_lanes(a):
    return a.reshape(SSM_GROUP_CH, SSM_GROUPS, SSM_STATE).transpose(1, 2, 0)


LATE = ("w_up_t", "w_down", "w_out")
EARLY_GRADS = ("w_up_t", "w_down", "w_out")


def _local_step(x3, mod, tgt3, W, P, late_shards=None, scatter_grads=False):
    B, S, _ = x3.shape
    T = B * S
    seq_blocks = S // ATT_BLOCK
    sh1, sc1, gt1, sh2, sc2, gt2 = [m.reshape(B, 1, D_MODEL) for m in jnp.split(mod, 6, axis=-1)]
    g_mix, g_ffn, g_final = P["g_mix"].reshape(1, D_MODEL), P["g_ffn"].reshape(1, D_MODEL), P["g_final"].reshape(1, D_MODEL)
    b_gate = P["b_gate"].reshape(1, 2 * D_MODEL)
    d_skip, b_glu = P["d_skip"].reshape(1, SSM_WIDTH), P["b_glu"].reshape(1, SSM_WIDTH)
    w_conv, b_conv = P["w_conv"], P["b_conv"].reshape(1, D_FF)

    u1 = _norm_mod(x3, g_mix, sc1, sh1).reshape(T, D_MODEL)
    proj = _mm(u1, W["w_in_t"], tb=True, name="mm_proj", out_dtype=BF16)
    proj3 = proj.reshape(B, S, IN_WIDTH)
    us = proj[:, 3 * ATT_WIDTH:3 * ATT_WIDTH + SSM_WIDTH]
    o_att3, lse4, late = _attention_fwd(proj3, seq_blocks, _Riders(late_shards, "gather") if late_shards else None)
    if late_shards:
        W = dict(W, **{n: f.reshape(-1, LANES) for n, f in zip(LATE, late)})
        w_conv = late[len(LATE)].transpose(1, 0, 2).reshape(3, D_FF)
        W.update(_unflatten_full(late[len(LATE) + 1]))
    o_att = o_att3.reshape(T, ATT_WIDTH)
    y_att = _mm(o_att, W["w_proj_att"], name="mm_proj_att", out_dtype=BF16)

    lr = P["a_re"].reshape(1, SSM_LANES)
    li = P["a_im"].reshape(1, SSM_LANES)
    ldt = jnp.repeat(P["log_dt"], SSM_STATE).reshape(1, SSM_LANES)
    br, bi = _lanes_from_groups(P["b_re"]), _lanes_from_groups(P["b_im"])
    cr = P["c_re"].transpose(1, 0, 2).reshape(SSM_GROUP_CH, SSM_LANES)
    ci = P["c_im"].transpose(1, 0, 2).reshape(SSM_GROUP_CH, SSM_LANES)
    abar, w_bu, w_c = _ssm_params(lr, li, ldt, br, bi, cr, ci)
    xs3, y_core3 = _ssm_scan_fwd(proj3, abar, w_bu, w_c)
    y5, s_out = _ssm_post(y_core3.reshape(T, SSM_WIDTH), us, d_skip, W["w_glu"], b_glu)
    y_ssm = _mm(s_out, W["w_proj_ssm"], name="mm_proj_ssm", out_dtype=BF16)

    merged = _merge(proj, y_att, y_ssm, b_gate)
    mix = _mm(merged, W["w_out"], name="mm_out", out_dtype=BF16)
    mix3 = mix.reshape(B, S, D_MODEL)

    h1, u2 = _resid_norm_mod(x3, mix3, gt1, g_ffn, sc2, sh2)
    u2 = u2.reshape(T, D_MODEL)
    up3 = _mm(u2, W["w_up_t"], tb=True, name="mm_up", out_dtype=BF16).reshape(B, S, 2 * D_FF)
    act = _conv_act(up3, w_conv, b_conv).reshape(T, D_FF)
    ffn3 = _mm(act, W["w_down"], name="mm_down", out_dtype=BF16).reshape(B, S, D_MODEL)
    dh2, dffn, dgt2, dg_final, loss = _final_loss(h1, ffn3, tgt3, gt2, g_final)

    dffn = dffn.reshape(T, D_MODEL)
    gw = {}
    gw["w_down"] = _mm(act, dffn, ta=True, out_dtype=BF16, name="mm_dw_down")
    dact3 = _mm(dffn, W["w_down"], tb=True, name="mm_dact", out_dtype=BF16).reshape(B, S, D_FF)
    dup3, dw_conv, db_conv = _conv_bwd(up3, dact3, w_conv, b_conv)
    dup = dup3.reshape(2, T, D_FF)
    gw["w_up_t"] = _mm(dup, u2, ta=True, out_dtype=BF16, name="mm_dw_up")
    du2 = _mm(dup, W["w_up_t"], name="mm_du2", out_dtype=BF16).reshape(B, S, D_MODEL)
    dh1, dsh2, dsc2, dg_ffn, dgt1, dmix = _norm_bwd(h1, du2, dh2, g_ffn, sc2, "norm_bwd2", mix3=mix3, gt=gt1)

    dmix = dmix.reshape(T, D_MODEL)
    gw["w_out"] = _mm(merged, dmix, ta=True, out_dtype=BF16, name="mm_dw_out")
    dmerged = _mm(dmix, W["w_out"], tb=True, name="mm_dmerged", out_dtype=BF16)
    dy_att, dy_ssm, dga, dgs, db_att, db_ssm = _merge_bwd(proj, y_att, y_ssm, b_gate, dmerged)

    gw["w_proj_ssm"] = _mm(s_out, dy_ssm, ta=True, name="mm_dw_proj_ssm")
    ds_out = _mm(dy_ssm, W["w_proj_ssm"], tb=True, name="mm_ds_out")
    dy5, dd_skip, db_glu, dw_glu = _ssm_post_bwd(y5, us, ds_out, d_skip, W["w_glu"], b_glu)
    gw["w_glu"] = dw_glu
    dus3, dab, dwbu, dwc = _ssm_scan_bwd(proj3, dy5.reshape(B, S, SSM_WIDTH), xs3, abar, w_bu, w_c, d_skip)
    dus = dus3.reshape(T, SSM_WIDTH)
    dlr, dli, dldt, dbr, dbi, dcr, dci = _ssm_params_bwd(lr, li, ldt, br, bi, dab, dwbu, dwc)

    gw["w_proj_att"] = _mm(o_att, dy_att, ta=True, name="mm_dw_proj_att")
    do_att = _mm(dy_att, W["w_proj_att"], tb=True, out_dtype=BF16, name="mm_do_att")
    early = [gw[n].reshape(N_XY, -1, LANES) for n in EARLY_GRADS]
    early.append(_flatten_full({n: gw[n].astype(BF16) for n, _, _ in BIG}))
    dq3, dk3, dv3, parts = _attention_bwd(proj3, do_att.reshape(B, S, ATT_WIDTH), o_att3, lse4, seq_blocks,
                                          _Riders(early, "scatter") if scatter_grads else None)
    dproj = jnp.concatenate([t.reshape(T, ATT_WIDTH) for t in (dq3, dk3, dv3)] + [dus, dga, dgs], axis=1)
    dmods = [None, None, dgt1, dsh2, dsc2, dgt2]
    native = dict(b_att=db_att, b_ssm=db_ssm, a_re=dlr, a_im=dli, log_dt=dldt, b_re=dbr, b_im=dbi, c_re=dcr, c_im=dci,
                  d_skip=dd_skip, b_glu=db_glu, g_ffn=dg_ffn, w_conv=dw_conv, b_conv=db_conv, g_final=dg_final, loss=loss)
    small_early = _pack_small(native, dmods, False)
    sums, sums_sib, last_parts = [], [], []
    if scatter_grads:
        sums = [_sum_slots(p, "sum_chips_%d" % i) for i, p in enumerate(parts)]
        riders = _RiderGroup([_Riders([small_early], "gather", "all"), _Riders(sums, "swap", "c")])
        gw["w_in_t"], rode = _mm(dproj, u1, ta=True, out_dtype=BF16, name="mm_dw_in", riders=riders)
        small_early, sums_sib = rode[0], rode[1:]
        du1, last_parts = _mm(dproj, W["w_in_t"], name="mm_du1", out_dtype=BF16,
                              riders=_Riders([gw["w_in_t"].reshape(N_XY, -1, LANES)], "scatter"))
    else:
        gw["w_in_t"] = _mm(dproj, u1, ta=True, out_dtype=BF16, name="mm_dw_in")
        du1 = _mm(dproj, W["w_in_t"], name="mm_du1", out_dtype=BF16)
    du1 = du1.reshape(B, S, D_MODEL)
    dx, dsh1, dsc1, dg_mix = _norm_bwd(x3, du1, dh1, g_mix, sc1, "norm_bwd1")
    dmods[0], dmods[1] = dsh1, dsc1
    native["g_mix"] = dg_mix
    return loss, dx, dmods, gw, native, (sums, sums_sib, last_parts), small_early


WEIGHTS = ['w_ada', 'b_ada', 'g_mix', 'w_in', 'b_gate', 'a_re', 'a_im', 'log_dt', 'b_re', 'b_im', 'c_re', 'c_im', 'd_skip',
           'w_glu', 'b_glu', 'w_proj_att', 'w_proj_ssm', 'w_out', 'g_ffn', 'w_up', 'w_conv', 'b_conv', 'w_down', 'g_final']
SMALL = ['g_mix', 'b_gate', 'a_re', 'a_im', 'log_dt', 'b_re', 'b_im', 'c_re', 'c_im', 'd_skip', 'b_glu', 'g_ffn', 'w_conv',
         'b_conv', 'g_final']


def kernel(x, c, w_ada, b_ada, g_mix, w_in, b_gate, a_re, a_im, log_dt, b_re, b_im, c_re, c_im, d_skip, w_glu, b_glu, w_proj_att, w_proj_ssm, w_out, g_ffn, w_up, w_conv, b_conv, w_down, g_final, loss_target, m_w_ada, m_b_ada, m_g_mix, m_w_in, m_b_gate, m_a_re, m_a_im, m_log_dt, m_b_re, m_b_im, m_c_re, m_c_im, m_d_skip, m_w_glu, m_b_glu, m_w_proj_att, m_w_proj_ssm, m_w_out, m_g_ffn, m_w_up, m_w_conv, m_b_conv, m_w_down, m_g_final, v_w_ada, v_b_ada, v_g_mix, v_w_in, v_b_gate, v_a_re, v_a_im, v_log_dt, v_b_re, v_b_im, v_c_re, v_c_im, v_d_skip, v_w_glu, v_b_glu, v_w_proj_att, v_w_proj_ssm, v_w_out, v_g_ffn, v_w_up, v_w_conv, v_b_conv, v_w_down, v_g_final):
    args = dict(locals())
    w = {n: args[n] for n in WEIGHTS}
    m = {n: args["m_" + n] for n in WEIGHTS}
    v = {n: args["v_" + n] for n in WEIGHTS}
    B, S, _ = x.shape
    ix, iy, ic = lax.axis_index("x"), lax.axis_index("y"), lax.axis_index("c")
    chip = 2 * ix + iy
    ada_cols = w_ada.shape[2]

    c_all = _exchange(c, "all", "gather", "gather_c").reshape(8 * B, D_MODEL)
    b_cols = lax.dynamic_slice_in_dim(b_ada, chip * ada_cols, ada_cols, axis=1)
    mod_cols = _ada_fwd(c_all, w_ada[0], b_cols)
    mod_all = _exchange(mod_cols, "xy", "gather", "gather_mod")
    mod_all = mod_all.transpose(1, 0, 2).reshape(8 * B, 6 * D_MODEL)
    mod = lax.dynamic_slice_in_dim(mod_all, (4 * ix + 2 * iy + ic) * B, B, axis=0)

    shard = {n + ("_t" if t else ""): (w[n][0].T if t else w[n][0]).astype(BF16) for n, t in DIRECT}
    misc = _flatten_shards({n: w[n][0] for n, _, _ in BIG}).astype(BF16)
    (w_in_full,) = _gather_weights([shard["w_in_t"]], "gather_weights")
    W = {"w_in_t": w_in_full.reshape(-1, LANES)}

    P = {n: w[n][0] for n in SMALL if n not in ("w_conv", "g_final")}
    P["w_conv"] = None
    P["g_final"] = g_final

    loss, dx, dmods, gw, native, parts, small_early = _local_step(x, mod, loss_target, W, P,
                                                                  [shard[n] for n in LATE] + [w_conv[0], misc], True)

    small_late = _exchange(_pack_small(native, dmods, True), "all", "gather", "gather_small")
    native_sum, dmod_all = _sum_unpack_small(small_early, small_late, B)
    loss = native_sum["loss"][0, 0]
    g_small = _small_from_native(native_sum)
    dmod_all = dmod_all.reshape(8 * B, N_MOD * D_MODEL)
    dmod_cols = lax.dynamic_slice_in_dim(dmod_all, chip * ada_cols, ada_cols, axis=1)
    g_w_ada, g_b_ada = _ada_bwd(c_all, dmod_all, dmod_cols)

    red, red_sib, last_parts = parts
    red = red + [_sum_slots(last_parts[0], "sum_chips_w_in")]
    red_sib = red_sib + [_exchange(red[-1], "c", "swap", "share_cores")]
    order = list(EARLY_GRADS) + ["misc", "w_in_t"]
    halves = dict(zip(order, zip(red, red_sib)))

    grads = {"w_ada": g_w_ada[None], "b_ada": g_b_ada}
    grads["w_up"] = _add2(*halves["w_up_t"], F32, "add_cores_w_up").T[None]
    for k, gk in _unflatten_shard(_add2(*halves["misc"], F32, "add_cores_misc")).items():
        grads[k] = gk[None]
    wc_cols = w_conv.shape[2]
    for n in SMALL:
        g = g_small[n]
        if n == "w_conv":
            g = lax.dynamic_slice_in_dim(g, chip * wc_cols, wc_cols, axis=1)
        grads[n] = g.reshape(w[n].shape)

    delta, new_m, new_v = {}, {}, {}
    for n in ["w_ada"] + [b for b, _ in DIRECT] + [b for b, _, _ in BIG]:
        shp = w[n].shape
        if n == "w_in":
            r, s = halves["w_in_t"]
            d2, m2, v2, g2 = _adamw(w[n][0].T, r, m[n][0].T, v[n][0].T, "adamw_" + n, g_other=s)
            d2, m2, v2, grads[n] = d2.T, m2.T, v2.T, g2.T[None]
        elif n in ("w_down", "w_out"):
            r, s = halves[n]
            d2, m2, v2, g2 = _adamw(w[n][0], r, m[n][0], v[n][0], "adamw_" + n, g_other=s)
            grads[n] = g2[None]
        else:
            d2, m2, v2 = _adamw(w[n][0], grads[n][0], m[n][0], v[n][0], "adamw_" + n)
        delta[n], new_m[n], new_v[n] = d2.reshape(shp), m2.reshape(shp), v2.reshape(shp)
    rest = ["b_ada"] + SMALL

    def drop(a):
        return a.reshape(1, -1) if a.ndim == 1 else (a if a.ndim == 2 else a[0])

    upd = _adamw_multi([(drop(w[n]), drop(grads[n]), drop(m[n]), drop(v[n])) for n in rest])
    for n, (dd, mm, vv) in zip(rest, upd):
        delta[n], new_m[n], new_v[n] = dd.reshape(w[n].shape), mm.reshape(w[n].shape), vv.reshape(w[n].shape)

    return (loss, dx, *[grads[n] for n in WEIGHTS], *[delta[n] for n in WEIGHTS], *[new_m[n] for n in WEIGHTS],
            *[new_v[n] for n in WEIGHTS])
```

```python
import functools
import math

import jax
import jax.numpy as jnp
from jax import lax
from jax.experimental import pallas as pl
from jax.experimental.pallas import tpu as pltpu

F32, BF16 = jnp.float32, jnp.bfloat16

D_MODEL = 1024
N_HEADS = 8
HEAD_DIM = 64
ATT_WIDTH = 512
SSM_GROUPS = 16
SSM_GROUP_CH = 16
SSM_WIDTH = 256
SSM_STATE = 64
SSM_LANES = SSM_GROUPS * SSM_STATE
D_FF = 2048
IN_WIDTH = 3 * ATT_WIDTH + SSM_WIDTH + 2 * D_MODEL
ATT_BLOCK = 128
N_PATTERNS = 3
EPS = 1e-6
NEG_INF = -1e30

ADAM_LR, ADAM_B1, ADAM_B2, ADAM_EPS, ADAM_WD, ADAM_STEP = 0.001, 0.9, 0.999, 1e-08, 0.01, 10

V7X_VMEM_LIMIT_BYTES = 56 * 1024 * 1024
LANES = 1024


def _pcall(body, *, name, out_shape, grid=(), in_specs=None, out_specs=None, scratch_shapes=(), dims=None):
    params = dict(vmem_limit_bytes=V7X_VMEM_LIMIT_BYTES)
    if dims is not None:
        params["dimension_semantics"] = dims
    specs = {}
    if in_specs is not None:
        specs = dict(grid=grid, in_specs=in_specs, out_specs=out_specs)
    return pl.pallas_call(body, name=name, out_shape=out_shape, scratch_shapes=scratch_shapes,
                          compiler_params=pltpu.CompilerParams(**params), **specs)


def _sds(shape, dtype):
    return jax.ShapeDtypeStruct(tuple(shape), dtype)


def _tile(n, target):
    if n <= target:
        return n
    for t in range(target - target % 128, 0, -128):
        if n % t == 0:
            return t
    raise ValueError((n, target))


def _sig(v):
    return pl.reciprocal(1.0 + jnp.exp(-v), approx=True)


def _mm(a, b, *, name, ta=False, tb=False, out_dtype=F32, tm=2048, tn=1024, tk=1024, riders=None):
    halves = a.ndim == 3
    if halves:
        a_rows, a_cols = a.shape[1], 2 * a.shape[2]
    else:
        a_rows, a_cols = a.shape
    if ta:
        K, M = a_rows, a_cols
    else:
        M, K = a_rows, a_cols
    if tb:
        N, K2 = b.shape
    else:
        K2, N = b.shape
    assert K == K2, (a.shape, b.shape)
    if halves:
        tm, tk = (min(tm, M // 2), tk) if ta else (tm, min(tk, K // 2))
    tm, tn, tk = _tile(M, tm), _tile(N, tn), _tile(K, tk)
    nk = K // tk
    if halves and ta:
        per = a.shape[2] // tm
        a_spec = pl.BlockSpec((None, tk, tm), lambda i, j, k: (i // per, k, i % per))
    elif halves:
        per = a.shape[2] // tk
        a_spec = pl.BlockSpec((None, tm, tk), lambda i, j, k: (k // per, i, k % per))
    else:
        a_spec = pl.BlockSpec((tk, tm), lambda i, j, k: (k, i)) if ta else pl.BlockSpec((tm, tk), lambda i, j, k: (i, k))
    b_spec = pl.BlockSpec((tn, tk), lambda i, j, k: (j, k)) if tb else pl.BlockSpec((tk, tn), lambda i, j, k: (k, j))
    dn = (((0 if ta else 1,), (1 if tb else 0,)), ((), ()))

    def body(a_ref, b_ref, o_ref, acc_ref):
        k = pl.program_id(2)

        @pl.when(k == 0)
        def _():
            acc_ref[...] = jnp.zeros_like(acc_ref)

        acc_ref[...] += lax.dot_general(a_ref[...].astype(BF16), b_ref[...].astype(BF16), dn,
                                        preferred_element_type=F32)

        @pl.when(k == nk - 1)
        def _():
            o_ref[...] = acc_ref[...].astype(out_dtype)

    def body_single(a_ref, b_ref, o_ref):
        o_ref[...] = lax.dot_general(a_ref[...].astype(BF16), b_ref[...].astype(BF16), dn,
                                     preferred_element_type=F32).astype(out_dtype)

    grid = (M // tm, N // tn, nk)
    scratch = [] if nk == 1 else [pltpu.VMEM((tm, tn), F32)]
    o_spec = pl.BlockSpec((tm, tn), lambda i, j, k: (i, j))
    if riders is None:
        return _pcall(body_single if nk == 1 else body, name=name, out_shape=_sds((M, N), out_dtype), grid=grid,
                      in_specs=[a_spec, b_spec], out_specs=o_spec, scratch_shapes=scratch,
                      dims=("parallel", "parallel", "arbitrary"))(a, b)
    rs = riders
    res = _pcall(_with_riders(body_single if nk == 1 else body, rs, 2, 1, len(scratch), tuple(g - 1 for g in grid)),
                 name=name, out_shape=(_sds((M, N), out_dtype),) + tuple(rs.out_shape), grid=grid,
                 in_specs=[a_spec, b_spec] + rs.specs, out_specs=(o_spec,) + tuple(rs.specs),
                 scratch_shapes=scratch + rs.scratch, dims=("arbitrary", "arbitrary", "arbitrary"))(a, b, *rs.arrs)
    return res[0], list(res[1:])


def _ada_fwd(c_all, w_ada, b_ada_cols):
    n = w_ada.shape[1]

    def body(c_ref, w_ref, b_ref, o_ref):
        c = c_ref[...]
        act = c * _sig(c)
        o_ref[...] = jnp.dot(act.astype(BF16), w_ref[...].astype(BF16), preferred_element_type=F32) + b_ref[...]

    return _pcall(body, name="ada_fwd", out_shape=_sds((c_all.shape[0], n), F32))(c_all, w_ada, b_ada_cols)


def _ada_bwd(c_all, dmod_all, dmod_cols):
    n = dmod_cols.shape[1]

    def body(c_ref, da_ref, dc_ref, gw_ref, gb_ref):
        c = c_ref[...]
        act = c * _sig(c)
        gw_ref[...] = lax.dot_general(act, dc_ref[...], (((0,), (0,)), ((), ())), preferred_element_type=F32,
                                      precision=lax.Precision.HIGHEST)
        gb_ref[...] = jnp.sum(da_ref[...], axis=0, keepdims=True)

    return _pcall(body, name="ada_bwd", out_shape=(_sds((D_MODEL, n), F32), _sds((1, dmod_all.shape[1]), F32)))(
        c_all, dmod_all, dmod_cols)


ROW_TILE = 1024


def _row_specs(B, S):
    ts = min(S, ROW_TILE)
    row = pl.BlockSpec((1, ts, D_MODEL), lambda b, s: (b, s, 0))
    bvec = pl.BlockSpec((1, 1, D_MODEL), lambda b, s: (b, 0, 0))
    gvec = pl.BlockSpec((1, D_MODEL), lambda b, s: (0, 0))
    return ts, row, bvec, gvec


def _norm_mod(x3, g, sc, sh):
    B, S, _ = x3.shape
    ts, row, bvec, gvec = _row_specs(B, S)

    def body(x_ref, g_ref, sc_ref, sh_ref, u_ref):
        x = x_ref[0]
        r = lax.rsqrt(jnp.mean(x * x, axis=-1, keepdims=True) + EPS)
        u_ref[0] = ((x * r) * g_ref[...] * (1.0 + sc_ref[0]) + sh_ref[0]).astype(BF16)

    return _pcall(body, name="norm_mod1", out_shape=_sds(x3.shape, BF16), grid=(B, S // ts),
                  in_specs=[row, gvec, bvec, bvec], out_specs=row, dims=("parallel", "parallel"))(x3, g, sc, sh)


def _resid_norm_mod(x3, mix3, gt, g, sc, sh):
    B, S, _ = x3.shape
    ts, row, bvec, gvec = _row_specs(B, S)

    def body(x_ref, m_ref, gt_ref, g_ref, sc_ref, sh_ref, h_ref, u_ref):
        h = x_ref[0] + gt_ref[0] * m_ref[0]
        h_ref[0] = h
        r = lax.rsqrt(jnp.mean(h * h, axis=-1, keepdims=True) + EPS)
        u_ref[0] = ((h * r) * g_ref[...] * (1.0 + sc_ref[0]) + sh_ref[0]).astype(BF16)

    return _pcall(body, name="resid_norm_mod2", out_shape=(_sds(x3.shape, F32), _sds(x3.shape, BF16)),
                  grid=(B, S // ts), in_specs=[row, row, bvec, gvec, bvec, bvec], out_specs=(row, row),
                  dims=("parallel", "parallel"))(x3, mix3, gt, g, sc, sh)


def _norm_bwd(h3, du3, dres3, g, sc, name, mix3=None, gt=None, riders=None):
    B, S, _ = h3.shape
    ts, row, bvec, gvec = _row_specs(B, S)
    with_gate = mix3 is not None

    def body(*refs):
        if with_gate:
            h_ref, du_ref, dr_ref, g_ref, sc_ref, m_ref, gt_ref, dh_ref, dsh_ref, dsc_ref, dg_ref, dgt_ref, dm_ref = refs
        else:
            h_ref, du_ref, dr_ref, g_ref, sc_ref, dh_ref, dsh_ref, dsc_ref, dg_ref = refs
        b, s = pl.program_id(0), pl.program_id(1)
        h = h_ref[0]
        r = lax.rsqrt(jnp.mean(h * h, axis=-1, keepdims=True) + EPS)
        xn = h * r
        du = du_ref[0].astype(F32)
        g = g_ref[...]
        sc1 = 1.0 + sc_ref[0]
        dxn = du * g * sc1
        dh = dr_ref[0].astype(F32) + r * (dxn - xn * jnp.mean(dxn * xn, axis=-1, keepdims=True))
        dh_ref[0] = dh.astype(dh_ref.dtype)

        @pl.when(s == 0)
        def _():
            dsh_ref[...] = jnp.zeros_like(dsh_ref)
            dsc_ref[...] = jnp.zeros_like(dsc_ref)
            if with_gate:
                dgt_ref[...] = jnp.zeros_like(dgt_ref)

        @pl.when((s == 0) & (b == 0))
        def _():
            dg_ref[...] = jnp.zeros_like(dg_ref)

        dux = du * xn
        dsh_ref[0] += jnp.sum(du, axis=0, keepdims=True)
        dsc_ref[0] += jnp.sum(dux * g, axis=0, keepdims=True)
        dg_ref[...] += jnp.sum(dux * sc1, axis=0, keepdims=True)
        if with_gate:
            dgt_ref[0] += jnp.sum(dh * m_ref[0], axis=0, keepdims=True)
            dm_ref[0] = (dh * gt_ref[0]).astype(BF16)

    bshape = _sds((B, 1, D_MODEL), F32)
    in_specs = [row, row, row, gvec, bvec]
    out_shape = [_sds(h3.shape, BF16 if with_gate else F32), bshape, bshape, _sds((1, D_MODEL), F32)]
    out_specs = [row, bvec, bvec, gvec]
    args = [h3, du3, dres3, g, sc]
    if with_gate:
        in_specs += [row, bvec]
        out_shape += [bshape, _sds(h3.shape, BF16)]
        out_specs += [bvec, row]
        args += [mix3, gt]
    if riders is None:
        return _pcall(body, name=name, out_shape=tuple(out_shape), grid=(B, S // ts), in_specs=in_specs,
                      out_specs=tuple(out_specs), dims=("arbitrary", "arbitrary"))(*args)
    rs = riders
    res = _pcall(_with_riders(body, rs, len(args), len(out_shape), 0, (B - 1, S // ts - 1)), name=name,
                 out_shape=tuple(out_shape) + tuple(rs.out_shape), grid=(B, S // ts), in_specs=in_specs + rs.specs,
                 out_specs=tuple(out_specs) + tuple(rs.specs), scratch_shapes=rs.scratch,
                 dims=("arbitrary", "arbitrary"))(*args, *rs.arrs)
    return tuple(res[:len(out_shape)]) + (list(res[len(out_shape):]),)


def _final_loss(h1, ffn3, tgt3, gt, gfin):
    B, S, _ = h1.shape
    ts, row, bvec, gvec = _row_specs(B, S)
    one = pl.BlockSpec((1, 1), lambda b, s: (0, 0))

    def body(h_ref, f_ref, t_ref, gt_ref, gf_ref, dh_ref, dff_ref, dgt_ref, dgf_ref, loss_ref):
        b, s = pl.program_id(0), pl.program_id(1)
        f = f_ref[0].astype(F32)
        gtv = gt_ref[0]
        gf = gf_ref[...]
        h2 = h_ref[0] + gtv * f
        r = lax.rsqrt(jnp.mean(h2 * h2, axis=-1, keepdims=True) + EPS)
        n = h2 * r
        e = n * gf - t_ref[0]
        dy = e * (1.0 / D_MODEL)
        dn = dy * gf
        dh2 = r * (dn - n * jnp.mean(dn * n, axis=-1, keepdims=True))
        dh_ref[0] = dh2.astype(BF16)
        dff_ref[0] = (dh2 * gtv).astype(BF16)

        @pl.when(s == 0)
        def _():
            dgt_ref[...] = jnp.zeros_like(dgt_ref)

        @pl.when((s == 0) & (b == 0))
        def _():
            dgf_ref[...] = jnp.zeros_like(dgf_ref)
            loss_ref[...] = jnp.zeros_like(loss_ref)

        dgt_ref[0] += jnp.sum(dh2 * f, axis=0, keepdims=True)
        dgf_ref[...] += jnp.sum(dy * n, axis=0, keepdims=True)
        rows = jnp.sum(e * e, axis=1, keepdims=True)
        loss_ref[...] += jnp.sum(rows, axis=0, keepdims=True) * (0.5 / D_MODEL)

    return _pcall(body, name="final_loss",
                  out_shape=(_sds(h1.shape, BF16), _sds(h1.shape, BF16), _sds((B, 1, D_MODEL), F32),
                             _sds((1, D_MODEL), F32), _sds((1, 1), F32)),
                  grid=(B, S // ts), in_specs=[row, row, row, bvec, gvec], out_specs=(row, row, bvec, gvec, one),
                  dims=("arbitrary", "arbitrary"))(h1, ffn3, tgt3, gt, gfin)


ATT_GROUP = 4
ATT_GW = ATT_GROUP * HEAD_DIM
ATT_GROUPS = N_HEADS // ATT_GROUP
ATT_PAIRS = ATT_GW // ATT_BLOCK
ATT_UNROLL = 5
ATT_RESIDUE_UNROLL = 4
NT_DIMS = (((1,), (1,)), ((), ()))
TN_DIMS = (((0,), (0,)), ((), ()))


def _att_rows(start, d):
    if d == 1:
        return pl.ds(start if isinstance(start, int) else pl.multiple_of(start, ATT_BLOCK), ATT_BLOCK)
    return pl.ds(start, ATT_BLOCK, stride=d)


def _att_fill_bias(bias_ref, g, d):
    a = lax.broadcasted_iota(jnp.int32, (ATT_BLOCK, ATT_BLOCK), 0)
    j = lax.broadcasted_iota(jnp.int32, (ATT_BLOCK, ATT_BLOCK), 1)
    dist = (a - j).astype(F32)
    for hh in range(ATT_GROUP):
        t, e = divmod(hh, 2)
        rs = slice(e * ATT_BLOCK, (e + 1) * ATT_BLOCK)
        lo = 2.0 ** (-8.0 * (hh + 1) / N_HEADS) * d
        hi = 2.0 ** (-8.0 * (ATT_GROUP + hh + 1) / N_HEADS) * d
        slope = jnp.where(g == 0, lo, hi).astype(F32)
        bias_ref[t, rs, 0:ATT_BLOCK] = jnp.where(a >= j, -slope * dist, NEG_INF)
        bias_ref[t, rs, ATT_BLOCK:] = jnp.where(j >= a, -slope * (dist + float(ATT_BLOCK)), NEG_INF)


def _stack_heads(v2, low):
    return jnp.concatenate([jnp.where(low, v2, 0.0), jnp.where(low, 0.0, v2)], axis=0).astype(BF16)


def _unstack_heads(r2, low):
    return jnp.where(low, r2[0:ATT_BLOCK], r2[ATT_BLOCK:])


class _Riders:
    def __init__(self, arrs, mode, group="xy"):
        self.arrs, self.mode, self.n, self.group = list(arrs), mode, len(arrs), group
        k = len(_GROUP_MASKS[group])
        self.scratch = [pltpu.SemaphoreType.DMA((k * self.n,)), pltpu.SemaphoreType.DMA((k * self.n,))]
        if mode == "swap":
            assert group == "c"
            self.out_shape = [_sds(a.shape, a.dtype) for a in self.arrs]
        else:
            slot_shapes = [a.shape if mode == "gather" else a.shape[1:] for a in self.arrs]
            self.out_shape = [_sds((_GROUP_SLOTS[group],) + s, a.dtype) for s, a in zip(slot_shapes, self.arrs)]
            self.scratch += [pltpu.SemaphoreType.DMA((2 * self.n,))] + [pltpu.VMEM(s, a.dtype)
                                                                        for s, a in zip(slot_shapes, self.arrs)]
        self.specs = [pl.BlockSpec(memory_space=pl.ANY)] * self.n

    def _remote(self, x_refs, o_refs, send_sems, recv_sems):
        x, y, c = lax.axis_index("x"), lax.axis_index("y"), lax.axis_index("c")
        me = _group_slot(self.group, x, y, c)
        masks = _GROUP_MASKS[self.group]
        cps = []
        for i in range(self.n):
            for k, (dx, dy, dc) in enumerate(masks):
                px, py, pc = _flip(x, dx), _flip(y, dy), _flip(c, dc)
                src = x_refs[i].at[_group_slot(self.group, px, py, pc)] if self.mode == "scatter" else x_refs[i]
                dst = o_refs[i] if self.mode == "swap" else o_refs[i].at[me]
                cps.append(pltpu.make_async_remote_copy(
                    src_ref=src, dst_ref=dst, send_sem=send_sems.at[len(masks) * i + k],
                    recv_sem=recv_sems.at[len(masks) * i + k], device_id=(px, py, pc),
                    device_id_type=pl.DeviceIdType.MESH))
        return cps, me

    def start(self, x_refs, o_refs, scratch):
        cps, me = self._remote(x_refs, o_refs, scratch[0], scratch[1])
        for cp in cps:
            cp.start()
        if self.mode == "swap":
            return
        local_sems, bufs = scratch[2], scratch[3:]
        for i in range(self.n):
            src = x_refs[i] if self.mode == "gather" else x_refs[i].at[me]
            load = pltpu.make_async_copy(src, bufs[i], local_sems.at[2 * i])
            load.start()
            load.wait()
            pltpu.make_async_copy(bufs[i], o_refs[i].at[me], local_sems.at[2 * i + 1]).start()

    def wait(self, x_refs, o_refs, scratch):
        cps, me = self._remote(x_refs, o_refs, scratch[0], scratch[1])
        for cp in cps:
            cp.wait()
        if self.mode == "swap":
            return
        local_sems, bufs = scratch[2], scratch[3:]
        for i in range(self.n):
            pltpu.make_async_copy(bufs[i], o_refs[i].at[me], local_sems.at[2 * i + 1]).wait()


class _RiderGroup:
    def __init__(self, members):
        self.members = list(members)
        self.n = sum(m.n for m in self.members)
        self.arrs = [a for m in self.members for a in m.arrs]
        self.out_shape = [s for m in self.members for s in m.out_shape]
        self.specs = [s for m in self.members for s in m.specs]
        self.scratch = [s for m in self.members for s in m.scratch]

    def _each(self, x_refs, o_refs, scratch):
        i = j = 0
        for m in self.members:
            yield m, x_refs[i:i + m.n], o_refs[i:i + m.n], scratch[j:j + len(m.scratch)]
            i, j = i + m.n, j + len(m.scratch)

    def start(self, x_refs, o_refs, scratch):
        for m, xs, os, sc in self._each(x_refs, o_refs, scratch):
            m.start(xs, os, sc)

    def wait(self, x_refs, o_refs, scratch):
        for m, xs, os, sc in self._each(x_refs, o_refs, scratch):
            m.wait(xs, os, sc)


def _with_riders(compute, riders, n_in, n_out, n_scratch, last_step):
    if riders is None:
        return compute
    n = riders.n

    def body(*refs):
        ins, x_refs = refs[:n_in], refs[n_in:n_in + n]
        outs, o_refs = refs[n_in + n:n_in + n + n_out], refs[n_in + n + n_out:n_in + 2 * n + n_out]
        scratch = refs[n_in + 2 * n + n_out:]
        own, ride = scratch[:n_scratch], scratch[n_scratch:]
        ids = [pl.program_id(i) for i in range(len(last_step))]
        first = functools.reduce(jnp.logical_and, [i == 0 for i in ids])
        last = functools.reduce(jnp.logical_and, [i == l for i, l in zip(ids, last_step)])

        @pl.when(first)
        def _():
            riders.start(x_refs, o_refs, ride)

        compute(*ins, *outs, *own)

        @pl.when(last)
        def _():
            riders.wait(x_refs, o_refs, ride)

    return body


def _attention_fwd(proj3, seq_blocks, riders=None):
    B, S, _ = proj3.shape
    scale = HEAD_DIM ** -0.5
    nq = ATT_WIDTH // ATT_GW

    def col(k):
        return pl.BlockSpec((1, S, ATT_GW), lambda b, g, k=k: (b, 0, k * nq + g))

    o_spec = pl.BlockSpec((1, S, ATT_GW), lambda b, g: (b, 0, g))
    l_spec = pl.BlockSpec((1, 1, S, ATT_BLOCK), lambda b, g: (b, g, 0, 0))

    def compute(q_ref, k_ref, v_ref, o_ref, lse_ref, qf, kf, vf, os, ls, bias):
        g = pl.program_id(1)
        for t in range(ATT_PAIRS):
            ts = slice(t * ATT_BLOCK, (t + 1) * ATT_BLOCK)
            qf[t] = q_ref[0, :, ts].astype(F32) * scale
            kf[t] = k_ref[0, :, ts].astype(F32)
            vf[t] = v_ref[0, :, ts].astype(F32)
        lane = lax.broadcasted_iota(jnp.int32, (ATT_BLOCK, ATT_BLOCK), 1)
        low = lane < HEAD_DIM

        def block(p, d, r, n, has_prev):
            start = n * (ATT_BLOCK * d) + r
            rows = _att_rows(start, d)
            prows = _att_rows(start - ATT_BLOCK * d, d) if has_prev else None
            lse_t = jnp.zeros((ATT_BLOCK, ATT_BLOCK), F32)
            for t in range(ATT_PAIRS):
                q2 = _stack_heads(qf[t, rows, :], low)
                k2 = kf[t, rows, :].astype(BF16)
                v2 = vf[t, rows, :].astype(BF16)
                if has_prev:
                    k2 = jnp.concatenate([k2, kf[t, prows, :].astype(BF16)], axis=0)
                    v2 = jnp.concatenate([v2, vf[t, prows, :].astype(BF16)], axis=0)
                    b2 = bias[t]
                else:
                    b2 = bias[t, :, 0:ATT_BLOCK]
                s = lax.dot_general(q2, k2, NT_DIMS, preferred_element_type=F32) + b2
                m = jnp.max(s, axis=1, keepdims=True)
                pr = jnp.exp(s - m)
                den = jnp.sum(pr, axis=1, keepdims=True)
                o = jnp.dot(pr.astype(BF16), v2, preferred_element_type=F32) * (1.0 / den)
                os[p, t, rows, :] = _unstack_heads(o, low)
                lse2 = m + jnp.log(den)
                lse_t = jnp.where(lane == 2 * t, lse2[0:ATT_BLOCK], lse_t)
                lse_t = jnp.where(lane == 2 * t + 1, lse2[ATT_BLOCK:], lse_t)
            ls[p, rows, :] = lse_t

        for p in range(N_PATTERNS):
            d = 4 ** p
            _att_fill_bias(bias, g, d)
            _att_one_pattern(block, p, d, seq_blocks // d)

        def combine(i, carry):
            rows = pl.ds(pl.multiple_of(i * ATT_BLOCK, ATT_BLOCK), ATT_BLOCK)
            l0, l1, l2 = ls[0, rows, :], ls[1, rows, :], ls[2, rows, :]
            m = jnp.maximum(jnp.maximum(l0, l1), l2)
            lse = m + jnp.log(jnp.exp(l0 - m) + jnp.exp(l1 - m) + jnp.exp(l2 - m))
            lse_ref[0, 0, rows, :] = lse
            w = [jnp.exp(l0 - lse), jnp.exp(l1 - lse), jnp.exp(l2 - lse)]
            for t in range(ATT_PAIRS):
                acc = jnp.zeros((ATT_BLOCK, ATT_BLOCK), F32)
                for p in range(N_PATTERNS):
                    wt = jnp.where(low, w[p][:, 2 * t:2 * t + 1], w[p][:, 2 * t + 1:2 * t + 2])
                    acc = acc + wt * os[p, t, rows, :]
                o_ref[0, rows, t * ATT_BLOCK:(t + 1) * ATT_BLOCK] = acc.astype(BF16)
            return carry

        lax.fori_loop(0, S // ATT_BLOCK, combine, 0, unroll=2)

    scratch = ([pltpu.VMEM((ATT_PAIRS, S, ATT_BLOCK), F32)] * 3
               + [pltpu.VMEM((N_PATTERNS, ATT_PAIRS, S, ATT_BLOCK), F32), pltpu.VMEM((N_PATTERNS, S, ATT_BLOCK), F32),
                  pltpu.VMEM((ATT_PAIRS, 2 * ATT_BLOCK, 2 * ATT_BLOCK), F32)])
    rs = riders
    res = _pcall(_with_riders(compute, rs, 3, 2, len(scratch), (B - 1, ATT_GROUPS - 1)), name="attention_fwd",
                 out_shape=(_sds((B, S, ATT_WIDTH), BF16), _sds((B, ATT_GROUPS, S, ATT_BLOCK), F32))
                 + (tuple(rs.out_shape) if rs else ()),
                 grid=(B, ATT_GROUPS), in_specs=[col(0), col(1), col(2)] + (rs.specs if rs else []),
                 out_specs=(o_spec, l_spec) + (tuple(rs.specs) if rs else ()),
                 scratch_shapes=scratch + (rs.scratch if rs else []),
                 dims=("arbitrary", "arbitrary"))(proj3, proj3, proj3, *(rs.arrs if rs else []))
    return res[0], res[1], list(res[2:])


def _att_one_pattern(block, p, d, nb):
    def per_residue(r, carry):
        block(p, d, r, 0, False)
        if nb > 1:
            def per_block(n, c2):
                block(p, d, r, n, True)
                return c2
            lax.fori_loop(1, nb, per_block, 0, unroll=ATT_UNROLL if (nb - 1) % ATT_UNROLL == 0 else nb - 1)
        return carry

    if d == 1:
        per_residue(0, 0)
    else:
        lax.fori_loop(0, d, per_residue, 0, unroll=ATT_RESIDUE_UNROLL if nb == 1 else 1)


def _attention_bwd(proj3, do3, o3, lse4, seq_blocks, riders=None):
    B, S, _ = proj3.shape
    scale = HEAD_DIM ** -0.5
    nq = ATT_WIDTH // ATT_GW

    def col(k):
        return pl.BlockSpec((1, S, ATT_GW), lambda b, g, k=k: (b, 0, k * nq + g))

    o_spec = pl.BlockSpec((1, S, ATT_GW), lambda b, g: (b, 0, g))
    l_spec = pl.BlockSpec((1, 1, S, ATT_BLOCK), lambda b, g: (b, g, 0, 0))

    def compute(q_ref, k_ref, v_ref, do_ref, o_ref, lse_ref, dq_ref, dk_ref, dv_ref,
                qf, kf, vf, dof, dl, aq, ak, av, bias):
        g = pl.program_id(1)
        for t in range(ATT_PAIRS):
            ts = slice(t * ATT_BLOCK, (t + 1) * ATT_BLOCK)
            qf[t] = q_ref[0, :, ts].astype(F32) * scale
            kf[t] = k_ref[0, :, ts].astype(F32)
            vf[t] = v_ref[0, :, ts].astype(F32)
            dof[t] = do_ref[0, :, ts].astype(F32)
        aq[...] = jnp.zeros_like(aq)
        ak[...] = jnp.zeros_like(ak)
        av[...] = jnp.zeros_like(av)
        lane = lax.broadcasted_iota(jnp.int32, (ATT_BLOCK, ATT_BLOCK), 1)
        low = lane < HEAD_DIM

        def fill_delta(i, carry):
            rows = pl.ds(pl.multiple_of(i * ATT_BLOCK, ATT_BLOCK), ATT_BLOCK)
            acc = jnp.zeros((ATT_BLOCK, ATT_BLOCK), F32)
            for t in range(ATT_PAIRS):
                prod = dof[t, rows, :] * o_ref[0, rows, t * ATT_BLOCK:(t + 1) * ATT_BLOCK].astype(F32)
                lo = jnp.sum(jnp.where(low, prod, 0.0), axis=1, keepdims=True)
                hi = jnp.sum(prod, axis=1, keepdims=True) - lo
                acc = jnp.where(lane == 2 * t, lo, acc)
                acc = jnp.where(lane == 2 * t + 1, hi, acc)
            dl[rows, :] = acc
            return carry

        lax.fori_loop(0, S // ATT_BLOCK, fill_delta, 0, unroll=2)

        def block(p, d, r, n, has_prev):
            start = n * (ATT_BLOCK * d) + r
            rows = _att_rows(start, d)
            prows = _att_rows(start - ATT_BLOCK * d, d) if has_prev else None
            lse_t = lse_ref[0, 0, rows, :]
            dl_t = dl[rows, :]
            for t in range(ATT_PAIRS):
                q2 = _stack_heads(qf[t, rows, :], low)
                do2 = _stack_heads(dof[t, rows, :], low)
                k2 = kf[t, rows, :].astype(BF16)
                v2 = vf[t, rows, :].astype(BF16)
                if has_prev:
                    k2 = jnp.concatenate([k2, kf[t, prows, :].astype(BF16)], axis=0)
                    v2 = jnp.concatenate([v2, vf[t, prows, :].astype(BF16)], axis=0)
                    b2 = bias[t]
                else:
                    b2 = bias[t, :, 0:ATT_BLOCK]
                lse2 = jnp.concatenate([lse_t[:, 2 * t:2 * t + 1], lse_t[:, 2 * t + 1:2 * t + 2]], axis=0)
                dl2 = jnp.concatenate([dl_t[:, 2 * t:2 * t + 1], dl_t[:, 2 * t + 1:2 * t + 2]], axis=0)
                s = lax.dot_general(q2, k2, NT_DIMS, preferred_element_type=F32) + b2
                pr = jnp.exp(s - lse2)
                ds = (pr * (lax.dot_general(do2, v2, NT_DIMS, preferred_element_type=F32) - dl2)).astype(BF16)
                dq = _unstack_heads(jnp.dot(ds, k2, preferred_element_type=F32), low)
                dk = lax.dot_general(ds, q2, TN_DIMS, preferred_element_type=F32)
                dv = lax.dot_general(pr.astype(BF16), do2, TN_DIMS, preferred_element_type=F32)
                aq[t, rows, :] = aq[t, rows, :] + dq * scale
                ak[t, rows, :] = ak[t, rows, :] + dk[0:ATT_BLOCK]
                av[t, rows, :] = av[t, rows, :] + dv[0:ATT_BLOCK]
                if has_prev:
                    ak[t, prows, :] = ak[t, prows, :] + dk[ATT_BLOCK:]
                    av[t, prows, :] = av[t, prows, :] + dv[ATT_BLOCK:]

        for p in range(N_PATTERNS):
            d = 4 ** p
            _att_fill_bias(bias, g, d)
            _att_one_pattern(block, p, d, seq_blocks // d)

        for t in range(ATT_PAIRS):
            ts = slice(t * ATT_BLOCK, (t + 1) * ATT_BLOCK)
            dq_ref[0, :, ts] = aq[t].astype(BF16)
            dk_ref[0, :, ts] = ak[t].astype(BF16)
            dv_ref[0, :, ts] = av[t].astype(BF16)

    shp = _sds((B, S, ATT_WIDTH), BF16)
    pair_buf = pltpu.VMEM((ATT_PAIRS, S, ATT_BLOCK), F32)
    scratch = ([pair_buf] * 4 + [pltpu.VMEM((S, ATT_BLOCK), F32)] + [pair_buf] * 3
               + [pltpu.VMEM((ATT_PAIRS, 2 * ATT_BLOCK, 2 * ATT_BLOCK), F32)])
    rs = riders
    res = _pcall(_with_riders(compute, rs, 6, 3, len(scratch), (B - 1, ATT_GROUPS - 1)), name="attention_bwd",
                 out_shape=(shp, shp, shp) + (tuple(rs.out_shape) if rs else ()), grid=(B, ATT_GROUPS),
                 in_specs=[col(0), col(1), col(2), o_spec, o_spec, l_spec] + (rs.specs if rs else []),
                 out_specs=(o_spec, o_spec, o_spec) + (tuple(rs.specs) if rs else ()),
                 scratch_shapes=scratch + (rs.scratch if rs else []),
                 dims=("arbitrary", "arbitrary"))(proj3, proj3, proj3, do3, o3, lse4, *(rs.arrs if rs else []))
    return res[0], res[1], res[2], list(res[3:])


def _expand_groups(m):
    rows = SSM_WIDTH
    t = jnp.concatenate([m] * SSM_GROUPS, axis=0)
    r = lax.broadcasted_iota(jnp.int32, (rows, SSM_LANES), 0)
    l = lax.broadcasted_iota(jnp.int32, (rows, SSM_LANES), 1)
    keep = lax.shift_right_logical(r, 4) == lax.shift_right_logical(l, 6)
    return jnp.where(keep, t, 0.0)


def _collapse_groups(m):
    rows = SSM_WIDTH
    r = lax.broadcasted_iota(jnp.int32, (rows, SSM_LANES), 0)
    l = lax.broadcasted_iota(jnp.int32, (rows, SSM_LANES), 1)
    keep = lax.shift_right_logical(r, 4) == lax.shift_right_logical(l, 6)
    t = jnp.where(keep, m, 0.0)
    acc = t[0:SSM_GROUP_CH]
    for g in range(1, SSM_GROUPS):
        acc = acc + t[g * SSM_GROUP_CH:(g + 1) * SSM_GROUP_CH]
    return acc


def _zoh(lr, li, ldt):
    dt = jnp.exp(ldt)
    mag = jnp.exp(lr * dt)
    ang = li * dt
    cs, sn = jnp.cos(ang), jnp.sin(ang)
    ab_re, ab_im = mag * cs, mag * sn
    nr, ni = ab_re - 1.0, ab_im
    den = lr * lr + li * li
    n_re = nr * lr + ni * li
    n_im = ni * lr - nr * li
    return dict(dt=dt, mag=mag, cs=cs, sn=sn, ab_re=ab_re, ab_im=ab_im, nr=nr, ni=ni, den=den, n_re=n_re, n_im=n_im,
                f_re=n_re / den, f_im=n_im / den)


def _ssm_params(lr, li, ldt, br, bi, cr, ci):
    def body(lr_ref, li_ref, ldt_ref, br_ref, bi_ref, cr_ref, ci_ref, ab_ref, w_ref, c_ref):
        z = _zoh(lr_ref[...], li_ref[...], ldt_ref[...])
        ab_ref[0:1, :] = z["ab_re"]
        ab_ref[1:2, :] = z["ab_im"]
        br, bi = br_ref[...], bi_ref[...]
        w_ref[:, 0:SSM_LANES] = _expand_groups(z["f_re"] * br - z["f_im"] * bi).astype(BF16)
        w_ref[:, SSM_LANES:] = _expand_groups(z["f_re"] * bi + z["f_im"] * br).astype(BF16)
        c_ref[:, 0:SSM_LANES] = _expand_groups(cr_ref[...]).astype(BF16)
        c_ref[:, SSM_LANES:] = _expand_groups(-ci_ref[...]).astype(BF16)

    return _pcall(body, name="ssm_params",
                  out_shape=(_sds((2, SSM_LANES), F32), _sds((SSM_WIDTH, 2 * SSM_LANES), BF16),
                             _sds((SSM_WIDTH, 2 * SSM_LANES), BF16)))(lr, li, ldt, br, bi, cr, ci)


def _ssm_params_bwd(lr, li, ldt, br, bi, dab, dw, dc):
    def body(lr_ref, li_ref, ldt_ref, br_ref, bi_ref, dab_ref, dw_ref, dc_ref,
             dlr_ref, dli_ref, dldt_ref, dbr_ref, dbi_ref, dcr_ref, dci_ref):
        lr, li = lr_ref[...], li_ref[...]
        z = _zoh(lr, li, ldt_ref[...])
        br, bi = br_ref[...], bi_ref[...]
        dbb_re = _collapse_groups(dw_ref[:, 0:SSM_LANES])
        dbb_im = _collapse_groups(dw_ref[:, SSM_LANES:])
        dcr_ref[...] = _collapse_groups(dc_ref[:, 0:SSM_LANES])
        dci_ref[...] = -_collapse_groups(dc_ref[:, SSM_LANES:])
        f_re, f_im = z["f_re"], z["f_im"]
        dbr_ref[...] = f_re * dbb_re + f_im * dbb_im
        dbi_ref[...] = f_re * dbb_im - f_im * dbb_re
        df_re = jnp.sum(dbb_re * br + dbb_im * bi, axis=0, keepdims=True)
        df_im = jnp.sum(dbb_im * br - dbb_re * bi, axis=0, keepdims=True)
        den = z["den"]
        dn_re, dn_im = df_re / den, df_im / den
        dden = -(df_re * z["n_re"] + df_im * z["n_im"]) / (den * den)
        dnr = dn_re * lr - dn_im * li
        dni = dn_re * li + dn_im * lr
        dlr = dn_re * z["nr"] + dn_im * z["ni"] + 2.0 * dden * lr
        dli = dn_re * z["ni"] - dn_im * z["nr"] + 2.0 * dden * li
        dab_re = dab_ref[0:1, :] + dnr
        dab_im = dab_ref[1:2, :] + dni
        mag, cs, sn, dt = z["mag"], z["cs"], z["sn"], z["dt"]
        dmag = dab_re * cs + dab_im * sn
        dang = mag * (dab_im * cs - dab_re * sn)
        dlr_ref[...] = dlr + dmag * mag * dt
        dli_ref[...] = dli + dang * dt
        ddt = dmag * mag * lr + dang * li
        per_lane = jnp.broadcast_to(ddt * dt, (8, SSM_LANES))
        lane = lax.broadcasted_iota(jnp.int32, (SSM_LANES, 128), 0)
        col = lax.broadcasted_iota(jnp.int32, (SSM_LANES, 128), 1)
        ind = jnp.where(lax.shift_right_logical(lane, 6) == col, 1.0, 0.0)
        dldt_ref[...] = jnp.dot(per_lane, ind, preferred_element_type=F32, precision=lax.Precision.HIGHEST)[0:1]

    vec = _sds((1, SSM_LANES), F32)
    mat = _sds((SSM_GROUP_CH, SSM_LANES), F32)
    return _pcall(body, name="ssm_params_bwd", out_shape=(vec, vec, _sds((1, 128), F32), mat, mat, mat, mat))(
        lr, li, ldt, br, bi, dab, dw, dc)


SCAN_CHUNK = 512


def _scan_consts(ar, ai, k_ref, reverse):
    row = lax.broadcasted_iota(jnp.int32, (8, SSM_LANES), 0)
    pw = [(ar, ai)]
    for _ in range(7):
        pr, pi = pw[-1]
        pw.append((pr * ar - pi * ai, pr * ai + pi * ar))
    for n, k in enumerate((1, 2, 4)):
        keep = (row < 8 - k) if reverse else (row >= k)
        k_ref[2 * n] = jnp.where(keep, jnp.broadcast_to(pw[k - 1][0], (8, SSM_LANES)), 0.0)
        k_ref[2 * n + 1] = jnp.where(keep, jnp.broadcast_to(pw[k - 1][1], (8, SSM_LANES)), 0.0)
    cr = jnp.zeros((8, SSM_LANES), F32)
    ci = jnp.zeros((8, SSM_LANES), F32)
    for r in range(8):
        e = (8 - r) if reverse else (r + 1)
        cr = jnp.where(row == r, jnp.broadcast_to(pw[e - 1][0], (8, SSM_LANES)), cr)
        ci = jnp.where(row == r, jnp.broadcast_to(pw[e - 1][1], (8, SSM_LANES)), ci)
    k_ref[6] = cr
    k_ref[7] = ci


def _scan_tile(xr, xi, k_ref, car, cai, reverse):
    for n, k in enumerate((1, 2, 4)):
        sh = (8 - k) if reverse else k
        sr = pltpu.roll(xr, sh, 0)
        si = pltpu.roll(xi, sh, 0)
        mr, mi = k_ref[2 * n], k_ref[2 * n + 1]
        xr, xi = xr + mr * sr - mi * si, xi + mr * si + mi * sr
    pr, pi = k_ref[6], k_ref[7]
    xr, xi = xr + pr * car - pi * cai, xi + pr * cai + pi * car
    return xr, xi


US_BLOCK = (3 * ATT_WIDTH) // SSM_WIDTH


def _ssm_scan_fwd(proj3, abar, w_bu, w_c):
    B, S, _ = proj3.shape
    ch = min(S, SCAN_CHUNK)
    u_spec = pl.BlockSpec((1, ch, SSM_WIDTH), lambda b, c: (b, c, US_BLOCK))
    x_spec = pl.BlockSpec((1, ch, 2 * SSM_LANES), lambda b, c: (b, c, 0))
    y_spec = pl.BlockSpec((1, ch, SSM_WIDTH), lambda b, c: (b, c, 0))
    w_spec = pl.BlockSpec((SSM_WIDTH, 2 * SSM_LANES), lambda b, c: (0, 0))

    def body(ab_ref, u_ref, wb_ref, wc_ref, x_ref, y_ref, k_ref, carry_ref):
        _scan_consts(ab_ref[0:1, :], ab_ref[1:2, :], k_ref, False)

        @pl.when(pl.program_id(1) == 0)
        def _():
            carry_ref[...] = jnp.zeros_like(carry_ref)

        x_ref[0] = jnp.dot(u_ref[0], wb_ref[...], preferred_element_type=F32)

        def step(i, carry):
            base = pl.multiple_of(i * 8, 8)
            xr = x_ref[0, pl.ds(base, 8), 0:SSM_LANES]
            xi = x_ref[0, pl.ds(base, 8), SSM_LANES:]
            xr, xi = _scan_tile(xr, xi, k_ref, carry[0], carry[1], False)
            x_ref[0, pl.ds(base, 8), 0:SSM_LANES] = xr
            x_ref[0, pl.ds(base, 8), SSM_LANES:] = xi
            return (jnp.broadcast_to(xr[7:8], (8, SSM_LANES)), jnp.broadcast_to(xi[7:8], (8, SSM_LANES)))

        cr, ci = lax.fori_loop(0, ch // 8, step, (carry_ref[0], carry_ref[1]))
        carry_ref[0] = cr
        carry_ref[1] = ci
        y_ref[0] = lax.dot_general(x_ref[0].astype(BF16), wc_ref[...], NT_DIMS, preferred_element_type=F32)

    return _pcall(body, name="ssm_scan_fwd",
                  out_shape=(_sds((B, S, 2 * SSM_LANES), F32), _sds((B, S, SSM_WIDTH), F32)), grid=(B, S // ch),
                  in_specs=[pl.BlockSpec((2, SSM_LANES), lambda b, c: (0, 0)), u_spec, w_spec, w_spec],
                  out_specs=(x_spec, y_spec),
                  scratch_shapes=[pltpu.VMEM((8, 8, SSM_LANES), F32), pltpu.VMEM((2, 8, SSM_LANES), F32)],
                  dims=("arbitrary", "arbitrary"))(abar, proj3, w_bu, w_c)


def _ssm_scan_bwd(proj3, dy3, xs3, abar, w_bu, w_c, dsk):
    B, S, _ = proj3.shape
    ch = min(S, SCAN_CHUNK)
    nc = S // ch
    u_spec = pl.BlockSpec((1, ch, SSM_WIDTH), lambda b, c: (b, nc - 1 - c, US_BLOCK))
    x_spec = pl.BlockSpec((1, ch, 2 * SSM_LANES), lambda b, c: (b, nc - 1 - c, 0))
    y_spec = pl.BlockSpec((1, ch, SSM_WIDTH), lambda b, c: (b, nc - 1 - c, 0))
    w_spec = pl.BlockSpec((SSM_WIDTH, 2 * SSM_LANES), lambda b, c: (0, 0))
    ab_spec = pl.BlockSpec((2, SSM_LANES), lambda b, c: (0, 0))
    d_spec = pl.BlockSpec((1, SSM_WIDTH), lambda b, c: (0, 0))

    def body(ab_ref, u_ref, dy_ref, xs_ref, wb_ref, wc_ref, d_ref, du_ref, da_ref, dwb_ref, dwc_ref,
             g_ref, k_ref, carry_ref, acc_ref):
        b, c = pl.program_id(0), pl.program_id(1)
        _scan_consts(ab_ref[0:1, :], -ab_ref[1:2, :], k_ref, True)
        row = lax.broadcasted_iota(jnp.int32, (8, SSM_LANES), 0)

        @pl.when(c == 0)
        def _():
            carry_ref[...] = jnp.zeros_like(carry_ref)

        @pl.when((c == 0) & (b == 0))
        def _():
            acc_ref[...] = jnp.zeros_like(acc_ref)
            dwb_ref[...] = jnp.zeros_like(dwb_ref)
            dwc_ref[...] = jnp.zeros_like(dwc_ref)

        dy = dy_ref[0]
        dyb = dy.astype(BF16)
        g_ref[...] = jnp.dot(dyb, wc_ref[...], preferred_element_type=F32)

        def step(i, carry):
            car, cai, ar_acc, ai_acc = carry
            base = pl.multiple_of((ch // 8 - 1 - i) * 8, 8)
            gr = g_ref[pl.ds(base, 8), 0:SSM_LANES]
            gi = g_ref[pl.ds(base, 8), SSM_LANES:]
            gr, gi = _scan_tile(gr, gi, k_ref, car, cai, True)
            g_ref[pl.ds(base, 8), 0:SSM_LANES] = gr
            g_ref[pl.ds(base, 8), SSM_LANES:] = gi
            nr = jnp.where(row == 7, car, pltpu.roll(gr, 7, 0))
            ni = jnp.where(row == 7, cai, pltpu.roll(gi, 7, 0))
            xr = xs_ref[0, pl.ds(base, 8), 0:SSM_LANES]
            xi = xs_ref[0, pl.ds(base, 8), SSM_LANES:]
            ar_acc = ar_acc + nr * xr + ni * xi
            ai_acc = ai_acc + ni * xr - nr * xi
            return (jnp.broadcast_to(gr[0:1], (8, SSM_LANES)), jnp.broadcast_to(gi[0:1], (8, SSM_LANES)), ar_acc, ai_acc)

        cr, ci, ar_acc, ai_acc = lax.fori_loop(0, ch // 8, step, (carry_ref[0], carry_ref[1], acc_ref[0], acc_ref[1]))
        carry_ref[0] = cr
        carry_ref[1] = ci
        acc_ref[0] = ar_acc
        acc_ref[1] = ai_acc
        da_ref[0:1, :] = jnp.sum(ar_acc, axis=0, keepdims=True)
        da_ref[1:2, :] = jnp.sum(ai_acc, axis=0, keepdims=True)

        gb = g_ref[...].astype(BF16)
        du = lax.dot_general(gb, wb_ref[...], NT_DIMS, preferred_element_type=F32) + d_ref[...] * dy
        du_ref[0] = du.astype(BF16)
        xb = xs_ref[0].astype(BF16)
        u = u_ref[0]
        for j in range(2 * SSM_LANES // SSM_WIDTH):
            rows = slice((j % (SSM_LANES // SSM_WIDTH)) * 64, (j % (SSM_LANES // SSM_WIDTH)) * 64 + 64)
            cols = slice(j * SSM_WIDTH, (j + 1) * SSM_WIDTH)
            dwb_ref[rows, cols] += lax.dot_general(u[:, rows], gb[:, cols], TN_DIMS, preferred_element_type=F32)
            dwc_ref[rows, cols] += lax.dot_general(dyb[:, rows], xb[:, cols], TN_DIMS, preferred_element_type=F32)

    mat = _sds((SSM_WIDTH, 2 * SSM_LANES), F32)
    return _pcall(body, name="ssm_scan_bwd",
                  out_shape=(_sds((B, S, SSM_WIDTH), BF16), _sds((2, SSM_LANES), F32), mat, mat), grid=(B, nc),
                  in_specs=[ab_spec, u_spec, y_spec, x_spec, w_spec, w_spec, d_spec],
                  out_specs=(y_spec, ab_spec, w_spec, w_spec),
                  scratch_shapes=[pltpu.VMEM((ch, 2 * SSM_LANES), F32), pltpu.VMEM((8, 8, SSM_LANES), F32),
                                  pltpu.VMEM((2, 8, SSM_LANES), F32), pltpu.VMEM((2, 8, SSM_LANES), F32)],
                  dims=("arbitrary", "arbitrary"))(abar, proj3, dy3, xs3, w_bu, w_c, dsk)


GELU_K = math.sqrt(2.0 / math.pi)
GELU_C = 0.044715


def _gelu_parts(y):
    t = jnp.tanh(GELU_K * (y + GELU_C * y * y * y))
    return 0.5 * y * (1.0 + t), t


def _ssm_post(yc, us, dsk, wglu, bglu):
    T, N = yc.shape
    tm = min(T, 2048)
    row = pl.BlockSpec((tm, N), lambda i: (i, 0))
    vec = pl.BlockSpec((1, N), lambda i: (0, 0))
    mat = pl.BlockSpec((N, N), lambda i: (0, 0))

    def body(yc_ref, us_ref, d_ref, w_ref, b_ref, y_ref, s_ref):
        y = yc_ref[...] + d_ref[...] * us_ref[...]
        y_ref[...] = y
        z, _ = _gelu_parts(y)
        gl = jnp.dot(z.astype(BF16), w_ref[...], preferred_element_type=F32) + b_ref[...]
        s_ref[...] = (z * _sig(gl)).astype(BF16)

    return _pcall(body, name="ssm_post", out_shape=(_sds((T, N), F32), _sds((T, N), BF16)), grid=(T // tm,),
                  in_specs=[row, row, vec, mat, vec], out_specs=(row, row), dims=("parallel",))(yc, us, dsk, wglu, bglu)


def _ssm_post_bwd(y5, us, ds, dsk, wglu, bglu):
    T, N = y5.shape
    tm = min(T, 2048)
    row = pl.BlockSpec((tm, N), lambda i: (i, 0))
    vec = pl.BlockSpec((1, N), lambda i: (0, 0))
    mat = pl.BlockSpec((N, N), lambda i: (0, 0))

    def body(y_ref, us_ref, ds_ref, d_ref, w_ref, b_ref, dy_ref, dd_ref, db_ref, dw_ref):
        @pl.when(pl.program_id(0) == 0)
        def _():
            dd_ref[...] = jnp.zeros_like(dd_ref)
            db_ref[...] = jnp.zeros_like(db_ref)
            dw_ref[...] = jnp.zeros_like(dw_ref)

        y = y_ref[...]
        z, t = _gelu_parts(y)
        zb = z.astype(BF16)
        gl = jnp.dot(zb, w_ref[...], preferred_element_type=F32) + b_ref[...]
        sg = _sig(gl)
        ds = ds_ref[...]
        dgl = ds * z * sg * (1.0 - sg)
        dglb = dgl.astype(BF16)
        dz = ds * sg + lax.dot_general(dglb, w_ref[...], (((1,), (1,)), ((), ())), preferred_element_type=F32)
        dgelu = 0.5 * (1.0 + t) + 0.5 * y * (1.0 - t * t) * GELU_K * (1.0 + 3.0 * GELU_C * y * y)
        dy = dz * dgelu
        dy_ref[...] = dy
        dd_ref[...] += jnp.sum(dy * us_ref[...], axis=0, keepdims=True)
        db_ref[...] += jnp.sum(dgl, axis=0, keepdims=True)
        dw_ref[...] += lax.dot_general(zb, dglb, (((0,), (0,)), ((), ())), preferred_element_type=F32)

    return _pcall(body, name="ssm_post_bwd",
                  out_shape=(_sds((T, N), F32), _sds((1, N), F32), _sds((1, N), F32), _sds((N, N), F32)),
                  grid=(T // tm,), in_specs=[row, row, row, vec, mat, vec], out_specs=(row, vec, vec, mat),
                  dims=("arbitrary",))(y5, us, ds, dsk, wglu, bglu)


GATE_TILE = 256
GATE_ATT_BLOCK0 = (3 * ATT_WIDTH + SSM_WIDTH) // GATE_TILE
GATE_SSM_BLOCK0 = (3 * ATT_WIDTH + SSM_WIDTH + D_MODEL) // GATE_TILE


def _merge(proj, y_att, y_ssm, b_gate):
    T = proj.shape[0]
    tm = min(T, 4096)
    nj = D_MODEL // GATE_TILE
    ga = pl.BlockSpec((tm, GATE_TILE), lambda i, j: (i, GATE_ATT_BLOCK0 + j))
    gs = pl.BlockSpec((tm, GATE_TILE), lambda i, j: (i, GATE_SSM_BLOCK0 + j))
    yy = pl.BlockSpec((tm, GATE_TILE), lambda i, j: (i, j))
    ba = pl.BlockSpec((1, GATE_TILE), lambda i, j: (0, j))
    bs = pl.BlockSpec((1, GATE_TILE), lambda i, j: (0, nj + j))

    def body(ga_ref, gs_ref, ya_ref, ys_ref, ba_ref, bs_ref, o_ref):
        o_ref[...] = (_sig(ga_ref[...] + ba_ref[...]) * ya_ref[...]
                      + _sig(gs_ref[...] + bs_ref[...]) * ys_ref[...]).astype(BF16)

    return _pcall(body, name="merge", out_shape=_sds((T, D_MODEL), BF16), grid=(T // tm, nj),
                  in_specs=[ga, gs, yy, yy, ba, bs], out_specs=yy, dims=("parallel", "parallel"))(
        proj, proj, y_att, y_ssm, b_gate, b_gate)


def _merge_bwd(proj, y_att, y_ssm, b_gate, dmerged):
    T = proj.shape[0]
    tm = min(T, 2048)
    nj = D_MODEL // GATE_TILE
    ga = pl.BlockSpec((tm, GATE_TILE), lambda j, i: (i, GATE_ATT_BLOCK0 + j))
    gs = pl.BlockSpec((tm, GATE_TILE), lambda j, i: (i, GATE_SSM_BLOCK0 + j))
    yy = pl.BlockSpec((tm, GATE_TILE), lambda j, i: (i, j))
    ba = pl.BlockSpec((1, GATE_TILE), lambda j, i: (0, j))
    bs = pl.BlockSpec((1, GATE_TILE), lambda j, i: (0, nj + j))

    def body(ga_ref, gs_ref, ya_ref, ys_ref, ba_ref, bs_ref, dm_ref, dya_ref, dys_ref, dga_ref, dgs_ref, dba_ref, dbs_ref):
        @pl.when(pl.program_id(1) == 0)
        def _():
            dba_ref[...] = jnp.zeros_like(dba_ref)
            dbs_ref[...] = jnp.zeros_like(dbs_ref)

        dm = dm_ref[...].astype(F32)
        sa = _sig(ga_ref[...] + ba_ref[...])
        ss = _sig(gs_ref[...] + bs_ref[...])
        dya_ref[...] = (dm * sa).astype(BF16)
        dys_ref[...] = (dm * ss).astype(BF16)
        dga = dm * ya_ref[...] * sa * (1.0 - sa)
        dgs = dm * ys_ref[...] * ss * (1.0 - ss)
        dga_ref[...] = dga.astype(BF16)
        dgs_ref[...] = dgs.astype(BF16)
        dba_ref[...] += jnp.sum(dga, axis=0, keepdims=True)
        dbs_ref[...] += jnp.sum(dgs, axis=0, keepdims=True)

    big = _sds((T, D_MODEL), BF16)
    vec = _sds((1, D_MODEL), F32)
    return _pcall(body, name="merge_bwd", out_shape=(big, big, big, big, vec, vec), grid=(nj, T // tm),
                  in_specs=[ga, gs, yy, yy, ba, bs, yy], out_specs=(yy, yy, yy, yy, ba, ba),
                  dims=("arbitrary", "arbitrary"))(proj, proj, y_att, y_ssm, b_gate, b_gate, dmerged)


CONV_TILE = 256


def _shift_rows(a, j, up=False):
    n = a.shape[0]
    r = pltpu.roll(a, n - j if up else j, 0)
    row = lax.broadcasted_iota(jnp.int32, (8, a.shape[1]), 0)
    if up:
        return jnp.concatenate([r[:n - 8], jnp.where(row < 8 - j, r[n - 8:], 0.0)], axis=0)
    return jnp.concatenate([jnp.where(row >= j, r[:8], 0.0), r[8:]], axis=0)


def _conv_pre(a, w_ref, b_ref):
    conv = b_ref[...] + w_ref[0:1, :] * a
    shifted = []
    for j in (1, 2):
        sh = _shift_rows(a, j)
        shifted.append(sh)
        conv = conv + w_ref[j:j + 1, :] * sh
    return conv, shifted


def _conv_act(up3, w_conv, b_conv):
    B, S, _ = up3.shape
    nj = D_FF // CONV_TILE
    a_spec = pl.BlockSpec((1, S, CONV_TILE), lambda b, j: (b, 0, j))
    v_spec = pl.BlockSpec((1, S, CONV_TILE), lambda b, j: (b, 0, nj + j))
    w_spec = pl.BlockSpec((3, CONV_TILE), lambda b, j: (0, j))
    b_spec = pl.BlockSpec((1, CONV_TILE), lambda b, j: (0, j))

    def body(a_ref, v_ref, w_ref, b_ref, o_ref):
        a = a_ref[0].astype(F32)
        conv, _ = _conv_pre(a, w_ref, b_ref)
        o_ref[0] = (conv * _sig(conv) * v_ref[0]).astype(BF16)

    return _pcall(body, name="conv_act", out_shape=_sds((B, S, D_FF), BF16), grid=(B, nj),
                  in_specs=[a_spec, v_spec, w_spec, b_spec], out_specs=a_spec, dims=("parallel", "parallel"))(
        up3, up3, w_conv, b_conv)


def _conv_bwd(up3, dact3, w_conv, b_conv):
    B, S, _ = up3.shape
    nj = D_FF // CONV_TILE
    a_spec = pl.BlockSpec((1, S, CONV_TILE), lambda j, b: (b, 0, j))
    v_spec = pl.BlockSpec((1, S, CONV_TILE), lambda j, b: (b, 0, nj + j))
    o_spec = pl.BlockSpec((2, 1, S, CONV_TILE), lambda j, b: (0, b, 0, j))
    w_spec = pl.BlockSpec((3, CONV_TILE), lambda j, b: (0, j))
    b_spec = pl.BlockSpec((1, CONV_TILE), lambda j, b: (0, j))

    def body(a_ref, v_ref, d_ref, w_ref, b_ref, dup_ref, dw_ref, db_ref):
        @pl.when(pl.program_id(1) == 0)
        def _():
            dw_ref[...] = jnp.zeros_like(dw_ref)
            db_ref[...] = jnp.zeros_like(db_ref)

        a = a_ref[0].astype(F32)
        d = d_ref[0].astype(F32)
        conv, shifted = _conv_pre(a, w_ref, b_ref)
        sg = _sig(conv)
        dup_ref[1, 0] = (d * conv * sg).astype(BF16)
        dconv = d * v_ref[0] * (sg * (1.0 + conv * (1.0 - sg)))
        da = w_ref[0:1, :] * dconv
        for j in (1, 2):
            da = da + w_ref[j:j + 1, :] * _shift_rows(dconv, j, up=True)
        dup_ref[0, 0] = da.astype(BF16)
        db_ref[...] += jnp.sum(dconv, axis=0, keepdims=True)
        dw_ref[0:1, :] += jnp.sum(dconv * a, axis=0, keepdims=True)
        dw_ref[1:2, :] += jnp.sum(dconv * shifted[0], axis=0, keepdims=True)
        dw_ref[2:3, :] += jnp.sum(dconv * shifted[1], axis=0, keepdims=True)

    return _pcall(body, name="conv_bwd",
                  out_shape=(_sds((2, B, S, D_FF), BF16), _sds((3, D_FF), F32), _sds((1, D_FF), F32)),
                  grid=(nj, B), in_specs=[a_spec, v_spec, a_spec, w_spec, b_spec],
                  out_specs=(o_spec, w_spec, b_spec), dims=("arbitrary", "arbitrary"))(up3, up3, dact3, w_conv, b_conv)


def _rows_tile(r, cap=1024):
    for t in range(min(r, cap) - min(r, cap) % 8, 7, -8):
        if r % t == 0:
            return t
    return r


def _add2(a, b, out_dtype, name):
    R, N = a.shape
    tr = _rows_tile(R)
    spec = pl.BlockSpec((tr, N), lambda i: (i, 0))

    def body(a_ref, b_ref, o_ref):
        o_ref[...] = (a_ref[...] + b_ref[...]).astype(out_dtype)

    return _pcall(body, name=name, out_shape=_sds((R, N), out_dtype), grid=(R // tr,), in_specs=[spec, spec],
                  out_specs=spec, dims=("parallel",))(a, b)


def _sum_slots(q, name):
    n, R, N = q.shape
    tr = _rows_tile(R)

    def body(q_ref, o_ref):
        acc = q_ref[0].astype(F32)
        for s in range(1, n):
            acc = acc + q_ref[s].astype(F32)
        o_ref[...] = acc

    return _pcall(body, name=name, out_shape=_sds((R, N), F32), grid=(R // tr,),
                  in_specs=[pl.BlockSpec((n, tr, N), lambda i: (0, i, 0))], out_specs=pl.BlockSpec((tr, N), lambda i: (i, 0)),
                  dims=("parallel",))(q)


NATIVE = (("b_re", 16, 1024), ("b_im", 16, 1024), ("c_re", 16, 1024), ("c_im", 16, 1024), ("g_mix", 1, 1024),
          ("b_att", 1, 1024), ("b_ssm", 1, 1024), ("a_re", 1, 1024), ("a_im", 1, 1024), ("log_dt", 1, 128),
          ("d_skip", 1, 256), ("b_glu", 1, 256), ("g_ffn", 1, 1024), ("g_final", 1, 1024), ("b_conv", 1, 2048),
          ("w_conv", 3, 2048), ("loss", 1, 1))
N_MOD = 6
NATIVE_LATE = ("g_mix",)
MODS_LATE = (0, 1)


def _small_plan(late):
    pieces = [p for p in NATIVE if (p[0] in NATIVE_LATE) == late]
    mods = [k for k in range(N_MOD) if (k in MODS_LATE) == late]
    starts, r = {}, 0
    for name, rows, cols in pieces:
        starts[name] = r
        r += rows * (-(-cols // LANES))
    return pieces, mods, starts, -(-r // 8) * 8


def _pack_small(native, dmods, late):
    pieces, mods, starts, n_sum = _small_plan(late)
    B = dmods[mods[0]].shape[0]
    total = n_sum + 8 * len(mods)

    def body(*refs):
        xs, ms, o_ref = refs[:len(pieces)], refs[len(pieces):-1], refs[-1]
        o_ref[...] = jnp.zeros_like(o_ref)
        for (name, rows, cols), x_ref in zip(pieces, xs):
            chunks = -(-cols // LANES)
            if chunks == 1 and rows % 8 == 0:
                o_ref[starts[name]:starts[name] + rows, 0:cols] = x_ref[...]
                continue
            for i in range(rows):
                for q in range(chunks):
                    wd = min(LANES, cols - q * LANES)
                    r = starts[name] + i * chunks + q
                    o_ref[r:r + 1, 0:wd] = x_ref[i:i + 1, q * LANES:q * LANES + wd]
        for k, m_ref in enumerate(ms):
            for b in range(B):
                o_ref[n_sum + 8 * k + b:n_sum + 8 * k + b + 1, :] = m_ref[b]

    return _pcall(body, name="pack_small_late" if late else "pack_small_early", out_shape=_sds((total, LANES), F32))(
        *[native[n] for n, _, _ in pieces], *[dmods[k] for k in mods])


def _sum_unpack_small(gathered_early, gathered_late, B):
    plans = [_small_plan(False), _small_plan(True)]
    nd = gathered_early.shape[0]
    n_out = len(NATIVE)

    def body(*refs):
        g_refs, outs, dm_ref, accs = refs[0:2], refs[2:2 + n_out], refs[2 + n_out], refs[3 + n_out:]
        o = 0
        for g_ref, acc, (pieces, mods, starts, n_sum) in zip(g_refs, accs, plans):
            s = g_ref[0, 0:n_sum, :]
            for d in range(1, nd):
                s = s + g_ref[d, 0:n_sum, :]
            acc[...] = s
            for name, rows, cols in pieces:
                o_ref = outs[o]
                o += 1
                chunks = -(-cols // LANES)
                if chunks == 1 and rows % 8 == 0:
                    o_ref[...] = acc[starts[name]:starts[name] + rows, 0:cols]
                    continue
                for i in range(rows):
                    for q in range(chunks):
                        wd = min(LANES, cols - q * LANES)
                        r = starts[name] + i * chunks + q
                        o_ref[i:i + 1, q * LANES:q * LANES + wd] = acc[r:r + 1, 0:wd]
            for d in range(nd):
                for j, k in enumerate(mods):
                    dm_ref[d, :, k * D_MODEL:(k + 1) * D_MODEL] = g_ref[d, n_sum + 8 * j:n_sum + 8 * j + B, :]

    ordered = [p for pieces, _, _, _ in plans for p in pieces]
    out_shape = tuple(_sds((rows, cols), F32) for _, rows, cols in ordered) + (_sds((nd, B, N_MOD * D_MODEL), F32),)
    res = _pcall(body, name="sum_unpack_small", out_shape=out_shape,
                 scratch_shapes=[pltpu.VMEM((n_sum, LANES), F32) for _, _, _, n_sum in plans])(gathered_early, gathered_late)
    return {n: r for (n, _, _), r in zip(ordered, res[:-1])}, res[-1]


def _small_from_native(nat):
    lanes3 = lambda a: a.reshape(SSM_GROUP_CH, SSM_GROUPS, SSM_STATE)
    return dict(
        g_mix=nat["g_mix"].reshape(D_MODEL), b_gate=jnp.concatenate([nat["b_att"], nat["b_ssm"]], axis=1).reshape(2 * D_MODEL),
        a_re=nat["a_re"].reshape(SSM_GROUPS, SSM_STATE), a_im=nat["a_im"].reshape(SSM_GROUPS, SSM_STATE),
        log_dt=nat["log_dt"][0, :SSM_GROUPS], b_re=_groups_from_lanes(nat["b_re"]), b_im=_groups_from_lanes(nat["b_im"]),
        c_re=lanes3(nat["c_re"]).transpose(1, 0, 2), c_im=lanes3(nat["c_im"]).transpose(1, 0, 2),
        d_skip=nat["d_skip"].reshape(SSM_WIDTH), b_glu=nat["b_glu"].reshape(SSM_WIDTH), g_ffn=nat["g_ffn"].reshape(D_MODEL),
        w_conv=nat["w_conv"], b_conv=nat["b_conv"].reshape(D_FF), g_final=nat["g_final"].reshape(D_MODEL))


def _adamw_multi(params):
    n = len(params)
    bc1 = 1.0 - ADAM_B1 ** ADAM_STEP
    bc2 = 1.0 - ADAM_B2 ** ADAM_STEP

    def body(*refs):
        ins, outs = refs[:4 * n], refs[4 * n:]
        for i in range(n):
            w_ref, g_ref, m_ref, v_ref = ins[4 * i:4 * i + 4]
            d_ref, nm_ref, nv_ref = outs[3 * i:3 * i + 3]
            g = g_ref[...]
            m = ADAM_B1 * m_ref[...] + (1.0 - ADAM_B1) * g
            v = ADAM_B2 * v_ref[...] + (1.0 - ADAM_B2) * (g * g)
            nm_ref[...] = m
            nv_ref[...] = v
            d_ref[...] = -ADAM_LR * ((m / bc1) / (jnp.sqrt(v / bc2) + ADAM_EPS) + ADAM_WD * w_ref[...])

    flat = [a for p in params for a in p]
    out_shape = tuple(_sds(p[0].shape, F32) for p in params for _ in range(3))
    res = _pcall(body, name="adamw_small", out_shape=out_shape)(*flat)
    return [tuple(res[3 * i:3 * i + 3]) for i in range(n)]


def _adamw(w, g, m, v, name, g_other=None):
    R, N = w.shape
    tr = _rows_tile(R, 256)
    spec = pl.BlockSpec((tr, N), lambda i: (i, 0))
    bc1 = 1.0 - ADAM_B1 ** ADAM_STEP
    bc2 = 1.0 - ADAM_B2 ** ADAM_STEP
    two = g_other is not None

    def body(*refs):
        w_ref, g_ref, m_ref, v_ref = refs[:4]
        d_ref, nm_ref, nv_ref = refs[4 + two:7 + two]
        g = g_ref[...]
        if two:
            g = g + refs[4][...]
            refs[8][...] = g
        m = ADAM_B1 * m_ref[...] + (1.0 - ADAM_B1) * g
        v = ADAM_B2 * v_ref[...] + (1.0 - ADAM_B2) * (g * g)
        nm_ref[...] = m
        nv_ref[...] = v
        d_ref[...] = -ADAM_LR * ((m / bc1) / (jnp.sqrt(v / bc2) + ADAM_EPS) + ADAM_WD * w_ref[...])

    shp = _sds((R, N), F32)
    args = (w, g, m, v) + ((g_other,) if two else ())
    return _pcall(body, name=name, out_shape=(shp,) * (3 + two), grid=(R // tr,), in_specs=[spec] * len(args),
                  out_specs=(spec,) * (3 + two), dims=("parallel",))(*args)


_GROUP_MASKS = {
    "all": [(dx, dy, dc) for dx in (0, 1) for dy in (0, 1) for dc in (0, 1) if (dx, dy, dc) != (0, 0, 0)],
    "xy": [(1, 0, 0), (0, 1, 0), (1, 1, 0)],
    "c": [(0, 0, 1)],
}
_GROUP_SLOTS = {"all": 8, "xy": 4, "c": 2}


def _group_slot(group, x, y, c):
    return {"all": 4 * x + 2 * y + c, "xy": 2 * x + y, "c": c}[group]


def _flip(v, d):
    return 1 - v if d else v


def _exchange(arr, group, mode, name):
    return _exchange_list([arr], group, mode, name)[0]


def _exchange_list(arrs, group, mode, name):
    masks = _GROUP_MASKS[group]
    n = len(masks)
    na = len(arrs)
    assert mode in ("gather", "swap") and (mode == "gather" or group == "c")
    has_local = mode == "gather"
    out_shapes = [((_GROUP_SLOTS[group],) if has_local else ()) + arr.shape for arr in arrs]
    bounce = [pltpu.VMEM(arr.shape, arr.dtype) for arr in arrs] if has_local else []

    def body(*refs):
        x_refs, o_refs = refs[:na], refs[na:2 * na]
        send_sems, recv_sems = refs[2 * na], refs[2 * na + 1]
        x, y, c = lax.axis_index("x"), lax.axis_index("y"), lax.axis_index("c")
        me = _group_slot(group, x, y, c)
        if has_local:
            local_sems = refs[2 * na + 2]
            bufs = refs[2 * na + 3:]
            loads = []
            for i in range(na):
                loads.append(pltpu.make_async_copy(x_refs[i], bufs[i], local_sems.at[2 * i]))
                loads[-1].start()
        copies = []
        for i in range(na):
            x_ref, o_ref = x_refs[i], o_refs[i]
            for k, (dx, dy, dc) in enumerate(masks):
                px, py, pc = _flip(x, dx), _flip(y, dy), _flip(c, dc)
                src, dst = (x_ref, o_ref.at[me]) if has_local else (x_ref, o_ref)
                cp =pltpu.make_async_remote_copy(src_ref=src, dst_ref=dst, send_sem=send_sems.at[i * n + k],
                                                  recv_sem=recv_sems.at[i * n + k], device_id=(px, py, pc),
                                                  device_id_type=pl.DeviceIdType.MESH)
                cp.start()
                copies.append(cp)
        if has_local:
            stores = []
            for i in range(na):
                loads[i].wait()
                stores.append(pltpu.make_async_copy(bufs[i], o_refs[i].at[me], local_sems.at[2 * i + 1]))
                stores[-1].start()
        for cp in copies:
            cp.wait()
        if has_local:
            for st in stores:
                st.wait()

    anyspec = pl.BlockSpec(memory_space=pl.ANY)
    scratch = [pltpu.SemaphoreType.DMA((n * na,)), pltpu.SemaphoreType.DMA((n * na,))]
    if has_local:
        scratch += [pltpu.SemaphoreType.DMA((2 * na,))] + bounce
    outs = pl.pallas_call(body, name=name, out_shape=tuple(_sds(s, a.dtype) for s, a in zip(out_shapes, arrs)),
                          in_specs=[anyspec] * na, out_specs=tuple([anyspec] * na), scratch_shapes=scratch,
                          compiler_params=pltpu.CompilerParams(vmem_limit_bytes=V7X_VMEM_LIMIT_BYTES))(*arrs)
    return list(outs)


def _gather_weights(shards, name):
    na = len(shards)
    masks = _GROUP_MASKS["xy"]
    n = len(masks)

    def body(*refs):
        x_refs, o_refs = refs[:na], refs[na:2 * na]
        send_sems, recv_sems, local_sems = refs[2 * na:2 * na + 3]
        bufs = refs[2 * na + 3:]
        x, y, c = lax.axis_index("x"), lax.axis_index("y"), lax.axis_index("c")
        me = 2 * x + y
        sibling = (x, y, 1 - c)
        loads = []
        for i in range(na):
            loads.append(pltpu.make_async_copy(x_refs[i], bufs[i], local_sems.at[2 * i]))
            loads[-1].start()

        def half_of(i, slot, cc):
            h = shards[i].shape[0] // 2
            return o_refs[i].at[slot, pl.ds(pl.multiple_of(cc * h, 8), h), :]

        def src_half(i, cc):
            h = shards[i].shape[0] // 2
            return x_refs[i].at[pl.ds(pl.multiple_of(cc * h, 8), h), :]

        sends = []
        for i in range(na):
            for k, (dx, dy, _) in enumerate(masks):
                cp = pltpu.make_async_remote_copy(src_ref=src_half(i, c), dst_ref=half_of(i, me, c),
                                                  send_sem=send_sems.at[i * 2 * n + k], recv_sem=recv_sems.at[i * 2 * n + k],
                                                  device_id=(_flip(x, dx), _flip(y, dy), c),
                                                  device_id_type=pl.DeviceIdType.MESH)
                cp.start()
                sends.append(cp)
        stores = []
        for i in range(na):
            loads[i].wait()
            stores.append(pltpu.make_async_copy(bufs[i], o_refs[i].at[me], local_sems.at[2 * i + 1]))
            stores[-1].start()
        for i in range(na):
            for k, (dx, dy, _) in enumerate(masks):
                slot = 2 * _flip(x, dx) + _flip(y, dy)
                landed = pltpu.make_async_remote_copy(src_ref=src_half(i, c), dst_ref=half_of(i, slot, c),
                                                      send_sem=send_sems.at[i * 2 * n + k],
                                                      recv_sem=recv_sems.at[i * 2 * n + k], device_id=sibling,
                                                      device_id_type=pl.DeviceIdType.MESH)
                landed.wait_recv()
                fwd = pltpu.make_async_remote_copy(src_ref=half_of(i, slot, c), dst_ref=half_of(i, slot, c),
                                                   send_sem=send_sems.at[i * 2 * n + n + k],
                                                   recv_sem=recv_sems.at[i * 2 * n + n + k], device_id=sibling,
                                                   device_id_type=pl.DeviceIdType.MESH)
                fwd.start()
                sends.append(fwd)
        for i in range(na):
            for k, (dx, dy, _) in enumerate(masks):
                slot = 2 * _flip(x, dx) + _flip(y, dy)
                pltpu.make_async_remote_copy(src_ref=half_of(i, slot, 1 - c), dst_ref=half_of(i, slot, 1 - c),
                                             send_sem=send_sems.at[i * 2 * n + n + k],
                                             recv_sem=recv_sems.at[i * 2 * n + n + k], device_id=sibling,
                                             device_id_type=pl.DeviceIdType.MESH).wait_recv()
        for cp in sends:
            cp.wait_send()
        for st in stores:
            st.wait()

    anyspec = pl.BlockSpec(memory_space=pl.ANY)
    scratch = [pltpu.SemaphoreType.DMA((2 * n * na,)), pltpu.SemaphoreType.DMA((2 * n * na,)),
               pltpu.SemaphoreType.DMA((2 * na,))] + [pltpu.VMEM(s.shape, s.dtype) for s in shards]
    outs = pl.pallas_call(body, name=name, out_shape=tuple(_sds((N_XY,) + s.shape, s.dtype) for s in shards),
                          in_specs=[anyspec] * na, out_specs=tuple([anyspec] * na), scratch_shapes=scratch,
                          compiler_params=pltpu.CompilerParams(vmem_limit_bytes=V7X_VMEM_LIMIT_BYTES))(*shards)
    return list(outs)


BIG = (("w_proj_att", (ATT_WIDTH, D_MODEL), 1), ("w_proj_ssm", (SSM_WIDTH, D_MODEL), 1),
       ("w_glu", (SSM_WIDTH, SSM_WIDTH), 0))
DIRECT = (("w_in", True), ("w_up", True), ("w_down", False), ("w_out", False))
N_XY = 4


def _big_rows(shape):
    return shape[0] * shape[1] // N_XY // LANES


FLAT_ROWS = sum(_big_rows(s) for _, s, _ in BIG)


def _shard_shape(shape, axis):
    return (shape[0] // N_XY, shape[1]) if axis == 0 else (shape[0], shape[1] // N_XY)


def _flatten_shards(shards):
    return jnp.concatenate([shards[n].reshape(_big_rows(s), LANES) for n, s, _ in BIG], axis=0)


def _unflatten_shard(flat):
    out, r = {}, 0
    for n, s, ax in BIG:
        k = _big_rows(s)
        out[n] = flat[r:r + k].reshape(_shard_shape(s, ax))
        r += k
    return out


def _unflatten_full(flat4):
    out, r = {}, 0
    for n, s, ax in BIG:
        k = _big_rows(s)
        sh = _shard_shape(s, ax)
        t = flat4[:, r:r + k].reshape((N_XY,) + sh)
        out[n] = t.reshape(s) if ax == 0 else t.transpose(1, 0, 2).reshape(s)
        r += k
    return out


def _flatten_full(full):
    parts = []
    for n, s, ax in BIG:
        sh = _shard_shape(s, ax)
        t = full[n]
        t = t.reshape((N_XY,) + sh) if ax == 0 else t.reshape(s[0], N_XY, sh[1]).transpose(1, 0, 2)
        parts.append(t.reshape(N_XY, _big_rows(s), LANES))
    return jnp.concatenate(parts, axis=1)


def _lanes_from_groups(a):
    return a.transpose(2, 0, 1).reshape(SSM_GROUP_CH, SSM_LANES)


def _groups_from_lanes(a):
    return a.reshape(SSM_GROUP_CH, SSM_GROUPS, SSM_STATE).transpose(1, 2, 0)


LATE = ("w_up_t", "w_down", "w_out")
EARLY_GRADS = ("w_up_t", "w_down", "w_out")


def _local_step(x3, mod, tgt3, W, P, late_shards=None, scatter_grads=False):
    B, S, _ = x3.shape
    T = B * S
    seq_blocks = S // ATT_BLOCK
    sh1, sc1, gt1, sh2, sc2, gt2 = [m.reshape(B, 1, D_MODEL) for m in jnp.split(mod, 6, axis=-1)]
    g_mix, g_ffn, g_final = P["g_mix"].reshape(1, D_MODEL), P["g_ffn"].reshape(1, D_MODEL), P["g_final"].reshape(1, D_MODEL)
    b_gate = P["b_gate"].reshape(1, 2 * D_MODEL)
    d_skip, b_glu = P["d_skip"].reshape(1, SSM_WIDTH), P["b_glu"].reshape(1, SSM_WIDTH)
    w_conv, b_conv = P["w_conv"], P["b_conv"].reshape(1, D_FF)

    u1 = _norm_mod(x3, g_mix, sc1, sh1).reshape(T, D_MODEL)
    proj = _mm(u1, W["w_in_t"], tb=True, name="mm_proj", out_dtype=BF16)
    proj3 = proj.reshape(B, S, IN_WIDTH)
    us = proj[:, 3 * ATT_WIDTH:3 * ATT_WIDTH + SSM_WIDTH]
    o_att3, lse4, late = _attention_fwd(proj3, seq_blocks, _Riders(late_shards, "gather") if late_shards else None)
    if late_shards:
        W = dict(W, **{n: f.reshape(-1, LANES) for n, f in zip(LATE, late)})
        w_conv = late[len(LATE)].transpose(1, 0, 2).reshape(3, D_FF)
        W.update(_unflatten_full(late[len(LATE) + 1]))
    o_att = o_att3.reshape(T, ATT_WIDTH)
    y_att = _mm(o_att, W["w_proj_att"], name="mm_proj_att", out_dtype=BF16)

    lr = P["a_re"].reshape(1, SSM_LANES)
    li = P["a_im"].reshape(1, SSM_LANES)
    ldt = jnp.repeat(P["log_dt"], SSM_STATE).reshape(1, SSM_LANES)
    br, bi = _lanes_from_groups(P["b_re"]), _lanes_from_groups(P["b_im"])
    cr = P["c_re"].transpose(1, 0, 2).reshape(SSM_GROUP_CH, SSM_LANES)
    ci = P["c_im"].transpose(1, 0, 2).reshape(SSM_GROUP_CH, SSM_LANES)
    abar, w_bu, w_c = _ssm_params(lr, li, ldt, br, bi, cr, ci)
    xs3, y_core3 = _ssm_scan_fwd(proj3, abar, w_bu, w_c)
    y5, s_out = _ssm_post(y_core3.reshape(T, SSM_WIDTH), us, d_skip, W["w_glu"], b_glu)
    y_ssm = _mm(s_out, W["w_proj_ssm"], name="mm_proj_ssm", out_dtype=BF16)

    merged = _merge(proj, y_att, y_ssm, b_gate)
    mix = _mm(merged, W["w_out"], name="mm_out", out_dtype=BF16)
    mix3 = mix.reshape(B, S, D_MODEL)

    h1, u2 = _resid_norm_mod(x3, mix3, gt1, g_ffn, sc2, sh2)
    u2 = u2.reshape(T, D_MODEL)
    up3 = _mm(u2, W["w_up_t"], tb=True, name="mm_up", out_dtype=BF16).reshape(B, S, 2 * D_FF)
    act = _conv_act(up3, w_conv, b_conv).reshape(T, D_FF)
    ffn3 = _mm(act, W["w_down"], name="mm_down", out_dtype=BF16).reshape(B, S, D_MODEL)
    dh2, dffn, dgt2, dg_final, loss = _final_loss(h1, ffn3, tgt3, gt2, g_final)

    dffn = dffn.reshape(T, D_MODEL)
    gw = {}
    gw["w_down"] = _mm(act, dffn, ta=True, out_dtype=BF16, name="mm_dw_down")
    dact3 = _mm(dffn, W["w_down"], tb=True, name="mm_dact", out_dtype=BF16).reshape(B, S, D_FF)
    dup3, dw_conv, db_conv = _conv_bwd(up3, dact3, w_conv, b_conv)
    dup = dup3.reshape(2, T, D_FF)
    gw["w_up_t"] = _mm(dup, u2, ta=True, out_dtype=BF16, name="mm_dw_up")
    du2 = _mm(dup, W["w_up_t"], name="mm_du2", out_dtype=BF16).reshape(B, S, D_MODEL)
    dh1, dsh2, dsc2, dg_ffn, dgt1, dmix = _norm_bwd(h1, du2, dh2, g_ffn, sc2, "norm_bwd2", mix3=mix3, gt=gt1)

    dmix = dmix.reshape(T, D_MODEL)
    gw["w_out"] = _mm(merged, dmix, ta=True, out_dtype=BF16, name="mm_dw_out")
    dmerged = _mm(dmix, W["w_out"], tb=True, name="mm_dmerged", out_dtype=BF16)
    dy_att, dy_ssm, dga, dgs, db_att, db_ssm = _merge_bwd(proj, y_att, y_ssm, b_gate, dmerged)

    gw["w_proj_ssm"] = _mm(s_out, dy_ssm, ta=True, name="mm_dw_proj_ssm")
    ds_out = _mm(dy_ssm, W["w_proj_ssm"], tb=True, name="mm_ds_out")
    dy5, dd_skip, db_glu, dw_glu = _ssm_post_bwd(y5, us, ds_out, d_skip, W["w_glu"], b_glu)
    gw["w_glu"] = dw_glu
    dus3, dab, dwbu, dwc = _ssm_scan_bwd(proj3, dy5.reshape(B, S, SSM_WIDTH), xs3, abar, w_bu, w_c, d_skip)
    dus = dus3.reshape(T, SSM_WIDTH)
    dlr, dli, dldt, dbr, dbi, dcr, dci = _ssm_params_bwd(lr, li, ldt, br, bi, dab, dwbu, dwc)

    gw["w_proj_att"] = _mm(o_att, dy_att, ta=True, name="mm_dw_proj_att")
    do_att = _mm(dy_att, W["w_proj_att"], tb=True, out_dtype=BF16, name="mm_do_att")
    early = [gw[n].reshape(N_XY, -1, LANES) for n in EARLY_GRADS]
    early.append(_flatten_full({n: gw[n].astype(BF16) for n, _, _ in BIG}))
    dq3, dk3, dv3, parts = _attention_bwd(proj3, do_att.reshape(B, S, ATT_WIDTH), o_att3, lse4, seq_blocks,
                                          _Riders(early, "scatter") if scatter_grads else None)
    dproj = jnp.concatenate([t.reshape(T, ATT_WIDTH) for t in (dq3, dk3, dv3)] + [dus, dga, dgs], axis=1)
    dmods = [None, None, dgt1, dsh2, dsc2, dgt2]
    native = dict(b_att=db_att, b_ssm=db_ssm, a_re=dlr, a_im=dli, log_dt=dldt, b_re=dbr, b_im=dbi, c_re=dcr, c_im=dci,
                  d_skip=dd_skip, b_glu=db_glu, g_ffn=dg_ffn, w_conv=dw_conv, b_conv=db_conv, g_final=dg_final, loss=loss)
    small_early = _pack_small(native, dmods, False)
    sums, sums_sib, last_parts = [], [], []
    if scatter_grads:
        sums = [_sum_slots(p, "sum_chips_%d" % i) for i, p in enumerate(parts)]
        riders = _RiderGroup([_Riders([small_early], "gather", "all"), _Riders(sums, "swap", "c")])
        gw["w_in_t"], rode = _mm(dproj, u1, ta=True, out_dtype=BF16, name="mm_dw_in", riders=riders)
        small_early, sums_sib = rode[0], rode[1:]
        du1, last_parts = _mm(dproj, W["w_in_t"], name="mm_du1", out_dtype=BF16,
                              riders=_Riders([gw["w_in_t"].reshape(N_XY, -1, LANES)], "scatter"))
    else:
        gw["w_in_t"] = _mm(dproj, u1, ta=True, out_dtype=BF16, name="mm_dw_in")
        du1 = _mm(dproj, W["w_in_t"], name="mm_du1", out_dtype=BF16)
    du1 = du1.reshape(B, S, D_MODEL)
    dx, dsh1, dsc1, dg_mix = _norm_bwd(x3, du1, dh1, g_mix, sc1, "norm_bwd1")
    dmods[0], dmods[1] = dsh1, dsc1
    native["g_mix"] = dg_mix
    return loss, dx, dmods, gw, native, (sums, sums_sib, last_parts), small_early


WEIGHTS = ['w_ada', 'b_ada', 'g_mix', 'w_in', 'b_gate', 'a_re', 'a_im', 'log_dt', 'b_re', 'b_im', 'c_re', 'c_im', 'd_skip',
           'w_glu', 'b_glu', 'w_proj_att', 'w_proj_ssm', 'w_out', 'g_ffn', 'w_up', 'w_conv', 'b_conv', 'w_down', 'g_final']
SMALL = ['g_mix', 'b_gate', 'a_re', 'a_im', 'log_dt', 'b_re', 'b_im', 'c_re', 'c_im', 'd_skip', 'b_glu', 'g_ffn', 'w_conv',
         'b_conv', 'g_final']


def kernel(x, c, w_ada, b_ada, g_mix, w_in, b_gate, a_re, a_im, log_dt, b_re, b_im, c_re, c_im, d_skip, w_glu, b_glu, w_proj_att, w_proj_ssm, w_out, g_ffn, w_up, w_conv, b_conv, w_down, g_final, loss_target, m_w_ada, m_b_ada, m_g_mix, m_w_in, m_b_gate, m_a_re, m_a_im, m_log_dt, m_b_re, m_b_im, m_c_re, m_c_im, m_d_skip, m_w_glu, m_b_glu, m_w_proj_att, m_w_proj_ssm, m_w_out, m_g_ffn, m_w_up, m_w_conv, m_b_conv, m_w_down, m_g_final, v_w_ada, v_b_ada, v_g_mix, v_w_in, v_b_gate, v_a_re, v_a_im, v_log_dt, v_b_re, v_b_im, v_c_re, v_c_im, v_d_skip, v_w_glu, v_b_glu, v_w_proj_att, v_w_proj_ssm, v_w_out, v_g_ffn, v_w_up, v_w_conv, v_b_conv, v_w_down, v_g_final):
    args = dict(locals())
    w = {n: args[n] for n in WEIGHTS}
    m = {n: args["m_" + n] for n in WEIGHTS}
    v = {n: args["v_" + n] for n in WEIGHTS}
    B, S, _ = x.shape
    ix, iy, ic = lax.axis_index("x"), lax.axis_index("y"), lax.axis_index("c")
    chip = 2 * ix + iy
    ada_cols = w_ada.shape[2]

    c_all = _exchange(c, "all", "gather", "gather_c").reshape(8 * B, D_MODEL)
    b_cols = lax.dynamic_slice_in_dim(b_ada, chip * ada_cols, ada_cols, axis=1)
    mod_cols = _ada_fwd(c_all, w_ada[0], b_cols)
    mod_all = _exchange(mod_cols, "xy", "gather", "gather_mod")
    mod_all = mod_all.transpose(1, 0, 2).reshape(8 * B, 6 * D_MODEL)
    mod = lax.dynamic_slice_in_dim(mod_all, (4 * ix + 2 * iy + ic) * B, B, axis=0)

    shard = {n + ("_t" if t else ""): (w[n][0].T if t else w[n][0]).astype(BF16) for n, t in DIRECT}
    misc = _flatten_shards({n: w[n][0] for n, _, _ in BIG}).astype(BF16)
    (w_in_full,) = _gather_weights([shard["w_in_t"]], "gather_weights")
    W = {"w_in_t": w_in_full.reshape(-1, LANES)}

    P = {n: w[n][0] for n in SMALL if n not in ("w_conv", "g_final")}
    P["w_conv"] = None
    P["g_final"] = g_final

    loss, dx, dmods, gw, native, parts, small_early = _local_step(x, mod, loss_target, W, P,
                                                                  [shard[n] for n in LATE] + [w_conv[0], misc], True)

    small_late = _exchange(_pack_small(native, dmods, True), "all", "gather", "gather_small")
    native_sum, dmod_all = _sum_unpack_small(small_early, small_late, B)
    loss = native_sum["loss"][0, 0]
    g_small = _small_from_native(native_sum)
    dmod_all = dmod_all.reshape(8 * B, N_MOD * D_MODEL)
    dmod_cols = lax.dynamic_slice_in_dim(dmod_all, chip * ada_cols, ada_cols, axis=1)
    g_w_ada, g_b_ada = _ada_bwd(c_all, dmod_all, dmod_cols)

    red, red_sib, last_parts = parts
    red = red + [_sum_slots(last_parts[0], "sum_chips_w_in")]
    red_sib = red_sib + [_exchange(red[-1], "c", "swap", "share_cores")]
    order = list(EARLY_GRADS) + ["misc", "w_in_t"]
    halves = dict(zip(order, zip(red, red_sib)))

    grads = {"w_ada": g_w_ada[None], "b_ada": g_b_ada}
    grads["w_up"] = _add2(*halves["w_up_t"], F32, "add_cores_w_up").T[None]
    for k, gk in _unflatten_shard(_add2(*halves["misc"], F32, "add_cores_misc")).items():
        grads[k] = gk[None]
    wc_cols = w_conv.shape[2]
    for n in SMALL:
        g = g_small[n]
        if n == "w_conv":
            g = lax.dynamic_slice_in_dim(g, chip * wc_cols, wc_cols, axis=1)
        grads[n] = g.reshape(w[n].shape)

    delta, new_m, new_v = {}, {}, {}
    for n in ["w_ada"] + [b for b, _ in DIRECT] + [b for b, _, _ in BIG]:
        shp = w[n].shape
        if n == "w_in":
            r, s = halves["w_in_t"]
            d2, m2, v2, g2 = _adamw(w[n][0].T, r, m[n][0].T, v[n][0].T, "adamw_" + n, g_other=s)
            d2, m2, v2, grads[n] = d2.T, m2.T, v2.T, g2.T[None]
        elif n in ("w_down", "w_out"):
            r, s = halves[n]
            d2, m2, v2, g2 = _adamw(w[n][0], r, m[n][0], v[n][0], "adamw_" + n, g_other=s)
            grads[n] = g2[None]
        else:
            d2, m2, v2 = _adamw(w[n][0], grads[n][0], m[n][0], v[n][0], "adamw_" + n)
        delta[n], new_m[n], new_v[n] = d2.reshape(shp), m2.reshape(shp), v2.reshape(shp)
    rest = ["b_ada"] + SMALL

    def drop(a):
        return a.reshape(1, -1) if a.ndim == 1 else (a if a.ndim == 2 else a[0])

    upd = _adamw_multi([(drop(w[n]), drop(grads[n]), drop(m[n]), drop(v[n])) for n in rest])
    for n, (dd, mm, vv) in zip(rest, upd):
        delta[n], new_m[n], new_v[n] = dd.reshape(w[n].shape), mm.reshape(w[n].shape), vv.reshape(w[n].shape)

    return (loss, dx, *[grads[n] for n in WEIGHTS], *[delta[n] for n in WEIGHTS], *[new_m[n] for n in WEIGHTS],
            *[new_v[n] for n in WEIGHTS])
```

```python
import functools
import math

import jax
import jax.numpy as jnp
from jax import lax
from jax.experimental import pallas as pl
from jax.experimental.pallas import tpu as pltpu

F32, BF16 = jnp.float32, jnp.bfloat16

D_MODEL = 1024
N_HEADS = 8
HEAD_DIM = 64
ATT_WIDTH = 512
SSM_GROUPS = 16
SSM_GROUP_CH = 16
SSM_WIDTH = 256
SSM_STATE = 64
SSM_LANES = SSM_GROUPS * SSM_STATE
D_FF = 2048
IN_WIDTH = 3 * ATT_WIDTH + SSM_WIDTH + 2 * D_MODEL
ATT_BLOCK = 128
N_PATTERNS = 3
EPS = 1e-6
NEG_INF = -1e30

ADAM_LR, ADAM_B1, ADAM_B2, ADAM_EPS, ADAM_WD, ADAM_STEP = 0.001, 0.9, 0.999, 1e-08, 0.01, 10

V7X_VMEM_LIMIT_BYTES = 56 * 1024 * 1024
LANES = 1024


def _pcall(body, *, name, out_shape, grid=(), in_specs=None, out_specs=None, scratch_shapes=(), dims=None):
    params = dict(vmem_limit_bytes=V7X_VMEM_LIMIT_BYTES)
    if dims is not None:
        params["dimension_semantics"] = dims
    specs = {}
    if in_specs is not None:
        specs = dict(grid=grid, in_specs=in_specs, out_specs=out_specs)
    return pl.pallas_call(body, name=name, out_shape=out_shape, scratch_shapes=scratch_shapes,
                          compiler_params=pltpu.CompilerParams(**params), **specs)


def _sds(shape, dtype):
    return jax.ShapeDtypeStruct(tuple(shape), dtype)


def _tile(n, target):
    if n <= target:
        return n
    for t in range(target - target % 128, 0, -128):
        if n % t == 0:
            return t
    raise ValueError((n, target))


def _sig(v):
    return pl.reciprocal(1.0 + jnp.exp(-v), approx=True)


def _mm(a, b, *, name, ta=False, tb=False, out_dtype=F32, tm=2048, tn=1024, tk=1024, riders=None):
    halves = a.ndim == 3
    if halves:
        a_rows, a_cols = a.shape[1], 2 * a.shape[2]
    else:
        a_rows, a_cols = a.shape
    if ta:
        K, M = a_rows, a_cols
    else:
        M, K = a_rows, a_cols
    if tb:
        N, K2 = b.shape
    else:
        K2, N = b.shape
    assert K == K2, (a.shape, b.shape)
    if not ta:
        tk = 2 * tk
    if halves:
        tm, tk = (min(tm, M // 2), tk) if ta else (tm, min(tk, K // 2))
    tm, tn, tk = _tile(M, tm), _tile(N, tn), _tile(K, tk)
    nk = K // tk
    if halves and ta:
        per = a.shape[2] // tm
        a_spec = pl.BlockSpec((None, tk, tm), lambda i, j, k: (i // per, k, i % per))
    elif halves:
        per = a.shape[2] // tk
        a_spec = pl.BlockSpec((None, tm, tk), lambda i, j, k: (k // per, i, k % per))
    else:
        a_spec = pl.BlockSpec((tk, tm), lambda i, j, k: (k, i)) if ta else pl.BlockSpec((tm, tk), lambda i, j, k: (i, k))
    b_spec = pl.BlockSpec((tn, tk), lambda i, j, k: (j, k)) if tb else pl.BlockSpec((tk, tn), lambda i, j, k: (k, j))
    dn = (((0 if ta else 1,), (1 if tb else 0,)), ((), ()))

    def body(a_ref, b_ref, o_ref, acc_ref):
        k = pl.program_id(2)

        @pl.when(k == 0)
        def _():
            acc_ref[...] = jnp.zeros_like(acc_ref)

        acc_ref[...] += lax.dot_general(a_ref[...].astype(BF16), b_ref[...].astype(BF16), dn,
                                        preferred_element_type=F32)

        @pl.when(k == nk - 1)
        def _():
            o_ref[...] = acc_ref[...].astype(out_dtype)

    def body_single(a_ref, b_ref, o_ref):
        o_ref[...] = lax.dot_general(a_ref[...].astype(BF16), b_ref[...].astype(BF16), dn,
                                     preferred_element_type=F32).astype(out_dtype)

    grid = (M // tm, N // tn, nk)
    scratch = [] if nk == 1 else [pltpu.VMEM((tm, tn), F32)]
    o_spec = pl.BlockSpec((tm, tn), lambda i, j, k: (i, j))
    if riders is None:
        return _pcall(body_single if nk == 1 else body, name=name, out_shape=_sds((M, N), out_dtype), grid=grid,
                      in_specs=[a_spec, b_spec], out_specs=o_spec, scratch_shapes=scratch,
                      dims=("parallel", "parallel", "arbitrary"))(a, b)
    rs = riders
    res = _pcall(_with_riders(body_single if nk == 1 else body, rs, 2, 1, len(scratch), tuple(g - 1 for g in grid)),
                 name=name, out_shape=(_sds((M, N), out_dtype),) + tuple(rs.out_shape), grid=grid,
                 in_specs=[a_spec, b_spec] + rs.specs, out_specs=(o_spec,) + tuple(rs.specs),
                 scratch_shapes=scratch + rs.scratch, dims=("arbitrary", "arbitrary", "arbitrary"))(a, b, *rs.arrs)
    return res[0], list(res[1:])


def _ada_fwd(c_all, w_ada, b_ada_cols):
    n = w_ada.shape[1]

    def body(c_ref, w_ref, b_ref, o_ref):
        c = c_ref[...]
        act = c * _sig(c)
        o_ref[...] = jnp.dot(act.astype(BF16), w_ref[...].astype(BF16), preferred_element_type=F32) + b_ref[...]

    return _pcall(body, name="ada_fwd", out_shape=_sds((c_all.shape[0], n), F32))(c_all, w_ada, b_ada_cols)


def _ada_bwd(c_all, dmod_all, dmod_cols):
    n = dmod_cols.shape[1]

    def body(c_ref, da_ref, dc_ref, gw_ref, gb_ref):
        c = c_ref[...]
        act = c * _sig(c)
        gw_ref[...] = lax.dot_general(act, dc_ref[...], (((0,), (0,)), ((), ())), preferred_element_type=F32,
                                      precision=lax.Precision.HIGHEST)
        gb_ref[...] = jnp.sum(da_ref[...], axis=0, keepdims=True)

    return _pcall(body, name="ada_bwd", out_shape=(_sds((D_MODEL, n), F32), _sds((1, dmod_all.shape[1]), F32)))(
        c_all, dmod_all, dmod_cols)


ROW_TILE = 1024


def _row_specs(B, S):
    ts = min(S, ROW_TILE)
    row = pl.BlockSpec((1, ts, D_MODEL), lambda b, s: (b, s, 0))
    bvec = pl.BlockSpec((1, 1, D_MODEL), lambda b, s: (b, 0, 0))
    gvec = pl.BlockSpec((1, D_MODEL), lambda b, s: (0, 0))
    return ts, row, bvec, gvec


def _norm_mod(x3, g, sc, sh):
    B, S, _ = x3.shape
    ts, row, bvec, gvec = _row_specs(B, S)

    def body(x_ref, g_ref, sc_ref, sh_ref, u_ref):
        x = x_ref[0]
        r = lax.rsqrt(jnp.mean(x * x, axis=-1, keepdims=True) + EPS)
        u_ref[0] = ((x * r) * g_ref[...] * (1.0 + sc_ref[0]) + sh_ref[0]).astype(BF16)

    return _pcall(body, name="norm_mod1", out_shape=_sds(x3.shape, BF16), grid=(B, S // ts),
                  in_specs=[row, gvec, bvec, bvec], out_specs=row, dims=("parallel", "parallel"))(x3, g, sc, sh)


def _resid_norm_mod(x3, mix3, gt, g, sc, sh):
    B, S, _ = x3.shape
    ts, row, bvec, gvec = _row_specs(B, S)

    def body(x_ref, m_ref, gt_ref, g_ref, sc_ref, sh_ref, h_ref, u_ref):
        h = x_ref[0] + gt_ref[0] * m_ref[0]
        h_ref[0] = h
        r = lax.rsqrt(jnp.mean(h * h, axis=-1, keepdims=True) + EPS)
        u_ref[0] = ((h * r) * g_ref[...] * (1.0 + sc_ref[0]) + sh_ref[0]).astype(BF16)

    return _pcall(body, name="resid_norm_mod2", out_shape=(_sds(x3.shape, F32), _sds(x3.shape, BF16)),
                  grid=(B, S // ts), in_specs=[row, row, bvec, gvec, bvec, bvec], out_specs=(row, row),
                  dims=("parallel", "parallel"))(x3, mix3, gt, g, sc, sh)


def _norm_bwd(h3, du3, dres3, g, sc, name, mix3=None, gt=None, riders=None):
    B, S, _ = h3.shape
    ts, row, bvec, gvec = _row_specs(B, S)
    with_gate = mix3 is not None

    def body(*refs):
        if with_gate:
            h_ref, du_ref, dr_ref, g_ref, sc_ref, m_ref, gt_ref, dh_ref, dsh_ref, dsc_ref, dg_ref, dgt_ref, dm_ref = refs
        else:
            h_ref, du_ref, dr_ref, g_ref, sc_ref, dh_ref, dsh_ref, dsc_ref, dg_ref = refs
        b, s = pl.program_id(0), pl.program_id(1)
        h = h_ref[0]
        r = lax.rsqrt(jnp.mean(h * h, axis=-1, keepdims=True) + EPS)
        xn = h * r
        du = du_ref[0].astype(F32)
        g = g_ref[...]
        sc1 = 1.0 + sc_ref[0]
        dxn = du * g * sc1
        dh = dr_ref[0].astype(F32) + r * (dxn - xn * jnp.mean(dxn * xn, axis=-1, keepdims=True))
        dh_ref[0] = dh.astype(dh_ref.dtype)

        @pl.when(s == 0)
        def _():
            dsh_ref[...] = jnp.zeros_like(dsh_ref)
            dsc_ref[...] = jnp.zeros_like(dsc_ref)
            if with_gate:
                dgt_ref[...] = jnp.zeros_like(dgt_ref)

        @pl.when((s == 0) & (b == 0))
        def _():
            dg_ref[...] = jnp.zeros_like(dg_ref)

        dux = du * xn
        dsh_ref[0] += jnp.sum(du, axis=0, keepdims=True)
        dsc_ref[0] += jnp.sum(dux * g, axis=0, keepdims=True)
        dg_ref[...] += jnp.sum(dux * sc1, axis=0, keepdims=True)
        if with_gate:
            dgt_ref[0] += jnp.sum(dh * m_ref[0], axis=0, keepdims=True)
            dm_ref[0] = (dh * gt_ref[0]).astype(BF16)

    bshape = _sds((B, 1, D_MODEL), F32)
    in_specs = [row, row, row, gvec, bvec]
    out_shape = [_sds(h3.shape, BF16 if with_gate else F32), bshape, bshape, _sds((1, D_MODEL), F32)]
    out_specs = [row, bvec, bvec, gvec]
    args = [h3, du3, dres3, g, sc]
    if with_gate:
        in_specs += [row, bvec]
        out_shape += [bshape, _sds(h3.shape, BF16)]
        out_specs += [bvec, row]
        args += [mix3, gt]
    if riders is None:
        return _pcall(body, name=name, out_shape=tuple(out_shape), grid=(B, S // ts), in_specs=in_specs,
                      out_specs=tuple(out_specs), dims=("arbitrary", "arbitrary"))(*args)
    rs = riders
    res = _pcall(_with_riders(body, rs, len(args), len(out_shape), 0, (B - 1, S // ts - 1)), name=name,
                 out_shape=tuple(out_shape) + tuple(rs.out_shape), grid=(B, S // ts), in_specs=in_specs + rs.specs,
                 out_specs=tuple(out_specs) + tuple(rs.specs), scratch_shapes=rs.scratch,
                 dims=("arbitrary", "arbitrary"))(*args, *rs.arrs)
    return tuple(res[:len(out_shape)]) + (list(res[len(out_shape):]),)


def _final_loss(h1, ffn3, tgt3, gt, gfin):
    B, S, _ = h1.shape
    ts, row, bvec, gvec = _row_specs(B, S)
    one = pl.BlockSpec((1, 1), lambda b, s: (0, 0))

    def body(h_ref, f_ref, t_ref, gt_ref, gf_ref, dh_ref, dff_ref, dgt_ref, dgf_ref, loss_ref):
        b, s = pl.program_id(0), pl.program_id(1)
        f = f_ref[0].astype(F32)
        gtv = gt_ref[0]
        gf = gf_ref[...]
        h2 = h_ref[0] + gtv * f
        r = lax.rsqrt(jnp.mean(h2 * h2, axis=-1, keepdims=True) + EPS)
        n = h2 * r
        e = n * gf - t_ref[0]
        dy = e * (1.0 / D_MODEL)
        dn = dy * gf
        dh2 = r * (dn - n * jnp.mean(dn * n, axis=-1, keepdims=True))
        dh_ref[0] = dh2.astype(BF16)
        dff_ref[0] = (dh2 * gtv).astype(BF16)

        @pl.when(s == 0)
        def _():
            dgt_ref[...] = jnp.zeros_like(dgt_ref)

        @pl.when((s == 0) & (b == 0))
        def _():
            dgf_ref[...] = jnp.zeros_like(dgf_ref)
            loss_ref[...] = jnp.zeros_like(loss_ref)

        dgt_ref[0] += jnp.sum(dh2 * f, axis=0, keepdims=True)
        dgf_ref[...] += jnp.sum(dy * n, axis=0, keepdims=True)
        rows = jnp.sum(e * e, axis=1, keepdims=True)
        loss_ref[...] += jnp.sum(rows, axis=0, keepdims=True) * (0.5 / D_MODEL)

    return _pcall(body, name="final_loss",
                  out_shape=(_sds(h1.shape, BF16), _sds(h1.shape, BF16), _sds((B, 1, D_MODEL), F32),
                             _sds((1, D_MODEL), F32), _sds((1, 1), F32)),
                  grid=(B, S // ts), in_specs=[row, row, row, bvec, gvec], out_specs=(row, row, bvec, gvec, one),
                  dims=("arbitrary", "arbitrary"))(h1, ffn3, tgt3, gt, gfin)


ATT_GROUP = 4
ATT_GW = ATT_GROUP * HEAD_DIM
ATT_GROUPS = N_HEADS // ATT_GROUP
ATT_PAIRS = ATT_GW // ATT_BLOCK
ATT_UNROLL = 5
ATT_RESIDUE_UNROLL = 4
NT_DIMS = (((1,), (1,)), ((), ()))
TN_DIMS = (((0,), (0,)), ((), ()))


def _att_rows(start, d):
    if d == 1:
        return pl.ds(start if isinstance(start, int) else pl.multiple_of(start, ATT_BLOCK), ATT_BLOCK)
    return pl.ds(start, ATT_BLOCK, stride=d)


def _att_fill_bias(bias_ref, g, d):
    a = lax.broadcasted_iota(jnp.int32, (ATT_BLOCK, ATT_BLOCK), 0)
    j = lax.broadcasted_iota(jnp.int32, (ATT_BLOCK, ATT_BLOCK), 1)
    dist = (a - j).astype(F32)
    for hh in range(ATT_GROUP):
        t, e = divmod(hh, 2)
        rs = slice(e * ATT_BLOCK, (e + 1) * ATT_BLOCK)
        lo = 2.0 ** (-8.0 * (hh + 1) / N_HEADS) * d
        hi = 2.0 ** (-8.0 * (ATT_GROUP + hh + 1) / N_HEADS) * d
        slope = jnp.where(g == 0, lo, hi).astype(F32)
        bias_ref[t, rs, 0:ATT_BLOCK] = jnp.where(a >= j, -slope * dist, NEG_INF)
        bias_ref[t, rs, ATT_BLOCK:] = jnp.where(j >= a, -slope * (dist + float(ATT_BLOCK)), NEG_INF)


def _stack_heads(v2, low):
    return jnp.concatenate([jnp.where(low, v2, 0.0), jnp.where(low, 0.0, v2)], axis=0).astype(BF16)


def _unstack_heads(r2, low):
    return jnp.where(low, r2[0:ATT_BLOCK], r2[ATT_BLOCK:])


class _Riders:
    def __init__(self, arrs, mode, group="xy"):
        self.arrs, self.mode, self.n, self.group = list(arrs), mode, len(arrs), group
        k = len(_GROUP_MASKS[group])
        self.scratch = [pltpu.SemaphoreType.DMA((k * self.n,)), pltpu.SemaphoreType.DMA((k * self.n,))]
        if mode == "swap":
            assert group == "c"
            self.out_shape = [_sds(a.shape, a.dtype) for a in self.arrs]
        else:
            slot_shapes = [a.shape if mode == "gather" else a.shape[1:] for a in self.arrs]
            self.out_shape = [_sds((_GROUP_SLOTS[group],) + s, a.dtype) for s, a in zip(slot_shapes, self.arrs)]
            self.scratch += [pltpu.SemaphoreType.DMA((2 * self.n,))] + [pltpu.VMEM(s, a.dtype)
                                                                        for s, a in zip(slot_shapes, self.arrs)]
        self.specs = [pl.BlockSpec(memory_space=pl.ANY)] * self.n

    def _remote(self, x_refs, o_refs, send_sems, recv_sems):
        x, y, c = lax.axis_index("x"), lax.axis_index("y"), lax.axis_index("c")
        me = _group_slot(self.group, x, y, c)
        masks = _GROUP_MASKS[self.group]
        cps = []
        for i in range(self.n):
            for k, (dx, dy, dc) in enumerate(masks):
                px, py, pc = _flip(x, dx), _flip(y, dy), _flip(c, dc)
                src = x_refs[i].at[_group_slot(self.group, px, py, pc)] if self.mode == "scatter" else x_refs[i]
                dst = o_refs[i] if self.mode == "swap" else o_refs[i].at[me]
                cps.append(pltpu.make_async_remote_copy(
                    src_ref=src, dst_ref=dst, send_sem=send_sems.at[len(masks) * i + k],
                    recv_sem=recv_sems.at[len(masks) * i + k], device_id=(px, py, pc),
                    device_id_type=pl.DeviceIdType.MESH))
        return cps, me

    def start(self, x_refs, o_refs, scratch):
        cps, me = self._remote(x_refs, o_refs, scratch[0], scratch[1])
        for cp in cps:
            cp.start()
        if self.mode == "swap":
            return
        local_sems, bufs = scratch[2], scratch[3:]
        for i in range(self.n):
            src = x_refs[i] if self.mode == "gather" else x_refs[i].at[me]
            load = pltpu.make_async_copy(src, bufs[i], local_sems.at[2 * i])
            load.start()
            load.wait()
            pltpu.make_async_copy(bufs[i], o_refs[i].at[me], local_sems.at[2 * i + 1]).start()

    def wait(self, x_refs, o_refs, scratch):
        cps, me = self._remote(x_refs, o_refs, scratch[0], scratch[1])
        for cp in cps:
            cp.wait()
        if self.mode == "swap":
            return
        local_sems, bufs = scratch[2], scratch[3:]
        for i in range(self.n):
            pltpu.make_async_copy(bufs[i], o_refs[i].at[me], local_sems.at[2 * i + 1]).wait()


class _RiderGroup:
    def __init__(self, members):
        self.members = list(members)
        self.n = sum(m.n for m in self.members)
        self.arrs = [a for m in self.members for a in m.arrs]
        self.out_shape = [s for m in self.members for s in m.out_shape]
        self.specs = [s for m in self.members for s in m.specs]
        self.scratch = [s for m in self.members for s in m.scratch]

    def _each(self, x_refs, o_refs, scratch):
        i = j = 0
        for m in self.members:
            yield m, x_refs[i:i + m.n], o_refs[i:i + m.n], scratch[j:j + len(m.scratch)]
            i, j = i + m.n, j + len(m.scratch)

    def start(self, x_refs, o_refs, scratch):
        for m, xs, os, sc in self._each(x_refs, o_refs, scratch):
            m.start(xs, os, sc)

    def wait(self, x_refs, o_refs, scratch):
        for m, xs, os, sc in self._each(x_refs, o_refs, scratch):
            m.wait(xs, os, sc)


def _with_riders(compute, riders, n_in, n_out, n_scratch, last_step):
    if riders is None:
        return compute
    n = riders.n

    def body(*refs):
        ins, x_refs = refs[:n_in], refs[n_in:n_in + n]
        outs, o_refs = refs[n_in + n:n_in + n + n_out], refs[n_in + n + n_out:n_in + 2 * n + n_out]
        scratch = refs[n_in + 2 * n + n_out:]
        own, ride = scratch[:n_scratch], scratch[n_scratch:]
        ids = [pl.program_id(i) for i in range(len(last_step))]
        first = functools.reduce(jnp.logical_and, [i == 0 for i in ids])
        last = functools.reduce(jnp.logical_and, [i == l for i, l in zip(ids, last_step)])

        @pl.when(first)
        def _():
            riders.start(x_refs, o_refs, ride)

        compute(*ins, *outs, *own)

        @pl.when(last)
        def _():
            riders.wait(x_refs, o_refs, ride)

    return body


def _attention_fwd(proj3, seq_blocks, riders=None):
    B, S, _ = proj3.shape
    scale = HEAD_DIM ** -0.5
    nq = ATT_WIDTH // ATT_GW

    def col(k):
        return pl.BlockSpec((1, S, ATT_GW), lambda b, g, k=k: (b, 0, k * nq + g))

    o_spec = pl.BlockSpec((1, S, ATT_GW), lambda b, g: (b, 0, g))
    l_spec = pl.BlockSpec((1, 1, S, ATT_BLOCK), lambda b, g: (b, g, 0, 0))

    def compute(q_ref, k_ref, v_ref, o_ref, lse_ref, qf, kf, vf, os, ls, bias):
        g = pl.program_id(1)
        for t in range(ATT_PAIRS):
            ts = slice(t * ATT_BLOCK, (t + 1) * ATT_BLOCK)
            qf[t] = q_ref[0, :, ts].astype(F32) * scale
            kf[t] = k_ref[0, :, ts].astype(F32)
            vf[t] = v_ref[0, :, ts].astype(F32)
        lane = lax.broadcasted_iota(jnp.int32, (ATT_BLOCK, ATT_BLOCK), 1)
        low = lane < HEAD_DIM

        def block(p, d, r, n, has_prev):
            start = n * (ATT_BLOCK * d) + r
            rows = _att_rows(start, d)
            prows = _att_rows(start - ATT_BLOCK * d, d) if has_prev else None
            lse_t = jnp.zeros((ATT_BLOCK, ATT_BLOCK), F32)
            for t in range(ATT_PAIRS):
                q2 = _stack_heads(qf[t, rows, :], low)
                k2 = kf[t, rows, :].astype(BF16)
                v2 = vf[t, rows, :].astype(BF16)
                if has_prev:
                    k2 = jnp.concatenate([k2, kf[t, prows, :].astype(BF16)], axis=0)
                    v2 = jnp.concatenate([v2, vf[t, prows, :].astype(BF16)], axis=0)
                    b2 = bias[t]
                else:
                    b2 = bias[t, :, 0:ATT_BLOCK]
                s = lax.dot_general(q2, k2, NT_DIMS, preferred_element_type=F32) + b2
                m = jnp.max(s, axis=1, keepdims=True)
                pr = jnp.exp(s - m)
                den = jnp.sum(pr, axis=1, keepdims=True)
                o = jnp.dot(pr.astype(BF16), v2, preferred_element_type=F32) * (1.0 / den)
                os[p, t, rows, :] = _unstack_heads(o, low)
                lse2 = m + jnp.log(den)
                lse_t = jnp.where(lane == 2 * t, lse2[0:ATT_BLOCK], lse_t)
                lse_t = jnp.where(lane == 2 * t + 1, lse2[ATT_BLOCK:], lse_t)
            ls[p, rows, :] = lse_t

        for p in range(N_PATTERNS):
            d = 4 ** p
            _att_fill_bias(bias, g, d)
            _att_one_pattern(block, p, d, seq_blocks // d)

        def combine(i, carry):
            rows = pl.ds(pl.multiple_of(i * ATT_BLOCK, ATT_BLOCK), ATT_BLOCK)
            l0, l1, l2 = ls[0, rows, :], ls[1, rows, :], ls[2, rows, :]
            m = jnp.maximum(jnp.maximum(l0, l1), l2)
            lse = m + jnp.log(jnp.exp(l0 - m) + jnp.exp(l1 - m) + jnp.exp(l2 - m))
            lse_ref[0, 0, rows, :] = lse
            w = [jnp.exp(l0 - lse), jnp.exp(l1 - lse), jnp.exp(l2 - lse)]
            for t in range(ATT_PAIRS):
                acc = jnp.zeros((ATT_BLOCK, ATT_BLOCK), F32)
                for p in range(N_PATTERNS):
                    wt = jnp.where(low, w[p][:, 2 * t:2 * t + 1], w[p][:, 2 * t + 1:2 * t + 2])
                    acc = acc + wt * os[p, t, rows, :]
                o_ref[0, rows, t * ATT_BLOCK:(t + 1) * ATT_BLOCK] = acc.astype(BF16)
            return carry

        lax.fori_loop(0, S // ATT_BLOCK, combine, 0, unroll=2)

    scratch = ([pltpu.VMEM((ATT_PAIRS, S, ATT_BLOCK), F32)] * 3
               + [pltpu.VMEM((N_PATTERNS, ATT_PAIRS, S, ATT_BLOCK), F32), pltpu.VMEM((N_PATTERNS, S, ATT_BLOCK), F32),
                  pltpu.VMEM((ATT_PAIRS, 2 * ATT_BLOCK, 2 * ATT_BLOCK), F32)])
    rs = riders
    res = _pcall(_with_riders(compute, rs, 3, 2, len(scratch), (B - 1, ATT_GROUPS - 1)), name="attention_fwd",
                 out_shape=(_sds((B, S, ATT_WIDTH), BF16), _sds((B, ATT_GROUPS, S, ATT_BLOCK), F32))
                 + (tuple(rs.out_shape) if rs else ()),
                 grid=(B, ATT_GROUPS), in_specs=[col(0), col(1), col(2)] + (rs.specs if rs else []),
                 out_specs=(o_spec, l_spec) + (tuple(rs.specs) if rs else ()),
                 scratch_shapes=scratch + (rs.scratch if rs else []),
                 dims=("arbitrary", "arbitrary"))(proj3, proj3, proj3, *(rs.arrs if rs else []))
    return res[0], res[1], list(res[2:])


def _att_one_pattern(block, p, d, nb):
    def per_residue(r, carry):
        block(p, d, r, 0, False)
        if nb > 1:
            def per_block(n, c2):
                block(p, d, r, n, True)
                return c2
            lax.fori_loop(1, nb, per_block, 0, unroll=ATT_UNROLL if (nb - 1) % ATT_UNROLL == 0 else nb - 1)
        return carry

    if d == 1:
        per_residue(0, 0)
    else:
        lax.fori_loop(0, d, per_residue, 0, unroll=ATT_RESIDUE_UNROLL if nb == 1 else 1)


def _attention_bwd(proj3, do3, o3, lse4, seq_blocks, riders=None):
    B, S, _ = proj3.shape
    scale = HEAD_DIM ** -0.5
    nq = ATT_WIDTH // ATT_GW

    def col(k):
        return pl.BlockSpec((1, S, ATT_GW), lambda b, g, k=k: (b, 0, k * nq + g))

    o_spec = pl.BlockSpec((1, S, ATT_GW), lambda b, g: (b, 0, g))
    l_spec = pl.BlockSpec((1, 1, S, ATT_BLOCK), lambda b, g: (b, g, 0, 0))

    def compute(q_ref, k_ref, v_ref, do_ref, o_ref, lse_ref, dq_ref, dk_ref, dv_ref,
                qf, kf, vf, dof, dl, aq, ak, av, bias):
        g = pl.program_id(1)
        for t in range(ATT_PAIRS):
            ts = slice(t * ATT_BLOCK, (t + 1) * ATT_BLOCK)
            qf[t] = q_ref[0, :, ts].astype(F32) * scale
            kf[t] = k_ref[0, :, ts].astype(F32)
            vf[t] = v_ref[0, :, ts].astype(F32)
            dof[t] = do_ref[0, :, ts].astype(F32)
        aq[...] = jnp.zeros_like(aq)
        ak[...] = jnp.zeros_like(ak)
        av[...] = jnp.zeros_like(av)
        lane = lax.broadcasted_iota(jnp.int32, (ATT_BLOCK, ATT_BLOCK), 1)
        low = lane < HEAD_DIM

        def fill_delta(i, carry):
            rows = pl.ds(pl.multiple_of(i * ATT_BLOCK, ATT_BLOCK), ATT_BLOCK)
            acc = jnp.zeros((ATT_BLOCK, ATT_BLOCK), F32)
            for t in range(ATT_PAIRS):
                prod = dof[t, rows, :] * o_ref[0, rows, t * ATT_BLOCK:(t + 1) * ATT_BLOCK].astype(F32)
                lo = jnp.sum(jnp.where(low, prod, 0.0), axis=1, keepdims=True)
                hi = jnp.sum(prod, axis=1, keepdims=True) - lo
                acc = jnp.where(lane == 2 * t, lo, acc)
                acc = jnp.where(lane == 2 * t + 1, hi, acc)
            dl[rows, :] = acc
            return carry

        lax.fori_loop(0, S // ATT_BLOCK, fill_delta, 0, unroll=2)

        def block(p, d, r, n, has_prev):
            start = n * (ATT_BLOCK * d) + r
            rows = _att_rows(start, d)
            prows = _att_rows(start - ATT_BLOCK * d, d) if has_prev else None
            lse_t = lse_ref[0, 0, rows, :]
            dl_t = dl[rows, :]
            for t in range(ATT_PAIRS):
                q2 = _stack_heads(qf[t, rows, :], low)
                do2 = _stack_heads(dof[t, rows, :], low)
                k2 = kf[t, rows, :].astype(BF16)
                v2 = vf[t, rows, :].astype(BF16)
                if has_prev:
                    k2 = jnp.concatenate([k2, kf[t, prows, :].astype(BF16)], axis=0)
                    v2 = jnp.concatenate([v2, vf[t, prows, :].astype(BF16)], axis=0)
                    b2 = bias[t]
                else:
                    b2 = bias[t, :, 0:ATT_BLOCK]
                lse2 = jnp.concatenate([lse_t[:, 2 * t:2 * t + 1], lse_t[:, 2 * t + 1:2 * t + 2]], axis=0)
                dl2 = jnp.concatenate([dl_t[:, 2 * t:2 * t + 1], dl_t[:, 2 * t + 1:2 * t + 2]], axis=0)
                s = lax.dot_general(q2, k2, NT_DIMS, preferred_element_type=F32) + b2
                pr = jnp.exp(s - lse2)
                ds = (pr * (lax.dot_general(do2, v2, NT_DIMS, preferred_element_type=F32) - dl2)).astype(BF16)
                dq = _unstack_heads(jnp.dot(ds, k2, preferred_element_type=F32), low)
                dk = lax.dot_general(ds, q2, TN_DIMS, preferred_element_type=F32)
                dv = lax.dot_general(pr.astype(BF16), do2, TN_DIMS, preferred_element_type=F32)
                aq[t, rows, :] = aq[t, rows, :] + dq * scale
                ak[t, rows, :] = ak[t, rows, :] + dk[0:ATT_BLOCK]
                av[t, rows, :] = av[t, rows, :] + dv[0:ATT_BLOCK]
                if has_prev:
                    ak[t, prows, :] = ak[t, prows, :] + dk[ATT_BLOCK:]
                    av[t, prows, :] = av[t, prows, :] + dv[ATT_BLOCK:]

        for p in range(N_PATTERNS):
            d = 4 ** p
            _att_fill_bias(bias, g, d)
            _att_one_pattern(block, p, d, seq_blocks // d)

        for t in range(ATT_PAIRS):
            ts = slice(t * ATT_BLOCK, (t + 1) * ATT_BLOCK)
            dq_ref[0, :, ts] = aq[t].astype(BF16)
            dk_ref[0, :, ts] = ak[t].astype(BF16)
            dv_ref[0, :, ts] = av[t].astype(BF16)

    shp = _sds((B, S, ATT_WIDTH), BF16)
    pair_buf = pltpu.VMEM((ATT_PAIRS, S, ATT_BLOCK), F32)
    scratch = ([pair_buf] * 4 + [pltpu.VMEM((S, ATT_BLOCK), F32)] + [pair_buf] * 3
               + [pltpu.VMEM((ATT_PAIRS, 2 * ATT_BLOCK, 2 * ATT_BLOCK), F32)])
    rs = riders
    res = _pcall(_with_riders(compute, rs, 6, 3, len(scratch), (B - 1, ATT_GROUPS - 1)), name="attention_bwd",
                 out_shape=(shp, shp, shp) + (tuple(rs.out_shape) if rs else ()), grid=(B, ATT_GROUPS),
                 in_specs=[col(0), col(1), col(2), o_spec, o_spec, l_spec] + (rs.specs if rs else []),
                 out_specs=(o_spec, o_spec, o_spec) + (tuple(rs.specs) if rs else ()),
                 scratch_shapes=scratch + (rs.scratch if rs else []),
                 dims=("arbitrary", "arbitrary"))(proj3, proj3, proj3, do3, o3, lse4, *(rs.arrs if rs else []))
    return res[0], res[1], res[2], list(res[3:])


def _expand_groups(m):
    rows = SSM_WIDTH
    t = jnp.concatenate([m] * SSM_GROUPS, axis=0)
    r = lax.broadcasted_iota(jnp.int32, (rows, SSM_LANES), 0)
    l = lax.broadcasted_iota(jnp.int32, (rows, SSM_LANES), 1)
    keep = lax.shift_right_logical(r, 4) == lax.shift_right_logical(l, 6)
    return jnp.where(keep, t, 0.0)


def _collapse_groups(m):
    rows = SSM_WIDTH
    r = lax.broadcasted_iota(jnp.int32, (rows, SSM_LANES), 0)
    l = lax.broadcasted_iota(jnp.int32, (rows, SSM_LANES), 1)
    keep = lax.shift_right_logical(r, 4) == lax.shift_right_logical(l, 6)
    t = jnp.where(keep, m, 0.0)
    acc = t[0:SSM_GROUP_CH]
    for g in range(1, SSM_GROUPS):
        acc = acc + t[g * SSM_GROUP_CH:(g + 1) * SSM_GROUP_CH]
    return acc


def _zoh(lr, li, ldt):
    dt = jnp.exp(ldt)
    mag = jnp.exp(lr * dt)
    ang = li * dt
    cs, sn = jnp.cos(ang), jnp.sin(ang)
    ab_re, ab_im = mag * cs, mag * sn
    nr, ni = ab_re - 1.0, ab_im
    den = lr * lr + li * li
    n_re = nr * lr + ni * li
    n_im = ni * lr - nr * li
    return dict(dt=dt, mag=mag, cs=cs, sn=sn, ab_re=ab_re, ab_im=ab_im, nr=nr, ni=ni, den=den, n_re=n_re, n_im=n_im,
                f_re=n_re / den, f_im=n_im / den)


def _ssm_params(lr, li, ldt, br, bi, cr, ci):
    def body(lr_ref, li_ref, ldt_ref, br_ref, bi_ref, cr_ref, ci_ref, ab_ref, w_ref, c_ref):
        z = _zoh(lr_ref[...], li_ref[...], ldt_ref[...])
        ab_ref[0:1, :] = z["ab_re"]
        ab_ref[1:2, :] = z["ab_im"]
        br, bi = br_ref[...], bi_ref[...]
        w_ref[:, 0:SSM_LANES] = _expand_groups(z["f_re"] * br - z["f_im"] * bi).astype(BF16)
        w_ref[:, SSM_LANES:] = _expand_groups(z["f_re"] * bi + z["f_im"] * br).astype(BF16)
        c_ref[:, 0:SSM_LANES] = _expand_groups(cr_ref[...]).astype(BF16)
        c_ref[:, SSM_LANES:] = _expand_groups(-ci_ref[...]).astype(BF16)

    return _pcall(body, name="ssm_params",
                  out_shape=(_sds((2, SSM_LANES), F32), _sds((SSM_WIDTH, 2 * SSM_LANES), BF16),
                             _sds((SSM_WIDTH, 2 * SSM_LANES), BF16)))(lr, li, ldt, br, bi, cr, ci)


def _ssm_params_bwd(lr, li, ldt, br, bi, dab, dw, dc):
    def body(lr_ref, li_ref, ldt_ref, br_ref, bi_ref, dab_ref, dw_ref, dc_ref,
             dlr_ref, dli_ref, dldt_ref, dbr_ref, dbi_ref, dcr_ref, dci_ref):
        lr, li = lr_ref[...], li_ref[...]
        z = _zoh(lr, li, ldt_ref[...])
        br, bi = br_ref[...], bi_ref[...]
        dbb_re = _collapse_groups(dw_ref[:, 0:SSM_LANES])
        dbb_im = _collapse_groups(dw_ref[:, SSM_LANES:])
        dcr_ref[...] = _collapse_groups(dc_ref[:, 0:SSM_LANES])
        dci_ref[...] = -_collapse_groups(dc_ref[:, SSM_LANES:])
        f_re, f_im = z["f_re"], z["f_im"]
        dbr_ref[...] = f_re * dbb_re + f_im * dbb_im
        dbi_ref[...] = f_re * dbb_im - f_im * dbb_re
        df_re = jnp.sum(dbb_re * br + dbb_im * bi, axis=0, keepdims=True)
        df_im = jnp.sum(dbb_im * br - dbb_re * bi, axis=0, keepdims=True)
        den = z["den"]
        dn_re, dn_im = df_re / den, df_im / den
        dden = -(df_re * z["n_re"] + df_im * z["n_im"]) / (den * den)
        dnr = dn_re * lr - dn_im * li
        dni = dn_re * li + dn_im * lr
        dlr = dn_re * z["nr"] + dn_im * z["ni"] + 2.0 * dden * lr
        dli = dn_re * z["ni"] - dn_im * z["nr"] + 2.0 * dden * li
        dab_re = dab_ref[0:1, :] + dnr
        dab_im = dab_ref[1:2, :] + dni
        mag, cs, sn, dt = z["mag"], z["cs"], z["sn"], z["dt"]
        dmag = dab_re * cs + dab_im * sn
        dang = mag * (dab_im * cs - dab_re * sn)
        dlr_ref[...] = dlr + dmag * mag * dt
        dli_ref[...] = dli + dang * dt
        ddt = dmag * mag * lr + dang * li
        per_lane = jnp.broadcast_to(ddt * dt, (8, SSM_LANES))
        lane = lax.broadcasted_iota(jnp.int32, (SSM_LANES, 128), 0)
        col = lax.broadcasted_iota(jnp.int32, (SSM_LANES, 128), 1)
        ind = jnp.where(lax.shift_right_logical(lane, 6) == col, 1.0, 0.0)
        dldt_ref[...] = jnp.dot(per_lane, ind, preferred_element_type=F32, precision=lax.Precision.HIGHEST)[0:1]

    vec = _sds((1, SSM_LANES), F32)
    mat = _sds((SSM_GROUP_CH, SSM_LANES), F32)
    return _pcall(body, name="ssm_params_bwd", out_shape=(vec, vec, _sds((1, 128), F32), mat, mat, mat, mat))(
        lr, li, ldt, br, bi, dab, dw, dc)


SCAN_CHUNK = 512


def _scan_consts(ar, ai, k_ref, reverse):
    row = lax.broadcasted_iota(jnp.int32, (8, SSM_LANES), 0)
    pw = [(ar, ai)]
    for _ in range(7):
        pr, pi = pw[-1]
        pw.append((pr * ar - pi * ai, pr * ai + pi * ar))
    for n, k in enumerate((1, 2, 4)):
        keep = (row < 8 - k) if reverse else (row >= k)
        k_ref[2 * n] = jnp.where(keep, jnp.broadcast_to(pw[k - 1][0], (8, SSM_LANES)), 0.0)
        k_ref[2 * n + 1] = jnp.where(keep, jnp.broadcast_to(pw[k - 1][1], (8, SSM_LANES)), 0.0)
    cr = jnp.zeros((8, SSM_LANES), F32)
    ci = jnp.zeros((8, SSM_LANES), F32)
    for r in range(8):
        e = (8 - r) if reverse else (r + 1)
        cr = jnp.where(row == r, jnp.broadcast_to(pw[e - 1][0], (8, SSM_LANES)), cr)
        ci = jnp.where(row == r, jnp.broadcast_to(pw[e - 1][1], (8, SSM_LANES)), ci)
    k_ref[6] = cr
    k_ref[7] = ci


def _scan_tile(xr, xi, k_ref, car, cai, reverse):
    for n, k in enumerate((1, 2, 4)):
        sh = (8 - k) if reverse else k
        sr = pltpu.roll(xr, sh, 0)
        si = pltpu.roll(xi, sh, 0)
        mr, mi = k_ref[2 * n], k_ref[2 * n + 1]
        xr, xi = xr + mr * sr - mi * si, xi + mr * si + mi * sr
    pr, pi = k_ref[6], k_ref[7]
    xr, xi = xr + pr * car - pi * cai, xi + pr * cai + pi * car
    return xr, xi


US_BLOCK = (3 * ATT_WIDTH) // SSM_WIDTH


def _ssm_scan_fwd(proj3, abar, w_bu, w_c):
    B, S, _ = proj3.shape
    ch = min(S, SCAN_CHUNK)
    u_spec = pl.BlockSpec((1, ch, SSM_WIDTH), lambda b, c: (b, c, US_BLOCK))
    x_spec = pl.BlockSpec((1, ch, 2 * SSM_LANES), lambda b, c: (b, c, 0))
    y_spec = pl.BlockSpec((1, ch, SSM_WIDTH), lambda b, c: (b, c, 0))
    w_spec = pl.BlockSpec((SSM_WIDTH, 2 * SSM_LANES), lambda b, c: (0, 0))

    def body(ab_ref, u_ref, wb_ref, wc_ref, x_ref, y_ref, k_ref, carry_ref):
        _scan_consts(ab_ref[0:1, :], ab_ref[1:2, :], k_ref, False)

        @pl.when(pl.program_id(1) == 0)
        def _():
            carry_ref[...] = jnp.zeros_like(carry_ref)

        x_ref[0] = jnp.dot(u_ref[0], wb_ref[...], preferred_element_type=F32)

        def step(i, carry):
            base = pl.multiple_of(i * 8, 8)
            xr = x_ref[0, pl.ds(base, 8), 0:SSM_LANES]
            xi = x_ref[0, pl.ds(base, 8), SSM_LANES:]
            xr, xi = _scan_tile(xr, xi, k_ref, carry[0], carry[1], False)
            x_ref[0, pl.ds(base, 8), 0:SSM_LANES] = xr
            x_ref[0, pl.ds(base, 8), SSM_LANES:] = xi
            return (jnp.broadcast_to(xr[7:8], (8, SSM_LANES)), jnp.broadcast_to(xi[7:8], (8, SSM_LANES)))

        cr, ci = lax.fori_loop(0, ch // 8, step, (carry_ref[0], carry_ref[1]))
        carry_ref[0] = cr
        carry_ref[1] = ci
        y_ref[0] = lax.dot_general(x_ref[0].astype(BF16), wc_ref[...], NT_DIMS, preferred_element_type=F32)

    return _pcall(body, name="ssm_scan_fwd",
                  out_shape=(_sds((B, S, 2 * SSM_LANES), F32), _sds((B, S, SSM_WIDTH), F32)), grid=(B, S // ch),
                  in_specs=[pl.BlockSpec((2, SSM_LANES), lambda b, c: (0, 0)), u_spec, w_spec, w_spec],
                  out_specs=(x_spec, y_spec),
                  scratch_shapes=[pltpu.VMEM((8, 8, SSM_LANES), F32), pltpu.VMEM((2, 8, SSM_LANES), F32)],
                  dims=("arbitrary", "arbitrary"))(abar, proj3, w_bu, w_c)


def _ssm_scan_bwd(proj3, dy3, xs3, abar, w_bu, w_c, dsk):
    B, S, _ = proj3.shape
    ch = min(S, SCAN_CHUNK)
    nc = S // ch
    u_spec = pl.BlockSpec((1, ch, SSM_WIDTH), lambda b, c: (b, nc - 1 - c, US_BLOCK))
    x_spec = pl.BlockSpec((1, ch, 2 * SSM_LANES), lambda b, c: (b, nc - 1 - c, 0))
    y_spec = pl.BlockSpec((1, ch, SSM_WIDTH), lambda b, c: (b, nc - 1 - c, 0))
    w_spec = pl.BlockSpec((SSM_WIDTH, 2 * SSM_LANES), lambda b, c: (0, 0))
    ab_spec = pl.BlockSpec((2, SSM_LANES), lambda b, c: (0, 0))
    d_spec = pl.BlockSpec((1, SSM_WIDTH), lambda b, c: (0, 0))

    def body(ab_ref, u_ref, dy_ref, xs_ref, wb_ref, wc_ref, d_ref, du_ref, da_ref, dwb_ref, dwc_ref,
             g_ref, k_ref, carry_ref, acc_ref):
        b, c = pl.program_id(0), pl.program_id(1)
        _scan_consts(ab_ref[0:1, :], -ab_ref[1:2, :], k_ref, True)
        row = lax.broadcasted_iota(jnp.int32, (8, SSM_LANES), 0)

        @pl.when(c == 0)
        def _():
            carry_ref[...] = jnp.zeros_like(carry_ref)

        @pl.when((c == 0) & (b == 0))
        def _():
            acc_ref[...] = jnp.zeros_like(acc_ref)
            dwb_ref[...] = jnp.zeros_like(dwb_ref)
            dwc_ref[...] = jnp.zeros_like(dwc_ref)

        dy = dy_ref[0]
        dyb = dy.astype(BF16)
        g_ref[...] = jnp.dot(dyb, wc_ref[...], preferred_element_type=F32)

        def step(i, carry):
            car, cai, ar_acc, ai_acc = carry
            base = pl.multiple_of((ch // 8 - 1 - i) * 8, 8)
            gr = g_ref[pl.ds(base, 8), 0:SSM_LANES]
            gi = g_ref[pl.ds(base, 8), SSM_LANES:]
            gr, gi = _scan_tile(gr, gi, k_ref, car, cai, True)
            g_ref[pl.ds(base, 8), 0:SSM_LANES] = gr
            g_ref[pl.ds(base, 8), SSM_LANES:] = gi
            nr = jnp.where(row == 7, car, pltpu.roll(gr, 7, 0))
            ni = jnp.where(row == 7, cai, pltpu.roll(gi, 7, 0))
            xr = xs_ref[0, pl.ds(base, 8), 0:SSM_LANES]
            xi = xs_ref[0, pl.ds(base, 8), SSM_LANES:]
            ar_acc = ar_acc + nr * xr + ni * xi
            ai_acc = ai_acc + ni * xr - nr * xi
            return (jnp.broadcast_to(gr[0:1], (8, SSM_LANES)), jnp.broadcast_to(gi[0:1], (8, SSM_LANES)), ar_acc, ai_acc)

        cr, ci, ar_acc, ai_acc = lax.fori_loop(0, ch // 8, step, (carry_ref[0], carry_ref[1], acc_ref[0], acc_ref[1]))
        carry_ref[0] = cr
        carry_ref[1] = ci
        acc_ref[0] = ar_acc
        acc_ref[1] = ai_acc
        da_ref[0:1, :] = jnp.sum(ar_acc, axis=0, keepdims=True)
        da_ref[1:2, :] = jnp.sum(ai_acc, axis=0, keepdims=True)

        gb = g_ref[...].astype(BF16)
        du = lax.dot_general(gb, wb_ref[...], NT_DIMS, preferred_element_type=F32) + d_ref[...] * dy
        du_ref[0] = du.astype(BF16)
        xb = xs_ref[0].astype(BF16)
        u = u_ref[0]
        for j in range(2 * SSM_LANES // SSM_WIDTH):
            rows = slice((j % (SSM_LANES // SSM_WIDTH)) * 64, (j % (SSM_LANES // SSM_WIDTH)) * 64 + 64)
            cols = slice(j * SSM_WIDTH, (j + 1) * SSM_WIDTH)
            dwb_ref[rows, cols] += lax.dot_general(u[:, rows], gb[:, cols], TN_DIMS, preferred_element_type=F32)
            dwc_ref[rows, cols] += lax.dot_general(dyb[:, rows], xb[:, cols], TN_DIMS, preferred_element_type=F32)

    mat = _sds((SSM_WIDTH, 2 * SSM_LANES), F32)
    return _pcall(body, name="ssm_scan_bwd",
                  out_shape=(_sds((B, S, SSM_WIDTH), BF16), _sds((2, SSM_LANES), F32), mat, mat), grid=(B, nc),
                  in_specs=[ab_spec, u_spec, y_spec, x_spec, w_spec, w_spec, d_spec],
                  out_specs=(y_spec, ab_spec, w_spec, w_spec),
                  scratch_shapes=[pltpu.VMEM((ch, 2 * SSM_LANES), F32), pltpu.VMEM((8, 8, SSM_LANES), F32),
                                  pltpu.VMEM((2, 8, SSM_LANES), F32), pltpu.VMEM((2, 8, SSM_LANES), F32)],
                  dims=("arbitrary", "arbitrary"))(abar, proj3, dy3, xs3, w_bu, w_c, dsk)


GELU_K = math.sqrt(2.0 / math.pi)
GELU_C = 0.044715


def _gelu_parts(y):
    t = jnp.tanh(GELU_K * (y + GELU_C * y * y * y))
    return 0.5 * y * (1.0 + t), t


def _ssm_post(yc, us, dsk, wglu, bglu):
    T, N = yc.shape
    tm = min(T, 2048)
    row = pl.BlockSpec((tm, N), lambda i: (i, 0))
    vec = pl.BlockSpec((1, N), lambda i: (0, 0))
    mat = pl.BlockSpec((N, N), lambda i: (0, 0))

    def body(yc_ref, us_ref, d_ref, w_ref, b_ref, y_ref, s_ref):
        y = yc_ref[...] + d_ref[...] * us_ref[...]
        y_ref[...] = y
        z, _ = _gelu_parts(y)
        gl = jnp.dot(z.astype(BF16), w_ref[...], preferred_element_type=F32) + b_ref[...]
        s_ref[...] = (z * _sig(gl)).astype(BF16)

    return _pcall(body, name="ssm_post", out_shape=(_sds((T, N), F32), _sds((T, N), BF16)), grid=(T // tm,),
                  in_specs=[row, row, vec, mat, vec], out_specs=(row, row), dims=("parallel",))(yc, us, dsk, wglu, bglu)


def _ssm_post_bwd(y5, us, ds, dsk, wglu, bglu):
    T, N = y5.shape
    tm = min(T, 2048)
    row = pl.BlockSpec((tm, N), lambda i: (i, 0))
    vec = pl.BlockSpec((1, N), lambda i: (0, 0))
    mat = pl.BlockSpec((N, N), lambda i: (0, 0))

    def body(y_ref, us_ref, ds_ref, d_ref, w_ref, b_ref, dy_ref, dd_ref, db_ref, dw_ref):
        @pl.when(pl.program_id(0) == 0)
        def _():
            dd_ref[...] = jnp.zeros_like(dd_ref)
            db_ref[...] = jnp.zeros_like(db_ref)
            dw_ref[...] = jnp.zeros_like(dw_ref)

        y = y_ref[...]
        z, t = _gelu_parts(y)
        zb = z.astype(BF16)
        gl = jnp.dot(zb, w_ref[...], preferred_element_type=F32) + b_ref[...]
        sg = _sig(gl)
        ds = ds_ref[...]
        dgl = ds * z * sg * (1.0 - sg)
        dglb = dgl.astype(BF16)
        dz = ds * sg + lax.dot_general(dglb, w_ref[...], (((1,), (1,)), ((), ())), preferred_element_type=F32)
        dgelu = 0.5 * (1.0 + t) + 0.5 * y * (1.0 - t * t) * GELU_K * (1.0 + 3.0 * GELU_C * y * y)
        dy = dz * dgelu
        dy_ref[...] = dy
        dd_ref[...] += jnp.sum(dy * us_ref[...], axis=0, keepdims=True)
        db_ref[...] += jnp.sum(dgl, axis=0, keepdims=True)
        dw_ref[...] += lax.dot_general(zb, dglb, (((0,), (0,)), ((), ())), preferred_element_type=F32)

    return _pcall(body, name="ssm_post_bwd",
                  out_shape=(_sds((T, N), F32), _sds((1, N), F32), _sds((1, N), F32), _sds((N, N), F32)),
                  grid=(T // tm,), in_specs=[row, row, row, vec, mat, vec], out_specs=(row, vec, vec, mat),
                  dims=("arbitrary",))(y5, us, ds, dsk, wglu, bglu)


GATE_TILE = 256
GATE_ATT_BLOCK0 = (3 * ATT_WIDTH + SSM_WIDTH) // GATE_TILE
GATE_SSM_BLOCK0 = (3 * ATT_WIDTH + SSM_WIDTH + D_MODEL) // GATE_TILE


def _merge(proj, y_att, y_ssm, b_gate):
    T = proj.shape[0]
    tm = min(T, 4096)
    nj = D_MODEL // GATE_TILE
    ga = pl.BlockSpec((tm, GATE_TILE), lambda i, j: (i, GATE_ATT_BLOCK0 + j))
    gs = pl.BlockSpec((tm, GATE_TILE), lambda i, j: (i, GATE_SSM_BLOCK0 + j))
    yy = pl.BlockSpec((tm, GATE_TILE), lambda i, j: (i, j))
    ba = pl.BlockSpec((1, GATE_TILE), lambda i, j: (0, j))
    bs = pl.BlockSpec((1, GATE_TILE), lambda i, j: (0, nj + j))

    def body(ga_ref, gs_ref, ya_ref, ys_ref, ba_ref, bs_ref, o_ref):
        o_ref[...] = (_sig(ga_ref[...] + ba_ref[...]) * ya_ref[...]
                      + _sig(gs_ref[...] + bs_ref[...]) * ys_ref[...]).astype(BF16)

    return _pcall(body, name="merge", out_shape=_sds((T, D_MODEL), BF16), grid=(T // tm, nj),
                  in_specs=[ga, gs, yy, yy, ba, bs], out_specs=yy, dims=("parallel", "parallel"))(
        proj, proj, y_att, y_ssm, b_gate, b_gate)


def _merge_bwd(proj, y_att, y_ssm, b_gate, dmerged):
    T = proj.shape[0]
    tm = min(T, 2048)
    nj = D_MODEL // GATE_TILE
    ga = pl.BlockSpec((tm, GATE_TILE), lambda j, i: (i, GATE_ATT_BLOCK0 + j))
    gs = pl.BlockSpec((tm, GATE_TILE), lambda j, i: (i, GATE_SSM_BLOCK0 + j))
    yy = pl.BlockSpec((tm, GATE_TILE), lambda j, i: (i, j))
    ba = pl.BlockSpec((1, GATE_TILE), lambda j, i: (0, j))
    bs = pl.BlockSpec((1, GATE_TILE), lambda j, i: (0, nj + j))

    def body(ga_ref, gs_ref, ya_ref, ys_ref, ba_ref, bs_ref, dm_ref, dya_ref, dys_ref, dga_ref, dgs_ref, dba_ref, dbs_ref):
        @pl.when(pl.program_id(1) == 0)
        def _():
            dba_ref[...] = jnp.zeros_like(dba_ref)
            dbs_ref[...] = jnp.zeros_like(dbs_ref)

        dm = dm_ref[...].astype(F32)
        sa = _sig(ga_ref[...] + ba_ref[...])
        ss = _sig(gs_ref[...] + bs_ref[...])
        dya_ref[...] = (dm * sa).astype(BF16)
        dys_ref[...] = (dm * ss).astype(BF16)
        dga = dm * ya_ref[...] * sa * (1.0 - sa)
        dgs = dm * ys_ref[...] * ss * (1.0 - ss)
        dga_ref[...] = dga.astype(BF16)
        dgs_ref[...] = dgs.astype(BF16)
        dba_ref[...] += jnp.sum(dga, axis=0, keepdims=True)
        dbs_ref[...] += jnp.sum(dgs, axis=0, keepdims=True)

    big = _sds((T, D_MODEL), BF16)
    vec = _sds((1, D_MODEL), F32)
    return _pcall(body, name="merge_bwd", out_shape=(big, big, big, big, vec, vec), grid=(nj, T // tm),
                  in_specs=[ga, gs, yy, yy, ba, bs, yy], out_specs=(yy, yy, yy, yy, ba, ba),
                  dims=("arbitrary", "arbitrary"))(proj, proj, y_att, y_ssm, b_gate, b_gate, dmerged)


CONV_TILE = 256


def _shift_rows(a, j, up=False):
    n = a.shape[0]
    r = pltpu.roll(a, n - j if up else j, 0)
    row = lax.broadcasted_iota(jnp.int32, (8, a.shape[1]), 0)
    if up:
        return jnp.concatenate([r[:n - 8], jnp.where(row < 8 - j, r[n - 8:], 0.0)], axis=0)
    return jnp.concatenate([jnp.where(row >= j, r[:8], 0.0), r[8:]], axis=0)


def _conv_pre(a, w_ref, b_ref):
    conv = b_ref[...] + w_ref[0:1, :] * a
    shifted = []
    for j in (1, 2):
        sh = _shift_rows(a, j)
        shifted.append(sh)
        conv = conv + w_ref[j:j + 1, :] * sh
    return conv, shifted


def _conv_act(up3, w_conv, b_conv):
    B, S, _ = up3.shape
    nj = D_FF // CONV_TILE
    a_spec = pl.BlockSpec((1, S, CONV_TILE), lambda b, j: (b, 0, j))
    v_spec = pl.BlockSpec((1, S, CONV_TILE), lambda b, j: (b, 0, nj + j))
    w_spec = pl.BlockSpec((3, CONV_TILE), lambda b, j: (0, j))
    b_spec = pl.BlockSpec((1, CONV_TILE), lambda b, j: (0, j))

    def body(a_ref, v_ref, w_ref, b_ref, o_ref):
        a = a_ref[0].astype(F32)
        conv, _ = _conv_pre(a, w_ref, b_ref)
        o_ref[0] = (conv * _sig(conv) * v_ref[0]).astype(BF16)

    return _pcall(body, name="conv_act", out_shape=_sds((B, S, D_FF), BF16), grid=(B, nj),
                  in_specs=[a_spec, v_spec, w_spec, b_spec], out_specs=a_spec, dims=("parallel", "parallel"))(
        up3, up3, w_conv, b_conv)


def _conv_bwd(up3, dact3, w_conv, b_conv):
    B, S, _ = up3.shape
    nj = D_FF // CONV_TILE
    a_spec = pl.BlockSpec((1, S, CONV_TILE), lambda j, b: (b, 0, j))
    v_spec = pl.BlockSpec((1, S, CONV_TILE), lambda j, b: (b, 0, nj + j))
    o_spec = pl.BlockSpec((2, 1, S, CONV_TILE), lambda j, b: (0, b, 0, j))
    w_spec = pl.BlockSpec((3, CONV_TILE), lambda j, b: (0, j))
    b_spec = pl.BlockSpec((1, CONV_TILE), lambda j, b: (0, j))

    def body(a_ref, v_ref, d_ref, w_ref, b_ref, dup_ref, dw_ref, db_ref):
        @pl.when(pl.program_id(1) == 0)
        def _():
            dw_ref[...] = jnp.zeros_like(dw_ref)
            db_ref[...] = jnp.zeros_like(db_ref)

        a = a_ref[0].astype(F32)
        d = d_ref[0].astype(F32)
        conv, shifted = _conv_pre(a, w_ref, b_ref)
        sg = _sig(conv)
        dup_ref[1, 0] = (d * conv * sg).astype(BF16)
        dconv = d * v_ref[0] * (sg * (1.0 + conv * (1.0 - sg)))
        da = w_ref[0:1, :] * dconv
        for j in (1, 2):
            da = da + w_ref[j:j + 1, :] * _shift_rows(dconv, j, up=True)
        dup_ref[0, 0] = da.astype(BF16)
        db_ref[...] += jnp.sum(dconv, axis=0, keepdims=True)
        dw_ref[0:1, :] += jnp.sum(dconv * a, axis=0, keepdims=True)
        dw_ref[1:2, :] += jnp.sum(dconv * shifted[0], axis=0, keepdims=True)
        dw_ref[2:3, :] += jnp.sum(dconv * shifted[1], axis=0, keepdims=True)

    return _pcall(body, name="conv_bwd",
                  out_shape=(_sds((2, B, S, D_FF), BF16), _sds((3, D_FF), F32), _sds((1, D_FF), F32)),
                  grid=(nj, B), in_specs=[a_spec, v_spec, a_spec, w_spec, b_spec],
                  out_specs=(o_spec, w_spec, b_spec), dims=("arbitrary", "arbitrary"))(up3, up3, dact3, w_conv, b_conv)


def _rows_tile(r, cap=1024):
    for t in range(min(r, cap) - min(r, cap) % 8, 7, -8):
        if r % t == 0:
            return t
    return r


def _add2(a, b, out_dtype, name):
    R, N = a.shape
    tr = _rows_tile(R)
    spec = pl.BlockSpec((tr, N), lambda i: (i, 0))

    def body(a_ref, b_ref, o_ref):
        o_ref[...] = (a_ref[...] + b_ref[...]).astype(out_dtype)

    return _pcall(body, name=name, out_shape=_sds((R, N), out_dtype), grid=(R // tr,), in_specs=[spec, spec],
                  out_specs=spec, dims=("parallel",))(a, b)


def _sum_slots(q, name):
    n, R, N = q.shape
    tr = _rows_tile(R)

    def body(q_ref, o_ref):
        acc = q_ref[0].astype(F32)
        for s in range(1, n):
            acc = acc + q_ref[s].astype(F32)
        o_ref[...] = acc

    return _pcall(body, name=name, out_shape=_sds((R, N), F32), grid=(R // tr,),
                  in_specs=[pl.BlockSpec((n, tr, N), lambda i: (0, i, 0))], out_specs=pl.BlockSpec((tr, N), lambda i: (i, 0)),
                  dims=("parallel",))(q)


NATIVE = (("b_re", 16, 1024), ("b_im", 16, 1024), ("c_re", 16, 1024), ("c_im", 16, 1024), ("g_mix", 1, 1024),
          ("b_att", 1, 1024), ("b_ssm", 1, 1024), ("a_re", 1, 1024), ("a_im", 1, 1024), ("log_dt", 1, 128),
          ("d_skip", 1, 256), ("b_glu", 1, 256), ("g_ffn", 1, 1024), ("g_final", 1, 1024), ("b_conv", 1, 2048),
          ("w_conv", 3, 2048), ("loss", 1, 1))
N_MOD = 6
NATIVE_LATE = ("g_mix",)
MODS_LATE = (0, 1)


def _small_plan(late):
    pieces = [p for p in NATIVE if (p[0] in NATIVE_LATE) == late]
    mods = [k for k in range(N_MOD) if (k in MODS_LATE) == late]
    starts, r = {}, 0
    for name, rows, cols in pieces:
        starts[name] = r
        r += rows * (-(-cols // LANES))
    return pieces, mods, starts, -(-r // 8) * 8


def _pack_small(native, dmods, late):
    pieces, mods, starts, n_sum = _small_plan(late)
    B = dmods[mods[0]].shape[0]
    total = n_sum + 8 * len(mods)

    def body(*refs):
        xs, ms, o_ref = refs[:len(pieces)], refs[len(pieces):-1], refs[-1]
        o_ref[...] = jnp.zeros_like(o_ref)
        for (name, rows, cols), x_ref in zip(pieces, xs):
            chunks = -(-cols // LANES)
            if chunks == 1 and rows % 8 == 0:
                o_ref[starts[name]:starts[name] + rows, 0:cols] = x_ref[...]
                continue
            for i in range(rows):
                for q in range(chunks):
                    wd = min(LANES, cols - q * LANES)
                    r = starts[name] + i * chunks + q
                    o_ref[r:r + 1, 0:wd] = x_ref[i:i + 1, q * LANES:q * LANES + wd]
        for k, m_ref in enumerate(ms):
            for b in range(B):
                o_ref[n_sum + 8 * k + b:n_sum + 8 * k + b + 1, :] = m_ref[b]

    return _pcall(body, name="pack_small_late" if late else "pack_small_early", out_shape=_sds((total, LANES), F32))(
        *[native[n] for n, _, _ in pieces], *[dmods[k] for k in mods])


def _sum_unpack_small(gathered_early, gathered_late, B):
    plans = [_small_plan(False), _small_plan(True)]
    nd = gathered_early.shape[0]
    n_out = len(NATIVE)

    def body(*refs):
        g_refs, outs, dm_ref, accs = refs[0:2], refs[2:2 + n_out], refs[2 + n_out], refs[3 + n_out:]
        o = 0
        for g_ref, acc, (pieces, mods, starts, n_sum) in zip(g_refs, accs, plans):
            s = g_ref[0, 0:n_sum, :]
            for d in range(1, nd):
                s = s + g_ref[d, 0:n_sum, :]
            acc[...] = s
            for name, rows, cols in pieces:
                o_ref = outs[o]
                o += 1
                chunks = -(-cols // LANES)
                if chunks == 1 and rows % 8 == 0:
                    o_ref[...] = acc[starts[name]:starts[name] + rows, 0:cols]
                    continue
                for i in range(rows):
                    for q in range(chunks):
                        wd = min(LANES, cols - q * LANES)
                        r = starts[name] + i * chunks + q
                        o_ref[i:i + 1, q * LANES:q * LANES + wd] = acc[r:r + 1, 0:wd]
            for d in range(nd):
                for j, k in enumerate(mods):
                    dm_ref[d, :, k * D_MODEL:(k + 1) * D_MODEL] = g_ref[d, n_sum + 8 * j:n_sum + 8 * j + B, :]

    ordered = [p for pieces, _, _, _ in plans for p in pieces]
    out_shape = tuple(_sds((rows, cols), F32) for _, rows, cols in ordered) + (_sds((nd, B, N_MOD * D_MODEL), F32),)
    res = _pcall(body, name="sum_unpack_small", out_shape=out_shape,
                 scratch_shapes=[pltpu.VMEM((n_sum, LANES), F32) for _, _, _, n_sum in plans])(gathered_early, gathered_late)
    return {n: r for (n, _, _), r in zip(ordered, res[:-1])}, res[-1]


def _small_from_native(nat):
    lanes3 = lambda a: a.reshape(SSM_GROUP_CH, SSM_GROUPS, SSM_STATE)
    return dict(
        g_mix=nat["g_mix"].reshape(D_MODEL), b_gate=jnp.concatenate([nat["b_att"], nat["b_ssm"]], axis=1).reshape(2 * D_MODEL),
        a_re=nat["a_re"].reshape(SSM_GROUPS, SSM_STATE), a_im=nat["a_im"].reshape(SSM_GROUPS, SSM_STATE),
        log_dt=nat["log_dt"][0, :SSM_GROUPS], b_re=_groups_from_lanes(nat["b_re"]), b_im=_groups_from_lanes(nat["b_im"]),
        c_re=lanes3(nat["c_re"]).transpose(1, 0, 2), c_im=lanes3(nat["c_im"]).transpose(1, 0, 2),
        d_skip=nat["d_skip"].reshape(SSM_WIDTH), b_glu=nat["b_glu"].reshape(SSM_WIDTH), g_ffn=nat["g_ffn"].reshape(D_MODEL),
        w_conv=nat["w_conv"], b_conv=nat["b_conv"].reshape(D_FF), g_final=nat["g_final"].reshape(D_MODEL))


def _adamw_multi(params):
    n = len(params)
    bc1 = 1.0 - ADAM_B1 ** ADAM_STEP
    bc2 = 1.0 - ADAM_B2 ** ADAM_STEP

    def body(*refs):
        ins, outs = refs[:4 * n], refs[4 * n:]
        for i in range(n):
            w_ref, g_ref, m_ref, v_ref = ins[4 * i:4 * i + 4]
            d_ref, nm_ref, nv_ref = outs[3 * i:3 * i + 3]
            g = g_ref[...]
            m = ADAM_B1 * m_ref[...] + (1.0 - ADAM_B1) * g
            v = ADAM_B2 * v_ref[...] + (1.0 - ADAM_B2) * (g * g)
            nm_ref[...] = m
            nv_ref[...] = v
            d_ref[...] = -ADAM_LR * ((m / bc1) / (jnp.sqrt(v / bc2) + ADAM_EPS) + ADAM_WD * w_ref[...])

    flat = [a for p in params for a in p]
    out_shape = tuple(_sds(p[0].shape, F32) for p in params for _ in range(3))
    res = _pcall(body, name="adamw_small", out_shape=out_shape)(*flat)
    return [tuple(res[3 * i:3 * i + 3]) for i in range(n)]


def _adamw(w, g, m, v, name, g_other=None):
    R, N = w.shape
    tr = _rows_tile(R, 256)
    spec = pl.BlockSpec((tr, N), lambda i: (i, 0))
    bc1 = 1.0 - ADAM_B1 ** ADAM_STEP
    bc2 = 1.0 - ADAM_B2 ** ADAM_STEP
    two = g_other is not None

    def body(*refs):
        w_ref, g_ref, m_ref, v_ref = refs[:4]
        d_ref, nm_ref, nv_ref = refs[4 + two:7 + two]
        g = g_ref[...]
        if two:
            g = g + refs[4][...]
            refs[8][...] = g
        m = ADAM_B1 * m_ref[...] + (1.0 - ADAM_B1) * g
        v = ADAM_B2 * v_ref[...] + (1.0 - ADAM_B2) * (g * g)
        nm_ref[...] = m
        nv_ref[...] = v
        d_ref[...] = -ADAM_LR * ((m / bc1) / (jnp.sqrt(v / bc2) + ADAM_EPS) + ADAM_WD * w_ref[...])

    shp = _sds((R, N), F32)
    args = (w, g, m, v) + ((g_other,) if two else ())
    return _pcall(body, name=name, out_shape=(shp,) * (3 + two), grid=(R // tr,), in_specs=[spec] * len(args),
                  out_specs=(spec,) * (3 + two), dims=("parallel",))(*args)


_GROUP_MASKS = {
    "all": [(dx, dy, dc) for dx in (0, 1) for dy in (0, 1) for dc in (0, 1) if (dx, dy, dc) != (0, 0, 0)],
    "xy": [(1, 0, 0), (0, 1, 0), (1, 1, 0)],
    "c": [(0, 0, 1)],
}
_GROUP_SLOTS = {"all": 8, "xy": 4, "c": 2}


def _group_slot(group, x, y, c):
    return {"all": 4 * x + 2 * y + c, "xy": 2 * x + y, "c": c}[group]


def _flip(v, d):
    return 1 - v if d else v


def _exchange(arr, group, mode, name):
    return _exchange_list([arr], group, mode, name)[0]


def _exchange_list(arrs, group, mode, name):
    masks = _GROUP_MASKS[group]
    n = len(masks)
    na = len(arrs)
    assert mode in ("gather", "swap") and (mode == "gather" or group == "c")
    has_local = mode == "gather"
    out_shapes = [((_GROUP_SLOTS[group],) if has_local else ()) + arr.shape for arr in arrs]
    bounce = [pltpu.VMEM(arr.shape, arr.dtype) for arr in arrs] if has_local else []

    def body(*refs):
        x_refs, o_refs = refs[:na], refs[na:2 * na]
        send_sems, recv_sems = refs[2 * na], refs[2 * na + 1]
        x, y, c = lax.axis_index("x"), lax.axis_index("y"), lax.axis_index("c")
        me = _group_slot(group, x, y, c)
        if has_local:
            local_sems = refs[2 * na + 2]
            bufs = refs[2 * na + 3:]
            loads = []
            for i in range(na):
                loads.append(pltpu.make_async_copy(x_refs[i], bufs[i], local_sems.at[2 * i]))
                loads[-1].start()
        copies = []
        for i in range(na):
            x_ref, o_ref = x_refs[i], o_refs[i]
            for k, (dx, dy, dc) in enumerate(masks):
                px, py, pc = _flip(x, dx), _flip(y, dy), _flip(c, dc)
                src, dst = (x_ref, o_ref.at[me]) if has_local else (x_ref, o_ref)
                cp =pltpu.make_async_remote_copy(src_ref=src, dst_ref=dst, send_sem=send_sems.at[i * n + k],
                                                  recv_sem=recv_sems.at[i * n + k], device_id=(px, py, pc),
                                                  device_id_type=pl.DeviceIdType.MESH)
                cp.start()
                copies.append(cp)
        if has_local:
            stores = []
            for i in range(na):
                loads[i].wait()
                stores.append(pltpu.make_async_copy(bufs[i], o_refs[i].at[me], local_sems.at[2 * i + 1]))
                stores[-1].start()
        for cp in copies:
            cp.wait()
        if has_local:
            for st in stores:
                st.wait()

    anyspec = pl.BlockSpec(memory_space=pl.ANY)
    scratch = [pltpu.SemaphoreType.DMA((n * na,)), pltpu.SemaphoreType.DMA((n * na,))]
    if has_local:
        scratch += [pltpu.SemaphoreType.DMA((2 * na,))] + bounce
    outs = pl.pallas_call(body, name=name, out_shape=tuple(_sds(s, a.dtype) for s, a in zip(out_shapes, arrs)),
                          in_specs=[anyspec] * na, out_specs=tuple([anyspec] * na), scratch_shapes=scratch,
                          compiler_params=pltpu.CompilerParams(vmem_limit_bytes=V7X_VMEM_LIMIT_BYTES))(*arrs)
    return list(outs)


def _gather_weights(shards, name):
    na = len(shards)
    masks = _GROUP_MASKS["xy"]
    n = len(masks)

    def body(*refs):
        x_refs, o_refs = refs[:na], refs[na:2 * na]
        send_sems, recv_sems, local_sems = refs[2 * na:2 * na + 3]
        bufs = refs[2 * na + 3:]
        x, y, c = lax.axis_index("x"), lax.axis_index("y"), lax.axis_index("c")
        me = 2 * x + y
        sibling = (x, y, 1 - c)
        loads = []
        for i in range(na):
            loads.append(pltpu.make_async_copy(x_refs[i], bufs[i], local_sems.at[2 * i]))
            loads[-1].start()

        def half_of(i, slot, cc):
            h = shards[i].shape[0] // 2
            return o_refs[i].at[slot, pl.ds(pl.multiple_of(cc * h, 8), h), :]

        def src_half(i, cc):
            h = shards[i].shape[0] // 2
            return x_refs[i].at[pl.ds(pl.multiple_of(cc * h, 8), h), :]

        sends = []
        for i in range(na):
            for k, (dx, dy, _) in enumerate(masks):
                cp = pltpu.make_async_remote_copy(src_ref=src_half(i, c), dst_ref=half_of(i, me, c),
                                                  send_sem=send_sems.at[i * 2 * n + k], recv_sem=recv_sems.at[i * 2 * n + k],
                                                  device_id=(_flip(x, dx), _flip(y, dy), c),
                                                  device_id_type=pl.DeviceIdType.MESH)
                cp.start()
                sends.append(cp)
        stores = []
        for i in range(na):
            loads[i].wait()
            stores.append(pltpu.make_async_copy(bufs[i], o_refs[i].at[me], local_sems.at[2 * i + 1]))
            stores[-1].start()
        for i in range(na):
            for k, (dx, dy, _) in enumerate(masks):
                slot = 2 * _flip(x, dx) + _flip(y, dy)
                landed = pltpu.make_async_remote_copy(src_ref=src_half(i, c), dst_ref=half_of(i, slot, c),
                                                      send_sem=send_sems.at[i * 2 * n + k],
                                                      recv_sem=recv_sems.at[i * 2 * n + k], device_id=sibling,
                                                      device_id_type=pl.DeviceIdType.MESH)
                landed.wait_recv()
                fwd = pltpu.make_async_remote_copy(src_ref=half_of(i, slot, c), dst_ref=half_of(i, slot, c),
                                                   send_sem=send_sems.at[i * 2 * n + n + k],
                                                   recv_sem=recv_sems.at[i * 2 * n + n + k], device_id=sibling,
                                                   device_id_type=pl.DeviceIdType.MESH)
                fwd.start()
                sends.append(fwd)
        for i in range(na):
            for k, (dx, dy, _) in enumerate(masks):
                slot = 2 * _flip(x, dx) + _flip(y, dy)
                pltpu.make_async_remote_copy(src_ref=half_of(i, slot, 1 - c), dst_ref=half_of(i, slot, 1 - c),
                                             send_sem=send_sems.at[i * 2 * n + n + k],
                                             recv_sem=recv_sems.at[i * 2 * n + n + k], device_id=sibling,
                                             device_id_type=pl.DeviceIdType.MESH).wait_recv()
        for cp in sends:
            cp.wait_send()
        for st in stores:
            st.wait()

    anyspec = pl.BlockSpec(memory_space=pl.ANY)
    scratch = [pltpu.SemaphoreType.DMA((2 * n * na,)), pltpu.SemaphoreType.DMA((2 * n * na,)),
               pltpu.SemaphoreType.DMA((2 * na,))] + [pltpu.VMEM(s.shape, s.dtype) for s in shards]
    outs = pl.pallas_call(body, name=name, out_shape=tuple(_sds((N_XY,) + s.shape, s.dtype) for s in shards),
                          in_specs=[anyspec] * na, out_specs=tuple([anyspec] * na), scratch_shapes=scratch,
                          compiler_params=pltpu.CompilerParams(vmem_limit_bytes=V7X_VMEM_LIMIT_BYTES))(*shards)
    return list(outs)


BIG = (("w_proj_att", (ATT_WIDTH, D_MODEL), 1), ("w_proj_ssm", (SSM_WIDTH, D_MODEL), 1),
       ("w_glu", (SSM_WIDTH, SSM_WIDTH), 0))
DIRECT = (("w_in", True), ("w_up", True), ("w_down", False), ("w_out", False))
N_XY = 4


def _big_rows(shape):
    return shape[0] * shape[1] // N_XY // LANES


FLAT_ROWS = sum(_big_rows(s) for _, s, _ in BIG)


def _shard_shape(shape, axis):
    return (shape[0] // N_XY, shape[1]) if axis == 0 else (shape[0], shape[1] // N_XY)


def _flatten_shards(shards):
    return jnp.concatenate([shards[n].reshape(_big_rows(s), LANES) for n, s, _ in BIG], axis=0)


def _unflatten_shard(flat):
    out, r = {}, 0
    for n, s, ax in BIG:
        k = _big_rows(s)
        out[n] = flat[r:r + k].reshape(_shard_shape(s, ax))
        r += k
    return out


def _unflatten_full(flat4):
    out, r = {}, 0
    for n, s, ax in BIG:
        k = _big_rows(s)
        sh = _shard_shape(s, ax)
        t = flat4[:, r:r + k].reshape((N_XY,) + sh)
        out[n] = t.reshape(s) if ax == 0 else t.transpose(1, 0, 2).reshape(s)
        r += k
    return out


def _flatten_full(full):
    parts = []
    for n, s, ax in BIG:
        sh = _shard_shape(s, ax)
        t = full[n]
        t = t.reshape((N_XY,) + sh) if ax == 0 else t.reshape(s[0], N_XY, sh[1]).transpose(1, 0, 2)
        parts.append(t.reshape(N_XY, _big_rows(s), LANES))
    return jnp.concatenate(parts, axis=1)


def _lanes_from_groups(a):
    return a.transpose(2, 0, 1).reshape(SSM_GROUP_CH, SSM_LANES)


def _groups_from_lanes(a):
    return a.reshape(SSM_GROUP_CH, SSM_GROUPS, SSM_STATE).transpose(1, 2, 0)


LATE = ("w_up_t", "w_down", "w_out")
EARLY_GRADS = ("w_up_t", "w_down", "w_out")


def _local_step(x3, mod, tgt3, W, P, late_shards=None, scatter_grads=False):
    B, S, _ = x3.shape
    T = B * S
    seq_blocks = S // ATT_BLOCK
    sh1, sc1, gt1, sh2, sc2, gt2 = [m.reshape(B, 1, D_MODEL) for m in jnp.split(mod, 6, axis=-1)]
    g_mix, g_ffn, g_final = P["g_mix"].reshape(1, D_MODEL), P["g_ffn"].reshape(1, D_MODEL), P["g_final"].reshape(1, D_MODEL)
    b_gate = P["b_gate"].reshape(1, 2 * D_MODEL)
    d_skip, b_glu = P["d_skip"].reshape(1, SSM_WIDTH), P["b_glu"].reshape(1, SSM_WIDTH)
    w_conv, b_conv = P["w_conv"], P["b_conv"].reshape(1, D_FF)

    u1 = _norm_mod(x3, g_mix, sc1, sh1).reshape(T, D_MODEL)
    proj = _mm(u1, W["w_in_t"], tb=True, name="mm_proj", out_dtype=BF16)
    proj3 = proj.reshape(B, S, IN_WIDTH)
    us = proj[:, 3 * ATT_WIDTH:3 * ATT_WIDTH + SSM_WIDTH]
    o_att3, lse4, late = _attention_fwd(proj3, seq_blocks, _Riders(late_shards, "gather") if late_shards else None)
    if late_shards:
        W = dict(W, **{n: f.reshape(-1, LANES) for n, f in zip(LATE, late)})
        w_conv = late[len(LATE)].transpose(1, 0, 2).reshape(3, D_FF)
        W.update(_unflatten_full(late[len(LATE) + 1]))
    o_att = o_att3.reshape(T, ATT_WIDTH)
    y_att = _mm(o_att, W["w_proj_att"], name="mm_proj_att", out_dtype=BF16)

    lr = P["a_re"].reshape(1, SSM_LANES)
    li = P["a_im"].reshape(1, SSM_LANES)
    ldt = jnp.repeat(P["log_dt"], SSM_STATE).reshape(1, SSM_LANES)
    br, bi = _lanes_from_groups(P["b_re"]), _lanes_from_groups(P["b_im"])
    cr = P["c_re"].transpose(1, 0, 2).reshape(SSM_GROUP_CH, SSM_LANES)
    ci = P["c_im"].transpose(1, 0, 2).reshape(SSM_GROUP_CH, SSM_LANES)
    abar, w_bu, w_c = _ssm_params(lr, li, ldt, br, bi, cr, ci)
    xs3, y_core3 = _ssm_scan_fwd(proj3, abar, w_bu, w_c)
    y5, s_out = _ssm_post(y_core3.reshape(T, SSM_WIDTH), us, d_skip, W["w_glu"], b_glu)
    y_ssm = _mm(s_out, W["w_proj_ssm"], name="mm_proj_ssm", out_dtype=BF16)

    merged = _merge(proj, y_att, y_ssm, b_gate)
    mix = _mm(merged, W["w_out"], name="mm_out", out_dtype=BF16)
    mix3 = mix.reshape(B, S, D_MODEL)

    h1, u2 = _resid_norm_mod(x3, mix3, gt1, g_ffn, sc2, sh2)
    u2 = u2.reshape(T, D_MODEL)
    up3 = _mm(u2, W["w_up_t"], tb=True, name="mm_up", out_dtype=BF16).reshape(B, S, 2 * D_FF)
    act = _conv_act(up3, w_conv, b_conv).reshape(T, D_FF)
    ffn3 = _mm(act, W["w_down"], name="mm_down", out_dtype=BF16).reshape(B, S, D_MODEL)
    dh2, dffn, dgt2, dg_final, loss = _final_loss(h1, ffn3, tgt3, gt2, g_final)

    dffn = dffn.reshape(T, D_MODEL)
    gw = {}
    gw["w_down"] = _mm(act, dffn, ta=True, out_dtype=BF16, name="mm_dw_down")
    dact3 = _mm(dffn, W["w_down"], tb=True, name="mm_dact", out_dtype=BF16).reshape(B, S, D_FF)
    dup3, dw_conv, db_conv = _conv_bwd(up3, dact3, w_conv, b_conv)
    dup = dup3.reshape(2, T, D_FF)
    gw["w_up_t"] = _mm(dup, u2, ta=True, out_dtype=BF16, name="mm_dw_up")
    du2 = _mm(dup, W["w_up_t"], name="mm_du2", out_dtype=BF16).reshape(B, S, D_MODEL)
    dh1, dsh2, dsc2, dg_ffn, dgt1, dmix = _norm_bwd(h1, du2, dh2, g_ffn, sc2, "norm_bwd2", mix3=mix3, gt=gt1)

    dmix = dmix.reshape(T, D_MODEL)
    gw["w_out"] = _mm(merged, dmix, ta=True, out_dtype=BF16, name="mm_dw_out")
    dmerged = _mm(dmix, W["w_out"], tb=True, name="mm_dmerged", out_dtype=BF16)
    dy_att, dy_ssm, dga, dgs, db_att, db_ssm = _merge_bwd(proj, y_att, y_ssm, b_gate, dmerged)

    gw["w_proj_ssm"] = _mm(s_out, dy_ssm, ta=True, name="mm_dw_proj_ssm")
    ds_out = _mm(dy_ssm, W["w_proj_ssm"], tb=True, name="mm_ds_out")
    dy5, dd_skip, db_glu, dw_glu = _ssm_post_bwd(y5, us, ds_out, d_skip, W["w_glu"], b_glu)
    gw["w_glu"] = dw_glu
    dus3, dab, dwbu, dwc = _ssm_scan_bwd(proj3, dy5.reshape(B, S, SSM_WIDTH), xs3, abar, w_bu, w_c, d_skip)
    dus = dus3.reshape(T, SSM_WIDTH)
    dlr, dli, dldt, dbr, dbi, dcr, dci = _ssm_params_bwd(lr, li, ldt, br, bi, dab, dwbu, dwc)

    gw["w_proj_att"] = _mm(o_att, dy_att, ta=True, name="mm_dw_proj_att")
    do_att = _mm(dy_att, W["w_proj_att"], tb=True, out_dtype=BF16, name="mm_do_att")
    early = [gw[n].reshape(N_XY, -1, LANES) for n in EARLY_GRADS]
    early.append(_flatten_full({n: gw[n].astype(BF16) for n, _, _ in BIG}))
    dq3, dk3, dv3, parts = _attention_bwd(proj3, do_att.reshape(B, S, ATT_WIDTH), o_att3, lse4, seq_blocks,
                                          _Riders(early, "scatter") if scatter_grads else None)
    dproj = jnp.concatenate([t.reshape(T, ATT_WIDTH) for t in (dq3, dk3, dv3)] + [dus, dga, dgs], axis=1)
    dmods = [None, None, dgt1, dsh2, dsc2, dgt2]
    native = dict(b_att=db_att, b_ssm=db_ssm, a_re=dlr, a_im=dli, log_dt=dldt, b_re=dbr, b_im=dbi, c_re=dcr, c_im=dci,
                  d_skip=dd_skip, b_glu=db_glu, g_ffn=dg_ffn, w_conv=dw_conv, b_conv=db_conv, g_final=dg_final, loss=loss)
    small_early = _pack_small(native, dmods, False)
    sums, sums_sib, last_parts = [], [], []
    if scatter_grads:
        sums = [_sum_slots(p, "sum_chips_%d" % i) for i, p in enumerate(parts)]
        riders = _RiderGroup([_Riders([small_early], "gather", "all"), _Riders(sums, "swap", "c")])
        gw["w_in_t"], rode = _mm(dproj, u1, ta=True, out_dtype=BF16, name="mm_dw_in", riders=riders)
        small_early, sums_sib = rode[0], rode[1:]
        du1, last_parts = _mm(dproj, W["w_in_t"], name="mm_du1", out_dtype=BF16,
                              riders=_Riders([gw["w_in_t"].reshape(N_XY, -1, LANES)], "scatter"))
    else:
        gw["w_in_t"] = _mm(dproj, u1, ta=True, out_dtype=BF16, name="mm_dw_in")
        du1 = _mm(dproj, W["w_in_t"], name="mm_du1", out_dtype=BF16)
    du1 = du1.reshape(B, S, D_MODEL)
    dx, dsh1, dsc1, dg_mix = _norm_bwd(x3, du1, dh1, g_mix, sc1, "norm_bwd1")
    dmods[0], dmods[1] = dsh1, dsc1
    native["g_mix"] = dg_mix
    return loss, dx, dmods, gw, native, (sums, sums_sib, last_parts), small_early


WEIGHTS = ['w_ada', 'b_ada', 'g_mix', 'w_in', 'b_gate', 'a_re', 'a_im', 'log_dt', 'b_re', 'b_im', 'c_re', 'c_im', 'd_skip',
           'w_glu', 'b_glu', 'w_proj_att', 'w_proj_ssm', 'w_out', 'g_ffn', 'w_up', 'w_conv', 'b_conv', 'w_down', 'g_final']
SMALL = ['g_mix', 'b_gate', 'a_re', 'a_im', 'log_dt', 'b_re', 'b_im', 'c_re', 'c_im', 'd_skip', 'b_glu', 'g_ffn', 'w_conv',
         'b_conv', 'g_final']


def kernel(x, c, w_ada, b_ada, g_mix, w_in, b_gate, a_re, a_im, log_dt, b_re, b_im, c_re, c_im, d_skip, w_glu, b_glu, w_proj_att, w_proj_ssm, w_out, g_ffn, w_up, w_conv, b_conv, w_down, g_final, loss_target, m_w_ada, m_b_ada, m_g_mix, m_w_in, m_b_gate, m_a_re, m_a_im, m_log_dt, m_b_re, m_b_im, m_c_re, m_c_im, m_d_skip, m_w_glu, m_b_glu, m_w_proj_att, m_w_proj_ssm, m_w_out, m_g_ffn, m_w_up, m_w_conv, m_b_conv, m_w_down, m_g_final, v_w_ada, v_b_ada, v_g_mix, v_w_in, v_b_gate, v_a_re, v_a_im, v_log_dt, v_b_re, v_b_im, v_c_re, v_c_im, v_d_skip, v_w_glu, v_b_glu, v_w_proj_att, v_w_proj_ssm, v_w_out, v_g_ffn, v_w_up, v_w_conv, v_b_conv, v_w_down, v_g_final):
    args = dict(locals())
    w = {n: args[n] for n in WEIGHTS}
    m = {n: args["m_" + n] for n in WEIGHTS}
    v = {n: args["v_" + n] for n in WEIGHTS}
    B, S, _ = x.shape
    ix, iy, ic = lax.axis_index("x"), lax.axis_index("y"), lax.axis_index("c")
    chip = 2 * ix + iy
    ada_cols = w_ada.shape[2]

    c_all = _exchange(c, "all", "gather", "gather_c").reshape(8 * B, D_MODEL)
    b_cols = lax.dynamic_slice_in_dim(b_ada, chip * ada_cols, ada_cols, axis=1)
    mod_cols = _ada_fwd(c_all, w_ada[0], b_cols)
    mod_all = _exchange(mod_cols, "xy", "gather", "gather_mod")
    mod_all = mod_all.transpose(1, 0, 2).reshape(8 * B, 6 * D_MODEL)
    mod = lax.dynamic_slice_in_dim(mod_all, (4 * ix + 2 * iy + ic) * B, B, axis=0)

    shard = {n + ("_t" if t else ""): (w[n][0].T if t else w[n][0]).astype(BF16) for n, t in DIRECT}
    misc = _flatten_shards({n: w[n][0] for n, _, _ in BIG}).astype(BF16)
    (w_in_full,) = _gather_weights([shard["w_in_t"]], "gather_weights")
    W = {"w_in_t": w_in_full.reshape(-1, LANES)}

    P = {n: w[n][0] for n in SMALL if n not in ("w_conv", "g_final")}
    P["w_conv"] = None
    P["g_final"] = g_final

    loss, dx, dmods, gw, native, parts, small_early = _local_step(x, mod, loss_target, W, P,
                                                                  [shard[n] for n in LATE] + [w_conv[0], misc], True)

    small_late = _exchange(_pack_small(native, dmods, True), "all", "gather", "gather_small")
    native_sum, dmod_all = _sum_unpack_small(small_early, small_late, B)
    loss = native_sum["loss"][0, 0]
    g_small = _small_from_native(native_sum)
    dmod_all = dmod_all.reshape(8 * B, N_MOD * D_MODEL)
    dmod_cols = lax.dynamic_slice_in_dim(dmod_all, chip * ada_cols, ada_cols, axis=1)
    g_w_ada, g_b_ada = _ada_bwd(c_all, dmod_all, dmod_cols)

    red, red_sib, last_parts = parts
    red = red + [_sum_slots(last_parts[0], "sum_chips_w_in")]
    red_sib = red_sib + [_exchange(red[-1], "c", "swap", "share_cores")]
    order = list(EARLY_GRADS) + ["misc", "w_in_t"]
    halves = dict(zip(order, zip(red, red_sib)))

    grads = {"w_ada": g_w_ada[None], "b_ada": g_b_ada}
    grads["w_up"] = _add2(*halves["w_up_t"], F32, "add_cores_w_up").T[None]
    for k, gk in _unflatten_shard(_add2(*halves["misc"], F32, "add_cores_misc")).items():
        grads[k] = gk[None]
    wc_cols = w_conv.shape[2]
    for n in SMALL:
        g = g_small[n]
        if n == "w_conv":
            g = lax.dynamic_slice_in_dim(g, chip * wc_cols, wc_cols, axis=1)
        grads[n] = g.reshape(w[n].shape)

    delta, new_m, new_v = {}, {}, {}
    for n in ["w_ada"] + [b for b, _ in DIRECT] + [b for b, _, _ in BIG]:
        shp = w[n].shape
        if n == "w_in":
            r, s = halves["w_in_t"]
            d2, m2, v2, g2 = _adamw(w[n][0].T, r, m[n][0].T, v[n][0].T, "adamw_" + n, g_other=s)
            d2, m2, v2, grads[n] = d2.T, m2.T, v2.T, g2.T[None]
        elif n in ("w_down", "w_out"):
            r, s = halves[n]
            d2, m2, v2, g2 = _adamw(w[n][0], r, m[n][0], v[n][0], "adamw_" + n, g_other=s)
            grads[n] = g2[None]
        else:
            d2, m2, v2 = _adamw(w[n][0], grads[n][0], m[n][0], v[n][0], "adamw_" + n)
        delta[n], new_m[n], new_v[n] = d2.reshape(shp), m2.reshape(shp), v2.reshape(shp)
    rest = ["b_ada"] + SMALL

    def drop(a):
        return a.reshape(1, -1) if a.ndim == 1 else (a if a.ndim == 2 else a[0])

    upd = _adamw_multi([(drop(w[n]), drop(grads[n]), drop(m[n]), drop(v[n])) for n in rest])
    for n, (dd, mm, vv) in zip(rest, upd):
        delta[n], new_m[n], new_v[n] = dd.reshape(w[n].shape), mm.reshape(w[n].shape), vv.reshape(w[n].shape)

    return (loss, dx, *[grads[n] for n in WEIGHTS], *[delta[n] for n in WEIGHTS], *[new_m[n] for n in WEIGHTS],
            *[new_v[n] for n in WEIGHTS])
```

```python
import functools
import math

import jax
import jax.numpy as jnp
from jax import lax
from jax.experimental import pallas as pl
from jax.experimental.pallas import tpu as pltpu

F32, BF16 = jnp.float32, jnp.bfloat16

D_MODEL = 1024
N_HEADS = 8
HEAD_DIM = 64
ATT_WIDTH = 512
SSM_GROUPS = 16
SSM_GROUP_CH = 16
SSM_WIDTH = 256
SSM_STATE = 64
SSM_LANES = SSM_GROUPS * SSM_STATE
D_FF = 2048
IN_WIDTH = 3 * ATT_WIDTH + SSM_WIDTH + 2 * D_MODEL
ATT_BLOCK = 128
N_PATTERNS = 3
EPS = 1e-6
NEG_INF = -1e30

ADAM_LR, ADAM_B1, ADAM_B2, ADAM_EPS, ADAM_WD, ADAM_STEP = 0.001, 0.9, 0.999, 1e-08, 0.01, 10

V7X_VMEM_LIMIT_BYTES = 56 * 1024 * 1024
LANES = 1024


def _pcall(body, *, name, out_shape, grid=(), in_specs=None, out_specs=None, scratch_shapes=(), dims=None):
    params = dict(vmem_limit_bytes=V7X_VMEM_LIMIT_BYTES)
    if dims is not None:
        params["dimension_semantics"] = dims
    specs = {}
    if in_specs is not None:
        specs = dict(grid=grid, in_specs=in_specs, out_specs=out_specs)
    return pl.pallas_call(body, name=name, out_shape=out_shape, scratch_shapes=scratch_shapes,
                          compiler_params=pltpu.CompilerParams(**params), **specs)


def _sds(shape, dtype):
    return jax.ShapeDtypeStruct(tuple(shape), dtype)


def _tile(n, target):
    if n <= target:
        return n
    for t in range(target - target % 128, 0, -128):
        if n % t == 0:
            return t
    raise ValueError((n, target))


def _sig(v):
    return pl.reciprocal(1.0 + jnp.exp(-v), approx=True)


def _mm(a, b, *, name, ta=False, tb=False, out_dtype=F32, tm=2048, tn=1024, tk=1024, riders=None):
    halves = a.ndim == 3
    if halves:
        a_rows, a_cols = a.shape[1], 2 * a.shape[2]
    else:
        a_rows, a_cols = a.shape
    if ta:
        K, M = a_rows, a_cols
    else:
        M, K = a_rows, a_cols
    if tb:
        N, K2 = b.shape
    else:
        K2, N = b.shape
    assert K == K2, (a.shape, b.shape)
    if halves:
        tm, tk = (min(tm, M // 2), tk) if ta else (tm, min(tk, K // 2))
    tm, tn, tk = _tile(M, tm), _tile(N, tn), _tile(K, tk)
    nk = K // tk
    if halves and ta:
        per = a.shape[2] // tm
        a_spec = pl.BlockSpec((None, tk, tm), lambda i, j, k: (i // per, k, i % per))
    elif halves:
        per = a.shape[2] // tk
        a_spec = pl.BlockSpec((None, tm, tk), lambda i, j, k: (k // per, i, k % per))
    else:
        a_spec = pl.BlockSpec((tk, tm), lambda i, j, k: (k, i)) if ta else pl.BlockSpec((tm, tk), lambda i, j, k: (i, k))
    b_spec = pl.BlockSpec((tn, tk), lambda i, j, k: (j, k)) if tb else pl.BlockSpec((tk, tn), lambda i, j, k: (k, j))
    dn = (((0 if ta else 1,), (1 if tb else 0,)), ((), ()))

    def body(a_ref, b_ref, o_ref, acc_ref):
        k = pl.program_id(2)

        @pl.when(k == 0)
        def _():
            acc_ref[...] = jnp.zeros_like(acc_ref)

        acc_ref[...] += lax.dot_general(a_ref[...].astype(BF16), b_ref[...].astype(BF16), dn,
                                        preferred_element_type=F32)

        @pl.when(k == nk - 1)
        def _():
            o_ref[...] = acc_ref[...].astype(out_dtype)

    def body_single(a_ref, b_ref, o_ref):
        o_ref[...] = lax.dot_general(a_ref[...].astype(BF16), b_ref[...].astype(BF16), dn,
                                     preferred_element_type=F32).astype(out_dtype)

    grid = (M // tm, N // tn, nk)
    scratch = [] if nk == 1 else [pltpu.VMEM((tm, tn), F32)]
    o_spec = pl.BlockSpec((tm, tn), lambda i, j, k: (i, j))
    if riders is None:
        return _pcall(body_single if nk == 1 else body, name=name, out_shape=_sds((M, N), out_dtype), grid=grid,
                      in_specs=[a_spec, b_spec], out_specs=o_spec, scratch_shapes=scratch,
                      dims=("parallel", "parallel", "arbitrary"))(a, b)
    rs = riders
    res = _pcall(_with_riders(body_single if nk == 1 else body, rs, 2, 1, len(scratch), tuple(g - 1 for g in grid)),
                 name=name, out_shape=(_sds((M, N), out_dtype),) + tuple(rs.out_shape), grid=grid,
                 in_specs=[a_spec, b_spec] + rs.specs, out_specs=(o_spec,) + tuple(rs.specs),
                 scratch_shapes=scratch + rs.scratch, dims=("arbitrary", "arbitrary", "arbitrary"))(a, b, *rs.arrs)
    return res[0], list(res[1:])


def _ada_fwd(c_all, w_ada, b_ada_cols):
    n = w_ada.shape[1]

    def body(c_ref, w_ref, b_ref, o_ref):
        c = c_ref[...]
        act = c * _sig(c)
        o_ref[...] = jnp.dot(act.astype(BF16), w_ref[...].astype(BF16), preferred_element_type=F32) + b_ref[...]

    return _pcall(body, name="ada_fwd", out_shape=_sds((c_all.shape[0], n), F32))(c_all, w_ada, b_ada_cols)


def _ada_bwd(c_all, dmod_all, dmod_cols):
    n = dmod_cols.shape[1]

    def body(c_ref, da_ref, dc_ref, gw_ref, gb_ref):
        c = c_ref[...]
        act = c * _sig(c)
        gw_ref[...] = lax.dot_general(act, dc_ref[...], (((0,), (0,)), ((), ())), preferred_element_type=F32,
                                      precision=lax.Precision.HIGHEST)
        gb_ref[...] = jnp.sum(da_ref[...], axis=0, keepdims=True)

    return _pcall(body, name="ada_bwd", out_shape=(_sds((D_MODEL, n), F32), _sds((1, dmod_all.shape[1]), F32)))(
        c_all, dmod_all, dmod_cols)


ROW_TILE = 1024


def _row_specs(B, S):
    ts = min(S, ROW_TILE)
    row = pl.BlockSpec((1, ts, D_MODEL), lambda b, s: (b, s, 0))
    bvec = pl.BlockSpec((1, 1, D_MODEL), lambda b, s: (b, 0, 0))
    gvec = pl.BlockSpec((1, D_MODEL), lambda b, s: (0, 0))
    return ts, row, bvec, gvec


def _norm_mod(x3, g, sc, sh):
    B, S, _ = x3.shape
    ts, row, bvec, gvec = _row_specs(B, S)

    def body(x_ref, g_ref, sc_ref, sh_ref, u_ref):
        x = x_ref[0]
        r = lax.rsqrt(jnp.mean(x * x, axis=-1, keepdims=True) + EPS)
        u_ref[0] = ((x * r) * g_ref[...] * (1.0 + sc_ref[0]) + sh_ref[0]).astype(BF16)

    return _pcall(body, name="norm_mod1", out_shape=_sds(x3.shape, BF16), grid=(B, S // ts),
                  in_specs=[row, gvec, bvec, bvec], out_specs=row, dims=("parallel", "parallel"))(x3, g, sc, sh)


def _resid_norm_mod(x3, mix3, gt, g, sc, sh):
    B, S, _ = x3.shape
    ts, row, bvec, gvec = _row_specs(B, S)

    def body(x_ref, m_ref, gt_ref, g_ref, sc_ref, sh_ref, h_ref, u_ref):
        h = x_ref[0] + gt_ref[0] * m_ref[0]
        h_ref[0] = h
        r = lax.rsqrt(jnp.mean(h * h, axis=-1, keepdims=True) + EPS)
        u_ref[0] = ((h * r) * g_ref[...] * (1.0 + sc_ref[0]) + sh_ref[0]).astype(BF16)

    return _pcall(body, name="resid_norm_mod2", out_shape=(_sds(x3.shape, F32), _sds(x3.shape, BF16)),
                  grid=(B, S // ts), in_specs=[row, row, bvec, gvec, bvec, bvec], out_specs=(row, row),
                  dims=("parallel", "parallel"))(x3, mix3, gt, g, sc, sh)


def _norm_bwd(h3, du3, dres3, g, sc, name, mix3=None, gt=None, riders=None):
    B, S, _ = h3.shape
    ts, row, bvec, gvec = _row_specs(B, S)
    with_gate = mix3 is not None

    def body(*refs):
        if with_gate:
            h_ref, du_ref, dr_ref, g_ref, sc_ref, m_ref, gt_ref, dh_ref, dsh_ref, dsc_ref, dg_ref, dgt_ref, dm_ref = refs
        else:
            h_ref, du_ref, dr_ref, g_ref, sc_ref, dh_ref, dsh_ref, dsc_ref, dg_ref = refs
        b, s = pl.program_id(0), pl.program_id(1)
        h = h_ref[0]
        r = lax.rsqrt(jnp.mean(h * h, axis=-1, keepdims=True) + EPS)
        xn = h * r
        du = du_ref[0].astype(F32)
        g = g_ref[...]
        sc1 = 1.0 + sc_ref[0]
        dxn = du * g * sc1
        dh = dr_ref[0].astype(F32) + r * (dxn - xn * jnp.mean(dxn * xn, axis=-1, keepdims=True))
        dh_ref[0] = dh.astype(dh_ref.dtype)

        @pl.when(s == 0)
        def _():
            dsh_ref[...] = jnp.zeros_like(dsh_ref)
            dsc_ref[...] = jnp.zeros_like(dsc_ref)
            if with_gate:
                dgt_ref[...] = jnp.zeros_like(dgt_ref)

        @pl.when((s == 0) & (b == 0))
        def _():
            dg_ref[...] = jnp.zeros_like(dg_ref)

        dux = du * xn
        dsh_ref[0] += jnp.sum(du, axis=0, keepdims=True)
        dsc_ref[0] += jnp.sum(dux * g, axis=0, keepdims=True)
        dg_ref[...] += jnp.sum(dux * sc1, axis=0, keepdims=True)
        if with_gate:
            dgt_ref[0] += jnp.sum(dh * m_ref[0], axis=0, keepdims=True)
            dm_ref[0] = (dh * gt_ref[0]).astype(BF16)

    bshape = _sds((B, 1, D_MODEL), F32)
    in_specs = [row, row, row, gvec, bvec]
    out_shape = [_sds(h3.shape, BF16 if with_gate else F32), bshape, bshape, _sds((1, D_MODEL), F32)]
    out_specs = [row, bvec, bvec, gvec]
    args = [h3, du3, dres3, g, sc]
    if with_gate:
        in_specs += [row, bvec]
        out_shape += [bshape, _sds(h3.shape, BF16)]
        out_specs += [bvec, row]
        args += [mix3, gt]
    if riders is None:
        return _pcall(body, name=name, out_shape=tuple(out_shape), grid=(B, S // ts), in_specs=in_specs,
                      out_specs=tuple(out_specs), dims=("arbitrary", "arbitrary"))(*args)
    rs = riders
    res = _pcall(_with_riders(body, rs, len(args), len(out_shape), 0, (B - 1, S // ts - 1)), name=name,
                 out_shape=tuple(out_shape) + tuple(rs.out_shape), grid=(B, S // ts), in_specs=in_specs + rs.specs,
                 out_specs=tuple(out_specs) + tuple(rs.specs), scratch_shapes=rs.scratch,
                 dims=("arbitrary", "arbitrary"))(*args, *rs.arrs)
    return tuple(res[:len(out_shape)]) + (list(res[len(out_shape):]),)


def _final_loss(h1, ffn3, tgt3, gt, gfin):
    B, S, _ = h1.shape
    ts, row, bvec, gvec = _row_specs(B, S)
    one = pl.BlockSpec((1, 1), lambda b, s: (0, 0))

    def body(h_ref, f_ref, t_ref, gt_ref, gf_ref, dh_ref, dff_ref, dgt_ref, dgf_ref, loss_ref):
        b, s = pl.program_id(0), pl.program_id(1)
        f = f_ref[0].astype(F32)
        gtv = gt_ref[0]
        gf = gf_ref[...]
        h2 = h_ref[0] + gtv * f
        r = lax.rsqrt(jnp.mean(h2 * h2, axis=-1, keepdims=True) + EPS)
        n = h2 * r
        e = n * gf - t_ref[0]
        dy = e * (1.0 / D_MODEL)
        dn = dy * gf
        dh2 = r * (dn - n * jnp.mean(dn * n, axis=-1, keepdims=True))
        dh_ref[0] = dh2.astype(BF16)
        dff_ref[0] = (dh2 * gtv).astype(BF16)

        @pl.when(s == 0)
        def _():
            dgt_ref[...] = jnp.zeros_like(dgt_ref)

        @pl.when((s == 0) & (b == 0))
        def _():
            dgf_ref[...] = jnp.zeros_like(dgf_ref)
            loss_ref[...] = jnp.zeros_like(loss_ref)

        dgt_ref[0] += jnp.sum(dh2 * f, axis=0, keepdims=True)
        dgf_ref[...] += jnp.sum(dy * n, axis=0, keepdims=True)
        rows = jnp.sum(e * e, axis=1, keepdims=True)
        loss_ref[...] += jnp.sum(rows, axis=0, keepdims=True) * (0.5 / D_MODEL)

    return _pcall(body, name="final_loss",
                  out_shape=(_sds(h1.shape, BF16), _sds(h1.shape, BF16), _sds((B, 1, D_MODEL), F32),
                             _sds((1, D_MODEL), F32), _sds((1, 1), F32)),
                  grid=(B, S // ts), in_specs=[row, row, row, bvec, gvec], out_specs=(row, row, bvec, gvec, one),
                  dims=("arbitrary", "arbitrary"))(h1, ffn3, tgt3, gt, gfin)


ATT_GROUP = 4
ATT_GW = ATT_GROUP * HEAD_DIM
ATT_GROUPS = N_HEADS // ATT_GROUP
ATT_PAIRS = ATT_GW // ATT_BLOCK
ATT_UNROLL = 5
ATT_RESIDUE_UNROLL = 8
NT_DIMS = (((1,), (1,)), ((), ()))
TN_DIMS = (((0,), (0,)), ((), ()))


def _att_rows(start, d):
    if d == 1:
        return pl.ds(start if isinstance(start, int) else pl.multiple_of(start, ATT_BLOCK), ATT_BLOCK)
    return pl.ds(start, ATT_BLOCK, stride=d)


def _att_fill_bias(bias_ref, g, d):
    a = lax.broadcasted_iota(jnp.int32, (ATT_BLOCK, ATT_BLOCK), 0)
    j = lax.broadcasted_iota(jnp.int32, (ATT_BLOCK, ATT_BLOCK), 1)
    dist = (a - j).astype(F32)
    for hh in range(ATT_GROUP):
        t, e = divmod(hh, 2)
        rs = slice(e * ATT_BLOCK, (e + 1) * ATT_BLOCK)
        lo = 2.0 ** (-8.0 * (hh + 1) / N_HEADS) * d
        hi = 2.0 ** (-8.0 * (ATT_GROUP + hh + 1) / N_HEADS) * d
        slope = jnp.where(g == 0, lo, hi).astype(F32)
        bias_ref[t, rs, 0:ATT_BLOCK] = jnp.where(a >= j, -slope * dist, NEG_INF)
        bias_ref[t, rs, ATT_BLOCK:] = jnp.where(j >= a, -slope * (dist + float(ATT_BLOCK)), NEG_INF)


def _stack_heads(v2, low):
    return jnp.concatenate([jnp.where(low, v2, 0.0), jnp.where(low, 0.0, v2)], axis=0).astype(BF16)


def _unstack_heads(r2, low):
    return jnp.where(low, r2[0:ATT_BLOCK], r2[ATT_BLOCK:])


class _Riders:
    def __init__(self, arrs, mode, group="xy"):
        self.arrs, self.mode, self.n, self.group = list(arrs), mode, len(arrs), group
        k = len(_GROUP_MASKS[group])
        self.scratch = [pltpu.SemaphoreType.DMA((k * self.n,)), pltpu.SemaphoreType.DMA((k * self.n,))]
        if mode == "swap":
            assert group == "c"
            self.out_shape = [_sds(a.shape, a.dtype) for a in self.arrs]
        else:
            slot_shapes = [a.shape if mode == "gather" else a.shape[1:] for a in self.arrs]
            self.out_shape = [_sds((_GROUP_SLOTS[group],) + s, a.dtype) for s, a in zip(slot_shapes, self.arrs)]
            self.scratch += [pltpu.SemaphoreType.DMA((2 * self.n,))] + [pltpu.VMEM(s, a.dtype)
                                                                        for s, a in zip(slot_shapes, self.arrs)]
        self.specs = [pl.BlockSpec(memory_space=pl.ANY)] * self.n

    def _remote(self, x_refs, o_refs, send_sems, recv_sems):
        x, y, c = lax.axis_index("x"), lax.axis_index("y"), lax.axis_index("c")
        me = _group_slot(self.group, x, y, c)
        masks = _GROUP_MASKS[self.group]
        cps = []
        for i in range(self.n):
            for k, (dx, dy, dc) in enumerate(masks):
                px, py, pc = _flip(x, dx), _flip(y, dy), _flip(c, dc)
                src = x_refs[i].at[_group_slot(self.group, px, py, pc)] if self.mode == "scatter" else x_refs[i]
                dst = o_refs[i] if self.mode == "swap" else o_refs[i].at[me]
                cps.append(pltpu.make_async_remote_copy(
                    src_ref=src, dst_ref=dst, send_sem=send_sems.at[len(masks) * i + k],
                    recv_sem=recv_sems.at[len(masks) * i + k], device_id=(px, py, pc),
                    device_id_type=pl.DeviceIdType.MESH))
        return cps, me

    def start(self, x_refs, o_refs, scratch):
        cps, me = self._remote(x_refs, o_refs, scratch[0], scratch[1])
        for cp in cps:
            cp.start()
        if self.mode == "swap":
            return
        local_sems, bufs = scratch[2], scratch[3:]
        for i in range(self.n):
            src = x_refs[i] if self.mode == "gather" else x_refs[i].at[me]
            load = pltpu.make_async_copy(src, bufs[i], local_sems.at[2 * i])
            load.start()
            load.wait()
            pltpu.make_async_copy(bufs[i], o_refs[i].at[me], local_sems.at[2 * i + 1]).start()

    def wait(self, x_refs, o_refs, scratch):
        cps, me = self._remote(x_refs, o_refs, scratch[0], scratch[1])
        for cp in cps:
            cp.wait()
        if self.mode == "swap":
            return
        local_sems, bufs = scratch[2], scratch[3:]
        for i in range(self.n):
            pltpu.make_async_copy(bufs[i], o_refs[i].at[me], local_sems.at[2 * i + 1]).wait()


class _RiderGroup:
    def __init__(self, members):
        self.members = list(members)
        self.n = sum(m.n for m in self.members)
        self.arrs = [a for m in self.members for a in m.arrs]
        self.out_shape = [s for m in self.members for s in m.out_shape]
        self.specs = [s for m in self.members for s in m.specs]
        self.scratch = [s for m in self.members for s in m.scratch]

    def _each(self, x_refs, o_refs, scratch):
        i = j = 0
        for m in self.members:
            yield m, x_refs[i:i + m.n], o_refs[i:i + m.n], scratch[j:j + len(m.scratch)]
            i, j = i + m.n, j + len(m.scratch)

    def start(self, x_refs, o_refs, scratch):
        for m, xs, os, sc in self._each(x_refs, o_refs, scratch):
            m.start(xs, os, sc)

    def wait(self, x_refs, o_refs, scratch):
        for m, xs, os, sc in self._each(x_refs, o_refs, scratch):
            m.wait(xs, os, sc)


def _with_riders(compute, riders, n_in, n_out, n_scratch, last_step):
    if riders is None:
        return compute
    n = riders.n

    def body(*refs):
        ins, x_refs = refs[:n_in], refs[n_in:n_in + n]
        outs, o_refs = refs[n_in + n:n_in + n + n_out], refs[n_in + n + n_out:n_in + 2 * n + n_out]
        scratch = refs[n_in + 2 * n + n_out:]
        own, ride = scratch[:n_scratch], scratch[n_scratch:]
        ids = [pl.program_id(i) for i in range(len(last_step))]
        first = functools.reduce(jnp.logical_and, [i == 0 for i in ids])
        last = functools.reduce(jnp.logical_and, [i == l for i, l in zip(ids, last_step)])

        @pl.when(first)
        def _():
            riders.start(x_refs, o_refs, ride)

        compute(*ins, *outs, *own)

        @pl.when(last)
        def _():
            riders.wait(x_refs, o_refs, ride)

    return body


def _attention_fwd(proj3, seq_blocks, riders=None):
    B, S, _ = proj3.shape
    scale = HEAD_DIM ** -0.5
    nq = ATT_WIDTH // ATT_GW

    def col(k):
        return pl.BlockSpec((1, S, ATT_GW), lambda b, g, k=k: (b, 0, k * nq + g))

    o_spec = pl.BlockSpec((1, S, ATT_GW), lambda b, g: (b, 0, g))
    l_spec = pl.BlockSpec((1, 1, S, ATT_BLOCK), lambda b, g: (b, g, 0, 0))

    def compute(q_ref, k_ref, v_ref, o_ref, lse_ref, qf, kf, vf, os, ls, bias):
        g = pl.program_id(1)
        for t in range(ATT_PAIRS):
            ts = slice(t * ATT_BLOCK, (t + 1) * ATT_BLOCK)
            qf[t] = q_ref[0, :, ts].astype(F32) * scale
            kf[t] = k_ref[0, :, ts].astype(F32)
            vf[t] = v_ref[0, :, ts].astype(F32)
        lane = lax.broadcasted_iota(jnp.int32, (ATT_BLOCK, ATT_BLOCK), 1)
        low = lane < HEAD_DIM

        def block(p, d, r, n, has_prev):
            start = n * (ATT_BLOCK * d) + r
            rows = _att_rows(start, d)
            prows = _att_rows(start - ATT_BLOCK * d, d) if has_prev else None
            lse_t = jnp.zeros((ATT_BLOCK, ATT_BLOCK), F32)
            for t in range(ATT_PAIRS):
                q2 = _stack_heads(qf[t, rows, :], low)
                k2 = kf[t, rows, :].astype(BF16)
                v2 = vf[t, rows, :].astype(BF16)
                if has_prev:
                    k2 = jnp.concatenate([k2, kf[t, prows, :].astype(BF16)], axis=0)
                    v2 = jnp.concatenate([v2, vf[t, prows, :].astype(BF16)], axis=0)
                    b2 = bias[t]
                else:
                    b2 = bias[t, :, 0:ATT_BLOCK]
                s = lax.dot_general(q2, k2, NT_DIMS, preferred_element_type=F32) + b2
                m = jnp.max(s, axis=1, keepdims=True)
                pr = jnp.exp(s - m)
                den = jnp.sum(pr, axis=1, keepdims=True)
                o = jnp.dot(pr.astype(BF16), v2, preferred_element_type=F32) * (1.0 / den)
                os[p, t, rows, :] = _unstack_heads(o, low)
                lse2 = m + jnp.log(den)
                lse_t = jnp.where(lane == 2 * t, lse2[0:ATT_BLOCK], lse_t)
                lse_t = jnp.where(lane == 2 * t + 1, lse2[ATT_BLOCK:], lse_t)
            ls[p, rows, :] = lse_t

        for p in range(N_PATTERNS):
            d = 4 ** p
            _att_fill_bias(bias, g, d)
            _att_one_pattern(block, p, d, seq_blocks // d)

        def combine(i, carry):
            rows = pl.ds(pl.multiple_of(i * ATT_BLOCK, ATT_BLOCK), ATT_BLOCK)
            l0, l1, l2 = ls[0, rows, :], ls[1, rows, :], ls[2, rows, :]
            m = jnp.maximum(jnp.maximum(l0, l1), l2)
            lse = m + jnp.log(jnp.exp(l0 - m) + jnp.exp(l1 - m) + jnp.exp(l2 - m))
            lse_ref[0, 0, rows, :] = lse
            w = [jnp.exp(l0 - lse), jnp.exp(l1 - lse), jnp.exp(l2 - lse)]
            for t in range(ATT_PAIRS):
                acc = jnp.zeros((ATT_BLOCK, ATT_BLOCK), F32)
                for p in range(N_PATTERNS):
                    wt = jnp.where(low, w[p][:, 2 * t:2 * t + 1], w[p][:, 2 * t + 1:2 * t + 2])
                    acc = acc + wt * os[p, t, rows, :]
                o_ref[0, rows, t * ATT_BLOCK:(t + 1) * ATT_BLOCK] = acc.astype(BF16)
            return carry

        lax.fori_loop(0, S // ATT_BLOCK, combine, 0, unroll=2)

    scratch = ([pltpu.VMEM((ATT_PAIRS, S, ATT_BLOCK), F32)] * 3
               + [pltpu.VMEM((N_PATTERNS, ATT_PAIRS, S, ATT_BLOCK), F32), pltpu.VMEM((N_PATTERNS, S, ATT_BLOCK), F32),
                  pltpu.VMEM((ATT_PAIRS, 2 * ATT_BLOCK, 2 * ATT_BLOCK), F32)])
    rs = riders
    res = _pcall(_with_riders(compute, rs, 3, 2, len(scratch), (B - 1, ATT_GROUPS - 1)), name="attention_fwd",
                 out_shape=(_sds((B, S, ATT_WIDTH), BF16), _sds((B, ATT_GROUPS, S, ATT_BLOCK), F32))
                 + (tuple(rs.out_shape) if rs else ()),
                 grid=(B, ATT_GROUPS), in_specs=[col(0), col(1), col(2)] + (rs.specs if rs else []),
                 out_specs=(o_spec, l_spec) + (tuple(rs.specs) if rs else ()),
                 scratch_shapes=scratch + (rs.scratch if rs else []),
                 dims=("arbitrary", "arbitrary"))(proj3, proj3, proj3, *(rs.arrs if rs else []))
    return res[0], res[1], list(res[2:])


def _att_one_pattern(block, p, d, nb):
    def per_residue(r, carry):
        block(p, d, r, 0, False)
        if nb > 1:
            def per_block(n, c2):
                block(p, d, r, n, True)
                return c2
            lax.fori_loop(1, nb, per_block, 0, unroll=ATT_UNROLL if (nb - 1) % ATT_UNROLL == 0 else nb - 1)
        return carry

    if d == 1:
        per_residue(0, 0)
    else:
        lax.fori_loop(0, d, per_residue, 0, unroll=ATT_RESIDUE_UNROLL if nb == 1 else 1)


def _attention_bwd(proj3, do3, o3, lse4, seq_blocks, riders=None):
    B, S, _ = proj3.shape
    scale = HEAD_DIM ** -0.5
    nq = ATT_WIDTH // ATT_GW

    def col(k):
        return pl.BlockSpec((1, S, ATT_GW), lambda b, g, k=k: (b, 0, k * nq + g))

    o_spec = pl.BlockSpec((1, S, ATT_GW), lambda b, g: (b, 0, g))
    l_spec = pl.BlockSpec((1, 1, S, ATT_BLOCK), lambda b, g: (b, g, 0, 0))

    def compute(q_ref, k_ref, v_ref, do_ref, o_ref, lse_ref, dq_ref, dk_ref, dv_ref,
                qf, kf, vf, dof, dl, aq, ak, av, bias):
        g = pl.program_id(1)
        for t in range(ATT_PAIRS):
            ts = slice(t * ATT_BLOCK, (t + 1) * ATT_BLOCK)
            qf[t] = q_ref[0, :, ts].astype(F32) * scale
            kf[t] = k_ref[0, :, ts].astype(F32)
            vf[t] = v_ref[0, :, ts].astype(F32)
            dof[t] = do_ref[0, :, ts].astype(F32)
        aq[...] = jnp.zeros_like(aq)
        ak[...] = jnp.zeros_like(ak)
        av[...] = jnp.zeros_like(av)
        lane = lax.broadcasted_iota(jnp.int32, (ATT_BLOCK, ATT_BLOCK), 1)
        low = lane < HEAD_DIM

        def fill_delta(i, carry):
            rows = pl.ds(pl.multiple_of(i * ATT_BLOCK, ATT_BLOCK), ATT_BLOCK)
            acc = jnp.zeros((ATT_BLOCK, ATT_BLOCK), F32)
            for t in range(ATT_PAIRS):
                prod = dof[t, rows, :] * o_ref[0, rows, t * ATT_BLOCK:(t + 1) * ATT_BLOCK].astype(F32)
                lo = jnp.sum(jnp.where(low, prod, 0.0), axis=1, keepdims=True)
                hi = jnp.sum(prod, axis=1, keepdims=True) - lo
                acc = jnp.where(lane == 2 * t, lo, acc)
                acc = jnp.where(lane == 2 * t + 1, hi, acc)
            dl[rows, :] = acc
            return carry

        lax.fori_loop(0, S // ATT_BLOCK, fill_delta, 0, unroll=2)

        def block(p, d, r, n, has_prev):
            start = n * (ATT_BLOCK * d) + r
            rows = _att_rows(start, d)
            prows = _att_rows(start - ATT_BLOCK * d, d) if has_prev else None
            lse_t = lse_ref[0, 0, rows, :]
            dl_t = dl[rows, :]
            for t in range(ATT_PAIRS):
                q2 = _stack_heads(qf[t, rows, :], low)
                do2 = _stack_heads(dof[t, rows, :], low)
                k2 = kf[t, rows, :].astype(BF16)
                v2 = vf[t, rows, :].astype(BF16)
                if has_prev:
                    k2 = jnp.concatenate([k2, kf[t, prows, :].astype(BF16)], axis=0)
                    v2 = jnp.concatenate([v2, vf[t, prows, :].astype(BF16)], axis=0)
                    b2 = bias[t]
                else:
                    b2 = bias[t, :, 0:ATT_BLOCK]
                lse2 = jnp.concatenate([lse_t[:, 2 * t:2 * t + 1], lse_t[:, 2 * t + 1:2 * t + 2]], axis=0)
                dl2 = jnp.concatenate([dl_t[:, 2 * t:2 * t + 1], dl_t[:, 2 * t + 1:2 * t + 2]], axis=0)
                s = lax.dot_general(q2, k2, NT_DIMS, preferred_element_type=F32) + b2
                pr = jnp.exp(s - lse2)
                ds = (pr * (lax.dot_general(do2, v2, NT_DIMS, preferred_element_type=F32) - dl2)).astype(BF16)
                dq = _unstack_heads(jnp.dot(ds, k2, preferred_element_type=F32), low)
                dk = lax.dot_general(ds, q2, TN_DIMS, preferred_element_type=F32)
                dv = lax.dot_general(pr.astype(BF16), do2, TN_DIMS, preferred_element_type=F32)
                aq[t, rows, :] = aq[t, rows, :] + dq * scale
                ak[t, rows, :] = ak[t, rows, :] + dk[0:ATT_BLOCK]
                av[t, rows, :] = av[t, rows, :] + dv[0:ATT_BLOCK]
                if has_prev:
                    ak[t, prows, :] = ak[t, prows, :] + dk[ATT_BLOCK:]
                    av[t, prows, :] = av[t, prows, :] + dv[ATT_BLOCK:]

        for p in range(N_PATTERNS):
            d = 4 ** p
            _att_fill_bias(bias, g, d)
            _att_one_pattern(block, p, d, seq_blocks // d)

        for t in range(ATT_PAIRS):
            ts = slice(t * ATT_BLOCK, (t + 1) * ATT_BLOCK)
            dq_ref[0, :, ts] = aq[t].astype(BF16)
            dk_ref[0, :, ts] = ak[t].astype(BF16)
            dv_ref[0, :, ts] = av[t].astype(BF16)

    shp = _sds((B, S, ATT_WIDTH), BF16)
    pair_buf = pltpu.VMEM((ATT_PAIRS, S, ATT_BLOCK), F32)
    scratch = ([pair_buf] * 4 + [pltpu.VMEM((S, ATT_BLOCK), F32)] + [pair_buf] * 3
               + [pltpu.VMEM((ATT_PAIRS, 2 * ATT_BLOCK, 2 * ATT_BLOCK), F32)])
    rs = riders
    res = _pcall(_with_riders(compute, rs, 6, 3, len(scratch), (B - 1, ATT_GROUPS - 1)), name="attention_bwd",
                 out_shape=(shp, shp, shp) + (tuple(rs.out_shape) if rs else ()), grid=(B, ATT_GROUPS),
                 in_specs=[col(0), col(1), col(2), o_spec, o_spec, l_spec] + (rs.specs if rs else []),
                 out_specs=(o_spec, o_spec, o_spec) + (tuple(rs.specs) if rs else ()),
                 scratch_shapes=scratch + (rs.scratch if rs else []),
                 dims=("arbitrary", "arbitrary"))(proj3, proj3, proj3, do3, o3, lse4, *(rs.arrs if rs else []))
    return res[0], res[1], res[2], list(res[3:])


def _expand_groups(m):
    rows = SSM_WIDTH
    t = jnp.concatenate([m] * SSM_GROUPS, axis=0)
    r = lax.broadcasted_iota(jnp.int32, (rows, SSM_LANES), 0)
    l = lax.broadcasted_iota(jnp.int32, (rows, SSM_LANES), 1)
    keep = lax.shift_right_logical(r, 4) == lax.shift_right_logical(l, 6)
    return jnp.where(keep, t, 0.0)


def _collapse_groups(m):
    rows = SSM_WIDTH
    r = lax.broadcasted_iota(jnp.int32, (rows, SSM_LANES), 0)
    l = lax.broadcasted_iota(jnp.int32, (rows, SSM_LANES), 1)
    keep = lax.shift_right_logical(r, 4) == lax.shift_right_logical(l, 6)
    t = jnp.where(keep, m, 0.0)
    acc = t[0:SSM_GROUP_CH]
    for g in range(1, SSM_GROUPS):
        acc = acc + t[g * SSM_GROUP_CH:(g + 1) * SSM_GROUP_CH]
    return acc


def _zoh(lr, li, ldt):
    dt = jnp.exp(ldt)
    mag = jnp.exp(lr * dt)
    ang = li * dt
    cs, sn = jnp.cos(ang), jnp.sin(ang)
    ab_re, ab_im = mag * cs, mag * sn
    nr, ni = ab_re - 1.0, ab_im
    den = lr * lr + li * li
    n_re = nr * lr + ni * li
    n_im = ni * lr - nr * li
    return dict(dt=dt, mag=mag, cs=cs, sn=sn, ab_re=ab_re, ab_im=ab_im, nr=nr, ni=ni, den=den, n_re=n_re, n_im=n_im,
                f_re=n_re / den, f_im=n_im / den)


def _ssm_params(lr, li, ldt, br, bi, cr, ci):
    def body(lr_ref, li_ref, ldt_ref, br_ref, bi_ref, cr_ref, ci_ref, ab_ref, w_ref, c_ref):
        z = _zoh(lr_ref[...], li_ref[...], ldt_ref[...])
        ab_ref[0:1, :] = z["ab_re"]
        ab_ref[1:2, :] = z["ab_im"]
        br, bi = br_ref[...], bi_ref[...]
        w_ref[:, 0:SSM_LANES] = _expand_groups(z["f_re"] * br - z["f_im"] * bi).astype(BF16)
        w_ref[:, SSM_LANES:] = _expand_groups(z["f_re"] * bi + z["f_im"] * br).astype(BF16)
        c_ref[:, 0:SSM_LANES] = _expand_groups(cr_ref[...]).astype(BF16)
        c_ref[:, SSM_LANES:] = _expand_groups(-ci_ref[...]).astype(BF16)

    return _pcall(body, name="ssm_params",
                  out_shape=(_sds((2, SSM_LANES), F32), _sds((SSM_WIDTH, 2 * SSM_LANES), BF16),
                             _sds((SSM_WIDTH, 2 * SSM_LANES), BF16)))(lr, li, ldt, br, bi, cr, ci)


def _ssm_params_bwd(lr, li, ldt, br, bi, dab, dw, dc):
    def body(lr_ref, li_ref, ldt_ref, br_ref, bi_ref, dab_ref, dw_ref, dc_ref,
             dlr_ref, dli_ref, dldt_ref, dbr_ref, dbi_ref, dcr_ref, dci_ref):
        lr, li = lr_ref[...], li_ref[...]
        z = _zoh(lr, li, ldt_ref[...])
        br, bi = br_ref[...], bi_ref[...]
        dbb_re = _collapse_groups(dw_ref[:, 0:SSM_LANES])
        dbb_im = _collapse_groups(dw_ref[:, SSM_LANES:])
        dcr_ref[...] = _collapse_groups(dc_ref[:, 0:SSM_LANES])
        dci_ref[...] = -_collapse_groups(dc_ref[:, SSM_LANES:])
        f_re, f_im = z["f_re"], z["f_im"]
        dbr_ref[...] = f_re * dbb_re + f_im * dbb_im
        dbi_ref[...] = f_re * dbb_im - f_im * dbb_re
        df_re = jnp.sum(dbb_re * br + dbb_im * bi, axis=0, keepdims=True)
        df_im = jnp.sum(dbb_im * br - dbb_re * bi, axis=0, keepdims=True)
        den = z["den"]
        dn_re, dn_im = df_re / den, df_im / den
        dden = -(df_re * z["n_re"] + df_im * z["n_im"]) / (den * den)
        dnr = dn_re * lr - dn_im * li
        dni = dn_re * li + dn_im * lr
        dlr = dn_re * z["nr"] + dn_im * z["ni"] + 2.0 * dden * lr
        dli = dn_re * z["ni"] - dn_im * z["nr"] + 2.0 * dden * li
        dab_re = dab_ref[0:1, :] + dnr
        dab_im = dab_ref[1:2, :] + dni
        mag, cs, sn, dt = z["mag"], z["cs"], z["sn"], z["dt"]
        dmag = dab_re * cs + dab_im * sn
        dang = mag * (dab_im * cs - dab_re * sn)
        dlr_ref[...] = dlr + dmag * mag * dt
        dli_ref[...] = dli + dang * dt
        ddt = dmag * mag * lr + dang * li
        per_lane = jnp.broadcast_to(ddt * dt, (8, SSM_LANES))
        lane = lax.broadcasted_iota(jnp.int32, (SSM_LANES, 128), 0)
        col = lax.broadcasted_iota(jnp.int32, (SSM_LANES, 128), 1)
        ind = jnp.where(lax.shift_right_logical(lane, 6) == col, 1.0, 0.0)
        dldt_ref[...] = jnp.dot(per_lane, ind, preferred_element_type=F32, precision=lax.Precision.HIGHEST)[0:1]

    vec = _sds((1, SSM_LANES), F32)
    mat = _sds((SSM_GROUP_CH, SSM_LANES), F32)
    return _pcall(body, name="ssm_params_bwd", out_shape=(vec, vec, _sds((1, 128), F32), mat, mat, mat, mat))(
        lr, li, ldt, br, bi, dab, dw, dc)


SCAN_CHUNK = 512


def _scan_consts(ar, ai, k_ref, reverse):
    row = lax.broadcasted_iota(jnp.int32, (8, SSM_LANES), 0)
    pw = [(ar, ai)]
    for _ in range(7):
        pr, pi = pw[-1]
        pw.append((pr * ar - pi * ai, pr * ai + pi * ar))
    for n, k in enumerate((1, 2, 4)):
        keep = (row < 8 - k) if reverse else (row >= k)
        k_ref[2 * n] = jnp.where(keep, jnp.broadcast_to(pw[k - 1][0], (8, SSM_LANES)), 0.0)
        k_ref[2 * n + 1] = jnp.where(keep, jnp.broadcast_to(pw[k - 1][1], (8, SSM_LANES)), 0.0)
    cr = jnp.zeros((8, SSM_LANES), F32)
    ci = jnp.zeros((8, SSM_LANES), F32)
    for r in range(8):
        e = (8 - r) if reverse else (r + 1)
        cr = jnp.where(row == r, jnp.broadcast_to(pw[e - 1][0], (8, SSM_LANES)), cr)
        ci = jnp.where(row == r, jnp.broadcast_to(pw[e - 1][1], (8, SSM_LANES)), ci)
    k_ref[6] = cr
    k_ref[7] = ci


def _scan_tile(xr, xi, k_ref, car, cai, reverse):
    for n, k in enumerate((1, 2, 4)):
        sh = (8 - k) if reverse else k
        sr = pltpu.roll(xr, sh, 0)
        si = pltpu.roll(xi, sh, 0)
        mr, mi = k_ref[2 * n], k_ref[2 * n + 1]
        xr, xi = xr + mr * sr - mi * si, xi + mr * si + mi * sr
    pr, pi = k_ref[6], k_ref[7]
    xr, xi = xr + pr * car - pi * cai, xi + pr * cai + pi * car
    return xr, xi


US_BLOCK = (3 * ATT_WIDTH) // SSM_WIDTH


def _ssm_scan_fwd(proj3, abar, w_bu, w_c):
    B, S, _ = proj3.shape
    ch = min(S, SCAN_CHUNK)
    u_spec = pl.BlockSpec((1, ch, SSM_WIDTH), lambda b, c: (b, c, US_BLOCK))
    x_spec = pl.BlockSpec((1, ch, 2 * SSM_LANES), lambda b, c: (b, c, 0))
    y_spec = pl.BlockSpec((1, ch, SSM_WIDTH), lambda b, c: (b, c, 0))
    w_spec = pl.BlockSpec((SSM_WIDTH, 2 * SSM_LANES), lambda b, c: (0, 0))

    def body(ab_ref, u_ref, wb_ref, wc_ref, x_ref, y_ref, k_ref, carry_ref):
        _scan_consts(ab_ref[0:1, :], ab_ref[1:2, :], k_ref, False)

        @pl.when(pl.program_id(1) == 0)
        def _():
            carry_ref[...] = jnp.zeros_like(carry_ref)

        x_ref[0] = jnp.dot(u_ref[0], wb_ref[...], preferred_element_type=F32)

        def step(i, carry):
            base = pl.multiple_of(i * 8, 8)
            xr = x_ref[0, pl.ds(base, 8), 0:SSM_LANES]
            xi = x_ref[0, pl.ds(base, 8), SSM_LANES:]
            xr, xi = _scan_tile(xr, xi, k_ref, carry[0], carry[1], False)
            x_ref[0, pl.ds(base, 8), 0:SSM_LANES] = xr
            x_ref[0, pl.ds(base, 8), SSM_LANES:] = xi
            return (jnp.broadcast_to(xr[7:8], (8, SSM_LANES)), jnp.broadcast_to(xi[7:8], (8, SSM_LANES)))

        cr, ci = lax.fori_loop(0, ch // 8, step, (carry_ref[0], carry_ref[1]))
        carry_ref[0] = cr
        carry_ref[1] = ci
        y_ref[0] = lax.dot_general(x_ref[0].astype(BF16), wc_ref[...], NT_DIMS, preferred_element_type=F32)

    return _pcall(body, name="ssm_scan_fwd",
                  out_shape=(_sds((B, S, 2 * SSM_LANES), F32), _sds((B, S, SSM_WIDTH), F32)), grid=(B, S // ch),
                  in_specs=[pl.BlockSpec((2, SSM_LANES), lambda b, c: (0, 0)), u_spec, w_spec, w_spec],
                  out_specs=(x_spec, y_spec),
                  scratch_shapes=[pltpu.VMEM((8, 8, SSM_LANES), F32), pltpu.VMEM((2, 8, SSM_LANES), F32)],
                  dims=("arbitrary", "arbitrary"))(abar, proj3, w_bu, w_c)


def _ssm_scan_bwd(proj3, dy3, xs3, abar, w_bu, w_c, dsk):
    B, S, _ = proj3.shape
    ch = min(S, SCAN_CHUNK)
    nc = S // ch
    u_spec = pl.BlockSpec((1, ch, SSM_WIDTH), lambda b, c: (b, nc - 1 - c, US_BLOCK))
    x_spec = pl.BlockSpec((1, ch, 2 * SSM_LANES), lambda b, c: (b, nc - 1 - c, 0))
    y_spec = pl.BlockSpec((1, ch, SSM_WIDTH), lambda b, c: (b, nc - 1 - c, 0))
    w_spec = pl.BlockSpec((SSM_WIDTH, 2 * SSM_LANES), lambda b, c: (0, 0))
    ab_spec = pl.BlockSpec((2, SSM_LANES), lambda b, c: (0, 0))
    d_spec = pl.BlockSpec((1, SSM_WIDTH), lambda b, c: (0, 0))

    def body(ab_ref, u_ref, dy_ref, xs_ref, wb_ref, wc_ref, d_ref, du_ref, da_ref, dwb_ref, dwc_ref,
             g_ref, k_ref, carry_ref, acc_ref):
        b, c = pl.program_id(0), pl.program_id(1)
        _scan_consts(ab_ref[0:1, :], -ab_ref[1:2, :], k_ref, True)
        row = lax.broadcasted_iota(jnp.int32, (8, SSM_LANES), 0)

        @pl.when(c == 0)
        def _():
            carry_ref[...] = jnp.zeros_like(carry_ref)

        @pl.when((c == 0) & (b == 0))
        def _():
            acc_ref[...] = jnp.zeros_like(acc_ref)
            dwb_ref[...] = jnp.zeros_like(dwb_ref)
            dwc_ref[...] = jnp.zeros_like(dwc_ref)

        dy = dy_ref[0]
        dyb = dy.astype(BF16)
        g_ref[...] = jnp.dot(dyb, wc_ref[...], preferred_element_type=F32)

        def step(i, carry):
            car, cai, ar_acc, ai_acc = carry
            base = pl.multiple_of((ch // 8 - 1 - i) * 8, 8)
            gr = g_ref[pl.ds(base, 8), 0:SSM_LANES]
            gi = g_ref[pl.ds(base, 8), SSM_LANES:]
            gr, gi = _scan_tile(gr, gi, k_ref, car, cai, True)
            g_ref[pl.ds(base, 8), 0:SSM_LANES] = gr
            g_ref[pl.ds(base, 8), SSM_LANES:] = gi
            nr = jnp.where(row == 7, car, pltpu.roll(gr, 7, 0))
            ni = jnp.where(row == 7, cai, pltpu.roll(gi, 7, 0))
            xr = xs_ref[0, pl.ds(base, 8), 0:SSM_LANES]
            xi = xs_ref[0, pl.ds(base, 8), SSM_LANES:]
            ar_acc = ar_acc + nr * xr + ni * xi
            ai_acc = ai_acc + ni * xr - nr * xi
            return (jnp.broadcast_to(gr[0:1], (8, SSM_LANES)), jnp.broadcast_to(gi[0:1], (8, SSM_LANES)), ar_acc, ai_acc)

        cr, ci, ar_acc, ai_acc = lax.fori_loop(0, ch // 8, step, (carry_ref[0], carry_ref[1], acc_ref[0], acc_ref[1]))
        carry_ref[0] = cr
        carry_ref[1] = ci
        acc_ref[0] = ar_acc
        acc_ref[1] = ai_acc
        da_ref[0:1, :] = jnp.sum(ar_acc, axis=0, keepdims=True)
        da_ref[1:2, :] = jnp.sum(ai_acc, axis=0, keepdims=True)

        gb = g_ref[...].astype(BF16)
        du = lax.dot_general(gb, wb_ref[...], NT_DIMS, preferred_element_type=F32) + d_ref[...] * dy
        du_ref[0] = du.astype(BF16)
        xb = xs_ref[0].astype(BF16)
        u = u_ref[0]
        for j in range(2 * SSM_LANES // SSM_WIDTH):
            rows = slice((j % (SSM_LANES // SSM_WIDTH)) * 64, (j % (SSM_LANES // SSM_WIDTH)) * 64 + 64)
            cols = slice(j * SSM_WIDTH, (j + 1) * SSM_WIDTH)
            dwb_ref[rows, cols] += lax.dot_general(u[:, rows], gb[:, cols], TN_DIMS, preferred_element_type=F32)
            dwc_ref[rows, cols] += lax.dot_general(dyb[:, rows], xb[:, cols], TN_DIMS, preferred_element_type=F32)

    mat = _sds((SSM_WIDTH, 2 * SSM_LANES), F32)
    return _pcall(body, name="ssm_scan_bwd",
                  out_shape=(_sds((B, S, SSM_WIDTH), BF16), _sds((2, SSM_LANES), F32), mat, mat), grid=(B, nc),
                  in_specs=[ab_spec, u_spec, y_spec, x_spec, w_spec, w_spec, d_spec],
                  out_specs=(y_spec, ab_spec, w_spec, w_spec),
                  scratch_shapes=[pltpu.VMEM((ch, 2 * SSM_LANES), F32), pltpu.VMEM((8, 8, SSM_LANES), F32),
                                  pltpu.VMEM((2, 8, SSM_LANES), F32), pltpu.VMEM((2, 8, SSM_LANES), F32)],
                  dims=("arbitrary", "arbitrary"))(abar, proj3, dy3, xs3, w_bu, w_c, dsk)


GELU_K = math.sqrt(2.0 / math.pi)
GELU_C = 0.044715


def _gelu_parts(y):
    t = jnp.tanh(GELU_K * (y + GELU_C * y * y * y))
    return 0.5 * y * (1.0 + t), t


def _ssm_post(yc, us, dsk, wglu, bglu):
    T, N = yc.shape
    tm = min(T, 1024)
    row = pl.BlockSpec((tm, N), lambda i: (i, 0))
    vec = pl.BlockSpec((1, N), lambda i: (0, 0))
    mat = pl.BlockSpec((N, N), lambda i: (0, 0))

    def body(yc_ref, us_ref, d_ref, w_ref, b_ref, y_ref, s_ref):
        y = yc_ref[...] + d_ref[...] * us_ref[...]
        y_ref[...] = y
        z, _ = _gelu_parts(y)
        gl = jnp.dot(z.astype(BF16), w_ref[...], preferred_element_type=F32) + b_ref[...]
        s_ref[...] = (z * _sig(gl)).astype(BF16)

    return _pcall(body, name="ssm_post", out_shape=(_sds((T, N), F32), _sds((T, N), BF16)), grid=(T // tm,),
                  in_specs=[row, row, vec, mat, vec], out_specs=(row, row), dims=("parallel",))(yc, us, dsk, wglu, bglu)


def _ssm_post_bwd(y5, us, ds, dsk, wglu, bglu):
    T, N = y5.shape
    tm = min(T, 1024)
    row = pl.BlockSpec((tm, N), lambda i: (i, 0))
    vec = pl.BlockSpec((1, N), lambda i: (0, 0))
    mat = pl.BlockSpec((N, N), lambda i: (0, 0))

    def body(y_ref, us_ref, ds_ref, d_ref, w_ref, b_ref, dy_ref, dd_ref, db_ref, dw_ref):
        @pl.when(pl.program_id(0) == 0)
        def _():
            dd_ref[...] = jnp.zeros_like(dd_ref)
            db_ref[...] = jnp.zeros_like(db_ref)
            dw_ref[...] = jnp.zeros_like(dw_ref)

        y = y_ref[...]
        z, t = _gelu_parts(y)
        zb = z.astype(BF16)
        gl = jnp.dot(zb, w_ref[...], preferred_element_type=F32) + b_ref[...]
        sg = _sig(gl)
        ds = ds_ref[...]
        dgl = ds * z * sg * (1.0 - sg)
        dglb = dgl.astype(BF16)
        dz = ds * sg + lax.dot_general(dglb, w_ref[...], (((1,), (1,)), ((), ())), preferred_element_type=F32)
        dgelu = 0.5 * (1.0 + t) + 0.5 * y * (1.0 - t * t) * GELU_K * (1.0 + 3.0 * GELU_C * y * y)
        dy = dz * dgelu
        dy_ref[...] = dy
        dd_ref[...] += jnp.sum(dy * us_ref[...], axis=0, keepdims=True)
        db_ref[...] += jnp.sum(dgl, axis=0, keepdims=True)
        dw_ref[...] += lax.dot_general(zb, dglb, (((0,), (0,)), ((), ())), preferred_element_type=F32)

    return _pcall(body, name="ssm_post_bwd",
                  out_shape=(_sds((T, N), F32), _sds((1, N), F32), _sds((1, N), F32), _sds((N, N), F32)),
                  grid=(T // tm,), in_specs=[row, row, row, vec, mat, vec], out_specs=(row, vec, vec, mat),
                  dims=("arbitrary",))(y5, us, ds, dsk, wglu, bglu)


GATE_TILE = 256
GATE_ATT_BLOCK0 = (3 * ATT_WIDTH + SSM_WIDTH) // GATE_TILE
GATE_SSM_BLOCK0 = (3 * ATT_WIDTH + SSM_WIDTH + D_MODEL) // GATE_TILE


def _merge(proj, y_att, y_ssm, b_gate):
    T = proj.shape[0]
    tm = min(T, 4096)
    nj = D_MODEL // GATE_TILE
    ga = pl.BlockSpec((tm, GATE_TILE), lambda i, j: (i, GATE_ATT_BLOCK0 + j))
    gs = pl.BlockSpec((tm, GATE_TILE), lambda i, j: (i, GATE_SSM_BLOCK0 + j))
    yy = pl.BlockSpec((tm, GATE_TILE), lambda i, j: (i, j))
    ba = pl.BlockSpec((1, GATE_TILE), lambda i, j: (0, j))
    bs = pl.BlockSpec((1, GATE_TILE), lambda i, j: (0, nj + j))

    def body(ga_ref, gs_ref, ya_ref, ys_ref, ba_ref, bs_ref, o_ref):
        o_ref[...] = (_sig(ga_ref[...] + ba_ref[...]) * ya_ref[...]
                      + _sig(gs_ref[...] + bs_ref[...]) * ys_ref[...]).astype(BF16)

    return _pcall(body, name="merge", out_shape=_sds((T, D_MODEL), BF16), grid=(T // tm, nj),
                  in_specs=[ga, gs, yy, yy, ba, bs], out_specs=yy, dims=("parallel", "parallel"))(
        proj, proj, y_att, y_ssm, b_gate, b_gate)


def _merge_bwd(proj, y_att, y_ssm, b_gate, dmerged):
    T = proj.shape[0]
    tm = min(T, 2048)
    nj = D_MODEL // GATE_TILE
    ga = pl.BlockSpec((tm, GATE_TILE), lambda j, i: (i, GATE_ATT_BLOCK0 + j))
    gs = pl.BlockSpec((tm, GATE_TILE), lambda j, i: (i, GATE_SSM_BLOCK0 + j))
    yy = pl.BlockSpec((tm, GATE_TILE), lambda j, i: (i, j))
    ba = pl.BlockSpec((1, GATE_TILE), lambda j, i: (0, j))
    bs = pl.BlockSpec((1, GATE_TILE), lambda j, i: (0, nj + j))

    def body(ga_ref, gs_ref, ya_ref, ys_ref, ba_ref, bs_ref, dm_ref, dya_ref, dys_ref, dga_ref, dgs_ref, dba_ref, dbs_ref):
        @pl.when(pl.program_id(1) == 0)
        def _():
            dba_ref[...] = jnp.zeros_like(dba_ref)
            dbs_ref[...] = jnp.zeros_like(dbs_ref)

        dm = dm_ref[...].astype(F32)
        sa = _sig(ga_ref[...] + ba_ref[...])
        ss = _sig(gs_ref[...] + bs_ref[...])
        dya_ref[...] = (dm * sa).astype(BF16)
        dys_ref[...] = (dm * ss).astype(BF16)
        dga = dm * ya_ref[...] * sa * (1.0 - sa)
        dgs = dm * ys_ref[...] * ss * (1.0 - ss)
        dga_ref[...] = dga.astype(BF16)
        dgs_ref[...] = dgs.astype(BF16)
        dba_ref[...] += jnp.sum(dga, axis=0, keepdims=True)
        dbs_ref[...] += jnp.sum(dgs, axis=0, keepdims=True)

    big = _sds((T, D_MODEL), BF16)
    vec = _sds((1, D_MODEL), F32)
    return _pcall(body, name="merge_bwd", out_shape=(big, big, big, big, vec, vec), grid=(nj, T // tm),
                  in_specs=[ga, gs, yy, yy, ba, bs, yy], out_specs=(yy, yy, yy, yy, ba, ba),
                  dims=("arbitrary", "arbitrary"))(proj, proj, y_att, y_ssm, b_gate, b_gate, dmerged)


CONV_TILE = 256


def _shift_rows(a, j, up=False):
    n = a.shape[0]
    r = pltpu.roll(a, n - j if up else j, 0)
    row = lax.broadcasted_iota(jnp.int32, (8, a.shape[1]), 0)
    if up:
        return jnp.concatenate([r[:n - 8], jnp.where(row < 8 - j, r[n - 8:], 0.0)], axis=0)
    return jnp.concatenate([jnp.where(row >= j, r[:8], 0.0), r[8:]], axis=0)


def _conv_pre(a, w_ref, b_ref):
    conv = b_ref[...] + w_ref[0:1, :] * a
    shifted = []
    for j in (1, 2):
        sh = _shift_rows(a, j)
        shifted.append(sh)
        conv = conv + w_ref[j:j + 1, :] * sh
    return conv, shifted


def _conv_act(up3, w_conv, b_conv):
    B, S, _ = up3.shape
    nj = D_FF // CONV_TILE
    a_spec = pl.BlockSpec((1, S, CONV_TILE), lambda b, j: (b, 0, j))
    v_spec = pl.BlockSpec((1, S, CONV_TILE), lambda b, j: (b, 0, nj + j))
    w_spec = pl.BlockSpec((3, CONV_TILE), lambda b, j: (0, j))
    b_spec = pl.BlockSpec((1, CONV_TILE), lambda b, j: (0, j))

    def body(a_ref, v_ref, w_ref, b_ref, o_ref):
        a = a_ref[0].astype(F32)
        conv, _ = _conv_pre(a, w_ref, b_ref)
        o_ref[0] = (conv * _sig(conv) * v_ref[0]).astype(BF16)

    return _pcall(body, name="conv_act", out_shape=_sds((B, S, D_FF), BF16), grid=(B, nj),
                  in_specs=[a_spec, v_spec, w_spec, b_spec], out_specs=a_spec, dims=("parallel", "parallel"))(
        up3, up3, w_conv, b_conv)


def _conv_bwd(up3, dact3, w_conv, b_conv):
    B, S, _ = up3.shape
    nj = D_FF // CONV_TILE
    a_spec = pl.BlockSpec((1, S, CONV_TILE), lambda j, b: (b, 0, j))
    v_spec = pl.BlockSpec((1, S, CONV_TILE), lambda j, b: (b, 0, nj + j))
    o_spec = pl.BlockSpec((2, 1, S, CONV_TILE), lambda j, b: (0, b, 0, j))
    w_spec = pl.BlockSpec((3, CONV_TILE), lambda j, b: (0, j))
    b_spec = pl.BlockSpec((1, CONV_TILE), lambda j, b: (0, j))

    def body(a_ref, v_ref, d_ref, w_ref, b_ref, dup_ref, dw_ref, db_ref):
        @pl.when(pl.program_id(1) == 0)
        def _():
            dw_ref[...] = jnp.zeros_like(dw_ref)
            db_ref[...] = jnp.zeros_like(db_ref)

        a = a_ref[0].astype(F32)
        d = d_ref[0].astype(F32)
        conv, shifted = _conv_pre(a, w_ref, b_ref)
        sg = _sig(conv)
        dup_ref[1, 0] = (d * conv * sg).astype(BF16)
        dconv = d * v_ref[0] * (sg * (1.0 + conv * (1.0 - sg)))
        da = w_ref[0:1, :] * dconv
        for j in (1, 2):
            da = da + w_ref[j:j + 1, :] * _shift_rows(dconv, j, up=True)
        dup_ref[0, 0] = da.astype(BF16)
        db_ref[...] += jnp.sum(dconv, axis=0, keepdims=True)
        dw_ref[0:1, :] += jnp.sum(dconv * a, axis=0, keepdims=True)
        dw_ref[1:2, :] += jnp.sum(dconv * shifted[0], axis=0, keepdims=True)
        dw_ref[2:3, :] += jnp.sum(dconv * shifted[1], axis=0, keepdims=True)

    return _pcall(body, name="conv_bwd",
                  out_shape=(_sds((2, B, S, D_FF), BF16), _sds((3, D_FF), F32), _sds((1, D_FF), F32)),
                  grid=(nj, B), in_specs=[a_spec, v_spec, a_spec, w_spec, b_spec],
                  out_specs=(o_spec, w_spec, b_spec), dims=("arbitrary", "arbitrary"))(up3, up3, dact3, w_conv, b_conv)


def _rows_tile(r, cap=640):
    for t in range(min(r, cap) - min(r, cap) % 8, 7, -8):
        if r % t == 0:
            return t
    return r


def _add2(a, b, out_dtype, name):
    R, N = a.shape
    tr = _rows_tile(R)
    spec = pl.BlockSpec((tr, N), lambda i: (i, 0))

    def body(a_ref, b_ref, o_ref):
        o_ref[...] = (a_ref[...] + b_ref[...]).astype(out_dtype)

    return _pcall(body, name=name, out_shape=_sds((R, N), out_dtype), grid=(R // tr,), in_specs=[spec, spec],
                  out_specs=spec, dims=("parallel",))(a, b)


def _sum_slots(q, name):
    n, R, N = q.shape
    tr = _rows_tile(R)

    def body(q_ref, o_ref):
        acc = q_ref[0].astype(F32)
        for s in range(1, n):
            acc = acc + q_ref[s].astype(F32)
        o_ref[...] = acc

    return _pcall(body, name=name, out_shape=_sds((R, N), F32), grid=(R // tr,),
                  in_specs=[pl.BlockSpec((n, tr, N), lambda i: (0, i, 0))], out_specs=pl.BlockSpec((tr, N), lambda i: (i, 0)),
                  dims=("parallel",))(q)


NATIVE = (("b_re", 16, 1024), ("b_im", 16, 1024), ("c_re", 16, 1024), ("c_im", 16, 1024), ("g_mix", 1, 1024),
          ("b_att", 1, 1024), ("b_ssm", 1, 1024), ("a_re", 1, 1024), ("a_im", 1, 1024), ("log_dt", 1, 128),
          ("d_skip", 1, 256), ("b_glu", 1, 256), ("g_ffn", 1, 1024), ("g_final", 1, 1024), ("b_conv", 1, 2048),
          ("w_conv", 3, 2048), ("loss", 1, 1))
N_MOD = 6
NATIVE_LATE = ("g_mix",)
MODS_LATE = (0, 1)


def _small_plan(late):
    pieces = [p for p in NATIVE if (p[0] in NATIVE_LATE) == late]
    mods = [k for k in range(N_MOD) if (k in MODS_LATE) == late]
    starts, r = {}, 0
    for name, rows, cols in pieces:
        starts[name] = r
        r += rows * (-(-cols // LANES))
    return pieces, mods, starts, -(-r // 8) * 8


def _pack_small(native, dmods, late):
    pieces, mods, starts, n_sum = _small_plan(late)
    B = dmods[mods[0]].shape[0]
    total = n_sum + 8 * len(mods)

    def body(*refs):
        xs, ms, o_ref = refs[:len(pieces)], refs[len(pieces):-1], refs[-1]
        o_ref[...] = jnp.zeros_like(o_ref)
        for (name, rows, cols), x_ref in zip(pieces, xs):
            chunks = -(-cols // LANES)
            if chunks == 1 and rows % 8 == 0:
                o_ref[starts[name]:starts[name] + rows, 0:cols] = x_ref[...]
                continue
            for i in range(rows):
                for q in range(chunks):
                    wd = min(LANES, cols - q * LANES)
                    r = starts[name] + i * chunks + q
                    o_ref[r:r + 1, 0:wd] = x_ref[i:i + 1, q * LANES:q * LANES + wd]
        for k, m_ref in enumerate(ms):
            for b in range(B):
                o_ref[n_sum + 8 * k + b:n_sum + 8 * k + b + 1, :] = m_ref[b]

    return _pcall(body, name="pack_small_late" if late else "pack_small_early", out_shape=_sds((total, LANES), F32))(
        *[native[n] for n, _, _ in pieces], *[dmods[k] for k in mods])


def _sum_unpack_small(gathered_early, gathered_late, B):
    plans = [_small_plan(False), _small_plan(True)]
    nd = gathered_early.shape[0]
    n_out = len(NATIVE)

    def body(*refs):
        g_refs, outs, dm_ref, accs = refs[0:2], refs[2:2 + n_out], refs[2 + n_out], refs[3 + n_out:]
        o = 0
        for g_ref, acc, (pieces, mods, starts, n_sum) in zip(g_refs, accs, plans):
            s = g_ref[0, 0:n_sum, :]
            for d in range(1, nd):
                s = s + g_ref[d, 0:n_sum, :]
            acc[...] = s
            for name, rows, cols in pieces:
                o_ref = outs[o]
                o += 1
                chunks = -(-cols // LANES)
                if chunks == 1 and rows % 8 == 0:
                    o_ref[...] = acc[starts[name]:starts[name] + rows, 0:cols]
                    continue
                for i in range(rows):
                    for q in range(chunks):
                        wd = min(LANES, cols - q * LANES)
                        r = starts[name] + i * chunks + q
                        o_ref[i:i + 1, q * LANES:q * LANES + wd] = acc[r:r + 1, 0:wd]
            for d in range(nd):
                for j, k in enumerate(mods):
                    dm_ref[d, :, k * D_MODEL:(k + 1) * D_MODEL] = g_ref[d, n_sum + 8 * j:n_sum + 8 * j + B, :]

    ordered = [p for pieces, _, _, _ in plans for p in pieces]
    out_shape = tuple(_sds((rows, cols), F32) for _, rows, cols in ordered) + (_sds((nd, B, N_MOD * D_MODEL), F32),)
    res = _pcall(body, name="sum_unpack_small", out_shape=out_shape,
                 scratch_shapes=[pltpu.VMEM((n_sum, LANES), F32) for _, _, _, n_sum in plans])(gathered_early, gathered_late)
    return {n: r for (n, _, _), r in zip(ordered, res[:-1])}, res[-1]


def _small_from_native(nat):
    lanes3 = lambda a: a.reshape(SSM_GROUP_CH, SSM_GROUPS, SSM_STATE)
    return dict(
        g_mix=nat["g_mix"].reshape(D_MODEL), b_gate=jnp.concatenate([nat["b_att"], nat["b_ssm"]], axis=1).reshape(2 * D_MODEL),
        a_re=nat["a_re"].reshape(SSM_GROUPS, SSM_STATE), a_im=nat["a_im"].reshape(SSM_GROUPS, SSM_STATE),
        log_dt=nat["log_dt"][0, :SSM_GROUPS], b_re=_groups_from_lanes(nat["b_re"]), b_im=_groups_from_lanes(nat["b_im"]),
        c_re=lanes3(nat["c_re"]).transpose(1, 0, 2), c_im=lanes3(nat["c_im"]).transpose(1, 0, 2),
        d_skip=nat["d_skip"].reshape(SSM_WIDTH), b_glu=nat["b_glu"].reshape(SSM_WIDTH), g_ffn=nat["g_ffn"].reshape(D_MODEL),
        w_conv=nat["w_conv"], b_conv=nat["b_conv"].reshape(D_FF), g_final=nat["g_final"].reshape(D_MODEL))


def _adamw_multi(params):
    n = len(params)
    bc1 = 1.0 - ADAM_B1 ** ADAM_STEP
    bc2 = 1.0 - ADAM_B2 ** ADAM_STEP

    def body(*refs):
        ins, outs = refs[:4 * n], refs[4 * n:]
        for i in range(n):
            w_ref, g_ref, m_ref, v_ref = ins[4 * i:4 * i + 4]
            d_ref, nm_ref, nv_ref = outs[3 * i:3 * i + 3]
            g = g_ref[...]
            m = ADAM_B1 * m_ref[...] + (1.0 - ADAM_B1) * g
            v = ADAM_B2 * v_ref[...] + (1.0 - ADAM_B2) * (g * g)
            nm_ref[...] = m
            nv_ref[...] = v
            d_ref[...] = -ADAM_LR * ((m / bc1) / (jnp.sqrt(v / bc2) + ADAM_EPS) + ADAM_WD * w_ref[...])

    flat = [a for p in params for a in p]
    out_shape = tuple(_sds(p[0].shape, F32) for p in params for _ in range(3))
    res = _pcall(body, name="adamw_small", out_shape=out_shape)(*flat)
    return [tuple(res[3 * i:3 * i + 3]) for i in range(n)]


def _adamw(w, g, m, v, name, g_other=None):
    R, N = w.shape
    tr = _rows_tile(R, 256)
    spec = pl.BlockSpec((tr, N), lambda i: (i, 0))
    bc1 = 1.0 - ADAM_B1 ** ADAM_STEP
    bc2 = 1.0 - ADAM_B2 ** ADAM_STEP
    two = g_other is not None

    def body(*refs):
        w_ref, g_ref, m_ref, v_ref = refs[:4]
        d_ref, nm_ref, nv_ref = refs[4 + two:7 + two]
        g = g_ref[...]
        if two:
            g = g + refs[4][...]
            refs[8][...] = g
        m = ADAM_B1 * m_ref[...] + (1.0 - ADAM_B1) * g
        v = ADAM_B2 * v_ref[...] + (1.0 - ADAM_B2) * (g * g)
        nm_ref[...] = m
        nv_ref[...] = v
        d_ref[...] = -ADAM_LR * ((m / bc1) / (jnp.sqrt(v / bc2) + ADAM_EPS) + ADAM_WD * w_ref[...])

    shp = _sds((R, N), F32)
    args = (w, g, m, v) + ((g_other,) if two else ())
    return _pcall(body, name=name, out_shape=(shp,) * (3 + two), grid=(R // tr,), in_specs=[spec] * len(args),
                  out_specs=(spec,) * (3 + two), dims=("parallel",))(*args)


_GROUP_MASKS = {
    "all": [(dx, dy, dc) for dx in (0, 1) for dy in (0, 1) for dc in (0, 1) if (dx, dy, dc) != (0, 0, 0)],
    "xy": [(1, 0, 0), (0, 1, 0), (1, 1, 0)],
    "c": [(0, 0, 1)],
}
_GROUP_SLOTS = {"all": 8, "xy": 4, "c": 2}


def _group_slot(group, x, y, c):
    return {"all": 4 * x + 2 * y + c, "xy": 2 * x + y, "c": c}[group]


def _flip(v, d):
    return 1 - v if d else v


def _exchange(arr, group, mode, name):
    return _exchange_list([arr], group, mode, name)[0]


def _exchange_list(arrs, group, mode, name):
    masks = _GROUP_MASKS[group]
    n = len(masks)
    na = len(arrs)
    assert mode in ("gather", "swap") and (mode == "gather" or group == "c")
    has_local = mode == "gather"
    out_shapes = [((_GROUP_SLOTS[group],) if has_local else ()) + arr.shape for arr in arrs]
    bounce = [pltpu.VMEM(arr.shape, arr.dtype) for arr in arrs] if has_local else []

    def body(*refs):
        x_refs, o_refs = refs[:na], refs[na:2 * na]
        send_sems, recv_sems = refs[2 * na], refs[2 * na + 1]
        x, y, c = lax.axis_index("x"), lax.axis_index("y"), lax.axis_index("c")
        me = _group_slot(group, x, y, c)
        if has_local:
            local_sems = refs[2 * na + 2]
            bufs = refs[2 * na + 3:]
            loads = []
            for i in range(na):
                loads.append(pltpu.make_async_copy(x_refs[i], bufs[i], local_sems.at[2 * i]))
                loads[-1].start()
        copies = []
        for i in range(na):
            x_ref, o_ref = x_refs[i], o_refs[i]
            for k, (dx, dy, dc) in enumerate(masks):
                px, py, pc = _flip(x, dx), _flip(y, dy), _flip(c, dc)
                src, dst = (x_ref, o_ref.at[me]) if has_local else (x_ref, o_ref)
                cp =pltpu.make_async_remote_copy(src_ref=src, dst_ref=dst, send_sem=send_sems.at[i * n + k],
                                                  recv_sem=recv_sems.at[i * n + k], device_id=(px, py, pc),
                                                  device_id_type=pl.DeviceIdType.MESH)
                cp.start()
                copies.append(cp)
        if has_local:
            stores = []
            for i in range(na):
                loads[i].wait()
                stores.append(pltpu.make_async_copy(bufs[i], o_refs[i].at[me], local_sems.at[2 * i + 1]))
                stores[-1].start()
        for cp in copies:
            cp.wait()
        if has_local:
            for st in stores:
                st.wait()

    anyspec = pl.BlockSpec(memory_space=pl.ANY)
    scratch = [pltpu.SemaphoreType.DMA((n * na,)), pltpu.SemaphoreType.DMA((n * na,))]
    if has_local:
        scratch += [pltpu.SemaphoreType.DMA((2 * na,))] + bounce
    outs = pl.pallas_call(body, name=name, out_shape=tuple(_sds(s, a.dtype) for s, a in zip(out_shapes, arrs)),
                          in_specs=[anyspec] * na, out_specs=tuple([anyspec] * na), scratch_shapes=scratch,
                          compiler_params=pltpu.CompilerParams(vmem_limit_bytes=V7X_VMEM_LIMIT_BYTES))(*arrs)
    return list(outs)


def _gather_weights(shards, name):
    na = len(shards)
    masks = _GROUP_MASKS["xy"]
    n = len(masks)

    def body(*refs):
        x_refs, o_refs = refs[:na], refs[na:2 * na]
        send_sems, recv_sems, local_sems = refs[2 * na:2 * na + 3]
        bufs = refs[2 * na + 3:]
        x, y, c = lax.axis_index("x"), lax.axis_index("y"), lax.axis_index("c")
        me = 2 * x + y
        sibling = (x, y, 1 - c)
        loads = []
        for i in range(na):
            loads.append(pltpu.make_async_copy(x_refs[i], bufs[i], local_sems.at[2 * i]))
            loads[-1].start()

        def half_of(i, slot, cc):
            h = shards[i].shape[0] // 2
            return o_refs[i].at[slot, pl.ds(pl.multiple_of(cc * h, 8), h), :]

        def src_half(i, cc):
            h = shards[i].shape[0] // 2
            return x_refs[i].at[pl.ds(pl.multiple_of(cc * h, 8), h), :]

        sends = []
        for i in range(na):
            for k, (dx, dy, _) in enumerate(masks):
                cp = pltpu.make_async_remote_copy(src_ref=src_half(i, c), dst_ref=half_of(i, me, c),
                                                  send_sem=send_sems.at[i * 2 * n + k], recv_sem=recv_sems.at[i * 2 * n + k],
                                                  device_id=(_flip(x, dx), _flip(y, dy), c),
                                                  device_id_type=pl.DeviceIdType.MESH)
                cp.start()
                sends.append(cp)
        stores = []
        for i in range(na):
            loads[i].wait()
            stores.append(pltpu.make_async_copy(bufs[i], o_refs[i].at[me], local_sems.at[2 * i + 1]))
            stores[-1].start()
        for i in range(na):
            for k, (dx, dy, _) in enumerate(masks):
                slot = 2 * _flip(x, dx) + _flip(y, dy)
                landed = pltpu.make_async_remote_copy(src_ref=src_half(i, c), dst_ref=half_of(i, slot, c),
                                                      send_sem=send_sems.at[i * 2 * n + k],
                                                      recv_sem=recv_sems.at[i * 2 * n + k], device_id=sibling,
                                                      device_id_type=pl.DeviceIdType.MESH)
                landed.wait_recv()
                fwd = pltpu.make_async_remote_copy(src_ref=half_of(i, slot, c), dst_ref=half_of(i, slot, c),
                                                   send_sem=send_sems.at[i * 2 * n + n + k],
                                                   recv_sem=recv_sems.at[i * 2 * n + n + k], device_id=sibling,
                                                   device_id_type=pl.DeviceIdType.MESH)
                fwd.start()
                sends.append(fwd)
        for i in range(na):
            for k, (dx, dy, _) in enumerate(masks):
                slot = 2 * _flip(x, dx) + _flip(y, dy)
                pltpu.make_async_remote_copy(src_ref=half_of(i, slot, 1 - c), dst_ref=half_of(i, slot, 1 - c),
                                             send_sem=send_sems.at[i * 2 * n + n + k],
                                             recv_sem=recv_sems.at[i * 2 * n + n + k], device_id=sibling,
                                             device_id_type=pl.DeviceIdType.MESH).wait_recv()
        for cp in sends:
            cp.wait_send()
        for st in stores:
            st.wait()

    anyspec = pl.BlockSpec(memory_space=pl.ANY)
    scratch = [pltpu.SemaphoreType.DMA((2 * n * na,)), pltpu.SemaphoreType.DMA((2 * n * na,)),
               pltpu.SemaphoreType.DMA((2 * na,))] + [pltpu.VMEM(s.shape, s.dtype) for s in shards]
    outs = pl.pallas_call(body, name=name, out_shape=tuple(_sds((N_XY,) + s.shape, s.dtype) for s in shards),
                          in_specs=[anyspec] * na, out_specs=tuple([anyspec] * na), scratch_shapes=scratch,
                          compiler_params=pltpu.CompilerParams(vmem_limit_bytes=V7X_VMEM_LIMIT_BYTES))(*shards)
    return list(outs)


BIG = (("w_proj_att", (ATT_WIDTH, D_MODEL), 1), ("w_proj_ssm", (SSM_WIDTH, D_MODEL), 1),
       ("w_glu", (SSM_WIDTH, SSM_WIDTH), 0))
DIRECT = (("w_in", True), ("w_up", True), ("w_down", False), ("w_out", False))
N_XY = 4


def _big_rows(shape):
    return shape[0] * shape[1] // N_XY // LANES


FLAT_ROWS = sum(_big_rows(s) for _, s, _ in BIG)


def _shard_shape(shape, axis):
    return (shape[0] // N_XY, shape[1]) if axis == 0 else (shape[0], shape[1] // N_XY)


def _flatten_shards(shards):
    return jnp.concatenate([shards[n].reshape(_big_rows(s), LANES) for n, s, _ in BIG], axis=0)


def _unflatten_shard(flat):
    out, r = {}, 0
    for n, s, ax in BIG:
        k = _big_rows(s)
        out[n] = flat[r:r + k].reshape(_shard_shape(s, ax))
        r += k
    return out


def _unflatten_full(flat4):
    out, r = {}, 0
    for n, s, ax in BIG:
        k = _big_rows(s)
        sh = _shard_shape(s, ax)
        t = flat4[:, r:r + k].reshape((N_XY,) + sh)
        out[n] = t.reshape(s) if ax == 0 else t.transpose(1, 0, 2).reshape(s)
        r += k
    return out


def _flatten_full(full):
    parts = []
    for n, s, ax in BIG:
        sh = _shard_shape(s, ax)
        t = full[n]
        t = t.reshape((N_XY,) + sh) if ax == 0 else t.reshape(s[0], N_XY, sh[1]).transpose(1, 0, 2)
        parts.append(t.reshape(N_XY, _big_rows(s), LANES))
    return jnp.concatenate(parts, axis=1)


def _lanes_from_groups(a):
    return a.transpose(2, 0, 1).reshape(SSM_GROUP_CH, SSM_LANES)


def _groups_from_lanes(a):
    return a.reshape(SSM_GROUP_CH, SSM_GROUPS, SSM_STATE).transpose(1, 2, 0)


LATE = ("w_up_t", "w_down", "w_out")
EARLY_GRADS = ("w_up_t", "w_down", "w_out")


def _local_step(x3, mod, tgt3, W, P, late_shards=None, scatter_grads=False):
    B, S, _ = x3.shape
    T = B * S
    seq_blocks = S // ATT_BLOCK
    sh1, sc1, gt1, sh2, sc2, gt2 = [m.reshape(B, 1, D_MODEL) for m in jnp.split(mod, 6, axis=-1)]
    g_mix, g_ffn, g_final = P["g_mix"].reshape(1, D_MODEL), P["g_ffn"].reshape(1, D_MODEL), P["g_final"].reshape(1, D_MODEL)
    b_gate = P["b_gate"].reshape(1, 2 * D_MODEL)
    d_skip, b_glu = P["d_skip"].reshape(1, SSM_WIDTH), P["b_glu"].reshape(1, SSM_WIDTH)
    w_conv, b_conv = P["w_conv"], P["b_conv"].reshape(1, D_FF)

    u1 = _norm_mod(x3, g_mix, sc1, sh1).reshape(T, D_MODEL)
    proj = _mm(u1, W["w_in_t"], tb=True, name="mm_proj", out_dtype=BF16)
    proj3 = proj.reshape(B, S, IN_WIDTH)
    us = proj[:, 3 * ATT_WIDTH:3 * ATT_WIDTH + SSM_WIDTH]
    o_att3, lse4, late = _attention_fwd(proj3, seq_blocks, _Riders(late_shards, "gather") if late_shards else None)
    if late_shards:
        W = dict(W, **{n: f.reshape(-1, LANES) for n, f in zip(LATE, late)})
        w_conv = late[len(LATE)].transpose(1, 0, 2).reshape(3, D_FF)
        W.update(_unflatten_full(late[len(LATE) + 1]))
    o_att = o_att3.reshape(T, ATT_WIDTH)
    y_att = _mm(o_att, W["w_proj_att"], name="mm_proj_att", out_dtype=BF16)

    lr = P["a_re"].reshape(1, SSM_LANES)
    li = P["a_im"].reshape(1, SSM_LANES)
    ldt = jnp.repeat(P["log_dt"], SSM_STATE).reshape(1, SSM_LANES)
    br, bi = _lanes_from_groups(P["b_re"]), _lanes_from_groups(P["b_im"])
    cr = P["c_re"].transpose(1, 0, 2).reshape(SSM_GROUP_CH, SSM_LANES)
    ci = P["c_im"].transpose(1, 0, 2).reshape(SSM_GROUP_CH, SSM_LANES)
    abar, w_bu, w_c = _ssm_params(lr, li, ldt, br, bi, cr, ci)
    xs3, y_core3 = _ssm_scan_fwd(proj3, abar, w_bu, w_c)
    y5, s_out = _ssm_post(y_core3.reshape(T, SSM_WIDTH), us, d_skip, W["w_glu"], b_glu)
    y_ssm = _mm(s_out, W["w_proj_ssm"], name="mm_proj_ssm", out_dtype=BF16)

    merged = _merge(proj, y_att, y_ssm, b_gate)
    mix = _mm(merged, W["w_out"], name="mm_out", out_dtype=BF16)
    mix3 = mix.reshape(B, S, D_MODEL)

    h1, u2 = _resid_norm_mod(x3, mix3, gt1, g_ffn, sc2, sh2)
    u2 = u2.reshape(T, D_MODEL)
    up3 = _mm(u2, W["w_up_t"], tb=True, name="mm_up", out_dtype=BF16).reshape(B, S, 2 * D_FF)
    act = _conv_act(up3, w_conv, b_conv).reshape(T, D_FF)
    ffn3 = _mm(act, W["w_down"], name="mm_down", out_dtype=BF16).reshape(B, S, D_MODEL)
    dh2, dffn, dgt2, dg_final, loss = _final_loss(h1, ffn3, tgt3, gt2, g_final)

    dffn = dffn.reshape(T, D_MODEL)
    gw = {}
    gw["w_down"] = _mm(act, dffn, ta=True, out_dtype=BF16, name="mm_dw_down")
    dact3 = _mm(dffn, W["w_down"], tb=True, name="mm_dact", out_dtype=BF16).reshape(B, S, D_FF)
    dup3, dw_conv, db_conv = _conv_bwd(up3, dact3, w_conv, b_conv)
    dup = dup3.reshape(2, T, D_FF)
    gw["w_up_t"] = _mm(dup, u2, ta=True, out_dtype=BF16, name="mm_dw_up")
    du2 = _mm(dup, W["w_up_t"], name="mm_du2", out_dtype=BF16).reshape(B, S, D_MODEL)
    dh1, dsh2, dsc2, dg_ffn, dgt1, dmix = _norm_bwd(h1, du2, dh2, g_ffn, sc2, "norm_bwd2", mix3=mix3, gt=gt1)

    dmix = dmix.reshape(T, D_MODEL)
    gw["w_out"] = _mm(merged, dmix, ta=True, out_dtype=BF16, name="mm_dw_out")
    dmerged = _mm(dmix, W["w_out"], tb=True, name="mm_dmerged", out_dtype=BF16)
    dy_att, dy_ssm, dga, dgs, db_att, db_ssm = _merge_bwd(proj, y_att, y_ssm, b_gate, dmerged)

    gw["w_proj_ssm"] = _mm(s_out, dy_ssm, ta=True, name="mm_dw_proj_ssm")
    ds_out = _mm(dy_ssm, W["w_proj_ssm"], tb=True, name="mm_ds_out")
    dy5, dd_skip, db_glu, dw_glu = _ssm_post_bwd(y5, us, ds_out, d_skip, W["w_glu"], b_glu)
    gw["w_glu"] = dw_glu
    dus3, dab, dwbu, dwc = _ssm_scan_bwd(proj3, dy5.reshape(B, S, SSM_WIDTH), xs3, abar, w_bu, w_c, d_skip)
    dus = dus3.reshape(T, SSM_WIDTH)
    dlr, dli, dldt, dbr, dbi, dcr, dci = _ssm_params_bwd(lr, li, ldt, br, bi, dab, dwbu, dwc)

    gw["w_proj_att"] = _mm(o_att, dy_att, ta=True, name="mm_dw_proj_att")
    do_att = _mm(dy_att, W["w_proj_att"], tb=True, out_dtype=BF16, name="mm_do_att")
    early = [gw[n].reshape(N_XY, -1, LANES) for n in EARLY_GRADS]
    early.append(_flatten_full({n: gw[n].astype(BF16) for n, _, _ in BIG}))
    dq3, dk3, dv3, parts = _attention_bwd(proj3, do_att.reshape(B, S, ATT_WIDTH), o_att3, lse4, seq_blocks,
                                          _Riders(early, "scatter") if scatter_grads else None)
    dproj = jnp.concatenate([t.reshape(T, ATT_WIDTH) for t in (dq3, dk3, dv3)] + [dus, dga, dgs], axis=1)
    dmods = [None, None, dgt1, dsh2, dsc2, dgt2]
    native = dict(b_att=db_att, b_ssm=db_ssm, a_re=dlr, a_im=dli, log_dt=dldt, b_re=dbr, b_im=dbi, c_re=dcr, c_im=dci,
                  d_skip=dd_skip, b_glu=db_glu, g_ffn=dg_ffn, w_conv=dw_conv, b_conv=db_conv, g_final=dg_final, loss=loss)
    small_early = _pack_small(native, dmods, False)
    sums, sums_sib, last_parts = [], [], []
    if scatter_grads:
        sums = [_sum_slots(p, "sum_chips_%d" % i) for i, p in enumerate(parts)]
        riders = _RiderGroup([_Riders([small_early], "gather", "all"), _Riders(sums, "swap", "c")])
        gw["w_in_t"], rode = _mm(dproj, u1, ta=True, out_dtype=BF16, name="mm_dw_in", riders=riders)
        small_early, sums_sib = rode[0], rode[1:]
        du1, last_parts = _mm(dproj, W["w_in_t"], name="mm_du1", out_dtype=BF16,
                              riders=_Riders([gw["w_in_t"].reshape(N_XY, -1, LANES)], "scatter"))
    else:
        gw["w_in_t"] = _mm(dproj, u1, ta=True, out_dtype=BF16, name="mm_dw_in")
        du1 = _mm(dproj, W["w_in_t"], name="mm_du1", out_dtype=BF16)
    du1 = du1.reshape(B, S, D_MODEL)
    dx, dsh1, dsc1, dg_mix = _norm_bwd(x3, du1, dh1, g_mix, sc1, "norm_bwd1")
    dmods[0], dmods[1] = dsh1, dsc1
    native["g_mix"] = dg_mix
    return loss, dx, dmods, gw, native, (sums, sums_sib, last_parts), small_early


WEIGHTS = ['w_ada', 'b_ada', 'g_mix', 'w_in', 'b_gate', 'a_re', 'a_im', 'log_dt', 'b_re', 'b_im', 'c_re', 'c_im', 'd_skip',
           'w_glu', 'b_glu', 'w_proj_att', 'w_proj_ssm', 'w_out', 'g_ffn', 'w_up', 'w_conv', 'b_conv', 'w_down', 'g_final']
SMALL = ['g_mix', 'b_gate', 'a_re', 'a_im', 'log_dt', 'b_re', 'b_im', 'c_re', 'c_im', 'd_skip', 'b_glu', 'g_ffn', 'w_conv',
         'b_conv', 'g_final']


def kernel(x, c, w_ada, b_ada, g_mix, w_in, b_gate, a_re, a_im, log_dt, b_re, b_im, c_re, c_im, d_skip, w_glu, b_glu, w_proj_att, w_proj_ssm, w_out, g_ffn, w_up, w_conv, b_conv, w_down, g_final, loss_target, m_w_ada, m_b_ada, m_g_mix, m_w_in, m_b_gate, m_a_re, m_a_im, m_log_dt, m_b_re, m_b_im, m_c_re, m_c_im, m_d_skip, m_w_glu, m_b_glu, m_w_proj_att, m_w_proj_ssm, m_w_out, m_g_ffn, m_w_up, m_w_conv, m_b_conv, m_w_down, m_g_final, v_w_ada, v_b_ada, v_g_mix, v_w_in, v_b_gate, v_a_re, v_a_im, v_log_dt, v_b_re, v_b_im, v_c_re, v_c_im, v_d_skip, v_w_glu, v_b_glu, v_w_proj_att, v_w_proj_ssm, v_w_out, v_g_ffn, v_w_up, v_w_conv, v_b_conv, v_w_down, v_g_final):
    args = dict(locals())
    w = {n: args[n] for n in WEIGHTS}
    m = {n: args["m_" + n] for n in WEIGHTS}
    v = {n: args["v_" + n] for n in WEIGHTS}
    B, S, _ = x.shape
    ix, iy, ic = lax.axis_index("x"), lax.axis_index("y"), lax.axis_index("c")
    chip = 2 * ix + iy
    ada_cols = w_ada.shape[2]

    c_all = _exchange(c, "all", "gather", "gather_c").reshape(8 * B, D_MODEL)
    b_cols = lax.dynamic_slice_in_dim(b_ada, chip * ada_cols, ada_cols, axis=1)
    mod_cols = _ada_fwd(c_all, w_ada[0], b_cols)
    mod_all = _exchange(mod_cols, "xy", "gather", "gather_mod")
    mod_all = mod_all.transpose(1, 0, 2).reshape(8 * B, 6 * D_MODEL)
    mod = lax.dynamic_slice_in_dim(mod_all, (4 * ix + 2 * iy + ic) * B, B, axis=0)

    shard = {n + ("_t" if t else ""): (w[n][0].T if t else w[n][0]).astype(BF16) for n, t in DIRECT}
    misc = _flatten_shards({n: w[n][0] for n, _, _ in BIG}).astype(BF16)
    (w_in_full,) = _gather_weights([shard["w_in_t"]], "gather_weights")
    W = {"w_in_t": w_in_full.reshape(-1, LANES)}

    P = {n: w[n][0] for n in SMALL if n not in ("w_conv", "g_final")}
    P["w_conv"] = None
    P["g_final"] = g_final

    loss, dx, dmods, gw, native, parts, small_early = _local_step(x, mod, loss_target, W, P,
                                                                  [shard[n] for n in LATE] + [w_conv[0], misc], True)

    small_late = _exchange(_pack_small(native, dmods, True), "all", "gather", "gather_small")
    native_sum, dmod_all = _sum_unpack_small(small_early, small_late, B)
    loss = native_sum["loss"][0, 0]
    g_small = _small_from_native(native_sum)
    dmod_all = dmod_all.reshape(8 * B, N_MOD * D_MODEL)
    dmod_cols = lax.dynamic_slice_in_dim(dmod_all, chip * ada_cols, ada_cols, axis=1)
    g_w_ada, g_b_ada = _ada_bwd(c_all, dmod_all, dmod_cols)

    red, red_sib, last_parts = parts
    red = red + [_sum_slots(last_parts[0], "sum_chips_w_in")]
    red_sib = red_sib + [_exchange(red[-1], "c", "swap", "share_cores")]
    order = list(EARLY_GRADS) + ["misc", "w_in_t"]
    halves = dict(zip(order, zip(red, red_sib)))

    grads = {"w_ada": g_w_ada[None], "b_ada": g_b_ada}
    grads["w_up"] = _add2(*halves["w_up_t"], F32, "add_cores_w_up").T[None]
    for k, gk in _unflatten_shard(_add2(*halves["misc"], F32, "add_cores_misc")).items():
        grads[k] = gk[None]
    wc_cols = w_conv.shape[2]
    for n in SMALL:
        g = g_small[n]
        if n == "w_conv":
            g = lax.dynamic_slice_in_dim(g, chip * wc_cols, wc_cols, axis=1)
        grads[n] = g.reshape(w[n].shape)

    delta, new_m, new_v = {}, {}, {}
    for n in ["w_ada"] + [b for b, _ in DIRECT] + [b for b, _, _ in BIG]:
        shp = w[n].shape
        if n == "w_in":
            r, s = halves["w_in_t"]
            d2, m2, v2, g2 = _adamw(w[n][0].T, r, m[n][0].T, v[n][0].T, "adamw_" + n, g_other=s)
            d2, m2, v2, grads[n] = d2.T, m2.T, v2.T, g2.T[None]
        elif n in ("w_down", "w_out"):
            r, s = halves[n]
            d2, m2, v2, g2 = _adamw(w[n][0], r, m[n][0], v[n][0], "adamw_" + n, g_other=s)
            grads[n] = g2[None]
        else:
            d2, m2, v2 = _adamw(w[n][0], grads[n][0], m[n][0], v[n][0], "adamw_" + n)
        delta[n], new_m[n], new_v[n] = d2.reshape(shp), m2.reshape(shp), v2.reshape(shp)
    rest = ["b_ada"] + SMALL

    def drop(a):
        return a.reshape(1, -1) if a.ndim == 1 else (a if a.ndim == 2 else a[0])

    upd = _adamw_multi([(drop(w[n]), drop(grads[n]), drop(m[n]), drop(v[n])) for n in rest])
    for n, (dd, mm, vv) in zip(rest, upd):
        delta[n], new_m[n], new_v[n] = dd.reshape(w[n].shape), mm.reshape(w[n].shape), vv.reshape(w[n].shape)

    return (loss, dx, *[grads[n] for n in WEIGHTS], *[delta[n] for n in WEIGHTS], *[new_m[n] for n in WEIGHTS],
            *[new_v[n] for n in WEIGHTS])
```

```python
import functools
import math

import jax
import jax.numpy as jnp
from jax import lax
from jax.experimental import pallas as pl
from jax.experimental.pallas import tpu as pltpu

F32, BF16 = jnp.float32, jnp.bfloat16

D_MODEL = 1024
N_HEADS = 8
HEAD_DIM = 64
ATT_WIDTH = 512
SSM_GROUPS = 16
SSM_GROUP_CH = 16
SSM_WIDTH = 256
SSM_STATE = 64
SSM_LANES = SSM_GROUPS * SSM_STATE
D_FF = 2048
IN_WIDTH = 3 * ATT_WIDTH + SSM_WIDTH + 2 * D_MODEL
ATT_BLOCK = 128
N_PATTERNS = 3
EPS = 1e-6
NEG_INF = -1e30

ADAM_LR, ADAM_B1, ADAM_B2, ADAM_EPS, ADAM_WD, ADAM_STEP = 0.001, 0.9, 0.999, 1e-08, 0.01, 10

V7X_VMEM_LIMIT_BYTES = 56 * 1024 * 1024
LANES = 1024


def _pcall(body, *, name, out_shape, grid=(), in_specs=None, out_specs=None, scratch_shapes=(), dims=None):
    params = dict(vmem_limit_bytes=V7X_VMEM_LIMIT_BYTES)
    if dims is not None:
        params["dimension_semantics"] = dims
    specs = {}
    if in_specs is not None:
        specs = dict(grid=grid, in_specs=in_specs, out_specs=out_specs)
    return pl.pallas_call(body, name=name, out_shape=out_shape, scratch_shapes=scratch_shapes,
                          compiler_params=pltpu.CompilerParams(**params), **specs)


def _sds(shape, dtype):
    return jax.ShapeDtypeStruct(tuple(shape), dtype)


def _tile(n, target):
    if n <= target:
        return n
    for t in range(target - target % 128, 0, -128):
        if n % t == 0:
            return t
    raise ValueError((n, target))


def _sig(v):
    return pl.reciprocal(1.0 + jnp.exp(-v), approx=True)


def _mm(a, b, *, name, ta=False, tb=False, out_dtype=F32, tm=2048, tn=1024, tk=1024, riders=None):
    halves = a.ndim == 3
    if halves:
        a_rows, a_cols = a.shape[1], 2 * a.shape[2]
    else:
        a_rows, a_cols = a.shape
    if ta:
        K, M = a_rows, a_cols
    else:
        M, K = a_rows, a_cols
    if tb:
        N, K2 = b.shape
    else:
        K2, N = b.shape
    assert K == K2, (a.shape, b.shape)
    if halves:
        tm, tk = (min(tm, M // 2), tk) if ta else (tm, min(tk, K // 2))
    tm, tn, tk = _tile(M, tm), _tile(N, tn), _tile(K, tk)
    nk = K // tk
    if halves and ta:
        per = a.shape[2] // tm
        a_spec = pl.BlockSpec((None, tk, tm), lambda i, j, k: (i // per, k, i % per))
    elif halves:
        per = a.shape[2] // tk
        a_spec = pl.BlockSpec((None, tm, tk), lambda i, j, k: (k // per, i, k % per))
    else:
        a_spec = pl.BlockSpec((tk, tm), lambda i, j, k: (k, i)) if ta else pl.BlockSpec((tm, tk), lambda i, j, k: (i, k))
    b_spec = pl.BlockSpec((tn, tk), lambda i, j, k: (j, k)) if tb else pl.BlockSpec((tk, tn), lambda i, j, k: (k, j))
    dn = (((0 if ta else 1,), (1 if tb else 0,)), ((), ()))

    def body(a_ref, b_ref, o_ref, acc_ref):
        k = pl.program_id(2)

        @pl.when(k == 0)
        def _():
            acc_ref[...] = jnp.zeros_like(acc_ref)

        acc_ref[...] += lax.dot_general(a_ref[...].astype(BF16), b_ref[...].astype(BF16), dn,
                                        preferred_element_type=F32)

        @pl.when(k == nk - 1)
        def _():
            o_ref[...] = acc_ref[...].astype(out_dtype)

    def body_single(a_ref, b_ref, o_ref):
        o_ref[...] = lax.dot_general(a_ref[...].astype(BF16), b_ref[...].astype(BF16), dn,
                                     preferred_element_type=F32).astype(out_dtype)

    grid = (M // tm, N // tn, nk)
    scratch = [] if nk == 1 else [pltpu.VMEM((tm, tn), F32)]
    o_spec = pl.BlockSpec((tm, tn), lambda i, j, k: (i, j))
    if riders is None:
        return _pcall(body_single if nk == 1 else body, name=name, out_shape=_sds((M, N), out_dtype), grid=grid,
                      in_specs=[a_spec, b_spec], out_specs=o_spec, scratch_shapes=scratch,
                      dims=("parallel", "parallel", "arbitrary"))(a, b)
    rs = riders
    res = _pcall(_with_riders(body_single if nk == 1 else body, rs, 2, 1, len(scratch), tuple(g - 1 for g in grid)),
                 name=name, out_shape=(_sds((M, N), out_dtype),) + tuple(rs.out_shape), grid=grid,
                 in_specs=[a_spec, b_spec] + rs.specs, out_specs=(o_spec,) + tuple(rs.specs),
                 scratch_shapes=scratch + rs.scratch, dims=("arbitrary", "arbitrary", "arbitrary"))(a, b, *rs.arrs)
    return res[0], list(res[1:])


def _ada_fwd(c_all, w_ada, b_ada_cols):
    n = w_ada.shape[1]

    def body(c_ref, w_ref, b_ref, o_ref):
        c = c_ref[...]
        act = c * _sig(c)
        o_ref[...] = jnp.dot(act.astype(BF16), w_ref[...].astype(BF16), preferred_element_type=F32) + b_ref[...]

    return _pcall(body, name="ada_fwd", out_shape=_sds((c_all.shape[0], n), F32))(c_all, w_ada, b_ada_cols)


def _ada_bwd(c_all, dmod_all, dmod_cols):
    n = dmod_cols.shape[1]

    def body(c_ref, da_ref, dc_ref, gw_ref, gb_ref):
        c = c_ref[...]
        act = c * _sig(c)
        gw_ref[...] = lax.dot_general(act, dc_ref[...], (((0,), (0,)), ((), ())), preferred_element_type=F32,
                                      precision=lax.Precision.HIGHEST)
        gb_ref[...] = jnp.sum(da_ref[...], axis=0, keepdims=True)

    return _pcall(body, name="ada_bwd", out_shape=(_sds((D_MODEL, n), F32), _sds((1, dmod_all.shape[1]), F32)))(
        c_all, dmod_all, dmod_cols)


ROW_TILE = 1024


def _row_specs(B, S):
    ts = min(S, ROW_TILE)
    row = pl.BlockSpec((1, ts, D_MODEL), lambda b, s: (b, s, 0))
    bvec = pl.BlockSpec((1, 1, D_MODEL), lambda b, s: (b, 0, 0))
    gvec = pl.BlockSpec((1, D_MODEL), lambda b, s: (0, 0))
    return ts, row, bvec, gvec


def _norm_mod(x3, g, sc, sh):
    B, S, _ = x3.shape
    ts, row, bvec, gvec = _row_specs(B, S)

    def body(x_ref, g_ref, sc_ref, sh_ref, u_ref):
        x = x_ref[0]
        r = lax.rsqrt(jnp.mean(x * x, axis=-1, keepdims=True) + EPS)
        u_ref[0] = ((x * r) * g_ref[...] * (1.0 + sc_ref[0]) + sh_ref[0]).astype(BF16)

    return _pcall(body, name="norm_mod1", out_shape=_sds(x3.shape, BF16), grid=(B, S // ts),
                  in_specs=[row, gvec, bvec, bvec], out_specs=row, dims=("parallel", "parallel"))(x3, g, sc, sh)


def _resid_norm_mod(x3, mix3, gt, g, sc, sh):
    B, S, _ = x3.shape
    ts, row, bvec, gvec = _row_specs(B, S)

    def body(x_ref, m_ref, gt_ref, g_ref, sc_ref, sh_ref, h_ref, u_ref):
        h = x_ref[0] + gt_ref[0] * m_ref[0]
        h_ref[0] = h
        r = lax.rsqrt(jnp.mean(h * h, axis=-1, keepdims=True) + EPS)
        u_ref[0] = ((h * r) * g_ref[...] * (1.0 + sc_ref[0]) + sh_ref[0]).astype(BF16)

    return _pcall(body, name="resid_norm_mod2", out_shape=(_sds(x3.shape, F32), _sds(x3.shape, BF16)),
                  grid=(B, S // ts), in_specs=[row, row, bvec, gvec, bvec, bvec], out_specs=(row, row),
                  dims=("parallel", "parallel"))(x3, mix3, gt, g, sc, sh)


def _norm_bwd(h3, du3, dres3, g, sc, name, mix3=None, gt=None, riders=None):
    B, S, _ = h3.shape
    ts, row, bvec, gvec = _row_specs(B, S)
    with_gate = mix3 is not None

    def body(*refs):
        if with_gate:
            h_ref, du_ref, dr_ref, g_ref, sc_ref, m_ref, gt_ref, dh_ref, dsh_ref, dsc_ref, dg_ref, dgt_ref, dm_ref = refs
        else:
            h_ref, du_ref, dr_ref, g_ref, sc_ref, dh_ref, dsh_ref, dsc_ref, dg_ref = refs
        b, s = pl.program_id(0), pl.program_id(1)
        h = h_ref[0]
        r = lax.rsqrt(jnp.mean(h * h, axis=-1, keepdims=True) + EPS)
        xn = h * r
        du = du_ref[0].astype(F32)
        g = g_ref[...]
        sc1 = 1.0 + sc_ref[0]
        dxn = du * g * sc1
        dh = dr_ref[0].astype(F32) + r * (dxn - xn * jnp.mean(dxn * xn, axis=-1, keepdims=True))
        dh_ref[0] = dh.astype(dh_ref.dtype)

        @pl.when(s == 0)
        def _():
            dsh_ref[...] = jnp.zeros_like(dsh_ref)
            dsc_ref[...] = jnp.zeros_like(dsc_ref)
            if with_gate:
                dgt_ref[...] = jnp.zeros_like(dgt_ref)

        @pl.when((s == 0) & (b == 0))
        def _():
            dg_ref[...] = jnp.zeros_like(dg_ref)

        dux = du * xn
        dsh_ref[0] += jnp.sum(du, axis=0, keepdims=True)
        dsc_ref[0] += jnp.sum(dux * g, axis=0, keepdims=True)
        dg_ref[...] += jnp.sum(dux * sc1, axis=0, keepdims=True)
        if with_gate:
            dgt_ref[0] += jnp.sum(dh * m_ref[0], axis=0, keepdims=True)
            dm_ref[0] = (dh * gt_ref[0]).astype(BF16)

    bshape = _sds((B, 1, D_MODEL), F32)
    in_specs = [row, row, row, gvec, bvec]
    out_shape = [_sds(h3.shape, BF16 if with_gate else F32), bshape, bshape, _sds((1, D_MODEL), F32)]
    out_specs = [row, bvec, bvec, gvec]
    args = [h3, du3, dres3, g, sc]
    if with_gate:
        in_specs += [row, bvec]
        out_shape += [bshape, _sds(h3.shape, BF16)]
        out_specs += [bvec, row]
        args += [mix3, gt]
    if riders is None:
        return _pcall(body, name=name, out_shape=tuple(out_shape), grid=(B, S // ts), in_specs=in_specs,
                      out_specs=tuple(out_specs), dims=("arbitrary", "arbitrary"))(*args)
    rs = riders
    res = _pcall(_with_riders(body, rs, len(args), len(out_shape), 0, (B - 1, S // ts - 1)), name=name,
                 out_shape=tuple(out_shape) + tuple(rs.out_shape), grid=(B, S // ts), in_specs=in_specs + rs.specs,
                 out_specs=tuple(out_specs) + tuple(rs.specs), scratch_shapes=rs.scratch,
                 dims=("arbitrary", "arbitrary"))(*args, *rs.arrs)
    return tuple(res[:len(out_shape)]) + (list(res[len(out_shape):]),)


def _final_loss(h1, ffn3, tgt3, gt, gfin):
    B, S, _ = h1.shape
    ts, row, bvec, gvec = _row_specs(B, S)
    one = pl.BlockSpec((1, 1), lambda b, s: (0, 0))

    def body(h_ref, f_ref, t_ref, gt_ref, gf_ref, dh_ref, dff_ref, dgt_ref, dgf_ref, loss_ref):
        b, s = pl.program_id(0), pl.program_id(1)
        f = f_ref[0].astype(F32)
        gtv = gt_ref[0]
        gf = gf_ref[...]
        h2 = h_ref[0] + gtv * f
        r = lax.rsqrt(jnp.mean(h2 * h2, axis=-1, keepdims=True) + EPS)
        n = h2 * r
        e = n * gf - t_ref[0]
        dy = e * (1.0 / D_MODEL)
        dn = dy * gf
        dh2 = r * (dn - n * jnp.mean(dn * n, axis=-1, keepdims=True))
        dh_ref[0] = dh2.astype(BF16)
        dff_ref[0] = (dh2 * gtv).astype(BF16)

        @pl.when(s == 0)
        def _():
            dgt_ref[...] = jnp.zeros_like(dgt_ref)

        @pl.when((s == 0) & (b == 0))
        def _():
            dgf_ref[...] = jnp.zeros_like(dgf_ref)
            loss_ref[...] = jnp.zeros_like(loss_ref)

        dgt_ref[0] += jnp.sum(dh2 * f, axis=0, keepdims=True)
        dgf_ref[...] += jnp.sum(dy * n, axis=0, keepdims=True)
        rows = jnp.sum(e * e, axis=1, keepdims=True)
        loss_ref[...] += jnp.sum(rows, axis=0, keepdims=True) * (0.5 / D_MODEL)

    return _pcall(body, name="final_loss",
                  out_shape=(_sds(h1.shape, BF16), _sds(h1.shape, BF16), _sds((B, 1, D_MODEL), F32),
                             _sds((1, D_MODEL), F32), _sds((1, 1), F32)),
                  grid=(B, S // ts), in_specs=[row, row, row, bvec, gvec], out_specs=(row, row, bvec, gvec, one),
                  dims=("arbitrary", "arbitrary"))(h1, ffn3, tgt3, gt, gfin)


ATT_GROUP = 4
ATT_GW = ATT_GROUP * HEAD_DIM
ATT_GROUPS = N_HEADS // ATT_GROUP
ATT_PAIRS = ATT_GW // ATT_BLOCK
ATT_UNROLL = 5
ATT_RESIDUE_UNROLL = 8
NT_DIMS = (((1,), (1,)), ((), ()))
TN_DIMS = (((0,), (0,)), ((), ()))


def _att_rows(start, d):
    if d == 1:
        return pl.ds(start if isinstance(start, int) else pl.multiple_of(start, ATT_BLOCK), ATT_BLOCK)
    return pl.ds(start, ATT_BLOCK, stride=d)


def _att_fill_bias(bias_ref, g, d):
    a = lax.broadcasted_iota(jnp.int32, (ATT_BLOCK, ATT_BLOCK), 0)
    j = lax.broadcasted_iota(jnp.int32, (ATT_BLOCK, ATT_BLOCK), 1)
    dist = (a - j).astype(F32)
    for hh in range(ATT_GROUP):
        t, e = divmod(hh, 2)
        rs = slice(e * ATT_BLOCK, (e + 1) * ATT_BLOCK)
        lo = 2.0 ** (-8.0 * (hh + 1) / N_HEADS) * d
        hi = 2.0 ** (-8.0 * (ATT_GROUP + hh + 1) / N_HEADS) * d
        slope = jnp.where(g == 0, lo, hi).astype(F32)
        bias_ref[t, rs, 0:ATT_BLOCK] = jnp.where(a >= j, -slope * dist, NEG_INF)
        bias_ref[t, rs, ATT_BLOCK:] = jnp.where(j >= a, -slope * (dist + float(ATT_BLOCK)), NEG_INF)


def _stack_heads(v2, low):
    return jnp.concatenate([jnp.where(low, v2, 0.0), jnp.where(low, 0.0, v2)], axis=0).astype(BF16)


def _unstack_heads(r2, low):
    return jnp.where(low, r2[0:ATT_BLOCK], r2[ATT_BLOCK:])


class _Riders:
    def __init__(self, arrs, mode, group="xy"):
        self.arrs, self.mode, self.n, self.group = list(arrs), mode, len(arrs), group
        k = len(_GROUP_MASKS[group])
        self.scratch = [pltpu.SemaphoreType.DMA((k * self.n,)), pltpu.SemaphoreType.DMA((k * self.n,))]
        if mode == "swap":
            assert group == "c"
            self.out_shape = [_sds(a.shape, a.dtype) for a in self.arrs]
        else:
            slot_shapes = [a.shape if mode == "gather" else a.shape[1:] for a in self.arrs]
            self.out_shape = [_sds((_GROUP_SLOTS[group],) + s, a.dtype) for s, a in zip(slot_shapes, self.arrs)]
            self.scratch += [pltpu.SemaphoreType.DMA((2 * self.n,))] + [pltpu.VMEM(s, a.dtype)
                                                                        for s, a in zip(slot_shapes, self.arrs)]
        self.specs = [pl.BlockSpec(memory_space=pl.ANY)] * self.n

    def _remote(self, x_refs, o_refs, send_sems, recv_sems):
        x, y, c = lax.axis_index("x"), lax.axis_index("y"), lax.axis_index("c")
        me = _group_slot(self.group, x, y, c)
        masks = _GROUP_MASKS[self.group]
        cps = []
        for i in range(self.n):
            for k, (dx, dy, dc) in enumerate(masks):
                px, py, pc = _flip(x, dx), _flip(y, dy), _flip(c, dc)
                src = x_refs[i].at[_group_slot(self.group, px, py, pc)] if self.mode == "scatter" else x_refs[i]
                dst = o_refs[i] if self.mode == "swap" else o_refs[i].at[me]
                cps.append(pltpu.make_async_remote_copy(
                    src_ref=src, dst_ref=dst, send_sem=send_sems.at[len(masks) * i + k],
                    recv_sem=recv_sems.at[len(masks) * i + k], device_id=(px, py, pc),
                    device_id_type=pl.DeviceIdType.MESH))
        return cps, me

    def start(self, x_refs, o_refs, scratch):
        cps, me = self._remote(x_refs, o_refs, scratch[0], scratch[1])
        for cp in cps:
            cp.start()
        if self.mode == "swap":
            return
        local_sems, bufs = scratch[2], scratch[3:]
        for i in range(self.n):
            src = x_refs[i] if self.mode == "gather" else x_refs[i].at[me]
            load = pltpu.make_async_copy(src, bufs[i], local_sems.at[2 * i])
            load.start()
            load.wait()
            pltpu.make_async_copy(bufs[i], o_refs[i].at[me], local_sems.at[2 * i + 1]).start()

    def wait(self, x_refs, o_refs, scratch):
        cps, me = self._remote(x_refs, o_refs, scratch[0], scratch[1])
        for cp in cps:
            cp.wait()
        if self.mode == "swap":
            return
        local_sems, bufs = scratch[2], scratch[3:]
        for i in range(self.n):
            pltpu.make_async_copy(bufs[i], o_refs[i].at[me], local_sems.at[2 * i + 1]).wait()


class _RiderGroup:
    def __init__(self, members):
        self.members = list(members)
        self.n = sum(m.n for m in self.members)
        self.arrs = [a for m in self.members for a in m.arrs]
        self.out_shape = [s for m in self.members for s in m.out_shape]
        self.specs = [s for m in self.members for s in m.specs]
        self.scratch = [s for m in self.members for s in m.scratch]

    def _each(self, x_refs, o_refs, scratch):
        i = j = 0
        for m in self.members:
            yield m, x_refs[i:i + m.n], o_refs[i:i + m.n], scratch[j:j + len(m.scratch)]
            i, j = i + m.n, j + len(m.scratch)

    def start(self, x_refs, o_refs, scratch):
        for m, xs, os, sc in self._each(x_refs, o_refs, scratch):
            m.start(xs, os, sc)

    def wait(self, x_refs, o_refs, scratch):
        for m, xs, os, sc in self._each(x_refs, o_refs, scratch):
            m.wait(xs, os, sc)


def _with_riders(compute, riders, n_in, n_out, n_scratch, last_step):
    if riders is None:
        return compute
    n = riders.n

    def body(*refs):
        ins, x_refs = refs[:n_in], refs[n_in:n_in + n]
        outs, o_refs = refs[n_in + n:n_in + n + n_out], refs[n_in + n + n_out:n_in + 2 * n + n_out]
        scratch = refs[n_in + 2 * n + n_out:]
        own, ride = scratch[:n_scratch], scratch[n_scratch:]
        ids = [pl.program_id(i) for i in range(len(last_step))]
        first = functools.reduce(jnp.logical_and, [i == 0 for i in ids])
        last = functools.reduce(jnp.logical_and, [i == l for i, l in zip(ids, last_step)])

        @pl.when(first)
        def _():
            riders.start(x_refs, o_refs, ride)

        compute(*ins, *outs, *own)

        @pl.when(last)
        def _():
            riders.wait(x_refs, o_refs, ride)

    return body


def _attention_fwd(proj3, seq_blocks, riders=None):
    B, S, _ = proj3.shape
    scale = HEAD_DIM ** -0.5
    nq = ATT_WIDTH // ATT_GW

    def col(k):
        return pl.BlockSpec((1, S, ATT_GW), lambda b, g, k=k: (b, 0, k * nq + g))

    o_spec = pl.BlockSpec((1, S, ATT_GW), lambda b, g: (b, 0, g))
    l_spec = pl.BlockSpec((1, 1, S, ATT_BLOCK), lambda b, g: (b, g, 0, 0))

    def compute(q_ref, k_ref, v_ref, o_ref, lse_ref, qf, kf, vf, os, ls, bias):
        g = pl.program_id(1)
        for t in range(ATT_PAIRS):
            ts = slice(t * ATT_BLOCK, (t + 1) * ATT_BLOCK)
            qf[t] = q_ref[0, :, ts].astype(F32) * scale
            kf[t] = k_ref[0, :, ts].astype(F32)
            vf[t] = v_ref[0, :, ts].astype(F32)
        lane = lax.broadcasted_iota(jnp.int32, (ATT_BLOCK, ATT_BLOCK), 1)
        low = lane < HEAD_DIM

        def block(p, d, r, n, has_prev):
            start = n * (ATT_BLOCK * d) + r
            rows = _att_rows(start, d)
            prows = _att_rows(start - ATT_BLOCK * d, d) if has_prev else None
            lse_t = jnp.zeros((ATT_BLOCK, ATT_BLOCK), F32)
            for t in range(ATT_PAIRS):
                q2 = _stack_heads(qf[t, rows, :], low)
                k2 = kf[t, rows, :].astype(BF16)
                v2 = vf[t, rows, :].astype(BF16)
                if has_prev:
                    k2 = jnp.concatenate([k2, kf[t, prows, :].astype(BF16)], axis=0)
                    v2 = jnp.concatenate([v2, vf[t, prows, :].astype(BF16)], axis=0)
                    b2 = bias[t]
                else:
                    b2 = bias[t, :, 0:ATT_BLOCK]
                s = lax.dot_general(q2, k2, NT_DIMS, preferred_element_type=F32) + b2
                m = jnp.max(s, axis=1, keepdims=True)
                pr = jnp.exp(s - m)
                den = jnp.sum(pr, axis=1, keepdims=True)
                o = jnp.dot(pr.astype(BF16), v2, preferred_element_type=F32) * (1.0 / den)
                os[p, t, rows, :] = _unstack_heads(o, low)
                lse2 = m + jnp.log(den)
                lse_t = jnp.where(lane == 2 * t, lse2[0:ATT_BLOCK], lse_t)
                lse_t = jnp.where(lane == 2 * t + 1, lse2[ATT_BLOCK:], lse_t)
            ls[p, rows, :] = lse_t

        for p in range(N_PATTERNS):
            d = 4 ** p
            _att_fill_bias(bias, g, d)
            _att_one_pattern(block, p, d, seq_blocks // d)

        def combine(i, carry):
            rows = pl.ds(pl.multiple_of(i * ATT_BLOCK, ATT_BLOCK), ATT_BLOCK)
            l0, l1, l2 = ls[0, rows, :], ls[1, rows, :], ls[2, rows, :]
            m = jnp.maximum(jnp.maximum(l0, l1), l2)
            lse = m + jnp.log(jnp.exp(l0 - m) + jnp.exp(l1 - m) + jnp.exp(l2 - m))
            lse_ref[0, 0, rows, :] = lse
            w = [jnp.exp(l0 - lse), jnp.exp(l1 - lse), jnp.exp(l2 - lse)]
            for t in range(ATT_PAIRS):
                acc = jnp.zeros((ATT_BLOCK, ATT_BLOCK), F32)
                for p in range(N_PATTERNS):
                    wt = jnp.where(low, w[p][:, 2 * t:2 * t + 1], w[p][:, 2 * t + 1:2 * t + 2])
                    acc = acc + wt * os[p, t, rows, :]
                o_ref[0, rows, t * ATT_BLOCK:(t + 1) * ATT_BLOCK] = acc.astype(BF16)
            return carry

        lax.fori_loop(0, S // ATT_BLOCK, combine, 0, unroll=2)

    scratch = ([pltpu.VMEM((ATT_PAIRS, S, ATT_BLOCK), F32)] * 3
               + [pltpu.VMEM((N_PATTERNS, ATT_PAIRS, S, ATT_BLOCK), F32), pltpu.VMEM((N_PATTERNS, S, ATT_BLOCK), F32),
                  pltpu.VMEM((ATT_PAIRS, 2 * ATT_BLOCK, 2 * ATT_BLOCK), F32)])
    rs = riders
    res = _pcall(_with_riders(compute, rs, 3, 2, len(scratch), (B - 1, ATT_GROUPS - 1)), name="attention_fwd",
                 out_shape=(_sds((B, S, ATT_WIDTH), BF16), _sds((B, ATT_GROUPS, S, ATT_BLOCK), F32))
                 + (tuple(rs.out_shape) if rs else ()),
                 grid=(B, ATT_GROUPS), in_specs=[col(0), col(1), col(2)] + (rs.specs if rs else []),
                 out_specs=(o_spec, l_spec) + (tuple(rs.specs) if rs else ()),
                 scratch_shapes=scratch + (rs.scratch if rs else []),
                 dims=("arbitrary", "arbitrary"))(proj3, proj3, proj3, *(rs.arrs if rs else []))
    return res[0], res[1], list(res[2:])


def _att_one_pattern(block, p, d, nb):
    def per_residue(r, carry):
        block(p, d, r, 0, False)
        if nb > 1:
            def per_block(n, c2):
                block(p, d, r, n, True)
                return c2
            lax.fori_loop(1, nb, per_block, 0, unroll=ATT_UNROLL if (nb - 1) % ATT_UNROLL == 0 else nb - 1)
        return carry

    if d == 1:
        per_residue(0, 0)
    else:
        lax.fori_loop(0, d, per_residue, 0, unroll=ATT_RESIDUE_UNROLL if nb == 1 else 1)


def _attention_bwd(proj3, do3, o3, lse4, seq_blocks, riders=None):
    B, S, _ = proj3.shape
    scale = HEAD_DIM ** -0.5
    nq = ATT_WIDTH // ATT_GW

    def col(k):
        return pl.BlockSpec((1, S, ATT_GW), lambda b, g, k=k: (b, 0, k * nq + g))

    o_spec = pl.BlockSpec((1, S, ATT_GW), lambda b, g: (b, 0, g))
    l_spec = pl.BlockSpec((1, 1, S, ATT_BLOCK), lambda b, g: (b, g, 0, 0))

    def compute(q_ref, k_ref, v_ref, do_ref, o_ref, lse_ref, dq_ref, dk_ref, dv_ref,
                qf, kf, vf, dof, dl, aq, ak, av, bias):
        g = pl.program_id(1)
        for t in range(ATT_PAIRS):
            ts = slice(t * ATT_BLOCK, (t + 1) * ATT_BLOCK)
            qf[t] = q_ref[0, :, ts].astype(F32) * scale
            kf[t] = k_ref[0, :, ts].astype(F32)
            vf[t] = v_ref[0, :, ts].astype(F32)
            dof[t] = do_ref[0, :, ts].astype(F32)
        aq[...] = jnp.zeros_like(aq)
        ak[...] = jnp.zeros_like(ak)
        av[...] = jnp.zeros_like(av)
        lane = lax.broadcasted_iota(jnp.int32, (ATT_BLOCK, ATT_BLOCK), 1)
        low = lane < HEAD_DIM

        def fill_delta(i, carry):
            rows = pl.ds(pl.multiple_of(i * ATT_BLOCK, ATT_BLOCK), ATT_BLOCK)
            acc = jnp.zeros((ATT_BLOCK, ATT_BLOCK), F32)
            for t in range(ATT_PAIRS):
                prod = dof[t, rows, :] * o_ref[0, rows, t * ATT_BLOCK:(t + 1) * ATT_BLOCK].astype(F32)
                lo = jnp.sum(jnp.where(low, prod, 0.0), axis=1, keepdims=True)
                hi = jnp.sum(prod, axis=1, keepdims=True) - lo
                acc = jnp.where(lane == 2 * t, lo, acc)
                acc = jnp.where(lane == 2 * t + 1, hi, acc)
            dl[rows, :] = acc
            return carry

        lax.fori_loop(0, S // ATT_BLOCK, fill_delta, 0, unroll=2)

        def block(p, d, r, n, has_prev):
            start = n * (ATT_BLOCK * d) + r
            rows = _att_rows(start, d)
            prows = _att_rows(start - ATT_BLOCK * d, d) if has_prev else None
            lse_t = lse_ref[0, 0, rows, :]
            dl_t = dl[rows, :]
            for t in range(ATT_PAIRS):
                q2 = _stack_heads(qf[t, rows, :], low)
                do2 = _stack_heads(dof[t, rows, :], low)
                k2 = kf[t, rows, :].astype(BF16)
                v2 = vf[t, rows, :].astype(BF16)
                if has_prev:
                    k2 = jnp.concatenate([k2, kf[t, prows, :].astype(BF16)], axis=0)
                    v2 = jnp.concatenate([v2, vf[t, prows, :].astype(BF16)], axis=0)
                    b2 = bias[t]
                else:
                    b2 = bias[t, :, 0:ATT_BLOCK]
                lse2 = jnp.concatenate([lse_t[:, 2 * t:2 * t + 1], lse_t[:, 2 * t + 1:2 * t + 2]], axis=0)
                dl2 = jnp.concatenate([dl_t[:, 2 * t:2 * t + 1], dl_t[:, 2 * t + 1:2 * t + 2]], axis=0)
                s = lax.dot_general(q2, k2, NT_DIMS, preferred_element_type=F32) + b2
                pr = jnp.exp(s - lse2)
                ds = (pr * (lax.dot_general(do2, v2, NT_DIMS, preferred_element_type=F32) - dl2)).astype(BF16)
                dq = _unstack_heads(jnp.dot(ds, k2, preferred_element_type=F32), low)
                dk = lax.dot_general(ds, q2, TN_DIMS, preferred_element_type=F32)
                dv = lax.dot_general(pr.astype(BF16), do2, TN_DIMS, preferred_element_type=F32)
                aq[t, rows, :] = aq[t, rows, :] + dq * scale
                ak[t, rows, :] = ak[t, rows, :] + dk[0:ATT_BLOCK]
                av[t, rows, :] = av[t, rows, :] + dv[0:ATT_BLOCK]
                if has_prev:
                    ak[t, prows, :] = ak[t, prows, :] + dk[ATT_BLOCK:]
                    av[t, prows, :] = av[t, prows, :] + dv[ATT_BLOCK:]

        for p in range(N_PATTERNS):
            d = 4 ** p
            _att_fill_bias(bias, g, d)
            _att_one_pattern(block, p, d, seq_blocks // d)

        for t in range(ATT_PAIRS):
            ts = slice(t * ATT_BLOCK, (t + 1) * ATT_BLOCK)
            dq_ref[0, :, ts] = aq[t].astype(BF16)
            dk_ref[0, :, ts] = ak[t].astype(BF16)
            dv_ref[0, :, ts] = av[t].astype(BF16)

    shp = _sds((B, S, ATT_WIDTH), BF16)
    pair_buf = pltpu.VMEM((ATT_PAIRS, S, ATT_BLOCK), F32)
    scratch = ([pair_buf] * 4 + [pltpu.VMEM((S, ATT_BLOCK), F32)] + [pair_buf] * 3
               + [pltpu.VMEM((ATT_PAIRS, 2 * ATT_BLOCK, 2 * ATT_BLOCK), F32)])
    rs = riders
    res = _pcall(_with_riders(compute, rs, 6, 3, len(scratch), (B - 1, ATT_GROUPS - 1)), name="attention_bwd",
                 out_shape=(shp, shp, shp) + (tuple(rs.out_shape) if rs else ()), grid=(B, ATT_GROUPS),
                 in_specs=[col(0), col(1), col(2), o_spec, o_spec, l_spec] + (rs.specs if rs else []),
                 out_specs=(o_spec, o_spec, o_spec) + (tuple(rs.specs) if rs else ()),
                 scratch_shapes=scratch + (rs.scratch if rs else []),
                 dims=("arbitrary", "arbitrary"))(proj3, proj3, proj3, do3, o3, lse4, *(rs.arrs if rs else []))
    return res[0], res[1], res[2], list(res[3:])


def _expand_groups(m):
    rows = SSM_WIDTH
    t = jnp.concatenate([m] * SSM_GROUPS, axis=0)
    r = lax.broadcasted_iota(jnp.int32, (rows, SSM_LANES), 0)
    l = lax.broadcasted_iota(jnp.int32, (rows, SSM_LANES), 1)
    keep = lax.shift_right_logical(r, 4) == lax.shift_right_logical(l, 6)
    return jnp.where(keep, t, 0.0)


def _collapse_groups(m):
    rows = SSM_WIDTH
    r = lax.broadcasted_iota(jnp.int32, (rows, SSM_LANES), 0)
    l = lax.broadcasted_iota(jnp.int32, (rows, SSM_LANES), 1)
    keep = lax.shift_right_logical(r, 4) == lax.shift_right_logical(l, 6)
    t = jnp.where(keep, m, 0.0)
    acc = t[0:SSM_GROUP_CH]
    for g in range(1, SSM_GROUPS):
        acc = acc + t[g * SSM_GROUP_CH:(g + 1) * SSM_GROUP_CH]
    return acc


def _zoh(lr, li, ldt):
    dt = jnp.exp(ldt)
    mag = jnp.exp(lr * dt)
    ang = li * dt
    cs, sn = jnp.cos(ang), jnp.sin(ang)
    ab_re, ab_im = mag * cs, mag * sn
    nr, ni = ab_re - 1.0, ab_im
    den = lr * lr + li * li
    n_re = nr * lr + ni * li
    n_im = ni * lr - nr * li
    return dict(dt=dt, mag=mag, cs=cs, sn=sn, ab_re=ab_re, ab_im=ab_im, nr=nr, ni=ni, den=den, n_re=n_re, n_im=n_im,
                f_re=n_re / den, f_im=n_im / den)


def _ssm_params(lr, li, ldt, br, bi, cr, ci):
    def body(lr_ref, li_ref, ldt_ref, br_ref, bi_ref, cr_ref, ci_ref, ab_ref, w_ref, c_ref):
        z = _zoh(lr_ref[...], li_ref[...], ldt_ref[...])
        ab_ref[0:1, :] = z["ab_re"]
        ab_ref[1:2, :] = z["ab_im"]
        br, bi = br_ref[...], bi_ref[...]
        w_ref[:, 0:SSM_LANES] = _expand_groups(z["f_re"] * br - z["f_im"] * bi).astype(BF16)
        w_ref[:, SSM_LANES:] = _expand_groups(z["f_re"] * bi + z["f_im"] * br).astype(BF16)
        c_ref[:, 0:SSM_LANES] = _expand_groups(cr_ref[...]).astype(BF16)
        c_ref[:, SSM_LANES:] = _expand_groups(-ci_ref[...]).astype(BF16)

    return _pcall(body, name="ssm_params",
                  out_shape=(_sds((2, SSM_LANES), F32), _sds((SSM_WIDTH, 2 * SSM_LANES), BF16),
                             _sds((SSM_WIDTH, 2 * SSM_LANES), BF16)))(lr, li, ldt, br, bi, cr, ci)


def _ssm_params_bwd(lr, li, ldt, br, bi, dab, dw, dc):
    def body(lr_ref, li_ref, ldt_ref, br_ref, bi_ref, dab_ref, dw_ref, dc_ref,
             dlr_ref, dli_ref, dldt_ref, dbr_ref, dbi_ref, dcr_ref, dci_ref):
        lr, li = lr_ref[...], li_ref[...]
        z = _zoh(lr, li, ldt_ref[...])
        br, bi = br_ref[...], bi_ref[...]
        dbb_re = _collapse_groups(dw_ref[:, 0:SSM_LANES])
        dbb_im = _collapse_groups(dw_ref[:, SSM_LANES:])
        dcr_ref[...] = _collapse_groups(dc_ref[:, 0:SSM_LANES])
        dci_ref[...] = -_collapse_groups(dc_ref[:, SSM_LANES:])
        f_re, f_im = z["f_re"], z["f_im"]
        dbr_ref[...] = f_re * dbb_re + f_im * dbb_im
        dbi_ref[...] = f_re * dbb_im - f_im * dbb_re
        df_re = jnp.sum(dbb_re * br + dbb_im * bi, axis=0, keepdims=True)
        df_im = jnp.sum(dbb_im * br - dbb_re * bi, axis=0, keepdims=True)
        den = z["den"]
        dn_re, dn_im = df_re / den, df_im / den
        dden = -(df_re * z["n_re"] + df_im * z["n_im"]) / (den * den)
        dnr = dn_re * lr - dn_im * li
        dni = dn_re * li + dn_im * lr
        dlr = dn_re * z["nr"] + dn_im * z["ni"] + 2.0 * dden * lr
        dli = dn_re * z["ni"] - dn_im * z["nr"] + 2.0 * dden * li
        dab_re = dab_ref[0:1, :] + dnr
        dab_im = dab_ref[1:2, :] + dni
        mag, cs, sn, dt = z["mag"], z["cs"], z["sn"], z["dt"]
        dmag = dab_re * cs + dab_im * sn
        dang = mag * (dab_im * cs - dab_re * sn)
        dlr_ref[...] = dlr + dmag * mag * dt
        dli_ref[...] = dli + dang * dt
        ddt = dmag * mag * lr + dang * li
        per_lane = jnp.broadcast_to(ddt * dt, (8, SSM_LANES))
        lane = lax.broadcasted_iota(jnp.int32, (SSM_LANES, 128), 0)
        col = lax.broadcasted_iota(jnp.int32, (SSM_LANES, 128), 1)
        ind = jnp.where(lax.shift_right_logical(lane, 6) == col, 1.0, 0.0)
        dldt_ref[...] = jnp.dot(per_lane, ind, preferred_element_type=F32, precision=lax.Precision.HIGHEST)[0:1]

    vec = _sds((1, SSM_LANES), F32)
    mat = _sds((SSM_GROUP_CH, SSM_LANES), F32)
    return _pcall(body, name="ssm_params_bwd", out_shape=(vec, vec, _sds((1, 128), F32), mat, mat, mat, mat))(
        lr, li, ldt, br, bi, dab, dw, dc)


SCAN_CHUNK = 1024


def _scan_consts(ar, ai, k_ref, reverse):
    row = lax.broadcasted_iota(jnp.int32, (8, SSM_LANES), 0)
    pw = [(ar, ai)]
    for _ in range(7):
        pr, pi = pw[-1]
        pw.append((pr * ar - pi * ai, pr * ai + pi * ar))
    for n, k in enumerate((1, 2, 4)):
        keep = (row < 8 - k) if reverse else (row >= k)
        k_ref[2 * n] = jnp.where(keep, jnp.broadcast_to(pw[k - 1][0], (8, SSM_LANES)), 0.0)
        k_ref[2 * n + 1] = jnp.where(keep, jnp.broadcast_to(pw[k - 1][1], (8, SSM_LANES)), 0.0)
    cr = jnp.zeros((8, SSM_LANES), F32)
    ci = jnp.zeros((8, SSM_LANES), F32)
    for r in range(8):
        e = (8 - r) if reverse else (r + 1)
        cr = jnp.where(row == r, jnp.broadcast_to(pw[e - 1][0], (8, SSM_LANES)), cr)
        ci = jnp.where(row == r, jnp.broadcast_to(pw[e - 1][1], (8, SSM_LANES)), ci)
    k_ref[6] = cr
    k_ref[7] = ci


def _scan_tile(xr, xi, k_ref, car, cai, reverse):
    for n, k in enumerate((1, 2, 4)):
        sh = (8 - k) if reverse else k
        sr = pltpu.roll(xr, sh, 0)
        si = pltpu.roll(xi, sh, 0)
        mr, mi = k_ref[2 * n], k_ref[2 * n + 1]
        xr, xi = xr + mr * sr - mi * si, xi + mr * si + mi * sr
    pr, pi = k_ref[6], k_ref[7]
    xr, xi = xr + pr * car - pi * cai, xi + pr * cai + pi * car
    return xr, xi


US_BLOCK = (3 * ATT_WIDTH) // SSM_WIDTH


def _ssm_scan_fwd(proj3, abar, w_bu, w_c):
    B, S, _ = proj3.shape
    ch = min(S, SCAN_CHUNK)
    u_spec = pl.BlockSpec((1, ch, SSM_WIDTH), lambda b, c: (b, c, US_BLOCK))
    x_spec = pl.BlockSpec((1, ch, 2 * SSM_LANES), lambda b, c: (b, c, 0))
    y_spec = pl.BlockSpec((1, ch, SSM_WIDTH), lambda b, c: (b, c, 0))
    w_spec = pl.BlockSpec((SSM_WIDTH, 2 * SSM_LANES), lambda b, c: (0, 0))

    def body(ab_ref, u_ref, wb_ref, wc_ref, x_ref, y_ref, k_ref, carry_ref):
        _scan_consts(ab_ref[0:1, :], ab_ref[1:2, :], k_ref, False)

        @pl.when(pl.program_id(1) == 0)
        def _():
            carry_ref[...] = jnp.zeros_like(carry_ref)

        x_ref[0] = jnp.dot(u_ref[0], wb_ref[...], preferred_element_type=F32)

        def step(i, carry):
            base = pl.multiple_of(i * 8, 8)
            xr = x_ref[0, pl.ds(base, 8), 0:SSM_LANES]
            xi = x_ref[0, pl.ds(base, 8), SSM_LANES:]
            xr, xi = _scan_tile(xr, xi, k_ref, carry[0], carry[1], False)
            x_ref[0, pl.ds(base, 8), 0:SSM_LANES] = xr
            x_ref[0, pl.ds(base, 8), SSM_LANES:] = xi
            return (jnp.broadcast_to(xr[7:8], (8, SSM_LANES)), jnp.broadcast_to(xi[7:8], (8, SSM_LANES)))

        cr, ci = lax.fori_loop(0, ch // 8, step, (carry_ref[0], carry_ref[1]))
        carry_ref[0] = cr
        carry_ref[1] = ci
        y_ref[0] = lax.dot_general(x_ref[0].astype(BF16), wc_ref[...], NT_DIMS, preferred_element_type=F32)

    return _pcall(body, name="ssm_scan_fwd",
                  out_shape=(_sds((B, S, 2 * SSM_LANES), F32), _sds((B, S, SSM_WIDTH), F32)), grid=(B, S // ch),
                  in_specs=[pl.BlockSpec((2, SSM_LANES), lambda b, c: (0, 0)), u_spec, w_spec, w_spec],
                  out_specs=(x_spec, y_spec),
                  scratch_shapes=[pltpu.VMEM((8, 8, SSM_LANES), F32), pltpu.VMEM((2, 8, SSM_LANES), F32)],
                  dims=("arbitrary", "arbitrary"))(abar, proj3, w_bu, w_c)


def _ssm_scan_bwd(proj3, dy3, xs3, abar, w_bu, w_c, dsk):
    B, S, _ = proj3.shape
    ch = min(S, SCAN_CHUNK)
    nc = S // ch
    u_spec = pl.BlockSpec((1, ch, SSM_WIDTH), lambda b, c: (b, nc - 1 - c, US_BLOCK))
    x_spec = pl.BlockSpec((1, ch, 2 * SSM_LANES), lambda b, c: (b, nc - 1 - c, 0))
    y_spec = pl.BlockSpec((1, ch, SSM_WIDTH), lambda b, c: (b, nc - 1 - c, 0))
    w_spec = pl.BlockSpec((SSM_WIDTH, 2 * SSM_LANES), lambda b, c: (0, 0))
    ab_spec = pl.BlockSpec((2, SSM_LANES), lambda b, c: (0, 0))
    d_spec = pl.BlockSpec((1, SSM_WIDTH), lambda b, c: (0, 0))

    def body(ab_ref, u_ref, dy_ref, xs_ref, wb_ref, wc_ref, d_ref, du_ref, da_ref, dwb_ref, dwc_ref,
             g_ref, k_ref, carry_ref, acc_ref):
        b, c = pl.program_id(0), pl.program_id(1)
        _scan_consts(ab_ref[0:1, :], -ab_ref[1:2, :], k_ref, True)
        row = lax.broadcasted_iota(jnp.int32, (8, SSM_LANES), 0)

        @pl.when(c == 0)
        def _():
            carry_ref[...] = jnp.zeros_like(carry_ref)

        @pl.when((c == 0) & (b == 0))
        def _():
            acc_ref[...] = jnp.zeros_like(acc_ref)
            dwb_ref[...] = jnp.zeros_like(dwb_ref)
            dwc_ref[...] = jnp.zeros_like(dwc_ref)

        dy = dy_ref[0]
        dyb = dy.astype(BF16)
        g_ref[...] = jnp.dot(dyb, wc_ref[...], preferred_element_type=F32)

        def step(i, carry):
            car, cai, ar_acc, ai_acc = carry
            base = pl.multiple_of((ch // 8 - 1 - i) * 8, 8)
            gr = g_ref[pl.ds(base, 8), 0:SSM_LANES]
            gi = g_ref[pl.ds(base, 8), SSM_LANES:]
            gr, gi = _scan_tile(gr, gi, k_ref, car, cai, True)
            g_ref[pl.ds(base, 8), 0:SSM_LANES] = gr
            g_ref[pl.ds(base, 8), SSM_LANES:] = gi
            nr = jnp.where(row == 7, car, pltpu.roll(gr, 7, 0))
            ni = jnp.where(row == 7, cai, pltpu.roll(gi, 7, 0))
            xr = xs_ref[0, pl.ds(base, 8), 0:SSM_LANES]
            xi = xs_ref[0, pl.ds(base, 8), SSM_LANES:]
            ar_acc = ar_acc + nr * xr + ni * xi
            ai_acc = ai_acc + ni * xr - nr * xi
            return (jnp.broadcast_to(gr[0:1], (8, SSM_LANES)), jnp.broadcast_to(gi[0:1], (8, SSM_LANES)), ar_acc, ai_acc)

        cr, ci, ar_acc, ai_acc = lax.fori_loop(0, ch // 8, step, (carry_ref[0], carry_ref[1], acc_ref[0], acc_ref[1]))
        carry_ref[0] = cr
        carry_ref[1] = ci
        acc_ref[0] = ar_acc
        acc_ref[1] = ai_acc
        da_ref[0:1, :] = jnp.sum(ar_acc, axis=0, keepdims=True)
        da_ref[1:2, :] = jnp.sum(ai_acc, axis=0, keepdims=True)

        gb = g_ref[...].astype(BF16)
        du = lax.dot_general(gb, wb_ref[...], NT_DIMS, preferred_element_type=F32) + d_ref[...] * dy
        du_ref[0] = du.astype(BF16)
        xb = xs_ref[0].astype(BF16)
        u = u_ref[0]
        for j in range(2 * SSM_LANES // SSM_WIDTH):
            rows = slice((j % (SSM_LANES // SSM_WIDTH)) * 64, (j % (SSM_LANES // SSM_WIDTH)) * 64 + 64)
            cols = slice(j * SSM_WIDTH, (j + 1) * SSM_WIDTH)
            dwb_ref[rows, cols] += lax.dot_general(u[:, rows], gb[:, cols], TN_DIMS, preferred_element_type=F32)
            dwc_ref[rows, cols] += lax.dot_general(dyb[:, rows], xb[:, cols], TN_DIMS, preferred_element_type=F32)

    mat = _sds((SSM_WIDTH, 2 * SSM_LANES), F32)
    return _pcall(body, name="ssm_scan_bwd",
                  out_shape=(_sds((B, S, SSM_WIDTH), BF16), _sds((2, SSM_LANES), F32), mat, mat), grid=(B, nc),
                  in_specs=[ab_spec, u_spec, y_spec, x_spec, w_spec, w_spec, d_spec],
                  out_specs=(y_spec, ab_spec, w_spec, w_spec),
                  scratch_shapes=[pltpu.VMEM((ch, 2 * SSM_LANES), F32), pltpu.VMEM((8, 8, SSM_LANES), F32),
                                  pltpu.VMEM((2, 8, SSM_LANES), F32), pltpu.VMEM((2, 8, SSM_LANES), F32)],
                  dims=("arbitrary", "arbitrary"))(abar, proj3, dy3, xs3, w_bu, w_c, dsk)


GELU_K = math.sqrt(2.0 / math.pi)
GELU_C = 0.044715


def _gelu_parts(y):
    t = jnp.tanh(GELU_K * (y + GELU_C * y * y * y))
    return 0.5 * y * (1.0 + t), t


def _ssm_post(yc, us, dsk, wglu, bglu):
    T, N = yc.shape
    tm = min(T, 1024)
    row = pl.BlockSpec((tm, N), lambda i: (i, 0))
    vec = pl.BlockSpec((1, N), lambda i: (0, 0))
    mat = pl.BlockSpec((N, N), lambda i: (0, 0))

    def body(yc_ref, us_ref, d_ref, w_ref, b_ref, y_ref, s_ref):
        y = yc_ref[...] + d_ref[...] * us_ref[...]
        y_ref[...] = y
        z, _ = _gelu_parts(y)
        gl = jnp.dot(z.astype(BF16), w_ref[...], preferred_element_type=F32) + b_ref[...]
        s_ref[...] = (z * _sig(gl)).astype(BF16)

    return _pcall(body, name="ssm_post", out_shape=(_sds((T, N), F32), _sds((T, N), BF16)), grid=(T // tm,),
                  in_specs=[row, row, vec, mat, vec], out_specs=(row, row), dims=("parallel",))(yc, us, dsk, wglu, bglu)


def _ssm_post_bwd(y5, us, ds, dsk, wglu, bglu):
    T, N = y5.shape
    tm = min(T, 1024)
    row = pl.BlockSpec((tm, N), lambda i: (i, 0))
    vec = pl.BlockSpec((1, N), lambda i: (0, 0))
    mat = pl.BlockSpec((N, N), lambda i: (0, 0))

    def body(y_ref, us_ref, ds_ref, d_ref, w_ref, b_ref, dy_ref, dd_ref, db_ref, dw_ref):
        @pl.when(pl.program_id(0) == 0)
        def _():
            dd_ref[...] = jnp.zeros_like(dd_ref)
            db_ref[...] = jnp.zeros_like(db_ref)
            dw_ref[...] = jnp.zeros_like(dw_ref)

        y = y_ref[...]
        z, t = _gelu_parts(y)
        zb = z.astype(BF16)
        gl = jnp.dot(zb, w_ref[...], preferred_element_type=F32) + b_ref[...]
        sg = _sig(gl)
        ds = ds_ref[...]
        dgl = ds * z * sg * (1.0 - sg)
        dglb = dgl.astype(BF16)
        dz = ds * sg + lax.dot_general(dglb, w_ref[...], (((1,), (1,)), ((), ())), preferred_element_type=F32)
        dgelu = 0.5 * (1.0 + t) + 0.5 * y * (1.0 - t * t) * GELU_K * (1.0 + 3.0 * GELU_C * y * y)
        dy = dz * dgelu
        dy_ref[...] = dy
        dd_ref[...] += jnp.sum(dy * us_ref[...], axis=0, keepdims=True)
        db_ref[...] += jnp.sum(dgl, axis=0, keepdims=True)
        dw_ref[...] += lax.dot_general(zb, dglb, (((0,), (0,)), ((), ())), preferred_element_type=F32)

    return _pcall(body, name="ssm_post_bwd",
                  out_shape=(_sds((T, N), F32), _sds((1, N), F32), _sds((1, N), F32), _sds((N, N), F32)),
                  grid=(T // tm,), in_specs=[row, row, row, vec, mat, vec], out_specs=(row, vec, vec, mat),
                  dims=("arbitrary",))(y5, us, ds, dsk, wglu, bglu)


GATE_TILE = 256
GATE_ATT_BLOCK0 = (3 * ATT_WIDTH + SSM_WIDTH) // GATE_TILE
GATE_SSM_BLOCK0 = (3 * ATT_WIDTH + SSM_WIDTH + D_MODEL) // GATE_TILE


def _merge(proj, y_att, y_ssm, b_gate):
    T = proj.shape[0]
    tm = min(T, 4096)
    nj = D_MODEL // GATE_TILE
    ga = pl.BlockSpec((tm, GATE_TILE), lambda i, j: (i, GATE_ATT_BLOCK0 + j))
    gs = pl.BlockSpec((tm, GATE_TILE), lambda i, j: (i, GATE_SSM_BLOCK0 + j))
    yy = pl.BlockSpec((tm, GATE_TILE), lambda i, j: (i, j))
    ba = pl.BlockSpec((1, GATE_TILE), lambda i, j: (0, j))
    bs = pl.BlockSpec((1, GATE_TILE), lambda i, j: (0, nj + j))

    def body(ga_ref, gs_ref, ya_ref, ys_ref, ba_ref, bs_ref, o_ref):
        o_ref[...] = (_sig(ga_ref[...] + ba_ref[...]) * ya_ref[...]
                      + _sig(gs_ref[...] + bs_ref[...]) * ys_ref[...]).astype(BF16)

    return _pcall(body, name="merge", out_shape=_sds((T, D_MODEL), BF16), grid=(T // tm, nj),
                  in_specs=[ga, gs, yy, yy, ba, bs], out_specs=yy, dims=("parallel", "parallel"))(
        proj, proj, y_att, y_ssm, b_gate, b_gate)


def _merge_bwd(proj, y_att, y_ssm, b_gate, dmerged):
    T = proj.shape[0]
    tm = min(T, 2048)
    nj = D_MODEL // GATE_TILE
    ga = pl.BlockSpec((tm, GATE_TILE), lambda j, i: (i, GATE_ATT_BLOCK0 + j))
    gs = pl.BlockSpec((tm, GATE_TILE), lambda j, i: (i, GATE_SSM_BLOCK0 + j))
    yy = pl.BlockSpec((tm, GATE_TILE), lambda j, i: (i, j))
    ba = pl.BlockSpec((1, GATE_TILE), lambda j, i: (0, j))
    bs = pl.BlockSpec((1, GATE_TILE), lambda j, i: (0, nj + j))

    def body(ga_ref, gs_ref, ya_ref, ys_ref, ba_ref, bs_ref, dm_ref, dya_ref, dys_ref, dga_ref, dgs_ref, dba_ref, dbs_ref):
        @pl.when(pl.program_id(1) == 0)
        def _():
            dba_ref[...] = jnp.zeros_like(dba_ref)
            dbs_ref[...] = jnp.zeros_like(dbs_ref)

        dm = dm_ref[...].astype(F32)
        sa = _sig(ga_ref[...] + ba_ref[...])
        ss = _sig(gs_ref[...] + bs_ref[...])
        dya_ref[...] = (dm * sa).astype(BF16)
        dys_ref[...] = (dm * ss).astype(BF16)
        dga = dm * ya_ref[...] * sa * (1.0 - sa)
        dgs = dm * ys_ref[...] * ss * (1.0 - ss)
        dga_ref[...] = dga.astype(BF16)
        dgs_ref[...] = dgs.astype(BF16)
        dba_ref[...] += jnp.sum(dga, axis=0, keepdims=True)
        dbs_ref[...] += jnp.sum(dgs, axis=0, keepdims=True)

    big = _sds((T, D_MODEL), BF16)
    vec = _sds((1, D_MODEL), F32)
    return _pcall(body, name="merge_bwd", out_shape=(big, big, big, big, vec, vec), grid=(nj, T // tm),
                  in_specs=[ga, gs, yy, yy, ba, bs, yy], out_specs=(yy, yy, yy, yy, ba, ba),
                  dims=("arbitrary", "arbitrary"))(proj, proj, y_att, y_ssm, b_gate, b_gate, dmerged)


CONV_TILE = 256


def _shift_rows(a, j, up=False):
    n = a.shape[0]
    r = pltpu.roll(a, n - j if up else j, 0)
    row = lax.broadcasted_iota(jnp.int32, (8, a.shape[1]), 0)
    if up:
        return jnp.concatenate([r[:n - 8], jnp.where(row < 8 - j, r[n - 8:], 0.0)], axis=0)
    return jnp.concatenate([jnp.where(row >= j, r[:8], 0.0), r[8:]], axis=0)


def _conv_pre(a, w_ref, b_ref):
    conv = b_ref[...] + w_ref[0:1, :] * a
    shifted = []
    for j in (1, 2):
        sh = _shift_rows(a, j)
        shifted.append(sh)
        conv = conv + w_ref[j:j + 1, :] * sh
    return conv, shifted


def _conv_act(up3, w_conv, b_conv):
    B, S, _ = up3.shape
    nj = D_FF // CONV_TILE
    a_spec = pl.BlockSpec((1, S, CONV_TILE), lambda b, j: (b, 0, j))
    v_spec = pl.BlockSpec((1, S, CONV_TILE), lambda b, j: (b, 0, nj + j))
    w_spec = pl.BlockSpec((3, CONV_TILE), lambda b, j: (0, j))
    b_spec = pl.BlockSpec((1, CONV_TILE), lambda b, j: (0, j))

    def body(a_ref, v_ref, w_ref, b_ref, o_ref):
        a = a_ref[0].astype(F32)
        conv, _ = _conv_pre(a, w_ref, b_ref)
        o_ref[0] = (conv * _sig(conv) * v_ref[0]).astype(BF16)

    return _pcall(body, name="conv_act", out_shape=_sds((B, S, D_FF), BF16), grid=(B, nj),
                  in_specs=[a_spec, v_spec, w_spec, b_spec], out_specs=a_spec, dims=("parallel", "parallel"))(
        up3, up3, w_conv, b_conv)


def _conv_bwd(up3, dact3, w_conv, b_conv):
    B, S, _ = up3.shape
    nj = D_FF // CONV_TILE
    a_spec = pl.BlockSpec((1, S, CONV_TILE), lambda j, b: (b, 0, j))
    v_spec = pl.BlockSpec((1, S, CONV_TILE), lambda j, b: (b, 0, nj + j))
    o_spec = pl.BlockSpec((2, 1, S, CONV_TILE), lambda j, b: (0, b, 0, j))
    w_spec = pl.BlockSpec((3, CONV_TILE), lambda j, b: (0, j))
    b_spec = pl.BlockSpec((1, CONV_TILE), lambda j, b: (0, j))

    def body(a_ref, v_ref, d_ref, w_ref, b_ref, dup_ref, dw_ref, db_ref):
        @pl.when(pl.program_id(1) == 0)
        def _():
            dw_ref[...] = jnp.zeros_like(dw_ref)
            db_ref[...] = jnp.zeros_like(db_ref)

        a = a_ref[0].astype(F32)
        d = d_ref[0].astype(F32)
        conv, shifted = _conv_pre(a, w_ref, b_ref)
        sg = _sig(conv)
        dup_ref[1, 0] = (d * conv * sg).astype(BF16)
        dconv = d * v_ref[0] * (sg * (1.0 + conv * (1.0 - sg)))
        da = w_ref[0:1, :] * dconv
        for j in (1, 2):
            da = da + w_ref[j:j + 1, :] * _shift_rows(dconv, j, up=True)
        dup_ref[0, 0] = da.astype(BF16)
        db_ref[...] += jnp.sum(dconv, axis=0, keepdims=True)
        dw_ref[0:1, :] += jnp.sum(dconv * a, axis=0, keepdims=True)
        dw_ref[1:2, :] += jnp.sum(dconv * shifted[0], axis=0, keepdims=True)
        dw_ref[2:3, :] += jnp.sum(dconv * shifted[1], axis=0, keepdims=True)

    return _pcall(body, name="conv_bwd",
                  out_shape=(_sds((2, B, S, D_FF), BF16), _sds((3, D_FF), F32), _sds((1, D_FF), F32)),
                  grid=(nj, B), in_specs=[a_spec, v_spec, a_spec, w_spec, b_spec],
                  out_specs=(o_spec, w_spec, b_spec), dims=("arbitrary", "arbitrary"))(up3, up3, dact3, w_conv, b_conv)


def _rows_tile(r, cap=640):
    for t in range(min(r, cap) - min(r, cap) % 8, 7, -8):
        if r % t == 0:
            return t
    return r


def _add2(a, b, out_dtype, name):
    R, N = a.shape
    tr = _rows_tile(R)
    spec = pl.BlockSpec((tr, N), lambda i: (i, 0))

    def body(a_ref, b_ref, o_ref):
        o_ref[...] = (a_ref[...] + b_ref[...]).astype(out_dtype)

    return _pcall(body, name=name, out_shape=_sds((R, N), out_dtype), grid=(R // tr,), in_specs=[spec, spec],
                  out_specs=spec, dims=("parallel",))(a, b)


def _sum_slots(q, name):
    n, R, N = q.shape
    tr = _rows_tile(R)

    def body(q_ref, o_ref):
        acc = q_ref[0].astype(F32)
        for s in range(1, n):
            acc = acc + q_ref[s].astype(F32)
        o_ref[...] = acc

    return _pcall(body, name=name, out_shape=_sds((R, N), F32), grid=(R // tr,),
                  in_specs=[pl.BlockSpec((n, tr, N), lambda i: (0, i, 0))], out_specs=pl.BlockSpec((tr, N), lambda i: (i, 0)),
                  dims=("parallel",))(q)


NATIVE = (("b_re", 16, 1024), ("b_im", 16, 1024), ("c_re", 16, 1024), ("c_im", 16, 1024), ("g_mix", 1, 1024),
          ("b_att", 1, 1024), ("b_ssm", 1, 1024), ("a_re", 1, 1024), ("a_im", 1, 1024), ("log_dt", 1, 128),
          ("d_skip", 1, 256), ("b_glu", 1, 256), ("g_ffn", 1, 1024), ("g_final", 1, 1024), ("b_conv", 1, 2048),
          ("w_conv", 3, 2048), ("loss", 1, 1))
N_MOD = 6
NATIVE_LATE = ("g_mix",)
MODS_LATE = (0, 1)


def _small_plan(late):
    pieces = [p for p in NATIVE if (p[0] in NATIVE_LATE) == late]
    mods = [k for k in range(N_MOD) if (k in MODS_LATE) == late]
    starts, r = {}, 0
    for name, rows, cols in pieces:
        starts[name] = r
        r += rows * (-(-cols // LANES))
    return pieces, mods, starts, -(-r // 8) * 8


def _pack_small(native, dmods, late):
    pieces, mods, starts, n_sum = _small_plan(late)
    B = dmods[mods[0]].shape[0]
    total = n_sum + 8 * len(mods)

    def body(*refs):
        xs, ms, o_ref = refs[:len(pieces)], refs[len(pieces):-1], refs[-1]
        o_ref[...] = jnp.zeros_like(o_ref)
        for (name, rows, cols), x_ref in zip(pieces, xs):
            chunks = -(-cols // LANES)
            if chunks == 1 and rows % 8 == 0:
                o_ref[starts[name]:starts[name] + rows, 0:cols] = x_ref[...]
                continue
            for i in range(rows):
                for q in range(chunks):
                    wd = min(LANES, cols - q * LANES)
                    r = starts[name] + i * chunks + q
                    o_ref[r:r + 1, 0:wd] = x_ref[i:i + 1, q * LANES:q * LANES + wd]
        for k, m_ref in enumerate(ms):
            for b in range(B):
                o_ref[n_sum + 8 * k + b:n_sum + 8 * k + b + 1, :] = m_ref[b]

    return _pcall(body, name="pack_small_late" if late else "pack_small_early", out_shape=_sds((total, LANES), F32))(
        *[native[n] for n, _, _ in pieces], *[dmods[k] for k in mods])


def _sum_unpack_small(gathered_early, gathered_late, B):
    plans = [_small_plan(False), _small_plan(True)]
    nd = gathered_early.shape[0]
    n_out = len(NATIVE)

    def body(*refs):
        g_refs, outs, dm_ref, accs = refs[0:2], refs[2:2 + n_out], refs[2 + n_out], refs[3 + n_out:]
        o = 0
        for g_ref, acc, (pieces, mods, starts, n_sum) in zip(g_refs, accs, plans):
            s = g_ref[0, 0:n_sum, :]
            for d in range(1, nd):
                s = s + g_ref[d, 0:n_sum, :]
            acc[...] = s
            for name, rows, cols in pieces:
                o_ref = outs[o]
                o += 1
                chunks = -(-cols // LANES)
                if chunks == 1 and rows % 8 == 0:
                    o_ref[...] = acc[starts[name]:starts[name] + rows, 0:cols]
                    continue
                for i in range(rows):
                    for q in range(chunks):
                        wd = min(LANES, cols - q * LANES)
                        r = starts[name] + i * chunks + q
                        o_ref[i:i + 1, q * LANES:q * LANES + wd] = acc[r:r + 1, 0:wd]
            for d in range(nd):
                for j, k in enumerate(mods):
                    dm_ref[d, :, k * D_MODEL:(k + 1) * D_MODEL] = g_ref[d, n_sum + 8 * j:n_sum + 8 * j + B, :]

    ordered = [p for pieces, _, _, _ in plans for p in pieces]
    out_shape = tuple(_sds((rows, cols), F32) for _, rows, cols in ordered) + (_sds((nd, B, N_MOD * D_MODEL), F32),)
    res = _pcall(body, name="sum_unpack_small", out_shape=out_shape,
                 scratch_shapes=[pltpu.VMEM((n_sum, LANES), F32) for _, _, _, n_sum in plans])(gathered_early, gathered_late)
    return {n: r for (n, _, _), r in zip(ordered, res[:-1])}, res[-1]


def _small_from_native(nat):
    lanes3 = lambda a: a.reshape(SSM_GROUP_CH, SSM_GROUPS, SSM_STATE)
    return dict(
        g_mix=nat["g_mix"].reshape(D_MODEL), b_gate=jnp.concatenate([nat["b_att"], nat["b_ssm"]], axis=1).reshape(2 * D_MODEL),
        a_re=nat["a_re"].reshape(SSM_GROUPS, SSM_STATE), a_im=nat["a_im"].reshape(SSM_GROUPS, SSM_STATE),
        log_dt=nat["log_dt"][0, :SSM_GROUPS], b_re=_groups_from_lanes(nat["b_re"]), b_im=_groups_from_lanes(nat["b_im"]),
        c_re=lanes3(nat["c_re"]).transpose(1, 0, 2), c_im=lanes3(nat["c_im"]).transpose(1, 0, 2),
        d_skip=nat["d_skip"].reshape(SSM_WIDTH), b_glu=nat["b_glu"].reshape(SSM_WIDTH), g_ffn=nat["g_ffn"].reshape(D_MODEL),
        w_conv=nat["w_conv"], b_conv=nat["b_conv"].reshape(D_FF), g_final=nat["g_final"].reshape(D_MODEL))


def _adamw_multi(params):
    n = len(params)
    bc1 = 1.0 - ADAM_B1 ** ADAM_STEP
    bc2 = 1.0 - ADAM_B2 ** ADAM_STEP

    def body(*refs):
        ins, outs = refs[:4 * n], refs[4 * n:]
        for i in range(n):
            w_ref, g_ref, m_ref, v_ref = ins[4 * i:4 * i + 4]
            d_ref, nm_ref, nv_ref = outs[3 * i:3 * i + 3]
            g = g_ref[...]
            m = ADAM_B1 * m_ref[...] + (1.0 - ADAM_B1) * g
            v = ADAM_B2 * v_ref[...] + (1.0 - ADAM_B2) * (g * g)
            nm_ref[...] = m
            nv_ref[...] = v
            d_ref[...] = -ADAM_LR * ((m / bc1) / (jnp.sqrt(v / bc2) + ADAM_EPS) + ADAM_WD * w_ref[...])

    flat = [a for p in params for a in p]
    out_shape = tuple(_sds(p[0].shape, F32) for p in params for _ in range(3))
    res = _pcall(body, name="adamw_small", out_shape=out_shape)(*flat)
    return [tuple(res[3 * i:3 * i + 3]) for i in range(n)]


def _adamw(w, g, m, v, name, g_other=None):
    R, N = w.shape
    tr = _rows_tile(R, 256)
    spec = pl.BlockSpec((tr, N), lambda i: (i, 0))
    bc1 = 1.0 - ADAM_B1 ** ADAM_STEP
    bc2 = 1.0 - ADAM_B2 ** ADAM_STEP
    two = g_other is not None

    def body(*refs):
        w_ref, g_ref, m_ref, v_ref = refs[:4]
        d_ref, nm_ref, nv_ref = refs[4 + two:7 + two]
        g = g_ref[...]
        if two:
            g = g + refs[4][...]
            refs[8][...] = g
        m = ADAM_B1 * m_ref[...] + (1.0 - ADAM_B1) * g
        v = ADAM_B2 * v_ref[...] + (1.0 - ADAM_B2) * (g * g)
        nm_ref[...] = m
        nv_ref[...] = v
        d_ref[...] = -ADAM_LR * ((m / bc1) / (jnp.sqrt(v / bc2) + ADAM_EPS) + ADAM_WD * w_ref[...])

    shp = _sds((R, N), F32)
    args = (w, g, m, v) + ((g_other,) if two else ())
    return _pcall(body, name=name, out_shape=(shp,) * (3 + two), grid=(R // tr,), in_specs=[spec] * len(args),
                  out_specs=(spec,) * (3 + two), dims=("parallel",))(*args)


_GROUP_MASKS = {
    "all": [(dx, dy, dc) for dx in (0, 1) for dy in (0, 1) for dc in (0, 1) if (dx, dy, dc) != (0, 0, 0)],
    "xy": [(1, 0, 0), (0, 1, 0), (1, 1, 0)],
    "c": [(0, 0, 1)],
}
_GROUP_SLOTS = {"all": 8, "xy": 4, "c": 2}


def _group_slot(group, x, y, c):
    return {"all": 4 * x + 2 * y + c, "xy": 2 * x + y, "c": c}[group]


def _flip(v, d):
    return 1 - v if d else v


def _exchange(arr, group, mode, name):
    return _exchange_list([arr], group, mode, name)[0]


def _exchange_list(arrs, group, mode, name):
    masks = _GROUP_MASKS[group]
    n = len(masks)
    na = len(arrs)
    assert mode in ("gather", "swap") and (mode == "gather" or group == "c")
    has_local = mode == "gather"
    out_shapes = [((_GROUP_SLOTS[group],) if has_local else ()) + arr.shape for arr in arrs]
    bounce = [pltpu.VMEM(arr.shape, arr.dtype) for arr in arrs] if has_local else []

    def body(*refs):
        x_refs, o_refs = refs[:na], refs[na:2 * na]
        send_sems, recv_sems = refs[2 * na], refs[2 * na + 1]
        x, y, c = lax.axis_index("x"), lax.axis_index("y"), lax.axis_index("c")
        me = _group_slot(group, x, y, c)
        if has_local:
            local_sems = refs[2 * na + 2]
            bufs = refs[2 * na + 3:]
            loads = []
            for i in range(na):
                loads.append(pltpu.make_async_copy(x_refs[i], bufs[i], local_sems.at[2 * i]))
                loads[-1].start()
        copies = []
        for i in range(na):
            x_ref, o_ref = x_refs[i], o_refs[i]
            for k, (dx, dy, dc) in enumerate(masks):
                px, py, pc = _flip(x, dx), _flip(y, dy), _flip(c, dc)
                src, dst = (x_ref, o_ref.at[me]) if has_local else (x_ref, o_ref)
                cp =pltpu.make_async_remote_copy(src_ref=src, dst_ref=dst, send_sem=send_sems.at[i * n + k],
                                                  recv_sem=recv_sems.at[i * n + k], device_id=(px, py, pc),
                                                  device_id_type=pl.DeviceIdType.MESH)
                cp.start()
                copies.append(cp)
        if has_local:
            stores = []
            for i in range(na):
                loads[i].wait()
                stores.append(pltpu.make_async_copy(bufs[i], o_refs[i].at[me], local_sems.at[2 * i + 1]))
                stores[-1].start()
        for cp in copies:
            cp.wait()
        if has_local:
            for st in stores:
                st.wait()

    anyspec = pl.BlockSpec(memory_space=pl.ANY)
    scratch = [pltpu.SemaphoreType.DMA((n * na,)), pltpu.SemaphoreType.DMA((n * na,))]
    if has_local:
        scratch += [pltpu.SemaphoreType.DMA((2 * na,))] + bounce
    outs = pl.pallas_call(body, name=name, out_shape=tuple(_sds(s, a.dtype) for s, a in zip(out_shapes, arrs)),
                          in_specs=[anyspec] * na, out_specs=tuple([anyspec] * na), scratch_shapes=scratch,
                          compiler_params=pltpu.CompilerParams(vmem_limit_bytes=V7X_VMEM_LIMIT_BYTES))(*arrs)
    return list(outs)


def _gather_weights(shards, name):
    na = len(shards)
    masks = _GROUP_MASKS["xy"]
    n = len(masks)

    def body(*refs):
        x_refs, o_refs = refs[:na], refs[na:2 * na]
        send_sems, recv_sems, local_sems = refs[2 * na:2 * na + 3]
        bufs = refs[2 * na + 3:]
        x, y, c = lax.axis_index("x"), lax.axis_index("y"), lax.axis_index("c")
        me = 2 * x + y
        sibling = (x, y, 1 - c)
        loads = []
        for i in range(na):
            loads.append(pltpu.make_async_copy(x_refs[i], bufs[i], local_sems.at[2 * i]))
            loads[-1].start()

        def half_of(i, slot, cc):
            h = shards[i].shape[0] // 2
            return o_refs[i].at[slot, pl.ds(pl.multiple_of(cc * h, 8), h), :]

        def src_half(i, cc):
            h = shards[i].shape[0] // 2
            return x_refs[i].at[pl.ds(pl.multiple_of(cc * h, 8), h), :]

        sends = []
        for i in range(na):
            for k, (dx, dy, _) in enumerate(masks):
                cp = pltpu.make_async_remote_copy(src_ref=src_half(i, c), dst_ref=half_of(i, me, c),
                                                  send_sem=send_sems.at[i * 2 * n + k], recv_sem=recv_sems.at[i * 2 * n + k],
                                                  device_id=(_flip(x, dx), _flip(y, dy), c),
                                                  device_id_type=pl.DeviceIdType.MESH)
                cp.start()
                sends.append(cp)
        stores = []
        for i in range(na):
            loads[i].wait()
            stores.append(pltpu.make_async_copy(bufs[i], o_refs[i].at[me], local_sems.at[2 * i + 1]))
            stores[-1].start()
        for i in range(na):
            for k, (dx, dy, _) in enumerate(masks):
                slot = 2 * _flip(x, dx) + _flip(y, dy)
                landed = pltpu.make_async_remote_copy(src_ref=src_half(i, c), dst_ref=half_of(i, slot, c),
                                                      send_sem=send_sems.at[i * 2 * n + k],
                                                      recv_sem=recv_sems.at[i * 2 * n + k], device_id=sibling,
                                                      device_id_type=pl.DeviceIdType.MESH)
                landed.wait_recv()
                fwd = pltpu.make_async_remote_copy(src_ref=half_of(i, slot, c), dst_ref=half_of(i, slot, c),
                                                   send_sem=send_sems.at[i * 2 * n + n + k],
                                                   recv_sem=recv_sems.at[i * 2 * n + n + k], device_id=sibling,
                                                   device_id_type=pl.DeviceIdType.MESH)
                fwd.start()
                sends.append(fwd)
        for i in range(na):
            for k, (dx, dy, _) in enumerate(masks):
                slot = 2 * _flip(x, dx) + _flip(y, dy)
                pltpu.make_async_remote_copy(src_ref=half_of(i, slot, 1 - c), dst_ref=half_of(i, slot, 1 - c),
                                             send_sem=send_sems.at[i * 2 * n + n + k],
                                             recv_sem=recv_sems.at[i * 2 * n + n + k], device_id=sibling,
                                             device_id_type=pl.DeviceIdType.MESH).wait_recv()
        for cp in sends:
            cp.wait_send()
        for st in stores:
            st.wait()

    anyspec = pl.BlockSpec(memory_space=pl.ANY)
    scratch = [pltpu.SemaphoreType.DMA((2 * n * na,)), pltpu.SemaphoreType.DMA((2 * n * na,)),
               pltpu.SemaphoreType.DMA((2 * na,))] + [pltpu.VMEM(s.shape, s.dtype) for s in shards]
    outs = pl.pallas_call(body, name=name, out_shape=tuple(_sds((N_XY,) + s.shape, s.dtype) for s in shards),
                          in_specs=[anyspec] * na, out_specs=tuple([anyspec] * na), scratch_shapes=scratch,
                          compiler_params=pltpu.CompilerParams(vmem_limit_bytes=V7X_VMEM_LIMIT_BYTES))(*shards)
    return list(outs)


BIG = (("w_proj_att", (ATT_WIDTH, D_MODEL), 1), ("w_proj_ssm", (SSM_WIDTH, D_MODEL), 1),
       ("w_glu", (SSM_WIDTH, SSM_WIDTH), 0))
DIRECT = (("w_in", True), ("w_up", True), ("w_down", False), ("w_out", False))
N_XY = 4


def _big_rows(shape):
    return shape[0] * shape[1] // N_XY // LANES


FLAT_ROWS = sum(_big_rows(s) for _, s, _ in BIG)


def _shard_shape(shape, axis):
    return (shape[0] // N_XY, shape[1]) if axis == 0 else (shape[0], shape[1] // N_XY)


def _flatten_shards(shards):
    return jnp.concatenate([shards[n].reshape(_big_rows(s), LANES) for n, s, _ in BIG], axis=0)


def _unflatten_shard(flat):
    out, r = {}, 0
    for n, s, ax in BIG:
        k = _big_rows(s)
        out[n] = flat[r:r + k].reshape(_shard_shape(s, ax))
        r += k
    return out


def _unflatten_full(flat4):
    out, r = {}, 0
    for n, s, ax in BIG:
        k = _big_rows(s)
        sh = _shard_shape(s, ax)
        t = flat4[:, r:r + k].reshape((N_XY,) + sh)
        out[n] = t.reshape(s) if ax == 0 else t.transpose(1, 0, 2).reshape(s)
        r += k
    return out


def _flatten_full(full):
    parts = []
    for n, s, ax in BIG:
        sh = _shard_shape(s, ax)
        t = full[n]
        t = t.reshape((N_XY,) + sh) if ax == 0 else t.reshape(s[0], N_XY, sh[1]).transpose(1, 0, 2)
        parts.append(t.reshape(N_XY, _big_rows(s), LANES))
    return jnp.concatenate(parts, axis=1)


def _lanes_from_groups(a):
    return a.transpose(2, 0, 1).reshape(SSM_GROUP_CH, SSM_LANES)


def _groups_from_lanes(a):
    return a.reshape(SSM_GROUP_CH, SSM_GROUPS, SSM_STATE).transpose(1, 2, 0)


LATE = ("w_up_t", "w_down", "w_out")
EARLY_GRADS = ("w_up_t", "w_down", "w_out")


def _local_step(x3, mod, tgt3, W, P, late_shards=None, scatter_grads=False):
    B, S, _ = x3.shape
    T = B * S
    seq_blocks = S // ATT_BLOCK
    sh1, sc1, gt1, sh2, sc2, gt2 = [m.reshape(B, 1, D_MODEL) for m in jnp.split(mod, 6, axis=-1)]
    g_mix, g_ffn, g_final = P["g_mix"].reshape(1, D_MODEL), P["g_ffn"].reshape(1, D_MODEL), P["g_final"].reshape(1, D_MODEL)
    b_gate = P["b_gate"].reshape(1, 2 * D_MODEL)
    d_skip, b_glu = P["d_skip"].reshape(1, SSM_WIDTH), P["b_glu"].reshape(1, SSM_WIDTH)
    w_conv, b_conv = P["w_conv"], P["b_conv"].reshape(1, D_FF)

    u1 = _norm_mod(x3, g_mix, sc1, sh1).reshape(T, D_MODEL)
    proj = _mm(u1, W["w_in_t"], tb=True, name="mm_proj", out_dtype=BF16)
    proj3 = proj.reshape(B, S, IN_WIDTH)
    us = proj[:, 3 * ATT_WIDTH:3 * ATT_WIDTH + SSM_WIDTH]
    o_att3, lse4, late = _attention_fwd(proj3, seq_blocks, _Riders(late_shards, "gather") if late_shards else None)
    if late_shards:
        W = dict(W, **{n: f.reshape(-1, LANES) for n, f in zip(LATE, late)})
        w_conv = late[len(LATE)].transpose(1, 0, 2).reshape(3, D_FF)
        W.update(_unflatten_full(late[len(LATE) + 1]))
    o_att = o_att3.reshape(T, ATT_WIDTH)
    y_att = _mm(o_att, W["w_proj_att"], name="mm_proj_att", out_dtype=BF16)

    lr = P["a_re"].reshape(1, SSM_LANES)
    li = P["a_im"].reshape(1, SSM_LANES)
    ldt = jnp.repeat(P["log_dt"], SSM_STATE).reshape(1, SSM_LANES)
    br, bi = _lanes_from_groups(P["b_re"]), _lanes_from_groups(P["b_im"])
    cr = P["c_re"].transpose(1, 0, 2).reshape(SSM_GROUP_CH, SSM_LANES)
    ci = P["c_im"].transpose(1, 0, 2).reshape(SSM_GROUP_CH, SSM_LANES)
    abar, w_bu, w_c = _ssm_params(lr, li, ldt, br, bi, cr, ci)
    xs3, y_core3 = _ssm_scan_fwd(proj3, abar, w_bu, w_c)
    y5, s_out = _ssm_post(y_core3.reshape(T, SSM_WIDTH), us, d_skip, W["w_glu"], b_glu)
    y_ssm = _mm(s_out, W["w_proj_ssm"], name="mm_proj_ssm", out_dtype=BF16)

    merged = _merge(proj, y_att, y_ssm, b_gate)
    mix = _mm(merged, W["w_out"], name="mm_out", out_dtype=BF16)
    mix3 = mix.reshape(B, S, D_MODEL)

    h1, u2 = _resid_norm_mod(x3, mix3, gt1, g_ffn, sc2, sh2)
    u2 = u2.reshape(T, D_MODEL)
    up3 = _mm(u2, W["w_up_t"], tb=True, name="mm_up", out_dtype=BF16).reshape(B, S, 2 * D_FF)
    act = _conv_act(up3, w_conv, b_conv).reshape(T, D_FF)
    ffn3 = _mm(act, W["w_down"], name="mm_down", out_dtype=BF16).reshape(B, S, D_MODEL)
    dh2, dffn, dgt2, dg_final, loss = _final_loss(h1, ffn3, tgt3, gt2, g_final)

    dffn = dffn.reshape(T, D_MODEL)
    gw = {}
    gw["w_down"] = _mm(act, dffn, ta=True, out_dtype=BF16, name="mm_dw_down")
    dact3 = _mm(dffn, W["w_down"], tb=True, name="mm_dact", out_dtype=BF16).reshape(B, S, D_FF)
    dup3, dw_conv, db_conv = _conv_bwd(up3, dact3, w_conv, b_conv)
    dup = dup3.reshape(2, T, D_FF)
    gw["w_up_t"] = _mm(dup, u2, ta=True, out_dtype=BF16, name="mm_dw_up")
    du2 = _mm(dup, W["w_up_t"], name="mm_du2", out_dtype=BF16).reshape(B, S, D_MODEL)
    dh1, dsh2, dsc2, dg_ffn, dgt1, dmix = _norm_bwd(h1, du2, dh2, g_ffn, sc2, "norm_bwd2", mix3=mix3, gt=gt1)

    dmix = dmix.reshape(T, D_MODEL)
    gw["w_out"] = _mm(merged, dmix, ta=True, out_dtype=BF16, name="mm_dw_out")
    dmerged = _mm(dmix, W["w_out"], tb=True, name="mm_dmerged", out_dtype=BF16)
    dy_att, dy_ssm, dga, dgs, db_att, db_ssm = _merge_bwd(proj, y_att, y_ssm, b_gate, dmerged)

    gw["w_proj_ssm"] = _mm(s_out, dy_ssm, ta=True, name="mm_dw_proj_ssm")
    ds_out = _mm(dy_ssm, W["w_proj_ssm"], tb=True, name="mm_ds_out")
    dy5, dd_skip, db_glu, dw_glu = _ssm_post_bwd(y5, us, ds_out, d_skip, W["w_glu"], b_glu)
    gw["w_glu"] = dw_glu
    dus3, dab, dwbu, dwc = _ssm_scan_bwd(proj3, dy5.reshape(B, S, SSM_WIDTH), xs3, abar, w_bu, w_c, d_skip)
    dus = dus3.reshape(T, SSM_WIDTH)
    dlr, dli, dldt, dbr, dbi, dcr, dci = _ssm_params_bwd(lr, li, ldt, br, bi, dab, dwbu, dwc)

    gw["w_proj_att"] = _mm(o_att, dy_att, ta=True, name="mm_dw_proj_att")
    do_att = _mm(dy_att, W["w_proj_att"], tb=True, out_dtype=BF16, name="mm_do_att")
    early = [gw[n].reshape(N_XY, -1, LANES) for n in EARLY_GRADS]
    early.append(_flatten_full({n: gw[n].astype(BF16) for n, _, _ in BIG}))
    dq3, dk3, dv3, parts = _attention_bwd(proj3, do_att.reshape(B, S, ATT_WIDTH), o_att3, lse4, seq_blocks,
                                          _Riders(early, "scatter") if scatter_grads else None)
    dproj = jnp.concatenate([t.reshape(T, ATT_WIDTH) for t in (dq3, dk3, dv3)] + [dus, dga, dgs], axis=1)
    dmods = [None, None, dgt1, dsh2, dsc2, dgt2]
    native = dict(b_att=db_att, b_ssm=db_ssm, a_re=dlr, a_im=dli, log_dt=dldt, b_re=dbr, b_im=dbi, c_re=dcr, c_im=dci,
                  d_skip=dd_skip, b_glu=db_glu, g_ffn=dg_ffn, w_conv=dw_conv, b_conv=db_conv, g_final=dg_final, loss=loss)
    small_early = _pack_small(native, dmods, False)
    sums, sums_sib, last_parts = [], [], []
    if scatter_grads:
        sums = [_sum_slots(p, "sum_chips_%d" % i) for i, p in enumerate(parts)]
        riders = _RiderGroup([_Riders([small_early], "gather", "all"), _Riders(sums, "swap", "c")])
        gw["w_in_t"], rode = _mm(dproj, u1, ta=True, out_dtype=BF16, name="mm_dw_in", riders=riders)
        small_early, sums_sib = rode[0], rode[1:]
        du1, last_parts = _mm(dproj, W["w_in_t"], name="mm_du1", out_dtype=BF16,
                              riders=_Riders([gw["w_in_t"].reshape(N_XY, -1, LANES)], "scatter"))
    else:
        gw["w_in_t"] = _mm(dproj, u1, ta=True, out_dtype=BF16, name="mm_dw_in")
        du1 = _mm(dproj, W["w_in_t"], name="mm_du1", out_dtype=BF16)
    du1 = du1.reshape(B, S, D_MODEL)
    dx, dsh1, dsc1, dg_mix = _norm_bwd(x3, du1, dh1, g_mix, sc1, "norm_bwd1")
    dmods[0], dmods[1] = dsh1, dsc1
    native["g_mix"] = dg_mix
    return loss, dx, dmods, gw, native, (sums, sums_sib, last_parts), small_early


WEIGHTS = ['w_ada', 'b_ada', 'g_mix', 'w_in', 'b_gate', 'a_re', 'a_im', 'log_dt', 'b_re', 'b_im', 'c_re', 'c_im', 'd_skip',
           'w_glu', 'b_glu', 'w_proj_att', 'w_proj_ssm', 'w_out', 'g_ffn', 'w_up', 'w_conv', 'b_conv', 'w_down', 'g_final']
SMALL = ['g_mix', 'b_gate', 'a_re', 'a_im', 'log_dt', 'b_re', 'b_im', 'c_re', 'c_im', 'd_skip', 'b_glu', 'g_ffn', 'w_conv',
         'b_conv', 'g_final']


def kernel(x, c, w_ada, b_ada, g_mix, w_in, b_gate, a_re, a_im, log_dt, b_re, b_im, c_re, c_im, d_skip, w_glu, b_glu, w_proj_att, w_proj_ssm, w_out, g_ffn, w_up, w_conv, b_conv, w_down, g_final, loss_target, m_w_ada, m_b_ada, m_g_mix, m_w_in, m_b_gate, m_a_re, m_a_im, m_log_dt, m_b_re, m_b_im, m_c_re, m_c_im, m_d_skip, m_w_glu, m_b_glu, m_w_proj_att, m_w_proj_ssm, m_w_out, m_g_ffn, m_w_up, m_w_conv, m_b_conv, m_w_down, m_g_final, v_w_ada, v_b_ada, v_g_mix, v_w_in, v_b_gate, v_a_re, v_a_im, v_log_dt, v_b_re, v_b_im, v_c_re, v_c_im, v_d_skip, v_w_glu, v_b_glu, v_w_proj_att, v_w_proj_ssm, v_w_out, v_g_ffn, v_w_up, v_w_conv, v_b_conv, v_w_down, v_g_final):
    args = dict(locals())
    w = {n: args[n] for n in WEIGHTS}
    m = {n: args["m_" + n] for n in WEIGHTS}
    v = {n: args["v_" + n] for n in WEIGHTS}
    B, S, _ = x.shape
    ix, iy, ic = lax.axis_index("x"), lax.axis_index("y"), lax.axis_index("c")
    chip = 2 * ix + iy
    ada_cols = w_ada.shape[2]

    c_all = _exchange(c, "all", "gather", "gather_c").reshape(8 * B, D_MODEL)
    b_cols = lax.dynamic_slice_in_dim(b_ada, chip * ada_cols, ada_cols, axis=1)
    mod_cols = _ada_fwd(c_all, w_ada[0], b_cols)
    mod_all = _exchange(mod_cols, "xy", "gather", "gather_mod")
    mod_all = mod_all.transpose(1, 0, 2).reshape(8 * B, 6 * D_MODEL)
    mod = lax.dynamic_slice_in_dim(mod_all, (4 * ix + 2 * iy + ic) * B, B, axis=0)

    shard = {n + ("_t" if t else ""): (w[n][0].T if t else w[n][0]).astype(BF16) for n, t in DIRECT}
    misc = _flatten_shards({n: w[n][0] for n, _, _ in BIG}).astype(BF16)
    (w_in_full,) = _gather_weights([shard["w_in_t"]], "gather_weights")
    W = {"w_in_t": w_in_full.reshape(-1, LANES)}

    P = {n: w[n][0] for n in SMALL if n not in ("w_conv", "g_final")}
    P["w_conv"] = None
    P["g_final"] = g_final

    loss, dx, dmods, gw, native, parts, small_early = _local_step(x, mod, loss_target, W, P,
                                                                  [shard[n] for n in LATE] + [w_conv[0], misc], True)

    small_late = _exchange(_pack_small(native, dmods, True), "all", "gather", "gather_small")
    native_sum, dmod_all = _sum_unpack_small(small_early, small_late, B)
    loss = native_sum["loss"][0, 0]
    g_small = _small_from_native(native_sum)
    dmod_all = dmod_all.reshape(8 * B, N_MOD * D_MODEL)
    dmod_cols = lax.dynamic_slice_in_dim(dmod_all, chip * ada_cols, ada_cols, axis=1)
    g_w_ada, g_b_ada = _ada_bwd(c_all, dmod_all, dmod_cols)

    red, red_sib, last_parts = parts
    red = red + [_sum_slots(last_parts[0], "sum_chips_w_in")]
    red_sib = red_sib + [_exchange(red[-1], "c", "swap", "share_cores")]
    order = list(EARLY_GRADS) + ["misc", "w_in_t"]
    halves = dict(zip(order, zip(red, red_sib)))

    grads = {"w_ada": g_w_ada[None], "b_ada": g_b_ada}
    grads["w_up"] = _add2(*halves["w_up_t"], F32, "add_cores_w_up").T[None]
    for k, gk in _unflatten_shard(_add2(*halves["misc"], F32, "add_cores_misc")).items():
        grads[k] = gk[None]
    wc_cols = w_conv.shape[2]
    for n in SMALL:
        g = g_small[n]
        if n == "w_conv":
            g = lax.dynamic_slice_in_dim(g, chip * wc_cols, wc_cols, axis=1)
        grads[n] = g.reshape(w[n].shape)

    delta, new_m, new_v = {}, {}, {}
    for n in ["w_ada"] + [b for b, _ in DIRECT] + [b for b, _, _ in BIG]:
        shp = w[n].shape
        if n == "w_in":
            r, s = halves["w_in_t"]
            d2, m2, v2, g2 = _adamw(w[n][0].T, r, m[n][0].T, v[n][0].T, "adamw_" + n, g_other=s)
            d2, m2, v2, grads[n] = d2.T, m2.T, v2.T, g2.T[None]
        elif n in ("w_down", "w_out"):
            r, s = halves[n]
            d2, m2, v2, g2 = _adamw(w[n][0], r, m[n][0], v[n][0], "adamw_" + n, g_other=s)
            grads[n] = g2[None]
        else:
            d2, m2, v2 = _adamw(w[n][0], grads[n][0], m[n][0], v[n][0], "adamw_" + n)
        delta[n], new_m[n], new_v[n] = d2.reshape(shp), m2.reshape(shp), v2.reshape(shp)
    rest = ["b_ada"] + SMALL

    def drop(a):
        return a.reshape(1, -1) if a.ndim == 1 else (a if a.ndim == 2 else a[0])

    upd = _adamw_multi([(drop(w[n]), drop(grads[n]), drop(m[n]), drop(v[n])) for n in rest])
    for n, (dd, mm, vv) in zip(rest, upd):
        delta[n], new_m[n], new_v[n] = dd.reshape(w[n].shape), mm.reshape(w[n].shape), vv.reshape(w[n].shape)

    return (loss, dx, *[grads[n] for n in WEIGHTS], *[delta[n] for n in WEIGHTS], *[new_m[n] for n in WEIGHTS],
            *[new_v[n] for n in WEIGHTS])
```

```python
import functools
import math

import jax
import jax.numpy as jnp
from jax import lax
from jax.experimental import pallas as pl
from jax.experimental.pallas import tpu as pltpu

F32, BF16 = jnp.float32, jnp.bfloat16

D_MODEL = 1024
N_HEADS = 8
HEAD_DIM = 64
ATT_WIDTH = 512
SSM_GROUPS = 16
SSM_GROUP_CH = 16
SSM_WIDTH = 256
SSM_STATE = 64
SSM_LANES = SSM_GROUPS * SSM_STATE
D_FF = 2048
IN_WIDTH = 3 * ATT_WIDTH + SSM_WIDTH + 2 * D_MODEL
ATT_BLOCK = 128
N_PATTERNS = 3
EPS = 1e-6
NEG_INF = -1e30

ADAM_LR, ADAM_B1, ADAM_B2, ADAM_EPS, ADAM_WD, ADAM_STEP = 0.001, 0.9, 0.999, 1e-08, 0.01, 10

V7X_VMEM_LIMIT_BYTES = 56 * 1024 * 1024
LANES = 1024


def _pcall(body, *, name, out_shape, grid=(), in_specs=None, out_specs=None, scratch_shapes=(), dims=None):
    params = dict(vmem_limit_bytes=V7X_VMEM_LIMIT_BYTES)
    if dims is not None:
        params["dimension_semantics"] = dims
    specs = {}
    if in_specs is not None:
        specs = dict(grid=grid, in_specs=in_specs, out_specs=out_specs)
    return pl.pallas_call(body, name=name, out_shape=out_shape, scratch_shapes=scratch_shapes,
                          compiler_params=pltpu.CompilerParams(**params), **specs)


def _sds(shape, dtype):
    return jax.ShapeDtypeStruct(tuple(shape), dtype)


def _tile(n, target):
    if n <= target:
        return n
    for t in range(target - target % 128, 0, -128):
        if n % t == 0:
            return t
    raise ValueError((n, target))


def _sig(v):
    return pl.reciprocal(1.0 + jnp.exp(-v), approx=True)


def _mm(a, b, *, name, ta=False, tb=False, out_dtype=F32, tm=2048, tn=1024, tk=1024, riders=None):
    halves = a.ndim == 3
    if halves:
        a_rows, a_cols = a.shape[1], 2 * a.shape[2]
    else:
        a_rows, a_cols = a.shape
    if ta:
        K, M = a_rows, a_cols
    else:
        M, K = a_rows, a_cols
    if tb:
        N, K2 = b.shape
    else:
        K2, N = b.shape
    assert K == K2, (a.shape, b.shape)
    if halves:
        tm, tk = (min(tm, M // 2), tk) if ta else (tm, min(tk, K // 2))
    tm, tn, tk = _tile(M, tm), _tile(N, tn), _tile(K, tk)
    nk = K // tk
    if halves and ta:
        per = a.shape[2] // tm
        a_spec = pl.BlockSpec((None, tk, tm), lambda i, j, k: (i // per, k, i % per))
    elif halves:
        per = a.shape[2] // tk
        a_spec = pl.BlockSpec((None, tm, tk), lambda i, j, k: (k // per, i, k % per))
    else:
        a_spec = pl.BlockSpec((tk, tm), lambda i, j, k: (k, i)) if ta else pl.BlockSpec((tm, tk), lambda i, j, k: (i, k))
    b_spec = pl.BlockSpec((tn, tk), lambda i, j, k: (j, k)) if tb else pl.BlockSpec((tk, tn), lambda i, j, k: (k, j))
    dn = (((0 if ta else 1,), (1 if tb else 0,)), ((), ()))

    def body(a_ref, b_ref, o_ref, acc_ref):
        k = pl.program_id(2)

        @pl.when(k == 0)
        def _():
            acc_ref[...] = jnp.zeros_like(acc_ref)

        acc_ref[...] += lax.dot_general(a_ref[...].astype(BF16), b_ref[...].astype(BF16), dn,
                                        preferred_element_type=F32)

        @pl.when(k == nk - 1)
        def _():
            o_ref[...] = acc_ref[...].astype(out_dtype)

    def body_single(a_ref, b_ref, o_ref):
        o_ref[...] = lax.dot_general(a_ref[...].astype(BF16), b_ref[...].astype(BF16), dn,
                                     preferred_element_type=F32).astype(out_dtype)

    grid = (M // tm, N // tn, nk)
    scratch = [] if nk == 1 else [pltpu.VMEM((tm, tn), F32)]
    o_spec = pl.BlockSpec((tm, tn), lambda i, j, k: (i, j))
    if riders is None:
        return _pcall(body_single if nk == 1 else body, name=name, out_shape=_sds((M, N), out_dtype), grid=grid,
                      in_specs=[a_spec, b_spec], out_specs=o_spec, scratch_shapes=scratch,
                      dims=("parallel", "parallel", "arbitrary"))(a, b)
    rs = riders
    res = _pcall(_with_riders(body_single if nk == 1 else body, rs, 2, 1, len(scratch), tuple(g - 1 for g in grid)),
                 name=name, out_shape=(_sds((M, N), out_dtype),) + tuple(rs.out_shape), grid=grid,
                 in_specs=[a_spec, b_spec] + rs.specs, out_specs=(o_spec,) + tuple(rs.specs),
                 scratch_shapes=scratch + rs.scratch, dims=("arbitrary", "arbitrary", "arbitrary"))(a, b, *rs.arrs)
    return res[0], list(res[1:])


def _ada_fwd(c_all, w_ada, b_ada_cols):
    n = w_ada.shape[1]

    def body(c_ref, w_ref, b_ref, o_ref):
        c = c_ref[...]
        act = c * _sig(c)
        o_ref[...] = jnp.dot(act.astype(BF16), w_ref[...].astype(BF16), preferred_element_type=F32) + b_ref[...]

    return _pcall(body, name="ada_fwd", out_shape=_sds((c_all.shape[0], n), F32))(c_all, w_ada, b_ada_cols)


def _ada_bwd(c_all, dmod_all, dmod_cols):
    n = dmod_cols.shape[1]

    def body(c_ref, da_ref, dc_ref, gw_ref, gb_ref):
        c = c_ref[...]
        act = c * _sig(c)
        gw_ref[...] = lax.dot_general(act, dc_ref[...], (((0,), (0,)), ((), ())), preferred_element_type=F32,
                                      precision=lax.Precision.HIGHEST)
        gb_ref[...] = jnp.sum(da_ref[...], axis=0, keepdims=True)

    return _pcall(body, name="ada_bwd", out_shape=(_sds((D_MODEL, n), F32), _sds((1, dmod_all.shape[1]), F32)))(
        c_all, dmod_all, dmod_cols)


ROW_TILE = 1024


def _row_specs(B, S):
    ts = min(S, ROW_TILE)
    row = pl.BlockSpec((1, ts, D_MODEL), lambda b, s: (b, s, 0))
    bvec = pl.BlockSpec((1, 1, D_MODEL), lambda b, s: (b, 0, 0))
    gvec = pl.BlockSpec((1, D_MODEL), lambda b, s: (0, 0))
    return ts, row, bvec, gvec


def _norm_mod(x3, g, sc, sh):
    B, S, _ = x3.shape
    ts, row, bvec, gvec = _row_specs(B, S)

    def body(x_ref, g_ref, sc_ref, sh_ref, u_ref):
        x = x_ref[0]
        r = lax.rsqrt(jnp.mean(x * x, axis=-1, keepdims=True) + EPS)
        u_ref[0] = ((x * r) * g_ref[...] * (1.0 + sc_ref[0]) + sh_ref[0]).astype(BF16)

    return _pcall(body, name="norm_mod1", out_shape=_sds(x3.shape, BF16), grid=(B, S // ts),
                  in_specs=[row, gvec, bvec, bvec], out_specs=row, dims=("parallel", "parallel"))(x3, g, sc, sh)


def _resid_norm_mod(x3, mix3, gt, g, sc, sh):
    B, S, _ = x3.shape
    ts, row, bvec, gvec = _row_specs(B, S)

    def body(x_ref, m_ref, gt_ref, g_ref, sc_ref, sh_ref, h_ref, u_ref):
        h = x_ref[0] + gt_ref[0] * m_ref[0]
        h_ref[0] = h
        r = lax.rsqrt(jnp.mean(h * h, axis=-1, keepdims=True) + EPS)
        u_ref[0] = ((h * r) * g_ref[...] * (1.0 + sc_ref[0]) + sh_ref[0]).astype(BF16)

    return _pcall(body, name="resid_norm_mod2", out_shape=(_sds(x3.shape, F32), _sds(x3.shape, BF16)),
                  grid=(B, S // ts), in_specs=[row, row, bvec, gvec, bvec, bvec], out_specs=(row, row),
                  dims=("parallel", "parallel"))(x3, mix3, gt, g, sc, sh)


def _norm_bwd(h3, du3, dres3, g, sc, name, mix3=None, gt=None, riders=None):
    B, S, _ = h3.shape
    ts, row, bvec, gvec = _row_specs(B, S)
    with_gate = mix3 is not None

    def body(*refs):
        if with_gate:
            h_ref, du_ref, dr_ref, g_ref, sc_ref, m_ref, gt_ref, dh_ref, dsh_ref, dsc_ref, dg_ref, dgt_ref, dm_ref = refs
        else:
            h_ref, du_ref, dr_ref, g_ref, sc_ref, dh_ref, dsh_ref, dsc_ref, dg_ref = refs
        b, s = pl.program_id(0), pl.program_id(1)
        h = h_ref[0]
        r = lax.rsqrt(jnp.mean(h * h, axis=-1, keepdims=True) + EPS)
        xn = h * r
        du = du_ref[0].astype(F32)
        g = g_ref[...]
        sc1 = 1.0 + sc_ref[0]
        dxn = du * g * sc1
        dh = dr_ref[0].astype(F32) + r * (dxn - xn * jnp.mean(dxn * xn, axis=-1, keepdims=True))
        dh_ref[0] = dh.astype(dh_ref.dtype)

        @pl.when(s == 0)
        def _():
            dsh_ref[...] = jnp.zeros_like(dsh_ref)
            dsc_ref[...] = jnp.zeros_like(dsc_ref)
            if with_gate:
                dgt_ref[...] = jnp.zeros_like(dgt_ref)

        @pl.when((s == 0) & (b == 0))
        def _():
            dg_ref[...] = jnp.zeros_like(dg_ref)

        dux = du * xn
        dsh_ref[0] += jnp.sum(du, axis=0, keepdims=True)
        dsc_ref[0] += jnp.sum(dux * g, axis=0, keepdims=True)
        dg_ref[...] += jnp.sum(dux * sc1, axis=0, keepdims=True)
        if with_gate:
            dgt_ref[0] += jnp.sum(dh * m_ref[0], axis=0, keepdims=True)
            dm_ref[0] = (dh * gt_ref[0]).astype(BF16)

    bshape = _sds((B, 1, D_MODEL), F32)
    in_specs = [row, row, row, gvec, bvec]
    out_shape = [_sds(h3.shape, BF16 if with_gate else F32), bshape, bshape, _sds((1, D_MODEL), F32)]
    out_specs = [row, bvec, bvec, gvec]
    args = [h3, du3, dres3, g, sc]
    if with_gate:
        in_specs += [row, bvec]
        out_shape += [bshape, _sds(h3.shape, BF16)]
        out_specs += [bvec, row]
        args += [mix3, gt]
    if riders is None:
        return _pcall(body, name=name, out_shape=tuple(out_shape), grid=(B, S // ts), in_specs=in_specs,
                      out_specs=tuple(out_specs), dims=("arbitrary", "arbitrary"))(*args)
    rs = riders
    res = _pcall(_with_riders(body, rs, len(args), len(out_shape), 0, (B - 1, S // ts - 1)), name=name,
                 out_shape=tuple(out_shape) + tuple(rs.out_shape), grid=(B, S // ts), in_specs=in_specs + rs.specs,
                 out_specs=tuple(out_specs) + tuple(rs.specs), scratch_shapes=rs.scratch,
                 dims=("arbitrary", "arbitrary"))(*args, *rs.arrs)
    return tuple(res[:len(out_shape)]) + (list(res[len(out_shape):]),)


def _final_loss(h1, ffn3, tgt3, gt, gfin):
    B, S, _ = h1.shape
    ts, row, bvec, gvec = _row_specs(B, S)
    one = pl.BlockSpec((1, 1), lambda b, s: (0, 0))

    def body(h_ref, f_ref, t_ref, gt_ref, gf_ref, dh_ref, dff_ref, dgt_ref, dgf_ref, loss_ref):
        b, s = pl.program_id(0), pl.program_id(1)
        f = f_ref[0].astype(F32)
        gtv = gt_ref[0]
        gf = gf_ref[...]
        h2 = h_ref[0] + gtv * f
        r = lax.rsqrt(jnp.mean(h2 * h2, axis=-1, keepdims=True) + EPS)
        n = h2 * r
        e = n * gf - t_ref[0]
        dy = e * (1.0 / D_MODEL)
        dn = dy * gf
        dh2 = r * (dn - n * jnp.mean(dn * n, axis=-1, keepdims=True))
        dh_ref[0] = dh2.astype(BF16)
        dff_ref[0] = (dh2 * gtv).astype(BF16)

        @pl.when(s == 0)
        def _():
            dgt_ref[...] = jnp.zeros_like(dgt_ref)

        @pl.when((s == 0) & (b == 0))
        def _():
            dgf_ref[...] = jnp.zeros_like(dgf_ref)
            loss_ref[...] = jnp.zeros_like(loss_ref)

        dgt_ref[0] += jnp.sum(dh2 * f, axis=0, keepdims=True)
        dgf_ref[...] += jnp.sum(dy * n, axis=0, keepdims=True)
        rows = jnp.sum(e * e, axis=1, keepdims=True)
        loss_ref[...] += jnp.sum(rows, axis=0, keepdims=True) * (0.5 / D_MODEL)

    return _pcall(body, name="final_loss",
                  out_shape=(_sds(h1.shape, BF16), _sds(h1.shape, BF16), _sds((B, 1, D_MODEL), F32),
                             _sds((1, D_MODEL), F32), _sds((1, 1), F32)),
                  grid=(B, S // ts), in_specs=[row, row, row, bvec, gvec], out_specs=(row, row, bvec, gvec, one),
                  dims=("arbitrary", "arbitrary"))(h1, ffn3, tgt3, gt, gfin)


ATT_GROUP = 4
ATT_GW = ATT_GROUP * HEAD_DIM
ATT_GROUPS = N_HEADS // ATT_GROUP
ATT_PAIRS = ATT_GW // ATT_BLOCK
ATT_UNROLL = 5
ATT_RESIDUE_UNROLL = 8
NT_DIMS = (((1,), (1,)), ((), ()))
TN_DIMS = (((0,), (0,)), ((), ()))


def _att_rows(start, d):
    if d == 1:
        return pl.ds(start if isinstance(start, int) else pl.multiple_of(start, ATT_BLOCK), ATT_BLOCK)
    return pl.ds(start, ATT_BLOCK, stride=d)


def _att_fill_bias(bias_ref, g, d):
    a = lax.broadcasted_iota(jnp.int32, (ATT_BLOCK, ATT_BLOCK), 0)
    j = lax.broadcasted_iota(jnp.int32, (ATT_BLOCK, ATT_BLOCK), 1)
    dist = (a - j).astype(F32)
    for hh in range(ATT_GROUP):
        t, e = divmod(hh, 2)
        rs = slice(e * ATT_BLOCK, (e + 1) * ATT_BLOCK)
        lo = 2.0 ** (-8.0 * (hh + 1) / N_HEADS) * d
        hi = 2.0 ** (-8.0 * (ATT_GROUP + hh + 1) / N_HEADS) * d
        slope = jnp.where(g == 0, lo, hi).astype(F32)
        bias_ref[t, rs, 0:ATT_BLOCK] = jnp.where(a >= j, -slope * dist, NEG_INF)
        bias_ref[t, rs, ATT_BLOCK:] = jnp.where(j >= a, -slope * (dist + float(ATT_BLOCK)), NEG_INF)


def _stack_heads(v2, low):
    return jnp.concatenate([jnp.where(low, v2, 0.0), jnp.where(low, 0.0, v2)], axis=0).astype(BF16)


def _unstack_heads(r2, low):
    return jnp.where(low, r2[0:ATT_BLOCK], r2[ATT_BLOCK:])


class _Riders:
    def __init__(self, arrs, mode, group="xy"):
        self.arrs, self.mode, self.n, self.group = list(arrs), mode, len(arrs), group
        k = len(_GROUP_MASKS[group])
        self.scratch = [pltpu.SemaphoreType.DMA((k * self.n,)), pltpu.SemaphoreType.DMA((k * self.n,))]
        if mode == "swap":
            assert group == "c"
            self.out_shape = [_sds(a.shape, a.dtype) for a in self.arrs]
        else:
            slot_shapes = [a.shape if mode == "gather" else a.shape[1:] for a in self.arrs]
            self.out_shape = [_sds((_GROUP_SLOTS[group],) + s, a.dtype) for s, a in zip(slot_shapes, self.arrs)]
            self.scratch += [pltpu.SemaphoreType.DMA((2 * self.n,))] + [pltpu.VMEM(s, a.dtype)
                                                                        for s, a in zip(slot_shapes, self.arrs)]
        self.specs = [pl.BlockSpec(memory_space=pl.ANY)] * self.n

    def _remote(self, x_refs, o_refs, send_sems, recv_sems):
        x, y, c = lax.axis_index("x"), lax.axis_index("y"), lax.axis_index("c")
        me = _group_slot(self.group, x, y, c)
        masks = _GROUP_MASKS[self.group]
        cps = []
        for i in range(self.n):
            for k, (dx, dy, dc) in enumerate(masks):
                px, py, pc = _flip(x, dx), _flip(y, dy), _flip(c, dc)
                src = x_refs[i].at[_group_slot(self.group, px, py, pc)] if self.mode == "scatter" else x_refs[i]
                dst = o_refs[i] if self.mode == "swap" else o_refs[i].at[me]
                cps.append(pltpu.make_async_remote_copy(
                    src_ref=src, dst_ref=dst, send_sem=send_sems.at[len(masks) * i + k],
                    recv_sem=recv_sems.at[len(masks) * i + k], device_id=(px, py, pc),
                    device_id_type=pl.DeviceIdType.MESH))
        return cps, me

    def start(self, x_refs, o_refs, scratch):
        cps, me = self._remote(x_refs, o_refs, scratch[0], scratch[1])
        for cp in cps:
            cp.start()
        if self.mode == "swap":
            return
        local_sems, bufs = scratch[2], scratch[3:]
        for i in range(self.n):
            src = x_refs[i] if self.mode == "gather" else x_refs[i].at[me]
            load = pltpu.make_async_copy(src, bufs[i], local_sems.at[2 * i])
            load.start()
            load.wait()
            pltpu.make_async_copy(bufs[i], o_refs[i].at[me], local_sems.at[2 * i + 1]).start()

    def wait(self, x_refs, o_refs, scratch):
        cps, me = self._remote(x_refs, o_refs, scratch[0], scratch[1])
        for cp in cps:
            cp.wait()
        if self.mode == "swap":
            return
        local_sems, bufs = scratch[2], scratch[3:]
        for i in range(self.n):
            pltpu.make_async_copy(bufs[i], o_refs[i].at[me], local_sems.at[2 * i + 1]).wait()


class _RiderGroup:
    def __init__(self, members):
        self.members = list(members)
        self.n = sum(m.n for m in self.members)
        self.arrs = [a for m in self.members for a in m.arrs]
        self.out_shape = [s for m in self.members for s in m.out_shape]
        self.specs = [s for m in self.members for s in m.specs]
        self.scratch = [s for m in self.members for s in m.scratch]

    def _each(self, x_refs, o_refs, scratch):
        i = j = 0
        for m in self.members:
            yield m, x_refs[i:i + m.n], o_refs[i:i + m.n], scratch[j:j + len(m.scratch)]
            i, j = i + m.n, j + len(m.scratch)

    def start(self, x_refs, o_refs, scratch):
        for m, xs, os, sc in self._each(x_refs, o_refs, scratch):
            m.start(xs, os, sc)

    def wait(self, x_refs, o_refs, scratch):
        for m, xs, os, sc in self._each(x_refs, o_refs, scratch):
            m.wait(xs, os, sc)


def _with_riders(compute, riders, n_in, n_out, n_scratch, last_step):
    if riders is None:
        return compute
    n = riders.n

    def body(*refs):
        ins, x_refs = refs[:n_in], refs[n_in:n_in + n]
        outs, o_refs = refs[n_in + n:n_in + n + n_out], refs[n_in + n + n_out:n_in + 2 * n + n_out]
        scratch = refs[n_in + 2 * n + n_out:]
        own, ride = scratch[:n_scratch], scratch[n_scratch:]
        ids = [pl.program_id(i) for i in range(len(last_step))]
        first = functools.reduce(jnp.logical_and, [i == 0 for i in ids])
        last = functools.reduce(jnp.logical_and, [i == l for i, l in zip(ids, last_step)])

        @pl.when(first)
        def _():
            riders.start(x_refs, o_refs, ride)

        compute(*ins, *outs, *own)

        @pl.when(last)
        def _():
            riders.wait(x_refs, o_refs, ride)

    return body


def _attention_fwd(proj3, seq_blocks, riders=None):
    B, S, _ = proj3.shape
    scale = HEAD_DIM ** -0.5
    nq = ATT_WIDTH // ATT_GW

    def col(k):
        return pl.BlockSpec((1, S, ATT_GW), lambda b, g, k=k: (b, 0, k * nq + g))

    o_spec = pl.BlockSpec((1, S, ATT_GW), lambda b, g: (b, 0, g))
    l_spec = pl.BlockSpec((1, 1, S, ATT_BLOCK), lambda b, g: (b, g, 0, 0))

    def compute(q_ref, k_ref, v_ref, o_ref, lse_ref, qf, kf, vf, os, ls, bias):
        g = pl.program_id(1)
        for t in range(ATT_PAIRS):
            ts = slice(t * ATT_BLOCK, (t + 1) * ATT_BLOCK)
            qf[t] = q_ref[0, :, ts].astype(F32) * scale
            kf[t] = k_ref[0, :, ts].astype(F32)
            vf[t] = v_ref[0, :, ts].astype(F32)
        lane = lax.broadcasted_iota(jnp.int32, (ATT_BLOCK, ATT_BLOCK), 1)
        low = lane < HEAD_DIM

        def block(p, d, r, n, has_prev):
            start = n * (ATT_BLOCK * d) + r
            rows = _att_rows(start, d)
            prows = _att_rows(start - ATT_BLOCK * d, d) if has_prev else None
            lse_t = jnp.zeros((ATT_BLOCK, ATT_BLOCK), F32)
            for t in range(ATT_PAIRS):
                q2 = _stack_heads(qf[t, rows, :], low)
                k2 = kf[t, rows, :].astype(BF16)
                v2 = vf[t, rows, :].astype(BF16)
                if has_prev:
                    k2 = jnp.concatenate([k2, kf[t, prows, :].astype(BF16)], axis=0)
                    v2 = jnp.concatenate([v2, vf[t, prows, :].astype(BF16)], axis=0)
                    b2 = bias[t]
                else:
                    b2 = bias[t, :, 0:ATT_BLOCK]
                s = lax.dot_general(q2, k2, NT_DIMS, preferred_element_type=F32) + b2
                m = jnp.max(s, axis=1, keepdims=True)
                pr = jnp.exp(s - m)
                den = jnp.sum(pr, axis=1, keepdims=True)
                o = jnp.dot(pr.astype(BF16), v2, preferred_element_type=F32) * (1.0 / den)
                os[p, t, rows, :] = _unstack_heads(o, low)
                lse2 = m + jnp.log(den)
                lse_t = jnp.where(lane == 2 * t, lse2[0:ATT_BLOCK], lse_t)
                lse_t = jnp.where(lane == 2 * t + 1, lse2[ATT_BLOCK:], lse_t)
            ls[p, rows, :] = lse_t

        for p in range(N_PATTERNS):
            d = 4 ** p
            _att_fill_bias(bias, g, d)
            _att_one_pattern(block, p, d, seq_blocks // d)

        def combine(i, carry):
            rows = pl.ds(pl.multiple_of(i * ATT_BLOCK, ATT_BLOCK), ATT_BLOCK)
            l0, l1, l2 = ls[0, rows, :], ls[1, rows, :], ls[2, rows, :]
            m = jnp.maximum(jnp.maximum(l0, l1), l2)
            lse = m + jnp.log(jnp.exp(l0 - m) + jnp.exp(l1 - m) + jnp.exp(l2 - m))
            lse_ref[0, 0, rows, :] = lse
            w = [jnp.exp(l0 - lse), jnp.exp(l1 - lse), jnp.exp(l2 - lse)]
            for t in range(ATT_PAIRS):
                acc = jnp.zeros((ATT_BLOCK, ATT_BLOCK), F32)
                for p in range(N_PATTERNS):
                    wt = jnp.where(low, w[p][:, 2 * t:2 * t + 1], w[p][:, 2 * t + 1:2 * t + 2])
                    acc = acc + wt * os[p, t, rows, :]
                o_ref[0, rows, t * ATT_BLOCK:(t + 1) * ATT_BLOCK] = acc.astype(BF16)
            return carry

        lax.fori_loop(0, S // ATT_BLOCK, combine, 0, unroll=4)

    scratch = ([pltpu.VMEM((ATT_PAIRS, S, ATT_BLOCK), F32)] * 3
               + [pltpu.VMEM((N_PATTERNS, ATT_PAIRS, S, ATT_BLOCK), F32), pltpu.VMEM((N_PATTERNS, S, ATT_BLOCK), F32),
                  pltpu.VMEM((ATT_PAIRS, 2 * ATT_BLOCK, 2 * ATT_BLOCK), F32)])
    rs = riders
    res = _pcall(_with_riders(compute, rs, 3, 2, len(scratch), (B - 1, ATT_GROUPS - 1)), name="attention_fwd",
                 out_shape=(_sds((B, S, ATT_WIDTH), BF16), _sds((B, ATT_GROUPS, S, ATT_BLOCK), F32))
                 + (tuple(rs.out_shape) if rs else ()),
                 grid=(B, ATT_GROUPS), in_specs=[col(0), col(1), col(2)] + (rs.specs if rs else []),
                 out_specs=(o_spec, l_spec) + (tuple(rs.specs) if rs else ()),
                 scratch_shapes=scratch + (rs.scratch if rs else []),
                 dims=("arbitrary", "arbitrary"))(proj3, proj3, proj3, *(rs.arrs if rs else []))
    return res[0], res[1], list(res[2:])


def _att_one_pattern(block, p, d, nb):
    def per_residue(r, carry):
        block(p, d, r, 0, False)
        if nb > 1:
            def per_block(n, c2):
                block(p, d, r, n, True)
                return c2
            lax.fori_loop(1, nb, per_block, 0, unroll=ATT_UNROLL if (nb - 1) % ATT_UNROLL == 0 else nb - 1)
        return carry

    if d == 1:
        per_residue(0, 0)
    else:
        lax.fori_loop(0, d, per_residue, 0, unroll=ATT_RESIDUE_UNROLL if nb == 1 else 1)


def _attention_bwd(proj3, do3, o3, lse4, seq_blocks, riders=None):
    B, S, _ = proj3.shape
    scale = HEAD_DIM ** -0.5
    nq = ATT_WIDTH // ATT_GW

    def col(k):
        return pl.BlockSpec((1, S, ATT_GW), lambda b, g, k=k: (b, 0, k * nq + g))

    o_spec = pl.BlockSpec((1, S, ATT_GW), lambda b, g: (b, 0, g))
    l_spec = pl.BlockSpec((1, 1, S, ATT_BLOCK), lambda b, g: (b, g, 0, 0))

    def compute(q_ref, k_ref, v_ref, do_ref, o_ref, lse_ref, dq_ref, dk_ref, dv_ref,
                qf, kf, vf, dof, dl, aq, ak, av, bias):
        g = pl.program_id(1)
        for t in range(ATT_PAIRS):
            ts = slice(t * ATT_BLOCK, (t + 1) * ATT_BLOCK)
            qf[t] = q_ref[0, :, ts].astype(F32) * scale
            kf[t] = k_ref[0, :, ts].astype(F32)
            vf[t] = v_ref[0, :, ts].astype(F32)
            dof[t] = do_ref[0, :, ts].astype(F32)
        aq[...] = jnp.zeros_like(aq)
        ak[...] = jnp.zeros_like(ak)
        av[...] = jnp.zeros_like(av)
        lane = lax.broadcasted_iota(jnp.int32, (ATT_BLOCK, ATT_BLOCK), 1)
        low = lane < HEAD_DIM

        def fill_delta(i, carry):
            rows = pl.ds(pl.multiple_of(i * ATT_BLOCK, ATT_BLOCK), ATT_BLOCK)
            acc = jnp.zeros((ATT_BLOCK, ATT_BLOCK), F32)
            for t in range(ATT_PAIRS):
                prod = dof[t, rows, :] * o_ref[0, rows, t * ATT_BLOCK:(t + 1) * ATT_BLOCK].astype(F32)
                lo = jnp.sum(jnp.where(low, prod, 0.0), axis=1, keepdims=True)
                hi = jnp.sum(prod, axis=1, keepdims=True) - lo
                acc = jnp.where(lane == 2 * t, lo, acc)
                acc = jnp.where(lane == 2 * t + 1, hi, acc)
            dl[rows, :] = acc
            return carry

        lax.fori_loop(0, S // ATT_BLOCK, fill_delta, 0, unroll=2)

        def block(p, d, r, n, has_prev):
            start = n * (ATT_BLOCK * d) + r
            rows = _att_rows(start, d)
            prows = _att_rows(start - ATT_BLOCK * d, d) if has_prev else None
            lse_t = lse_ref[0, 0, rows, :]
            dl_t = dl[rows, :]
            for t in range(ATT_PAIRS):
                q2 = _stack_heads(qf[t, rows, :], low)
                do2 = _stack_heads(dof[t, rows, :], low)
                k2 = kf[t, rows, :].astype(BF16)
                v2 = vf[t, rows, :].astype(BF16)
                if has_prev:
                    k2 = jnp.concatenate([k2, kf[t, prows, :].astype(BF16)], axis=0)
                    v2 = jnp.concatenate([v2, vf[t, prows, :].astype(BF16)], axis=0)
                    b2 = bias[t]
                else:
                    b2 = bias[t, :, 0:ATT_BLOCK]
                lse2 = jnp.concatenate([lse_t[:, 2 * t:2 * t + 1], lse_t[:, 2 * t + 1:2 * t + 2]], axis=0)
                dl2 = jnp.concatenate([dl_t[:, 2 * t:2 * t + 1], dl_t[:, 2 * t + 1:2 * t + 2]], axis=0)
                s = lax.dot_general(q2, k2, NT_DIMS, preferred_element_type=F32) + b2
                pr = jnp.exp(s - lse2)
                ds = (pr * (lax.dot_general(do2, v2, NT_DIMS, preferred_element_type=F32) - dl2)).astype(BF16)
                dq = _unstack_heads(jnp.dot(ds, k2, preferred_element_type=F32), low)
                dk = lax.dot_general(ds, q2, TN_DIMS, preferred_element_type=F32)
                dv = lax.dot_general(pr.astype(BF16), do2, TN_DIMS, preferred_element_type=F32)
                aq[t, rows, :] = aq[t, rows, :] + dq * scale
                ak[t, rows, :] = ak[t, rows, :] + dk[0:ATT_BLOCK]
                av[t, rows, :] = av[t, rows, :] + dv[0:ATT_BLOCK]
                if has_prev:
                    ak[t, prows, :] = ak[t, prows, :] + dk[ATT_BLOCK:]
                    av[t, prows, :] = av[t, prows, :] + dv[ATT_BLOCK:]

        for p in range(N_PATTERNS):
            d = 4 ** p
            _att_fill_bias(bias, g, d)
            _att_one_pattern(block, p, d, seq_blocks // d)

        for t in range(ATT_PAIRS):
            ts = slice(t * ATT_BLOCK, (t + 1) * ATT_BLOCK)
            dq_ref[0, :, ts] = aq[t].astype(BF16)
            dk_ref[0, :, ts] = ak[t].astype(BF16)
            dv_ref[0, :, ts] = av[t].astype(BF16)

    shp = _sds((B, S, ATT_WIDTH), BF16)
    pair_buf = pltpu.VMEM((ATT_PAIRS, S, ATT_BLOCK), F32)
    scratch = ([pair_buf] * 4 + [pltpu.VMEM((S, ATT_BLOCK), F32)] + [pair_buf] * 3
               + [pltpu.VMEM((ATT_PAIRS, 2 * ATT_BLOCK, 2 * ATT_BLOCK), F32)])
    rs = riders
    res = _pcall(_with_riders(compute, rs, 6, 3, len(scratch), (B - 1, ATT_GROUPS - 1)), name="attention_bwd",
                 out_shape=(shp, shp, shp) + (tuple(rs.out_shape) if rs else ()), grid=(B, ATT_GROUPS),
                 in_specs=[col(0), col(1), col(2), o_spec, o_spec, l_spec] + (rs.specs if rs else []),
                 out_specs=(o_spec, o_spec, o_spec) + (tuple(rs.specs) if rs else ()),
                 scratch_shapes=scratch + (rs.scratch if rs else []),
                 dims=("arbitrary", "arbitrary"))(proj3, proj3, proj3, do3, o3, lse4, *(rs.arrs if rs else []))
    return res[0], res[1], res[2], list(res[3:])


def _expand_groups(m):
    rows = SSM_WIDTH
    t = jnp.concatenate([m] * SSM_GROUPS, axis=0)
    r = lax.broadcasted_iota(jnp.int32, (rows, SSM_LANES), 0)
    l = lax.broadcasted_iota(jnp.int32, (rows, SSM_LANES), 1)
    keep = lax.shift_right_logical(r, 4) == lax.shift_right_logical(l, 6)
    return jnp.where(keep, t, 0.0)


def _collapse_groups(m):
    rows = SSM_WIDTH
    r = lax.broadcasted_iota(jnp.int32, (rows, SSM_LANES), 0)
    l = lax.broadcasted_iota(jnp.int32, (rows, SSM_LANES), 1)
    keep = lax.shift_right_logical(r, 4) == lax.shift_right_logical(l, 6)
    t = jnp.where(keep, m, 0.0)
    acc = t[0:SSM_GROUP_CH]
    for g in range(1, SSM_GROUPS):
        acc = acc + t[g * SSM_GROUP_CH:(g + 1) * SSM_GROUP_CH]
    return acc


def _zoh(lr, li, ldt):
    dt = jnp.exp(ldt)
    mag = jnp.exp(lr * dt)
    ang = li * dt
    cs, sn = jnp.cos(ang), jnp.sin(ang)
    ab_re, ab_im = mag * cs, mag * sn
    nr, ni = ab_re - 1.0, ab_im
    den = lr * lr + li * li
    n_re = nr * lr + ni * li
    n_im = ni * lr - nr * li
    return dict(dt=dt, mag=mag, cs=cs, sn=sn, ab_re=ab_re, ab_im=ab_im, nr=nr, ni=ni, den=den, n_re=n_re, n_im=n_im,
                f_re=n_re / den, f_im=n_im / den)


def _ssm_params(lr, li, ldt, br, bi, cr, ci):
    def body(lr_ref, li_ref, ldt_ref, br_ref, bi_ref, cr_ref, ci_ref, ab_ref, w_ref, c_ref):
        z = _zoh(lr_ref[...], li_ref[...], ldt_ref[...])
        ab_ref[0:1, :] = z["ab_re"]
        ab_ref[1:2, :] = z["ab_im"]
        br, bi = br_ref[...], bi_ref[...]
        w_ref[:, 0:SSM_LANES] = _expand_groups(z["f_re"] * br - z["f_im"] * bi).astype(BF16)
        w_ref[:, SSM_LANES:] = _expand_groups(z["f_re"] * bi + z["f_im"] * br).astype(BF16)
        c_ref[:, 0:SSM_LANES] = _expand_groups(cr_ref[...]).astype(BF16)
        c_ref[:, SSM_LANES:] = _expand_groups(-ci_ref[...]).astype(BF16)

    return _pcall(body, name="ssm_params",
                  out_shape=(_sds((2, SSM_LANES), F32), _sds((SSM_WIDTH, 2 * SSM_LANES), BF16),
                             _sds((SSM_WIDTH, 2 * SSM_LANES), BF16)))(lr, li, ldt, br, bi, cr, ci)


def _ssm_params_bwd(lr, li, ldt, br, bi, dab, dw, dc):
    def body(lr_ref, li_ref, ldt_ref, br_ref, bi_ref, dab_ref, dw_ref, dc_ref,
             dlr_ref, dli_ref, dldt_ref, dbr_ref, dbi_ref, dcr_ref, dci_ref):
        lr, li = lr_ref[...], li_ref[...]
        z = _zoh(lr, li, ldt_ref[...])
        br, bi = br_ref[...], bi_ref[...]
        dbb_re = _collapse_groups(dw_ref[:, 0:SSM_LANES])
        dbb_im = _collapse_groups(dw_ref[:, SSM_LANES:])
        dcr_ref[...] = _collapse_groups(dc_ref[:, 0:SSM_LANES])
        dci_ref[...] = -_collapse_groups(dc_ref[:, SSM_LANES:])
        f_re, f_im = z["f_re"], z["f_im"]
        dbr_ref[...] = f_re * dbb_re + f_im * dbb_im
        dbi_ref[...] = f_re * dbb_im - f_im * dbb_re
        df_re = jnp.sum(dbb_re * br + dbb_im * bi, axis=0, keepdims=True)
        df_im = jnp.sum(dbb_im * br - dbb_re * bi, axis=0, keepdims=True)
        den = z["den"]
        dn_re, dn_im = df_re / den, df_im / den
        dden = -(df_re * z["n_re"] + df_im * z["n_im"]) / (den * den)
        dnr = dn_re * lr - dn_im * li
        dni = dn_re * li + dn_im * lr
        dlr = dn_re * z["nr"] + dn_im * z["ni"] + 2.0 * dden * lr
        dli = dn_re * z["ni"] - dn_im * z["nr"] + 2.0 * dden * li
        dab_re = dab_ref[0:1, :] + dnr
        dab_im = dab_ref[1:2, :] + dni
        mag, cs, sn, dt = z["mag"], z["cs"], z["sn"], z["dt"]
        dmag = dab_re * cs + dab_im * sn
        dang = mag * (dab_im * cs - dab_re * sn)
        dlr_ref[...] = dlr + dmag * mag * dt
        dli_ref[...] = dli + dang * dt
        ddt = dmag * mag * lr + dang * li
        per_lane = jnp.broadcast_to(ddt * dt, (8, SSM_LANES))
        lane = lax.broadcasted_iota(jnp.int32, (SSM_LANES, 128), 0)
        col = lax.broadcasted_iota(jnp.int32, (SSM_LANES, 128), 1)
        ind = jnp.where(lax.shift_right_logical(lane, 6) == col, 1.0, 0.0)
        dldt_ref[...] = jnp.dot(per_lane, ind, preferred_element_type=F32, precision=lax.Precision.HIGHEST)[0:1]

    vec = _sds((1, SSM_LANES), F32)
    mat = _sds((SSM_GROUP_CH, SSM_LANES), F32)
    return _pcall(body, name="ssm_params_bwd", out_shape=(vec, vec, _sds((1, 128), F32), mat, mat, mat, mat))(
        lr, li, ldt, br, bi, dab, dw, dc)


SCAN_CHUNK = 1024


def _scan_consts(ar, ai, k_ref, reverse):
    row = lax.broadcasted_iota(jnp.int32, (8, SSM_LANES), 0)
    pw = [(ar, ai)]
    for _ in range(7):
        pr, pi = pw[-1]
        pw.append((pr * ar - pi * ai, pr * ai + pi * ar))
    for n, k in enumerate((1, 2, 4)):
        keep = (row < 8 - k) if reverse else (row >= k)
        k_ref[2 * n] = jnp.where(keep, jnp.broadcast_to(pw[k - 1][0], (8, SSM_LANES)), 0.0)
        k_ref[2 * n + 1] = jnp.where(keep, jnp.broadcast_to(pw[k - 1][1], (8, SSM_LANES)), 0.0)
    cr = jnp.zeros((8, SSM_LANES), F32)
    ci = jnp.zeros((8, SSM_LANES), F32)
    for r in range(8):
        e = (8 - r) if reverse else (r + 1)
        cr = jnp.where(row == r, jnp.broadcast_to(pw[e - 1][0], (8, SSM_LANES)), cr)
        ci = jnp.where(row == r, jnp.broadcast_to(pw[e - 1][1], (8, SSM_LANES)), ci)
    k_ref[6] = cr
    k_ref[7] = ci


def _scan_tile(xr, xi, k_ref, car, cai, reverse):
    for n, k in enumerate((1, 2, 4)):
        sh = (8 - k) if reverse else k
        sr = pltpu.roll(xr, sh, 0)
        si = pltpu.roll(xi, sh, 0)
        mr, mi = k_ref[2 * n], k_ref[2 * n + 1]
        xr, xi = xr + mr * sr - mi * si, xi + mr * si + mi * sr
    pr, pi = k_ref[6], k_ref[7]
    xr, xi = xr + pr * car - pi * cai, xi + pr * cai + pi * car
    return xr, xi


US_BLOCK = (3 * ATT_WIDTH) // SSM_WIDTH


def _ssm_scan_fwd(proj3, abar, w_bu, w_c):
    B, S, _ = proj3.shape
    ch = min(S, SCAN_CHUNK)
    u_spec = pl.BlockSpec((1, ch, SSM_WIDTH), lambda b, c: (b, c, US_BLOCK))
    x_spec = pl.BlockSpec((1, ch, 2 * SSM_LANES), lambda b, c: (b, c, 0))
    y_spec = pl.BlockSpec((1, ch, SSM_WIDTH), lambda b, c: (b, c, 0))
    w_spec = pl.BlockSpec((SSM_WIDTH, 2 * SSM_LANES), lambda b, c: (0, 0))

    def body(ab_ref, u_ref, wb_ref, wc_ref, x_ref, y_ref, k_ref, carry_ref):
        _scan_consts(ab_ref[0:1, :], ab_ref[1:2, :], k_ref, False)

        @pl.when(pl.program_id(1) == 0)
        def _():
            carry_ref[...] = jnp.zeros_like(carry_ref)

        x_ref[0] = jnp.dot(u_ref[0], wb_ref[...], preferred_element_type=F32)

        def step(i, carry):
            base = pl.multiple_of(i * 8, 8)
            xr = x_ref[0, pl.ds(base, 8), 0:SSM_LANES]
            xi = x_ref[0, pl.ds(base, 8), SSM_LANES:]
            xr, xi = _scan_tile(xr, xi, k_ref, carry[0], carry[1], False)
            x_ref[0, pl.ds(base, 8), 0:SSM_LANES] = xr
            x_ref[0, pl.ds(base, 8), SSM_LANES:] = xi
            return (jnp.broadcast_to(xr[7:8], (8, SSM_LANES)), jnp.broadcast_to(xi[7:8], (8, SSM_LANES)))

        cr, ci = lax.fori_loop(0, ch // 8, step, (carry_ref[0], carry_ref[1]))
        carry_ref[0] = cr
        carry_ref[1] = ci
        y_ref[0] = lax.dot_general(x_ref[0].astype(BF16), wc_ref[...], NT_DIMS, preferred_element_type=F32)

    return _pcall(body, name="ssm_scan_fwd",
                  out_shape=(_sds((B, S, 2 * SSM_LANES), F32), _sds((B, S, SSM_WIDTH), F32)), grid=(B, S // ch),
                  in_specs=[pl.BlockSpec((2, SSM_LANES), lambda b, c: (0, 0)), u_spec, w_spec, w_spec],
                  out_specs=(x_spec, y_spec),
                  scratch_shapes=[pltpu.VMEM((8, 8, SSM_LANES), F32), pltpu.VMEM((2, 8, SSM_LANES), F32)],
                  dims=("arbitrary", "arbitrary"))(abar, proj3, w_bu, w_c)


def _ssm_scan_bwd(proj3, dy3, xs3, abar, w_bu, w_c, dsk):
    B, S, _ = proj3.shape
    ch = min(S, SCAN_CHUNK)
    nc = S // ch
    u_spec = pl.BlockSpec((1, ch, SSM_WIDTH), lambda b, c: (b, nc - 1 - c, US_BLOCK))
    x_spec = pl.BlockSpec((1, ch, 2 * SSM_LANES), lambda b, c: (b, nc - 1 - c, 0))
    y_spec = pl.BlockSpec((1, ch, SSM_WIDTH), lambda b, c: (b, nc - 1 - c, 0))
    w_spec = pl.BlockSpec((SSM_WIDTH, 2 * SSM_LANES), lambda b, c: (0, 0))
    ab_spec = pl.BlockSpec((2, SSM_LANES), lambda b, c: (0, 0))
    d_spec = pl.BlockSpec((1, SSM_WIDTH), lambda b, c: (0, 0))

    def body(ab_ref, u_ref, dy_ref, xs_ref, wb_ref, wc_ref, d_ref, du_ref, da_ref, dwb_ref, dwc_ref,
             g_ref, k_ref, carry_ref, acc_ref):
        b, c = pl.program_id(0), pl.program_id(1)
        _scan_consts(ab_ref[0:1, :], -ab_ref[1:2, :], k_ref, True)
        row = lax.broadcasted_iota(jnp.int32, (8, SSM_LANES), 0)

        @pl.when(c == 0)
        def _():
            carry_ref[...] = jnp.zeros_like(carry_ref)

        @pl.when((c == 0) & (b == 0))
        def _():
            acc_ref[...] = jnp.zeros_like(acc_ref)
            dwb_ref[...] = jnp.zeros_like(dwb_ref)
            dwc_ref[...] = jnp.zeros_like(dwc_ref)

        dy = dy_ref[0]
        dyb = dy.astype(BF16)
        g_ref[...] = jnp.dot(dyb, wc_ref[...], preferred_element_type=F32)

        def step(i, carry):
            car, cai, ar_acc, ai_acc = carry
            base = pl.multiple_of((ch // 8 - 1 - i) * 8, 8)
            gr = g_ref[pl.ds(base, 8), 0:SSM_LANES]
            gi = g_ref[pl.ds(base, 8), SSM_LANES:]
            gr, gi = _scan_tile(gr, gi, k_ref, car, cai, True)
            g_ref[pl.ds(base, 8), 0:SSM_LANES] = gr
            g_ref[pl.ds(base, 8), SSM_LANES:] = gi
            nr = jnp.where(row == 7, car, pltpu.roll(gr, 7, 0))
            ni = jnp.where(row == 7, cai, pltpu.roll(gi, 7, 0))
            xr = xs_ref[0, pl.ds(base, 8), 0:SSM_LANES]
            xi = xs_ref[0, pl.ds(base, 8), SSM_LANES:]
            ar_acc = ar_acc + nr * xr + ni * xi
            ai_acc = ai_acc + ni * xr - nr * xi
            return (jnp.broadcast_to(gr[0:1], (8, SSM_LANES)), jnp.broadcast_to(gi[0:1], (8, SSM_LANES)), ar_acc, ai_acc)

        cr, ci, ar_acc, ai_acc = lax.fori_loop(0, ch // 8, step, (carry_ref[0], carry_ref[1], acc_ref[0], acc_ref[1]))
        carry_ref[0] = cr
        carry_ref[1] = ci
        acc_ref[0] = ar_acc
        acc_ref[1] = ai_acc
        da_ref[0:1, :] = jnp.sum(ar_acc, axis=0, keepdims=True)
        da_ref[1:2, :] = jnp.sum(ai_acc, axis=0, keepdims=True)

        gb = g_ref[...].astype(BF16)
        du = lax.dot_general(gb, wb_ref[...], NT_DIMS, preferred_element_type=F32) + d_ref[...] * dy
        du_ref[0] = du.astype(BF16)
        xb = xs_ref[0].astype(BF16)
        u = u_ref[0]
        for j in range(2 * SSM_LANES // SSM_WIDTH):
            rows = slice((j % (SSM_LANES // SSM_WIDTH)) * 64, (j % (SSM_LANES // SSM_WIDTH)) * 64 + 64)
            cols = slice(j * SSM_WIDTH, (j + 1) * SSM_WIDTH)
            dwb_ref[rows, cols] += lax.dot_general(u[:, rows], gb[:, cols], TN_DIMS, preferred_element_type=F32)
            dwc_ref[rows, cols] += lax.dot_general(dyb[:, rows], xb[:, cols], TN_DIMS, preferred_element_type=F32)

    mat = _sds((SSM_WIDTH, 2 * SSM_LANES), F32)
    return _pcall(body, name="ssm_scan_bwd",
                  out_shape=(_sds((B, S, SSM_WIDTH), BF16), _sds((2, SSM_LANES), F32), mat, mat), grid=(B, nc),
                  in_specs=[ab_spec, u_spec, y_spec, x_spec, w_spec, w_spec, d_spec],
                  out_specs=(y_spec, ab_spec, w_spec, w_spec),
                  scratch_shapes=[pltpu.VMEM((ch, 2 * SSM_LANES), F32), pltpu.VMEM((8, 8, SSM_LANES), F32),
                                  pltpu.VMEM((2, 8, SSM_LANES), F32), pltpu.VMEM((2, 8, SSM_LANES), F32)],
                  dims=("arbitrary", "arbitrary"))(abar, proj3, dy3, xs3, w_bu, w_c, dsk)


GELU_K = math.sqrt(2.0 / math.pi)
GELU_C = 0.044715


def _gelu_parts(y):
    t = jnp.tanh(GELU_K * (y + GELU_C * y * y * y))
    return 0.5 * y * (1.0 + t), t


def _ssm_post(yc, us, dsk, wglu, bglu):
    T, N = yc.shape
    tm = min(T, 1024)
    row = pl.BlockSpec((tm, N), lambda i: (i, 0))
    vec = pl.BlockSpec((1, N), lambda i: (0, 0))
    mat = pl.BlockSpec((N, N), lambda i: (0, 0))

    def body(yc_ref, us_ref, d_ref, w_ref, b_ref, y_ref, s_ref):
        y = yc_ref[...] + d_ref[...] * us_ref[...]
        y_ref[...] = y
        z, _ = _gelu_parts(y)
        gl = jnp.dot(z.astype(BF16), w_ref[...], preferred_element_type=F32) + b_ref[...]
        s_ref[...] = (z * _sig(gl)).astype(BF16)

    return _pcall(body, name="ssm_post", out_shape=(_sds((T, N), F32), _sds((T, N), BF16)), grid=(T // tm,),
                  in_specs=[row, row, vec, mat, vec], out_specs=(row, row), dims=("parallel",))(yc, us, dsk, wglu, bglu)


def _ssm_post_bwd(y5, us, ds, dsk, wglu, bglu):
    T, N = y5.shape
    tm = min(T, 1024)
    row = pl.BlockSpec((tm, N), lambda i: (i, 0))
    vec = pl.BlockSpec((1, N), lambda i: (0, 0))
    mat = pl.BlockSpec((N, N), lambda i: (0, 0))

    def body(y_ref, us_ref, ds_ref, d_ref, w_ref, b_ref, dy_ref, dd_ref, db_ref, dw_ref):
        @pl.when(pl.program_id(0) == 0)
        def _():
            dd_ref[...] = jnp.zeros_like(dd_ref)
            db_ref[...] = jnp.zeros_like(db_ref)
            dw_ref[...] = jnp.zeros_like(dw_ref)

        y = y_ref[...]
        z, t = _gelu_parts(y)
        zb = z.astype(BF16)
        gl = jnp.dot(zb, w_ref[...], preferred_element_type=F32) + b_ref[...]
        sg = _sig(gl)
        ds = ds_ref[...]
        dgl = ds * z * sg * (1.0 - sg)
        dglb = dgl.astype(BF16)
        dz = ds * sg + lax.dot_general(dglb, w_ref[...], (((1,), (1,)), ((), ())), preferred_element_type=F32)
        dgelu = 0.5 * (1.0 + t) + 0.5 * y * (1.0 - t * t) * GELU_K * (1.0 + 3.0 * GELU_C * y * y)
        dy = dz * dgelu
        dy_ref[...] = dy
        dd_ref[...] += jnp.sum(dy * us_ref[...], axis=0, keepdims=True)
        db_ref[...] += jnp.sum(dgl, axis=0, keepdims=True)
        dw_ref[...] += lax.dot_general(zb, dglb, (((0,), (0,)), ((), ())), preferred_element_type=F32)

    return _pcall(body, name="ssm_post_bwd",
                  out_shape=(_sds((T, N), F32), _sds((1, N), F32), _sds((1, N), F32), _sds((N, N), F32)),
                  grid=(T // tm,), in_specs=[row, row, row, vec, mat, vec], out_specs=(row, vec, vec, mat),
                  dims=("arbitrary",))(y5, us, ds, dsk, wglu, bglu)


GATE_TILE = 256
GATE_ATT_BLOCK0 = (3 * ATT_WIDTH + SSM_WIDTH) // GATE_TILE
GATE_SSM_BLOCK0 = (3 * ATT_WIDTH + SSM_WIDTH + D_MODEL) // GATE_TILE


def _merge(proj, y_att, y_ssm, b_gate):
    T = proj.shape[0]
    tm = min(T, 4096)
    nj = D_MODEL // GATE_TILE
    ga = pl.BlockSpec((tm, GATE_TILE), lambda i, j: (i, GATE_ATT_BLOCK0 + j))
    gs = pl.BlockSpec((tm, GATE_TILE), lambda i, j: (i, GATE_SSM_BLOCK0 + j))
    yy = pl.BlockSpec((tm, GATE_TILE), lambda i, j: (i, j))
    ba = pl.BlockSpec((1, GATE_TILE), lambda i, j: (0, j))
    bs = pl.BlockSpec((1, GATE_TILE), lambda i, j: (0, nj + j))

    def body(ga_ref, gs_ref, ya_ref, ys_ref, ba_ref, bs_ref, o_ref):
        o_ref[...] = (_sig(ga_ref[...] + ba_ref[...]) * ya_ref[...]
                      + _sig(gs_ref[...] + bs_ref[...]) * ys_ref[...]).astype(BF16)

    return _pcall(body, name="merge", out_shape=_sds((T, D_MODEL), BF16), grid=(T // tm, nj),
                  in_specs=[ga, gs, yy, yy, ba, bs], out_specs=yy, dims=("parallel", "parallel"))(
        proj, proj, y_att, y_ssm, b_gate, b_gate)


def _merge_bwd(proj, y_att, y_ssm, b_gate, dmerged):
    T = proj.shape[0]
    tm = min(T, 2048)
    nj = D_MODEL // GATE_TILE
    ga = pl.BlockSpec((tm, GATE_TILE), lambda j, i: (i, GATE_ATT_BLOCK0 + j))
    gs = pl.BlockSpec((tm, GATE_TILE), lambda j, i: (i, GATE_SSM_BLOCK0 + j))
    yy = pl.BlockSpec((tm, GATE_TILE), lambda j, i: (i, j))
    ba = pl.BlockSpec((1, GATE_TILE), lambda j, i: (0, j))
    bs = pl.BlockSpec((1, GATE_TILE), lambda j, i: (0, nj + j))

    def body(ga_ref, gs_ref, ya_ref, ys_ref, ba_ref, bs_ref, dm_ref, dya_ref, dys_ref, dga_ref, dgs_ref, dba_ref, dbs_ref):
        @pl.when(pl.program_id(1) == 0)
        def _():
            dba_ref[...] = jnp.zeros_like(dba_ref)
            dbs_ref[...] = jnp.zeros_like(dbs_ref)

        dm = dm_ref[...].astype(F32)
        sa = _sig(ga_ref[...] + ba_ref[...])
        ss = _sig(gs_ref[...] + bs_ref[...])
        dya_ref[...] = (dm * sa).astype(BF16)
        dys_ref[...] = (dm * ss).astype(BF16)
        dga = dm * ya_ref[...] * sa * (1.0 - sa)
        dgs = dm * ys_ref[...] * ss * (1.0 - ss)
        dga_ref[...] = dga.astype(BF16)
        dgs_ref[...] = dgs.astype(BF16)
        dba_ref[...] += jnp.sum(dga, axis=0, keepdims=True)
        dbs_ref[...] += jnp.sum(dgs, axis=0, keepdims=True)

    big = _sds((T, D_MODEL), BF16)
    vec = _sds((1, D_MODEL), F32)
    return _pcall(body, name="merge_bwd", out_shape=(big, big, big, big, vec, vec), grid=(nj, T // tm),
                  in_specs=[ga, gs, yy, yy, ba, bs, yy], out_specs=(yy, yy, yy, yy, ba, ba),
                  dims=("arbitrary", "arbitrary"))(proj, proj, y_att, y_ssm, b_gate, b_gate, dmerged)


CONV_TILE = 256


def _shift_rows(a, j, up=False):
    n = a.shape[0]
    r = pltpu.roll(a, n - j if up else j, 0)
    row = lax.broadcasted_iota(jnp.int32, (8, a.shape[1]), 0)
    if up:
        return jnp.concatenate([r[:n - 8], jnp.where(row < 8 - j, r[n - 8:], 0.0)], axis=0)
    return jnp.concatenate([jnp.where(row >= j, r[:8], 0.0), r[8:]], axis=0)


def _conv_pre(a, w_ref, b_ref):
    conv = b_ref[...] + w_ref[0:1, :] * a
    shifted = []
    for j in (1, 2):
        sh = _shift_rows(a, j)
        shifted.append(sh)
        conv = conv + w_ref[j:j + 1, :] * sh
    return conv, shifted


def _conv_act(up3, w_conv, b_conv):
    B, S, _ = up3.shape
    nj = D_FF // CONV_TILE
    a_spec = pl.BlockSpec((1, S, CONV_TILE), lambda b, j: (b, 0, j))
    v_spec = pl.BlockSpec((1, S, CONV_TILE), lambda b, j: (b, 0, nj + j))
    w_spec = pl.BlockSpec((3, CONV_TILE), lambda b, j: (0, j))
    b_spec = pl.BlockSpec((1, CONV_TILE), lambda b, j: (0, j))

    def body(a_ref, v_ref, w_ref, b_ref, o_ref):
        a = a_ref[0].astype(F32)
        conv, _ = _conv_pre(a, w_ref, b_ref)
        o_ref[0] = (conv * _sig(conv) * v_ref[0]).astype(BF16)

    return _pcall(body, name="conv_act", out_shape=_sds((B, S, D_FF), BF16), grid=(B, nj),
                  in_specs=[a_spec, v_spec, w_spec, b_spec], out_specs=a_spec, dims=("parallel", "parallel"))(
        up3, up3, w_conv, b_conv)


def _conv_bwd(up3, dact3, w_conv, b_conv):
    B, S, _ = up3.shape
    nj = D_FF // CONV_TILE
    a_spec = pl.BlockSpec((1, S, CONV_TILE), lambda j, b: (b, 0, j))
    v_spec = pl.BlockSpec((1, S, CONV_TILE), lambda j, b: (b, 0, nj + j))
    o_spec = pl.BlockSpec((2, 1, S, CONV_TILE), lambda j, b: (0, b, 0, j))
    w_spec = pl.BlockSpec((3, CONV_TILE), lambda j, b: (0, j))
    b_spec = pl.BlockSpec((1, CONV_TILE), lambda j, b: (0, j))

    def body(a_ref, v_ref, d_ref, w_ref, b_ref, dup_ref, dw_ref, db_ref):
        @pl.when(pl.program_id(1) == 0)
        def _():
            dw_ref[...] = jnp.zeros_like(dw_ref)
            db_ref[...] = jnp.zeros_like(db_ref)

        a = a_ref[0].astype(F32)
        d = d_ref[0].astype(F32)
        conv, shifted = _conv_pre(a, w_ref, b_ref)
        sg = _sig(conv)
        dup_ref[1, 0] = (d * conv * sg).astype(BF16)
        dconv = d * v_ref[0] * (sg * (1.0 + conv * (1.0 - sg)))
        da = w_ref[0:1, :] * dconv
        for j in (1, 2):
            da = da + w_ref[j:j + 1, :] * _shift_rows(dconv, j, up=True)
        dup_ref[0, 0] = da.astype(BF16)
        db_ref[...] += jnp.sum(dconv, axis=0, keepdims=True)
        dw_ref[0:1, :] += jnp.sum(dconv * a, axis=0, keepdims=True)
        dw_ref[1:2, :] += jnp.sum(dconv * shifted[0], axis=0, keepdims=True)
        dw_ref[2:3, :] += jnp.sum(dconv * shifted[1], axis=0, keepdims=True)

    return _pcall(body, name="conv_bwd",
                  out_shape=(_sds((2, B, S, D_FF), BF16), _sds((3, D_FF), F32), _sds((1, D_FF), F32)),
                  grid=(nj, B), in_specs=[a_spec, v_spec, a_spec, w_spec, b_spec],
                  out_specs=(o_spec, w_spec, b_spec), dims=("arbitrary", "arbitrary"))(up3, up3, dact3, w_conv, b_conv)


def _rows_tile(r, cap=640):
    for t in range(min(r, cap) - min(r, cap) % 8, 7, -8):
        if r % t == 0:
            return t
    return r


def _add2(a, b, out_dtype, name):
    R, N = a.shape
    tr = _rows_tile(R)
    spec = pl.BlockSpec((tr, N), lambda i: (i, 0))

    def body(a_ref, b_ref, o_ref):
        o_ref[...] = (a_ref[...] + b_ref[...]).astype(out_dtype)

    return _pcall(body, name=name, out_shape=_sds((R, N), out_dtype), grid=(R // tr,), in_specs=[spec, spec],
                  out_specs=spec, dims=("parallel",))(a, b)


def _sum_slots(q, name):
    n, R, N = q.shape
    tr = _rows_tile(R)

    def body(q_ref, o_ref):
        acc = q_ref[0].astype(F32)
        for s in range(1, n):
            acc = acc + q_ref[s].astype(F32)
        o_ref[...] = acc

    return _pcall(body, name=name, out_shape=_sds((R, N), F32), grid=(R // tr,),
                  in_specs=[pl.BlockSpec((n, tr, N), lambda i: (0, i, 0))], out_specs=pl.BlockSpec((tr, N), lambda i: (i, 0)),
                  dims=("parallel",))(q)


NATIVE = (("b_re", 16, 1024), ("b_im", 16, 1024), ("c_re", 16, 1024), ("c_im", 16, 1024), ("g_mix", 1, 1024),
          ("b_att", 1, 1024), ("b_ssm", 1, 1024), ("a_re", 1, 1024), ("a_im", 1, 1024), ("log_dt", 1, 128),
          ("d_skip", 1, 256), ("b_glu", 1, 256), ("g_ffn", 1, 1024), ("g_final", 1, 1024), ("b_conv", 1, 2048),
          ("w_conv", 3, 2048), ("loss", 1, 1))
N_MOD = 6
NATIVE_LATE = ("g_mix",)
MODS_LATE = (0, 1)


def _small_plan(late):
    pieces = [p for p in NATIVE if (p[0] in NATIVE_LATE) == late]
    mods = [k for k in range(N_MOD) if (k in MODS_LATE) == late]
    starts, r = {}, 0
    for name, rows, cols in pieces:
        starts[name] = r
        r += rows * (-(-cols // LANES))
    return pieces, mods, starts, -(-r // 8) * 8


def _pack_small(native, dmods, late):
    pieces, mods, starts, n_sum = _small_plan(late)
    B = dmods[mods[0]].shape[0]
    total = n_sum + 8 * len(mods)

    def body(*refs):
        xs, ms, o_ref = refs[:len(pieces)], refs[len(pieces):-1], refs[-1]
        o_ref[...] = jnp.zeros_like(o_ref)
        for (name, rows, cols), x_ref in zip(pieces, xs):
            chunks = -(-cols // LANES)
            if chunks == 1 and rows % 8 == 0:
                o_ref[starts[name]:starts[name] + rows, 0:cols] = x_ref[...]
                continue
            for i in range(rows):
                for q in range(chunks):
                    wd = min(LANES, cols - q * LANES)
                    r = starts[name] + i * chunks + q
                    o_ref[r:r + 1, 0:wd] = x_ref[i:i + 1, q * LANES:q * LANES + wd]
        for k, m_ref in enumerate(ms):
            for b in range(B):
                o_ref[n_sum + 8 * k + b:n_sum + 8 * k + b + 1, :] = m_ref[b]

    return _pcall(body, name="pack_small_late" if late else "pack_small_early", out_shape=_sds((total, LANES), F32))(
        *[native[n] for n, _, _ in pieces], *[dmods[k] for k in mods])


def _sum_unpack_small(gathered_early, gathered_late, B):
    plans = [_small_plan(False), _small_plan(True)]
    nd = gathered_early.shape[0]
    n_out = len(NATIVE)

    def body(*refs):
        g_refs, outs, dm_ref, accs = refs[0:2], refs[2:2 + n_out], refs[2 + n_out], refs[3 + n_out:]
        o = 0
        for g_ref, acc, (pieces, mods, starts, n_sum) in zip(g_refs, accs, plans):
            s = g_ref[0, 0:n_sum, :]
            for d in range(1, nd):
                s = s + g_ref[d, 0:n_sum, :]
            acc[...] = s
            for name, rows, cols in pieces:
                o_ref = outs[o]
                o += 1
                chunks = -(-cols // LANES)
                if chunks == 1 and rows % 8 == 0:
                    o_ref[...] = acc[starts[name]:starts[name] + rows, 0:cols]
                    continue
                for i in range(rows):
                    for q in range(chunks):
                        wd = min(LANES, cols - q * LANES)
                        r = starts[name] + i * chunks + q
                        o_ref[i:i + 1, q * LANES:q * LANES + wd] = acc[r:r + 1, 0:wd]
            for d in range(nd):
                for j, k in enumerate(mods):
                    dm_ref[d, :, k * D_MODEL:(k + 1) * D_MODEL] = g_ref[d, n_sum + 8 * j:n_sum + 8 * j + B, :]

    ordered = [p for pieces, _, _, _ in plans for p in pieces]
    out_shape = tuple(_sds((rows, cols), F32) for _, rows, cols in ordered) + (_sds((nd, B, N_MOD * D_MODEL), F32),)
    res = _pcall(body, name="sum_unpack_small", out_shape=out_shape,
                 scratch_shapes=[pltpu.VMEM((n_sum, LANES), F32) for _, _, _, n_sum in plans])(gathered_early, gathered_late)
    return {n: r for (n, _, _), r in zip(ordered, res[:-1])}, res[-1]


def _small_from_native(nat):
    lanes3 = lambda a: a.reshape(SSM_GROUP_CH, SSM_GROUPS, SSM_STATE)
    return dict(
        g_mix=nat["g_mix"].reshape(D_MODEL), b_gate=jnp.concatenate([nat["b_att"], nat["b_ssm"]], axis=1).reshape(2 * D_MODEL),
        a_re=nat["a_re"].reshape(SSM_GROUPS, SSM_STATE), a_im=nat["a_im"].reshape(SSM_GROUPS, SSM_STATE),
        log_dt=nat["log_dt"][0, :SSM_GROUPS], b_re=_groups_from_lanes(nat["b_re"]), b_im=_groups_from_lanes(nat["b_im"]),
        c_re=lanes3(nat["c_re"]).transpose(1, 0, 2), c_im=lanes3(nat["c_im"]).transpose(1, 0, 2),
        d_skip=nat["d_skip"].reshape(SSM_WIDTH), b_glu=nat["b_glu"].reshape(SSM_WIDTH), g_ffn=nat["g_ffn"].reshape(D_MODEL),
        w_conv=nat["w_conv"], b_conv=nat["b_conv"].reshape(D_FF), g_final=nat["g_final"].reshape(D_MODEL))


def _adamw_multi(params):
    n = len(params)
    bc1 = 1.0 - ADAM_B1 ** ADAM_STEP
    bc2 = 1.0 - ADAM_B2 ** ADAM_STEP

    def body(*refs):
        ins, outs = refs[:4 * n], refs[4 * n:]
        for i in range(n):
            w_ref, g_ref, m_ref, v_ref = ins[4 * i:4 * i + 4]
            d_ref, nm_ref, nv_ref = outs[3 * i:3 * i + 3]
            g = g_ref[...]
            m = ADAM_B1 * m_ref[...] + (1.0 - ADAM_B1) * g
            v = ADAM_B2 * v_ref[...] + (1.0 - ADAM_B2) * (g * g)
            nm_ref[...] = m
            nv_ref[...] = v
            d_ref[...] = -ADAM_LR * ((m / bc1) / (jnp.sqrt(v / bc2) + ADAM_EPS) + ADAM_WD * w_ref[...])

    flat = [a for p in params for a in p]
    out_shape = tuple(_sds(p[0].shape, F32) for p in params for _ in range(3))
    res = _pcall(body, name="adamw_small", out_shape=out_shape)(*flat)
    return [tuple(res[3 * i:3 * i + 3]) for i in range(n)]


def _adamw(w, g, m, v, name, g_other=None):
    R, N = w.shape
    tr = _rows_tile(R, 256)
    spec = pl.BlockSpec((tr, N), lambda i: (i, 0))
    bc1 = 1.0 - ADAM_B1 ** ADAM_STEP
    bc2 = 1.0 - ADAM_B2 ** ADAM_STEP
    two = g_other is not None

    def body(*refs):
        w_ref, g_ref, m_ref, v_ref = refs[:4]
        d_ref, nm_ref, nv_ref = refs[4 + two:7 + two]
        g = g_ref[...]
        if two:
            g = g + refs[4][...]
            refs[8][...] = g
        m = ADAM_B1 * m_ref[...] + (1.0 - ADAM_B1) * g
        v = ADAM_B2 * v_ref[...] + (1.0 - ADAM_B2) * (g * g)
        nm_ref[...] = m
        nv_ref[...] = v
        d_ref[...] = -ADAM_LR * ((m / bc1) / (jnp.sqrt(v / bc2) + ADAM_EPS) + ADAM_WD * w_ref[...])

    shp = _sds((R, N), F32)
    args = (w, g, m, v) + ((g_other,) if two else ())
    return _pcall(body, name=name, out_shape=(shp,) * (3 + two), grid=(R // tr,), in_specs=[spec] * len(args),
                  out_specs=(spec,) * (3 + two), dims=("parallel",))(*args)


_GROUP_MASKS = {
    "all": [(dx, dy, dc) for dx in (0, 1) for dy in (0, 1) for dc in (0, 1) if (dx, dy, dc) != (0, 0, 0)],
    "xy": [(1, 0, 0), (0, 1, 0), (1, 1, 0)],
    "c": [(0, 0, 1)],
}
_GROUP_SLOTS = {"all": 8, "xy": 4, "c": 2}


def _group_slot(group, x, y, c):
    return {"all": 4 * x + 2 * y + c, "xy": 2 * x + y, "c": c}[group]


def _flip(v, d):
    return 1 - v if d else v


def _exchange(arr, group, mode, name):
    return _exchange_list([arr], group, mode, name)[0]


def _exchange_list(arrs, group, mode, name):
    masks = _GROUP_MASKS[group]
    n = len(masks)
    na = len(arrs)
    assert mode in ("gather", "swap") and (mode == "gather" or group == "c")
    has_local = mode == "gather"
    out_shapes = [((_GROUP_SLOTS[group],) if has_local else ()) + arr.shape for arr in arrs]
    bounce = [pltpu.VMEM(arr.shape, arr.dtype) for arr in arrs] if has_local else []

    def body(*refs):
        x_refs, o_refs = refs[:na], refs[na:2 * na]
        send_sems, recv_sems = refs[2 * na], refs[2 * na + 1]
        x, y, c = lax.axis_index("x"), lax.axis_index("y"), lax.axis_index("c")
        me = _group_slot(group, x, y, c)
        if has_local:
            local_sems = refs[2 * na + 2]
            bufs = refs[2 * na + 3:]
            loads = []
            for i in range(na):
                loads.append(pltpu.make_async_copy(x_refs[i], bufs[i], local_sems.at[2 * i]))
                loads[-1].start()
        copies = []
        for i in range(na):
            x_ref, o_ref = x_refs[i], o_refs[i]
            for k, (dx, dy, dc) in enumerate(masks):
                px, py, pc = _flip(x, dx), _flip(y, dy), _flip(c, dc)
                src, dst = (x_ref, o_ref.at[me]) if has_local else (x_ref, o_ref)
                cp =pltpu.make_async_remote_copy(src_ref=src, dst_ref=dst, send_sem=send_sems.at[i * n + k],
                                                  recv_sem=recv_sems.at[i * n + k], device_id=(px, py, pc),
                                                  device_id_type=pl.DeviceIdType.MESH)
                cp.start()
                copies.append(cp)
        if has_local:
            stores = []
            for i in range(na):
                loads[i].wait()
                stores.append(pltpu.make_async_copy(bufs[i], o_refs[i].at[me], local_sems.at[2 * i + 1]))
                stores[-1].start()
        for cp in copies:
            cp.wait()
        if has_local:
            for st in stores:
                st.wait()

    anyspec = pl.BlockSpec(memory_space=pl.ANY)
    scratch = [pltpu.SemaphoreType.DMA((n * na,)), pltpu.SemaphoreType.DMA((n * na,))]
    if has_local:
        scratch += [pltpu.SemaphoreType.DMA((2 * na,))] + bounce
    outs = pl.pallas_call(body, name=name, out_shape=tuple(_sds(s, a.dtype) for s, a in zip(out_shapes, arrs)),
                          in_specs=[anyspec] * na, out_specs=tuple([anyspec] * na), scratch_shapes=scratch,
                          compiler_params=pltpu.CompilerParams(vmem_limit_bytes=V7X_VMEM_LIMIT_BYTES))(*arrs)
    return list(outs)


def _gather_weights(shards, name):
    na = len(shards)
    masks = _GROUP_MASKS["xy"]
    n = len(masks)

    def body(*refs):
        x_refs, o_refs = refs[:na], refs[na:2 * na]
        send_sems, recv_sems, local_sems = refs[2 * na:2 * na + 3]
        bufs = refs[2 * na + 3:]
        x, y, c = lax.axis_index("x"), lax.axis_index("y"), lax.axis_index("c")
        me = 2 * x + y
        sibling = (x, y, 1 - c)
        loads = []
        for i in range(na):
            loads.append(pltpu.make_async_copy(x_refs[i], bufs[i], local_sems.at[2 * i]))
            loads[-1].start()

        def half_of(i, slot, cc):
            h = shards[i].shape[0] // 2
            return o_refs[i].at[slot, pl.ds(pl.multiple_of(cc * h, 8), h), :]

        def src_half(i, cc):
            h = shards[i].shape[0] // 2
            return x_refs[i].at[pl.ds(pl.multiple_of(cc * h, 8), h), :]

        sends = []
        for i in range(na):
            for k, (dx, dy, _) in enumerate(masks):
                cp = pltpu.make_async_remote_copy(src_ref=src_half(i, c), dst_ref=half_of(i, me, c),
                                                  send_sem=send_sems.at[i * 2 * n + k], recv_sem=recv_sems.at[i * 2 * n + k],
                                                  device_id=(_flip(x, dx), _flip(y, dy), c),
                                                  device_id_type=pl.DeviceIdType.MESH)
                cp.start()
                sends.append(cp)
        stores = []
        for i in range(na):
            loads[i].wait()
            stores.append(pltpu.make_async_copy(bufs[i], o_refs[i].at[me], local_sems.at[2 * i + 1]))
            stores[-1].start()
        for i in range(na):
            for k, (dx, dy, _) in enumerate(masks):
                slot = 2 * _flip(x, dx) + _flip(y, dy)
                landed = pltpu.make_async_remote_copy(src_ref=src_half(i, c), dst_ref=half_of(i, slot, c),
                                                      send_sem=send_sems.at[i * 2 * n + k],
                                                      recv_sem=recv_sems.at[i * 2 * n + k], device_id=sibling,
                                                      device_id_type=pl.DeviceIdType.MESH)
                landed.wait_recv()
                fwd = pltpu.make_async_remote_copy(src_ref=half_of(i, slot, c), dst_ref=half_of(i, slot, c),
                                                   send_sem=send_sems.at[i * 2 * n + n + k],
                                                   recv_sem=recv_sems.at[i * 2 * n + n + k], device_id=sibling,
                                                   device_id_type=pl.DeviceIdType.MESH)
                fwd.start()
                sends.append(fwd)
        for i in range(na):
            for k, (dx, dy, _) in enumerate(masks):
                slot = 2 * _flip(x, dx) + _flip(y, dy)
                pltpu.make_async_remote_copy(src_ref=half_of(i, slot, 1 - c), dst_ref=half_of(i, slot, 1 - c),
                                             send_sem=send_sems.at[i * 2 * n + n + k],
                                             recv_sem=recv_sems.at[i * 2 * n + n + k], device_id=sibling,
                                             device_id_type=pl.DeviceIdType.MESH).wait_recv()
        for cp in sends:
            cp.wait_send()
        for st in stores:
            st.wait()

    anyspec = pl.BlockSpec(memory_space=pl.ANY)
    scratch = [pltpu.SemaphoreType.DMA((2 * n * na,)), pltpu.SemaphoreType.DMA((2 * n * na,)),
               pltpu.SemaphoreType.DMA((2 * na,))] + [pltpu.VMEM(s.shape, s.dtype) for s in shards]
    outs = pl.pallas_call(body, name=name, out_shape=tuple(_sds((N_XY,) + s.shape, s.dtype) for s in shards),
                          in_specs=[anyspec] * na, out_specs=tuple([anyspec] * na), scratch_shapes=scratch,
                          compiler_params=pltpu.CompilerParams(vmem_limit_bytes=V7X_VMEM_LIMIT_BYTES))(*shards)
    return list(outs)


BIG = (("w_proj_att", (ATT_WIDTH, D_MODEL), 1), ("w_proj_ssm", (SSM_WIDTH, D_MODEL), 1),
       ("w_glu", (SSM_WIDTH, SSM_WIDTH), 0))
DIRECT = (("w_in", True), ("w_up", True), ("w_down", False), ("w_out", False))
N_XY = 4


def _big_rows(shape):
    return shape[0] * shape[1] // N_XY // LANES


FLAT_ROWS = sum(_big_rows(s) for _, s, _ in BIG)


def _shard_shape(shape, axis):
    return (shape[0] // N_XY, shape[1]) if axis == 0 else (shape[0], shape[1] // N_XY)


def _flatten_shards(shards):
    return jnp.concatenate([shards[n].reshape(_big_rows(s), LANES) for n, s, _ in BIG], axis=0)


def _unflatten_shard(flat):
    out, r = {}, 0
    for n, s, ax in BIG:
        k = _big_rows(s)
        out[n] = flat[r:r + k].reshape(_shard_shape(s, ax))
        r += k
    return out


def _unflatten_full(flat4):
    out, r = {}, 0
    for n, s, ax in BIG:
        k = _big_rows(s)
        sh = _shard_shape(s, ax)
        t = flat4[:, r:r + k].reshape((N_XY,) + sh)
        out[n] = t.reshape(s) if ax == 0 else t.transpose(1, 0, 2).reshape(s)
        r += k
    return out


def _flatten_full(full):
    parts = []
    for n, s, ax in BIG:
        sh = _shard_shape(s, ax)
        t = full[n]
        t = t.reshape((N_XY,) + sh) if ax == 0 else t.reshape(s[0], N_XY, sh[1]).transpose(1, 0, 2)
        parts.append(t.reshape(N_XY, _big_rows(s), LANES))
    return jnp.concatenate(parts, axis=1)


def _lanes_from_groups(a):
    return a.transpose(2, 0, 1).reshape(SSM_GROUP_CH, SSM_LANES)


def _groups_from_lanes(a):
    return a.reshape(SSM_GROUP_CH, SSM_GROUPS, SSM_STATE).transpose(1, 2, 0)


LATE = ("w_up_t", "w_down", "w_out")
EARLY_GRADS = ("w_up_t", "w_down", "w_out")


def _local_step(x3, mod, tgt3, W, P, late_shards=None, scatter_grads=False):
    B, S, _ = x3.shape
    T = B * S
    seq_blocks = S // ATT_BLOCK
    sh1, sc1, gt1, sh2, sc2, gt2 = [m.reshape(B, 1, D_MODEL) for m in jnp.split(mod, 6, axis=-1)]
    g_mix, g_ffn, g_final = P["g_mix"].reshape(1, D_MODEL), P["g_ffn"].reshape(1, D_MODEL), P["g_final"].reshape(1, D_MODEL)
    b_gate = P["b_gate"].reshape(1, 2 * D_MODEL)
    d_skip, b_glu = P["d_skip"].reshape(1, SSM_WIDTH), P["b_glu"].reshape(1, SSM_WIDTH)
    w_conv, b_conv = P["w_conv"], P["b_conv"].reshape(1, D_FF)

    u1 = _norm_mod(x3, g_mix, sc1, sh1).reshape(T, D_MODEL)
    proj = _mm(u1, W["w_in_t"], tb=True, name="mm_proj", out_dtype=BF16)
    proj3 = proj.reshape(B, S, IN_WIDTH)
    us = proj[:, 3 * ATT_WIDTH:3 * ATT_WIDTH + SSM_WIDTH]
    o_att3, lse4, late = _attention_fwd(proj3, seq_blocks, _Riders(late_shards, "gather") if late_shards else None)
    if late_shards:
        W = dict(W, **{n: f.reshape(-1, LANES) for n, f in zip(LATE, late)})
        w_conv = late[len(LATE)].transpose(1, 0, 2).reshape(3, D_FF)
        W.update(_unflatten_full(late[len(LATE) + 1]))
    o_att = o_att3.reshape(T, ATT_WIDTH)
    y_att = _mm(o_att, W["w_proj_att"], name="mm_proj_att", out_dtype=BF16)

    lr = P["a_re"].reshape(1, SSM_LANES)
    li = P["a_im"].reshape(1, SSM_LANES)
    ldt = jnp.repeat(P["log_dt"], SSM_STATE).reshape(1, SSM_LANES)
    br, bi = _lanes_from_groups(P["b_re"]), _lanes_from_groups(P["b_im"])
    cr = P["c_re"].transpose(1, 0, 2).reshape(SSM_GROUP_CH, SSM_LANES)
    ci = P["c_im"].transpose(1, 0, 2).reshape(SSM_GROUP_CH, SSM_LANES)
    abar, w_bu, w_c = _ssm_params(lr, li, ldt, br, bi, cr, ci)
    xs3, y_core3 = _ssm_scan_fwd(proj3, abar, w_bu, w_c)
    y5, s_out = _ssm_post(y_core3.reshape(T, SSM_WIDTH), us, d_skip, W["w_glu"], b_glu)
    y_ssm = _mm(s_out, W["w_proj_ssm"], name="mm_proj_ssm", out_dtype=BF16)

    merged = _merge(proj, y_att, y_ssm, b_gate)
    mix = _mm(merged, W["w_out"], name="mm_out", out_dtype=BF16)
    mix3 = mix.reshape(B, S, D_MODEL)

    h1, u2 = _resid_norm_mod(x3, mix3, gt1, g_ffn, sc2, sh2)
    u2 = u2.reshape(T, D_MODEL)
    up3 = _mm(u2, W["w_up_t"], tb=True, name="mm_up", out_dtype=BF16).reshape(B, S, 2 * D_FF)
    act = _conv_act(up3, w_conv, b_conv).reshape(T, D_FF)
    ffn3 = _mm(act, W["w_down"], name="mm_down", out_dtype=BF16).reshape(B, S, D_MODEL)
    dh2, dffn, dgt2, dg_final, loss = _final_loss(h1, ffn3, tgt3, gt2, g_final)

    dffn = dffn.reshape(T, D_MODEL)
    gw = {}
    gw["w_down"] = _mm(act, dffn, ta=True, out_dtype=BF16, name="mm_dw_down")
    dact3 = _mm(dffn, W["w_down"], tb=True, name="mm_dact", out_dtype=BF16).reshape(B, S, D_FF)
    dup3, dw_conv, db_conv = _conv_bwd(up3, dact3, w_conv, b_conv)
    dup = dup3.reshape(2, T, D_FF)
    gw["w_up_t"] = _mm(dup, u2, ta=True, out_dtype=BF16, name="mm_dw_up")
    du2 = _mm(dup, W["w_up_t"], name="mm_du2", out_dtype=BF16).reshape(B, S, D_MODEL)
    dh1, dsh2, dsc2, dg_ffn, dgt1, dmix = _norm_bwd(h1, du2, dh2, g_ffn, sc2, "norm_bwd2", mix3=mix3, gt=gt1)

    dmix = dmix.reshape(T, D_MODEL)
    gw["w_out"] = _mm(merged, dmix, ta=True, out_dtype=BF16, name="mm_dw_out")
    dmerged = _mm(dmix, W["w_out"], tb=True, name="mm_dmerged", out_dtype=BF16)
    dy_att, dy_ssm, dga, dgs, db_att, db_ssm = _merge_bwd(proj, y_att, y_ssm, b_gate, dmerged)

    gw["w_proj_ssm"] = _mm(s_out, dy_ssm, ta=True, name="mm_dw_proj_ssm")
    ds_out = _mm(dy_ssm, W["w_proj_ssm"], tb=True, name="mm_ds_out")
    dy5, dd_skip, db_glu, dw_glu = _ssm_post_bwd(y5, us, ds_out, d_skip, W["w_glu"], b_glu)
    gw["w_glu"] = dw_glu
    dus3, dab, dwbu, dwc = _ssm_scan_bwd(proj3, dy5.reshape(B, S, SSM_WIDTH), xs3, abar, w_bu, w_c, d_skip)
    dus = dus3.reshape(T, SSM_WIDTH)
    dlr, dli, dldt, dbr, dbi, dcr, dci = _ssm_params_bwd(lr, li, ldt, br, bi, dab, dwbu, dwc)

    gw["w_proj_att"] = _mm(o_att, dy_att, ta=True, name="mm_dw_proj_att")
    do_att = _mm(dy_att, W["w_proj_att"], tb=True, out_dtype=BF16, name="mm_do_att")
    early = [gw[n].reshape(N_XY, -1, LANES) for n in EARLY_GRADS]
    early.append(_flatten_full({n: gw[n].astype(BF16) for n, _, _ in BIG}))
    dq3, dk3, dv3, parts = _attention_bwd(proj3, do_att.reshape(B, S, ATT_WIDTH), o_att3, lse4, seq_blocks,
                                          _Riders(early, "scatter") if scatter_grads else None)
    dproj = jnp.concatenate([t.reshape(T, ATT_WIDTH) for t in (dq3, dk3, dv3)] + [dus, dga, dgs], axis=1)
    dmods = [None, None, dgt1, dsh2, dsc2, dgt2]
    native = dict(b_att=db_att, b_ssm=db_ssm, a_re=dlr, a_im=dli, log_dt=dldt, b_re=dbr, b_im=dbi, c_re=dcr, c_im=dci,
                  d_skip=dd_skip, b_glu=db_glu, g_ffn=dg_ffn, w_conv=dw_conv, b_conv=db_conv, g_final=dg_final, loss=loss)
    small_early = _pack_small(native, dmods, False)
    sums, sums_sib, last_parts = [], [], []
    if scatter_grads:
        sums = [_sum_slots(p, "sum_chips_%d" % i) for i, p in enumerate(parts)]
        riders = _RiderGroup([_Riders([small_early], "gather", "all"), _Riders(sums, "swap", "c")])
        gw["w_in_t"], rode = _mm(dproj, u1, ta=True, out_dtype=BF16, name="mm_dw_in", riders=riders)
        small_early, sums_sib = rode[0], rode[1:]
        du1, last_parts = _mm(dproj, W["w_in_t"], name="mm_du1", out_dtype=BF16,
                              riders=_Riders([gw["w_in_t"].reshape(N_XY, -1, LANES)], "scatter"))
    else:
        gw["w_in_t"] = _mm(dproj, u1, ta=True, out_dtype=BF16, name="mm_dw_in")
        du1 = _mm(dproj, W["w_in_t"], name="mm_du1", out_dtype=BF16)
    du1 = du1.reshape(B, S, D_MODEL)
    dx, dsh1, dsc1, dg_mix = _norm_bwd(x3, du1, dh1, g_mix, sc1, "norm_bwd1")
    dmods[0], dmods[1] = dsh1, dsc1
    native["g_mix"] = dg_mix
    return loss, dx, dmods, gw, native, (sums, sums_sib, last_parts), small_early


WEIGHTS = ['w_ada', 'b_ada', 'g_mix', 'w_in', 'b_gate', 'a_re', 'a_im', 'log_dt', 'b_re', 'b_im', 'c_re', 'c_im', 'd_skip',
           'w_glu', 'b_glu', 'w_proj_att', 'w_proj_ssm', 'w_out', 'g_ffn', 'w_up', 'w_conv', 'b_conv', 'w_down', 'g_final']
SMALL = ['g_mix', 'b_gate', 'a_re', 'a_im', 'log_dt', 'b_re', 'b_im', 'c_re', 'c_im', 'd_skip', 'b_glu', 'g_ffn', 'w_conv',
         'b_conv', 'g_final']


def kernel(x, c, w_ada, b_ada, g_mix, w_in, b_gate, a_re, a_im, log_dt, b_re, b_im, c_re, c_im, d_skip, w_glu, b_glu, w_proj_att, w_proj_ssm, w_out, g_ffn, w_up, w_conv, b_conv, w_down, g_final, loss_target, m_w_ada, m_b_ada, m_g_mix, m_w_in, m_b_gate, m_a_re, m_a_im, m_log_dt, m_b_re, m_b_im, m_c_re, m_c_im, m_d_skip, m_w_glu, m_b_glu, m_w_proj_att, m_w_proj_ssm, m_w_out, m_g_ffn, m_w_up, m_w_conv, m_b_conv, m_w_down, m_g_final, v_w_ada, v_b_ada, v_g_mix, v_w_in, v_b_gate, v_a_re, v_a_im, v_log_dt, v_b_re, v_b_im, v_c_re, v_c_im, v_d_skip, v_w_glu, v_b_glu, v_w_proj_att, v_w_proj_ssm, v_w_out, v_g_ffn, v_w_up, v_w_conv, v_b_conv, v_w_down, v_g_final):
    args = dict(locals())
    w = {n: args[n] for n in WEIGHTS}
    m = {n: args["m_" + n] for n in WEIGHTS}
    v = {n: args["v_" + n] for n in WEIGHTS}
    B, S, _ = x.shape
    ix, iy, ic = lax.axis_index("x"), lax.axis_index("y"), lax.axis_index("c")
    chip = 2 * ix + iy
    ada_cols = w_ada.shape[2]

    c_all = _exchange(c, "all", "gather", "gather_c").reshape(8 * B, D_MODEL)
    b_cols = lax.dynamic_slice_in_dim(b_ada, chip * ada_cols, ada_cols, axis=1)
    mod_cols = _ada_fwd(c_all, w_ada[0], b_cols)
    mod_all = _exchange(mod_cols, "xy", "gather", "gather_mod")
    mod_all = mod_all.transpose(1, 0, 2).reshape(8 * B, 6 * D_MODEL)
    mod = lax.dynamic_slice_in_dim(mod_all, (4 * ix + 2 * iy + ic) * B, B, axis=0)

    shard = {n + ("_t" if t else ""): (w[n][0].T if t else w[n][0]).astype(BF16) for n, t in DIRECT}
    misc = _flatten_shards({n: w[n][0] for n, _, _ in BIG}).astype(BF16)
    (w_in_full,) = _gather_weights([shard["w_in_t"]], "gather_weights")
    W = {"w_in_t": w_in_full.reshape(-1, LANES)}

    P = {n: w[n][0] for n in SMALL if n not in ("w_conv", "g_final")}
    P["w_conv"] = None
    P["g_final"] = g_final

    loss, dx, dmods, gw, native, parts, small_early = _local_step(x, mod, loss_target, W, P,
                                                                  [shard[n] for n in LATE] + [w_conv[0], misc], True)

    small_late = _exchange(_pack_small(native, dmods, True), "all", "gather", "gather_small")
    native_sum, dmod_all = _sum_unpack_small(small_early, small_late, B)
    loss = native_sum["loss"][0, 0]
    g_small = _small_from_native(native_sum)
    dmod_all = dmod_all.reshape(8 * B, N_MOD * D_MODEL)
    dmod_cols = lax.dynamic_slice_in_dim(dmod_all, chip * ada_cols, ada_cols, axis=1)
    g_w_ada, g_b_ada = _ada_bwd(c_all, dmod_all, dmod_cols)

    red, red_sib, last_parts = parts
    red = red + [_sum_slots(last_parts[0], "sum_chips_w_in")]
    red_sib = red_sib + [_exchange(red[-1], "c", "swap", "share_cores")]
    order = list(EARLY_GRADS) + ["misc", "w_in_t"]
    halves = dict(zip(order, zip(red, red_sib)))

    grads = {"w_ada": g_w_ada[None], "b_ada": g_b_ada}
    grads["w_up"] = _add2(*halves["w_up_t"], F32, "add_cores_w_up").T[None]
    for k, gk in _unflatten_shard(_add2(*halves["misc"], F32, "add_cores_misc")).items():
        grads[k] = gk[None]
    wc_cols = w_conv.shape[2]
    for n in SMALL:
        g = g_small[n]
        if n == "w_conv":
            g = lax.dynamic_slice_in_dim(g, chip * wc_cols, wc_cols, axis=1)
        grads[n] = g.reshape(w[n].shape)

    delta, new_m, new_v = {}, {}, {}
    for n in ["w_ada"] + [b for b, _ in DIRECT] + [b for b, _, _ in BIG]:
        shp = w[n].shape
        if n == "w_in":
            r, s = halves["w_in_t"]
            d2, m2, v2, g2 = _adamw(w[n][0].T, r, m[n][0].T, v[n][0].T, "adamw_" + n, g_other=s)
            d2, m2, v2, grads[n] = d2.T, m2.T, v2.T, g2.T[None]
        elif n in ("w_down", "w_out"):
            r, s = halves[n]
            d2, m2, v2, g2 = _adamw(w[n][0], r, m[n][0], v[n][0], "adamw_" + n, g_other=s)
            grads[n] = g2[None]
        else:
            d2, m2, v2 = _adamw(w[n][0], grads[n][0], m[n][0], v[n][0], "adamw_" + n)
        delta[n], new_m[n], new_v[n] = d2.reshape(shp), m2.reshape(shp), v2.reshape(shp)
    rest = ["b_ada"] + SMALL

    def drop(a):
        return a.reshape(1, -1) if a.ndim == 1 else (a if a.ndim == 2 else a[0])

    upd = _adamw_multi([(drop(w[n]), drop(grads[n]), drop(m[n]), drop(v[n])) for n in rest])
    for n, (dd, mm, vv) in zip(rest, upd):
        delta[n], new_m[n], new_v[n] = dd.reshape(w[n].shape), mm.reshape(w[n].shape), vv.reshape(w[n].shape)

    return (loss, dx, *[grads[n] for n in WEIGHTS], *[delta[n] for n in WEIGHTS], *[new_m[n] for n in WEIGHTS],
            *[new_v[n] for n in WEIGHTS])
```
